```python
import jax, jax.numpy as jnp
from jax import lax
import numpy as np

D_MODEL = 1024
BATCH = 8
SEQ = 8192
DEPTH = 1

N_GROUPS = 3
HEADS_PER_GROUP = 4
HEAD_DIM = 128
DILATED_CONFIGS = ((128, 1), (512, 4), (2048, 16))
ATTN_WIDTH = N_GROUPS * HEADS_PER_GROUP * HEAD_DIM
ATTN_OUT = HEADS_PER_GROUP * HEAD_DIM
CONV_WIDTH = D_MODEL
CONV_K = 3
D_FF = 2816
EPS = 1e-6
N_MOD = 9
IN_SIZES = (ATTN_WIDTH, ATTN_WIDTH, ATTN_WIDTH, CONV_WIDTH, CONV_WIDTH, CONV_WIDTH, D_MODEL, D_MODEL)
IN_WIDTH = sum(IN_SIZES)
IN_SPLITS = tuple(int(s) for s in np.cumsum(IN_SIZES)[:-1])

kernel_name = "hybrid_dilated_attn_shortconv_macaron_adaln"


def rms_norm(x, g):
    x32 = x.astype(jnp.float32)
    y = x32 * lax.rsqrt(jnp.mean(x32 * x32, axis=-1, keepdims=True) + EPS)
    return (y * g.astype(jnp.float32)).astype(x.dtype)


def modulate(h, shift, scale):
    return h * (1.0 + scale[:, None, :]) + shift[:, None, :]


def swiglu(h, w_gate, w_up, w_down):
    return (jax.nn.silu(h @ w_gate) * (h @ w_up)) @ w_down


def dilated_band_attention(q, k, v, window, dilation):
    B, S, H, E = q.shape
    nw = window // dilation
    L = S // dilation
    nb = -(-L // nw)
    Lp = nb * nw

    def to_blocks(t):
        t = t.reshape(B, L, dilation, H, E).transpose(0, 2, 3, 1, 4)
        t = jnp.pad(t, ((0, 0), (0, 0), (0, 0), (0, Lp - L), (0, 0)))
        return t.reshape(B, dilation, H, nb, nw, E)

    def with_prev(t):
        prev = jnp.pad(t[:, :, :, :-1], ((0, 0), (0, 0), (0, 0), (1, 0), (0, 0), (0, 0)))
        return jnp.concatenate([prev, t], axis=4)

    qb = to_blocks(q)
    kc = with_prev(to_blocks(k))
    vc = with_prev(to_blocks(v))
    s = jnp.einsum('brhnqe,brhnke->brhnqk', qb, kc) * (E ** -0.5)
    qi = jnp.arange(nw)
    kj = jnp.arange(2 * nw)
    rel = nw + qi[:, None] - kj[None, :]
    band = (rel >= 0) & (rel <= nw)
    key_pos = jnp.arange(nb)[:, None] * nw - nw + kj[None, :]
    mask = band[None, :, :] & (key_pos >= 0)[:, None, :]
    s = jnp.where(mask, s, -jnp.inf)
    m = jnp.max(s, axis=-1, keepdims=True)
    p = jnp.exp(s - m)
    denom = jnp.sum(p, axis=-1, keepdims=True)
    o = jnp.einsum('brhnqk,brhnke->brhnqe', p, vc) / denom
    lse = (m + jnp.log(denom))[..., 0]
    o = o.reshape(B, dilation, H, Lp, E)[:, :, :, :L].transpose(0, 3, 1, 2, 4).reshape(B, S, H, E)
    lse = lse.reshape(B, dilation, H, Lp)[:, :, :, :L].transpose(0, 3, 1, 2).reshape(B, S, H)
    return o, lse


def hybrid_mixer(h, w_in, q_norm, k_norm, conv_w, w_attn_branch, w_conv_branch, w_out):
    B, S, _ = h.shape
    proj = h @ w_in
    q, k, v, u, b_gate, c_gate, g_attn, g_conv = jnp.split(proj, IN_SPLITS, axis=-1)
    q = rms_norm(q.reshape(B, S, N_GROUPS, HEADS_PER_GROUP, HEAD_DIM), q_norm).astype(jnp.float32)
    k = rms_norm(k.reshape(B, S, N_GROUPS, HEADS_PER_GROUP, HEAD_DIM), k_norm).astype(jnp.float32)
    v = v.reshape(B, S, N_GROUPS, HEADS_PER_GROUP, HEAD_DIM).astype(jnp.float32)
    outs, lses = [], []
    for g, (window, dilation) in enumerate(DILATED_CONFIGS):
        o_g, lse_g = dilated_band_attention(q[:, :, g], k[:, :, g], v[:, :, g], window, dilation)
        outs.append(o_g)
        lses.append(lse_g)
    weights = jax.nn.softmax(jnp.stack(lses, axis=0), axis=0)
    o = jnp.einsum('gbsh,gbshe->bshe', weights, jnp.stack(outs, axis=0))
    y_attn = o.reshape(B, S, ATTN_OUT).astype(h.dtype) @ w_attn_branch
    xc = c_gate * u
    xp = jnp.pad(xc, ((0, 0), (CONV_K - 1, 0), (0, 0)))
    conv = xp[:, 0:S] * conv_w[0]
    for j in range(1, CONV_K):
        conv = conv + xp[:, j:j + S] * conv_w[j]
    y_conv = (b_gate * conv) @ w_conv_branch
    merged = jax.nn.sigmoid(g_attn) * y_attn + jax.nn.sigmoid(g_conv) * y_conv
    return merged @ w_out


def _fwd_setup_inputs(seed: int = 0) -> dict:
    key = jax.random.key(seed)
    ks = jax.random.split(key, 24)
    f32 = jnp.float32
    L, D = DEPTH, D_MODEL

    def nrm(k, shape, scale):
        return jax.random.normal(k, shape, f32) * scale

    return {
        'x': nrm(ks[0], (BATCH, SEQ, D), 1.0),
        'c': nrm(ks[1], (BATCH, D), 1.0),
        'w_ada': nrm(ks[2], (L, D, N_MOD * D), 0.5 * D ** -0.5),
        'b_ada': nrm(ks[3], (L, N_MOD * D), 0.01),
        'norm_ffn1': 1.0 + nrm(ks[4], (L, D), 0.01),
        'ffn1_w_gate': nrm(ks[5], (L, D, D_FF), D ** -0.5),
        'ffn1_w_up': nrm(ks[6], (L, D, D_FF), D ** -0.5),
        'ffn1_w_down': nrm(ks[7], (L, D_FF, D), D_FF ** -0.5),
        'norm_mix': 1.0 + nrm(ks[8], (L, D), 0.01),
        'w_in': nrm(ks[9], (L, D, IN_WIDTH), D ** -0.5),
        'q_norm': 1.0 + nrm(ks[10], (L, HEAD_DIM), 0.01),
        'k_norm': 1.0 + nrm(ks[11], (L, HEAD_DIM), 0.01),
        'conv_w': nrm(ks[12], (L, CONV_K, CONV_WIDTH), CONV_K ** -0.5),
        'w_attn_branch': nrm(ks[13], (L, ATTN_OUT, D), ATTN_OUT ** -0.5),
        'w_conv_branch': nrm(ks[14], (L, CONV_WIDTH, D), CONV_WIDTH ** -0.5),
        'w_out': nrm(ks[15], (L, D, D), D ** -0.5),
        'norm_ffn2': 1.0 + nrm(ks[16], (L, D), 0.01),
        'ffn2_w_gate': nrm(ks[17], (L, D, D_FF), D ** -0.5),
        'ffn2_w_up': nrm(ks[18], (L, D, D_FF), D ** -0.5),
        'ffn2_w_down': nrm(ks[19], (L, D_FF, D), D_FF ** -0.5),
    }


def _fwd_reference(x, c, w_ada, b_ada, norm_ffn1, ffn1_w_gate, ffn1_w_up, ffn1_w_down,
              norm_mix, w_in, q_norm, k_norm, conv_w, w_attn_branch, w_conv_branch, w_out,
              norm_ffn2, ffn2_w_gate, ffn2_w_up, ffn2_w_down):
    c_act = jax.nn.silu(c)
    for l in range(DEPTH):
        mod = c_act @ w_ada[l] + b_ada[l]
        sh1, sc1, gt1, sh2, sc2, gt2, sh3, sc3, gt3 = jnp.split(mod, N_MOD, axis=-1)
        h = modulate(rms_norm(x, norm_ffn1[l]), sh1, sc1)
        x = x + 0.5 * gt1[:, None, :] * swiglu(h, ffn1_w_gate[l], ffn1_w_up[l], ffn1_w_down[l])
        h = modulate(rms_norm(x, norm_mix[l]), sh2, sc2)
        x = x + gt2[:, None, :] * hybrid_mixer(h, w_in[l], q_norm[l], k_norm[l], conv_w[l],
                                               w_attn_branch[l], w_conv_branch[l], w_out[l])
        h = modulate(rms_norm(x, norm_ffn2[l]), sh3, sc3)
        x = x + 0.5 * gt3[:, None, :] * swiglu(h, ffn2_w_gate[l], ffn2_w_up[l], ffn2_w_down[l])
    return x


import jax as _jax
import jax.numpy as _jnp

TWIN_FORMAT = 'train_step'
FWD_PARAMS = ['x', 'c', 'w_ada', 'b_ada', 'norm_ffn1', 'ffn1_w_gate', 'ffn1_w_up', 'ffn1_w_down', 'norm_mix', 'w_in', 'q_norm', 'k_norm', 'conv_w', 'w_attn_branch', 'w_conv_branch', 'w_out', 'norm_ffn2', 'ffn2_w_gate', 'ffn2_w_up', 'ffn2_w_down']
TWIN_WEIGHTS = ['w_ada', 'b_ada', 'norm_ffn1', 'ffn1_w_gate', 'ffn1_w_up', 'ffn1_w_down', 'norm_mix', 'w_in', 'q_norm', 'k_norm', 'conv_w', 'w_attn_branch', 'w_conv_branch', 'w_out', 'norm_ffn2', 'ffn2_w_gate', 'ffn2_w_up', 'ffn2_w_down']
TWIN_DIFF_INPUT = 'x'
TWIN_INPUTS = ['x', 'c', 'w_ada', 'b_ada', 'norm_ffn1', 'ffn1_w_gate', 'ffn1_w_up', 'ffn1_w_down', 'norm_mix', 'w_in', 'q_norm', 'k_norm', 'conv_w', 'w_attn_branch', 'w_conv_branch', 'w_out', 'norm_ffn2', 'ffn2_w_gate', 'ffn2_w_up', 'ffn2_w_down', 'loss_target', 'm_w_ada', 'm_b_ada', 'm_norm_ffn1', 'm_ffn1_w_gate', 'm_ffn1_w_up', 'm_ffn1_w_down', 'm_norm_mix', 'm_w_in', 'm_q_norm', 'm_k_norm', 'm_conv_w', 'm_w_attn_branch', 'm_w_conv_branch', 'm_w_out', 'm_norm_ffn2', 'm_ffn2_w_gate', 'm_ffn2_w_up', 'm_ffn2_w_down', 'v_w_ada', 'v_b_ada', 'v_norm_ffn1', 'v_ffn1_w_gate', 'v_ffn1_w_up', 'v_ffn1_w_down', 'v_norm_mix', 'v_w_in', 'v_q_norm', 'v_k_norm', 'v_conv_w', 'v_w_attn_branch', 'v_w_conv_branch', 'v_w_out', 'v_norm_ffn2', 'v_ffn2_w_gate', 'v_ffn2_w_up', 'v_ffn2_w_down']
TWIN_OUTPUTS = ['loss', 'grad_x', 'grad_w_ada', 'grad_b_ada', 'grad_norm_ffn1', 'grad_ffn1_w_gate', 'grad_ffn1_w_up', 'grad_ffn1_w_down', 'grad_norm_mix', 'grad_w_in', 'grad_q_norm', 'grad_k_norm', 'grad_conv_w', 'grad_w_attn_branch', 'grad_w_conv_branch', 'grad_w_out', 'grad_norm_ffn2', 'grad_ffn2_w_gate', 'grad_ffn2_w_up', 'grad_ffn2_w_down', 'delta_w_ada', 'delta_b_ada', 'delta_norm_ffn1', 'delta_ffn1_w_gate', 'delta_ffn1_w_up', 'delta_ffn1_w_down', 'delta_norm_mix', 'delta_w_in', 'delta_q_norm', 'delta_k_norm', 'delta_conv_w', 'delta_w_attn_branch', 'delta_w_conv_branch', 'delta_w_out', 'delta_norm_ffn2', 'delta_ffn2_w_gate', 'delta_ffn2_w_up', 'delta_ffn2_w_down', 'new_m_w_ada', 'new_m_b_ada', 'new_m_norm_ffn1', 'new_m_ffn1_w_gate', 'new_m_ffn1_w_up', 'new_m_ffn1_w_down', 'new_m_norm_mix', 'new_m_w_in', 'new_m_q_norm', 'new_m_k_norm', 'new_m_conv_w', 'new_m_w_attn_branch', 'new_m_w_conv_branch', 'new_m_w_out', 'new_m_norm_ffn2', 'new_m_ffn2_w_gate', 'new_m_ffn2_w_up', 'new_m_ffn2_w_down', 'new_v_w_ada', 'new_v_b_ada', 'new_v_norm_ffn1', 'new_v_ffn1_w_gate', 'new_v_ffn1_w_up', 'new_v_ffn1_w_down', 'new_v_norm_mix', 'new_v_w_in', 'new_v_q_norm', 'new_v_k_norm', 'new_v_conv_w', 'new_v_w_attn_branch', 'new_v_w_conv_branch', 'new_v_w_out', 'new_v_norm_ffn2', 'new_v_ffn2_w_gate', 'new_v_ffn2_w_up', 'new_v_ffn2_w_down']
TWIN_LEAF_KINDS = {'loss': 'loss', 'grad_x': 'grad_x', 'grad_w_ada': 'grad_w', 'grad_b_ada': 'grad_w', 'grad_norm_ffn1': 'grad_w', 'grad_ffn1_w_gate': 'grad_w', 'grad_ffn1_w_up': 'grad_w', 'grad_ffn1_w_down': 'grad_w', 'grad_norm_mix': 'grad_w', 'grad_w_in': 'grad_w', 'grad_q_norm': 'grad_w', 'grad_k_norm': 'grad_w', 'grad_conv_w': 'grad_w', 'grad_w_attn_branch': 'grad_w', 'grad_w_conv_branch': 'grad_w', 'grad_w_out': 'grad_w', 'grad_norm_ffn2': 'grad_w', 'grad_ffn2_w_gate': 'grad_w', 'grad_ffn2_w_up': 'grad_w', 'grad_ffn2_w_down': 'grad_w', 'delta_w_ada': 'delta_w', 'delta_b_ada': 'delta_w', 'delta_norm_ffn1': 'delta_w', 'delta_ffn1_w_gate': 'delta_w', 'delta_ffn1_w_up': 'delta_w', 'delta_ffn1_w_down': 'delta_w', 'delta_norm_mix': 'delta_w', 'delta_w_in': 'delta_w', 'delta_q_norm': 'delta_w', 'delta_k_norm': 'delta_w', 'delta_conv_w': 'delta_w', 'delta_w_attn_branch': 'delta_w', 'delta_w_conv_branch': 'delta_w', 'delta_w_out': 'delta_w', 'delta_norm_ffn2': 'delta_w', 'delta_ffn2_w_gate': 'delta_w', 'delta_ffn2_w_up': 'delta_w', 'delta_ffn2_w_down': 'delta_w', 'new_m_w_ada': 'new_m', 'new_m_b_ada': 'new_m', 'new_m_norm_ffn1': 'new_m', 'new_m_ffn1_w_gate': 'new_m', 'new_m_ffn1_w_up': 'new_m', 'new_m_ffn1_w_down': 'new_m', 'new_m_norm_mix': 'new_m', 'new_m_w_in': 'new_m', 'new_m_q_norm': 'new_m', 'new_m_k_norm': 'new_m', 'new_m_conv_w': 'new_m', 'new_m_w_attn_branch': 'new_m', 'new_m_w_conv_branch': 'new_m', 'new_m_w_out': 'new_m', 'new_m_norm_ffn2': 'new_m', 'new_m_ffn2_w_gate': 'new_m', 'new_m_ffn2_w_up': 'new_m', 'new_m_ffn2_w_down': 'new_m', 'new_v_w_ada': 'new_v', 'new_v_b_ada': 'new_v', 'new_v_norm_ffn1': 'new_v', 'new_v_ffn1_w_gate': 'new_v', 'new_v_ffn1_w_up': 'new_v', 'new_v_ffn1_w_down': 'new_v', 'new_v_norm_mix': 'new_v', 'new_v_w_in': 'new_v', 'new_v_q_norm': 'new_v', 'new_v_k_norm': 'new_v', 'new_v_conv_w': 'new_v', 'new_v_w_attn_branch': 'new_v', 'new_v_w_conv_branch': 'new_v', 'new_v_w_out': 'new_v', 'new_v_norm_ffn2': 'new_v', 'new_v_ffn2_w_gate': 'new_v', 'new_v_ffn2_w_up': 'new_v', 'new_v_ffn2_w_down': 'new_v'}


def _forward(args):
    return _fwd_reference(*[args[k] for k in FWD_PARAMS])


def _output_shape():
    def fwd():
        inp = _fwd_setup_inputs(0)
        return _fwd_reference(*[inp[k] for k in FWD_PARAMS])
    out = _jax.eval_shape(fwd)
    return out.shape, out.dtype

N_MICROBATCH = 1
ADAM_LR = 0.001
ADAM_B1 = 0.9
ADAM_B2 = 0.999
ADAM_EPS = 1e-08
ADAM_WD = 0.01
ADAM_STEP = 10
PER_EXAMPLE_BATCH_AXIS = {'x': 0, 'c': 0, 'loss_target': 0}
SHARED_INPUTS = []
_WEIGHT_DTYPES = {'w_ada': _jnp.float32, 'b_ada': _jnp.float32, 'norm_ffn1': _jnp.float32, 'ffn1_w_gate': _jnp.float32, 'ffn1_w_up': _jnp.float32, 'ffn1_w_down': _jnp.float32, 'norm_mix': _jnp.float32, 'w_in': _jnp.float32, 'q_norm': _jnp.float32, 'k_norm': _jnp.float32, 'conv_w': _jnp.float32, 'w_attn_branch': _jnp.float32, 'w_conv_branch': _jnp.float32, 'w_out': _jnp.float32, 'norm_ffn2': _jnp.float32, 'ffn2_w_gate': _jnp.float32, 'ffn2_w_up': _jnp.float32, 'ffn2_w_down': _jnp.float32}
MOMENT_SCALE = {'w_ada': 1.041883e+00, 'b_ada': 3.047245e+00, 'norm_ffn1': 1.420030e+00, 'ffn1_w_gate': 3.217094e-02, 'ffn1_w_up': 3.034790e-02, 'ffn1_w_down': 4.932654e-02, 'norm_mix': 8.407146e+00, 'w_in': 7.700193e-02, 'q_norm': 8.988245e-02, 'k_norm': 8.977466e-02, 'conv_w': 1.619439e+00, 'w_attn_branch': 2.613361e-02, 'w_conv_branch': 1.195214e-01, 'w_out': 1.173814e-01, 'norm_ffn2': 1.557061e+00, 'ffn2_w_gate': 3.201575e-02, 'ffn2_w_up': 2.923341e-02, 'ffn2_w_down': 4.770360e-02}


def _to_microbatches(a, axis):
    t = _jnp.moveaxis(a, axis, 0)
    t = t.reshape((N_MICROBATCH, t.shape[0] // N_MICROBATCH) + t.shape[1:])
    return _jnp.moveaxis(t, 1, axis + 1)


def setup_inputs(seed: int = 0) -> dict:
    inp = _fwd_setup_inputs(seed)
    key = _jax.random.fold_in(_jax.random.key(seed), 7919)
    shape, _ = _output_shape()
    out = dict(inp)
    out["loss_target"] = _jax.random.normal(_jax.random.fold_in(key, 0), shape, _jnp.float32)
    for i, name in enumerate(TWIN_WEIGHTS):
        w = inp[name].astype(_jnp.float32)
        if MOMENT_SCALE is None:
            s = _jnp.sqrt(_jnp.mean(_jnp.square(w)) + 1e-30)
        else:
            s = MOMENT_SCALE[name]
        km, kv = _jax.random.split(_jax.random.fold_in(key, i + 1))
        out[name] = w
        out["m_" + name] = s * _jax.random.normal(km, w.shape, _jnp.float32)
        out["v_" + name] = (s * s) * _jax.random.uniform(kv, w.shape, _jnp.float32, 0.5, 1.5)
    if N_MICROBATCH > 1:
        for name, axis in PER_EXAMPLE_BATCH_AXIS.items():
            out[name] = _to_microbatches(out[name], axis)
    return {'x': out['x'], 'c': out['c'], 'w_ada': out['w_ada'], 'b_ada': out['b_ada'], 'norm_ffn1': out['norm_ffn1'], 'ffn1_w_gate': out['ffn1_w_gate'], 'ffn1_w_up': out['ffn1_w_up'], 'ffn1_w_down': out['ffn1_w_down'], 'norm_mix': out['norm_mix'], 'w_in': out['w_in'], 'q_norm': out['q_norm'], 'k_norm': out['k_norm'], 'conv_w': out['conv_w'], 'w_attn_branch': out['w_attn_branch'], 'w_conv_branch': out['w_conv_branch'], 'w_out': out['w_out'], 'norm_ffn2': out['norm_ffn2'], 'ffn2_w_gate': out['ffn2_w_gate'], 'ffn2_w_up': out['ffn2_w_up'], 'ffn2_w_down': out['ffn2_w_down'], 'loss_target': out['loss_target'], 'm_w_ada': out['m_w_ada'], 'm_b_ada': out['m_b_ada'], 'm_norm_ffn1': out['m_norm_ffn1'], 'm_ffn1_w_gate': out['m_ffn1_w_gate'], 'm_ffn1_w_up': out['m_ffn1_w_up'], 'm_ffn1_w_down': out['m_ffn1_w_down'], 'm_norm_mix': out['m_norm_mix'], 'm_w_in': out['m_w_in'], 'm_q_norm': out['m_q_norm'], 'm_k_norm': out['m_k_norm'], 'm_conv_w': out['m_conv_w'], 'm_w_attn_branch': out['m_w_attn_branch'], 'm_w_conv_branch': out['m_w_conv_branch'], 'm_w_out': out['m_w_out'], 'm_norm_ffn2': out['m_norm_ffn2'], 'm_ffn2_w_gate': out['m_ffn2_w_gate'], 'm_ffn2_w_up': out['m_ffn2_w_up'], 'm_ffn2_w_down': out['m_ffn2_w_down'], 'v_w_ada': out['v_w_ada'], 'v_b_ada': out['v_b_ada'], 'v_norm_ffn1': out['v_norm_ffn1'], 'v_ffn1_w_gate': out['v_ffn1_w_gate'], 'v_ffn1_w_up': out['v_ffn1_w_up'], 'v_ffn1_w_down': out['v_ffn1_w_down'], 'v_norm_mix': out['v_norm_mix'], 'v_w_in': out['v_w_in'], 'v_q_norm': out['v_q_norm'], 'v_k_norm': out['v_k_norm'], 'v_conv_w': out['v_conv_w'], 'v_w_attn_branch': out['v_w_attn_branch'], 'v_w_conv_branch': out['v_w_conv_branch'], 'v_w_out': out['v_w_out'], 'v_norm_ffn2': out['v_norm_ffn2'], 'v_ffn2_w_gate': out['v_ffn2_w_gate'], 'v_ffn2_w_up': out['v_ffn2_w_up'], 'v_ffn2_w_down': out['v_ffn2_w_down']}


def _loss(weights, diff, rest, loss_target):
    with _jax.named_scope("forward"):
        args = {**rest, TWIN_DIFF_INPUT: diff, **{k: w.astype(_WEIGHT_DTYPES[k]) for k, w in weights.items()}}
        y = _forward(args)
    with _jax.named_scope("loss_head"):
        err = _jnp.square(y.astype(_jnp.float32) - loss_target)
        return 0.5 * _jnp.sum(_jnp.mean(err, axis=-1)) if err.ndim else 0.5 * err


def _adamw(w, g, m, v):
    m = ADAM_B1 * m + (1.0 - ADAM_B1) * g
    v = ADAM_B2 * v + (1.0 - ADAM_B2) * _jnp.square(g)
    m_hat = m / (1.0 - ADAM_B1 ** ADAM_STEP)
    v_hat = v / (1.0 - ADAM_B2 ** ADAM_STEP)
    delta = -ADAM_LR * (m_hat / (_jnp.sqrt(v_hat) + ADAM_EPS) + ADAM_WD * w)
    return delta, m, v


def reference(x, c, w_ada, b_ada, norm_ffn1, ffn1_w_gate, ffn1_w_up, ffn1_w_down, norm_mix, w_in, q_norm, k_norm, conv_w, w_attn_branch, w_conv_branch, w_out, norm_ffn2, ffn2_w_gate, ffn2_w_up, ffn2_w_down, loss_target, m_w_ada, m_b_ada, m_norm_ffn1, m_ffn1_w_gate, m_ffn1_w_up, m_ffn1_w_down, m_norm_mix, m_w_in, m_q_norm, m_k_norm, m_conv_w, m_w_attn_branch, m_w_conv_branch, m_w_out, m_norm_ffn2, m_ffn2_w_gate, m_ffn2_w_up, m_ffn2_w_down, v_w_ada, v_b_ada, v_norm_ffn1, v_ffn1_w_gate, v_ffn1_w_up, v_ffn1_w_down, v_norm_mix, v_w_in, v_q_norm, v_k_norm, v_conv_w, v_w_attn_branch, v_w_conv_branch, v_w_out, v_norm_ffn2, v_ffn2_w_gate, v_ffn2_w_up, v_ffn2_w_down):
    given = dict(x=x, c=c, w_ada=w_ada, b_ada=b_ada, norm_ffn1=norm_ffn1, ffn1_w_gate=ffn1_w_gate, ffn1_w_up=ffn1_w_up, ffn1_w_down=ffn1_w_down, norm_mix=norm_mix, w_in=w_in, q_norm=q_norm, k_norm=k_norm, conv_w=conv_w, w_attn_branch=w_attn_branch, w_conv_branch=w_conv_branch, w_out=w_out, norm_ffn2=norm_ffn2, ffn2_w_gate=ffn2_w_gate, ffn2_w_up=ffn2_w_up, ffn2_w_down=ffn2_w_down, loss_target=loss_target, m_w_ada=m_w_ada, m_b_ada=m_b_ada, m_norm_ffn1=m_norm_ffn1, m_ffn1_w_gate=m_ffn1_w_gate, m_ffn1_w_up=m_ffn1_w_up, m_ffn1_w_down=m_ffn1_w_down, m_norm_mix=m_norm_mix, m_w_in=m_w_in, m_q_norm=m_q_norm, m_k_norm=m_k_norm, m_conv_w=m_conv_w, m_w_attn_branch=m_w_attn_branch, m_w_conv_branch=m_w_conv_branch, m_w_out=m_w_out, m_norm_ffn2=m_norm_ffn2, m_ffn2_w_gate=m_ffn2_w_gate, m_ffn2_w_up=m_ffn2_w_up, m_ffn2_w_down=m_ffn2_w_down, v_w_ada=v_w_ada, v_b_ada=v_b_ada, v_norm_ffn1=v_norm_ffn1, v_ffn1_w_gate=v_ffn1_w_gate, v_ffn1_w_up=v_ffn1_w_up, v_ffn1_w_down=v_ffn1_w_down, v_norm_mix=v_norm_mix, v_w_in=v_w_in, v_q_norm=v_q_norm, v_k_norm=v_k_norm, v_conv_w=v_conv_w, v_w_attn_branch=v_w_attn_branch, v_w_conv_branch=v_w_conv_branch, v_w_out=v_w_out, v_norm_ffn2=v_norm_ffn2, v_ffn2_w_gate=v_ffn2_w_gate, v_ffn2_w_up=v_ffn2_w_up, v_ffn2_w_down=v_ffn2_w_down)
    weights = {n: given[n] for n in TWIN_WEIGHTS}
    shared = {n: given[n] for n in SHARED_INPUTS}
    per_example = {n: given[n] for n in ['x', 'c']}
    grad_fn = _jax.value_and_grad(_loss, argnums=(0, 1))

    def one_microbatch(ex, loss_target):
        ex = dict(ex)
        diff = ex.pop(TWIN_DIFF_INPUT)
        return grad_fn(weights, diff, {**shared, **ex}, loss_target)

    if N_MICROBATCH == 1:
        loss, (grad_w, grad_x) = one_microbatch(per_example, given["loss_target"])
    else:
        def body(carry, xs):
            loss_sum, grad_sum = carry
            l_k, (gw_k, gx_k) = one_microbatch(xs[0], xs[1])
            with _jax.named_scope("update"):
                return (loss_sum + l_k, _jax.tree.map(_jnp.add, grad_sum, gw_k)), gx_k

        init = (_jnp.zeros((), _jnp.float32), _jax.tree.map(_jnp.zeros_like, weights))
        (loss, grad_w), grad_x = _jax.lax.scan(body, init, (per_example, given["loss_target"]))
    with _jax.named_scope("update"):
        delta_w, new_m, new_v = {}, {}, {}
        for n in TWIN_WEIGHTS:
            delta_w[n], new_m[n], new_v[n] = _adamw(weights[n], grad_w[n], given["m_" + n], given["v_" + n])
    return (loss, grad_x, *[grad_w[n] for n in TWIN_WEIGHTS], *[delta_w[n] for n in TWIN_WEIGHTS],
            *[new_m[n] for n in TWIN_WEIGHTS], *[new_v[n] for n in TWIN_WEIGHTS])
```

```python
import functools

import jax
import jax.numpy as jnp
from jax import lax
from jax.experimental import pallas as pl
from jax.experimental.pallas import tpu as pltpu

F32 = jnp.float32
BF16 = jnp.bfloat16
MESH = pl.DeviceIdType.MESH

N_DEV = 8
D = 1024
FF = 2816
HD = 128
N_HEADS = 4
DILATIONS = (1, 4, 16)
BAND = 128
QKW = 2 * 3 * N_HEADS * HD
IN_W = 9728
COL = 512
V_BLK, U_BLK, B_BLK, C_BLK, GA_BLK, GC_BLK = 6, 9, 11, 13, 15, 17
EPS = 1e-6
N_MOD = 9
ADAM_LR, ADAM_B1, ADAM_B2, ADAM_EPS, ADAM_WD, ADAM_STEP = 0.001, 0.9, 0.999, 1e-08, 0.01, 10

NT_DIMS = (((1,), (1,)), ((), ()))
TN_DIMS = (((0,), (0,)), ((), ()))
NN_DIMS = (((1,), (0,)), ((), ()))


def _place():
    return lax.axis_index("x"), lax.axis_index("y"), lax.axis_index("c")


def _flip(coord, bit):
    return 1 - coord if bit else coord


def _params(*sem):
    return pltpu.CompilerParams(dimension_semantics=sem)


def _small_allgather(name, v):
    n = v.shape[-1]

    def body(v_ref, out_ref, send_sems, recv_sems):
        x, y, c = _place()
        me = 4 * x + 2 * y + c
        out_ref[me] = v_ref[...]
        copies = []
        for k in range(1, N_DEV):
            peer = (_flip(x, (k >> 2) & 1), _flip(y, (k >> 1) & 1), _flip(c, k & 1))
            cp = pltpu.make_async_remote_copy(
                src_ref=v_ref, dst_ref=out_ref.at[me], send_sem=send_sems.at[k - 1],
                recv_sem=recv_sems.at[k - 1], device_id=peer, device_id_type=MESH)
            cp.start()
            copies.append(cp)
        for cp in copies:
            cp.wait()

    return pl.pallas_call(
        body, name=name,
        out_shape=jax.ShapeDtypeStruct((N_DEV, 1, n), F32),
        in_specs=[pl.BlockSpec(memory_space=pltpu.VMEM)],
        out_specs=pl.BlockSpec(memory_space=pltpu.VMEM),
        scratch_shapes=[pltpu.SemaphoreType.DMA((N_DEV - 1,)), pltpu.SemaphoreType.DMA((N_DEV - 1,))],
    )(v)


def _allgather_weights(shards, dst_of, base_of, dst_shapes):
    n = len(shards)
    nd = len(dst_shapes)
    rows = [s.shape[0] for s in shards]

    def body(*refs):
        srcs, dsts = refs[:n], refs[n:n + nd]
        send_sems, recv_sems, local_sems = refs[n + nd:]
        x, y, c = _place()
        me, sibling = (x, y, c), (x, y, 1 - c)
        chips = [(1 - x, y), (x, 1 - y), (1 - x, 1 - y)]

        def slab(i, px, py, pc):
            start = pl.multiple_of(base_of[i] + (4 * px + 2 * py + pc) * rows[i], 16)
            return dsts[dst_of[i]].at[pl.ds(start, rows[i])]

        def copy(i, k, block, to, src=None):
            return pltpu.make_async_remote_copy(
                src_ref=slab(i, *block) if src is None else src, dst_ref=slab(i, *block),
                send_sem=send_sems.at[i, k], recv_sem=recv_sems.at[i, k],
                device_id=to, device_id_type=MESH)

        mine = [pltpu.make_async_copy(srcs[i], slab(i, *me), local_sems.at[i]) for i in range(n)]
        for cp in mine:
            cp.start()
        first = []
        for i in range(n):
            first.append(copy(i, 0, me, sibling, src=srcs[i]))
            first += [copy(i, 1 + j, me, (*chip, c), src=srcs[i]) for j, chip in enumerate(chips)]
        for cp in first:
            cp.start()
        passed = []
        for j, chip in enumerate(chips):
            for i in range(n):
                copy(i, 1 + j, (*chip, c), me).wait_recv()
                cp = copy(i, 4 + j, (*chip, c), sibling)
                cp.start()
                passed.append(cp)
        for i in range(n):
            copy(i, 0, sibling, me).wait_recv()
            for j, chip in enumerate(chips):
                copy(i, 4 + j, (*chip, 1 - c), me).wait_recv()
        for cp in first + passed:
            cp.wait_send()
        for cp in mine:
            cp.wait()

    hbm = pl.BlockSpec(memory_space=pltpu.HBM)
    return pl.pallas_call(
        body, name="allgather_weights",
        out_shape=[jax.ShapeDtypeStruct(s, BF16) for s in dst_shapes],
        in_specs=[hbm] * n, out_specs=[hbm] * nd,
        scratch_shapes=[pltpu.SemaphoreType.DMA((n, 7)), pltpu.SemaphoreType.DMA((n, 7)),
                        pltpu.SemaphoreType.DMA((n,))],
    )(*shards)


def _scatter_grads(grads, src_of, base_of, rows, cols):
    n = len(rows)
    ng = len(grads)

    def body(*refs):
        srcs, recvs = refs[:ng], refs[ng:ng + n]
        send_sems, recv_sems, local_sems = refs[ng + n:]
        x, y, c = _place()
        me = 4 * x + 2 * y + c

        def slab(i, idx):
            start = pl.multiple_of(base_of[i] + idx * rows[i], 16)
            return srcs[src_of[i]].at[pl.ds(start, rows[i])]

        mine = [pltpu.make_async_copy(slab(i, me), recvs[i].at[me], local_sems.at[i]) for i in range(n)]
        for cp in mine:
            cp.start()
        copies = []
        for k in range(1, N_DEV):
            px, py, pc = _flip(x, (k >> 2) & 1), _flip(y, (k >> 1) & 1), _flip(c, k & 1)
            for i in range(n):
                cp = pltpu.make_async_remote_copy(
                    src_ref=slab(i, 4 * px + 2 * py + pc), dst_ref=recvs[i].at[me],
                    send_sem=send_sems.at[i, k - 1], recv_sem=recv_sems.at[i, k - 1],
                    device_id=(px, py, pc), device_id_type=MESH)
                cp.start()
                copies.append(cp)
        for cp in copies:
            cp.wait()
        for cp in mine:
            cp.wait()

    hbm = pl.BlockSpec(memory_space=pltpu.HBM)
    return pl.pallas_call(
        body, name="scatter_grads",
        out_shape=[jax.ShapeDtypeStruct((N_DEV, rows[i], cols[i]), BF16) for i in range(n)],
        in_specs=[hbm] * ng, out_specs=[hbm] * n,
        scratch_shapes=[pltpu.SemaphoreType.DMA((n, 7)), pltpu.SemaphoreType.DMA((n, 7)),
                        pltpu.SemaphoreType.DMA((n,))],
    )(*grads)


def _sum_contributions(name, recv):
    _, rows, cols = recv.shape
    tr = rows if rows <= 512 else 304 if rows % 304 == 0 else 256

    def body(r_ref, o_ref):
        acc = r_ref[0].astype(F32)
        for k in range(1, N_DEV):
            acc = acc + r_ref[k].astype(F32)
        o_ref[...] = acc

    return pl.pallas_call(
        body, name=name, grid=(rows // tr,),
        out_shape=jax.ShapeDtypeStruct((rows, cols), F32),
        in_specs=[pl.BlockSpec((N_DEV, tr, cols), lambda i: (0, i, 0))],
        out_specs=pl.BlockSpec((tr, cols), lambda i: (i, 0)),
        compiler_params=_params("parallel"),
    )(recv)


def _mm(name, a, b, mode, out_dtype, tm, tn, tk):
    if mode == "TN":
        kk, m = a.shape
    else:
        m, kk = a.shape
    n = b.shape[0] if mode == "NT" else b.shape[1]
    tm, tn, tk = min(tm, m), min(tn, n), min(tk, kk)
    assert m % tm == 0 and n % tn == 0 and kk % tk == 0, (name, m, n, kk, tm, tn, tk)
    nk = kk // tk
    dims = {"NN": NN_DIMS, "NT": NT_DIMS, "TN": TN_DIMS}[mode]

    def body(a_ref, b_ref, o_ref, acc_ref):
        k = pl.program_id(2)
        part = lax.dot_general(a_ref[...], b_ref[...], dims, preferred_element_type=F32)

        @pl.when(k == 0)
        def _():
            acc_ref[...] = part

        @pl.when(k > 0)
        def _():
            acc_ref[...] += part

        @pl.when(k == nk - 1)
        def _():
            o_ref[...] = acc_ref[...].astype(out_dtype)

    a_spec = (pl.BlockSpec((tk, tm), lambda i, j, k: (k, i)) if mode == "TN"
              else pl.BlockSpec((tm, tk), lambda i, j, k: (i, k)))
    b_spec = (pl.BlockSpec((tn, tk), lambda i, j, k: (j, k)) if mode == "NT"
              else pl.BlockSpec((tk, tn), lambda i, j, k: (k, j)))
    return pl.pallas_call(
        body, name=name, grid=(m // tm, n // tn, nk),
        out_shape=jax.ShapeDtypeStruct((m, n), out_dtype),
        in_specs=[a_spec, b_spec],
        out_specs=pl.BlockSpec((tm, tn), lambda i, j, k: (i, j)),
        scratch_shapes=[pltpu.VMEM((tm, tn), F32)],
        compiler_params=_params("parallel", "parallel", "arbitrary"),
    )(a, b)


def _row(tm, w, off=0):
    return pl.BlockSpec((tm, w), lambda i: (i, off))


def _vec(w):
    return pl.BlockSpec((1, w), lambda i: (0, 0))


def _sigmoid(x):
    return 1.0 / (1.0 + jnp.exp(-x))


def _normmod(name, x, g, sc, sh, tm=256):
    s = x.shape[0]

    def body(x_ref, g_ref, sc_ref, sh_ref, h_ref):
        xv = x_ref[...]
        r = lax.rsqrt(jnp.mean(xv * xv, axis=-1, keepdims=True) + EPS)
        h_ref[...] = ((xv * r) * g_ref[...] * (1.0 + sc_ref[...]) + sh_ref[...]).astype(BF16)

    return pl.pallas_call(
        body, name=name, grid=(s // tm,),
        out_shape=jax.ShapeDtypeStruct((s, D), BF16),
        in_specs=[_row(tm, D), _vec(D), _vec(D), _vec(D)], out_specs=_row(tm, D),
        compiler_params=_params("parallel"),
    )(x, g, sc, sh)


def _normmod_bwd(name, dh, x, gin, g, sc, sh, tm=256):
    s = x.shape[0]

    def body(dh_ref, x_ref, gin_ref, g_ref, sc_ref, sh_ref, gout_ref, acc_ref):
        xv, dhv = x_ref[...], dh_ref[...]
        r = lax.rsqrt(jnp.mean(xv * xv, axis=-1, keepdims=True) + EPS)
        nv = xv * r
        gv, one_sc = g_ref[...], 1.0 + sc_ref[...]
        dn = dhv * gv * one_sc
        dx = r * (dn - nv * jnp.mean(dn * nv, axis=-1, keepdims=True))
        gout_ref[...] = gin_ref[...] + dx

        @pl.when(pl.program_id(0) == 0)
        def _():
            acc_ref[...] = jnp.zeros_like(acc_ref)

        dhn = dhv * nv
        acc_ref[0:1, :] += jnp.sum(dhv, axis=0, keepdims=True)
        acc_ref[1:2, :] += jnp.sum(dhn * gv, axis=0, keepdims=True)
        acc_ref[2:3, :] += jnp.sum(dhn * one_sc, axis=0, keepdims=True)

    return pl.pallas_call(
        body, name=name, grid=(s // tm,),
        out_shape=[jax.ShapeDtypeStruct((s, D), F32), jax.ShapeDtypeStruct((8, D), F32)],
        in_specs=[_row(tm, D), _row(tm, D), _row(tm, D), _vec(D), _vec(D), _vec(D)],
        out_specs=[_row(tm, D), pl.BlockSpec((8, D), lambda i: (0, 0))],
        compiler_params=_params("arbitrary"),
    )(dh, x, gin, g, sc, sh)


def _swiglu(name, ab, tm=256):
    s = ab.shape[0]

    def body(ab_ref, s_ref):
        a = ab_ref[:, :FF].astype(F32)
        b = ab_ref[:, FF:].astype(F32)
        s_ref[...] = (a * _sigmoid(a) * b).astype(BF16)

    return pl.pallas_call(
        body, name=name, grid=(s // tm,),
        out_shape=jax.ShapeDtypeStruct((s, FF), BF16),
        in_specs=[_row(tm, 2 * FF)], out_specs=_row(tm, FF),
        compiler_params=_params("parallel"),
    )(ab)


def _swiglu_bwd(name, ds, ab, tm=256):
    s = ab.shape[0]

    def body(ds_ref, ab_ref, dab_ref):
        a = ab_ref[:, :FF].astype(F32)
        b = ab_ref[:, FF:].astype(F32)
        dsv = ds_ref[...].astype(F32)
        sig = _sigmoid(a)
        dab_ref[:, :FF] = (dsv * b * (sig * (1.0 + a * (1.0 - sig)))).astype(BF16)
        dab_ref[:, FF:] = (dsv * (a * sig)).astype(BF16)

    return pl.pallas_call(
        body, name=name, grid=(s // tm,),
        out_shape=jax.ShapeDtypeStruct((s, 2 * FF), BF16),
        in_specs=[_row(tm, FF), _row(tm, 2 * FF)], out_specs=_row(tm, 2 * FF),
        compiler_params=_params("parallel"),
    )(ds, ab)


def _residual(name, x, f, gt, coef, tm=256):
    s = x.shape[0]

    def body(x_ref, f_ref, gt_ref, o_ref):
        o_ref[...] = x_ref[...] + (coef * gt_ref[...]) * f_ref[...]

    return pl.pallas_call(
        body, name=name, grid=(s // tm,),
        out_shape=jax.ShapeDtypeStruct((s, D), F32),
        in_specs=[_row(tm, D), _row(tm, D), _vec(D)], out_specs=_row(tm, D),
        compiler_params=_params("parallel"),
    )(x, f, gt)


def _gate_bwd(name, gin, f, gt, coef, tm=256):
    s = gin.shape[0]

    def body(g_ref, f_ref, gt_ref, df_ref, acc_ref):
        gv = g_ref[...]
        df_ref[...] = ((coef * gt_ref[...]) * gv).astype(BF16)

        @pl.when(pl.program_id(0) == 0)
        def _():
            acc_ref[...] = jnp.zeros_like(acc_ref)

        acc_ref[0:1, :] += coef * jnp.sum(gv * f_ref[...], axis=0, keepdims=True)

    return pl.pallas_call(
        body, name=name, grid=(s // tm,),
        out_shape=[jax.ShapeDtypeStruct((s, D), BF16), jax.ShapeDtypeStruct((8, D), F32)],
        in_specs=[_row(tm, D), _row(tm, D), _vec(D)],
        out_specs=[_row(tm, D), pl.BlockSpec((8, D), lambda i: (0, 0))],
        compiler_params=_params("arbitrary"),
    )(gin, f, gt)


def _loss_grad(x3, target, tm=256):
    s = x3.shape[0]

    def body(y_ref, t_ref, g_ref, l_ref):
        e = y_ref[...] - t_ref[...]
        g_ref[...] = e * (1.0 / D)

        @pl.when(pl.program_id(0) == 0)
        def _():
            l_ref[...] = jnp.zeros_like(l_ref)

        l_ref[...] += jnp.sum(jnp.mean(e * e, axis=-1, keepdims=True), axis=0, keepdims=True) * 0.5

    return pl.pallas_call(
        body, name="loss_grad", grid=(s // tm,),
        out_shape=[jax.ShapeDtypeStruct((s, D), F32), jax.ShapeDtypeStruct((8, 128), F32)],
        in_specs=[_row(tm, D), _row(tm, D)],
        out_specs=[_row(tm, D), pl.BlockSpec((8, 128), lambda i: (0, 0))],
        compiler_params=_params("arbitrary"),
    )(x3, target)


def _heads(x, fn):
    return jnp.concatenate([fn(x[:, h * HD:(h + 1) * HD], h) for h in range(COL // HD)], axis=1)


def _qknorm(proj, wqk, tm=256):
    s = proj.shape[0]

    def body(p_ref, w_ref, o_ref):
        pv = p_ref[...].astype(F32)
        wv = w_ref[...]

        def one(qh, h):
            r = lax.rsqrt(jnp.mean(qh * qh, axis=-1, keepdims=True) + EPS)
            return (qh * r) * wv[:, h * HD:(h + 1) * HD]

        o_ref[...] = _heads(pv, one).astype(BF16)

    return pl.pallas_call(
        body, name="qknorm", grid=(s // tm, QKW // COL),
        out_shape=jax.ShapeDtypeStruct((s, QKW), BF16),
        in_specs=[pl.BlockSpec((tm, COL), lambda i, j: (i, j)), pl.BlockSpec((1, COL), lambda i, j: (0, j))],
        out_specs=pl.BlockSpec((tm, COL), lambda i, j: (i, j)),
        compiler_params=_params("parallel", "parallel"),
    )(proj, wqk)


def _qknorm_bwd(proj, dqkn, wqk, dproj, tm=256):
    s = proj.shape[0]

    def body(p_ref, d_ref, w_ref, _, o_ref, acc_ref):
        pv = p_ref[...].astype(F32)
        dv = d_ref[...]
        wv = w_ref[...]
        sums = []

        def one(qh, h):
            dn = dv[:, h * HD:(h + 1) * HD]
            r = lax.rsqrt(jnp.mean(qh * qh, axis=-1, keepdims=True) + EPS)
            nh = qh * r
            sums.append(jnp.sum(dn * nh, axis=0, keepdims=True))
            dnw = dn * wv[:, h * HD:(h + 1) * HD]
            return r * (dnw - nh * jnp.mean(dnw * nh, axis=-1, keepdims=True))

        o_ref[...] = _heads(pv, one).astype(BF16)

        @pl.when(pl.program_id(1) == 0)
        def _():
            acc_ref[...] = jnp.zeros_like(acc_ref)

        acc_ref[0:1, :] += jnp.concatenate(sums, axis=1)

    return pl.pallas_call(
        body, name="qknorm_bwd", grid=(QKW // COL, s // tm),
        out_shape=[jax.ShapeDtypeStruct((s, IN_W), BF16), jax.ShapeDtypeStruct((8, QKW), F32)],
        in_specs=[pl.BlockSpec((tm, COL), lambda j, i: (i, j)), pl.BlockSpec((tm, COL), lambda j, i: (i, j)),
                  pl.BlockSpec((1, COL), lambda j, i: (0, j)), pl.BlockSpec(memory_space=pl.ANY)],
        out_specs=[pl.BlockSpec((tm, COL), lambda j, i: (i, j)), pl.BlockSpec((8, COL), lambda j, i: (0, j))],
        input_output_aliases={3: 0},
        compiler_params=_params("arbitrary", "arbitrary"),
    )(proj, dqkn, wqk, dproj)


def _attn_shapes(s, g):
    d = DILATIONS[g]
    length = s // d
    tq = min(512, length)
    sb = min(256, tq)
    assert length % tq == 0 and tq % sb == 0 and sb % BAND == 0
    return d, length, tq, sb


def _lanes(x, width):
    return jnp.concatenate([x] * (width // HD), axis=1)


def _attn_fwd(g, qkn, proj):
    s = qkn.shape[0]
    d, length, tq, sb = _attn_shapes(s, g)
    nj = tq // sb
    qb, vb = QKW // HD, IN_W // HD
    scale = HD ** -0.5

    def body(q_ref, kc_ref, kp_ref, vc_ref, vp_ref, o_ref, lse_ref, kbuf, vbuf):
        n = pl.program_id(2)
        kbuf[0:BAND] = kp_ref[...]
        kbuf[BAND:] = kc_ref[...]
        vbuf[0:BAND] = vp_ref[...]
        vbuf[BAND:] = vc_ref[...]
        for j in range(nj):
            q = q_ref[j * sb:(j + 1) * sb, :]
            k = kbuf[j * sb:j * sb + sb + BAND, :]
            v = vbuf[j * sb:j * sb + sb + BAND, :]
            sc = lax.dot_general(q, k, NT_DIMS, preferred_element_type=F32) * scale
            qi = lax.broadcasted_iota(jnp.int32, sc.shape, 0)
            kj = lax.broadcasted_iota(jnp.int32, sc.shape, 1)
            valid = (kj >= qi) & (kj <= qi + BAND)
            if j == 0:
                valid = valid & ((kj >= BAND) | (n > 0))
            sc = jnp.where(valid, sc, -1e30)
            m = jnp.max(sc, axis=-1, keepdims=True)
            p = jnp.exp(sc - m)
            l = jnp.sum(p, axis=-1, keepdims=True)
            o = lax.dot_general(p.astype(BF16), v, NN_DIMS, preferred_element_type=F32)
            o_ref[j * sb:(j + 1) * sb, :] = o / l
            lse_ref[j * sb:(j + 1) * sb, :] = jnp.broadcast_to(m + jnp.log(l), (sb, HD))

    nb = tq // BAND
    cur = lambda base: pl.BlockSpec((tq, HD), lambda r, h, n: (n, base(r) + g * N_HEADS + h))
    prev = lambda base: pl.BlockSpec(
        (BAND, HD), lambda r, h, n: (jnp.maximum(n * nb - 1, 0), base(r) + g * N_HEADS + h))
    qcol, kcol, vcol = (lambda r: r * qb), (lambda r: r * qb + 12), (lambda r: r * vb + 24)
    out = pl.BlockSpec((tq, HD), lambda r, h, n: (n, r * N_HEADS + h))
    o, lse = pl.pallas_call(
        body, name=f"attn_fwd_g{g}", grid=(d, N_HEADS, length // tq),
        out_shape=[jax.ShapeDtypeStruct((length, d * COL), F32)] * 2,
        in_specs=[cur(qcol), cur(kcol), prev(kcol), cur(vcol), prev(vcol)],
        out_specs=[out, out],
        scratch_shapes=[pltpu.VMEM((tq + BAND, HD), BF16), pltpu.VMEM((tq + BAND, HD), BF16)],
        compiler_params=_params("parallel", "parallel", "arbitrary"),
    )(qkn.reshape(length, d * QKW), qkn.reshape(length, d * QKW), qkn.reshape(length, d * QKW),
      proj.reshape(length, d * IN_W), proj.reshape(length, d * IN_W))
    return o.reshape(s, COL), lse.reshape(s, COL)


def _attn_combine(os_, lses, tm=256):
    s = os_[0].shape[0]

    def body(o0, o1, o2, l0, l1, l2, o_ref, lse_ref):
        a, b, c = l0[...], l1[...], l2[...]
        m = jnp.maximum(jnp.maximum(a, b), c)
        ea, eb, ec = jnp.exp(a - m), jnp.exp(b - m), jnp.exp(c - m)
        tot = ea + eb + ec
        o_ref[...] = ((ea * o0[...] + eb * o1[...] + ec * o2[...]) / tot).astype(BF16)
        lse_ref[...] = m + jnp.log(tot)

    return pl.pallas_call(
        body, name="attn_combine", grid=(s // tm,),
        out_shape=[jax.ShapeDtypeStruct((s, COL), BF16), jax.ShapeDtypeStruct((s, COL), F32)],
        in_specs=[_row(tm, COL)] * 6, out_specs=[_row(tm, COL)] * 2,
        compiler_params=_params("parallel"),
    )(*os_, *lses)


def _attn_delta(do, o, tm=256):
    s = do.shape[0]

    def body(do_ref, o_ref, del_ref, dob_ref):
        dov = do_ref[...]
        prod = dov * o_ref[...].astype(F32)
        del_ref[...] = _heads(prod, lambda ph, h: jnp.broadcast_to(
            jnp.sum(ph, axis=-1, keepdims=True), ph.shape))
        dob_ref[...] = dov.astype(BF16)

    return pl.pallas_call(
        body, name="attn_delta", grid=(s // tm,),
        out_shape=[jax.ShapeDtypeStruct((s, COL), F32), jax.ShapeDtypeStruct((s, COL), BF16)],
        in_specs=[_row(tm, COL)] * 2, out_specs=[_row(tm, COL)] * 2,
        compiler_params=_params("parallel"),
    )(do, o)


def _attn_dq(g, qkn, proj, dob, lse, delta, dqkn):
    s = qkn.shape[0]
    d, length, tq, sb = _attn_shapes(s, g)
    nj = tq // sb
    qb, vb = QKW // HD, IN_W // HD
    scale = HD ** -0.5
    chained = dqkn is not None

    def body(q_ref, kc_ref, kp_ref, vc_ref, vp_ref, do_ref, lse_ref, del_ref, *rest):
        dq_ref, kbuf, vbuf = rest[-3:]
        n = pl.program_id(2)
        kbuf[0:BAND] = kp_ref[...]
        kbuf[BAND:] = kc_ref[...]
        vbuf[0:BAND] = vp_ref[...]
        vbuf[BAND:] = vc_ref[...]
        for j in range(nj):
            rows = slice(j * sb, (j + 1) * sb)
            k = kbuf[j * sb:j * sb + sb + BAND, :]
            v = vbuf[j * sb:j * sb + sb + BAND, :]
            sc = lax.dot_general(q_ref[rows, :], k, NT_DIMS, preferred_element_type=F32) * scale
            qi = lax.broadcasted_iota(jnp.int32, sc.shape, 0)
            kj = lax.broadcasted_iota(jnp.int32, sc.shape, 1)
            valid = (kj >= qi) & (kj <= qi + BAND)
            if j == 0:
                valid = valid & ((kj >= BAND) | (n > 0))
            p = jnp.exp(jnp.where(valid, sc - _lanes(lse_ref[rows, :], sb + BAND), -1e30))
            dp = lax.dot_general(do_ref[rows, :], v, NT_DIMS, preferred_element_type=F32)
            ds = p * (dp - _lanes(del_ref[rows, :], sb + BAND)) * scale
            dq_ref[rows, :] = lax.dot_general(ds.astype(BF16), k, NN_DIMS, preferred_element_type=F32)

    nb = tq // BAND
    cur = lambda base: pl.BlockSpec((tq, HD), lambda r, h, n: (n, base(r) + g * N_HEADS + h))
    prev = lambda base: pl.BlockSpec(
        (BAND, HD), lambda r, h, n: (jnp.maximum(n * nb - 1, 0), base(r) + g * N_HEADS + h))
    qcol, kcol, vcol = (lambda r: r * qb), (lambda r: r * qb + 12), (lambda r: r * vb + 24)
    tok = pl.BlockSpec((tq, HD), lambda r, h, n: (n, r * N_HEADS + h))
    qv, pv = qkn.reshape(length, d * QKW), proj.reshape(length, d * IN_W)
    tv = lambda a: a.reshape(length, d * COL)
    args = [qv, qv, qv, pv, pv, tv(dob), tv(lse), tv(delta)]
    specs = [cur(qcol), cur(kcol), prev(kcol), cur(vcol), prev(vcol), tok, tok, tok]
    if chained:
        args.append(dqkn.reshape(length, d * QKW))
        specs.append(pl.BlockSpec(memory_space=pl.ANY))
    out = pl.pallas_call(
        body, name=f"attn_dq_g{g}", grid=(d, N_HEADS, length // tq),
        out_shape=jax.ShapeDtypeStruct((length, d * QKW), F32),
        in_specs=specs, out_specs=cur(qcol),
        input_output_aliases={8: 0} if chained else {},
        scratch_shapes=[pltpu.VMEM((tq + BAND, HD), BF16), pltpu.VMEM((tq + BAND, HD), BF16)],
        compiler_params=_params("arbitrary", "arbitrary", "arbitrary"),
    )(*args)
    return out.reshape(s, QKW)


def _attn_dkv(g, qkn, proj, dob, lse, delta, dqkn, dproj):
    s = qkn.shape[0]
    d, length, tq, sb = _attn_shapes(s, g)
    nj = tq // sb
    nt = length // tq
    qb, vb = QKW // HD, IN_W // HD
    scale = HD ** -0.5

    def body(k_ref, v_ref, qc_ref, qn_ref, doc_ref, don_ref, lc_ref, ln_ref, dc_ref, dn_ref, _a, _b,
             dk_ref, dv_ref, qbuf, dobuf, lbuf, dbuf):
        n = pl.program_id(2)
        for buf, c_ref, n_ref in ((qbuf, qc_ref, qn_ref), (dobuf, doc_ref, don_ref),
                                  (lbuf, lc_ref, ln_ref), (dbuf, dc_ref, dn_ref)):
            buf[0:tq] = c_ref[...]
            buf[tq:] = n_ref[...]
        for j in range(nj):
            rows = slice(j * sb, (j + 1) * sb)
            qrows = slice(j * sb, j * sb + sb + BAND)
            q, do = qbuf[qrows, :], dobuf[qrows, :]
            sc = lax.dot_general(q, k_ref[rows, :], NT_DIMS, preferred_element_type=F32) * scale
            qi = lax.broadcasted_iota(jnp.int32, sc.shape, 0)
            kj = lax.broadcasted_iota(jnp.int32, sc.shape, 1)
            valid = (qi >= kj) & (qi <= kj + BAND)
            if j == nj - 1:
                valid = valid & ((qi < sb) | (n < nt - 1))
            p = jnp.exp(jnp.where(valid, sc - _lanes(lbuf[qrows, :], sb), -1e30))
            dp = lax.dot_general(do, v_ref[rows, :], NT_DIMS, preferred_element_type=F32)
            ds = p * (dp - _lanes(dbuf[qrows, :], sb)) * scale
            dv_ref[rows, :] = lax.dot_general(
                p.astype(BF16), do, TN_DIMS, preferred_element_type=F32).astype(BF16)
            dk_ref[rows, :] = lax.dot_general(ds.astype(BF16), q, TN_DIMS, preferred_element_type=F32)

    nb = tq // BAND
    last = length // BAND - 1
    cur = lambda base: pl.BlockSpec((tq, HD), lambda r, h, n: (n, base(r) + g * N_HEADS + h))
    nxt = lambda base: pl.BlockSpec(
        (BAND, HD), lambda r, h, n: (jnp.minimum((n + 1) * nb, last), base(r) + g * N_HEADS + h))
    qcol, kcol, vcol = (lambda r: r * qb), (lambda r: r * qb + 12), (lambda r: r * vb + 24)
    tokc = pl.BlockSpec((tq, HD), lambda r, h, n: (n, r * N_HEADS + h))
    tokn = pl.BlockSpec((BAND, HD), lambda r, h, n: (jnp.minimum((n + 1) * nb, last), r * N_HEADS + h))
    qv, pv = qkn.reshape(length, d * QKW), proj.reshape(length, d * IN_W)
    tv = lambda a: a.reshape(length, d * COL)
    anyspec = pl.BlockSpec(memory_space=pl.ANY)
    dk, dv = pl.pallas_call(
        body, name=f"attn_dkv_g{g}", grid=(d, N_HEADS, nt),
        out_shape=[jax.ShapeDtypeStruct((length, d * QKW), F32), jax.ShapeDtypeStruct((length, d * IN_W), BF16)],
        in_specs=[cur(kcol), cur(vcol), cur(qcol), nxt(qcol), tokc, tokn, tokc, tokn, tokc, tokn,
                  anyspec, anyspec],
        out_specs=[cur(kcol), cur(vcol)],
        input_output_aliases={10: 0, 11: 1},
        scratch_shapes=[pltpu.VMEM((tq + BAND, HD), BF16), pltpu.VMEM((tq + BAND, HD), BF16),
                        pltpu.VMEM((tq + BAND, HD), F32), pltpu.VMEM((tq + BAND, HD), F32)],
        compiler_params=_params("arbitrary", "arbitrary", "arbitrary"),
    )(qv, pv, qv, qv, tv(dob), tv(dob), tv(lse), tv(lse), tv(delta), tv(delta),
      dqkn.reshape(length, d * QKW), dproj.reshape(length, d * IN_W))
    return dk.reshape(s, QKW), dv.reshape(s, IN_W)


def _shift_down(x, before, k):
    rolled = pltpu.roll(x, k, 0)
    head = jnp.where(lax.broadcasted_iota(jnp.int32, before.shape, 0) < k, pltpu.roll(before, k, 0), rolled[:8])
    return jnp.concatenate([head, rolled[8:]], axis=0)


def _shift_up(x, after, k):
    rows = x.shape[0]
    rolled = pltpu.roll(x, rows - k, 0)
    tail = jnp.where(lax.broadcasted_iota(jnp.int32, after.shape, 0) >= 8 - k,
                     pltpu.roll(after, 8 - k, 0), rolled[rows - 8:])
    return jnp.concatenate([rolled[:rows - 8], tail], axis=0)


def _conv_fwd(proj, cw, tm=256):
    s = proj.shape[0]
    r16 = tm // 16

    def body(u_ref, b_ref, c_ref, up_ref, cp_ref, w_ref, z_ref):
        i = pl.program_id(1)
        xc = c_ref[...].astype(F32) * u_ref[...].astype(F32)
        xp = jnp.where(i > 0, cp_ref[8:16, :].astype(F32) * up_ref[8:16, :].astype(F32), 0.0)
        w = w_ref[...]
        conv = _shift_down(xc, xp, 2) * w[0:1] + _shift_down(xc, xp, 1) * w[1:2] + xc * w[2:3]
        z_ref[...] = (b_ref[...].astype(F32) * conv).astype(BF16)

    tile = lambda blk: pl.BlockSpec((tm, COL), lambda j, i: (i, blk + j))
    before = lambda blk: pl.BlockSpec((16, COL), lambda j, i: (jnp.maximum(i * r16 - 1, 0), blk + j))
    return pl.pallas_call(
        body, name="conv_fwd", grid=(D // COL, s // tm),
        out_shape=jax.ShapeDtypeStruct((s, D), BF16),
        in_specs=[tile(U_BLK), tile(B_BLK), tile(C_BLK), before(U_BLK), before(C_BLK),
                  pl.BlockSpec((3, COL), lambda j, i: (0, j))],
        out_specs=pl.BlockSpec((tm, COL), lambda j, i: (i, j)),
        compiler_params=_params("parallel", "parallel"),
    )(proj, proj, proj, proj, proj, cw)


def _conv_bwd(dz, proj, cw, dproj, tm=256):
    s = proj.shape[0]
    r8, r16 = tm // 8, tm // 16
    nrow = s // tm

    def body(dz_ref, u_ref, b_ref, c_ref, up_ref, cp_ref, dzn_ref, bn_ref, w_ref, _, o_ref, acc_ref):
        piece, i = pl.program_id(1), pl.program_id(2)
        u, c = u_ref[...].astype(F32), c_ref[...].astype(F32)
        bv = b_ref[...].astype(F32)
        dzv = dz_ref[...]
        w = w_ref[...]

        @pl.when((piece == 0) & (i == 0))
        def _():
            acc_ref[...] = jnp.zeros_like(acc_ref)

        @pl.when(piece == 1)
        def _():
            xc = c * u
            xp = jnp.where(i > 0, cp_ref[8:16, :].astype(F32) * up_ref[8:16, :].astype(F32), 0.0)
            x2, x1 = _shift_down(xc, xp, 2), _shift_down(xc, xp, 1)
            o_ref[...] = (dzv * (x2 * w[0:1] + x1 * w[1:2] + xc * w[2:3])).astype(BF16)
            dconv = dzv * bv
            acc_ref[0:1, :] += jnp.sum(dconv * x2, axis=0, keepdims=True)
            acc_ref[1:2, :] += jnp.sum(dconv * x1, axis=0, keepdims=True)
            acc_ref[2:3, :] += jnp.sum(dconv * xc, axis=0, keepdims=True)

        @pl.when(piece != 1)
        def _():
            dconv = dzv * bv
            dn = jnp.where(i < nrow - 1, dzn_ref[...] * bn_ref[0:8, :].astype(F32), 0.0)
            dxc = dconv * w[2:3] + _shift_up(dconv, dn, 1) * w[1:2] + _shift_up(dconv, dn, 2) * w[0:1]
            o_ref[...] = (dxc * jnp.where(piece == 0, c, u)).astype(BF16)

    tile = lambda blk: pl.BlockSpec((tm, COL), lambda j, p, i: (i, blk + j))
    before = lambda blk: pl.BlockSpec((16, COL), lambda j, p, i: (jnp.maximum(i * r16 - 1, 0), blk + j))
    after = lambda rows, blk: pl.BlockSpec(
        (rows, COL), lambda j, p, i: (jnp.minimum((i + 1) * (tm // rows), s // rows - 1), blk + j))
    return pl.pallas_call(
        body, name="conv_bwd", grid=(D // COL, 3, nrow),
        out_shape=[jax.ShapeDtypeStruct((s, IN_W), BF16), jax.ShapeDtypeStruct((8, D), F32)],
        in_specs=[tile(0), tile(U_BLK), tile(B_BLK), tile(C_BLK), before(U_BLK), before(C_BLK),
                  after(8, 0), after(16, B_BLK), pl.BlockSpec((3, COL), lambda j, p, i: (0, j)),
                  pl.BlockSpec(memory_space=pl.ANY)],
        out_specs=[pl.BlockSpec((tm, COL), lambda j, p, i: (i, U_BLK + 2 * p + j)),
                   pl.BlockSpec((8, COL), lambda j, p, i: (0, j))],
        input_output_aliases={9: 0},
        compiler_params=_params("arbitrary", "arbitrary", "arbitrary"),
    )(dz, proj, proj, proj, proj, proj, dz, proj, cw, dproj)


def _merge_fwd(ya, yc, proj, tm=256):
    s = proj.shape[0]

    def body(ya_ref, yc_ref, ga_ref, gc_ref, o_ref):
        o_ref[...] = (_sigmoid(ga_ref[...].astype(F32)) * ya_ref[...].astype(F32)
                      + _sigmoid(gc_ref[...].astype(F32)) * yc_ref[...].astype(F32)).astype(BF16)

    tile = lambda blk: pl.BlockSpec((tm, COL), lambda j, i: (i, blk + j))
    return pl.pallas_call(
        body, name="merge_fwd", grid=(D // COL, s // tm),
        out_shape=jax.ShapeDtypeStruct((s, D), BF16),
        in_specs=[tile(0), tile(0), tile(GA_BLK), tile(GC_BLK)], out_specs=tile(0),
        compiler_params=_params("parallel", "parallel"),
    )(ya, yc, proj, proj)


def _merge_bwd_branches(dm, proj, tm=256):
    s = proj.shape[0]

    def body(dm_ref, ga_ref, gc_ref, dya_ref, dyc_ref):
        dmv = dm_ref[...]
        dya_ref[...] = (dmv * _sigmoid(ga_ref[...].astype(F32))).astype(BF16)
        dyc_ref[...] = (dmv * _sigmoid(gc_ref[...].astype(F32))).astype(BF16)

    tile = lambda blk: pl.BlockSpec((tm, COL), lambda j, i: (i, blk + j))
    return pl.pallas_call(
        body, name="merge_bwd_branches", grid=(D // COL, s // tm),
        out_shape=[jax.ShapeDtypeStruct((s, D), BF16)] * 2,
        in_specs=[tile(0), tile(GA_BLK), tile(GC_BLK)], out_specs=[tile(0)] * 2,
        compiler_params=_params("parallel", "parallel"),
    )(dm, proj, proj)


def _merge_bwd_gates(dm, ya, yc, proj, tm=256):
    s = proj.shape[0]
    half = D // COL

    def body(dm_ref, ya_ref, yc_ref, g_ref, o_ref):
        y = jnp.where(pl.program_id(0) < half, ya_ref[...].astype(F32), yc_ref[...].astype(F32))
        sig = _sigmoid(g_ref[...].astype(F32))
        o_ref[...] = (dm_ref[...] * y * sig * (1.0 - sig)).astype(BF16)

    chan = pl.BlockSpec((tm, COL), lambda jj, i: (i, jj % half))
    gate = pl.BlockSpec((tm, COL), lambda jj, i: (i, GA_BLK + jj))
    return pl.pallas_call(
        body, name="merge_bwd_gates", grid=(2 * half, s // tm),
        out_shape=jax.ShapeDtypeStruct((s, IN_W), BF16),
        in_specs=[chan, chan, chan, gate], out_specs=gate,
        compiler_params=_params("parallel", "parallel"),
    )(dm, ya, yc, proj)


def _mod_part(c_all, w_ada, b_part):
    def body(c_ref, w_ref, b_ref, o_ref):
        cv = c_ref[...]
        act = cv * _sigmoid(cv)
        o_ref[...] = jnp.dot(act, w_ref[...], preferred_element_type=F32,
                             precision=lax.Precision.HIGHEST) + b_ref[...]

    return pl.pallas_call(
        body, name="mod_part", out_shape=jax.ShapeDtypeStruct((N_DEV, w_ada.shape[1]), F32),
    )(c_all, w_ada, b_part)


def _w_ada_grad(c_all_t, dmod_part):
    def body(c_ref, d_ref, o_ref):
        cv = c_ref[...]
        act = cv * _sigmoid(cv)
        dv = d_ref[...]
        acc = act[:, 0:1] * dv[0:1, :]
        for b in range(1, N_DEV):
            acc = acc + act[:, b:b + 1] * dv[b:b + 1, :]
        o_ref[...] = acc

    return pl.pallas_call(
        body, name="w_ada_grad", out_shape=jax.ShapeDtypeStruct((D, dmod_part.shape[1]), F32),
    )(c_all_t, dmod_part)


def _sum_rows(name, v):
    def body(v_ref, o_ref):
        acc = v_ref[0]
        for k in range(1, N_DEV):
            acc = acc + v_ref[k]
        o_ref[...] = acc

    return pl.pallas_call(body, name=name, out_shape=jax.ShapeDtypeStruct(v.shape[1:], F32))(v)


def _adamw(name, w, g, m, v):
    rows, cols = w.shape
    tr = 256 if rows % 256 == 0 and rows * cols > 512 * 1024 else rows
    c1 = 1.0 - ADAM_B1 ** ADAM_STEP
    c2 = 1.0 - ADAM_B2 ** ADAM_STEP

    def body(w_ref, g_ref, m_ref, v_ref, d_ref, nm_ref, nv_ref):
        gv = g_ref[...]
        nm = ADAM_B1 * m_ref[...] + (1.0 - ADAM_B1) * gv
        nv = ADAM_B2 * v_ref[...] + (1.0 - ADAM_B2) * (gv * gv)
        nm_ref[...] = nm
        nv_ref[...] = nv
        d_ref[...] = -ADAM_LR * ((nm / c1) / (jnp.sqrt(nv / c2) + ADAM_EPS) + ADAM_WD * w_ref[...])

    spec = pl.BlockSpec((tr, cols), lambda i: (i, 0))
    return pl.pallas_call(
        body, name=name, grid=(rows // tr,),
        out_shape=[jax.ShapeDtypeStruct((rows, cols), F32)] * 3,
        in_specs=[spec] * 4, out_specs=[spec] * 3,
        compiler_params=_params("parallel"),
    )(w, g, m, v)


def _ffn_fwd(tag, x, g, sc, sh, gt, wgu, wd):
    h = _normmod(f"{tag}_normmod", x, g, sc, sh)
    ab = _mm(f"{tag}_gate_up", h, wgu, "NT", BF16, 1024, 512, 1024)
    sw = _swiglu(f"{tag}_swiglu", ab)
    f = _mm(f"{tag}_down", sw, wd, "NN", F32, 1024, 1024, FF)
    return _residual(f"{tag}_residual", x, f, gt, 0.5), (h, ab, sw, f)


def _ffn_bwd(tag, gout, x, saved, g, sc, sh, gt, wgu, wd):
    h, ab, sw, f = saved
    df, gt_acc = _gate_bwd(f"{tag}_gate_bwd", gout, f, gt, 0.5)
    ds = _mm(f"{tag}_d_hidden", df, wd, "NT", BF16, 1024, 1408, 1024)
    dwd = _mm(f"{tag}_dw_down", sw, df, "TN", BF16, 1408, 1024, 512)
    dab = _swiglu_bwd(f"{tag}_swiglu_bwd", ds, ab)
    dwgu = _mm(f"{tag}_dw_gate_up", dab, h, "TN", BF16, 1408, 1024, 512)
    dh = _mm(f"{tag}_d_h", dab, wgu, "NN", F32, 1024, 1024, 512)
    gin, acc = _normmod_bwd(f"{tag}_normmod_bwd", dh, x, gout, g, sc, sh)
    return gin, acc, gt_acc[0:1], dwgu, dwd


def kernel(x, c, w_ada, b_ada, norm_ffn1, ffn1_w_gate, ffn1_w_up, ffn1_w_down, norm_mix, w_in, q_norm, k_norm, conv_w, w_attn_branch, w_conv_branch, w_out, norm_ffn2, ffn2_w_gate, ffn2_w_up, ffn2_w_down, loss_target, m_w_ada, m_b_ada, m_norm_ffn1, m_ffn1_w_gate, m_ffn1_w_up, m_ffn1_w_down, m_norm_mix, m_w_in, m_q_norm, m_k_norm, m_conv_w, m_w_attn_branch, m_w_conv_branch, m_w_out, m_norm_ffn2, m_ffn2_w_gate, m_ffn2_w_up, m_ffn2_w_down, v_w_ada, v_b_ada, v_norm_ffn1, v_ffn1_w_gate, v_ffn1_w_up, v_ffn1_w_down, v_norm_mix, v_w_in, v_q_norm, v_k_norm, v_conv_w, v_w_attn_branch, v_w_conv_branch, v_w_out, v_norm_ffn2, v_ffn2_w_gate, v_ffn2_w_up, v_ffn2_w_down):
    me = 4 * lax.axis_index("x") + 2 * lax.axis_index("y") + lax.axis_index("c")
    x0, target = x[0], loss_target[0]
    s = x0.shape[0]
    ada_cols = w_ada.shape[2]
    cw_cols = conv_w.shape[2]

    gathered = _small_allgather(
        "gather_c_conv", jnp.concatenate([c, conv_w[0].reshape(1, 3 * cw_cols)], axis=1))[:, 0]
    c_all = gathered[:, :D]
    cw = gathered[:, D:].reshape(N_DEV, 3, cw_cols).transpose(1, 0, 2).reshape(3, D)
    b_part = lax.dynamic_slice(b_ada, (0, me * ada_cols), (1, ada_cols))
    mod_part = _mod_part(c_all, w_ada[0], b_part)
    mod_all = _small_allgather("gather_mod", mod_part.reshape(1, N_DEV * ada_cols))
    mod = lax.dynamic_slice(mod_all.reshape(N_DEV, N_DEV, ada_cols), (0, me, 0), (N_DEV, 1, ada_cols))
    mod = mod.reshape(N_MOD, 1, D)
    sh1, sc1, gt1, sh2, sc2, gt2, sh3, sc3, gt3 = [mod[i] for i in range(N_MOD)]

    tb = lambda w: w[0].T.astype(BF16)
    nb = lambda w: w[0].astype(BF16)
    shards = [tb(ffn1_w_gate), tb(ffn1_w_up), nb(ffn1_w_down), tb(ffn2_w_gate), tb(ffn2_w_up), nb(ffn2_w_down),
              tb(w_in), tb(w_attn_branch), nb(w_conv_branch), nb(w_out)]
    dst_of = [0, 0, 1, 2, 2, 3, 4, 5, 6, 7]
    base_of = [0, FF, 0, 0, FF, 0, 0, 0, 0, 0]
    dst_shapes = [(2 * FF, D), (FF, D), (2 * FF, D), (FF, D), (IN_W, D), (D, COL), (D, D), (D, D)]
    wgu1, wd1, wgu2, wd2, win_t, wa_t, wc, wo = _allgather_weights(shards, dst_of, base_of, dst_shapes)

    x1, saved1 = _ffn_fwd("ffn1", x0, norm_ffn1, sc1, sh1, gt1, wgu1, wd1)
    h2 = _normmod("mix_normmod", x1, norm_mix, sc2, sh2)
    proj = _mm("mix_in_proj", h2, win_t, "NT", BF16, 1024, 512, 1024)
    wqk = jnp.concatenate([jnp.tile(q_norm, (1, 12)), jnp.tile(k_norm, (1, 12))], axis=1)
    qkn = _qknorm(proj, wqk)
    group_out = [_attn_fwd(g, qkn, proj) for g in range(3)]
    o, lse = _attn_combine([go[0] for go in group_out], [go[1] for go in group_out])
    ya = _mm("mix_attn_branch", o, wa_t, "NT", BF16, 1024, 1024, COL)
    z = _conv_fwd(proj, cw)
    yc = _mm("mix_conv_branch", z, wc, "NN", BF16, 1024, 1024, D)
    merged = _merge_fwd(ya, yc, proj)
    mix = _mm("mix_out_proj", merged, wo, "NN", F32, 1024, 1024, D)
    x2 = _residual("mix_residual", x1, mix, gt2, 1.0)
    x3, saved3 = _ffn_fwd("ffn2", x2, norm_ffn2, sc3, sh3, gt3, wgu2, wd2)
    g3, loss_part = _loss_grad(x3, target)
    loss = lax.psum(loss_part[0, 0], ("x", "y", "c"))

    g2, acc3, dgt3, dwgu2, dwd2 = _ffn_bwd("ffn2", g3, x2, saved3, norm_ffn2, sc3, sh3, gt3, wgu2, wd2)
    dmix, gt2_acc = _gate_bwd("mix_gate_bwd", g2, mix, gt2, 1.0)
    dmerged = _mm("mix_d_merged", dmix, wo, "NT", F32, 1024, 1024, D)
    dwo = _mm("mix_dw_out", merged, dmix, "TN", BF16, 1024, 1024, 512)
    dya, dyc = _merge_bwd_branches(dmerged, proj)
    dproj = _merge_bwd_gates(dmerged, ya, yc, proj)
    dwc = _mm("mix_dw_conv_branch", z, dyc, "TN", BF16, 1024, 1024, 512)
    dz = _mm("mix_d_z", dyc, wc, "NT", F32, 1024, 1024, D)
    dproj, cw_acc = _conv_bwd(dz, proj, cw, dproj)
    dwa_t = _mm("mix_dw_attn_branch", dya, o, "TN", BF16, 1024, COL, 512)
    do = _mm("mix_d_o", dya, wa_t, "NN", F32, 1024, COL, D)
    delta, dob = _attn_delta(do, o)
    dqkn = None
    for g in range(3):
        dqkn = _attn_dq(g, qkn, proj, dob, lse, delta, dqkn)
    for g in range(3):
        dqkn, dproj = _attn_dkv(g, qkn, proj, dob, lse, delta, dqkn, dproj)
    dproj, wqk_acc = _qknorm_bwd(proj, dqkn, wqk, dproj)
    dwin_t = _mm("mix_dw_in", dproj, h2, "TN", BF16, 2432, 1024, 512)
    dh2 = _mm("mix_d_h", dproj, win_t, "NN", F32, 1024, 1024, 512)
    g1, acc2 = _normmod_bwd("mix_normmod_bwd", dh2, x1, g2, norm_mix, sc2, sh2)
    g0, acc1, dgt1, dwgu1, dwd1 = _ffn_bwd("ffn1", g1, x0, saved1, norm_ffn1, sc1, sh1, gt1, wgu1, wd1)

    dqw = jnp.sum(wqk_acc[0, :QKW // 2].reshape(12, HD), axis=0)
    dkw = jnp.sum(wqk_acc[0, QKW // 2:].reshape(12, HD), axis=0)
    small = jnp.concatenate([
        acc1[0], acc1[1], dgt1[0], acc2[0], acc2[1], gt2_acc[0], acc3[0], acc3[1], dgt3[0],
        acc1[2], acc2[2], acc3[2], dqw, dkw, cw_acc[0:3].reshape(3 * D)]).reshape(1, -1)
    small_all = _small_allgather("gather_small_grads", small)
    small_sum = _sum_rows("sum_small_grads", small_all)[0]
    n_mod = N_MOD * D
    g_b_ada = small_sum[:n_mod].reshape(1, n_mod)
    g_norm1, g_norm2, g_norm3 = [small_sum[n_mod + i * D:n_mod + (i + 1) * D].reshape(1, D) for i in range(3)]
    off = n_mod + 3 * D
    g_qn, g_kn = small_sum[off:off + HD].reshape(1, HD), small_sum[off + HD:off + 2 * HD].reshape(1, HD)
    g_cw_full = small_sum[off + 2 * HD:].reshape(3, D)
    g_cw = lax.dynamic_slice(g_cw_full, (0, me * cw_cols), (3, cw_cols))
    dmod_part = lax.dynamic_slice(small_all[:, 0, :n_mod], (0, me * ada_cols), (N_DEV, ada_cols))
    g_w_ada = _w_ada_grad(c_all.T, dmod_part)

    grads = [dwgu1, dwd1, dwgu2, dwd2, dwin_t, dwa_t, dwc, dwo]
    src_of = [0, 0, 1, 2, 2, 3, 4, 5, 6, 7]
    rows = [sh_.shape[0] for sh_ in shards]
    cols = [sh_.shape[1] for sh_ in shards]
    recvs = _scatter_grads(grads, src_of, base_of, rows, cols)
    names = ["ffn1_gate", "ffn1_up", "ffn1_down", "ffn2_gate", "ffn2_up", "ffn2_down",
             "w_in", "attn_branch", "conv_branch", "w_out"]
    sums = [_sum_contributions(f"sum_{nm}", r) for nm, r in zip(names, recvs)]
    transposed = [True, True, False, True, True, False, True, True, False, False]
    gw = [sm.T if t else sm for sm, t in zip(sums, transposed)]
    g_f1g, g_f1u, g_f1d, g_f2g, g_f2u, g_f2d, g_win, g_wa, g_wc, g_wo = gw

    grad_list = [g_w_ada[None], g_b_ada, g_norm1, g_f1g[None], g_f1u[None], g_f1d[None], g_norm2, g_win[None],
                 g_qn, g_kn, g_cw[None], g_wa[None], g_wc[None], g_wo[None], g_norm3,
                 g_f2g[None], g_f2u[None], g_f2d[None]]
    weights = [w_ada, b_ada, norm_ffn1, ffn1_w_gate, ffn1_w_up, ffn1_w_down, norm_mix, w_in, q_norm, k_norm,
               conv_w, w_attn_branch, w_conv_branch, w_out, norm_ffn2, ffn2_w_gate, ffn2_w_up, ffn2_w_down]
    ms = [m_w_ada, m_b_ada, m_norm_ffn1, m_ffn1_w_gate, m_ffn1_w_up, m_ffn1_w_down, m_norm_mix, m_w_in, m_q_norm,
          m_k_norm, m_conv_w, m_w_attn_branch, m_w_conv_branch, m_w_out, m_norm_ffn2, m_ffn2_w_gate,
          m_ffn2_w_up, m_ffn2_w_down]
    vs = [v_w_ada, v_b_ada, v_norm_ffn1, v_ffn1_w_gate, v_ffn1_w_up, v_ffn1_w_down, v_norm_mix, v_w_in, v_q_norm,
          v_k_norm, v_conv_w, v_w_attn_branch, v_w_conv_branch, v_w_out, v_norm_ffn2, v_ffn2_w_gate,
          v_ffn2_w_up, v_ffn2_w_down]
    wnames = ["w_ada", "b_ada", "norm_ffn1", "ffn1_w_gate", "ffn1_w_up", "ffn1_w_down", "norm_mix", "w_in",
              "q_norm", "k_norm", "conv_w", "w_attn_branch", "w_conv_branch", "w_out", "norm_ffn2",
              "ffn2_w_gate", "ffn2_w_up", "ffn2_w_down"]
    deltas, new_ms, new_vs = [], [], []
    for nm, w, gr, m_, v_ in zip(wnames, weights, grad_list, ms, vs):
        two_d = (-1, w.shape[-1])
        dl, nm_, nv_ = _adamw(f"adamw_{nm}", w.reshape(two_d), gr.reshape(two_d), m_.reshape(two_d), v_.reshape(two_d))
        deltas.append(dl.reshape(w.shape))
        new_ms.append(nm_.reshape(w.shape))
        new_vs.append(nv_.reshape(w.shape))
    grad_out = [gr.reshape(w.shape) for gr, w in zip(grad_list, weights)]
    return (loss, g0[None], *grad_out, *deltas, *new_ms, *new_vs)
```

```python
import functools

import jax
import jax.numpy as jnp
from jax import lax
from jax.experimental import pallas as pl
from jax.experimental.pallas import tpu as pltpu

F32 = jnp.float32
BF16 = jnp.bfloat16
MESH = pl.DeviceIdType.MESH

N_DEV = 8
D = 1024
FF = 2816
HD = 128
N_HEADS = 4
DILATIONS = (1, 4, 16)
BAND = 128
QKW = 2 * 3 * N_HEADS * HD
IN_W = 9728
COL = 512
V_BLK, U_BLK, B_BLK, C_BLK, GA_BLK, GC_BLK = 6, 9, 11, 13, 15, 17
EPS = 1e-6
N_MOD = 9
ADAM_LR, ADAM_B1, ADAM_B2, ADAM_EPS, ADAM_WD, ADAM_STEP = 0.001, 0.9, 0.999, 1e-08, 0.01, 10

NT_DIMS = (((1,), (1,)), ((), ()))
TN_DIMS = (((0,), (0,)), ((), ()))
NN_DIMS = (((1,), (0,)), ((), ()))


def _place():
    return lax.axis_index("x"), lax.axis_index("y"), lax.axis_index("c")


def _flip(coord, bit):
    return 1 - coord if bit else coord


def _params(*sem):
    return pltpu.CompilerParams(dimension_semantics=sem)


def _small_allgather(name, v):
    n = v.shape[-1]

    def body(v_ref, out_ref, send_sems, recv_sems):
        x, y, c = _place()
        me = 4 * x + 2 * y + c
        out_ref[me] = v_ref[...]
        copies = []
        for k in range(1, N_DEV):
            peer = (_flip(x, (k >> 2) & 1), _flip(y, (k >> 1) & 1), _flip(c, k & 1))
            cp = pltpu.make_async_remote_copy(
                src_ref=v_ref, dst_ref=out_ref.at[me], send_sem=send_sems.at[k - 1],
                recv_sem=recv_sems.at[k - 1], device_id=peer, device_id_type=MESH)
            cp.start()
            copies.append(cp)
        for cp in copies:
            cp.wait()

    return pl.pallas_call(
        body, name=name,
        out_shape=jax.ShapeDtypeStruct((N_DEV, 1, n), F32),
        in_specs=[pl.BlockSpec(memory_space=pltpu.VMEM)],
        out_specs=pl.BlockSpec(memory_space=pltpu.VMEM),
        scratch_shapes=[pltpu.SemaphoreType.DMA((N_DEV - 1,)), pltpu.SemaphoreType.DMA((N_DEV - 1,))],
    )(v)


def _allgather_weights(shards, dst_of, base_of, dst_shapes):
    n = len(shards)
    nd = len(dst_shapes)
    rows = [s.shape[0] for s in shards]

    def body(*refs):
        srcs, dsts = refs[:n], refs[n:n + nd]
        send_sems, recv_sems, local_sems = refs[n + nd:]
        x, y, c = _place()
        me, sibling = (x, y, c), (x, y, 1 - c)
        chips = [(1 - x, y), (x, 1 - y), (1 - x, 1 - y)]

        def slab(i, px, py, pc):
            start = pl.multiple_of(base_of[i] + (4 * px + 2 * py + pc) * rows[i], 16)
            return dsts[dst_of[i]].at[pl.ds(start, rows[i])]

        def copy(i, k, block, to, src=None):
            return pltpu.make_async_remote_copy(
                src_ref=slab(i, *block) if src is None else src, dst_ref=slab(i, *block),
                send_sem=send_sems.at[i, k], recv_sem=recv_sems.at[i, k],
                device_id=to, device_id_type=MESH)

        mine = [pltpu.make_async_copy(srcs[i], slab(i, *me), local_sems.at[i]) for i in range(n)]
        for cp in mine:
            cp.start()
        first = []
        for i in range(n):
            first.append(copy(i, 0, me, sibling, src=srcs[i]))
            first += [copy(i, 1 + j, me, (*chip, c), src=srcs[i]) for j, chip in enumerate(chips)]
        for cp in first:
            cp.start()
        passed = []
        for j, chip in enumerate(chips):
            for i in range(n):
                copy(i, 1 + j, (*chip, c), me).wait_recv()
                cp = copy(i, 4 + j, (*chip, c), sibling)
                cp.start()
                passed.append(cp)
        for i in range(n):
            copy(i, 0, sibling, me).wait_recv()
            for j, chip in enumerate(chips):
                copy(i, 4 + j, (*chip, 1 - c), me).wait_recv()
        for cp in first + passed:
            cp.wait_send()
        for cp in mine:
            cp.wait()

    hbm = pl.BlockSpec(memory_space=pltpu.HBM)
    return pl.pallas_call(
        body, name="allgather_weights",
        out_shape=[jax.ShapeDtypeStruct(s, BF16) for s in dst_shapes],
        in_specs=[hbm] * n, out_specs=[hbm] * nd,
        scratch_shapes=[pltpu.SemaphoreType.DMA((n, 7)), pltpu.SemaphoreType.DMA((n, 7)),
                        pltpu.SemaphoreType.DMA((n,))],
    )(*shards)


def _scatter_grads(grads, src_of, base_of, rows, cols):
    n = len(rows)
    ng = len(grads)

    def body(*refs):
        srcs, recvs = refs[:ng], refs[ng:ng + n]
        send_sems, recv_sems, local_sems = refs[ng + n:]
        x, y, c = _place()
        me = 4 * x + 2 * y + c

        def slab(i, idx):
            start = pl.multiple_of(base_of[i] + idx * rows[i], 16)
            return srcs[src_of[i]].at[pl.ds(start, rows[i])]

        mine = [pltpu.make_async_copy(slab(i, me), recvs[i].at[me], local_sems.at[i]) for i in range(n)]
        for cp in mine:
            cp.start()
        copies = []
        for k in range(1, N_DEV):
            px, py, pc = _flip(x, (k >> 2) & 1), _flip(y, (k >> 1) & 1), _flip(c, k & 1)
            for i in range(n):
                cp = pltpu.make_async_remote_copy(
                    src_ref=slab(i, 4 * px + 2 * py + pc), dst_ref=recvs[i].at[me],
                    send_sem=send_sems.at[i, k - 1], recv_sem=recv_sems.at[i, k - 1],
                    device_id=(px, py, pc), device_id_type=MESH)
                cp.start()
                copies.append(cp)
        for cp in copies:
            cp.wait()
        for cp in mine:
            cp.wait()

    hbm = pl.BlockSpec(memory_space=pltpu.HBM)
    return pl.pallas_call(
        body, name="scatter_grads",
        out_shape=[jax.ShapeDtypeStruct((N_DEV, rows[i], cols[i]), BF16) for i in range(n)],
        in_specs=[hbm] * ng, out_specs=[hbm] * n,
        scratch_shapes=[pltpu.SemaphoreType.DMA((n, 7)), pltpu.SemaphoreType.DMA((n, 7)),
                        pltpu.SemaphoreType.DMA((n,))],
    )(*grads)


def _sum_contributions(name, recv):
    _, rows, cols = recv.shape
    tr = rows if rows <= 512 else 304 if rows % 304 == 0 else 256

    def body(r_ref, o_ref):
        acc = r_ref[0].astype(F32)
        for k in range(1, N_DEV):
            acc = acc + r_ref[k].astype(F32)
        o_ref[...] = acc

    return pl.pallas_call(
        body, name=name, grid=(rows // tr,),
        out_shape=jax.ShapeDtypeStruct((rows, cols), F32),
        in_specs=[pl.BlockSpec((N_DEV, tr, cols), lambda i: (0, i, 0))],
        out_specs=pl.BlockSpec((tr, cols), lambda i: (i, 0)),
        compiler_params=_params("parallel"),
    )(recv)


def _mm(name, a, b, mode, out_dtype, tm, tn, tk):
    if mode == "TN":
        kk, m = a.shape
    else:
        m, kk = a.shape
    n = b.shape[0] if mode == "NT" else b.shape[1]
    tm, tn, tk = min(tm, m), min(tn, n), min(tk, kk)
    assert m % tm == 0 and n % tn == 0 and kk % tk == 0, (name, m, n, kk, tm, tn, tk)
    nk = kk // tk
    dims = {"NN": NN_DIMS, "NT": NT_DIMS, "TN": TN_DIMS}[mode]

    def body(a_ref, b_ref, o_ref, acc_ref):
        k = pl.program_id(2)
        part = lax.dot_general(a_ref[...], b_ref[...], dims, preferred_element_type=F32)

        @pl.when(k == 0)
        def _():
            acc_ref[...] = part

        @pl.when(k > 0)
        def _():
            acc_ref[...] += part

        @pl.when(k == nk - 1)
        def _():
            o_ref[...] = acc_ref[...].astype(out_dtype)

    a_spec = (pl.BlockSpec((tk, tm), lambda i, j, k: (k, i)) if mode == "TN"
              else pl.BlockSpec((tm, tk), lambda i, j, k: (i, k)))
    b_spec = (pl.BlockSpec((tn, tk), lambda i, j, k: (j, k)) if mode == "NT"
              else pl.BlockSpec((tk, tn), lambda i, j, k: (k, j)))
    return pl.pallas_call(
        body, name=name, grid=(m // tm, n // tn, nk),
        out_shape=jax.ShapeDtypeStruct((m, n), out_dtype),
        in_specs=[a_spec, b_spec],
        out_specs=pl.BlockSpec((tm, tn), lambda i, j, k: (i, j)),
        scratch_shapes=[pltpu.VMEM((tm, tn), F32)],
        compiler_params=_params("parallel", "parallel", "arbitrary"),
    )(a, b)


def _row(tm, w, off=0):
    return pl.BlockSpec((tm, w), lambda i: (i, off))


def _vec(w):
    return pl.BlockSpec((1, w), lambda i: (0, 0))


def _sigmoid(x):
    return 1.0 / (1.0 + jnp.exp(-x))


def _normmod(name, x, g, sc, sh, tm=256):
    s = x.shape[0]

    def body(x_ref, g_ref, sc_ref, sh_ref, h_ref):
        xv = x_ref[...]
        r = lax.rsqrt(jnp.mean(xv * xv, axis=-1, keepdims=True) + EPS)
        h_ref[...] = ((xv * r) * g_ref[...] * (1.0 + sc_ref[...]) + sh_ref[...]).astype(BF16)

    return pl.pallas_call(
        body, name=name, grid=(s // tm,),
        out_shape=jax.ShapeDtypeStruct((s, D), BF16),
        in_specs=[_row(tm, D), _vec(D), _vec(D), _vec(D)], out_specs=_row(tm, D),
        compiler_params=_params("parallel"),
    )(x, g, sc, sh)


def _normmod_bwd(name, dh, x, gin, g, sc, sh, tm=256):
    s = x.shape[0]

    def body(dh_ref, x_ref, gin_ref, g_ref, sc_ref, sh_ref, gout_ref, acc_ref):
        xv, dhv = x_ref[...], dh_ref[...]
        r = lax.rsqrt(jnp.mean(xv * xv, axis=-1, keepdims=True) + EPS)
        nv = xv * r
        gv, one_sc = g_ref[...], 1.0 + sc_ref[...]
        dn = dhv * gv * one_sc
        dx = r * (dn - nv * jnp.mean(dn * nv, axis=-1, keepdims=True))
        gout_ref[...] = gin_ref[...] + dx

        @pl.when(pl.program_id(0) == 0)
        def _():
            acc_ref[...] = jnp.zeros_like(acc_ref)

        dhn = dhv * nv
        acc_ref[0:1, :] += jnp.sum(dhv, axis=0, keepdims=True)
        acc_ref[1:2, :] += jnp.sum(dhn * gv, axis=0, keepdims=True)
        acc_ref[2:3, :] += jnp.sum(dhn * one_sc, axis=0, keepdims=True)

    return pl.pallas_call(
        body, name=name, grid=(s // tm,),
        out_shape=[jax.ShapeDtypeStruct((s, D), F32), jax.ShapeDtypeStruct((8, D), F32)],
        in_specs=[_row(tm, D), _row(tm, D), _row(tm, D), _vec(D), _vec(D), _vec(D)],
        out_specs=[_row(tm, D), pl.BlockSpec((8, D), lambda i: (0, 0))],
        compiler_params=_params("arbitrary"),
    )(dh, x, gin, g, sc, sh)


def _swiglu(name, ab, tm=256):
    s = ab.shape[0]

    def body(ab_ref, s_ref):
        a = ab_ref[:, :FF].astype(F32)
        b = ab_ref[:, FF:].astype(F32)
        s_ref[...] = (a * _sigmoid(a) * b).astype(BF16)

    return pl.pallas_call(
        body, name=name, grid=(s // tm,),
        out_shape=jax.ShapeDtypeStruct((s, FF), BF16),
        in_specs=[_row(tm, 2 * FF)], out_specs=_row(tm, FF),
        compiler_params=_params("parallel"),
    )(ab)


def _swiglu_bwd(name, ds, ab, tm=256):
    s = ab.shape[0]

    def body(ds_ref, ab_ref, dab_ref):
        a = ab_ref[:, :FF].astype(F32)
        b = ab_ref[:, FF:].astype(F32)
        dsv = ds_ref[...].astype(F32)
        sig = _sigmoid(a)
        dab_ref[:, :FF] = (dsv * b * (sig * (1.0 + a * (1.0 - sig)))).astype(BF16)
        dab_ref[:, FF:] = (dsv * (a * sig)).astype(BF16)

    return pl.pallas_call(
        body, name=name, grid=(s // tm,),
        out_shape=jax.ShapeDtypeStruct((s, 2 * FF), BF16),
        in_specs=[_row(tm, FF), _row(tm, 2 * FF)], out_specs=_row(tm, 2 * FF),
        compiler_params=_params("parallel"),
    )(ds, ab)


def _residual(name, x, f, gt, coef, tm=256):
    s = x.shape[0]

    def body(x_ref, f_ref, gt_ref, o_ref):
        o_ref[...] = x_ref[...] + (coef * gt_ref[...]) * f_ref[...]

    return pl.pallas_call(
        body, name=name, grid=(s // tm,),
        out_shape=jax.ShapeDtypeStruct((s, D), F32),
        in_specs=[_row(tm, D), _row(tm, D), _vec(D)], out_specs=_row(tm, D),
        compiler_params=_params("parallel"),
    )(x, f, gt)


def _gate_bwd(name, gin, f, gt, coef, tm=256):
    s = gin.shape[0]

    def body(g_ref, f_ref, gt_ref, df_ref, acc_ref):
        gv = g_ref[...]
        df_ref[...] = ((coef * gt_ref[...]) * gv).astype(BF16)

        @pl.when(pl.program_id(0) == 0)
        def _():
            acc_ref[...] = jnp.zeros_like(acc_ref)

        acc_ref[0:1, :] += coef * jnp.sum(gv * f_ref[...], axis=0, keepdims=True)

    return pl.pallas_call(
        body, name=name, grid=(s // tm,),
        out_shape=[jax.ShapeDtypeStruct((s, D), BF16), jax.ShapeDtypeStruct((8, D), F32)],
        in_specs=[_row(tm, D), _row(tm, D), _vec(D)],
        out_specs=[_row(tm, D), pl.BlockSpec((8, D), lambda i: (0, 0))],
        compiler_params=_params("arbitrary"),
    )(gin, f, gt)


def _loss_grad(x3, target, tm=256):
    s = x3.shape[0]

    def body(y_ref, t_ref, g_ref, l_ref):
        e = y_ref[...] - t_ref[...]
        g_ref[...] = e * (1.0 / D)

        @pl.when(pl.program_id(0) == 0)
        def _():
            l_ref[...] = jnp.zeros_like(l_ref)

        l_ref[...] += jnp.sum(jnp.mean(e * e, axis=-1, keepdims=True), axis=0, keepdims=True) * 0.5

    return pl.pallas_call(
        body, name="loss_grad", grid=(s // tm,),
        out_shape=[jax.ShapeDtypeStruct((s, D), F32), jax.ShapeDtypeStruct((8, 128), F32)],
        in_specs=[_row(tm, D), _row(tm, D)],
        out_specs=[_row(tm, D), pl.BlockSpec((8, 128), lambda i: (0, 0))],
        compiler_params=_params("arbitrary"),
    )(x3, target)


def _heads(x, fn):
    return jnp.concatenate([fn(x[:, h * HD:(h + 1) * HD], h) for h in range(COL // HD)], axis=1)


def _qknorm(proj, wqk, tm=256):
    s = proj.shape[0]

    def body(p_ref, w_ref, o_ref):
        pv = p_ref[...].astype(F32)
        wv = w_ref[...]

        def one(qh, h):
            r = lax.rsqrt(jnp.mean(qh * qh, axis=-1, keepdims=True) + EPS)
            return (qh * r) * wv[:, h * HD:(h + 1) * HD]

        o_ref[...] = _heads(pv, one).astype(BF16)

    return pl.pallas_call(
        body, name="qknorm", grid=(s // tm, QKW // COL),
        out_shape=jax.ShapeDtypeStruct((s, QKW), BF16),
        in_specs=[pl.BlockSpec((tm, COL), lambda i, j: (i, j)), pl.BlockSpec((1, COL), lambda i, j: (0, j))],
        out_specs=pl.BlockSpec((tm, COL), lambda i, j: (i, j)),
        compiler_params=_params("parallel", "parallel"),
    )(proj, wqk)


def _qknorm_bwd(proj, dqkn, wqk, dproj, tm=256):
    s = proj.shape[0]

    def body(p_ref, d_ref, w_ref, _, o_ref, acc_ref):
        pv = p_ref[...].astype(F32)
        dv = d_ref[...]
        wv = w_ref[...]
        sums = []

        def one(qh, h):
            dn = dv[:, h * HD:(h + 1) * HD]
            r = lax.rsqrt(jnp.mean(qh * qh, axis=-1, keepdims=True) + EPS)
            nh = qh * r
            sums.append(jnp.sum(dn * nh, axis=0, keepdims=True))
            dnw = dn * wv[:, h * HD:(h + 1) * HD]
            return r * (dnw - nh * jnp.mean(dnw * nh, axis=-1, keepdims=True))

        o_ref[...] = _heads(pv, one).astype(BF16)

        @pl.when(pl.program_id(1) == 0)
        def _():
            acc_ref[...] = jnp.zeros_like(acc_ref)

        acc_ref[0:1, :] += jnp.concatenate(sums, axis=1)

    return pl.pallas_call(
        body, name="qknorm_bwd", grid=(QKW // COL, s // tm),
        out_shape=[jax.ShapeDtypeStruct((s, IN_W), BF16), jax.ShapeDtypeStruct((8, QKW), F32)],
        in_specs=[pl.BlockSpec((tm, COL), lambda j, i: (i, j)), pl.BlockSpec((tm, COL), lambda j, i: (i, j)),
                  pl.BlockSpec((1, COL), lambda j, i: (0, j)), pl.BlockSpec(memory_space=pl.ANY)],
        out_specs=[pl.BlockSpec((tm, COL), lambda j, i: (i, j)), pl.BlockSpec((8, COL), lambda j, i: (0, j))],
        input_output_aliases={3: 0},
        compiler_params=_params("arbitrary", "arbitrary"),
    )(proj, dqkn, wqk, dproj)


def _attn_shapes(s, g):
    d = DILATIONS[g]
    tb = min(s, max(2048, 256 * d))
    sb = min(256, tb // d)
    pb = BAND * d
    assert s % tb == 0 and tb % pb == 0 and (tb // d) % sb == 0 and sb % BAND == 0
    return d, tb, sb, pb


def _lanes(x, width):
    return jnp.concatenate([x] * (width // HD), axis=1)


def _every(start, size, d):
    return pl.ds(start, size, stride=d) if d > 1 else pl.ds(start, size)


def _attn_specs(g, tb, pb, s, ahead):
    ratio = tb // pb
    if ahead:
        nbr = lambda n: jnp.minimum((n + 1) * ratio, s // pb - 1)
    else:
        nbr = lambda n: jnp.maximum(n * ratio - 1, 0)
    cur = lambda base: pl.BlockSpec((tb, HD), lambda h, n: (n, base + g * N_HEADS + h))
    side = lambda base: pl.BlockSpec((pb, HD), lambda h, n: (nbr(n), base + g * N_HEADS + h))
    tok = pl.BlockSpec((tb, HD), lambda h, n: (n, h))
    tok_side = pl.BlockSpec((pb, HD), lambda h, n: (nbr(n), h))
    return cur, side, tok, tok_side


Q_COL, K_COL, V_COL = 0, 12, 24


def _attn_fwd(g, qkn, proj):
    s = qkn.shape[0]
    d, tb, sb, pb = _attn_shapes(s, g)
    nj = tb // d // sb
    scale = HD ** -0.5

    def body(q_ref, kc_ref, kp_ref, vc_ref, vp_ref, o_ref, lse_ref, qf, kf, vf):
        n = pl.program_id(1)
        qf[...] = q_ref[...].astype(F32)
        kf[0:pb] = kp_ref[...].astype(F32)
        kf[pb:] = kc_ref[...].astype(F32)
        vf[0:pb] = vp_ref[...].astype(F32)
        vf[pb:] = vc_ref[...].astype(F32)
        for r in range(d):
            for j in range(nj):
                at = j * sb * d + r
                q = qf[_every(at, sb, d), :].astype(BF16)
                k = kf[_every(at, sb + BAND, d), :].astype(BF16)
                v = vf[_every(at, sb + BAND, d), :].astype(BF16)
                sc = lax.dot_general(q, k, NT_DIMS, preferred_element_type=F32) * scale
                qi = lax.broadcasted_iota(jnp.int32, sc.shape, 0)
                kj = lax.broadcasted_iota(jnp.int32, sc.shape, 1)
                valid = (kj >= qi) & (kj <= qi + BAND)
                if j == 0:
                    valid = valid & ((kj >= BAND) | (n > 0))
                sc = jnp.where(valid, sc, -1e30)
                m = jnp.max(sc, axis=-1, keepdims=True)
                p = jnp.exp(sc - m)
                l = jnp.sum(p, axis=-1, keepdims=True)
                o = lax.dot_general(p.astype(BF16), v, NN_DIMS, preferred_element_type=F32)
                o_ref[_every(at, sb, d), :] = o / l
                lse_ref[_every(at, sb, d), :] = jnp.broadcast_to(m + jnp.log(l), (sb, HD))

    cur, side, tok, _ = _attn_specs(g, tb, pb, s, ahead=False)
    return pl.pallas_call(
        body, name=f"attn_fwd_g{g}", grid=(N_HEADS, s // tb),
        out_shape=[jax.ShapeDtypeStruct((s, COL), F32)] * 2,
        in_specs=[cur(Q_COL), cur(K_COL), side(K_COL), cur(V_COL), side(V_COL)],
        out_specs=[tok, tok],
        scratch_shapes=[pltpu.VMEM((tb, HD), F32), pltpu.VMEM((tb + pb, HD), F32),
                        pltpu.VMEM((tb + pb, HD), F32)],
        compiler_params=_params("parallel", "arbitrary"),
    )(qkn, qkn, qkn, proj, proj)


def _attn_combine(os_, lses, tm=256):
    s = os_[0].shape[0]

    def body(o0, o1, o2, l0, l1, l2, o_ref, lse_ref):
        a, b, c = l0[...], l1[...], l2[...]
        m = jnp.maximum(jnp.maximum(a, b), c)
        ea, eb, ec = jnp.exp(a - m), jnp.exp(b - m), jnp.exp(c - m)
        tot = ea + eb + ec
        o_ref[...] = ((ea * o0[...] + eb * o1[...] + ec * o2[...]) / tot).astype(BF16)
        lse_ref[...] = m + jnp.log(tot)

    return pl.pallas_call(
        body, name="attn_combine", grid=(s // tm,),
        out_shape=[jax.ShapeDtypeStruct((s, COL), BF16), jax.ShapeDtypeStruct((s, COL), F32)],
        in_specs=[_row(tm, COL)] * 6, out_specs=[_row(tm, COL)] * 2,
        compiler_params=_params("parallel"),
    )(*os_, *lses)


def _attn_delta(do, o, tm=256):
    s = do.shape[0]

    def body(do_ref, o_ref, del_ref):
        prod = do_ref[...] * o_ref[...].astype(F32)
        del_ref[...] = _heads(prod, lambda ph, h: jnp.broadcast_to(
            jnp.sum(ph, axis=-1, keepdims=True), ph.shape))

    return pl.pallas_call(
        body, name="attn_delta", grid=(s // tm,),
        out_shape=jax.ShapeDtypeStruct((s, COL), F32),
        in_specs=[_row(tm, COL)] * 2, out_specs=_row(tm, COL),
        compiler_params=_params("parallel"),
    )(do, o)


def _attn_dq(g, qkn, proj, do, lse, delta, dqkn):
    s = qkn.shape[0]
    d, tb, sb, pb = _attn_shapes(s, g)
    nj = tb // d // sb
    scale = HD ** -0.5
    chained = dqkn is not None

    def body(q_ref, kc_ref, kp_ref, vc_ref, vp_ref, do_ref, lse_ref, del_ref, *rest):
        dq_ref, qf, kf, vf = rest[-4:]
        n = pl.program_id(1)
        qf[...] = q_ref[...].astype(F32)
        kf[0:pb] = kp_ref[...].astype(F32)
        kf[pb:] = kc_ref[...].astype(F32)
        vf[0:pb] = vp_ref[...].astype(F32)
        vf[pb:] = vc_ref[...].astype(F32)
        for r in range(d):
            for j in range(nj):
                at = j * sb * d + r
                rows = _every(at, sb, d)
                q = qf[rows, :].astype(BF16)
                k = kf[_every(at, sb + BAND, d), :].astype(BF16)
                v = vf[_every(at, sb + BAND, d), :].astype(BF16)
                sc = lax.dot_general(q, k, NT_DIMS, preferred_element_type=F32) * scale
                qi = lax.broadcasted_iota(jnp.int32, sc.shape, 0)
                kj = lax.broadcasted_iota(jnp.int32, sc.shape, 1)
                valid = (kj >= qi) & (kj <= qi + BAND)
                if j == 0:
                    valid = valid & ((kj >= BAND) | (n > 0))
                p = jnp.exp(jnp.where(valid, sc - _lanes(lse_ref[rows, :], sb + BAND), -1e30))
                dp = lax.dot_general(do_ref[rows, :].astype(BF16), v, NT_DIMS, preferred_element_type=F32)
                ds = p * (dp - _lanes(del_ref[rows, :], sb + BAND)) * scale
                dq_ref[rows, :] = lax.dot_general(ds.astype(BF16), k, NN_DIMS, preferred_element_type=F32)

    cur, side, tok, _ = _attn_specs(g, tb, pb, s, ahead=False)
    args = [qkn, qkn, qkn, proj, proj, do, lse, delta]
    specs = [cur(Q_COL), cur(K_COL), side(K_COL), cur(V_COL), side(V_COL), tok, tok, tok]
    if chained:
        args.append(dqkn)
        specs.append(pl.BlockSpec(memory_space=pl.ANY))
    return pl.pallas_call(
        body, name=f"attn_dq_g{g}", grid=(N_HEADS, s // tb),
        out_shape=jax.ShapeDtypeStruct((s, QKW), F32),
        in_specs=specs, out_specs=cur(Q_COL),
        input_output_aliases={8: 0} if chained else {},
        scratch_shapes=[pltpu.VMEM((tb, HD), F32), pltpu.VMEM((tb + pb, HD), F32),
                        pltpu.VMEM((tb + pb, HD), F32)],
        compiler_params=_params("arbitrary", "arbitrary"),
    )(*args)


def _attn_dkv(g, qkn, proj, do, lse, delta, dqkn, dproj):
    s = qkn.shape[0]
    d, tb, sb, pb = _attn_shapes(s, g)
    nj = tb // d // sb
    nt = s // tb
    scale = HD ** -0.5

    def body(k_ref, v_ref, qc_ref, qn_ref, doc_ref, don_ref, lc_ref, ln_ref, dc_ref, dn_ref, _a, _b,
             dk_ref, dv_ref, kf, vf, qf, dvf):
        n = pl.program_id(1)
        kf[...] = k_ref[...].astype(F32)
        vf[...] = v_ref[...].astype(F32)
        qf[0:tb] = qc_ref[...].astype(F32)
        qf[tb:] = qn_ref[...].astype(F32)

        def window(c_ref, n_ref, r, j):
            at = j * sb * d + r
            if j < nj - 1:
                return c_ref[_every(at, sb + BAND, d), :]
            return jnp.concatenate([c_ref[_every(at, sb, d), :], n_ref[_every(r, BAND, d), :]], axis=0)

        for r in range(d):
            for j in range(nj):
                at = j * sb * d + r
                rows = _every(at, sb, d)
                k = kf[rows, :].astype(BF16)
                v = vf[rows, :].astype(BF16)
                q = qf[_every(at, sb + BAND, d), :].astype(BF16)
                dov = window(doc_ref, don_ref, r, j).astype(BF16)
                sc = lax.dot_general(q, k, NT_DIMS, preferred_element_type=F32) * scale
                qi = lax.broadcasted_iota(jnp.int32, sc.shape, 0)
                kj = lax.broadcasted_iota(jnp.int32, sc.shape, 1)
                valid = (qi >= kj) & (qi <= kj + BAND)
                if j == nj - 1:
                    valid = valid & ((qi < sb) | (n < nt - 1))
                p = jnp.exp(jnp.where(valid, sc - _lanes(window(lc_ref, ln_ref, r, j), sb), -1e30))
                dp = lax.dot_general(dov, v, NT_DIMS, preferred_element_type=F32)
                ds = p * (dp - _lanes(window(dc_ref, dn_ref, r, j), sb)) * scale
                dvf[rows, :] = lax.dot_general(p.astype(BF16), dov, TN_DIMS, preferred_element_type=F32)
                dk_ref[rows, :] = lax.dot_general(ds.astype(BF16), q, TN_DIMS, preferred_element_type=F32)
        dv_ref[...] = dvf[...].astype(BF16)

    cur, side, tok, tok_side = _attn_specs(g, tb, pb, s, ahead=True)
    anyspec = pl.BlockSpec(memory_space=pl.ANY)
    return pl.pallas_call(
        body, name=f"attn_dkv_g{g}", grid=(N_HEADS, nt),
        out_shape=[jax.ShapeDtypeStruct((s, QKW), F32), jax.ShapeDtypeStruct((s, IN_W), BF16)],
        in_specs=[cur(K_COL), cur(V_COL), cur(Q_COL), side(Q_COL), tok, tok_side, tok, tok_side,
                  tok, tok_side, anyspec, anyspec],
        out_specs=[cur(K_COL), cur(V_COL)],
        input_output_aliases={10: 0, 11: 1},
        scratch_shapes=[pltpu.VMEM((tb, HD), F32), pltpu.VMEM((tb, HD), F32),
                        pltpu.VMEM((tb + pb, HD), F32), pltpu.VMEM((tb, HD), F32)],
        compiler_params=_params("arbitrary", "arbitrary"),
    )(qkn, proj, qkn, qkn, do, do, lse, lse, delta, delta, dqkn, dproj)


def _shift_down(x, before, k):
    rolled = pltpu.roll(x, k, 0)
    head = jnp.where(lax.broadcasted_iota(jnp.int32, before.shape, 0) < k, pltpu.roll(before, k, 0), rolled[:8])
    return jnp.concatenate([head, rolled[8:]], axis=0)


def _shift_up(x, after, k):
    rows = x.shape[0]
    rolled = pltpu.roll(x, rows - k, 0)
    tail = jnp.where(lax.broadcasted_iota(jnp.int32, after.shape, 0) >= 8 - k,
                     pltpu.roll(after, 8 - k, 0), rolled[rows - 8:])
    return jnp.concatenate([rolled[:rows - 8], tail], axis=0)


def _conv_fwd(proj, cw, tm=256):
    s = proj.shape[0]
    r16 = tm // 16

    def body(u_ref, b_ref, c_ref, up_ref, cp_ref, w_ref, z_ref):
        i = pl.program_id(1)
        xc = c_ref[...].astype(F32) * u_ref[...].astype(F32)
        xp = jnp.where(i > 0, cp_ref[8:16, :].astype(F32) * up_ref[8:16, :].astype(F32), 0.0)
        w = w_ref[...]
        conv = _shift_down(xc, xp, 2) * w[0:1] + _shift_down(xc, xp, 1) * w[1:2] + xc * w[2:3]
        z_ref[...] = (b_ref[...].astype(F32) * conv).astype(BF16)

    tile = lambda blk: pl.BlockSpec((tm, COL), lambda j, i: (i, blk + j))
    before = lambda blk: pl.BlockSpec((16, COL), lambda j, i: (jnp.maximum(i * r16 - 1, 0), blk + j))
    return pl.pallas_call(
        body, name="conv_fwd", grid=(D // COL, s // tm),
        out_shape=jax.ShapeDtypeStruct((s, D), BF16),
        in_specs=[tile(U_BLK), tile(B_BLK), tile(C_BLK), before(U_BLK), before(C_BLK),
                  pl.BlockSpec((3, COL), lambda j, i: (0, j))],
        out_specs=pl.BlockSpec((tm, COL), lambda j, i: (i, j)),
        compiler_params=_params("parallel", "parallel"),
    )(proj, proj, proj, proj, proj, cw)


def _conv_bwd(dz, proj, cw, dproj, tm=256):
    s = proj.shape[0]
    r8, r16 = tm // 8, tm // 16
    nrow = s // tm

    def body(dz_ref, u_ref, b_ref, c_ref, up_ref, cp_ref, dzn_ref, bn_ref, w_ref, _, o_ref, acc_ref):
        piece, i = pl.program_id(1), pl.program_id(2)
        u, c = u_ref[...].astype(F32), c_ref[...].astype(F32)
        bv = b_ref[...].astype(F32)
        dzv = dz_ref[...]
        w = w_ref[...]

        @pl.when((piece == 0) & (i == 0))
        def _():
            acc_ref[...] = jnp.zeros_like(acc_ref)

        @pl.when(piece == 1)
        def _():
            xc = c * u
            xp = jnp.where(i > 0, cp_ref[8:16, :].astype(F32) * up_ref[8:16, :].astype(F32), 0.0)
            x2, x1 = _shift_down(xc, xp, 2), _shift_down(xc, xp, 1)
            o_ref[...] = (dzv * (x2 * w[0:1] + x1 * w[1:2] + xc * w[2:3])).astype(BF16)
            dconv = dzv * bv
            acc_ref[0:1, :] += jnp.sum(dconv * x2, axis=0, keepdims=True)
            acc_ref[1:2, :] += jnp.sum(dconv * x1, axis=0, keepdims=True)
            acc_ref[2:3, :] += jnp.sum(dconv * xc, axis=0, keepdims=True)

        @pl.when(piece != 1)
        def _():
            dconv = dzv * bv
            dn = jnp.where(i < nrow - 1, dzn_ref[...] * bn_ref[0:8, :].astype(F32), 0.0)
            dxc = dconv * w[2:3] + _shift_up(dconv, dn, 1) * w[1:2] + _shift_up(dconv, dn, 2) * w[0:1]
            o_ref[...] = (dxc * jnp.where(piece == 0, c, u)).astype(BF16)

    tile = lambda blk: pl.BlockSpec((tm, COL), lambda j, p, i: (i, blk + j))
    before = lambda blk: pl.BlockSpec((16, COL), lambda j, p, i: (jnp.maximum(i * r16 - 1, 0), blk + j))
    after = lambda rows, blk: pl.BlockSpec(
        (rows, COL), lambda j, p, i: (jnp.minimum((i + 1) * (tm // rows), s // rows - 1), blk + j))
    return pl.pallas_call(
        body, name="conv_bwd", grid=(D // COL, 3, nrow),
        out_shape=[jax.ShapeDtypeStruct((s, IN_W), BF16), jax.ShapeDtypeStruct((8, D), F32)],
        in_specs=[tile(0), tile(U_BLK), tile(B_BLK), tile(C_BLK), before(U_BLK), before(C_BLK),
                  after(8, 0), after(16, B_BLK), pl.BlockSpec((3, COL), lambda j, p, i: (0, j)),
                  pl.BlockSpec(memory_space=pl.ANY)],
        out_specs=[pl.BlockSpec((tm, COL), lambda j, p, i: (i, U_BLK + 2 * p + j)),
                   pl.BlockSpec((8, COL), lambda j, p, i: (0, j))],
        input_output_aliases={9: 0},
        compiler_params=_params("arbitrary", "arbitrary", "arbitrary"),
    )(dz, proj, proj, proj, proj, proj, dz, proj, cw, dproj)


def _merge_fwd(ya, yc, proj, tm=256):
    s = proj.shape[0]

    def body(ya_ref, yc_ref, ga_ref, gc_ref, o_ref):
        o_ref[...] = (_sigmoid(ga_ref[...].astype(F32)) * ya_ref[...].astype(F32)
                      + _sigmoid(gc_ref[...].astype(F32)) * yc_ref[...].astype(F32)).astype(BF16)

    tile = lambda blk: pl.BlockSpec((tm, COL), lambda j, i: (i, blk + j))
    return pl.pallas_call(
        body, name="merge_fwd", grid=(D // COL, s // tm),
        out_shape=jax.ShapeDtypeStruct((s, D), BF16),
        in_specs=[tile(0), tile(0), tile(GA_BLK), tile(GC_BLK)], out_specs=tile(0),
        compiler_params=_params("parallel", "parallel"),
    )(ya, yc, proj, proj)


def _merge_bwd_branches(dm, proj, tm=256):
    s = proj.shape[0]

    def body(dm_ref, ga_ref, gc_ref, dya_ref, dyc_ref):
        dmv = dm_ref[...]
        dya_ref[...] = (dmv * _sigmoid(ga_ref[...].astype(F32))).astype(BF16)
        dyc_ref[...] = (dmv * _sigmoid(gc_ref[...].astype(F32))).astype(BF16)

    tile = lambda blk: pl.BlockSpec((tm, COL), lambda j, i: (i, blk + j))
    return pl.pallas_call(
        body, name="merge_bwd_branches", grid=(D // COL, s // tm),
        out_shape=[jax.ShapeDtypeStruct((s, D), BF16)] * 2,
        in_specs=[tile(0), tile(GA_BLK), tile(GC_BLK)], out_specs=[tile(0)] * 2,
        compiler_params=_params("parallel", "parallel"),
    )(dm, proj, proj)


def _merge_bwd_gates(dm, ya, yc, proj, tm=256):
    s = proj.shape[0]
    half = D // COL

    def body(dm_ref, ya_ref, yc_ref, g_ref, o_ref):
        y = jnp.where(pl.program_id(0) < half, ya_ref[...].astype(F32), yc_ref[...].astype(F32))
        sig = _sigmoid(g_ref[...].astype(F32))
        o_ref[...] = (dm_ref[...] * y * sig * (1.0 - sig)).astype(BF16)

    chan = pl.BlockSpec((tm, COL), lambda jj, i: (i, jj % half))
    gate = pl.BlockSpec((tm, COL), lambda jj, i: (i, GA_BLK + jj))
    return pl.pallas_call(
        body, name="merge_bwd_gates", grid=(2 * half, s // tm),
        out_shape=jax.ShapeDtypeStruct((s, IN_W), BF16),
        in_specs=[chan, chan, chan, gate], out_specs=gate,
        compiler_params=_params("parallel", "parallel"),
    )(dm, ya, yc, proj)


def _mod_part(c_all, w_ada, b_part):
    def body(c_ref, w_ref, b_ref, o_ref):
        cv = c_ref[...]
        act = cv * _sigmoid(cv)
        o_ref[...] = jnp.dot(act, w_ref[...], preferred_element_type=F32,
                             precision=lax.Precision.HIGHEST) + b_ref[...]

    return pl.pallas_call(
        body, name="mod_part", out_shape=jax.ShapeDtypeStruct((N_DEV, w_ada.shape[1]), F32),
    )(c_all, w_ada, b_part)


def _w_ada_grad(c_all_t, dmod_part):
    def body(c_ref, d_ref, o_ref):
        cv = c_ref[...]
        act = cv * _sigmoid(cv)
        dv = d_ref[...]
        acc = act[:, 0:1] * dv[0:1, :]
        for b in range(1, N_DEV):
            acc = acc + act[:, b:b + 1] * dv[b:b + 1, :]
        o_ref[...] = acc

    return pl.pallas_call(
        body, name="w_ada_grad", out_shape=jax.ShapeDtypeStruct((D, dmod_part.shape[1]), F32),
    )(c_all_t, dmod_part)


def _sum_rows(name, v):
    def body(v_ref, o_ref):
        acc = v_ref[0]
        for k in range(1, N_DEV):
            acc = acc + v_ref[k]
        o_ref[...] = acc

    return pl.pallas_call(body, name=name, out_shape=jax.ShapeDtypeStruct(v.shape[1:], F32))(v)


def _adamw(name, w, g, m, v):
    rows, cols = w.shape
    tr = 256 if rows % 256 == 0 and rows * cols > 512 * 1024 else rows
    c1 = 1.0 - ADAM_B1 ** ADAM_STEP
    c2 = 1.0 - ADAM_B2 ** ADAM_STEP

    def body(w_ref, g_ref, m_ref, v_ref, d_ref, nm_ref, nv_ref):
        gv = g_ref[...]
        nm = ADAM_B1 * m_ref[...] + (1.0 - ADAM_B1) * gv
        nv = ADAM_B2 * v_ref[...] + (1.0 - ADAM_B2) * (gv * gv)
        nm_ref[...] = nm
        nv_ref[...] = nv
        d_ref[...] = -ADAM_LR * ((nm / c1) / (jnp.sqrt(nv / c2) + ADAM_EPS) + ADAM_WD * w_ref[...])

    spec = pl.BlockSpec((tr, cols), lambda i: (i, 0))
    return pl.pallas_call(
        body, name=name, grid=(rows // tr,),
        out_shape=[jax.ShapeDtypeStruct((rows, cols), F32)] * 3,
        in_specs=[spec] * 4, out_specs=[spec] * 3,
        compiler_params=_params("parallel"),
    )(w, g, m, v)


def _ffn_fwd(tag, x, g, sc, sh, gt, wgu, wd):
    h = _normmod(f"{tag}_normmod", x, g, sc, sh)
    ab = _mm(f"{tag}_gate_up", h, wgu, "NT", BF16, 1024, 512, 1024)
    sw = _swiglu(f"{tag}_swiglu", ab)
    f = _mm(f"{tag}_down", sw, wd, "NN", F32, 1024, 1024, FF)
    return _residual(f"{tag}_residual", x, f, gt, 0.5), (h, ab, sw, f)


def _ffn_bwd(tag, gout, x, saved, g, sc, sh, gt, wgu, wd):
    h, ab, sw, f = saved
    df, gt_acc = _gate_bwd(f"{tag}_gate_bwd", gout, f, gt, 0.5)
    ds = _mm(f"{tag}_d_hidden", df, wd, "NT", BF16, 1024, 1408, 1024)
    dwd = _mm(f"{tag}_dw_down", sw, df, "TN", BF16, 1408, 1024, 512)
    dab = _swiglu_bwd(f"{tag}_swiglu_bwd", ds, ab)
    dwgu = _mm(f"{tag}_dw_gate_up", dab, h, "TN", BF16, 1408, 1024, 512)
    dh = _mm(f"{tag}_d_h", dab, wgu, "NN", F32, 1024, 1024, 512)
    gin, acc = _normmod_bwd(f"{tag}_normmod_bwd", dh, x, gout, g, sc, sh)
    return gin, acc, gt_acc[0:1], dwgu, dwd


def kernel(x, c, w_ada, b_ada, norm_ffn1, ffn1_w_gate, ffn1_w_up, ffn1_w_down, norm_mix, w_in, q_norm, k_norm, conv_w, w_attn_branch, w_conv_branch, w_out, norm_ffn2, ffn2_w_gate, ffn2_w_up, ffn2_w_down, loss_target, m_w_ada, m_b_ada, m_norm_ffn1, m_ffn1_w_gate, m_ffn1_w_up, m_ffn1_w_down, m_norm_mix, m_w_in, m_q_norm, m_k_norm, m_conv_w, m_w_attn_branch, m_w_conv_branch, m_w_out, m_norm_ffn2, m_ffn2_w_gate, m_ffn2_w_up, m_ffn2_w_down, v_w_ada, v_b_ada, v_norm_ffn1, v_ffn1_w_gate, v_ffn1_w_up, v_ffn1_w_down, v_norm_mix, v_w_in, v_q_norm, v_k_norm, v_conv_w, v_w_attn_branch, v_w_conv_branch, v_w_out, v_norm_ffn2, v_ffn2_w_gate, v_ffn2_w_up, v_ffn2_w_down):
    me = 4 * lax.axis_index("x") + 2 * lax.axis_index("y") + lax.axis_index("c")
    x0, target = x[0], loss_target[0]
    s = x0.shape[0]
    ada_cols = w_ada.shape[2]
    cw_cols = conv_w.shape[2]

    gathered = _small_allgather(
        "gather_c_conv", jnp.concatenate([c, conv_w[0].reshape(1, 3 * cw_cols)], axis=1))[:, 0]
    c_all = gathered[:, :D]
    cw = gathered[:, D:].reshape(N_DEV, 3, cw_cols).transpose(1, 0, 2).reshape(3, D)
    b_part = lax.dynamic_slice(b_ada, (0, me * ada_cols), (1, ada_cols))
    mod_part = _mod_part(c_all, w_ada[0], b_part)
    mod_all = _small_allgather("gather_mod", mod_part.reshape(1, N_DEV * ada_cols))
    mod = lax.dynamic_slice(mod_all.reshape(N_DEV, N_DEV, ada_cols), (0, me, 0), (N_DEV, 1, ada_cols))
    mod = mod.reshape(N_MOD, 1, D)
    sh1, sc1, gt1, sh2, sc2, gt2, sh3, sc3, gt3 = [mod[i] for i in range(N_MOD)]

    tb = lambda w: w[0].T.astype(BF16)
    nb = lambda w: w[0].astype(BF16)
    shards = [tb(ffn1_w_gate), tb(ffn1_w_up), nb(ffn1_w_down), tb(ffn2_w_gate), tb(ffn2_w_up), nb(ffn2_w_down),
              tb(w_in), tb(w_attn_branch), nb(w_conv_branch), nb(w_out)]
    dst_of = [0, 0, 1, 2, 2, 3, 4, 5, 6, 7]
    base_of = [0, FF, 0, 0, FF, 0, 0, 0, 0, 0]
    dst_shapes = [(2 * FF, D), (FF, D), (2 * FF, D), (FF, D), (IN_W, D), (D, COL), (D, D), (D, D)]
    wgu1, wd1, wgu2, wd2, win_t, wa_t, wc, wo = _allgather_weights(shards, dst_of, base_of, dst_shapes)

    x1, saved1 = _ffn_fwd("ffn1", x0, norm_ffn1, sc1, sh1, gt1, wgu1, wd1)
    h2 = _normmod("mix_normmod", x1, norm_mix, sc2, sh2)
    proj = _mm("mix_in_proj", h2, win_t, "NT", BF16, 1024, 512, 1024)
    wqk = jnp.concatenate([jnp.tile(q_norm, (1, 12)), jnp.tile(k_norm, (1, 12))], axis=1)
    qkn = _qknorm(proj, wqk)
    group_out = [_attn_fwd(g, qkn, proj) for g in range(3)]
    o, lse = _attn_combine([go[0] for go in group_out], [go[1] for go in group_out])
    ya = _mm("mix_attn_branch", o, wa_t, "NT", BF16, 1024, 1024, COL)
    z = _conv_fwd(proj, cw)
    yc = _mm("mix_conv_branch", z, wc, "NN", BF16, 1024, 1024, D)
    merged = _merge_fwd(ya, yc, proj)
    mix = _mm("mix_out_proj", merged, wo, "NN", F32, 1024, 1024, D)
    x2 = _residual("mix_residual", x1, mix, gt2, 1.0)
    x3, saved3 = _ffn_fwd("ffn2", x2, norm_ffn2, sc3, sh3, gt3, wgu2, wd2)
    g3, loss_part = _loss_grad(x3, target)
    loss = lax.psum(loss_part[0, 0], ("x", "y", "c"))

    g2, acc3, dgt3, dwgu2, dwd2 = _ffn_bwd("ffn2", g3, x2, saved3, norm_ffn2, sc3, sh3, gt3, wgu2, wd2)
    dmix, gt2_acc = _gate_bwd("mix_gate_bwd", g2, mix, gt2, 1.0)
    dmerged = _mm("mix_d_merged", dmix, wo, "NT", F32, 1024, 1024, D)
    dwo = _mm("mix_dw_out", merged, dmix, "TN", BF16, 1024, 1024, 512)
    dya, dyc = _merge_bwd_branches(dmerged, proj)
    dproj = _merge_bwd_gates(dmerged, ya, yc, proj)
    dwc = _mm("mix_dw_conv_branch", z, dyc, "TN", BF16, 1024, 1024, 512)
    dz = _mm("mix_d_z", dyc, wc, "NT", F32, 1024, 1024, D)
    dproj, cw_acc = _conv_bwd(dz, proj, cw, dproj)
    dwa_t = _mm("mix_dw_attn_branch", dya, o, "TN", BF16, 1024, COL, 512)
    do = _mm("mix_d_o", dya, wa_t, "NN", F32, 1024, COL, D)
    delta = _attn_delta(do, o)
    dqkn = None
    for g in range(3):
        dqkn = _attn_dq(g, qkn, proj, do, lse, delta, dqkn)
    for g in range(3):
        dqkn, dproj = _attn_dkv(g, qkn, proj, do, lse, delta, dqkn, dproj)
    dproj, wqk_acc = _qknorm_bwd(proj, dqkn, wqk, dproj)
    dwin_t = _mm("mix_dw_in", dproj, h2, "TN", BF16, 2432, 1024, 512)
    dh2 = _mm("mix_d_h", dproj, win_t, "NN", F32, 1024, 1024, 512)
    g1, acc2 = _normmod_bwd("mix_normmod_bwd", dh2, x1, g2, norm_mix, sc2, sh2)
    g0, acc1, dgt1, dwgu1, dwd1 = _ffn_bwd("ffn1", g1, x0, saved1, norm_ffn1, sc1, sh1, gt1, wgu1, wd1)

    dqw = jnp.sum(wqk_acc[0, :QKW // 2].reshape(12, HD), axis=0)
    dkw = jnp.sum(wqk_acc[0, QKW // 2:].reshape(12, HD), axis=0)
    small = jnp.concatenate([
        acc1[0], acc1[1], dgt1[0], acc2[0], acc2[1], gt2_acc[0], acc3[0], acc3[1], dgt3[0],
        acc1[2], acc2[2], acc3[2], dqw, dkw, cw_acc[0:3].reshape(3 * D)]).reshape(1, -1)
    small_all = _small_allgather("gather_small_grads", small)
    small_sum = _sum_rows("sum_small_grads", small_all)[0]
    n_mod = N_MOD * D
    g_b_ada = small_sum[:n_mod].reshape(1, n_mod)
    g_norm1, g_norm2, g_norm3 = [small_sum[n_mod + i * D:n_mod + (i + 1) * D].reshape(1, D) for i in range(3)]
    off = n_mod + 3 * D
    g_qn, g_kn = small_sum[off:off + HD].reshape(1, HD), small_sum[off + HD:off + 2 * HD].reshape(1, HD)
    g_cw_full = small_sum[off + 2 * HD:].reshape(3, D)
    g_cw = lax.dynamic_slice(g_cw_full, (0, me * cw_cols), (3, cw_cols))
    dmod_part = lax.dynamic_slice(small_all[:, 0, :n_mod], (0, me * ada_cols), (N_DEV, ada_cols))
    g_w_ada = _w_ada_grad(c_all.T, dmod_part)

    grads = [dwgu1, dwd1, dwgu2, dwd2, dwin_t, dwa_t, dwc, dwo]
    src_of = [0, 0, 1, 2, 2, 3, 4, 5, 6, 7]
    rows = [sh_.shape[0] for sh_ in shards]
    cols = [sh_.shape[1] for sh_ in shards]
    recvs = _scatter_grads(grads, src_of, base_of, rows, cols)
    names = ["ffn1_gate", "ffn1_up", "ffn1_down", "ffn2_gate", "ffn2_up", "ffn2_down",
             "w_in", "attn_branch", "conv_branch", "w_out"]
    sums = [_sum_contributions(f"sum_{nm}", r) for nm, r in zip(names, recvs)]
    transposed = [True, True, False, True, True, False, True, True, False, False]
    gw = [sm.T if t else sm for sm, t in zip(sums, transposed)]
    g_f1g, g_f1u, g_f1d, g_f2g, g_f2u, g_f2d, g_win, g_wa, g_wc, g_wo = gw

    grad_list = [g_w_ada[None], g_b_ada, g_norm1, g_f1g[None], g_f1u[None], g_f1d[None], g_norm2, g_win[None],
                 g_qn, g_kn, g_cw[None], g_wa[None], g_wc[None], g_wo[None], g_norm3,
                 g_f2g[None], g_f2u[None], g_f2d[None]]
    weights = [w_ada, b_ada, norm_ffn1, ffn1_w_gate, ffn1_w_up, ffn1_w_down, norm_mix, w_in, q_norm, k_norm,
               conv_w, w_attn_branch, w_conv_branch, w_out, norm_ffn2, ffn2_w_gate, ffn2_w_up, ffn2_w_down]
    ms = [m_w_ada, m_b_ada, m_norm_ffn1, m_ffn1_w_gate, m_ffn1_w_up, m_ffn1_w_down, m_norm_mix, m_w_in, m_q_norm,
          m_k_norm, m_conv_w, m_w_attn_branch, m_w_conv_branch, m_w_out, m_norm_ffn2, m_ffn2_w_gate,
          m_ffn2_w_up, m_ffn2_w_down]
    vs = [v_w_ada, v_b_ada, v_norm_ffn1, v_ffn1_w_gate, v_ffn1_w_up, v_ffn1_w_down, v_norm_mix, v_w_in, v_q_norm,
          v_k_norm, v_conv_w, v_w_attn_branch, v_w_conv_branch, v_w_out, v_norm_ffn2, v_ffn2_w_gate,
          v_ffn2_w_up, v_ffn2_w_down]
    wnames = ["w_ada", "b_ada", "norm_ffn1", "ffn1_w_gate", "ffn1_w_up", "ffn1_w_down", "norm_mix", "w_in",
              "q_norm", "k_norm", "conv_w", "w_attn_branch", "w_conv_branch", "w_out", "norm_ffn2",
              "ffn2_w_gate", "ffn2_w_up", "ffn2_w_down"]
    deltas, new_ms, new_vs = [], [], []
    for nm, w, gr, m_, v_ in zip(wnames, weights, grad_list, ms, vs):
        two_d = (-1, w.shape[-1])
        dl, nm_, nv_ = _adamw(f"adamw_{nm}", w.reshape(two_d), gr.reshape(two_d), m_.reshape(two_d), v_.reshape(two_d))
        deltas.append(dl.reshape(w.shape))
        new_ms.append(nm_.reshape(w.shape))
        new_vs.append(nv_.reshape(w.shape))
    grad_out = [gr.reshape(w.shape) for gr, w in zip(grad_list, weights)]
    return (loss, g0[None], *grad_out, *deltas, *new_ms, *new_vs)
```

```python
import functools

import jax
import jax.numpy as jnp
from jax import lax
from jax.experimental import pallas as pl
from jax.experimental.pallas import tpu as pltpu

F32 = jnp.float32
BF16 = jnp.bfloat16
MESH = pl.DeviceIdType.MESH

N_DEV = 8
D = 1024
FF = 2816
HD = 128
N_HEADS = 4
DILATIONS = (1, 4, 16)
BAND = 128
QKW = 2 * 3 * N_HEADS * HD
IN_W = 9728
COL = 512
V_BLK, U_BLK, B_BLK, C_BLK, GA_BLK, GC_BLK = 6, 9, 11, 13, 15, 17
EPS = 1e-6
N_MOD = 9
ADAM_LR, ADAM_B1, ADAM_B2, ADAM_EPS, ADAM_WD, ADAM_STEP = 0.001, 0.9, 0.999, 1e-08, 0.01, 10

NT_DIMS = (((1,), (1,)), ((), ()))
TN_DIMS = (((0,), (0,)), ((), ()))
NN_DIMS = (((1,), (0,)), ((), ()))


def _place():
    return lax.axis_index("x"), lax.axis_index("y"), lax.axis_index("c")


def _flip(coord, bit):
    return 1 - coord if bit else coord


def _params(*sem):
    return pltpu.CompilerParams(dimension_semantics=sem)


def _small_allgather(name, v):
    n = v.shape[-1]

    def body(v_ref, out_ref, send_sems, recv_sems):
        x, y, c = _place()
        me = 4 * x + 2 * y + c
        out_ref[me] = v_ref[...]
        copies = []
        for k in range(1, N_DEV):
            peer = (_flip(x, (k >> 2) & 1), _flip(y, (k >> 1) & 1), _flip(c, k & 1))
            cp = pltpu.make_async_remote_copy(
                src_ref=v_ref, dst_ref=out_ref.at[me], send_sem=send_sems.at[k - 1],
                recv_sem=recv_sems.at[k - 1], device_id=peer, device_id_type=MESH)
            cp.start()
            copies.append(cp)
        for cp in copies:
            cp.wait()

    return pl.pallas_call(
        body, name=name,
        out_shape=jax.ShapeDtypeStruct((N_DEV, 1, n), F32),
        in_specs=[pl.BlockSpec(memory_space=pltpu.VMEM)],
        out_specs=pl.BlockSpec(memory_space=pltpu.VMEM),
        scratch_shapes=[pltpu.SemaphoreType.DMA((N_DEV - 1,)), pltpu.SemaphoreType.DMA((N_DEV - 1,))],
    )(v)


class _Plan:
    def __init__(self, operands, out_shapes, sems, phases):
        self.operands, self.out_shapes, self.sems, self.phases = operands, out_shapes, sems, phases


def _gather_plan(shards, dst_of, base_of, dst_shapes):
    n = len(shards)
    rows = [s.shape[0] for s in shards]

    def phases(srcs, dsts, sems):
        send_sems, recv_sems, local_sems = sems
        x, y, c = _place()
        me, sibling = (x, y, c), (x, y, 1 - c)
        chips = [(1 - x, y), (x, 1 - y), (1 - x, 1 - y)]

        def slab(i, px, py, pc):
            start = pl.multiple_of(base_of[i] + (4 * px + 2 * py + pc) * rows[i], 16)
            return dsts[dst_of[i]].at[pl.ds(start, rows[i])]

        def copy(i, k, block, to, src=None):
            return pltpu.make_async_remote_copy(
                src_ref=slab(i, *block) if src is None else src, dst_ref=slab(i, *block),
                send_sem=send_sems.at[i, k], recv_sem=recv_sems.at[i, k],
                device_id=to, device_id_type=MESH)

        def mine():
            return [pltpu.make_async_copy(srcs[i], slab(i, *me), local_sems.at[i]) for i in range(n)]

        def first():
            out = []
            for i in range(n):
                out.append(copy(i, 0, me, sibling, src=srcs[i]))
                out += [copy(i, 1 + j, me, (*chip, c), src=srcs[i]) for j, chip in enumerate(chips)]
            return out

        def passed():
            return [(copy(i, 1 + j, (*chip, c), me), copy(i, 4 + j, (*chip, c), sibling))
                    for j, chip in enumerate(chips) for i in range(n)]

        def start():
            for cp in mine() + first():
                cp.start()

        def middle():
            for landed, onward in passed():
                landed.wait_recv()
                onward.start()

        def finish():
            for i in range(n):
                copy(i, 0, sibling, me).wait_recv()
                for j, chip in enumerate(chips):
                    copy(i, 4 + j, (*chip, 1 - c), me).wait_recv()
            for cp in first() + [onward for _, onward in passed()]:
                cp.wait_send()
            for cp in mine():
                cp.wait()

        return start, middle, finish

    sems = [pltpu.SemaphoreType.DMA((n, 7)), pltpu.SemaphoreType.DMA((n, 7)), pltpu.SemaphoreType.DMA((n,))]
    return _Plan(list(shards), [jax.ShapeDtypeStruct(s, BF16) for s in dst_shapes], sems, phases)


def _scatter_plan(grads, src_of, base_of, rows, cols):
    n = len(rows)

    def phases(srcs, recvs, sems):
        send_sems, recv_sems, local_sems = sems
        x, y, c = _place()
        me = 4 * x + 2 * y + c

        def slab(i, idx):
            start = pl.multiple_of(base_of[i] + idx * rows[i], 16)
            return srcs[src_of[i]].at[pl.ds(start, rows[i])]

        def copies():
            out = [pltpu.make_async_copy(slab(i, me), recvs[i].at[me], local_sems.at[i]) for i in range(n)]
            for k in range(1, N_DEV):
                px, py, pc = _flip(x, (k >> 2) & 1), _flip(y, (k >> 1) & 1), _flip(c, k & 1)
                out += [pltpu.make_async_remote_copy(
                    src_ref=slab(i, 4 * px + 2 * py + pc), dst_ref=recvs[i].at[me],
                    send_sem=send_sems.at[i, k - 1], recv_sem=recv_sems.at[i, k - 1],
                    device_id=(px, py, pc), device_id_type=MESH) for i in range(n)]
            return out

        def start():
            for cp in copies():
                cp.start()

        def finish():
            for cp in copies():
                cp.wait()

        return start, None, finish

    sems = [pltpu.SemaphoreType.DMA((n, 7)), pltpu.SemaphoreType.DMA((n, 7)), pltpu.SemaphoreType.DMA((n,))]
    out_shapes = [jax.ShapeDtypeStruct((N_DEV, rows[i], cols[i]), BF16) for i in range(n)]
    return _Plan(list(grads), out_shapes, sems, phases)


def _run_plan(name, plan):
    n_in, n_out = len(plan.operands), len(plan.out_shapes)

    def body(*refs):
        for phase in plan.phases(refs[:n_in], refs[n_in:n_in + n_out], refs[n_in + n_out:]):
            if phase is not None:
                phase()

    hbm = pl.BlockSpec(memory_space=pltpu.HBM)
    return pl.pallas_call(
        body, name=name, out_shape=plan.out_shapes,
        in_specs=[hbm] * n_in, out_specs=[hbm] * n_out, scratch_shapes=plan.sems,
    )(*plan.operands)


def _sum_contributions(name, recv):
    _, rows, cols = recv.shape
    tr = rows if rows <= 512 else 304 if rows % 304 == 0 else 256

    def body(r_ref, o_ref):
        acc = r_ref[0].astype(F32)
        for k in range(1, N_DEV):
            acc = acc + r_ref[k].astype(F32)
        o_ref[...] = acc

    return pl.pallas_call(
        body, name=name, grid=(rows // tr,),
        out_shape=jax.ShapeDtypeStruct((rows, cols), F32),
        in_specs=[pl.BlockSpec((N_DEV, tr, cols), lambda i: (0, i, 0))],
        out_specs=pl.BlockSpec((tr, cols), lambda i: (i, 0)),
        compiler_params=_params("parallel"),
    )(recv)


def _mm(name, a, b, mode, out_dtype, tm, tn, tk, carry=None):
    if mode == "TN":
        kk, m = a.shape
    else:
        m, kk = a.shape
    n = b.shape[0] if mode == "NT" else b.shape[1]
    tm, tn, tk = min(tm, m), min(tn, n), min(tk, kk)
    assert m % tm == 0 and n % tn == 0 and kk % tk == 0, (name, m, n, kk, tm, tn, tk)
    ni, nj, nk = m // tm, n // tn, kk // tk
    steps = ni * nj * nk
    dims = {"NN": NN_DIMS, "NT": NT_DIMS, "TN": TN_DIMS}[mode]
    n_in = len(carry.operands) if carry else 0
    n_out = len(carry.out_shapes) if carry else 0
    n_acc = 1 if nk > 1 else 0
    assert not carry or steps >= 3

    def body(a_ref, b_ref, *rest):
        o_ref = rest[n_in]
        k = pl.program_id(2)
        step = (pl.program_id(0) * nj + pl.program_id(1)) * nk + k
        if carry:
            start, middle, finish = carry.phases(
                rest[:n_in], rest[n_in + 1:n_in + 1 + n_out], rest[n_in + 1 + n_out + n_acc:])
            pl.when(step == 0)(start)
        part = lax.dot_general(a_ref[...], b_ref[...], dims, preferred_element_type=F32)
        if nk == 1:
            o_ref[...] = part.astype(out_dtype)
        else:
            acc_ref = rest[n_in + 1 + n_out]

            @pl.when(k == 0)
            def _():
                acc_ref[...] = part

            @pl.when(k > 0)
            def _():
                acc_ref[...] += part

            @pl.when(k == nk - 1)
            def _():
                o_ref[...] = acc_ref[...].astype(out_dtype)

        if carry:
            if middle is not None:
                pl.when(step == (steps * 3) // 5)(middle)
            pl.when(step == steps - 1)(finish)

    a_spec = (pl.BlockSpec((tk, tm), lambda i, j, k: (k, i)) if mode == "TN"
              else pl.BlockSpec((tm, tk), lambda i, j, k: (i, k)))
    b_spec = (pl.BlockSpec((tn, tk), lambda i, j, k: (j, k)) if mode == "NT"
              else pl.BlockSpec((tk, tn), lambda i, j, k: (k, j)))
    hbm = pl.BlockSpec(memory_space=pltpu.HBM)
    out = pl.pallas_call(
        body, name=name, grid=(ni, nj, nk),
        out_shape=[jax.ShapeDtypeStruct((m, n), out_dtype)] + (carry.out_shapes if carry else []),
        in_specs=[a_spec, b_spec] + [hbm] * n_in,
        out_specs=[pl.BlockSpec((tm, tn), lambda i, j, k: (i, j))] + [hbm] * n_out,
        scratch_shapes=[pltpu.VMEM((tm, tn), F32)] * n_acc + (carry.sems if carry else []),
        compiler_params=(_params("arbitrary", "arbitrary", "arbitrary") if carry
                         else _params("parallel", "parallel", "arbitrary")),
    )(a, b, *(carry.operands if carry else []))
    return out if carry else out[0]


def _row(tm, w, off=0):
    return pl.BlockSpec((tm, w), lambda i: (i, off))


def _vec(w):
    return pl.BlockSpec((1, w), lambda i: (0, 0))


def _sigmoid(x):
    return 1.0 / (1.0 + jnp.exp(-x))


def _normmod(name, x, g, sc, sh, tm=512):
    s = x.shape[0]

    def body(x_ref, g_ref, sc_ref, sh_ref, h_ref):
        xv = x_ref[...]
        r = lax.rsqrt(jnp.mean(xv * xv, axis=-1, keepdims=True) + EPS)
        h_ref[...] = ((xv * r) * g_ref[...] * (1.0 + sc_ref[...]) + sh_ref[...]).astype(BF16)

    return pl.pallas_call(
        body, name=name, grid=(s // tm,),
        out_shape=jax.ShapeDtypeStruct((s, D), BF16),
        in_specs=[_row(tm, D), _vec(D), _vec(D), _vec(D)], out_specs=_row(tm, D),
        compiler_params=_params("parallel"),
    )(x, g, sc, sh)


def _normmod_bwd(name, dh, x, gin, g, sc, sh, tm=512):
    s = x.shape[0]

    def body(dh_ref, x_ref, gin_ref, g_ref, sc_ref, sh_ref, gout_ref, acc_ref):
        xv, dhv = x_ref[...], dh_ref[...]
        r = lax.rsqrt(jnp.mean(xv * xv, axis=-1, keepdims=True) + EPS)
        nv = xv * r
        gv, one_sc = g_ref[...], 1.0 + sc_ref[...]
        dn = dhv * gv * one_sc
        dx = r * (dn - nv * jnp.mean(dn * nv, axis=-1, keepdims=True))
        gout_ref[...] = gin_ref[...] + dx

        @pl.when(pl.program_id(0) == 0)
        def _():
            acc_ref[...] = jnp.zeros_like(acc_ref)

        dhn = dhv * nv
        acc_ref[0:1, :] += jnp.sum(dhv, axis=0, keepdims=True)
        acc_ref[1:2, :] += jnp.sum(dhn * gv, axis=0, keepdims=True)
        acc_ref[2:3, :] += jnp.sum(dhn * one_sc, axis=0, keepdims=True)

    return pl.pallas_call(
        body, name=name, grid=(s // tm,),
        out_shape=[jax.ShapeDtypeStruct((s, D), F32), jax.ShapeDtypeStruct((8, D), F32)],
        in_specs=[_row(tm, D), _row(tm, D), _row(tm, D), _vec(D), _vec(D), _vec(D)],
        out_specs=[_row(tm, D), pl.BlockSpec((8, D), lambda i: (0, 0))],
        compiler_params=_params("arbitrary"),
    )(dh, x, gin, g, sc, sh)


def _swiglu(name, ab, tm=512):
    s = ab.shape[0]

    def body(ab_ref, s_ref):
        a = ab_ref[:, :FF].astype(F32)
        b = ab_ref[:, FF:].astype(F32)
        s_ref[...] = (a * _sigmoid(a) * b).astype(BF16)

    return pl.pallas_call(
        body, name=name, grid=(s // tm,),
        out_shape=jax.ShapeDtypeStruct((s, FF), BF16),
        in_specs=[_row(tm, 2 * FF)], out_specs=_row(tm, FF),
        compiler_params=_params("parallel"),
    )(ab)


def _swiglu_bwd(name, ds, ab, tm=256):
    s = ab.shape[0]

    def body(ds_ref, ab_ref, dab_ref):
        a = ab_ref[:, :FF].astype(F32)
        b = ab_ref[:, FF:].astype(F32)
        dsv = ds_ref[...].astype(F32)
        sig = _sigmoid(a)
        dab_ref[:, :FF] = (dsv * b * (sig * (1.0 + a * (1.0 - sig)))).astype(BF16)
        dab_ref[:, FF:] = (dsv * (a * sig)).astype(BF16)

    return pl.pallas_call(
        body, name=name, grid=(s // tm,),
        out_shape=jax.ShapeDtypeStruct((s, 2 * FF), BF16),
        in_specs=[_row(tm, FF), _row(tm, 2 * FF)], out_specs=_row(tm, 2 * FF),
        compiler_params=_params("parallel"),
    )(ds, ab)


def _residual(name, x, f, gt, coef, tm=512):
    s = x.shape[0]

    def body(x_ref, f_ref, gt_ref, o_ref):
        o_ref[...] = x_ref[...] + (coef * gt_ref[...]) * f_ref[...]

    return pl.pallas_call(
        body, name=name, grid=(s // tm,),
        out_shape=jax.ShapeDtypeStruct((s, D), F32),
        in_specs=[_row(tm, D), _row(tm, D), _vec(D)], out_specs=_row(tm, D),
        compiler_params=_params("parallel"),
    )(x, f, gt)


def _gate_bwd(name, gin, f, gt, coef, tm=512):
    s = gin.shape[0]

    def body(g_ref, f_ref, gt_ref, df_ref, acc_ref):
        gv = g_ref[...]
        df_ref[...] = ((coef * gt_ref[...]) * gv).astype(BF16)

        @pl.when(pl.program_id(0) == 0)
        def _():
            acc_ref[...] = jnp.zeros_like(acc_ref)

        acc_ref[0:1, :] += coef * jnp.sum(gv * f_ref[...], axis=0, keepdims=True)

    return pl.pallas_call(
        body, name=name, grid=(s // tm,),
        out_shape=[jax.ShapeDtypeStruct((s, D), BF16), jax.ShapeDtypeStruct((8, D), F32)],
        in_specs=[_row(tm, D), _row(tm, D), _vec(D)],
        out_specs=[_row(tm, D), pl.BlockSpec((8, D), lambda i: (0, 0))],
        compiler_params=_params("arbitrary"),
    )(gin, f, gt)


def _loss_grad(x3, target, tm=512):
    s = x3.shape[0]

    def body(y_ref, t_ref, g_ref, l_ref):
        e = y_ref[...] - t_ref[...]
        g_ref[...] = e * (1.0 / D)

        @pl.when(pl.program_id(0) == 0)
        def _():
            l_ref[...] = jnp.zeros_like(l_ref)

        l_ref[...] += jnp.sum(jnp.mean(e * e, axis=-1, keepdims=True), axis=0, keepdims=True) * 0.5

    return pl.pallas_call(
        body, name="loss_grad", grid=(s // tm,),
        out_shape=[jax.ShapeDtypeStruct((s, D), F32), jax.ShapeDtypeStruct((8, 128), F32)],
        in_specs=[_row(tm, D), _row(tm, D)],
        out_specs=[_row(tm, D), pl.BlockSpec((8, 128), lambda i: (0, 0))],
        compiler_params=_params("arbitrary"),
    )(x3, target)


def _heads(x, fn):
    return jnp.concatenate([fn(x[:, h * HD:(h + 1) * HD], h) for h in range(COL // HD)], axis=1)


def _qknorm(proj, wqk, tm=512):
    s = proj.shape[0]

    def body(p_ref, w_ref, o_ref):
        pv = p_ref[...].astype(F32)
        wv = w_ref[...]

        def one(qh, h):
            r = lax.rsqrt(jnp.mean(qh * qh, axis=-1, keepdims=True) + EPS)
            return (qh * r) * wv[:, h * HD:(h + 1) * HD]

        o_ref[...] = _heads(pv, one).astype(BF16)

    return pl.pallas_call(
        body, name="qknorm", grid=(s // tm, QKW // COL),
        out_shape=jax.ShapeDtypeStruct((s, QKW), BF16),
        in_specs=[pl.BlockSpec((tm, COL), lambda i, j: (i, j)), pl.BlockSpec((1, COL), lambda i, j: (0, j))],
        out_specs=pl.BlockSpec((tm, COL), lambda i, j: (i, j)),
        compiler_params=_params("parallel", "parallel"),
    )(proj, wqk)


def _qknorm_bwd(proj, dqkn, wqk, dproj, tm=512):
    s = proj.shape[0]

    def body(p_ref, d_ref, w_ref, _, o_ref, acc_ref):
        pv = p_ref[...].astype(F32)
        dv = d_ref[...]
        wv = w_ref[...]
        sums = []

        def one(qh, h):
            dn = dv[:, h * HD:(h + 1) * HD]
            r = lax.rsqrt(jnp.mean(qh * qh, axis=-1, keepdims=True) + EPS)
            nh = qh * r
            sums.append(jnp.sum(dn * nh, axis=0, keepdims=True))
            dnw = dn * wv[:, h * HD:(h + 1) * HD]
            return r * (dnw - nh * jnp.mean(dnw * nh, axis=-1, keepdims=True))

        o_ref[...] = _heads(pv, one).astype(BF16)

        @pl.when(pl.program_id(1) == 0)
        def _():
            acc_ref[...] = jnp.zeros_like(acc_ref)

        acc_ref[0:1, :] += jnp.concatenate(sums, axis=1)

    return pl.pallas_call(
        body, name="qknorm_bwd", grid=(QKW // COL, s // tm),
        out_shape=[jax.ShapeDtypeStruct((s, IN_W), BF16), jax.ShapeDtypeStruct((8, QKW), F32)],
        in_specs=[pl.BlockSpec((tm, COL), lambda j, i: (i, j)), pl.BlockSpec((tm, COL), lambda j, i: (i, j)),
                  pl.BlockSpec((1, COL), lambda j, i: (0, j)), pl.BlockSpec(memory_space=pl.ANY)],
        out_specs=[pl.BlockSpec((tm, COL), lambda j, i: (i, j)), pl.BlockSpec((8, COL), lambda j, i: (0, j))],
        input_output_aliases={3: 0},
        compiler_params=_params("arbitrary", "arbitrary"),
    )(proj, dqkn, wqk, dproj)


def _attn_shapes(s, g):
    d = DILATIONS[g]
    tb = min(s, max(2048, 256 * d))
    sb = min(256, tb // d)
    pb = BAND * d
    assert s % tb == 0 and tb % pb == 0 and (tb // d) % sb == 0 and sb % BAND == 0
    return d, tb, sb, pb


def _lanes(x, width):
    return jnp.concatenate([x] * (width // HD), axis=1)


def _every(start, size, d):
    return pl.ds(start, size, stride=d) if d > 1 else pl.ds(start, size)


def _attn_specs(g, tb, pb, s, ahead):
    ratio = tb // pb
    if ahead:
        nbr = lambda n: jnp.minimum((n + 1) * ratio, s // pb - 1)
    else:
        nbr = lambda n: jnp.maximum(n * ratio - 1, 0)
    cur = lambda base: pl.BlockSpec((tb, HD), lambda h, n: (n, base + g * N_HEADS + h))
    side = lambda base: pl.BlockSpec((pb, HD), lambda h, n: (nbr(n), base + g * N_HEADS + h))
    tok = pl.BlockSpec((tb, HD), lambda h, n: (n, h))
    tok_side = pl.BlockSpec((pb, HD), lambda h, n: (nbr(n), h))
    return cur, side, tok, tok_side


Q_COL, K_COL, V_COL = 0, 12, 24


def _attn_fwd(g, qkn, proj):
    s = qkn.shape[0]
    d, tb, sb, pb = _attn_shapes(s, g)
    nj = tb // d // sb
    scale = HD ** -0.5

    def body(q_ref, kc_ref, kp_ref, vc_ref, vp_ref, o_ref, lse_ref, qf, kf, vf):
        n = pl.program_id(1)
        qf[...] = q_ref[...].astype(F32)
        kf[0:pb] = kp_ref[...].astype(F32)
        kf[pb:] = kc_ref[...].astype(F32)
        vf[0:pb] = vp_ref[...].astype(F32)
        vf[pb:] = vc_ref[...].astype(F32)
        for r in range(d):
            for j in range(nj):
                at = j * sb * d + r
                q = qf[_every(at, sb, d), :].astype(BF16)
                k = kf[_every(at, sb + BAND, d), :].astype(BF16)
                v = vf[_every(at, sb + BAND, d), :].astype(BF16)
                sc = lax.dot_general(q, k, NT_DIMS, preferred_element_type=F32) * scale
                qi = lax.broadcasted_iota(jnp.int32, sc.shape, 0)
                kj = lax.broadcasted_iota(jnp.int32, sc.shape, 1)
                valid = (kj >= qi) & (kj <= qi + BAND)
                if j == 0:
                    valid = valid & ((kj >= BAND) | (n > 0))
                sc = jnp.where(valid, sc, -1e30)
                m = jnp.max(sc, axis=-1, keepdims=True)
                p = jnp.exp(sc - m)
                l = jnp.sum(p, axis=-1, keepdims=True)
                o = lax.dot_general(p.astype(BF16), v, NN_DIMS, preferred_element_type=F32)
                o_ref[_every(at, sb, d), :] = o / l
                lse_ref[_every(at, sb, d), :] = jnp.broadcast_to(m + jnp.log(l), (sb, HD))

    cur, side, tok, _ = _attn_specs(g, tb, pb, s, ahead=False)
    return pl.pallas_call(
        body, name=f"attn_fwd_g{g}", grid=(N_HEADS, s // tb),
        out_shape=[jax.ShapeDtypeStruct((s, COL), F32)] * 2,
        in_specs=[cur(Q_COL), cur(K_COL), side(K_COL), cur(V_COL), side(V_COL)],
        out_specs=[tok, tok],
        scratch_shapes=[pltpu.VMEM((tb, HD), F32), pltpu.VMEM((tb + pb, HD), F32),
                        pltpu.VMEM((tb + pb, HD), F32)],
        compiler_params=_params("parallel", "arbitrary"),
    )(qkn, qkn, qkn, proj, proj)


def _attn_combine(os_, lses, tm=512):
    s = os_[0].shape[0]

    def body(o0, o1, o2, l0, l1, l2, o_ref, lse_ref):
        a, b, c = l0[...], l1[...], l2[...]
        m = jnp.maximum(jnp.maximum(a, b), c)
        ea, eb, ec = jnp.exp(a - m), jnp.exp(b - m), jnp.exp(c - m)
        tot = ea + eb + ec
        o_ref[...] = ((ea * o0[...] + eb * o1[...] + ec * o2[...]) / tot).astype(BF16)
        lse_ref[...] = m + jnp.log(tot)

    return pl.pallas_call(
        body, name="attn_combine", grid=(s // tm,),
        out_shape=[jax.ShapeDtypeStruct((s, COL), BF16), jax.ShapeDtypeStruct((s, COL), F32)],
        in_specs=[_row(tm, COL)] * 6, out_specs=[_row(tm, COL)] * 2,
        compiler_params=_params("parallel"),
    )(*os_, *lses)


def _attn_delta(do, o, tm=512):
    s = do.shape[0]

    def body(do_ref, o_ref, del_ref):
        prod = do_ref[...] * o_ref[...].astype(F32)
        del_ref[...] = _heads(prod, lambda ph, h: jnp.broadcast_to(
            jnp.sum(ph, axis=-1, keepdims=True), ph.shape))

    return pl.pallas_call(
        body, name="attn_delta", grid=(s // tm,),
        out_shape=jax.ShapeDtypeStruct((s, COL), F32),
        in_specs=[_row(tm, COL)] * 2, out_specs=_row(tm, COL),
        compiler_params=_params("parallel"),
    )(do, o)


def _attn_dq(g, qkn, proj, do, lse, delta, dqkn):
    s = qkn.shape[0]
    d, tb, sb, pb = _attn_shapes(s, g)
    nj = tb // d // sb
    scale = HD ** -0.5
    chained = dqkn is not None

    def body(q_ref, kc_ref, kp_ref, vc_ref, vp_ref, do_ref, lse_ref, del_ref, *rest):
        dq_ref, qf, kf, vf = rest[-4:]
        n = pl.program_id(1)
        qf[...] = q_ref[...].astype(F32)
        kf[0:pb] = kp_ref[...].astype(F32)
        kf[pb:] = kc_ref[...].astype(F32)
        vf[0:pb] = vp_ref[...].astype(F32)
        vf[pb:] = vc_ref[...].astype(F32)
        for r in range(d):
            for j in range(nj):
                at = j * sb * d + r
                rows = _every(at, sb, d)
                q = qf[rows, :].astype(BF16)
                k = kf[_every(at, sb + BAND, d), :].astype(BF16)
                v = vf[_every(at, sb + BAND, d), :].astype(BF16)
                sc = lax.dot_general(q, k, NT_DIMS, preferred_element_type=F32) * scale
                qi = lax.broadcasted_iota(jnp.int32, sc.shape, 0)
                kj = lax.broadcasted_iota(jnp.int32, sc.shape, 1)
                valid = (kj >= qi) & (kj <= qi + BAND)
                if j == 0:
                    valid = valid & ((kj >= BAND) | (n > 0))
                p = jnp.exp(jnp.where(valid, sc - _lanes(lse_ref[rows, :], sb + BAND), -1e30))
                dp = lax.dot_general(do_ref[rows, :].astype(BF16), v, NT_DIMS, preferred_element_type=F32)
                ds = p * (dp - _lanes(del_ref[rows, :], sb + BAND)) * scale
                dq_ref[rows, :] = lax.dot_general(ds.astype(BF16), k, NN_DIMS, preferred_element_type=F32)

    cur, side, tok, _ = _attn_specs(g, tb, pb, s, ahead=False)
    args = [qkn, qkn, qkn, proj, proj, do, lse, delta]
    specs = [cur(Q_COL), cur(K_COL), side(K_COL), cur(V_COL), side(V_COL), tok, tok, tok]
    if chained:
        args.append(dqkn)
        specs.append(pl.BlockSpec(memory_space=pl.ANY))
    return pl.pallas_call(
        body, name=f"attn_dq_g{g}", grid=(N_HEADS, s // tb),
        out_shape=jax.ShapeDtypeStruct((s, QKW), F32),
        in_specs=specs, out_specs=cur(Q_COL),
        input_output_aliases={8: 0} if chained else {},
        scratch_shapes=[pltpu.VMEM((tb, HD), F32), pltpu.VMEM((tb + pb, HD), F32),
                        pltpu.VMEM((tb + pb, HD), F32)],
        compiler_params=_params("arbitrary", "arbitrary"),
    )(*args)


def _attn_dkv(g, qkn, proj, do, lse, delta, dqkn, dproj):
    s = qkn.shape[0]
    d, tb, sb, pb = _attn_shapes(s, g)
    nj = tb // d // sb
    nt = s // tb
    scale = HD ** -0.5

    def body(k_ref, v_ref, qc_ref, qn_ref, doc_ref, don_ref, lc_ref, ln_ref, dc_ref, dn_ref, _a, _b,
             dk_ref, dv_ref, kf, vf, qf, dvf):
        n = pl.program_id(1)
        kf[...] = k_ref[...].astype(F32)
        vf[...] = v_ref[...].astype(F32)
        qf[0:tb] = qc_ref[...].astype(F32)
        qf[tb:] = qn_ref[...].astype(F32)

        def window(c_ref, n_ref, r, j):
            at = j * sb * d + r
            if j < nj - 1:
                return c_ref[_every(at, sb + BAND, d), :]
            return jnp.concatenate([c_ref[_every(at, sb, d), :], n_ref[_every(r, BAND, d), :]], axis=0)

        for r in range(d):
            for j in range(nj):
                at = j * sb * d + r
                rows = _every(at, sb, d)
                k = kf[rows, :].astype(BF16)
                v = vf[rows, :].astype(BF16)
                q = qf[_every(at, sb + BAND, d), :].astype(BF16)
                dov = window(doc_ref, don_ref, r, j).astype(BF16)
                sc = lax.dot_general(q, k, NT_DIMS, preferred_element_type=F32) * scale
                qi = lax.broadcasted_iota(jnp.int32, sc.shape, 0)
                kj = lax.broadcasted_iota(jnp.int32, sc.shape, 1)
                valid = (qi >= kj) & (qi <= kj + BAND)
                if j == nj - 1:
                    valid = valid & ((qi < sb) | (n < nt - 1))
                p = jnp.exp(jnp.where(valid, sc - _lanes(window(lc_ref, ln_ref, r, j), sb), -1e30))
                dp = lax.dot_general(dov, v, NT_DIMS, preferred_element_type=F32)
                ds = p * (dp - _lanes(window(dc_ref, dn_ref, r, j), sb)) * scale
                dvf[rows, :] = lax.dot_general(p.astype(BF16), dov, TN_DIMS, preferred_element_type=F32)
                dk_ref[rows, :] = lax.dot_general(ds.astype(BF16), q, TN_DIMS, preferred_element_type=F32)
        dv_ref[...] = dvf[...].astype(BF16)

    cur, side, tok, tok_side = _attn_specs(g, tb, pb, s, ahead=True)
    anyspec = pl.BlockSpec(memory_space=pl.ANY)
    return pl.pallas_call(
        body, name=f"attn_dkv_g{g}", grid=(N_HEADS, nt),
        out_shape=[jax.ShapeDtypeStruct((s, QKW), F32), jax.ShapeDtypeStruct((s, IN_W), BF16)],
        in_specs=[cur(K_COL), cur(V_COL), cur(Q_COL), side(Q_COL), tok, tok_side, tok, tok_side,
                  tok, tok_side, anyspec, anyspec],
        out_specs=[cur(K_COL), cur(V_COL)],
        input_output_aliases={10: 0, 11: 1},
        scratch_shapes=[pltpu.VMEM((tb, HD), F32), pltpu.VMEM((tb, HD), F32),
                        pltpu.VMEM((tb + pb, HD), F32), pltpu.VMEM((tb, HD), F32)],
        compiler_params=_params("arbitrary", "arbitrary"),
    )(qkn, proj, qkn, qkn, do, do, lse, lse, delta, delta, dqkn, dproj)


def _shift_down(x, before, k):
    rolled = pltpu.roll(x, k, 0)
    head = jnp.where(lax.broadcasted_iota(jnp.int32, before.shape, 0) < k, pltpu.roll(before, k, 0), rolled[:8])
    return jnp.concatenate([head, rolled[8:]], axis=0)


def _shift_up(x, after, k):
    rows = x.shape[0]
    rolled = pltpu.roll(x, rows - k, 0)
    tail = jnp.where(lax.broadcasted_iota(jnp.int32, after.shape, 0) >= 8 - k,
                     pltpu.roll(after, 8 - k, 0), rolled[rows - 8:])
    return jnp.concatenate([rolled[:rows - 8], tail], axis=0)


def _conv_fwd(proj, cw, tm=512):
    s = proj.shape[0]
    r16 = tm // 16

    def body(u_ref, b_ref, c_ref, up_ref, cp_ref, w_ref, z_ref):
        i = pl.program_id(1)
        xc = c_ref[...].astype(F32) * u_ref[...].astype(F32)
        xp = jnp.where(i > 0, cp_ref[8:16, :].astype(F32) * up_ref[8:16, :].astype(F32), 0.0)
        w = w_ref[...]
        conv = _shift_down(xc, xp, 2) * w[0:1] + _shift_down(xc, xp, 1) * w[1:2] + xc * w[2:3]
        z_ref[...] = (b_ref[...].astype(F32) * conv).astype(BF16)

    tile = lambda blk: pl.BlockSpec((tm, COL), lambda j, i: (i, blk + j))
    before = lambda blk: pl.BlockSpec((16, COL), lambda j, i: (jnp.maximum(i * r16 - 1, 0), blk + j))
    return pl.pallas_call(
        body, name="conv_fwd", grid=(D // COL, s // tm),
        out_shape=jax.ShapeDtypeStruct((s, D), BF16),
        in_specs=[tile(U_BLK), tile(B_BLK), tile(C_BLK), before(U_BLK), before(C_BLK),
                  pl.BlockSpec((3, COL), lambda j, i: (0, j))],
        out_specs=pl.BlockSpec((tm, COL), lambda j, i: (i, j)),
        compiler_params=_params("parallel", "parallel"),
    )(proj, proj, proj, proj, proj, cw)


def _conv_bwd(dz, proj, cw, dproj, tm=512):
    s = proj.shape[0]
    r8, r16 = tm // 8, tm // 16
    nrow = s // tm

    def body(dz_ref, u_ref, b_ref, c_ref, up_ref, cp_ref, dzn_ref, bn_ref, w_ref, _, o_ref, acc_ref):
        piece, i = pl.program_id(1), pl.program_id(2)
        u, c = u_ref[...].astype(F32), c_ref[...].astype(F32)
        bv = b_ref[...].astype(F32)
        dzv = dz_ref[...]
        w = w_ref[...]

        @pl.when((piece == 0) & (i == 0))
        def _():
            acc_ref[...] = jnp.zeros_like(acc_ref)

        @pl.when(piece == 1)
        def _():
            xc = c * u
            xp = jnp.where(i > 0, cp_ref[8:16, :].astype(F32) * up_ref[8:16, :].astype(F32), 0.0)
            x2, x1 = _shift_down(xc, xp, 2), _shift_down(xc, xp, 1)
            o_ref[...] = (dzv * (x2 * w[0:1] + x1 * w[1:2] + xc * w[2:3])).astype(BF16)
            dconv = dzv * bv
            acc_ref[0:1, :] += jnp.sum(dconv * x2, axis=0, keepdims=True)
            acc_ref[1:2, :] += jnp.sum(dconv * x1, axis=0, keepdims=True)
            acc_ref[2:3, :] += jnp.sum(dconv * xc, axis=0, keepdims=True)

        @pl.when(piece != 1)
        def _():
            dconv = dzv * bv
            dn = jnp.where(i < nrow - 1, dzn_ref[...] * bn_ref[0:8, :].astype(F32), 0.0)
            dxc = dconv * w[2:3] + _shift_up(dconv, dn, 1) * w[1:2] + _shift_up(dconv, dn, 2) * w[0:1]
            o_ref[...] = (dxc * jnp.where(piece == 0, c, u)).astype(BF16)

    tile = lambda blk: pl.BlockSpec((tm, COL), lambda j, p, i: (i, blk + j))
    before = lambda blk: pl.BlockSpec((16, COL), lambda j, p, i: (jnp.maximum(i * r16 - 1, 0), blk + j))
    after = lambda rows, blk: pl.BlockSpec(
        (rows, COL), lambda j, p, i: (jnp.minimum((i + 1) * (tm // rows), s // rows - 1), blk + j))
    return pl.pallas_call(
        body, name="conv_bwd", grid=(D // COL, 3, nrow),
        out_shape=[jax.ShapeDtypeStruct((s, IN_W), BF16), jax.ShapeDtypeStruct((8, D), F32)],
        in_specs=[tile(0), tile(U_BLK), tile(B_BLK), tile(C_BLK), before(U_BLK), before(C_BLK),
                  after(8, 0), after(16, B_BLK), pl.BlockSpec((3, COL), lambda j, p, i: (0, j)),
                  pl.BlockSpec(memory_space=pl.ANY)],
        out_specs=[pl.BlockSpec((tm, COL), lambda j, p, i: (i, U_BLK + 2 * p + j)),
                   pl.BlockSpec((8, COL), lambda j, p, i: (0, j))],
        input_output_aliases={9: 0},
        compiler_params=_params("arbitrary", "arbitrary", "arbitrary"),
    )(dz, proj, proj, proj, proj, proj, dz, proj, cw, dproj)


def _merge_fwd(ya, yc, proj, tm=512):
    s = proj.shape[0]

    def body(ya_ref, yc_ref, ga_ref, gc_ref, o_ref):
        o_ref[...] = (_sigmoid(ga_ref[...].astype(F32)) * ya_ref[...].astype(F32)
                      + _sigmoid(gc_ref[...].astype(F32)) * yc_ref[...].astype(F32)).astype(BF16)

    tile = lambda blk: pl.BlockSpec((tm, COL), lambda j, i: (i, blk + j))
    return pl.pallas_call(
        body, name="merge_fwd", grid=(D // COL, s // tm),
        out_shape=jax.ShapeDtypeStruct((s, D), BF16),
        in_specs=[tile(0), tile(0), tile(GA_BLK), tile(GC_BLK)], out_specs=tile(0),
        compiler_params=_params("parallel", "parallel"),
    )(ya, yc, proj, proj)


def _merge_bwd_branches(dm, proj, tm=512):
    s = proj.shape[0]

    def body(dm_ref, ga_ref, gc_ref, dya_ref, dyc_ref):
        dmv = dm_ref[...]
        dya_ref[...] = (dmv * _sigmoid(ga_ref[...].astype(F32))).astype(BF16)
        dyc_ref[...] = (dmv * _sigmoid(gc_ref[...].astype(F32))).astype(BF16)

    tile = lambda blk: pl.BlockSpec((tm, COL), lambda j, i: (i, blk + j))
    return pl.pallas_call(
        body, name="merge_bwd_branches", grid=(D // COL, s // tm),
        out_shape=[jax.ShapeDtypeStruct((s, D), BF16)] * 2,
        in_specs=[tile(0), tile(GA_BLK), tile(GC_BLK)], out_specs=[tile(0)] * 2,
        compiler_params=_params("parallel", "parallel"),
    )(dm, proj, proj)


def _merge_bwd_gates(dm, ya, yc, proj, tm=512):
    s = proj.shape[0]
    half = D // COL

    def body(dm_ref, ya_ref, yc_ref, g_ref, o_ref):
        y = jnp.where(pl.program_id(0) < half, ya_ref[...].astype(F32), yc_ref[...].astype(F32))
        sig = _sigmoid(g_ref[...].astype(F32))
        o_ref[...] = (dm_ref[...] * y * sig * (1.0 - sig)).astype(BF16)

    chan = pl.BlockSpec((tm, COL), lambda jj, i: (i, jj % half))
    gate = pl.BlockSpec((tm, COL), lambda jj, i: (i, GA_BLK + jj))
    return pl.pallas_call(
        body, name="merge_bwd_gates", grid=(2 * half, s // tm),
        out_shape=jax.ShapeDtypeStruct((s, IN_W), BF16),
        in_specs=[chan, chan, chan, gate], out_specs=gate,
        compiler_params=_params("parallel", "parallel"),
    )(dm, ya, yc, proj)


def _mod_part(c_all, w_ada, b_part):
    def body(c_ref, w_ref, b_ref, o_ref):
        cv = c_ref[...]
        act = cv * _sigmoid(cv)
        o_ref[...] = jnp.dot(act, w_ref[...], preferred_element_type=F32,
                             precision=lax.Precision.HIGHEST) + b_ref[...]

    return pl.pallas_call(
        body, name="mod_part", out_shape=jax.ShapeDtypeStruct((N_DEV, w_ada.shape[1]), F32),
    )(c_all, w_ada, b_part)


def _w_ada_grad(c_all_t, dmod_part):
    def body(c_ref, d_ref, o_ref):
        cv = c_ref[...]
        act = cv * _sigmoid(cv)
        dv = d_ref[...]
        acc = act[:, 0:1] * dv[0:1, :]
        for b in range(1, N_DEV):
            acc = acc + act[:, b:b + 1] * dv[b:b + 1, :]
        o_ref[...] = acc

    return pl.pallas_call(
        body, name="w_ada_grad", out_shape=jax.ShapeDtypeStruct((D, dmod_part.shape[1]), F32),
    )(c_all_t, dmod_part)


def _sum_rows(name, v):
    def body(v_ref, o_ref):
        acc = v_ref[0]
        for k in range(1, N_DEV):
            acc = acc + v_ref[k]
        o_ref[...] = acc

    return pl.pallas_call(body, name=name, out_shape=jax.ShapeDtypeStruct(v.shape[1:], F32))(v)


def _adamw(name, w, g, m, v):
    rows, cols = w.shape
    tr = 256 if rows % 256 == 0 and rows * cols > 512 * 1024 else rows
    c1 = 1.0 - ADAM_B1 ** ADAM_STEP
    c2 = 1.0 - ADAM_B2 ** ADAM_STEP

    def body(w_ref, g_ref, m_ref, v_ref, d_ref, nm_ref, nv_ref):
        gv = g_ref[...]
        nm = ADAM_B1 * m_ref[...] + (1.0 - ADAM_B1) * gv
        nv = ADAM_B2 * v_ref[...] + (1.0 - ADAM_B2) * (gv * gv)
        nm_ref[...] = nm
        nv_ref[...] = nv
        d_ref[...] = -ADAM_LR * ((nm / c1) / (jnp.sqrt(nv / c2) + ADAM_EPS) + ADAM_WD * w_ref[...])

    spec = pl.BlockSpec((tr, cols), lambda i: (i, 0))
    return pl.pallas_call(
        body, name=name, grid=(rows // tr,),
        out_shape=[jax.ShapeDtypeStruct((rows, cols), F32)] * 3,
        in_specs=[spec] * 4, out_specs=[spec] * 3,
        compiler_params=_params("parallel"),
    )(w, g, m, v)


def _ffn_fwd(tag, x, g, sc, sh, gt, wgu, wd, carry=None):
    h = _normmod(f"{tag}_normmod", x, g, sc, sh)
    ab, *carried = _mm(f"{tag}_gate_up", h, wgu, "NT", BF16, 1024, 512, 1024, carry=carry) if carry else (
        _mm(f"{tag}_gate_up", h, wgu, "NT", BF16, 1024, 512, 1024),)
    sw = _swiglu(f"{tag}_swiglu", ab)
    f = _mm(f"{tag}_down", sw, wd, "NN", F32, 1024, 1024, FF)
    return _residual(f"{tag}_residual", x, f, gt, 0.5), (h, ab, sw, f), carried


def _ffn_bwd(tag, gout, x, saved, g, sc, sh, gt, wgu, wd, carry_down=None, carry_gate_up=None):
    h, ab, sw, f = saved
    df, gt_acc = _gate_bwd(f"{tag}_gate_bwd", gout, f, gt, 0.5)
    ds = _mm(f"{tag}_d_hidden", df, wd, "NT", BF16, 1024, 1408, 1024)
    dwd = _mm(f"{tag}_dw_down", sw, df, "TN", BF16, 1408, 1024, 512)
    dab = _swiglu_bwd(f"{tag}_swiglu_bwd", ds, ab)
    carried = []
    if carry_down:
        dwgu, *got = _mm(f"{tag}_dw_gate_up", dab, h, "TN", BF16, 1408, 1024, 512, carry=carry_down(dwd))
        carried += got
    else:
        dwgu = _mm(f"{tag}_dw_gate_up", dab, h, "TN", BF16, 1408, 1024, 512)
    if carry_gate_up:
        dh, *got = _mm(f"{tag}_d_h", dab, wgu, "NN", F32, 1024, 1024, 512, carry=carry_gate_up(dwgu))
        carried += got
    else:
        dh = _mm(f"{tag}_d_h", dab, wgu, "NN", F32, 1024, 1024, 512)
    gin, acc = _normmod_bwd(f"{tag}_normmod_bwd", dh, x, gout, g, sc, sh)
    return gin, acc, gt_acc[0:1], dwgu, dwd, carried


def kernel(x, c, w_ada, b_ada, norm_ffn1, ffn1_w_gate, ffn1_w_up, ffn1_w_down, norm_mix, w_in, q_norm, k_norm, conv_w, w_attn_branch, w_conv_branch, w_out, norm_ffn2, ffn2_w_gate, ffn2_w_up, ffn2_w_down, loss_target, m_w_ada, m_b_ada, m_norm_ffn1, m_ffn1_w_gate, m_ffn1_w_up, m_ffn1_w_down, m_norm_mix, m_w_in, m_q_norm, m_k_norm, m_conv_w, m_w_attn_branch, m_w_conv_branch, m_w_out, m_norm_ffn2, m_ffn2_w_gate, m_ffn2_w_up, m_ffn2_w_down, v_w_ada, v_b_ada, v_norm_ffn1, v_ffn1_w_gate, v_ffn1_w_up, v_ffn1_w_down, v_norm_mix, v_w_in, v_q_norm, v_k_norm, v_conv_w, v_w_attn_branch, v_w_conv_branch, v_w_out, v_norm_ffn2, v_ffn2_w_gate, v_ffn2_w_up, v_ffn2_w_down):
    me = 4 * lax.axis_index("x") + 2 * lax.axis_index("y") + lax.axis_index("c")
    x0, target = x[0], loss_target[0]
    s = x0.shape[0]
    ada_cols = w_ada.shape[2]
    cw_cols = conv_w.shape[2]

    gathered = _small_allgather(
        "gather_c_conv", jnp.concatenate([c, conv_w[0].reshape(1, 3 * cw_cols)], axis=1))[:, 0]
    c_all = gathered[:, :D]
    cw = gathered[:, D:].reshape(N_DEV, 3, cw_cols).transpose(1, 0, 2).reshape(3, D)
    b_part = lax.dynamic_slice(b_ada, (0, me * ada_cols), (1, ada_cols))
    mod_part = _mod_part(c_all, w_ada[0], b_part)
    mod_all = _small_allgather("gather_mod", mod_part.reshape(1, N_DEV * ada_cols))
    mod = lax.dynamic_slice(mod_all.reshape(N_DEV, N_DEV, ada_cols), (0, me, 0), (N_DEV, 1, ada_cols))
    mod = mod.reshape(N_MOD, 1, D)
    sh1, sc1, gt1, sh2, sc2, gt2, sh3, sc3, gt3 = [mod[i] for i in range(N_MOD)]

    tb = lambda w: w[0].T.astype(BF16)
    nb = lambda w: w[0].astype(BF16)
    ffn1_shards = [tb(ffn1_w_gate), tb(ffn1_w_up), nb(ffn1_w_down)]
    ffn2_shards = [tb(ffn2_w_gate), tb(ffn2_w_up), nb(ffn2_w_down)]
    mix_shards = [tb(w_in), tb(w_attn_branch), nb(w_conv_branch), nb(w_out)]
    ffn_dst, ffn_base, ffn_shapes = [0, 0, 1], [0, FF, 0], [(2 * FF, D), (FF, D)]
    mix_dst, mix_base, mix_shapes = [0, 1, 2, 3], [0, 0, 0, 0], [(IN_W, D), (D, COL), (D, D), (D, D)]
    wgu1, wd1 = _run_plan("gather_ffn1_weights", _gather_plan(ffn1_shards, ffn_dst, ffn_base, ffn_shapes))

    x1, saved1, (win_t, wa_t, wc, wo) = _ffn_fwd(
        "ffn1", x0, norm_ffn1, sc1, sh1, gt1, wgu1, wd1,
        carry=_gather_plan(mix_shards, mix_dst, mix_base, mix_shapes))
    h2 = _normmod("mix_normmod", x1, norm_mix, sc2, sh2)
    proj, wgu2, wd2 = _mm("mix_in_proj", h2, win_t, "NT", BF16, 1024, 512, 1024,
                          carry=_gather_plan(ffn2_shards, ffn_dst, ffn_base, ffn_shapes))
    wqk = jnp.concatenate([jnp.tile(q_norm, (1, 12)), jnp.tile(k_norm, (1, 12))], axis=1)
    qkn = _qknorm(proj, wqk)
    group_out = [_attn_fwd(g, qkn, proj) for g in range(3)]
    o, lse = _attn_combine([go[0] for go in group_out], [go[1] for go in group_out])
    ya = _mm("mix_attn_branch", o, wa_t, "NT", BF16, 1024, 1024, COL)
    z = _conv_fwd(proj, cw)
    yc = _mm("mix_conv_branch", z, wc, "NN", BF16, 1024, 1024, D)
    merged = _merge_fwd(ya, yc, proj)
    mix = _mm("mix_out_proj", merged, wo, "NN", F32, 1024, 1024, D)
    x2 = _residual("mix_residual", x1, mix, gt2, 1.0)
    x3, saved3, _ = _ffn_fwd("ffn2", x2, norm_ffn2, sc3, sh3, gt3, wgu2, wd2)
    g3, loss_part = _loss_grad(x3, target)
    loss = lax.psum(loss_part[0, 0], ("x", "y", "c"))

    ffn_rows = [sh_.shape[0] for sh_ in ffn1_shards]
    mix_rows = [sh_.shape[0] for sh_ in mix_shards]
    g2, acc3, dgt3, dwgu2, dwd2, _ = _ffn_bwd("ffn2", g3, x2, saved3, norm_ffn2, sc3, sh3, gt3, wgu2, wd2)
    dmix, gt2_acc = _gate_bwd("mix_gate_bwd", g2, mix, gt2, 1.0)
    dmerged = _mm("mix_d_merged", dmix, wo, "NT", F32, 1024, 1024, D)
    dwo = _mm("mix_dw_out", merged, dmix, "TN", BF16, 1024, 1024, 512)
    dya, dyc = _merge_bwd_branches(dmerged, proj)
    dproj = _merge_bwd_gates(dmerged, ya, yc, proj)
    dwc = _mm("mix_dw_conv_branch", z, dyc, "TN", BF16, 1024, 1024, 512)
    dz = _mm("mix_d_z", dyc, wc, "NT", F32, 1024, 1024, D)
    dproj, cw_acc = _conv_bwd(dz, proj, cw, dproj)
    dwa_t = _mm("mix_dw_attn_branch", dya, o, "TN", BF16, 1024, COL, 512)
    do = _mm("mix_d_o", dya, wa_t, "NN", F32, 1024, COL, D)
    delta = _attn_delta(do, o)
    dqkn = None
    for g in range(3):
        dqkn = _attn_dq(g, qkn, proj, do, lse, delta, dqkn)
    for g in range(3):
        dqkn, dproj = _attn_dkv(g, qkn, proj, do, lse, delta, dqkn, dproj)
    dproj, wqk_acc = _qknorm_bwd(proj, dqkn, wqk, dproj)
    dwin_t, r_f2g, r_f2u, r_f2d, r_wa, r_wc, r_wo = _mm(
        "mix_dw_in", dproj, h2, "TN", BF16, 2432, 1024, 512,
        carry=_scatter_plan([dwgu2, dwd2, dwa_t, dwc, dwo], [0, 0, 1, 2, 3, 4], [0, FF, 0, 0, 0, 0],
                            ffn_rows + mix_rows[1:], [D, D, D, COL, D, D]))
    dh2, r_win = _mm("mix_d_h", dproj, win_t, "NN", F32, 1024, 1024, 512,
                     carry=_scatter_plan([dwin_t], [0], [0], mix_rows[:1], [D]))
    g1, acc2 = _normmod_bwd("mix_normmod_bwd", dh2, x1, g2, norm_mix, sc2, sh2)
    g0, acc1, dgt1, dwgu1, dwd1, (r_f1d, r_f1g, r_f1u) = _ffn_bwd(
        "ffn1", g1, x0, saved1, norm_ffn1, sc1, sh1, gt1, wgu1, wd1,
        carry_down=lambda dwd: _scatter_plan([dwd], [0], [0], ffn_rows[2:], [D]),
        carry_gate_up=lambda dwgu: _scatter_plan([dwgu], [0, 0], [0, FF], ffn_rows[:2], [D, D]))

    dqw = jnp.sum(wqk_acc[0, :QKW // 2].reshape(12, HD), axis=0)
    dkw = jnp.sum(wqk_acc[0, QKW // 2:].reshape(12, HD), axis=0)
    small = jnp.concatenate([
        acc1[0], acc1[1], dgt1[0], acc2[0], acc2[1], gt2_acc[0], acc3[0], acc3[1], dgt3[0],
        acc1[2], acc2[2], acc3[2], dqw, dkw, cw_acc[0:3].reshape(3 * D)]).reshape(1, -1)
    small_all = _small_allgather("gather_small_grads", small)
    small_sum = _sum_rows("sum_small_grads", small_all)[0]
    n_mod = N_MOD * D
    g_b_ada = small_sum[:n_mod].reshape(1, n_mod)
    g_norm1, g_norm2, g_norm3 = [small_sum[n_mod + i * D:n_mod + (i + 1) * D].reshape(1, D) for i in range(3)]
    off = n_mod + 3 * D
    g_qn, g_kn = small_sum[off:off + HD].reshape(1, HD), small_sum[off + HD:off + 2 * HD].reshape(1, HD)
    g_cw_full = small_sum[off + 2 * HD:].reshape(3, D)
    g_cw = lax.dynamic_slice(g_cw_full, (0, me * cw_cols), (3, cw_cols))
    dmod_part = lax.dynamic_slice(small_all[:, 0, :n_mod], (0, me * ada_cols), (N_DEV, ada_cols))
    g_w_ada = _w_ada_grad(c_all.T, dmod_part)

    recvs = [r_f1g, r_f1u, r_f1d, r_f2g, r_f2u, r_f2d, r_win, r_wa, r_wc, r_wo]
    names = ["ffn1_gate", "ffn1_up", "ffn1_down", "ffn2_gate", "ffn2_up", "ffn2_down",
             "w_in", "attn_branch", "conv_branch", "w_out"]
    sums = [_sum_contributions(f"sum_{nm}", r) for nm, r in zip(names, recvs)]
    transposed = [True, True, False, True, True, False, True, True, False, False]
    gw = [sm.T if t else sm for sm, t in zip(sums, transposed)]
    g_f1g, g_f1u, g_f1d, g_f2g, g_f2u, g_f2d, g_win, g_wa, g_wc, g_wo = gw

    grad_list = [g_w_ada[None], g_b_ada, g_norm1, g_f1g[None], g_f1u[None], g_f1d[None], g_norm2, g_win[None],
                 g_qn, g_kn, g_cw[None], g_wa[None], g_wc[None], g_wo[None], g_norm3,
                 g_f2g[None], g_f2u[None], g_f2d[None]]
    weights = [w_ada, b_ada, norm_ffn1, ffn1_w_gate, ffn1_w_up, ffn1_w_down, norm_mix, w_in, q_norm, k_norm,
               conv_w, w_attn_branch, w_conv_branch, w_out, norm_ffn2, ffn2_w_gate, ffn2_w_up, ffn2_w_down]
    ms = [m_w_ada, m_b_ada, m_norm_ffn1, m_ffn1_w_gate, m_ffn1_w_up, m_ffn1_w_down, m_norm_mix, m_w_in, m_q_norm,
          m_k_norm, m_conv_w, m_w_attn_branch, m_w_conv_branch, m_w_out, m_norm_ffn2, m_ffn2_w_gate,
          m_ffn2_w_up, m_ffn2_w_down]
    vs = [v_w_ada, v_b_ada, v_norm_ffn1, v_ffn1_w_gate, v_ffn1_w_up, v_ffn1_w_down, v_norm_mix, v_w_in, v_q_norm,
          v_k_norm, v_conv_w, v_w_attn_branch, v_w_conv_branch, v_w_out, v_norm_ffn2, v_ffn2_w_gate,
          v_ffn2_w_up, v_ffn2_w_down]
    wnames = ["w_ada", "b_ada", "norm_ffn1", "ffn1_w_gate", "ffn1_w_up", "ffn1_w_down", "norm_mix", "w_in",
              "q_norm", "k_norm", "conv_w", "w_attn_branch", "w_conv_branch", "w_out", "norm_ffn2",
              "ffn2_w_gate", "ffn2_w_up", "ffn2_w_down"]
    deltas, new_ms, new_vs = [], [], []
    for nm, w, gr, m_, v_ in zip(wnames, weights, grad_list, ms, vs):
        two_d = (-1, w.shape[-1])
        dl, nm_, nv_ = _adamw(f"adamw_{nm}", w.reshape(two_d), gr.reshape(two_d), m_.reshape(two_d), v_.reshape(two_d))
        deltas.append(dl.reshape(w.shape))
        new_ms.append(nm_.reshape(w.shape))
        new_vs.append(nv_.reshape(w.shape))
    grad_out = [gr.reshape(w.shape) for gr, w in zip(grad_list, weights)]
    return (loss, g0[None], *grad_out, *deltas, *new_ms, *new_vs)
```

```python
import functools

import jax
import jax.numpy as jnp
from jax import lax
from jax.experimental import pallas as pl
from jax.experimental.pallas import tpu as pltpu

F32 = jnp.float32
BF16 = jnp.bfloat16
MESH = pl.DeviceIdType.MESH

N_DEV = 8
D = 1024
FF = 2816
HD = 128
N_HEADS = 4
DILATIONS = (1, 4, 16)
BAND = 128
QKW = 2 * 3 * N_HEADS * HD
IN_W = 9728
COL = 512
V_BLK, U_BLK, B_BLK, C_BLK, GA_BLK, GC_BLK = 6, 9, 11, 13, 15, 17
EPS = 1e-6
N_MOD = 9
ADAM_LR, ADAM_B1, ADAM_B2, ADAM_EPS, ADAM_WD, ADAM_STEP = 0.001, 0.9, 0.999, 1e-08, 0.01, 10

NT_DIMS = (((1,), (1,)), ((), ()))
TN_DIMS = (((0,), (0,)), ((), ()))
NN_DIMS = (((1,), (0,)), ((), ()))


def _place():
    return lax.axis_index("x"), lax.axis_index("y"), lax.axis_index("c")


def _flip(coord, bit):
    return 1 - coord if bit else coord


def _params(*sem):
    return pltpu.CompilerParams(dimension_semantics=sem)


def _small_allgather(name, v):
    n = v.shape[-1]

    def body(v_ref, out_ref, send_sems, recv_sems):
        x, y, c = _place()
        me = 4 * x + 2 * y + c
        out_ref[me] = v_ref[...]
        copies = []
        for k in range(1, N_DEV):
            peer = (_flip(x, (k >> 2) & 1), _flip(y, (k >> 1) & 1), _flip(c, k & 1))
            cp = pltpu.make_async_remote_copy(
                src_ref=v_ref, dst_ref=out_ref.at[me], send_sem=send_sems.at[k - 1],
                recv_sem=recv_sems.at[k - 1], device_id=peer, device_id_type=MESH)
            cp.start()
            copies.append(cp)
        for cp in copies:
            cp.wait()

    return pl.pallas_call(
        body, name=name,
        out_shape=jax.ShapeDtypeStruct((N_DEV, 1, n), F32),
        in_specs=[pl.BlockSpec(memory_space=pltpu.VMEM)],
        out_specs=pl.BlockSpec(memory_space=pltpu.VMEM),
        scratch_shapes=[pltpu.SemaphoreType.DMA((N_DEV - 1,)), pltpu.SemaphoreType.DMA((N_DEV - 1,))],
    )(v)


class _Plan:
    def __init__(self, operands, out_shapes, sems, phases):
        self.operands, self.out_shapes, self.sems, self.phases = operands, out_shapes, sems, phases


def _slab_start(base, rows, jump, idx):
    return pl.multiple_of(base + idx * rows + (idx // 4) * jump, 16)


def _gather_plan(shards, dst_of, base_of, dst_shapes, jump_of=None):
    n = len(shards)
    rows = [s.shape[0] for s in shards]
    jump_of = jump_of or [0] * n

    def phases(srcs, dsts, sems):
        send_sems, recv_sems, local_sems = sems
        x, y, c = _place()
        me, sibling = (x, y, c), (x, y, 1 - c)
        chips = [(1 - x, y), (x, 1 - y), (1 - x, 1 - y)]

        def slab(i, px, py, pc):
            start = _slab_start(base_of[i], rows[i], jump_of[i], 4 * px + 2 * py + pc)
            return dsts[dst_of[i]].at[pl.ds(start, rows[i])]

        def copy(i, k, block, to, src=None):
            return pltpu.make_async_remote_copy(
                src_ref=slab(i, *block) if src is None else src, dst_ref=slab(i, *block),
                send_sem=send_sems.at[i, k], recv_sem=recv_sems.at[i, k],
                device_id=to, device_id_type=MESH)

        def mine():
            return [pltpu.make_async_copy(srcs[i], slab(i, *me), local_sems.at[i]) for i in range(n)]

        def first():
            out = []
            for i in range(n):
                out.append(copy(i, 0, me, sibling, src=srcs[i]))
                out += [copy(i, 1 + j, me, (*chip, c), src=srcs[i]) for j, chip in enumerate(chips)]
            return out

        def passed():
            return [(copy(i, 1 + j, (*chip, c), me), copy(i, 4 + j, (*chip, c), sibling))
                    for j, chip in enumerate(chips) for i in range(n)]

        def start():
            for cp in mine() + first():
                cp.start()

        def middle():
            for landed, onward in passed():
                landed.wait_recv()
                onward.start()

        def finish():
            for i in range(n):
                copy(i, 0, sibling, me).wait_recv()
                for j, chip in enumerate(chips):
                    copy(i, 4 + j, (*chip, 1 - c), me).wait_recv()
            for cp in first() + [onward for _, onward in passed()]:
                cp.wait_send()
            for cp in mine():
                cp.wait()

        return start, middle, finish

    sems = [pltpu.SemaphoreType.DMA((n, 7)), pltpu.SemaphoreType.DMA((n, 7)), pltpu.SemaphoreType.DMA((n,))]
    return _Plan(list(shards), [jax.ShapeDtypeStruct(s, BF16) for s in dst_shapes], sems, phases)


def _scatter_plan(grads, src_of, base_of, rows, cols, jump_of=None):
    n = len(rows)
    jump_of = jump_of or [0] * n

    def phases(srcs, recvs, sems):
        send_sems, recv_sems, local_sems = sems
        x, y, c = _place()
        me = 4 * x + 2 * y + c

        def slab(i, idx):
            start = _slab_start(base_of[i], rows[i], jump_of[i], idx)
            return srcs[src_of[i]].at[pl.ds(start, rows[i])]

        def copies():
            out = [pltpu.make_async_copy(slab(i, me), recvs[i].at[me], local_sems.at[i]) for i in range(n)]
            for k in range(1, N_DEV):
                px, py, pc = _flip(x, (k >> 2) & 1), _flip(y, (k >> 1) & 1), _flip(c, k & 1)
                out += [pltpu.make_async_remote_copy(
                    src_ref=slab(i, 4 * px + 2 * py + pc), dst_ref=recvs[i].at[me],
                    send_sem=send_sems.at[i, k - 1], recv_sem=recv_sems.at[i, k - 1],
                    device_id=(px, py, pc), device_id_type=MESH) for i in range(n)]
            return out

        def start():
            for cp in copies():
                cp.start()

        def finish():
            for cp in copies():
                cp.wait()

        return start, None, finish

    sems = [pltpu.SemaphoreType.DMA((n, 7)), pltpu.SemaphoreType.DMA((n, 7)), pltpu.SemaphoreType.DMA((n,))]
    out_shapes = [jax.ShapeDtypeStruct((N_DEV, rows[i], cols[i]), BF16) for i in range(n)]
    return _Plan(list(grads), out_shapes, sems, phases)


def _run_plan(name, plan):
    n_in, n_out = len(plan.operands), len(plan.out_shapes)

    def body(*refs):
        for phase in plan.phases(refs[:n_in], refs[n_in:n_in + n_out], refs[n_in + n_out:]):
            if phase is not None:
                phase()

    hbm = pl.BlockSpec(memory_space=pltpu.HBM)
    return pl.pallas_call(
        body, name=name, out_shape=plan.out_shapes,
        in_specs=[hbm] * n_in, out_specs=[hbm] * n_out, scratch_shapes=plan.sems,
    )(*plan.operands)


def _sum_contributions(name, recv):
    _, rows, cols = recv.shape
    tr = rows if rows <= 512 else 304 if rows % 304 == 0 else 256

    def body(r_ref, o_ref):
        acc = r_ref[0].astype(F32)
        for k in range(1, N_DEV):
            acc = acc + r_ref[k].astype(F32)
        o_ref[...] = acc

    return pl.pallas_call(
        body, name=name, grid=(rows // tr,),
        out_shape=jax.ShapeDtypeStruct((rows, cols), F32),
        in_specs=[pl.BlockSpec((N_DEV, tr, cols), lambda i: (0, i, 0))],
        out_specs=pl.BlockSpec((tr, cols), lambda i: (i, 0)),
        compiler_params=_params("parallel"),
    )(recv)


def _mm(name, a, b, mode, out_dtype, tm, tn, tk, *, carry=None, tiles_in=(), tiles_out=(), epilogue=None,
        n_outer=False):
    if mode == "TN":
        kk, m = a.shape
    else:
        m, kk = a.shape
    n = b.shape[0] if mode == "NT" else b.shape[1]
    tm, tn, tk = min(tm, m), min(tn, n), min(tk, kk)
    assert m % tm == 0 and n % tn == 0 and kk % tk == 0, (name, m, n, kk, tm, tn, tk)
    ni, nj, nk = m // tm, n // tn, kk // tk
    steps = ni * nj * nk
    dims = {"NN": NN_DIMS, "NT": NT_DIMS, "TN": TN_DIMS}[mode]
    if epilogue is None:
        tiles_out = [(jax.ShapeDtypeStruct((m, n), out_dtype), (tm, tn), lambda i, j: (i, j))]
    n_tin, n_tout = len(tiles_in), len(tiles_out)
    n_in = len(carry.operands) if carry else 0
    n_out = len(carry.out_shapes) if carry else 0
    n_acc = 1 if nk > 1 else 0
    assert not carry or steps >= 3
    ij = (lambda p, q: (q, p)) if n_outer else (lambda p, q: (p, q))
    inner = ni if n_outer else nj

    def body(a_ref, b_ref, *rest):
        tin = rest[:n_tin]
        cin = rest[n_tin:n_tin + n_in]
        tout = rest[n_tin + n_in:n_tin + n_in + n_tout]
        cout = rest[n_tin + n_in + n_tout:n_tin + n_in + n_tout + n_out]
        scratch = rest[n_tin + n_in + n_tout + n_out:]
        k = pl.program_id(2)
        visit = pl.program_id(0) * inner + pl.program_id(1)
        step = visit * nk + k
        if carry:
            start, middle, finish = carry.phases(cin, cout, scratch[n_acc:])
            pl.when(step == 0)(start)
        part = lax.dot_general(a_ref[...], b_ref[...], dims, preferred_element_type=F32)

        def store(prod):
            if epilogue is None:
                tout[0][...] = prod.astype(out_dtype)
            else:
                epilogue(prod, visit == 0, tin, tout)

        if nk == 1:
            store(part)
        else:
            acc_ref = scratch[0]

            @pl.when(k == 0)
            def _():
                acc_ref[...] = part

            @pl.when((k > 0) & (k < nk - 1))
            def _():
                acc_ref[...] += part

            @pl.when(k == nk - 1)
            def _():
                store(acc_ref[...] + part)

        if carry:
            if middle is not None:
                pl.when(step == (steps * 3) // 5)(middle)
            pl.when(step == steps - 1)(finish)

    def spec(shape, fn):
        return pl.BlockSpec(shape, lambda p, q, k: fn(*ij(p, q)))

    a_spec = (pl.BlockSpec((tk, tm), lambda p, q, k: (k, ij(p, q)[0])) if mode == "TN"
              else pl.BlockSpec((tm, tk), lambda p, q, k: (ij(p, q)[0], k)))
    b_spec = (pl.BlockSpec((tn, tk), lambda p, q, k: (ij(p, q)[1], k)) if mode == "NT"
              else pl.BlockSpec((tk, tn), lambda p, q, k: (k, ij(p, q)[1])))
    hbm = pl.BlockSpec(memory_space=pltpu.HBM)
    sequential = carry or epilogue
    out = pl.pallas_call(
        body, name=name, grid=(nj, ni, nk) if n_outer else (ni, nj, nk),
        out_shape=[t[0] for t in tiles_out] + (carry.out_shapes if carry else []),
        in_specs=[a_spec, b_spec] + [spec(t[1], t[2]) for t in tiles_in] + [hbm] * n_in,
        out_specs=[spec(t[1], t[2]) for t in tiles_out] + [hbm] * n_out,
        scratch_shapes=[pltpu.VMEM((tm, tn), F32)] * n_acc + (carry.sems if carry else []),
        compiler_params=(_params("arbitrary", "arbitrary", "arbitrary") if sequential
                         else _params("parallel", "parallel", "arbitrary")),
    )(a, b, *[t[0] for t in tiles_in], *(carry.operands if carry else []))
    return out if (carry or epilogue) else out[0]


def _row(tm, w, off=0):
    return pl.BlockSpec((tm, w), lambda i: (i, off))


def _vec(w):
    return pl.BlockSpec((1, w), lambda i: (0, 0))


def _sigmoid(x):
    return 1.0 / (1.0 + jnp.exp(-x))


def _normmod(name, x, g, sc, sh, tm=512):
    s = x.shape[0]

    def body(x_ref, g_ref, sc_ref, sh_ref, h_ref):
        xv = x_ref[...]
        r = lax.rsqrt(jnp.mean(xv * xv, axis=-1, keepdims=True) + EPS)
        h_ref[...] = ((xv * r) * g_ref[...] * (1.0 + sc_ref[...]) + sh_ref[...]).astype(BF16)

    return pl.pallas_call(
        body, name=name, grid=(s // tm,),
        out_shape=jax.ShapeDtypeStruct((s, D), BF16),
        in_specs=[_row(tm, D), _vec(D), _vec(D), _vec(D)], out_specs=_row(tm, D),
        compiler_params=_params("parallel"),
    )(x, g, sc, sh)


def _normmod_bwd(name, dh, x, gin, g, sc, sh, tm=512):
    s = x.shape[0]

    def body(dh_ref, x_ref, gin_ref, g_ref, sc_ref, sh_ref, gout_ref, acc_ref):
        xv, dhv = x_ref[...], dh_ref[...]
        r = lax.rsqrt(jnp.mean(xv * xv, axis=-1, keepdims=True) + EPS)
        nv = xv * r
        gv, one_sc = g_ref[...], 1.0 + sc_ref[...]
        dn = dhv * gv * one_sc
        dx = r * (dn - nv * jnp.mean(dn * nv, axis=-1, keepdims=True))
        gout_ref[...] = gin_ref[...] + dx

        @pl.when(pl.program_id(0) == 0)
        def _():
            acc_ref[...] = jnp.zeros_like(acc_ref)

        dhn = dhv * nv
        acc_ref[0:1, :] += jnp.sum(dhv, axis=0, keepdims=True)
        acc_ref[1:2, :] += jnp.sum(dhn * gv, axis=0, keepdims=True)
        acc_ref[2:3, :] += jnp.sum(dhn * one_sc, axis=0, keepdims=True)

    return pl.pallas_call(
        body, name=name, grid=(s // tm,),
        out_shape=[jax.ShapeDtypeStruct((s, D), F32), jax.ShapeDtypeStruct((8, D), F32)],
        in_specs=[_row(tm, D), _row(tm, D), _row(tm, D), _vec(D), _vec(D), _vec(D)],
        out_specs=[_row(tm, D), pl.BlockSpec((8, D), lambda i: (0, 0))],
        compiler_params=_params("arbitrary"),
    )(dh, x, gin, g, sc, sh)


def _swiglu(name, ab, tm=512):
    s = ab.shape[0]

    def body(ab_ref, s_ref):
        a = ab_ref[:, :FF].astype(F32)
        b = ab_ref[:, FF:].astype(F32)
        s_ref[...] = (a * _sigmoid(a) * b).astype(BF16)

    return pl.pallas_call(
        body, name=name, grid=(s // tm,),
        out_shape=jax.ShapeDtypeStruct((s, FF), BF16),
        in_specs=[_row(tm, 2 * FF)], out_specs=_row(tm, FF),
        compiler_params=_params("parallel"),
    )(ab)


def _swiglu_bwd(name, ds, ab, tm=256):
    s = ab.shape[0]

    def body(ds_ref, ab_ref, dab_ref):
        a = ab_ref[:, :FF].astype(F32)
        b = ab_ref[:, FF:].astype(F32)
        dsv = ds_ref[...].astype(F32)
        sig = _sigmoid(a)
        dab_ref[:, :FF] = (dsv * b * (sig * (1.0 + a * (1.0 - sig)))).astype(BF16)
        dab_ref[:, FF:] = (dsv * (a * sig)).astype(BF16)

    return pl.pallas_call(
        body, name=name, grid=(s // tm,),
        out_shape=jax.ShapeDtypeStruct((s, 2 * FF), BF16),
        in_specs=[_row(tm, FF), _row(tm, 2 * FF)], out_specs=_row(tm, 2 * FF),
        compiler_params=_params("parallel"),
    )(ds, ab)


def _residual(name, x, f, gt, coef, tm=512):
    s = x.shape[0]

    def body(x_ref, f_ref, gt_ref, o_ref):
        o_ref[...] = x_ref[...] + (coef * gt_ref[...]) * f_ref[...]

    return pl.pallas_call(
        body, name=name, grid=(s // tm,),
        out_shape=jax.ShapeDtypeStruct((s, D), F32),
        in_specs=[_row(tm, D), _row(tm, D), _vec(D)], out_specs=_row(tm, D),
        compiler_params=_params("parallel"),
    )(x, f, gt)


def _gate_bwd(name, gin, f, gt, coef, tm=512):
    s = gin.shape[0]

    def body(g_ref, f_ref, gt_ref, df_ref, acc_ref):
        gv = g_ref[...]
        df_ref[...] = ((coef * gt_ref[...]) * gv).astype(BF16)

        @pl.when(pl.program_id(0) == 0)
        def _():
            acc_ref[...] = jnp.zeros_like(acc_ref)

        acc_ref[0:1, :] += coef * jnp.sum(gv * f_ref[...], axis=0, keepdims=True)

    return pl.pallas_call(
        body, name=name, grid=(s // tm,),
        out_shape=[jax.ShapeDtypeStruct((s, D), BF16), jax.ShapeDtypeStruct((8, D), F32)],
        in_specs=[_row(tm, D), _row(tm, D), _vec(D)],
        out_specs=[_row(tm, D), pl.BlockSpec((8, D), lambda i: (0, 0))],
        compiler_params=_params("arbitrary"),
    )(gin, f, gt)


def _loss_grad(x3, target, tm=512):
    s = x3.shape[0]

    def body(y_ref, t_ref, g_ref, l_ref):
        e = y_ref[...] - t_ref[...]
        g_ref[...] = e * (1.0 / D)

        @pl.when(pl.program_id(0) == 0)
        def _():
            l_ref[...] = jnp.zeros_like(l_ref)

        l_ref[...] += jnp.sum(jnp.mean(e * e, axis=-1, keepdims=True), axis=0, keepdims=True) * 0.5

    return pl.pallas_call(
        body, name="loss_grad", grid=(s // tm,),
        out_shape=[jax.ShapeDtypeStruct((s, D), F32), jax.ShapeDtypeStruct((8, 128), F32)],
        in_specs=[_row(tm, D), _row(tm, D)],
        out_specs=[_row(tm, D), pl.BlockSpec((8, 128), lambda i: (0, 0))],
        compiler_params=_params("arbitrary"),
    )(x3, target)


def _heads(x, fn):
    return jnp.concatenate([fn(x[:, h * HD:(h + 1) * HD], h) for h in range(COL // HD)], axis=1)


def _qknorm(proj, wqk, tm=512):
    s = proj.shape[0]

    def body(p_ref, w_ref, o_ref):
        pv = p_ref[...].astype(F32)
        wv = w_ref[...]

        def one(qh, h):
            r = lax.rsqrt(jnp.mean(qh * qh, axis=-1, keepdims=True) + EPS)
            return (qh * r) * wv[:, h * HD:(h + 1) * HD]

        o_ref[...] = _heads(pv, one).astype(BF16)

    return pl.pallas_call(
        body, name="qknorm", grid=(s // tm, QKW // COL),
        out_shape=jax.ShapeDtypeStruct((s, QKW), BF16),
        in_specs=[pl.BlockSpec((tm, COL), lambda i, j: (i, j)), pl.BlockSpec((1, COL), lambda i, j: (0, j))],
        out_specs=pl.BlockSpec((tm, COL), lambda i, j: (i, j)),
        compiler_params=_params("parallel", "parallel"),
    )(proj, wqk)


def _qknorm_bwd(proj, dqkn, wqk, dproj, tm=512):
    s = proj.shape[0]

    def body(p_ref, d_ref, w_ref, _, o_ref, acc_ref):
        pv = p_ref[...].astype(F32)
        dv = d_ref[...]
        wv = w_ref[...]
        sums = []

        def one(qh, h):
            dn = dv[:, h * HD:(h + 1) * HD]
            r = lax.rsqrt(jnp.mean(qh * qh, axis=-1, keepdims=True) + EPS)
            nh = qh * r
            sums.append(jnp.sum(dn * nh, axis=0, keepdims=True))
            dnw = dn * wv[:, h * HD:(h + 1) * HD]
            return r * (dnw - nh * jnp.mean(dnw * nh, axis=-1, keepdims=True))

        o_ref[...] = _heads(pv, one).astype(BF16)

        @pl.when(pl.program_id(1) == 0)
        def _():
            acc_ref[...] = jnp.zeros_like(acc_ref)

        acc_ref[0:1, :] += jnp.concatenate(sums, axis=1)

    return pl.pallas_call(
        body, name="qknorm_bwd", grid=(QKW // COL, s // tm),
        out_shape=[jax.ShapeDtypeStruct((s, IN_W), BF16), jax.ShapeDtypeStruct((8, QKW), F32)],
        in_specs=[pl.BlockSpec((tm, COL), lambda j, i: (i, j)), pl.BlockSpec((tm, COL), lambda j, i: (i, j)),
                  pl.BlockSpec((1, COL), lambda j, i: (0, j)), pl.BlockSpec(memory_space=pl.ANY)],
        out_specs=[pl.BlockSpec((tm, COL), lambda j, i: (i, j)), pl.BlockSpec((8, COL), lambda j, i: (0, j))],
        input_output_aliases={3: 0},
        compiler_params=_params("arbitrary", "arbitrary"),
    )(proj, dqkn, wqk, dproj)


def _attn_shapes(s, g):
    d = DILATIONS[g]
    tb = min(s, max(2048, 256 * d))
    sb = min(256, tb // d)
    pb = BAND * d
    assert s % tb == 0 and tb % pb == 0 and (tb // d) % sb == 0 and sb % BAND == 0
    return d, tb, sb, pb


def _lanes(x, width):
    return jnp.concatenate([x] * (width // HD), axis=1)


def _every(start, size, d):
    return pl.ds(start, size, stride=d) if d > 1 else pl.ds(start, size)


def _attn_specs(g, tb, pb, s, ahead):
    ratio = tb // pb
    if ahead:
        nbr = lambda n: jnp.minimum((n + 1) * ratio, s // pb - 1)
    else:
        nbr = lambda n: jnp.maximum(n * ratio - 1, 0)
    cur = lambda base: pl.BlockSpec((tb, HD), lambda h, n: (n, base + g * N_HEADS + h))
    side = lambda base: pl.BlockSpec((pb, HD), lambda h, n: (nbr(n), base + g * N_HEADS + h))
    tok = pl.BlockSpec((tb, HD), lambda h, n: (n, h))
    tok_side = pl.BlockSpec((pb, HD), lambda h, n: (nbr(n), h))
    return cur, side, tok, tok_side


Q_COL, K_COL, V_COL = 0, 12, 24


def _attn_fwd(g, qkn, proj):
    s = qkn.shape[0]
    d, tb, sb, pb = _attn_shapes(s, g)
    nj = tb // d // sb
    scale = HD ** -0.5

    def body(q_ref, kc_ref, kp_ref, vc_ref, vp_ref, o_ref, lse_ref, qf, kf, vf):
        n = pl.program_id(1)
        qf[...] = q_ref[...].astype(F32)
        kf[0:pb] = kp_ref[...].astype(F32)
        kf[pb:] = kc_ref[...].astype(F32)
        vf[0:pb] = vp_ref[...].astype(F32)
        vf[pb:] = vc_ref[...].astype(F32)
        for r in range(d):
            for j in range(nj):
                at = j * sb * d + r
                q = qf[_every(at, sb, d), :].astype(BF16)
                k = kf[_every(at, sb + BAND, d), :].astype(BF16)
                v = vf[_every(at, sb + BAND, d), :].astype(BF16)
                sc = lax.dot_general(q, k, NT_DIMS, preferred_element_type=F32) * scale
                qi = lax.broadcasted_iota(jnp.int32, sc.shape, 0)
                kj = lax.broadcasted_iota(jnp.int32, sc.shape, 1)
                valid = (kj >= qi) & (kj <= qi + BAND)
                if j == 0:
                    valid = valid & ((kj >= BAND) | (n > 0))
                sc = jnp.where(valid, sc, -1e30)
                m = jnp.max(sc, axis=-1, keepdims=True)
                p = jnp.exp(sc - m)
                l = jnp.sum(p, axis=-1, keepdims=True)
                o = lax.dot_general(p.astype(BF16), v, NN_DIMS, preferred_element_type=F32)
                o_ref[_every(at, sb, d), :] = o / l
                lse_ref[_every(at, sb, d), :] = jnp.broadcast_to(m + jnp.log(l), (sb, HD))

    cur, side, tok, _ = _attn_specs(g, tb, pb, s, ahead=False)
    return pl.pallas_call(
        body, name=f"attn_fwd_g{g}", grid=(N_HEADS, s // tb),
        out_shape=[jax.ShapeDtypeStruct((s, COL), F32)] * 2,
        in_specs=[cur(Q_COL), cur(K_COL), side(K_COL), cur(V_COL), side(V_COL)],
        out_specs=[tok, tok],
        scratch_shapes=[pltpu.VMEM((tb, HD), F32), pltpu.VMEM((tb + pb, HD), F32),
                        pltpu.VMEM((tb + pb, HD), F32)],
        compiler_params=_params("parallel", "arbitrary"),
    )(qkn, qkn, qkn, proj, proj)


def _attn_combine(os_, lses, tm=512):
    s = os_[0].shape[0]

    def body(o0, o1, o2, l0, l1, l2, o_ref, lse_ref):
        a, b, c = l0[...], l1[...], l2[...]
        m = jnp.maximum(jnp.maximum(a, b), c)
        ea, eb, ec = jnp.exp(a - m), jnp.exp(b - m), jnp.exp(c - m)
        tot = ea + eb + ec
        o_ref[...] = ((ea * o0[...] + eb * o1[...] + ec * o2[...]) / tot).astype(BF16)
        lse_ref[...] = m + jnp.log(tot)

    return pl.pallas_call(
        body, name="attn_combine", grid=(s // tm,),
        out_shape=[jax.ShapeDtypeStruct((s, COL), BF16), jax.ShapeDtypeStruct((s, COL), F32)],
        in_specs=[_row(tm, COL)] * 6, out_specs=[_row(tm, COL)] * 2,
        compiler_params=_params("parallel"),
    )(*os_, *lses)


def _attn_delta(do, o, tm=512):
    s = do.shape[0]

    def body(do_ref, o_ref, del_ref):
        prod = do_ref[...] * o_ref[...].astype(F32)
        del_ref[...] = _heads(prod, lambda ph, h: jnp.broadcast_to(
            jnp.sum(ph, axis=-1, keepdims=True), ph.shape))

    return pl.pallas_call(
        body, name="attn_delta", grid=(s // tm,),
        out_shape=jax.ShapeDtypeStruct((s, COL), F32),
        in_specs=[_row(tm, COL)] * 2, out_specs=_row(tm, COL),
        compiler_params=_params("parallel"),
    )(do, o)


def _attn_dq(g, qkn, proj, do, lse, delta, dqkn):
    s = qkn.shape[0]
    d, tb, sb, pb = _attn_shapes(s, g)
    nj = tb // d // sb
    scale = HD ** -0.5
    chained = dqkn is not None

    def body(q_ref, kc_ref, kp_ref, vc_ref, vp_ref, do_ref, lse_ref, del_ref, *rest):
        dq_ref, qf, kf, vf = rest[-4:]
        n = pl.program_id(1)
        qf[...] = q_ref[...].astype(F32)
        kf[0:pb] = kp_ref[...].astype(F32)
        kf[pb:] = kc_ref[...].astype(F32)
        vf[0:pb] = vp_ref[...].astype(F32)
        vf[pb:] = vc_ref[...].astype(F32)
        for r in range(d):
            for j in range(nj):
                at = j * sb * d + r
                rows = _every(at, sb, d)
                q = qf[rows, :].astype(BF16)
                k = kf[_every(at, sb + BAND, d), :].astype(BF16)
                v = vf[_every(at, sb + BAND, d), :].astype(BF16)
                sc = lax.dot_general(q, k, NT_DIMS, preferred_element_type=F32) * scale
                qi = lax.broadcasted_iota(jnp.int32, sc.shape, 0)
                kj = lax.broadcasted_iota(jnp.int32, sc.shape, 1)
                valid = (kj >= qi) & (kj <= qi + BAND)
                if j == 0:
                    valid = valid & ((kj >= BAND) | (n > 0))
                p = jnp.exp(jnp.where(valid, sc - _lanes(lse_ref[rows, :], sb + BAND), -1e30))
                dp = lax.dot_general(do_ref[rows, :].astype(BF16), v, NT_DIMS, preferred_element_type=F32)
                ds = p * (dp - _lanes(del_ref[rows, :], sb + BAND)) * scale
                dq_ref[rows, :] = lax.dot_general(ds.astype(BF16), k, NN_DIMS, preferred_element_type=F32)

    cur, side, tok, _ = _attn_specs(g, tb, pb, s, ahead=False)
    args = [qkn, qkn, qkn, proj, proj, do, lse, delta]
    specs = [cur(Q_COL), cur(K_COL), side(K_COL), cur(V_COL), side(V_COL), tok, tok, tok]
    if chained:
        args.append(dqkn)
        specs.append(pl.BlockSpec(memory_space=pl.ANY))
    return pl.pallas_call(
        body, name=f"attn_dq_g{g}", grid=(N_HEADS, s // tb),
        out_shape=jax.ShapeDtypeStruct((s, QKW), F32),
        in_specs=specs, out_specs=cur(Q_COL),
        input_output_aliases={8: 0} if chained else {},
        scratch_shapes=[pltpu.VMEM((tb, HD), F32), pltpu.VMEM((tb + pb, HD), F32),
                        pltpu.VMEM((tb + pb, HD), F32)],
        compiler_params=_params("arbitrary", "arbitrary"),
    )(*args)


def _attn_dkv(g, qkn, proj, do, lse, delta, dqkn, dproj):
    s = qkn.shape[0]
    d, tb, sb, pb = _attn_shapes(s, g)
    nj = tb // d // sb
    nt = s // tb
    scale = HD ** -0.5

    def body(k_ref, v_ref, qc_ref, qn_ref, doc_ref, don_ref, lc_ref, ln_ref, dc_ref, dn_ref, _a, _b,
             dk_ref, dv_ref, kf, vf, qf, dvf):
        n = pl.program_id(1)
        kf[...] = k_ref[...].astype(F32)
        vf[...] = v_ref[...].astype(F32)
        qf[0:tb] = qc_ref[...].astype(F32)
        qf[tb:] = qn_ref[...].astype(F32)

        def window(c_ref, n_ref, r, j):
            at = j * sb * d + r
            if j < nj - 1:
                return c_ref[_every(at, sb + BAND, d), :]
            return jnp.concatenate([c_ref[_every(at, sb, d), :], n_ref[_every(r, BAND, d), :]], axis=0)

        for r in range(d):
            for j in range(nj):
                at = j * sb * d + r
                rows = _every(at, sb, d)
                k = kf[rows, :].astype(BF16)
                v = vf[rows, :].astype(BF16)
                q = qf[_every(at, sb + BAND, d), :].astype(BF16)
                dov = window(doc_ref, don_ref, r, j).astype(BF16)
                sc = lax.dot_general(q, k, NT_DIMS, preferred_element_type=F32) * scale
                qi = lax.broadcasted_iota(jnp.int32, sc.shape, 0)
                kj = lax.broadcasted_iota(jnp.int32, sc.shape, 1)
                valid = (qi >= kj) & (qi <= kj + BAND)
                if j == nj - 1:
                    valid = valid & ((qi < sb) | (n < nt - 1))
                p = jnp.exp(jnp.where(valid, sc - _lanes(window(lc_ref, ln_ref, r, j), sb), -1e30))
                dp = lax.dot_general(dov, v, NT_DIMS, preferred_element_type=F32)
                ds = p * (dp - _lanes(window(dc_ref, dn_ref, r, j), sb)) * scale
                dvf[rows, :] = lax.dot_general(p.astype(BF16), dov, TN_DIMS, preferred_element_type=F32)
                dk_ref[rows, :] = lax.dot_general(ds.astype(BF16), q, TN_DIMS, preferred_element_type=F32)
        dv_ref[...] = dvf[...].astype(BF16)

    cur, side, tok, tok_side = _attn_specs(g, tb, pb, s, ahead=True)
    anyspec = pl.BlockSpec(memory_space=pl.ANY)
    return pl.pallas_call(
        body, name=f"attn_dkv_g{g}", grid=(N_HEADS, nt),
        out_shape=[jax.ShapeDtypeStruct((s, QKW), F32), jax.ShapeDtypeStruct((s, IN_W), BF16)],
        in_specs=[cur(K_COL), cur(V_COL), cur(Q_COL), side(Q_COL), tok, tok_side, tok, tok_side,
                  tok, tok_side, anyspec, anyspec],
        out_specs=[cur(K_COL), cur(V_COL)],
        input_output_aliases={10: 0, 11: 1},
        scratch_shapes=[pltpu.VMEM((tb, HD), F32), pltpu.VMEM((tb, HD), F32),
                        pltpu.VMEM((tb + pb, HD), F32), pltpu.VMEM((tb, HD), F32)],
        compiler_params=_params("arbitrary", "arbitrary"),
    )(qkn, proj, qkn, qkn, do, do, lse, lse, delta, delta, dqkn, dproj)


def _shift_down(x, before, k):
    rolled = pltpu.roll(x, k, 0)
    head = jnp.where(lax.broadcasted_iota(jnp.int32, before.shape, 0) < k, pltpu.roll(before, k, 0), rolled[:8])
    return jnp.concatenate([head, rolled[8:]], axis=0)


def _shift_up(x, after, k):
    rows = x.shape[0]
    rolled = pltpu.roll(x, rows - k, 0)
    tail = jnp.where(lax.broadcasted_iota(jnp.int32, after.shape, 0) >= 8 - k,
                     pltpu.roll(after, 8 - k, 0), rolled[rows - 8:])
    return jnp.concatenate([rolled[:rows - 8], tail], axis=0)


def _conv_fwd(proj, cw, tm=512):
    s = proj.shape[0]
    r16 = tm // 16

    def body(u_ref, b_ref, c_ref, up_ref, cp_ref, w_ref, z_ref):
        i = pl.program_id(1)
        xc = c_ref[...].astype(F32) * u_ref[...].astype(F32)
        xp = jnp.where(i > 0, cp_ref[8:16, :].astype(F32) * up_ref[8:16, :].astype(F32), 0.0)
        w = w_ref[...]
        conv = _shift_down(xc, xp, 2) * w[0:1] + _shift_down(xc, xp, 1) * w[1:2] + xc * w[2:3]
        z_ref[...] = (b_ref[...].astype(F32) * conv).astype(BF16)

    tile = lambda blk: pl.BlockSpec((tm, COL), lambda j, i: (i, blk + j))
    before = lambda blk: pl.BlockSpec((16, COL), lambda j, i: (jnp.maximum(i * r16 - 1, 0), blk + j))
    return pl.pallas_call(
        body, name="conv_fwd", grid=(D // COL, s // tm),
        out_shape=jax.ShapeDtypeStruct((s, D), BF16),
        in_specs=[tile(U_BLK), tile(B_BLK), tile(C_BLK), before(U_BLK), before(C_BLK),
                  pl.BlockSpec((3, COL), lambda j, i: (0, j))],
        out_specs=pl.BlockSpec((tm, COL), lambda j, i: (i, j)),
        compiler_params=_params("parallel", "parallel"),
    )(proj, proj, proj, proj, proj, cw)


def _conv_bwd(dz, proj, cw, dproj, tm=512):
    s = proj.shape[0]
    r8, r16 = tm // 8, tm // 16
    nrow = s // tm

    def body(dz_ref, u_ref, b_ref, c_ref, up_ref, cp_ref, dzn_ref, bn_ref, w_ref, _, o_ref, acc_ref):
        piece, i = pl.program_id(1), pl.program_id(2)
        u, c = u_ref[...].astype(F32), c_ref[...].astype(F32)
        bv = b_ref[...].astype(F32)
        dzv = dz_ref[...]
        w = w_ref[...]

        @pl.when((piece == 0) & (i == 0))
        def _():
            acc_ref[...] = jnp.zeros_like(acc_ref)

        @pl.when(piece == 1)
        def _():
            xc = c * u
            xp = jnp.where(i > 0, cp_ref[8:16, :].astype(F32) * up_ref[8:16, :].astype(F32), 0.0)
            x2, x1 = _shift_down(xc, xp, 2), _shift_down(xc, xp, 1)
            o_ref[...] = (dzv * (x2 * w[0:1] + x1 * w[1:2] + xc * w[2:3])).astype(BF16)
            dconv = dzv * bv
            acc_ref[0:1, :] += jnp.sum(dconv * x2, axis=0, keepdims=True)
            acc_ref[1:2, :] += jnp.sum(dconv * x1, axis=0, keepdims=True)
            acc_ref[2:3, :] += jnp.sum(dconv * xc, axis=0, keepdims=True)

        @pl.when(piece != 1)
        def _():
            dconv = dzv * bv
            dn = jnp.where(i < nrow - 1, dzn_ref[...] * bn_ref[0:8, :].astype(F32), 0.0)
            dxc = dconv * w[2:3] + _shift_up(dconv, dn, 1) * w[1:2] + _shift_up(dconv, dn, 2) * w[0:1]
            o_ref[...] = (dxc * jnp.where(piece == 0, c, u)).astype(BF16)

    tile = lambda blk: pl.BlockSpec((tm, COL), lambda j, p, i: (i, blk + j))
    before = lambda blk: pl.BlockSpec((16, COL), lambda j, p, i: (jnp.maximum(i * r16 - 1, 0), blk + j))
    after = lambda rows, blk: pl.BlockSpec(
        (rows, COL), lambda j, p, i: (jnp.minimum((i + 1) * (tm // rows), s // rows - 1), blk + j))
    return pl.pallas_call(
        body, name="conv_bwd", grid=(D // COL, 3, nrow),
        out_shape=[jax.ShapeDtypeStruct((s, IN_W), BF16), jax.ShapeDtypeStruct((8, D), F32)],
        in_specs=[tile(0), tile(U_BLK), tile(B_BLK), tile(C_BLK), before(U_BLK), before(C_BLK),
                  after(8, 0), after(16, B_BLK), pl.BlockSpec((3, COL), lambda j, p, i: (0, j)),
                  pl.BlockSpec(memory_space=pl.ANY)],
        out_specs=[pl.BlockSpec((tm, COL), lambda j, p, i: (i, U_BLK + 2 * p + j)),
                   pl.BlockSpec((8, COL), lambda j, p, i: (0, j))],
        input_output_aliases={9: 0},
        compiler_params=_params("arbitrary", "arbitrary", "arbitrary"),
    )(dz, proj, proj, proj, proj, proj, dz, proj, cw, dproj)


def _merge_fwd(ya, yc, proj, tm=512):
    s = proj.shape[0]

    def body(ya_ref, yc_ref, ga_ref, gc_ref, o_ref):
        o_ref[...] = (_sigmoid(ga_ref[...].astype(F32)) * ya_ref[...].astype(F32)
                      + _sigmoid(gc_ref[...].astype(F32)) * yc_ref[...].astype(F32)).astype(BF16)

    tile = lambda blk: pl.BlockSpec((tm, COL), lambda j, i: (i, blk + j))
    return pl.pallas_call(
        body, name="merge_fwd", grid=(D // COL, s // tm),
        out_shape=jax.ShapeDtypeStruct((s, D), BF16),
        in_specs=[tile(0), tile(0), tile(GA_BLK), tile(GC_BLK)], out_specs=tile(0),
        compiler_params=_params("parallel", "parallel"),
    )(ya, yc, proj, proj)


def _merge_bwd_branches(dm, proj, tm=512):
    s = proj.shape[0]

    def body(dm_ref, ga_ref, gc_ref, dya_ref, dyc_ref):
        dmv = dm_ref[...]
        dya_ref[...] = (dmv * _sigmoid(ga_ref[...].astype(F32))).astype(BF16)
        dyc_ref[...] = (dmv * _sigmoid(gc_ref[...].astype(F32))).astype(BF16)

    tile = lambda blk: pl.BlockSpec((tm, COL), lambda j, i: (i, blk + j))
    return pl.pallas_call(
        body, name="merge_bwd_branches", grid=(D // COL, s // tm),
        out_shape=[jax.ShapeDtypeStruct((s, D), BF16)] * 2,
        in_specs=[tile(0), tile(GA_BLK), tile(GC_BLK)], out_specs=[tile(0)] * 2,
        compiler_params=_params("parallel", "parallel"),
    )(dm, proj, proj)


def _merge_bwd_gates(dm, ya, yc, proj, tm=512):
    s = proj.shape[0]
    half = D // COL

    def body(dm_ref, ya_ref, yc_ref, g_ref, o_ref):
        y = jnp.where(pl.program_id(0) < half, ya_ref[...].astype(F32), yc_ref[...].astype(F32))
        sig = _sigmoid(g_ref[...].astype(F32))
        o_ref[...] = (dm_ref[...] * y * sig * (1.0 - sig)).astype(BF16)

    chan = pl.BlockSpec((tm, COL), lambda jj, i: (i, jj % half))
    gate = pl.BlockSpec((tm, COL), lambda jj, i: (i, GA_BLK + jj))
    return pl.pallas_call(
        body, name="merge_bwd_gates", grid=(2 * half, s // tm),
        out_shape=jax.ShapeDtypeStruct((s, IN_W), BF16),
        in_specs=[chan, chan, chan, gate], out_specs=gate,
        compiler_params=_params("parallel", "parallel"),
    )(dm, ya, yc, proj)


def _mod_part(c_all, w_ada, b_part):
    def body(c_ref, w_ref, b_ref, o_ref):
        cv = c_ref[...]
        act = cv * _sigmoid(cv)
        o_ref[...] = jnp.dot(act, w_ref[...], preferred_element_type=F32,
                             precision=lax.Precision.HIGHEST) + b_ref[...]

    return pl.pallas_call(
        body, name="mod_part", out_shape=jax.ShapeDtypeStruct((N_DEV, w_ada.shape[1]), F32),
    )(c_all, w_ada, b_part)


def _w_ada_grad(c_all_t, dmod_part):
    def body(c_ref, d_ref, o_ref):
        cv = c_ref[...]
        act = cv * _sigmoid(cv)
        dv = d_ref[...]
        acc = act[:, 0:1] * dv[0:1, :]
        for b in range(1, N_DEV):
            acc = acc + act[:, b:b + 1] * dv[b:b + 1, :]
        o_ref[...] = acc

    return pl.pallas_call(
        body, name="w_ada_grad", out_shape=jax.ShapeDtypeStruct((D, dmod_part.shape[1]), F32),
    )(c_all_t, dmod_part)


def _sum_rows(name, v):
    def body(v_ref, o_ref):
        acc = v_ref[0]
        for k in range(1, N_DEV):
            acc = acc + v_ref[k]
        o_ref[...] = acc

    return pl.pallas_call(body, name=name, out_shape=jax.ShapeDtypeStruct(v.shape[1:], F32))(v)


def _adamw(name, w, g, m, v):
    rows, cols = w.shape
    tr = 256 if rows % 256 == 0 and rows * cols > 512 * 1024 else rows
    c1 = 1.0 - ADAM_B1 ** ADAM_STEP
    c2 = 1.0 - ADAM_B2 ** ADAM_STEP

    def body(w_ref, g_ref, m_ref, v_ref, d_ref, nm_ref, nv_ref):
        gv = g_ref[...]
        nm = ADAM_B1 * m_ref[...] + (1.0 - ADAM_B1) * gv
        nv = ADAM_B2 * v_ref[...] + (1.0 - ADAM_B2) * (gv * gv)
        nm_ref[...] = nm
        nv_ref[...] = nv
        d_ref[...] = -ADAM_LR * ((nm / c1) / (jnp.sqrt(nv / c2) + ADAM_EPS) + ADAM_WD * w_ref[...])

    spec = pl.BlockSpec((tr, cols), lambda i: (i, 0))
    return pl.pallas_call(
        body, name=name, grid=(rows // tr,),
        out_shape=[jax.ShapeDtypeStruct((rows, cols), F32)] * 3,
        in_specs=[spec] * 4, out_specs=[spec] * 3,
        compiler_params=_params("parallel"),
    )(w, g, m, v)


HALF = FF // 2


def _sds(shape, dtype):
    return jax.ShapeDtypeStruct(shape, dtype)


def _row_tile(w):
    return lambda tm: ((tm, w), lambda i, j: (i, 0))


def _one(w):
    return lambda rows: ((rows, w), lambda i, j: (0, 0))


def _gate_up_swiglu(name, h, wgu, carry=None, tm=512):
    s = h.shape[0]
    tm = min(tm, s)

    def epilogue(prod, first, tin, tout):
        ab_ref, s_ref = tout
        ab_ref[...] = prod.astype(BF16)
        a, b = prod[:, :HALF], prod[:, HALF:]
        s_ref[...] = (a * _sigmoid(a) * b).astype(BF16)

    return _mm(name, h, wgu, "NT", None, tm, FF, D, carry=carry, n_outer=True, epilogue=epilogue,
               tiles_out=[(_sds((s, 2 * FF), BF16), (tm, FF), lambda i, j: (i, j)),
                          (_sds((s, FF), BF16), (tm, HALF), lambda i, j: (i, j))])


def _d_hidden_swiglu(name, df, wd, ab, tm=512):
    s = df.shape[0]
    tm = min(tm, s)

    def epilogue(prod, first, tin, tout):
        a = tin[0][:, :HALF].astype(F32)
        b = tin[0][:, HALF:].astype(F32)
        sig = _sigmoid(a)
        tout[0][:, :HALF] = (prod * b * (sig * (1.0 + a * (1.0 - sig)))).astype(BF16)
        tout[0][:, HALF:] = (prod * (a * sig)).astype(BF16)

    return _mm(name, df, wd, "NT", None, tm, HALF, D, n_outer=True, epilogue=epilogue,
               tiles_in=[(ab, (tm, FF), lambda i, j: (i, j))],
               tiles_out=[(_sds((s, 2 * FF), BF16), (tm, FF), lambda i, j: (i, j))])[0]


def _out_residual(name, a, w, x, gt, coef, nxt, tm=512, tk=FF):
    s = a.shape[0]
    tm = min(tm, s)

    def epilogue(prod, first, tin, tout):
        x_ref, gt_ref, g_ref, sc_ref, sh_ref = tin
        f_ref, xn_ref, h_ref = tout
        f_ref[...] = prod
        xn = x_ref[...] + (coef * gt_ref[...]) * prod
        xn_ref[...] = xn
        r = lax.rsqrt(jnp.mean(xn * xn, axis=-1, keepdims=True) + EPS)
        h_ref[...] = ((xn * r) * g_ref[...] * (1.0 + sc_ref[...]) + sh_ref[...]).astype(BF16)

    row, vec = _row_tile(D)(tm), _one(D)(1)
    return _mm(name, a, w, "NN", None, tm, D, tk, epilogue=epilogue,
               tiles_in=[(x, *row), (gt, *vec)] + [(v, *vec) for v in nxt],
               tiles_out=[(_sds((s, D), F32), *row), (_sds((s, D), F32), *row), (_sds((s, D), BF16), *row)])


def _out_loss(name, a, w, x, gt, coef, target, tm=512):
    s = a.shape[0]
    tm = min(tm, s)

    def epilogue(prod, first, tin, tout):
        x_ref, gt_ref, t_ref = tin
        f_ref, g_ref, df_ref, acc_ref = tout
        f_ref[...] = prod
        cg = coef * gt_ref[...]
        e = x_ref[...] + cg * prod - t_ref[...]
        gv = e * (1.0 / D)
        g_ref[...] = gv
        df_ref[...] = (cg * gv).astype(BF16)

        @pl.when(first)
        def _():
            acc_ref[...] = jnp.zeros_like(acc_ref)

        acc_ref[0:1, :] += coef * jnp.sum(gv * prod, axis=0, keepdims=True)
        acc_ref[1:2, :] += (0.5 / D) * jnp.sum(e * e, axis=0, keepdims=True)

    row, vec = _row_tile(D)(tm), _one(D)(1)
    return _mm(name, a, w, "NN", None, tm, D, FF, epilogue=epilogue,
               tiles_in=[(x, *row), (gt, *vec), (target, *row)],
               tiles_out=[(_sds((s, D), F32), *row), (_sds((s, D), F32), *row), (_sds((s, D), BF16), *row),
                          (_sds((8, D), F32), *_one(D)(8))])


def _d_h_norm_bwd(name, da, w, x, gin, g, sc, sh, tk, before=None, carry=None, tm=512):
    s = da.shape[0]
    tm = min(tm, s)
    coef = before[2] if before else None

    def epilogue(prod, first, tin, tout):
        x_ref, gin_ref, g_ref, sc_ref, sh_ref = tin[:5]
        gout_ref, acc_ref = tout[:2]
        xv = x_ref[...]
        r = lax.rsqrt(jnp.mean(xv * xv, axis=-1, keepdims=True) + EPS)
        nv = xv * r
        gv, one_sc = g_ref[...], 1.0 + sc_ref[...]
        dn = prod * gv * one_sc
        gout = gin_ref[...] + r * (dn - nv * jnp.mean(dn * nv, axis=-1, keepdims=True))
        gout_ref[...] = gout

        @pl.when(first)
        def _():
            acc_ref[...] = jnp.zeros_like(acc_ref)

        dhn = prod * nv
        acc_ref[0:1, :] += jnp.sum(prod, axis=0, keepdims=True)
        acc_ref[1:2, :] += jnp.sum(dhn * gv, axis=0, keepdims=True)
        acc_ref[2:3, :] += jnp.sum(dhn * one_sc, axis=0, keepdims=True)
        if before:
            f_ref, gt_ref = tin[5:]
            tout[2][...] = ((coef * gt_ref[...]) * gout).astype(BF16)
            acc_ref[3:4, :] += coef * jnp.sum(gout * f_ref[...], axis=0, keepdims=True)

    row, vec = _row_tile(D)(tm), _one(D)(1)
    tiles_in = [(x, *row), (gin, *row), (g, *vec), (sc, *vec), (sh, *vec)]
    tiles_out = [(_sds((s, D), F32), *row), (_sds((8, D), F32), *_one(D)(8))]
    if before:
        tiles_in += [(before[0], *row), (before[1], *vec)]
        tiles_out.append((_sds((s, D), BF16), *row))
    return _mm(name, da, w, "NN", None, tm, D, tk, epilogue=epilogue, carry=carry,
               tiles_in=tiles_in, tiles_out=tiles_out)


def _ffn_bwd(tag, df, x, gin, h, ab, sw, g, sc, sh, wgu, wd, before=None, carry_down=None, carry_gate_up=None,
             tk_dw=512):
    dab = _d_hidden_swiglu(f"{tag}_d_hidden", df, wd, ab)
    dwd = _mm(f"{tag}_dw_down", sw, df, "TN", BF16, HALF, D, tk_dw)
    carried = []
    if carry_down:
        dwgu, *got = _mm(f"{tag}_dw_gate_up", dab, h, "TN", BF16, HALF, D, tk_dw, carry=carry_down(dwd))
        carried += got
    else:
        dwgu = _mm(f"{tag}_dw_gate_up", dab, h, "TN", BF16, HALF, D, tk_dw)
    res = _d_h_norm_bwd(f"{tag}_d_h", dab, wgu, x, gin, g, sc, sh, FF, before=before,
                        carry=carry_gate_up(dwgu) if carry_gate_up else None)
    n_own = 3 if before else 2
    return res[:n_own], dwgu, dwd, carried + list(res[n_own:])


def kernel(x, c, w_ada, b_ada, norm_ffn1, ffn1_w_gate, ffn1_w_up, ffn1_w_down, norm_mix, w_in, q_norm, k_norm, conv_w, w_attn_branch, w_conv_branch, w_out, norm_ffn2, ffn2_w_gate, ffn2_w_up, ffn2_w_down, loss_target, m_w_ada, m_b_ada, m_norm_ffn1, m_ffn1_w_gate, m_ffn1_w_up, m_ffn1_w_down, m_norm_mix, m_w_in, m_q_norm, m_k_norm, m_conv_w, m_w_attn_branch, m_w_conv_branch, m_w_out, m_norm_ffn2, m_ffn2_w_gate, m_ffn2_w_up, m_ffn2_w_down, v_w_ada, v_b_ada, v_norm_ffn1, v_ffn1_w_gate, v_ffn1_w_up, v_ffn1_w_down, v_norm_mix, v_w_in, v_q_norm, v_k_norm, v_conv_w, v_w_attn_branch, v_w_conv_branch, v_w_out, v_norm_ffn2, v_ffn2_w_gate, v_ffn2_w_up, v_ffn2_w_down):
    me = 4 * lax.axis_index("x") + 2 * lax.axis_index("y") + lax.axis_index("c")
    x0, target = x[0], loss_target[0]
    s = x0.shape[0]
    ada_cols = w_ada.shape[2]
    cw_cols = conv_w.shape[2]

    gathered = _small_allgather(
        "gather_c_conv", jnp.concatenate([c, conv_w[0].reshape(1, 3 * cw_cols)], axis=1))[:, 0]
    c_all = gathered[:, :D]
    cw = gathered[:, D:].reshape(N_DEV, 3, cw_cols).transpose(1, 0, 2).reshape(3, D)
    b_part = lax.dynamic_slice(b_ada, (0, me * ada_cols), (1, ada_cols))
    mod_part = _mod_part(c_all, w_ada[0], b_part)
    mod_all = _small_allgather("gather_mod", mod_part.reshape(1, N_DEV * ada_cols))
    mod = lax.dynamic_slice(mod_all.reshape(N_DEV, N_DEV, ada_cols), (0, me, 0), (N_DEV, 1, ada_cols))
    mod = mod.reshape(N_MOD, 1, D)
    sh1, sc1, gt1, sh2, sc2, gt2, sh3, sc3, gt3 = [mod[i] for i in range(N_MOD)]

    tb = lambda w: w[0].T.astype(BF16)
    nb = lambda w: w[0].astype(BF16)
    ffn1_shards = [tb(ffn1_w_gate), tb(ffn1_w_up), nb(ffn1_w_down)]
    ffn2_shards = [tb(ffn2_w_gate), tb(ffn2_w_up), nb(ffn2_w_down)]
    mix_shards = [tb(w_in), tb(w_attn_branch), nb(w_conv_branch), nb(w_out)]
    ffn_dst, ffn_base, ffn_jump, ffn_shapes = [0, 0, 1], [0, HALF, 0], [HALF, HALF, 0], [(2 * FF, D), (FF, D)]
    mix_dst, mix_base, mix_shapes = [0, 1, 2, 3], [0, 0, 0, 0], [(IN_W, D), (D, COL), (D, D), (D, D)]
    wgu1, wd1 = _run_plan("gather_ffn1_weights",
                          _gather_plan(ffn1_shards, ffn_dst, ffn_base, ffn_shapes, ffn_jump))

    h1 = _normmod("ffn1_normmod", x0, norm_ffn1, sc1, sh1)
    ab1, s1, win_t, wa_t, wc, wo = _gate_up_swiglu(
        "ffn1_gate_up", h1, wgu1, carry=_gather_plan(mix_shards, mix_dst, mix_base, mix_shapes))
    f1, x1, h2 = _out_residual("ffn1_down", s1, wd1, x0, gt1, 0.5, (norm_mix, sc2, sh2))
    proj, wgu2, wd2 = _mm("mix_in_proj", h2, win_t, "NT", BF16, 1024, 512, 1024,
                          carry=_gather_plan(ffn2_shards, ffn_dst, ffn_base, ffn_shapes, ffn_jump))
    wqk = jnp.concatenate([jnp.tile(q_norm, (1, 12)), jnp.tile(k_norm, (1, 12))], axis=1)
    qkn = _qknorm(proj, wqk)
    group_out = [_attn_fwd(g, qkn, proj) for g in range(3)]
    o, lse = _attn_combine([go[0] for go in group_out], [go[1] for go in group_out])
    ya = _mm("mix_attn_branch", o, wa_t, "NT", BF16, 1024, 1024, COL)
    z = _conv_fwd(proj, cw)
    yc = _mm("mix_conv_branch", z, wc, "NN", BF16, 1024, 1024, D)
    merged = _merge_fwd(ya, yc, proj)
    mix, x2, h3 = _out_residual("mix_out_proj", merged, wo, x1, gt2, 1.0, (norm_ffn2, sc3, sh3), tk=D)
    ab3, s3 = _gate_up_swiglu("ffn2_gate_up", h3, wgu2)
    f3, g3, df3, acc_out = _out_loss("ffn2_down", s3, wd2, x2, gt3, 0.5, target)
    loss = lax.psum(jnp.sum(acc_out[1]), ("x", "y", "c"))

    ffn_rows = [sh_.shape[0] for sh_ in ffn1_shards]
    mix_rows = [sh_.shape[0] for sh_ in mix_shards]
    (g2, acc3, dmix), dwgu2, dwd2, _ = _ffn_bwd(
        "ffn2", df3, x2, g3, h3, ab3, s3, norm_ffn2, sc3, sh3, wgu2, wd2, before=(mix, gt2, 1.0), tk_dw=2048)
    dmerged = _mm("mix_d_merged", dmix, wo, "NT", F32, 1024, 1024, D)
    dwo = _mm("mix_dw_out", merged, dmix, "TN", BF16, 1024, 1024, 512)
    dya, dyc = _merge_bwd_branches(dmerged, proj)
    dproj = _merge_bwd_gates(dmerged, ya, yc, proj)
    dwc = _mm("mix_dw_conv_branch", z, dyc, "TN", BF16, 1024, 1024, 512)
    dz = _mm("mix_d_z", dyc, wc, "NT", F32, 1024, 1024, D)
    dproj, cw_acc = _conv_bwd(dz, proj, cw, dproj)
    dwa_t = _mm("mix_dw_attn_branch", dya, o, "TN", BF16, 1024, COL, 512)
    do = _mm("mix_d_o", dya, wa_t, "NN", F32, 1024, COL, D)
    delta = _attn_delta(do, o)
    dqkn = None
    for g in range(3):
        dqkn = _attn_dq(g, qkn, proj, do, lse, delta, dqkn)
    for g in range(3):
        dqkn, dproj = _attn_dkv(g, qkn, proj, do, lse, delta, dqkn, dproj)
    dproj, wqk_acc = _qknorm_bwd(proj, dqkn, wqk, dproj)
    dwin_t, r_f2g, r_f2u, r_f2d, r_wa, r_wc, r_wo = _mm(
        "mix_dw_in", dproj, h2, "TN", BF16, 2432, 1024, 1024,
        carry=_scatter_plan([dwgu2, dwd2, dwa_t, dwc, dwo], [0, 0, 1, 2, 3, 4], [0, HALF, 0, 0, 0, 0],
                            ffn_rows + mix_rows[1:], [D, D, D, COL, D, D], [HALF, HALF, 0, 0, 0, 0]))
    g1, acc2, df1, r_win = _d_h_norm_bwd(
        "mix_d_h", dproj, win_t, x1, g2, norm_mix, sc2, sh2, 2432, before=(f1, gt1, 0.5),
        carry=_scatter_plan([dwin_t], [0], [0], mix_rows[:1], [D]))
    (g0, acc1), dwgu1, dwd1, (r_f1d, r_f1g, r_f1u) = _ffn_bwd(
        "ffn1", df1, x0, g1, h1, ab1, s1, norm_ffn1, sc1, sh1, wgu1, wd1,
        carry_down=lambda dwd: _scatter_plan([dwd], [0], [0], ffn_rows[2:], [D]),
        carry_gate_up=lambda dwgu: _scatter_plan([dwgu], [0, 0], [0, HALF], ffn_rows[:2], [D, D], [HALF, HALF]))

    dqw = jnp.sum(wqk_acc[0, :QKW // 2].reshape(12, HD), axis=0)
    dkw = jnp.sum(wqk_acc[0, QKW // 2:].reshape(12, HD), axis=0)
    small = jnp.concatenate([
        acc1[0], acc1[1], acc2[3], acc2[0], acc2[1], acc3[3], acc3[0], acc3[1], acc_out[0],
        acc1[2], acc2[2], acc3[2], dqw, dkw, cw_acc[0:3].reshape(3 * D)]).reshape(1, -1)
    small_all = _small_allgather("gather_small_grads", small)
    small_sum = _sum_rows("sum_small_grads", small_all)[0]
    n_mod = N_MOD * D
    g_b_ada = small_sum[:n_mod].reshape(1, n_mod)
    g_norm1, g_norm2, g_norm3 = [small_sum[n_mod + i * D:n_mod + (i + 1) * D].reshape(1, D) for i in range(3)]
    off = n_mod + 3 * D
    g_qn, g_kn = small_sum[off:off + HD].reshape(1, HD), small_sum[off + HD:off + 2 * HD].reshape(1, HD)
    g_cw_full = small_sum[off + 2 * HD:].reshape(3, D)
    g_cw = lax.dynamic_slice(g_cw_full, (0, me * cw_cols), (3, cw_cols))
    dmod_part = lax.dynamic_slice(small_all[:, 0, :n_mod], (0, me * ada_cols), (N_DEV, ada_cols))
    g_w_ada = _w_ada_grad(c_all.T, dmod_part)

    recvs = [r_f1g, r_f1u, r_f1d, r_f2g, r_f2u, r_f2d, r_win, r_wa, r_wc, r_wo]
    names = ["ffn1_gate", "ffn1_up", "ffn1_down", "ffn2_gate", "ffn2_up", "ffn2_down",
             "w_in", "attn_branch", "conv_branch", "w_out"]
    sums = [_sum_contributions(f"sum_{nm}", r) for nm, r in zip(names, recvs)]
    transposed = [True, True, False, True, True, False, True, True, False, False]
    gw = [sm.T if t else sm for sm, t in zip(sums, transposed)]
    g_f1g, g_f1u, g_f1d, g_f2g, g_f2u, g_f2d, g_win, g_wa, g_wc, g_wo = gw

    grad_list = [g_w_ada[None], g_b_ada, g_norm1, g_f1g[None], g_f1u[None], g_f1d[None], g_norm2, g_win[None],
                 g_qn, g_kn, g_cw[None], g_wa[None], g_wc[None], g_wo[None], g_norm3,
                 g_f2g[None], g_f2u[None], g_f2d[None]]
    weights = [w_ada, b_ada, norm_ffn1, ffn1_w_gate, ffn1_w_up, ffn1_w_down, norm_mix, w_in, q_norm, k_norm,
               conv_w, w_attn_branch, w_conv_branch, w_out, norm_ffn2, ffn2_w_gate, ffn2_w_up, ffn2_w_down]
    ms = [m_w_ada, m_b_ada, m_norm_ffn1, m_ffn1_w_gate, m_ffn1_w_up, m_ffn1_w_down, m_norm_mix, m_w_in, m_q_norm,
          m_k_norm, m_conv_w, m_w_attn_branch, m_w_conv_branch, m_w_out, m_norm_ffn2, m_ffn2_w_gate,
          m_ffn2_w_up, m_ffn2_w_down]
    vs = [v_w_ada, v_b_ada, v_norm_ffn1, v_ffn1_w_gate, v_ffn1_w_up, v_ffn1_w_down, v_norm_mix, v_w_in, v_q_norm,
          v_k_norm, v_conv_w, v_w_attn_branch, v_w_conv_branch, v_w_out, v_norm_ffn2, v_ffn2_w_gate,
          v_ffn2_w_up, v_ffn2_w_down]
    wnames = ["w_ada", "b_ada", "norm_ffn1", "ffn1_w_gate", "ffn1_w_up", "ffn1_w_down", "norm_mix", "w_in",
              "q_norm", "k_norm", "conv_w", "w_attn_branch", "w_conv_branch", "w_out", "norm_ffn2",
              "ffn2_w_gate", "ffn2_w_up", "ffn2_w_down"]
    deltas, new_ms, new_vs = [], [], []
    for nm, w, gr, m_, v_ in zip(wnames, weights, grad_list, ms, vs):
        two_d = (-1, w.shape[-1])
        dl, nm_, nv_ = _adamw(f"adamw_{nm}", w.reshape(two_d), gr.reshape(two_d), m_.reshape(two_d), v_.reshape(two_d))
        deltas.append(dl.reshape(w.shape))
        new_ms.append(nm_.reshape(w.shape))
        new_vs.append(nv_.reshape(w.shape))
    grad_out = [gr.reshape(w.shape) for gr, w in zip(grad_list, weights)]
    return (loss, g0[None], *grad_out, *deltas, *new_ms, *new_vs)
```

```python
import functools

import jax
import jax.numpy as jnp
from jax import lax
from jax.experimental import pallas as pl
from jax.experimental.pallas import tpu as pltpu

F32 = jnp.float32
BF16 = jnp.bfloat16
MESH = pl.DeviceIdType.MESH

N_DEV = 8
D = 1024
FF = 2816
HD = 128
N_HEADS = 4
DILATIONS = (1, 4, 16)
BAND = 128
QKW = 2 * 3 * N_HEADS * HD
IN_W = 9728
COL = 512
V_BLK, U_BLK, B_BLK, C_BLK, GA_BLK, GC_BLK = 6, 9, 11, 13, 15, 17
EPS = 1e-6
N_MOD = 9
ADAM_LR, ADAM_B1, ADAM_B2, ADAM_EPS, ADAM_WD, ADAM_STEP = 0.001, 0.9, 0.999, 1e-08, 0.01, 10

NT_DIMS = (((1,), (1,)), ((), ()))
TN_DIMS = (((0,), (0,)), ((), ()))
NN_DIMS = (((1,), (0,)), ((), ()))


def _place():
    return lax.axis_index("x"), lax.axis_index("y"), lax.axis_index("c")


def _flip(coord, bit):
    return 1 - coord if bit else coord


def _params(*sem):
    return pltpu.CompilerParams(dimension_semantics=sem)


def _small_allgather(name, v):
    n = v.shape[-1]

    def body(v_ref, out_ref, send_sems, recv_sems):
        x, y, c = _place()
        me = 4 * x + 2 * y + c
        out_ref[me] = v_ref[...]
        copies = []
        for k in range(1, N_DEV):
            peer = (_flip(x, (k >> 2) & 1), _flip(y, (k >> 1) & 1), _flip(c, k & 1))
            cp = pltpu.make_async_remote_copy(
                src_ref=v_ref, dst_ref=out_ref.at[me], send_sem=send_sems.at[k - 1],
                recv_sem=recv_sems.at[k - 1], device_id=peer, device_id_type=MESH)
            cp.start()
            copies.append(cp)
        for cp in copies:
            cp.wait()

    return pl.pallas_call(
        body, name=name,
        out_shape=jax.ShapeDtypeStruct((N_DEV, 1, n), F32),
        in_specs=[pl.BlockSpec(memory_space=pltpu.VMEM)],
        out_specs=pl.BlockSpec(memory_space=pltpu.VMEM),
        scratch_shapes=[pltpu.SemaphoreType.DMA((N_DEV - 1,)), pltpu.SemaphoreType.DMA((N_DEV - 1,))],
    )(v)


class _Plan:
    def __init__(self, operands, out_shapes, sems, phases):
        self.operands, self.out_shapes, self.sems, self.phases = operands, out_shapes, sems, phases


def _slab_start(base, rows, jump, idx):
    return pl.multiple_of(base + idx * rows + (idx // 4) * jump, 16)


def _gather_plan(shards, dst_of, base_of, dst_shapes, jump_of=None):
    n = len(shards)
    rows = [s.shape[0] for s in shards]
    jump_of = jump_of or [0] * n

    def phases(srcs, dsts, sems):
        send_sems, recv_sems, local_sems = sems
        x, y, c = _place()
        me, sibling = (x, y, c), (x, y, 1 - c)
        chips = [(1 - x, y), (x, 1 - y), (1 - x, 1 - y)]

        def slab(i, px, py, pc):
            start = _slab_start(base_of[i], rows[i], jump_of[i], 4 * px + 2 * py + pc)
            return dsts[dst_of[i]].at[pl.ds(start, rows[i])]

        def copy(i, k, block, to, src=None):
            return pltpu.make_async_remote_copy(
                src_ref=slab(i, *block) if src is None else src, dst_ref=slab(i, *block),
                send_sem=send_sems.at[i, k], recv_sem=recv_sems.at[i, k],
                device_id=to, device_id_type=MESH)

        def mine():
            return [pltpu.make_async_copy(srcs[i], slab(i, *me), local_sems.at[i]) for i in range(n)]

        def first():
            out = []
            for i in range(n):
                out.append(copy(i, 0, me, sibling, src=srcs[i]))
                out += [copy(i, 1 + j, me, (*chip, c), src=srcs[i]) for j, chip in enumerate(chips)]
            return out

        def passed():
            return [(copy(i, 1 + j, (*chip, c), me), copy(i, 4 + j, (*chip, c), sibling))
                    for j, chip in enumerate(chips) for i in range(n)]

        def start():
            for cp in mine() + first():
                cp.start()

        def middle():
            for landed, onward in passed():
                landed.wait_recv()
                onward.start()

        def finish():
            for i in range(n):
                copy(i, 0, sibling, me).wait_recv()
                for j, chip in enumerate(chips):
                    copy(i, 4 + j, (*chip, 1 - c), me).wait_recv()
            for cp in first() + [onward for _, onward in passed()]:
                cp.wait_send()
            for cp in mine():
                cp.wait()

        return start, middle, finish

    sems = [pltpu.SemaphoreType.DMA((n, 7)), pltpu.SemaphoreType.DMA((n, 7)), pltpu.SemaphoreType.DMA((n,))]
    return _Plan(list(shards), [jax.ShapeDtypeStruct(s, BF16) for s in dst_shapes], sems, phases)


def _scatter_plan(grads, src_of, base_of, rows, cols, jump_of=None):
    n = len(rows)
    jump_of = jump_of or [0] * n

    def phases(srcs, recvs, sems):
        send_sems, recv_sems, local_sems = sems
        x, y, c = _place()
        me = 4 * x + 2 * y + c

        def slab(i, idx):
            start = _slab_start(base_of[i], rows[i], jump_of[i], idx)
            return srcs[src_of[i]].at[pl.ds(start, rows[i])]

        def copies():
            out = [pltpu.make_async_copy(slab(i, me), recvs[i].at[me], local_sems.at[i]) for i in range(n)]
            for k in range(1, N_DEV):
                px, py, pc = _flip(x, (k >> 2) & 1), _flip(y, (k >> 1) & 1), _flip(c, k & 1)
                out += [pltpu.make_async_remote_copy(
                    src_ref=slab(i, 4 * px + 2 * py + pc), dst_ref=recvs[i].at[me],
                    send_sem=send_sems.at[i, k - 1], recv_sem=recv_sems.at[i, k - 1],
                    device_id=(px, py, pc), device_id_type=MESH) for i in range(n)]
            return out

        def start():
            for cp in copies():
                cp.start()

        def finish():
            for cp in copies():
                cp.wait()

        return start, None, finish

    sems = [pltpu.SemaphoreType.DMA((n, 7)), pltpu.SemaphoreType.DMA((n, 7)), pltpu.SemaphoreType.DMA((n,))]
    out_shapes = [jax.ShapeDtypeStruct((N_DEV, rows[i], cols[i]), BF16) for i in range(n)]
    return _Plan(list(grads), out_shapes, sems, phases)


def _run_plan(name, plan):
    n_in, n_out = len(plan.operands), len(plan.out_shapes)

    def body(*refs):
        for phase in plan.phases(refs[:n_in], refs[n_in:n_in + n_out], refs[n_in + n_out:]):
            if phase is not None:
                phase()

    hbm = pl.BlockSpec(memory_space=pltpu.HBM)
    return pl.pallas_call(
        body, name=name, out_shape=plan.out_shapes,
        in_specs=[hbm] * n_in, out_specs=[hbm] * n_out, scratch_shapes=plan.sems,
    )(*plan.operands)


def _sum_contributions(name, recv):
    _, rows, cols = recv.shape
    tr = rows if rows <= 512 else 304 if rows % 304 == 0 else 256

    def body(r_ref, o_ref):
        acc = r_ref[0].astype(F32)
        for k in range(1, N_DEV):
            acc = acc + r_ref[k].astype(F32)
        o_ref[...] = acc

    return pl.pallas_call(
        body, name=name, grid=(rows // tr,),
        out_shape=jax.ShapeDtypeStruct((rows, cols), F32),
        in_specs=[pl.BlockSpec((N_DEV, tr, cols), lambda i: (0, i, 0))],
        out_specs=pl.BlockSpec((tr, cols), lambda i: (i, 0)),
        compiler_params=_params("parallel"),
    )(recv)


def _mm(name, a, b, mode, out_dtype, tm, tn, tk, *, carry=None, tiles_in=(), tiles_out=(), epilogue=None,
        n_outer=False, keep_b=False, sub=1):
    if mode == "TN":
        kk, m = a.shape
    else:
        m, kk = a.shape
    n = b.shape[0] if mode == "NT" else b.shape[1]
    tm, tn, tk = min(tm, m), min(tn, n), min(tk, kk)
    assert m % tm == 0 and n % tn == 0 and kk % tk == 0, (name, m, n, kk, tm, tn, tk)
    ni, nj, nk = m // tm, n // tn, kk // tk
    steps = ni * nj * nk
    dims = {"NN": NN_DIMS, "NT": NT_DIMS, "TN": TN_DIMS}[mode]
    if epilogue is None:
        tiles_out = [(jax.ShapeDtypeStruct((m, n), out_dtype), (tm, tn), lambda i, j: (i, j))]
    n_tin, n_tout = len(tiles_in), len(tiles_out)
    n_in = len(carry.operands) if carry else 0
    n_out = len(carry.out_shapes) if carry else 0
    n_acc = 1 if nk > 1 else 0
    n_keep = 2 if keep_b else 0
    assert not carry or steps >= 3
    assert not keep_b or (nk == 1 and nj == 1)
    assert sub == 1 or (mode != "TN" and tm % (8 * sub) == 0)
    ij = (lambda p, q: (q, p)) if n_outer else (lambda p, q: (p, q))
    inner = ni if n_outer else nj
    rs = tm // sub

    def body(a_ref, b_ref, *rest):
        tin = rest[:n_tin]
        cin = rest[n_tin:n_tin + n_in]
        tout = rest[n_tin + n_in:n_tin + n_in + n_tout]
        cout = rest[n_tin + n_in + n_tout:n_tin + n_in + n_tout + n_out]
        scratch = rest[n_tin + n_in + n_tout + n_out:]
        k = pl.program_id(2)
        visit = pl.program_id(0) * inner + pl.program_id(1)
        step = visit * nk + k
        if keep_b:
            b_kept, b_sem = scratch[n_acc:n_acc + 2]

            @pl.when(step == 0)
            def _():
                cp = pltpu.make_async_copy(b_ref, b_kept, b_sem)
                cp.start()
                cp.wait()

            b_ref = b_kept
        if carry:
            start, middle, finish = carry.phases(cin, cout, scratch[n_acc + n_keep:])
            pl.when(step == 0)(start)

        def rows_of(refs, specs, c):
            if sub == 1:
                return refs
            return [r.at[pl.ds(c * rs, rs)] if t[1][0] == tm else r for r, t in zip(refs, specs)]

        for c in range(sub):
            a_rows = a_ref[...] if sub == 1 else a_ref[pl.ds(c * rs, rs), :]
            part = lax.dot_general(a_rows, b_ref[...], dims, preferred_element_type=F32)
            tin_c, tout_c = rows_of(tin, tiles_in, c), rows_of(tout, tiles_out, c)

            def store(prod, tin_c=tin_c, tout_c=tout_c, c=c):
                if epilogue is None:
                    tout_c[0][...] = prod.astype(out_dtype)
                else:
                    epilogue(prod, jnp.logical_and(visit == 0, c == 0), tin_c, tout_c)

            if nk == 1:
                store(part)
            else:
                acc_c = scratch[0] if sub == 1 else scratch[0].at[pl.ds(c * rs, rs)]

                @pl.when(k == 0)
                def _(acc_c=acc_c, part=part):
                    acc_c[...] = part

                @pl.when((k > 0) & (k < nk - 1))
                def _(acc_c=acc_c, part=part):
                    acc_c[...] += part

                @pl.when(k == nk - 1)
                def _(acc_c=acc_c, part=part, store=store):
                    store(acc_c[...] + part)

        if carry:
            if middle is not None:
                pl.when(step == (steps * 3) // 5)(middle)
            pl.when(step == steps - 1)(finish)

    def spec(shape, fn):
        return pl.BlockSpec(shape, lambda p, q, k: fn(*ij(p, q)))

    a_spec = (pl.BlockSpec((tk, tm), lambda p, q, k: (k, ij(p, q)[0])) if mode == "TN"
              else pl.BlockSpec((tm, tk), lambda p, q, k: (ij(p, q)[0], k)))
    if keep_b:
        b_spec = pl.BlockSpec(memory_space=pl.ANY)
    elif mode == "NT":
        b_spec = pl.BlockSpec((tn, tk), lambda p, q, k: (ij(p, q)[1], k))
    else:
        b_spec = pl.BlockSpec((tk, tn), lambda p, q, k: (k, ij(p, q)[1]))
    hbm = pl.BlockSpec(memory_space=pltpu.HBM)
    sequential = carry or epilogue or keep_b
    out = pl.pallas_call(
        body, name=name, grid=(nj, ni, nk) if n_outer else (ni, nj, nk),
        out_shape=[t[0] for t in tiles_out] + (carry.out_shapes if carry else []),
        in_specs=[a_spec, b_spec] + [spec(t[1], t[2]) for t in tiles_in] + [hbm] * n_in,
        out_specs=[spec(t[1], t[2]) for t in tiles_out] + [hbm] * n_out,
        scratch_shapes=([pltpu.VMEM((tm, tn), F32)] * n_acc
                        + ([pltpu.VMEM(b.shape, b.dtype), pltpu.SemaphoreType.DMA] if keep_b else [])
                        + (carry.sems if carry else [])),
        compiler_params=(_params("arbitrary", "arbitrary", "arbitrary") if sequential
                         else _params("parallel", "parallel", "arbitrary")),
    )(a, b, *[t[0] for t in tiles_in], *(carry.operands if carry else []))
    return out if (carry or epilogue) else out[0]


def _row(tm, w, off=0):
    return pl.BlockSpec((tm, w), lambda i: (i, off))


def _vec(w):
    return pl.BlockSpec((1, w), lambda i: (0, 0))


def _sigmoid(x):
    return 1.0 / (1.0 + jnp.exp(-x))


def _normmod(name, x, g, sc, sh, tm=512):
    s = x.shape[0]

    def body(x_ref, g_ref, sc_ref, sh_ref, h_ref):
        xv = x_ref[...]
        r = lax.rsqrt(jnp.mean(xv * xv, axis=-1, keepdims=True) + EPS)
        h_ref[...] = ((xv * r) * g_ref[...] * (1.0 + sc_ref[...]) + sh_ref[...]).astype(BF16)

    return pl.pallas_call(
        body, name=name, grid=(s // tm,),
        out_shape=jax.ShapeDtypeStruct((s, D), BF16),
        in_specs=[_row(tm, D), _vec(D), _vec(D), _vec(D)], out_specs=_row(tm, D),
        compiler_params=_params("parallel"),
    )(x, g, sc, sh)


def _normmod_bwd(name, dh, x, gin, g, sc, sh, tm=512):
    s = x.shape[0]

    def body(dh_ref, x_ref, gin_ref, g_ref, sc_ref, sh_ref, gout_ref, acc_ref):
        xv, dhv = x_ref[...], dh_ref[...]
        r = lax.rsqrt(jnp.mean(xv * xv, axis=-1, keepdims=True) + EPS)
        nv = xv * r
        gv, one_sc = g_ref[...], 1.0 + sc_ref[...]
        dn = dhv * gv * one_sc
        dx = r * (dn - nv * jnp.mean(dn * nv, axis=-1, keepdims=True))
        gout_ref[...] = gin_ref[...] + dx

        @pl.when(pl.program_id(0) == 0)
        def _():
            acc_ref[...] = jnp.zeros_like(acc_ref)

        dhn = dhv * nv
        acc_ref[0:1, :] += jnp.sum(dhv, axis=0, keepdims=True)
        acc_ref[1:2, :] += jnp.sum(dhn * gv, axis=0, keepdims=True)
        acc_ref[2:3, :] += jnp.sum(dhn * one_sc, axis=0, keepdims=True)

    return pl.pallas_call(
        body, name=name, grid=(s // tm,),
        out_shape=[jax.ShapeDtypeStruct((s, D), F32), jax.ShapeDtypeStruct((8, D), F32)],
        in_specs=[_row(tm, D), _row(tm, D), _row(tm, D), _vec(D), _vec(D), _vec(D)],
        out_specs=[_row(tm, D), pl.BlockSpec((8, D), lambda i: (0, 0))],
        compiler_params=_params("arbitrary"),
    )(dh, x, gin, g, sc, sh)


def _swiglu(name, ab, tm=512):
    s = ab.shape[0]

    def body(ab_ref, s_ref):
        a = ab_ref[:, :FF].astype(F32)
        b = ab_ref[:, FF:].astype(F32)
        s_ref[...] = (a * _sigmoid(a) * b).astype(BF16)

    return pl.pallas_call(
        body, name=name, grid=(s // tm,),
        out_shape=jax.ShapeDtypeStruct((s, FF), BF16),
        in_specs=[_row(tm, 2 * FF)], out_specs=_row(tm, FF),
        compiler_params=_params("parallel"),
    )(ab)


def _swiglu_bwd(name, ds, ab, tm=256):
    s = ab.shape[0]

    def body(ds_ref, ab_ref, dab_ref):
        a = ab_ref[:, :FF].astype(F32)
        b = ab_ref[:, FF:].astype(F32)
        dsv = ds_ref[...].astype(F32)
        sig = _sigmoid(a)
        dab_ref[:, :FF] = (dsv * b * (sig * (1.0 + a * (1.0 - sig)))).astype(BF16)
        dab_ref[:, FF:] = (dsv * (a * sig)).astype(BF16)

    return pl.pallas_call(
        body, name=name, grid=(s // tm,),
        out_shape=jax.ShapeDtypeStruct((s, 2 * FF), BF16),
        in_specs=[_row(tm, FF), _row(tm, 2 * FF)], out_specs=_row(tm, 2 * FF),
        compiler_params=_params("parallel"),
    )(ds, ab)


def _residual(name, x, f, gt, coef, tm=512):
    s = x.shape[0]

    def body(x_ref, f_ref, gt_ref, o_ref):
        o_ref[...] = x_ref[...] + (coef * gt_ref[...]) * f_ref[...]

    return pl.pallas_call(
        body, name=name, grid=(s // tm,),
        out_shape=jax.ShapeDtypeStruct((s, D), F32),
        in_specs=[_row(tm, D), _row(tm, D), _vec(D)], out_specs=_row(tm, D),
        compiler_params=_params("parallel"),
    )(x, f, gt)


def _gate_bwd(name, gin, f, gt, coef, tm=512):
    s = gin.shape[0]

    def body(g_ref, f_ref, gt_ref, df_ref, acc_ref):
        gv = g_ref[...]
        df_ref[...] = ((coef * gt_ref[...]) * gv).astype(BF16)

        @pl.when(pl.program_id(0) == 0)
        def _():
            acc_ref[...] = jnp.zeros_like(acc_ref)

        acc_ref[0:1, :] += coef * jnp.sum(gv * f_ref[...], axis=0, keepdims=True)

    return pl.pallas_call(
        body, name=name, grid=(s // tm,),
        out_shape=[jax.ShapeDtypeStruct((s, D), BF16), jax.ShapeDtypeStruct((8, D), F32)],
        in_specs=[_row(tm, D), _row(tm, D), _vec(D)],
        out_specs=[_row(tm, D), pl.BlockSpec((8, D), lambda i: (0, 0))],
        compiler_params=_params("arbitrary"),
    )(gin, f, gt)


def _loss_grad(x3, target, tm=512):
    s = x3.shape[0]

    def body(y_ref, t_ref, g_ref, l_ref):
        e = y_ref[...] - t_ref[...]
        g_ref[...] = e * (1.0 / D)

        @pl.when(pl.program_id(0) == 0)
        def _():
            l_ref[...] = jnp.zeros_like(l_ref)

        l_ref[...] += jnp.sum(jnp.mean(e * e, axis=-1, keepdims=True), axis=0, keepdims=True) * 0.5

    return pl.pallas_call(
        body, name="loss_grad", grid=(s // tm,),
        out_shape=[jax.ShapeDtypeStruct((s, D), F32), jax.ShapeDtypeStruct((8, 128), F32)],
        in_specs=[_row(tm, D), _row(tm, D)],
        out_specs=[_row(tm, D), pl.BlockSpec((8, 128), lambda i: (0, 0))],
        compiler_params=_params("arbitrary"),
    )(x3, target)


def _heads(x, fn):
    return jnp.concatenate([fn(x[:, h * HD:(h + 1) * HD], h) for h in range(COL // HD)], axis=1)


def _qknorm(proj, wqk, tm=512):
    s = proj.shape[0]

    def body(p_ref, w_ref, o_ref):
        pv = p_ref[...].astype(F32)
        wv = w_ref[...]

        def one(qh, h):
            r = lax.rsqrt(jnp.mean(qh * qh, axis=-1, keepdims=True) + EPS)
            return (qh * r) * wv[:, h * HD:(h + 1) * HD]

        o_ref[...] = _heads(pv, one).astype(BF16)

    return pl.pallas_call(
        body, name="qknorm", grid=(s // tm, QKW // COL),
        out_shape=jax.ShapeDtypeStruct((s, QKW), BF16),
        in_specs=[pl.BlockSpec((tm, COL), lambda i, j: (i, j)), pl.BlockSpec((1, COL), lambda i, j: (0, j))],
        out_specs=pl.BlockSpec((tm, COL), lambda i, j: (i, j)),
        compiler_params=_params("parallel", "parallel"),
    )(proj, wqk)


def _qknorm_bwd(proj, dqkn, wqk, dproj, tm=512):
    s = proj.shape[0]

    def body(p_ref, d_ref, w_ref, _, o_ref, acc_ref):
        pv = p_ref[...].astype(F32)
        dv = d_ref[...]
        wv = w_ref[...]
        sums = []

        def one(qh, h):
            dn = dv[:, h * HD:(h + 1) * HD]
            r = lax.rsqrt(jnp.mean(qh * qh, axis=-1, keepdims=True) + EPS)
            nh = qh * r
            sums.append(jnp.sum(dn * nh, axis=0, keepdims=True))
            dnw = dn * wv[:, h * HD:(h + 1) * HD]
            return r * (dnw - nh * jnp.mean(dnw * nh, axis=-1, keepdims=True))

        o_ref[...] = _heads(pv, one).astype(BF16)

        @pl.when(pl.program_id(1) == 0)
        def _():
            acc_ref[...] = jnp.zeros_like(acc_ref)

        acc_ref[0:1, :] += jnp.concatenate(sums, axis=1)

    return pl.pallas_call(
        body, name="qknorm_bwd", grid=(QKW // COL, s // tm),
        out_shape=[jax.ShapeDtypeStruct((s, IN_W), BF16), jax.ShapeDtypeStruct((8, QKW), F32)],
        in_specs=[pl.BlockSpec((tm, COL), lambda j, i: (i, j)), pl.BlockSpec((tm, COL), lambda j, i: (i, j)),
                  pl.BlockSpec((1, COL), lambda j, i: (0, j)), pl.BlockSpec(memory_space=pl.ANY)],
        out_specs=[pl.BlockSpec((tm, COL), lambda j, i: (i, j)), pl.BlockSpec((8, COL), lambda j, i: (0, j))],
        input_output_aliases={3: 0},
        compiler_params=_params("arbitrary", "arbitrary"),
    )(proj, dqkn, wqk, dproj)


def _attn_shapes(s, g):
    d = DILATIONS[g]
    tb = min(s, max(2048, 256 * d))
    sb = min(256, tb // d)
    pb = BAND * d
    assert s % tb == 0 and tb % pb == 0 and (tb // d) % sb == 0 and sb % BAND == 0
    return d, tb, sb, pb


def _lanes(x, width):
    return jnp.concatenate([x] * (width // HD), axis=1)


def _every(start, size, d):
    return pl.ds(start, size, stride=d) if d > 1 else pl.ds(start, size)


def _attn_specs(g, tb, pb, s, ahead):
    ratio = tb // pb
    if ahead:
        nbr = lambda n: jnp.minimum((n + 1) * ratio, s // pb - 1)
    else:
        nbr = lambda n: jnp.maximum(n * ratio - 1, 0)
    cur = lambda base: pl.BlockSpec((tb, HD), lambda h, n: (n, base + g * N_HEADS + h))
    side = lambda base: pl.BlockSpec((pb, HD), lambda h, n: (nbr(n), base + g * N_HEADS + h))
    tok = pl.BlockSpec((tb, HD), lambda h, n: (n, h))
    tok_side = pl.BlockSpec((pb, HD), lambda h, n: (nbr(n), h))
    return cur, side, tok, tok_side


Q_COL, K_COL, V_COL = 0, 12, 24


def _attn_fwd(g, qkn, proj):
    s = qkn.shape[0]
    d, tb, sb, pb = _attn_shapes(s, g)
    nj = tb // d // sb
    scale = HD ** -0.5

    def body(q_ref, kc_ref, kp_ref, vc_ref, vp_ref, o_ref, lse_ref, qf, kf, vf):
        n = pl.program_id(1)
        qf[...] = q_ref[...].astype(F32)
        kf[0:pb] = kp_ref[...].astype(F32)
        kf[pb:] = kc_ref[...].astype(F32)
        vf[0:pb] = vp_ref[...].astype(F32)
        vf[pb:] = vc_ref[...].astype(F32)
        for r in range(d):
            for j in range(nj):
                at = j * sb * d + r
                q = qf[_every(at, sb, d), :].astype(BF16)
                k = kf[_every(at, sb + BAND, d), :].astype(BF16)
                v = vf[_every(at, sb + BAND, d), :].astype(BF16)
                sc = lax.dot_general(q, k, NT_DIMS, preferred_element_type=F32) * scale
                qi = lax.broadcasted_iota(jnp.int32, sc.shape, 0)
                kj = lax.broadcasted_iota(jnp.int32, sc.shape, 1)
                valid = (kj >= qi) & (kj <= qi + BAND)
                if j == 0:
                    valid = valid & ((kj >= BAND) | (n > 0))
                sc = jnp.where(valid, sc, -1e30)
                m = jnp.max(sc, axis=-1, keepdims=True)
                p = jnp.exp(sc - m)
                l = jnp.sum(p, axis=-1, keepdims=True)
                o = lax.dot_general(p.astype(BF16), v, NN_DIMS, preferred_element_type=F32)
                o_ref[_every(at, sb, d), :] = o / l
                lse_ref[_every(at, sb, d), :] = jnp.broadcast_to(m + jnp.log(l), (sb, HD))

    cur, side, tok, _ = _attn_specs(g, tb, pb, s, ahead=False)
    return pl.pallas_call(
        body, name=f"attn_fwd_g{g}", grid=(N_HEADS, s // tb),
        out_shape=[jax.ShapeDtypeStruct((s, COL), F32)] * 2,
        in_specs=[cur(Q_COL), cur(K_COL), side(K_COL), cur(V_COL), side(V_COL)],
        out_specs=[tok, tok],
        scratch_shapes=[pltpu.VMEM((tb, HD), F32), pltpu.VMEM((tb + pb, HD), F32),
                        pltpu.VMEM((tb + pb, HD), F32)],
        compiler_params=_params("parallel", "arbitrary"),
    )(qkn, qkn, qkn, proj, proj)


def _attn_combine(os_, lses, tm=512):
    s = os_[0].shape[0]

    def body(o0, o1, o2, l0, l1, l2, o_ref, lse_ref):
        a, b, c = l0[...], l1[...], l2[...]
        m = jnp.maximum(jnp.maximum(a, b), c)
        ea, eb, ec = jnp.exp(a - m), jnp.exp(b - m), jnp.exp(c - m)
        tot = ea + eb + ec
        o_ref[...] = ((ea * o0[...] + eb * o1[...] + ec * o2[...]) / tot).astype(BF16)
        lse_ref[...] = m + jnp.log(tot)

    return pl.pallas_call(
        body, name="attn_combine", grid=(s // tm,),
        out_shape=[jax.ShapeDtypeStruct((s, COL), BF16), jax.ShapeDtypeStruct((s, COL), F32)],
        in_specs=[_row(tm, COL)] * 6, out_specs=[_row(tm, COL)] * 2,
        compiler_params=_params("parallel"),
    )(*os_, *lses)


def _attn_delta(do, o, tm=512):
    s = do.shape[0]

    def body(do_ref, o_ref, del_ref):
        prod = do_ref[...] * o_ref[...].astype(F32)
        del_ref[...] = _heads(prod, lambda ph, h: jnp.broadcast_to(
            jnp.sum(ph, axis=-1, keepdims=True), ph.shape))

    return pl.pallas_call(
        body, name="attn_delta", grid=(s // tm,),
        out_shape=jax.ShapeDtypeStruct((s, COL), F32),
        in_specs=[_row(tm, COL)] * 2, out_specs=_row(tm, COL),
        compiler_params=_params("parallel"),
    )(do, o)


def _attn_dq(g, qkn, proj, do, lse, delta, dqkn):
    s = qkn.shape[0]
    d, tb, sb, pb = _attn_shapes(s, g)
    nj = tb // d // sb
    scale = HD ** -0.5
    chained = dqkn is not None

    def body(q_ref, kc_ref, kp_ref, vc_ref, vp_ref, do_ref, lse_ref, del_ref, *rest):
        dq_ref, qf, kf, vf = rest[-4:]
        n = pl.program_id(1)
        qf[...] = q_ref[...].astype(F32)
        kf[0:pb] = kp_ref[...].astype(F32)
        kf[pb:] = kc_ref[...].astype(F32)
        vf[0:pb] = vp_ref[...].astype(F32)
        vf[pb:] = vc_ref[...].astype(F32)
        for r in range(d):
            for j in range(nj):
                at = j * sb * d + r
                rows = _every(at, sb, d)
                q = qf[rows, :].astype(BF16)
                k = kf[_every(at, sb + BAND, d), :].astype(BF16)
                v = vf[_every(at, sb + BAND, d), :].astype(BF16)
                sc = lax.dot_general(q, k, NT_DIMS, preferred_element_type=F32) * scale
                qi = lax.broadcasted_iota(jnp.int32, sc.shape, 0)
                kj = lax.broadcasted_iota(jnp.int32, sc.shape, 1)
                valid = (kj >= qi) & (kj <= qi + BAND)
                if j == 0:
                    valid = valid & ((kj >= BAND) | (n > 0))
                p = jnp.exp(jnp.where(valid, sc - _lanes(lse_ref[rows, :], sb + BAND), -1e30))
                dp = lax.dot_general(do_ref[rows, :].astype(BF16), v, NT_DIMS, preferred_element_type=F32)
                ds = p * (dp - _lanes(del_ref[rows, :], sb + BAND)) * scale
                dq_ref[rows, :] = lax.dot_general(ds.astype(BF16), k, NN_DIMS, preferred_element_type=F32)

    cur, side, tok, _ = _attn_specs(g, tb, pb, s, ahead=False)
    args = [qkn, qkn, qkn, proj, proj, do, lse, delta]
    specs = [cur(Q_COL), cur(K_COL), side(K_COL), cur(V_COL), side(V_COL), tok, tok, tok]
    if chained:
        args.append(dqkn)
        specs.append(pl.BlockSpec(memory_space=pl.ANY))
    return pl.pallas_call(
        body, name=f"attn_dq_g{g}", grid=(N_HEADS, s // tb),
        out_shape=jax.ShapeDtypeStruct((s, QKW), F32),
        in_specs=specs, out_specs=cur(Q_COL),
        input_output_aliases={8: 0} if chained else {},
        scratch_shapes=[pltpu.VMEM((tb, HD), F32), pltpu.VMEM((tb + pb, HD), F32),
                        pltpu.VMEM((tb + pb, HD), F32)],
        compiler_params=_params("arbitrary", "arbitrary"),
    )(*args)


def _attn_dkv(g, qkn, proj, do, lse, delta, dqkn, dproj):
    s = qkn.shape[0]
    d, tb, sb, pb = _attn_shapes(s, g)
    nj = tb // d // sb
    nt = s // tb
    scale = HD ** -0.5

    def body(k_ref, v_ref, qc_ref, qn_ref, doc_ref, don_ref, lc_ref, ln_ref, dc_ref, dn_ref, _a, _b,
             dk_ref, dv_ref, kf, vf, qf, dvf):
        n = pl.program_id(1)
        kf[...] = k_ref[...].astype(F32)
        vf[...] = v_ref[...].astype(F32)
        qf[0:tb] = qc_ref[...].astype(F32)
        qf[tb:] = qn_ref[...].astype(F32)

        def window(c_ref, n_ref, r, j):
            at = j * sb * d + r
            if j < nj - 1:
                return c_ref[_every(at, sb + BAND, d), :]
            return jnp.concatenate([c_ref[_every(at, sb, d), :], n_ref[_every(r, BAND, d), :]], axis=0)

        for r in range(d):
            for j in range(nj):
                at = j * sb * d + r
                rows = _every(at, sb, d)
                k = kf[rows, :].astype(BF16)
                v = vf[rows, :].astype(BF16)
                q = qf[_every(at, sb + BAND, d), :].astype(BF16)
                dov = window(doc_ref, don_ref, r, j).astype(BF16)
                sc = lax.dot_general(q, k, NT_DIMS, preferred_element_type=F32) * scale
                qi = lax.broadcasted_iota(jnp.int32, sc.shape, 0)
                kj = lax.broadcasted_iota(jnp.int32, sc.shape, 1)
                valid = (qi >= kj) & (qi <= kj + BAND)
                if j == nj - 1:
                    valid = valid & ((qi < sb) | (n < nt - 1))
                p = jnp.exp(jnp.where(valid, sc - _lanes(window(lc_ref, ln_ref, r, j), sb), -1e30))
                dp = lax.dot_general(dov, v, NT_DIMS, preferred_element_type=F32)
                ds = p * (dp - _lanes(window(dc_ref, dn_ref, r, j), sb)) * scale
                dvf[rows, :] = lax.dot_general(p.astype(BF16), dov, TN_DIMS, preferred_element_type=F32)
                dk_ref[rows, :] = lax.dot_general(ds.astype(BF16), q, TN_DIMS, preferred_element_type=F32)
        dv_ref[...] = dvf[...].astype(BF16)

    cur, side, tok, tok_side = _attn_specs(g, tb, pb, s, ahead=True)
    anyspec = pl.BlockSpec(memory_space=pl.ANY)
    return pl.pallas_call(
        body, name=f"attn_dkv_g{g}", grid=(N_HEADS, nt),
        out_shape=[jax.ShapeDtypeStruct((s, QKW), F32), jax.ShapeDtypeStruct((s, IN_W), BF16)],
        in_specs=[cur(K_COL), cur(V_COL), cur(Q_COL), side(Q_COL), tok, tok_side, tok, tok_side,
                  tok, tok_side, anyspec, anyspec],
        out_specs=[cur(K_COL), cur(V_COL)],
        input_output_aliases={10: 0, 11: 1},
        scratch_shapes=[pltpu.VMEM((tb, HD), F32), pltpu.VMEM((tb, HD), F32),
                        pltpu.VMEM((tb + pb, HD), F32), pltpu.VMEM((tb, HD), F32)],
        compiler_params=_params("arbitrary", "arbitrary"),
    )(qkn, proj, qkn, qkn, do, do, lse, lse, delta, delta, dqkn, dproj)


def _shift_down(x, before, k):
    rolled = pltpu.roll(x, k, 0)
    head = jnp.where(lax.broadcasted_iota(jnp.int32, before.shape, 0) < k, pltpu.roll(before, k, 0), rolled[:8])
    return jnp.concatenate([head, rolled[8:]], axis=0)


def _shift_up(x, after, k):
    rows = x.shape[0]
    rolled = pltpu.roll(x, rows - k, 0)
    tail = jnp.where(lax.broadcasted_iota(jnp.int32, after.shape, 0) >= 8 - k,
                     pltpu.roll(after, 8 - k, 0), rolled[rows - 8:])
    return jnp.concatenate([rolled[:rows - 8], tail], axis=0)


def _conv_fwd(proj, cw, tm=512):
    s = proj.shape[0]
    r16 = tm // 16

    def body(u_ref, b_ref, c_ref, up_ref, cp_ref, w_ref, z_ref):
        i = pl.program_id(1)
        xc = c_ref[...].astype(F32) * u_ref[...].astype(F32)
        xp = jnp.where(i > 0, cp_ref[8:16, :].astype(F32) * up_ref[8:16, :].astype(F32), 0.0)
        w = w_ref[...]
        conv = _shift_down(xc, xp, 2) * w[0:1] + _shift_down(xc, xp, 1) * w[1:2] + xc * w[2:3]
        z_ref[...] = (b_ref[...].astype(F32) * conv).astype(BF16)

    tile = lambda blk: pl.BlockSpec((tm, COL), lambda j, i: (i, blk + j))
    before = lambda blk: pl.BlockSpec((16, COL), lambda j, i: (jnp.maximum(i * r16 - 1, 0), blk + j))
    return pl.pallas_call(
        body, name="conv_fwd", grid=(D // COL, s // tm),
        out_shape=jax.ShapeDtypeStruct((s, D), BF16),
        in_specs=[tile(U_BLK), tile(B_BLK), tile(C_BLK), before(U_BLK), before(C_BLK),
                  pl.BlockSpec((3, COL), lambda j, i: (0, j))],
        out_specs=pl.BlockSpec((tm, COL), lambda j, i: (i, j)),
        compiler_params=_params("parallel", "parallel"),
    )(proj, proj, proj, proj, proj, cw)


def _conv_bwd(dz, proj, cw, dproj, tm=512):
    s = proj.shape[0]
    r8, r16 = tm // 8, tm // 16
    nrow = s // tm

    def body(dz_ref, u_ref, b_ref, c_ref, up_ref, cp_ref, dzn_ref, bn_ref, w_ref, _, o_ref, acc_ref):
        piece, i = pl.program_id(1), pl.program_id(2)
        u, c = u_ref[...].astype(F32), c_ref[...].astype(F32)
        bv = b_ref[...].astype(F32)
        dzv = dz_ref[...]
        w = w_ref[...]

        @pl.when((piece == 0) & (i == 0))
        def _():
            acc_ref[...] = jnp.zeros_like(acc_ref)

        @pl.when(piece == 1)
        def _():
            xc = c * u
            xp = jnp.where(i > 0, cp_ref[8:16, :].astype(F32) * up_ref[8:16, :].astype(F32), 0.0)
            x2, x1 = _shift_down(xc, xp, 2), _shift_down(xc, xp, 1)
            o_ref[...] = (dzv * (x2 * w[0:1] + x1 * w[1:2] + xc * w[2:3])).astype(BF16)
            dconv = dzv * bv
            acc_ref[0:1, :] += jnp.sum(dconv * x2, axis=0, keepdims=True)
            acc_ref[1:2, :] += jnp.sum(dconv * x1, axis=0, keepdims=True)
            acc_ref[2:3, :] += jnp.sum(dconv * xc, axis=0, keepdims=True)

        @pl.when(piece != 1)
        def _():
            dconv = dzv * bv
            dn = jnp.where(i < nrow - 1, dzn_ref[...] * bn_ref[0:8, :].astype(F32), 0.0)
            dxc = dconv * w[2:3] + _shift_up(dconv, dn, 1) * w[1:2] + _shift_up(dconv, dn, 2) * w[0:1]
            o_ref[...] = (dxc * jnp.where(piece == 0, c, u)).astype(BF16)

    tile = lambda blk: pl.BlockSpec((tm, COL), lambda j, p, i: (i, blk + j))
    before = lambda blk: pl.BlockSpec((16, COL), lambda j, p, i: (jnp.maximum(i * r16 - 1, 0), blk + j))
    after = lambda rows, blk: pl.BlockSpec(
        (rows, COL), lambda j, p, i: (jnp.minimum((i + 1) * (tm // rows), s // rows - 1), blk + j))
    return pl.pallas_call(
        body, name="conv_bwd", grid=(D // COL, 3, nrow),
        out_shape=[jax.ShapeDtypeStruct((s, IN_W), BF16), jax.ShapeDtypeStruct((8, D), F32)],
        in_specs=[tile(0), tile(U_BLK), tile(B_BLK), tile(C_BLK), before(U_BLK), before(C_BLK),
                  after(8, 0), after(16, B_BLK), pl.BlockSpec((3, COL), lambda j, p, i: (0, j)),
                  pl.BlockSpec(memory_space=pl.ANY)],
        out_specs=[pl.BlockSpec((tm, COL), lambda j, p, i: (i, U_BLK + 2 * p + j)),
                   pl.BlockSpec((8, COL), lambda j, p, i: (0, j))],
        input_output_aliases={9: 0},
        compiler_params=_params("arbitrary", "arbitrary", "arbitrary"),
    )(dz, proj, proj, proj, proj, proj, dz, proj, cw, dproj)


def _merge_fwd(ya, yc, proj, tm=512):
    s = proj.shape[0]

    def body(ya_ref, yc_ref, ga_ref, gc_ref, o_ref):
        o_ref[...] = (_sigmoid(ga_ref[...].astype(F32)) * ya_ref[...].astype(F32)
                      + _sigmoid(gc_ref[...].astype(F32)) * yc_ref[...].astype(F32)).astype(BF16)

    tile = lambda blk: pl.BlockSpec((tm, COL), lambda j, i: (i, blk + j))
    return pl.pallas_call(
        body, name="merge_fwd", grid=(D // COL, s // tm),
        out_shape=jax.ShapeDtypeStruct((s, D), BF16),
        in_specs=[tile(0), tile(0), tile(GA_BLK), tile(GC_BLK)], out_specs=tile(0),
        compiler_params=_params("parallel", "parallel"),
    )(ya, yc, proj, proj)


def _merge_bwd_branches(dm, proj, tm=512):
    s = proj.shape[0]

    def body(dm_ref, ga_ref, gc_ref, dya_ref, dyc_ref):
        dmv = dm_ref[...]
        dya_ref[...] = (dmv * _sigmoid(ga_ref[...].astype(F32))).astype(BF16)
        dyc_ref[...] = (dmv * _sigmoid(gc_ref[...].astype(F32))).astype(BF16)

    tile = lambda blk: pl.BlockSpec((tm, COL), lambda j, i: (i, blk + j))
    return pl.pallas_call(
        body, name="merge_bwd_branches", grid=(D // COL, s // tm),
        out_shape=[jax.ShapeDtypeStruct((s, D), BF16)] * 2,
        in_specs=[tile(0), tile(GA_BLK), tile(GC_BLK)], out_specs=[tile(0)] * 2,
        compiler_params=_params("parallel", "parallel"),
    )(dm, proj, proj)


def _merge_bwd_gates(dm, ya, yc, proj, tm=512):
    s = proj.shape[0]
    half = D // COL

    def body(dm_ref, ya_ref, yc_ref, g_ref, o_ref):
        y = jnp.where(pl.program_id(0) < half, ya_ref[...].astype(F32), yc_ref[...].astype(F32))
        sig = _sigmoid(g_ref[...].astype(F32))
        o_ref[...] = (dm_ref[...] * y * sig * (1.0 - sig)).astype(BF16)

    chan = pl.BlockSpec((tm, COL), lambda jj, i: (i, jj % half))
    gate = pl.BlockSpec((tm, COL), lambda jj, i: (i, GA_BLK + jj))
    return pl.pallas_call(
        body, name="merge_bwd_gates", grid=(2 * half, s // tm),
        out_shape=jax.ShapeDtypeStruct((s, IN_W), BF16),
        in_specs=[chan, chan, chan, gate], out_specs=gate,
        compiler_params=_params("parallel", "parallel"),
    )(dm, ya, yc, proj)


def _mod_part(c_all, w_ada, b_part):
    def body(c_ref, w_ref, b_ref, o_ref):
        cv = c_ref[...]
        act = cv * _sigmoid(cv)
        o_ref[...] = jnp.dot(act, w_ref[...], preferred_element_type=F32,
                             precision=lax.Precision.HIGHEST) + b_ref[...]

    return pl.pallas_call(
        body, name="mod_part", out_shape=jax.ShapeDtypeStruct((N_DEV, w_ada.shape[1]), F32),
    )(c_all, w_ada, b_part)


def _w_ada_grad(c_all_t, dmod_part):
    def body(c_ref, d_ref, o_ref):
        cv = c_ref[...]
        act = cv * _sigmoid(cv)
        dv = d_ref[...]
        acc = act[:, 0:1] * dv[0:1, :]
        for b in range(1, N_DEV):
            acc = acc + act[:, b:b + 1] * dv[b:b + 1, :]
        o_ref[...] = acc

    return pl.pallas_call(
        body, name="w_ada_grad", out_shape=jax.ShapeDtypeStruct((D, dmod_part.shape[1]), F32),
    )(c_all_t, dmod_part)


def _sum_rows(name, v):
    def body(v_ref, o_ref):
        acc = v_ref[0]
        for k in range(1, N_DEV):
            acc = acc + v_ref[k]
        o_ref[...] = acc

    return pl.pallas_call(body, name=name, out_shape=jax.ShapeDtypeStruct(v.shape[1:], F32))(v)


def _adamw(name, w, g, m, v):
    rows, cols = w.shape
    tr = 256 if rows % 256 == 0 and rows * cols > 512 * 1024 else rows
    c1 = 1.0 - ADAM_B1 ** ADAM_STEP
    c2 = 1.0 - ADAM_B2 ** ADAM_STEP

    def body(w_ref, g_ref, m_ref, v_ref, d_ref, nm_ref, nv_ref):
        gv = g_ref[...]
        nm = ADAM_B1 * m_ref[...] + (1.0 - ADAM_B1) * gv
        nv = ADAM_B2 * v_ref[...] + (1.0 - ADAM_B2) * (gv * gv)
        nm_ref[...] = nm
        nv_ref[...] = nv
        d_ref[...] = -ADAM_LR * ((nm / c1) / (jnp.sqrt(nv / c2) + ADAM_EPS) + ADAM_WD * w_ref[...])

    spec = pl.BlockSpec((tr, cols), lambda i: (i, 0))
    return pl.pallas_call(
        body, name=name, grid=(rows // tr,),
        out_shape=[jax.ShapeDtypeStruct((rows, cols), F32)] * 3,
        in_specs=[spec] * 4, out_specs=[spec] * 3,
        compiler_params=_params("parallel"),
    )(w, g, m, v)


HALF = FF // 2


def _sds(shape, dtype):
    return jax.ShapeDtypeStruct(shape, dtype)


def _row_tile(w):
    return lambda tm: ((tm, w), lambda i, j: (i, 0))


def _one(w):
    return lambda rows: ((rows, w), lambda i, j: (0, 0))


def _gate_up_swiglu(name, h, wgu, carry=None, tm=512):
    s = h.shape[0]
    tm = min(tm, s)

    def epilogue(prod, first, tin, tout):
        ab_ref, s_ref = tout
        ab_ref[...] = prod.astype(BF16)
        a, b = prod[:, :HALF], prod[:, HALF:]
        s_ref[...] = (a * _sigmoid(a) * b).astype(BF16)

    return _mm(name, h, wgu, "NT", None, tm, FF, D, carry=carry, n_outer=True, epilogue=epilogue, sub=4,
               tiles_out=[(_sds((s, 2 * FF), BF16), (tm, FF), lambda i, j: (i, j)),
                          (_sds((s, FF), BF16), (tm, HALF), lambda i, j: (i, j))])


def _d_hidden_swiglu(name, df, wd, ab, tm=512):
    s = df.shape[0]
    tm = min(tm, s)

    def epilogue(prod, first, tin, tout):
        a = tin[0][:, :HALF].astype(F32)
        b = tin[0][:, HALF:].astype(F32)
        sig = _sigmoid(a)
        tout[0][:, :HALF] = (prod * b * (sig * (1.0 + a * (1.0 - sig)))).astype(BF16)
        tout[0][:, HALF:] = (prod * (a * sig)).astype(BF16)

    return _mm(name, df, wd, "NT", None, tm, HALF, D, n_outer=True, epilogue=epilogue, sub=4,
               tiles_in=[(ab, (tm, FF), lambda i, j: (i, j))],
               tiles_out=[(_sds((s, 2 * FF), BF16), (tm, FF), lambda i, j: (i, j))])[0]


def _out_residual(name, a, w, x, gt, coef, nxt, tm=512, tk=FF):
    s = a.shape[0]
    tm = min(tm, s)

    def epilogue(prod, first, tin, tout):
        x_ref, gt_ref, g_ref, sc_ref, sh_ref = tin
        f_ref, xn_ref, h_ref = tout
        f_ref[...] = prod
        xn = x_ref[...] + (coef * gt_ref[...]) * prod
        xn_ref[...] = xn
        r = lax.rsqrt(jnp.mean(xn * xn, axis=-1, keepdims=True) + EPS)
        h_ref[...] = ((xn * r) * g_ref[...] * (1.0 + sc_ref[...]) + sh_ref[...]).astype(BF16)

    row, vec = _row_tile(D)(tm), _one(D)(1)
    return _mm(name, a, w, "NN", None, tm, D, tk, epilogue=epilogue, sub=4,
               tiles_in=[(x, *row), (gt, *vec)] + [(v, *vec) for v in nxt],
               tiles_out=[(_sds((s, D), F32), *row), (_sds((s, D), F32), *row), (_sds((s, D), BF16), *row)])


def _out_loss(name, a, w, x, gt, coef, target, tm=512):
    s = a.shape[0]
    tm = min(tm, s)

    def epilogue(prod, first, tin, tout):
        x_ref, gt_ref, t_ref = tin
        f_ref, g_ref, df_ref, acc_ref = tout
        f_ref[...] = prod
        cg = coef * gt_ref[...]
        e = x_ref[...] + cg * prod - t_ref[...]
        gv = e * (1.0 / D)
        g_ref[...] = gv
        df_ref[...] = (cg * gv).astype(BF16)

        @pl.when(first)
        def _():
            acc_ref[...] = jnp.zeros_like(acc_ref)

        acc_ref[0:1, :] += coef * jnp.sum(gv * prod, axis=0, keepdims=True)
        acc_ref[1:2, :] += (0.5 / D) * jnp.sum(e * e, axis=0, keepdims=True)

    row, vec = _row_tile(D)(tm), _one(D)(1)
    return _mm(name, a, w, "NN", None, tm, D, FF, epilogue=epilogue, sub=4,
               tiles_in=[(x, *row), (gt, *vec), (target, *row)],
               tiles_out=[(_sds((s, D), F32), *row), (_sds((s, D), F32), *row), (_sds((s, D), BF16), *row),
                          (_sds((8, D), F32), *_one(D)(8))])


def _d_h_norm_bwd(name, da, w, x, gin, g, sc, sh, before=None, carry=None, tm=256):
    s = da.shape[0]
    tm = min(tm, s)
    coef = before[2] if before else None

    def epilogue(prod, first, tin, tout):
        x_ref, gin_ref, g_ref, sc_ref, sh_ref = tin[:5]
        gout_ref, acc_ref = tout[:2]
        xv = x_ref[...]
        r = lax.rsqrt(jnp.mean(xv * xv, axis=-1, keepdims=True) + EPS)
        nv = xv * r
        gv, one_sc = g_ref[...], 1.0 + sc_ref[...]
        dn = prod * gv * one_sc
        gout = gin_ref[...] + r * (dn - nv * jnp.mean(dn * nv, axis=-1, keepdims=True))
        gout_ref[...] = gout

        @pl.when(first)
        def _():
            acc_ref[...] = jnp.zeros_like(acc_ref)

        dhn = prod * nv
        acc_ref[0:1, :] += jnp.sum(prod, axis=0, keepdims=True)
        acc_ref[1:2, :] += jnp.sum(dhn * gv, axis=0, keepdims=True)
        acc_ref[2:3, :] += jnp.sum(dhn * one_sc, axis=0, keepdims=True)
        if before:
            f_ref, gt_ref = tin[5:]
            tout[2][...] = ((coef * gt_ref[...]) * gout).astype(BF16)
            acc_ref[3:4, :] += coef * jnp.sum(gout * f_ref[...], axis=0, keepdims=True)

    row, vec = _row_tile(D)(tm), _one(D)(1)
    tiles_in = [(x, *row), (gin, *row), (g, *vec), (sc, *vec), (sh, *vec)]
    tiles_out = [(_sds((s, D), F32), *row), (_sds((8, D), F32), *_one(D)(8))]
    if before:
        tiles_in += [(before[0], *row), (before[1], *vec)]
        tiles_out.append((_sds((s, D), BF16), *row))
    return _mm(name, da, w, "NN", None, tm, D, da.shape[1], epilogue=epilogue, carry=carry, keep_b=True, sub=2,
               tiles_in=tiles_in, tiles_out=tiles_out)


def _ffn_bwd(tag, df, x, gin, h, ab, sw, g, sc, sh, wgu, wd, before=None, carry_down=None, carry_gate_up=None,
             tk_dw=2048):
    dab = _d_hidden_swiglu(f"{tag}_d_hidden", df, wd, ab)
    dwd = _mm(f"{tag}_dw_down", sw, df, "TN", BF16, HALF, D, tk_dw)
    carried = []
    if carry_down:
        dwgu, *got = _mm(f"{tag}_dw_gate_up", dab, h, "TN", BF16, HALF, D, tk_dw, carry=carry_down(dwd))
        carried += got
    else:
        dwgu = _mm(f"{tag}_dw_gate_up", dab, h, "TN", BF16, HALF, D, tk_dw)
    res = _d_h_norm_bwd(f"{tag}_d_h", dab, wgu, x, gin, g, sc, sh, before=before,
                        carry=carry_gate_up(dwgu) if carry_gate_up else None)
    n_own = 3 if before else 2
    return res[:n_own], dwgu, dwd, carried + list(res[n_own:])


def kernel(x, c, w_ada, b_ada, norm_ffn1, ffn1_w_gate, ffn1_w_up, ffn1_w_down, norm_mix, w_in, q_norm, k_norm, conv_w, w_attn_branch, w_conv_branch, w_out, norm_ffn2, ffn2_w_gate, ffn2_w_up, ffn2_w_down, loss_target, m_w_ada, m_b_ada, m_norm_ffn1, m_ffn1_w_gate, m_ffn1_w_up, m_ffn1_w_down, m_norm_mix, m_w_in, m_q_norm, m_k_norm, m_conv_w, m_w_attn_branch, m_w_conv_branch, m_w_out, m_norm_ffn2, m_ffn2_w_gate, m_ffn2_w_up, m_ffn2_w_down, v_w_ada, v_b_ada, v_norm_ffn1, v_ffn1_w_gate, v_ffn1_w_up, v_ffn1_w_down, v_norm_mix, v_w_in, v_q_norm, v_k_norm, v_conv_w, v_w_attn_branch, v_w_conv_branch, v_w_out, v_norm_ffn2, v_ffn2_w_gate, v_ffn2_w_up, v_ffn2_w_down):
    me = 4 * lax.axis_index("x") + 2 * lax.axis_index("y") + lax.axis_index("c")
    x0, target = x[0], loss_target[0]
    s = x0.shape[0]
    ada_cols = w_ada.shape[2]
    cw_cols = conv_w.shape[2]

    gathered = _small_allgather(
        "gather_c_conv", jnp.concatenate([c, conv_w[0].reshape(1, 3 * cw_cols)], axis=1))[:, 0]
    c_all = gathered[:, :D]
    cw = gathered[:, D:].reshape(N_DEV, 3, cw_cols).transpose(1, 0, 2).reshape(3, D)
    b_part = lax.dynamic_slice(b_ada, (0, me * ada_cols), (1, ada_cols))
    mod_part = _mod_part(c_all, w_ada[0], b_part)
    mod_all = _small_allgather("gather_mod", mod_part.reshape(1, N_DEV * ada_cols))
    mod = lax.dynamic_slice(mod_all.reshape(N_DEV, N_DEV, ada_cols), (0, me, 0), (N_DEV, 1, ada_cols))
    mod = mod.reshape(N_MOD, 1, D)
    sh1, sc1, gt1, sh2, sc2, gt2, sh3, sc3, gt3 = [mod[i] for i in range(N_MOD)]

    tb = lambda w: w[0].T.astype(BF16)
    nb = lambda w: w[0].astype(BF16)
    ffn1_shards = [tb(ffn1_w_gate), tb(ffn1_w_up), nb(ffn1_w_down)]
    ffn2_shards = [tb(ffn2_w_gate), tb(ffn2_w_up), nb(ffn2_w_down)]
    mix_shards = [tb(w_in), tb(w_attn_branch), nb(w_conv_branch), nb(w_out)]
    ffn_dst, ffn_base, ffn_jump, ffn_shapes = [0, 0, 1], [0, HALF, 0], [HALF, HALF, 0], [(2 * FF, D), (FF, D)]
    mix_dst, mix_base, mix_shapes = [0, 1, 2, 3], [0, 0, 0, 0], [(IN_W, D), (D, COL), (D, D), (D, D)]
    wgu1, wd1 = _run_plan("gather_ffn1_weights",
                          _gather_plan(ffn1_shards, ffn_dst, ffn_base, ffn_shapes, ffn_jump))

    h1 = _normmod("ffn1_normmod", x0, norm_ffn1, sc1, sh1)
    ab1, s1, win_t = _gate_up_swiglu(
        "ffn1_gate_up", h1, wgu1, carry=_gather_plan(mix_shards[:1], mix_dst[:1], mix_base[:1], mix_shapes[:1]))
    f1, x1, h2 = _out_residual("ffn1_down", s1, wd1, x0, gt1, 0.5, (norm_mix, sc2, sh2))
    proj, wgu2, wd2, wa_t, wc, wo = _mm(
        "mix_in_proj", h2, win_t, "NT", BF16, 512, IN_W // 4, D, n_outer=True,
        carry=_gather_plan(ffn2_shards + mix_shards[1:], ffn_dst + [2, 3, 4], ffn_base + [0, 0, 0],
                           ffn_shapes + mix_shapes[1:], ffn_jump + [0, 0, 0]))
    wqk = jnp.concatenate([jnp.tile(q_norm, (1, 12)), jnp.tile(k_norm, (1, 12))], axis=1)
    qkn = _qknorm(proj, wqk)
    group_out = [_attn_fwd(g, qkn, proj) for g in range(3)]
    o, lse = _attn_combine([go[0] for go in group_out], [go[1] for go in group_out])
    ya = _mm("mix_attn_branch", o, wa_t, "NT", BF16, 1024, 1024, COL)
    z = _conv_fwd(proj, cw)
    yc = _mm("mix_conv_branch", z, wc, "NN", BF16, 1024, 1024, D)
    merged = _merge_fwd(ya, yc, proj)
    mix, x2, h3 = _out_residual("mix_out_proj", merged, wo, x1, gt2, 1.0, (norm_ffn2, sc3, sh3), tk=D)
    ab3, s3 = _gate_up_swiglu("ffn2_gate_up", h3, wgu2)
    f3, g3, df3, acc_out = _out_loss("ffn2_down", s3, wd2, x2, gt3, 0.5, target)
    loss = lax.psum(jnp.sum(acc_out[1]), ("x", "y", "c"))

    ffn_rows = [sh_.shape[0] for sh_ in ffn1_shards]
    mix_rows = [sh_.shape[0] for sh_ in mix_shards]
    (g2, acc3, dmix), dwgu2, dwd2, _ = _ffn_bwd(
        "ffn2", df3, x2, g3, h3, ab3, s3, norm_ffn2, sc3, sh3, wgu2, wd2, before=(mix, gt2, 1.0))
    dmerged = _mm("mix_d_merged", dmix, wo, "NT", F32, 1024, 1024, D)
    dwo = _mm("mix_dw_out", merged, dmix, "TN", BF16, 1024, 1024, 2048)
    dya, dyc = _merge_bwd_branches(dmerged, proj)
    dproj = _merge_bwd_gates(dmerged, ya, yc, proj)
    dwc = _mm("mix_dw_conv_branch", z, dyc, "TN", BF16, 1024, 1024, 2048)
    dz = _mm("mix_d_z", dyc, wc, "NT", F32, 1024, 1024, D)
    dproj, cw_acc = _conv_bwd(dz, proj, cw, dproj)
    dwa_t = _mm("mix_dw_attn_branch", dya, o, "TN", BF16, 1024, COL, 2048)
    do = _mm("mix_d_o", dya, wa_t, "NN", F32, 1024, COL, D)
    delta = _attn_delta(do, o)
    dqkn = None
    for g in range(3):
        dqkn = _attn_dq(g, qkn, proj, do, lse, delta, dqkn)
    for g in range(3):
        dqkn, dproj = _attn_dkv(g, qkn, proj, do, lse, delta, dqkn, dproj)
    dproj, wqk_acc = _qknorm_bwd(proj, dqkn, wqk, dproj)
    dwin_t, r_f2g, r_f2u, r_f2d, r_wa, r_wc, r_wo = _mm(
        "mix_dw_in", dproj, h2, "TN", BF16, IN_W // 4, D, 1024,
        carry=_scatter_plan([dwgu2, dwd2, dwa_t, dwc, dwo], [0, 0, 1, 2, 3, 4], [0, HALF, 0, 0, 0, 0],
                            ffn_rows + mix_rows[1:], [D, D, D, COL, D, D], [HALF, HALF, 0, 0, 0, 0]))
    g1, acc2, df1, r_win = _d_h_norm_bwd(
        "mix_d_h", dproj, win_t, x1, g2, norm_mix, sc2, sh2, before=(f1, gt1, 0.5),
        carry=_scatter_plan([dwin_t], [0], [0], mix_rows[:1], [D]))
    (g0, acc1), dwgu1, dwd1, (r_f1d, r_f1g, r_f1u) = _ffn_bwd(
        "ffn1", df1, x0, g1, h1, ab1, s1, norm_ffn1, sc1, sh1, wgu1, wd1,
        carry_down=lambda dwd: _scatter_plan([dwd], [0], [0], ffn_rows[2:], [D]),
        carry_gate_up=lambda dwgu: _scatter_plan([dwgu], [0, 0], [0, HALF], ffn_rows[:2], [D, D], [HALF, HALF]))

    dqw = jnp.sum(wqk_acc[0, :QKW // 2].reshape(12, HD), axis=0)
    dkw = jnp.sum(wqk_acc[0, QKW // 2:].reshape(12, HD), axis=0)
    small = jnp.concatenate([
        acc1[0], acc1[1], acc2[3], acc2[0], acc2[1], acc3[3], acc3[0], acc3[1], acc_out[0],
        acc1[2], acc2[2], acc3[2], dqw, dkw, cw_acc[0:3].reshape(3 * D)]).reshape(1, -1)
    small_all = _small_allgather("gather_small_grads", small)
    small_sum = _sum_rows("sum_small_grads", small_all)[0]
    n_mod = N_MOD * D
    g_b_ada = small_sum[:n_mod].reshape(1, n_mod)
    g_norm1, g_norm2, g_norm3 = [small_sum[n_mod + i * D:n_mod + (i + 1) * D].reshape(1, D) for i in range(3)]
    off = n_mod + 3 * D
    g_qn, g_kn = small_sum[off:off + HD].reshape(1, HD), small_sum[off + HD:off + 2 * HD].reshape(1, HD)
    g_cw_full = small_sum[off + 2 * HD:].reshape(3, D)
    g_cw = lax.dynamic_slice(g_cw_full, (0, me * cw_cols), (3, cw_cols))
    dmod_part = lax.dynamic_slice(small_all[:, 0, :n_mod], (0, me * ada_cols), (N_DEV, ada_cols))
    g_w_ada = _w_ada_grad(c_all.T, dmod_part)

    recvs = [r_f1g, r_f1u, r_f1d, r_f2g, r_f2u, r_f2d, r_win, r_wa, r_wc, r_wo]
    names = ["ffn1_gate", "ffn1_up", "ffn1_down", "ffn2_gate", "ffn2_up", "ffn2_down",
             "w_in", "attn_branch", "conv_branch", "w_out"]
    sums = [_sum_contributions(f"sum_{nm}", r) for nm, r in zip(names, recvs)]
    transposed = [True, True, False, True, True, False, True, True, False, False]
    gw = [sm.T if t else sm for sm, t in zip(sums, transposed)]
    g_f1g, g_f1u, g_f1d, g_f2g, g_f2u, g_f2d, g_win, g_wa, g_wc, g_wo = gw

    grad_list = [g_w_ada[None], g_b_ada, g_norm1, g_f1g[None], g_f1u[None], g_f1d[None], g_norm2, g_win[None],
                 g_qn, g_kn, g_cw[None], g_wa[None], g_wc[None], g_wo[None], g_norm3,
                 g_f2g[None], g_f2u[None], g_f2d[None]]
    weights = [w_ada, b_ada, norm_ffn1, ffn1_w_gate, ffn1_w_up, ffn1_w_down, norm_mix, w_in, q_norm, k_norm,
               conv_w, w_attn_branch, w_conv_branch, w_out, norm_ffn2, ffn2_w_gate, ffn2_w_up, ffn2_w_down]
    ms = [m_w_ada, m_b_ada, m_norm_ffn1, m_ffn1_w_gate, m_ffn1_w_up, m_ffn1_w_down, m_norm_mix, m_w_in, m_q_norm,
          m_k_norm, m_conv_w, m_w_attn_branch, m_w_conv_branch, m_w_out, m_norm_ffn2, m_ffn2_w_gate,
          m_ffn2_w_up, m_ffn2_w_down]
    vs = [v_w_ada, v_b_ada, v_norm_ffn1, v_ffn1_w_gate, v_ffn1_w_up, v_ffn1_w_down, v_norm_mix, v_w_in, v_q_norm,
          v_k_norm, v_conv_w, v_w_attn_branch, v_w_conv_branch, v_w_out, v_norm_ffn2, v_ffn2_w_gate,
          v_ffn2_w_up, v_ffn2_w_down]
    wnames = ["w_ada", "b_ada", "norm_ffn1", "ffn1_w_gate", "ffn1_w_up", "ffn1_w_down", "norm_mix", "w_in",
              "q_norm", "k_norm", "conv_w", "w_attn_branch", "w_conv_branch", "w_out", "norm_ffn2",
              "ffn2_w_gate", "ffn2_w_up", "ffn2_w_down"]
    deltas, new_ms, new_vs = [], [], []
    for nm, w, gr, m_, v_ in zip(wnames, weights, grad_list, ms, vs):
        two_d = (-1, w.shape[-1])
        dl, nm_, nv_ = _adamw(f"adamw_{nm}", w.reshape(two_d), gr.reshape(two_d), m_.reshape(two_d), v_.reshape(two_d))
        deltas.append(dl.reshape(w.shape))
        new_ms.append(nm_.reshape(w.shape))
        new_vs.append(nv_.reshape(w.shape))
    grad_out = [gr.reshape(w.shape) for gr, w in zip(grad_list, weights)]
    return (loss, g0[None], *grad_out, *deltas, *new_ms, *new_vs)
```

```python
import functools

import jax
import jax.numpy as jnp
from jax import lax
from jax.experimental import pallas as pl
from jax.experimental.pallas import tpu as pltpu

F32 = jnp.float32
BF16 = jnp.bfloat16
MESH = pl.DeviceIdType.MESH

N_DEV = 8
D = 1024
FF = 2816
HD = 128
N_HEADS = 4
DILATIONS = (1, 4, 16)
BAND = 128
QKW = 2 * 3 * N_HEADS * HD
IN_W = 9728
COL = 512
V_BLK, U_BLK, B_BLK, C_BLK, GA_BLK, GC_BLK = 6, 9, 11, 13, 15, 17
EPS = 1e-6
N_MOD = 9
ADAM_LR, ADAM_B1, ADAM_B2, ADAM_EPS, ADAM_WD, ADAM_STEP = 0.001, 0.9, 0.999, 1e-08, 0.01, 10

NT_DIMS = (((1,), (1,)), ((), ()))
TN_DIMS = (((0,), (0,)), ((), ()))
NN_DIMS = (((1,), (0,)), ((), ()))


def _place():
    return lax.axis_index("x"), lax.axis_index("y"), lax.axis_index("c")


def _flip(coord, bit):
    return 1 - coord if bit else coord


def _params(*sem):
    return pltpu.CompilerParams(dimension_semantics=sem)


def _small_allgather(name, v):
    n = v.shape[-1]

    def body(v_ref, out_ref, send_sems, recv_sems):
        x, y, c = _place()
        me = 4 * x + 2 * y + c
        out_ref[me] = v_ref[...]
        copies = []
        for k in range(1, N_DEV):
            peer = (_flip(x, (k >> 2) & 1), _flip(y, (k >> 1) & 1), _flip(c, k & 1))
            cp = pltpu.make_async_remote_copy(
                src_ref=v_ref, dst_ref=out_ref.at[me], send_sem=send_sems.at[k - 1],
                recv_sem=recv_sems.at[k - 1], device_id=peer, device_id_type=MESH)
            cp.start()
            copies.append(cp)
        for cp in copies:
            cp.wait()

    return pl.pallas_call(
        body, name=name,
        out_shape=jax.ShapeDtypeStruct((N_DEV, 1, n), F32),
        in_specs=[pl.BlockSpec(memory_space=pltpu.VMEM)],
        out_specs=pl.BlockSpec(memory_space=pltpu.VMEM),
        scratch_shapes=[pltpu.SemaphoreType.DMA((N_DEV - 1,)), pltpu.SemaphoreType.DMA((N_DEV - 1,))],
    )(v)


class _Plan:
    def __init__(self, operands, out_shapes, sems, phases):
        self.operands, self.out_shapes, self.sems, self.phases = operands, out_shapes, sems, phases


def _slab_start(base, rows, jump, idx):
    return pl.multiple_of(base + idx * rows + (idx // 4) * jump, 16)


def _gather_plan(shards, dst_of, base_of, dst_shapes, jump_of=None):
    n = len(shards)
    rows = [s.shape[0] for s in shards]
    jump_of = jump_of or [0] * n

    def phases(srcs, dsts, sems):
        send_sems, recv_sems, local_sems = sems
        x, y, c = _place()
        me, sibling = (x, y, c), (x, y, 1 - c)
        chips = [(1 - x, y), (x, 1 - y), (1 - x, 1 - y)]

        def slab(i, px, py, pc):
            start = _slab_start(base_of[i], rows[i], jump_of[i], 4 * px + 2 * py + pc)
            return dsts[dst_of[i]].at[pl.ds(start, rows[i])]

        def copy(i, k, block, to, src=None):
            return pltpu.make_async_remote_copy(
                src_ref=slab(i, *block) if src is None else src, dst_ref=slab(i, *block),
                send_sem=send_sems.at[i, k], recv_sem=recv_sems.at[i, k],
                device_id=to, device_id_type=MESH)

        def mine():
            return [pltpu.make_async_copy(srcs[i], slab(i, *me), local_sems.at[i]) for i in range(n)]

        def first():
            out = []
            for i in range(n):
                out.append(copy(i, 0, me, sibling, src=srcs[i]))
                out += [copy(i, 1 + j, me, (*chip, c), src=srcs[i]) for j, chip in enumerate(chips)]
            return out

        def passed():
            return [(copy(i, 1 + j, (*chip, c), me), copy(i, 4 + j, (*chip, c), sibling))
                    for j, chip in enumerate(chips) for i in range(n)]

        def start():
            for cp in mine() + first():
                cp.start()

        def middle():
            for landed, onward in passed():
                landed.wait_recv()
                onward.start()

        def finish():
            for i in range(n):
                copy(i, 0, sibling, me).wait_recv()
                for j, chip in enumerate(chips):
                    copy(i, 4 + j, (*chip, 1 - c), me).wait_recv()
            for cp in first() + [onward for _, onward in passed()]:
                cp.wait_send()
            for cp in mine():
                cp.wait()

        return start, middle, finish

    sems = [pltpu.SemaphoreType.DMA((n, 7)), pltpu.SemaphoreType.DMA((n, 7)), pltpu.SemaphoreType.DMA((n,))]
    return _Plan(list(shards), [jax.ShapeDtypeStruct(s, BF16) for s in dst_shapes], sems, phases)


def _scatter_plan(grads, src_of, base_of, rows, cols, jump_of=None):
    n = len(rows)
    jump_of = jump_of or [0] * n

    def phases(srcs, recvs, sems):
        send_sems, recv_sems, local_sems = sems
        x, y, c = _place()
        me = 4 * x + 2 * y + c

        def slab(i, idx):
            start = _slab_start(base_of[i], rows[i], jump_of[i], idx)
            return srcs[src_of[i]].at[pl.ds(start, rows[i])]

        def copies():
            out = [pltpu.make_async_copy(slab(i, me), recvs[i].at[me], local_sems.at[i]) for i in range(n)]
            for k in range(1, N_DEV):
                px, py, pc = _flip(x, (k >> 2) & 1), _flip(y, (k >> 1) & 1), _flip(c, k & 1)
                out += [pltpu.make_async_remote_copy(
                    src_ref=slab(i, 4 * px + 2 * py + pc), dst_ref=recvs[i].at[me],
                    send_sem=send_sems.at[i, k - 1], recv_sem=recv_sems.at[i, k - 1],
                    device_id=(px, py, pc), device_id_type=MESH) for i in range(n)]
            return out

        def start():
            for cp in copies():
                cp.start()

        def finish():
            for cp in copies():
                cp.wait()

        return start, None, finish

    sems = [pltpu.SemaphoreType.DMA((n, 7)), pltpu.SemaphoreType.DMA((n, 7)), pltpu.SemaphoreType.DMA((n,))]
    out_shapes = [jax.ShapeDtypeStruct((N_DEV, rows[i], cols[i]), BF16) for i in range(n)]
    return _Plan(list(grads), out_shapes, sems, phases)


def _run_plan(name, plan):
    n_in, n_out = len(plan.operands), len(plan.out_shapes)

    def body(*refs):
        for phase in plan.phases(refs[:n_in], refs[n_in:n_in + n_out], refs[n_in + n_out:]):
            if phase is not None:
                phase()

    hbm = pl.BlockSpec(memory_space=pltpu.HBM)
    return pl.pallas_call(
        body, name=name, out_shape=plan.out_shapes,
        in_specs=[hbm] * n_in, out_specs=[hbm] * n_out, scratch_shapes=plan.sems,
    )(*plan.operands)


def _sum_contributions(name, recv):
    _, rows, cols = recv.shape
    tr = rows if rows <= 512 else 304 if rows % 304 == 0 else 256

    def body(r_ref, o_ref):
        acc = r_ref[0].astype(F32)
        for k in range(1, N_DEV):
            acc = acc + r_ref[k].astype(F32)
        o_ref[...] = acc

    return pl.pallas_call(
        body, name=name, grid=(rows // tr,),
        out_shape=jax.ShapeDtypeStruct((rows, cols), F32),
        in_specs=[pl.BlockSpec((N_DEV, tr, cols), lambda i: (0, i, 0))],
        out_specs=pl.BlockSpec((tr, cols), lambda i: (i, 0)),
        compiler_params=_params("parallel"),
    )(recv)


def _mm(name, a, b, mode, out_dtype, tm, tn, tk, *, carry=None, tiles_in=(), tiles_out=(), epilogue=None,
        n_outer=False, keep_b=False, sub=1):
    if mode == "TN":
        kk, m = a.shape
    else:
        m, kk = a.shape
    n = b.shape[0] if mode == "NT" else b.shape[1]
    tm, tn, tk = min(tm, m), min(tn, n), min(tk, kk)
    assert m % tm == 0 and n % tn == 0 and kk % tk == 0, (name, m, n, kk, tm, tn, tk)
    ni, nj, nk = m // tm, n // tn, kk // tk
    steps = ni * nj * nk
    dims = {"NN": NN_DIMS, "NT": NT_DIMS, "TN": TN_DIMS}[mode]
    if epilogue is None:
        tiles_out = [(jax.ShapeDtypeStruct((m, n), out_dtype), (tm, tn), lambda i, j: (i, j))]
    n_tin, n_tout = len(tiles_in), len(tiles_out)
    n_in = len(carry.operands) if carry else 0
    n_out = len(carry.out_shapes) if carry else 0
    n_acc = 1 if nk > 1 else 0
    n_keep = 2 if keep_b else 0
    assert not carry or steps >= 3
    assert not keep_b or (nk == 1 and nj == 1)
    assert sub == 1 or (mode != "TN" and tm % (8 * sub) == 0)
    ij = (lambda p, q: (q, p)) if n_outer else (lambda p, q: (p, q))
    inner = ni if n_outer else nj
    rs = tm // sub

    def body(a_ref, b_ref, *rest):
        tin = rest[:n_tin]
        cin = rest[n_tin:n_tin + n_in]
        tout = rest[n_tin + n_in:n_tin + n_in + n_tout]
        cout = rest[n_tin + n_in + n_tout:n_tin + n_in + n_tout + n_out]
        scratch = rest[n_tin + n_in + n_tout + n_out:]
        k = pl.program_id(2)
        visit = pl.program_id(0) * inner + pl.program_id(1)
        step = visit * nk + k
        if keep_b:
            b_kept, b_sem = scratch[n_acc:n_acc + 2]

            @pl.when(step == 0)
            def _():
                cp = pltpu.make_async_copy(b_ref, b_kept, b_sem)
                cp.start()
                cp.wait()

            b_ref = b_kept
        if carry:
            start, middle, finish = carry.phases(cin, cout, scratch[n_acc + n_keep:])
            pl.when(step == 0)(start)

        def rows_of(refs, specs, c):
            if sub == 1:
                return refs
            return [r.at[pl.ds(c * rs, rs)] if t[1][0] == tm else r for r, t in zip(refs, specs)]

        for c in range(sub):
            a_rows = a_ref[...] if sub == 1 else a_ref[pl.ds(c * rs, rs), :]
            part = lax.dot_general(a_rows, b_ref[...], dims, preferred_element_type=F32)
            tin_c, tout_c = rows_of(tin, tiles_in, c), rows_of(tout, tiles_out, c)

            def store(prod, tin_c=tin_c, tout_c=tout_c, c=c):
                if epilogue is None:
                    tout_c[0][...] = prod.astype(out_dtype)
                else:
                    epilogue(prod, jnp.logical_and(visit == 0, c == 0), tin_c, tout_c)

            if nk == 1:
                store(part)
            else:
                acc_c = scratch[0] if sub == 1 else scratch[0].at[pl.ds(c * rs, rs)]

                @pl.when(k == 0)
                def _(acc_c=acc_c, part=part):
                    acc_c[...] = part

                @pl.when((k > 0) & (k < nk - 1))
                def _(acc_c=acc_c, part=part):
                    acc_c[...] += part

                @pl.when(k == nk - 1)
                def _(acc_c=acc_c, part=part, store=store):
                    store(acc_c[...] + part)

        if carry:
            if middle is not None:
                pl.when(step == (steps * 3) // 5)(middle)
            pl.when(step == steps - 1)(finish)

    def spec(shape, fn):
        return pl.BlockSpec(shape, lambda p, q, k: fn(*ij(p, q)))

    a_spec = (pl.BlockSpec((tk, tm), lambda p, q, k: (k, ij(p, q)[0])) if mode == "TN"
              else pl.BlockSpec((tm, tk), lambda p, q, k: (ij(p, q)[0], k)))
    if keep_b:
        b_spec = pl.BlockSpec(memory_space=pl.ANY)
    elif mode == "NT":
        b_spec = pl.BlockSpec((tn, tk), lambda p, q, k: (ij(p, q)[1], k))
    else:
        b_spec = pl.BlockSpec((tk, tn), lambda p, q, k: (k, ij(p, q)[1]))
    hbm = pl.BlockSpec(memory_space=pltpu.HBM)
    sequential = carry or epilogue or keep_b
    out = pl.pallas_call(
        body, name=name, grid=(nj, ni, nk) if n_outer else (ni, nj, nk),
        out_shape=[t[0] for t in tiles_out] + (carry.out_shapes if carry else []),
        in_specs=[a_spec, b_spec] + [spec(t[1], t[2]) for t in tiles_in] + [hbm] * n_in,
        out_specs=[spec(t[1], t[2]) for t in tiles_out] + [hbm] * n_out,
        scratch_shapes=([pltpu.VMEM((tm, tn), F32)] * n_acc
                        + ([pltpu.VMEM(b.shape, b.dtype), pltpu.SemaphoreType.DMA] if keep_b else [])
                        + (carry.sems if carry else [])),
        compiler_params=(_params("arbitrary", "arbitrary", "arbitrary") if sequential
                         else _params("parallel", "parallel", "arbitrary")),
    )(a, b, *[t[0] for t in tiles_in], *(carry.operands if carry else []))
    return out if (carry or epilogue) else out[0]


def _row(tm, w, off=0):
    return pl.BlockSpec((tm, w), lambda i: (i, off))


def _vec(w):
    return pl.BlockSpec((1, w), lambda i: (0, 0))


def _sigmoid(x):
    return 1.0 / (1.0 + jnp.exp(-x))


def _normmod(name, x, g, sc, sh, tm=512):
    s = x.shape[0]

    def body(x_ref, g_ref, sc_ref, sh_ref, h_ref):
        xv = x_ref[...]
        r = lax.rsqrt(jnp.mean(xv * xv, axis=-1, keepdims=True) + EPS)
        h_ref[...] = ((xv * r) * g_ref[...] * (1.0 + sc_ref[...]) + sh_ref[...]).astype(BF16)

    return pl.pallas_call(
        body, name=name, grid=(s // tm,),
        out_shape=jax.ShapeDtypeStruct((s, D), BF16),
        in_specs=[_row(tm, D), _vec(D), _vec(D), _vec(D)], out_specs=_row(tm, D),
        compiler_params=_params("parallel"),
    )(x, g, sc, sh)


def _normmod_bwd(name, dh, x, gin, g, sc, sh, tm=512):
    s = x.shape[0]

    def body(dh_ref, x_ref, gin_ref, g_ref, sc_ref, sh_ref, gout_ref, acc_ref):
        xv, dhv = x_ref[...], dh_ref[...]
        r = lax.rsqrt(jnp.mean(xv * xv, axis=-1, keepdims=True) + EPS)
        nv = xv * r
        gv, one_sc = g_ref[...], 1.0 + sc_ref[...]
        dn = dhv * gv * one_sc
        dx = r * (dn - nv * jnp.mean(dn * nv, axis=-1, keepdims=True))
        gout_ref[...] = gin_ref[...] + dx

        @pl.when(pl.program_id(0) == 0)
        def _():
            acc_ref[...] = jnp.zeros_like(acc_ref)

        dhn = dhv * nv
        acc_ref[0:1, :] += jnp.sum(dhv, axis=0, keepdims=True)
        acc_ref[1:2, :] += jnp.sum(dhn * gv, axis=0, keepdims=True)
        acc_ref[2:3, :] += jnp.sum(dhn * one_sc, axis=0, keepdims=True)

    return pl.pallas_call(
        body, name=name, grid=(s // tm,),
        out_shape=[jax.ShapeDtypeStruct((s, D), F32), jax.ShapeDtypeStruct((8, D), F32)],
        in_specs=[_row(tm, D), _row(tm, D), _row(tm, D), _vec(D), _vec(D), _vec(D)],
        out_specs=[_row(tm, D), pl.BlockSpec((8, D), lambda i: (0, 0))],
        compiler_params=_params("arbitrary"),
    )(dh, x, gin, g, sc, sh)


def _swiglu(name, ab, tm=512):
    s = ab.shape[0]

    def body(ab_ref, s_ref):
        a = ab_ref[:, :FF].astype(F32)
        b = ab_ref[:, FF:].astype(F32)
        s_ref[...] = (a * _sigmoid(a) * b).astype(BF16)

    return pl.pallas_call(
        body, name=name, grid=(s // tm,),
        out_shape=jax.ShapeDtypeStruct((s, FF), BF16),
        in_specs=[_row(tm, 2 * FF)], out_specs=_row(tm, FF),
        compiler_params=_params("parallel"),
    )(ab)


def _swiglu_bwd(name, ds, ab, tm=256):
    s = ab.shape[0]

    def body(ds_ref, ab_ref, dab_ref):
        a = ab_ref[:, :FF].astype(F32)
        b = ab_ref[:, FF:].astype(F32)
        dsv = ds_ref[...].astype(F32)
        sig = _sigmoid(a)
        dab_ref[:, :FF] = (dsv * b * (sig * (1.0 + a * (1.0 - sig)))).astype(BF16)
        dab_ref[:, FF:] = (dsv * (a * sig)).astype(BF16)

    return pl.pallas_call(
        body, name=name, grid=(s // tm,),
        out_shape=jax.ShapeDtypeStruct((s, 2 * FF), BF16),
        in_specs=[_row(tm, FF), _row(tm, 2 * FF)], out_specs=_row(tm, 2 * FF),
        compiler_params=_params("parallel"),
    )(ds, ab)


def _residual(name, x, f, gt, coef, tm=512):
    s = x.shape[0]

    def body(x_ref, f_ref, gt_ref, o_ref):
        o_ref[...] = x_ref[...] + (coef * gt_ref[...]) * f_ref[...]

    return pl.pallas_call(
        body, name=name, grid=(s // tm,),
        out_shape=jax.ShapeDtypeStruct((s, D), F32),
        in_specs=[_row(tm, D), _row(tm, D), _vec(D)], out_specs=_row(tm, D),
        compiler_params=_params("parallel"),
    )(x, f, gt)


def _gate_bwd(name, gin, f, gt, coef, tm=512):
    s = gin.shape[0]

    def body(g_ref, f_ref, gt_ref, df_ref, acc_ref):
        gv = g_ref[...]
        df_ref[...] = ((coef * gt_ref[...]) * gv).astype(BF16)

        @pl.when(pl.program_id(0) == 0)
        def _():
            acc_ref[...] = jnp.zeros_like(acc_ref)

        acc_ref[0:1, :] += coef * jnp.sum(gv * f_ref[...], axis=0, keepdims=True)

    return pl.pallas_call(
        body, name=name, grid=(s // tm,),
        out_shape=[jax.ShapeDtypeStruct((s, D), BF16), jax.ShapeDtypeStruct((8, D), F32)],
        in_specs=[_row(tm, D), _row(tm, D), _vec(D)],
        out_specs=[_row(tm, D), pl.BlockSpec((8, D), lambda i: (0, 0))],
        compiler_params=_params("arbitrary"),
    )(gin, f, gt)


def _loss_grad(x3, target, tm=512):
    s = x3.shape[0]

    def body(y_ref, t_ref, g_ref, l_ref):
        e = y_ref[...] - t_ref[...]
        g_ref[...] = e * (1.0 / D)

        @pl.when(pl.program_id(0) == 0)
        def _():
            l_ref[...] = jnp.zeros_like(l_ref)

        l_ref[...] += jnp.sum(jnp.mean(e * e, axis=-1, keepdims=True), axis=0, keepdims=True) * 0.5

    return pl.pallas_call(
        body, name="loss_grad", grid=(s // tm,),
        out_shape=[jax.ShapeDtypeStruct((s, D), F32), jax.ShapeDtypeStruct((8, 128), F32)],
        in_specs=[_row(tm, D), _row(tm, D)],
        out_specs=[_row(tm, D), pl.BlockSpec((8, 128), lambda i: (0, 0))],
        compiler_params=_params("arbitrary"),
    )(x3, target)


def _heads(x, fn):
    return jnp.concatenate([fn(x[:, h * HD:(h + 1) * HD], h) for h in range(COL // HD)], axis=1)


def _qknorm(proj, wqk, tm=512):
    s = proj.shape[0]

    def body(p_ref, w_ref, o_ref):
        pv = p_ref[...].astype(F32)
        wv = w_ref[...]

        def one(qh, h):
            r = lax.rsqrt(jnp.mean(qh * qh, axis=-1, keepdims=True) + EPS)
            return (qh * r) * wv[:, h * HD:(h + 1) * HD]

        o_ref[...] = _heads(pv, one).astype(BF16)

    return pl.pallas_call(
        body, name="qknorm", grid=(s // tm, QKW // COL),
        out_shape=jax.ShapeDtypeStruct((s, QKW), BF16),
        in_specs=[pl.BlockSpec((tm, COL), lambda i, j: (i, j)), pl.BlockSpec((1, COL), lambda i, j: (0, j))],
        out_specs=pl.BlockSpec((tm, COL), lambda i, j: (i, j)),
        compiler_params=_params("parallel", "parallel"),
    )(proj, wqk)


def _qknorm_bwd(proj, dqkn, wqk, dproj, tm=512):
    s = proj.shape[0]

    def body(p_ref, d_ref, w_ref, _, o_ref, acc_ref):
        pv = p_ref[...].astype(F32)
        dv = d_ref[...]
        wv = w_ref[...]
        sums = []

        def one(qh, h):
            dn = dv[:, h * HD:(h + 1) * HD]
            r = lax.rsqrt(jnp.mean(qh * qh, axis=-1, keepdims=True) + EPS)
            nh = qh * r
            sums.append(jnp.sum(dn * nh, axis=0, keepdims=True))
            dnw = dn * wv[:, h * HD:(h + 1) * HD]
            return r * (dnw - nh * jnp.mean(dnw * nh, axis=-1, keepdims=True))

        o_ref[...] = _heads(pv, one).astype(BF16)

        @pl.when(pl.program_id(1) == 0)
        def _():
            acc_ref[...] = jnp.zeros_like(acc_ref)

        acc_ref[0:1, :] += jnp.concatenate(sums, axis=1)

    return pl.pallas_call(
        body, name="qknorm_bwd", grid=(QKW // COL, s // tm),
        out_shape=[jax.ShapeDtypeStruct((s, IN_W), BF16), jax.ShapeDtypeStruct((8, QKW), F32)],
        in_specs=[pl.BlockSpec((tm, COL), lambda j, i: (i, j)), pl.BlockSpec((tm, COL), lambda j, i: (i, j)),
                  pl.BlockSpec((1, COL), lambda j, i: (0, j)), pl.BlockSpec(memory_space=pl.ANY)],
        out_specs=[pl.BlockSpec((tm, COL), lambda j, i: (i, j)), pl.BlockSpec((8, COL), lambda j, i: (0, j))],
        input_output_aliases={3: 0},
        compiler_params=_params("arbitrary", "arbitrary"),
    )(proj, dqkn, wqk, dproj)


def _attn_shapes(s, g):
    d = DILATIONS[g]
    tb = min(s, max(2048, 256 * d))
    sb = min(256, tb // d)
    pb = BAND * d
    assert s % tb == 0 and tb % pb == 0 and (tb // d) % sb == 0 and sb % BAND == 0
    return d, tb, sb, pb


def _lanes(x, width):
    return jnp.concatenate([x] * (width // HD), axis=1)


def _every(start, size, d):
    return pl.ds(start, size, stride=d) if d > 1 else pl.ds(start, size)


def _attn_specs(g, tb, pb, s, ahead):
    ratio = tb // pb
    if ahead:
        nbr = lambda n: jnp.minimum((n + 1) * ratio, s // pb - 1)
    else:
        nbr = lambda n: jnp.maximum(n * ratio - 1, 0)
    cur = lambda base: pl.BlockSpec((tb, HD), lambda h, n: (n, base + g * N_HEADS + h))
    side = lambda base: pl.BlockSpec((pb, HD), lambda h, n: (nbr(n), base + g * N_HEADS + h))
    tok = pl.BlockSpec((tb, HD), lambda h, n: (n, h))
    tok_side = pl.BlockSpec((pb, HD), lambda h, n: (nbr(n), h))
    return cur, side, tok, tok_side


Q_COL, K_COL, V_COL = 0, 12, 24


def _attn_fwd(g, qkn, proj):
    s = qkn.shape[0]
    d, tb, sb, pb = _attn_shapes(s, g)
    nj = tb // d // sb
    scale = HD ** -0.5

    def body(q_ref, kc_ref, kp_ref, vc_ref, vp_ref, o_ref, lse_ref, qf, kf, vf):
        n = pl.program_id(1)
        qf[...] = q_ref[...].astype(F32)
        kf[0:pb] = kp_ref[...].astype(F32)
        kf[pb:] = kc_ref[...].astype(F32)
        vf[0:pb] = vp_ref[...].astype(F32)
        vf[pb:] = vc_ref[...].astype(F32)
        for r in range(d):
            for j in range(nj):
                at = j * sb * d + r
                q = qf[_every(at, sb, d), :].astype(BF16)
                k = kf[_every(at, sb + BAND, d), :].astype(BF16)
                v = vf[_every(at, sb + BAND, d), :].astype(BF16)
                sc = lax.dot_general(q, k, NT_DIMS, preferred_element_type=F32) * scale
                qi = lax.broadcasted_iota(jnp.int32, sc.shape, 0)
                kj = lax.broadcasted_iota(jnp.int32, sc.shape, 1)
                valid = (kj >= qi) & (kj <= qi + BAND)
                if j == 0:
                    valid = valid & ((kj >= BAND) | (n > 0))
                sc = jnp.where(valid, sc, -1e30)
                m = jnp.max(sc, axis=-1, keepdims=True)
                p = jnp.exp(sc - m)
                l = jnp.sum(p, axis=-1, keepdims=True)
                o = lax.dot_general(p.astype(BF16), v, NN_DIMS, preferred_element_type=F32)
                o_ref[_every(at, sb, d), :] = o / l
                lse_ref[_every(at, sb, d), :] = jnp.broadcast_to(m + jnp.log(l), (sb, HD))

    cur, side, tok, _ = _attn_specs(g, tb, pb, s, ahead=False)
    return pl.pallas_call(
        body, name=f"attn_fwd_g{g}", grid=(N_HEADS, s // tb),
        out_shape=[jax.ShapeDtypeStruct((s, COL), F32)] * 2,
        in_specs=[cur(Q_COL), cur(K_COL), side(K_COL), cur(V_COL), side(V_COL)],
        out_specs=[tok, tok],
        scratch_shapes=[pltpu.VMEM((tb, HD), F32), pltpu.VMEM((tb + pb, HD), F32),
                        pltpu.VMEM((tb + pb, HD), F32)],
        compiler_params=_params("parallel", "arbitrary"),
    )(qkn, qkn, qkn, proj, proj)


def _attn_combine(os_, lses, tm=512):
    s = os_[0].shape[0]

    def body(o0, o1, o2, l0, l1, l2, o_ref, lse_ref):
        a, b, c = l0[...], l1[...], l2[...]
        m = jnp.maximum(jnp.maximum(a, b), c)
        ea, eb, ec = jnp.exp(a - m), jnp.exp(b - m), jnp.exp(c - m)
        tot = ea + eb + ec
        o_ref[...] = ((ea * o0[...] + eb * o1[...] + ec * o2[...]) / tot).astype(BF16)
        lse_ref[...] = m + jnp.log(tot)

    return pl.pallas_call(
        body, name="attn_combine", grid=(s // tm,),
        out_shape=[jax.ShapeDtypeStruct((s, COL), BF16), jax.ShapeDtypeStruct((s, COL), F32)],
        in_specs=[_row(tm, COL)] * 6, out_specs=[_row(tm, COL)] * 2,
        compiler_params=_params("parallel"),
    )(*os_, *lses)


def _attn_delta(do, o, tm=512):
    s = do.shape[0]

    def body(do_ref, o_ref, del_ref):
        prod = do_ref[...] * o_ref[...].astype(F32)
        del_ref[...] = _heads(prod, lambda ph, h: jnp.broadcast_to(
            jnp.sum(ph, axis=-1, keepdims=True), ph.shape))

    return pl.pallas_call(
        body, name="attn_delta", grid=(s // tm,),
        out_shape=jax.ShapeDtypeStruct((s, COL), F32),
        in_specs=[_row(tm, COL)] * 2, out_specs=_row(tm, COL),
        compiler_params=_params("parallel"),
    )(do, o)


def _attn_dq(g, qkn, proj, do, lse, delta, dqkn):
    s = qkn.shape[0]
    d, tb, sb, pb = _attn_shapes(s, g)
    nj = tb // d // sb
    scale = HD ** -0.5
    chained = dqkn is not None

    def body(q_ref, kc_ref, kp_ref, vc_ref, vp_ref, do_ref, lse_ref, del_ref, *rest):
        dq_ref, qf, kf, vf = rest[-4:]
        n = pl.program_id(1)
        qf[...] = q_ref[...].astype(F32)
        kf[0:pb] = kp_ref[...].astype(F32)
        kf[pb:] = kc_ref[...].astype(F32)
        vf[0:pb] = vp_ref[...].astype(F32)
        vf[pb:] = vc_ref[...].astype(F32)
        for r in range(d):
            for j in range(nj):
                at = j * sb * d + r
                rows = _every(at, sb, d)
                q = qf[rows, :].astype(BF16)
                k = kf[_every(at, sb + BAND, d), :].astype(BF16)
                v = vf[_every(at, sb + BAND, d), :].astype(BF16)
                sc = lax.dot_general(q, k, NT_DIMS, preferred_element_type=F32) * scale
                qi = lax.broadcasted_iota(jnp.int32, sc.shape, 0)
                kj = lax.broadcasted_iota(jnp.int32, sc.shape, 1)
                valid = (kj >= qi) & (kj <= qi + BAND)
                if j == 0:
                    valid = valid & ((kj >= BAND) | (n > 0))
                p = jnp.exp(jnp.where(valid, sc - _lanes(lse_ref[rows, :], sb + BAND), -1e30))
                dp = lax.dot_general(do_ref[rows, :].astype(BF16), v, NT_DIMS, preferred_element_type=F32)
                ds = p * (dp - _lanes(del_ref[rows, :], sb + BAND)) * scale
                dq_ref[rows, :] = lax.dot_general(ds.astype(BF16), k, NN_DIMS, preferred_element_type=F32)

    cur, side, tok, _ = _attn_specs(g, tb, pb, s, ahead=False)
    args = [qkn, qkn, qkn, proj, proj, do, lse, delta]
    specs = [cur(Q_COL), cur(K_COL), side(K_COL), cur(V_COL), side(V_COL), tok, tok, tok]
    if chained:
        args.append(dqkn)
        specs.append(pl.BlockSpec(memory_space=pl.ANY))
    return pl.pallas_call(
        body, name=f"attn_dq_g{g}", grid=(N_HEADS, s // tb),
        out_shape=jax.ShapeDtypeStruct((s, QKW), F32),
        in_specs=specs, out_specs=cur(Q_COL),
        input_output_aliases={8: 0} if chained else {},
        scratch_shapes=[pltpu.VMEM((tb, HD), F32), pltpu.VMEM((tb + pb, HD), F32),
                        pltpu.VMEM((tb + pb, HD), F32)],
        compiler_params=_params("arbitrary", "arbitrary"),
    )(*args)


def _attn_dkv(g, qkn, proj, do, lse, delta, dqkn, dproj):
    s = qkn.shape[0]
    d, tb, sb, pb = _attn_shapes(s, g)
    nj = tb // d // sb
    nt = s // tb
    scale = HD ** -0.5

    def body(k_ref, v_ref, qc_ref, qn_ref, doc_ref, don_ref, lc_ref, ln_ref, dc_ref, dn_ref, _a, _b,
             dk_ref, dv_ref, kf, vf, qf, dvf):
        n = pl.program_id(1)
        kf[...] = k_ref[...].astype(F32)
        vf[...] = v_ref[...].astype(F32)
        qf[0:tb] = qc_ref[...].astype(F32)
        qf[tb:] = qn_ref[...].astype(F32)

        def window(c_ref, n_ref, r, j):
            at = j * sb * d + r
            if j < nj - 1:
                return c_ref[_every(at, sb + BAND, d), :]
            return jnp.concatenate([c_ref[_every(at, sb, d), :], n_ref[_every(r, BAND, d), :]], axis=0)

        for r in range(d):
            for j in range(nj):
                at = j * sb * d + r
                rows = _every(at, sb, d)
                k = kf[rows, :].astype(BF16)
                v = vf[rows, :].astype(BF16)
                q = qf[_every(at, sb + BAND, d), :].astype(BF16)
                dov = window(doc_ref, don_ref, r, j).astype(BF16)
                sc = lax.dot_general(q, k, NT_DIMS, preferred_element_type=F32) * scale
                qi = lax.broadcasted_iota(jnp.int32, sc.shape, 0)
                kj = lax.broadcasted_iota(jnp.int32, sc.shape, 1)
                valid = (qi >= kj) & (qi <= kj + BAND)
                if j == nj - 1:
                    valid = valid & ((qi < sb) | (n < nt - 1))
                p = jnp.exp(jnp.where(valid, sc - _lanes(window(lc_ref, ln_ref, r, j), sb), -1e30))
                dp = lax.dot_general(dov, v, NT_DIMS, preferred_element_type=F32)
                ds = p * (dp - _lanes(window(dc_ref, dn_ref, r, j), sb)) * scale
                dvf[rows, :] = lax.dot_general(p.astype(BF16), dov, TN_DIMS, preferred_element_type=F32)
                dk_ref[rows, :] = lax.dot_general(ds.astype(BF16), q, TN_DIMS, preferred_element_type=F32)
        dv_ref[...] = dvf[...].astype(BF16)

    cur, side, tok, tok_side = _attn_specs(g, tb, pb, s, ahead=True)
    anyspec = pl.BlockSpec(memory_space=pl.ANY)
    return pl.pallas_call(
        body, name=f"attn_dkv_g{g}", grid=(N_HEADS, nt),
        out_shape=[jax.ShapeDtypeStruct((s, QKW), F32), jax.ShapeDtypeStruct((s, IN_W), BF16)],
        in_specs=[cur(K_COL), cur(V_COL), cur(Q_COL), side(Q_COL), tok, tok_side, tok, tok_side,
                  tok, tok_side, anyspec, anyspec],
        out_specs=[cur(K_COL), cur(V_COL)],
        input_output_aliases={10: 0, 11: 1},
        scratch_shapes=[pltpu.VMEM((tb, HD), F32), pltpu.VMEM((tb, HD), F32),
                        pltpu.VMEM((tb + pb, HD), F32), pltpu.VMEM((tb, HD), F32)],
        compiler_params=_params("arbitrary", "arbitrary"),
    )(qkn, proj, qkn, qkn, do, do, lse, lse, delta, delta, dqkn, dproj)


def _shift_down(x, before, k):
    rolled = pltpu.roll(x, k, 0)
    head = jnp.where(lax.broadcasted_iota(jnp.int32, before.shape, 0) < k, pltpu.roll(before, k, 0), rolled[:8])
    return jnp.concatenate([head, rolled[8:]], axis=0)


def _shift_up(x, after, k):
    rows = x.shape[0]
    rolled = pltpu.roll(x, rows - k, 0)
    tail = jnp.where(lax.broadcasted_iota(jnp.int32, after.shape, 0) >= 8 - k,
                     pltpu.roll(after, 8 - k, 0), rolled[rows - 8:])
    return jnp.concatenate([rolled[:rows - 8], tail], axis=0)


def _conv_fwd(proj, cw, tm=512):
    s = proj.shape[0]
    r16 = tm // 16

    def body(u_ref, b_ref, c_ref, up_ref, cp_ref, w_ref, z_ref):
        i = pl.program_id(1)
        xc = c_ref[...].astype(F32) * u_ref[...].astype(F32)
        xp = jnp.where(i > 0, cp_ref[8:16, :].astype(F32) * up_ref[8:16, :].astype(F32), 0.0)
        w = w_ref[...]
        conv = _shift_down(xc, xp, 2) * w[0:1] + _shift_down(xc, xp, 1) * w[1:2] + xc * w[2:3]
        z_ref[...] = (b_ref[...].astype(F32) * conv).astype(BF16)

    tile = lambda blk: pl.BlockSpec((tm, COL), lambda j, i: (i, blk + j))
    before = lambda blk: pl.BlockSpec((16, COL), lambda j, i: (jnp.maximum(i * r16 - 1, 0), blk + j))
    return pl.pallas_call(
        body, name="conv_fwd", grid=(D // COL, s // tm),
        out_shape=jax.ShapeDtypeStruct((s, D), BF16),
        in_specs=[tile(U_BLK), tile(B_BLK), tile(C_BLK), before(U_BLK), before(C_BLK),
                  pl.BlockSpec((3, COL), lambda j, i: (0, j))],
        out_specs=pl.BlockSpec((tm, COL), lambda j, i: (i, j)),
        compiler_params=_params("parallel", "parallel"),
    )(proj, proj, proj, proj, proj, cw)


def _conv_bwd(dz, proj, cw, dproj, tm=512):
    s = proj.shape[0]
    r8, r16 = tm // 8, tm // 16
    nrow = s // tm

    def body(dz_ref, u_ref, b_ref, c_ref, up_ref, cp_ref, dzn_ref, bn_ref, w_ref, _, o_ref, acc_ref):
        piece, i = pl.program_id(1), pl.program_id(2)
        u, c = u_ref[...].astype(F32), c_ref[...].astype(F32)
        bv = b_ref[...].astype(F32)
        dzv = dz_ref[...]
        w = w_ref[...]

        @pl.when((piece == 0) & (i == 0))
        def _():
            acc_ref[...] = jnp.zeros_like(acc_ref)

        @pl.when(piece == 1)
        def _():
            xc = c * u
            xp = jnp.where(i > 0, cp_ref[8:16, :].astype(F32) * up_ref[8:16, :].astype(F32), 0.0)
            x2, x1 = _shift_down(xc, xp, 2), _shift_down(xc, xp, 1)
            o_ref[...] = (dzv * (x2 * w[0:1] + x1 * w[1:2] + xc * w[2:3])).astype(BF16)
            dconv = dzv * bv
            acc_ref[0:1, :] += jnp.sum(dconv * x2, axis=0, keepdims=True)
            acc_ref[1:2, :] += jnp.sum(dconv * x1, axis=0, keepdims=True)
            acc_ref[2:3, :] += jnp.sum(dconv * xc, axis=0, keepdims=True)

        @pl.when(piece != 1)
        def _():
            dconv = dzv * bv
            dn = jnp.where(i < nrow - 1, dzn_ref[...] * bn_ref[0:8, :].astype(F32), 0.0)
            dxc = dconv * w[2:3] + _shift_up(dconv, dn, 1) * w[1:2] + _shift_up(dconv, dn, 2) * w[0:1]
            o_ref[...] = (dxc * jnp.where(piece == 0, c, u)).astype(BF16)

    tile = lambda blk: pl.BlockSpec((tm, COL), lambda j, p, i: (i, blk + j))
    before = lambda blk: pl.BlockSpec((16, COL), lambda j, p, i: (jnp.maximum(i * r16 - 1, 0), blk + j))
    after = lambda rows, blk: pl.BlockSpec(
        (rows, COL), lambda j, p, i: (jnp.minimum((i + 1) * (tm // rows), s // rows - 1), blk + j))
    return pl.pallas_call(
        body, name="conv_bwd", grid=(D // COL, 3, nrow),
        out_shape=[jax.ShapeDtypeStruct((s, IN_W), BF16), jax.ShapeDtypeStruct((8, D), F32)],
        in_specs=[tile(0), tile(U_BLK), tile(B_BLK), tile(C_BLK), before(U_BLK), before(C_BLK),
                  after(8, 0), after(16, B_BLK), pl.BlockSpec((3, COL), lambda j, p, i: (0, j)),
                  pl.BlockSpec(memory_space=pl.ANY)],
        out_specs=[pl.BlockSpec((tm, COL), lambda j, p, i: (i, U_BLK + 2 * p + j)),
                   pl.BlockSpec((8, COL), lambda j, p, i: (0, j))],
        input_output_aliases={9: 0},
        compiler_params=_params("arbitrary", "arbitrary", "arbitrary"),
    )(dz, proj, proj, proj, proj, proj, dz, proj, cw, dproj)


def _merge_fwd(ya, yc, proj, tm=512):
    s = proj.shape[0]

    def body(ya_ref, yc_ref, ga_ref, gc_ref, o_ref):
        o_ref[...] = (_sigmoid(ga_ref[...].astype(F32)) * ya_ref[...].astype(F32)
                      + _sigmoid(gc_ref[...].astype(F32)) * yc_ref[...].astype(F32)).astype(BF16)

    tile = lambda blk: pl.BlockSpec((tm, COL), lambda j, i: (i, blk + j))
    return pl.pallas_call(
        body, name="merge_fwd", grid=(D // COL, s // tm),
        out_shape=jax.ShapeDtypeStruct((s, D), BF16),
        in_specs=[tile(0), tile(0), tile(GA_BLK), tile(GC_BLK)], out_specs=tile(0),
        compiler_params=_params("parallel", "parallel"),
    )(ya, yc, proj, proj)


def _merge_bwd_branches(dm, proj, tm=512):
    s = proj.shape[0]

    def body(dm_ref, ga_ref, gc_ref, dya_ref, dyc_ref):
        dmv = dm_ref[...]
        dya_ref[...] = (dmv * _sigmoid(ga_ref[...].astype(F32))).astype(BF16)
        dyc_ref[...] = (dmv * _sigmoid(gc_ref[...].astype(F32))).astype(BF16)

    tile = lambda blk: pl.BlockSpec((tm, COL), lambda j, i: (i, blk + j))
    return pl.pallas_call(
        body, name="merge_bwd_branches", grid=(D // COL, s // tm),
        out_shape=[jax.ShapeDtypeStruct((s, D), BF16)] * 2,
        in_specs=[tile(0), tile(GA_BLK), tile(GC_BLK)], out_specs=[tile(0)] * 2,
        compiler_params=_params("parallel", "parallel"),
    )(dm, proj, proj)


def _merge_bwd_gates(dm, ya, yc, proj, tm=512):
    s = proj.shape[0]
    half = D // COL

    def body(dm_ref, ya_ref, yc_ref, g_ref, o_ref):
        y = jnp.where(pl.program_id(0) < half, ya_ref[...].astype(F32), yc_ref[...].astype(F32))
        sig = _sigmoid(g_ref[...].astype(F32))
        o_ref[...] = (dm_ref[...] * y * sig * (1.0 - sig)).astype(BF16)

    chan = pl.BlockSpec((tm, COL), lambda jj, i: (i, jj % half))
    gate = pl.BlockSpec((tm, COL), lambda jj, i: (i, GA_BLK + jj))
    return pl.pallas_call(
        body, name="merge_bwd_gates", grid=(2 * half, s // tm),
        out_shape=jax.ShapeDtypeStruct((s, IN_W), BF16),
        in_specs=[chan, chan, chan, gate], out_specs=gate,
        compiler_params=_params("parallel", "parallel"),
    )(dm, ya, yc, proj)


def _mod_part(c_all, w_ada, b_part):
    def body(c_ref, w_ref, b_ref, o_ref):
        cv = c_ref[...]
        act = cv * _sigmoid(cv)
        o_ref[...] = jnp.dot(act, w_ref[...], preferred_element_type=F32,
                             precision=lax.Precision.HIGHEST) + b_ref[...]

    return pl.pallas_call(
        body, name="mod_part", out_shape=jax.ShapeDtypeStruct((N_DEV, w_ada.shape[1]), F32),
    )(c_all, w_ada, b_part)


def _w_ada_grad(c_all_t, dmod_part):
    def body(c_ref, d_ref, o_ref):
        cv = c_ref[...]
        act = cv * _sigmoid(cv)
        dv = d_ref[...]
        acc = act[:, 0:1] * dv[0:1, :]
        for b in range(1, N_DEV):
            acc = acc + act[:, b:b + 1] * dv[b:b + 1, :]
        o_ref[...] = acc

    return pl.pallas_call(
        body, name="w_ada_grad", out_shape=jax.ShapeDtypeStruct((D, dmod_part.shape[1]), F32),
    )(c_all_t, dmod_part)


def _sum_rows(name, v):
    def body(v_ref, o_ref):
        acc = v_ref[0]
        for k in range(1, N_DEV):
            acc = acc + v_ref[k]
        o_ref[...] = acc

    return pl.pallas_call(body, name=name, out_shape=jax.ShapeDtypeStruct(v.shape[1:], F32))(v)


def _adamw(name, w, g, m, v):
    rows, cols = w.shape
    limit = max(8, (1 << 20) // (4 * cols))
    tr = rows if rows <= limit else next((t for t in range(limit - limit % 8, 7, -8) if rows % t == 0), rows)
    c1 = 1.0 - ADAM_B1 ** ADAM_STEP
    c2 = 1.0 - ADAM_B2 ** ADAM_STEP

    def body(w_ref, g_ref, m_ref, v_ref, d_ref, nm_ref, nv_ref):
        gv = g_ref[...]
        nm = ADAM_B1 * m_ref[...] + (1.0 - ADAM_B1) * gv
        nv = ADAM_B2 * v_ref[...] + (1.0 - ADAM_B2) * (gv * gv)
        nm_ref[...] = nm
        nv_ref[...] = nv
        d_ref[...] = -ADAM_LR * ((nm / c1) / (jnp.sqrt(nv / c2) + ADAM_EPS) + ADAM_WD * w_ref[...])

    spec = pl.BlockSpec((tr, cols), lambda i: (i, 0))
    return pl.pallas_call(
        body, name=name, grid=(rows // tr,),
        out_shape=[jax.ShapeDtypeStruct((rows, cols), F32)] * 3,
        in_specs=[spec] * 4, out_specs=[spec] * 3,
        compiler_params=_params("parallel"),
    )(w, g, m, v)


HALF = FF // 2


def _sds(shape, dtype):
    return jax.ShapeDtypeStruct(shape, dtype)


def _row_tile(w):
    return lambda tm: ((tm, w), lambda i, j: (i, 0))


def _one(w):
    return lambda rows: ((rows, w), lambda i, j: (0, 0))


def _gate_up_swiglu(name, h, wgu, carry=None, tm=512):
    s = h.shape[0]
    tm = min(tm, s)

    def epilogue(prod, first, tin, tout):
        ab_ref, s_ref = tout
        ab_ref[...] = prod.astype(BF16)
        a, b = prod[:, :HALF], prod[:, HALF:]
        s_ref[...] = (a * _sigmoid(a) * b).astype(BF16)

    return _mm(name, h, wgu, "NT", None, tm, FF, D, carry=carry, n_outer=True, epilogue=epilogue,
               tiles_out=[(_sds((s, 2 * FF), BF16), (tm, FF), lambda i, j: (i, j)),
                          (_sds((s, FF), BF16), (tm, HALF), lambda i, j: (i, j))])


def _d_hidden_swiglu(name, df, wd, ab, tm=512):
    s = df.shape[0]
    tm = min(tm, s)

    def epilogue(prod, first, tin, tout):
        a = tin[0][:, :HALF].astype(F32)
        b = tin[0][:, HALF:].astype(F32)
        sig = _sigmoid(a)
        tout[0][:, :HALF] = (prod * b * (sig * (1.0 + a * (1.0 - sig)))).astype(BF16)
        tout[0][:, HALF:] = (prod * (a * sig)).astype(BF16)

    return _mm(name, df, wd, "NT", None, tm, HALF, D, n_outer=True, epilogue=epilogue,
               tiles_in=[(ab, (tm, FF), lambda i, j: (i, j))],
               tiles_out=[(_sds((s, 2 * FF), BF16), (tm, FF), lambda i, j: (i, j))])[0]


def _out_residual(name, a, w, x, gt, coef, nxt, tm=512, tk=FF):
    s = a.shape[0]
    tm = min(tm, s)

    def epilogue(prod, first, tin, tout):
        x_ref, gt_ref, g_ref, sc_ref, sh_ref = tin
        f_ref, xn_ref, h_ref = tout
        f_ref[...] = prod
        xn = x_ref[...] + (coef * gt_ref[...]) * prod
        xn_ref[...] = xn
        r = lax.rsqrt(jnp.mean(xn * xn, axis=-1, keepdims=True) + EPS)
        h_ref[...] = ((xn * r) * g_ref[...] * (1.0 + sc_ref[...]) + sh_ref[...]).astype(BF16)

    row, vec = _row_tile(D)(tm), _one(D)(1)
    return _mm(name, a, w, "NN", None, tm, D, tk, epilogue=epilogue,
               tiles_in=[(x, *row), (gt, *vec)] + [(v, *vec) for v in nxt],
               tiles_out=[(_sds((s, D), F32), *row), (_sds((s, D), F32), *row), (_sds((s, D), BF16), *row)])


def _out_loss(name, a, w, x, gt, coef, target, tm=512):
    s = a.shape[0]
    tm = min(tm, s)

    def epilogue(prod, first, tin, tout):
        x_ref, gt_ref, t_ref = tin
        f_ref, g_ref, df_ref, acc_ref = tout
        f_ref[...] = prod
        cg = coef * gt_ref[...]
        e = x_ref[...] + cg * prod - t_ref[...]
        gv = e * (1.0 / D)
        g_ref[...] = gv
        df_ref[...] = (cg * gv).astype(BF16)

        @pl.when(first)
        def _():
            acc_ref[...] = jnp.zeros_like(acc_ref)

        acc_ref[0:1, :] += coef * jnp.sum(gv * prod, axis=0, keepdims=True)
        acc_ref[1:2, :] += (0.5 / D) * jnp.sum(e * e, axis=0, keepdims=True)

    row, vec = _row_tile(D)(tm), _one(D)(1)
    return _mm(name, a, w, "NN", None, tm, D, FF, epilogue=epilogue,
               tiles_in=[(x, *row), (gt, *vec), (target, *row)],
               tiles_out=[(_sds((s, D), F32), *row), (_sds((s, D), F32), *row), (_sds((s, D), BF16), *row),
                          (_sds((8, D), F32), *_one(D)(8))])


def _d_h_norm_bwd(name, da, w, x, gin, g, sc, sh, before=None, carry=None, tm=256):
    s = da.shape[0]
    tm = min(tm, s)
    coef = before[2] if before else None

    def epilogue(prod, first, tin, tout):
        x_ref, gin_ref, g_ref, sc_ref, sh_ref = tin[:5]
        gout_ref, acc_ref = tout[:2]
        xv = x_ref[...]
        r = lax.rsqrt(jnp.mean(xv * xv, axis=-1, keepdims=True) + EPS)
        nv = xv * r
        gv, one_sc = g_ref[...], 1.0 + sc_ref[...]
        dn = prod * gv * one_sc
        gout = gin_ref[...] + r * (dn - nv * jnp.mean(dn * nv, axis=-1, keepdims=True))
        gout_ref[...] = gout

        @pl.when(first)
        def _():
            acc_ref[...] = jnp.zeros_like(acc_ref)

        dhn = prod * nv
        acc_ref[0:1, :] += jnp.sum(prod, axis=0, keepdims=True)
        acc_ref[1:2, :] += jnp.sum(dhn * gv, axis=0, keepdims=True)
        acc_ref[2:3, :] += jnp.sum(dhn * one_sc, axis=0, keepdims=True)
        if before:
            f_ref, gt_ref = tin[5:]
            tout[2][...] = ((coef * gt_ref[...]) * gout).astype(BF16)
            acc_ref[3:4, :] += coef * jnp.sum(gout * f_ref[...], axis=0, keepdims=True)

    row, vec = _row_tile(D)(tm), _one(D)(1)
    tiles_in = [(x, *row), (gin, *row), (g, *vec), (sc, *vec), (sh, *vec)]
    tiles_out = [(_sds((s, D), F32), *row), (_sds((8, D), F32), *_one(D)(8))]
    if before:
        tiles_in += [(before[0], *row), (before[1], *vec)]
        tiles_out.append((_sds((s, D), BF16), *row))
    return _mm(name, da, w, "NN", None, tm, D, da.shape[1], epilogue=epilogue, carry=carry, keep_b=True,
               tiles_in=tiles_in, tiles_out=tiles_out)


def _gate_tiles(proj, tm):
    return [(proj, (tm, COL), (lambda i, j, blk=blk: (i, blk))) for blk in (GA_BLK, GA_BLK + 1, GC_BLK, GC_BLK + 1)]


def _conv_branch_merge(z, wc, ya, proj, tm=512):
    s = z.shape[0]
    tm = min(tm, s)

    def epilogue(prod, first, tin, tout):
        ya_ref, ga0, ga1, gc0, gc1 = tin
        tout[0][...] = prod.astype(BF16)
        for half, (ga, gc) in enumerate(((ga0, gc0), (ga1, gc1))):
            cols = slice(half * COL, (half + 1) * COL)
            tout[1][:, cols] = (_sigmoid(ga[...].astype(F32)) * ya_ref[:, cols].astype(F32)
                                + _sigmoid(gc[...].astype(F32)) * prod[:, cols]).astype(BF16)

    row = _row_tile(D)(tm)
    return _mm("mix_conv_branch", z, wc, "NN", None, tm, D, D, epilogue=epilogue,
               tiles_in=[(ya, *row)] + _gate_tiles(proj, tm),
               tiles_out=[(_sds((s, D), BF16), *row), (_sds((s, D), BF16), *row)])


def _d_merged_branches(dmix, wo, proj, tm=512):
    s = dmix.shape[0]
    tm = min(tm, s)

    def epilogue(prod, first, tin, tout):
        ga0, ga1, gc0, gc1 = tin
        tout[0][...] = prod
        for half, (ga, gc) in enumerate(((ga0, gc0), (ga1, gc1))):
            cols = slice(half * COL, (half + 1) * COL)
            tout[1][:, cols] = (prod[:, cols] * _sigmoid(ga[...].astype(F32))).astype(BF16)
            tout[2][:, cols] = (prod[:, cols] * _sigmoid(gc[...].astype(F32))).astype(BF16)

    row = _row_tile(D)(tm)
    return _mm("mix_d_merged", dmix, wo, "NT", None, tm, D, D, epilogue=epilogue,
               tiles_in=_gate_tiles(proj, tm),
               tiles_out=[(_sds((s, D), F32), *row), (_sds((s, D), BF16), *row), (_sds((s, D), BF16), *row)])


def _d_o_delta(dya, wa_t, o, tm=1024):
    s = dya.shape[0]
    tm = min(tm, s)

    def epilogue(prod, first, tin, tout):
        tout[0][...] = prod
        tout[1][...] = _heads(prod * tin[0][...].astype(F32), lambda ph, h: jnp.broadcast_to(
            jnp.sum(ph, axis=-1, keepdims=True), ph.shape))

    row = _row_tile(COL)(tm)
    return _mm("mix_d_o", dya, wa_t, "NN", None, tm, COL, D, epilogue=epilogue,
               tiles_in=[(o, *row)], tiles_out=[(_sds((s, COL), F32), *row), (_sds((s, COL), F32), *row)])


def _ffn_bwd(tag, df, x, gin, h, ab, sw, g, sc, sh, wgu, wd, before=None, carry_down=None, carry_gate_up=None,
             tk_dw=2048):
    dab = _d_hidden_swiglu(f"{tag}_d_hidden", df, wd, ab)
    dwd = _mm(f"{tag}_dw_down", sw, df, "TN", BF16, HALF, D, tk_dw)
    carried = []
    if carry_down:
        dwgu, *got = _mm(f"{tag}_dw_gate_up", dab, h, "TN", BF16, HALF, D, tk_dw, carry=carry_down(dwd))
        carried += got
    else:
        dwgu = _mm(f"{tag}_dw_gate_up", dab, h, "TN", BF16, HALF, D, tk_dw)
    res = _d_h_norm_bwd(f"{tag}_d_h", dab, wgu, x, gin, g, sc, sh, before=before,
                        carry=carry_gate_up(dwgu) if carry_gate_up else None)
    n_own = 3 if before else 2
    return res[:n_own], dwgu, dwd, carried + list(res[n_own:])


def kernel(x, c, w_ada, b_ada, norm_ffn1, ffn1_w_gate, ffn1_w_up, ffn1_w_down, norm_mix, w_in, q_norm, k_norm, conv_w, w_attn_branch, w_conv_branch, w_out, norm_ffn2, ffn2_w_gate, ffn2_w_up, ffn2_w_down, loss_target, m_w_ada, m_b_ada, m_norm_ffn1, m_ffn1_w_gate, m_ffn1_w_up, m_ffn1_w_down, m_norm_mix, m_w_in, m_q_norm, m_k_norm, m_conv_w, m_w_attn_branch, m_w_conv_branch, m_w_out, m_norm_ffn2, m_ffn2_w_gate, m_ffn2_w_up, m_ffn2_w_down, v_w_ada, v_b_ada, v_norm_ffn1, v_ffn1_w_gate, v_ffn1_w_up, v_ffn1_w_down, v_norm_mix, v_w_in, v_q_norm, v_k_norm, v_conv_w, v_w_attn_branch, v_w_conv_branch, v_w_out, v_norm_ffn2, v_ffn2_w_gate, v_ffn2_w_up, v_ffn2_w_down):
    me = 4 * lax.axis_index("x") + 2 * lax.axis_index("y") + lax.axis_index("c")
    x0, target = x[0], loss_target[0]
    s = x0.shape[0]
    ada_cols = w_ada.shape[2]
    cw_cols = conv_w.shape[2]

    gathered = _small_allgather(
        "gather_c_conv", jnp.concatenate([c, conv_w[0].reshape(1, 3 * cw_cols)], axis=1))[:, 0]
    c_all = gathered[:, :D]
    cw = gathered[:, D:].reshape(N_DEV, 3, cw_cols).transpose(1, 0, 2).reshape(3, D)
    b_part = lax.dynamic_slice(b_ada, (0, me * ada_cols), (1, ada_cols))
    mod_part = _mod_part(c_all, w_ada[0], b_part)
    mod_all = _small_allgather("gather_mod", mod_part.reshape(1, N_DEV * ada_cols))
    mod = lax.dynamic_slice(mod_all.reshape(N_DEV, N_DEV, ada_cols), (0, me, 0), (N_DEV, 1, ada_cols))
    mod = mod.reshape(N_MOD, 1, D)
    sh1, sc1, gt1, sh2, sc2, gt2, sh3, sc3, gt3 = [mod[i] for i in range(N_MOD)]

    tb = lambda w: w[0].T.astype(BF16)
    nb = lambda w: w[0].astype(BF16)
    ffn1_shards = [tb(ffn1_w_gate), tb(ffn1_w_up), nb(ffn1_w_down)]
    ffn2_shards = [tb(ffn2_w_gate), tb(ffn2_w_up), nb(ffn2_w_down)]
    mix_shards = [tb(w_in), tb(w_attn_branch), nb(w_conv_branch), nb(w_out)]
    ffn_dst, ffn_base, ffn_jump, ffn_shapes = [0, 0, 1], [0, HALF, 0], [HALF, HALF, 0], [(2 * FF, D), (FF, D)]
    mix_dst, mix_base, mix_shapes = [0, 1, 2, 3], [0, 0, 0, 0], [(IN_W, D), (D, COL), (D, D), (D, D)]
    wgu1, wd1 = _run_plan("gather_ffn1_weights",
                          _gather_plan(ffn1_shards, ffn_dst, ffn_base, ffn_shapes, ffn_jump))

    h1 = _normmod("ffn1_normmod", x0, norm_ffn1, sc1, sh1)
    ab1, s1, win_t = _gate_up_swiglu(
        "ffn1_gate_up", h1, wgu1, carry=_gather_plan(mix_shards[:1], mix_dst[:1], mix_base[:1], mix_shapes[:1]))
    f1, x1, h2 = _out_residual("ffn1_down", s1, wd1, x0, gt1, 0.5, (norm_mix, sc2, sh2))
    proj, wgu2, wd2, wa_t, wc, wo = _mm(
        "mix_in_proj", h2, win_t, "NT", BF16, 512, IN_W // 4, D, n_outer=True,
        carry=_gather_plan(ffn2_shards + mix_shards[1:], ffn_dst + [2, 3, 4], ffn_base + [0, 0, 0],
                           ffn_shapes + mix_shapes[1:], ffn_jump + [0, 0, 0]))
    wqk = jnp.concatenate([jnp.tile(q_norm, (1, 12)), jnp.tile(k_norm, (1, 12))], axis=1)
    qkn = _qknorm(proj, wqk)
    group_out = [_attn_fwd(g, qkn, proj) for g in range(3)]
    o, lse = _attn_combine([go[0] for go in group_out], [go[1] for go in group_out])
    ya = _mm("mix_attn_branch", o, wa_t, "NT", BF16, 1024, 1024, COL)
    z = _conv_fwd(proj, cw)
    yc, merged = _conv_branch_merge(z, wc, ya, proj)
    mix, x2, h3 = _out_residual("mix_out_proj", merged, wo, x1, gt2, 1.0, (norm_ffn2, sc3, sh3), tk=D)
    ab3, s3 = _gate_up_swiglu("ffn2_gate_up", h3, wgu2)
    f3, g3, df3, acc_out = _out_loss("ffn2_down", s3, wd2, x2, gt3, 0.5, target)
    loss = lax.psum(jnp.sum(acc_out[1]), ("x", "y", "c"))

    ffn_rows = [sh_.shape[0] for sh_ in ffn1_shards]
    mix_rows = [sh_.shape[0] for sh_ in mix_shards]
    (g2, acc3, dmix), dwgu2, dwd2, _ = _ffn_bwd(
        "ffn2", df3, x2, g3, h3, ab3, s3, norm_ffn2, sc3, sh3, wgu2, wd2, before=(mix, gt2, 1.0))
    dmerged, dya, dyc = _d_merged_branches(dmix, wo, proj)
    dwo = _mm("mix_dw_out", merged, dmix, "TN", BF16, 1024, 1024, 2048)
    dproj = _merge_bwd_gates(dmerged, ya, yc, proj)
    dwc = _mm("mix_dw_conv_branch", z, dyc, "TN", BF16, 1024, 1024, 2048)
    dz = _mm("mix_d_z", dyc, wc, "NT", F32, 1024, 1024, D)
    dproj, cw_acc = _conv_bwd(dz, proj, cw, dproj)
    dwa_t = _mm("mix_dw_attn_branch", dya, o, "TN", BF16, 1024, COL, 2048)
    do, delta = _d_o_delta(dya, wa_t, o)
    dqkn = None
    for g in range(3):
        dqkn = _attn_dq(g, qkn, proj, do, lse, delta, dqkn)
    for g in range(3):
        dqkn, dproj = _attn_dkv(g, qkn, proj, do, lse, delta, dqkn, dproj)
    dproj, wqk_acc = _qknorm_bwd(proj, dqkn, wqk, dproj)
    dwin_t, r_f2g, r_f2u, r_f2d, r_wa, r_wc, r_wo = _mm(
        "mix_dw_in", dproj, h2, "TN", BF16, IN_W // 4, D, 1024,
        carry=_scatter_plan([dwgu2, dwd2, dwa_t, dwc, dwo], [0, 0, 1, 2, 3, 4], [0, HALF, 0, 0, 0, 0],
                            ffn_rows + mix_rows[1:], [D, D, D, COL, D, D], [HALF, HALF, 0, 0, 0, 0]))
    g1, acc2, df1, r_win = _d_h_norm_bwd(
        "mix_d_h", dproj, win_t, x1, g2, norm_mix, sc2, sh2, before=(f1, gt1, 0.5),
        carry=_scatter_plan([dwin_t], [0], [0], mix_rows[:1], [D]))
    (g0, acc1), dwgu1, dwd1, (r_f1d, r_f1g, r_f1u) = _ffn_bwd(
        "ffn1", df1, x0, g1, h1, ab1, s1, norm_ffn1, sc1, sh1, wgu1, wd1,
        carry_down=lambda dwd: _scatter_plan([dwd], [0], [0], ffn_rows[2:], [D]),
        carry_gate_up=lambda dwgu: _scatter_plan([dwgu], [0, 0], [0, HALF], ffn_rows[:2], [D, D], [HALF, HALF]))

    dqw = jnp.sum(wqk_acc[0, :QKW // 2].reshape(12, HD), axis=0)
    dkw = jnp.sum(wqk_acc[0, QKW // 2:].reshape(12, HD), axis=0)
    small = jnp.concatenate([
        acc1[0], acc1[1], acc2[3], acc2[0], acc2[1], acc3[3], acc3[0], acc3[1], acc_out[0],
        acc1[2], acc2[2], acc3[2], dqw, dkw, cw_acc[0:3].reshape(3 * D)]).reshape(1, -1)
    small_all = _small_allgather("gather_small_grads", small)
    small_sum = _sum_rows("sum_small_grads", small_all)[0]
    n_mod = N_MOD * D
    g_b_ada = small_sum[:n_mod].reshape(1, n_mod)
    g_norm1, g_norm2, g_norm3 = [small_sum[n_mod + i * D:n_mod + (i + 1) * D].reshape(1, D) for i in range(3)]
    off = n_mod + 3 * D
    g_qn, g_kn = small_sum[off:off + HD].reshape(1, HD), small_sum[off + HD:off + 2 * HD].reshape(1, HD)
    g_cw_full = small_sum[off + 2 * HD:].reshape(3, D)
    g_cw = lax.dynamic_slice(g_cw_full, (0, me * cw_cols), (3, cw_cols))
    dmod_part = lax.dynamic_slice(small_all[:, 0, :n_mod], (0, me * ada_cols), (N_DEV, ada_cols))
    g_w_ada = _w_ada_grad(c_all.T, dmod_part)

    recvs = [r_f1g, r_f1u, r_f1d, r_f2g, r_f2u, r_f2d, r_win, r_wa, r_wc, r_wo]
    names = ["ffn1_gate", "ffn1_up", "ffn1_down", "ffn2_gate", "ffn2_up", "ffn2_down",
             "w_in", "attn_branch", "conv_branch", "w_out"]
    sums = [_sum_contributions(f"sum_{nm}", r) for nm, r in zip(names, recvs)]
    g_f1g, g_f1u, g_f1d, g_f2g, g_f2u, g_f2d, g_win, g_wa, g_wc, g_wo = sums

    as_rows = {"ffn1_w_gate", "ffn1_w_up", "w_in", "ffn2_w_gate", "ffn2_w_up"}
    grad_list = [g_w_ada[None], g_b_ada, g_norm1, g_f1g, g_f1u, g_f1d[None], g_norm2, g_win,
                 g_qn, g_kn, g_cw[None], g_wa.T[None], g_wc[None], g_wo[None], g_norm3,
                 g_f2g, g_f2u, g_f2d[None]]
    weights = [w_ada, b_ada, norm_ffn1, ffn1_w_gate, ffn1_w_up, ffn1_w_down, norm_mix, w_in, q_norm, k_norm,
               conv_w, w_attn_branch, w_conv_branch, w_out, norm_ffn2, ffn2_w_gate, ffn2_w_up, ffn2_w_down]
    ms = [m_w_ada, m_b_ada, m_norm_ffn1, m_ffn1_w_gate, m_ffn1_w_up, m_ffn1_w_down, m_norm_mix, m_w_in, m_q_norm,
          m_k_norm, m_conv_w, m_w_attn_branch, m_w_conv_branch, m_w_out, m_norm_ffn2, m_ffn2_w_gate,
          m_ffn2_w_up, m_ffn2_w_down]
    vs = [v_w_ada, v_b_ada, v_norm_ffn1, v_ffn1_w_gate, v_ffn1_w_up, v_ffn1_w_down, v_norm_mix, v_w_in, v_q_norm,
          v_k_norm, v_conv_w, v_w_attn_branch, v_w_conv_branch, v_w_out, v_norm_ffn2, v_ffn2_w_gate,
          v_ffn2_w_up, v_ffn2_w_down]
    wnames = ["w_ada", "b_ada", "norm_ffn1", "ffn1_w_gate", "ffn1_w_up", "ffn1_w_down", "norm_mix", "w_in",
              "q_norm", "k_norm", "conv_w", "w_attn_branch", "w_conv_branch", "w_out", "norm_ffn2",
              "ffn2_w_gate", "ffn2_w_up", "ffn2_w_down"]
    grad_out, deltas, new_ms, new_vs = [], [], [], []
    for nm, w, gr, m_, v_ in zip(wnames, weights, grad_list, ms, vs):
        if nm in as_rows:
            res = _adamw(f"adamw_{nm}", w[0].T, gr, m_[0].T, v_[0].T)
            gr, dl, nm_, nv_ = [r.T[None] for r in (gr, *res)]
        else:
            two_d = (-1, w.shape[-1])
            res = _adamw(f"adamw_{nm}", w.reshape(two_d), gr.reshape(two_d), m_.reshape(two_d), v_.reshape(two_d))
            gr, dl, nm_, nv_ = [r.reshape(w.shape) for r in (gr, *res)]
        grad_out.append(gr)
        deltas.append(dl)
        new_ms.append(nm_)
        new_vs.append(nv_)
    return (loss, g0[None], *grad_out, *deltas, *new_ms, *new_vs)
```

```python
import functools

import jax
import jax.numpy as jnp
from jax import lax
from jax.experimental import pallas as pl
from jax.experimental.pallas import tpu as pltpu

F32 = jnp.float32
BF16 = jnp.bfloat16
MESH = pl.DeviceIdType.MESH

N_DEV = 8
D = 1024
FF = 2816
HD = 128
N_HEADS = 4
DILATIONS = (1, 4, 16)
BAND = 128
QKW = 2 * 3 * N_HEADS * HD
IN_W = 9728
COL = 512
V_BLK, U_BLK, B_BLK, C_BLK, GA_BLK, GC_BLK = 6, 9, 11, 13, 15, 17
EPS = 1e-6
N_MOD = 9
ADAM_LR, ADAM_B1, ADAM_B2, ADAM_EPS, ADAM_WD, ADAM_STEP = 0.001, 0.9, 0.999, 1e-08, 0.01, 10

NT_DIMS = (((1,), (1,)), ((), ()))
TN_DIMS = (((0,), (0,)), ((), ()))
NN_DIMS = (((1,), (0,)), ((), ()))


def _place():
    return lax.axis_index("x"), lax.axis_index("y"), lax.axis_index("c")


def _flip(coord, bit):
    return 1 - coord if bit else coord


def _params(*sem):
    return pltpu.CompilerParams(dimension_semantics=sem)


def _small_allgather(name, v):
    n = v.shape[-1]

    def body(v_ref, out_ref, send_sems, recv_sems):
        x, y, c = _place()
        me = 4 * x + 2 * y + c
        out_ref[me] = v_ref[...]
        copies = []
        for k in range(1, N_DEV):
            peer = (_flip(x, (k >> 2) & 1), _flip(y, (k >> 1) & 1), _flip(c, k & 1))
            cp = pltpu.make_async_remote_copy(
                src_ref=v_ref, dst_ref=out_ref.at[me], send_sem=send_sems.at[k - 1],
                recv_sem=recv_sems.at[k - 1], device_id=peer, device_id_type=MESH)
            cp.start()
            copies.append(cp)
        for cp in copies:
            cp.wait()

    return pl.pallas_call(
        body, name=name,
        out_shape=jax.ShapeDtypeStruct((N_DEV, 1, n), F32),
        in_specs=[pl.BlockSpec(memory_space=pltpu.VMEM)],
        out_specs=pl.BlockSpec(memory_space=pltpu.VMEM),
        scratch_shapes=[pltpu.SemaphoreType.DMA((N_DEV - 1,)), pltpu.SemaphoreType.DMA((N_DEV - 1,))],
    )(v)


class _Plan:
    def __init__(self, operands, out_shapes, sems, phases):
        self.operands, self.out_shapes, self.sems, self.phases = operands, out_shapes, sems, phases


def _slab_start(base, rows, jump, idx):
    return pl.multiple_of(base + idx * rows + (idx // 4) * jump, 16)


def _gather_plan(shards, dst_of, base_of, dst_shapes, jump_of=None):
    n = len(shards)
    rows = [s.shape[0] for s in shards]
    jump_of = jump_of or [0] * n

    def phases(srcs, dsts, sems):
        send_sems, recv_sems, local_sems = sems
        x, y, c = _place()
        me, sibling = (x, y, c), (x, y, 1 - c)
        chips = [(1 - x, y), (x, 1 - y), (1 - x, 1 - y)]

        def slab(i, px, py, pc):
            start = _slab_start(base_of[i], rows[i], jump_of[i], 4 * px + 2 * py + pc)
            return dsts[dst_of[i]].at[pl.ds(start, rows[i])]

        def copy(i, k, block, to, src=None):
            return pltpu.make_async_remote_copy(
                src_ref=slab(i, *block) if src is None else src, dst_ref=slab(i, *block),
                send_sem=send_sems.at[i, k], recv_sem=recv_sems.at[i, k],
                device_id=to, device_id_type=MESH)

        def mine():
            return [pltpu.make_async_copy(srcs[i], slab(i, *me), local_sems.at[i]) for i in range(n)]

        def first():
            out = []
            for i in range(n):
                out.append(copy(i, 0, me, sibling, src=srcs[i]))
                out += [copy(i, 1 + j, me, (*chip, c), src=srcs[i]) for j, chip in enumerate(chips)]
            return out

        def passed():
            return [(copy(i, 1 + j, (*chip, c), me), copy(i, 4 + j, (*chip, c), sibling))
                    for j, chip in enumerate(chips) for i in range(n)]

        def start():
            for cp in mine() + first():
                cp.start()

        def middle():
            for landed, onward in passed():
                landed.wait_recv()
                onward.start()

        def finish():
            for i in range(n):
                copy(i, 0, sibling, me).wait_recv()
                for j, chip in enumerate(chips):
                    copy(i, 4 + j, (*chip, 1 - c), me).wait_recv()
            for cp in first() + [onward for _, onward in passed()]:
                cp.wait_send()
            for cp in mine():
                cp.wait()

        return start, middle, finish

    sems = [pltpu.SemaphoreType.DMA((n, 7)), pltpu.SemaphoreType.DMA((n, 7)), pltpu.SemaphoreType.DMA((n,))]
    return _Plan(list(shards), [jax.ShapeDtypeStruct(s, BF16) for s in dst_shapes], sems, phases)


def _scatter_plan(grads, src_of, base_of, rows, cols, jump_of=None):
    n = len(rows)
    jump_of = jump_of or [0] * n

    def phases(srcs, recvs, sems):
        send_sems, recv_sems, local_sems = sems
        x, y, c = _place()
        me = 4 * x + 2 * y + c

        def slab(i, idx):
            start = _slab_start(base_of[i], rows[i], jump_of[i], idx)
            return srcs[src_of[i]].at[pl.ds(start, rows[i])]

        def copies():
            out = [pltpu.make_async_copy(slab(i, me), recvs[i].at[me], local_sems.at[i]) for i in range(n)]
            for k in range(1, N_DEV):
                px, py, pc = _flip(x, (k >> 2) & 1), _flip(y, (k >> 1) & 1), _flip(c, k & 1)
                out += [pltpu.make_async_remote_copy(
                    src_ref=slab(i, 4 * px + 2 * py + pc), dst_ref=recvs[i].at[me],
                    send_sem=send_sems.at[i, k - 1], recv_sem=recv_sems.at[i, k - 1],
                    device_id=(px, py, pc), device_id_type=MESH) for i in range(n)]
            return out

        def start():
            for cp in copies():
                cp.start()

        def finish():
            for cp in copies():
                cp.wait()

        return start, None, finish

    sems = [pltpu.SemaphoreType.DMA((n, 7)), pltpu.SemaphoreType.DMA((n, 7)), pltpu.SemaphoreType.DMA((n,))]
    out_shapes = [jax.ShapeDtypeStruct((N_DEV, rows[i], cols[i]), BF16) for i in range(n)]
    return _Plan(list(grads), out_shapes, sems, phases)


def _run_plan(name, plan):
    n_in, n_out = len(plan.operands), len(plan.out_shapes)

    def body(*refs):
        for phase in plan.phases(refs[:n_in], refs[n_in:n_in + n_out], refs[n_in + n_out:]):
            if phase is not None:
                phase()

    hbm = pl.BlockSpec(memory_space=pltpu.HBM)
    return pl.pallas_call(
        body, name=name, out_shape=plan.out_shapes,
        in_specs=[hbm] * n_in, out_specs=[hbm] * n_out, scratch_shapes=plan.sems,
    )(*plan.operands)


def _sum_contributions(name, recv):
    _, rows, cols = recv.shape
    tr = rows if rows <= 512 else 304 if rows % 304 == 0 else 256

    def body(r_ref, o_ref):
        acc = r_ref[0].astype(F32)
        for k in range(1, N_DEV):
            acc = acc + r_ref[k].astype(F32)
        o_ref[...] = acc

    return pl.pallas_call(
        body, name=name, grid=(rows // tr,),
        out_shape=jax.ShapeDtypeStruct((rows, cols), F32),
        in_specs=[pl.BlockSpec((N_DEV, tr, cols), lambda i: (0, i, 0))],
        out_specs=pl.BlockSpec((tr, cols), lambda i: (i, 0)),
        compiler_params=_params("parallel"),
    )(recv)


def _mm(name, a, b, mode, out_dtype, tm, tn, tk, *, carry=None, tiles_in=(), tiles_out=(), epilogue=None,
        n_outer=False, keep_b=False, sub=1):
    if mode == "TN":
        kk, m = a.shape
    else:
        m, kk = a.shape
    n = b.shape[0] if mode == "NT" else b.shape[1]
    tm, tn, tk = min(tm, m), min(tn, n), min(tk, kk)
    assert m % tm == 0 and n % tn == 0 and kk % tk == 0, (name, m, n, kk, tm, tn, tk)
    ni, nj, nk = m // tm, n // tn, kk // tk
    steps = ni * nj * nk
    dims = {"NN": NN_DIMS, "NT": NT_DIMS, "TN": TN_DIMS}[mode]
    if epilogue is None:
        tiles_out = [(jax.ShapeDtypeStruct((m, n), out_dtype), (tm, tn), lambda i, j: (i, j))]
    n_tin, n_tout = len(tiles_in), len(tiles_out)
    n_in = len(carry.operands) if carry else 0
    n_out = len(carry.out_shapes) if carry else 0
    n_acc = 1 if nk > 1 else 0
    n_keep = 2 if keep_b else 0
    assert not carry or steps >= 3
    assert not keep_b or (nk == 1 and nj == 1)
    assert sub == 1 or (mode != "TN" and tm % (8 * sub) == 0)
    ij = (lambda p, q: (q, p)) if n_outer else (lambda p, q: (p, q))
    inner = ni if n_outer else nj
    rs = tm // sub

    def body(a_ref, b_ref, *rest):
        tin = rest[:n_tin]
        cin = rest[n_tin:n_tin + n_in]
        tout = rest[n_tin + n_in:n_tin + n_in + n_tout]
        cout = rest[n_tin + n_in + n_tout:n_tin + n_in + n_tout + n_out]
        scratch = rest[n_tin + n_in + n_tout + n_out:]
        k = pl.program_id(2)
        visit = pl.program_id(0) * inner + pl.program_id(1)
        step = visit * nk + k
        if keep_b:
            b_kept, b_sem = scratch[n_acc:n_acc + 2]

            @pl.when(step == 0)
            def _():
                cp = pltpu.make_async_copy(b_ref, b_kept, b_sem)
                cp.start()
                cp.wait()

            b_ref = b_kept
        if carry:
            start, middle, finish = carry.phases(cin, cout, scratch[n_acc + n_keep:])
            pl.when(step == 0)(start)

        def rows_of(refs, specs, c):
            if sub == 1:
                return refs
            return [r.at[pl.ds(c * rs, rs)] if t[1][0] == tm else r for r, t in zip(refs, specs)]

        for c in range(sub):
            a_rows = a_ref[...] if sub == 1 else a_ref[pl.ds(c * rs, rs), :]
            part = lax.dot_general(a_rows, b_ref[...], dims, preferred_element_type=F32)
            tin_c, tout_c = rows_of(tin, tiles_in, c), rows_of(tout, tiles_out, c)

            def store(prod, tin_c=tin_c, tout_c=tout_c, c=c):
                if epilogue is None:
                    tout_c[0][...] = prod.astype(out_dtype)
                else:
                    epilogue(prod, jnp.logical_and(visit == 0, c == 0), tin_c, tout_c)

            if nk == 1:
                store(part)
            else:
                acc_c = scratch[0] if sub == 1 else scratch[0].at[pl.ds(c * rs, rs)]

                @pl.when(k == 0)
                def _(acc_c=acc_c, part=part):
                    acc_c[...] = part

                @pl.when((k > 0) & (k < nk - 1))
                def _(acc_c=acc_c, part=part):
                    acc_c[...] += part

                @pl.when(k == nk - 1)
                def _(acc_c=acc_c, part=part, store=store):
                    store(acc_c[...] + part)

        if carry:
            if middle is not None:
                pl.when(step == (steps * 3) // 5)(middle)
            pl.when(step == steps - 1)(finish)

    def spec(shape, fn):
        return pl.BlockSpec(shape, lambda p, q, k: fn(*ij(p, q)))

    a_spec = (pl.BlockSpec((tk, tm), lambda p, q, k: (k, ij(p, q)[0])) if mode == "TN"
              else pl.BlockSpec((tm, tk), lambda p, q, k: (ij(p, q)[0], k)))
    if keep_b:
        b_spec = pl.BlockSpec(memory_space=pl.ANY)
    elif mode == "NT":
        b_spec = pl.BlockSpec((tn, tk), lambda p, q, k: (ij(p, q)[1], k))
    else:
        b_spec = pl.BlockSpec((tk, tn), lambda p, q, k: (k, ij(p, q)[1]))
    hbm = pl.BlockSpec(memory_space=pltpu.HBM)
    sequential = carry or epilogue or keep_b
    out = pl.pallas_call(
        body, name=name, grid=(nj, ni, nk) if n_outer else (ni, nj, nk),
        out_shape=[t[0] for t in tiles_out] + (carry.out_shapes if carry else []),
        in_specs=[a_spec, b_spec] + [spec(t[1], t[2]) for t in tiles_in] + [hbm] * n_in,
        out_specs=[spec(t[1], t[2]) for t in tiles_out] + [hbm] * n_out,
        scratch_shapes=([pltpu.VMEM((tm, tn), F32)] * n_acc
                        + ([pltpu.VMEM(b.shape, b.dtype), pltpu.SemaphoreType.DMA] if keep_b else [])
                        + (carry.sems if carry else [])),
        compiler_params=(_params("arbitrary", "arbitrary", "arbitrary") if sequential
                         else _params("parallel", "parallel", "arbitrary")),
    )(a, b, *[t[0] for t in tiles_in], *(carry.operands if carry else []))
    return out if (carry or epilogue) else out[0]


def _row(tm, w, off=0):
    return pl.BlockSpec((tm, w), lambda i: (i, off))


def _vec(w):
    return pl.BlockSpec((1, w), lambda i: (0, 0))


def _sigmoid(x):
    return 0.5 * jnp.tanh(0.5 * x) + 0.5


def _normmod(name, x, g, sc, sh, tm=512):
    s = x.shape[0]

    def body(x_ref, g_ref, sc_ref, sh_ref, h_ref):
        xv = x_ref[...]
        r = lax.rsqrt(jnp.mean(xv * xv, axis=-1, keepdims=True) + EPS)
        h_ref[...] = ((xv * r) * g_ref[...] * (1.0 + sc_ref[...]) + sh_ref[...]).astype(BF16)

    return pl.pallas_call(
        body, name=name, grid=(s // tm,),
        out_shape=jax.ShapeDtypeStruct((s, D), BF16),
        in_specs=[_row(tm, D), _vec(D), _vec(D), _vec(D)], out_specs=_row(tm, D),
        compiler_params=_params("parallel"),
    )(x, g, sc, sh)


def _normmod_bwd(name, dh, x, gin, g, sc, sh, tm=512):
    s = x.shape[0]

    def body(dh_ref, x_ref, gin_ref, g_ref, sc_ref, sh_ref, gout_ref, acc_ref):
        xv, dhv = x_ref[...], dh_ref[...]
        r = lax.rsqrt(jnp.mean(xv * xv, axis=-1, keepdims=True) + EPS)
        nv = xv * r
        gv, one_sc = g_ref[...], 1.0 + sc_ref[...]
        dn = dhv * gv * one_sc
        dx = r * (dn - nv * jnp.mean(dn * nv, axis=-1, keepdims=True))
        gout_ref[...] = gin_ref[...] + dx

        @pl.when(pl.program_id(0) == 0)
        def _():
            acc_ref[...] = jnp.zeros_like(acc_ref)

        dhn = dhv * nv
        acc_ref[0:1, :] += jnp.sum(dhv, axis=0, keepdims=True)
        acc_ref[1:2, :] += jnp.sum(dhn * gv, axis=0, keepdims=True)
        acc_ref[2:3, :] += jnp.sum(dhn * one_sc, axis=0, keepdims=True)

    return pl.pallas_call(
        body, name=name, grid=(s // tm,),
        out_shape=[jax.ShapeDtypeStruct((s, D), F32), jax.ShapeDtypeStruct((8, D), F32)],
        in_specs=[_row(tm, D), _row(tm, D), _row(tm, D), _vec(D), _vec(D), _vec(D)],
        out_specs=[_row(tm, D), pl.BlockSpec((8, D), lambda i: (0, 0))],
        compiler_params=_params("arbitrary"),
    )(dh, x, gin, g, sc, sh)


def _swiglu(name, ab, tm=512):
    s = ab.shape[0]

    def body(ab_ref, s_ref):
        a = ab_ref[:, :FF].astype(F32)
        b = ab_ref[:, FF:].astype(F32)
        s_ref[...] = (a * _sigmoid(a) * b).astype(BF16)

    return pl.pallas_call(
        body, name=name, grid=(s // tm,),
        out_shape=jax.ShapeDtypeStruct((s, FF), BF16),
        in_specs=[_row(tm, 2 * FF)], out_specs=_row(tm, FF),
        compiler_params=_params("parallel"),
    )(ab)


def _swiglu_bwd(name, ds, ab, tm=256):
    s = ab.shape[0]

    def body(ds_ref, ab_ref, dab_ref):
        a = ab_ref[:, :FF].astype(F32)
        b = ab_ref[:, FF:].astype(F32)
        dsv = ds_ref[...].astype(F32)
        sig = _sigmoid(a)
        dab_ref[:, :FF] = (dsv * b * (sig * (1.0 + a * (1.0 - sig)))).astype(BF16)
        dab_ref[:, FF:] = (dsv * (a * sig)).astype(BF16)

    return pl.pallas_call(
        body, name=name, grid=(s // tm,),
        out_shape=jax.ShapeDtypeStruct((s, 2 * FF), BF16),
        in_specs=[_row(tm, FF), _row(tm, 2 * FF)], out_specs=_row(tm, 2 * FF),
        compiler_params=_params("parallel"),
    )(ds, ab)


def _residual(name, x, f, gt, coef, tm=512):
    s = x.shape[0]

    def body(x_ref, f_ref, gt_ref, o_ref):
        o_ref[...] = x_ref[...] + (coef * gt_ref[...]) * f_ref[...]

    return pl.pallas_call(
        body, name=name, grid=(s // tm,),
        out_shape=jax.ShapeDtypeStruct((s, D), F32),
        in_specs=[_row(tm, D), _row(tm, D), _vec(D)], out_specs=_row(tm, D),
        compiler_params=_params("parallel"),
    )(x, f, gt)


def _gate_bwd(name, gin, f, gt, coef, tm=512):
    s = gin.shape[0]

    def body(g_ref, f_ref, gt_ref, df_ref, acc_ref):
        gv = g_ref[...]
        df_ref[...] = ((coef * gt_ref[...]) * gv).astype(BF16)

        @pl.when(pl.program_id(0) == 0)
        def _():
            acc_ref[...] = jnp.zeros_like(acc_ref)

        acc_ref[0:1, :] += coef * jnp.sum(gv * f_ref[...], axis=0, keepdims=True)

    return pl.pallas_call(
        body, name=name, grid=(s // tm,),
        out_shape=[jax.ShapeDtypeStruct((s, D), BF16), jax.ShapeDtypeStruct((8, D), F32)],
        in_specs=[_row(tm, D), _row(tm, D), _vec(D)],
        out_specs=[_row(tm, D), pl.BlockSpec((8, D), lambda i: (0, 0))],
        compiler_params=_params("arbitrary"),
    )(gin, f, gt)


def _loss_grad(x3, target, tm=512):
    s = x3.shape[0]

    def body(y_ref, t_ref, g_ref, l_ref):
        e = y_ref[...] - t_ref[...]
        g_ref[...] = e * (1.0 / D)

        @pl.when(pl.program_id(0) == 0)
        def _():
            l_ref[...] = jnp.zeros_like(l_ref)

        l_ref[...] += jnp.sum(jnp.mean(e * e, axis=-1, keepdims=True), axis=0, keepdims=True) * 0.5

    return pl.pallas_call(
        body, name="loss_grad", grid=(s // tm,),
        out_shape=[jax.ShapeDtypeStruct((s, D), F32), jax.ShapeDtypeStruct((8, 128), F32)],
        in_specs=[_row(tm, D), _row(tm, D)],
        out_specs=[_row(tm, D), pl.BlockSpec((8, 128), lambda i: (0, 0))],
        compiler_params=_params("arbitrary"),
    )(x3, target)


def _heads(x, fn):
    return jnp.concatenate([fn(x[:, h * HD:(h + 1) * HD], h) for h in range(COL // HD)], axis=1)


def _qknorm(proj, wqk, tm=1024):
    s = proj.shape[0]

    def body(p_ref, w_ref, o_ref):
        pv = p_ref[...].astype(F32)
        wv = w_ref[...]

        def one(qh, h):
            r = lax.rsqrt(jnp.mean(qh * qh, axis=-1, keepdims=True) + EPS)
            return (qh * r) * wv[:, h * HD:(h + 1) * HD]

        o_ref[...] = _heads(pv, one).astype(BF16)

    return pl.pallas_call(
        body, name="qknorm", grid=(s // tm, QKW // COL),
        out_shape=jax.ShapeDtypeStruct((s, QKW), BF16),
        in_specs=[pl.BlockSpec((tm, COL), lambda i, j: (i, j)), pl.BlockSpec((1, COL), lambda i, j: (0, j))],
        out_specs=pl.BlockSpec((tm, COL), lambda i, j: (i, j)),
        compiler_params=_params("parallel", "parallel"),
    )(proj, wqk)


def _qknorm_bwd(proj, dqkn, wqk, dproj, tm=1024):
    s = proj.shape[0]

    def body(p_ref, d_ref, w_ref, _, o_ref, acc_ref):
        pv = p_ref[...].astype(F32)
        dv = d_ref[...]
        wv = w_ref[...]
        sums = []

        def one(qh, h):
            dn = dv[:, h * HD:(h + 1) * HD]
            r = lax.rsqrt(jnp.mean(qh * qh, axis=-1, keepdims=True) + EPS)
            nh = qh * r
            sums.append(jnp.sum(dn * nh, axis=0, keepdims=True))
            dnw = dn * wv[:, h * HD:(h + 1) * HD]
            return r * (dnw - nh * jnp.mean(dnw * nh, axis=-1, keepdims=True))

        o_ref[...] = _heads(pv, one).astype(BF16)

        @pl.when(pl.program_id(1) == 0)
        def _():
            acc_ref[...] = jnp.zeros_like(acc_ref)

        acc_ref[0:1, :] += jnp.concatenate(sums, axis=1)

    return pl.pallas_call(
        body, name="qknorm_bwd", grid=(QKW // COL, s // tm),
        out_shape=[jax.ShapeDtypeStruct((s, IN_W), BF16), jax.ShapeDtypeStruct((8, QKW), F32)],
        in_specs=[pl.BlockSpec((tm, COL), lambda j, i: (i, j)), pl.BlockSpec((tm, COL), lambda j, i: (i, j)),
                  pl.BlockSpec((1, COL), lambda j, i: (0, j)), pl.BlockSpec(memory_space=pl.ANY)],
        out_specs=[pl.BlockSpec((tm, COL), lambda j, i: (i, j)), pl.BlockSpec((8, COL), lambda j, i: (0, j))],
        input_output_aliases={3: 0},
        compiler_params=_params("arbitrary", "arbitrary"),
    )(proj, dqkn, wqk, dproj)


def _attn_shapes(s, g):
    d = DILATIONS[g]
    tb = min(s, max(2048, 256 * d))
    sb = min(256, tb // d)
    pb = BAND * d
    assert s % tb == 0 and tb % pb == 0 and (tb // d) % sb == 0 and sb % BAND == 0
    return d, tb, sb, pb


def _lanes(x, width):
    return jnp.concatenate([x] * (width // HD), axis=1)


def _every(start, size, d):
    return pl.ds(start, size, stride=d) if d > 1 else pl.ds(start, size)


def _attn_specs(g, tb, pb, s, ahead):
    ratio = tb // pb
    if ahead:
        nbr = lambda n: jnp.minimum((n + 1) * ratio, s // pb - 1)
    else:
        nbr = lambda n: jnp.maximum(n * ratio - 1, 0)
    cur = lambda base: pl.BlockSpec((tb, HD), lambda h, n: (n, base + g * N_HEADS + h))
    side = lambda base: pl.BlockSpec((pb, HD), lambda h, n: (nbr(n), base + g * N_HEADS + h))
    tok = pl.BlockSpec((tb, HD), lambda h, n: (n, h))
    tok_side = pl.BlockSpec((pb, HD), lambda h, n: (nbr(n), h))
    return cur, side, tok, tok_side


Q_COL, K_COL, V_COL = 0, 12, 24


def _attn_fwd(g, qkn, proj):
    s = qkn.shape[0]
    d, tb, sb, pb = _attn_shapes(s, g)
    ft = F32 if d > 1 else BF16
    nj = tb // d // sb
    scale = HD ** -0.5

    def body(q_ref, kc_ref, kp_ref, vc_ref, vp_ref, o_ref, lse_ref, qf, kf, vf):
        n = pl.program_id(1)
        qf[...] = q_ref[...].astype(ft)
        kf[0:pb] = kp_ref[...].astype(ft)
        kf[pb:] = kc_ref[...].astype(ft)
        vf[0:pb] = vp_ref[...].astype(ft)
        vf[pb:] = vc_ref[...].astype(ft)
        for r in range(d):
            for j in range(nj):
                at = j * sb * d + r
                q = qf[_every(at, sb, d), :].astype(BF16)
                k = kf[_every(at, sb + BAND, d), :].astype(BF16)
                v = vf[_every(at, sb + BAND, d), :].astype(BF16)
                sc = lax.dot_general(q, k, NT_DIMS, preferred_element_type=F32) * scale
                qi = lax.broadcasted_iota(jnp.int32, sc.shape, 0)
                kj = lax.broadcasted_iota(jnp.int32, sc.shape, 1)
                valid = (kj >= qi) & (kj <= qi + BAND)
                if j == 0:
                    valid = valid & ((kj >= BAND) | (n > 0))
                sc = jnp.where(valid, sc, -1e30)
                m = jnp.max(sc, axis=-1, keepdims=True)
                p = jnp.exp(sc - m)
                l = jnp.sum(p, axis=-1, keepdims=True)
                o = lax.dot_general(p.astype(BF16), v, NN_DIMS, preferred_element_type=F32)
                o_ref[_every(at, sb, d), :] = o / l
                lse_ref[_every(at, sb, d), :] = jnp.broadcast_to(m + jnp.log(l), (sb, HD))

    cur, side, tok, _ = _attn_specs(g, tb, pb, s, ahead=False)
    return pl.pallas_call(
        body, name=f"attn_fwd_g{g}", grid=(N_HEADS, s // tb),
        out_shape=[jax.ShapeDtypeStruct((s, COL), F32)] * 2,
        in_specs=[cur(Q_COL), cur(K_COL), side(K_COL), cur(V_COL), side(V_COL)],
        out_specs=[tok, tok],
        scratch_shapes=[pltpu.VMEM((tb, HD), ft), pltpu.VMEM((tb + pb, HD), ft),
                        pltpu.VMEM((tb + pb, HD), ft)],
        compiler_params=_params("parallel", "arbitrary"),
    )(qkn, qkn, qkn, proj, proj)


def _attn_combine(os_, lses, tm=512):
    s = os_[0].shape[0]

    def body(o0, o1, o2, l0, l1, l2, o_ref, lse_ref):
        a, b, c = l0[...], l1[...], l2[...]
        m = jnp.maximum(jnp.maximum(a, b), c)
        ea, eb, ec = jnp.exp(a - m), jnp.exp(b - m), jnp.exp(c - m)
        tot = ea + eb + ec
        o_ref[...] = ((ea * o0[...] + eb * o1[...] + ec * o2[...]) / tot).astype(BF16)
        lse_ref[...] = m + jnp.log(tot)

    return pl.pallas_call(
        body, name="attn_combine", grid=(s // tm,),
        out_shape=[jax.ShapeDtypeStruct((s, COL), BF16), jax.ShapeDtypeStruct((s, COL), F32)],
        in_specs=[_row(tm, COL)] * 6, out_specs=[_row(tm, COL)] * 2,
        compiler_params=_params("parallel"),
    )(*os_, *lses)


def _attn_delta(do, o, tm=512):
    s = do.shape[0]

    def body(do_ref, o_ref, del_ref):
        prod = do_ref[...] * o_ref[...].astype(F32)
        del_ref[...] = _heads(prod, lambda ph, h: jnp.broadcast_to(
            jnp.sum(ph, axis=-1, keepdims=True), ph.shape))

    return pl.pallas_call(
        body, name="attn_delta", grid=(s // tm,),
        out_shape=jax.ShapeDtypeStruct((s, COL), F32),
        in_specs=[_row(tm, COL)] * 2, out_specs=_row(tm, COL),
        compiler_params=_params("parallel"),
    )(do, o)


def _attn_dq(g, qkn, proj, do, lse, delta, dqkn):
    s = qkn.shape[0]
    d, tb, sb, pb = _attn_shapes(s, g)
    ft = F32 if d > 1 else BF16
    nj = tb // d // sb
    scale = HD ** -0.5
    chained = dqkn is not None

    def body(q_ref, kc_ref, kp_ref, vc_ref, vp_ref, do_ref, lse_ref, del_ref, *rest):
        dq_ref, qf, kf, vf = rest[-4:]
        n = pl.program_id(1)
        qf[...] = q_ref[...].astype(ft)
        kf[0:pb] = kp_ref[...].astype(ft)
        kf[pb:] = kc_ref[...].astype(ft)
        vf[0:pb] = vp_ref[...].astype(ft)
        vf[pb:] = vc_ref[...].astype(ft)
        for r in range(d):
            for j in range(nj):
                at = j * sb * d + r
                rows = _every(at, sb, d)
                q = qf[rows, :].astype(BF16)
                k = kf[_every(at, sb + BAND, d), :].astype(BF16)
                v = vf[_every(at, sb + BAND, d), :].astype(BF16)
                sc = lax.dot_general(q, k, NT_DIMS, preferred_element_type=F32) * scale
                qi = lax.broadcasted_iota(jnp.int32, sc.shape, 0)
                kj = lax.broadcasted_iota(jnp.int32, sc.shape, 1)
                valid = (kj >= qi) & (kj <= qi + BAND)
                if j == 0:
                    valid = valid & ((kj >= BAND) | (n > 0))
                p = jnp.exp(jnp.where(valid, sc - _lanes(lse_ref[rows, :], sb + BAND), -1e30))
                dp = lax.dot_general(do_ref[rows, :].astype(BF16), v, NT_DIMS, preferred_element_type=F32)
                ds = p * (dp - _lanes(del_ref[rows, :], sb + BAND)) * scale
                dq_ref[rows, :] = lax.dot_general(ds.astype(BF16), k, NN_DIMS, preferred_element_type=F32)

    cur, side, tok, _ = _attn_specs(g, tb, pb, s, ahead=False)
    args = [qkn, qkn, qkn, proj, proj, do, lse, delta]
    specs = [cur(Q_COL), cur(K_COL), side(K_COL), cur(V_COL), side(V_COL), tok, tok, tok]
    if chained:
        args.append(dqkn)
        specs.append(pl.BlockSpec(memory_space=pl.ANY))
    return pl.pallas_call(
        body, name=f"attn_dq_g{g}", grid=(N_HEADS, s // tb),
        out_shape=jax.ShapeDtypeStruct((s, QKW), F32),
        in_specs=specs, out_specs=cur(Q_COL),
        input_output_aliases={8: 0} if chained else {},
        scratch_shapes=[pltpu.VMEM((tb, HD), ft), pltpu.VMEM((tb + pb, HD), ft),
                        pltpu.VMEM((tb + pb, HD), ft)],
        compiler_params=_params("arbitrary", "arbitrary"),
    )(*args)


def _attn_dkv(g, qkn, proj, do, lse, delta, dqkn, dproj):
    s = qkn.shape[0]
    d, tb, sb, pb = _attn_shapes(s, g)
    ft = F32 if d > 1 else BF16
    nj = tb // d // sb
    nt = s // tb
    scale = HD ** -0.5

    def body(k_ref, v_ref, qc_ref, qn_ref, doc_ref, don_ref, lc_ref, ln_ref, dc_ref, dn_ref, _a, _b,
             dk_ref, dv_ref, kf, vf, qf, dvf):
        n = pl.program_id(1)
        kf[...] = k_ref[...].astype(ft)
        vf[...] = v_ref[...].astype(ft)
        qf[0:tb] = qc_ref[...].astype(ft)
        qf[tb:] = qn_ref[...].astype(ft)

        def window(c_ref, n_ref, r, j):
            at = j * sb * d + r
            if j < nj - 1:
                return c_ref[_every(at, sb + BAND, d), :]
            return jnp.concatenate([c_ref[_every(at, sb, d), :], n_ref[_every(r, BAND, d), :]], axis=0)

        for r in range(d):
            for j in range(nj):
                at = j * sb * d + r
                rows = _every(at, sb, d)
                k = kf[rows, :].astype(BF16)
                v = vf[rows, :].astype(BF16)
                q = qf[_every(at, sb + BAND, d), :].astype(BF16)
                dov = window(doc_ref, don_ref, r, j).astype(BF16)
                sc = lax.dot_general(q, k, NT_DIMS, preferred_element_type=F32) * scale
                qi = lax.broadcasted_iota(jnp.int32, sc.shape, 0)
                kj = lax.broadcasted_iota(jnp.int32, sc.shape, 1)
                valid = (qi >= kj) & (qi <= kj + BAND)
                if j == nj - 1:
                    valid = valid & ((qi < sb) | (n < nt - 1))
                p = jnp.exp(jnp.where(valid, sc - _lanes(window(lc_ref, ln_ref, r, j), sb), -1e30))
                dp = lax.dot_general(dov, v, NT_DIMS, preferred_element_type=F32)
                ds = p * (dp - _lanes(window(dc_ref, dn_ref, r, j), sb)) * scale
                dvf[rows, :] = lax.dot_general(p.astype(BF16), dov, TN_DIMS, preferred_element_type=F32)
                dk_ref[rows, :] = lax.dot_general(ds.astype(BF16), q, TN_DIMS, preferred_element_type=F32)
        dv_ref[...] = dvf[...].astype(BF16)

    cur, side, tok, tok_side = _attn_specs(g, tb, pb, s, ahead=True)
    anyspec = pl.BlockSpec(memory_space=pl.ANY)
    return pl.pallas_call(
        body, name=f"attn_dkv_g{g}", grid=(N_HEADS, nt),
        out_shape=[jax.ShapeDtypeStruct((s, QKW), F32), jax.ShapeDtypeStruct((s, IN_W), BF16)],
        in_specs=[cur(K_COL), cur(V_COL), cur(Q_COL), side(Q_COL), tok, tok_side, tok, tok_side,
                  tok, tok_side, anyspec, anyspec],
        out_specs=[cur(K_COL), cur(V_COL)],
        input_output_aliases={10: 0, 11: 1},
        scratch_shapes=[pltpu.VMEM((tb, HD), ft), pltpu.VMEM((tb, HD), ft),
                        pltpu.VMEM((tb + pb, HD), ft), pltpu.VMEM((tb, HD), F32)],
        compiler_params=_params("arbitrary", "arbitrary"),
    )(qkn, proj, qkn, qkn, do, do, lse, lse, delta, delta, dqkn, dproj)


def _shift_down(x, before, k):
    rolled = pltpu.roll(x, k, 0)
    head = jnp.where(lax.broadcasted_iota(jnp.int32, before.shape, 0) < k, pltpu.roll(before, k, 0), rolled[:8])
    return jnp.concatenate([head, rolled[8:]], axis=0)


def _shift_up(x, after, k):
    rows = x.shape[0]
    rolled = pltpu.roll(x, rows - k, 0)
    tail = jnp.where(lax.broadcasted_iota(jnp.int32, after.shape, 0) >= 8 - k,
                     pltpu.roll(after, 8 - k, 0), rolled[rows - 8:])
    return jnp.concatenate([rolled[:rows - 8], tail], axis=0)


def _conv_fwd(proj, cw, tm=1024):
    s = proj.shape[0]
    r16 = tm // 16

    def body(u_ref, b_ref, c_ref, up_ref, cp_ref, w_ref, z_ref):
        i = pl.program_id(1)
        xc = c_ref[...].astype(F32) * u_ref[...].astype(F32)
        xp = jnp.where(i > 0, cp_ref[8:16, :].astype(F32) * up_ref[8:16, :].astype(F32), 0.0)
        w = w_ref[...]
        conv = _shift_down(xc, xp, 2) * w[0:1] + _shift_down(xc, xp, 1) * w[1:2] + xc * w[2:3]
        z_ref[...] = (b_ref[...].astype(F32) * conv).astype(BF16)

    tile = lambda blk: pl.BlockSpec((tm, COL), lambda j, i: (i, blk + j))
    before = lambda blk: pl.BlockSpec((16, COL), lambda j, i: (jnp.maximum(i * r16 - 1, 0), blk + j))
    return pl.pallas_call(
        body, name="conv_fwd", grid=(D // COL, s // tm),
        out_shape=jax.ShapeDtypeStruct((s, D), BF16),
        in_specs=[tile(U_BLK), tile(B_BLK), tile(C_BLK), before(U_BLK), before(C_BLK),
                  pl.BlockSpec((3, COL), lambda j, i: (0, j))],
        out_specs=pl.BlockSpec((tm, COL), lambda j, i: (i, j)),
        compiler_params=_params("parallel", "parallel"),
    )(proj, proj, proj, proj, proj, cw)


def _conv_bwd(dz, proj, cw, dproj, tm=1024):
    s = proj.shape[0]
    r8, r16 = tm // 8, tm // 16
    nrow = s // tm

    def body(dz_ref, u_ref, b_ref, c_ref, up_ref, cp_ref, dzn_ref, bn_ref, w_ref, _, o_ref, acc_ref):
        piece, i = pl.program_id(1), pl.program_id(2)
        u, c = u_ref[...].astype(F32), c_ref[...].astype(F32)
        bv = b_ref[...].astype(F32)
        dzv = dz_ref[...]
        w = w_ref[...]

        @pl.when((piece == 0) & (i == 0))
        def _():
            acc_ref[...] = jnp.zeros_like(acc_ref)

        @pl.when(piece == 1)
        def _():
            xc = c * u
            xp = jnp.where(i > 0, cp_ref[8:16, :].astype(F32) * up_ref[8:16, :].astype(F32), 0.0)
            x2, x1 = _shift_down(xc, xp, 2), _shift_down(xc, xp, 1)
            o_ref[...] = (dzv * (x2 * w[0:1] + x1 * w[1:2] + xc * w[2:3])).astype(BF16)
            dconv = dzv * bv
            acc_ref[0:1, :] += jnp.sum(dconv * x2, axis=0, keepdims=True)
            acc_ref[1:2, :] += jnp.sum(dconv * x1, axis=0, keepdims=True)
            acc_ref[2:3, :] += jnp.sum(dconv * xc, axis=0, keepdims=True)

        @pl.when(piece != 1)
        def _():
            dconv = dzv * bv
            dn = jnp.where(i < nrow - 1, dzn_ref[...] * bn_ref[0:8, :].astype(F32), 0.0)
            dxc = dconv * w[2:3] + _shift_up(dconv, dn, 1) * w[1:2] + _shift_up(dconv, dn, 2) * w[0:1]
            o_ref[...] = (dxc * jnp.where(piece == 0, c, u)).astype(BF16)

    tile = lambda blk: pl.BlockSpec((tm, COL), lambda j, p, i: (i, blk + j))
    before = lambda blk: pl.BlockSpec((16, COL), lambda j, p, i: (jnp.maximum(i * r16 - 1, 0), blk + j))
    after = lambda rows, blk: pl.BlockSpec(
        (rows, COL), lambda j, p, i: (jnp.minimum((i + 1) * (tm // rows), s // rows - 1), blk + j))
    return pl.pallas_call(
        body, name="conv_bwd", grid=(D // COL, 3, nrow),
        out_shape=[jax.ShapeDtypeStruct((s, IN_W), BF16), jax.ShapeDtypeStruct((8, D), F32)],
        in_specs=[tile(0), tile(U_BLK), tile(B_BLK), tile(C_BLK), before(U_BLK), before(C_BLK),
                  after(8, 0), after(16, B_BLK), pl.BlockSpec((3, COL), lambda j, p, i: (0, j)),
                  pl.BlockSpec(memory_space=pl.ANY)],
        out_specs=[pl.BlockSpec((tm, COL), lambda j, p, i: (i, U_BLK + 2 * p + j)),
                   pl.BlockSpec((8, COL), lambda j, p, i: (0, j))],
        input_output_aliases={9: 0},
        compiler_params=_params("arbitrary", "arbitrary", "arbitrary"),
    )(dz, proj, proj, proj, proj, proj, dz, proj, cw, dproj)


def _merge_fwd(ya, yc, proj, tm=512):
    s = proj.shape[0]

    def body(ya_ref, yc_ref, ga_ref, gc_ref, o_ref):
        o_ref[...] = (_sigmoid(ga_ref[...].astype(F32)) * ya_ref[...].astype(F32)
                      + _sigmoid(gc_ref[...].astype(F32)) * yc_ref[...].astype(F32)).astype(BF16)

    tile = lambda blk: pl.BlockSpec((tm, COL), lambda j, i: (i, blk + j))
    return pl.pallas_call(
        body, name="merge_fwd", grid=(D // COL, s // tm),
        out_shape=jax.ShapeDtypeStruct((s, D), BF16),
        in_specs=[tile(0), tile(0), tile(GA_BLK), tile(GC_BLK)], out_specs=tile(0),
        compiler_params=_params("parallel", "parallel"),
    )(ya, yc, proj, proj)


def _merge_bwd_branches(dm, proj, tm=512):
    s = proj.shape[0]

    def body(dm_ref, ga_ref, gc_ref, dya_ref, dyc_ref):
        dmv = dm_ref[...]
        dya_ref[...] = (dmv * _sigmoid(ga_ref[...].astype(F32))).astype(BF16)
        dyc_ref[...] = (dmv * _sigmoid(gc_ref[...].astype(F32))).astype(BF16)

    tile = lambda blk: pl.BlockSpec((tm, COL), lambda j, i: (i, blk + j))
    return pl.pallas_call(
        body, name="merge_bwd_branches", grid=(D // COL, s // tm),
        out_shape=[jax.ShapeDtypeStruct((s, D), BF16)] * 2,
        in_specs=[tile(0), tile(GA_BLK), tile(GC_BLK)], out_specs=[tile(0)] * 2,
        compiler_params=_params("parallel", "parallel"),
    )(dm, proj, proj)


def _merge_bwd_gates(dm, ya, yc, proj, tm=1024):
    s = proj.shape[0]
    half = D // COL

    def body(dm_ref, ya_ref, yc_ref, g_ref, o_ref):
        y = jnp.where(pl.program_id(0) < half, ya_ref[...].astype(F32), yc_ref[...].astype(F32))
        sig = _sigmoid(g_ref[...].astype(F32))
        o_ref[...] = (dm_ref[...] * y * sig * (1.0 - sig)).astype(BF16)

    chan = pl.BlockSpec((tm, COL), lambda jj, i: (i, jj % half))
    gate = pl.BlockSpec((tm, COL), lambda jj, i: (i, GA_BLK + jj))
    return pl.pallas_call(
        body, name="merge_bwd_gates", grid=(2 * half, s // tm),
        out_shape=jax.ShapeDtypeStruct((s, IN_W), BF16),
        in_specs=[chan, chan, chan, gate], out_specs=gate,
        compiler_params=_params("parallel", "parallel"),
    )(dm, ya, yc, proj)


def _mod_part(c_all, w_ada, b_part):
    def body(c_ref, w_ref, b_ref, o_ref):
        cv = c_ref[...]
        act = cv * _sigmoid(cv)
        o_ref[...] = jnp.dot(act, w_ref[...], preferred_element_type=F32,
                             precision=lax.Precision.HIGHEST) + b_ref[...]

    return pl.pallas_call(
        body, name="mod_part", out_shape=jax.ShapeDtypeStruct((N_DEV, w_ada.shape[1]), F32),
    )(c_all, w_ada, b_part)


def _w_ada_grad(c_all_t, dmod_part):
    def body(c_ref, d_ref, o_ref):
        cv = c_ref[...]
        act = cv * _sigmoid(cv)
        dv = d_ref[...]
        acc = act[:, 0:1] * dv[0:1, :]
        for b in range(1, N_DEV):
            acc = acc + act[:, b:b + 1] * dv[b:b + 1, :]
        o_ref[...] = acc

    return pl.pallas_call(
        body, name="w_ada_grad", out_shape=jax.ShapeDtypeStruct((D, dmod_part.shape[1]), F32),
    )(c_all_t, dmod_part)


def _sum_rows(name, v):
    def body(v_ref, o_ref):
        acc = v_ref[0]
        for k in range(1, N_DEV):
            acc = acc + v_ref[k]
        o_ref[...] = acc

    return pl.pallas_call(body, name=name, out_shape=jax.ShapeDtypeStruct(v.shape[1:], F32))(v)


def _adamw(name, w, g, m, v):
    rows, cols = w.shape
    limit = max(16, (1 << 20) // (4 * cols))
    tr = rows if rows <= limit else next((t for t in range(limit - limit % 16, 15, -16) if rows % t == 0), rows)
    c1 = 1.0 - ADAM_B1 ** ADAM_STEP
    c2 = 1.0 - ADAM_B2 ** ADAM_STEP
    parts = g.ndim == 3

    def body(w_ref, g_ref, m_ref, v_ref, go_ref, d_ref, nm_ref, nv_ref):
        if parts:
            gv = g_ref[0].astype(F32)
            for k in range(1, N_DEV):
                gv = gv + g_ref[k].astype(F32)
        else:
            gv = g_ref[...]
        go_ref[...] = gv
        nm = ADAM_B1 * m_ref[...] + (1.0 - ADAM_B1) * gv
        nv = ADAM_B2 * v_ref[...] + (1.0 - ADAM_B2) * (gv * gv)
        nm_ref[...] = nm
        nv_ref[...] = nv
        d_ref[...] = -ADAM_LR * ((nm / c1) / (jnp.sqrt(nv / c2) + ADAM_EPS) + ADAM_WD * w_ref[...])

    spec = pl.BlockSpec((tr, cols), lambda i: (i, 0))
    g_spec = pl.BlockSpec((N_DEV, tr, cols), lambda i: (0, i, 0)) if parts else spec
    return pl.pallas_call(
        body, name=name, grid=(rows // tr,),
        out_shape=[jax.ShapeDtypeStruct((rows, cols), F32)] * 4,
        in_specs=[spec, g_spec, spec, spec], out_specs=[spec] * 4,
        compiler_params=_params("parallel"),
    )(w, g, m, v)


HALF = FF // 2


def _sds(shape, dtype):
    return jax.ShapeDtypeStruct(shape, dtype)


def _row_tile(w):
    return lambda tm: ((tm, w), lambda i, j: (i, 0))


def _one(w):
    return lambda rows: ((rows, w), lambda i, j: (0, 0))


def _gate_up_swiglu(name, h, wgu, carry=None, tm=512):
    s = h.shape[0]
    tm = min(tm, s)

    def epilogue(prod, first, tin, tout):
        ab_ref, s_ref = tout
        ab_ref[...] = prod.astype(BF16)
        a, b = prod[:, :HALF], prod[:, HALF:]
        s_ref[...] = (a * _sigmoid(a) * b).astype(BF16)

    return _mm(name, h, wgu, "NT", None, tm, FF, D, carry=carry, n_outer=True, epilogue=epilogue,
               tiles_out=[(_sds((s, 2 * FF), BF16), (tm, FF), lambda i, j: (i, j)),
                          (_sds((s, FF), BF16), (tm, HALF), lambda i, j: (i, j))])


def _d_hidden_swiglu(name, df, wd, ab, tm=512):
    s = df.shape[0]
    tm = min(tm, s)

    def epilogue(prod, first, tin, tout):
        a = tin[0][:, :HALF].astype(F32)
        b = tin[0][:, HALF:].astype(F32)
        sig = _sigmoid(a)
        tout[0][:, :HALF] = (prod * b * (sig * (1.0 + a * (1.0 - sig)))).astype(BF16)
        tout[0][:, HALF:] = (prod * (a * sig)).astype(BF16)

    return _mm(name, df, wd, "NT", None, tm, HALF, D, n_outer=True, epilogue=epilogue,
               tiles_in=[(ab, (tm, FF), lambda i, j: (i, j))],
               tiles_out=[(_sds((s, 2 * FF), BF16), (tm, FF), lambda i, j: (i, j))])[0]


def _out_residual(name, a, w, x, gt, coef, nxt, tm=512, tk=FF):
    s = a.shape[0]
    tm = min(tm, s)

    def epilogue(prod, first, tin, tout):
        x_ref, gt_ref, g_ref, sc_ref, sh_ref = tin
        f_ref, xn_ref, h_ref = tout
        f_ref[...] = prod
        xn = x_ref[...] + (coef * gt_ref[...]) * prod
        xn_ref[...] = xn
        r = lax.rsqrt(jnp.mean(xn * xn, axis=-1, keepdims=True) + EPS)
        h_ref[...] = ((xn * r) * g_ref[...] * (1.0 + sc_ref[...]) + sh_ref[...]).astype(BF16)

    row, vec = _row_tile(D)(tm), _one(D)(1)
    return _mm(name, a, w, "NN", None, tm, D, tk, epilogue=epilogue,
               tiles_in=[(x, *row), (gt, *vec)] + [(v, *vec) for v in nxt],
               tiles_out=[(_sds((s, D), F32), *row), (_sds((s, D), F32), *row), (_sds((s, D), BF16), *row)])


def _out_loss(name, a, w, x, gt, coef, target, tm=512):
    s = a.shape[0]
    tm = min(tm, s)

    def epilogue(prod, first, tin, tout):
        x_ref, gt_ref, t_ref = tin
        f_ref, g_ref, df_ref, acc_ref = tout
        f_ref[...] = prod
        cg = coef * gt_ref[...]
        e = x_ref[...] + cg * prod - t_ref[...]
        gv = e * (1.0 / D)
        g_ref[...] = gv
        df_ref[...] = (cg * gv).astype(BF16)

        @pl.when(first)
        def _():
            acc_ref[...] = jnp.zeros_like(acc_ref)

        acc_ref[0:1, :] += coef * jnp.sum(gv * prod, axis=0, keepdims=True)
        acc_ref[1:2, :] += (0.5 / D) * jnp.sum(e * e, axis=0, keepdims=True)

    row, vec = _row_tile(D)(tm), _one(D)(1)
    return _mm(name, a, w, "NN", None, tm, D, FF, epilogue=epilogue,
               tiles_in=[(x, *row), (gt, *vec), (target, *row)],
               tiles_out=[(_sds((s, D), F32), *row), (_sds((s, D), F32), *row), (_sds((s, D), BF16), *row),
                          (_sds((8, D), F32), *_one(D)(8))])


def _d_h_norm_bwd(name, da, w, x, gin, g, sc, sh, before=None, carry=None, tm=256):
    s = da.shape[0]
    tm = min(tm, s)
    coef = before[2] if before else None

    def epilogue(prod, first, tin, tout):
        x_ref, gin_ref, g_ref, sc_ref, sh_ref = tin[:5]
        gout_ref, acc_ref = tout[:2]
        xv = x_ref[...]
        r = lax.rsqrt(jnp.mean(xv * xv, axis=-1, keepdims=True) + EPS)
        nv = xv * r
        gv, one_sc = g_ref[...], 1.0 + sc_ref[...]
        dn = prod * gv * one_sc
        gout = gin_ref[...] + r * (dn - nv * jnp.mean(dn * nv, axis=-1, keepdims=True))
        gout_ref[...] = gout

        @pl.when(first)
        def _():
            acc_ref[...] = jnp.zeros_like(acc_ref)

        dhn = prod * nv
        acc_ref[0:1, :] += jnp.sum(prod, axis=0, keepdims=True)
        acc_ref[1:2, :] += jnp.sum(dhn * gv, axis=0, keepdims=True)
        acc_ref[2:3, :] += jnp.sum(dhn * one_sc, axis=0, keepdims=True)
        if before:
            f_ref, gt_ref = tin[5:]
            tout[2][...] = ((coef * gt_ref[...]) * gout).astype(BF16)
            acc_ref[3:4, :] += coef * jnp.sum(gout * f_ref[...], axis=0, keepdims=True)

    row, vec = _row_tile(D)(tm), _one(D)(1)
    tiles_in = [(x, *row), (gin, *row), (g, *vec), (sc, *vec), (sh, *vec)]
    tiles_out = [(_sds((s, D), F32), *row), (_sds((8, D), F32), *_one(D)(8))]
    if before:
        tiles_in += [(before[0], *row), (before[1], *vec)]
        tiles_out.append((_sds((s, D), BF16), *row))
    return _mm(name, da, w, "NN", None, tm, D, da.shape[1], epilogue=epilogue, carry=carry, keep_b=True,
               tiles_in=tiles_in, tiles_out=tiles_out)


def _gate_tiles(proj, tm):
    return [(proj, (tm, COL), (lambda i, j, blk=blk: (i, blk))) for blk in (GA_BLK, GA_BLK + 1, GC_BLK, GC_BLK + 1)]


def _conv_branch_merge(z, wc, ya, proj, tm=512):
    s = z.shape[0]
    tm = min(tm, s)

    def epilogue(prod, first, tin, tout):
        ya_ref, ga0, ga1, gc0, gc1 = tin
        tout[0][...] = prod.astype(BF16)
        for half, (ga, gc) in enumerate(((ga0, gc0), (ga1, gc1))):
            cols = slice(half * COL, (half + 1) * COL)
            tout[1][:, cols] = (_sigmoid(ga[...].astype(F32)) * ya_ref[:, cols].astype(F32)
                                + _sigmoid(gc[...].astype(F32)) * prod[:, cols]).astype(BF16)

    row = _row_tile(D)(tm)
    return _mm("mix_conv_branch", z, wc, "NN", None, tm, D, D, epilogue=epilogue,
               tiles_in=[(ya, *row)] + _gate_tiles(proj, tm),
               tiles_out=[(_sds((s, D), BF16), *row), (_sds((s, D), BF16), *row)])


def _d_merged_branches(dmix, wo, proj, tm=512):
    s = dmix.shape[0]
    tm = min(tm, s)

    def epilogue(prod, first, tin, tout):
        ga0, ga1, gc0, gc1 = tin
        tout[0][...] = prod
        for half, (ga, gc) in enumerate(((ga0, gc0), (ga1, gc1))):
            cols = slice(half * COL, (half + 1) * COL)
            tout[1][:, cols] = (prod[:, cols] * _sigmoid(ga[...].astype(F32))).astype(BF16)
            tout[2][:, cols] = (prod[:, cols] * _sigmoid(gc[...].astype(F32))).astype(BF16)

    row = _row_tile(D)(tm)
    return _mm("mix_d_merged", dmix, wo, "NT", None, tm, D, D, epilogue=epilogue,
               tiles_in=_gate_tiles(proj, tm),
               tiles_out=[(_sds((s, D), F32), *row), (_sds((s, D), BF16), *row), (_sds((s, D), BF16), *row)])


def _d_o_delta(dya, wa_t, o, tm=1024):
    s = dya.shape[0]
    tm = min(tm, s)

    def epilogue(prod, first, tin, tout):
        tout[0][...] = prod
        tout[1][...] = _heads(prod * tin[0][...].astype(F32), lambda ph, h: jnp.broadcast_to(
            jnp.sum(ph, axis=-1, keepdims=True), ph.shape))

    row = _row_tile(COL)(tm)
    return _mm("mix_d_o", dya, wa_t, "NN", None, tm, COL, D, epilogue=epilogue,
               tiles_in=[(o, *row)], tiles_out=[(_sds((s, COL), F32), *row), (_sds((s, COL), F32), *row)])


def _ffn_bwd(tag, df, x, gin, h, ab, sw, g, sc, sh, wgu, wd, before=None, carry_down=None, carry_gate_up=None,
             tk_dw=2048):
    dab = _d_hidden_swiglu(f"{tag}_d_hidden", df, wd, ab)
    dwd = _mm(f"{tag}_dw_down", sw, df, "TN", BF16, HALF, D, tk_dw)
    carried = []
    if carry_down:
        dwgu, *got = _mm(f"{tag}_dw_gate_up", dab, h, "TN", BF16, HALF, D, tk_dw, carry=carry_down(dwd))
        carried += got
    else:
        dwgu = _mm(f"{tag}_dw_gate_up", dab, h, "TN", BF16, HALF, D, tk_dw)
    res = _d_h_norm_bwd(f"{tag}_d_h", dab, wgu, x, gin, g, sc, sh, before=before,
                        carry=carry_gate_up(dwgu) if carry_gate_up else None)
    n_own = 3 if before else 2
    return res[:n_own], dwgu, dwd, carried + list(res[n_own:])


def kernel(x, c, w_ada, b_ada, norm_ffn1, ffn1_w_gate, ffn1_w_up, ffn1_w_down, norm_mix, w_in, q_norm, k_norm, conv_w, w_attn_branch, w_conv_branch, w_out, norm_ffn2, ffn2_w_gate, ffn2_w_up, ffn2_w_down, loss_target, m_w_ada, m_b_ada, m_norm_ffn1, m_ffn1_w_gate, m_ffn1_w_up, m_ffn1_w_down, m_norm_mix, m_w_in, m_q_norm, m_k_norm, m_conv_w, m_w_attn_branch, m_w_conv_branch, m_w_out, m_norm_ffn2, m_ffn2_w_gate, m_ffn2_w_up, m_ffn2_w_down, v_w_ada, v_b_ada, v_norm_ffn1, v_ffn1_w_gate, v_ffn1_w_up, v_ffn1_w_down, v_norm_mix, v_w_in, v_q_norm, v_k_norm, v_conv_w, v_w_attn_branch, v_w_conv_branch, v_w_out, v_norm_ffn2, v_ffn2_w_gate, v_ffn2_w_up, v_ffn2_w_down):
    me = 4 * lax.axis_index("x") + 2 * lax.axis_index("y") + lax.axis_index("c")
    x0, target = x[0], loss_target[0]
    s = x0.shape[0]
    ada_cols = w_ada.shape[2]
    cw_cols = conv_w.shape[2]

    gathered = _small_allgather(
        "gather_c_conv", jnp.concatenate([c, conv_w[0].reshape(1, 3 * cw_cols)], axis=1))[:, 0]
    c_all = gathered[:, :D]
    cw = gathered[:, D:].reshape(N_DEV, 3, cw_cols).transpose(1, 0, 2).reshape(3, D)
    b_part = lax.dynamic_slice(b_ada, (0, me * ada_cols), (1, ada_cols))
    mod_part = _mod_part(c_all, w_ada[0], b_part)
    mod_all = _small_allgather("gather_mod", mod_part.reshape(1, N_DEV * ada_cols))
    mod = lax.dynamic_slice(mod_all.reshape(N_DEV, N_DEV, ada_cols), (0, me, 0), (N_DEV, 1, ada_cols))
    mod = mod.reshape(N_MOD, 1, D)
    sh1, sc1, gt1, sh2, sc2, gt2, sh3, sc3, gt3 = [mod[i] for i in range(N_MOD)]

    tb = lambda w: w[0].T.astype(BF16)
    nb = lambda w: w[0].astype(BF16)
    ffn1_shards = [tb(ffn1_w_gate), tb(ffn1_w_up), nb(ffn1_w_down)]
    ffn2_shards = [tb(ffn2_w_gate), tb(ffn2_w_up), nb(ffn2_w_down)]
    mix_shards = [tb(w_in), tb(w_attn_branch), nb(w_conv_branch), nb(w_out)]
    ffn_dst, ffn_base, ffn_jump, ffn_shapes = [0, 0, 1], [0, HALF, 0], [HALF, HALF, 0], [(2 * FF, D), (FF, D)]
    mix_dst, mix_base, mix_shapes = [0, 1, 2, 3], [0, 0, 0, 0], [(IN_W, D), (D, COL), (D, D), (D, D)]
    wgu1, wd1 = _run_plan("gather_ffn1_weights",
                          _gather_plan(ffn1_shards, ffn_dst, ffn_base, ffn_shapes, ffn_jump))

    h1 = _normmod("ffn1_normmod", x0, norm_ffn1, sc1, sh1)
    ab1, s1, win_t = _gate_up_swiglu(
        "ffn1_gate_up", h1, wgu1, carry=_gather_plan(mix_shards[:1], mix_dst[:1], mix_base[:1], mix_shapes[:1]))
    f1, x1, h2 = _out_residual("ffn1_down", s1, wd1, x0, gt1, 0.5, (norm_mix, sc2, sh2))
    proj, wgu2, wd2, wa_t, wc, wo = _mm(
        "mix_in_proj", h2, win_t, "NT", BF16, 512, IN_W // 4, D, n_outer=True,
        carry=_gather_plan(ffn2_shards + mix_shards[1:], ffn_dst + [2, 3, 4], ffn_base + [0, 0, 0],
                           ffn_shapes + mix_shapes[1:], ffn_jump + [0, 0, 0]))
    wqk = jnp.concatenate([jnp.tile(q_norm, (1, 12)), jnp.tile(k_norm, (1, 12))], axis=1)
    qkn = _qknorm(proj, wqk)
    group_out = [_attn_fwd(g, qkn, proj) for g in range(3)]
    o, lse = _attn_combine([go[0] for go in group_out], [go[1] for go in group_out])
    ya = _mm("mix_attn_branch", o, wa_t, "NT", BF16, 1024, 1024, COL)
    z = _conv_fwd(proj, cw)
    yc, merged = _conv_branch_merge(z, wc, ya, proj)
    mix, x2, h3 = _out_residual("mix_out_proj", merged, wo, x1, gt2, 1.0, (norm_ffn2, sc3, sh3), tk=D)
    ab3, s3 = _gate_up_swiglu("ffn2_gate_up", h3, wgu2)
    f3, g3, df3, acc_out = _out_loss("ffn2_down", s3, wd2, x2, gt3, 0.5, target)
    loss = lax.psum(jnp.sum(acc_out[1]), ("x", "y", "c"))

    ffn_rows = [sh_.shape[0] for sh_ in ffn1_shards]
    mix_rows = [sh_.shape[0] for sh_ in mix_shards]
    (g2, acc3, dmix), dwgu2, dwd2, _ = _ffn_bwd(
        "ffn2", df3, x2, g3, h3, ab3, s3, norm_ffn2, sc3, sh3, wgu2, wd2, before=(mix, gt2, 1.0))
    dmerged, dya, dyc = _d_merged_branches(dmix, wo, proj)
    dwo = _mm("mix_dw_out", merged, dmix, "TN", BF16, 1024, 1024, 2048)
    dproj = _merge_bwd_gates(dmerged, ya, yc, proj)
    dwc = _mm("mix_dw_conv_branch", z, dyc, "TN", BF16, 1024, 1024, 2048)
    dz = _mm("mix_d_z", dyc, wc, "NT", F32, 1024, 1024, D)
    dproj, cw_acc = _conv_bwd(dz, proj, cw, dproj)
    dwa_t = _mm("mix_dw_attn_branch", dya, o, "TN", BF16, 1024, COL, 2048)
    do, delta = _d_o_delta(dya, wa_t, o)
    dqkn = None
    for g in range(3):
        dqkn = _attn_dq(g, qkn, proj, do, lse, delta, dqkn)
    for g in range(3):
        dqkn, dproj = _attn_dkv(g, qkn, proj, do, lse, delta, dqkn, dproj)
    dproj, wqk_acc = _qknorm_bwd(proj, dqkn, wqk, dproj)
    dwin_t, r_f2g, r_f2u, r_f2d, r_wa, r_wc, r_wo = _mm(
        "mix_dw_in", dproj, h2, "TN", BF16, COL, D, 2048,
        carry=_scatter_plan([dwgu2, dwd2, dwa_t, dwc, dwo], [0, 0, 1, 2, 3, 4], [0, HALF, 0, 0, 0, 0],
                            ffn_rows + mix_rows[1:], [D, D, D, COL, D, D], [HALF, HALF, 0, 0, 0, 0]))
    g1, acc2, df1, r_win = _d_h_norm_bwd(
        "mix_d_h", dproj, win_t, x1, g2, norm_mix, sc2, sh2, before=(f1, gt1, 0.5),
        carry=_scatter_plan([dwin_t], [0], [0], mix_rows[:1], [D]))
    (g0, acc1), dwgu1, dwd1, (r_f1d, r_f1g, r_f1u) = _ffn_bwd(
        "ffn1", df1, x0, g1, h1, ab1, s1, norm_ffn1, sc1, sh1, wgu1, wd1,
        carry_down=lambda dwd: _scatter_plan([dwd], [0], [0], ffn_rows[2:], [D]),
        carry_gate_up=lambda dwgu: _scatter_plan([dwgu], [0, 0], [0, HALF], ffn_rows[:2], [D, D], [HALF, HALF]))

    dqw = jnp.sum(wqk_acc[0, :QKW // 2].reshape(12, HD), axis=0)
    dkw = jnp.sum(wqk_acc[0, QKW // 2:].reshape(12, HD), axis=0)
    small = jnp.concatenate([
        acc1[0], acc1[1], acc2[3], acc2[0], acc2[1], acc3[3], acc3[0], acc3[1], acc_out[0],
        acc1[2], acc2[2], acc3[2], dqw, dkw, cw_acc[0:3].reshape(3 * D)]).reshape(1, -1)
    small_all = _small_allgather("gather_small_grads", small)
    small_sum = _sum_rows("sum_small_grads", small_all)[0]
    n_mod = N_MOD * D
    g_b_ada = small_sum[:n_mod].reshape(1, n_mod)
    g_norm1, g_norm2, g_norm3 = [small_sum[n_mod + i * D:n_mod + (i + 1) * D].reshape(1, D) for i in range(3)]
    off = n_mod + 3 * D
    g_qn, g_kn = small_sum[off:off + HD].reshape(1, HD), small_sum[off + HD:off + 2 * HD].reshape(1, HD)
    g_cw_full = small_sum[off + 2 * HD:].reshape(3, D)
    g_cw = lax.dynamic_slice(g_cw_full, (0, me * cw_cols), (3, cw_cols))
    dmod_part = lax.dynamic_slice(small_all[:, 0, :n_mod], (0, me * ada_cols), (N_DEV, ada_cols))
    g_w_ada = _w_ada_grad(c_all.T, dmod_part)

    as_rows = {"ffn1_w_gate", "ffn1_w_up", "w_in", "w_attn_branch", "ffn2_w_gate", "ffn2_w_up"}
    grad_list = [g_w_ada, g_b_ada, g_norm1, r_f1g, r_f1u, r_f1d, g_norm2, r_win,
                 g_qn, g_kn, g_cw, r_wa, r_wc, r_wo, g_norm3, r_f2g, r_f2u, r_f2d]
    weights = [w_ada, b_ada, norm_ffn1, ffn1_w_gate, ffn1_w_up, ffn1_w_down, norm_mix, w_in, q_norm, k_norm,
               conv_w, w_attn_branch, w_conv_branch, w_out, norm_ffn2, ffn2_w_gate, ffn2_w_up, ffn2_w_down]
    ms = [m_w_ada, m_b_ada, m_norm_ffn1, m_ffn1_w_gate, m_ffn1_w_up, m_ffn1_w_down, m_norm_mix, m_w_in, m_q_norm,
          m_k_norm, m_conv_w, m_w_attn_branch, m_w_conv_branch, m_w_out, m_norm_ffn2, m_ffn2_w_gate,
          m_ffn2_w_up, m_ffn2_w_down]
    vs = [v_w_ada, v_b_ada, v_norm_ffn1, v_ffn1_w_gate, v_ffn1_w_up, v_ffn1_w_down, v_norm_mix, v_w_in, v_q_norm,
          v_k_norm, v_conv_w, v_w_attn_branch, v_w_conv_branch, v_w_out, v_norm_ffn2, v_ffn2_w_gate,
          v_ffn2_w_up, v_ffn2_w_down]
    wnames = ["w_ada", "b_ada", "norm_ffn1", "ffn1_w_gate", "ffn1_w_up", "ffn1_w_down", "norm_mix", "w_in",
              "q_norm", "k_norm", "conv_w", "w_attn_branch", "w_conv_branch", "w_out", "norm_ffn2",
              "ffn2_w_gate", "ffn2_w_up", "ffn2_w_down"]
    grad_out, deltas, new_ms, new_vs = [], [], [], []
    for nm, w, gr, m_, v_ in zip(wnames, weights, grad_list, ms, vs):
        if nm in as_rows:
            res = _adamw(f"adamw_{nm}", w[0].T, gr, m_[0].T, v_[0].T)
            gr, dl, nm_, nv_ = [r.T[None] for r in res]
        else:
            two_d = (-1, w.shape[-1])
            res = _adamw(f"adamw_{nm}", w.reshape(two_d), gr if gr.ndim == 3 else gr.reshape(two_d),
                         m_.reshape(two_d), v_.reshape(two_d))
            gr, dl, nm_, nv_ = [r.reshape(w.shape) for r in res]
        grad_out.append(gr)
        deltas.append(dl)
        new_ms.append(nm_)
        new_vs.append(nv_)
    return (loss, g0[None], *grad_out, *deltas, *new_ms, *new_vs)
```

```python
import functools

import jax
import jax.numpy as jnp
from jax import lax
from jax.experimental import pallas as pl
from jax.experimental.pallas import tpu as pltpu

F32 = jnp.float32
BF16 = jnp.bfloat16
MESH = pl.DeviceIdType.MESH

N_DEV = 8
D = 1024
FF = 2816
HD = 128
N_HEADS = 4
DILATIONS = (1, 4, 16)
BAND = 128
QKW = 2 * 3 * N_HEADS * HD
IN_W = 9728
COL = 512
V_BLK, U_BLK, B_BLK, C_BLK, GA_BLK, GC_BLK = 6, 9, 11, 13, 15, 17
EPS = 1e-6
N_MOD = 9
ADAM_LR, ADAM_B1, ADAM_B2, ADAM_EPS, ADAM_WD, ADAM_STEP = 0.001, 0.9, 0.999, 1e-08, 0.01, 10

NT_DIMS = (((1,), (1,)), ((), ()))
TN_DIMS = (((0,), (0,)), ((), ()))
NN_DIMS = (((1,), (0,)), ((), ()))


def _place():
    return lax.axis_index("x"), lax.axis_index("y"), lax.axis_index("c")


def _flip(coord, bit):
    return 1 - coord if bit else coord


def _params(*sem):
    return pltpu.CompilerParams(dimension_semantics=sem)


def _small_allgather(name, v):
    n = v.shape[-1]

    def body(v_ref, out_ref, send_sems, recv_sems):
        x, y, c = _place()
        me = 4 * x + 2 * y + c
        out_ref[me] = v_ref[...]
        copies = []
        for k in range(1, N_DEV):
            peer = (_flip(x, (k >> 2) & 1), _flip(y, (k >> 1) & 1), _flip(c, k & 1))
            cp = pltpu.make_async_remote_copy(
                src_ref=v_ref, dst_ref=out_ref.at[me], send_sem=send_sems.at[k - 1],
                recv_sem=recv_sems.at[k - 1], device_id=peer, device_id_type=MESH)
            cp.start()
            copies.append(cp)
        for cp in copies:
            cp.wait()

    return pl.pallas_call(
        body, name=name,
        out_shape=jax.ShapeDtypeStruct((N_DEV, 1, n), F32),
        in_specs=[pl.BlockSpec(memory_space=pltpu.VMEM)],
        out_specs=pl.BlockSpec(memory_space=pltpu.VMEM),
        scratch_shapes=[pltpu.SemaphoreType.DMA((N_DEV - 1,)), pltpu.SemaphoreType.DMA((N_DEV - 1,))],
    )(v)


class _Plan:
    def __init__(self, operands, out_shapes, sems, phases):
        self.operands, self.out_shapes, self.sems, self.phases = operands, out_shapes, sems, phases


def _slab_start(base, rows, jump, idx):
    return pl.multiple_of(base + idx * rows + (idx // 4) * jump, 16)


def _gather_plan(shards, dst_of, base_of, dst_shapes, jump_of=None):
    n = len(shards)
    rows = [s.shape[0] for s in shards]
    jump_of = jump_of or [0] * n

    def phases(srcs, dsts, sems):
        send_sems, recv_sems, local_sems = sems
        x, y, c = _place()
        me, sibling = (x, y, c), (x, y, 1 - c)
        chips = [(1 - x, y), (x, 1 - y), (1 - x, 1 - y)]

        def slab(i, px, py, pc):
            start = _slab_start(base_of[i], rows[i], jump_of[i], 4 * px + 2 * py + pc)
            return dsts[dst_of[i]].at[pl.ds(start, rows[i])]

        def copy(i, k, block, to, src=None):
            return pltpu.make_async_remote_copy(
                src_ref=slab(i, *block) if src is None else src, dst_ref=slab(i, *block),
                send_sem=send_sems.at[i, k], recv_sem=recv_sems.at[i, k],
                device_id=to, device_id_type=MESH)

        def mine():
            return [pltpu.make_async_copy(srcs[i], slab(i, *me), local_sems.at[i]) for i in range(n)]

        def first():
            out = []
            for i in range(n):
                out.append(copy(i, 0, me, sibling, src=srcs[i]))
                out += [copy(i, 1 + j, me, (*chip, c), src=srcs[i]) for j, chip in enumerate(chips)]
            return out

        def passed():
            return [(copy(i, 1 + j, (*chip, c), me), copy(i, 4 + j, (*chip, c), sibling))
                    for j, chip in enumerate(chips) for i in range(n)]

        def start():
            for cp in mine() + first():
                cp.start()

        def middle():
            for landed, onward in passed():
                landed.wait_recv()
                onward.start()

        def finish():
            for i in range(n):
                copy(i, 0, sibling, me).wait_recv()
                for j, chip in enumerate(chips):
                    copy(i, 4 + j, (*chip, 1 - c), me).wait_recv()
            for cp in first() + [onward for _, onward in passed()]:
                cp.wait_send()
            for cp in mine():
                cp.wait()

        return start, middle, finish

    sems = [pltpu.SemaphoreType.DMA((n, 7)), pltpu.SemaphoreType.DMA((n, 7)), pltpu.SemaphoreType.DMA((n,))]
    return _Plan(list(shards), [jax.ShapeDtypeStruct(s, BF16) for s in dst_shapes], sems, phases)


def _scatter_plan(grads, src_of, base_of, rows, cols, jump_of=None):
    n = len(rows)
    jump_of = jump_of or [0] * n

    def phases(srcs, recvs, sems):
        send_sems, recv_sems, local_sems = sems
        x, y, c = _place()
        me = 4 * x + 2 * y + c

        def slab(i, idx):
            start = _slab_start(base_of[i], rows[i], jump_of[i], idx)
            return srcs[src_of[i]].at[pl.ds(start, rows[i])]

        def copies():
            out = [pltpu.make_async_copy(slab(i, me), recvs[i].at[me], local_sems.at[i]) for i in range(n)]
            for k in range(1, N_DEV):
                px, py, pc = _flip(x, (k >> 2) & 1), _flip(y, (k >> 1) & 1), _flip(c, k & 1)
                out += [pltpu.make_async_remote_copy(
                    src_ref=slab(i, 4 * px + 2 * py + pc), dst_ref=recvs[i].at[me],
                    send_sem=send_sems.at[i, k - 1], recv_sem=recv_sems.at[i, k - 1],
                    device_id=(px, py, pc), device_id_type=MESH) for i in range(n)]
            return out

        def start():
            for cp in copies():
                cp.start()

        def finish():
            for cp in copies():
                cp.wait()

        return start, None, finish

    sems = [pltpu.SemaphoreType.DMA((n, 7)), pltpu.SemaphoreType.DMA((n, 7)), pltpu.SemaphoreType.DMA((n,))]
    out_shapes = [jax.ShapeDtypeStruct((N_DEV, rows[i], cols[i]), BF16) for i in range(n)]
    return _Plan(list(grads), out_shapes, sems, phases)


def _run_plan(name, plan):
    n_in, n_out = len(plan.operands), len(plan.out_shapes)

    def body(*refs):
        for phase in plan.phases(refs[:n_in], refs[n_in:n_in + n_out], refs[n_in + n_out:]):
            if phase is not None:
                phase()

    hbm = pl.BlockSpec(memory_space=pltpu.HBM)
    return pl.pallas_call(
        body, name=name, out_shape=plan.out_shapes,
        in_specs=[hbm] * n_in, out_specs=[hbm] * n_out, scratch_shapes=plan.sems,
    )(*plan.operands)


def _sum_contributions(name, recv):
    _, rows, cols = recv.shape
    tr = rows if rows <= 512 else 304 if rows % 304 == 0 else 256

    def body(r_ref, o_ref):
        acc = r_ref[0].astype(F32)
        for k in range(1, N_DEV):
            acc = acc + r_ref[k].astype(F32)
        o_ref[...] = acc

    return pl.pallas_call(
        body, name=name, grid=(rows // tr,),
        out_shape=jax.ShapeDtypeStruct((rows, cols), F32),
        in_specs=[pl.BlockSpec((N_DEV, tr, cols), lambda i: (0, i, 0))],
        out_specs=pl.BlockSpec((tr, cols), lambda i: (i, 0)),
        compiler_params=_params("parallel"),
    )(recv)


def _mm(name, a, b, mode, out_dtype, tm, tn, tk, *, carry=None, tiles_in=(), tiles_out=(), epilogue=None,
        n_outer=False, keep_b=False, sub=1):
    if mode == "TN":
        kk, m = a.shape
    else:
        m, kk = a.shape
    n = b.shape[0] if mode == "NT" else b.shape[1]
    tm, tn, tk = min(tm, m), min(tn, n), min(tk, kk)
    assert m % tm == 0 and n % tn == 0 and kk % tk == 0, (name, m, n, kk, tm, tn, tk)
    ni, nj, nk = m // tm, n // tn, kk // tk
    steps = ni * nj * nk
    dims = {"NN": NN_DIMS, "NT": NT_DIMS, "TN": TN_DIMS}[mode]
    if epilogue is None:
        tiles_out = [(jax.ShapeDtypeStruct((m, n), out_dtype), (tm, tn), lambda i, j: (i, j))]
    n_tin, n_tout = len(tiles_in), len(tiles_out)
    n_in = len(carry.operands) if carry else 0
    n_out = len(carry.out_shapes) if carry else 0
    n_acc = 1 if nk > 1 else 0
    n_keep = 2 if keep_b else 0
    assert not carry or steps >= 3
    assert not keep_b or (nk == 1 and nj == 1)
    assert sub == 1 or (mode != "TN" and tm % (8 * sub) == 0)
    ij = (lambda p, q: (q, p)) if n_outer else (lambda p, q: (p, q))
    inner = ni if n_outer else nj
    rs = tm // sub

    def body(a_ref, b_ref, *rest):
        tin = rest[:n_tin]
        cin = rest[n_tin:n_tin + n_in]
        tout = rest[n_tin + n_in:n_tin + n_in + n_tout]
        cout = rest[n_tin + n_in + n_tout:n_tin + n_in + n_tout + n_out]
        scratch = rest[n_tin + n_in + n_tout + n_out:]
        k = pl.program_id(2)
        visit = pl.program_id(0) * inner + pl.program_id(1)
        step = visit * nk + k
        if keep_b:
            b_kept, b_sem = scratch[n_acc:n_acc + 2]

            @pl.when(step == 0)
            def _():
                cp = pltpu.make_async_copy(b_ref, b_kept, b_sem)
                cp.start()
                cp.wait()

            b_ref = b_kept
        if carry:
            start, middle, finish = carry.phases(cin, cout, scratch[n_acc + n_keep:])
            pl.when(step == 0)(start)

        def rows_of(refs, specs, c):
            if sub == 1:
                return refs
            return [r.at[pl.ds(c * rs, rs)] if t[1][0] == tm else r for r, t in zip(refs, specs)]

        for c in range(sub):
            a_rows = a_ref[...] if sub == 1 else a_ref[pl.ds(c * rs, rs), :]
            part = lax.dot_general(a_rows, b_ref[...], dims, preferred_element_type=F32)
            tin_c, tout_c = rows_of(tin, tiles_in, c), rows_of(tout, tiles_out, c)

            def store(prod, tin_c=tin_c, tout_c=tout_c, c=c):
                if epilogue is None:
                    tout_c[0][...] = prod.astype(out_dtype)
                else:
                    epilogue(prod, jnp.logical_and(visit == 0, c == 0), tin_c, tout_c)

            if nk == 1:
                store(part)
            else:
                acc_c = scratch[0] if sub == 1 else scratch[0].at[pl.ds(c * rs, rs)]

                @pl.when(k == 0)
                def _(acc_c=acc_c, part=part):
                    acc_c[...] = part

                @pl.when((k > 0) & (k < nk - 1))
                def _(acc_c=acc_c, part=part):
                    acc_c[...] += part

                @pl.when(k == nk - 1)
                def _(acc_c=acc_c, part=part, store=store):
                    store(acc_c[...] + part)

        if carry:
            if middle is not None:
                pl.when(step == (steps * 3) // 5)(middle)
            pl.when(step == steps - 1)(finish)

    def spec(shape, fn):
        return pl.BlockSpec(shape, lambda p, q, k: fn(*ij(p, q)))

    a_spec = (pl.BlockSpec((tk, tm), lambda p, q, k: (k, ij(p, q)[0])) if mode == "TN"
              else pl.BlockSpec((tm, tk), lambda p, q, k: (ij(p, q)[0], k)))
    if keep_b:
        b_spec = pl.BlockSpec(memory_space=pl.ANY)
    elif mode == "NT":
        b_spec = pl.BlockSpec((tn, tk), lambda p, q, k: (ij(p, q)[1], k))
    else:
        b_spec = pl.BlockSpec((tk, tn), lambda p, q, k: (k, ij(p, q)[1]))
    hbm = pl.BlockSpec(memory_space=pltpu.HBM)
    sequential = carry or epilogue or keep_b
    out = pl.pallas_call(
        body, name=name, grid=(nj, ni, nk) if n_outer else (ni, nj, nk),
        out_shape=[t[0] for t in tiles_out] + (carry.out_shapes if carry else []),
        in_specs=[a_spec, b_spec] + [spec(t[1], t[2]) for t in tiles_in] + [hbm] * n_in,
        out_specs=[spec(t[1], t[2]) for t in tiles_out] + [hbm] * n_out,
        scratch_shapes=([pltpu.VMEM((tm, tn), F32)] * n_acc
                        + ([pltpu.VMEM(b.shape, b.dtype), pltpu.SemaphoreType.DMA] if keep_b else [])
                        + (carry.sems if carry else [])),
        compiler_params=(_params("arbitrary", "arbitrary", "arbitrary") if sequential
                         else _params("parallel", "parallel", "arbitrary")),
    )(a, b, *[t[0] for t in tiles_in], *(carry.operands if carry else []))
    return out if (carry or epilogue) else out[0]


def _mm_lagged(name, a, b, mode, tm, tn, *, tiles_in, tiles_out, epilogue, carry=None, n_outer=False,
               keep_b=False):
    m, kk = a.shape
    n = b.shape[0] if mode == "NT" else b.shape[1]
    tm, tn = min(tm, m), min(tn, n)
    assert mode in ("NN", "NT") and m % tm == 0 and n % tn == 0, (name, m, n, tm, tn)
    ni, nj = m // tm, n // tn
    tiles = ni * nj
    dims = NN_DIMS if mode == "NN" else NT_DIMS
    n_tin, n_tout = len(tiles_in), len(tiles_out)
    n_in = len(carry.operands) if carry else 0
    n_out = len(carry.out_shapes) if carry else 0
    n_keep = 2 if keep_b else 0
    assert not keep_b or nj == 1

    def place(t):
        return (t % ni, t // ni) if n_outer else (t // nj, t % nj)

    def current(v):
        return place(jnp.minimum(v, tiles - 1))

    def previous(v):
        return place(jnp.maximum(v - 1, 0))

    def body(a_ref, b_ref, *rest):
        tin = rest[:n_tin]
        cin = rest[n_tin:n_tin + n_in]
        tout = rest[n_tin + n_in:n_tin + n_in + n_tout]
        cout = rest[n_tin + n_in + n_tout:n_tin + n_in + n_tout + n_out]
        scratch = rest[n_tin + n_in + n_tout + n_out:]
        prod_ref = scratch[0]
        v = pl.program_id(0)
        if keep_b:
            b_kept, b_sem = scratch[1:3]

            @pl.when(v == 0)
            def _():
                cp = pltpu.make_async_copy(b_ref, b_kept, b_sem)
                cp.start()
                cp.wait()

            b_ref = b_kept
        if carry:
            start, middle, finish = carry.phases(cin, cout, scratch[1 + n_keep:])
            pl.when(v == 0)(start)

        @pl.when(v == 0)
        def _():
            prod_ref[1] = jnp.zeros((tm, tn), F32)

        for parity in (0, 1):
            @pl.when(v % 2 == parity)
            def _(parity=parity):
                prod_ref[parity] = lax.dot_general(a_ref[...], b_ref[...], dims, preferred_element_type=F32)
                epilogue(prod_ref[1 - parity], v <= 1, tin, tout)

        if carry:
            if middle is not None:
                pl.when(v == (tiles * 3) // 5)(middle)
            pl.when(v == tiles)(finish)

    a_spec = pl.BlockSpec((tm, kk), lambda v: (current(v)[0], 0))
    if keep_b:
        b_spec = pl.BlockSpec(memory_space=pl.ANY)
    elif mode == "NT":
        b_spec = pl.BlockSpec((tn, kk), lambda v: (current(v)[1], 0))
    else:
        b_spec = pl.BlockSpec((kk, tn), lambda v: (0, current(v)[1]))

    def spec(shape, fn):
        return pl.BlockSpec(shape, lambda v: fn(*previous(v)))

    hbm = pl.BlockSpec(memory_space=pltpu.HBM)
    return pl.pallas_call(
        body, name=name, grid=(tiles + 1,),
        out_shape=[t[0] for t in tiles_out] + (carry.out_shapes if carry else []),
        in_specs=[a_spec, b_spec] + [spec(t[1], t[2]) for t in tiles_in] + [hbm] * n_in,
        out_specs=[spec(t[1], t[2]) for t in tiles_out] + [hbm] * n_out,
        scratch_shapes=([pltpu.VMEM((2, tm, tn), F32)]
                        + ([pltpu.VMEM(b.shape, b.dtype), pltpu.SemaphoreType.DMA] if keep_b else [])
                        + (carry.sems if carry else [])),
        compiler_params=_params("arbitrary"),
    )(a, b, *[t[0] for t in tiles_in], *(carry.operands if carry else []))


def _row(tm, w, off=0):
    return pl.BlockSpec((tm, w), lambda i: (i, off))


def _vec(w):
    return pl.BlockSpec((1, w), lambda i: (0, 0))


def _sigmoid(x):
    return 0.5 * jnp.tanh(0.5 * x) + 0.5


def _normmod(name, x, g, sc, sh, tm=512):
    s = x.shape[0]

    def body(x_ref, g_ref, sc_ref, sh_ref, h_ref):
        xv = x_ref[...]
        r = lax.rsqrt(jnp.mean(xv * xv, axis=-1, keepdims=True) + EPS)
        h_ref[...] = ((xv * r) * g_ref[...] * (1.0 + sc_ref[...]) + sh_ref[...]).astype(BF16)

    return pl.pallas_call(
        body, name=name, grid=(s // tm,),
        out_shape=jax.ShapeDtypeStruct((s, D), BF16),
        in_specs=[_row(tm, D), _vec(D), _vec(D), _vec(D)], out_specs=_row(tm, D),
        compiler_params=_params("parallel"),
    )(x, g, sc, sh)


def _normmod_bwd(name, dh, x, gin, g, sc, sh, tm=512):
    s = x.shape[0]

    def body(dh_ref, x_ref, gin_ref, g_ref, sc_ref, sh_ref, gout_ref, acc_ref):
        xv, dhv = x_ref[...], dh_ref[...]
        r = lax.rsqrt(jnp.mean(xv * xv, axis=-1, keepdims=True) + EPS)
        nv = xv * r
        gv, one_sc = g_ref[...], 1.0 + sc_ref[...]
        dn = dhv * gv * one_sc
        dx = r * (dn - nv * jnp.mean(dn * nv, axis=-1, keepdims=True))
        gout_ref[...] = gin_ref[...] + dx

        @pl.when(pl.program_id(0) == 0)
        def _():
            acc_ref[...] = jnp.zeros_like(acc_ref)

        dhn = dhv * nv
        acc_ref[0:1, :] += jnp.sum(dhv, axis=0, keepdims=True)
        acc_ref[1:2, :] += jnp.sum(dhn * gv, axis=0, keepdims=True)
        acc_ref[2:3, :] += jnp.sum(dhn * one_sc, axis=0, keepdims=True)

    return pl.pallas_call(
        body, name=name, grid=(s // tm,),
        out_shape=[jax.ShapeDtypeStruct((s, D), F32), jax.ShapeDtypeStruct((8, D), F32)],
        in_specs=[_row(tm, D), _row(tm, D), _row(tm, D), _vec(D), _vec(D), _vec(D)],
        out_specs=[_row(tm, D), pl.BlockSpec((8, D), lambda i: (0, 0))],
        compiler_params=_params("arbitrary"),
    )(dh, x, gin, g, sc, sh)


def _swiglu(name, ab, tm=512):
    s = ab.shape[0]

    def body(ab_ref, s_ref):
        a = ab_ref[:, :FF].astype(F32)
        b = ab_ref[:, FF:].astype(F32)
        s_ref[...] = (a * _sigmoid(a) * b).astype(BF16)

    return pl.pallas_call(
        body, name=name, grid=(s // tm,),
        out_shape=jax.ShapeDtypeStruct((s, FF), BF16),
        in_specs=[_row(tm, 2 * FF)], out_specs=_row(tm, FF),
        compiler_params=_params("parallel"),
    )(ab)


def _swiglu_bwd(name, ds, ab, tm=256):
    s = ab.shape[0]

    def body(ds_ref, ab_ref, dab_ref):
        a = ab_ref[:, :FF].astype(F32)
        b = ab_ref[:, FF:].astype(F32)
        dsv = ds_ref[...].astype(F32)
        sig = _sigmoid(a)
        dab_ref[:, :FF] = (dsv * b * (sig * (1.0 + a * (1.0 - sig)))).astype(BF16)
        dab_ref[:, FF:] = (dsv * (a * sig)).astype(BF16)

    return pl.pallas_call(
        body, name=name, grid=(s // tm,),
        out_shape=jax.ShapeDtypeStruct((s, 2 * FF), BF16),
        in_specs=[_row(tm, FF), _row(tm, 2 * FF)], out_specs=_row(tm, 2 * FF),
        compiler_params=_params("parallel"),
    )(ds, ab)


def _residual(name, x, f, gt, coef, tm=512):
    s = x.shape[0]

    def body(x_ref, f_ref, gt_ref, o_ref):
        o_ref[...] = x_ref[...] + (coef * gt_ref[...]) * f_ref[...]

    return pl.pallas_call(
        body, name=name, grid=(s // tm,),
        out_shape=jax.ShapeDtypeStruct((s, D), F32),
        in_specs=[_row(tm, D), _row(tm, D), _vec(D)], out_specs=_row(tm, D),
        compiler_params=_params("parallel"),
    )(x, f, gt)


def _gate_bwd(name, gin, f, gt, coef, tm=512):
    s = gin.shape[0]

    def body(g_ref, f_ref, gt_ref, df_ref, acc_ref):
        gv = g_ref[...]
        df_ref[...] = ((coef * gt_ref[...]) * gv).astype(BF16)

        @pl.when(pl.program_id(0) == 0)
        def _():
            acc_ref[...] = jnp.zeros_like(acc_ref)

        acc_ref[0:1, :] += coef * jnp.sum(gv * f_ref[...], axis=0, keepdims=True)

    return pl.pallas_call(
        body, name=name, grid=(s // tm,),
        out_shape=[jax.ShapeDtypeStruct((s, D), BF16), jax.ShapeDtypeStruct((8, D), F32)],
        in_specs=[_row(tm, D), _row(tm, D), _vec(D)],
        out_specs=[_row(tm, D), pl.BlockSpec((8, D), lambda i: (0, 0))],
        compiler_params=_params("arbitrary"),
    )(gin, f, gt)


def _loss_grad(x3, target, tm=512):
    s = x3.shape[0]

    def body(y_ref, t_ref, g_ref, l_ref):
        e = y_ref[...] - t_ref[...]
        g_ref[...] = e * (1.0 / D)

        @pl.when(pl.program_id(0) == 0)
        def _():
            l_ref[...] = jnp.zeros_like(l_ref)

        l_ref[...] += jnp.sum(jnp.mean(e * e, axis=-1, keepdims=True), axis=0, keepdims=True) * 0.5

    return pl.pallas_call(
        body, name="loss_grad", grid=(s // tm,),
        out_shape=[jax.ShapeDtypeStruct((s, D), F32), jax.ShapeDtypeStruct((8, 128), F32)],
        in_specs=[_row(tm, D), _row(tm, D)],
        out_specs=[_row(tm, D), pl.BlockSpec((8, 128), lambda i: (0, 0))],
        compiler_params=_params("arbitrary"),
    )(x3, target)


def _heads(x, fn):
    return jnp.concatenate([fn(x[:, h * HD:(h + 1) * HD], h) for h in range(COL // HD)], axis=1)


def _qknorm(proj, wqk, tm=1024):
    s = proj.shape[0]

    def body(p_ref, w_ref, o_ref):
        pv = p_ref[...].astype(F32)
        wv = w_ref[...]

        def one(qh, h):
            r = lax.rsqrt(jnp.mean(qh * qh, axis=-1, keepdims=True) + EPS)
            return (qh * r) * wv[:, h * HD:(h + 1) * HD]

        o_ref[...] = _heads(pv, one).astype(BF16)

    return pl.pallas_call(
        body, name="qknorm", grid=(s // tm, QKW // COL),
        out_shape=jax.ShapeDtypeStruct((s, QKW), BF16),
        in_specs=[pl.BlockSpec((tm, COL), lambda i, j: (i, j)), pl.BlockSpec((1, COL), lambda i, j: (0, j))],
        out_specs=pl.BlockSpec((tm, COL), lambda i, j: (i, j)),
        compiler_params=_params("parallel", "parallel"),
    )(proj, wqk)


def _qknorm_bwd(proj, dqkn, wqk, dproj, tm=1024):
    s = proj.shape[0]

    def body(p_ref, d_ref, w_ref, _, o_ref, acc_ref):
        pv = p_ref[...].astype(F32)
        dv = d_ref[...]
        wv = w_ref[...]
        sums = []

        def one(qh, h):
            dn = dv[:, h * HD:(h + 1) * HD]
            r = lax.rsqrt(jnp.mean(qh * qh, axis=-1, keepdims=True) + EPS)
            nh = qh * r
            sums.append(jnp.sum(dn * nh, axis=0, keepdims=True))
            dnw = dn * wv[:, h * HD:(h + 1) * HD]
            return r * (dnw - nh * jnp.mean(dnw * nh, axis=-1, keepdims=True))

        o_ref[...] = _heads(pv, one).astype(BF16)

        @pl.when(pl.program_id(1) == 0)
        def _():
            acc_ref[...] = jnp.zeros_like(acc_ref)

        acc_ref[0:1, :] += jnp.concatenate(sums, axis=1)

    return pl.pallas_call(
        body, name="qknorm_bwd", grid=(QKW // COL, s // tm),
        out_shape=[jax.ShapeDtypeStruct((s, IN_W), BF16), jax.ShapeDtypeStruct((8, QKW), F32)],
        in_specs=[pl.BlockSpec((tm, COL), lambda j, i: (i, j)), pl.BlockSpec((tm, COL), lambda j, i: (i, j)),
                  pl.BlockSpec((1, COL), lambda j, i: (0, j)), pl.BlockSpec(memory_space=pl.ANY)],
        out_specs=[pl.BlockSpec((tm, COL), lambda j, i: (i, j)), pl.BlockSpec((8, COL), lambda j, i: (0, j))],
        input_output_aliases={3: 0},
        compiler_params=_params("arbitrary", "arbitrary"),
    )(proj, dqkn, wqk, dproj)


def _attn_shapes(s, g):
    d = DILATIONS[g]
    tb = min(s, max(2048, 256 * d))
    sb = min(256, tb // d)
    pb = BAND * d
    assert s % tb == 0 and tb % pb == 0 and (tb // d) % sb == 0 and sb % BAND == 0
    return d, tb, sb, pb


def _lanes(x, width):
    return jnp.concatenate([x] * (width // HD), axis=1)


def _every(start, size, d):
    return pl.ds(start, size, stride=d) if d > 1 else pl.ds(start, size)


def _attn_specs(g, tb, pb, s, ahead):
    ratio = tb // pb
    if ahead:
        nbr = lambda n: jnp.minimum((n + 1) * ratio, s // pb - 1)
    else:
        nbr = lambda n: jnp.maximum(n * ratio - 1, 0)
    cur = lambda base: pl.BlockSpec((tb, HD), lambda h, n: (n, base + g * N_HEADS + h))
    side = lambda base: pl.BlockSpec((pb, HD), lambda h, n: (nbr(n), base + g * N_HEADS + h))
    tok = pl.BlockSpec((tb, HD), lambda h, n: (n, h))
    tok_side = pl.BlockSpec((pb, HD), lambda h, n: (nbr(n), h))
    return cur, side, tok, tok_side


Q_COL, K_COL, V_COL = 0, 12, 24


def _attn_fwd(g, qkn, proj):
    s = qkn.shape[0]
    d, tb, sb, pb = _attn_shapes(s, g)
    ft = F32 if d > 1 else BF16
    nj = tb // d // sb
    scale = HD ** -0.5

    def body(q_ref, kc_ref, kp_ref, vc_ref, vp_ref, o_ref, lse_ref, qf, kf, vf):
        n = pl.program_id(1)
        qf[...] = q_ref[...].astype(ft)
        kf[0:pb] = kp_ref[...].astype(ft)
        kf[pb:] = kc_ref[...].astype(ft)
        vf[0:pb] = vp_ref[...].astype(ft)
        vf[pb:] = vc_ref[...].astype(ft)
        for r in range(d):
            for j in range(nj):
                at = j * sb * d + r
                q = qf[_every(at, sb, d), :].astype(BF16)
                k = kf[_every(at, sb + BAND, d), :].astype(BF16)
                v = vf[_every(at, sb + BAND, d), :].astype(BF16)
                sc = lax.dot_general(q, k, NT_DIMS, preferred_element_type=F32) * scale
                qi = lax.broadcasted_iota(jnp.int32, sc.shape, 0)
                kj = lax.broadcasted_iota(jnp.int32, sc.shape, 1)
                valid = (kj >= qi) & (kj <= qi + BAND)
                if j == 0:
                    valid = valid & ((kj >= BAND) | (n > 0))
                sc = jnp.where(valid, sc, -1e30)
                m = jnp.max(sc, axis=-1, keepdims=True)
                p = jnp.exp(sc - m)
                l = jnp.sum(p, axis=-1, keepdims=True)
                o = lax.dot_general(p.astype(BF16), v, NN_DIMS, preferred_element_type=F32)
                o_ref[_every(at, sb, d), :] = o / l
                lse_ref[_every(at, sb, d), :] = jnp.broadcast_to(m + jnp.log(l), (sb, HD))

    cur, side, tok, _ = _attn_specs(g, tb, pb, s, ahead=False)
    return pl.pallas_call(
        body, name=f"attn_fwd_g{g}", grid=(N_HEADS, s // tb),
        out_shape=[jax.ShapeDtypeStruct((s, COL), F32)] * 2,
        in_specs=[cur(Q_COL), cur(K_COL), side(K_COL), cur(V_COL), side(V_COL)],
        out_specs=[tok, tok],
        scratch_shapes=[pltpu.VMEM((tb, HD), ft), pltpu.VMEM((tb + pb, HD), ft),
                        pltpu.VMEM((tb + pb, HD), ft)],
        compiler_params=_params("parallel", "arbitrary"),
    )(qkn, qkn, qkn, proj, proj)


def _attn_combine(os_, lses, tm=512):
    s = os_[0].shape[0]

    def body(o0, o1, o2, l0, l1, l2, o_ref, lse_ref):
        a, b, c = l0[...], l1[...], l2[...]
        m = jnp.maximum(jnp.maximum(a, b), c)
        ea, eb, ec = jnp.exp(a - m), jnp.exp(b - m), jnp.exp(c - m)
        tot = ea + eb + ec
        o_ref[...] = ((ea * o0[...] + eb * o1[...] + ec * o2[...]) / tot).astype(BF16)
        lse_ref[...] = m + jnp.log(tot)

    return pl.pallas_call(
        body, name="attn_combine", grid=(s // tm,),
        out_shape=[jax.ShapeDtypeStruct((s, COL), BF16), jax.ShapeDtypeStruct((s, COL), F32)],
        in_specs=[_row(tm, COL)] * 6, out_specs=[_row(tm, COL)] * 2,
        compiler_params=_params("parallel"),
    )(*os_, *lses)


def _attn_delta(do, o, tm=512):
    s = do.shape[0]

    def body(do_ref, o_ref, del_ref):
        prod = do_ref[...] * o_ref[...].astype(F32)
        del_ref[...] = _heads(prod, lambda ph, h: jnp.broadcast_to(
            jnp.sum(ph, axis=-1, keepdims=True), ph.shape))

    return pl.pallas_call(
        body, name="attn_delta", grid=(s // tm,),
        out_shape=jax.ShapeDtypeStruct((s, COL), F32),
        in_specs=[_row(tm, COL)] * 2, out_specs=_row(tm, COL),
        compiler_params=_params("parallel"),
    )(do, o)


def _attn_dq(g, qkn, proj, do, lse, delta, dqkn):
    s = qkn.shape[0]
    d, tb, sb, pb = _attn_shapes(s, g)
    ft = F32 if d > 1 else BF16
    nj = tb // d // sb
    scale = HD ** -0.5
    chained = dqkn is not None

    def body(q_ref, kc_ref, kp_ref, vc_ref, vp_ref, do_ref, lse_ref, del_ref, *rest):
        dq_ref, qf, kf, vf = rest[-4:]
        n = pl.program_id(1)
        qf[...] = q_ref[...].astype(ft)
        kf[0:pb] = kp_ref[...].astype(ft)
        kf[pb:] = kc_ref[...].astype(ft)
        vf[0:pb] = vp_ref[...].astype(ft)
        vf[pb:] = vc_ref[...].astype(ft)
        for r in range(d):
            for j in range(nj):
                at = j * sb * d + r
                rows = _every(at, sb, d)
                q = qf[rows, :].astype(BF16)
                k = kf[_every(at, sb + BAND, d), :].astype(BF16)
                v = vf[_every(at, sb + BAND, d), :].astype(BF16)
                sc = lax.dot_general(q, k, NT_DIMS, preferred_element_type=F32) * scale
                qi = lax.broadcasted_iota(jnp.int32, sc.shape, 0)
                kj = lax.broadcasted_iota(jnp.int32, sc.shape, 1)
                valid = (kj >= qi) & (kj <= qi + BAND)
                if j == 0:
                    valid = valid & ((kj >= BAND) | (n > 0))
                p = jnp.exp(jnp.where(valid, sc - _lanes(lse_ref[rows, :], sb + BAND), -1e30))
                dp = lax.dot_general(do_ref[rows, :].astype(BF16), v, NT_DIMS, preferred_element_type=F32)
                ds = p * (dp - _lanes(del_ref[rows, :], sb + BAND)) * scale
                dq_ref[rows, :] = lax.dot_general(ds.astype(BF16), k, NN_DIMS, preferred_element_type=F32)

    cur, side, tok, _ = _attn_specs(g, tb, pb, s, ahead=False)
    args = [qkn, qkn, qkn, proj, proj, do, lse, delta]
    specs = [cur(Q_COL), cur(K_COL), side(K_COL), cur(V_COL), side(V_COL), tok, tok, tok]
    if chained:
        args.append(dqkn)
        specs.append(pl.BlockSpec(memory_space=pl.ANY))
    return pl.pallas_call(
        body, name=f"attn_dq_g{g}", grid=(N_HEADS, s // tb),
        out_shape=jax.ShapeDtypeStruct((s, QKW), F32),
        in_specs=specs, out_specs=cur(Q_COL),
        input_output_aliases={8: 0} if chained else {},
        scratch_shapes=[pltpu.VMEM((tb, HD), ft), pltpu.VMEM((tb + pb, HD), ft),
                        pltpu.VMEM((tb + pb, HD), ft)],
        compiler_params=_params("arbitrary", "arbitrary"),
    )(*args)


def _attn_dkv(g, qkn, proj, do, lse, delta, dqkn, dproj):
    s = qkn.shape[0]
    d, tb, sb, pb = _attn_shapes(s, g)
    ft = F32 if d > 1 else BF16
    nj = tb // d // sb
    nt = s // tb
    scale = HD ** -0.5

    def body(k_ref, v_ref, qc_ref, qn_ref, doc_ref, don_ref, lc_ref, ln_ref, dc_ref, dn_ref, _a, _b,
             dk_ref, dv_ref, kf, vf, qf, dvf):
        n = pl.program_id(1)
        kf[...] = k_ref[...].astype(ft)
        vf[...] = v_ref[...].astype(ft)
        qf[0:tb] = qc_ref[...].astype(ft)
        qf[tb:] = qn_ref[...].astype(ft)

        def window(c_ref, n_ref, r, j):
            at = j * sb * d + r
            if j < nj - 1:
                return c_ref[_every(at, sb + BAND, d), :]
            return jnp.concatenate([c_ref[_every(at, sb, d), :], n_ref[_every(r, BAND, d), :]], axis=0)

        for r in range(d):
            for j in range(nj):
                at = j * sb * d + r
                rows = _every(at, sb, d)
                k = kf[rows, :].astype(BF16)
                v = vf[rows, :].astype(BF16)
                q = qf[_every(at, sb + BAND, d), :].astype(BF16)
                dov = window(doc_ref, don_ref, r, j).astype(BF16)
                sc = lax.dot_general(q, k, NT_DIMS, preferred_element_type=F32) * scale
                qi = lax.broadcasted_iota(jnp.int32, sc.shape, 0)
                kj = lax.broadcasted_iota(jnp.int32, sc.shape, 1)
                valid = (qi >= kj) & (qi <= kj + BAND)
                if j == nj - 1:
                    valid = valid & ((qi < sb) | (n < nt - 1))
                p = jnp.exp(jnp.where(valid, sc - _lanes(window(lc_ref, ln_ref, r, j), sb), -1e30))
                dp = lax.dot_general(dov, v, NT_DIMS, preferred_element_type=F32)
                ds = p * (dp - _lanes(window(dc_ref, dn_ref, r, j), sb)) * scale
                dvf[rows, :] = lax.dot_general(p.astype(BF16), dov, TN_DIMS, preferred_element_type=F32)
                dk_ref[rows, :] = lax.dot_general(ds.astype(BF16), q, TN_DIMS, preferred_element_type=F32)
        dv_ref[...] = dvf[...].astype(BF16)

    cur, side, tok, tok_side = _attn_specs(g, tb, pb, s, ahead=True)
    anyspec = pl.BlockSpec(memory_space=pl.ANY)
    return pl.pallas_call(
        body, name=f"attn_dkv_g{g}", grid=(N_HEADS, nt),
        out_shape=[jax.ShapeDtypeStruct((s, QKW), F32), jax.ShapeDtypeStruct((s, IN_W), BF16)],
        in_specs=[cur(K_COL), cur(V_COL), cur(Q_COL), side(Q_COL), tok, tok_side, tok, tok_side,
                  tok, tok_side, anyspec, anyspec],
        out_specs=[cur(K_COL), cur(V_COL)],
        input_output_aliases={10: 0, 11: 1},
        scratch_shapes=[pltpu.VMEM((tb, HD), ft), pltpu.VMEM((tb, HD), ft),
                        pltpu.VMEM((tb + pb, HD), ft), pltpu.VMEM((tb, HD), F32)],
        compiler_params=_params("arbitrary", "arbitrary"),
    )(qkn, proj, qkn, qkn, do, do, lse, lse, delta, delta, dqkn, dproj)


def _shift_down(x, before, k):
    rolled = pltpu.roll(x, k, 0)
    head = jnp.where(lax.broadcasted_iota(jnp.int32, before.shape, 0) < k, pltpu.roll(before, k, 0), rolled[:8])
    return jnp.concatenate([head, rolled[8:]], axis=0)


def _shift_up(x, after, k):
    rows = x.shape[0]
    rolled = pltpu.roll(x, rows - k, 0)
    tail = jnp.where(lax.broadcasted_iota(jnp.int32, after.shape, 0) >= 8 - k,
                     pltpu.roll(after, 8 - k, 0), rolled[rows - 8:])
    return jnp.concatenate([rolled[:rows - 8], tail], axis=0)


def _conv_fwd(proj, cw, tm=1024):
    s = proj.shape[0]
    r16 = tm // 16

    def body(u_ref, b_ref, c_ref, up_ref, cp_ref, w_ref, z_ref):
        i = pl.program_id(1)
        xc = c_ref[...].astype(F32) * u_ref[...].astype(F32)
        xp = jnp.where(i > 0, cp_ref[8:16, :].astype(F32) * up_ref[8:16, :].astype(F32), 0.0)
        w = w_ref[...]
        conv = _shift_down(xc, xp, 2) * w[0:1] + _shift_down(xc, xp, 1) * w[1:2] + xc * w[2:3]
        z_ref[...] = (b_ref[...].astype(F32) * conv).astype(BF16)

    tile = lambda blk: pl.BlockSpec((tm, COL), lambda j, i: (i, blk + j))
    before = lambda blk: pl.BlockSpec((16, COL), lambda j, i: (jnp.maximum(i * r16 - 1, 0), blk + j))
    return pl.pallas_call(
        body, name="conv_fwd", grid=(D // COL, s // tm),
        out_shape=jax.ShapeDtypeStruct((s, D), BF16),
        in_specs=[tile(U_BLK), tile(B_BLK), tile(C_BLK), before(U_BLK), before(C_BLK),
                  pl.BlockSpec((3, COL), lambda j, i: (0, j))],
        out_specs=pl.BlockSpec((tm, COL), lambda j, i: (i, j)),
        compiler_params=_params("parallel", "parallel"),
    )(proj, proj, proj, proj, proj, cw)


def _conv_bwd(dz, proj, cw, dproj, tm=1024):
    s = proj.shape[0]
    r8, r16 = tm // 8, tm // 16
    nrow = s // tm

    def body(dz_ref, u_ref, b_ref, c_ref, up_ref, cp_ref, dzn_ref, bn_ref, w_ref, _, o_ref, acc_ref):
        piece, i = pl.program_id(1), pl.program_id(2)
        u, c = u_ref[...].astype(F32), c_ref[...].astype(F32)
        bv = b_ref[...].astype(F32)
        dzv = dz_ref[...]
        w = w_ref[...]

        @pl.when((piece == 0) & (i == 0))
        def _():
            acc_ref[...] = jnp.zeros_like(acc_ref)

        @pl.when(piece == 1)
        def _():
            xc = c * u
            xp = jnp.where(i > 0, cp_ref[8:16, :].astype(F32) * up_ref[8:16, :].astype(F32), 0.0)
            x2, x1 = _shift_down(xc, xp, 2), _shift_down(xc, xp, 1)
            o_ref[...] = (dzv * (x2 * w[0:1] + x1 * w[1:2] + xc * w[2:3])).astype(BF16)
            dconv = dzv * bv
            acc_ref[0:1, :] += jnp.sum(dconv * x2, axis=0, keepdims=True)
            acc_ref[1:2, :] += jnp.sum(dconv * x1, axis=0, keepdims=True)
            acc_ref[2:3, :] += jnp.sum(dconv * xc, axis=0, keepdims=True)

        @pl.when(piece != 1)
        def _():
            dconv = dzv * bv
            dn = jnp.where(i < nrow - 1, dzn_ref[...] * bn_ref[0:8, :].astype(F32), 0.0)
            dxc = dconv * w[2:3] + _shift_up(dconv, dn, 1) * w[1:2] + _shift_up(dconv, dn, 2) * w[0:1]
            o_ref[...] = (dxc * jnp.where(piece == 0, c, u)).astype(BF16)

    tile = lambda blk: pl.BlockSpec((tm, COL), lambda j, p, i: (i, blk + j))
    before = lambda blk: pl.BlockSpec((16, COL), lambda j, p, i: (jnp.maximum(i * r16 - 1, 0), blk + j))
    after = lambda rows, blk: pl.BlockSpec(
        (rows, COL), lambda j, p, i: (jnp.minimum((i + 1) * (tm // rows), s // rows - 1), blk + j))
    return pl.pallas_call(
        body, name="conv_bwd", grid=(D // COL, 3, nrow),
        out_shape=[jax.ShapeDtypeStruct((s, IN_W), BF16), jax.ShapeDtypeStruct((8, D), F32)],
        in_specs=[tile(0), tile(U_BLK), tile(B_BLK), tile(C_BLK), before(U_BLK), before(C_BLK),
                  after(8, 0), after(16, B_BLK), pl.BlockSpec((3, COL), lambda j, p, i: (0, j)),
                  pl.BlockSpec(memory_space=pl.ANY)],
        out_specs=[pl.BlockSpec((tm, COL), lambda j, p, i: (i, U_BLK + 2 * p + j)),
                   pl.BlockSpec((8, COL), lambda j, p, i: (0, j))],
        input_output_aliases={9: 0},
        compiler_params=_params("arbitrary", "arbitrary", "arbitrary"),
    )(dz, proj, proj, proj, proj, proj, dz, proj, cw, dproj)


def _merge_fwd(ya, yc, proj, tm=512):
    s = proj.shape[0]

    def body(ya_ref, yc_ref, ga_ref, gc_ref, o_ref):
        o_ref[...] = (_sigmoid(ga_ref[...].astype(F32)) * ya_ref[...].astype(F32)
                      + _sigmoid(gc_ref[...].astype(F32)) * yc_ref[...].astype(F32)).astype(BF16)

    tile = lambda blk: pl.BlockSpec((tm, COL), lambda j, i: (i, blk + j))
    return pl.pallas_call(
        body, name="merge_fwd", grid=(D // COL, s // tm),
        out_shape=jax.ShapeDtypeStruct((s, D), BF16),
        in_specs=[tile(0), tile(0), tile(GA_BLK), tile(GC_BLK)], out_specs=tile(0),
        compiler_params=_params("parallel", "parallel"),
    )(ya, yc, proj, proj)


def _merge_bwd_branches(dm, proj, tm=512):
    s = proj.shape[0]

    def body(dm_ref, ga_ref, gc_ref, dya_ref, dyc_ref):
        dmv = dm_ref[...]
        dya_ref[...] = (dmv * _sigmoid(ga_ref[...].astype(F32))).astype(BF16)
        dyc_ref[...] = (dmv * _sigmoid(gc_ref[...].astype(F32))).astype(BF16)

    tile = lambda blk: pl.BlockSpec((tm, COL), lambda j, i: (i, blk + j))
    return pl.pallas_call(
        body, name="merge_bwd_branches", grid=(D // COL, s // tm),
        out_shape=[jax.ShapeDtypeStruct((s, D), BF16)] * 2,
        in_specs=[tile(0), tile(GA_BLK), tile(GC_BLK)], out_specs=[tile(0)] * 2,
        compiler_params=_params("parallel", "parallel"),
    )(dm, proj, proj)


def _merge_bwd_gates(dm, ya, yc, proj, tm=1024):
    s = proj.shape[0]
    half = D // COL

    def body(dm_ref, ya_ref, yc_ref, g_ref, o_ref):
        y = jnp.where(pl.program_id(0) < half, ya_ref[...].astype(F32), yc_ref[...].astype(F32))
        sig = _sigmoid(g_ref[...].astype(F32))
        o_ref[...] = (dm_ref[...] * y * sig * (1.0 - sig)).astype(BF16)

    chan = pl.BlockSpec((tm, COL), lambda jj, i: (i, jj % half))
    gate = pl.BlockSpec((tm, COL), lambda jj, i: (i, GA_BLK + jj))
    return pl.pallas_call(
        body, name="merge_bwd_gates", grid=(2 * half, s // tm),
        out_shape=jax.ShapeDtypeStruct((s, IN_W), BF16),
        in_specs=[chan, chan, chan, gate], out_specs=gate,
        compiler_params=_params("parallel", "parallel"),
    )(dm, ya, yc, proj)


def _mod_part(c_all, w_ada, b_part):
    def body(c_ref, w_ref, b_ref, o_ref):
        cv = c_ref[...]
        act = cv * _sigmoid(cv)
        o_ref[...] = jnp.dot(act, w_ref[...], preferred_element_type=F32,
                             precision=lax.Precision.HIGHEST) + b_ref[...]

    return pl.pallas_call(
        body, name="mod_part", out_shape=jax.ShapeDtypeStruct((N_DEV, w_ada.shape[1]), F32),
    )(c_all, w_ada, b_part)


def _w_ada_grad(c_all_t, dmod_part):
    def body(c_ref, d_ref, o_ref):
        cv = c_ref[...]
        act = cv * _sigmoid(cv)
        dv = d_ref[...]
        acc = act[:, 0:1] * dv[0:1, :]
        for b in range(1, N_DEV):
            acc = acc + act[:, b:b + 1] * dv[b:b + 1, :]
        o_ref[...] = acc

    return pl.pallas_call(
        body, name="w_ada_grad", out_shape=jax.ShapeDtypeStruct((D, dmod_part.shape[1]), F32),
    )(c_all_t, dmod_part)


def _sum_rows(name, v):
    def body(v_ref, o_ref):
        acc = v_ref[0]
        for k in range(1, N_DEV):
            acc = acc + v_ref[k]
        o_ref[...] = acc

    return pl.pallas_call(body, name=name, out_shape=jax.ShapeDtypeStruct(v.shape[1:], F32))(v)


def _adamw(name, w, g, m, v):
    rows, cols = w.shape
    limit = max(16, (1 << 20) // (4 * cols))
    tr = rows if rows <= limit else next((t for t in range(limit - limit % 16, 15, -16) if rows % t == 0), rows)
    c1 = 1.0 - ADAM_B1 ** ADAM_STEP
    c2 = 1.0 - ADAM_B2 ** ADAM_STEP
    parts = g.ndim == 3

    def body(w_ref, g_ref, m_ref, v_ref, go_ref, d_ref, nm_ref, nv_ref):
        if parts:
            gv = g_ref[0].astype(F32)
            for k in range(1, N_DEV):
                gv = gv + g_ref[k].astype(F32)
        else:
            gv = g_ref[...]
        go_ref[...] = gv
        nm = ADAM_B1 * m_ref[...] + (1.0 - ADAM_B1) * gv
        nv = ADAM_B2 * v_ref[...] + (1.0 - ADAM_B2) * (gv * gv)
        nm_ref[...] = nm
        nv_ref[...] = nv
        d_ref[...] = -ADAM_LR * ((nm / c1) / (jnp.sqrt(nv / c2) + ADAM_EPS) + ADAM_WD * w_ref[...])

    spec = pl.BlockSpec((tr, cols), lambda i: (i, 0))
    g_spec = pl.BlockSpec((N_DEV, tr, cols), lambda i: (0, i, 0)) if parts else spec
    return pl.pallas_call(
        body, name=name, grid=(rows // tr,),
        out_shape=[jax.ShapeDtypeStruct((rows, cols), F32)] * 4,
        in_specs=[spec, g_spec, spec, spec], out_specs=[spec] * 4,
        compiler_params=_params("parallel"),
    )(w, g, m, v)


HALF = FF // 2


def _sds(shape, dtype):
    return jax.ShapeDtypeStruct(shape, dtype)


def _row_tile(w):
    return lambda tm: ((tm, w), lambda i, j: (i, 0))


def _one(w):
    return lambda rows: ((rows, w), lambda i, j: (0, 0))


def _gate_up_swiglu(name, h, wgu, carry=None, tm=512):
    s = h.shape[0]
    tm = min(tm, s)

    def epilogue(prod, first, tin, tout):
        ab_ref, s_ref = tout
        ab_ref[...] = prod.astype(BF16)
        a, b = prod[:, :HALF], prod[:, HALF:]
        s_ref[...] = (a * _sigmoid(a) * b).astype(BF16)

    return _mm_lagged(name, h, wgu, "NT", tm, FF, carry=carry, n_outer=True, epilogue=epilogue, tiles_in=[],
               tiles_out=[(_sds((s, 2 * FF), BF16), (tm, FF), lambda i, j: (i, j)),
                          (_sds((s, FF), BF16), (tm, HALF), lambda i, j: (i, j))])


def _d_hidden_swiglu(name, df, wd, ab, tm=512):
    s = df.shape[0]
    tm = min(tm, s)

    def epilogue(prod, first, tin, tout):
        a = tin[0][:, :HALF].astype(F32)
        b = tin[0][:, HALF:].astype(F32)
        sig = _sigmoid(a)
        tout[0][:, :HALF] = (prod * b * (sig * (1.0 + a * (1.0 - sig)))).astype(BF16)
        tout[0][:, HALF:] = (prod * (a * sig)).astype(BF16)

    return _mm_lagged(name, df, wd, "NT", tm, HALF, n_outer=True, epilogue=epilogue,
               tiles_in=[(ab, (tm, FF), lambda i, j: (i, j))],
               tiles_out=[(_sds((s, 2 * FF), BF16), (tm, FF), lambda i, j: (i, j))])[0]


def _out_residual(name, a, w, x, gt, coef, nxt, tm=512, tk=FF):
    s = a.shape[0]
    tm = min(tm, s)

    def epilogue(prod, first, tin, tout):
        x_ref, gt_ref, g_ref, sc_ref, sh_ref = tin
        f_ref, xn_ref, h_ref = tout
        f_ref[...] = prod
        xn = x_ref[...] + (coef * gt_ref[...]) * prod
        xn_ref[...] = xn
        r = lax.rsqrt(jnp.mean(xn * xn, axis=-1, keepdims=True) + EPS)
        h_ref[...] = ((xn * r) * g_ref[...] * (1.0 + sc_ref[...]) + sh_ref[...]).astype(BF16)

    row, vec = _row_tile(D)(tm), _one(D)(1)
    return _mm_lagged(name, a, w, "NN", tm, D, epilogue=epilogue,
               tiles_in=[(x, *row), (gt, *vec)] + [(v, *vec) for v in nxt],
               tiles_out=[(_sds((s, D), F32), *row), (_sds((s, D), F32), *row), (_sds((s, D), BF16), *row)])


def _out_loss(name, a, w, x, gt, coef, target, tm=512):
    s = a.shape[0]
    tm = min(tm, s)

    def epilogue(prod, first, tin, tout):
        x_ref, gt_ref, t_ref = tin
        f_ref, g_ref, df_ref, acc_ref = tout
        f_ref[...] = prod
        cg = coef * gt_ref[...]
        e = x_ref[...] + cg * prod - t_ref[...]
        gv = e * (1.0 / D)
        g_ref[...] = gv
        df_ref[...] = (cg * gv).astype(BF16)

        @pl.when(first)
        def _():
            acc_ref[...] = jnp.zeros_like(acc_ref)

        acc_ref[0:1, :] += coef * jnp.sum(gv * prod, axis=0, keepdims=True)
        acc_ref[1:2, :] += (0.5 / D) * jnp.sum(e * e, axis=0, keepdims=True)

    row, vec = _row_tile(D)(tm), _one(D)(1)
    return _mm_lagged(name, a, w, "NN", tm, D, epilogue=epilogue,
               tiles_in=[(x, *row), (gt, *vec), (target, *row)],
               tiles_out=[(_sds((s, D), F32), *row), (_sds((s, D), F32), *row), (_sds((s, D), BF16), *row),
                          (_sds((8, D), F32), *_one(D)(8))])


def _d_h_norm_bwd(name, da, w, x, gin, g, sc, sh, before=None, carry=None, tm=256):
    s = da.shape[0]
    tm = min(tm, s)
    coef = before[2] if before else None

    def epilogue(prod, first, tin, tout):
        x_ref, gin_ref, g_ref, sc_ref, sh_ref = tin[:5]
        gout_ref, acc_ref = tout[:2]
        xv = x_ref[...]
        r = lax.rsqrt(jnp.mean(xv * xv, axis=-1, keepdims=True) + EPS)
        nv = xv * r
        gv, one_sc = g_ref[...], 1.0 + sc_ref[...]
        dn = prod * gv * one_sc
        gout = gin_ref[...] + r * (dn - nv * jnp.mean(dn * nv, axis=-1, keepdims=True))
        gout_ref[...] = gout

        @pl.when(first)
        def _():
            acc_ref[...] = jnp.zeros_like(acc_ref)

        dhn = prod * nv
        acc_ref[0:1, :] += jnp.sum(prod, axis=0, keepdims=True)
        acc_ref[1:2, :] += jnp.sum(dhn * gv, axis=0, keepdims=True)
        acc_ref[2:3, :] += jnp.sum(dhn * one_sc, axis=0, keepdims=True)
        if before:
            f_ref, gt_ref = tin[5:]
            tout[2][...] = ((coef * gt_ref[...]) * gout).astype(BF16)
            acc_ref[3:4, :] += coef * jnp.sum(gout * f_ref[...], axis=0, keepdims=True)

    row, vec = _row_tile(D)(tm), _one(D)(1)
    tiles_in = [(x, *row), (gin, *row), (g, *vec), (sc, *vec), (sh, *vec)]
    tiles_out = [(_sds((s, D), F32), *row), (_sds((8, D), F32), *_one(D)(8))]
    if before:
        tiles_in += [(before[0], *row), (before[1], *vec)]
        tiles_out.append((_sds((s, D), BF16), *row))
    return _mm_lagged(name, da, w, "NN", tm, D, epilogue=epilogue, carry=carry, keep_b=True,
               tiles_in=tiles_in, tiles_out=tiles_out)


def _gate_tiles(proj, tm):
    return [(proj, (tm, COL), (lambda i, j, blk=blk: (i, blk))) for blk in (GA_BLK, GA_BLK + 1, GC_BLK, GC_BLK + 1)]


def _conv_branch_merge(z, wc, ya, proj, tm=512):
    s = z.shape[0]
    tm = min(tm, s)

    def epilogue(prod, first, tin, tout):
        ya_ref, ga0, ga1, gc0, gc1 = tin
        tout[0][...] = prod.astype(BF16)
        for half, (ga, gc) in enumerate(((ga0, gc0), (ga1, gc1))):
            cols = slice(half * COL, (half + 1) * COL)
            tout[1][:, cols] = (_sigmoid(ga[...].astype(F32)) * ya_ref[:, cols].astype(F32)
                                + _sigmoid(gc[...].astype(F32)) * prod[:, cols]).astype(BF16)

    row = _row_tile(D)(tm)
    return _mm_lagged("mix_conv_branch", z, wc, "NN", tm, D, epilogue=epilogue,
               tiles_in=[(ya, *row)] + _gate_tiles(proj, tm),
               tiles_out=[(_sds((s, D), BF16), *row), (_sds((s, D), BF16), *row)])


def _d_merged_branches(dmix, wo, proj, tm=512):
    s = dmix.shape[0]
    tm = min(tm, s)

    def epilogue(prod, first, tin, tout):
        ga0, ga1, gc0, gc1 = tin
        tout[0][...] = prod
        for half, (ga, gc) in enumerate(((ga0, gc0), (ga1, gc1))):
            cols = slice(half * COL, (half + 1) * COL)
            tout[1][:, cols] = (prod[:, cols] * _sigmoid(ga[...].astype(F32))).astype(BF16)
            tout[2][:, cols] = (prod[:, cols] * _sigmoid(gc[...].astype(F32))).astype(BF16)

    row = _row_tile(D)(tm)
    return _mm_lagged("mix_d_merged", dmix, wo, "NT", tm, D, epilogue=epilogue,
               tiles_in=_gate_tiles(proj, tm),
               tiles_out=[(_sds((s, D), F32), *row), (_sds((s, D), BF16), *row), (_sds((s, D), BF16), *row)])


def _d_o_delta(dya, wa_t, o, tm=1024):
    s = dya.shape[0]
    tm = min(tm, s)

    def epilogue(prod, first, tin, tout):
        tout[0][...] = prod
        tout[1][...] = _heads(prod * tin[0][...].astype(F32), lambda ph, h: jnp.broadcast_to(
            jnp.sum(ph, axis=-1, keepdims=True), ph.shape))

    row = _row_tile(COL)(tm)
    return _mm_lagged("mix_d_o", dya, wa_t, "NN", tm, COL, epilogue=epilogue,
               tiles_in=[(o, *row)], tiles_out=[(_sds((s, COL), F32), *row), (_sds((s, COL), F32), *row)])


def _ffn_bwd(tag, df, x, gin, h, ab, sw, g, sc, sh, wgu, wd, before=None, carry_down=None, carry_gate_up=None,
             tk_dw=2048):
    dab = _d_hidden_swiglu(f"{tag}_d_hidden", df, wd, ab)
    dwd = _mm(f"{tag}_dw_down", sw, df, "TN", BF16, HALF, D, tk_dw)
    carried = []
    if carry_down:
        dwgu, *got = _mm(f"{tag}_dw_gate_up", dab, h, "TN", BF16, HALF, D, tk_dw, carry=carry_down(dwd))
        carried += got
    else:
        dwgu = _mm(f"{tag}_dw_gate_up", dab, h, "TN", BF16, HALF, D, tk_dw)
    res = _d_h_norm_bwd(f"{tag}_d_h", dab, wgu, x, gin, g, sc, sh, before=before,
                        carry=carry_gate_up(dwgu) if carry_gate_up else None)
    n_own = 3 if before else 2
    return res[:n_own], dwgu, dwd, carried + list(res[n_own:])


def kernel(x, c, w_ada, b_ada, norm_ffn1, ffn1_w_gate, ffn1_w_up, ffn1_w_down, norm_mix, w_in, q_norm, k_norm, conv_w, w_attn_branch, w_conv_branch, w_out, norm_ffn2, ffn2_w_gate, ffn2_w_up, ffn2_w_down, loss_target, m_w_ada, m_b_ada, m_norm_ffn1, m_ffn1_w_gate, m_ffn1_w_up, m_ffn1_w_down, m_norm_mix, m_w_in, m_q_norm, m_k_norm, m_conv_w, m_w_attn_branch, m_w_conv_branch, m_w_out, m_norm_ffn2, m_ffn2_w_gate, m_ffn2_w_up, m_ffn2_w_down, v_w_ada, v_b_ada, v_norm_ffn1, v_ffn1_w_gate, v_ffn1_w_up, v_ffn1_w_down, v_norm_mix, v_w_in, v_q_norm, v_k_norm, v_conv_w, v_w_attn_branch, v_w_conv_branch, v_w_out, v_norm_ffn2, v_ffn2_w_gate, v_ffn2_w_up, v_ffn2_w_down):
    me = 4 * lax.axis_index("x") + 2 * lax.axis_index("y") + lax.axis_index("c")
    x0, target = x[0], loss_target[0]
    s = x0.shape[0]
    ada_cols = w_ada.shape[2]
    cw_cols = conv_w.shape[2]

    gathered = _small_allgather(
        "gather_c_conv", jnp.concatenate([c, conv_w[0].reshape(1, 3 * cw_cols)], axis=1))[:, 0]
    c_all = gathered[:, :D]
    cw = gathered[:, D:].reshape(N_DEV, 3, cw_cols).transpose(1, 0, 2).reshape(3, D)
    b_part = lax.dynamic_slice(b_ada, (0, me * ada_cols), (1, ada_cols))
    mod_part = _mod_part(c_all, w_ada[0], b_part)
    mod_all = _small_allgather("gather_mod", mod_part.reshape(1, N_DEV * ada_cols))
    mod = lax.dynamic_slice(mod_all.reshape(N_DEV, N_DEV, ada_cols), (0, me, 0), (N_DEV, 1, ada_cols))
    mod = mod.reshape(N_MOD, 1, D)
    sh1, sc1, gt1, sh2, sc2, gt2, sh3, sc3, gt3 = [mod[i] for i in range(N_MOD)]

    tb = lambda w: w[0].T.astype(BF16)
    nb = lambda w: w[0].astype(BF16)
    ffn1_shards = [tb(ffn1_w_gate), tb(ffn1_w_up), nb(ffn1_w_down)]
    ffn2_shards = [tb(ffn2_w_gate), tb(ffn2_w_up), nb(ffn2_w_down)]
    mix_shards = [tb(w_in), tb(w_attn_branch), nb(w_conv_branch), nb(w_out)]
    ffn_dst, ffn_base, ffn_jump, ffn_shapes = [0, 0, 1], [0, HALF, 0], [HALF, HALF, 0], [(2 * FF, D), (FF, D)]
    mix_dst, mix_base, mix_shapes = [0, 1, 2, 3], [0, 0, 0, 0], [(IN_W, D), (D, COL), (D, D), (D, D)]
    wgu1, wd1 = _run_plan("gather_ffn1_weights",
                          _gather_plan(ffn1_shards, ffn_dst, ffn_base, ffn_shapes, ffn_jump))

    h1 = _normmod("ffn1_normmod", x0, norm_ffn1, sc1, sh1)
    ab1, s1, win_t = _gate_up_swiglu(
        "ffn1_gate_up", h1, wgu1, carry=_gather_plan(mix_shards[:1], mix_dst[:1], mix_base[:1], mix_shapes[:1]))
    f1, x1, h2 = _out_residual("ffn1_down", s1, wd1, x0, gt1, 0.5, (norm_mix, sc2, sh2))
    proj, wgu2, wd2, wa_t, wc, wo = _mm(
        "mix_in_proj", h2, win_t, "NT", BF16, 512, IN_W // 4, D, n_outer=True,
        carry=_gather_plan(ffn2_shards + mix_shards[1:], ffn_dst + [2, 3, 4], ffn_base + [0, 0, 0],
                           ffn_shapes + mix_shapes[1:], ffn_jump + [0, 0, 0]))
    wqk = jnp.concatenate([jnp.tile(q_norm, (1, 12)), jnp.tile(k_norm, (1, 12))], axis=1)
    qkn = _qknorm(proj, wqk)
    group_out = [_attn_fwd(g, qkn, proj) for g in range(3)]
    o, lse = _attn_combine([go[0] for go in group_out], [go[1] for go in group_out])
    ya = _mm("mix_attn_branch", o, wa_t, "NT", BF16, 1024, 1024, COL)
    z = _conv_fwd(proj, cw)
    yc, merged = _conv_branch_merge(z, wc, ya, proj)
    mix, x2, h3 = _out_residual("mix_out_proj", merged, wo, x1, gt2, 1.0, (norm_ffn2, sc3, sh3), tk=D)
    ab3, s3 = _gate_up_swiglu("ffn2_gate_up", h3, wgu2)
    f3, g3, df3, acc_out = _out_loss("ffn2_down", s3, wd2, x2, gt3, 0.5, target)
    loss = lax.psum(jnp.sum(acc_out[1]), ("x", "y", "c"))

    ffn_rows = [sh_.shape[0] for sh_ in ffn1_shards]
    mix_rows = [sh_.shape[0] for sh_ in mix_shards]
    (g2, acc3, dmix), dwgu2, dwd2, _ = _ffn_bwd(
        "ffn2", df3, x2, g3, h3, ab3, s3, norm_ffn2, sc3, sh3, wgu2, wd2, before=(mix, gt2, 1.0))
    dmerged, dya, dyc = _d_merged_branches(dmix, wo, proj)
    dwo = _mm("mix_dw_out", merged, dmix, "TN", BF16, 1024, 1024, 2048)
    dproj = _merge_bwd_gates(dmerged, ya, yc, proj)
    dwc = _mm("mix_dw_conv_branch", z, dyc, "TN", BF16, 1024, 1024, 2048)
    dz = _mm("mix_d_z", dyc, wc, "NT", F32, 1024, 1024, D)
    dproj, cw_acc = _conv_bwd(dz, proj, cw, dproj)
    dwa_t = _mm("mix_dw_attn_branch", dya, o, "TN", BF16, 1024, COL, 2048)
    do, delta = _d_o_delta(dya, wa_t, o)
    dqkn = None
    for g in range(3):
        dqkn = _attn_dq(g, qkn, proj, do, lse, delta, dqkn)
    for g in range(3):
        dqkn, dproj = _attn_dkv(g, qkn, proj, do, lse, delta, dqkn, dproj)
    dproj, wqk_acc = _qknorm_bwd(proj, dqkn, wqk, dproj)
    dwin_t, r_f2g, r_f2u, r_f2d, r_wa, r_wc, r_wo = _mm(
        "mix_dw_in", dproj, h2, "TN", BF16, IN_W // 4, D, 1024,
        carry=_scatter_plan([dwgu2, dwd2, dwa_t, dwc, dwo], [0, 0, 1, 2, 3, 4], [0, HALF, 0, 0, 0, 0],
                            ffn_rows + mix_rows[1:], [D, D, D, COL, D, D], [HALF, HALF, 0, 0, 0, 0]))
    g1, acc2, df1, r_win = _d_h_norm_bwd(
        "mix_d_h", dproj, win_t, x1, g2, norm_mix, sc2, sh2, before=(f1, gt1, 0.5),
        carry=_scatter_plan([dwin_t], [0], [0], mix_rows[:1], [D]))
    (g0, acc1), dwgu1, dwd1, (r_f1d, r_f1g, r_f1u) = _ffn_bwd(
        "ffn1", df1, x0, g1, h1, ab1, s1, norm_ffn1, sc1, sh1, wgu1, wd1,
        carry_down=lambda dwd: _scatter_plan([dwd], [0], [0], ffn_rows[2:], [D]),
        carry_gate_up=lambda dwgu: _scatter_plan([dwgu], [0, 0], [0, HALF], ffn_rows[:2], [D, D], [HALF, HALF]))

    dqw = jnp.sum(wqk_acc[0, :QKW // 2].reshape(12, HD), axis=0)
    dkw = jnp.sum(wqk_acc[0, QKW // 2:].reshape(12, HD), axis=0)
    small = jnp.concatenate([
        acc1[0], acc1[1], acc2[3], acc2[0], acc2[1], acc3[3], acc3[0], acc3[1], acc_out[0],
        acc1[2], acc2[2], acc3[2], dqw, dkw, cw_acc[0:3].reshape(3 * D)]).reshape(1, -1)
    small_all = _small_allgather("gather_small_grads", small)
    small_sum = _sum_rows("sum_small_grads", small_all)[0]
    n_mod = N_MOD * D
    g_b_ada = small_sum[:n_mod].reshape(1, n_mod)
    g_norm1, g_norm2, g_norm3 = [small_sum[n_mod + i * D:n_mod + (i + 1) * D].reshape(1, D) for i in range(3)]
    off = n_mod + 3 * D
    g_qn, g_kn = small_sum[off:off + HD].reshape(1, HD), small_sum[off + HD:off + 2 * HD].reshape(1, HD)
    g_cw_full = small_sum[off + 2 * HD:].reshape(3, D)
    g_cw = lax.dynamic_slice(g_cw_full, (0, me * cw_cols), (3, cw_cols))
    dmod_part = lax.dynamic_slice(small_all[:, 0, :n_mod], (0, me * ada_cols), (N_DEV, ada_cols))
    g_w_ada = _w_ada_grad(c_all.T, dmod_part)

    as_rows = {"ffn1_w_gate", "ffn1_w_up", "w_in", "w_attn_branch", "ffn2_w_gate", "ffn2_w_up"}
    grad_list = [g_w_ada, g_b_ada, g_norm1, r_f1g, r_f1u, r_f1d, g_norm2, r_win,
                 g_qn, g_kn, g_cw, r_wa, r_wc, r_wo, g_norm3, r_f2g, r_f2u, r_f2d]
    weights = [w_ada, b_ada, norm_ffn1, ffn1_w_gate, ffn1_w_up, ffn1_w_down, norm_mix, w_in, q_norm, k_norm,
               conv_w, w_attn_branch, w_conv_branch, w_out, norm_ffn2, ffn2_w_gate, ffn2_w_up, ffn2_w_down]
    ms = [m_w_ada, m_b_ada, m_norm_ffn1, m_ffn1_w_gate, m_ffn1_w_up, m_ffn1_w_down, m_norm_mix, m_w_in, m_q_norm,
          m_k_norm, m_conv_w, m_w_attn_branch, m_w_conv_branch, m_w_out, m_norm_ffn2, m_ffn2_w_gate,
          m_ffn2_w_up, m_ffn2_w_down]
    vs = [v_w_ada, v_b_ada, v_norm_ffn1, v_ffn1_w_gate, v_ffn1_w_up, v_ffn1_w_down, v_norm_mix, v_w_in, v_q_norm,
          v_k_norm, v_conv_w, v_w_attn_branch, v_w_conv_branch, v_w_out, v_norm_ffn2, v_ffn2_w_gate,
          v_ffn2_w_up, v_ffn2_w_down]
    wnames = ["w_ada", "b_ada", "norm_ffn1", "ffn1_w_gate", "ffn1_w_up", "ffn1_w_down", "norm_mix", "w_in",
              "q_norm", "k_norm", "conv_w", "w_attn_branch", "w_conv_branch", "w_out", "norm_ffn2",
              "ffn2_w_gate", "ffn2_w_up", "ffn2_w_down"]
    grad_out, deltas, new_ms, new_vs = [], [], [], []
    for nm, w, gr, m_, v_ in zip(wnames, weights, grad_list, ms, vs):
        if nm in as_rows:
            res = _adamw(f"adamw_{nm}", w[0].T, gr, m_[0].T, v_[0].T)
            gr, dl, nm_, nv_ = [r.T[None] for r in res]
        else:
            two_d = (-1, w.shape[-1])
            res = _adamw(f"adamw_{nm}", w.reshape(two_d), gr if gr.ndim == 3 else gr.reshape(two_d),
                         m_.reshape(two_d), v_.reshape(two_d))
            gr, dl, nm_, nv_ = [r.reshape(w.shape) for r in res]
        grad_out.append(gr)
        deltas.append(dl)
        new_ms.append(nm_)
        new_vs.append(nv_)
    return (loss, g0[None], *grad_out, *deltas, *new_ms, *new_vs)
```

```python
import functools

import jax
import jax.numpy as jnp
from jax import lax
from jax.experimental import pallas as pl
from jax.experimental.pallas import tpu as pltpu

F32 = jnp.float32
BF16 = jnp.bfloat16
MESH = pl.DeviceIdType.MESH

N_DEV = 8
D = 1024
FF = 2816
HD = 128
N_HEADS = 4
DILATIONS = (1, 4, 16)
BAND = 128
QKW = 2 * 3 * N_HEADS * HD
IN_W = 9728
COL = 512
V_BLK, U_BLK, B_BLK, C_BLK, GA_BLK, GC_BLK = 6, 9, 11, 13, 15, 17
EPS = 1e-6
N_MOD = 9
ADAM_LR, ADAM_B1, ADAM_B2, ADAM_EPS, ADAM_WD, ADAM_STEP = 0.001, 0.9, 0.999, 1e-08, 0.01, 10

NT_DIMS = (((1,), (1,)), ((), ()))
TN_DIMS = (((0,), (0,)), ((), ()))
NN_DIMS = (((1,), (0,)), ((), ()))


def _place():
    return lax.axis_index("x"), lax.axis_index("y"), lax.axis_index("c")


def _flip(coord, bit):
    return 1 - coord if bit else coord


def _params(*sem):
    return pltpu.CompilerParams(dimension_semantics=sem)


def _small_allgather(name, v):
    n = v.shape[-1]

    def body(v_ref, out_ref, send_sems, recv_sems):
        x, y, c = _place()
        me = 4 * x + 2 * y + c
        out_ref[me] = v_ref[...]
        copies = []
        for k in range(1, N_DEV):
            peer = (_flip(x, (k >> 2) & 1), _flip(y, (k >> 1) & 1), _flip(c, k & 1))
            cp = pltpu.make_async_remote_copy(
                src_ref=v_ref, dst_ref=out_ref.at[me], send_sem=send_sems.at[k - 1],
                recv_sem=recv_sems.at[k - 1], device_id=peer, device_id_type=MESH)
            cp.start()
            copies.append(cp)
        for cp in copies:
            cp.wait()

    return pl.pallas_call(
        body, name=name,
        out_shape=jax.ShapeDtypeStruct((N_DEV, 1, n), F32),
        in_specs=[pl.BlockSpec(memory_space=pltpu.VMEM)],
        out_specs=pl.BlockSpec(memory_space=pltpu.VMEM),
        scratch_shapes=[pltpu.SemaphoreType.DMA((N_DEV - 1,)), pltpu.SemaphoreType.DMA((N_DEV - 1,))],
    )(v)


class _Plan:
    def __init__(self, operands, out_shapes, sems, phases):
        self.operands, self.out_shapes, self.sems, self.phases = operands, out_shapes, sems, phases


def _slab_start(base, rows, jump, idx):
    return pl.multiple_of(base + idx * rows + (idx // 4) * jump, 16)


def _gather_plan(shards, dst_of, base_of, dst_shapes, jump_of=None):
    n = len(shards)
    rows = [s.shape[0] for s in shards]
    jump_of = jump_of or [0] * n

    def phases(srcs, dsts, sems):
        send_sems, recv_sems, local_sems = sems
        x, y, c = _place()
        me, sibling = (x, y, c), (x, y, 1 - c)
        chips = [(1 - x, y), (x, 1 - y), (1 - x, 1 - y)]

        def slab(i, px, py, pc):
            start = _slab_start(base_of[i], rows[i], jump_of[i], 4 * px + 2 * py + pc)
            return dsts[dst_of[i]].at[pl.ds(start, rows[i])]

        def copy(i, k, block, to, src=None):
            return pltpu.make_async_remote_copy(
                src_ref=slab(i, *block) if src is None else src, dst_ref=slab(i, *block),
                send_sem=send_sems.at[i, k], recv_sem=recv_sems.at[i, k],
                device_id=to, device_id_type=MESH)

        def mine():
            return [pltpu.make_async_copy(srcs[i], slab(i, *me), local_sems.at[i]) for i in range(n)]

        def first():
            out = []
            for i in range(n):
                out.append(copy(i, 0, me, sibling, src=srcs[i]))
                out += [copy(i, 1 + j, me, (*chip, c), src=srcs[i]) for j, chip in enumerate(chips)]
            return out

        def passed():
            return [(copy(i, 1 + j, (*chip, c), me), copy(i, 4 + j, (*chip, c), sibling))
                    for j, chip in enumerate(chips) for i in range(n)]

        def start():
            for cp in mine() + first():
                cp.start()

        def middle():
            for landed, onward in passed():
                landed.wait_recv()
                onward.start()

        def finish():
            for i in range(n):
                copy(i, 0, sibling, me).wait_recv()
                for j, chip in enumerate(chips):
                    copy(i, 4 + j, (*chip, 1 - c), me).wait_recv()
            for cp in first() + [onward for _, onward in passed()]:
                cp.wait_send()
            for cp in mine():
                cp.wait()

        return start, middle, finish

    sems = [pltpu.SemaphoreType.DMA((n, 7)), pltpu.SemaphoreType.DMA((n, 7)), pltpu.SemaphoreType.DMA((n,))]
    return _Plan(list(shards), [jax.ShapeDtypeStruct(s, BF16) for s in dst_shapes], sems, phases)


def _scatter_plan(grads, src_of, base_of, rows, cols, jump_of=None):
    n = len(rows)
    jump_of = jump_of or [0] * n

    def phases(srcs, recvs, sems):
        send_sems, recv_sems, local_sems = sems
        x, y, c = _place()
        me = 4 * x + 2 * y + c

        def slab(i, idx):
            start = _slab_start(base_of[i], rows[i], jump_of[i], idx)
            return srcs[src_of[i]].at[pl.ds(start, rows[i])]

        def copies():
            out = [pltpu.make_async_copy(slab(i, me), recvs[i].at[me], local_sems.at[i]) for i in range(n)]
            for k in range(1, N_DEV):
                px, py, pc = _flip(x, (k >> 2) & 1), _flip(y, (k >> 1) & 1), _flip(c, k & 1)
                out += [pltpu.make_async_remote_copy(
                    src_ref=slab(i, 4 * px + 2 * py + pc), dst_ref=recvs[i].at[me],
                    send_sem=send_sems.at[i, k - 1], recv_sem=recv_sems.at[i, k - 1],
                    device_id=(px, py, pc), device_id_type=MESH) for i in range(n)]
            return out

        def start():
            for cp in copies():
                cp.start()

        def finish():
            for cp in copies():
                cp.wait()

        return start, None, finish

    sems = [pltpu.SemaphoreType.DMA((n, 7)), pltpu.SemaphoreType.DMA((n, 7)), pltpu.SemaphoreType.DMA((n,))]
    out_shapes = [jax.ShapeDtypeStruct((N_DEV, rows[i], cols[i]), BF16) for i in range(n)]
    return _Plan(list(grads), out_shapes, sems, phases)


def _run_plan(name, plan):
    n_in, n_out = len(plan.operands), len(plan.out_shapes)

    def body(*refs):
        for phase in plan.phases(refs[:n_in], refs[n_in:n_in + n_out], refs[n_in + n_out:]):
            if phase is not None:
                phase()

    hbm = pl.BlockSpec(memory_space=pltpu.HBM)
    return pl.pallas_call(
        body, name=name, out_shape=plan.out_shapes,
        in_specs=[hbm] * n_in, out_specs=[hbm] * n_out, scratch_shapes=plan.sems,
    )(*plan.operands)


def _sum_contributions(name, recv):
    _, rows, cols = recv.shape
    tr = rows if rows <= 512 else 304 if rows % 304 == 0 else 256

    def body(r_ref, o_ref):
        acc = r_ref[0].astype(F32)
        for k in range(1, N_DEV):
            acc = acc + r_ref[k].astype(F32)
        o_ref[...] = acc

    return pl.pallas_call(
        body, name=name, grid=(rows // tr,),
        out_shape=jax.ShapeDtypeStruct((rows, cols), F32),
        in_specs=[pl.BlockSpec((N_DEV, tr, cols), lambda i: (0, i, 0))],
        out_specs=pl.BlockSpec((tr, cols), lambda i: (i, 0)),
        compiler_params=_params("parallel"),
    )(recv)


def _mm(name, a, b, mode, out_dtype, tm, tn, tk, *, carry=None, tiles_in=(), tiles_out=(), epilogue=None,
        n_outer=False, keep_b=False, col_chunks=None):
    if mode == "TN":
        kk, m = a.shape
    else:
        m, kk = a.shape
    n = b.shape[0] if mode == "NT" else b.shape[1]
    tm, tn, tk = min(tm, m), min(tn, n), min(tk, kk)
    assert m % tm == 0 and n % tn == 0 and kk % tk == 0, (name, m, n, kk, tm, tn, tk)
    ni, nj, nk = m // tm, n // tn, kk // tk
    steps = ni * nj * nk
    dims = {"NN": NN_DIMS, "NT": NT_DIMS, "TN": TN_DIMS}[mode]
    if epilogue is None:
        tiles_out = [(jax.ShapeDtypeStruct((m, n), out_dtype), (tm, tn), lambda i, j: (i, j))]
    n_tin, n_tout = len(tiles_in), len(tiles_out)
    n_in = len(carry.operands) if carry else 0
    n_out = len(carry.out_shapes) if carry else 0
    n_acc = 1 if nk > 1 else 0
    n_keep = 2 if keep_b else 0
    assert not carry or steps >= 3
    assert not keep_b or (nk == 1 and nj == 1)
    assert not col_chunks or (epilogue is not None and nk == 1 and mode != "TN")
    ij = (lambda p, q: (q, p)) if n_outer else (lambda p, q: (p, q))
    inner = ni if n_outer else nj

    def body(a_ref, b_ref, *rest):
        tin = rest[:n_tin]
        cin = rest[n_tin:n_tin + n_in]
        tout = rest[n_tin + n_in:n_tin + n_in + n_tout]
        cout = rest[n_tin + n_in + n_tout:n_tin + n_in + n_tout + n_out]
        scratch = rest[n_tin + n_in + n_tout + n_out:]
        k = pl.program_id(2)
        visit = pl.program_id(0) * inner + pl.program_id(1)
        step = visit * nk + k
        if keep_b:
            b_kept, b_sem = scratch[n_acc:n_acc + 2]

            @pl.when(step == 0)
            def _():
                cp = pltpu.make_async_copy(b_ref, b_kept, b_sem)
                cp.start()
                cp.wait()

            b_ref = b_kept
        if carry:
            start, middle, finish = carry.phases(cin, cout, scratch[n_acc + n_keep:])
            pl.when(step == 0)(start)

        def store(prod, c=0, cols=()):
            if epilogue is None:
                tout[0][...] = prod.astype(out_dtype)
            else:
                epilogue(prod, jnp.logical_and(visit == 0, c == 0), tin, tout, *cols)

        if col_chunks:
            for c, (c0, cw) in enumerate(col_chunks):
                b_part = b_ref[pl.ds(c0, cw), :] if mode == "NT" else b_ref[:, pl.ds(c0, cw)]
                store(lax.dot_general(a_ref[...], b_part, dims, preferred_element_type=F32), c, ((c0, cw),))
        else:
            part = lax.dot_general(a_ref[...], b_ref[...], dims, preferred_element_type=F32)
            if nk == 1:
                store(part)
            else:
                acc_ref = scratch[0]

                @pl.when(k == 0)
                def _():
                    acc_ref[...] = part

                @pl.when((k > 0) & (k < nk - 1))
                def _():
                    acc_ref[...] += part

                @pl.when(k == nk - 1)
                def _():
                    store(acc_ref[...] + part)

        if carry:
            if middle is not None:
                pl.when(step == (steps * 3) // 5)(middle)
            pl.when(step == steps - 1)(finish)

    def spec(shape, fn):
        return pl.BlockSpec(shape, lambda p, q, k: fn(*ij(p, q)))

    a_spec = (pl.BlockSpec((tk, tm), lambda p, q, k: (k, ij(p, q)[0])) if mode == "TN"
              else pl.BlockSpec((tm, tk), lambda p, q, k: (ij(p, q)[0], k)))
    if keep_b:
        b_spec = pl.BlockSpec(memory_space=pl.ANY)
    elif mode == "NT":
        b_spec = pl.BlockSpec((tn, tk), lambda p, q, k: (ij(p, q)[1], k))
    else:
        b_spec = pl.BlockSpec((tk, tn), lambda p, q, k: (k, ij(p, q)[1]))
    hbm = pl.BlockSpec(memory_space=pltpu.HBM)
    sequential = carry or epilogue or keep_b
    out = pl.pallas_call(
        body, name=name, grid=(nj, ni, nk) if n_outer else (ni, nj, nk),
        out_shape=[t[0] for t in tiles_out] + (carry.out_shapes if carry else []),
        in_specs=[a_spec, b_spec] + [spec(t[1], t[2]) for t in tiles_in] + [hbm] * n_in,
        out_specs=[spec(t[1], t[2]) for t in tiles_out] + [hbm] * n_out,
        scratch_shapes=([pltpu.VMEM((tm, tn), F32)] * n_acc
                        + ([pltpu.VMEM(b.shape, b.dtype), pltpu.SemaphoreType.DMA] if keep_b else [])
                        + (carry.sems if carry else [])),
        compiler_params=(_params("arbitrary", "arbitrary", "arbitrary") if sequential
                         else _params("parallel", "parallel", "arbitrary")),
    )(a, b, *[t[0] for t in tiles_in], *(carry.operands if carry else []))
    return out if (carry or epilogue) else out[0]


def _row(tm, w, off=0):
    return pl.BlockSpec((tm, w), lambda i: (i, off))


def _vec(w):
    return pl.BlockSpec((1, w), lambda i: (0, 0))


def _sigmoid(x):
    return 0.5 * jnp.tanh(0.5 * x) + 0.5


def _normmod(name, x, g, sc, sh, tm=512):
    s = x.shape[0]

    def body(x_ref, g_ref, sc_ref, sh_ref, h_ref):
        xv = x_ref[...]
        r = lax.rsqrt(jnp.mean(xv * xv, axis=-1, keepdims=True) + EPS)
        h_ref[...] = ((xv * r) * g_ref[...] * (1.0 + sc_ref[...]) + sh_ref[...]).astype(BF16)

    return pl.pallas_call(
        body, name=name, grid=(s // tm,),
        out_shape=jax.ShapeDtypeStruct((s, D), BF16),
        in_specs=[_row(tm, D), _vec(D), _vec(D), _vec(D)], out_specs=_row(tm, D),
        compiler_params=_params("parallel"),
    )(x, g, sc, sh)


def _normmod_bwd(name, dh, x, gin, g, sc, sh, tm=512):
    s = x.shape[0]

    def body(dh_ref, x_ref, gin_ref, g_ref, sc_ref, sh_ref, gout_ref, acc_ref):
        xv, dhv = x_ref[...], dh_ref[...]
        r = lax.rsqrt(jnp.mean(xv * xv, axis=-1, keepdims=True) + EPS)
        nv = xv * r
        gv, one_sc = g_ref[...], 1.0 + sc_ref[...]
        dn = dhv * gv * one_sc
        dx = r * (dn - nv * jnp.mean(dn * nv, axis=-1, keepdims=True))
        gout_ref[...] = gin_ref[...] + dx

        @pl.when(pl.program_id(0) == 0)
        def _():
            acc_ref[...] = jnp.zeros_like(acc_ref)

        dhn = dhv * nv
        acc_ref[0:1, :] += jnp.sum(dhv, axis=0, keepdims=True)
        acc_ref[1:2, :] += jnp.sum(dhn * gv, axis=0, keepdims=True)
        acc_ref[2:3, :] += jnp.sum(dhn * one_sc, axis=0, keepdims=True)

    return pl.pallas_call(
        body, name=name, grid=(s // tm,),
        out_shape=[jax.ShapeDtypeStruct((s, D), F32), jax.ShapeDtypeStruct((8, D), F32)],
        in_specs=[_row(tm, D), _row(tm, D), _row(tm, D), _vec(D), _vec(D), _vec(D)],
        out_specs=[_row(tm, D), pl.BlockSpec((8, D), lambda i: (0, 0))],
        compiler_params=_params("arbitrary"),
    )(dh, x, gin, g, sc, sh)


def _swiglu(name, ab, tm=512):
    s = ab.shape[0]

    def body(ab_ref, s_ref):
        a = ab_ref[:, :FF].astype(F32)
        b = ab_ref[:, FF:].astype(F32)
        s_ref[...] = (a * _sigmoid(a) * b).astype(BF16)

    return pl.pallas_call(
        body, name=name, grid=(s // tm,),
        out_shape=jax.ShapeDtypeStruct((s, FF), BF16),
        in_specs=[_row(tm, 2 * FF)], out_specs=_row(tm, FF),
        compiler_params=_params("parallel"),
    )(ab)


def _swiglu_bwd(name, ds, ab, tm=256):
    s = ab.shape[0]

    def body(ds_ref, ab_ref, dab_ref):
        a = ab_ref[:, :FF].astype(F32)
        b = ab_ref[:, FF:].astype(F32)
        dsv = ds_ref[...].astype(F32)
        sig = _sigmoid(a)
        dab_ref[:, :FF] = (dsv * b * (sig * (1.0 + a * (1.0 - sig)))).astype(BF16)
        dab_ref[:, FF:] = (dsv * (a * sig)).astype(BF16)

    return pl.pallas_call(
        body, name=name, grid=(s // tm,),
        out_shape=jax.ShapeDtypeStruct((s, 2 * FF), BF16),
        in_specs=[_row(tm, FF), _row(tm, 2 * FF)], out_specs=_row(tm, 2 * FF),
        compiler_params=_params("parallel"),
    )(ds, ab)


def _residual(name, x, f, gt, coef, tm=512):
    s = x.shape[0]

    def body(x_ref, f_ref, gt_ref, o_ref):
        o_ref[...] = x_ref[...] + (coef * gt_ref[...]) * f_ref[...]

    return pl.pallas_call(
        body, name=name, grid=(s // tm,),
        out_shape=jax.ShapeDtypeStruct((s, D), F32),
        in_specs=[_row(tm, D), _row(tm, D), _vec(D)], out_specs=_row(tm, D),
        compiler_params=_params("parallel"),
    )(x, f, gt)


def _gate_bwd(name, gin, f, gt, coef, tm=512):
    s = gin.shape[0]

    def body(g_ref, f_ref, gt_ref, df_ref, acc_ref):
        gv = g_ref[...]
        df_ref[...] = ((coef * gt_ref[...]) * gv).astype(BF16)

        @pl.when(pl.program_id(0) == 0)
        def _():
            acc_ref[...] = jnp.zeros_like(acc_ref)

        acc_ref[0:1, :] += coef * jnp.sum(gv * f_ref[...], axis=0, keepdims=True)

    return pl.pallas_call(
        body, name=name, grid=(s // tm,),
        out_shape=[jax.ShapeDtypeStruct((s, D), BF16), jax.ShapeDtypeStruct((8, D), F32)],
        in_specs=[_row(tm, D), _row(tm, D), _vec(D)],
        out_specs=[_row(tm, D), pl.BlockSpec((8, D), lambda i: (0, 0))],
        compiler_params=_params("arbitrary"),
    )(gin, f, gt)


def _loss_grad(x3, target, tm=512):
    s = x3.shape[0]

    def body(y_ref, t_ref, g_ref, l_ref):
        e = y_ref[...] - t_ref[...]
        g_ref[...] = e * (1.0 / D)

        @pl.when(pl.program_id(0) == 0)
        def _():
            l_ref[...] = jnp.zeros_like(l_ref)

        l_ref[...] += jnp.sum(jnp.mean(e * e, axis=-1, keepdims=True), axis=0, keepdims=True) * 0.5

    return pl.pallas_call(
        body, name="loss_grad", grid=(s // tm,),
        out_shape=[jax.ShapeDtypeStruct((s, D), F32), jax.ShapeDtypeStruct((8, 128), F32)],
        in_specs=[_row(tm, D), _row(tm, D)],
        out_specs=[_row(tm, D), pl.BlockSpec((8, 128), lambda i: (0, 0))],
        compiler_params=_params("arbitrary"),
    )(x3, target)


def _heads(x, fn):
    return jnp.concatenate([fn(x[:, h * HD:(h + 1) * HD], h) for h in range(COL // HD)], axis=1)


def _qknorm(proj, wqk, tm=1024):
    s = proj.shape[0]

    def body(p_ref, w_ref, o_ref):
        pv = p_ref[...].astype(F32)
        wv = w_ref[...]

        def one(qh, h):
            r = lax.rsqrt(jnp.mean(qh * qh, axis=-1, keepdims=True) + EPS)
            return (qh * r) * wv[:, h * HD:(h + 1) * HD]

        o_ref[...] = _heads(pv, one).astype(BF16)

    return pl.pallas_call(
        body, name="qknorm", grid=(s // tm, QKW // COL),
        out_shape=jax.ShapeDtypeStruct((s, QKW), BF16),
        in_specs=[pl.BlockSpec((tm, COL), lambda i, j: (i, j)), pl.BlockSpec((1, COL), lambda i, j: (0, j))],
        out_specs=pl.BlockSpec((tm, COL), lambda i, j: (i, j)),
        compiler_params=_params("parallel", "parallel"),
    )(proj, wqk)


def _qknorm_bwd(proj, dqkn, wqk, dproj, tm=1024):
    s = proj.shape[0]

    def body(p_ref, d_ref, w_ref, _, o_ref, acc_ref):
        pv = p_ref[...].astype(F32)
        dv = d_ref[...]
        wv = w_ref[...]
        sums = []

        def one(qh, h):
            dn = dv[:, h * HD:(h + 1) * HD]
            r = lax.rsqrt(jnp.mean(qh * qh, axis=-1, keepdims=True) + EPS)
            nh = qh * r
            sums.append(jnp.sum(dn * nh, axis=0, keepdims=True))
            dnw = dn * wv[:, h * HD:(h + 1) * HD]
            return r * (dnw - nh * jnp.mean(dnw * nh, axis=-1, keepdims=True))

        o_ref[...] = _heads(pv, one).astype(BF16)

        @pl.when(pl.program_id(1) == 0)
        def _():
            acc_ref[...] = jnp.zeros_like(acc_ref)

        acc_ref[0:1, :] += jnp.concatenate(sums, axis=1)

    return pl.pallas_call(
        body, name="qknorm_bwd", grid=(QKW // COL, s // tm),
        out_shape=[jax.ShapeDtypeStruct((s, IN_W), BF16), jax.ShapeDtypeStruct((8, QKW), F32)],
        in_specs=[pl.BlockSpec((tm, COL), lambda j, i: (i, j)), pl.BlockSpec((tm, COL), lambda j, i: (i, j)),
                  pl.BlockSpec((1, COL), lambda j, i: (0, j)), pl.BlockSpec(memory_space=pl.ANY)],
        out_specs=[pl.BlockSpec((tm, COL), lambda j, i: (i, j)), pl.BlockSpec((8, COL), lambda j, i: (0, j))],
        input_output_aliases={3: 0},
        compiler_params=_params("arbitrary", "arbitrary"),
    )(proj, dqkn, wqk, dproj)


def _attn_shapes(s, g):
    d = DILATIONS[g]
    tb = min(s, max(2048, 256 * d))
    sb = min(256, tb // d)
    pb = BAND * d
    assert s % tb == 0 and tb % pb == 0 and (tb // d) % sb == 0 and sb % BAND == 0
    return d, tb, sb, pb


def _lanes(x, width):
    return jnp.concatenate([x] * (width // HD), axis=1)


def _every(start, size, d):
    return pl.ds(start, size, stride=d) if d > 1 else pl.ds(start, size)


def _attn_specs(g, tb, pb, s, ahead):
    ratio = tb // pb
    if ahead:
        nbr = lambda n: jnp.minimum((n + 1) * ratio, s // pb - 1)
    else:
        nbr = lambda n: jnp.maximum(n * ratio - 1, 0)
    cur = lambda base: pl.BlockSpec((tb, HD), lambda h, n: (n, base + g * N_HEADS + h))
    side = lambda base: pl.BlockSpec((pb, HD), lambda h, n: (nbr(n), base + g * N_HEADS + h))
    tok = pl.BlockSpec((tb, HD), lambda h, n: (n, h))
    tok_side = pl.BlockSpec((pb, HD), lambda h, n: (nbr(n), h))
    return cur, side, tok, tok_side


Q_COL, K_COL, V_COL = 0, 12, 24


def _attn_fwd(g, qkn, proj):
    s = qkn.shape[0]
    d, tb, sb, pb = _attn_shapes(s, g)
    ft = F32 if d > 1 else BF16
    nj = tb // d // sb
    scale = HD ** -0.5

    def body(q_ref, kc_ref, kp_ref, vc_ref, vp_ref, o_ref, lse_ref, qf, kf, vf):
        n = pl.program_id(1)
        qf[...] = q_ref[...].astype(ft)
        kf[0:pb] = kp_ref[...].astype(ft)
        kf[pb:] = kc_ref[...].astype(ft)
        vf[0:pb] = vp_ref[...].astype(ft)
        vf[pb:] = vc_ref[...].astype(ft)
        for r in range(d):
            for j in range(nj):
                at = j * sb * d + r
                q = qf[_every(at, sb, d), :].astype(BF16)
                k = kf[_every(at, sb + BAND, d), :].astype(BF16)
                v = vf[_every(at, sb + BAND, d), :].astype(BF16)
                sc = lax.dot_general(q, k, NT_DIMS, preferred_element_type=F32) * scale
                qi = lax.broadcasted_iota(jnp.int32, sc.shape, 0)
                kj = lax.broadcasted_iota(jnp.int32, sc.shape, 1)
                valid = (kj >= qi) & (kj <= qi + BAND)
                if j == 0:
                    valid = valid & ((kj >= BAND) | (n > 0))
                sc = jnp.where(valid, sc, -1e30)
                m = jnp.max(sc, axis=-1, keepdims=True)
                p = jnp.exp(sc - m)
                l = jnp.sum(p, axis=-1, keepdims=True)
                o = lax.dot_general(p.astype(BF16), v, NN_DIMS, preferred_element_type=F32)
                o_ref[_every(at, sb, d), :] = o / l
                lse_ref[_every(at, sb, d), :] = jnp.broadcast_to(m + jnp.log(l), (sb, HD))

    cur, side, tok, _ = _attn_specs(g, tb, pb, s, ahead=False)
    return pl.pallas_call(
        body, name=f"attn_fwd_g{g}", grid=(N_HEADS, s // tb),
        out_shape=[jax.ShapeDtypeStruct((s, COL), F32)] * 2,
        in_specs=[cur(Q_COL), cur(K_COL), side(K_COL), cur(V_COL), side(V_COL)],
        out_specs=[tok, tok],
        scratch_shapes=[pltpu.VMEM((tb, HD), ft), pltpu.VMEM((tb + pb, HD), ft),
                        pltpu.VMEM((tb + pb, HD), ft)],
        compiler_params=_params("parallel", "arbitrary"),
    )(qkn, qkn, qkn, proj, proj)


def _attn_combine(os_, lses, tm=512):
    s = os_[0].shape[0]

    def body(o0, o1, o2, l0, l1, l2, o_ref, lse_ref):
        a, b, c = l0[...], l1[...], l2[...]
        m = jnp.maximum(jnp.maximum(a, b), c)
        ea, eb, ec = jnp.exp(a - m), jnp.exp(b - m), jnp.exp(c - m)
        tot = ea + eb + ec
        o_ref[...] = ((ea * o0[...] + eb * o1[...] + ec * o2[...]) / tot).astype(BF16)
        lse_ref[...] = m + jnp.log(tot)

    return pl.pallas_call(
        body, name="attn_combine", grid=(s // tm,),
        out_shape=[jax.ShapeDtypeStruct((s, COL), BF16), jax.ShapeDtypeStruct((s, COL), F32)],
        in_specs=[_row(tm, COL)] * 6, out_specs=[_row(tm, COL)] * 2,
        compiler_params=_params("parallel"),
    )(*os_, *lses)


def _attn_delta(do, o, tm=512):
    s = do.shape[0]

    def body(do_ref, o_ref, del_ref):
        prod = do_ref[...] * o_ref[...].astype(F32)
        del_ref[...] = _heads(prod, lambda ph, h: jnp.broadcast_to(
            jnp.sum(ph, axis=-1, keepdims=True), ph.shape))

    return pl.pallas_call(
        body, name="attn_delta", grid=(s // tm,),
        out_shape=jax.ShapeDtypeStruct((s, COL), F32),
        in_specs=[_row(tm, COL)] * 2, out_specs=_row(tm, COL),
        compiler_params=_params("parallel"),
    )(do, o)


def _attn_dq(g, qkn, proj, do, lse, delta, dqkn):
    s = qkn.shape[0]
    d, tb, sb, pb = _attn_shapes(s, g)
    ft = F32 if d > 1 else BF16
    nj = tb // d // sb
    scale = HD ** -0.5
    chained = dqkn is not None

    def body(q_ref, kc_ref, kp_ref, vc_ref, vp_ref, do_ref, lse_ref, del_ref, *rest):
        dq_ref, qf, kf, vf = rest[-4:]
        n = pl.program_id(1)
        qf[...] = q_ref[...].astype(ft)
        kf[0:pb] = kp_ref[...].astype(ft)
        kf[pb:] = kc_ref[...].astype(ft)
        vf[0:pb] = vp_ref[...].astype(ft)
        vf[pb:] = vc_ref[...].astype(ft)
        for r in range(d):
            for j in range(nj):
                at = j * sb * d + r
                rows = _every(at, sb, d)
                q = qf[rows, :].astype(BF16)
                k = kf[_every(at, sb + BAND, d), :].astype(BF16)
                v = vf[_every(at, sb + BAND, d), :].astype(BF16)
                sc = lax.dot_general(q, k, NT_DIMS, preferred_element_type=F32) * scale
                qi = lax.broadcasted_iota(jnp.int32, sc.shape, 0)
                kj = lax.broadcasted_iota(jnp.int32, sc.shape, 1)
                valid = (kj >= qi) & (kj <= qi + BAND)
                if j == 0:
                    valid = valid & ((kj >= BAND) | (n > 0))
                p = jnp.exp(jnp.where(valid, sc - _lanes(lse_ref[rows, :], sb + BAND), -1e30))
                dp = lax.dot_general(do_ref[rows, :].astype(BF16), v, NT_DIMS, preferred_element_type=F32)
                ds = p * (dp - _lanes(del_ref[rows, :], sb + BAND)) * scale
                dq_ref[rows, :] = lax.dot_general(ds.astype(BF16), k, NN_DIMS, preferred_element_type=F32)

    cur, side, tok, _ = _attn_specs(g, tb, pb, s, ahead=False)
    args = [qkn, qkn, qkn, proj, proj, do, lse, delta]
    specs = [cur(Q_COL), cur(K_COL), side(K_COL), cur(V_COL), side(V_COL), tok, tok, tok]
    if chained:
        args.append(dqkn)
        specs.append(pl.BlockSpec(memory_space=pl.ANY))
    return pl.pallas_call(
        body, name=f"attn_dq_g{g}", grid=(N_HEADS, s // tb),
        out_shape=jax.ShapeDtypeStruct((s, QKW), F32),
        in_specs=specs, out_specs=cur(Q_COL),
        input_output_aliases={8: 0} if chained else {},
        scratch_shapes=[pltpu.VMEM((tb, HD), ft), pltpu.VMEM((tb + pb, HD), ft),
                        pltpu.VMEM((tb + pb, HD), ft)],
        compiler_params=_params("arbitrary", "arbitrary"),
    )(*args)


def _attn_dkv(g, qkn, proj, do, lse, delta, dqkn, dproj):
    s = qkn.shape[0]
    d, tb, sb, pb = _attn_shapes(s, g)
    ft = F32 if d > 1 else BF16
    nj = tb // d // sb
    nt = s // tb
    scale = HD ** -0.5

    def body(k_ref, v_ref, qc_ref, qn_ref, doc_ref, don_ref, lc_ref, ln_ref, dc_ref, dn_ref, _a, _b,
             dk_ref, dv_ref, kf, vf, qf, dvf):
        n = pl.program_id(1)
        kf[...] = k_ref[...].astype(ft)
        vf[...] = v_ref[...].astype(ft)
        qf[0:tb] = qc_ref[...].astype(ft)
        qf[tb:] = qn_ref[...].astype(ft)

        def window(c_ref, n_ref, r, j):
            at = j * sb * d + r
            if j < nj - 1:
                return c_ref[_every(at, sb + BAND, d), :]
            return jnp.concatenate([c_ref[_every(at, sb, d), :], n_ref[_every(r, BAND, d), :]], axis=0)

        for r in range(d):
            for j in range(nj):
                at = j * sb * d + r
                rows = _every(at, sb, d)
                k = kf[rows, :].astype(BF16)
                v = vf[rows, :].astype(BF16)
                q = qf[_every(at, sb + BAND, d), :].astype(BF16)
                dov = window(doc_ref, don_ref, r, j).astype(BF16)
                sc = lax.dot_general(q, k, NT_DIMS, preferred_element_type=F32) * scale
                qi = lax.broadcasted_iota(jnp.int32, sc.shape, 0)
                kj = lax.broadcasted_iota(jnp.int32, sc.shape, 1)
                valid = (qi >= kj) & (qi <= kj + BAND)
                if j == nj - 1:
                    valid = valid & ((qi < sb) | (n < nt - 1))
                p = jnp.exp(jnp.where(valid, sc - _lanes(window(lc_ref, ln_ref, r, j), sb), -1e30))
                dp = lax.dot_general(dov, v, NT_DIMS, preferred_element_type=F32)
                ds = p * (dp - _lanes(window(dc_ref, dn_ref, r, j), sb)) * scale
                dvf[rows, :] = lax.dot_general(p.astype(BF16), dov, TN_DIMS, preferred_element_type=F32)
                dk_ref[rows, :] = lax.dot_general(ds.astype(BF16), q, TN_DIMS, preferred_element_type=F32)
        dv_ref[...] = dvf[...].astype(BF16)

    cur, side, tok, tok_side = _attn_specs(g, tb, pb, s, ahead=True)
    anyspec = pl.BlockSpec(memory_space=pl.ANY)
    return pl.pallas_call(
        body, name=f"attn_dkv_g{g}", grid=(N_HEADS, nt),
        out_shape=[jax.ShapeDtypeStruct((s, QKW), F32), jax.ShapeDtypeStruct((s, IN_W), BF16)],
        in_specs=[cur(K_COL), cur(V_COL), cur(Q_COL), side(Q_COL), tok, tok_side, tok, tok_side,
                  tok, tok_side, anyspec, anyspec],
        out_specs=[cur(K_COL), cur(V_COL)],
        input_output_aliases={10: 0, 11: 1},
        scratch_shapes=[pltpu.VMEM((tb, HD), ft), pltpu.VMEM((tb, HD), ft),
                        pltpu.VMEM((tb + pb, HD), ft), pltpu.VMEM((tb, HD), F32)],
        compiler_params=_params("arbitrary", "arbitrary"),
    )(qkn, proj, qkn, qkn, do, do, lse, lse, delta, delta, dqkn, dproj)


def _shift_down(x, before, k):
    rolled = pltpu.roll(x, k, 0)
    head = jnp.where(lax.broadcasted_iota(jnp.int32, before.shape, 0) < k, pltpu.roll(before, k, 0), rolled[:8])
    return jnp.concatenate([head, rolled[8:]], axis=0)


def _shift_up(x, after, k):
    rows = x.shape[0]
    rolled = pltpu.roll(x, rows - k, 0)
    tail = jnp.where(lax.broadcasted_iota(jnp.int32, after.shape, 0) >= 8 - k,
                     pltpu.roll(after, 8 - k, 0), rolled[rows - 8:])
    return jnp.concatenate([rolled[:rows - 8], tail], axis=0)


def _conv_fwd(proj, cw, tm=1024):
    s = proj.shape[0]
    r16 = tm // 16

    def body(u_ref, b_ref, c_ref, up_ref, cp_ref, w_ref, z_ref):
        i = pl.program_id(1)
        xc = c_ref[...].astype(F32) * u_ref[...].astype(F32)
        xp = jnp.where(i > 0, cp_ref[8:16, :].astype(F32) * up_ref[8:16, :].astype(F32), 0.0)
        w = w_ref[...]
        conv = _shift_down(xc, xp, 2) * w[0:1] + _shift_down(xc, xp, 1) * w[1:2] + xc * w[2:3]
        z_ref[...] = (b_ref[...].astype(F32) * conv).astype(BF16)

    tile = lambda blk: pl.BlockSpec((tm, COL), lambda j, i: (i, blk + j))
    before = lambda blk: pl.BlockSpec((16, COL), lambda j, i: (jnp.maximum(i * r16 - 1, 0), blk + j))
    return pl.pallas_call(
        body, name="conv_fwd", grid=(D // COL, s // tm),
        out_shape=jax.ShapeDtypeStruct((s, D), BF16),
        in_specs=[tile(U_BLK), tile(B_BLK), tile(C_BLK), before(U_BLK), before(C_BLK),
                  pl.BlockSpec((3, COL), lambda j, i: (0, j))],
        out_specs=pl.BlockSpec((tm, COL), lambda j, i: (i, j)),
        compiler_params=_params("parallel", "parallel"),
    )(proj, proj, proj, proj, proj, cw)


def _conv_bwd(dz, proj, cw, dproj, tm=1024):
    s = proj.shape[0]
    r8, r16 = tm // 8, tm // 16
    nrow = s // tm

    def body(dz_ref, u_ref, b_ref, c_ref, up_ref, cp_ref, dzn_ref, bn_ref, w_ref, _, o_ref, acc_ref):
        piece, i = pl.program_id(1), pl.program_id(2)
        u, c = u_ref[...].astype(F32), c_ref[...].astype(F32)
        bv = b_ref[...].astype(F32)
        dzv = dz_ref[...]
        w = w_ref[...]

        @pl.when((piece == 0) & (i == 0))
        def _():
            acc_ref[...] = jnp.zeros_like(acc_ref)

        @pl.when(piece == 1)
        def _():
            xc = c * u
            xp = jnp.where(i > 0, cp_ref[8:16, :].astype(F32) * up_ref[8:16, :].astype(F32), 0.0)
            x2, x1 = _shift_down(xc, xp, 2), _shift_down(xc, xp, 1)
            o_ref[...] = (dzv * (x2 * w[0:1] + x1 * w[1:2] + xc * w[2:3])).astype(BF16)
            dconv = dzv * bv
            acc_ref[0:1, :] += jnp.sum(dconv * x2, axis=0, keepdims=True)
            acc_ref[1:2, :] += jnp.sum(dconv * x1, axis=0, keepdims=True)
            acc_ref[2:3, :] += jnp.sum(dconv * xc, axis=0, keepdims=True)

        @pl.when(piece != 1)
        def _():
            dconv = dzv * bv
            dn = jnp.where(i < nrow - 1, dzn_ref[...] * bn_ref[0:8, :].astype(F32), 0.0)
            dxc = dconv * w[2:3] + _shift_up(dconv, dn, 1) * w[1:2] + _shift_up(dconv, dn, 2) * w[0:1]
            o_ref[...] = (dxc * jnp.where(piece == 0, c, u)).astype(BF16)

    tile = lambda blk: pl.BlockSpec((tm, COL), lambda j, p, i: (i, blk + j))
    before = lambda blk: pl.BlockSpec((16, COL), lambda j, p, i: (jnp.maximum(i * r16 - 1, 0), blk + j))
    after = lambda rows, blk: pl.BlockSpec(
        (rows, COL), lambda j, p, i: (jnp.minimum((i + 1) * (tm // rows), s // rows - 1), blk + j))
    return pl.pallas_call(
        body, name="conv_bwd", grid=(D // COL, 3, nrow),
        out_shape=[jax.ShapeDtypeStruct((s, IN_W), BF16), jax.ShapeDtypeStruct((8, D), F32)],
        in_specs=[tile(0), tile(U_BLK), tile(B_BLK), tile(C_BLK), before(U_BLK), before(C_BLK),
                  after(8, 0), after(16, B_BLK), pl.BlockSpec((3, COL), lambda j, p, i: (0, j)),
                  pl.BlockSpec(memory_space=pl.ANY)],
        out_specs=[pl.BlockSpec((tm, COL), lambda j, p, i: (i, U_BLK + 2 * p + j)),
                   pl.BlockSpec((8, COL), lambda j, p, i: (0, j))],
        input_output_aliases={9: 0},
        compiler_params=_params("arbitrary", "arbitrary", "arbitrary"),
    )(dz, proj, proj, proj, proj, proj, dz, proj, cw, dproj)


def _merge_fwd(ya, yc, proj, tm=512):
    s = proj.shape[0]

    def body(ya_ref, yc_ref, ga_ref, gc_ref, o_ref):
        o_ref[...] = (_sigmoid(ga_ref[...].astype(F32)) * ya_ref[...].astype(F32)
                      + _sigmoid(gc_ref[...].astype(F32)) * yc_ref[...].astype(F32)).astype(BF16)

    tile = lambda blk: pl.BlockSpec((tm, COL), lambda j, i: (i, blk + j))
    return pl.pallas_call(
        body, name="merge_fwd", grid=(D // COL, s // tm),
        out_shape=jax.ShapeDtypeStruct((s, D), BF16),
        in_specs=[tile(0), tile(0), tile(GA_BLK), tile(GC_BLK)], out_specs=tile(0),
        compiler_params=_params("parallel", "parallel"),
    )(ya, yc, proj, proj)


def _merge_bwd_branches(dm, proj, tm=512):
    s = proj.shape[0]

    def body(dm_ref, ga_ref, gc_ref, dya_ref, dyc_ref):
        dmv = dm_ref[...]
        dya_ref[...] = (dmv * _sigmoid(ga_ref[...].astype(F32))).astype(BF16)
        dyc_ref[...] = (dmv * _sigmoid(gc_ref[...].astype(F32))).astype(BF16)

    tile = lambda blk: pl.BlockSpec((tm, COL), lambda j, i: (i, blk + j))
    return pl.pallas_call(
        body, name="merge_bwd_branches", grid=(D // COL, s // tm),
        out_shape=[jax.ShapeDtypeStruct((s, D), BF16)] * 2,
        in_specs=[tile(0), tile(GA_BLK), tile(GC_BLK)], out_specs=[tile(0)] * 2,
        compiler_params=_params("parallel", "parallel"),
    )(dm, proj, proj)


def _merge_bwd_gates(dm, ya, yc, proj, tm=1024):
    s = proj.shape[0]
    half = D // COL

    def body(dm_ref, ya_ref, yc_ref, g_ref, o_ref):
        y = jnp.where(pl.program_id(0) < half, ya_ref[...].astype(F32), yc_ref[...].astype(F32))
        sig = _sigmoid(g_ref[...].astype(F32))
        o_ref[...] = (dm_ref[...] * y * sig * (1.0 - sig)).astype(BF16)

    chan = pl.BlockSpec((tm, COL), lambda jj, i: (i, jj % half))
    gate = pl.BlockSpec((tm, COL), lambda jj, i: (i, GA_BLK + jj))
    return pl.pallas_call(
        body, name="merge_bwd_gates", grid=(2 * half, s // tm),
        out_shape=jax.ShapeDtypeStruct((s, IN_W), BF16),
        in_specs=[chan, chan, chan, gate], out_specs=gate,
        compiler_params=_params("parallel", "parallel"),
    )(dm, ya, yc, proj)


def _mod_part(c_all, w_ada, b_part):
    def body(c_ref, w_ref, b_ref, o_ref):
        cv = c_ref[...]
        act = cv * _sigmoid(cv)
        o_ref[...] = jnp.dot(act, w_ref[...], preferred_element_type=F32,
                             precision=lax.Precision.HIGHEST) + b_ref[...]

    return pl.pallas_call(
        body, name="mod_part", out_shape=jax.ShapeDtypeStruct((N_DEV, w_ada.shape[1]), F32),
    )(c_all, w_ada, b_part)


def _w_ada_grad(c_all_t, dmod_part):
    def body(c_ref, d_ref, o_ref):
        cv = c_ref[...]
        act = cv * _sigmoid(cv)
        dv = d_ref[...]
        acc = act[:, 0:1] * dv[0:1, :]
        for b in range(1, N_DEV):
            acc = acc + act[:, b:b + 1] * dv[b:b + 1, :]
        o_ref[...] = acc

    return pl.pallas_call(
        body, name="w_ada_grad", out_shape=jax.ShapeDtypeStruct((D, dmod_part.shape[1]), F32),
    )(c_all_t, dmod_part)


def _sum_rows(name, v):
    def body(v_ref, o_ref):
        acc = v_ref[0]
        for k in range(1, N_DEV):
            acc = acc + v_ref[k]
        o_ref[...] = acc

    return pl.pallas_call(body, name=name, out_shape=jax.ShapeDtypeStruct(v.shape[1:], F32))(v)


def _adamw(name, w, g, m, v):
    rows, cols = w.shape
    limit = max(16, (1 << 20) // (4 * cols))
    tr = rows if rows <= limit else next((t for t in range(limit - limit % 16, 15, -16) if rows % t == 0), rows)
    c1 = 1.0 - ADAM_B1 ** ADAM_STEP
    c2 = 1.0 - ADAM_B2 ** ADAM_STEP
    parts = g.ndim == 3

    def body(w_ref, g_ref, m_ref, v_ref, go_ref, d_ref, nm_ref, nv_ref):
        if parts:
            gv = g_ref[0].astype(F32)
            for k in range(1, N_DEV):
                gv = gv + g_ref[k].astype(F32)
        else:
            gv = g_ref[...]
        go_ref[...] = gv
        nm = ADAM_B1 * m_ref[...] + (1.0 - ADAM_B1) * gv
        nv = ADAM_B2 * v_ref[...] + (1.0 - ADAM_B2) * (gv * gv)
        nm_ref[...] = nm
        nv_ref[...] = nv
        d_ref[...] = -ADAM_LR * ((nm / c1) / (jnp.sqrt(nv / c2) + ADAM_EPS) + ADAM_WD * w_ref[...])

    spec = pl.BlockSpec((tr, cols), lambda i: (i, 0))
    g_spec = pl.BlockSpec((N_DEV, tr, cols), lambda i: (0, i, 0)) if parts else spec
    return pl.pallas_call(
        body, name=name, grid=(rows // tr,),
        out_shape=[jax.ShapeDtypeStruct((rows, cols), F32)] * 4,
        in_specs=[spec, g_spec, spec, spec], out_specs=[spec] * 4,
        compiler_params=_params("parallel"),
    )(w, g, m, v)


def _adamw_small(ws, gs, ms, vs):
    n = len(ws)
    c1 = 1.0 - ADAM_B1 ** ADAM_STEP
    c2 = 1.0 - ADAM_B2 ** ADAM_STEP

    def body(*refs):
        for i in range(n):
            w_ref, g_ref, m_ref, v_ref = refs[i], refs[n + i], refs[2 * n + i], refs[3 * n + i]
            d_ref, nm_ref, nv_ref = refs[4 * n + 3 * i:4 * n + 3 * i + 3]
            gv = g_ref[...]
            nm = ADAM_B1 * m_ref[...] + (1.0 - ADAM_B1) * gv
            nv = ADAM_B2 * v_ref[...] + (1.0 - ADAM_B2) * (gv * gv)
            nm_ref[...] = nm
            nv_ref[...] = nv
            d_ref[...] = -ADAM_LR * ((nm / c1) / (jnp.sqrt(nv / c2) + ADAM_EPS) + ADAM_WD * w_ref[...])

    outs = pl.pallas_call(
        body, name="adamw_small",
        out_shape=[jax.ShapeDtypeStruct(w.shape, F32) for w in ws for _ in range(3)],
    )(*ws, *gs, *ms, *vs)
    return [tuple(outs[3 * i:3 * i + 3]) for i in range(n)]


HALF = FF // 2


def _sds(shape, dtype):
    return jax.ShapeDtypeStruct(shape, dtype)


def _row_tile(w):
    return lambda tm: ((tm, w), lambda i, j: (i, 0))


def _one(w):
    return lambda rows: ((rows, w), lambda i, j: (0, 0))


def _gate_up_swiglu(name, h, wgu, carry=None, tm=512):
    s = h.shape[0]
    tm = min(tm, s)

    def epilogue(prod, first, tin, tout):
        ab_ref, s_ref = tout
        ab_ref[...] = prod.astype(BF16)
        a, b = prod[:, :HALF], prod[:, HALF:]
        s_ref[...] = (a * _sigmoid(a) * b).astype(BF16)

    return _mm(name, h, wgu, "NT", None, tm, FF, D, carry=carry, n_outer=True, epilogue=epilogue,
               tiles_out=[(_sds((s, 2 * FF), BF16), (tm, FF), lambda i, j: (i, j)),
                          (_sds((s, FF), BF16), (tm, HALF), lambda i, j: (i, j))])


def _d_hidden_swiglu(name, df, wd, ab, tm=512):
    s = df.shape[0]
    tm = min(tm, s)

    def epilogue(prod, first, tin, tout, cols):
        da_cols = slice(cols[0], cols[0] + cols[1])
        db_cols = slice(HALF + cols[0], HALF + cols[0] + cols[1])
        a = tin[0][:, da_cols].astype(F32)
        b = tin[0][:, db_cols].astype(F32)
        sig = _sigmoid(a)
        tout[0][:, da_cols] = (prod * b * (sig * (1.0 + a * (1.0 - sig)))).astype(BF16)
        tout[0][:, db_cols] = (prod * (a * sig)).astype(BF16)

    chunks = [(c0, min(384, HALF - c0)) for c0 in range(0, HALF, 384)]
    return _mm(name, df, wd, "NT", None, tm, HALF, D, n_outer=True, epilogue=epilogue, col_chunks=chunks,
               tiles_in=[(ab, (tm, FF), lambda i, j: (i, j))],
               tiles_out=[(_sds((s, 2 * FF), BF16), (tm, FF), lambda i, j: (i, j))])[0]


def _out_residual(name, a, w, x, gt, coef, nxt, tm=512, tk=FF):
    s = a.shape[0]
    tm = min(tm, s)

    def epilogue(prod, first, tin, tout):
        x_ref, gt_ref, g_ref, sc_ref, sh_ref = tin
        f_ref, xn_ref, h_ref = tout
        f_ref[...] = prod
        xn = x_ref[...] + (coef * gt_ref[...]) * prod
        xn_ref[...] = xn
        r = lax.rsqrt(jnp.mean(xn * xn, axis=-1, keepdims=True) + EPS)
        h_ref[...] = ((xn * r) * g_ref[...] * (1.0 + sc_ref[...]) + sh_ref[...]).astype(BF16)

    row, vec = _row_tile(D)(tm), _one(D)(1)
    return _mm(name, a, w, "NN", None, tm, D, tk, epilogue=epilogue,
               tiles_in=[(x, *row), (gt, *vec)] + [(v, *vec) for v in nxt],
               tiles_out=[(_sds((s, D), F32), *row), (_sds((s, D), F32), *row), (_sds((s, D), BF16), *row)])


def _out_loss(name, a, w, x, gt, coef, target, tm=512):
    s = a.shape[0]
    tm = min(tm, s)

    def epilogue(prod, first, tin, tout):
        x_ref, gt_ref, t_ref = tin
        f_ref, g_ref, df_ref, acc_ref = tout
        f_ref[...] = prod
        cg = coef * gt_ref[...]
        e = x_ref[...] + cg * prod - t_ref[...]
        gv = e * (1.0 / D)
        g_ref[...] = gv
        df_ref[...] = (cg * gv).astype(BF16)

        @pl.when(first)
        def _():
            acc_ref[...] = jnp.zeros_like(acc_ref)

        acc_ref[0:1, :] += coef * jnp.sum(gv * prod, axis=0, keepdims=True)
        acc_ref[1:2, :] += (0.5 / D) * jnp.sum(e * e, axis=0, keepdims=True)

    row, vec = _row_tile(D)(tm), _one(D)(1)
    return _mm(name, a, w, "NN", None, tm, D, FF, epilogue=epilogue,
               tiles_in=[(x, *row), (gt, *vec), (target, *row)],
               tiles_out=[(_sds((s, D), F32), *row), (_sds((s, D), F32), *row), (_sds((s, D), BF16), *row),
                          (_sds((8, D), F32), *_one(D)(8))])


def _d_h_norm_bwd(name, da, w, x, gin, g, sc, sh, before=None, carry=None, tm=256):
    s = da.shape[0]
    tm = min(tm, s)
    coef = before[2] if before else None

    def epilogue(prod, first, tin, tout):
        x_ref, gin_ref, g_ref, sc_ref, sh_ref = tin[:5]
        gout_ref, acc_ref = tout[:2]
        xv = x_ref[...]
        r = lax.rsqrt(jnp.mean(xv * xv, axis=-1, keepdims=True) + EPS)
        nv = xv * r
        gv, one_sc = g_ref[...], 1.0 + sc_ref[...]
        dn = prod * gv * one_sc
        gout = gin_ref[...] + r * (dn - nv * jnp.mean(dn * nv, axis=-1, keepdims=True))
        gout_ref[...] = gout

        @pl.when(first)
        def _():
            acc_ref[...] = jnp.zeros_like(acc_ref)

        dhn = prod * nv
        acc_ref[0:1, :] += jnp.sum(prod, axis=0, keepdims=True)
        acc_ref[1:2, :] += jnp.sum(dhn * gv, axis=0, keepdims=True)
        acc_ref[2:3, :] += jnp.sum(dhn * one_sc, axis=0, keepdims=True)
        if before:
            f_ref, gt_ref = tin[5:]
            tout[2][...] = ((coef * gt_ref[...]) * gout).astype(BF16)
            acc_ref[3:4, :] += coef * jnp.sum(gout * f_ref[...], axis=0, keepdims=True)

    row, vec = _row_tile(D)(tm), _one(D)(1)
    tiles_in = [(x, *row), (gin, *row), (g, *vec), (sc, *vec), (sh, *vec)]
    tiles_out = [(_sds((s, D), F32), *row), (_sds((8, D), F32), *_one(D)(8))]
    if before:
        tiles_in += [(before[0], *row), (before[1], *vec)]
        tiles_out.append((_sds((s, D), BF16), *row))
    return _mm(name, da, w, "NN", None, tm, D, da.shape[1], epilogue=epilogue, carry=carry, keep_b=True,
               tiles_in=tiles_in, tiles_out=tiles_out)


def _gate_tiles(proj, tm):
    return [(proj, (tm, COL), (lambda i, j, blk=blk: (i, blk))) for blk in (GA_BLK, GA_BLK + 1, GC_BLK, GC_BLK + 1)]


def _conv_branch_merge(z, wc, ya, proj, tm=512):
    s = z.shape[0]
    tm = min(tm, s)

    def epilogue(prod, first, tin, tout):
        ya_ref, ga0, ga1, gc0, gc1 = tin
        tout[0][...] = prod.astype(BF16)
        for half, (ga, gc) in enumerate(((ga0, gc0), (ga1, gc1))):
            cols = slice(half * COL, (half + 1) * COL)
            tout[1][:, cols] = (_sigmoid(ga[...].astype(F32)) * ya_ref[:, cols].astype(F32)
                                + _sigmoid(gc[...].astype(F32)) * prod[:, cols]).astype(BF16)

    row = _row_tile(D)(tm)
    return _mm("mix_conv_branch", z, wc, "NN", None, tm, D, D, epilogue=epilogue,
               tiles_in=[(ya, *row)] + _gate_tiles(proj, tm),
               tiles_out=[(_sds((s, D), BF16), *row), (_sds((s, D), BF16), *row)])


def _d_merged_branches(dmix, wo, proj, tm=512):
    s = dmix.shape[0]
    tm = min(tm, s)

    def epilogue(prod, first, tin, tout):
        ga0, ga1, gc0, gc1 = tin
        tout[0][...] = prod
        for half, (ga, gc) in enumerate(((ga0, gc0), (ga1, gc1))):
            cols = slice(half * COL, (half + 1) * COL)
            tout[1][:, cols] = (prod[:, cols] * _sigmoid(ga[...].astype(F32))).astype(BF16)
            tout[2][:, cols] = (prod[:, cols] * _sigmoid(gc[...].astype(F32))).astype(BF16)

    row = _row_tile(D)(tm)
    return _mm("mix_d_merged", dmix, wo, "NT", None, tm, D, D, epilogue=epilogue,
               tiles_in=_gate_tiles(proj, tm),
               tiles_out=[(_sds((s, D), F32), *row), (_sds((s, D), BF16), *row), (_sds((s, D), BF16), *row)])


def _d_o_delta(dya, wa_t, o, tm=1024):
    s = dya.shape[0]
    tm = min(tm, s)

    def epilogue(prod, first, tin, tout):
        tout[0][...] = prod
        tout[1][...] = _heads(prod * tin[0][...].astype(F32), lambda ph, h: jnp.broadcast_to(
            jnp.sum(ph, axis=-1, keepdims=True), ph.shape))

    row = _row_tile(COL)(tm)
    return _mm("mix_d_o", dya, wa_t, "NN", None, tm, COL, D, epilogue=epilogue,
               tiles_in=[(o, *row)], tiles_out=[(_sds((s, COL), F32), *row), (_sds((s, COL), F32), *row)])


def _ffn_bwd(tag, df, x, gin, h, ab, sw, g, sc, sh, wgu, wd, before=None, carry_down=None, carry_gate_up=None,
             tk_dw=2048):
    dab = _d_hidden_swiglu(f"{tag}_d_hidden", df, wd, ab)
    dwd = _mm(f"{tag}_dw_down", sw, df, "TN", BF16, HALF, D, tk_dw)
    carried = []
    if carry_down:
        dwgu, *got = _mm(f"{tag}_dw_gate_up", dab, h, "TN", BF16, HALF, D, tk_dw, carry=carry_down(dwd))
        carried += got
    else:
        dwgu = _mm(f"{tag}_dw_gate_up", dab, h, "TN", BF16, HALF, D, tk_dw)
    res = _d_h_norm_bwd(f"{tag}_d_h", dab, wgu, x, gin, g, sc, sh, before=before,
                        carry=carry_gate_up(dwgu) if carry_gate_up else None)
    n_own = 3 if before else 2
    return res[:n_own], dwgu, dwd, carried + list(res[n_own:])


def kernel(x, c, w_ada, b_ada, norm_ffn1, ffn1_w_gate, ffn1_w_up, ffn1_w_down, norm_mix, w_in, q_norm, k_norm, conv_w, w_attn_branch, w_conv_branch, w_out, norm_ffn2, ffn2_w_gate, ffn2_w_up, ffn2_w_down, loss_target, m_w_ada, m_b_ada, m_norm_ffn1, m_ffn1_w_gate, m_ffn1_w_up, m_ffn1_w_down, m_norm_mix, m_w_in, m_q_norm, m_k_norm, m_conv_w, m_w_attn_branch, m_w_conv_branch, m_w_out, m_norm_ffn2, m_ffn2_w_gate, m_ffn2_w_up, m_ffn2_w_down, v_w_ada, v_b_ada, v_norm_ffn1, v_ffn1_w_gate, v_ffn1_w_up, v_ffn1_w_down, v_norm_mix, v_w_in, v_q_norm, v_k_norm, v_conv_w, v_w_attn_branch, v_w_conv_branch, v_w_out, v_norm_ffn2, v_ffn2_w_gate, v_ffn2_w_up, v_ffn2_w_down):
    me = 4 * lax.axis_index("x") + 2 * lax.axis_index("y") + lax.axis_index("c")
    x0, target = x[0], loss_target[0]
    s = x0.shape[0]
    ada_cols = w_ada.shape[2]
    cw_cols = conv_w.shape[2]

    gathered = _small_allgather(
        "gather_c_conv", jnp.concatenate([c, conv_w[0].reshape(1, 3 * cw_cols)], axis=1))[:, 0]
    c_all = gathered[:, :D]
    cw = gathered[:, D:].reshape(N_DEV, 3, cw_cols).transpose(1, 0, 2).reshape(3, D)
    b_part = lax.dynamic_slice(b_ada, (0, me * ada_cols), (1, ada_cols))
    mod_part = _mod_part(c_all, w_ada[0], b_part)
    mod_all = _small_allgather("gather_mod", mod_part.reshape(1, N_DEV * ada_cols))
    mod = lax.dynamic_slice(mod_all.reshape(N_DEV, N_DEV, ada_cols), (0, me, 0), (N_DEV, 1, ada_cols))
    mod = mod.reshape(N_MOD, 1, D)
    sh1, sc1, gt1, sh2, sc2, gt2, sh3, sc3, gt3 = [mod[i] for i in range(N_MOD)]

    tb = lambda w: w[0].T.astype(BF16)
    nb = lambda w: w[0].astype(BF16)
    ffn1_shards = [tb(ffn1_w_gate), tb(ffn1_w_up), nb(ffn1_w_down)]
    ffn2_shards = [tb(ffn2_w_gate), tb(ffn2_w_up), nb(ffn2_w_down)]
    mix_shards = [tb(w_in), tb(w_attn_branch), nb(w_conv_branch), nb(w_out)]
    ffn_dst, ffn_base, ffn_jump, ffn_shapes = [0, 0, 1], [0, HALF, 0], [HALF, HALF, 0], [(2 * FF, D), (FF, D)]
    mix_dst, mix_base, mix_shapes = [0, 1, 2, 3], [0, 0, 0, 0], [(IN_W, D), (D, COL), (D, D), (D, D)]
    wgu1, wd1 = _run_plan("gather_ffn1_weights",
                          _gather_plan(ffn1_shards, ffn_dst, ffn_base, ffn_shapes, ffn_jump))

    h1 = _normmod("ffn1_normmod", x0, norm_ffn1, sc1, sh1)
    ab1, s1, win_t = _gate_up_swiglu(
        "ffn1_gate_up", h1, wgu1, carry=_gather_plan(mix_shards[:1], mix_dst[:1], mix_base[:1], mix_shapes[:1]))
    f1, x1, h2 = _out_residual("ffn1_down", s1, wd1, x0, gt1, 0.5, (norm_mix, sc2, sh2))
    proj, wgu2, wd2, wa_t, wc, wo = _mm(
        "mix_in_proj", h2, win_t, "NT", BF16, 1024, IN_W // 4, D, n_outer=True,
        carry=_gather_plan(ffn2_shards + mix_shards[1:], ffn_dst + [2, 3, 4], ffn_base + [0, 0, 0],
                           ffn_shapes + mix_shapes[1:], ffn_jump + [0, 0, 0]))
    wqk = jnp.concatenate([jnp.tile(q_norm, (1, 12)), jnp.tile(k_norm, (1, 12))], axis=1)
    qkn = _qknorm(proj, wqk)
    group_out = [_attn_fwd(g, qkn, proj) for g in range(3)]
    o, lse = _attn_combine([go[0] for go in group_out], [go[1] for go in group_out])
    ya = _mm("mix_attn_branch", o, wa_t, "NT", BF16, 1024, 1024, COL)
    z = _conv_fwd(proj, cw)
    yc, merged = _conv_branch_merge(z, wc, ya, proj)
    mix, x2, h3 = _out_residual("mix_out_proj", merged, wo, x1, gt2, 1.0, (norm_ffn2, sc3, sh3), tk=D)
    ab3, s3 = _gate_up_swiglu("ffn2_gate_up", h3, wgu2)
    f3, g3, df3, acc_out = _out_loss("ffn2_down", s3, wd2, x2, gt3, 0.5, target)
    loss = lax.psum(jnp.sum(acc_out[1]), ("x", "y", "c"))

    ffn_rows = [sh_.shape[0] for sh_ in ffn1_shards]
    mix_rows = [sh_.shape[0] for sh_ in mix_shards]
    (g2, acc3, dmix), dwgu2, dwd2, _ = _ffn_bwd(
        "ffn2", df3, x2, g3, h3, ab3, s3, norm_ffn2, sc3, sh3, wgu2, wd2, before=(mix, gt2, 1.0))
    dmerged, dya, dyc = _d_merged_branches(dmix, wo, proj)
    dwo = _mm("mix_dw_out", merged, dmix, "TN", BF16, 1024, 1024, 2048)
    dproj = _merge_bwd_gates(dmerged, ya, yc, proj)
    dwc = _mm("mix_dw_conv_branch", z, dyc, "TN", BF16, 1024, 1024, 2048)
    dz = _mm("mix_d_z", dyc, wc, "NT", F32, 1024, 1024, D)
    dproj, cw_acc = _conv_bwd(dz, proj, cw, dproj)
    dwa_t = _mm("mix_dw_attn_branch", dya, o, "TN", BF16, 1024, COL, 2048)
    do, delta = _d_o_delta(dya, wa_t, o)
    dqkn = None
    for g in range(3):
        dqkn = _attn_dq(g, qkn, proj, do, lse, delta, dqkn)
    for g in range(3):
        dqkn, dproj = _attn_dkv(g, qkn, proj, do, lse, delta, dqkn, dproj)
    dproj, wqk_acc = _qknorm_bwd(proj, dqkn, wqk, dproj)
    dwin_t, r_f2g, r_f2u, r_f2d, r_wa, r_wc, r_wo = _mm(
        "mix_dw_in", dproj, h2, "TN", BF16, IN_W // 4, COL, 2048,
        carry=_scatter_plan([dwgu2, dwd2, dwa_t, dwc, dwo], [0, 0, 1, 2, 3, 4], [0, HALF, 0, 0, 0, 0],
                            ffn_rows + mix_rows[1:], [D, D, D, COL, D, D], [HALF, HALF, 0, 0, 0, 0]))
    g1, acc2, df1, r_win = _d_h_norm_bwd(
        "mix_d_h", dproj, win_t, x1, g2, norm_mix, sc2, sh2, before=(f1, gt1, 0.5),
        carry=_scatter_plan([dwin_t], [0], [0], mix_rows[:1], [D]))
    (g0, acc1), dwgu1, dwd1, (r_f1d, r_f1g, r_f1u) = _ffn_bwd(
        "ffn1", df1, x0, g1, h1, ab1, s1, norm_ffn1, sc1, sh1, wgu1, wd1,
        carry_down=lambda dwd: _scatter_plan([dwd], [0], [0], ffn_rows[2:], [D]),
        carry_gate_up=lambda dwgu: _scatter_plan([dwgu], [0, 0], [0, HALF], ffn_rows[:2], [D, D], [HALF, HALF]))

    dqw = jnp.sum(wqk_acc[0, :QKW // 2].reshape(12, HD), axis=0)
    dkw = jnp.sum(wqk_acc[0, QKW // 2:].reshape(12, HD), axis=0)
    small = jnp.concatenate([
        acc1[0], acc1[1], acc2[3], acc2[0], acc2[1], acc3[3], acc3[0], acc3[1], acc_out[0],
        acc1[2], acc2[2], acc3[2], dqw, dkw, cw_acc[0:3].reshape(3 * D)]).reshape(1, -1)
    small_all = _small_allgather("gather_small_grads", small)
    small_sum = _sum_rows("sum_small_grads", small_all)[0]
    n_mod = N_MOD * D
    g_b_ada = small_sum[:n_mod].reshape(1, n_mod)
    g_norm1, g_norm2, g_norm3 = [small_sum[n_mod + i * D:n_mod + (i + 1) * D].reshape(1, D) for i in range(3)]
    off = n_mod + 3 * D
    g_qn, g_kn = small_sum[off:off + HD].reshape(1, HD), small_sum[off + HD:off + 2 * HD].reshape(1, HD)
    g_cw_full = small_sum[off + 2 * HD:].reshape(3, D)
    g_cw = lax.dynamic_slice(g_cw_full, (0, me * cw_cols), (3, cw_cols))
    dmod_part = lax.dynamic_slice(small_all[:, 0, :n_mod], (0, me * ada_cols), (N_DEV, ada_cols))
    g_w_ada = _w_ada_grad(c_all.T, dmod_part)

    as_rows = {"ffn1_w_gate", "ffn1_w_up", "w_in", "w_attn_branch", "ffn2_w_gate", "ffn2_w_up"}
    grad_list = [g_w_ada, g_b_ada, g_norm1, r_f1g, r_f1u, r_f1d, g_norm2, r_win,
                 g_qn, g_kn, g_cw, r_wa, r_wc, r_wo, g_norm3, r_f2g, r_f2u, r_f2d]
    weights = [w_ada, b_ada, norm_ffn1, ffn1_w_gate, ffn1_w_up, ffn1_w_down, norm_mix, w_in, q_norm, k_norm,
               conv_w, w_attn_branch, w_conv_branch, w_out, norm_ffn2, ffn2_w_gate, ffn2_w_up, ffn2_w_down]
    ms = [m_w_ada, m_b_ada, m_norm_ffn1, m_ffn1_w_gate, m_ffn1_w_up, m_ffn1_w_down, m_norm_mix, m_w_in, m_q_norm,
          m_k_norm, m_conv_w, m_w_attn_branch, m_w_conv_branch, m_w_out, m_norm_ffn2, m_ffn2_w_gate,
          m_ffn2_w_up, m_ffn2_w_down]
    vs = [v_w_ada, v_b_ada, v_norm_ffn1, v_ffn1_w_gate, v_ffn1_w_up, v_ffn1_w_down, v_norm_mix, v_w_in, v_q_norm,
          v_k_norm, v_conv_w, v_w_attn_branch, v_w_conv_branch, v_w_out, v_norm_ffn2, v_ffn2_w_gate,
          v_ffn2_w_up, v_ffn2_w_down]
    wnames = ["w_ada", "b_ada", "norm_ffn1", "ffn1_w_gate", "ffn1_w_up", "ffn1_w_down", "norm_mix", "w_in",
              "q_norm", "k_norm", "conv_w", "w_attn_branch", "w_conv_branch", "w_out", "norm_ffn2",
              "ffn2_w_gate", "ffn2_w_up", "ffn2_w_down"]
    small = [i for i, gr in enumerate(grad_list) if gr.ndim == 2 and gr.size <= 16384]
    flat = lambda a, i: a.reshape(-1, weights[i].shape[-1])
    small_res = dict(zip(small, _adamw_small(
        [flat(weights[i], i) for i in small], [flat(grad_list[i], i) for i in small],
        [flat(ms[i], i) for i in small], [flat(vs[i], i) for i in small])))
    grad_out, deltas, new_ms, new_vs = [], [], [], []
    for idx, (nm, w, gr, m_, v_) in enumerate(zip(wnames, weights, grad_list, ms, vs)):
        if idx in small_res:
            gr, dl, nm_, nv_ = [r.reshape(w.shape) for r in (gr, *small_res[idx])]
        elif nm in as_rows:
            res = _adamw(f"adamw_{nm}", w[0].T, gr, m_[0].T, v_[0].T)
            gr, dl, nm_, nv_ = [r.T[None] for r in res]
        else:
            two_d = (-1, w.shape[-1])
            res = _adamw(f"adamw_{nm}", w.reshape(two_d), gr if gr.ndim == 3 else gr.reshape(two_d),
                         m_.reshape(two_d), v_.reshape(two_d))
            gr, dl, nm_, nv_ = [r.reshape(w.shape) for r in res]
        grad_out.append(gr)
        deltas.append(dl)
        new_ms.append(nm_)
        new_vs.append(nv_)
    return (loss, g0[None], *grad_out, *deltas, *new_ms, *new_vs)
```

```python
import functools

import jax
import jax.numpy as jnp
from jax import lax
from jax.experimental import pallas as pl
from jax.experimental.pallas import tpu as pltpu

F32 = jnp.float32
BF16 = jnp.bfloat16
MESH = pl.DeviceIdType.MESH

N_DEV = 8
D = 1024
FF = 2816
HD = 128
N_HEADS = 4
DILATIONS = (1, 4, 16)
BAND = 128
QKW = 2 * 3 * N_HEADS * HD
IN_W = 9728
COL = 512
V_BLK, U_BLK, B_BLK, C_BLK, GA_BLK, GC_BLK = 6, 9, 11, 13, 15, 17
EPS = 1e-6
N_MOD = 9
ADAM_LR, ADAM_B1, ADAM_B2, ADAM_EPS, ADAM_WD, ADAM_STEP = 0.001, 0.9, 0.999, 1e-08, 0.01, 10

NT_DIMS = (((1,), (1,)), ((), ()))
TN_DIMS = (((0,), (0,)), ((), ()))
NN_DIMS = (((1,), (0,)), ((), ()))


def _place():
    return lax.axis_index("x"), lax.axis_index("y"), lax.axis_index("c")


def _flip(coord, bit):
    return 1 - coord if bit else coord


def _params(*sem):
    return pltpu.CompilerParams(dimension_semantics=sem)


def _small_allgather(name, v):
    n = v.shape[-1]

    def body(v_ref, out_ref, send_sems, recv_sems):
        x, y, c = _place()
        me = 4 * x + 2 * y + c
        out_ref[me] = v_ref[...]
        copies = []
        for k in range(1, N_DEV):
            peer = (_flip(x, (k >> 2) & 1), _flip(y, (k >> 1) & 1), _flip(c, k & 1))
            cp = pltpu.make_async_remote_copy(
                src_ref=v_ref, dst_ref=out_ref.at[me], send_sem=send_sems.at[k - 1],
                recv_sem=recv_sems.at[k - 1], device_id=peer, device_id_type=MESH)
            cp.start()
            copies.append(cp)
        for cp in copies:
            cp.wait()

    return pl.pallas_call(
        body, name=name,
        out_shape=jax.ShapeDtypeStruct((N_DEV, 1, n), F32),
        in_specs=[pl.BlockSpec(memory_space=pltpu.VMEM)],
        out_specs=pl.BlockSpec(memory_space=pltpu.VMEM),
        scratch_shapes=[pltpu.SemaphoreType.DMA((N_DEV - 1,)), pltpu.SemaphoreType.DMA((N_DEV - 1,))],
    )(v)


class _Plan:
    def __init__(self, operands, out_shapes, sems, phases):
        self.operands, self.out_shapes, self.sems, self.phases = operands, out_shapes, sems, phases


def _slab_start(base, rows, jump, idx):
    return pl.multiple_of(base + idx * rows + (idx // 4) * jump, 16)


def _gather_plan(shards, dst_of, base_of, dst_shapes, jump_of=None):
    n = len(shards)
    rows = [s.shape[0] for s in shards]
    jump_of = jump_of or [0] * n

    def phases(srcs, dsts, sems):
        send_sems, recv_sems, local_sems = sems
        x, y, c = _place()
        me, sibling = (x, y, c), (x, y, 1 - c)
        chips = [(1 - x, y), (x, 1 - y), (1 - x, 1 - y)]

        def slab(i, px, py, pc):
            start = _slab_start(base_of[i], rows[i], jump_of[i], 4 * px + 2 * py + pc)
            return dsts[dst_of[i]].at[pl.ds(start, rows[i])]

        def copy(i, k, block, to, src=None):
            return pltpu.make_async_remote_copy(
                src_ref=slab(i, *block) if src is None else src, dst_ref=slab(i, *block),
                send_sem=send_sems.at[i, k], recv_sem=recv_sems.at[i, k],
                device_id=to, device_id_type=MESH)

        def mine():
            return [pltpu.make_async_copy(srcs[i], slab(i, *me), local_sems.at[i]) for i in range(n)]

        def first():
            out = []
            for i in range(n):
                out.append(copy(i, 0, me, sibling, src=srcs[i]))
                out += [copy(i, 1 + j, me, (*chip, c), src=srcs[i]) for j, chip in enumerate(chips)]
            return out

        def passed():
            return [(copy(i, 1 + j, (*chip, c), me), copy(i, 4 + j, (*chip, c), sibling))
                    for j, chip in enumerate(chips) for i in range(n)]

        def start():
            for cp in mine() + first():
                cp.start()

        def middle():
            for landed, onward in passed():
                landed.wait_recv()
                onward.start()

        def finish():
            for i in range(n):
                copy(i, 0, sibling, me).wait_recv()
                for j, chip in enumerate(chips):
                    copy(i, 4 + j, (*chip, 1 - c), me).wait_recv()
            for cp in first() + [onward for _, onward in passed()]:
                cp.wait_send()
            for cp in mine():
                cp.wait()

        return start, middle, finish

    sems = [pltpu.SemaphoreType.DMA((n, 7)), pltpu.SemaphoreType.DMA((n, 7)), pltpu.SemaphoreType.DMA((n,))]
    return _Plan(list(shards), [jax.ShapeDtypeStruct(s, BF16) for s in dst_shapes], sems, phases)


def _scatter_plan(grads, src_of, base_of, rows, cols, jump_of=None):
    n = len(rows)
    jump_of = jump_of or [0] * n

    def phases(srcs, recvs, sems):
        send_sems, recv_sems, local_sems = sems
        x, y, c = _place()
        me = 4 * x + 2 * y + c

        def slab(i, idx):
            start = _slab_start(base_of[i], rows[i], jump_of[i], idx)
            return srcs[src_of[i]].at[pl.ds(start, rows[i])]

        def copies():
            out = [pltpu.make_async_copy(slab(i, me), recvs[i].at[me], local_sems.at[i]) for i in range(n)]
            for k in range(1, N_DEV):
                px, py, pc = _flip(x, (k >> 2) & 1), _flip(y, (k >> 1) & 1), _flip(c, k & 1)
                out += [pltpu.make_async_remote_copy(
                    src_ref=slab(i, 4 * px + 2 * py + pc), dst_ref=recvs[i].at[me],
                    send_sem=send_sems.at[i, k - 1], recv_sem=recv_sems.at[i, k - 1],
                    device_id=(px, py, pc), device_id_type=MESH) for i in range(n)]
            return out

        def start():
            for cp in copies():
                cp.start()

        def finish():
            for cp in copies():
                cp.wait()

        return start, None, finish

    sems = [pltpu.SemaphoreType.DMA((n, 7)), pltpu.SemaphoreType.DMA((n, 7)), pltpu.SemaphoreType.DMA((n,))]
    out_shapes = [jax.ShapeDtypeStruct((N_DEV, rows[i], cols[i]), BF16) for i in range(n)]
    return _Plan(list(grads), out_shapes, sems, phases)


def _run_plan(name, plan):
    n_in, n_out = len(plan.operands), len(plan.out_shapes)

    def body(*refs):
        for phase in plan.phases(refs[:n_in], refs[n_in:n_in + n_out], refs[n_in + n_out:]):
            if phase is not None:
                phase()

    hbm = pl.BlockSpec(memory_space=pltpu.HBM)
    return pl.pallas_call(
        body, name=name, out_shape=plan.out_shapes,
        in_specs=[hbm] * n_in, out_specs=[hbm] * n_out, scratch_shapes=plan.sems,
    )(*plan.operands)


def _sum_contributions(name, recv):
    _, rows, cols = recv.shape
    tr = rows if rows <= 512 else 304 if rows % 304 == 0 else 256

    def body(r_ref, o_ref):
        acc = r_ref[0].astype(F32)
        for k in range(1, N_DEV):
            acc = acc + r_ref[k].astype(F32)
        o_ref[...] = acc

    return pl.pallas_call(
        body, name=name, grid=(rows // tr,),
        out_shape=jax.ShapeDtypeStruct((rows, cols), F32),
        in_specs=[pl.BlockSpec((N_DEV, tr, cols), lambda i: (0, i, 0))],
        out_specs=pl.BlockSpec((tr, cols), lambda i: (i, 0)),
        compiler_params=_params("parallel"),
    )(recv)


def _mm(name, a, b, mode, out_dtype, tm, tn, tk, *, carry=None, tiles_in=(), tiles_out=(), epilogue=None,
        n_outer=False, keep_b=False, col_chunks=None):
    if mode == "TN":
        kk, m = a.shape
    else:
        m, kk = a.shape
    n = b.shape[0] if mode == "NT" else b.shape[1]
    tm, tn, tk = min(tm, m), min(tn, n), min(tk, kk)
    assert m % tm == 0 and n % tn == 0 and kk % tk == 0, (name, m, n, kk, tm, tn, tk)
    ni, nj, nk = m // tm, n // tn, kk // tk
    steps = ni * nj * nk
    dims = {"NN": NN_DIMS, "NT": NT_DIMS, "TN": TN_DIMS}[mode]
    if epilogue is None:
        tiles_out = [(jax.ShapeDtypeStruct((m, n), out_dtype), (tm, tn), lambda i, j: (i, j))]
    n_tin, n_tout = len(tiles_in), len(tiles_out)
    n_in = len(carry.operands) if carry else 0
    n_out = len(carry.out_shapes) if carry else 0
    n_acc = 1 if nk > 1 else 0
    n_keep = 2 if keep_b else 0
    assert not carry or steps >= 3
    assert not keep_b or (nk == 1 and nj == 1)
    assert not col_chunks or (epilogue is not None and nk == 1 and mode != "TN")
    ij = (lambda p, q: (q, p)) if n_outer else (lambda p, q: (p, q))
    inner = ni if n_outer else nj

    def body(a_ref, b_ref, *rest):
        tin = rest[:n_tin]
        cin = rest[n_tin:n_tin + n_in]
        tout = rest[n_tin + n_in:n_tin + n_in + n_tout]
        cout = rest[n_tin + n_in + n_tout:n_tin + n_in + n_tout + n_out]
        scratch = rest[n_tin + n_in + n_tout + n_out:]
        k = pl.program_id(2)
        visit = pl.program_id(0) * inner + pl.program_id(1)
        step = visit * nk + k
        if keep_b:
            b_kept, b_sem = scratch[n_acc:n_acc + 2]

            @pl.when(step == 0)
            def _():
                cp = pltpu.make_async_copy(b_ref, b_kept, b_sem)
                cp.start()
                cp.wait()

            b_ref = b_kept
        if carry:
            start, middle, finish = carry.phases(cin, cout, scratch[n_acc + n_keep:])
            pl.when(step == 0)(start)

        def store(prod, c=0, cols=()):
            if epilogue is None:
                tout[0][...] = prod.astype(out_dtype)
            else:
                epilogue(prod, jnp.logical_and(visit == 0, c == 0), tin, tout, *cols)

        if col_chunks:
            for c, (c0, cw) in enumerate(col_chunks):
                b_part = b_ref[pl.ds(c0, cw), :] if mode == "NT" else b_ref[:, pl.ds(c0, cw)]
                store(lax.dot_general(a_ref[...], b_part, dims, preferred_element_type=F32), c, ((c0, cw),))
        else:
            part = lax.dot_general(a_ref[...], b_ref[...], dims, preferred_element_type=F32)
            if nk == 1:
                store(part)
            else:
                acc_ref = scratch[0]

                @pl.when(k == 0)
                def _():
                    acc_ref[...] = part

                @pl.when((k > 0) & (k < nk - 1))
                def _():
                    acc_ref[...] += part

                @pl.when(k == nk - 1)
                def _():
                    store(acc_ref[...] + part)

        if carry:
            if middle is not None:
                pl.when(step == (steps * 3) // 5)(middle)
            pl.when(step == steps - 1)(finish)

    def spec(shape, fn):
        return pl.BlockSpec(shape, lambda p, q, k: fn(*ij(p, q)))

    a_spec = (pl.BlockSpec((tk, tm), lambda p, q, k: (k, ij(p, q)[0])) if mode == "TN"
              else pl.BlockSpec((tm, tk), lambda p, q, k: (ij(p, q)[0], k)))
    if keep_b:
        b_spec = pl.BlockSpec(memory_space=pl.ANY)
    elif mode == "NT":
        b_spec = pl.BlockSpec((tn, tk), lambda p, q, k: (ij(p, q)[1], k))
    else:
        b_spec = pl.BlockSpec((tk, tn), lambda p, q, k: (k, ij(p, q)[1]))
    hbm = pl.BlockSpec(memory_space=pltpu.HBM)
    sequential = carry or epilogue or keep_b
    out = pl.pallas_call(
        body, name=name, grid=(nj, ni, nk) if n_outer else (ni, nj, nk),
        out_shape=[t[0] for t in tiles_out] + (carry.out_shapes if carry else []),
        in_specs=[a_spec, b_spec] + [spec(t[1], t[2]) for t in tiles_in] + [hbm] * n_in,
        out_specs=[spec(t[1], t[2]) for t in tiles_out] + [hbm] * n_out,
        scratch_shapes=([pltpu.VMEM((tm, tn), F32)] * n_acc
                        + ([pltpu.VMEM(b.shape, b.dtype), pltpu.SemaphoreType.DMA] if keep_b else [])
                        + (carry.sems if carry else [])),
        compiler_params=(_params("arbitrary", "arbitrary", "arbitrary") if sequential
                         else _params("parallel", "parallel", "arbitrary")),
    )(a, b, *[t[0] for t in tiles_in], *(carry.operands if carry else []))
    return out if (carry or epilogue) else out[0]


def _row(tm, w, off=0):
    return pl.BlockSpec((tm, w), lambda i: (i, off))


def _vec(w):
    return pl.BlockSpec((1, w), lambda i: (0, 0))


def _sigmoid(x):
    return 0.5 * jnp.tanh(0.5 * x) + 0.5


def _normmod(name, x, g, sc, sh, tm=512):
    s = x.shape[0]

    def body(x_ref, g_ref, sc_ref, sh_ref, h_ref):
        xv = x_ref[...]
        r = lax.rsqrt(jnp.mean(xv * xv, axis=-1, keepdims=True) + EPS)
        h_ref[...] = ((xv * r) * g_ref[...] * (1.0 + sc_ref[...]) + sh_ref[...]).astype(BF16)

    return pl.pallas_call(
        body, name=name, grid=(s // tm,),
        out_shape=jax.ShapeDtypeStruct((s, D), BF16),
        in_specs=[_row(tm, D), _vec(D), _vec(D), _vec(D)], out_specs=_row(tm, D),
        compiler_params=_params("parallel"),
    )(x, g, sc, sh)


def _normmod_bwd(name, dh, x, gin, g, sc, sh, tm=512):
    s = x.shape[0]

    def body(dh_ref, x_ref, gin_ref, g_ref, sc_ref, sh_ref, gout_ref, acc_ref):
        xv, dhv = x_ref[...], dh_ref[...]
        r = lax.rsqrt(jnp.mean(xv * xv, axis=-1, keepdims=True) + EPS)
        nv = xv * r
        gv, one_sc = g_ref[...], 1.0 + sc_ref[...]
        dn = dhv * gv * one_sc
        dx = r * (dn - nv * jnp.mean(dn * nv, axis=-1, keepdims=True))
        gout_ref[...] = gin_ref[...] + dx

        @pl.when(pl.program_id(0) == 0)
        def _():
            acc_ref[...] = jnp.zeros_like(acc_ref)

        dhn = dhv * nv
        acc_ref[0:1, :] += jnp.sum(dhv, axis=0, keepdims=True)
        acc_ref[1:2, :] += jnp.sum(dhn * gv, axis=0, keepdims=True)
        acc_ref[2:3, :] += jnp.sum(dhn * one_sc, axis=0, keepdims=True)

    return pl.pallas_call(
        body, name=name, grid=(s // tm,),
        out_shape=[jax.ShapeDtypeStruct((s, D), F32), jax.ShapeDtypeStruct((8, D), F32)],
        in_specs=[_row(tm, D), _row(tm, D), _row(tm, D), _vec(D), _vec(D), _vec(D)],
        out_specs=[_row(tm, D), pl.BlockSpec((8, D), lambda i: (0, 0))],
        compiler_params=_params("arbitrary"),
    )(dh, x, gin, g, sc, sh)


def _swiglu(name, ab, tm=512):
    s = ab.shape[0]

    def body(ab_ref, s_ref):
        a = ab_ref[:, :FF].astype(F32)
        b = ab_ref[:, FF:].astype(F32)
        s_ref[...] = (a * _sigmoid(a) * b).astype(BF16)

    return pl.pallas_call(
        body, name=name, grid=(s // tm,),
        out_shape=jax.ShapeDtypeStruct((s, FF), BF16),
        in_specs=[_row(tm, 2 * FF)], out_specs=_row(tm, FF),
        compiler_params=_params("parallel"),
    )(ab)


def _swiglu_bwd(name, ds, ab, tm=256):
    s = ab.shape[0]

    def body(ds_ref, ab_ref, dab_ref):
        a = ab_ref[:, :FF].astype(F32)
        b = ab_ref[:, FF:].astype(F32)
        dsv = ds_ref[...].astype(F32)
        sig = _sigmoid(a)
        dab_ref[:, :FF] = (dsv * b * (sig * (1.0 + a * (1.0 - sig)))).astype(BF16)
        dab_ref[:, FF:] = (dsv * (a * sig)).astype(BF16)

    return pl.pallas_call(
        body, name=name, grid=(s // tm,),
        out_shape=jax.ShapeDtypeStruct((s, 2 * FF), BF16),
        in_specs=[_row(tm, FF), _row(tm, 2 * FF)], out_specs=_row(tm, 2 * FF),
        compiler_params=_params("parallel"),
    )(ds, ab)


def _residual(name, x, f, gt, coef, tm=512):
    s = x.shape[0]

    def body(x_ref, f_ref, gt_ref, o_ref):
        o_ref[...] = x_ref[...] + (coef * gt_ref[...]) * f_ref[...]

    return pl.pallas_call(
        body, name=name, grid=(s // tm,),
        out_shape=jax.ShapeDtypeStruct((s, D), F32),
        in_specs=[_row(tm, D), _row(tm, D), _vec(D)], out_specs=_row(tm, D),
        compiler_params=_params("parallel"),
    )(x, f, gt)


def _gate_bwd(name, gin, f, gt, coef, tm=512):
    s = gin.shape[0]

    def body(g_ref, f_ref, gt_ref, df_ref, acc_ref):
        gv = g_ref[...]
        df_ref[...] = ((coef * gt_ref[...]) * gv).astype(BF16)

        @pl.when(pl.program_id(0) == 0)
        def _():
            acc_ref[...] = jnp.zeros_like(acc_ref)

        acc_ref[0:1, :] += coef * jnp.sum(gv * f_ref[...], axis=0, keepdims=True)

    return pl.pallas_call(
        body, name=name, grid=(s // tm,),
        out_shape=[jax.ShapeDtypeStruct((s, D), BF16), jax.ShapeDtypeStruct((8, D), F32)],
        in_specs=[_row(tm, D), _row(tm, D), _vec(D)],
        out_specs=[_row(tm, D), pl.BlockSpec((8, D), lambda i: (0, 0))],
        compiler_params=_params("arbitrary"),
    )(gin, f, gt)


def _loss_grad(x3, target, tm=512):
    s = x3.shape[0]

    def body(y_ref, t_ref, g_ref, l_ref):
        e = y_ref[...] - t_ref[...]
        g_ref[...] = e * (1.0 / D)

        @pl.when(pl.program_id(0) == 0)
        def _():
            l_ref[...] = jnp.zeros_like(l_ref)

        l_ref[...] += jnp.sum(jnp.mean(e * e, axis=-1, keepdims=True), axis=0, keepdims=True) * 0.5

    return pl.pallas_call(
        body, name="loss_grad", grid=(s // tm,),
        out_shape=[jax.ShapeDtypeStruct((s, D), F32), jax.ShapeDtypeStruct((8, 128), F32)],
        in_specs=[_row(tm, D), _row(tm, D)],
        out_specs=[_row(tm, D), pl.BlockSpec((8, 128), lambda i: (0, 0))],
        compiler_params=_params("arbitrary"),
    )(x3, target)


def _heads(x, fn):
    return jnp.concatenate([fn(x[:, h * HD:(h + 1) * HD], h) for h in range(COL // HD)], axis=1)


def _qknorm(proj, wqk, tm=1024):
    s = proj.shape[0]

    def body(p_ref, w_ref, o_ref):
        pv = p_ref[...].astype(F32)
        wv = w_ref[...]

        def one(qh, h):
            r = lax.rsqrt(jnp.mean(qh * qh, axis=-1, keepdims=True) + EPS)
            return (qh * r) * wv[:, h * HD:(h + 1) * HD]

        o_ref[...] = _heads(pv, one).astype(BF16)

    return pl.pallas_call(
        body, name="qknorm", grid=(s // tm, QKW // COL),
        out_shape=jax.ShapeDtypeStruct((s, QKW), BF16),
        in_specs=[pl.BlockSpec((tm, COL), lambda i, j: (i, j)), pl.BlockSpec((1, COL), lambda i, j: (0, j))],
        out_specs=pl.BlockSpec((tm, COL), lambda i, j: (i, j)),
        compiler_params=_params("parallel", "parallel"),
    )(proj, wqk)


def _qknorm_bwd(name, proj, dn, w, dproj, blk0, tm=1024):
    s = proj.shape[0]
    nblk = dn.shape[1] // COL

    def body(p_ref, d_ref, w_ref, _, o_ref, acc_ref):
        pv = p_ref[...].astype(F32)
        dv = d_ref[...]
        wv = w_ref[...]
        sums = []

        def one(qh, h):
            dn = dv[:, h * HD:(h + 1) * HD]
            r = lax.rsqrt(jnp.mean(qh * qh, axis=-1, keepdims=True) + EPS)
            nh = qh * r
            sums.append(jnp.sum(dn * nh, axis=0, keepdims=True))
            dnw = dn * wv[:, h * HD:(h + 1) * HD]
            return r * (dnw - nh * jnp.mean(dnw * nh, axis=-1, keepdims=True))

        o_ref[...] = _heads(pv, one).astype(BF16)

        @pl.when(pl.program_id(1) == 0)
        def _():
            acc_ref[...] = jnp.zeros_like(acc_ref)

        acc_ref[0:1, :] += jnp.concatenate(sums, axis=1)

    return pl.pallas_call(
        body, name=name, grid=(nblk, s // tm),
        out_shape=[jax.ShapeDtypeStruct((s, IN_W), BF16), jax.ShapeDtypeStruct((8, nblk * COL), F32)],
        in_specs=[pl.BlockSpec((tm, COL), lambda j, i: (i, blk0 + j)), pl.BlockSpec((tm, COL), lambda j, i: (i, j)),
                  pl.BlockSpec((1, COL), lambda j, i: (0, j)), pl.BlockSpec(memory_space=pl.ANY)],
        out_specs=[pl.BlockSpec((tm, COL), lambda j, i: (i, blk0 + j)),
                   pl.BlockSpec((8, COL), lambda j, i: (0, j))],
        input_output_aliases={3: 0},
        compiler_params=_params("arbitrary", "arbitrary"),
    )(proj, dn, w, dproj)


def _attn_shapes(s, g):
    d = DILATIONS[g]
    tb = min(s, max(2048, 256 * d))
    sb = min(256, tb // d)
    pb = BAND * d
    assert s % tb == 0 and tb % pb == 0 and (tb // d) % sb == 0 and sb % BAND == 0
    return d, tb, sb, pb


def _lanes(x, width):
    return jnp.concatenate([x] * (width // HD), axis=1)


def _every(start, size, d):
    return pl.ds(start, size, stride=d) if d > 1 else pl.ds(start, size)


def _attn_specs(g, tb, pb, s, ahead):
    ratio = tb // pb
    if ahead:
        nbr = lambda n: jnp.minimum((n + 1) * ratio, s // pb - 1)
    else:
        nbr = lambda n: jnp.maximum(n * ratio - 1, 0)
    cur = lambda base: pl.BlockSpec((tb, HD), lambda h, n: (n, base + g * N_HEADS + h))
    side = lambda base: pl.BlockSpec((pb, HD), lambda h, n: (nbr(n), base + g * N_HEADS + h))
    tok = pl.BlockSpec((tb, HD), lambda h, n: (n, h))
    tok_side = pl.BlockSpec((pb, HD), lambda h, n: (nbr(n), h))
    return cur, side, tok, tok_side


Q_COL, K_COL, V_COL = 0, 12, 24


def _attn_fwd(g, qkn, proj):
    s = qkn.shape[0]
    d, tb, sb, pb = _attn_shapes(s, g)
    ft = F32 if d > 1 else BF16
    nj = tb // d // sb
    scale = HD ** -0.5

    def body(q_ref, kc_ref, kp_ref, vc_ref, vp_ref, o_ref, lse_ref, qf, kf, vf):
        n = pl.program_id(1)
        qf[...] = q_ref[...].astype(ft)
        kf[0:pb] = kp_ref[...].astype(ft)
        kf[pb:] = kc_ref[...].astype(ft)
        vf[0:pb] = vp_ref[...].astype(ft)
        vf[pb:] = vc_ref[...].astype(ft)
        for r in range(d):
            for j in range(nj):
                at = j * sb * d + r
                q = qf[_every(at, sb, d), :].astype(BF16)
                k = kf[_every(at, sb + BAND, d), :].astype(BF16)
                v = vf[_every(at, sb + BAND, d), :].astype(BF16)
                sc = lax.dot_general(q, k, NT_DIMS, preferred_element_type=F32) * scale
                qi = lax.broadcasted_iota(jnp.int32, sc.shape, 0)
                kj = lax.broadcasted_iota(jnp.int32, sc.shape, 1)
                valid = (kj >= qi) & (kj <= qi + BAND)
                if j == 0:
                    valid = valid & ((kj >= BAND) | (n > 0))
                sc = jnp.where(valid, sc, -1e30)
                m = jnp.max(sc, axis=-1, keepdims=True)
                p = jnp.exp(sc - m)
                l = jnp.sum(p, axis=-1, keepdims=True)
                o = lax.dot_general(p.astype(BF16), v, NN_DIMS, preferred_element_type=F32)
                o_ref[_every(at, sb, d), :] = o / l
                lse_ref[_every(at, sb, d), :] = jnp.broadcast_to(m + jnp.log(l), (sb, HD))

    cur, side, tok, _ = _attn_specs(g, tb, pb, s, ahead=False)
    return pl.pallas_call(
        body, name=f"attn_fwd_g{g}", grid=(N_HEADS, s // tb),
        out_shape=[jax.ShapeDtypeStruct((s, COL), F32)] * 2,
        in_specs=[cur(Q_COL), cur(K_COL), side(K_COL), cur(V_COL), side(V_COL)],
        out_specs=[tok, tok],
        scratch_shapes=[pltpu.VMEM((tb, HD), ft), pltpu.VMEM((tb + pb, HD), ft),
                        pltpu.VMEM((tb + pb, HD), ft)],
        compiler_params=_params("parallel", "arbitrary"),
    )(qkn, qkn, qkn, proj, proj)


def _attn_combine(os_, lses, tm=512):
    s = os_[0].shape[0]

    def body(o0, o1, o2, l0, l1, l2, o_ref, lse_ref):
        a, b, c = l0[...], l1[...], l2[...]
        m = jnp.maximum(jnp.maximum(a, b), c)
        ea, eb, ec = jnp.exp(a - m), jnp.exp(b - m), jnp.exp(c - m)
        tot = ea + eb + ec
        o_ref[...] = ((ea * o0[...] + eb * o1[...] + ec * o2[...]) / tot).astype(BF16)
        lse_ref[...] = m + jnp.log(tot)

    return pl.pallas_call(
        body, name="attn_combine", grid=(s // tm,),
        out_shape=[jax.ShapeDtypeStruct((s, COL), BF16), jax.ShapeDtypeStruct((s, COL), F32)],
        in_specs=[_row(tm, COL)] * 6, out_specs=[_row(tm, COL)] * 2,
        compiler_params=_params("parallel"),
    )(*os_, *lses)


def _attn_delta(do, o, tm=512):
    s = do.shape[0]

    def body(do_ref, o_ref, del_ref):
        prod = do_ref[...] * o_ref[...].astype(F32)
        del_ref[...] = _heads(prod, lambda ph, h: jnp.broadcast_to(
            jnp.sum(ph, axis=-1, keepdims=True), ph.shape))

    return pl.pallas_call(
        body, name="attn_delta", grid=(s // tm,),
        out_shape=jax.ShapeDtypeStruct((s, COL), F32),
        in_specs=[_row(tm, COL)] * 2, out_specs=_row(tm, COL),
        compiler_params=_params("parallel"),
    )(do, o)


def _attn_bwd(g, qkn, proj, do, lse, delta, dqn, dkn, dproj):
    s = qkn.shape[0]
    d, tb, sb, pb = _attn_shapes(s, g)
    ft = F32 if d > 1 else BF16
    nj = tb // d // sb
    nt = s // tb
    scale = HD ** -0.5
    chained = dqn is not None

    def body(k_ref, v_ref, qc_ref, qn_ref, doc_ref, don_ref, lc_ref, ln_ref, dc_ref, dn_ref, *rest):
        dq_ref, dk_ref, dv_ref, kf, vf, qf, dvf, later = rest[-8:]
        n = pl.program_id(1)
        kf[...] = k_ref[...].astype(ft)
        vf[...] = v_ref[...].astype(ft)
        qf[0:tb] = qc_ref[...].astype(ft)
        qf[tb:] = qn_ref[...].astype(ft)

        @pl.when(n == 0)
        def _():
            later[...] = jnp.zeros_like(later)

        def window(c_ref, n_ref, r, j):
            at = j * sb * d + r
            if j < nj - 1:
                return c_ref[_every(at, sb + BAND, d), :]
            return jnp.concatenate([c_ref[_every(at, sb, d), :], n_ref[_every(r, BAND, d), :]], axis=0)

        for r in range(d):
            tail = later[r]
            for j in range(nj):
                at = j * sb * d + r
                rows = _every(at, sb, d)
                k = kf[rows, :].astype(BF16)
                v = vf[rows, :].astype(BF16)
                q = qf[_every(at, sb + BAND, d), :].astype(BF16)
                dov = window(doc_ref, don_ref, r, j).astype(BF16)
                sc = lax.dot_general(q, k, NT_DIMS, preferred_element_type=F32) * scale
                qi = lax.broadcasted_iota(jnp.int32, sc.shape, 0)
                kj = lax.broadcasted_iota(jnp.int32, sc.shape, 1)
                valid = (qi >= kj) & (qi <= kj + BAND)
                if j == nj - 1:
                    valid = valid & ((qi < sb) | (n < nt - 1))
                p = jnp.exp(jnp.where(valid, sc - _lanes(window(lc_ref, ln_ref, r, j), sb), -1e30))
                dp = lax.dot_general(dov, v, NT_DIMS, preferred_element_type=F32)
                ds = (p * (dp - _lanes(window(dc_ref, dn_ref, r, j), sb)) * scale).astype(BF16)
                dvf[rows, :] = lax.dot_general(p.astype(BF16), dov, TN_DIMS, preferred_element_type=F32)
                dk_ref[rows, :] = lax.dot_general(ds, q, TN_DIMS, preferred_element_type=F32)
                dqw = lax.dot_general(ds, k, NN_DIMS, preferred_element_type=F32)
                first = dqw[:BAND] + tail
                dq_ref[rows, :] = first if sb == BAND else jnp.concatenate([first, dqw[BAND:sb]], axis=0)
                tail = dqw[sb:]
            later[r] = tail
        dv_ref[...] = dvf[...].astype(BF16)

    cur, side, tok, tok_side = _attn_specs(g, tb, pb, s, ahead=True)
    anyspec = pl.BlockSpec(memory_space=pl.ANY)
    n_heads_cols = 3 * N_HEADS * HD
    return pl.pallas_call(
        body, name=f"attn_bwd_g{g}", grid=(N_HEADS, nt),
        out_shape=[jax.ShapeDtypeStruct((s, n_heads_cols), F32), jax.ShapeDtypeStruct((s, n_heads_cols), F32),
                   jax.ShapeDtypeStruct((s, IN_W), BF16)],
        in_specs=[cur(K_COL), cur(V_COL), cur(Q_COL), side(Q_COL), tok, tok_side, tok, tok_side,
                  tok, tok_side] + ([anyspec, anyspec] if chained else []) + [anyspec],
        out_specs=[cur(0), cur(0), cur(V_COL)],
        input_output_aliases={10: 0, 11: 1, 12: 2} if chained else {10: 2},
        scratch_shapes=[pltpu.VMEM((tb, HD), ft), pltpu.VMEM((tb, HD), ft),
                        pltpu.VMEM((tb + pb, HD), ft), pltpu.VMEM((tb, HD), F32),
                        pltpu.VMEM((d, BAND, HD), F32)],
        compiler_params=_params("arbitrary", "arbitrary"),
    )(qkn, proj, qkn, qkn, do, do, lse, lse, delta, delta, *([dqn, dkn] if chained else []), dproj)


def _shift_down(x, before, k):
    rolled = pltpu.roll(x, k, 0)
    head = jnp.where(lax.broadcasted_iota(jnp.int32, before.shape, 0) < k, pltpu.roll(before, k, 0), rolled[:8])
    return jnp.concatenate([head, rolled[8:]], axis=0)


def _shift_up(x, after, k):
    rows = x.shape[0]
    rolled = pltpu.roll(x, rows - k, 0)
    tail = jnp.where(lax.broadcasted_iota(jnp.int32, after.shape, 0) >= 8 - k,
                     pltpu.roll(after, 8 - k, 0), rolled[rows - 8:])
    return jnp.concatenate([rolled[:rows - 8], tail], axis=0)


def _conv_fwd(proj, cw, tm=1024):
    s = proj.shape[0]
    r16 = tm // 16

    def body(u_ref, b_ref, c_ref, up_ref, cp_ref, w_ref, z_ref):
        i = pl.program_id(1)
        xc = c_ref[...].astype(F32) * u_ref[...].astype(F32)
        xp = jnp.where(i > 0, cp_ref[8:16, :].astype(F32) * up_ref[8:16, :].astype(F32), 0.0)
        w = w_ref[...]
        conv = _shift_down(xc, xp, 2) * w[0:1] + _shift_down(xc, xp, 1) * w[1:2] + xc * w[2:3]
        z_ref[...] = (b_ref[...].astype(F32) * conv).astype(BF16)

    tile = lambda blk: pl.BlockSpec((tm, COL), lambda j, i: (i, blk + j))
    before = lambda blk: pl.BlockSpec((16, COL), lambda j, i: (jnp.maximum(i * r16 - 1, 0), blk + j))
    return pl.pallas_call(
        body, name="conv_fwd", grid=(D // COL, s // tm),
        out_shape=jax.ShapeDtypeStruct((s, D), BF16),
        in_specs=[tile(U_BLK), tile(B_BLK), tile(C_BLK), before(U_BLK), before(C_BLK),
                  pl.BlockSpec((3, COL), lambda j, i: (0, j))],
        out_specs=pl.BlockSpec((tm, COL), lambda j, i: (i, j)),
        compiler_params=_params("parallel", "parallel"),
    )(proj, proj, proj, proj, proj, cw)


def _conv_bwd(dz, proj, cw, dproj, tm=1024):
    s = proj.shape[0]
    r8, r16 = tm // 8, tm // 16
    nrow = s // tm

    def body(dz_ref, u_ref, b_ref, c_ref, up_ref, cp_ref, dzn_ref, bn_ref, w_ref, _, o_ref, acc_ref):
        piece, i = pl.program_id(1), pl.program_id(2)
        u, c = u_ref[...].astype(F32), c_ref[...].astype(F32)
        bv = b_ref[...].astype(F32)
        dzv = dz_ref[...]
        w = w_ref[...]

        @pl.when((piece == 0) & (i == 0))
        def _():
            acc_ref[...] = jnp.zeros_like(acc_ref)

        @pl.when(piece == 1)
        def _():
            xc = c * u
            xp = jnp.where(i > 0, cp_ref[8:16, :].astype(F32) * up_ref[8:16, :].astype(F32), 0.0)
            x2, x1 = _shift_down(xc, xp, 2), _shift_down(xc, xp, 1)
            o_ref[...] = (dzv * (x2 * w[0:1] + x1 * w[1:2] + xc * w[2:3])).astype(BF16)
            dconv = dzv * bv
            acc_ref[0:1, :] += jnp.sum(dconv * x2, axis=0, keepdims=True)
            acc_ref[1:2, :] += jnp.sum(dconv * x1, axis=0, keepdims=True)
            acc_ref[2:3, :] += jnp.sum(dconv * xc, axis=0, keepdims=True)

        @pl.when(piece != 1)
        def _():
            dconv = dzv * bv
            dn = jnp.where(i < nrow - 1, dzn_ref[...] * bn_ref[0:8, :].astype(F32), 0.0)
            dxc = dconv * w[2:3] + _shift_up(dconv, dn, 1) * w[1:2] + _shift_up(dconv, dn, 2) * w[0:1]
            o_ref[...] = (dxc * jnp.where(piece == 0, c, u)).astype(BF16)

    tile = lambda blk: pl.BlockSpec((tm, COL), lambda j, p, i: (i, blk + j))
    before = lambda blk: pl.BlockSpec((16, COL), lambda j, p, i: (jnp.maximum(i * r16 - 1, 0), blk + j))
    after = lambda rows, blk: pl.BlockSpec(
        (rows, COL), lambda j, p, i: (jnp.minimum((i + 1) * (tm // rows), s // rows - 1), blk + j))
    return pl.pallas_call(
        body, name="conv_bwd", grid=(D // COL, 3, nrow),
        out_shape=[jax.ShapeDtypeStruct((s, IN_W), BF16), jax.ShapeDtypeStruct((8, D), F32)],
        in_specs=[tile(0), tile(U_BLK), tile(B_BLK), tile(C_BLK), before(U_BLK), before(C_BLK),
                  after(8, 0), after(16, B_BLK), pl.BlockSpec((3, COL), lambda j, p, i: (0, j)),
                  pl.BlockSpec(memory_space=pl.ANY)],
        out_specs=[pl.BlockSpec((tm, COL), lambda j, p, i: (i, U_BLK + 2 * p + j)),
                   pl.BlockSpec((8, COL), lambda j, p, i: (0, j))],
        input_output_aliases={9: 0},
        compiler_params=_params("arbitrary", "arbitrary", "arbitrary"),
    )(dz, proj, proj, proj, proj, proj, dz, proj, cw, dproj)


def _merge_fwd(ya, yc, proj, tm=512):
    s = proj.shape[0]

    def body(ya_ref, yc_ref, ga_ref, gc_ref, o_ref):
        o_ref[...] = (_sigmoid(ga_ref[...].astype(F32)) * ya_ref[...].astype(F32)
                      + _sigmoid(gc_ref[...].astype(F32)) * yc_ref[...].astype(F32)).astype(BF16)

    tile = lambda blk: pl.BlockSpec((tm, COL), lambda j, i: (i, blk + j))
    return pl.pallas_call(
        body, name="merge_fwd", grid=(D // COL, s // tm),
        out_shape=jax.ShapeDtypeStruct((s, D), BF16),
        in_specs=[tile(0), tile(0), tile(GA_BLK), tile(GC_BLK)], out_specs=tile(0),
        compiler_params=_params("parallel", "parallel"),
    )(ya, yc, proj, proj)


def _merge_bwd_branches(dm, proj, tm=512):
    s = proj.shape[0]

    def body(dm_ref, ga_ref, gc_ref, dya_ref, dyc_ref):
        dmv = dm_ref[...]
        dya_ref[...] = (dmv * _sigmoid(ga_ref[...].astype(F32))).astype(BF16)
        dyc_ref[...] = (dmv * _sigmoid(gc_ref[...].astype(F32))).astype(BF16)

    tile = lambda blk: pl.BlockSpec((tm, COL), lambda j, i: (i, blk + j))
    return pl.pallas_call(
        body, name="merge_bwd_branches", grid=(D // COL, s // tm),
        out_shape=[jax.ShapeDtypeStruct((s, D), BF16)] * 2,
        in_specs=[tile(0), tile(GA_BLK), tile(GC_BLK)], out_specs=[tile(0)] * 2,
        compiler_params=_params("parallel", "parallel"),
    )(dm, proj, proj)


def _merge_bwd_gates(dm, ya, yc, proj, tm=1024):
    s = proj.shape[0]
    half = D // COL

    def body(dm_ref, ya_ref, yc_ref, g_ref, o_ref):
        y = jnp.where(pl.program_id(0) < half, ya_ref[...].astype(F32), yc_ref[...].astype(F32))
        sig = _sigmoid(g_ref[...].astype(F32))
        o_ref[...] = (dm_ref[...] * y * sig * (1.0 - sig)).astype(BF16)

    chan = pl.BlockSpec((tm, COL), lambda jj, i: (i, jj % half))
    gate = pl.BlockSpec((tm, COL), lambda jj, i: (i, GA_BLK + jj))
    return pl.pallas_call(
        body, name="merge_bwd_gates", grid=(2 * half, s // tm),
        out_shape=jax.ShapeDtypeStruct((s, IN_W), BF16),
        in_specs=[chan, chan, chan, gate], out_specs=gate,
        compiler_params=_params("parallel", "parallel"),
    )(dm, ya, yc, proj)


def _mod_part(c_all, w_ada, b_part):
    def body(c_ref, w_ref, b_ref, o_ref):
        cv = c_ref[...]
        act = cv * _sigmoid(cv)
        o_ref[...] = jnp.dot(act, w_ref[...], preferred_element_type=F32,
                             precision=lax.Precision.HIGHEST) + b_ref[...]

    return pl.pallas_call(
        body, name="mod_part", out_shape=jax.ShapeDtypeStruct((N_DEV, w_ada.shape[1]), F32),
    )(c_all, w_ada, b_part)


def _w_ada_grad(c_all_t, dmod_part):
    def body(c_ref, d_ref, o_ref):
        cv = c_ref[...]
        act = cv * _sigmoid(cv)
        dv = d_ref[...]
        acc = act[:, 0:1] * dv[0:1, :]
        for b in range(1, N_DEV):
            acc = acc + act[:, b:b + 1] * dv[b:b + 1, :]
        o_ref[...] = acc

    return pl.pallas_call(
        body, name="w_ada_grad", out_shape=jax.ShapeDtypeStruct((D, dmod_part.shape[1]), F32),
    )(c_all_t, dmod_part)


def _sum_rows(name, v):
    def body(v_ref, o_ref):
        acc = v_ref[0]
        for k in range(1, N_DEV):
            acc = acc + v_ref[k]
        o_ref[...] = acc

    return pl.pallas_call(body, name=name, out_shape=jax.ShapeDtypeStruct(v.shape[1:], F32))(v)


def _adamw(name, w, g, m, v):
    rows, cols = w.shape
    limit = max(16, (1 << 20) // (4 * cols))
    tr = rows if rows <= limit else next((t for t in range(limit - limit % 16, 15, -16) if rows % t == 0), rows)
    c1 = 1.0 - ADAM_B1 ** ADAM_STEP
    c2 = 1.0 - ADAM_B2 ** ADAM_STEP
    parts = g.ndim == 3

    def body(w_ref, g_ref, m_ref, v_ref, go_ref, d_ref, nm_ref, nv_ref):
        if parts:
            gv = g_ref[0].astype(F32)
            for k in range(1, N_DEV):
                gv = gv + g_ref[k].astype(F32)
        else:
            gv = g_ref[...]
        go_ref[...] = gv
        nm = ADAM_B1 * m_ref[...] + (1.0 - ADAM_B1) * gv
        nv = ADAM_B2 * v_ref[...] + (1.0 - ADAM_B2) * (gv * gv)
        nm_ref[...] = nm
        nv_ref[...] = nv
        d_ref[...] = -ADAM_LR * ((nm / c1) / (jnp.sqrt(nv / c2) + ADAM_EPS) + ADAM_WD * w_ref[...])

    spec = pl.BlockSpec((tr, cols), lambda i: (i, 0))
    g_spec = pl.BlockSpec((N_DEV, tr, cols), lambda i: (0, i, 0)) if parts else spec
    return pl.pallas_call(
        body, name=name, grid=(rows // tr,),
        out_shape=[jax.ShapeDtypeStruct((rows, cols), F32)] * 4,
        in_specs=[spec, g_spec, spec, spec], out_specs=[spec] * 4,
        compiler_params=_params("parallel"),
    )(w, g, m, v)


def _adamw_small(ws, gs, ms, vs):
    n = len(ws)
    c1 = 1.0 - ADAM_B1 ** ADAM_STEP
    c2 = 1.0 - ADAM_B2 ** ADAM_STEP

    def body(*refs):
        for i in range(n):
            w_ref, g_ref, m_ref, v_ref = refs[i], refs[n + i], refs[2 * n + i], refs[3 * n + i]
            d_ref, nm_ref, nv_ref = refs[4 * n + 3 * i:4 * n + 3 * i + 3]
            gv = g_ref[...]
            nm = ADAM_B1 * m_ref[...] + (1.0 - ADAM_B1) * gv
            nv = ADAM_B2 * v_ref[...] + (1.0 - ADAM_B2) * (gv * gv)
            nm_ref[...] = nm
            nv_ref[...] = nv
            d_ref[...] = -ADAM_LR * ((nm / c1) / (jnp.sqrt(nv / c2) + ADAM_EPS) + ADAM_WD * w_ref[...])

    outs = pl.pallas_call(
        body, name="adamw_small",
        out_shape=[jax.ShapeDtypeStruct(w.shape, F32) for w in ws for _ in range(3)],
    )(*ws, *gs, *ms, *vs)
    return [tuple(outs[3 * i:3 * i + 3]) for i in range(n)]


HALF = FF // 2


def _sds(shape, dtype):
    return jax.ShapeDtypeStruct(shape, dtype)


def _row_tile(w):
    return lambda tm: ((tm, w), lambda i, j: (i, 0))


def _one(w):
    return lambda rows: ((rows, w), lambda i, j: (0, 0))


def _gate_up_swiglu(name, h, wgu, carry=None, tm=512):
    s = h.shape[0]
    tm = min(tm, s)

    def epilogue(prod, first, tin, tout):
        ab_ref, s_ref = tout
        ab_ref[...] = prod.astype(BF16)
        a, b = prod[:, :HALF], prod[:, HALF:]
        s_ref[...] = (a * _sigmoid(a) * b).astype(BF16)

    return _mm(name, h, wgu, "NT", None, tm, FF, D, carry=carry, n_outer=True, epilogue=epilogue,
               tiles_out=[(_sds((s, 2 * FF), BF16), (tm, FF), lambda i, j: (i, j)),
                          (_sds((s, FF), BF16), (tm, HALF), lambda i, j: (i, j))])


def _d_hidden_swiglu(name, df, wd, ab, tm=512):
    s = df.shape[0]
    tm = min(tm, s)

    def epilogue(prod, first, tin, tout, cols):
        da_cols = slice(cols[0], cols[0] + cols[1])
        db_cols = slice(HALF + cols[0], HALF + cols[0] + cols[1])
        a = tin[0][:, da_cols].astype(F32)
        b = tin[0][:, db_cols].astype(F32)
        sig = _sigmoid(a)
        tout[0][:, da_cols] = (prod * b * (sig * (1.0 + a * (1.0 - sig)))).astype(BF16)
        tout[0][:, db_cols] = (prod * (a * sig)).astype(BF16)

    chunks = [(c0, min(384, HALF - c0)) for c0 in range(0, HALF, 384)]
    return _mm(name, df, wd, "NT", None, tm, HALF, D, n_outer=True, epilogue=epilogue, col_chunks=chunks,
               tiles_in=[(ab, (tm, FF), lambda i, j: (i, j))],
               tiles_out=[(_sds((s, 2 * FF), BF16), (tm, FF), lambda i, j: (i, j))])[0]


def _out_residual(name, a, w, x, gt, coef, nxt, tm=512, tk=FF):
    s = a.shape[0]
    tm = min(tm, s)

    def epilogue(prod, first, tin, tout):
        x_ref, gt_ref, g_ref, sc_ref, sh_ref = tin
        f_ref, xn_ref, h_ref = tout
        f_ref[...] = prod
        xn = x_ref[...] + (coef * gt_ref[...]) * prod
        xn_ref[...] = xn
        r = lax.rsqrt(jnp.mean(xn * xn, axis=-1, keepdims=True) + EPS)
        h_ref[...] = ((xn * r) * g_ref[...] * (1.0 + sc_ref[...]) + sh_ref[...]).astype(BF16)

    row, vec = _row_tile(D)(tm), _one(D)(1)
    return _mm(name, a, w, "NN", None, tm, D, tk, epilogue=epilogue,
               tiles_in=[(x, *row), (gt, *vec)] + [(v, *vec) for v in nxt],
               tiles_out=[(_sds((s, D), F32), *row), (_sds((s, D), F32), *row), (_sds((s, D), BF16), *row)])


def _out_loss(name, a, w, x, gt, coef, target, tm=512):
    s = a.shape[0]
    tm = min(tm, s)

    def epilogue(prod, first, tin, tout):
        x_ref, gt_ref, t_ref = tin
        f_ref, g_ref, df_ref, acc_ref = tout
        f_ref[...] = prod
        cg = coef * gt_ref[...]
        e = x_ref[...] + cg * prod - t_ref[...]
        gv = e * (1.0 / D)
        g_ref[...] = gv
        df_ref[...] = (cg * gv).astype(BF16)

        @pl.when(first)
        def _():
            acc_ref[...] = jnp.zeros_like(acc_ref)

        acc_ref[0:1, :] += coef * jnp.sum(gv * prod, axis=0, keepdims=True)
        acc_ref[1:2, :] += (0.5 / D) * jnp.sum(e * e, axis=0, keepdims=True)

    row, vec = _row_tile(D)(tm), _one(D)(1)
    return _mm(name, a, w, "NN", None, tm, D, FF, epilogue=epilogue,
               tiles_in=[(x, *row), (gt, *vec), (target, *row)],
               tiles_out=[(_sds((s, D), F32), *row), (_sds((s, D), F32), *row), (_sds((s, D), BF16), *row),
                          (_sds((8, D), F32), *_one(D)(8))])


def _d_h_norm_bwd(name, da, w, x, gin, g, sc, sh, before=None, carry=None, tm=256):
    s = da.shape[0]
    tm = min(tm, s)
    coef = before[2] if before else None

    def epilogue(prod, first, tin, tout):
        x_ref, gin_ref, g_ref, sc_ref, sh_ref = tin[:5]
        gout_ref, acc_ref = tout[:2]
        xv = x_ref[...]
        r = lax.rsqrt(jnp.mean(xv * xv, axis=-1, keepdims=True) + EPS)
        nv = xv * r
        gv, one_sc = g_ref[...], 1.0 + sc_ref[...]
        dn = prod * gv * one_sc
        gout = gin_ref[...] + r * (dn - nv * jnp.mean(dn * nv, axis=-1, keepdims=True))
        gout_ref[...] = gout

        @pl.when(first)
        def _():
            acc_ref[...] = jnp.zeros_like(acc_ref)

        dhn = prod * nv
        acc_ref[0:1, :] += jnp.sum(prod, axis=0, keepdims=True)
        acc_ref[1:2, :] += jnp.sum(dhn * gv, axis=0, keepdims=True)
        acc_ref[2:3, :] += jnp.sum(dhn * one_sc, axis=0, keepdims=True)
        if before:
            f_ref, gt_ref = tin[5:]
            tout[2][...] = ((coef * gt_ref[...]) * gout).astype(BF16)
            acc_ref[3:4, :] += coef * jnp.sum(gout * f_ref[...], axis=0, keepdims=True)

    row, vec = _row_tile(D)(tm), _one(D)(1)
    tiles_in = [(x, *row), (gin, *row), (g, *vec), (sc, *vec), (sh, *vec)]
    tiles_out = [(_sds((s, D), F32), *row), (_sds((8, D), F32), *_one(D)(8))]
    if before:
        tiles_in += [(before[0], *row), (before[1], *vec)]
        tiles_out.append((_sds((s, D), BF16), *row))
    return _mm(name, da, w, "NN", None, tm, D, da.shape[1], epilogue=epilogue, carry=carry, keep_b=True,
               tiles_in=tiles_in, tiles_out=tiles_out)


def _gate_tiles(proj, tm):
    return [(proj, (tm, COL), (lambda i, j, blk=blk: (i, blk))) for blk in (GA_BLK, GA_BLK + 1, GC_BLK, GC_BLK + 1)]


def _conv_branch_merge(z, wc, ya, proj, tm=512):
    s = z.shape[0]
    tm = min(tm, s)

    def epilogue(prod, first, tin, tout):
        ya_ref, ga0, ga1, gc0, gc1 = tin
        tout[0][...] = prod.astype(BF16)
        for half, (ga, gc) in enumerate(((ga0, gc0), (ga1, gc1))):
            cols = slice(half * COL, (half + 1) * COL)
            tout[1][:, cols] = (_sigmoid(ga[...].astype(F32)) * ya_ref[:, cols].astype(F32)
                                + _sigmoid(gc[...].astype(F32)) * prod[:, cols]).astype(BF16)

    row = _row_tile(D)(tm)
    return _mm("mix_conv_branch", z, wc, "NN", None, tm, D, D, epilogue=epilogue,
               tiles_in=[(ya, *row)] + _gate_tiles(proj, tm),
               tiles_out=[(_sds((s, D), BF16), *row), (_sds((s, D), BF16), *row)])


def _d_merged_branches(dmix, wo, proj, tm=512):
    s = dmix.shape[0]
    tm = min(tm, s)

    def epilogue(prod, first, tin, tout):
        ga0, ga1, gc0, gc1 = tin
        tout[0][...] = prod
        for half, (ga, gc) in enumerate(((ga0, gc0), (ga1, gc1))):
            cols = slice(half * COL, (half + 1) * COL)
            tout[1][:, cols] = (prod[:, cols] * _sigmoid(ga[...].astype(F32))).astype(BF16)
            tout[2][:, cols] = (prod[:, cols] * _sigmoid(gc[...].astype(F32))).astype(BF16)

    row = _row_tile(D)(tm)
    return _mm("mix_d_merged", dmix, wo, "NT", None, tm, D, D, epilogue=epilogue,
               tiles_in=_gate_tiles(proj, tm),
               tiles_out=[(_sds((s, D), F32), *row), (_sds((s, D), BF16), *row), (_sds((s, D), BF16), *row)])


def _d_o_delta(dya, wa_t, o, tm=1024):
    s = dya.shape[0]
    tm = min(tm, s)

    def epilogue(prod, first, tin, tout):
        tout[0][...] = prod
        tout[1][...] = _heads(prod * tin[0][...].astype(F32), lambda ph, h: jnp.broadcast_to(
            jnp.sum(ph, axis=-1, keepdims=True), ph.shape))

    row = _row_tile(COL)(tm)
    return _mm("mix_d_o", dya, wa_t, "NN", None, tm, COL, D, epilogue=epilogue,
               tiles_in=[(o, *row)], tiles_out=[(_sds((s, COL), F32), *row), (_sds((s, COL), F32), *row)])


def _ffn_bwd(tag, df, x, gin, h, ab, sw, g, sc, sh, wgu, wd, before=None, carry_down=None, carry_gate_up=None,
             tk_dw=2048):
    dab = _d_hidden_swiglu(f"{tag}_d_hidden", df, wd, ab)
    dwd = _mm(f"{tag}_dw_down", sw, df, "TN", BF16, HALF, D, tk_dw)
    carried = []
    if carry_down:
        dwgu, *got = _mm(f"{tag}_dw_gate_up", dab, h, "TN", BF16, HALF, D, tk_dw, carry=carry_down(dwd))
        carried += got
    else:
        dwgu = _mm(f"{tag}_dw_gate_up", dab, h, "TN", BF16, HALF, D, tk_dw)
    res = _d_h_norm_bwd(f"{tag}_d_h", dab, wgu, x, gin, g, sc, sh, before=before,
                        carry=carry_gate_up(dwgu) if carry_gate_up else None)
    n_own = 3 if before else 2
    return res[:n_own], dwgu, dwd, carried + list(res[n_own:])


def kernel(x, c, w_ada, b_ada, norm_ffn1, ffn1_w_gate, ffn1_w_up, ffn1_w_down, norm_mix, w_in, q_norm, k_norm, conv_w, w_attn_branch, w_conv_branch, w_out, norm_ffn2, ffn2_w_gate, ffn2_w_up, ffn2_w_down, loss_target, m_w_ada, m_b_ada, m_norm_ffn1, m_ffn1_w_gate, m_ffn1_w_up, m_ffn1_w_down, m_norm_mix, m_w_in, m_q_norm, m_k_norm, m_conv_w, m_w_attn_branch, m_w_conv_branch, m_w_out, m_norm_ffn2, m_ffn2_w_gate, m_ffn2_w_up, m_ffn2_w_down, v_w_ada, v_b_ada, v_norm_ffn1, v_ffn1_w_gate, v_ffn1_w_up, v_ffn1_w_down, v_norm_mix, v_w_in, v_q_norm, v_k_norm, v_conv_w, v_w_attn_branch, v_w_conv_branch, v_w_out, v_norm_ffn2, v_ffn2_w_gate, v_ffn2_w_up, v_ffn2_w_down):
    me = 4 * lax.axis_index("x") + 2 * lax.axis_index("y") + lax.axis_index("c")
    x0, target = x[0], loss_target[0]
    s = x0.shape[0]
    ada_cols = w_ada.shape[2]
    cw_cols = conv_w.shape[2]

    gathered = _small_allgather(
        "gather_c_conv", jnp.concatenate([c, conv_w[0].reshape(1, 3 * cw_cols)], axis=1))[:, 0]
    c_all = gathered[:, :D]
    cw = gathered[:, D:].reshape(N_DEV, 3, cw_cols).transpose(1, 0, 2).reshape(3, D)
    b_part = lax.dynamic_slice(b_ada, (0, me * ada_cols), (1, ada_cols))
    mod_part = _mod_part(c_all, w_ada[0], b_part)
    mod_all = _small_allgather("gather_mod", mod_part.reshape(1, N_DEV * ada_cols))
    mod = lax.dynamic_slice(mod_all.reshape(N_DEV, N_DEV, ada_cols), (0, me, 0), (N_DEV, 1, ada_cols))
    mod = mod.reshape(N_MOD, 1, D)
    sh1, sc1, gt1, sh2, sc2, gt2, sh3, sc3, gt3 = [mod[i] for i in range(N_MOD)]

    tb = lambda w: w[0].T.astype(BF16)
    nb = lambda w: w[0].astype(BF16)
    ffn1_shards = [tb(ffn1_w_gate), tb(ffn1_w_up), nb(ffn1_w_down)]
    ffn2_shards = [tb(ffn2_w_gate), tb(ffn2_w_up), nb(ffn2_w_down)]
    mix_shards = [tb(w_in), tb(w_attn_branch), nb(w_conv_branch), nb(w_out)]
    ffn_dst, ffn_base, ffn_jump, ffn_shapes = [0, 0, 1], [0, HALF, 0], [HALF, HALF, 0], [(2 * FF, D), (FF, D)]
    mix_dst, mix_base, mix_shapes = [0, 1, 2, 3], [0, 0, 0, 0], [(IN_W, D), (D, COL), (D, D), (D, D)]
    wgu1, wd1 = _run_plan("gather_ffn1_weights",
                          _gather_plan(ffn1_shards, ffn_dst, ffn_base, ffn_shapes, ffn_jump))

    h1 = _normmod("ffn1_normmod", x0, norm_ffn1, sc1, sh1)
    ab1, s1, win_t = _gate_up_swiglu(
        "ffn1_gate_up", h1, wgu1, carry=_gather_plan(mix_shards[:1], mix_dst[:1], mix_base[:1], mix_shapes[:1]))
    f1, x1, h2 = _out_residual("ffn1_down", s1, wd1, x0, gt1, 0.5, (norm_mix, sc2, sh2))
    proj, wgu2, wd2, wa_t, wc, wo = _mm(
        "mix_in_proj", h2, win_t, "NT", BF16, 1024, IN_W // 4, D, n_outer=True,
        carry=_gather_plan(ffn2_shards + mix_shards[1:], ffn_dst + [2, 3, 4], ffn_base + [0, 0, 0],
                           ffn_shapes + mix_shapes[1:], ffn_jump + [0, 0, 0]))
    wqk = jnp.concatenate([jnp.tile(q_norm, (1, 12)), jnp.tile(k_norm, (1, 12))], axis=1)
    qkn = _qknorm(proj, wqk)
    group_out = [_attn_fwd(g, qkn, proj) for g in range(3)]
    o, lse = _attn_combine([go[0] for go in group_out], [go[1] for go in group_out])
    ya = _mm("mix_attn_branch", o, wa_t, "NT", BF16, 1024, 1024, COL)
    z = _conv_fwd(proj, cw)
    yc, merged = _conv_branch_merge(z, wc, ya, proj)
    mix, x2, h3 = _out_residual("mix_out_proj", merged, wo, x1, gt2, 1.0, (norm_ffn2, sc3, sh3), tk=D)
    ab3, s3 = _gate_up_swiglu("ffn2_gate_up", h3, wgu2)
    f3, g3, df3, acc_out = _out_loss("ffn2_down", s3, wd2, x2, gt3, 0.5, target)
    loss = lax.psum(jnp.sum(acc_out[1]), ("x", "y", "c"))

    ffn_rows = [sh_.shape[0] for sh_ in ffn1_shards]
    mix_rows = [sh_.shape[0] for sh_ in mix_shards]
    (g2, acc3, dmix), dwgu2, dwd2, _ = _ffn_bwd(
        "ffn2", df3, x2, g3, h3, ab3, s3, norm_ffn2, sc3, sh3, wgu2, wd2, before=(mix, gt2, 1.0))
    dmerged, dya, dyc = _d_merged_branches(dmix, wo, proj)
    dwo = _mm("mix_dw_out", merged, dmix, "TN", BF16, 1024, 1024, 2048)
    dproj = _merge_bwd_gates(dmerged, ya, yc, proj)
    dwc = _mm("mix_dw_conv_branch", z, dyc, "TN", BF16, 1024, 1024, 2048)
    dz = _mm("mix_d_z", dyc, wc, "NT", F32, 1024, 1024, D)
    dproj, cw_acc = _conv_bwd(dz, proj, cw, dproj)
    dwa_t = _mm("mix_dw_attn_branch", dya, o, "TN", BF16, 1024, COL, 2048)
    do, delta = _d_o_delta(dya, wa_t, o)
    dqn = dkn = None
    for g in range(3):
        dqn, dkn, dproj = _attn_bwd(g, qkn, proj, do, lse, delta, dqn, dkn, dproj)
    dproj, wq_acc = _qknorm_bwd("qnorm_bwd", proj, dqn, wqk[:, :QKW // 2], dproj, 0)
    dproj, wk_acc = _qknorm_bwd("knorm_bwd", proj, dkn, wqk[:, QKW // 2:], dproj, QKW // 2 // COL)
    dwin_t, r_f2g, r_f2u, r_f2d, r_wa, r_wc, r_wo = _mm(
        "mix_dw_in", dproj, h2, "TN", BF16, IN_W // 4, COL, 2048,
        carry=_scatter_plan([dwgu2, dwd2, dwa_t, dwc, dwo], [0, 0, 1, 2, 3, 4], [0, HALF, 0, 0, 0, 0],
                            ffn_rows + mix_rows[1:], [D, D, D, COL, D, D], [HALF, HALF, 0, 0, 0, 0]))
    g1, acc2, df1, r_win = _d_h_norm_bwd(
        "mix_d_h", dproj, win_t, x1, g2, norm_mix, sc2, sh2, before=(f1, gt1, 0.5),
        carry=_scatter_plan([dwin_t], [0], [0], mix_rows[:1], [D]))
    (g0, acc1), dwgu1, dwd1, (r_f1d, r_f1g, r_f1u) = _ffn_bwd(
        "ffn1", df1, x0, g1, h1, ab1, s1, norm_ffn1, sc1, sh1, wgu1, wd1,
        carry_down=lambda dwd: _scatter_plan([dwd], [0], [0], ffn_rows[2:], [D]),
        carry_gate_up=lambda dwgu: _scatter_plan([dwgu], [0, 0], [0, HALF], ffn_rows[:2], [D, D], [HALF, HALF]))

    dqw = jnp.sum(wq_acc[0].reshape(12, HD), axis=0)
    dkw = jnp.sum(wk_acc[0].reshape(12, HD), axis=0)
    small = jnp.concatenate([
        acc1[0], acc1[1], acc2[3], acc2[0], acc2[1], acc3[3], acc3[0], acc3[1], acc_out[0],
        acc1[2], acc2[2], acc3[2], dqw, dkw, cw_acc[0:3].reshape(3 * D)]).reshape(1, -1)
    small_all = _small_allgather("gather_small_grads", small)
    small_sum = _sum_rows("sum_small_grads", small_all)[0]
    n_mod = N_MOD * D
    g_b_ada = small_sum[:n_mod].reshape(1, n_mod)
    g_norm1, g_norm2, g_norm3 = [small_sum[n_mod + i * D:n_mod + (i + 1) * D].reshape(1, D) for i in range(3)]
    off = n_mod + 3 * D
    g_qn, g_kn = small_sum[off:off + HD].reshape(1, HD), small_sum[off + HD:off + 2 * HD].reshape(1, HD)
    g_cw_full = small_sum[off + 2 * HD:].reshape(3, D)
    g_cw = lax.dynamic_slice(g_cw_full, (0, me * cw_cols), (3, cw_cols))
    dmod_part = lax.dynamic_slice(small_all[:, 0, :n_mod], (0, me * ada_cols), (N_DEV, ada_cols))
    g_w_ada = _w_ada_grad(c_all.T, dmod_part)

    as_rows = {"ffn1_w_gate", "ffn1_w_up", "w_in", "w_attn_branch", "ffn2_w_gate", "ffn2_w_up"}
    grad_list = [g_w_ada, g_b_ada, g_norm1, r_f1g, r_f1u, r_f1d, g_norm2, r_win,
                 g_qn, g_kn, g_cw, r_wa, r_wc, r_wo, g_norm3, r_f2g, r_f2u, r_f2d]
    weights = [w_ada, b_ada, norm_ffn1, ffn1_w_gate, ffn1_w_up, ffn1_w_down, norm_mix, w_in, q_norm, k_norm,
               conv_w, w_attn_branch, w_conv_branch, w_out, norm_ffn2, ffn2_w_gate, ffn2_w_up, ffn2_w_down]
    ms = [m_w_ada, m_b_ada, m_norm_ffn1, m_ffn1_w_gate, m_ffn1_w_up, m_ffn1_w_down, m_norm_mix, m_w_in, m_q_norm,
          m_k_norm, m_conv_w, m_w_attn_branch, m_w_conv_branch, m_w_out, m_norm_ffn2, m_ffn2_w_gate,
          m_ffn2_w_up, m_ffn2_w_down]
    vs = [v_w_ada, v_b_ada, v_norm_ffn1, v_ffn1_w_gate, v_ffn1_w_up, v_ffn1_w_down, v_norm_mix, v_w_in, v_q_norm,
          v_k_norm, v_conv_w, v_w_attn_branch, v_w_conv_branch, v_w_out, v_norm_ffn2, v_ffn2_w_gate,
          v_ffn2_w_up, v_ffn2_w_down]
    wnames = ["w_ada", "b_ada", "norm_ffn1", "ffn1_w_gate", "ffn1_w_up", "ffn1_w_down", "norm_mix", "w_in",
              "q_norm", "k_norm", "conv_w", "w_attn_branch", "w_conv_branch", "w_out", "norm_ffn2",
              "ffn2_w_gate", "ffn2_w_up", "ffn2_w_down"]
    small = [i for i, gr in enumerate(grad_list) if gr.ndim == 2 and gr.size <= 16384]
    flat = lambda a, i: a.reshape(-1, weights[i].shape[-1])
    small_res = dict(zip(small, _adamw_small(
        [flat(weights[i], i) for i in small], [flat(grad_list[i], i) for i in small],
        [flat(ms[i], i) for i in small], [flat(vs[i], i) for i in small])))
    grad_out, deltas, new_ms, new_vs = [], [], [], []
    for idx, (nm, w, gr, m_, v_) in enumerate(zip(wnames, weights, grad_list, ms, vs)):
        if idx in small_res:
            gr, dl, nm_, nv_ = [r.reshape(w.shape) for r in (gr, *small_res[idx])]
        elif nm in as_rows:
            res = _adamw(f"adamw_{nm}", w[0].T, gr, m_[0].T, v_[0].T)
            gr, dl, nm_, nv_ = [r.T[None] for r in res]
        else:
            two_d = (-1, w.shape[-1])
            res = _adamw(f"adamw_{nm}", w.reshape(two_d), gr if gr.ndim == 3 else gr.reshape(two_d),
                         m_.reshape(two_d), v_.reshape(two_d))
            gr, dl, nm_, nv_ = [r.reshape(w.shape) for r in res]
        grad_out.append(gr)
        deltas.append(dl)
        new_ms.append(nm_)
        new_vs.append(nv_)
    return (loss, g0[None], *grad_out, *deltas, *new_ms, *new_vs)
```

```python
import functools

import jax
import jax.numpy as jnp
from jax import lax
from jax.experimental import pallas as pl
from jax.experimental.pallas import tpu as pltpu

F32 = jnp.float32
BF16 = jnp.bfloat16
MESH = pl.DeviceIdType.MESH

N_DEV = 8
D = 1024
FF = 2816
HD = 128
N_HEADS = 4
DILATIONS = (1, 4, 16)
BAND = 128
QKW = 2 * 3 * N_HEADS * HD
IN_W = 9728
COL = 512
V_BLK, U_BLK, B_BLK, C_BLK, GA_BLK, GC_BLK = 6, 9, 11, 13, 15, 17
EPS = 1e-6
N_MOD = 9
ADAM_LR, ADAM_B1, ADAM_B2, ADAM_EPS, ADAM_WD, ADAM_STEP = 0.001, 0.9, 0.999, 1e-08, 0.01, 10

NT_DIMS = (((1,), (1,)), ((), ()))
TN_DIMS = (((0,), (0,)), ((), ()))
NN_DIMS = (((1,), (0,)), ((), ()))


def _place():
    return lax.axis_index("x"), lax.axis_index("y"), lax.axis_index("c")


def _flip(coord, bit):
    return 1 - coord if bit else coord


def _params(*sem):
    return pltpu.CompilerParams(dimension_semantics=sem)


def _small_allgather(name, v):
    n = v.shape[-1]

    def body(v_ref, out_ref, send_sems, recv_sems):
        x, y, c = _place()
        me = 4 * x + 2 * y + c
        out_ref[me] = v_ref[...]
        copies = []
        for k in range(1, N_DEV):
            peer = (_flip(x, (k >> 2) & 1), _flip(y, (k >> 1) & 1), _flip(c, k & 1))
            cp = pltpu.make_async_remote_copy(
                src_ref=v_ref, dst_ref=out_ref.at[me], send_sem=send_sems.at[k - 1],
                recv_sem=recv_sems.at[k - 1], device_id=peer, device_id_type=MESH)
            cp.start()
            copies.append(cp)
        for cp in copies:
            cp.wait()

    return pl.pallas_call(
        body, name=name,
        out_shape=jax.ShapeDtypeStruct((N_DEV, 1, n), F32),
        in_specs=[pl.BlockSpec(memory_space=pltpu.VMEM)],
        out_specs=pl.BlockSpec(memory_space=pltpu.VMEM),
        scratch_shapes=[pltpu.SemaphoreType.DMA((N_DEV - 1,)), pltpu.SemaphoreType.DMA((N_DEV - 1,))],
    )(v)


class _Plan:
    def __init__(self, operands, out_shapes, sems, phases):
        self.operands, self.out_shapes, self.sems, self.phases = operands, out_shapes, sems, phases


def _slab_start(base, rows, jump, idx):
    return pl.multiple_of(base + idx * rows + (idx // 4) * jump, 16)


def _gather_plan(shards, dst_of, base_of, dst_shapes, jump_of=None):
    n = len(shards)
    rows = [s.shape[0] for s in shards]
    jump_of = jump_of or [0] * n

    def phases(srcs, dsts, sems):
        send_sems, recv_sems, local_sems = sems
        x, y, c = _place()
        me, sibling = (x, y, c), (x, y, 1 - c)
        chips = [(1 - x, y), (x, 1 - y), (1 - x, 1 - y)]

        def slab(i, px, py, pc):
            start = _slab_start(base_of[i], rows[i], jump_of[i], 4 * px + 2 * py + pc)
            return dsts[dst_of[i]].at[pl.ds(start, rows[i])]

        def copy(i, k, block, to, src=None):
            return pltpu.make_async_remote_copy(
                src_ref=slab(i, *block) if src is None else src, dst_ref=slab(i, *block),
                send_sem=send_sems.at[i, k], recv_sem=recv_sems.at[i, k],
                device_id=to, device_id_type=MESH)

        def mine():
            return [pltpu.make_async_copy(srcs[i], slab(i, *me), local_sems.at[i]) for i in range(n)]

        def first():
            out = []
            for i in range(n):
                out.append(copy(i, 0, me, sibling, src=srcs[i]))
                out += [copy(i, 1 + j, me, (*chip, c), src=srcs[i]) for j, chip in enumerate(chips)]
            return out

        def passed():
            return [(copy(i, 1 + j, (*chip, c), me), copy(i, 4 + j, (*chip, c), sibling))
                    for j, chip in enumerate(chips) for i in range(n)]

        def start():
            for cp in mine() + first():
                cp.start()

        def middle():
            for landed, onward in passed():
                landed.wait_recv()
                onward.start()

        def finish():
            for i in range(n):
                copy(i, 0, sibling, me).wait_recv()
                for j, chip in enumerate(chips):
                    copy(i, 4 + j, (*chip, 1 - c), me).wait_recv()
            for cp in first() + [onward for _, onward in passed()]:
                cp.wait_send()
            for cp in mine():
                cp.wait()

        return start, middle, finish

    sems = [pltpu.SemaphoreType.DMA((n, 7)), pltpu.SemaphoreType.DMA((n, 7)), pltpu.SemaphoreType.DMA((n,))]
    return _Plan(list(shards), [jax.ShapeDtypeStruct(s, BF16) for s in dst_shapes], sems, phases)


def _scatter_plan(grads, src_of, base_of, rows, cols, jump_of=None):
    n = len(rows)
    jump_of = jump_of or [0] * n

    def phases(srcs, recvs, sems):
        send_sems, recv_sems, local_sems = sems
        x, y, c = _place()
        me = 4 * x + 2 * y + c

        def slab(i, idx):
            start = _slab_start(base_of[i], rows[i], jump_of[i], idx)
            return srcs[src_of[i]].at[pl.ds(start, rows[i])]

        def copies():
            out = [pltpu.make_async_copy(slab(i, me), recvs[i].at[me], local_sems.at[i]) for i in range(n)]
            for k in range(1, N_DEV):
                px, py, pc = _flip(x, (k >> 2) & 1), _flip(y, (k >> 1) & 1), _flip(c, k & 1)
                out += [pltpu.make_async_remote_copy(
                    src_ref=slab(i, 4 * px + 2 * py + pc), dst_ref=recvs[i].at[me],
                    send_sem=send_sems.at[i, k - 1], recv_sem=recv_sems.at[i, k - 1],
                    device_id=(px, py, pc), device_id_type=MESH) for i in range(n)]
            return out

        def start():
            for cp in copies():
                cp.start()

        def finish():
            for cp in copies():
                cp.wait()

        return start, None, finish

    sems = [pltpu.SemaphoreType.DMA((n, 7)), pltpu.SemaphoreType.DMA((n, 7)), pltpu.SemaphoreType.DMA((n,))]
    out_shapes = [jax.ShapeDtypeStruct((N_DEV, rows[i], cols[i]), BF16) for i in range(n)]
    return _Plan(list(grads), out_shapes, sems, phases)


def _run_plan(name, plan):
    n_in, n_out = len(plan.operands), len(plan.out_shapes)

    def body(*refs):
        for phase in plan.phases(refs[:n_in], refs[n_in:n_in + n_out], refs[n_in + n_out:]):
            if phase is not None:
                phase()

    hbm = pl.BlockSpec(memory_space=pltpu.HBM)
    return pl.pallas_call(
        body, name=name, out_shape=plan.out_shapes,
        in_specs=[hbm] * n_in, out_specs=[hbm] * n_out, scratch_shapes=plan.sems,
    )(*plan.operands)


def _sum_contributions(name, recv):
    _, rows, cols = recv.shape
    tr = rows if rows <= 512 else 304 if rows % 304 == 0 else 256

    def body(r_ref, o_ref):
        acc = r_ref[0].astype(F32)
        for k in range(1, N_DEV):
            acc = acc + r_ref[k].astype(F32)
        o_ref[...] = acc

    return pl.pallas_call(
        body, name=name, grid=(rows // tr,),
        out_shape=jax.ShapeDtypeStruct((rows, cols), F32),
        in_specs=[pl.BlockSpec((N_DEV, tr, cols), lambda i: (0, i, 0))],
        out_specs=pl.BlockSpec((tr, cols), lambda i: (i, 0)),
        compiler_params=_params("parallel"),
    )(recv)


def _mm(name, a, b, mode, out_dtype, tm, tn, tk, *, carry=None, tiles_in=(), tiles_out=(), epilogue=None,
        n_outer=False, keep_b=False, col_chunks=None):
    if mode == "TN":
        kk, m = a.shape
    else:
        m, kk = a.shape
    n = b.shape[0] if mode == "NT" else b.shape[1]
    tm, tn, tk = min(tm, m), min(tn, n), min(tk, kk)
    assert m % tm == 0 and n % tn == 0 and kk % tk == 0, (name, m, n, kk, tm, tn, tk)
    ni, nj, nk = m // tm, n // tn, kk // tk
    steps = ni * nj * nk
    dims = {"NN": NN_DIMS, "NT": NT_DIMS, "TN": TN_DIMS}[mode]
    if epilogue is None:
        tiles_out = [(jax.ShapeDtypeStruct((m, n), out_dtype), (tm, tn), lambda i, j: (i, j))]
    n_tin, n_tout = len(tiles_in), len(tiles_out)
    n_in = len(carry.operands) if carry else 0
    n_out = len(carry.out_shapes) if carry else 0
    n_acc = 1 if nk > 1 else 0
    n_keep = 2 if keep_b else 0
    assert not carry or steps >= 3
    assert not keep_b or (nk == 1 and nj == 1)
    assert not col_chunks or (epilogue is not None and nk == 1 and mode != "TN")
    ij = (lambda p, q: (q, p)) if n_outer else (lambda p, q: (p, q))
    inner = ni if n_outer else nj

    def body(a_ref, b_ref, *rest):
        tin = rest[:n_tin]
        cin = rest[n_tin:n_tin + n_in]
        tout = rest[n_tin + n_in:n_tin + n_in + n_tout]
        cout = rest[n_tin + n_in + n_tout:n_tin + n_in + n_tout + n_out]
        scratch = rest[n_tin + n_in + n_tout + n_out:]
        k = pl.program_id(2)
        visit = pl.program_id(0) * inner + pl.program_id(1)
        step = visit * nk + k
        if keep_b:
            b_kept, b_sem = scratch[n_acc:n_acc + 2]

            @pl.when(step == 0)
            def _():
                cp = pltpu.make_async_copy(b_ref, b_kept, b_sem)
                cp.start()
                cp.wait()

            b_ref = b_kept
        if carry:
            start, middle, finish = carry.phases(cin, cout, scratch[n_acc + n_keep:])
            pl.when(step == 0)(start)

        def store(prod, c=0, cols=()):
            if epilogue is None:
                tout[0][...] = prod.astype(out_dtype)
            else:
                epilogue(prod, jnp.logical_and(visit == 0, c == 0), tin, tout, *cols)

        if col_chunks:
            for c, (c0, cw) in enumerate(col_chunks):
                b_part = b_ref[pl.ds(c0, cw), :] if mode == "NT" else b_ref[:, pl.ds(c0, cw)]
                store(lax.dot_general(a_ref[...], b_part, dims, preferred_element_type=F32), c, ((c0, cw),))
        else:
            part = lax.dot_general(a_ref[...], b_ref[...], dims, preferred_element_type=F32)
            if nk == 1:
                store(part)
            else:
                acc_ref = scratch[0]

                @pl.when(k == 0)
                def _():
                    acc_ref[...] = part

                @pl.when((k > 0) & (k < nk - 1))
                def _():
                    acc_ref[...] += part

                @pl.when(k == nk - 1)
                def _():
                    store(acc_ref[...] + part)

        if carry:
            if middle is not None:
                pl.when(step == (steps * 3) // 5)(middle)
            pl.when(step == steps - 1)(finish)

    def spec(shape, fn):
        return pl.BlockSpec(shape, lambda p, q, k: fn(*ij(p, q)))

    a_spec = (pl.BlockSpec((tk, tm), lambda p, q, k: (k, ij(p, q)[0])) if mode == "TN"
              else pl.BlockSpec((tm, tk), lambda p, q, k: (ij(p, q)[0], k)))
    if keep_b:
        b_spec = pl.BlockSpec(memory_space=pl.ANY)
    elif mode == "NT":
        b_spec = pl.BlockSpec((tn, tk), lambda p, q, k: (ij(p, q)[1], k))
    else:
        b_spec = pl.BlockSpec((tk, tn), lambda p, q, k: (k, ij(p, q)[1]))
    hbm = pl.BlockSpec(memory_space=pltpu.HBM)
    sequential = carry or epilogue or keep_b
    out = pl.pallas_call(
        body, name=name, grid=(nj, ni, nk) if n_outer else (ni, nj, nk),
        out_shape=[t[0] for t in tiles_out] + (carry.out_shapes if carry else []),
        in_specs=[a_spec, b_spec] + [spec(t[1], t[2]) for t in tiles_in] + [hbm] * n_in,
        out_specs=[spec(t[1], t[2]) for t in tiles_out] + [hbm] * n_out,
        scratch_shapes=([pltpu.VMEM((tm, tn), F32)] * n_acc
                        + ([pltpu.VMEM(b.shape, b.dtype), pltpu.SemaphoreType.DMA] if keep_b else [])
                        + (carry.sems if carry else [])),
        compiler_params=(_params("arbitrary", "arbitrary", "arbitrary") if sequential
                         else _params("parallel", "parallel", "arbitrary")),
    )(a, b, *[t[0] for t in tiles_in], *(carry.operands if carry else []))
    return out if (carry or epilogue) else out[0]


def _row(tm, w, off=0):
    return pl.BlockSpec((tm, w), lambda i: (i, off))


def _vec(w):
    return pl.BlockSpec((1, w), lambda i: (0, 0))


def _sigmoid(x):
    return 0.5 * jnp.tanh(0.5 * x) + 0.5


def _normmod(name, x, g, sc, sh, tm=512):
    s = x.shape[0]

    def body(x_ref, g_ref, sc_ref, sh_ref, h_ref):
        xv = x_ref[...]
        r = lax.rsqrt(jnp.mean(xv * xv, axis=-1, keepdims=True) + EPS)
        h_ref[...] = ((xv * r) * g_ref[...] * (1.0 + sc_ref[...]) + sh_ref[...]).astype(BF16)

    return pl.pallas_call(
        body, name=name, grid=(s // tm,),
        out_shape=jax.ShapeDtypeStruct((s, D), BF16),
        in_specs=[_row(tm, D), _vec(D), _vec(D), _vec(D)], out_specs=_row(tm, D),
        compiler_params=_params("parallel"),
    )(x, g, sc, sh)


def _normmod_bwd(name, dh, x, gin, g, sc, sh, tm=512):
    s = x.shape[0]

    def body(dh_ref, x_ref, gin_ref, g_ref, sc_ref, sh_ref, gout_ref, acc_ref):
        xv, dhv = x_ref[...], dh_ref[...]
        r = lax.rsqrt(jnp.mean(xv * xv, axis=-1, keepdims=True) + EPS)
        nv = xv * r
        gv, one_sc = g_ref[...], 1.0 + sc_ref[...]
        dn = dhv * gv * one_sc
        dx = r * (dn - nv * jnp.mean(dn * nv, axis=-1, keepdims=True))
        gout_ref[...] = gin_ref[...] + dx

        @pl.when(pl.program_id(0) == 0)
        def _():
            acc_ref[...] = jnp.zeros_like(acc_ref)

        dhn = dhv * nv
        acc_ref[0:1, :] += jnp.sum(dhv, axis=0, keepdims=True)
        acc_ref[1:2, :] += jnp.sum(dhn * gv, axis=0, keepdims=True)
        acc_ref[2:3, :] += jnp.sum(dhn * one_sc, axis=0, keepdims=True)

    return pl.pallas_call(
        body, name=name, grid=(s // tm,),
        out_shape=[jax.ShapeDtypeStruct((s, D), F32), jax.ShapeDtypeStruct((8, D), F32)],
        in_specs=[_row(tm, D), _row(tm, D), _row(tm, D), _vec(D), _vec(D), _vec(D)],
        out_specs=[_row(tm, D), pl.BlockSpec((8, D), lambda i: (0, 0))],
        compiler_params=_params("arbitrary"),
    )(dh, x, gin, g, sc, sh)


def _swiglu(name, ab, tm=512):
    s = ab.shape[0]

    def body(ab_ref, s_ref):
        a = ab_ref[:, :FF].astype(F32)
        b = ab_ref[:, FF:].astype(F32)
        s_ref[...] = (a * _sigmoid(a) * b).astype(BF16)

    return pl.pallas_call(
        body, name=name, grid=(s // tm,),
        out_shape=jax.ShapeDtypeStruct((s, FF), BF16),
        in_specs=[_row(tm, 2 * FF)], out_specs=_row(tm, FF),
        compiler_params=_params("parallel"),
    )(ab)


def _swiglu_bwd(name, ds, ab, tm=256):
    s = ab.shape[0]

    def body(ds_ref, ab_ref, dab_ref):
        a = ab_ref[:, :FF].astype(F32)
        b = ab_ref[:, FF:].astype(F32)
        dsv = ds_ref[...].astype(F32)
        sig = _sigmoid(a)
        dab_ref[:, :FF] = (dsv * b * (sig * (1.0 + a * (1.0 - sig)))).astype(BF16)
        dab_ref[:, FF:] = (dsv * (a * sig)).astype(BF16)

    return pl.pallas_call(
        body, name=name, grid=(s // tm,),
        out_shape=jax.ShapeDtypeStruct((s, 2 * FF), BF16),
        in_specs=[_row(tm, FF), _row(tm, 2 * FF)], out_specs=_row(tm, 2 * FF),
        compiler_params=_params("parallel"),
    )(ds, ab)


def _residual(name, x, f, gt, coef, tm=512):
    s = x.shape[0]

    def body(x_ref, f_ref, gt_ref, o_ref):
        o_ref[...] = x_ref[...] + (coef * gt_ref[...]) * f_ref[...]

    return pl.pallas_call(
        body, name=name, grid=(s // tm,),
        out_shape=jax.ShapeDtypeStruct((s, D), F32),
        in_specs=[_row(tm, D), _row(tm, D), _vec(D)], out_specs=_row(tm, D),
        compiler_params=_params("parallel"),
    )(x, f, gt)


def _gate_bwd(name, gin, f, gt, coef, tm=512):
    s = gin.shape[0]

    def body(g_ref, f_ref, gt_ref, df_ref, acc_ref):
        gv = g_ref[...]
        df_ref[...] = ((coef * gt_ref[...]) * gv).astype(BF16)

        @pl.when(pl.program_id(0) == 0)
        def _():
            acc_ref[...] = jnp.zeros_like(acc_ref)

        acc_ref[0:1, :] += coef * jnp.sum(gv * f_ref[...], axis=0, keepdims=True)

    return pl.pallas_call(
        body, name=name, grid=(s // tm,),
        out_shape=[jax.ShapeDtypeStruct((s, D), BF16), jax.ShapeDtypeStruct((8, D), F32)],
        in_specs=[_row(tm, D), _row(tm, D), _vec(D)],
        out_specs=[_row(tm, D), pl.BlockSpec((8, D), lambda i: (0, 0))],
        compiler_params=_params("arbitrary"),
    )(gin, f, gt)


def _loss_grad(x3, target, tm=512):
    s = x3.shape[0]

    def body(y_ref, t_ref, g_ref, l_ref):
        e = y_ref[...] - t_ref[...]
        g_ref[...] = e * (1.0 / D)

        @pl.when(pl.program_id(0) == 0)
        def _():
            l_ref[...] = jnp.zeros_like(l_ref)

        l_ref[...] += jnp.sum(jnp.mean(e * e, axis=-1, keepdims=True), axis=0, keepdims=True) * 0.5

    return pl.pallas_call(
        body, name="loss_grad", grid=(s // tm,),
        out_shape=[jax.ShapeDtypeStruct((s, D), F32), jax.ShapeDtypeStruct((8, 128), F32)],
        in_specs=[_row(tm, D), _row(tm, D)],
        out_specs=[_row(tm, D), pl.BlockSpec((8, 128), lambda i: (0, 0))],
        compiler_params=_params("arbitrary"),
    )(x3, target)


def _heads(x, fn):
    return jnp.concatenate([fn(x[:, h * HD:(h + 1) * HD], h) for h in range(COL // HD)], axis=1)


def _qknorm(proj, wqk, tm=1024):
    s = proj.shape[0]

    def body(p_ref, w_ref, o_ref):
        pv = p_ref[...].astype(F32)
        wv = w_ref[...]

        def one(qh, h):
            r = lax.rsqrt(jnp.mean(qh * qh, axis=-1, keepdims=True) + EPS)
            return (qh * r) * wv[:, h * HD:(h + 1) * HD]

        o_ref[...] = _heads(pv, one).astype(BF16)

    return pl.pallas_call(
        body, name="qknorm", grid=(s // tm, QKW // COL),
        out_shape=jax.ShapeDtypeStruct((s, QKW), BF16),
        in_specs=[pl.BlockSpec((tm, COL), lambda i, j: (i, j)), pl.BlockSpec((1, COL), lambda i, j: (0, j))],
        out_specs=pl.BlockSpec((tm, COL), lambda i, j: (i, j)),
        compiler_params=_params("parallel", "parallel"),
    )(proj, wqk)


def _qknorm_bwd(name, proj, dn, w, dproj, blk0, tm=1024):
    s = proj.shape[0]
    nblk = dn.shape[1] // COL

    def body(p_ref, d_ref, w_ref, _, o_ref, acc_ref):
        pv = p_ref[...].astype(F32)
        dv = d_ref[...]
        wv = w_ref[...]
        sums = []

        def one(qh, h):
            dn = dv[:, h * HD:(h + 1) * HD]
            r = lax.rsqrt(jnp.mean(qh * qh, axis=-1, keepdims=True) + EPS)
            nh = qh * r
            sums.append(jnp.sum(dn * nh, axis=0, keepdims=True))
            dnw = dn * wv[:, h * HD:(h + 1) * HD]
            return r * (dnw - nh * jnp.mean(dnw * nh, axis=-1, keepdims=True))

        o_ref[...] = _heads(pv, one).astype(BF16)

        @pl.when(pl.program_id(1) == 0)
        def _():
            acc_ref[...] = jnp.zeros_like(acc_ref)

        acc_ref[0:1, :] += jnp.concatenate(sums, axis=1)

    return pl.pallas_call(
        body, name=name, grid=(nblk, s // tm),
        out_shape=[jax.ShapeDtypeStruct((s, IN_W), BF16), jax.ShapeDtypeStruct((8, nblk * COL), F32)],
        in_specs=[pl.BlockSpec((tm, COL), lambda j, i: (i, blk0 + j)), pl.BlockSpec((tm, COL), lambda j, i: (i, j)),
                  pl.BlockSpec((1, COL), lambda j, i: (0, j)), pl.BlockSpec(memory_space=pl.ANY)],
        out_specs=[pl.BlockSpec((tm, COL), lambda j, i: (i, blk0 + j)),
                   pl.BlockSpec((8, COL), lambda j, i: (0, j))],
        input_output_aliases={3: 0},
        compiler_params=_params("arbitrary", "arbitrary"),
    )(proj, dn, w, dproj)


def _attn_shapes(s, g):
    d = DILATIONS[g]
    tb = min(s, max(2048, 256 * d))
    sb = min(256, tb // d)
    pb = BAND * d
    assert s % tb == 0 and tb % pb == 0 and (tb // d) % sb == 0 and sb % BAND == 0
    return d, tb, sb, pb


def _lanes(x, width):
    return jnp.concatenate([x] * (width // HD), axis=1)


def _every(start, size, d):
    return pl.ds(start, size, stride=d) if d > 1 else pl.ds(start, size)


def _attn_specs(g, tb, pb, s, ahead):
    ratio = tb // pb
    if ahead:
        nbr = lambda n: jnp.minimum((n + 1) * ratio, s // pb - 1)
    else:
        nbr = lambda n: jnp.maximum(n * ratio - 1, 0)
    cur = lambda base: pl.BlockSpec((tb, HD), lambda h, n: (n, base + g * N_HEADS + h))
    side = lambda base: pl.BlockSpec((pb, HD), lambda h, n: (nbr(n), base + g * N_HEADS + h))
    tok = pl.BlockSpec((tb, HD), lambda h, n: (n, h))
    tok_side = pl.BlockSpec((pb, HD), lambda h, n: (nbr(n), h))
    return cur, side, tok, tok_side


Q_COL, K_COL, V_COL = 0, 12, 24


def _attn_fwd(g, qkn, proj):
    s = qkn.shape[0]
    d, tb, sb, pb = _attn_shapes(s, g)
    ft = F32 if d > 1 else BF16
    nj = tb // d // sb
    scale = HD ** -0.5

    def body(q_ref, kc_ref, kp_ref, vc_ref, vp_ref, o_ref, lse_ref, qf, kf, vf):
        n = pl.program_id(1)
        qf[...] = q_ref[...].astype(ft)
        kf[0:pb] = kp_ref[...].astype(ft)
        kf[pb:] = kc_ref[...].astype(ft)
        vf[0:pb] = vp_ref[...].astype(ft)
        vf[pb:] = vc_ref[...].astype(ft)
        for r in range(d):
            for j in range(nj):
                at = j * sb * d + r
                q = qf[_every(at, sb, d), :].astype(BF16)
                k = kf[_every(at, sb + BAND, d), :].astype(BF16)
                v = vf[_every(at, sb + BAND, d), :].astype(BF16)
                sc = lax.dot_general(q, k, NT_DIMS, preferred_element_type=F32) * scale
                qi = lax.broadcasted_iota(jnp.int32, sc.shape, 0)
                kj = lax.broadcasted_iota(jnp.int32, sc.shape, 1)
                valid = (kj >= qi) & (kj <= qi + BAND)
                if j == 0:
                    valid = valid & ((kj >= BAND) | (n > 0))
                sc = jnp.where(valid, sc, -1e30)
                m = jnp.max(sc, axis=-1, keepdims=True)
                p = jnp.exp(sc - m)
                l = jnp.sum(p, axis=-1, keepdims=True)
                o = lax.dot_general(p.astype(BF16), v, NN_DIMS, preferred_element_type=F32)
                o_ref[_every(at, sb, d), :] = o / l
                lse_ref[_every(at, sb, d), :] = jnp.broadcast_to(m + jnp.log(l), (sb, HD))

    cur, side, tok, _ = _attn_specs(g, tb, pb, s, ahead=False)
    return pl.pallas_call(
        body, name=f"attn_fwd_g{g}", grid=(N_HEADS, s // tb),
        out_shape=[jax.ShapeDtypeStruct((s, COL), F32)] * 2,
        in_specs=[cur(Q_COL), cur(K_COL), side(K_COL), cur(V_COL), side(V_COL)],
        out_specs=[tok, tok],
        scratch_shapes=[pltpu.VMEM((tb, HD), ft), pltpu.VMEM((tb + pb, HD), ft),
                        pltpu.VMEM((tb + pb, HD), ft)],
        compiler_params=_params("parallel", "arbitrary"),
    )(qkn, qkn, qkn, proj, proj)


def _attn_combine(os_, lses, tm=512):
    s = os_[0].shape[0]

    def body(o0, o1, o2, l0, l1, l2, o_ref, lse_ref):
        a, b, c = l0[...], l1[...], l2[...]
        m = jnp.maximum(jnp.maximum(a, b), c)
        ea, eb, ec = jnp.exp(a - m), jnp.exp(b - m), jnp.exp(c - m)
        tot = ea + eb + ec
        o_ref[...] = ((ea * o0[...] + eb * o1[...] + ec * o2[...]) / tot).astype(BF16)
        lse_ref[...] = m + jnp.log(tot)

    return pl.pallas_call(
        body, name="attn_combine", grid=(s // tm,),
        out_shape=[jax.ShapeDtypeStruct((s, COL), BF16), jax.ShapeDtypeStruct((s, COL), F32)],
        in_specs=[_row(tm, COL)] * 6, out_specs=[_row(tm, COL)] * 2,
        compiler_params=_params("parallel"),
    )(*os_, *lses)


def _attn_delta(do, o, tm=512):
    s = do.shape[0]

    def body(do_ref, o_ref, del_ref):
        prod = do_ref[...] * o_ref[...].astype(F32)
        del_ref[...] = _heads(prod, lambda ph, h: jnp.broadcast_to(
            jnp.sum(ph, axis=-1, keepdims=True), ph.shape))

    return pl.pallas_call(
        body, name="attn_delta", grid=(s // tm,),
        out_shape=jax.ShapeDtypeStruct((s, COL), F32),
        in_specs=[_row(tm, COL)] * 2, out_specs=_row(tm, COL),
        compiler_params=_params("parallel"),
    )(do, o)


def _attn_bwd(g, qkn, proj, do, lse, delta, dqn, dkn, dproj):
    s = qkn.shape[0]
    d, tb, sb, pb = _attn_shapes(s, g)
    ft = F32 if d > 1 else BF16
    nj = tb // d // sb
    nt = s // tb
    scale = HD ** -0.5
    chained = dqn is not None

    def body(k_ref, v_ref, qc_ref, qn_ref, doc_ref, don_ref, lc_ref, ln_ref, dc_ref, dn_ref, *rest):
        dq_ref, dk_ref, dv_ref, kf, vf, qf, dvf, later = rest[-8:]
        n = pl.program_id(1)
        kf[...] = k_ref[...].astype(ft)
        vf[...] = v_ref[...].astype(ft)
        qf[0:tb] = qc_ref[...].astype(ft)
        qf[tb:] = qn_ref[...].astype(ft)

        @pl.when(n == 0)
        def _():
            later[...] = jnp.zeros_like(later)

        def window(c_ref, n_ref, r, j):
            at = j * sb * d + r
            if j < nj - 1:
                return c_ref[_every(at, sb + BAND, d), :]
            return jnp.concatenate([c_ref[_every(at, sb, d), :], n_ref[_every(r, BAND, d), :]], axis=0)

        for r in range(d):
            tail = later[r]
            for j in range(nj):
                at = j * sb * d + r
                rows = _every(at, sb, d)
                k = kf[rows, :].astype(BF16)
                v = vf[rows, :].astype(BF16)
                q = qf[_every(at, sb + BAND, d), :].astype(BF16)
                dov = window(doc_ref, don_ref, r, j).astype(BF16)
                sc = lax.dot_general(q, k, NT_DIMS, preferred_element_type=F32) * scale
                qi = lax.broadcasted_iota(jnp.int32, sc.shape, 0)
                kj = lax.broadcasted_iota(jnp.int32, sc.shape, 1)
                valid = (qi >= kj) & (qi <= kj + BAND)
                if j == nj - 1:
                    valid = valid & ((qi < sb) | (n < nt - 1))
                p = jnp.exp(jnp.where(valid, sc - _lanes(window(lc_ref, ln_ref, r, j), sb), -1e30))
                dp = lax.dot_general(dov, v, NT_DIMS, preferred_element_type=F32)
                ds = (p * (dp - _lanes(window(dc_ref, dn_ref, r, j), sb)) * scale).astype(BF16)
                dvf[rows, :] = lax.dot_general(p.astype(BF16), dov, TN_DIMS, preferred_element_type=F32)
                dk_ref[rows, :] = lax.dot_general(ds, q, TN_DIMS, preferred_element_type=F32)
                dqw = lax.dot_general(ds, k, NN_DIMS, preferred_element_type=F32)
                first = dqw[:BAND] + tail
                dq_ref[rows, :] = first if sb == BAND else jnp.concatenate([first, dqw[BAND:sb]], axis=0)
                tail = dqw[sb:]
            later[r] = tail
        dv_ref[...] = dvf[...].astype(BF16)

    cur, side, tok, tok_side = _attn_specs(g, tb, pb, s, ahead=True)
    anyspec = pl.BlockSpec(memory_space=pl.ANY)
    n_heads_cols = 3 * N_HEADS * HD
    return pl.pallas_call(
        body, name=f"attn_bwd_g{g}", grid=(N_HEADS, nt),
        out_shape=[jax.ShapeDtypeStruct((s, n_heads_cols), F32), jax.ShapeDtypeStruct((s, n_heads_cols), F32),
                   jax.ShapeDtypeStruct((s, IN_W), BF16)],
        in_specs=[cur(K_COL), cur(V_COL), cur(Q_COL), side(Q_COL), tok, tok_side, tok, tok_side,
                  tok, tok_side] + ([anyspec, anyspec] if chained else []) + [anyspec],
        out_specs=[cur(0), cur(0), cur(V_COL)],
        input_output_aliases={10: 0, 11: 1, 12: 2} if chained else {10: 2},
        scratch_shapes=[pltpu.VMEM((tb, HD), ft), pltpu.VMEM((tb, HD), ft),
                        pltpu.VMEM((tb + pb, HD), ft), pltpu.VMEM((tb, HD), F32),
                        pltpu.VMEM((d, BAND, HD), F32)],
        compiler_params=_params("arbitrary", "arbitrary"),
    )(qkn, proj, qkn, qkn, do, do, lse, lse, delta, delta, *([dqn, dkn] if chained else []), dproj)


def _shift_down(x, before, k):
    rolled = pltpu.roll(x, k, 0)
    head = jnp.where(lax.broadcasted_iota(jnp.int32, before.shape, 0) < k, pltpu.roll(before, k, 0), rolled[:8])
    return jnp.concatenate([head, rolled[8:]], axis=0)


def _shift_up(x, after, k):
    rows = x.shape[0]
    rolled = pltpu.roll(x, rows - k, 0)
    tail = jnp.where(lax.broadcasted_iota(jnp.int32, after.shape, 0) >= 8 - k,
                     pltpu.roll(after, 8 - k, 0), rolled[rows - 8:])
    return jnp.concatenate([rolled[:rows - 8], tail], axis=0)


def _conv_fwd(proj, cw, tm=1024):
    s = proj.shape[0]
    r16 = tm // 16

    def body(u_ref, b_ref, c_ref, up_ref, cp_ref, w_ref, z_ref):
        i = pl.program_id(1)
        xc = c_ref[...].astype(F32) * u_ref[...].astype(F32)
        xp = jnp.where(i > 0, cp_ref[8:16, :].astype(F32) * up_ref[8:16, :].astype(F32), 0.0)
        w = w_ref[...]
        conv = _shift_down(xc, xp, 2) * w[0:1] + _shift_down(xc, xp, 1) * w[1:2] + xc * w[2:3]
        z_ref[...] = (b_ref[...].astype(F32) * conv).astype(BF16)

    tile = lambda blk: pl.BlockSpec((tm, COL), lambda j, i: (i, blk + j))
    before = lambda blk: pl.BlockSpec((16, COL), lambda j, i: (jnp.maximum(i * r16 - 1, 0), blk + j))
    return pl.pallas_call(
        body, name="conv_fwd", grid=(D // COL, s // tm),
        out_shape=jax.ShapeDtypeStruct((s, D), BF16),
        in_specs=[tile(U_BLK), tile(B_BLK), tile(C_BLK), before(U_BLK), before(C_BLK),
                  pl.BlockSpec((3, COL), lambda j, i: (0, j))],
        out_specs=pl.BlockSpec((tm, COL), lambda j, i: (i, j)),
        compiler_params=_params("parallel", "parallel"),
    )(proj, proj, proj, proj, proj, cw)


def _conv_bwd(dz, proj, cw, dproj, tm=1024):
    s = proj.shape[0]
    r8, r16 = tm // 8, tm // 16
    nrow = s // tm

    def body(dz_ref, u_ref, b_ref, c_ref, up_ref, cp_ref, dzn_ref, bn_ref, w_ref, _, o_ref, acc_ref):
        piece, i = pl.program_id(1), pl.program_id(2)
        u, c = u_ref[...].astype(F32), c_ref[...].astype(F32)
        bv = b_ref[...].astype(F32)
        dzv = dz_ref[...]
        w = w_ref[...]

        @pl.when((piece == 0) & (i == 0))
        def _():
            acc_ref[...] = jnp.zeros_like(acc_ref)

        @pl.when(piece == 1)
        def _():
            xc = c * u
            xp = jnp.where(i > 0, cp_ref[8:16, :].astype(F32) * up_ref[8:16, :].astype(F32), 0.0)
            x2, x1 = _shift_down(xc, xp, 2), _shift_down(xc, xp, 1)
            o_ref[...] = (dzv * (x2 * w[0:1] + x1 * w[1:2] + xc * w[2:3])).astype(BF16)
            dconv = dzv * bv
            acc_ref[0:1, :] += jnp.sum(dconv * x2, axis=0, keepdims=True)
            acc_ref[1:2, :] += jnp.sum(dconv * x1, axis=0, keepdims=True)
            acc_ref[2:3, :] += jnp.sum(dconv * xc, axis=0, keepdims=True)

        @pl.when(piece != 1)
        def _():
            dconv = dzv * bv
            dn = jnp.where(i < nrow - 1, dzn_ref[...] * bn_ref[0:8, :].astype(F32), 0.0)
            dxc = dconv * w[2:3] + _shift_up(dconv, dn, 1) * w[1:2] + _shift_up(dconv, dn, 2) * w[0:1]
            o_ref[...] = (dxc * jnp.where(piece == 0, c, u)).astype(BF16)

    tile = lambda blk: pl.BlockSpec((tm, COL), lambda j, p, i: (i, blk + j))
    before = lambda blk: pl.BlockSpec((16, COL), lambda j, p, i: (jnp.maximum(i * r16 - 1, 0), blk + j))
    after = lambda rows, blk: pl.BlockSpec(
        (rows, COL), lambda j, p, i: (jnp.minimum((i + 1) * (tm // rows), s // rows - 1), blk + j))
    return pl.pallas_call(
        body, name="conv_bwd", grid=(D // COL, 3, nrow),
        out_shape=[jax.ShapeDtypeStruct((s, IN_W), BF16), jax.ShapeDtypeStruct((8, D), F32)],
        in_specs=[tile(0), tile(U_BLK), tile(B_BLK), tile(C_BLK), before(U_BLK), before(C_BLK),
                  after(8, 0), after(16, B_BLK), pl.BlockSpec((3, COL), lambda j, p, i: (0, j)),
                  pl.BlockSpec(memory_space=pl.ANY)],
        out_specs=[pl.BlockSpec((tm, COL), lambda j, p, i: (i, U_BLK + 2 * p + j)),
                   pl.BlockSpec((8, COL), lambda j, p, i: (0, j))],
        input_output_aliases={9: 0},
        compiler_params=_params("arbitrary", "arbitrary", "arbitrary"),
    )(dz, proj, proj, proj, proj, proj, dz, proj, cw, dproj)


def _merge_fwd(ya, yc, proj, tm=512):
    s = proj.shape[0]

    def body(ya_ref, yc_ref, ga_ref, gc_ref, o_ref):
        o_ref[...] = (_sigmoid(ga_ref[...].astype(F32)) * ya_ref[...].astype(F32)
                      + _sigmoid(gc_ref[...].astype(F32)) * yc_ref[...].astype(F32)).astype(BF16)

    tile = lambda blk: pl.BlockSpec((tm, COL), lambda j, i: (i, blk + j))
    return pl.pallas_call(
        body, name="merge_fwd", grid=(D // COL, s // tm),
        out_shape=jax.ShapeDtypeStruct((s, D), BF16),
        in_specs=[tile(0), tile(0), tile(GA_BLK), tile(GC_BLK)], out_specs=tile(0),
        compiler_params=_params("parallel", "parallel"),
    )(ya, yc, proj, proj)


def _merge_bwd_branches(dm, proj, tm=512):
    s = proj.shape[0]

    def body(dm_ref, ga_ref, gc_ref, dya_ref, dyc_ref):
        dmv = dm_ref[...]
        dya_ref[...] = (dmv * _sigmoid(ga_ref[...].astype(F32))).astype(BF16)
        dyc_ref[...] = (dmv * _sigmoid(gc_ref[...].astype(F32))).astype(BF16)

    tile = lambda blk: pl.BlockSpec((tm, COL), lambda j, i: (i, blk + j))
    return pl.pallas_call(
        body, name="merge_bwd_branches", grid=(D // COL, s // tm),
        out_shape=[jax.ShapeDtypeStruct((s, D), BF16)] * 2,
        in_specs=[tile(0), tile(GA_BLK), tile(GC_BLK)], out_specs=[tile(0)] * 2,
        compiler_params=_params("parallel", "parallel"),
    )(dm, proj, proj)


def _merge_bwd_gates(dm, ya, yc, proj, tm=1024):
    s = proj.shape[0]
    half = D // COL

    def body(dm_ref, ya_ref, yc_ref, g_ref, o_ref):
        y = jnp.where(pl.program_id(0) < half, ya_ref[...].astype(F32), yc_ref[...].astype(F32))
        sig = _sigmoid(g_ref[...].astype(F32))
        o_ref[...] = (dm_ref[...] * y * sig * (1.0 - sig)).astype(BF16)

    chan = pl.BlockSpec((tm, COL), lambda jj, i: (i, jj % half))
    gate = pl.BlockSpec((tm, COL), lambda jj, i: (i, GA_BLK + jj))
    return pl.pallas_call(
        body, name="merge_bwd_gates", grid=(2 * half, s // tm),
        out_shape=jax.ShapeDtypeStruct((s, IN_W), BF16),
        in_specs=[chan, chan, chan, gate], out_specs=gate,
        compiler_params=_params("parallel", "parallel"),
    )(dm, ya, yc, proj)


def _mod_part(c_all, w_ada, b_part):
    def body(c_ref, w_ref, b_ref, o_ref):
        cv = c_ref[...]
        act = cv * _sigmoid(cv)
        o_ref[...] = jnp.dot(act, w_ref[...], preferred_element_type=F32,
                             precision=lax.Precision.HIGHEST) + b_ref[...]

    return pl.pallas_call(
        body, name="mod_part", out_shape=jax.ShapeDtypeStruct((N_DEV, w_ada.shape[1]), F32),
    )(c_all, w_ada, b_part)


def _w_ada_grad(c_all_t, dmod_part):
    def body(c_ref, d_ref, o_ref):
        cv = c_ref[...]
        act = cv * _sigmoid(cv)
        dv = d_ref[...]
        acc = act[:, 0:1] * dv[0:1, :]
        for b in range(1, N_DEV):
            acc = acc + act[:, b:b + 1] * dv[b:b + 1, :]
        o_ref[...] = acc

    return pl.pallas_call(
        body, name="w_ada_grad", out_shape=jax.ShapeDtypeStruct((D, dmod_part.shape[1]), F32),
    )(c_all_t, dmod_part)


def _sum_rows(name, v):
    def body(v_ref, o_ref):
        acc = v_ref[0]
        for k in range(1, N_DEV):
            acc = acc + v_ref[k]
        o_ref[...] = acc

    return pl.pallas_call(body, name=name, out_shape=jax.ShapeDtypeStruct(v.shape[1:], F32))(v)


def _adamw(name, w, g, m, v):
    rows, cols = w.shape
    limit = max(16, (1 << 20) // (4 * cols))
    tr = rows if rows <= limit else next((t for t in range(limit - limit % 16, 15, -16) if rows % t == 0), rows)
    c1 = 1.0 - ADAM_B1 ** ADAM_STEP
    c2 = 1.0 - ADAM_B2 ** ADAM_STEP
    parts = g.ndim == 3

    def body(w_ref, g_ref, m_ref, v_ref, go_ref, d_ref, nm_ref, nv_ref):
        if parts:
            gv = g_ref[0].astype(F32)
            for k in range(1, N_DEV):
                gv = gv + g_ref[k].astype(F32)
        else:
            gv = g_ref[...]
        go_ref[...] = gv
        nm = ADAM_B1 * m_ref[...] + (1.0 - ADAM_B1) * gv
        nv = ADAM_B2 * v_ref[...] + (1.0 - ADAM_B2) * (gv * gv)
        nm_ref[...] = nm
        nv_ref[...] = nv
        d_ref[...] = -ADAM_LR * ((nm / c1) / (jnp.sqrt(nv / c2) + ADAM_EPS) + ADAM_WD * w_ref[...])

    spec = pl.BlockSpec((tr, cols), lambda i: (i, 0))
    g_spec = pl.BlockSpec((N_DEV, tr, cols), lambda i: (0, i, 0)) if parts else spec
    return pl.pallas_call(
        body, name=name, grid=(rows // tr,),
        out_shape=[jax.ShapeDtypeStruct((rows, cols), F32)] * 4,
        in_specs=[spec, g_spec, spec, spec], out_specs=[spec] * 4,
        compiler_params=_params("parallel"),
    )(w, g, m, v)


def _adamw_small(ws, gs, ms, vs):
    n = len(ws)
    c1 = 1.0 - ADAM_B1 ** ADAM_STEP
    c2 = 1.0 - ADAM_B2 ** ADAM_STEP

    def body(*refs):
        for i in range(n):
            w_ref, g_ref, m_ref, v_ref = refs[i], refs[n + i], refs[2 * n + i], refs[3 * n + i]
            d_ref, nm_ref, nv_ref = refs[4 * n + 3 * i:4 * n + 3 * i + 3]
            gv = g_ref[...]
            nm = ADAM_B1 * m_ref[...] + (1.0 - ADAM_B1) * gv
            nv = ADAM_B2 * v_ref[...] + (1.0 - ADAM_B2) * (gv * gv)
            nm_ref[...] = nm
            nv_ref[...] = nv
            d_ref[...] = -ADAM_LR * ((nm / c1) / (jnp.sqrt(nv / c2) + ADAM_EPS) + ADAM_WD * w_ref[...])

    outs = pl.pallas_call(
        body, name="adamw_small",
        out_shape=[jax.ShapeDtypeStruct(w.shape, F32) for w in ws for _ in range(3)],
    )(*ws, *gs, *ms, *vs)
    return [tuple(outs[3 * i:3 * i + 3]) for i in range(n)]


HALF = FF // 2


def _sds(shape, dtype):
    return jax.ShapeDtypeStruct(shape, dtype)


def _row_tile(w):
    return lambda tm: ((tm, w), lambda i, j: (i, 0))


def _one(w):
    return lambda rows: ((rows, w), lambda i, j: (0, 0))


def _gate_up_swiglu(name, h, wgu, carry=None, tm=512):
    s = h.shape[0]
    tm = min(tm, s)

    def epilogue(prod, first, tin, tout):
        pq_ref, s_ref = tout
        a, b = prod[:, :HALF], prod[:, HALF:]
        sig = _sigmoid(a)
        act = a * sig
        pq_ref[:, :HALF] = (b * (sig * (1.0 + a * (1.0 - sig)))).astype(BF16)
        pq_ref[:, HALF:] = act.astype(BF16)
        s_ref[...] = (act * b).astype(BF16)

    return _mm(name, h, wgu, "NT", None, tm, FF, D, carry=carry, n_outer=True, epilogue=epilogue,
               tiles_out=[(_sds((s, 2 * FF), BF16), (tm, FF), lambda i, j: (i, j)),
                          (_sds((s, FF), BF16), (tm, HALF), lambda i, j: (i, j))])


def _d_hidden_swiglu(name, df, wd, ab, tm=512):
    s = df.shape[0]
    tm = min(tm, s)

    def epilogue(prod, first, tin, tout, cols):
        da_cols = slice(cols[0], cols[0] + cols[1])
        db_cols = slice(HALF + cols[0], HALF + cols[0] + cols[1])
        tout[0][:, da_cols] = (prod * tin[0][:, da_cols].astype(F32)).astype(BF16)
        tout[0][:, db_cols] = (prod * tin[0][:, db_cols].astype(F32)).astype(BF16)

    chunks = [(c0, min(384, HALF - c0)) for c0 in range(0, HALF, 384)]
    return _mm(name, df, wd, "NT", None, tm, HALF, D, n_outer=True, epilogue=epilogue, col_chunks=chunks,
               tiles_in=[(ab, (tm, FF), lambda i, j: (i, j))],
               tiles_out=[(_sds((s, 2 * FF), BF16), (tm, FF), lambda i, j: (i, j))])[0]


def _out_residual(name, a, w, x, gt, coef, nxt, tm=512, tk=FF):
    s = a.shape[0]
    tm = min(tm, s)

    def epilogue(prod, first, tin, tout):
        x_ref, gt_ref, g_ref, sc_ref, sh_ref = tin
        f_ref, xn_ref, h_ref = tout
        f_ref[...] = prod
        xn = x_ref[...] + (coef * gt_ref[...]) * prod
        xn_ref[...] = xn
        r = lax.rsqrt(jnp.mean(xn * xn, axis=-1, keepdims=True) + EPS)
        h_ref[...] = ((xn * r) * g_ref[...] * (1.0 + sc_ref[...]) + sh_ref[...]).astype(BF16)

    row, vec = _row_tile(D)(tm), _one(D)(1)
    return _mm(name, a, w, "NN", None, tm, D, tk, epilogue=epilogue,
               tiles_in=[(x, *row), (gt, *vec)] + [(v, *vec) for v in nxt],
               tiles_out=[(_sds((s, D), F32), *row), (_sds((s, D), F32), *row), (_sds((s, D), BF16), *row)])


def _out_loss(name, a, w, x, gt, coef, target, tm=512):
    s = a.shape[0]
    tm = min(tm, s)

    def epilogue(prod, first, tin, tout):
        x_ref, gt_ref, t_ref = tin
        f_ref, g_ref, df_ref, acc_ref = tout
        f_ref[...] = prod
        cg = coef * gt_ref[...]
        e = x_ref[...] + cg * prod - t_ref[...]
        gv = e * (1.0 / D)
        g_ref[...] = gv
        df_ref[...] = (cg * gv).astype(BF16)

        @pl.when(first)
        def _():
            acc_ref[...] = jnp.zeros_like(acc_ref)

        acc_ref[0:1, :] += coef * jnp.sum(gv * prod, axis=0, keepdims=True)
        acc_ref[1:2, :] += (0.5 / D) * jnp.sum(e * e, axis=0, keepdims=True)

    row, vec = _row_tile(D)(tm), _one(D)(1)
    return _mm(name, a, w, "NN", None, tm, D, FF, epilogue=epilogue,
               tiles_in=[(x, *row), (gt, *vec), (target, *row)],
               tiles_out=[(_sds((s, D), F32), *row), (_sds((s, D), F32), *row), (_sds((s, D), BF16), *row),
                          (_sds((8, D), F32), *_one(D)(8))])


def _d_h_norm_bwd(name, da, w, x, gin, g, sc, sh, before=None, carry=None, tm=256):
    s = da.shape[0]
    tm = min(tm, s)
    coef = before[2] if before else None

    def epilogue(prod, first, tin, tout):
        x_ref, gin_ref, g_ref, sc_ref, sh_ref = tin[:5]
        gout_ref, acc_ref = tout[:2]
        xv = x_ref[...]
        r = lax.rsqrt(jnp.mean(xv * xv, axis=-1, keepdims=True) + EPS)
        nv = xv * r
        gv, one_sc = g_ref[...], 1.0 + sc_ref[...]
        dn = prod * gv * one_sc
        gout = gin_ref[...] + r * (dn - nv * jnp.mean(dn * nv, axis=-1, keepdims=True))
        gout_ref[...] = gout

        @pl.when(first)
        def _():
            acc_ref[...] = jnp.zeros_like(acc_ref)

        dhn = prod * nv
        acc_ref[0:1, :] += jnp.sum(prod, axis=0, keepdims=True)
        acc_ref[1:2, :] += jnp.sum(dhn * gv, axis=0, keepdims=True)
        acc_ref[2:3, :] += jnp.sum(dhn * one_sc, axis=0, keepdims=True)
        if before:
            f_ref, gt_ref = tin[5:]
            tout[2][...] = ((coef * gt_ref[...]) * gout).astype(BF16)
            acc_ref[3:4, :] += coef * jnp.sum(gout * f_ref[...], axis=0, keepdims=True)

    row, vec = _row_tile(D)(tm), _one(D)(1)
    tiles_in = [(x, *row), (gin, *row), (g, *vec), (sc, *vec), (sh, *vec)]
    tiles_out = [(_sds((s, D), F32), *row), (_sds((8, D), F32), *_one(D)(8))]
    if before:
        tiles_in += [(before[0], *row), (before[1], *vec)]
        tiles_out.append((_sds((s, D), BF16), *row))
    return _mm(name, da, w, "NN", None, tm, D, da.shape[1], epilogue=epilogue, carry=carry, keep_b=True,
               tiles_in=tiles_in, tiles_out=tiles_out)


def _gate_tiles(proj, tm):
    return [(proj, (tm, COL), (lambda i, j, blk=blk: (i, blk))) for blk in (GA_BLK, GA_BLK + 1, GC_BLK, GC_BLK + 1)]


def _conv_branch_merge(z, wc, ya, proj, tm=512):
    s = z.shape[0]
    tm = min(tm, s)

    def epilogue(prod, first, tin, tout):
        ya_ref, ga0, ga1, gc0, gc1 = tin
        tout[0][...] = prod.astype(BF16)
        for half, (ga, gc) in enumerate(((ga0, gc0), (ga1, gc1))):
            cols = slice(half * COL, (half + 1) * COL)
            tout[1][:, cols] = (_sigmoid(ga[...].astype(F32)) * ya_ref[:, cols].astype(F32)
                                + _sigmoid(gc[...].astype(F32)) * prod[:, cols]).astype(BF16)

    row = _row_tile(D)(tm)
    return _mm("mix_conv_branch", z, wc, "NN", None, tm, D, D, epilogue=epilogue,
               tiles_in=[(ya, *row)] + _gate_tiles(proj, tm),
               tiles_out=[(_sds((s, D), BF16), *row), (_sds((s, D), BF16), *row)])


def _d_merged_branches(dmix, wo, proj, tm=512):
    s = dmix.shape[0]
    tm = min(tm, s)

    def epilogue(prod, first, tin, tout):
        ga0, ga1, gc0, gc1 = tin
        tout[0][...] = prod
        for half, (ga, gc) in enumerate(((ga0, gc0), (ga1, gc1))):
            cols = slice(half * COL, (half + 1) * COL)
            tout[1][:, cols] = (prod[:, cols] * _sigmoid(ga[...].astype(F32))).astype(BF16)
            tout[2][:, cols] = (prod[:, cols] * _sigmoid(gc[...].astype(F32))).astype(BF16)

    row = _row_tile(D)(tm)
    return _mm("mix_d_merged", dmix, wo, "NT", None, tm, D, D, epilogue=epilogue,
               tiles_in=_gate_tiles(proj, tm),
               tiles_out=[(_sds((s, D), F32), *row), (_sds((s, D), BF16), *row), (_sds((s, D), BF16), *row)])


def _d_o_delta(dya, wa_t, o, tm=1024):
    s = dya.shape[0]
    tm = min(tm, s)

    def epilogue(prod, first, tin, tout):
        tout[0][...] = prod
        tout[1][...] = _heads(prod * tin[0][...].astype(F32), lambda ph, h: jnp.broadcast_to(
            jnp.sum(ph, axis=-1, keepdims=True), ph.shape))

    row = _row_tile(COL)(tm)
    return _mm("mix_d_o", dya, wa_t, "NN", None, tm, COL, D, epilogue=epilogue,
               tiles_in=[(o, *row)], tiles_out=[(_sds((s, COL), F32), *row), (_sds((s, COL), F32), *row)])


def _ffn_bwd(tag, df, x, gin, h, ab, sw, g, sc, sh, wgu, wd, before=None, carry_down=None, carry_gate_up=None,
             tk_dw=2048):
    dab = _d_hidden_swiglu(f"{tag}_d_hidden", df, wd, ab)
    dwd = _mm(f"{tag}_dw_down", sw, df, "TN", BF16, HALF, D, tk_dw)
    carried = []
    if carry_down:
        dwgu, *got = _mm(f"{tag}_dw_gate_up", dab, h, "TN", BF16, HALF, D, tk_dw, carry=carry_down(dwd))
        carried += got
    else:
        dwgu = _mm(f"{tag}_dw_gate_up", dab, h, "TN", BF16, HALF, D, tk_dw)
    res = _d_h_norm_bwd(f"{tag}_d_h", dab, wgu, x, gin, g, sc, sh, before=before,
                        carry=carry_gate_up(dwgu) if carry_gate_up else None)
    n_own = 3 if before else 2
    return res[:n_own], dwgu, dwd, carried + list(res[n_own:])


def kernel(x, c, w_ada, b_ada, norm_ffn1, ffn1_w_gate, ffn1_w_up, ffn1_w_down, norm_mix, w_in, q_norm, k_norm, conv_w, w_attn_branch, w_conv_branch, w_out, norm_ffn2, ffn2_w_gate, ffn2_w_up, ffn2_w_down, loss_target, m_w_ada, m_b_ada, m_norm_ffn1, m_ffn1_w_gate, m_ffn1_w_up, m_ffn1_w_down, m_norm_mix, m_w_in, m_q_norm, m_k_norm, m_conv_w, m_w_attn_branch, m_w_conv_branch, m_w_out, m_norm_ffn2, m_ffn2_w_gate, m_ffn2_w_up, m_ffn2_w_down, v_w_ada, v_b_ada, v_norm_ffn1, v_ffn1_w_gate, v_ffn1_w_up, v_ffn1_w_down, v_norm_mix, v_w_in, v_q_norm, v_k_norm, v_conv_w, v_w_attn_branch, v_w_conv_branch, v_w_out, v_norm_ffn2, v_ffn2_w_gate, v_ffn2_w_up, v_ffn2_w_down):
    me = 4 * lax.axis_index("x") + 2 * lax.axis_index("y") + lax.axis_index("c")
    x0, target = x[0], loss_target[0]
    s = x0.shape[0]
    ada_cols = w_ada.shape[2]
    cw_cols = conv_w.shape[2]

    gathered = _small_allgather(
        "gather_c_conv", jnp.concatenate([c, conv_w[0].reshape(1, 3 * cw_cols)], axis=1))[:, 0]
    c_all = gathered[:, :D]
    cw = gathered[:, D:].reshape(N_DEV, 3, cw_cols).transpose(1, 0, 2).reshape(3, D)
    b_part = lax.dynamic_slice(b_ada, (0, me * ada_cols), (1, ada_cols))
    mod_part = _mod_part(c_all, w_ada[0], b_part)
    mod_all = _small_allgather("gather_mod", mod_part.reshape(1, N_DEV * ada_cols))
    mod = lax.dynamic_slice(mod_all.reshape(N_DEV, N_DEV, ada_cols), (0, me, 0), (N_DEV, 1, ada_cols))
    mod = mod.reshape(N_MOD, 1, D)
    sh1, sc1, gt1, sh2, sc2, gt2, sh3, sc3, gt3 = [mod[i] for i in range(N_MOD)]

    tb = lambda w: w[0].T.astype(BF16)
    nb = lambda w: w[0].astype(BF16)
    ffn1_shards = [tb(ffn1_w_gate), tb(ffn1_w_up), nb(ffn1_w_down)]
    ffn2_shards = [tb(ffn2_w_gate), tb(ffn2_w_up), nb(ffn2_w_down)]
    mix_shards = [tb(w_in), tb(w_attn_branch), nb(w_conv_branch), nb(w_out)]
    ffn_dst, ffn_base, ffn_jump, ffn_shapes = [0, 0, 1], [0, HALF, 0], [HALF, HALF, 0], [(2 * FF, D), (FF, D)]
    mix_dst, mix_base, mix_shapes = [0, 1, 2, 3], [0, 0, 0, 0], [(IN_W, D), (D, COL), (D, D), (D, D)]
    wgu1, wd1 = _run_plan("gather_ffn1_weights",
                          _gather_plan(ffn1_shards, ffn_dst, ffn_base, ffn_shapes, ffn_jump))

    h1 = _normmod("ffn1_normmod", x0, norm_ffn1, sc1, sh1)
    ab1, s1, win_t = _gate_up_swiglu(
        "ffn1_gate_up", h1, wgu1, carry=_gather_plan(mix_shards[:1], mix_dst[:1], mix_base[:1], mix_shapes[:1]))
    f1, x1, h2 = _out_residual("ffn1_down", s1, wd1, x0, gt1, 0.5, (norm_mix, sc2, sh2))
    proj, wgu2, wd2, wa_t, wc, wo = _mm(
        "mix_in_proj", h2, win_t, "NT", BF16, 1024, IN_W // 4, D, n_outer=True,
        carry=_gather_plan(ffn2_shards + mix_shards[1:], ffn_dst + [2, 3, 4], ffn_base + [0, 0, 0],
                           ffn_shapes + mix_shapes[1:], ffn_jump + [0, 0, 0]))
    wqk = jnp.concatenate([jnp.tile(q_norm, (1, 12)), jnp.tile(k_norm, (1, 12))], axis=1)
    qkn = _qknorm(proj, wqk)
    group_out = [_attn_fwd(g, qkn, proj) for g in range(3)]
    o, lse = _attn_combine([go[0] for go in group_out], [go[1] for go in group_out])
    ya = _mm("mix_attn_branch", o, wa_t, "NT", BF16, 1024, 1024, COL)
    z = _conv_fwd(proj, cw)
    yc, merged = _conv_branch_merge(z, wc, ya, proj)
    mix, x2, h3 = _out_residual("mix_out_proj", merged, wo, x1, gt2, 1.0, (norm_ffn2, sc3, sh3), tk=D)
    ab3, s3 = _gate_up_swiglu("ffn2_gate_up", h3, wgu2)
    f3, g3, df3, acc_out = _out_loss("ffn2_down", s3, wd2, x2, gt3, 0.5, target)
    loss_part = jnp.sum(acc_out[1])

    ffn_rows = [sh_.shape[0] for sh_ in ffn1_shards]
    mix_rows = [sh_.shape[0] for sh_ in mix_shards]
    (g2, acc3, dmix), dwgu2, dwd2, _ = _ffn_bwd(
        "ffn2", df3, x2, g3, h3, ab3, s3, norm_ffn2, sc3, sh3, wgu2, wd2, before=(mix, gt2, 1.0))
    dmerged, dya, dyc = _d_merged_branches(dmix, wo, proj)
    dwo = _mm("mix_dw_out", merged, dmix, "TN", BF16, 1024, 1024, 2048)
    dproj = _merge_bwd_gates(dmerged, ya, yc, proj)
    dwc = _mm("mix_dw_conv_branch", z, dyc, "TN", BF16, 1024, 1024, 2048)
    dz = _mm("mix_d_z", dyc, wc, "NT", F32, 1024, 1024, D)
    dproj, cw_acc = _conv_bwd(dz, proj, cw, dproj)
    dwa_t = _mm("mix_dw_attn_branch", dya, o, "TN", BF16, 1024, COL, 2048)
    do, delta = _d_o_delta(dya, wa_t, o)
    dqn = dkn = None
    for g in range(3):
        dqn, dkn, dproj = _attn_bwd(g, qkn, proj, do, lse, delta, dqn, dkn, dproj)
    dproj, wq_acc = _qknorm_bwd("qnorm_bwd", proj, dqn, wqk[:, :QKW // 2], dproj, 0)
    dproj, wk_acc = _qknorm_bwd("knorm_bwd", proj, dkn, wqk[:, QKW // 2:], dproj, QKW // 2 // COL)
    dwin_t, r_f2g, r_f2u, r_f2d, r_wa, r_wc, r_wo = _mm(
        "mix_dw_in", dproj, h2, "TN", BF16, IN_W // 4, COL, 2048,
        carry=_scatter_plan([dwgu2, dwd2, dwa_t, dwc, dwo], [0, 0, 1, 2, 3, 4], [0, HALF, 0, 0, 0, 0],
                            ffn_rows + mix_rows[1:], [D, D, D, COL, D, D], [HALF, HALF, 0, 0, 0, 0]))
    g1, acc2, df1, r_win = _d_h_norm_bwd(
        "mix_d_h", dproj, win_t, x1, g2, norm_mix, sc2, sh2, before=(f1, gt1, 0.5),
        carry=_scatter_plan([dwin_t], [0], [0], mix_rows[:1], [D]))
    (g0, acc1), dwgu1, dwd1, (r_f1d, r_f1g, r_f1u) = _ffn_bwd(
        "ffn1", df1, x0, g1, h1, ab1, s1, norm_ffn1, sc1, sh1, wgu1, wd1,
        carry_down=lambda dwd: _scatter_plan([dwd], [0], [0], ffn_rows[2:], [D]),
        carry_gate_up=lambda dwgu: _scatter_plan([dwgu], [0, 0], [0, HALF], ffn_rows[:2], [D, D], [HALF, HALF]))

    dqw = jnp.sum(wq_acc[0].reshape(12, HD), axis=0)
    dkw = jnp.sum(wk_acc[0].reshape(12, HD), axis=0)
    small = jnp.concatenate([
        acc1[0], acc1[1], acc2[3], acc2[0], acc2[1], acc3[3], acc3[0], acc3[1], acc_out[0],
        acc1[2], acc2[2], acc3[2], dqw, dkw, cw_acc[0:3].reshape(3 * D),
        jnp.zeros((HD,), F32).at[0].set(loss_part)]).reshape(1, -1)
    small_all = _small_allgather("gather_small_grads", small)
    small_sum = _sum_rows("sum_small_grads", small_all)[0]
    n_mod = N_MOD * D
    g_b_ada = small_sum[:n_mod].reshape(1, n_mod)
    g_norm1, g_norm2, g_norm3 = [small_sum[n_mod + i * D:n_mod + (i + 1) * D].reshape(1, D) for i in range(3)]
    off = n_mod + 3 * D
    g_qn, g_kn = small_sum[off:off + HD].reshape(1, HD), small_sum[off + HD:off + 2 * HD].reshape(1, HD)
    g_cw_full = small_sum[off + 2 * HD:off + 2 * HD + 3 * D].reshape(3, D)
    loss = small_sum[off + 2 * HD + 3 * D]
    g_cw = lax.dynamic_slice(g_cw_full, (0, me * cw_cols), (3, cw_cols))
    dmod_part = lax.dynamic_slice(small_all[:, 0, :n_mod], (0, me * ada_cols), (N_DEV, ada_cols))
    g_w_ada = _w_ada_grad(c_all.T, dmod_part)

    as_rows = {"ffn1_w_gate", "ffn1_w_up", "w_in", "w_attn_branch", "ffn2_w_gate", "ffn2_w_up"}
    grad_list = [g_w_ada, g_b_ada, g_norm1, r_f1g, r_f1u, r_f1d, g_norm2, r_win,
                 g_qn, g_kn, g_cw, r_wa, r_wc, r_wo, g_norm3, r_f2g, r_f2u, r_f2d]
    weights = [w_ada, b_ada, norm_ffn1, ffn1_w_gate, ffn1_w_up, ffn1_w_down, norm_mix, w_in, q_norm, k_norm,
               conv_w, w_attn_branch, w_conv_branch, w_out, norm_ffn2, ffn2_w_gate, ffn2_w_up, ffn2_w_down]
    ms = [m_w_ada, m_b_ada, m_norm_ffn1, m_ffn1_w_gate, m_ffn1_w_up, m_ffn1_w_down, m_norm_mix, m_w_in, m_q_norm,
          m_k_norm, m_conv_w, m_w_attn_branch, m_w_conv_branch, m_w_out, m_norm_ffn2, m_ffn2_w_gate,
          m_ffn2_w_up, m_ffn2_w_down]
    vs = [v_w_ada, v_b_ada, v_norm_ffn1, v_ffn1_w_gate, v_ffn1_w_up, v_ffn1_w_down, v_norm_mix, v_w_in, v_q_norm,
          v_k_norm, v_conv_w, v_w_attn_branch, v_w_conv_branch, v_w_out, v_norm_ffn2, v_ffn2_w_gate,
          v_ffn2_w_up, v_ffn2_w_down]
    wnames = ["w_ada", "b_ada", "norm_ffn1", "ffn1_w_gate", "ffn1_w_up", "ffn1_w_down", "norm_mix", "w_in",
              "q_norm", "k_norm", "conv_w", "w_attn_branch", "w_conv_branch", "w_out", "norm_ffn2",
              "ffn2_w_gate", "ffn2_w_up", "ffn2_w_down"]
    small = [i for i, gr in enumerate(grad_list) if gr.ndim == 2 and gr.size <= 16384]
    flat = lambda a, i: a.reshape(-1, weights[i].shape[-1])
    small_res = dict(zip(small, _adamw_small(
        [flat(weights[i], i) for i in small], [flat(grad_list[i], i) for i in small],
        [flat(ms[i], i) for i in small], [flat(vs[i], i) for i in small])))
    grad_out, deltas, new_ms, new_vs = [], [], [], []
    for idx, (nm, w, gr, m_, v_) in enumerate(zip(wnames, weights, grad_list, ms, vs)):
        if idx in small_res:
            gr, dl, nm_, nv_ = [r.reshape(w.shape) for r in (gr, *small_res[idx])]
        elif nm in as_rows:
            res = _adamw(f"adamw_{nm}", w[0].T, gr, m_[0].T, v_[0].T)
            gr, dl, nm_, nv_ = [r.T[None] for r in res]
        else:
            two_d = (-1, w.shape[-1])
            res = _adamw(f"adamw_{nm}", w.reshape(two_d), gr if gr.ndim == 3 else gr.reshape(two_d),
                         m_.reshape(two_d), v_.reshape(two_d))
            gr, dl, nm_, nv_ = [r.reshape(w.shape) for r in res]
        grad_out.append(gr)
        deltas.append(dl)
        new_ms.append(nm_)
        new_vs.append(nv_)
    return (loss, g0[None], *grad_out, *deltas, *new_ms, *new_vs)
```

```python
import functools

import jax
import jax.numpy as jnp
from jax import lax
from jax.experimental import pallas as pl
from jax.experimental.pallas import tpu as pltpu
from jax.experimental.pallas import tpu_sc as plsc

F32 = jnp.float32
BF16 = jnp.bfloat16
MESH = pl.DeviceIdType.MESH

N_DEV = 8
D = 1024
FF = 2816
HD = 128
N_HEADS = 4
DILATIONS = (1, 4, 16)
BAND = 128
QKW = 2 * 3 * N_HEADS * HD
IN_W = 9728
COL = 512
V_BLK, U_BLK, B_BLK, C_BLK, GA_BLK, GC_BLK = 6, 9, 11, 13, 15, 17
EPS = 1e-6
N_MOD = 9
ADAM_LR, ADAM_B1, ADAM_B2, ADAM_EPS, ADAM_WD, ADAM_STEP = 0.001, 0.9, 0.999, 1e-08, 0.01, 10

NT_DIMS = (((1,), (1,)), ((), ()))
TN_DIMS = (((0,), (0,)), ((), ()))
NN_DIMS = (((1,), (0,)), ((), ()))


def _place():
    return lax.axis_index("x"), lax.axis_index("y"), lax.axis_index("c")


def _flip(coord, bit):
    return 1 - coord if bit else coord


def _params(*sem):
    return pltpu.CompilerParams(dimension_semantics=sem)


def _small_allgather(name, v):
    n = v.shape[-1]

    def body(v_ref, out_ref, send_sems, recv_sems):
        x, y, c = _place()
        me = 4 * x + 2 * y + c
        out_ref[me] = v_ref[...]
        copies = []
        for k in range(1, N_DEV):
            peer = (_flip(x, (k >> 2) & 1), _flip(y, (k >> 1) & 1), _flip(c, k & 1))
            cp = pltpu.make_async_remote_copy(
                src_ref=v_ref, dst_ref=out_ref.at[me], send_sem=send_sems.at[k - 1],
                recv_sem=recv_sems.at[k - 1], device_id=peer, device_id_type=MESH)
            cp.start()
            copies.append(cp)
        for cp in copies:
            cp.wait()

    return pl.pallas_call(
        body, name=name,
        out_shape=jax.ShapeDtypeStruct((N_DEV, 1, n), F32),
        in_specs=[pl.BlockSpec(memory_space=pltpu.VMEM)],
        out_specs=pl.BlockSpec(memory_space=pltpu.VMEM),
        scratch_shapes=[pltpu.SemaphoreType.DMA((N_DEV - 1,)), pltpu.SemaphoreType.DMA((N_DEV - 1,))],
    )(v)


class _Plan:
    def __init__(self, operands, out_shapes, sems, phases):
        self.operands, self.out_shapes, self.sems, self.phases = operands, out_shapes, sems, phases


def _slab_start(base, rows, jump, idx):
    return pl.multiple_of(base + idx * rows + (idx // 4) * jump, 16)


def _gather_plan(shards, dst_of, base_of, dst_shapes, jump_of=None):
    n = len(shards)
    rows = [s.shape[0] for s in shards]
    jump_of = jump_of or [0] * n

    def phases(srcs, dsts, sems):
        send_sems, recv_sems, local_sems = sems
        x, y, c = _place()
        me, sibling = (x, y, c), (x, y, 1 - c)
        chips = [(1 - x, y), (x, 1 - y), (1 - x, 1 - y)]

        def slab(i, px, py, pc):
            start = _slab_start(base_of[i], rows[i], jump_of[i], 4 * px + 2 * py + pc)
            return dsts[dst_of[i]].at[pl.ds(start, rows[i])]

        def copy(i, k, block, to, src=None):
            return pltpu.make_async_remote_copy(
                src_ref=slab(i, *block) if src is None else src, dst_ref=slab(i, *block),
                send_sem=send_sems.at[i, k], recv_sem=recv_sems.at[i, k],
                device_id=to, device_id_type=MESH)

        def mine():
            return [pltpu.make_async_copy(srcs[i], slab(i, *me), local_sems.at[i]) for i in range(n)]

        def first():
            out = []
            for i in range(n):
                out.append(copy(i, 0, me, sibling, src=srcs[i]))
                out += [copy(i, 1 + j, me, (*chip, c), src=srcs[i]) for j, chip in enumerate(chips)]
            return out

        def passed():
            return [(copy(i, 1 + j, (*chip, c), me), copy(i, 4 + j, (*chip, c), sibling))
                    for j, chip in enumerate(chips) for i in range(n)]

        def start():
            for cp in mine() + first():
                cp.start()

        def middle():
            for landed, onward in passed():
                landed.wait_recv()
                onward.start()

        def finish():
            for i in range(n):
                copy(i, 0, sibling, me).wait_recv()
                for j, chip in enumerate(chips):
                    copy(i, 4 + j, (*chip, 1 - c), me).wait_recv()
            for cp in first() + [onward for _, onward in passed()]:
                cp.wait_send()
            for cp in mine():
                cp.wait()

        return start, middle, finish

    sems = [pltpu.SemaphoreType.DMA((n, 7)), pltpu.SemaphoreType.DMA((n, 7)), pltpu.SemaphoreType.DMA((n,))]
    return _Plan(list(shards), [jax.ShapeDtypeStruct(s, BF16) for s in dst_shapes], sems, phases)


def _scatter_plan(grads, src_of, base_of, rows, cols, jump_of=None):
    n = len(rows)
    jump_of = jump_of or [0] * n

    def phases(srcs, recvs, sems):
        send_sems, recv_sems, local_sems = sems
        x, y, c = _place()
        me = 4 * x + 2 * y + c

        def slab(i, idx):
            start = _slab_start(base_of[i], rows[i], jump_of[i], idx)
            return srcs[src_of[i]].at[pl.ds(start, rows[i])]

        def copies():
            out = [pltpu.make_async_copy(slab(i, me), recvs[i].at[me], local_sems.at[i]) for i in range(n)]
            for k in range(1, N_DEV):
                px, py, pc = _flip(x, (k >> 2) & 1), _flip(y, (k >> 1) & 1), _flip(c, k & 1)
                out += [pltpu.make_async_remote_copy(
                    src_ref=slab(i, 4 * px + 2 * py + pc), dst_ref=recvs[i].at[me],
                    send_sem=send_sems.at[i, k - 1], recv_sem=recv_sems.at[i, k - 1],
                    device_id=(px, py, pc), device_id_type=MESH) for i in range(n)]
            return out

        def start():
            for cp in copies():
                cp.start()

        def finish():
            for cp in copies():
                cp.wait()

        return start, None, finish

    sems = [pltpu.SemaphoreType.DMA((n, 7)), pltpu.SemaphoreType.DMA((n, 7)), pltpu.SemaphoreType.DMA((n,))]
    out_shapes = [jax.ShapeDtypeStruct((N_DEV, rows[i], cols[i]), BF16) for i in range(n)]
    return _Plan(list(grads), out_shapes, sems, phases)


def _run_plan(name, plan):
    n_in, n_out = len(plan.operands), len(plan.out_shapes)

    def body(*refs):
        for phase in plan.phases(refs[:n_in], refs[n_in:n_in + n_out], refs[n_in + n_out:]):
            if phase is not None:
                phase()

    hbm = pl.BlockSpec(memory_space=pltpu.HBM)
    return pl.pallas_call(
        body, name=name, out_shape=plan.out_shapes,
        in_specs=[hbm] * n_in, out_specs=[hbm] * n_out, scratch_shapes=plan.sems,
    )(*plan.operands)


def _run_plan_on_sequencer(name, plan, collective_id):
    src_refs = [jax.new_ref(a, memory_space=pltpu.MemorySpace.HBM) for a in plan.operands]
    dst_refs = [jax.empty_ref(s, memory_space=pltpu.MemorySpace.HBM) for s in plan.out_shapes]

    @pl.kernel(mesh=plsc.ScalarSubcoreMesh(axis_name="sequencer", num_cores=1), name=name,
               scratch_types=tuple(plan.sems),
               compiler_params=pltpu.CompilerParams(collective_id=collective_id))
    def launch(*sems):
        x, y, c = _place()
        barrier = pltpu.get_barrier_semaphore()
        for k in range(1, N_DEV):
            peer = (_flip(x, (k >> 2) & 1), _flip(y, (k >> 1) & 1), _flip(c, k & 1))
            pl.semaphore_signal(barrier, inc=1, device_id=peer, device_id_type=MESH)
        pl.semaphore_wait(barrier, N_DEV - 1)
        for phase in plan.phases(src_refs, dst_refs, sems):
            if phase is not None:
                phase()

    launch()
    return [r[...] for r in dst_refs]


def _sum_contributions(name, recv):
    _, rows, cols = recv.shape
    tr = rows if rows <= 512 else 304 if rows % 304 == 0 else 256

    def body(r_ref, o_ref):
        acc = r_ref[0].astype(F32)
        for k in range(1, N_DEV):
            acc = acc + r_ref[k].astype(F32)
        o_ref[...] = acc

    return pl.pallas_call(
        body, name=name, grid=(rows // tr,),
        out_shape=jax.ShapeDtypeStruct((rows, cols), F32),
        in_specs=[pl.BlockSpec((N_DEV, tr, cols), lambda i: (0, i, 0))],
        out_specs=pl.BlockSpec((tr, cols), lambda i: (i, 0)),
        compiler_params=_params("parallel"),
    )(recv)


def _mm(name, a, b, mode, out_dtype, tm, tn, tk, *, carry=None, tiles_in=(), tiles_out=(), epilogue=None,
        n_outer=False, keep_b=False, col_chunks=None):
    if mode == "TN":
        kk, m = a.shape
    else:
        m, kk = a.shape
    n = b.shape[0] if mode == "NT" else b.shape[1]
    tm, tn, tk = min(tm, m), min(tn, n), min(tk, kk)
    assert m % tm == 0 and n % tn == 0 and kk % tk == 0, (name, m, n, kk, tm, tn, tk)
    ni, nj, nk = m // tm, n // tn, kk // tk
    steps = ni * nj * nk
    dims = {"NN": NN_DIMS, "NT": NT_DIMS, "TN": TN_DIMS}[mode]
    if epilogue is None:
        tiles_out = [(jax.ShapeDtypeStruct((m, n), out_dtype), (tm, tn), lambda i, j: (i, j))]
    n_tin, n_tout = len(tiles_in), len(tiles_out)
    n_in = len(carry.operands) if carry else 0
    n_out = len(carry.out_shapes) if carry else 0
    n_acc = 1 if nk > 1 else 0
    n_keep = 2 if keep_b else 0
    assert not carry or steps >= 3
    assert not keep_b or (nk == 1 and nj == 1)
    assert not col_chunks or (epilogue is not None and nk == 1 and mode != "TN")
    ij = (lambda p, q: (q, p)) if n_outer else (lambda p, q: (p, q))
    inner = ni if n_outer else nj

    def body(a_ref, b_ref, *rest):
        tin = rest[:n_tin]
        cin = rest[n_tin:n_tin + n_in]
        tout = rest[n_tin + n_in:n_tin + n_in + n_tout]
        cout = rest[n_tin + n_in + n_tout:n_tin + n_in + n_tout + n_out]
        scratch = rest[n_tin + n_in + n_tout + n_out:]
        k = pl.program_id(2)
        visit = pl.program_id(0) * inner + pl.program_id(1)
        step = visit * nk + k
        if keep_b:
            b_kept, b_sem = scratch[n_acc:n_acc + 2]

            @pl.when(step == 0)
            def _():
                cp = pltpu.make_async_copy(b_ref, b_kept, b_sem)
                cp.start()
                cp.wait()

            b_ref = b_kept
        if carry:
            start, middle, finish = carry.phases(cin, cout, scratch[n_acc + n_keep:])
            pl.when(step == 0)(start)

        def store(prod, c=0, cols=()):
            if epilogue is None:
                tout[0][...] = prod.astype(out_dtype)
            else:
                epilogue(prod, jnp.logical_and(visit == 0, c == 0), tin, tout, *cols)

        if col_chunks:
            for c, (c0, cw) in enumerate(col_chunks):
                b_part = b_ref[pl.ds(c0, cw), :] if mode == "NT" else b_ref[:, pl.ds(c0, cw)]
                store(lax.dot_general(a_ref[...], b_part, dims, preferred_element_type=F32), c, ((c0, cw),))
        else:
            part = lax.dot_general(a_ref[...], b_ref[...], dims, preferred_element_type=F32)
            if nk == 1:
                store(part)
            else:
                acc_ref = scratch[0]

                @pl.when(k == 0)
                def _():
                    acc_ref[...] = part

                @pl.when((k > 0) & (k < nk - 1))
                def _():
                    acc_ref[...] += part

                @pl.when(k == nk - 1)
                def _():
                    store(acc_ref[...] + part)

        if carry:
            if middle is not None:
                pl.when(step == (steps * 3) // 5)(middle)
            pl.when(step == steps - 1)(finish)

    def spec(shape, fn):
        return pl.BlockSpec(shape, lambda p, q, k: fn(*ij(p, q)))

    a_spec = (pl.BlockSpec((tk, tm), lambda p, q, k: (k, ij(p, q)[0])) if mode == "TN"
              else pl.BlockSpec((tm, tk), lambda p, q, k: (ij(p, q)[0], k)))
    if keep_b:
        b_spec = pl.BlockSpec(memory_space=pl.ANY)
    elif mode == "NT":
        b_spec = pl.BlockSpec((tn, tk), lambda p, q, k: (ij(p, q)[1], k))
    else:
        b_spec = pl.BlockSpec((tk, tn), lambda p, q, k: (k, ij(p, q)[1]))
    hbm = pl.BlockSpec(memory_space=pltpu.HBM)
    sequential = carry or epilogue or keep_b
    out = pl.pallas_call(
        body, name=name, grid=(nj, ni, nk) if n_outer else (ni, nj, nk),
        out_shape=[t[0] for t in tiles_out] + (carry.out_shapes if carry else []),
        in_specs=[a_spec, b_spec] + [spec(t[1], t[2]) for t in tiles_in] + [hbm] * n_in,
        out_specs=[spec(t[1], t[2]) for t in tiles_out] + [hbm] * n_out,
        scratch_shapes=([pltpu.VMEM((tm, tn), F32)] * n_acc
                        + ([pltpu.VMEM(b.shape, b.dtype), pltpu.SemaphoreType.DMA] if keep_b else [])
                        + (carry.sems if carry else [])),
        compiler_params=(_params("arbitrary", "arbitrary", "arbitrary") if sequential
                         else _params("parallel", "parallel", "arbitrary")),
    )(a, b, *[t[0] for t in tiles_in], *(carry.operands if carry else []))
    return out if (carry or epilogue) else out[0]


def _row(tm, w, off=0):
    return pl.BlockSpec((tm, w), lambda i: (i, off))


def _vec(w):
    return pl.BlockSpec((1, w), lambda i: (0, 0))


def _sigmoid(x):
    return 0.5 * jnp.tanh(0.5 * x) + 0.5


def _normmod(name, x, g, sc, sh, tm=512):
    s = x.shape[0]

    def body(x_ref, g_ref, sc_ref, sh_ref, h_ref):
        xv = x_ref[...]
        r = lax.rsqrt(jnp.mean(xv * xv, axis=-1, keepdims=True) + EPS)
        h_ref[...] = ((xv * r) * g_ref[...] * (1.0 + sc_ref[...]) + sh_ref[...]).astype(BF16)

    return pl.pallas_call(
        body, name=name, grid=(s // tm,),
        out_shape=jax.ShapeDtypeStruct((s, D), BF16),
        in_specs=[_row(tm, D), _vec(D), _vec(D), _vec(D)], out_specs=_row(tm, D),
        compiler_params=_params("parallel"),
    )(x, g, sc, sh)


def _normmod_bwd(name, dh, x, gin, g, sc, sh, tm=512):
    s = x.shape[0]

    def body(dh_ref, x_ref, gin_ref, g_ref, sc_ref, sh_ref, gout_ref, acc_ref):
        xv, dhv = x_ref[...], dh_ref[...]
        r = lax.rsqrt(jnp.mean(xv * xv, axis=-1, keepdims=True) + EPS)
        nv = xv * r
        gv, one_sc = g_ref[...], 1.0 + sc_ref[...]
        dn = dhv * gv * one_sc
        dx = r * (dn - nv * jnp.mean(dn * nv, axis=-1, keepdims=True))
        gout_ref[...] = gin_ref[...] + dx

        @pl.when(pl.program_id(0) == 0)
        def _():
            acc_ref[...] = jnp.zeros_like(acc_ref)

        dhn = dhv * nv
        acc_ref[0:1, :] += jnp.sum(dhv, axis=0, keepdims=True)
        acc_ref[1:2, :] += jnp.sum(dhn * gv, axis=0, keepdims=True)
        acc_ref[2:3, :] += jnp.sum(dhn * one_sc, axis=0, keepdims=True)

    return pl.pallas_call(
        body, name=name, grid=(s // tm,),
        out_shape=[jax.ShapeDtypeStruct((s, D), F32), jax.ShapeDtypeStruct((8, D), F32)],
        in_specs=[_row(tm, D), _row(tm, D), _row(tm, D), _vec(D), _vec(D), _vec(D)],
        out_specs=[_row(tm, D), pl.BlockSpec((8, D), lambda i: (0, 0))],
        compiler_params=_params("arbitrary"),
    )(dh, x, gin, g, sc, sh)


def _swiglu(name, ab, tm=512):
    s = ab.shape[0]

    def body(ab_ref, s_ref):
        a = ab_ref[:, :FF].astype(F32)
        b = ab_ref[:, FF:].astype(F32)
        s_ref[...] = (a * _sigmoid(a) * b).astype(BF16)

    return pl.pallas_call(
        body, name=name, grid=(s // tm,),
        out_shape=jax.ShapeDtypeStruct((s, FF), BF16),
        in_specs=[_row(tm, 2 * FF)], out_specs=_row(tm, FF),
        compiler_params=_params("parallel"),
    )(ab)


def _swiglu_bwd(name, ds, ab, tm=256):
    s = ab.shape[0]

    def body(ds_ref, ab_ref, dab_ref):
        a = ab_ref[:, :FF].astype(F32)
        b = ab_ref[:, FF:].astype(F32)
        dsv = ds_ref[...].astype(F32)
        sig = _sigmoid(a)
        dab_ref[:, :FF] = (dsv * b * (sig * (1.0 + a * (1.0 - sig)))).astype(BF16)
        dab_ref[:, FF:] = (dsv * (a * sig)).astype(BF16)

    return pl.pallas_call(
        body, name=name, grid=(s // tm,),
        out_shape=jax.ShapeDtypeStruct((s, 2 * FF), BF16),
        in_specs=[_row(tm, FF), _row(tm, 2 * FF)], out_specs=_row(tm, 2 * FF),
        compiler_params=_params("parallel"),
    )(ds, ab)


def _residual(name, x, f, gt, coef, tm=512):
    s = x.shape[0]

    def body(x_ref, f_ref, gt_ref, o_ref):
        o_ref[...] = x_ref[...] + (coef * gt_ref[...]) * f_ref[...]

    return pl.pallas_call(
        body, name=name, grid=(s // tm,),
        out_shape=jax.ShapeDtypeStruct((s, D), F32),
        in_specs=[_row(tm, D), _row(tm, D), _vec(D)], out_specs=_row(tm, D),
        compiler_params=_params("parallel"),
    )(x, f, gt)


def _gate_bwd(name, gin, f, gt, coef, tm=512):
    s = gin.shape[0]

    def body(g_ref, f_ref, gt_ref, df_ref, acc_ref):
        gv = g_ref[...]
        df_ref[...] = ((coef * gt_ref[...]) * gv).astype(BF16)

        @pl.when(pl.program_id(0) == 0)
        def _():
            acc_ref[...] = jnp.zeros_like(acc_ref)

        acc_ref[0:1, :] += coef * jnp.sum(gv * f_ref[...], axis=0, keepdims=True)

    return pl.pallas_call(
        body, name=name, grid=(s // tm,),
        out_shape=[jax.ShapeDtypeStruct((s, D), BF16), jax.ShapeDtypeStruct((8, D), F32)],
        in_specs=[_row(tm, D), _row(tm, D), _vec(D)],
        out_specs=[_row(tm, D), pl.BlockSpec((8, D), lambda i: (0, 0))],
        compiler_params=_params("arbitrary"),
    )(gin, f, gt)


def _loss_grad(x3, target, tm=512):
    s = x3.shape[0]

    def body(y_ref, t_ref, g_ref, l_ref):
        e = y_ref[...] - t_ref[...]
        g_ref[...] = e * (1.0 / D)

        @pl.when(pl.program_id(0) == 0)
        def _():
            l_ref[...] = jnp.zeros_like(l_ref)

        l_ref[...] += jnp.sum(jnp.mean(e * e, axis=-1, keepdims=True), axis=0, keepdims=True) * 0.5

    return pl.pallas_call(
        body, name="loss_grad", grid=(s // tm,),
        out_shape=[jax.ShapeDtypeStruct((s, D), F32), jax.ShapeDtypeStruct((8, 128), F32)],
        in_specs=[_row(tm, D), _row(tm, D)],
        out_specs=[_row(tm, D), pl.BlockSpec((8, 128), lambda i: (0, 0))],
        compiler_params=_params("arbitrary"),
    )(x3, target)


def _heads(x, fn):
    return jnp.concatenate([fn(x[:, h * HD:(h + 1) * HD], h) for h in range(COL // HD)], axis=1)


def _qknorm(proj, wqk, tm=1024):
    s = proj.shape[0]

    def body(p_ref, w_ref, o_ref):
        pv = p_ref[...].astype(F32)
        wv = w_ref[...]

        def one(qh, h):
            r = lax.rsqrt(jnp.mean(qh * qh, axis=-1, keepdims=True) + EPS)
            return (qh * r) * wv[:, h * HD:(h + 1) * HD]

        o_ref[...] = _heads(pv, one).astype(BF16)

    return pl.pallas_call(
        body, name="qknorm", grid=(s // tm, QKW // COL),
        out_shape=jax.ShapeDtypeStruct((s, QKW), BF16),
        in_specs=[pl.BlockSpec((tm, COL), lambda i, j: (i, j)), pl.BlockSpec((1, COL), lambda i, j: (0, j))],
        out_specs=pl.BlockSpec((tm, COL), lambda i, j: (i, j)),
        compiler_params=_params("parallel", "parallel"),
    )(proj, wqk)


def _qknorm_bwd(name, proj, dn, w, dproj, blk0, tm=1024):
    s = proj.shape[0]
    nblk = dn.shape[1] // COL

    def body(p_ref, d_ref, w_ref, _, o_ref, acc_ref):
        pv = p_ref[...].astype(F32)
        dv = d_ref[...]
        wv = w_ref[...]
        sums = []

        def one(qh, h):
            dn = dv[:, h * HD:(h + 1) * HD]
            r = lax.rsqrt(jnp.mean(qh * qh, axis=-1, keepdims=True) + EPS)
            nh = qh * r
            sums.append(jnp.sum(dn * nh, axis=0, keepdims=True))
            dnw = dn * wv[:, h * HD:(h + 1) * HD]
            return r * (dnw - nh * jnp.mean(dnw * nh, axis=-1, keepdims=True))

        o_ref[...] = _heads(pv, one).astype(BF16)

        @pl.when(pl.program_id(1) == 0)
        def _():
            acc_ref[...] = jnp.zeros_like(acc_ref)

        acc_ref[0:1, :] += jnp.concatenate(sums, axis=1)

    return pl.pallas_call(
        body, name=name, grid=(nblk, s // tm),
        out_shape=[jax.ShapeDtypeStruct((s, IN_W), BF16), jax.ShapeDtypeStruct((8, nblk * COL), F32)],
        in_specs=[pl.BlockSpec((tm, COL), lambda j, i: (i, blk0 + j)), pl.BlockSpec((tm, COL), lambda j, i: (i, j)),
                  pl.BlockSpec((1, COL), lambda j, i: (0, j)), pl.BlockSpec(memory_space=pl.ANY)],
        out_specs=[pl.BlockSpec((tm, COL), lambda j, i: (i, blk0 + j)),
                   pl.BlockSpec((8, COL), lambda j, i: (0, j))],
        input_output_aliases={3: 0},
        compiler_params=_params("arbitrary", "arbitrary"),
    )(proj, dn, w, dproj)


def _attn_shapes(s, g):
    d = DILATIONS[g]
    tb = min(s, max(2048, 256 * d))
    sb = min(256, tb // d)
    pb = BAND * d
    assert s % tb == 0 and tb % pb == 0 and (tb // d) % sb == 0 and sb % BAND == 0
    return d, tb, sb, pb


def _lanes(x, width):
    return jnp.concatenate([x] * (width // HD), axis=1)


def _every(start, size, d):
    return pl.ds(start, size, stride=d) if d > 1 else pl.ds(start, size)


def _attn_specs(g, tb, pb, s, ahead):
    ratio = tb // pb
    if ahead:
        nbr = lambda n: jnp.minimum((n + 1) * ratio, s // pb - 1)
    else:
        nbr = lambda n: jnp.maximum(n * ratio - 1, 0)
    cur = lambda base: pl.BlockSpec((tb, HD), lambda h, n: (n, base + g * N_HEADS + h))
    side = lambda base: pl.BlockSpec((pb, HD), lambda h, n: (nbr(n), base + g * N_HEADS + h))
    tok = pl.BlockSpec((tb, HD), lambda h, n: (n, h))
    tok_side = pl.BlockSpec((pb, HD), lambda h, n: (nbr(n), h))
    return cur, side, tok, tok_side


Q_COL, K_COL, V_COL = 0, 12, 24


def _attn_fwd(g, qkn, proj):
    s = qkn.shape[0]
    d, tb, sb, pb = _attn_shapes(s, g)
    ft = F32 if d > 1 else BF16
    nj = tb // d // sb
    scale = HD ** -0.5

    def body(q_ref, kc_ref, kp_ref, vc_ref, vp_ref, o_ref, lse_ref, qf, kf, vf):
        n = pl.program_id(1)
        qf[...] = q_ref[...].astype(ft)
        kf[0:pb] = kp_ref[...].astype(ft)
        kf[pb:] = kc_ref[...].astype(ft)
        vf[0:pb] = vp_ref[...].astype(ft)
        vf[pb:] = vc_ref[...].astype(ft)
        for r in range(d):
            for j in range(nj):
                at = j * sb * d + r
                q = qf[_every(at, sb, d), :].astype(BF16)
                k = kf[_every(at, sb + BAND, d), :].astype(BF16)
                v = vf[_every(at, sb + BAND, d), :].astype(BF16)
                sc = lax.dot_general(q, k, NT_DIMS, preferred_element_type=F32) * scale
                qi = lax.broadcasted_iota(jnp.int32, sc.shape, 0)
                kj = lax.broadcasted_iota(jnp.int32, sc.shape, 1)
                valid = (kj >= qi) & (kj <= qi + BAND)
                if j == 0:
                    valid = valid & ((kj >= BAND) | (n > 0))
                sc = jnp.where(valid, sc, -1e30)
                m = jnp.max(sc, axis=-1, keepdims=True)
                p = jnp.exp(sc - m)
                l = jnp.sum(p, axis=-1, keepdims=True)
                o = lax.dot_general(p.astype(BF16), v, NN_DIMS, preferred_element_type=F32)
                o_ref[_every(at, sb, d), :] = o / l
                lse_ref[_every(at, sb, d), :] = jnp.broadcast_to(m + jnp.log(l), (sb, HD))

    cur, side, tok, _ = _attn_specs(g, tb, pb, s, ahead=False)
    return pl.pallas_call(
        body, name=f"attn_fwd_g{g}", grid=(N_HEADS, s // tb),
        out_shape=[jax.ShapeDtypeStruct((s, COL), F32)] * 2,
        in_specs=[cur(Q_COL), cur(K_COL), side(K_COL), cur(V_COL), side(V_COL)],
        out_specs=[tok, tok],
        scratch_shapes=[pltpu.VMEM((tb, HD), ft), pltpu.VMEM((tb + pb, HD), ft),
                        pltpu.VMEM((tb + pb, HD), ft)],
        compiler_params=_params("parallel", "arbitrary"),
    )(qkn, qkn, qkn, proj, proj)


def _attn_combine(os_, lses, tm=512):
    s = os_[0].shape[0]

    def body(o0, o1, o2, l0, l1, l2, o_ref, lse_ref):
        a, b, c = l0[...], l1[...], l2[...]
        m = jnp.maximum(jnp.maximum(a, b), c)
        ea, eb, ec = jnp.exp(a - m), jnp.exp(b - m), jnp.exp(c - m)
        tot = ea + eb + ec
        o_ref[...] = ((ea * o0[...] + eb * o1[...] + ec * o2[...]) / tot).astype(BF16)
        lse_ref[...] = m + jnp.log(tot)

    return pl.pallas_call(
        body, name="attn_combine", grid=(s // tm,),
        out_shape=[jax.ShapeDtypeStruct((s, COL), BF16), jax.ShapeDtypeStruct((s, COL), F32)],
        in_specs=[_row(tm, COL)] * 6, out_specs=[_row(tm, COL)] * 2,
        compiler_params=_params("parallel"),
    )(*os_, *lses)


def _attn_delta(do, o, tm=512):
    s = do.shape[0]

    def body(do_ref, o_ref, del_ref):
        prod = do_ref[...] * o_ref[...].astype(F32)
        del_ref[...] = _heads(prod, lambda ph, h: jnp.broadcast_to(
            jnp.sum(ph, axis=-1, keepdims=True), ph.shape))

    return pl.pallas_call(
        body, name="attn_delta", grid=(s // tm,),
        out_shape=jax.ShapeDtypeStruct((s, COL), F32),
        in_specs=[_row(tm, COL)] * 2, out_specs=_row(tm, COL),
        compiler_params=_params("parallel"),
    )(do, o)


def _attn_bwd(g, qkn, proj, do, lse, delta, dqn, dkn, dproj):
    s = qkn.shape[0]
    d, tb, sb, pb = _attn_shapes(s, g)
    ft = F32 if d > 1 else BF16
    nj = tb // d // sb
    nt = s // tb
    scale = HD ** -0.5
    chained = dqn is not None

    def body(k_ref, v_ref, qc_ref, qn_ref, doc_ref, don_ref, lc_ref, ln_ref, dc_ref, dn_ref, *rest):
        dq_ref, dk_ref, dv_ref, kf, vf, qf, dvf, later = rest[-8:]
        n = pl.program_id(1)
        kf[...] = k_ref[...].astype(ft)
        vf[...] = v_ref[...].astype(ft)
        qf[0:tb] = qc_ref[...].astype(ft)
        qf[tb:] = qn_ref[...].astype(ft)

        @pl.when(n == 0)
        def _():
            later[...] = jnp.zeros_like(later)

        def window(c_ref, n_ref, r, j):
            at = j * sb * d + r
            if j < nj - 1:
                return c_ref[_every(at, sb + BAND, d), :]
            return jnp.concatenate([c_ref[_every(at, sb, d), :], n_ref[_every(r, BAND, d), :]], axis=0)

        for r in range(d):
            tail = later[r]
            for j in range(nj):
                at = j * sb * d + r
                rows = _every(at, sb, d)
                k = kf[rows, :].astype(BF16)
                v = vf[rows, :].astype(BF16)
                q = qf[_every(at, sb + BAND, d), :].astype(BF16)
                dov = window(doc_ref, don_ref, r, j).astype(BF16)
                sc = lax.dot_general(q, k, NT_DIMS, preferred_element_type=F32) * scale
                qi = lax.broadcasted_iota(jnp.int32, sc.shape, 0)
                kj = lax.broadcasted_iota(jnp.int32, sc.shape, 1)
                valid = (qi >= kj) & (qi <= kj + BAND)
                if j == nj - 1:
                    valid = valid & ((qi < sb) | (n < nt - 1))
                p = jnp.exp(jnp.where(valid, sc - _lanes(window(lc_ref, ln_ref, r, j), sb), -1e30))
                dp = lax.dot_general(dov, v, NT_DIMS, preferred_element_type=F32)
                ds = (p * (dp - _lanes(window(dc_ref, dn_ref, r, j), sb)) * scale).astype(BF16)
                dvf[rows, :] = lax.dot_general(p.astype(BF16), dov, TN_DIMS, preferred_element_type=F32)
                dk_ref[rows, :] = lax.dot_general(ds, q, TN_DIMS, preferred_element_type=F32)
                dqw = lax.dot_general(ds, k, NN_DIMS, preferred_element_type=F32)
                first = dqw[:BAND] + tail
                dq_ref[rows, :] = first if sb == BAND else jnp.concatenate([first, dqw[BAND:sb]], axis=0)
                tail = dqw[sb:]
            later[r] = tail
        dv_ref[...] = dvf[...].astype(BF16)

    cur, side, tok, tok_side = _attn_specs(g, tb, pb, s, ahead=True)
    anyspec = pl.BlockSpec(memory_space=pl.ANY)
    n_heads_cols = 3 * N_HEADS * HD
    return pl.pallas_call(
        body, name=f"attn_bwd_g{g}", grid=(N_HEADS, nt),
        out_shape=[jax.ShapeDtypeStruct((s, n_heads_cols), F32), jax.ShapeDtypeStruct((s, n_heads_cols), F32),
                   jax.ShapeDtypeStruct((s, IN_W), BF16)],
        in_specs=[cur(K_COL), cur(V_COL), cur(Q_COL), side(Q_COL), tok, tok_side, tok, tok_side,
                  tok, tok_side] + ([anyspec, anyspec] if chained else []) + [anyspec],
        out_specs=[cur(0), cur(0), cur(V_COL)],
        input_output_aliases={10: 0, 11: 1, 12: 2} if chained else {10: 2},
        scratch_shapes=[pltpu.VMEM((tb, HD), ft), pltpu.VMEM((tb, HD), ft),
                        pltpu.VMEM((tb + pb, HD), ft), pltpu.VMEM((tb, HD), F32),
                        pltpu.VMEM((d, BAND, HD), F32)],
        compiler_params=_params("arbitrary", "arbitrary"),
    )(qkn, proj, qkn, qkn, do, do, lse, lse, delta, delta, *([dqn, dkn] if chained else []), dproj)


def _shift_down(x, before, k):
    rolled = pltpu.roll(x, k, 0)
    head = jnp.where(lax.broadcasted_iota(jnp.int32, before.shape, 0) < k, pltpu.roll(before, k, 0), rolled[:8])
    return jnp.concatenate([head, rolled[8:]], axis=0)


def _shift_up(x, after, k):
    rows = x.shape[0]
    rolled = pltpu.roll(x, rows - k, 0)
    tail = jnp.where(lax.broadcasted_iota(jnp.int32, after.shape, 0) >= 8 - k,
                     pltpu.roll(after, 8 - k, 0), rolled[rows - 8:])
    return jnp.concatenate([rolled[:rows - 8], tail], axis=0)


def _conv_fwd(proj, cw, tm=1024):
    s = proj.shape[0]
    r16 = tm // 16

    def body(u_ref, b_ref, c_ref, up_ref, cp_ref, w_ref, z_ref):
        i = pl.program_id(1)
        xc = c_ref[...].astype(F32) * u_ref[...].astype(F32)
        xp = jnp.where(i > 0, cp_ref[8:16, :].astype(F32) * up_ref[8:16, :].astype(F32), 0.0)
        w = w_ref[...]
        conv = _shift_down(xc, xp, 2) * w[0:1] + _shift_down(xc, xp, 1) * w[1:2] + xc * w[2:3]
        z_ref[...] = (b_ref[...].astype(F32) * conv).astype(BF16)

    tile = lambda blk: pl.BlockSpec((tm, COL), lambda j, i: (i, blk + j))
    before = lambda blk: pl.BlockSpec((16, COL), lambda j, i: (jnp.maximum(i * r16 - 1, 0), blk + j))
    return pl.pallas_call(
        body, name="conv_fwd", grid=(D // COL, s // tm),
        out_shape=jax.ShapeDtypeStruct((s, D), BF16),
        in_specs=[tile(U_BLK), tile(B_BLK), tile(C_BLK), before(U_BLK), before(C_BLK),
                  pl.BlockSpec((3, COL), lambda j, i: (0, j))],
        out_specs=pl.BlockSpec((tm, COL), lambda j, i: (i, j)),
        compiler_params=_params("parallel", "parallel"),
    )(proj, proj, proj, proj, proj, cw)


def _conv_bwd(dz, proj, cw, dproj, tm=1024):
    s = proj.shape[0]
    r8, r16 = tm // 8, tm // 16
    nrow = s // tm

    def body(dz_ref, u_ref, b_ref, c_ref, up_ref, cp_ref, dzn_ref, bn_ref, w_ref, _, o_ref, acc_ref):
        piece, i = pl.program_id(1), pl.program_id(2)
        u, c = u_ref[...].astype(F32), c_ref[...].astype(F32)
        bv = b_ref[...].astype(F32)
        dzv = dz_ref[...]
        w = w_ref[...]

        @pl.when((piece == 0) & (i == 0))
        def _():
            acc_ref[...] = jnp.zeros_like(acc_ref)

        @pl.when(piece == 1)
        def _():
            xc = c * u
            xp = jnp.where(i > 0, cp_ref[8:16, :].astype(F32) * up_ref[8:16, :].astype(F32), 0.0)
            x2, x1 = _shift_down(xc, xp, 2), _shift_down(xc, xp, 1)
            o_ref[...] = (dzv * (x2 * w[0:1] + x1 * w[1:2] + xc * w[2:3])).astype(BF16)
            dconv = dzv * bv
            acc_ref[0:1, :] += jnp.sum(dconv * x2, axis=0, keepdims=True)
            acc_ref[1:2, :] += jnp.sum(dconv * x1, axis=0, keepdims=True)
            acc_ref[2:3, :] += jnp.sum(dconv * xc, axis=0, keepdims=True)

        @pl.when(piece != 1)
        def _():
            dconv = dzv * bv
            dn = jnp.where(i < nrow - 1, dzn_ref[...] * bn_ref[0:8, :].astype(F32), 0.0)
            dxc = dconv * w[2:3] + _shift_up(dconv, dn, 1) * w[1:2] + _shift_up(dconv, dn, 2) * w[0:1]
            o_ref[...] = (dxc * jnp.where(piece == 0, c, u)).astype(BF16)

    tile = lambda blk: pl.BlockSpec((tm, COL), lambda j, p, i: (i, blk + j))
    before = lambda blk: pl.BlockSpec((16, COL), lambda j, p, i: (jnp.maximum(i * r16 - 1, 0), blk + j))
    after = lambda rows, blk: pl.BlockSpec(
        (rows, COL), lambda j, p, i: (jnp.minimum((i + 1) * (tm // rows), s // rows - 1), blk + j))
    return pl.pallas_call(
        body, name="conv_bwd", grid=(D // COL, 3, nrow),
        out_shape=[jax.ShapeDtypeStruct((s, IN_W), BF16), jax.ShapeDtypeStruct((8, D), F32)],
        in_specs=[tile(0), tile(U_BLK), tile(B_BLK), tile(C_BLK), before(U_BLK), before(C_BLK),
                  after(8, 0), after(16, B_BLK), pl.BlockSpec((3, COL), lambda j, p, i: (0, j)),
                  pl.BlockSpec(memory_space=pl.ANY)],
        out_specs=[pl.BlockSpec((tm, COL), lambda j, p, i: (i, U_BLK + 2 * p + j)),
                   pl.BlockSpec((8, COL), lambda j, p, i: (0, j))],
        input_output_aliases={9: 0},
        compiler_params=_params("arbitrary", "arbitrary", "arbitrary"),
    )(dz, proj, proj, proj, proj, proj, dz, proj, cw, dproj)


def _merge_fwd(ya, yc, proj, tm=512):
    s = proj.shape[0]

    def body(ya_ref, yc_ref, ga_ref, gc_ref, o_ref):
        o_ref[...] = (_sigmoid(ga_ref[...].astype(F32)) * ya_ref[...].astype(F32)
                      + _sigmoid(gc_ref[...].astype(F32)) * yc_ref[...].astype(F32)).astype(BF16)

    tile = lambda blk: pl.BlockSpec((tm, COL), lambda j, i: (i, blk + j))
    return pl.pallas_call(
        body, name="merge_fwd", grid=(D // COL, s // tm),
        out_shape=jax.ShapeDtypeStruct((s, D), BF16),
        in_specs=[tile(0), tile(0), tile(GA_BLK), tile(GC_BLK)], out_specs=tile(0),
        compiler_params=_params("parallel", "parallel"),
    )(ya, yc, proj, proj)


def _merge_bwd_branches(dm, proj, tm=512):
    s = proj.shape[0]

    def body(dm_ref, ga_ref, gc_ref, dya_ref, dyc_ref):
        dmv = dm_ref[...]
        dya_ref[...] = (dmv * _sigmoid(ga_ref[...].astype(F32))).astype(BF16)
        dyc_ref[...] = (dmv * _sigmoid(gc_ref[...].astype(F32))).astype(BF16)

    tile = lambda blk: pl.BlockSpec((tm, COL), lambda j, i: (i, blk + j))
    return pl.pallas_call(
        body, name="merge_bwd_branches", grid=(D // COL, s // tm),
        out_shape=[jax.ShapeDtypeStruct((s, D), BF16)] * 2,
        in_specs=[tile(0), tile(GA_BLK), tile(GC_BLK)], out_specs=[tile(0)] * 2,
        compiler_params=_params("parallel", "parallel"),
    )(dm, proj, proj)


def _merge_bwd_gates(dm, ya, yc, proj, tm=1024):
    s = proj.shape[0]
    half = D // COL

    def body(dm_ref, ya_ref, yc_ref, g_ref, o_ref):
        y = jnp.where(pl.program_id(0) < half, ya_ref[...].astype(F32), yc_ref[...].astype(F32))
        sig = _sigmoid(g_ref[...].astype(F32))
        o_ref[...] = (dm_ref[...] * y * sig * (1.0 - sig)).astype(BF16)

    chan = pl.BlockSpec((tm, COL), lambda jj, i: (i, jj % half))
    gate = pl.BlockSpec((tm, COL), lambda jj, i: (i, GA_BLK + jj))
    return pl.pallas_call(
        body, name="merge_bwd_gates", grid=(2 * half, s // tm),
        out_shape=jax.ShapeDtypeStruct((s, IN_W), BF16),
        in_specs=[chan, chan, chan, gate], out_specs=gate,
        compiler_params=_params("parallel", "parallel"),
    )(dm, ya, yc, proj)


def _mod_part(c_all, w_ada, b_part):
    def body(c_ref, w_ref, b_ref, o_ref):
        cv = c_ref[...]
        act = cv * _sigmoid(cv)
        o_ref[...] = jnp.dot(act, w_ref[...], preferred_element_type=F32,
                             precision=lax.Precision.HIGHEST) + b_ref[...]

    return pl.pallas_call(
        body, name="mod_part", out_shape=jax.ShapeDtypeStruct((N_DEV, w_ada.shape[1]), F32),
    )(c_all, w_ada, b_part)


def _w_ada_grad(c_all_t, dmod_part):
    def body(c_ref, d_ref, o_ref):
        cv = c_ref[...]
        act = cv * _sigmoid(cv)
        dv = d_ref[...]
        acc = act[:, 0:1] * dv[0:1, :]
        for b in range(1, N_DEV):
            acc = acc + act[:, b:b + 1] * dv[b:b + 1, :]
        o_ref[...] = acc

    return pl.pallas_call(
        body, name="w_ada_grad", out_shape=jax.ShapeDtypeStruct((D, dmod_part.shape[1]), F32),
    )(c_all_t, dmod_part)


def _sum_rows(name, v):
    def body(v_ref, o_ref):
        acc = v_ref[0]
        for k in range(1, N_DEV):
            acc = acc + v_ref[k]
        o_ref[...] = acc

    return pl.pallas_call(body, name=name, out_shape=jax.ShapeDtypeStruct(v.shape[1:], F32))(v)


def _adamw(name, w, g, m, v):
    rows, cols = w.shape
    limit = max(16, (1 << 20) // (4 * cols))
    tr = rows if rows <= limit else next((t for t in range(limit - limit % 16, 15, -16) if rows % t == 0), rows)
    c1 = 1.0 - ADAM_B1 ** ADAM_STEP
    c2 = 1.0 - ADAM_B2 ** ADAM_STEP
    parts = g.ndim == 3

    def body(w_ref, g_ref, m_ref, v_ref, go_ref, d_ref, nm_ref, nv_ref):
        if parts:
            gv = g_ref[0].astype(F32)
            for k in range(1, N_DEV):
                gv = gv + g_ref[k].astype(F32)
        else:
            gv = g_ref[...]
        go_ref[...] = gv
        nm = ADAM_B1 * m_ref[...] + (1.0 - ADAM_B1) * gv
        nv = ADAM_B2 * v_ref[...] + (1.0 - ADAM_B2) * (gv * gv)
        nm_ref[...] = nm
        nv_ref[...] = nv
        d_ref[...] = -ADAM_LR * ((nm / c1) / (jnp.sqrt(nv / c2) + ADAM_EPS) + ADAM_WD * w_ref[...])

    spec = pl.BlockSpec((tr, cols), lambda i: (i, 0))
    g_spec = pl.BlockSpec((N_DEV, tr, cols), lambda i: (0, i, 0)) if parts else spec
    return pl.pallas_call(
        body, name=name, grid=(rows // tr,),
        out_shape=[jax.ShapeDtypeStruct((rows, cols), F32)] * 4,
        in_specs=[spec, g_spec, spec, spec], out_specs=[spec] * 4,
        compiler_params=_params("parallel"),
    )(w, g, m, v)


def _adamw_small(ws, gs, ms, vs):
    n = len(ws)
    c1 = 1.0 - ADAM_B1 ** ADAM_STEP
    c2 = 1.0 - ADAM_B2 ** ADAM_STEP

    def body(*refs):
        for i in range(n):
            w_ref, g_ref, m_ref, v_ref = refs[i], refs[n + i], refs[2 * n + i], refs[3 * n + i]
            d_ref, nm_ref, nv_ref = refs[4 * n + 3 * i:4 * n + 3 * i + 3]
            gv = g_ref[...]
            nm = ADAM_B1 * m_ref[...] + (1.0 - ADAM_B1) * gv
            nv = ADAM_B2 * v_ref[...] + (1.0 - ADAM_B2) * (gv * gv)
            nm_ref[...] = nm
            nv_ref[...] = nv
            d_ref[...] = -ADAM_LR * ((nm / c1) / (jnp.sqrt(nv / c2) + ADAM_EPS) + ADAM_WD * w_ref[...])

    outs = pl.pallas_call(
        body, name="adamw_small",
        out_shape=[jax.ShapeDtypeStruct(w.shape, F32) for w in ws for _ in range(3)],
    )(*ws, *gs, *ms, *vs)
    return [tuple(outs[3 * i:3 * i + 3]) for i in range(n)]


HALF = FF // 2


def _sds(shape, dtype):
    return jax.ShapeDtypeStruct(shape, dtype)


def _row_tile(w):
    return lambda tm: ((tm, w), lambda i, j: (i, 0))


def _one(w):
    return lambda rows: ((rows, w), lambda i, j: (0, 0))


def _gate_up_swiglu(name, h, wgu, carry=None, tm=512):
    s = h.shape[0]
    tm = min(tm, s)

    def epilogue(prod, first, tin, tout):
        pq_ref, s_ref = tout
        a, b = prod[:, :HALF], prod[:, HALF:]
        sig = _sigmoid(a)
        act = a * sig
        pq_ref[:, :HALF] = (b * (sig * (1.0 + a * (1.0 - sig)))).astype(BF16)
        pq_ref[:, HALF:] = act.astype(BF16)
        s_ref[...] = (act * b).astype(BF16)

    return _mm(name, h, wgu, "NT", None, tm, FF, D, carry=carry, n_outer=True, epilogue=epilogue,
               tiles_out=[(_sds((s, 2 * FF), BF16), (tm, FF), lambda i, j: (i, j)),
                          (_sds((s, FF), BF16), (tm, HALF), lambda i, j: (i, j))])


def _d_hidden_swiglu(name, df, wd, ab, tm=512):
    s = df.shape[0]
    tm = min(tm, s)

    def epilogue(prod, first, tin, tout, cols):
        da_cols = slice(cols[0], cols[0] + cols[1])
        db_cols = slice(HALF + cols[0], HALF + cols[0] + cols[1])
        tout[0][:, da_cols] = (prod * tin[0][:, da_cols].astype(F32)).astype(BF16)
        tout[0][:, db_cols] = (prod * tin[0][:, db_cols].astype(F32)).astype(BF16)

    chunks = [(c0, min(384, HALF - c0)) for c0 in range(0, HALF, 384)]
    return _mm(name, df, wd, "NT", None, tm, HALF, D, n_outer=True, epilogue=epilogue, col_chunks=chunks,
               tiles_in=[(ab, (tm, FF), lambda i, j: (i, j))],
               tiles_out=[(_sds((s, 2 * FF), BF16), (tm, FF), lambda i, j: (i, j))])[0]


def _out_residual(name, a, w, x, gt, coef, nxt, tm=512, tk=FF):
    s = a.shape[0]
    tm = min(tm, s)

    def epilogue(prod, first, tin, tout):
        x_ref, gt_ref, g_ref, sc_ref, sh_ref = tin
        f_ref, xn_ref, h_ref = tout
        f_ref[...] = prod
        xn = x_ref[...] + (coef * gt_ref[...]) * prod
        xn_ref[...] = xn
        r = lax.rsqrt(jnp.mean(xn * xn, axis=-1, keepdims=True) + EPS)
        h_ref[...] = ((xn * r) * g_ref[...] * (1.0 + sc_ref[...]) + sh_ref[...]).astype(BF16)

    row, vec = _row_tile(D)(tm), _one(D)(1)
    return _mm(name, a, w, "NN", None, tm, D, tk, epilogue=epilogue,
               tiles_in=[(x, *row), (gt, *vec)] + [(v, *vec) for v in nxt],
               tiles_out=[(_sds((s, D), F32), *row), (_sds((s, D), F32), *row), (_sds((s, D), BF16), *row)])


def _out_loss(name, a, w, x, gt, coef, target, tm=512):
    s = a.shape[0]
    tm = min(tm, s)

    def epilogue(prod, first, tin, tout):
        x_ref, gt_ref, t_ref = tin
        f_ref, g_ref, df_ref, acc_ref = tout
        f_ref[...] = prod
        cg = coef * gt_ref[...]
        e = x_ref[...] + cg * prod - t_ref[...]
        gv = e * (1.0 / D)
        g_ref[...] = gv
        df_ref[...] = (cg * gv).astype(BF16)

        @pl.when(first)
        def _():
            acc_ref[...] = jnp.zeros_like(acc_ref)

        acc_ref[0:1, :] += coef * jnp.sum(gv * prod, axis=0, keepdims=True)
        acc_ref[1:2, :] += (0.5 / D) * jnp.sum(e * e, axis=0, keepdims=True)

    row, vec = _row_tile(D)(tm), _one(D)(1)
    return _mm(name, a, w, "NN", None, tm, D, FF, epilogue=epilogue,
               tiles_in=[(x, *row), (gt, *vec), (target, *row)],
               tiles_out=[(_sds((s, D), F32), *row), (_sds((s, D), F32), *row), (_sds((s, D), BF16), *row),
                          (_sds((8, D), F32), *_one(D)(8))])


def _d_h_norm_bwd(name, da, w, x, gin, g, sc, sh, before=None, carry=None, tm=256):
    s = da.shape[0]
    tm = min(tm, s)
    coef = before[2] if before else None

    def epilogue(prod, first, tin, tout):
        x_ref, gin_ref, g_ref, sc_ref, sh_ref = tin[:5]
        gout_ref, acc_ref = tout[:2]
        xv = x_ref[...]
        r = lax.rsqrt(jnp.mean(xv * xv, axis=-1, keepdims=True) + EPS)
        nv = xv * r
        gv, one_sc = g_ref[...], 1.0 + sc_ref[...]
        dn = prod * gv * one_sc
        gout = gin_ref[...] + r * (dn - nv * jnp.mean(dn * nv, axis=-1, keepdims=True))
        gout_ref[...] = gout

        @pl.when(first)
        def _():
            acc_ref[...] = jnp.zeros_like(acc_ref)

        dhn = prod * nv
        acc_ref[0:1, :] += jnp.sum(prod, axis=0, keepdims=True)
        acc_ref[1:2, :] += jnp.sum(dhn * gv, axis=0, keepdims=True)
        acc_ref[2:3, :] += jnp.sum(dhn * one_sc, axis=0, keepdims=True)
        if before:
            f_ref, gt_ref = tin[5:]
            tout[2][...] = ((coef * gt_ref[...]) * gout).astype(BF16)
            acc_ref[3:4, :] += coef * jnp.sum(gout * f_ref[...], axis=0, keepdims=True)

    row, vec = _row_tile(D)(tm), _one(D)(1)
    tiles_in = [(x, *row), (gin, *row), (g, *vec), (sc, *vec), (sh, *vec)]
    tiles_out = [(_sds((s, D), F32), *row), (_sds((8, D), F32), *_one(D)(8))]
    if before:
        tiles_in += [(before[0], *row), (before[1], *vec)]
        tiles_out.append((_sds((s, D), BF16), *row))
    return _mm(name, da, w, "NN", None, tm, D, da.shape[1], epilogue=epilogue, carry=carry, keep_b=True,
               tiles_in=tiles_in, tiles_out=tiles_out)


def _gate_tiles(proj, tm):
    return [(proj, (tm, COL), (lambda i, j, blk=blk: (i, blk))) for blk in (GA_BLK, GA_BLK + 1, GC_BLK, GC_BLK + 1)]


def _conv_branch_merge(z, wc, ya, proj, tm=512):
    s = z.shape[0]
    tm = min(tm, s)

    def epilogue(prod, first, tin, tout):
        ya_ref, ga0, ga1, gc0, gc1 = tin
        tout[0][...] = prod.astype(BF16)
        for half, (ga, gc) in enumerate(((ga0, gc0), (ga1, gc1))):
            cols = slice(half * COL, (half + 1) * COL)
            tout[1][:, cols] = (_sigmoid(ga[...].astype(F32)) * ya_ref[:, cols].astype(F32)
                                + _sigmoid(gc[...].astype(F32)) * prod[:, cols]).astype(BF16)

    row = _row_tile(D)(tm)
    return _mm("mix_conv_branch", z, wc, "NN", None, tm, D, D, epilogue=epilogue,
               tiles_in=[(ya, *row)] + _gate_tiles(proj, tm),
               tiles_out=[(_sds((s, D), BF16), *row), (_sds((s, D), BF16), *row)])


def _d_merged_branches(dmix, wo, proj, tm=512):
    s = dmix.shape[0]
    tm = min(tm, s)

    def epilogue(prod, first, tin, tout):
        ga0, ga1, gc0, gc1 = tin
        tout[0][...] = prod
        for half, (ga, gc) in enumerate(((ga0, gc0), (ga1, gc1))):
            cols = slice(half * COL, (half + 1) * COL)
            tout[1][:, cols] = (prod[:, cols] * _sigmoid(ga[...].astype(F32))).astype(BF16)
            tout[2][:, cols] = (prod[:, cols] * _sigmoid(gc[...].astype(F32))).astype(BF16)

    row = _row_tile(D)(tm)
    return _mm("mix_d_merged", dmix, wo, "NT", None, tm, D, D, epilogue=epilogue,
               tiles_in=_gate_tiles(proj, tm),
               tiles_out=[(_sds((s, D), F32), *row), (_sds((s, D), BF16), *row), (_sds((s, D), BF16), *row)])


def _d_o_delta(dya, wa_t, o, tm=1024):
    s = dya.shape[0]
    tm = min(tm, s)

    def epilogue(prod, first, tin, tout):
        tout[0][...] = prod
        tout[1][...] = _heads(prod * tin[0][...].astype(F32), lambda ph, h: jnp.broadcast_to(
            jnp.sum(ph, axis=-1, keepdims=True), ph.shape))

    row = _row_tile(COL)(tm)
    return _mm("mix_d_o", dya, wa_t, "NN", None, tm, COL, D, epilogue=epilogue,
               tiles_in=[(o, *row)], tiles_out=[(_sds((s, COL), F32), *row), (_sds((s, COL), F32), *row)])


def _ffn_bwd(tag, df, x, gin, h, ab, sw, g, sc, sh, wgu, wd, before=None, carry_down=None, carry_gate_up=None,
             tk_dw=2048):
    dab = _d_hidden_swiglu(f"{tag}_d_hidden", df, wd, ab)
    dwd = _mm(f"{tag}_dw_down", sw, df, "TN", BF16, HALF, D, tk_dw)
    carried = []
    if carry_down:
        dwgu, *got = _mm(f"{tag}_dw_gate_up", dab, h, "TN", BF16, HALF, D, tk_dw, carry=carry_down(dwd))
        carried += got
    else:
        dwgu = _mm(f"{tag}_dw_gate_up", dab, h, "TN", BF16, HALF, D, tk_dw)
    res = _d_h_norm_bwd(f"{tag}_d_h", dab, wgu, x, gin, g, sc, sh, before=before,
                        carry=carry_gate_up(dwgu) if carry_gate_up else None)
    n_own = 3 if before else 2
    return res[:n_own], dwgu, dwd, carried + list(res[n_own:])


def kernel(x, c, w_ada, b_ada, norm_ffn1, ffn1_w_gate, ffn1_w_up, ffn1_w_down, norm_mix, w_in, q_norm, k_norm, conv_w, w_attn_branch, w_conv_branch, w_out, norm_ffn2, ffn2_w_gate, ffn2_w_up, ffn2_w_down, loss_target, m_w_ada, m_b_ada, m_norm_ffn1, m_ffn1_w_gate, m_ffn1_w_up, m_ffn1_w_down, m_norm_mix, m_w_in, m_q_norm, m_k_norm, m_conv_w, m_w_attn_branch, m_w_conv_branch, m_w_out, m_norm_ffn2, m_ffn2_w_gate, m_ffn2_w_up, m_ffn2_w_down, v_w_ada, v_b_ada, v_norm_ffn1, v_ffn1_w_gate, v_ffn1_w_up, v_ffn1_w_down, v_norm_mix, v_w_in, v_q_norm, v_k_norm, v_conv_w, v_w_attn_branch, v_w_conv_branch, v_w_out, v_norm_ffn2, v_ffn2_w_gate, v_ffn2_w_up, v_ffn2_w_down):
    me = 4 * lax.axis_index("x") + 2 * lax.axis_index("y") + lax.axis_index("c")
    x0, target = x[0], loss_target[0]
    s = x0.shape[0]
    ada_cols = w_ada.shape[2]
    cw_cols = conv_w.shape[2]

    gathered = _small_allgather(
        "gather_c_conv", jnp.concatenate([c, conv_w[0].reshape(1, 3 * cw_cols)], axis=1))[:, 0]
    c_all = gathered[:, :D]
    cw = gathered[:, D:].reshape(N_DEV, 3, cw_cols).transpose(1, 0, 2).reshape(3, D)
    b_part = lax.dynamic_slice(b_ada, (0, me * ada_cols), (1, ada_cols))
    mod_part = _mod_part(c_all, w_ada[0], b_part)
    mod_all = _small_allgather("gather_mod", mod_part.reshape(1, N_DEV * ada_cols))
    mod = lax.dynamic_slice(mod_all.reshape(N_DEV, N_DEV, ada_cols), (0, me, 0), (N_DEV, 1, ada_cols))
    mod = mod.reshape(N_MOD, 1, D)
    sh1, sc1, gt1, sh2, sc2, gt2, sh3, sc3, gt3 = [mod[i] for i in range(N_MOD)]

    tb = lambda w: w[0].T.astype(BF16)
    nb = lambda w: w[0].astype(BF16)
    ffn1_shards = [tb(ffn1_w_gate), tb(ffn1_w_up), nb(ffn1_w_down)]
    ffn2_shards = [tb(ffn2_w_gate), tb(ffn2_w_up), nb(ffn2_w_down)]
    mix_shards = [tb(w_in), tb(w_attn_branch), nb(w_conv_branch), nb(w_out)]
    ffn_dst, ffn_base, ffn_jump, ffn_shapes = [0, 0, 1], [0, HALF, 0], [HALF, HALF, 0], [(2 * FF, D), (FF, D)]
    mix_dst, mix_base, mix_shapes = [0, 1, 2, 3], [0, 0, 0, 0], [(IN_W, D), (D, COL), (D, D), (D, D)]
    wgu1, wd1 = _run_plan_on_sequencer(
        "gather_ffn1_weights", _gather_plan(ffn1_shards, ffn_dst, ffn_base, ffn_shapes, ffn_jump), 1)

    h1 = _normmod("ffn1_normmod", x0, norm_ffn1, sc1, sh1)
    ab1, s1, win_t = _gate_up_swiglu(
        "ffn1_gate_up", h1, wgu1, carry=_gather_plan(mix_shards[:1], mix_dst[:1], mix_base[:1], mix_shapes[:1]))
    f1, x1, h2 = _out_residual("ffn1_down", s1, wd1, x0, gt1, 0.5, (norm_mix, sc2, sh2))
    proj, wgu2, wd2, wa_t, wc, wo = _mm(
        "mix_in_proj", h2, win_t, "NT", BF16, 1024, IN_W // 4, D, n_outer=True,
        carry=_gather_plan(ffn2_shards + mix_shards[1:], ffn_dst + [2, 3, 4], ffn_base + [0, 0, 0],
                           ffn_shapes + mix_shapes[1:], ffn_jump + [0, 0, 0]))
    wqk = jnp.concatenate([jnp.tile(q_norm, (1, 12)), jnp.tile(k_norm, (1, 12))], axis=1)
    qkn = _qknorm(proj, wqk)
    group_out = [_attn_fwd(g, qkn, proj) for g in range(3)]
    o, lse = _attn_combine([go[0] for go in group_out], [go[1] for go in group_out])
    ya = _mm("mix_attn_branch", o, wa_t, "NT", BF16, 1024, 1024, COL)
    z = _conv_fwd(proj, cw)
    yc, merged = _conv_branch_merge(z, wc, ya, proj)
    mix, x2, h3 = _out_residual("mix_out_proj", merged, wo, x1, gt2, 1.0, (norm_ffn2, sc3, sh3), tk=D)
    ab3, s3 = _gate_up_swiglu("ffn2_gate_up", h3, wgu2)
    f3, g3, df3, acc_out = _out_loss("ffn2_down", s3, wd2, x2, gt3, 0.5, target)
    loss_part = jnp.sum(acc_out[1])

    ffn_rows = [sh_.shape[0] for sh_ in ffn1_shards]
    mix_rows = [sh_.shape[0] for sh_ in mix_shards]
    (g2, acc3, dmix), dwgu2, dwd2, _ = _ffn_bwd(
        "ffn2", df3, x2, g3, h3, ab3, s3, norm_ffn2, sc3, sh3, wgu2, wd2, before=(mix, gt2, 1.0))
    dmerged, dya, dyc = _d_merged_branches(dmix, wo, proj)
    dwo = _mm("mix_dw_out", merged, dmix, "TN", BF16, 1024, 1024, 2048)
    dproj = _merge_bwd_gates(dmerged, ya, yc, proj)
    dwc = _mm("mix_dw_conv_branch", z, dyc, "TN", BF16, 1024, 1024, 2048)
    dz = _mm("mix_d_z", dyc, wc, "NT", F32, 1024, 1024, D)
    dproj, cw_acc = _conv_bwd(dz, proj, cw, dproj)
    dwa_t = _mm("mix_dw_attn_branch", dya, o, "TN", BF16, 1024, COL, 2048)
    do, delta = _d_o_delta(dya, wa_t, o)
    dqn = dkn = None
    for g in range(3):
        dqn, dkn, dproj = _attn_bwd(g, qkn, proj, do, lse, delta, dqn, dkn, dproj)
    dproj, wq_acc = _qknorm_bwd("qnorm_bwd", proj, dqn, wqk[:, :QKW // 2], dproj, 0)
    dproj, wk_acc = _qknorm_bwd("knorm_bwd", proj, dkn, wqk[:, QKW // 2:], dproj, QKW // 2 // COL)
    dwin_t, r_f2g, r_f2u, r_f2d, r_wa, r_wc, r_wo = _mm(
        "mix_dw_in", dproj, h2, "TN", BF16, IN_W // 4, COL, 2048,
        carry=_scatter_plan([dwgu2, dwd2, dwa_t, dwc, dwo], [0, 0, 1, 2, 3, 4], [0, HALF, 0, 0, 0, 0],
                            ffn_rows + mix_rows[1:], [D, D, D, COL, D, D], [HALF, HALF, 0, 0, 0, 0]))
    g1, acc2, df1, r_win = _d_h_norm_bwd(
        "mix_d_h", dproj, win_t, x1, g2, norm_mix, sc2, sh2, before=(f1, gt1, 0.5),
        carry=_scatter_plan([dwin_t], [0], [0], mix_rows[:1], [D]))
    (g0, acc1), dwgu1, dwd1, (r_f1d, r_f1g, r_f1u) = _ffn_bwd(
        "ffn1", df1, x0, g1, h1, ab1, s1, norm_ffn1, sc1, sh1, wgu1, wd1,
        carry_down=lambda dwd: _scatter_plan([dwd], [0], [0], ffn_rows[2:], [D]),
        carry_gate_up=lambda dwgu: _scatter_plan([dwgu], [0, 0], [0, HALF], ffn_rows[:2], [D, D], [HALF, HALF]))

    dqw = jnp.sum(wq_acc[0].reshape(12, HD), axis=0)
    dkw = jnp.sum(wk_acc[0].reshape(12, HD), axis=0)
    small = jnp.concatenate([
        acc1[0], acc1[1], acc2[3], acc2[0], acc2[1], acc3[3], acc3[0], acc3[1], acc_out[0],
        acc1[2], acc2[2], acc3[2], dqw, dkw, cw_acc[0:3].reshape(3 * D),
        jnp.zeros((HD,), F32).at[0].set(loss_part)]).reshape(1, -1)
    small_all = _small_allgather("gather_small_grads", small)
    small_sum = _sum_rows("sum_small_grads", small_all)[0]
    n_mod = N_MOD * D
    g_b_ada = small_sum[:n_mod].reshape(1, n_mod)
    g_norm1, g_norm2, g_norm3 = [small_sum[n_mod + i * D:n_mod + (i + 1) * D].reshape(1, D) for i in range(3)]
    off = n_mod + 3 * D
    g_qn, g_kn = small_sum[off:off + HD].reshape(1, HD), small_sum[off + HD:off + 2 * HD].reshape(1, HD)
    g_cw_full = small_sum[off + 2 * HD:off + 2 * HD + 3 * D].reshape(3, D)
    loss = small_sum[off + 2 * HD + 3 * D]
    g_cw = lax.dynamic_slice(g_cw_full, (0, me * cw_cols), (3, cw_cols))
    dmod_part = lax.dynamic_slice(small_all[:, 0, :n_mod], (0, me * ada_cols), (N_DEV, ada_cols))
    g_w_ada = _w_ada_grad(c_all.T, dmod_part)

    as_rows = {"ffn1_w_gate", "ffn1_w_up", "w_in", "w_attn_branch", "ffn2_w_gate", "ffn2_w_up"}
    grad_list = [g_w_ada, g_b_ada, g_norm1, r_f1g, r_f1u, r_f1d, g_norm2, r_win,
                 g_qn, g_kn, g_cw, r_wa, r_wc, r_wo, g_norm3, r_f2g, r_f2u, r_f2d]
    weights = [w_ada, b_ada, norm_ffn1, ffn1_w_gate, ffn1_w_up, ffn1_w_down, norm_mix, w_in, q_norm, k_norm,
               conv_w, w_attn_branch, w_conv_branch, w_out, norm_ffn2, ffn2_w_gate, ffn2_w_up, ffn2_w_down]
    ms = [m_w_ada, m_b_ada, m_norm_ffn1, m_ffn1_w_gate, m_ffn1_w_up, m_ffn1_w_down, m_norm_mix, m_w_in, m_q_norm,
          m_k_norm, m_conv_w, m_w_attn_branch, m_w_conv_branch, m_w_out, m_norm_ffn2, m_ffn2_w_gate,
          m_ffn2_w_up, m_ffn2_w_down]
    vs = [v_w_ada, v_b_ada, v_norm_ffn1, v_ffn1_w_gate, v_ffn1_w_up, v_ffn1_w_down, v_norm_mix, v_w_in, v_q_norm,
          v_k_norm, v_conv_w, v_w_attn_branch, v_w_conv_branch, v_w_out, v_norm_ffn2, v_ffn2_w_gate,
          v_ffn2_w_up, v_ffn2_w_down]
    wnames = ["w_ada", "b_ada", "norm_ffn1", "ffn1_w_gate", "ffn1_w_up", "ffn1_w_down", "norm_mix", "w_in",
              "q_norm", "k_norm", "conv_w", "w_attn_branch", "w_conv_branch", "w_out", "norm_ffn2",
              "ffn2_w_gate", "ffn2_w_up", "ffn2_w_down"]
    small = [i for i, gr in enumerate(grad_list) if gr.ndim == 2 and gr.size <= 16384]
    flat = lambda a, i: a.reshape(-1, weights[i].shape[-1])
    small_res = dict(zip(small, _adamw_small(
        [flat(weights[i], i) for i in small], [flat(grad_list[i], i) for i in small],
        [flat(ms[i], i) for i in small], [flat(vs[i], i) for i in small])))
    grad_out, deltas, new_ms, new_vs = [], [], [], []
    for idx, (nm, w, gr, m_, v_) in enumerate(zip(wnames, weights, grad_list, ms, vs)):
        if idx in small_res:
            gr, dl, nm_, nv_ = [r.reshape(w.shape) for r in (gr, *small_res[idx])]
        elif nm in as_rows:
            res = _adamw(f"adamw_{nm}", w[0].T, gr, m_[0].T, v_[0].T)
            gr, dl, nm_, nv_ = [r.T[None] for r in res]
        else:
            two_d = (-1, w.shape[-1])
            res = _adamw(f"adamw_{nm}", w.reshape(two_d), gr if gr.ndim == 3 else gr.reshape(two_d),
                         m_.reshape(two_d), v_.reshape(two_d))
            gr, dl, nm_, nv_ = [r.reshape(w.shape) for r in res]
        grad_out.append(gr)
        deltas.append(dl)
        new_ms.append(nm_)
        new_vs.append(nv_)
    return (loss, g0[None], *grad_out, *deltas, *new_ms, *new_vs)
```

```python
import functools

import jax
import jax.numpy as jnp
from jax import lax
from jax.experimental import pallas as pl
from jax.experimental.pallas import tpu as pltpu
from jax.experimental.pallas import tpu_sc as plsc

F32 = jnp.float32
BF16 = jnp.bfloat16
MESH = pl.DeviceIdType.MESH

N_DEV = 8
D = 1024
FF = 2816
HD = 128
N_HEADS = 4
DILATIONS = (1, 4, 16)
BAND = 128
QKW = 2 * 3 * N_HEADS * HD
IN_W = 9728
COL = 512
V_BLK, U_BLK, B_BLK, C_BLK, GA_BLK, GC_BLK = 6, 9, 11, 13, 15, 17
EPS = 1e-6
N_MOD = 9
ADAM_LR, ADAM_B1, ADAM_B2, ADAM_EPS, ADAM_WD, ADAM_STEP = 0.001, 0.9, 0.999, 1e-08, 0.01, 10

NT_DIMS = (((1,), (1,)), ((), ()))
TN_DIMS = (((0,), (0,)), ((), ()))
NN_DIMS = (((1,), (0,)), ((), ()))


def _place():
    return lax.axis_index("x"), lax.axis_index("y"), lax.axis_index("c")


def _flip(coord, bit):
    return 1 - coord if bit else coord


def _params(*sem):
    return pltpu.CompilerParams(dimension_semantics=sem)


def _small_allgather(name, v):
    n = v.shape[-1]

    def body(v_ref, out_ref, send_sems, recv_sems):
        x, y, c = _place()
        me = 4 * x + 2 * y + c
        out_ref[me] = v_ref[...]
        copies = []
        for k in range(1, N_DEV):
            peer = (_flip(x, (k >> 2) & 1), _flip(y, (k >> 1) & 1), _flip(c, k & 1))
            cp = pltpu.make_async_remote_copy(
                src_ref=v_ref, dst_ref=out_ref.at[me], send_sem=send_sems.at[k - 1],
                recv_sem=recv_sems.at[k - 1], device_id=peer, device_id_type=MESH)
            cp.start()
            copies.append(cp)
        for cp in copies:
            cp.wait()

    return pl.pallas_call(
        body, name=name,
        out_shape=jax.ShapeDtypeStruct((N_DEV, 1, n), F32),
        in_specs=[pl.BlockSpec(memory_space=pltpu.VMEM)],
        out_specs=pl.BlockSpec(memory_space=pltpu.VMEM),
        scratch_shapes=[pltpu.SemaphoreType.DMA((N_DEV - 1,)), pltpu.SemaphoreType.DMA((N_DEV - 1,))],
    )(v)


class _Plan:
    def __init__(self, operands, out_shapes, sems, phases):
        self.operands, self.out_shapes, self.sems, self.phases = operands, out_shapes, sems, phases


def _slab_start(base, rows, jump, idx):
    return pl.multiple_of(base + idx * rows + (idx // 4) * jump, 16)


def _gather_plan(shards, dst_of, base_of, dst_shapes, jump_of=None):
    n = len(shards)
    rows = [s.shape[0] for s in shards]
    jump_of = jump_of or [0] * n

    def phases(srcs, dsts, sems):
        send_sems, recv_sems, local_sems = sems
        x, y, c = _place()
        me, sibling = (x, y, c), (x, y, 1 - c)
        chips = [(1 - x, y), (x, 1 - y), (1 - x, 1 - y)]

        def slab(i, px, py, pc):
            start = _slab_start(base_of[i], rows[i], jump_of[i], 4 * px + 2 * py + pc)
            return dsts[dst_of[i]].at[pl.ds(start, rows[i])]

        def copy(i, k, block, to, src=None):
            return pltpu.make_async_remote_copy(
                src_ref=slab(i, *block) if src is None else src, dst_ref=slab(i, *block),
                send_sem=send_sems.at[i, k], recv_sem=recv_sems.at[i, k],
                device_id=to, device_id_type=MESH)

        def mine():
            return [pltpu.make_async_copy(srcs[i], slab(i, *me), local_sems.at[i]) for i in range(n)]

        def first():
            out = []
            for i in range(n):
                out.append(copy(i, 0, me, sibling, src=srcs[i]))
                out += [copy(i, 1 + j, me, (*chip, c), src=srcs[i]) for j, chip in enumerate(chips)]
            return out

        def passed():
            return [(copy(i, 1 + j, (*chip, c), me), copy(i, 4 + j, (*chip, c), sibling))
                    for j, chip in enumerate(chips) for i in range(n)]

        def start():
            for cp in mine() + first():
                cp.start()

        def middle():
            for landed, onward in passed():
                landed.wait_recv()
                onward.start()

        def finish():
            for i in range(n):
                copy(i, 0, sibling, me).wait_recv()
                for j, chip in enumerate(chips):
                    copy(i, 4 + j, (*chip, 1 - c), me).wait_recv()
            for cp in first() + [onward for _, onward in passed()]:
                cp.wait_send()
            for cp in mine():
                cp.wait()

        return start, middle, finish

    sems = [pltpu.SemaphoreType.DMA((n, 7)), pltpu.SemaphoreType.DMA((n, 7)), pltpu.SemaphoreType.DMA((n,))]
    return _Plan(list(shards), [jax.ShapeDtypeStruct(s, BF16) for s in dst_shapes], sems, phases)


def _scatter_plan(grads, src_of, base_of, rows, cols, jump_of=None):
    n = len(rows)
    jump_of = jump_of or [0] * n

    def phases(srcs, recvs, sems):
        send_sems, recv_sems, local_sems = sems
        x, y, c = _place()
        me = 4 * x + 2 * y + c

        def slab(i, idx):
            start = _slab_start(base_of[i], rows[i], jump_of[i], idx)
            return srcs[src_of[i]].at[pl.ds(start, rows[i])]

        def copies():
            out = [pltpu.make_async_copy(slab(i, me), recvs[i].at[me], local_sems.at[i]) for i in range(n)]
            for k in range(1, N_DEV):
                px, py, pc = _flip(x, (k >> 2) & 1), _flip(y, (k >> 1) & 1), _flip(c, k & 1)
                out += [pltpu.make_async_remote_copy(
                    src_ref=slab(i, 4 * px + 2 * py + pc), dst_ref=recvs[i].at[me],
                    send_sem=send_sems.at[i, k - 1], recv_sem=recv_sems.at[i, k - 1],
                    device_id=(px, py, pc), device_id_type=MESH) for i in range(n)]
            return out

        def start():
            for cp in copies():
                cp.start()

        def finish():
            for cp in copies():
                cp.wait()

        return start, None, finish

    sems = [pltpu.SemaphoreType.DMA((n, 7)), pltpu.SemaphoreType.DMA((n, 7)), pltpu.SemaphoreType.DMA((n,))]
    out_shapes = [jax.ShapeDtypeStruct((N_DEV, rows[i], cols[i]), BF16) for i in range(n)]
    return _Plan(list(grads), out_shapes, sems, phases)


def _run_plan(name, plan):
    n_in, n_out = len(plan.operands), len(plan.out_shapes)

    def body(*refs):
        for phase in plan.phases(refs[:n_in], refs[n_in:n_in + n_out], refs[n_in + n_out:]):
            if phase is not None:
                phase()

    hbm = pl.BlockSpec(memory_space=pltpu.HBM)
    return pl.pallas_call(
        body, name=name, out_shape=plan.out_shapes,
        in_specs=[hbm] * n_in, out_specs=[hbm] * n_out, scratch_shapes=plan.sems,
    )(*plan.operands)


def _run_plan_on_sequencer(name, plan, collective_id):
    src_refs = [jax.new_ref(a, memory_space=pltpu.MemorySpace.HBM) for a in plan.operands]
    dst_refs = [jax.empty_ref(s, memory_space=pltpu.MemorySpace.HBM) for s in plan.out_shapes]

    @pl.kernel(mesh=plsc.ScalarSubcoreMesh(axis_name="sequencer", num_cores=1), name=name,
               scratch_types=tuple(plan.sems),
               compiler_params=pltpu.CompilerParams(collective_id=collective_id))
    def launch(*sems):
        x, y, c = _place()
        barrier = pltpu.get_barrier_semaphore()
        for k in range(1, N_DEV):
            peer = (_flip(x, (k >> 2) & 1), _flip(y, (k >> 1) & 1), _flip(c, k & 1))
            pl.semaphore_signal(barrier, inc=1, device_id=peer, device_id_type=MESH)
        pl.semaphore_wait(barrier, N_DEV - 1)
        for phase in plan.phases(src_refs, dst_refs, sems):
            if phase is not None:
                phase()

    launch()
    return [r[...] for r in dst_refs]


def _sum_contributions(name, recv):
    _, rows, cols = recv.shape
    tr = rows if rows <= 512 else 304 if rows % 304 == 0 else 256

    def body(r_ref, o_ref):
        acc = r_ref[0].astype(F32)
        for k in range(1, N_DEV):
            acc = acc + r_ref[k].astype(F32)
        o_ref[...] = acc

    return pl.pallas_call(
        body, name=name, grid=(rows // tr,),
        out_shape=jax.ShapeDtypeStruct((rows, cols), F32),
        in_specs=[pl.BlockSpec((N_DEV, tr, cols), lambda i: (0, i, 0))],
        out_specs=pl.BlockSpec((tr, cols), lambda i: (i, 0)),
        compiler_params=_params("parallel"),
    )(recv)


def _mm(name, a, b, mode, out_dtype, tm, tn, tk, *, carry=None, tiles_in=(), tiles_out=(), epilogue=None,
        n_outer=False, keep_b=False, col_chunks=None):
    if mode == "TN":
        kk, m = a.shape
    else:
        m, kk = a.shape
    n = b.shape[0] if mode == "NT" else b.shape[1]
    tm, tn, tk = min(tm, m), min(tn, n), min(tk, kk)
    assert m % tm == 0 and n % tn == 0 and kk % tk == 0, (name, m, n, kk, tm, tn, tk)
    ni, nj, nk = m // tm, n // tn, kk // tk
    steps = ni * nj * nk
    dims = {"NN": NN_DIMS, "NT": NT_DIMS, "TN": TN_DIMS}[mode]
    if epilogue is None:
        tiles_out = [(jax.ShapeDtypeStruct((m, n), out_dtype), (tm, tn), lambda i, j: (i, j))]
    n_tin, n_tout = len(tiles_in), len(tiles_out)
    n_in = len(carry.operands) if carry else 0
    n_out = len(carry.out_shapes) if carry else 0
    n_acc = 1 if nk > 1 else 0
    n_keep = 2 if keep_b else 0
    assert not carry or steps >= 3
    assert not keep_b or (nk == 1 and nj == 1)
    assert not col_chunks or (epilogue is not None and nk == 1 and mode != "TN")
    ij = (lambda p, q: (q, p)) if n_outer else (lambda p, q: (p, q))
    inner = ni if n_outer else nj

    def body(a_ref, b_ref, *rest):
        tin = rest[:n_tin]
        cin = rest[n_tin:n_tin + n_in]
        tout = rest[n_tin + n_in:n_tin + n_in + n_tout]
        cout = rest[n_tin + n_in + n_tout:n_tin + n_in + n_tout + n_out]
        scratch = rest[n_tin + n_in + n_tout + n_out:]
        k = pl.program_id(2)
        visit = pl.program_id(0) * inner + pl.program_id(1)
        step = visit * nk + k
        if keep_b:
            b_kept, b_sem = scratch[n_acc:n_acc + 2]

            @pl.when(step == 0)
            def _():
                cp = pltpu.make_async_copy(b_ref, b_kept, b_sem)
                cp.start()
                cp.wait()

            b_ref = b_kept
        if carry:
            start, middle, finish = carry.phases(cin, cout, scratch[n_acc + n_keep:])
            pl.when(step == 0)(start)

        def store(prod, c=0, cols=()):
            if epilogue is None:
                tout[0][...] = prod.astype(out_dtype)
            else:
                epilogue(prod, jnp.logical_and(visit == 0, c == 0), tin, tout, *cols)

        if col_chunks:
            for c, (c0, cw) in enumerate(col_chunks):
                b_part = b_ref[pl.ds(c0, cw), :] if mode == "NT" else b_ref[:, pl.ds(c0, cw)]
                store(lax.dot_general(a_ref[...], b_part, dims, preferred_element_type=F32), c, ((c0, cw),))
        else:
            part = lax.dot_general(a_ref[...], b_ref[...], dims, preferred_element_type=F32)
            if nk == 1:
                store(part)
            else:
                acc_ref = scratch[0]

                @pl.when(k == 0)
                def _():
                    acc_ref[...] = part

                @pl.when((k > 0) & (k < nk - 1))
                def _():
                    acc_ref[...] += part

                @pl.when(k == nk - 1)
                def _():
                    store(acc_ref[...] + part)

        if carry:
            if middle is not None:
                pl.when(step == (steps * 3) // 5)(middle)
            pl.when(step == steps - 1)(finish)

    def spec(shape, fn):
        return pl.BlockSpec(shape, lambda p, q, k: fn(*ij(p, q)))

    a_spec = (pl.BlockSpec((tk, tm), lambda p, q, k: (k, ij(p, q)[0])) if mode == "TN"
              else pl.BlockSpec((tm, tk), lambda p, q, k: (ij(p, q)[0], k)))
    if keep_b:
        b_spec = pl.BlockSpec(memory_space=pl.ANY)
    elif mode == "NT":
        b_spec = pl.BlockSpec((tn, tk), lambda p, q, k: (ij(p, q)[1], k))
    else:
        b_spec = pl.BlockSpec((tk, tn), lambda p, q, k: (k, ij(p, q)[1]))
    hbm = pl.BlockSpec(memory_space=pltpu.HBM)
    sequential = carry or epilogue or keep_b
    out = pl.pallas_call(
        body, name=name, grid=(nj, ni, nk) if n_outer else (ni, nj, nk),
        out_shape=[t[0] for t in tiles_out] + (carry.out_shapes if carry else []),
        in_specs=[a_spec, b_spec] + [spec(t[1], t[2]) for t in tiles_in] + [hbm] * n_in,
        out_specs=[spec(t[1], t[2]) for t in tiles_out] + [hbm] * n_out,
        scratch_shapes=([pltpu.VMEM((tm, tn), F32)] * n_acc
                        + ([pltpu.VMEM(b.shape, b.dtype), pltpu.SemaphoreType.DMA] if keep_b else [])
                        + (carry.sems if carry else [])),
        compiler_params=(_params("arbitrary", "arbitrary", "arbitrary") if sequential
                         else _params("parallel", "parallel", "arbitrary")),
    )(a, b, *[t[0] for t in tiles_in], *(carry.operands if carry else []))
    return out if (carry or epilogue) else out[0]


def _row(tm, w, off=0):
    return pl.BlockSpec((tm, w), lambda i: (i, off))


def _vec(w):
    return pl.BlockSpec((1, w), lambda i: (0, 0))


def _sigmoid(x):
    return 0.5 * jnp.tanh(0.5 * x) + 0.5


def _normmod(name, x, g, sc, sh, tm=512):
    s = x.shape[0]

    def body(x_ref, g_ref, sc_ref, sh_ref, h_ref):
        xv = x_ref[...]
        r = lax.rsqrt(jnp.mean(xv * xv, axis=-1, keepdims=True) + EPS)
        h_ref[...] = ((xv * r) * g_ref[...] * (1.0 + sc_ref[...]) + sh_ref[...]).astype(BF16)

    return pl.pallas_call(
        body, name=name, grid=(s // tm,),
        out_shape=jax.ShapeDtypeStruct((s, D), BF16),
        in_specs=[_row(tm, D), _vec(D), _vec(D), _vec(D)], out_specs=_row(tm, D),
        compiler_params=_params("parallel"),
    )(x, g, sc, sh)


def _normmod_bwd(name, dh, x, gin, g, sc, sh, tm=512):
    s = x.shape[0]

    def body(dh_ref, x_ref, gin_ref, g_ref, sc_ref, sh_ref, gout_ref, acc_ref):
        xv, dhv = x_ref[...], dh_ref[...]
        r = lax.rsqrt(jnp.mean(xv * xv, axis=-1, keepdims=True) + EPS)
        nv = xv * r
        gv, one_sc = g_ref[...], 1.0 + sc_ref[...]
        dn = dhv * gv * one_sc
        dx = r * (dn - nv * jnp.mean(dn * nv, axis=-1, keepdims=True))
        gout_ref[...] = gin_ref[...] + dx

        @pl.when(pl.program_id(0) == 0)
        def _():
            acc_ref[...] = jnp.zeros_like(acc_ref)

        dhn = dhv * nv
        acc_ref[0:1, :] += jnp.sum(dhv, axis=0, keepdims=True)
        acc_ref[1:2, :] += jnp.sum(dhn * gv, axis=0, keepdims=True)
        acc_ref[2:3, :] += jnp.sum(dhn * one_sc, axis=0, keepdims=True)

    return pl.pallas_call(
        body, name=name, grid=(s // tm,),
        out_shape=[jax.ShapeDtypeStruct((s, D), F32), jax.ShapeDtypeStruct((8, D), F32)],
        in_specs=[_row(tm, D), _row(tm, D), _row(tm, D), _vec(D), _vec(D), _vec(D)],
        out_specs=[_row(tm, D), pl.BlockSpec((8, D), lambda i: (0, 0))],
        compiler_params=_params("arbitrary"),
    )(dh, x, gin, g, sc, sh)


def _swiglu(name, ab, tm=512):
    s = ab.shape[0]

    def body(ab_ref, s_ref):
        a = ab_ref[:, :FF].astype(F32)
        b = ab_ref[:, FF:].astype(F32)
        s_ref[...] = (a * _sigmoid(a) * b).astype(BF16)

    return pl.pallas_call(
        body, name=name, grid=(s // tm,),
        out_shape=jax.ShapeDtypeStruct((s, FF), BF16),
        in_specs=[_row(tm, 2 * FF)], out_specs=_row(tm, FF),
        compiler_params=_params("parallel"),
    )(ab)


def _swiglu_bwd(name, ds, ab, tm=256):
    s = ab.shape[0]

    def body(ds_ref, ab_ref, dab_ref):
        a = ab_ref[:, :FF].astype(F32)
        b = ab_ref[:, FF:].astype(F32)
        dsv = ds_ref[...].astype(F32)
        sig = _sigmoid(a)
        dab_ref[:, :FF] = (dsv * b * (sig * (1.0 + a * (1.0 - sig)))).astype(BF16)
        dab_ref[:, FF:] = (dsv * (a * sig)).astype(BF16)

    return pl.pallas_call(
        body, name=name, grid=(s // tm,),
        out_shape=jax.ShapeDtypeStruct((s, 2 * FF), BF16),
        in_specs=[_row(tm, FF), _row(tm, 2 * FF)], out_specs=_row(tm, 2 * FF),
        compiler_params=_params("parallel"),
    )(ds, ab)


def _residual(name, x, f, gt, coef, tm=512):
    s = x.shape[0]

    def body(x_ref, f_ref, gt_ref, o_ref):
        o_ref[...] = x_ref[...] + (coef * gt_ref[...]) * f_ref[...]

    return pl.pallas_call(
        body, name=name, grid=(s // tm,),
        out_shape=jax.ShapeDtypeStruct((s, D), F32),
        in_specs=[_row(tm, D), _row(tm, D), _vec(D)], out_specs=_row(tm, D),
        compiler_params=_params("parallel"),
    )(x, f, gt)


def _gate_bwd(name, gin, f, gt, coef, tm=512):
    s = gin.shape[0]

    def body(g_ref, f_ref, gt_ref, df_ref, acc_ref):
        gv = g_ref[...]
        df_ref[...] = ((coef * gt_ref[...]) * gv).astype(BF16)

        @pl.when(pl.program_id(0) == 0)
        def _():
            acc_ref[...] = jnp.zeros_like(acc_ref)

        acc_ref[0:1, :] += coef * jnp.sum(gv * f_ref[...], axis=0, keepdims=True)

    return pl.pallas_call(
        body, name=name, grid=(s // tm,),
        out_shape=[jax.ShapeDtypeStruct((s, D), BF16), jax.ShapeDtypeStruct((8, D), F32)],
        in_specs=[_row(tm, D), _row(tm, D), _vec(D)],
        out_specs=[_row(tm, D), pl.BlockSpec((8, D), lambda i: (0, 0))],
        compiler_params=_params("arbitrary"),
    )(gin, f, gt)


def _loss_grad(x3, target, tm=512):
    s = x3.shape[0]

    def body(y_ref, t_ref, g_ref, l_ref):
        e = y_ref[...] - t_ref[...]
        g_ref[...] = e * (1.0 / D)

        @pl.when(pl.program_id(0) == 0)
        def _():
            l_ref[...] = jnp.zeros_like(l_ref)

        l_ref[...] += jnp.sum(jnp.mean(e * e, axis=-1, keepdims=True), axis=0, keepdims=True) * 0.5

    return pl.pallas_call(
        body, name="loss_grad", grid=(s // tm,),
        out_shape=[jax.ShapeDtypeStruct((s, D), F32), jax.ShapeDtypeStruct((8, 128), F32)],
        in_specs=[_row(tm, D), _row(tm, D)],
        out_specs=[_row(tm, D), pl.BlockSpec((8, 128), lambda i: (0, 0))],
        compiler_params=_params("arbitrary"),
    )(x3, target)


def _heads(x, fn):
    return jnp.concatenate([fn(x[:, h * HD:(h + 1) * HD], h) for h in range(COL // HD)], axis=1)


def _qknorm(proj, wqk, tm=1024):
    s = proj.shape[0]

    def body(p_ref, w_ref, o_ref):
        pv = p_ref[...].astype(F32)
        wv = w_ref[...]

        def one(qh, h):
            r = lax.rsqrt(jnp.mean(qh * qh, axis=-1, keepdims=True) + EPS)
            return (qh * r) * wv[:, h * HD:(h + 1) * HD]

        o_ref[...] = _heads(pv, one).astype(BF16)

    return pl.pallas_call(
        body, name="qknorm", grid=(s // tm, QKW // COL),
        out_shape=jax.ShapeDtypeStruct((s, QKW), BF16),
        in_specs=[pl.BlockSpec((tm, COL), lambda i, j: (i, j)), pl.BlockSpec((1, COL), lambda i, j: (0, j))],
        out_specs=pl.BlockSpec((tm, COL), lambda i, j: (i, j)),
        compiler_params=_params("parallel", "parallel"),
    )(proj, wqk)


def _qknorm_bwd(name, proj, dn, w, dproj, blk0, tm=1024):
    s = proj.shape[0]
    nblk = dn.shape[1] // COL

    def body(p_ref, d_ref, w_ref, _, o_ref, acc_ref):
        pv = p_ref[...].astype(F32)
        dv = d_ref[...]
        wv = w_ref[...]
        sums = []

        def one(qh, h):
            dn = dv[:, h * HD:(h + 1) * HD]
            r = lax.rsqrt(jnp.mean(qh * qh, axis=-1, keepdims=True) + EPS)
            nh = qh * r
            sums.append(jnp.sum(dn * nh, axis=0, keepdims=True))
            dnw = dn * wv[:, h * HD:(h + 1) * HD]
            return r * (dnw - nh * jnp.mean(dnw * nh, axis=-1, keepdims=True))

        o_ref[...] = _heads(pv, one).astype(BF16)

        @pl.when(pl.program_id(1) == 0)
        def _():
            acc_ref[...] = jnp.zeros_like(acc_ref)

        acc_ref[0:1, :] += jnp.concatenate(sums, axis=1)

    return pl.pallas_call(
        body, name=name, grid=(nblk, s // tm),
        out_shape=[jax.ShapeDtypeStruct((s, IN_W), BF16), jax.ShapeDtypeStruct((8, nblk * COL), F32)],
        in_specs=[pl.BlockSpec((tm, COL), lambda j, i: (i, blk0 + j)), pl.BlockSpec((tm, COL), lambda j, i: (i, j)),
                  pl.BlockSpec((1, COL), lambda j, i: (0, j)), pl.BlockSpec(memory_space=pl.ANY)],
        out_specs=[pl.BlockSpec((tm, COL), lambda j, i: (i, blk0 + j)),
                   pl.BlockSpec((8, COL), lambda j, i: (0, j))],
        input_output_aliases={3: 0},
        compiler_params=_params("arbitrary", "arbitrary"),
    )(proj, dn, w, dproj)


def _attn_shapes(s, g):
    d = DILATIONS[g]
    tb = min(s, max(2048, 256 * d))
    sb = min(256, tb // d)
    pb = BAND * d
    assert s % tb == 0 and tb % pb == 0 and (tb // d) % sb == 0 and sb % BAND == 0
    return d, tb, sb, pb


def _lanes(x, width):
    return jnp.concatenate([x] * (width // HD), axis=1)


def _every(start, size, d):
    return pl.ds(start, size, stride=d) if d > 1 else pl.ds(start, size)


def _attn_specs(g, tb, pb, s, ahead):
    ratio = tb // pb
    if ahead:
        nbr = lambda n: jnp.minimum((n + 1) * ratio, s // pb - 1)
    else:
        nbr = lambda n: jnp.maximum(n * ratio - 1, 0)
    cur = lambda base: pl.BlockSpec((tb, HD), lambda h, n: (n, base + g * N_HEADS + h))
    side = lambda base: pl.BlockSpec((pb, HD), lambda h, n: (nbr(n), base + g * N_HEADS + h))
    tok = pl.BlockSpec((tb, HD), lambda h, n: (n, h))
    tok_side = pl.BlockSpec((pb, HD), lambda h, n: (nbr(n), h))
    return cur, side, tok, tok_side


Q_COL, K_COL, V_COL = 0, 12, 24


def _attn_fwd(g, qkn, proj):
    s = qkn.shape[0]
    d, tb, sb, pb = _attn_shapes(s, g)
    ft = F32 if d > 1 else BF16
    nj = tb // d // sb
    scale = HD ** -0.5

    def body(q_ref, kc_ref, kp_ref, vc_ref, vp_ref, o_ref, lse_ref, qf, kf, vf):
        n = pl.program_id(1)
        qf[...] = q_ref[...].astype(ft)
        kf[0:pb] = kp_ref[...].astype(ft)
        kf[pb:] = kc_ref[...].astype(ft)
        vf[0:pb] = vp_ref[...].astype(ft)
        vf[pb:] = vc_ref[...].astype(ft)
        for r in range(d):
            for j in range(nj):
                at = j * sb * d + r
                q = qf[_every(at, sb, d), :].astype(BF16)
                k = kf[_every(at, sb + BAND, d), :].astype(BF16)
                v = vf[_every(at, sb + BAND, d), :].astype(BF16)
                sc = lax.dot_general(q, k, NT_DIMS, preferred_element_type=F32) * scale
                qi = lax.broadcasted_iota(jnp.int32, sc.shape, 0)
                kj = lax.broadcasted_iota(jnp.int32, sc.shape, 1)
                valid = (kj >= qi) & (kj <= qi + BAND)
                if j == 0:
                    valid = valid & ((kj >= BAND) | (n > 0))
                sc = jnp.where(valid, sc, -1e30)
                m = jnp.max(sc, axis=-1, keepdims=True)
                p = jnp.exp(sc - m)
                l = jnp.sum(p, axis=-1, keepdims=True)
                o = lax.dot_general(p.astype(BF16), v, NN_DIMS, preferred_element_type=F32)
                o_ref[_every(at, sb, d), :] = o / l
                lse_ref[_every(at, sb, d), :] = jnp.broadcast_to(m + jnp.log(l), (sb, HD))

    cur, side, tok, _ = _attn_specs(g, tb, pb, s, ahead=False)
    return pl.pallas_call(
        body, name=f"attn_fwd_g{g}", grid=(N_HEADS, s // tb),
        out_shape=[jax.ShapeDtypeStruct((s, COL), F32)] * 2,
        in_specs=[cur(Q_COL), cur(K_COL), side(K_COL), cur(V_COL), side(V_COL)],
        out_specs=[tok, tok],
        scratch_shapes=[pltpu.VMEM((tb, HD), ft), pltpu.VMEM((tb + pb, HD), ft),
                        pltpu.VMEM((tb + pb, HD), ft)],
        compiler_params=_params("parallel", "arbitrary"),
    )(qkn, qkn, qkn, proj, proj)


def _attn_combine(os_, lses, tm=512):
    s = os_[0].shape[0]

    def body(o0, o1, o2, l0, l1, l2, o_ref, lse_ref):
        a, b, c = l0[...], l1[...], l2[...]
        m = jnp.maximum(jnp.maximum(a, b), c)
        ea, eb, ec = jnp.exp(a - m), jnp.exp(b - m), jnp.exp(c - m)
        tot = ea + eb + ec
        o_ref[...] = ((ea * o0[...] + eb * o1[...] + ec * o2[...]) / tot).astype(BF16)
        lse_ref[...] = m + jnp.log(tot)

    return pl.pallas_call(
        body, name="attn_combine", grid=(s // tm,),
        out_shape=[jax.ShapeDtypeStruct((s, COL), BF16), jax.ShapeDtypeStruct((s, COL), F32)],
        in_specs=[_row(tm, COL)] * 6, out_specs=[_row(tm, COL)] * 2,
        compiler_params=_params("parallel"),
    )(*os_, *lses)


def _attn_delta(do, o, tm=512):
    s = do.shape[0]

    def body(do_ref, o_ref, del_ref):
        prod = do_ref[...] * o_ref[...].astype(F32)
        del_ref[...] = _heads(prod, lambda ph, h: jnp.broadcast_to(
            jnp.sum(ph, axis=-1, keepdims=True), ph.shape))

    return pl.pallas_call(
        body, name="attn_delta", grid=(s // tm,),
        out_shape=jax.ShapeDtypeStruct((s, COL), F32),
        in_specs=[_row(tm, COL)] * 2, out_specs=_row(tm, COL),
        compiler_params=_params("parallel"),
    )(do, o)


def _attn_bwd(g, qkn, proj, do, lse, delta, dqn, dkn, dproj):
    s = qkn.shape[0]
    d, tb, sb, pb = _attn_shapes(s, g)
    ft = F32 if d > 1 else BF16
    nj = tb // d // sb
    nt = s // tb
    scale = HD ** -0.5
    chained = dqn is not None

    def body(k_ref, v_ref, qc_ref, qn_ref, doc_ref, don_ref, lc_ref, ln_ref, dc_ref, dn_ref, *rest):
        dq_ref, dk_ref, dv_ref, kf, vf, qf, dvf, later = rest[-8:]
        n = pl.program_id(1)
        kf[...] = k_ref[...].astype(ft)
        vf[...] = v_ref[...].astype(ft)
        qf[0:tb] = qc_ref[...].astype(ft)
        qf[tb:] = qn_ref[...].astype(ft)

        @pl.when(n == 0)
        def _():
            later[...] = jnp.zeros_like(later)

        def window(c_ref, n_ref, r, j):
            at = j * sb * d + r
            if j < nj - 1:
                return c_ref[_every(at, sb + BAND, d), :]
            return jnp.concatenate([c_ref[_every(at, sb, d), :], n_ref[_every(r, BAND, d), :]], axis=0)

        for r in range(d):
            tail = later[r]
            for j in range(nj):
                at = j * sb * d + r
                rows = _every(at, sb, d)
                k = kf[rows, :].astype(BF16)
                v = vf[rows, :].astype(BF16)
                q = qf[_every(at, sb + BAND, d), :].astype(BF16)
                dov = window(doc_ref, don_ref, r, j).astype(BF16)
                sc = lax.dot_general(q, k, NT_DIMS, preferred_element_type=F32) * scale
                qi = lax.broadcasted_iota(jnp.int32, sc.shape, 0)
                kj = lax.broadcasted_iota(jnp.int32, sc.shape, 1)
                valid = (qi >= kj) & (qi <= kj + BAND)
                if j == nj - 1:
                    valid = valid & ((qi < sb) | (n < nt - 1))
                p = jnp.exp(jnp.where(valid, sc - _lanes(window(lc_ref, ln_ref, r, j), sb), -1e30))
                dp = lax.dot_general(dov, v, NT_DIMS, preferred_element_type=F32)
                ds = (p * (dp - _lanes(window(dc_ref, dn_ref, r, j), sb)) * scale).astype(BF16)
                dvf[rows, :] = lax.dot_general(p.astype(BF16), dov, TN_DIMS, preferred_element_type=F32)
                dk_ref[rows, :] = lax.dot_general(ds, q, TN_DIMS, preferred_element_type=F32)
                dqw = lax.dot_general(ds, k, NN_DIMS, preferred_element_type=F32)
                first = dqw[:BAND] + tail
                dq_ref[rows, :] = first if sb == BAND else jnp.concatenate([first, dqw[BAND:sb]], axis=0)
                tail = dqw[sb:]
            later[r] = tail
        dv_ref[...] = dvf[...].astype(BF16)

    cur, side, tok, tok_side = _attn_specs(g, tb, pb, s, ahead=True)
    anyspec = pl.BlockSpec(memory_space=pl.ANY)
    n_heads_cols = 3 * N_HEADS * HD
    return pl.pallas_call(
        body, name=f"attn_bwd_g{g}", grid=(N_HEADS, nt),
        out_shape=[jax.ShapeDtypeStruct((s, n_heads_cols), F32), jax.ShapeDtypeStruct((s, n_heads_cols), F32),
                   jax.ShapeDtypeStruct((s, IN_W), BF16)],
        in_specs=[cur(K_COL), cur(V_COL), cur(Q_COL), side(Q_COL), tok, tok_side, tok, tok_side,
                  tok, tok_side] + ([anyspec, anyspec] if chained else []) + [anyspec],
        out_specs=[cur(0), cur(0), cur(V_COL)],
        input_output_aliases={10: 0, 11: 1, 12: 2} if chained else {10: 2},
        scratch_shapes=[pltpu.VMEM((tb, HD), ft), pltpu.VMEM((tb, HD), ft),
                        pltpu.VMEM((tb + pb, HD), ft), pltpu.VMEM((tb, HD), F32),
                        pltpu.VMEM((d, BAND, HD), F32)],
        compiler_params=_params("arbitrary", "arbitrary"),
    )(qkn, proj, qkn, qkn, do, do, lse, lse, delta, delta, *([dqn, dkn] if chained else []), dproj)


def _shift_down(x, before, k):
    rolled = pltpu.roll(x, k, 0)
    head = jnp.where(lax.broadcasted_iota(jnp.int32, before.shape, 0) < k, pltpu.roll(before, k, 0), rolled[:8])
    return jnp.concatenate([head, rolled[8:]], axis=0)


def _shift_up(x, after, k):
    rows = x.shape[0]
    rolled = pltpu.roll(x, rows - k, 0)
    tail = jnp.where(lax.broadcasted_iota(jnp.int32, after.shape, 0) >= 8 - k,
                     pltpu.roll(after, 8 - k, 0), rolled[rows - 8:])
    return jnp.concatenate([rolled[:rows - 8], tail], axis=0)


def _conv_fwd(proj, cw, tm=1024):
    s = proj.shape[0]
    r16 = tm // 16

    def body(u_ref, b_ref, c_ref, up_ref, cp_ref, w_ref, z_ref):
        i = pl.program_id(1)
        xc = c_ref[...].astype(F32) * u_ref[...].astype(F32)
        xp = jnp.where(i > 0, cp_ref[8:16, :].astype(F32) * up_ref[8:16, :].astype(F32), 0.0)
        w = w_ref[...]
        conv = _shift_down(xc, xp, 2) * w[0:1] + _shift_down(xc, xp, 1) * w[1:2] + xc * w[2:3]
        z_ref[...] = (b_ref[...].astype(F32) * conv).astype(BF16)

    tile = lambda blk: pl.BlockSpec((tm, COL), lambda j, i: (i, blk + j))
    before = lambda blk: pl.BlockSpec((16, COL), lambda j, i: (jnp.maximum(i * r16 - 1, 0), blk + j))
    return pl.pallas_call(
        body, name="conv_fwd", grid=(D // COL, s // tm),
        out_shape=jax.ShapeDtypeStruct((s, D), BF16),
        in_specs=[tile(U_BLK), tile(B_BLK), tile(C_BLK), before(U_BLK), before(C_BLK),
                  pl.BlockSpec((3, COL), lambda j, i: (0, j))],
        out_specs=pl.BlockSpec((tm, COL), lambda j, i: (i, j)),
        compiler_params=_params("parallel", "parallel"),
    )(proj, proj, proj, proj, proj, cw)


def _conv_bwd(dz, proj, cw, dproj, tm=1024):
    s = proj.shape[0]
    r8, r16 = tm // 8, tm // 16
    nrow = s // tm

    def body(dz_ref, u_ref, b_ref, c_ref, up_ref, cp_ref, dzn_ref, bn_ref, w_ref, _, o_ref, acc_ref):
        piece, i = pl.program_id(1), pl.program_id(2)
        u, c = u_ref[...].astype(F32), c_ref[...].astype(F32)
        bv = b_ref[...].astype(F32)
        dzv = dz_ref[...]
        w = w_ref[...]

        @pl.when((piece == 0) & (i == 0))
        def _():
            acc_ref[...] = jnp.zeros_like(acc_ref)

        @pl.when(piece == 1)
        def _():
            xc = c * u
            xp = jnp.where(i > 0, cp_ref[8:16, :].astype(F32) * up_ref[8:16, :].astype(F32), 0.0)
            x2, x1 = _shift_down(xc, xp, 2), _shift_down(xc, xp, 1)
            o_ref[...] = (dzv * (x2 * w[0:1] + x1 * w[1:2] + xc * w[2:3])).astype(BF16)
            dconv = dzv * bv
            acc_ref[0:1, :] += jnp.sum(dconv * x2, axis=0, keepdims=True)
            acc_ref[1:2, :] += jnp.sum(dconv * x1, axis=0, keepdims=True)
            acc_ref[2:3, :] += jnp.sum(dconv * xc, axis=0, keepdims=True)

        @pl.when(piece != 1)
        def _():
            dconv = dzv * bv
            dn = jnp.where(i < nrow - 1, dzn_ref[...] * bn_ref[0:8, :].astype(F32), 0.0)
            dxc = dconv * w[2:3] + _shift_up(dconv, dn, 1) * w[1:2] + _shift_up(dconv, dn, 2) * w[0:1]
            o_ref[...] = (dxc * jnp.where(piece == 0, c, u)).astype(BF16)

    tile = lambda blk: pl.BlockSpec((tm, COL), lambda j, p, i: (i, blk + j))
    before = lambda blk: pl.BlockSpec((16, COL), lambda j, p, i: (jnp.maximum(i * r16 - 1, 0), blk + j))
    after = lambda rows, blk: pl.BlockSpec(
        (rows, COL), lambda j, p, i: (jnp.minimum((i + 1) * (tm // rows), s // rows - 1), blk + j))
    return pl.pallas_call(
        body, name="conv_bwd", grid=(D // COL, 3, nrow),
        out_shape=[jax.ShapeDtypeStruct((s, IN_W), BF16), jax.ShapeDtypeStruct((8, D), F32)],
        in_specs=[tile(0), tile(U_BLK), tile(B_BLK), tile(C_BLK), before(U_BLK), before(C_BLK),
                  after(8, 0), after(16, B_BLK), pl.BlockSpec((3, COL), lambda j, p, i: (0, j)),
                  pl.BlockSpec(memory_space=pl.ANY)],
        out_specs=[pl.BlockSpec((tm, COL), lambda j, p, i: (i, U_BLK + 2 * p + j)),
                   pl.BlockSpec((8, COL), lambda j, p, i: (0, j))],
        input_output_aliases={9: 0},
        compiler_params=_params("arbitrary", "arbitrary", "arbitrary"),
    )(dz, proj, proj, proj, proj, proj, dz, proj, cw, dproj)


def _merge_fwd(ya, yc, proj, tm=512):
    s = proj.shape[0]

    def body(ya_ref, yc_ref, ga_ref, gc_ref, o_ref):
        o_ref[...] = (_sigmoid(ga_ref[...].astype(F32)) * ya_ref[...].astype(F32)
                      + _sigmoid(gc_ref[...].astype(F32)) * yc_ref[...].astype(F32)).astype(BF16)

    tile = lambda blk: pl.BlockSpec((tm, COL), lambda j, i: (i, blk + j))
    return pl.pallas_call(
        body, name="merge_fwd", grid=(D // COL, s // tm),
        out_shape=jax.ShapeDtypeStruct((s, D), BF16),
        in_specs=[tile(0), tile(0), tile(GA_BLK), tile(GC_BLK)], out_specs=tile(0),
        compiler_params=_params("parallel", "parallel"),
    )(ya, yc, proj, proj)


def _merge_bwd_branches(dm, proj, tm=512):
    s = proj.shape[0]

    def body(dm_ref, ga_ref, gc_ref, dya_ref, dyc_ref):
        dmv = dm_ref[...]
        dya_ref[...] = (dmv * _sigmoid(ga_ref[...].astype(F32))).astype(BF16)
        dyc_ref[...] = (dmv * _sigmoid(gc_ref[...].astype(F32))).astype(BF16)

    tile = lambda blk: pl.BlockSpec((tm, COL), lambda j, i: (i, blk + j))
    return pl.pallas_call(
        body, name="merge_bwd_branches", grid=(D // COL, s // tm),
        out_shape=[jax.ShapeDtypeStruct((s, D), BF16)] * 2,
        in_specs=[tile(0), tile(GA_BLK), tile(GC_BLK)], out_specs=[tile(0)] * 2,
        compiler_params=_params("parallel", "parallel"),
    )(dm, proj, proj)


def _merge_bwd_gates(dm, ya, yc, proj, tm=1024):
    s = proj.shape[0]
    half = D // COL

    def body(dm_ref, ya_ref, yc_ref, g_ref, o_ref):
        y = jnp.where(pl.program_id(0) < half, ya_ref[...].astype(F32), yc_ref[...].astype(F32))
        sig = _sigmoid(g_ref[...].astype(F32))
        o_ref[...] = (dm_ref[...] * y * sig * (1.0 - sig)).astype(BF16)

    chan = pl.BlockSpec((tm, COL), lambda jj, i: (i, jj % half))
    gate = pl.BlockSpec((tm, COL), lambda jj, i: (i, GA_BLK + jj))
    return pl.pallas_call(
        body, name="merge_bwd_gates", grid=(2 * half, s // tm),
        out_shape=jax.ShapeDtypeStruct((s, IN_W), BF16),
        in_specs=[chan, chan, chan, gate], out_specs=gate,
        compiler_params=_params("parallel", "parallel"),
    )(dm, ya, yc, proj)


def _mod_part(c_all, w_ada, b_part):
    def body(c_ref, w_ref, b_ref, o_ref):
        cv = c_ref[...]
        act = cv * _sigmoid(cv)
        o_ref[...] = jnp.dot(act, w_ref[...], preferred_element_type=F32,
                             precision=lax.Precision.HIGHEST) + b_ref[...]

    return pl.pallas_call(
        body, name="mod_part", out_shape=jax.ShapeDtypeStruct((N_DEV, w_ada.shape[1]), F32),
    )(c_all, w_ada, b_part)


def _w_ada_grad(c_all_t, dmod_part):
    def body(c_ref, d_ref, o_ref):
        cv = c_ref[...]
        act = cv * _sigmoid(cv)
        dv = d_ref[...]
        acc = act[:, 0:1] * dv[0:1, :]
        for b in range(1, N_DEV):
            acc = acc + act[:, b:b + 1] * dv[b:b + 1, :]
        o_ref[...] = acc

    return pl.pallas_call(
        body, name="w_ada_grad", out_shape=jax.ShapeDtypeStruct((D, dmod_part.shape[1]), F32),
    )(c_all_t, dmod_part)


def _sum_rows(name, v):
    def body(v_ref, o_ref):
        acc = v_ref[0]
        for k in range(1, N_DEV):
            acc = acc + v_ref[k]
        o_ref[...] = acc

    return pl.pallas_call(body, name=name, out_shape=jax.ShapeDtypeStruct(v.shape[1:], F32))(v)


def _adamw(name, w, g, m, v):
    rows, cols = w.shape
    limit = max(16, (1 << 20) // (4 * cols))
    tr = rows if rows <= limit else next((t for t in range(limit - limit % 16, 15, -16) if rows % t == 0), rows)
    c1 = 1.0 - ADAM_B1 ** ADAM_STEP
    c2 = 1.0 - ADAM_B2 ** ADAM_STEP
    parts = g.ndim == 3

    def body(w_ref, g_ref, m_ref, v_ref, go_ref, d_ref, nm_ref, nv_ref):
        if parts:
            gv = g_ref[0].astype(F32)
            for k in range(1, N_DEV):
                gv = gv + g_ref[k].astype(F32)
        else:
            gv = g_ref[...]
        go_ref[...] = gv
        nm = ADAM_B1 * m_ref[...] + (1.0 - ADAM_B1) * gv
        nv = ADAM_B2 * v_ref[...] + (1.0 - ADAM_B2) * (gv * gv)
        nm_ref[...] = nm
        nv_ref[...] = nv
        d_ref[...] = -ADAM_LR * ((nm / c1) / (jnp.sqrt(nv / c2) + ADAM_EPS) + ADAM_WD * w_ref[...])

    spec = pl.BlockSpec((tr, cols), lambda i: (i, 0))
    g_spec = pl.BlockSpec((N_DEV, tr, cols), lambda i: (0, i, 0)) if parts else spec
    return pl.pallas_call(
        body, name=name, grid=(rows // tr,),
        out_shape=[jax.ShapeDtypeStruct((rows, cols), F32)] * 4,
        in_specs=[spec, g_spec, spec, spec], out_specs=[spec] * 4,
        compiler_params=_params("parallel"),
    )(w, g, m, v)


def _adamw_small(ws, gs, ms, vs):
    n = len(ws)
    c1 = 1.0 - ADAM_B1 ** ADAM_STEP
    c2 = 1.0 - ADAM_B2 ** ADAM_STEP

    def body(*refs):
        for i in range(n):
            w_ref, g_ref, m_ref, v_ref = refs[i], refs[n + i], refs[2 * n + i], refs[3 * n + i]
            d_ref, nm_ref, nv_ref = refs[4 * n + 3 * i:4 * n + 3 * i + 3]
            gv = g_ref[...]
            nm = ADAM_B1 * m_ref[...] + (1.0 - ADAM_B1) * gv
            nv = ADAM_B2 * v_ref[...] + (1.0 - ADAM_B2) * (gv * gv)
            nm_ref[...] = nm
            nv_ref[...] = nv
            d_ref[...] = -ADAM_LR * ((nm / c1) / (jnp.sqrt(nv / c2) + ADAM_EPS) + ADAM_WD * w_ref[...])

    outs = pl.pallas_call(
        body, name="adamw_small",
        out_shape=[jax.ShapeDtypeStruct(w.shape, F32) for w in ws for _ in range(3)],
    )(*ws, *gs, *ms, *vs)
    return [tuple(outs[3 * i:3 * i + 3]) for i in range(n)]


HALF = FF // 2


def _sds(shape, dtype):
    return jax.ShapeDtypeStruct(shape, dtype)


def _row_tile(w):
    return lambda tm: ((tm, w), lambda i, j: (i, 0))


def _one(w):
    return lambda rows: ((rows, w), lambda i, j: (0, 0))


def _gate_up_swiglu(name, h, wgu, carry=None, tm=512):
    s = h.shape[0]
    tm = min(tm, s)

    def epilogue(prod, first, tin, tout):
        pq_ref, s_ref = tout
        a, b = prod[:, :HALF], prod[:, HALF:]
        sig = _sigmoid(a)
        act = a * sig
        pq_ref[:, :HALF] = (b * (sig * (1.0 + a * (1.0 - sig)))).astype(BF16)
        pq_ref[:, HALF:] = act.astype(BF16)
        s_ref[...] = (act * b).astype(BF16)

    return _mm(name, h, wgu, "NT", None, tm, FF, D, carry=carry, n_outer=True, epilogue=epilogue,
               tiles_out=[(_sds((s, 2 * FF), BF16), (tm, FF), lambda i, j: (i, j)),
                          (_sds((s, FF), BF16), (tm, HALF), lambda i, j: (i, j))])


def _d_hidden_swiglu(name, df, wd, ab, tm=512):
    s = df.shape[0]
    tm = min(tm, s)

    def epilogue(prod, first, tin, tout, cols):
        da_cols = slice(cols[0], cols[0] + cols[1])
        db_cols = slice(HALF + cols[0], HALF + cols[0] + cols[1])
        tout[0][:, da_cols] = (prod * tin[0][:, da_cols].astype(F32)).astype(BF16)
        tout[0][:, db_cols] = (prod * tin[0][:, db_cols].astype(F32)).astype(BF16)

    chunks = [(c0, min(384, HALF - c0)) for c0 in range(0, HALF, 384)]
    return _mm(name, df, wd, "NT", None, tm, HALF, D, n_outer=True, epilogue=epilogue, col_chunks=chunks,
               tiles_in=[(ab, (tm, FF), lambda i, j: (i, j))],
               tiles_out=[(_sds((s, 2 * FF), BF16), (tm, FF), lambda i, j: (i, j))])[0]


def _out_residual(name, a, w, x, gt, coef, nxt, tm=512, tk=FF):
    s = a.shape[0]
    tm = min(tm, s)

    def epilogue(prod, first, tin, tout):
        x_ref, gt_ref, g_ref, sc_ref, sh_ref = tin
        f_ref, xn_ref, h_ref = tout
        f_ref[...] = prod
        xn = x_ref[...] + (coef * gt_ref[...]) * prod
        xn_ref[...] = xn
        r = lax.rsqrt(jnp.mean(xn * xn, axis=-1, keepdims=True) + EPS)
        h_ref[...] = ((xn * r) * g_ref[...] * (1.0 + sc_ref[...]) + sh_ref[...]).astype(BF16)

    row, vec = _row_tile(D)(tm), _one(D)(1)
    return _mm(name, a, w, "NN", None, tm, D, tk, epilogue=epilogue,
               tiles_in=[(x, *row), (gt, *vec)] + [(v, *vec) for v in nxt],
               tiles_out=[(_sds((s, D), F32), *row), (_sds((s, D), F32), *row), (_sds((s, D), BF16), *row)])


def _out_loss(name, a, w, x, gt, coef, target, tm=512):
    s = a.shape[0]
    tm = min(tm, s)

    def epilogue(prod, first, tin, tout):
        x_ref, gt_ref, t_ref = tin
        f_ref, g_ref, df_ref, acc_ref = tout
        f_ref[...] = prod
        cg = coef * gt_ref[...]
        e = x_ref[...] + cg * prod - t_ref[...]
        gv = e * (1.0 / D)
        g_ref[...] = gv
        df_ref[...] = (cg * gv).astype(BF16)

        @pl.when(first)
        def _():
            acc_ref[...] = jnp.zeros_like(acc_ref)

        acc_ref[0:1, :] += coef * jnp.sum(gv * prod, axis=0, keepdims=True)
        acc_ref[1:2, :] += (0.5 / D) * jnp.sum(e * e, axis=0, keepdims=True)

    row, vec = _row_tile(D)(tm), _one(D)(1)
    return _mm(name, a, w, "NN", None, tm, D, FF, epilogue=epilogue,
               tiles_in=[(x, *row), (gt, *vec), (target, *row)],
               tiles_out=[(_sds((s, D), F32), *row), (_sds((s, D), F32), *row), (_sds((s, D), BF16), *row),
                          (_sds((8, D), F32), *_one(D)(8))])


def _d_h_norm_bwd(name, da, w, x, gin, g, sc, sh, before=None, carry=None, tm=256):
    s = da.shape[0]
    tm = min(tm, s)
    coef = before[2] if before else None

    def epilogue(prod, first, tin, tout):
        x_ref, gin_ref, g_ref, sc_ref, sh_ref = tin[:5]
        gout_ref, acc_ref = tout[:2]
        xv = x_ref[...]
        r = lax.rsqrt(jnp.mean(xv * xv, axis=-1, keepdims=True) + EPS)
        nv = xv * r
        gv, one_sc = g_ref[...], 1.0 + sc_ref[...]
        dn = prod * gv * one_sc
        gout = gin_ref[...] + r * (dn - nv * jnp.mean(dn * nv, axis=-1, keepdims=True))
        gout_ref[...] = gout

        @pl.when(first)
        def _():
            acc_ref[...] = jnp.zeros_like(acc_ref)

        dhn = prod * nv
        acc_ref[0:1, :] += jnp.sum(prod, axis=0, keepdims=True)
        acc_ref[1:2, :] += jnp.sum(dhn * gv, axis=0, keepdims=True)
        acc_ref[2:3, :] += jnp.sum(dhn * one_sc, axis=0, keepdims=True)
        if before:
            f_ref, gt_ref = tin[5:]
            tout[2][...] = ((coef * gt_ref[...]) * gout).astype(BF16)
            acc_ref[3:4, :] += coef * jnp.sum(gout * f_ref[...], axis=0, keepdims=True)

    row, vec = _row_tile(D)(tm), _one(D)(1)
    tiles_in = [(x, *row), (gin, *row), (g, *vec), (sc, *vec), (sh, *vec)]
    tiles_out = [(_sds((s, D), F32), *row), (_sds((8, D), F32), *_one(D)(8))]
    if before:
        tiles_in += [(before[0], *row), (before[1], *vec)]
        tiles_out.append((_sds((s, D), BF16), *row))
    return _mm(name, da, w, "NN", None, tm, D, da.shape[1], epilogue=epilogue, carry=carry, keep_b=True,
               tiles_in=tiles_in, tiles_out=tiles_out)


def _gate_tiles(proj, tm):
    return [(proj, (tm, COL), (lambda i, j, blk=blk: (i, blk))) for blk in (GA_BLK, GA_BLK + 1, GC_BLK, GC_BLK + 1)]


def _conv_branch_merge(z, wc, ya, proj, tm=512):
    s = z.shape[0]
    tm = min(tm, s)

    def epilogue(prod, first, tin, tout):
        ya_ref, ga0, ga1, gc0, gc1 = tin
        tout[0][...] = prod.astype(BF16)
        for half, (ga, gc) in enumerate(((ga0, gc0), (ga1, gc1))):
            cols = slice(half * COL, (half + 1) * COL)
            tout[1][:, cols] = (_sigmoid(ga[...].astype(F32)) * ya_ref[:, cols].astype(F32)
                                + _sigmoid(gc[...].astype(F32)) * prod[:, cols]).astype(BF16)

    row = _row_tile(D)(tm)
    return _mm("mix_conv_branch", z, wc, "NN", None, tm, D, D, epilogue=epilogue,
               tiles_in=[(ya, *row)] + _gate_tiles(proj, tm),
               tiles_out=[(_sds((s, D), BF16), *row), (_sds((s, D), BF16), *row)])


def _d_merged_branches(dmix, wo, proj, tm=512):
    s = dmix.shape[0]
    tm = min(tm, s)

    def epilogue(prod, first, tin, tout):
        ga0, ga1, gc0, gc1 = tin
        tout[0][...] = prod
        for half, (ga, gc) in enumerate(((ga0, gc0), (ga1, gc1))):
            cols = slice(half * COL, (half + 1) * COL)
            tout[1][:, cols] = (prod[:, cols] * _sigmoid(ga[...].astype(F32))).astype(BF16)
            tout[2][:, cols] = (prod[:, cols] * _sigmoid(gc[...].astype(F32))).astype(BF16)

    row = _row_tile(D)(tm)
    return _mm("mix_d_merged", dmix, wo, "NT", None, tm, D, D, epilogue=epilogue,
               tiles_in=_gate_tiles(proj, tm),
               tiles_out=[(_sds((s, D), F32), *row), (_sds((s, D), BF16), *row), (_sds((s, D), BF16), *row)])


def _d_o_delta(dya, wa_t, o, tm=1024):
    s = dya.shape[0]
    tm = min(tm, s)

    def epilogue(prod, first, tin, tout):
        tout[0][...] = prod
        tout[1][...] = _heads(prod * tin[0][...].astype(F32), lambda ph, h: jnp.broadcast_to(
            jnp.sum(ph, axis=-1, keepdims=True), ph.shape))

    row = _row_tile(COL)(tm)
    return _mm("mix_d_o", dya, wa_t, "NN", None, tm, COL, D, epilogue=epilogue,
               tiles_in=[(o, *row)], tiles_out=[(_sds((s, COL), F32), *row), (_sds((s, COL), F32), *row)])


def _ffn_bwd(tag, df, x, gin, h, ab, sw, g, sc, sh, wgu, wd, before=None, carry_down=None, carry_gate_up=None,
             tk_dw=2048):
    dab = _d_hidden_swiglu(f"{tag}_d_hidden", df, wd, ab)
    dwd = _mm(f"{tag}_dw_down", sw, df, "TN", BF16, HALF, D, tk_dw)
    carried = []
    if carry_down:
        dwgu, *got = _mm(f"{tag}_dw_gate_up", dab, h, "TN", BF16, HALF, D, tk_dw, carry=carry_down(dwd))
        carried += got
    else:
        dwgu = _mm(f"{tag}_dw_gate_up", dab, h, "TN", BF16, HALF, D, tk_dw)
    res = _d_h_norm_bwd(f"{tag}_d_h", dab, wgu, x, gin, g, sc, sh, before=before,
                        carry=carry_gate_up(dwgu) if carry_gate_up else None)
    n_own = 3 if before else 2
    return res[:n_own], dwgu, dwd, carried + list(res[n_own:])


def kernel(x, c, w_ada, b_ada, norm_ffn1, ffn1_w_gate, ffn1_w_up, ffn1_w_down, norm_mix, w_in, q_norm, k_norm, conv_w, w_attn_branch, w_conv_branch, w_out, norm_ffn2, ffn2_w_gate, ffn2_w_up, ffn2_w_down, loss_target, m_w_ada, m_b_ada, m_norm_ffn1, m_ffn1_w_gate, m_ffn1_w_up, m_ffn1_w_down, m_norm_mix, m_w_in, m_q_norm, m_k_norm, m_conv_w, m_w_attn_branch, m_w_conv_branch, m_w_out, m_norm_ffn2, m_ffn2_w_gate, m_ffn2_w_up, m_ffn2_w_down, v_w_ada, v_b_ada, v_norm_ffn1, v_ffn1_w_gate, v_ffn1_w_up, v_ffn1_w_down, v_norm_mix, v_w_in, v_q_norm, v_k_norm, v_conv_w, v_w_attn_branch, v_w_conv_branch, v_w_out, v_norm_ffn2, v_ffn2_w_gate, v_ffn2_w_up, v_ffn2_w_down):
    me = 4 * lax.axis_index("x") + 2 * lax.axis_index("y") + lax.axis_index("c")
    x0, target = x[0], loss_target[0]
    s = x0.shape[0]
    ada_cols = w_ada.shape[2]
    cw_cols = conv_w.shape[2]

    gathered = _small_allgather(
        "gather_c_conv", jnp.concatenate([c, conv_w[0].reshape(1, 3 * cw_cols)], axis=1))[:, 0]
    c_all = gathered[:, :D]
    cw = gathered[:, D:].reshape(N_DEV, 3, cw_cols).transpose(1, 0, 2).reshape(3, D)
    b_part = lax.dynamic_slice(b_ada, (0, me * ada_cols), (1, ada_cols))
    mod_part = _mod_part(c_all, w_ada[0], b_part)
    mod_all = _small_allgather("gather_mod", mod_part.reshape(1, N_DEV * ada_cols))
    mod = lax.dynamic_slice(mod_all.reshape(N_DEV, N_DEV, ada_cols), (0, me, 0), (N_DEV, 1, ada_cols))
    mod = mod.reshape(N_MOD, 1, D)
    sh1, sc1, gt1, sh2, sc2, gt2, sh3, sc3, gt3 = [mod[i] for i in range(N_MOD)]

    tb = lambda w: w[0].T.astype(BF16)
    nb = lambda w: w[0].astype(BF16)
    ffn1_shards = [tb(ffn1_w_gate), tb(ffn1_w_up), nb(ffn1_w_down)]
    ffn2_shards = [tb(ffn2_w_gate), tb(ffn2_w_up), nb(ffn2_w_down)]
    mix_shards = [tb(w_in), tb(w_attn_branch), nb(w_conv_branch), nb(w_out)]
    ffn_dst, ffn_base, ffn_jump, ffn_shapes = [0, 0, 1], [0, HALF, 0], [HALF, HALF, 0], [(2 * FF, D), (FF, D)]
    mix_dst, mix_base, mix_shapes = [0, 1, 2, 3], [0, 0, 0, 0], [(IN_W, D), (D, COL), (D, D), (D, D)]
    wgu1, wd1 = _run_plan_on_sequencer(
        "gather_ffn1_weights", _gather_plan(ffn1_shards, ffn_dst, ffn_base, ffn_shapes, ffn_jump), 1)
    win_t, wa_t, wc, wo = _run_plan_on_sequencer(
        "gather_mix_weights", _gather_plan(mix_shards, mix_dst, mix_base, mix_shapes), 2)
    wgu2, wd2 = _run_plan_on_sequencer(
        "gather_ffn2_weights", _gather_plan(ffn2_shards, ffn_dst, ffn_base, ffn_shapes, ffn_jump), 3)

    h1 = _normmod("ffn1_normmod", x0, norm_ffn1, sc1, sh1)
    ab1, s1 = _gate_up_swiglu("ffn1_gate_up", h1, wgu1)
    f1, x1, h2 = _out_residual("ffn1_down", s1, wd1, x0, gt1, 0.5, (norm_mix, sc2, sh2))
    proj = _mm("mix_in_proj", h2, win_t, "NT", BF16, 1024, IN_W // 4, D, n_outer=True)
    wqk = jnp.concatenate([jnp.tile(q_norm, (1, 12)), jnp.tile(k_norm, (1, 12))], axis=1)
    qkn = _qknorm(proj, wqk)
    group_out = [_attn_fwd(g, qkn, proj) for g in range(3)]
    o, lse = _attn_combine([go[0] for go in group_out], [go[1] for go in group_out])
    ya = _mm("mix_attn_branch", o, wa_t, "NT", BF16, 1024, 1024, COL)
    z = _conv_fwd(proj, cw)
    yc, merged = _conv_branch_merge(z, wc, ya, proj)
    mix, x2, h3 = _out_residual("mix_out_proj", merged, wo, x1, gt2, 1.0, (norm_ffn2, sc3, sh3), tk=D)
    ab3, s3 = _gate_up_swiglu("ffn2_gate_up", h3, wgu2)
    f3, g3, df3, acc_out = _out_loss("ffn2_down", s3, wd2, x2, gt3, 0.5, target)
    loss_part = jnp.sum(acc_out[1])

    ffn_rows = [sh_.shape[0] for sh_ in ffn1_shards]
    mix_rows = [sh_.shape[0] for sh_ in mix_shards]
    (g2, acc3, dmix), dwgu2, dwd2, _ = _ffn_bwd(
        "ffn2", df3, x2, g3, h3, ab3, s3, norm_ffn2, sc3, sh3, wgu2, wd2, before=(mix, gt2, 1.0))
    dmerged, dya, dyc = _d_merged_branches(dmix, wo, proj)
    dwo = _mm("mix_dw_out", merged, dmix, "TN", BF16, 1024, 1024, 2048)
    dproj = _merge_bwd_gates(dmerged, ya, yc, proj)
    dwc = _mm("mix_dw_conv_branch", z, dyc, "TN", BF16, 1024, 1024, 2048)
    dz = _mm("mix_d_z", dyc, wc, "NT", F32, 1024, 1024, D)
    dproj, cw_acc = _conv_bwd(dz, proj, cw, dproj)
    dwa_t = _mm("mix_dw_attn_branch", dya, o, "TN", BF16, 1024, COL, 2048)
    do, delta = _d_o_delta(dya, wa_t, o)
    dqn = dkn = None
    for g in range(3):
        dqn, dkn, dproj = _attn_bwd(g, qkn, proj, do, lse, delta, dqn, dkn, dproj)
    dproj, wq_acc = _qknorm_bwd("qnorm_bwd", proj, dqn, wqk[:, :QKW // 2], dproj, 0)
    dproj, wk_acc = _qknorm_bwd("knorm_bwd", proj, dkn, wqk[:, QKW // 2:], dproj, QKW // 2 // COL)
    r_f2g, r_f2u, r_f2d, r_wa, r_wc, r_wo = _run_plan_on_sequencer(
        "scatter_ffn2_and_branch_grads",
        _scatter_plan([dwgu2, dwd2, dwa_t, dwc, dwo], [0, 0, 1, 2, 3, 4], [0, HALF, 0, 0, 0, 0],
                      ffn_rows + mix_rows[1:], [D, D, D, COL, D, D], [HALF, HALF, 0, 0, 0, 0]), 4)
    dwin_t = _mm("mix_dw_in", dproj, h2, "TN", BF16, IN_W // 4, COL, 2048)
    (r_win,) = _run_plan_on_sequencer(
        "scatter_w_in_grad", _scatter_plan([dwin_t], [0], [0], mix_rows[:1], [D]), 5)
    g1, acc2, df1 = _d_h_norm_bwd("mix_d_h", dproj, win_t, x1, g2, norm_mix, sc2, sh2, before=(f1, gt1, 0.5))
    (g0, acc1), dwgu1, dwd1, _ = _ffn_bwd(
        "ffn1", df1, x0, g1, h1, ab1, s1, norm_ffn1, sc1, sh1, wgu1, wd1)
    (r_f1d,) = _run_plan_on_sequencer(
        "scatter_ffn1_down_grad", _scatter_plan([dwd1], [0], [0], ffn_rows[2:], [D]), 6)
    r_f1g, r_f1u = _run_plan_on_sequencer(
        "scatter_ffn1_gate_up_grads",
        _scatter_plan([dwgu1], [0, 0], [0, HALF], ffn_rows[:2], [D, D], [HALF, HALF]), 7)

    dqw = jnp.sum(wq_acc[0].reshape(12, HD), axis=0)
    dkw = jnp.sum(wk_acc[0].reshape(12, HD), axis=0)
    small = jnp.concatenate([
        acc1[0], acc1[1], acc2[3], acc2[0], acc2[1], acc3[3], acc3[0], acc3[1], acc_out[0],
        acc1[2], acc2[2], acc3[2], dqw, dkw, cw_acc[0:3].reshape(3 * D),
        jnp.zeros((HD,), F32).at[0].set(loss_part)]).reshape(1, -1)
    small_all = _small_allgather("gather_small_grads", small)
    small_sum = _sum_rows("sum_small_grads", small_all)[0]
    n_mod = N_MOD * D
    g_b_ada = small_sum[:n_mod].reshape(1, n_mod)
    g_norm1, g_norm2, g_norm3 = [small_sum[n_mod + i * D:n_mod + (i + 1) * D].reshape(1, D) for i in range(3)]
    off = n_mod + 3 * D
    g_qn, g_kn = small_sum[off:off + HD].reshape(1, HD), small_sum[off + HD:off + 2 * HD].reshape(1, HD)
    g_cw_full = small_sum[off + 2 * HD:off + 2 * HD + 3 * D].reshape(3, D)
    loss = small_sum[off + 2 * HD + 3 * D]
    g_cw = lax.dynamic_slice(g_cw_full, (0, me * cw_cols), (3, cw_cols))
    dmod_part = lax.dynamic_slice(small_all[:, 0, :n_mod], (0, me * ada_cols), (N_DEV, ada_cols))
    g_w_ada = _w_ada_grad(c_all.T, dmod_part)

    as_rows = {"ffn1_w_gate", "ffn1_w_up", "w_in", "w_attn_branch", "ffn2_w_gate", "ffn2_w_up"}
    grad_list = [g_w_ada, g_b_ada, g_norm1, r_f1g, r_f1u, r_f1d, g_norm2, r_win,
                 g_qn, g_kn, g_cw, r_wa, r_wc, r_wo, g_norm3, r_f2g, r_f2u, r_f2d]
    weights = [w_ada, b_ada, norm_ffn1, ffn1_w_gate, ffn1_w_up, ffn1_w_down, norm_mix, w_in, q_norm, k_norm,
               conv_w, w_attn_branch, w_conv_branch, w_out, norm_ffn2, ffn2_w_gate, ffn2_w_up, ffn2_w_down]
    ms = [m_w_ada, m_b_ada, m_norm_ffn1, m_ffn1_w_gate, m_ffn1_w_up, m_ffn1_w_down, m_norm_mix, m_w_in, m_q_norm,
          m_k_norm, m_conv_w, m_w_attn_branch, m_w_conv_branch, m_w_out, m_norm_ffn2, m_ffn2_w_gate,
          m_ffn2_w_up, m_ffn2_w_down]
    vs = [v_w_ada, v_b_ada, v_norm_ffn1, v_ffn1_w_gate, v_ffn1_w_up, v_ffn1_w_down, v_norm_mix, v_w_in, v_q_norm,
          v_k_norm, v_conv_w, v_w_attn_branch, v_w_conv_branch, v_w_out, v_norm_ffn2, v_ffn2_w_gate,
          v_ffn2_w_up, v_ffn2_w_down]
    wnames = ["w_ada", "b_ada", "norm_ffn1", "ffn1_w_gate", "ffn1_w_up", "ffn1_w_down", "norm_mix", "w_in",
              "q_norm", "k_norm", "conv_w", "w_attn_branch", "w_conv_branch", "w_out", "norm_ffn2",
              "ffn2_w_gate", "ffn2_w_up", "ffn2_w_down"]
    small = [i for i, gr in enumerate(grad_list) if gr.ndim == 2 and gr.size <= 16384]
    flat = lambda a, i: a.reshape(-1, weights[i].shape[-1])
    small_res = dict(zip(small, _adamw_small(
        [flat(weights[i], i) for i in small], [flat(grad_list[i], i) for i in small],
        [flat(ms[i], i) for i in small], [flat(vs[i], i) for i in small])))
    grad_out, deltas, new_ms, new_vs = [], [], [], []
    for idx, (nm, w, gr, m_, v_) in enumerate(zip(wnames, weights, grad_list, ms, vs)):
        if idx in small_res:
            gr, dl, nm_, nv_ = [r.reshape(w.shape) for r in (gr, *small_res[idx])]
        elif nm in as_rows:
            res = _adamw(f"adamw_{nm}", w[0].T, gr, m_[0].T, v_[0].T)
            gr, dl, nm_, nv_ = [r.T[None] for r in res]
        else:
            two_d = (-1, w.shape[-1])
            res = _adamw(f"adamw_{nm}", w.reshape(two_d), gr if gr.ndim == 3 else gr.reshape(two_d),
                         m_.reshape(two_d), v_.reshape(two_d))
            gr, dl, nm_, nv_ = [r.reshape(w.shape) for r in res]
        grad_out.append(gr)
        deltas.append(dl)
        new_ms.append(nm_)
        new_vs.append(nv_)
    return (loss, g0[None], *grad_out, *deltas, *new_ms, *new_vs)
```

```python
import functools

import jax
import jax.numpy as jnp
from jax import lax
from jax.experimental import pallas as pl
from jax.experimental.pallas import tpu as pltpu
from jax.experimental.pallas import tpu_sc as plsc

F32 = jnp.float32
BF16 = jnp.bfloat16
MESH = pl.DeviceIdType.MESH

N_DEV = 8
D = 1024
FF = 2816
HD = 128
N_HEADS = 4
DILATIONS = (1, 4, 16)
BAND = 128
QKW = 2 * 3 * N_HEADS * HD
IN_W = 9728
COL = 512
V_BLK, U_BLK, B_BLK, C_BLK, GA_BLK, GC_BLK = 6, 9, 11, 13, 15, 17
EPS = 1e-6
N_MOD = 9
ADAM_LR, ADAM_B1, ADAM_B2, ADAM_EPS, ADAM_WD, ADAM_STEP = 0.001, 0.9, 0.999, 1e-08, 0.01, 10

NT_DIMS = (((1,), (1,)), ((), ()))
TN_DIMS = (((0,), (0,)), ((), ()))
NN_DIMS = (((1,), (0,)), ((), ()))


def _place():
    return lax.axis_index("x"), lax.axis_index("y"), lax.axis_index("c")


def _flip(coord, bit):
    return 1 - coord if bit else coord


def _params(*sem):
    return pltpu.CompilerParams(dimension_semantics=sem)


def _small_allgather(name, v):
    n = v.shape[-1]

    def body(v_ref, out_ref, send_sems, recv_sems):
        x, y, c = _place()
        me = 4 * x + 2 * y + c
        out_ref[me] = v_ref[...]
        copies = []
        for k in range(1, N_DEV):
            peer = (_flip(x, (k >> 2) & 1), _flip(y, (k >> 1) & 1), _flip(c, k & 1))
            cp = pltpu.make_async_remote_copy(
                src_ref=v_ref, dst_ref=out_ref.at[me], send_sem=send_sems.at[k - 1],
                recv_sem=recv_sems.at[k - 1], device_id=peer, device_id_type=MESH)
            cp.start()
            copies.append(cp)
        for cp in copies:
            cp.wait()

    return pl.pallas_call(
        body, name=name,
        out_shape=jax.ShapeDtypeStruct((N_DEV, 1, n), F32),
        in_specs=[pl.BlockSpec(memory_space=pltpu.VMEM)],
        out_specs=pl.BlockSpec(memory_space=pltpu.VMEM),
        scratch_shapes=[pltpu.SemaphoreType.DMA((N_DEV - 1,)), pltpu.SemaphoreType.DMA((N_DEV - 1,))],
    )(v)


class _Plan:
    def __init__(self, operands, out_shapes, sems, phases):
        self.operands, self.out_shapes, self.sems, self.phases = operands, out_shapes, sems, phases


def _slab_start(base, rows, jump, idx):
    return pl.multiple_of(base + idx * rows + (idx // 4) * jump, 16)


def _gather_plan(shards, dst_of, base_of, dst_shapes, jump_of=None):
    n = len(shards)
    rows = [s.shape[0] for s in shards]
    jump_of = jump_of or [0] * n

    def phases(srcs, dsts, sems):
        send_sems, recv_sems, local_sems = sems
        x, y, c = _place()
        me, sibling = (x, y, c), (x, y, 1 - c)
        chips = [(1 - x, y), (x, 1 - y), (1 - x, 1 - y)]

        def slab(i, px, py, pc):
            start = _slab_start(base_of[i], rows[i], jump_of[i], 4 * px + 2 * py + pc)
            return dsts[dst_of[i]].at[pl.ds(start, rows[i])]

        def copy(i, k, block, to, src=None):
            return pltpu.make_async_remote_copy(
                src_ref=slab(i, *block) if src is None else src, dst_ref=slab(i, *block),
                send_sem=send_sems.at[i, k], recv_sem=recv_sems.at[i, k],
                device_id=to, device_id_type=MESH)

        def mine():
            return [pltpu.make_async_copy(srcs[i], slab(i, *me), local_sems.at[i]) for i in range(n)]

        def first():
            out = []
            for i in range(n):
                out.append(copy(i, 0, me, sibling, src=srcs[i]))
                out += [copy(i, 1 + j, me, (*chip, c), src=srcs[i]) for j, chip in enumerate(chips)]
            return out

        def passed():
            return [(copy(i, 1 + j, (*chip, c), me), copy(i, 4 + j, (*chip, c), sibling))
                    for j, chip in enumerate(chips) for i in range(n)]

        def start():
            for cp in mine() + first():
                cp.start()

        def middle():
            for landed, onward in passed():
                landed.wait_recv()
                onward.start()

        def finish():
            for i in range(n):
                copy(i, 0, sibling, me).wait_recv()
                for j, chip in enumerate(chips):
                    copy(i, 4 + j, (*chip, 1 - c), me).wait_recv()
            for cp in first() + [onward for _, onward in passed()]:
                cp.wait_send()
            for cp in mine():
                cp.wait()

        return start, middle, finish

    sems = [pltpu.SemaphoreType.DMA((n, 7)), pltpu.SemaphoreType.DMA((n, 7)), pltpu.SemaphoreType.DMA((n,))]
    return _Plan(list(shards), [jax.ShapeDtypeStruct(s, BF16) for s in dst_shapes], sems, phases)


def _scatter_plan(grads, src_of, base_of, rows, cols, jump_of=None):
    n = len(rows)
    jump_of = jump_of or [0] * n

    def phases(srcs, recvs, sems):
        send_sems, recv_sems, local_sems = sems
        x, y, c = _place()
        me = 4 * x + 2 * y + c

        def slab(i, idx):
            start = _slab_start(base_of[i], rows[i], jump_of[i], idx)
            return srcs[src_of[i]].at[pl.ds(start, rows[i])]

        def copies():
            out = [pltpu.make_async_copy(slab(i, me), recvs[i].at[me], local_sems.at[i]) for i in range(n)]
            for k in range(1, N_DEV):
                px, py, pc = _flip(x, (k >> 2) & 1), _flip(y, (k >> 1) & 1), _flip(c, k & 1)
                out += [pltpu.make_async_remote_copy(
                    src_ref=slab(i, 4 * px + 2 * py + pc), dst_ref=recvs[i].at[me],
                    send_sem=send_sems.at[i, k - 1], recv_sem=recv_sems.at[i, k - 1],
                    device_id=(px, py, pc), device_id_type=MESH) for i in range(n)]
            return out

        def start():
            for cp in copies():
                cp.start()

        def finish():
            for cp in copies():
                cp.wait()

        return start, None, finish

    sems = [pltpu.SemaphoreType.DMA((n, 7)), pltpu.SemaphoreType.DMA((n, 7)), pltpu.SemaphoreType.DMA((n,))]
    out_shapes = [jax.ShapeDtypeStruct((N_DEV, rows[i], cols[i]), BF16) for i in range(n)]
    return _Plan(list(grads), out_shapes, sems, phases)


def _run_plan(name, plan):
    n_in, n_out = len(plan.operands), len(plan.out_shapes)

    def body(*refs):
        for phase in plan.phases(refs[:n_in], refs[n_in:n_in + n_out], refs[n_in + n_out:]):
            if phase is not None:
                phase()

    hbm = pl.BlockSpec(memory_space=pltpu.HBM)
    return pl.pallas_call(
        body, name=name, out_shape=plan.out_shapes,
        in_specs=[hbm] * n_in, out_specs=[hbm] * n_out, scratch_shapes=plan.sems,
    )(*plan.operands)


def _run_plan_on_sequencer(name, plan, collective_id):
    src_refs = [jax.new_ref(a, memory_space=pltpu.MemorySpace.HBM) for a in plan.operands]
    dst_refs = [jax.empty_ref(s, memory_space=pltpu.MemorySpace.HBM) for s in plan.out_shapes]

    @pl.kernel(mesh=plsc.ScalarSubcoreMesh(axis_name="sequencer", num_cores=1), name=name,
               scratch_types=tuple(plan.sems),
               compiler_params=pltpu.CompilerParams(collective_id=collective_id))
    def launch(*sems):
        x, y, c = _place()
        barrier = pltpu.get_barrier_semaphore()
        for k in range(1, N_DEV):
            peer = (_flip(x, (k >> 2) & 1), _flip(y, (k >> 1) & 1), _flip(c, k & 1))
            pl.semaphore_signal(barrier, inc=1, device_id=peer, device_id_type=MESH)
        pl.semaphore_wait(barrier, N_DEV - 1)
        for phase in plan.phases(src_refs, dst_refs, sems):
            if phase is not None:
                phase()

    launch()
    return [r[...] for r in dst_refs]


def _sum_contributions(name, recv):
    _, rows, cols = recv.shape
    tr = rows if rows <= 512 else 304 if rows % 304 == 0 else 256

    def body(r_ref, o_ref):
        acc = r_ref[0].astype(F32)
        for k in range(1, N_DEV):
            acc = acc + r_ref[k].astype(F32)
        o_ref[...] = acc

    return pl.pallas_call(
        body, name=name, grid=(rows // tr,),
        out_shape=jax.ShapeDtypeStruct((rows, cols), F32),
        in_specs=[pl.BlockSpec((N_DEV, tr, cols), lambda i: (0, i, 0))],
        out_specs=pl.BlockSpec((tr, cols), lambda i: (i, 0)),
        compiler_params=_params("parallel"),
    )(recv)


def _mm(name, a, b, mode, out_dtype, tm, tn, tk, *, carry=None, tiles_in=(), tiles_out=(), epilogue=None,
        n_outer=False, keep_b=False, col_chunks=None):
    if mode == "TN":
        kk, m = a.shape
    else:
        m, kk = a.shape
    n = b.shape[0] if mode == "NT" else b.shape[1]
    tm, tn, tk = min(tm, m), min(tn, n), min(tk, kk)
    assert m % tm == 0 and n % tn == 0 and kk % tk == 0, (name, m, n, kk, tm, tn, tk)
    ni, nj, nk = m // tm, n // tn, kk // tk
    steps = ni * nj * nk
    dims = {"NN": NN_DIMS, "NT": NT_DIMS, "TN": TN_DIMS}[mode]
    if epilogue is None:
        tiles_out = [(jax.ShapeDtypeStruct((m, n), out_dtype), (tm, tn), lambda i, j: (i, j))]
    n_tin, n_tout = len(tiles_in), len(tiles_out)
    n_in = len(carry.operands) if carry else 0
    n_out = len(carry.out_shapes) if carry else 0
    n_acc = 1 if nk > 1 else 0
    n_keep = 2 if keep_b else 0
    assert not carry or steps >= 3
    assert not keep_b or (nk == 1 and nj == 1)
    assert not col_chunks or (epilogue is not None and nk == 1 and mode != "TN")
    ij = (lambda p, q: (q, p)) if n_outer else (lambda p, q: (p, q))
    inner = ni if n_outer else nj

    def body(a_ref, b_ref, *rest):
        tin = rest[:n_tin]
        cin = rest[n_tin:n_tin + n_in]
        tout = rest[n_tin + n_in:n_tin + n_in + n_tout]
        cout = rest[n_tin + n_in + n_tout:n_tin + n_in + n_tout + n_out]
        scratch = rest[n_tin + n_in + n_tout + n_out:]
        k = pl.program_id(2)
        visit = pl.program_id(0) * inner + pl.program_id(1)
        step = visit * nk + k
        if keep_b:
            b_kept, b_sem = scratch[n_acc:n_acc + 2]

            @pl.when(step == 0)
            def _():
                cp = pltpu.make_async_copy(b_ref, b_kept, b_sem)
                cp.start()
                cp.wait()

            b_ref = b_kept
        if carry:
            start, middle, finish = carry.phases(cin, cout, scratch[n_acc + n_keep:])
            pl.when(step == 0)(start)

        def store(prod, c=0, cols=()):
            if epilogue is None:
                tout[0][...] = prod.astype(out_dtype)
            else:
                epilogue(prod, jnp.logical_and(visit == 0, c == 0), tin, tout, *cols)

        if col_chunks:
            for c, (c0, cw) in enumerate(col_chunks):
                b_part = b_ref[pl.ds(c0, cw), :] if mode == "NT" else b_ref[:, pl.ds(c0, cw)]
                store(lax.dot_general(a_ref[...], b_part, dims, preferred_element_type=F32), c, ((c0, cw),))
        else:
            part = lax.dot_general(a_ref[...], b_ref[...], dims, preferred_element_type=F32)
            if nk == 1:
                store(part)
            else:
                acc_ref = scratch[0]

                @pl.when(k == 0)
                def _():
                    acc_ref[...] = part

                @pl.when((k > 0) & (k < nk - 1))
                def _():
                    acc_ref[...] += part

                @pl.when(k == nk - 1)
                def _():
                    store(acc_ref[...] + part)

        if carry:
            if middle is not None:
                pl.when(step == (steps * 3) // 5)(middle)
            pl.when(step == steps - 1)(finish)

    def spec(shape, fn):
        return pl.BlockSpec(shape, lambda p, q, k: fn(*ij(p, q)))

    a_spec = (pl.BlockSpec((tk, tm), lambda p, q, k: (k, ij(p, q)[0])) if mode == "TN"
              else pl.BlockSpec((tm, tk), lambda p, q, k: (ij(p, q)[0], k)))
    if keep_b:
        b_spec = pl.BlockSpec(memory_space=pl.ANY)
    elif mode == "NT":
        b_spec = pl.BlockSpec((tn, tk), lambda p, q, k: (ij(p, q)[1], k))
    else:
        b_spec = pl.BlockSpec((tk, tn), lambda p, q, k: (k, ij(p, q)[1]))
    hbm = pl.BlockSpec(memory_space=pltpu.HBM)
    sequential = carry or epilogue or keep_b
    out = pl.pallas_call(
        body, name=name, grid=(nj, ni, nk) if n_outer else (ni, nj, nk),
        out_shape=[t[0] for t in tiles_out] + (carry.out_shapes if carry else []),
        in_specs=[a_spec, b_spec] + [spec(t[1], t[2]) for t in tiles_in] + [hbm] * n_in,
        out_specs=[spec(t[1], t[2]) for t in tiles_out] + [hbm] * n_out,
        scratch_shapes=([pltpu.VMEM((tm, tn), F32)] * n_acc
                        + ([pltpu.VMEM(b.shape, b.dtype), pltpu.SemaphoreType.DMA] if keep_b else [])
                        + (carry.sems if carry else [])),
        compiler_params=(_params("arbitrary", "arbitrary", "arbitrary") if sequential
                         else _params("parallel", "parallel", "arbitrary")),
    )(a, b, *[t[0] for t in tiles_in], *(carry.operands if carry else []))
    return out if (carry or epilogue) else out[0]


def _row(tm, w, off=0):
    return pl.BlockSpec((tm, w), lambda i: (i, off))


def _vec(w):
    return pl.BlockSpec((1, w), lambda i: (0, 0))


def _sigmoid(x):
    return 0.5 * jnp.tanh(0.5 * x) + 0.5


def _normmod(name, x, g, sc, sh, tm=512):
    s = x.shape[0]

    def body(x_ref, g_ref, sc_ref, sh_ref, h_ref):
        xv = x_ref[...]
        r = lax.rsqrt(jnp.mean(xv * xv, axis=-1, keepdims=True) + EPS)
        h_ref[...] = ((xv * r) * g_ref[...] * (1.0 + sc_ref[...]) + sh_ref[...]).astype(BF16)

    return pl.pallas_call(
        body, name=name, grid=(s // tm,),
        out_shape=jax.ShapeDtypeStruct((s, D), BF16),
        in_specs=[_row(tm, D), _vec(D), _vec(D), _vec(D)], out_specs=_row(tm, D),
        compiler_params=_params("parallel"),
    )(x, g, sc, sh)


def _normmod_bwd(name, dh, x, gin, g, sc, sh, tm=512):
    s = x.shape[0]

    def body(dh_ref, x_ref, gin_ref, g_ref, sc_ref, sh_ref, gout_ref, acc_ref):
        xv, dhv = x_ref[...], dh_ref[...]
        r = lax.rsqrt(jnp.mean(xv * xv, axis=-1, keepdims=True) + EPS)
        nv = xv * r
        gv, one_sc = g_ref[...], 1.0 + sc_ref[...]
        dn = dhv * gv * one_sc
        dx = r * (dn - nv * jnp.mean(dn * nv, axis=-1, keepdims=True))
        gout_ref[...] = gin_ref[...] + dx

        @pl.when(pl.program_id(0) == 0)
        def _():
            acc_ref[...] = jnp.zeros_like(acc_ref)

        dhn = dhv * nv
        acc_ref[0:1, :] += jnp.sum(dhv, axis=0, keepdims=True)
        acc_ref[1:2, :] += jnp.sum(dhn * gv, axis=0, keepdims=True)
        acc_ref[2:3, :] += jnp.sum(dhn * one_sc, axis=0, keepdims=True)

    return pl.pallas_call(
        body, name=name, grid=(s // tm,),
        out_shape=[jax.ShapeDtypeStruct((s, D), F32), jax.ShapeDtypeStruct((8, D), F32)],
        in_specs=[_row(tm, D), _row(tm, D), _row(tm, D), _vec(D), _vec(D), _vec(D)],
        out_specs=[_row(tm, D), pl.BlockSpec((8, D), lambda i: (0, 0))],
        compiler_params=_params("arbitrary"),
    )(dh, x, gin, g, sc, sh)


def _swiglu(name, ab, tm=512):
    s = ab.shape[0]

    def body(ab_ref, s_ref):
        a = ab_ref[:, :FF].astype(F32)
        b = ab_ref[:, FF:].astype(F32)
        s_ref[...] = (a * _sigmoid(a) * b).astype(BF16)

    return pl.pallas_call(
        body, name=name, grid=(s // tm,),
        out_shape=jax.ShapeDtypeStruct((s, FF), BF16),
        in_specs=[_row(tm, 2 * FF)], out_specs=_row(tm, FF),
        compiler_params=_params("parallel"),
    )(ab)


def _swiglu_bwd(name, ds, ab, tm=256):
    s = ab.shape[0]

    def body(ds_ref, ab_ref, dab_ref):
        a = ab_ref[:, :FF].astype(F32)
        b = ab_ref[:, FF:].astype(F32)
        dsv = ds_ref[...].astype(F32)
        sig = _sigmoid(a)
        dab_ref[:, :FF] = (dsv * b * (sig * (1.0 + a * (1.0 - sig)))).astype(BF16)
        dab_ref[:, FF:] = (dsv * (a * sig)).astype(BF16)

    return pl.pallas_call(
        body, name=name, grid=(s // tm,),
        out_shape=jax.ShapeDtypeStruct((s, 2 * FF), BF16),
        in_specs=[_row(tm, FF), _row(tm, 2 * FF)], out_specs=_row(tm, 2 * FF),
        compiler_params=_params("parallel"),
    )(ds, ab)


def _residual(name, x, f, gt, coef, tm=512):
    s = x.shape[0]

    def body(x_ref, f_ref, gt_ref, o_ref):
        o_ref[...] = x_ref[...] + (coef * gt_ref[...]) * f_ref[...]

    return pl.pallas_call(
        body, name=name, grid=(s // tm,),
        out_shape=jax.ShapeDtypeStruct((s, D), F32),
        in_specs=[_row(tm, D), _row(tm, D), _vec(D)], out_specs=_row(tm, D),
        compiler_params=_params("parallel"),
    )(x, f, gt)


def _gate_bwd(name, gin, f, gt, coef, tm=512):
    s = gin.shape[0]

    def body(g_ref, f_ref, gt_ref, df_ref, acc_ref):
        gv = g_ref[...]
        df_ref[...] = ((coef * gt_ref[...]) * gv).astype(BF16)

        @pl.when(pl.program_id(0) == 0)
        def _():
            acc_ref[...] = jnp.zeros_like(acc_ref)

        acc_ref[0:1, :] += coef * jnp.sum(gv * f_ref[...], axis=0, keepdims=True)

    return pl.pallas_call(
        body, name=name, grid=(s // tm,),
        out_shape=[jax.ShapeDtypeStruct((s, D), BF16), jax.ShapeDtypeStruct((8, D), F32)],
        in_specs=[_row(tm, D), _row(tm, D), _vec(D)],
        out_specs=[_row(tm, D), pl.BlockSpec((8, D), lambda i: (0, 0))],
        compiler_params=_params("arbitrary"),
    )(gin, f, gt)


def _loss_grad(x3, target, tm=512):
    s = x3.shape[0]

    def body(y_ref, t_ref, g_ref, l_ref):
        e = y_ref[...] - t_ref[...]
        g_ref[...] = e * (1.0 / D)

        @pl.when(pl.program_id(0) == 0)
        def _():
            l_ref[...] = jnp.zeros_like(l_ref)

        l_ref[...] += jnp.sum(jnp.mean(e * e, axis=-1, keepdims=True), axis=0, keepdims=True) * 0.5

    return pl.pallas_call(
        body, name="loss_grad", grid=(s // tm,),
        out_shape=[jax.ShapeDtypeStruct((s, D), F32), jax.ShapeDtypeStruct((8, 128), F32)],
        in_specs=[_row(tm, D), _row(tm, D)],
        out_specs=[_row(tm, D), pl.BlockSpec((8, 128), lambda i: (0, 0))],
        compiler_params=_params("arbitrary"),
    )(x3, target)


def _heads(x, fn):
    return jnp.concatenate([fn(x[:, h * HD:(h + 1) * HD], h) for h in range(COL // HD)], axis=1)


def _qknorm(proj, wqk, tm=1024):
    s = proj.shape[0]

    def body(p_ref, w_ref, o_ref):
        pv = p_ref[...].astype(F32)
        wv = w_ref[...]

        def one(qh, h):
            r = lax.rsqrt(jnp.mean(qh * qh, axis=-1, keepdims=True) + EPS)
            return (qh * r) * wv[:, h * HD:(h + 1) * HD]

        o_ref[...] = _heads(pv, one).astype(BF16)

    return pl.pallas_call(
        body, name="qknorm", grid=(s // tm, QKW // COL),
        out_shape=jax.ShapeDtypeStruct((s, QKW), BF16),
        in_specs=[pl.BlockSpec((tm, COL), lambda i, j: (i, j)), pl.BlockSpec((1, COL), lambda i, j: (0, j))],
        out_specs=pl.BlockSpec((tm, COL), lambda i, j: (i, j)),
        compiler_params=_params("parallel", "parallel"),
    )(proj, wqk)


def _qknorm_bwd(name, proj, dn, w, dproj, blk0, tm=1024):
    s = proj.shape[0]
    nblk = dn.shape[1] // COL

    def body(p_ref, d_ref, w_ref, _, o_ref, acc_ref):
        pv = p_ref[...].astype(F32)
        dv = d_ref[...]
        wv = w_ref[...]
        sums = []

        def one(qh, h):
            dn = dv[:, h * HD:(h + 1) * HD]
            r = lax.rsqrt(jnp.mean(qh * qh, axis=-1, keepdims=True) + EPS)
            nh = qh * r
            sums.append(jnp.sum(dn * nh, axis=0, keepdims=True))
            dnw = dn * wv[:, h * HD:(h + 1) * HD]
            return r * (dnw - nh * jnp.mean(dnw * nh, axis=-1, keepdims=True))

        o_ref[...] = _heads(pv, one).astype(BF16)

        @pl.when(pl.program_id(1) == 0)
        def _():
            acc_ref[...] = jnp.zeros_like(acc_ref)

        acc_ref[0:1, :] += jnp.concatenate(sums, axis=1)

    return pl.pallas_call(
        body, name=name, grid=(nblk, s // tm),
        out_shape=[jax.ShapeDtypeStruct((s, IN_W), BF16), jax.ShapeDtypeStruct((8, nblk * COL), F32)],
        in_specs=[pl.BlockSpec((tm, COL), lambda j, i: (i, blk0 + j)), pl.BlockSpec((tm, COL), lambda j, i: (i, j)),
                  pl.BlockSpec((1, COL), lambda j, i: (0, j)), pl.BlockSpec(memory_space=pl.ANY)],
        out_specs=[pl.BlockSpec((tm, COL), lambda j, i: (i, blk0 + j)),
                   pl.BlockSpec((8, COL), lambda j, i: (0, j))],
        input_output_aliases={3: 0},
        compiler_params=_params("arbitrary", "arbitrary"),
    )(proj, dn, w, dproj)


def _attn_shapes(s, g):
    d = DILATIONS[g]
    tb = min(s, max(2048, 256 * d))
    sb = min(256, tb // d)
    pb = BAND * d
    assert s % tb == 0 and tb % pb == 0 and (tb // d) % sb == 0 and sb % BAND == 0
    return d, tb, sb, pb


def _lanes(x, width):
    return jnp.concatenate([x] * (width // HD), axis=1)


def _every(start, size, d):
    return pl.ds(start, size, stride=d) if d > 1 else pl.ds(start, size)


def _attn_specs(g, tb, pb, s, ahead):
    ratio = tb // pb
    if ahead:
        nbr = lambda n: jnp.minimum((n + 1) * ratio, s // pb - 1)
    else:
        nbr = lambda n: jnp.maximum(n * ratio - 1, 0)
    cur = lambda base: pl.BlockSpec((tb, HD), lambda h, n: (n, base + g * N_HEADS + h))
    side = lambda base: pl.BlockSpec((pb, HD), lambda h, n: (nbr(n), base + g * N_HEADS + h))
    tok = pl.BlockSpec((tb, HD), lambda h, n: (n, h))
    tok_side = pl.BlockSpec((pb, HD), lambda h, n: (nbr(n), h))
    return cur, side, tok, tok_side


Q_COL, K_COL, V_COL = 0, 12, 24


def _attn_fwd(g, qkn, proj):
    s = qkn.shape[0]
    d, tb, sb, pb = _attn_shapes(s, g)
    ft = F32 if d > 1 else BF16
    nj = tb // d // sb
    scale = HD ** -0.5

    def body(q_ref, kc_ref, kp_ref, vc_ref, vp_ref, o_ref, lse_ref, qf, kf, vf):
        n = pl.program_id(1)
        qf[...] = q_ref[...].astype(ft)
        kf[0:pb] = kp_ref[...].astype(ft)
        kf[pb:] = kc_ref[...].astype(ft)
        vf[0:pb] = vp_ref[...].astype(ft)
        vf[pb:] = vc_ref[...].astype(ft)
        for r in range(d):
            for j in range(nj):
                at = j * sb * d + r
                q = qf[_every(at, sb, d), :].astype(BF16)
                k = kf[_every(at, sb + BAND, d), :].astype(BF16)
                v = vf[_every(at, sb + BAND, d), :].astype(BF16)
                sc = lax.dot_general(q, k, NT_DIMS, preferred_element_type=F32) * scale
                qi = lax.broadcasted_iota(jnp.int32, sc.shape, 0)
                kj = lax.broadcasted_iota(jnp.int32, sc.shape, 1)
                valid = (kj >= qi) & (kj <= qi + BAND)
                if j == 0:
                    valid = valid & ((kj >= BAND) | (n > 0))
                sc = jnp.where(valid, sc, -1e30)
                m = jnp.max(sc, axis=-1, keepdims=True)
                p = jnp.exp(sc - m)
                l = jnp.sum(p, axis=-1, keepdims=True)
                o = lax.dot_general(p.astype(BF16), v, NN_DIMS, preferred_element_type=F32)
                o_ref[_every(at, sb, d), :] = o / l
                lse_ref[_every(at, sb, d), :] = jnp.broadcast_to(m + jnp.log(l), (sb, HD))

    cur, side, tok, _ = _attn_specs(g, tb, pb, s, ahead=False)
    return pl.pallas_call(
        body, name=f"attn_fwd_g{g}", grid=(N_HEADS, s // tb),
        out_shape=[jax.ShapeDtypeStruct((s, COL), F32)] * 2,
        in_specs=[cur(Q_COL), cur(K_COL), side(K_COL), cur(V_COL), side(V_COL)],
        out_specs=[tok, tok],
        scratch_shapes=[pltpu.VMEM((tb, HD), ft), pltpu.VMEM((tb + pb, HD), ft),
                        pltpu.VMEM((tb + pb, HD), ft)],
        compiler_params=_params("parallel", "arbitrary"),
    )(qkn, qkn, qkn, proj, proj)


def _attn_combine(os_, lses, tm=512):
    s = os_[0].shape[0]

    def body(o0, o1, o2, l0, l1, l2, o_ref, lse_ref):
        a, b, c = l0[...], l1[...], l2[...]
        m = jnp.maximum(jnp.maximum(a, b), c)
        ea, eb, ec = jnp.exp(a - m), jnp.exp(b - m), jnp.exp(c - m)
        tot = ea + eb + ec
        o_ref[...] = ((ea * o0[...] + eb * o1[...] + ec * o2[...]) / tot).astype(BF16)
        lse_ref[...] = m + jnp.log(tot)

    return pl.pallas_call(
        body, name="attn_combine", grid=(s // tm,),
        out_shape=[jax.ShapeDtypeStruct((s, COL), BF16), jax.ShapeDtypeStruct((s, COL), F32)],
        in_specs=[_row(tm, COL)] * 6, out_specs=[_row(tm, COL)] * 2,
        compiler_params=_params("parallel"),
    )(*os_, *lses)


def _attn_delta(do, o, tm=512):
    s = do.shape[0]

    def body(do_ref, o_ref, del_ref):
        prod = do_ref[...] * o_ref[...].astype(F32)
        del_ref[...] = _heads(prod, lambda ph, h: jnp.broadcast_to(
            jnp.sum(ph, axis=-1, keepdims=True), ph.shape))

    return pl.pallas_call(
        body, name="attn_delta", grid=(s // tm,),
        out_shape=jax.ShapeDtypeStruct((s, COL), F32),
        in_specs=[_row(tm, COL)] * 2, out_specs=_row(tm, COL),
        compiler_params=_params("parallel"),
    )(do, o)


def _attn_bwd(g, qkn, proj, do, lse, delta, dqn, dkn, dproj):
    s = qkn.shape[0]
    d, tb, sb, pb = _attn_shapes(s, g)
    ft = F32 if d > 1 else BF16
    nj = tb // d // sb
    nt = s // tb
    scale = HD ** -0.5
    chained = dqn is not None

    def body(k_ref, v_ref, qc_ref, qn_ref, doc_ref, don_ref, lc_ref, ln_ref, dc_ref, dn_ref, *rest):
        dq_ref, dk_ref, dv_ref, kf, vf, qf, dvf, later = rest[-8:]
        n = pl.program_id(1)
        kf[...] = k_ref[...].astype(ft)
        vf[...] = v_ref[...].astype(ft)
        qf[0:tb] = qc_ref[...].astype(ft)
        qf[tb:] = qn_ref[...].astype(ft)

        @pl.when(n == 0)
        def _():
            later[...] = jnp.zeros_like(later)

        def window(c_ref, n_ref, r, j):
            at = j * sb * d + r
            if j < nj - 1:
                return c_ref[_every(at, sb + BAND, d), :]
            return jnp.concatenate([c_ref[_every(at, sb, d), :], n_ref[_every(r, BAND, d), :]], axis=0)

        for r in range(d):
            tail = later[r]
            for j in range(nj):
                at = j * sb * d + r
                rows = _every(at, sb, d)
                k = kf[rows, :].astype(BF16)
                v = vf[rows, :].astype(BF16)
                q = qf[_every(at, sb + BAND, d), :].astype(BF16)
                dov = window(doc_ref, don_ref, r, j).astype(BF16)
                sc = lax.dot_general(q, k, NT_DIMS, preferred_element_type=F32) * scale
                qi = lax.broadcasted_iota(jnp.int32, sc.shape, 0)
                kj = lax.broadcasted_iota(jnp.int32, sc.shape, 1)
                valid = (qi >= kj) & (qi <= kj + BAND)
                if j == nj - 1:
                    valid = valid & ((qi < sb) | (n < nt - 1))
                p = jnp.exp(jnp.where(valid, sc - _lanes(window(lc_ref, ln_ref, r, j), sb), -1e30))
                dp = lax.dot_general(dov, v, NT_DIMS, preferred_element_type=F32)
                ds = (p * (dp - _lanes(window(dc_ref, dn_ref, r, j), sb)) * scale).astype(BF16)
                dvf[rows, :] = lax.dot_general(p.astype(BF16), dov, TN_DIMS, preferred_element_type=F32)
                dk_ref[rows, :] = lax.dot_general(ds, q, TN_DIMS, preferred_element_type=F32)
                dqw = lax.dot_general(ds, k, NN_DIMS, preferred_element_type=F32)
                first = dqw[:BAND] + tail
                dq_ref[rows, :] = first if sb == BAND else jnp.concatenate([first, dqw[BAND:sb]], axis=0)
                tail = dqw[sb:]
            later[r] = tail
        dv_ref[...] = dvf[...].astype(BF16)

    cur, side, tok, tok_side = _attn_specs(g, tb, pb, s, ahead=True)
    anyspec = pl.BlockSpec(memory_space=pl.ANY)
    n_heads_cols = 3 * N_HEADS * HD
    return pl.pallas_call(
        body, name=f"attn_bwd_g{g}", grid=(N_HEADS, nt),
        out_shape=[jax.ShapeDtypeStruct((s, n_heads_cols), F32), jax.ShapeDtypeStruct((s, n_heads_cols), F32),
                   jax.ShapeDtypeStruct((s, IN_W), BF16)],
        in_specs=[cur(K_COL), cur(V_COL), cur(Q_COL), side(Q_COL), tok, tok_side, tok, tok_side,
                  tok, tok_side] + ([anyspec, anyspec] if chained else []) + [anyspec],
        out_specs=[cur(0), cur(0), cur(V_COL)],
        input_output_aliases={10: 0, 11: 1, 12: 2} if chained else {10: 2},
        scratch_shapes=[pltpu.VMEM((tb, HD), ft), pltpu.VMEM((tb, HD), ft),
                        pltpu.VMEM((tb + pb, HD), ft), pltpu.VMEM((tb, HD), F32),
                        pltpu.VMEM((d, BAND, HD), F32)],
        compiler_params=_params("arbitrary", "arbitrary"),
    )(qkn, proj, qkn, qkn, do, do, lse, lse, delta, delta, *([dqn, dkn] if chained else []), dproj)


def _shift_down(x, before, k):
    rolled = pltpu.roll(x, k, 0)
    head = jnp.where(lax.broadcasted_iota(jnp.int32, before.shape, 0) < k, pltpu.roll(before, k, 0), rolled[:8])
    return jnp.concatenate([head, rolled[8:]], axis=0)


def _shift_up(x, after, k):
    rows = x.shape[0]
    rolled = pltpu.roll(x, rows - k, 0)
    tail = jnp.where(lax.broadcasted_iota(jnp.int32, after.shape, 0) >= 8 - k,
                     pltpu.roll(after, 8 - k, 0), rolled[rows - 8:])
    return jnp.concatenate([rolled[:rows - 8], tail], axis=0)


def _conv_fwd(proj, cw, tm=1024):
    s = proj.shape[0]
    r16 = tm // 16

    def body(u_ref, b_ref, c_ref, up_ref, cp_ref, w_ref, z_ref):
        i = pl.program_id(1)
        xc = c_ref[...].astype(F32) * u_ref[...].astype(F32)
        xp = jnp.where(i > 0, cp_ref[8:16, :].astype(F32) * up_ref[8:16, :].astype(F32), 0.0)
        w = w_ref[...]
        conv = _shift_down(xc, xp, 2) * w[0:1] + _shift_down(xc, xp, 1) * w[1:2] + xc * w[2:3]
        z_ref[...] = (b_ref[...].astype(F32) * conv).astype(BF16)

    tile = lambda blk: pl.BlockSpec((tm, COL), lambda j, i: (i, blk + j))
    before = lambda blk: pl.BlockSpec((16, COL), lambda j, i: (jnp.maximum(i * r16 - 1, 0), blk + j))
    return pl.pallas_call(
        body, name="conv_fwd", grid=(D // COL, s // tm),
        out_shape=jax.ShapeDtypeStruct((s, D), BF16),
        in_specs=[tile(U_BLK), tile(B_BLK), tile(C_BLK), before(U_BLK), before(C_BLK),
                  pl.BlockSpec((3, COL), lambda j, i: (0, j))],
        out_specs=pl.BlockSpec((tm, COL), lambda j, i: (i, j)),
        compiler_params=_params("parallel", "parallel"),
    )(proj, proj, proj, proj, proj, cw)


def _conv_bwd(dz, proj, cw, dproj, tm=1024):
    s = proj.shape[0]
    r8, r16 = tm // 8, tm // 16
    nrow = s // tm

    def body(dz_ref, u_ref, b_ref, c_ref, up_ref, cp_ref, dzn_ref, bn_ref, w_ref, _, o_ref, acc_ref):
        piece, i = pl.program_id(1), pl.program_id(2)
        u, c = u_ref[...].astype(F32), c_ref[...].astype(F32)
        bv = b_ref[...].astype(F32)
        dzv = dz_ref[...]
        w = w_ref[...]

        @pl.when((piece == 0) & (i == 0))
        def _():
            acc_ref[...] = jnp.zeros_like(acc_ref)

        @pl.when(piece == 1)
        def _():
            xc = c * u
            xp = jnp.where(i > 0, cp_ref[8:16, :].astype(F32) * up_ref[8:16, :].astype(F32), 0.0)
            x2, x1 = _shift_down(xc, xp, 2), _shift_down(xc, xp, 1)
            o_ref[...] = (dzv * (x2 * w[0:1] + x1 * w[1:2] + xc * w[2:3])).astype(BF16)
            dconv = dzv * bv
            acc_ref[0:1, :] += jnp.sum(dconv * x2, axis=0, keepdims=True)
            acc_ref[1:2, :] += jnp.sum(dconv * x1, axis=0, keepdims=True)
            acc_ref[2:3, :] += jnp.sum(dconv * xc, axis=0, keepdims=True)

        @pl.when(piece != 1)
        def _():
            dconv = dzv * bv
            dn = jnp.where(i < nrow - 1, dzn_ref[...] * bn_ref[0:8, :].astype(F32), 0.0)
            dxc = dconv * w[2:3] + _shift_up(dconv, dn, 1) * w[1:2] + _shift_up(dconv, dn, 2) * w[0:1]
            o_ref[...] = (dxc * jnp.where(piece == 0, c, u)).astype(BF16)

    tile = lambda blk: pl.BlockSpec((tm, COL), lambda j, p, i: (i, blk + j))
    before = lambda blk: pl.BlockSpec((16, COL), lambda j, p, i: (jnp.maximum(i * r16 - 1, 0), blk + j))
    after = lambda rows, blk: pl.BlockSpec(
        (rows, COL), lambda j, p, i: (jnp.minimum((i + 1) * (tm // rows), s // rows - 1), blk + j))
    return pl.pallas_call(
        body, name="conv_bwd", grid=(D // COL, 3, nrow),
        out_shape=[jax.ShapeDtypeStruct((s, IN_W), BF16), jax.ShapeDtypeStruct((8, D), F32)],
        in_specs=[tile(0), tile(U_BLK), tile(B_BLK), tile(C_BLK), before(U_BLK), before(C_BLK),
                  after(8, 0), after(16, B_BLK), pl.BlockSpec((3, COL), lambda j, p, i: (0, j)),
                  pl.BlockSpec(memory_space=pl.ANY)],
        out_specs=[pl.BlockSpec((tm, COL), lambda j, p, i: (i, U_BLK + 2 * p + j)),
                   pl.BlockSpec((8, COL), lambda j, p, i: (0, j))],
        input_output_aliases={9: 0},
        compiler_params=_params("arbitrary", "arbitrary", "arbitrary"),
    )(dz, proj, proj, proj, proj, proj, dz, proj, cw, dproj)


def _merge_fwd(ya, yc, proj, tm=512):
    s = proj.shape[0]

    def body(ya_ref, yc_ref, ga_ref, gc_ref, o_ref):
        o_ref[...] = (_sigmoid(ga_ref[...].astype(F32)) * ya_ref[...].astype(F32)
                      + _sigmoid(gc_ref[...].astype(F32)) * yc_ref[...].astype(F32)).astype(BF16)

    tile = lambda blk: pl.BlockSpec((tm, COL), lambda j, i: (i, blk + j))
    return pl.pallas_call(
        body, name="merge_fwd", grid=(D // COL, s // tm),
        out_shape=jax.ShapeDtypeStruct((s, D), BF16),
        in_specs=[tile(0), tile(0), tile(GA_BLK), tile(GC_BLK)], out_specs=tile(0),
        compiler_params=_params("parallel", "parallel"),
    )(ya, yc, proj, proj)


def _merge_bwd_branches(dm, proj, tm=512):
    s = proj.shape[0]

    def body(dm_ref, ga_ref, gc_ref, dya_ref, dyc_ref):
        dmv = dm_ref[...]
        dya_ref[...] = (dmv * _sigmoid(ga_ref[...].astype(F32))).astype(BF16)
        dyc_ref[...] = (dmv * _sigmoid(gc_ref[...].astype(F32))).astype(BF16)

    tile = lambda blk: pl.BlockSpec((tm, COL), lambda j, i: (i, blk + j))
    return pl.pallas_call(
        body, name="merge_bwd_branches", grid=(D // COL, s // tm),
        out_shape=[jax.ShapeDtypeStruct((s, D), BF16)] * 2,
        in_specs=[tile(0), tile(GA_BLK), tile(GC_BLK)], out_specs=[tile(0)] * 2,
        compiler_params=_params("parallel", "parallel"),
    )(dm, proj, proj)


def _merge_bwd_gates(dm, ya, yc, proj, tm=1024):
    s = proj.shape[0]
    half = D // COL

    def body(dm_ref, ya_ref, yc_ref, g_ref, o_ref):
        y = jnp.where(pl.program_id(0) < half, ya_ref[...].astype(F32), yc_ref[...].astype(F32))
        sig = _sigmoid(g_ref[...].astype(F32))
        o_ref[...] = (dm_ref[...] * y * sig * (1.0 - sig)).astype(BF16)

    chan = pl.BlockSpec((tm, COL), lambda jj, i: (i, jj % half))
    gate = pl.BlockSpec((tm, COL), lambda jj, i: (i, GA_BLK + jj))
    return pl.pallas_call(
        body, name="merge_bwd_gates", grid=(2 * half, s // tm),
        out_shape=jax.ShapeDtypeStruct((s, IN_W), BF16),
        in_specs=[chan, chan, chan, gate], out_specs=gate,
        compiler_params=_params("parallel", "parallel"),
    )(dm, ya, yc, proj)


def _mod_part(c_all, w_ada, b_part):
    def body(c_ref, w_ref, b_ref, o_ref):
        cv = c_ref[...]
        act = cv * _sigmoid(cv)
        o_ref[...] = jnp.dot(act, w_ref[...], preferred_element_type=F32,
                             precision=lax.Precision.HIGHEST) + b_ref[...]

    return pl.pallas_call(
        body, name="mod_part", out_shape=jax.ShapeDtypeStruct((N_DEV, w_ada.shape[1]), F32),
    )(c_all, w_ada, b_part)


def _w_ada_grad(c_all_t, dmod_part):
    def body(c_ref, d_ref, o_ref):
        cv = c_ref[...]
        act = cv * _sigmoid(cv)
        dv = d_ref[...]
        acc = act[:, 0:1] * dv[0:1, :]
        for b in range(1, N_DEV):
            acc = acc + act[:, b:b + 1] * dv[b:b + 1, :]
        o_ref[...] = acc

    return pl.pallas_call(
        body, name="w_ada_grad", out_shape=jax.ShapeDtypeStruct((D, dmod_part.shape[1]), F32),
    )(c_all_t, dmod_part)


def _sum_rows(name, v):
    def body(v_ref, o_ref):
        acc = v_ref[0]
        for k in range(1, N_DEV):
            acc = acc + v_ref[k]
        o_ref[...] = acc

    return pl.pallas_call(body, name=name, out_shape=jax.ShapeDtypeStruct(v.shape[1:], F32))(v)


def _adamw(name, w, g, m, v):
    rows, cols = w.shape
    limit = max(16, (1 << 20) // (4 * cols))
    tr = rows if rows <= limit else next((t for t in range(limit - limit % 16, 15, -16) if rows % t == 0), rows)
    c1 = 1.0 - ADAM_B1 ** ADAM_STEP
    c2 = 1.0 - ADAM_B2 ** ADAM_STEP
    parts = g.ndim == 3

    def body(w_ref, g_ref, m_ref, v_ref, go_ref, d_ref, nm_ref, nv_ref):
        if parts:
            gv = g_ref[0].astype(F32)
            for k in range(1, N_DEV):
                gv = gv + g_ref[k].astype(F32)
        else:
            gv = g_ref[...]
        go_ref[...] = gv
        nm = ADAM_B1 * m_ref[...] + (1.0 - ADAM_B1) * gv
        nv = ADAM_B2 * v_ref[...] + (1.0 - ADAM_B2) * (gv * gv)
        nm_ref[...] = nm
        nv_ref[...] = nv
        d_ref[...] = -ADAM_LR * ((nm / c1) / (jnp.sqrt(nv / c2) + ADAM_EPS) + ADAM_WD * w_ref[...])

    spec = pl.BlockSpec((tr, cols), lambda i: (i, 0))
    g_spec = pl.BlockSpec((N_DEV, tr, cols), lambda i: (0, i, 0)) if parts else spec
    return pl.pallas_call(
        body, name=name, grid=(rows // tr,),
        out_shape=[jax.ShapeDtypeStruct((rows, cols), F32)] * 4,
        in_specs=[spec, g_spec, spec, spec], out_specs=[spec] * 4,
        compiler_params=_params("parallel"),
    )(w, g, m, v)


def _adamw_small(ws, gs, ms, vs):
    n = len(ws)
    c1 = 1.0 - ADAM_B1 ** ADAM_STEP
    c2 = 1.0 - ADAM_B2 ** ADAM_STEP

    def body(*refs):
        for i in range(n):
            w_ref, g_ref, m_ref, v_ref = refs[i], refs[n + i], refs[2 * n + i], refs[3 * n + i]
            d_ref, nm_ref, nv_ref = refs[4 * n + 3 * i:4 * n + 3 * i + 3]
            gv = g_ref[...]
            nm = ADAM_B1 * m_ref[...] + (1.0 - ADAM_B1) * gv
            nv = ADAM_B2 * v_ref[...] + (1.0 - ADAM_B2) * (gv * gv)
            nm_ref[...] = nm
            nv_ref[...] = nv
            d_ref[...] = -ADAM_LR * ((nm / c1) / (jnp.sqrt(nv / c2) + ADAM_EPS) + ADAM_WD * w_ref[...])

    outs = pl.pallas_call(
        body, name="adamw_small",
        out_shape=[jax.ShapeDtypeStruct(w.shape, F32) for w in ws for _ in range(3)],
    )(*ws, *gs, *ms, *vs)
    return [tuple(outs[3 * i:3 * i + 3]) for i in range(n)]


HALF = FF // 2


def _sds(shape, dtype):
    return jax.ShapeDtypeStruct(shape, dtype)


def _row_tile(w):
    return lambda tm: ((tm, w), lambda i, j: (i, 0))


def _one(w):
    return lambda rows: ((rows, w), lambda i, j: (0, 0))


def _gate_up_swiglu(name, h, wgu, carry=None, tm=512):
    s = h.shape[0]
    tm = min(tm, s)

    def epilogue(prod, first, tin, tout):
        pq_ref, s_ref = tout
        a, b = prod[:, :HALF], prod[:, HALF:]
        sig = _sigmoid(a)
        act = a * sig
        pq_ref[:, :HALF] = (b * (sig * (1.0 + a * (1.0 - sig)))).astype(BF16)
        pq_ref[:, HALF:] = act.astype(BF16)
        s_ref[...] = (act * b).astype(BF16)

    return _mm(name, h, wgu, "NT", None, tm, FF, D, carry=carry, n_outer=True, epilogue=epilogue,
               tiles_out=[(_sds((s, 2 * FF), BF16), (tm, FF), lambda i, j: (i, j)),
                          (_sds((s, FF), BF16), (tm, HALF), lambda i, j: (i, j))])


def _d_hidden_swiglu(name, df, wd, ab, tm=512):
    s = df.shape[0]
    tm = min(tm, s)

    def epilogue(prod, first, tin, tout, cols):
        da_cols = slice(cols[0], cols[0] + cols[1])
        db_cols = slice(HALF + cols[0], HALF + cols[0] + cols[1])
        tout[0][:, da_cols] = (prod * tin[0][:, da_cols].astype(F32)).astype(BF16)
        tout[0][:, db_cols] = (prod * tin[0][:, db_cols].astype(F32)).astype(BF16)

    chunks = [(c0, min(384, HALF - c0)) for c0 in range(0, HALF, 384)]
    return _mm(name, df, wd, "NT", None, tm, HALF, D, n_outer=True, epilogue=epilogue, col_chunks=chunks,
               tiles_in=[(ab, (tm, FF), lambda i, j: (i, j))],
               tiles_out=[(_sds((s, 2 * FF), BF16), (tm, FF), lambda i, j: (i, j))])[0]


def _out_residual(name, a, w, x, gt, coef, nxt, tm=512, tk=FF):
    s = a.shape[0]
    tm = min(tm, s)

    def epilogue(prod, first, tin, tout):
        x_ref, gt_ref, g_ref, sc_ref, sh_ref = tin
        f_ref, xn_ref, h_ref = tout
        f_ref[...] = prod
        xn = x_ref[...] + (coef * gt_ref[...]) * prod
        xn_ref[...] = xn
        r = lax.rsqrt(jnp.mean(xn * xn, axis=-1, keepdims=True) + EPS)
        h_ref[...] = ((xn * r) * g_ref[...] * (1.0 + sc_ref[...]) + sh_ref[...]).astype(BF16)

    row, vec = _row_tile(D)(tm), _one(D)(1)
    return _mm(name, a, w, "NN", None, tm, D, tk, epilogue=epilogue,
               tiles_in=[(x, *row), (gt, *vec)] + [(v, *vec) for v in nxt],
               tiles_out=[(_sds((s, D), F32), *row), (_sds((s, D), F32), *row), (_sds((s, D), BF16), *row)])


def _out_loss(name, a, w, x, gt, coef, target, tm=512):
    s = a.shape[0]
    tm = min(tm, s)

    def epilogue(prod, first, tin, tout):
        x_ref, gt_ref, t_ref = tin
        f_ref, g_ref, df_ref, acc_ref = tout
        f_ref[...] = prod
        cg = coef * gt_ref[...]
        e = x_ref[...] + cg * prod - t_ref[...]
        gv = e * (1.0 / D)
        g_ref[...] = gv
        df_ref[...] = (cg * gv).astype(BF16)

        @pl.when(first)
        def _():
            acc_ref[...] = jnp.zeros_like(acc_ref)

        acc_ref[0:1, :] += coef * jnp.sum(gv * prod, axis=0, keepdims=True)
        acc_ref[1:2, :] += (0.5 / D) * jnp.sum(e * e, axis=0, keepdims=True)

    row, vec = _row_tile(D)(tm), _one(D)(1)
    return _mm(name, a, w, "NN", None, tm, D, FF, epilogue=epilogue,
               tiles_in=[(x, *row), (gt, *vec), (target, *row)],
               tiles_out=[(_sds((s, D), F32), *row), (_sds((s, D), F32), *row), (_sds((s, D), BF16), *row),
                          (_sds((8, D), F32), *_one(D)(8))])


def _d_h_norm_bwd(name, da, w, x, gin, g, sc, sh, before=None, carry=None, tm=256):
    s = da.shape[0]
    tm = min(tm, s)
    coef = before[2] if before else None

    def epilogue(prod, first, tin, tout):
        x_ref, gin_ref, g_ref, sc_ref, sh_ref = tin[:5]
        gout_ref, acc_ref = tout[:2]
        xv = x_ref[...]
        r = lax.rsqrt(jnp.mean(xv * xv, axis=-1, keepdims=True) + EPS)
        nv = xv * r
        gv, one_sc = g_ref[...], 1.0 + sc_ref[...]
        dn = prod * gv * one_sc
        gout = gin_ref[...] + r * (dn - nv * jnp.mean(dn * nv, axis=-1, keepdims=True))
        gout_ref[...] = gout

        @pl.when(first)
        def _():
            acc_ref[...] = jnp.zeros_like(acc_ref)

        dhn = prod * nv
        acc_ref[0:1, :] += jnp.sum(prod, axis=0, keepdims=True)
        acc_ref[1:2, :] += jnp.sum(dhn * gv, axis=0, keepdims=True)
        acc_ref[2:3, :] += jnp.sum(dhn * one_sc, axis=0, keepdims=True)
        if before:
            f_ref, gt_ref = tin[5:]
            tout[2][...] = ((coef * gt_ref[...]) * gout).astype(BF16)
            acc_ref[3:4, :] += coef * jnp.sum(gout * f_ref[...], axis=0, keepdims=True)

    row, vec = _row_tile(D)(tm), _one(D)(1)
    tiles_in = [(x, *row), (gin, *row), (g, *vec), (sc, *vec), (sh, *vec)]
    tiles_out = [(_sds((s, D), F32), *row), (_sds((8, D), F32), *_one(D)(8))]
    if before:
        tiles_in += [(before[0], *row), (before[1], *vec)]
        tiles_out.append((_sds((s, D), BF16), *row))
    return _mm(name, da, w, "NN", None, tm, D, da.shape[1], epilogue=epilogue, carry=carry, keep_b=True,
               tiles_in=tiles_in, tiles_out=tiles_out)


def _gate_tiles(proj, tm):
    return [(proj, (tm, COL), (lambda i, j, blk=blk: (i, blk))) for blk in (GA_BLK, GA_BLK + 1, GC_BLK, GC_BLK + 1)]


def _conv_branch_merge(z, wc, ya, proj, tm=512):
    s = z.shape[0]
    tm = min(tm, s)

    def epilogue(prod, first, tin, tout):
        ya_ref, ga0, ga1, gc0, gc1 = tin
        tout[0][...] = prod.astype(BF16)
        for half, (ga, gc) in enumerate(((ga0, gc0), (ga1, gc1))):
            cols = slice(half * COL, (half + 1) * COL)
            tout[1][:, cols] = (_sigmoid(ga[...].astype(F32)) * ya_ref[:, cols].astype(F32)
                                + _sigmoid(gc[...].astype(F32)) * prod[:, cols]).astype(BF16)

    row = _row_tile(D)(tm)
    return _mm("mix_conv_branch", z, wc, "NN", None, tm, D, D, epilogue=epilogue,
               tiles_in=[(ya, *row)] + _gate_tiles(proj, tm),
               tiles_out=[(_sds((s, D), BF16), *row), (_sds((s, D), BF16), *row)])


def _d_merged_branches(dmix, wo, proj, tm=512):
    s = dmix.shape[0]
    tm = min(tm, s)

    def epilogue(prod, first, tin, tout):
        ga0, ga1, gc0, gc1 = tin
        tout[0][...] = prod
        for half, (ga, gc) in enumerate(((ga0, gc0), (ga1, gc1))):
            cols = slice(half * COL, (half + 1) * COL)
            tout[1][:, cols] = (prod[:, cols] * _sigmoid(ga[...].astype(F32))).astype(BF16)
            tout[2][:, cols] = (prod[:, cols] * _sigmoid(gc[...].astype(F32))).astype(BF16)

    row = _row_tile(D)(tm)
    return _mm("mix_d_merged", dmix, wo, "NT", None, tm, D, D, epilogue=epilogue,
               tiles_in=_gate_tiles(proj, tm),
               tiles_out=[(_sds((s, D), F32), *row), (_sds((s, D), BF16), *row), (_sds((s, D), BF16), *row)])


def _d_o_delta(dya, wa_t, o, tm=1024):
    s = dya.shape[0]
    tm = min(tm, s)

    def epilogue(prod, first, tin, tout):
        tout[0][...] = prod
        tout[1][...] = _heads(prod * tin[0][...].astype(F32), lambda ph, h: jnp.broadcast_to(
            jnp.sum(ph, axis=-1, keepdims=True), ph.shape))

    row = _row_tile(COL)(tm)
    return _mm("mix_d_o", dya, wa_t, "NN", None, tm, COL, D, epilogue=epilogue,
               tiles_in=[(o, *row)], tiles_out=[(_sds((s, COL), F32), *row), (_sds((s, COL), F32), *row)])


def _ffn_bwd(tag, df, x, gin, h, ab, sw, g, sc, sh, wgu, wd, before=None, tk_dw=2048):
    dab = _d_hidden_swiglu(f"{tag}_d_hidden", df, wd, ab)
    dwgu = _mm(f"{tag}_dw_gate_up", dab, h, "TN", BF16, HALF, D, tk_dw)
    dwd = _mm(f"{tag}_dw_down", sw, df, "TN", BF16, HALF, D, tk_dw)
    res = _d_h_norm_bwd(f"{tag}_d_h", dab, wgu, x, gin, g, sc, sh, before=before)
    return res, dwgu, dwd


def kernel(x, c, w_ada, b_ada, norm_ffn1, ffn1_w_gate, ffn1_w_up, ffn1_w_down, norm_mix, w_in, q_norm, k_norm, conv_w, w_attn_branch, w_conv_branch, w_out, norm_ffn2, ffn2_w_gate, ffn2_w_up, ffn2_w_down, loss_target, m_w_ada, m_b_ada, m_norm_ffn1, m_ffn1_w_gate, m_ffn1_w_up, m_ffn1_w_down, m_norm_mix, m_w_in, m_q_norm, m_k_norm, m_conv_w, m_w_attn_branch, m_w_conv_branch, m_w_out, m_norm_ffn2, m_ffn2_w_gate, m_ffn2_w_up, m_ffn2_w_down, v_w_ada, v_b_ada, v_norm_ffn1, v_ffn1_w_gate, v_ffn1_w_up, v_ffn1_w_down, v_norm_mix, v_w_in, v_q_norm, v_k_norm, v_conv_w, v_w_attn_branch, v_w_conv_branch, v_w_out, v_norm_ffn2, v_ffn2_w_gate, v_ffn2_w_up, v_ffn2_w_down):
    me = 4 * lax.axis_index("x") + 2 * lax.axis_index("y") + lax.axis_index("c")
    x0, target = x[0], loss_target[0]
    s = x0.shape[0]
    ada_cols = w_ada.shape[2]
    cw_cols = conv_w.shape[2]

    gathered = _small_allgather(
        "gather_c_conv", jnp.concatenate([c, conv_w[0].reshape(1, 3 * cw_cols)], axis=1))[:, 0]
    c_all = gathered[:, :D]
    cw = gathered[:, D:].reshape(N_DEV, 3, cw_cols).transpose(1, 0, 2).reshape(3, D)
    b_part = lax.dynamic_slice(b_ada, (0, me * ada_cols), (1, ada_cols))
    mod_part = _mod_part(c_all, w_ada[0], b_part)
    mod_all = _small_allgather("gather_mod", mod_part.reshape(1, N_DEV * ada_cols))
    mod = lax.dynamic_slice(mod_all.reshape(N_DEV, N_DEV, ada_cols), (0, me, 0), (N_DEV, 1, ada_cols))
    mod = mod.reshape(N_MOD, 1, D)
    sh1, sc1, gt1, sh2, sc2, gt2, sh3, sc3, gt3 = [mod[i] for i in range(N_MOD)]

    tb = lambda w: w[0].T.astype(BF16)
    nb = lambda w: w[0].astype(BF16)
    ffn1_shards = [tb(ffn1_w_gate), tb(ffn1_w_up), nb(ffn1_w_down)]
    ffn2_shards = [tb(ffn2_w_gate), tb(ffn2_w_up), nb(ffn2_w_down)]
    mix_shards = [tb(w_in), tb(w_attn_branch), nb(w_conv_branch), nb(w_out)]
    ffn_dst, ffn_base, ffn_jump, ffn_shapes = [0, 0, 1], [0, HALF, 0], [HALF, HALF, 0], [(2 * FF, D), (FF, D)]
    mix_dst, mix_base, mix_shapes = [0, 1, 2, 3], [0, 0, 0, 0], [(IN_W, D), (D, COL), (D, D), (D, D)]
    (wgu1,) = _run_plan_on_sequencer(
        "gather_ffn1_gate_up", _gather_plan(ffn1_shards[:2], ffn_dst[:2], ffn_base[:2], ffn_shapes[:1], ffn_jump[:2]), 1)
    (wd1,) = _run_plan_on_sequencer(
        "gather_ffn1_down", _gather_plan(ffn1_shards[2:], [0], [0], ffn_shapes[1:]), 8)
    win_t, wa_t, wc, wo = _run_plan_on_sequencer(
        "gather_mix_weights", _gather_plan(mix_shards, mix_dst, mix_base, mix_shapes), 2)
    wgu2, wd2 = _run_plan_on_sequencer(
        "gather_ffn2_weights", _gather_plan(ffn2_shards, ffn_dst, ffn_base, ffn_shapes, ffn_jump), 3)

    h1 = _normmod("ffn1_normmod", x0, norm_ffn1, sc1, sh1)
    ab1, s1 = _gate_up_swiglu("ffn1_gate_up", h1, wgu1)
    f1, x1, h2 = _out_residual("ffn1_down", s1, wd1, x0, gt1, 0.5, (norm_mix, sc2, sh2))
    proj = _mm("mix_in_proj", h2, win_t, "NT", BF16, 1024, IN_W // 4, D, n_outer=True)
    wqk = jnp.concatenate([jnp.tile(q_norm, (1, 12)), jnp.tile(k_norm, (1, 12))], axis=1)
    qkn = _qknorm(proj, wqk)
    group_out = [_attn_fwd(g, qkn, proj) for g in range(3)]
    o, lse = _attn_combine([go[0] for go in group_out], [go[1] for go in group_out])
    ya = _mm("mix_attn_branch", o, wa_t, "NT", BF16, 1024, 1024, COL)
    z = _conv_fwd(proj, cw)
    yc, merged = _conv_branch_merge(z, wc, ya, proj)
    mix, x2, h3 = _out_residual("mix_out_proj", merged, wo, x1, gt2, 1.0, (norm_ffn2, sc3, sh3), tk=D)
    ab3, s3 = _gate_up_swiglu("ffn2_gate_up", h3, wgu2)
    f3, g3, df3, acc_out = _out_loss("ffn2_down", s3, wd2, x2, gt3, 0.5, target)
    loss_part = jnp.sum(acc_out[1])

    ffn_rows = [sh_.shape[0] for sh_ in ffn1_shards]
    mix_rows = [sh_.shape[0] for sh_ in mix_shards]
    (g2, acc3, dmix), dwgu2, dwd2 = _ffn_bwd(
        "ffn2", df3, x2, g3, h3, ab3, s3, norm_ffn2, sc3, sh3, wgu2, wd2, before=(mix, gt2, 1.0))
    dmerged, dya, dyc = _d_merged_branches(dmix, wo, proj)
    dwo = _mm("mix_dw_out", merged, dmix, "TN", BF16, 1024, 1024, 2048)
    dproj = _merge_bwd_gates(dmerged, ya, yc, proj)
    dwc = _mm("mix_dw_conv_branch", z, dyc, "TN", BF16, 1024, 1024, 2048)
    dz = _mm("mix_d_z", dyc, wc, "NT", F32, 1024, 1024, D)
    dproj, cw_acc = _conv_bwd(dz, proj, cw, dproj)
    dwa_t = _mm("mix_dw_attn_branch", dya, o, "TN", BF16, 1024, COL, 2048)
    do, delta = _d_o_delta(dya, wa_t, o)
    dqn = dkn = None
    for g in range(3):
        dqn, dkn, dproj = _attn_bwd(g, qkn, proj, do, lse, delta, dqn, dkn, dproj)
    dproj, wq_acc = _qknorm_bwd("qnorm_bwd", proj, dqn, wqk[:, :QKW // 2], dproj, 0)
    dproj, wk_acc = _qknorm_bwd("knorm_bwd", proj, dkn, wqk[:, QKW // 2:], dproj, QKW // 2 // COL)
    r_f2g, r_f2u, r_f2d, r_wa, r_wc, r_wo = _run_plan_on_sequencer(
        "scatter_ffn2_and_branch_grads",
        _scatter_plan([dwgu2, dwd2, dwa_t, dwc, dwo], [0, 0, 1, 2, 3, 4], [0, HALF, 0, 0, 0, 0],
                      ffn_rows + mix_rows[1:], [D, D, D, COL, D, D], [HALF, HALF, 0, 0, 0, 0]), 4)
    dwin_t = _mm("mix_dw_in", dproj, h2, "TN", BF16, IN_W // 4, COL, 2048)
    (r_win,) = _run_plan_on_sequencer(
        "scatter_w_in_grad", _scatter_plan([dwin_t], [0], [0], mix_rows[:1], [D]), 5)
    g1, acc2, df1 = _d_h_norm_bwd("mix_d_h", dproj, win_t, x1, g2, norm_mix, sc2, sh2, before=(f1, gt1, 0.5))
    (g0, acc1), dwgu1, dwd1 = _ffn_bwd(
        "ffn1", df1, x0, g1, h1, ab1, s1, norm_ffn1, sc1, sh1, wgu1, wd1)
    r_f1g, r_f1u = _run_plan_on_sequencer(
        "scatter_ffn1_gate_up_grads",
        _scatter_plan([dwgu1], [0, 0], [0, HALF], ffn_rows[:2], [D, D], [HALF, HALF]), 7)
    (r_f1d,) = _run_plan_on_sequencer(
        "scatter_ffn1_down_grad", _scatter_plan([dwd1], [0], [0], ffn_rows[2:], [D]), 6)

    dqw = jnp.sum(wq_acc[0].reshape(12, HD), axis=0)
    dkw = jnp.sum(wk_acc[0].reshape(12, HD), axis=0)
    small = jnp.concatenate([
        acc1[0], acc1[1], acc2[3], acc2[0], acc2[1], acc3[3], acc3[0], acc3[1], acc_out[0],
        acc1[2], acc2[2], acc3[2], dqw, dkw, cw_acc[0:3].reshape(3 * D),
        jnp.zeros((HD,), F32).at[0].set(loss_part)]).reshape(1, -1)
    small_all = _small_allgather("gather_small_grads", small)
    small_sum = _sum_rows("sum_small_grads", small_all)[0]
    n_mod = N_MOD * D
    g_b_ada = small_sum[:n_mod].reshape(1, n_mod)
    g_norm1, g_norm2, g_norm3 = [small_sum[n_mod + i * D:n_mod + (i + 1) * D].reshape(1, D) for i in range(3)]
    off = n_mod + 3 * D
    g_qn, g_kn = small_sum[off:off + HD].reshape(1, HD), small_sum[off + HD:off + 2 * HD].reshape(1, HD)
    g_cw_full = small_sum[off + 2 * HD:off + 2 * HD + 3 * D].reshape(3, D)
    loss = small_sum[off + 2 * HD + 3 * D]
    g_cw = lax.dynamic_slice(g_cw_full, (0, me * cw_cols), (3, cw_cols))
    dmod_part = lax.dynamic_slice(small_all[:, 0, :n_mod], (0, me * ada_cols), (N_DEV, ada_cols))
    g_w_ada = _w_ada_grad(c_all.T, dmod_part)

    as_rows = {"ffn1_w_gate", "ffn1_w_up", "w_in", "w_attn_branch", "ffn2_w_gate", "ffn2_w_up"}
    grad_list = [g_w_ada, g_b_ada, g_norm1, r_f1g, r_f1u, r_f1d, g_norm2, r_win,
                 g_qn, g_kn, g_cw, r_wa, r_wc, r_wo, g_norm3, r_f2g, r_f2u, r_f2d]
    weights = [w_ada, b_ada, norm_ffn1, ffn1_w_gate, ffn1_w_up, ffn1_w_down, norm_mix, w_in, q_norm, k_norm,
               conv_w, w_attn_branch, w_conv_branch, w_out, norm_ffn2, ffn2_w_gate, ffn2_w_up, ffn2_w_down]
    ms = [m_w_ada, m_b_ada, m_norm_ffn1, m_ffn1_w_gate, m_ffn1_w_up, m_ffn1_w_down, m_norm_mix, m_w_in, m_q_norm,
          m_k_norm, m_conv_w, m_w_attn_branch, m_w_conv_branch, m_w_out, m_norm_ffn2, m_ffn2_w_gate,
          m_ffn2_w_up, m_ffn2_w_down]
    vs = [v_w_ada, v_b_ada, v_norm_ffn1, v_ffn1_w_gate, v_ffn1_w_up, v_ffn1_w_down, v_norm_mix, v_w_in, v_q_norm,
          v_k_norm, v_conv_w, v_w_attn_branch, v_w_conv_branch, v_w_out, v_norm_ffn2, v_ffn2_w_gate,
          v_ffn2_w_up, v_ffn2_w_down]
    wnames = ["w_ada", "b_ada", "norm_ffn1", "ffn1_w_gate", "ffn1_w_up", "ffn1_w_down", "norm_mix", "w_in",
              "q_norm", "k_norm", "conv_w", "w_attn_branch", "w_conv_branch", "w_out", "norm_ffn2",
              "ffn2_w_gate", "ffn2_w_up", "ffn2_w_down"]
    small = [i for i, gr in enumerate(grad_list) if gr.ndim == 2 and gr.size <= 16384]
    flat = lambda a, i: a.reshape(-1, weights[i].shape[-1])
    small_res = dict(zip(small, _adamw_small(
        [flat(weights[i], i) for i in small], [flat(grad_list[i], i) for i in small],
        [flat(ms[i], i) for i in small], [flat(vs[i], i) for i in small])))
    grad_out, deltas, new_ms, new_vs = [], [], [], []
    for idx, (nm, w, gr, m_, v_) in enumerate(zip(wnames, weights, grad_list, ms, vs)):
        if idx in small_res:
            gr, dl, nm_, nv_ = [r.reshape(w.shape) for r in (gr, *small_res[idx])]
        elif nm in as_rows:
            res = _adamw(f"adamw_{nm}", w[0].T, gr, m_[0].T, v_[0].T)
            gr, dl, nm_, nv_ = [r.T[None] for r in res]
        else:
            two_d = (-1, w.shape[-1])
            res = _adamw(f"adamw_{nm}", w.reshape(two_d), gr if gr.ndim == 3 else gr.reshape(two_d),
                         m_.reshape(two_d), v_.reshape(two_d))
            gr, dl, nm_, nv_ = [r.reshape(w.shape) for r in res]
        grad_out.append(gr)
        deltas.append(dl)
        new_ms.append(nm_)
        new_vs.append(nv_)
    return (loss, g0[None], *grad_out, *deltas, *new_ms, *new_vs)
```

```python
import functools

import jax
import jax.numpy as jnp
from jax import lax
from jax.experimental import pallas as pl
from jax.experimental.pallas import tpu as pltpu
from jax.experimental.pallas import tpu_sc as plsc

F32 = jnp.float32
BF16 = jnp.bfloat16
MESH = pl.DeviceIdType.MESH

N_DEV = 8
D = 1024
FF = 2816
HD = 128
N_HEADS = 4
DILATIONS = (1, 4, 16)
BAND = 128
QKW = 2 * 3 * N_HEADS * HD
IN_W = 9728
COL = 512
V_BLK, U_BLK, B_BLK, C_BLK, GA_BLK, GC_BLK = 6, 9, 11, 13, 15, 17
EPS = 1e-6
N_MOD = 9
ADAM_LR, ADAM_B1, ADAM_B2, ADAM_EPS, ADAM_WD, ADAM_STEP = 0.001, 0.9, 0.999, 1e-08, 0.01, 10

NT_DIMS = (((1,), (1,)), ((), ()))
TN_DIMS = (((0,), (0,)), ((), ()))
NN_DIMS = (((1,), (0,)), ((), ()))


def _place():
    return lax.axis_index("x"), lax.axis_index("y"), lax.axis_index("c")


def _flip(coord, bit):
    return 1 - coord if bit else coord


def _params(*sem):
    return pltpu.CompilerParams(dimension_semantics=sem)


def _small_allgather(name, v):
    n = v.shape[-1]

    def body(v_ref, out_ref, send_sems, recv_sems):
        x, y, c = _place()
        me = 4 * x + 2 * y + c
        out_ref[me] = v_ref[...]
        copies = []
        for k in range(1, N_DEV):
            peer = (_flip(x, (k >> 2) & 1), _flip(y, (k >> 1) & 1), _flip(c, k & 1))
            cp = pltpu.make_async_remote_copy(
                src_ref=v_ref, dst_ref=out_ref.at[me], send_sem=send_sems.at[k - 1],
                recv_sem=recv_sems.at[k - 1], device_id=peer, device_id_type=MESH)
            cp.start()
            copies.append(cp)
        for cp in copies:
            cp.wait()

    return pl.pallas_call(
        body, name=name,
        out_shape=jax.ShapeDtypeStruct((N_DEV, 1, n), F32),
        in_specs=[pl.BlockSpec(memory_space=pltpu.VMEM)],
        out_specs=pl.BlockSpec(memory_space=pltpu.VMEM),
        scratch_shapes=[pltpu.SemaphoreType.DMA((N_DEV - 1,)), pltpu.SemaphoreType.DMA((N_DEV - 1,))],
    )(v)


class _Plan:
    def __init__(self, operands, out_shapes, sems, phases):
        self.operands, self.out_shapes, self.sems, self.phases = operands, out_shapes, sems, phases


def _slab_start(base, rows, jump, idx):
    return pl.multiple_of(base + idx * rows + (idx // 4) * jump, 16)


def _gather_plan(shards, dst_of, base_of, dst_shapes, jump_of=None):
    n = len(shards)
    rows = [s.shape[0] for s in shards]
    jump_of = jump_of or [0] * n

    def phases(srcs, dsts, sems):
        send_sems, recv_sems, local_sems = sems
        x, y, c = _place()
        me, sibling = (x, y, c), (x, y, 1 - c)
        chips = [(1 - x, y), (x, 1 - y), (1 - x, 1 - y)]

        def slab(i, px, py, pc):
            start = _slab_start(base_of[i], rows[i], jump_of[i], 4 * px + 2 * py + pc)
            return dsts[dst_of[i]].at[pl.ds(start, rows[i])]

        def copy(i, k, block, to, src=None):
            return pltpu.make_async_remote_copy(
                src_ref=slab(i, *block) if src is None else src, dst_ref=slab(i, *block),
                send_sem=send_sems.at[i, k], recv_sem=recv_sems.at[i, k],
                device_id=to, device_id_type=MESH)

        def mine():
            return [pltpu.make_async_copy(srcs[i], slab(i, *me), local_sems.at[i]) for i in range(n)]

        def first():
            out = []
            for i in range(n):
                out.append(copy(i, 0, me, sibling, src=srcs[i]))
                out += [copy(i, 1 + j, me, (*chip, c), src=srcs[i]) for j, chip in enumerate(chips)]
            return out

        def passed():
            return [(copy(i, 1 + j, (*chip, c), me), copy(i, 4 + j, (*chip, c), sibling))
                    for j, chip in enumerate(chips) for i in range(n)]

        def start():
            for cp in mine() + first():
                cp.start()

        def middle():
            for landed, onward in passed():
                landed.wait_recv()
                onward.start()

        def finish():
            for i in range(n):
                copy(i, 0, sibling, me).wait_recv()
                for j, chip in enumerate(chips):
                    copy(i, 4 + j, (*chip, 1 - c), me).wait_recv()
            for cp in first() + [onward for _, onward in passed()]:
                cp.wait_send()
            for cp in mine():
                cp.wait()

        return start, middle, finish

    sems = [pltpu.SemaphoreType.DMA((n, 7)), pltpu.SemaphoreType.DMA((n, 7)), pltpu.SemaphoreType.DMA((n,))]
    return _Plan(list(shards), [jax.ShapeDtypeStruct(s, BF16) for s in dst_shapes], sems, phases)


def _scatter_plan(grads, src_of, base_of, rows, cols, jump_of=None):
    n = len(rows)
    jump_of = jump_of or [0] * n

    def phases(srcs, recvs, sems):
        send_sems, recv_sems, local_sems = sems
        x, y, c = _place()
        me = 4 * x + 2 * y + c

        def slab(i, idx):
            start = _slab_start(base_of[i], rows[i], jump_of[i], idx)
            return srcs[src_of[i]].at[pl.ds(start, rows[i])]

        def copies():
            out = [pltpu.make_async_copy(slab(i, me), recvs[i].at[me], local_sems.at[i]) for i in range(n)]
            for k in range(1, N_DEV):
                px, py, pc = _flip(x, (k >> 2) & 1), _flip(y, (k >> 1) & 1), _flip(c, k & 1)
                out += [pltpu.make_async_remote_copy(
                    src_ref=slab(i, 4 * px + 2 * py + pc), dst_ref=recvs[i].at[me],
                    send_sem=send_sems.at[i, k - 1], recv_sem=recv_sems.at[i, k - 1],
                    device_id=(px, py, pc), device_id_type=MESH) for i in range(n)]
            return out

        def start():
            for cp in copies():
                cp.start()

        def finish():
            for cp in copies():
                cp.wait()

        return start, None, finish

    sems = [pltpu.SemaphoreType.DMA((n, 7)), pltpu.SemaphoreType.DMA((n, 7)), pltpu.SemaphoreType.DMA((n,))]
    out_shapes = [jax.ShapeDtypeStruct((N_DEV, rows[i], cols[i]), BF16) for i in range(n)]
    return _Plan(list(grads), out_shapes, sems, phases)


def _run_plan(name, plan):
    n_in, n_out = len(plan.operands), len(plan.out_shapes)

    def body(*refs):
        for phase in plan.phases(refs[:n_in], refs[n_in:n_in + n_out], refs[n_in + n_out:]):
            if phase is not None:
                phase()

    hbm = pl.BlockSpec(memory_space=pltpu.HBM)
    return pl.pallas_call(
        body, name=name, out_shape=plan.out_shapes,
        in_specs=[hbm] * n_in, out_specs=[hbm] * n_out, scratch_shapes=plan.sems,
    )(*plan.operands)


def _run_plan_on_sequencer(name, plan, collective_id):
    src_refs = [jax.new_ref(a, memory_space=pltpu.MemorySpace.HBM) for a in plan.operands]
    dst_refs = [jax.empty_ref(s, memory_space=pltpu.MemorySpace.HBM) for s in plan.out_shapes]

    @pl.kernel(mesh=plsc.ScalarSubcoreMesh(axis_name="sequencer", num_cores=1), name=name,
               scratch_types=tuple(plan.sems),
               compiler_params=pltpu.CompilerParams(collective_id=collective_id))
    def launch(*sems):
        x, y, c = _place()
        barrier = pltpu.get_barrier_semaphore()
        for k in range(1, N_DEV):
            peer = (_flip(x, (k >> 2) & 1), _flip(y, (k >> 1) & 1), _flip(c, k & 1))
            pl.semaphore_signal(barrier, inc=1, device_id=peer, device_id_type=MESH)
        pl.semaphore_wait(barrier, N_DEV - 1)
        for phase in plan.phases(src_refs, dst_refs, sems):
            if phase is not None:
                phase()

    launch()
    return [r[...] for r in dst_refs]


def _sum_contributions(name, recv):
    _, rows, cols = recv.shape
    tr = rows if rows <= 512 else 304 if rows % 304 == 0 else 256

    def body(r_ref, o_ref):
        acc = r_ref[0].astype(F32)
        for k in range(1, N_DEV):
            acc = acc + r_ref[k].astype(F32)
        o_ref[...] = acc

    return pl.pallas_call(
        body, name=name, grid=(rows // tr,),
        out_shape=jax.ShapeDtypeStruct((rows, cols), F32),
        in_specs=[pl.BlockSpec((N_DEV, tr, cols), lambda i: (0, i, 0))],
        out_specs=pl.BlockSpec((tr, cols), lambda i: (i, 0)),
        compiler_params=_params("parallel"),
    )(recv)


def _mm(name, a, b, mode, out_dtype, tm, tn, tk, *, carry=None, tiles_in=(), tiles_out=(), epilogue=None,
        n_outer=False, keep_b=False, col_chunks=None):
    if mode == "TN":
        kk, m = a.shape
    else:
        m, kk = a.shape
    n = b.shape[0] if mode == "NT" else b.shape[1]
    tm, tn, tk = min(tm, m), min(tn, n), min(tk, kk)
    assert m % tm == 0 and n % tn == 0 and kk % tk == 0, (name, m, n, kk, tm, tn, tk)
    ni, nj, nk = m // tm, n // tn, kk // tk
    steps = ni * nj * nk
    dims = {"NN": NN_DIMS, "NT": NT_DIMS, "TN": TN_DIMS}[mode]
    if epilogue is None:
        tiles_out = [(jax.ShapeDtypeStruct((m, n), out_dtype), (tm, tn), lambda i, j: (i, j))]
    n_tin, n_tout = len(tiles_in), len(tiles_out)
    n_in = len(carry.operands) if carry else 0
    n_out = len(carry.out_shapes) if carry else 0
    n_acc = 1 if nk > 1 else 0
    n_keep = 2 if keep_b else 0
    assert not carry or steps >= 3
    assert not keep_b or (nk == 1 and nj == 1)
    assert not col_chunks or (epilogue is not None and nk == 1 and mode != "TN")
    ij = (lambda p, q: (q, p)) if n_outer else (lambda p, q: (p, q))
    inner = ni if n_outer else nj

    def body(a_ref, b_ref, *rest):
        tin = rest[:n_tin]
        cin = rest[n_tin:n_tin + n_in]
        tout = rest[n_tin + n_in:n_tin + n_in + n_tout]
        cout = rest[n_tin + n_in + n_tout:n_tin + n_in + n_tout + n_out]
        scratch = rest[n_tin + n_in + n_tout + n_out:]
        k = pl.program_id(2)
        visit = pl.program_id(0) * inner + pl.program_id(1)
        step = visit * nk + k
        if keep_b:
            b_kept, b_sem = scratch[n_acc:n_acc + 2]

            @pl.when(step == 0)
            def _():
                cp = pltpu.make_async_copy(b_ref, b_kept, b_sem)
                cp.start()
                cp.wait()

            b_ref = b_kept
        if carry:
            start, middle, finish = carry.phases(cin, cout, scratch[n_acc + n_keep:])
            pl.when(step == 0)(start)

        def store(prod, c=0, cols=()):
            if epilogue is None:
                tout[0][...] = prod.astype(out_dtype)
            else:
                epilogue(prod, jnp.logical_and(visit == 0, c == 0), tin, tout, *cols)

        if col_chunks:
            for c, (c0, cw) in enumerate(col_chunks):
                b_part = b_ref[pl.ds(c0, cw), :] if mode == "NT" else b_ref[:, pl.ds(c0, cw)]
                store(lax.dot_general(a_ref[...], b_part, dims, preferred_element_type=F32), c, ((c0, cw),))
        else:
            part = lax.dot_general(a_ref[...], b_ref[...], dims, preferred_element_type=F32)
            if nk == 1:
                store(part)
            else:
                acc_ref = scratch[0]

                @pl.when(k == 0)
                def _():
                    acc_ref[...] = part

                @pl.when((k > 0) & (k < nk - 1))
                def _():
                    acc_ref[...] += part

                @pl.when(k == nk - 1)
                def _():
                    store(acc_ref[...] + part)

        if carry:
            if middle is not None:
                pl.when(step == (steps * 3) // 5)(middle)
            pl.when(step == steps - 1)(finish)

    def spec(shape, fn):
        return pl.BlockSpec(shape, lambda p, q, k: fn(*ij(p, q)))

    a_spec = (pl.BlockSpec((tk, tm), lambda p, q, k: (k, ij(p, q)[0])) if mode == "TN"
              else pl.BlockSpec((tm, tk), lambda p, q, k: (ij(p, q)[0], k)))
    if keep_b:
        b_spec = pl.BlockSpec(memory_space=pl.ANY)
    elif mode == "NT":
        b_spec = pl.BlockSpec((tn, tk), lambda p, q, k: (ij(p, q)[1], k))
    else:
        b_spec = pl.BlockSpec((tk, tn), lambda p, q, k: (k, ij(p, q)[1]))
    hbm = pl.BlockSpec(memory_space=pltpu.HBM)
    sequential = carry or epilogue or keep_b
    out = pl.pallas_call(
        body, name=name, grid=(nj, ni, nk) if n_outer else (ni, nj, nk),
        out_shape=[t[0] for t in tiles_out] + (carry.out_shapes if carry else []),
        in_specs=[a_spec, b_spec] + [spec(t[1], t[2]) for t in tiles_in] + [hbm] * n_in,
        out_specs=[spec(t[1], t[2]) for t in tiles_out] + [hbm] * n_out,
        scratch_shapes=([pltpu.VMEM((tm, tn), F32)] * n_acc
                        + ([pltpu.VMEM(b.shape, b.dtype), pltpu.SemaphoreType.DMA] if keep_b else [])
                        + (carry.sems if carry else [])),
        compiler_params=(_params("arbitrary", "arbitrary", "arbitrary") if sequential
                         else _params("parallel", "parallel", "arbitrary")),
    )(a, b, *[t[0] for t in tiles_in], *(carry.operands if carry else []))
    return out if (carry or epilogue) else out[0]


def _row(tm, w, off=0):
    return pl.BlockSpec((tm, w), lambda i: (i, off))


def _vec(w):
    return pl.BlockSpec((1, w), lambda i: (0, 0))


def _sigmoid(x):
    return 0.5 * jnp.tanh(0.5 * x) + 0.5


def _normmod(name, x, g, sc, sh, tm=512):
    s = x.shape[0]

    def body(x_ref, g_ref, sc_ref, sh_ref, h_ref):
        xv = x_ref[...]
        r = lax.rsqrt(jnp.mean(xv * xv, axis=-1, keepdims=True) + EPS)
        h_ref[...] = ((xv * r) * g_ref[...] * (1.0 + sc_ref[...]) + sh_ref[...]).astype(BF16)

    return pl.pallas_call(
        body, name=name, grid=(s // tm,),
        out_shape=jax.ShapeDtypeStruct((s, D), BF16),
        in_specs=[_row(tm, D), _vec(D), _vec(D), _vec(D)], out_specs=_row(tm, D),
        compiler_params=_params("parallel"),
    )(x, g, sc, sh)


def _normmod_bwd(name, dh, x, gin, g, sc, sh, tm=512):
    s = x.shape[0]

    def body(dh_ref, x_ref, gin_ref, g_ref, sc_ref, sh_ref, gout_ref, acc_ref):
        xv, dhv = x_ref[...], dh_ref[...]
        r = lax.rsqrt(jnp.mean(xv * xv, axis=-1, keepdims=True) + EPS)
        nv = xv * r
        gv, one_sc = g_ref[...], 1.0 + sc_ref[...]
        dn = dhv * gv * one_sc
        dx = r * (dn - nv * jnp.mean(dn * nv, axis=-1, keepdims=True))
        gout_ref[...] = gin_ref[...] + dx

        @pl.when(pl.program_id(0) == 0)
        def _():
            acc_ref[...] = jnp.zeros_like(acc_ref)

        dhn = dhv * nv
        acc_ref[0:1, :] += jnp.sum(dhv, axis=0, keepdims=True)
        acc_ref[1:2, :] += jnp.sum(dhn * gv, axis=0, keepdims=True)
        acc_ref[2:3, :] += jnp.sum(dhn * one_sc, axis=0, keepdims=True)

    return pl.pallas_call(
        body, name=name, grid=(s // tm,),
        out_shape=[jax.ShapeDtypeStruct((s, D), F32), jax.ShapeDtypeStruct((8, D), F32)],
        in_specs=[_row(tm, D), _row(tm, D), _row(tm, D), _vec(D), _vec(D), _vec(D)],
        out_specs=[_row(tm, D), pl.BlockSpec((8, D), lambda i: (0, 0))],
        compiler_params=_params("arbitrary"),
    )(dh, x, gin, g, sc, sh)


def _swiglu(name, ab, tm=512):
    s = ab.shape[0]

    def body(ab_ref, s_ref):
        a = ab_ref[:, :FF].astype(F32)
        b = ab_ref[:, FF:].astype(F32)
        s_ref[...] = (a * _sigmoid(a) * b).astype(BF16)

    return pl.pallas_call(
        body, name=name, grid=(s // tm,),
        out_shape=jax.ShapeDtypeStruct((s, FF), BF16),
        in_specs=[_row(tm, 2 * FF)], out_specs=_row(tm, FF),
        compiler_params=_params("parallel"),
    )(ab)


def _swiglu_bwd(name, ds, ab, tm=256):
    s = ab.shape[0]

    def body(ds_ref, ab_ref, dab_ref):
        a = ab_ref[:, :FF].astype(F32)
        b = ab_ref[:, FF:].astype(F32)
        dsv = ds_ref[...].astype(F32)
        sig = _sigmoid(a)
        dab_ref[:, :FF] = (dsv * b * (sig * (1.0 + a * (1.0 - sig)))).astype(BF16)
        dab_ref[:, FF:] = (dsv * (a * sig)).astype(BF16)

    return pl.pallas_call(
        body, name=name, grid=(s // tm,),
        out_shape=jax.ShapeDtypeStruct((s, 2 * FF), BF16),
        in_specs=[_row(tm, FF), _row(tm, 2 * FF)], out_specs=_row(tm, 2 * FF),
        compiler_params=_params("parallel"),
    )(ds, ab)


def _residual(name, x, f, gt, coef, tm=512):
    s = x.shape[0]

    def body(x_ref, f_ref, gt_ref, o_ref):
        o_ref[...] = x_ref[...] + (coef * gt_ref[...]) * f_ref[...]

    return pl.pallas_call(
        body, name=name, grid=(s // tm,),
        out_shape=jax.ShapeDtypeStruct((s, D), F32),
        in_specs=[_row(tm, D), _row(tm, D), _vec(D)], out_specs=_row(tm, D),
        compiler_params=_params("parallel"),
    )(x, f, gt)


def _gate_bwd(name, gin, f, gt, coef, tm=512):
    s = gin.shape[0]

    def body(g_ref, f_ref, gt_ref, df_ref, acc_ref):
        gv = g_ref[...]
        df_ref[...] = ((coef * gt_ref[...]) * gv).astype(BF16)

        @pl.when(pl.program_id(0) == 0)
        def _():
            acc_ref[...] = jnp.zeros_like(acc_ref)

        acc_ref[0:1, :] += coef * jnp.sum(gv * f_ref[...], axis=0, keepdims=True)

    return pl.pallas_call(
        body, name=name, grid=(s // tm,),
        out_shape=[jax.ShapeDtypeStruct((s, D), BF16), jax.ShapeDtypeStruct((8, D), F32)],
        in_specs=[_row(tm, D), _row(tm, D), _vec(D)],
        out_specs=[_row(tm, D), pl.BlockSpec((8, D), lambda i: (0, 0))],
        compiler_params=_params("arbitrary"),
    )(gin, f, gt)


def _loss_grad(x3, target, tm=512):
    s = x3.shape[0]

    def body(y_ref, t_ref, g_ref, l_ref):
        e = y_ref[...] - t_ref[...]
        g_ref[...] = e * (1.0 / D)

        @pl.when(pl.program_id(0) == 0)
        def _():
            l_ref[...] = jnp.zeros_like(l_ref)

        l_ref[...] += jnp.sum(jnp.mean(e * e, axis=-1, keepdims=True), axis=0, keepdims=True) * 0.5

    return pl.pallas_call(
        body, name="loss_grad", grid=(s // tm,),
        out_shape=[jax.ShapeDtypeStruct((s, D), F32), jax.ShapeDtypeStruct((8, 128), F32)],
        in_specs=[_row(tm, D), _row(tm, D)],
        out_specs=[_row(tm, D), pl.BlockSpec((8, 128), lambda i: (0, 0))],
        compiler_params=_params("arbitrary"),
    )(x3, target)


def _heads(x, fn):
    return jnp.concatenate([fn(x[:, h * HD:(h + 1) * HD], h) for h in range(COL // HD)], axis=1)


def _qknorm(proj, wqk, tm=1024):
    s = proj.shape[0]

    def body(p_ref, w_ref, o_ref):
        pv = p_ref[...].astype(F32)
        wv = w_ref[...]

        def one(qh, h):
            r = lax.rsqrt(jnp.mean(qh * qh, axis=-1, keepdims=True) + EPS)
            return (qh * r) * wv[:, h * HD:(h + 1) * HD]

        o_ref[...] = _heads(pv, one).astype(BF16)

    return pl.pallas_call(
        body, name="qknorm", grid=(s // tm, QKW // COL),
        out_shape=jax.ShapeDtypeStruct((s, QKW), BF16),
        in_specs=[pl.BlockSpec((tm, COL), lambda i, j: (i, j)), pl.BlockSpec((1, COL), lambda i, j: (0, j))],
        out_specs=pl.BlockSpec((tm, COL), lambda i, j: (i, j)),
        compiler_params=_params("parallel", "parallel"),
    )(proj, wqk)


def _qknorm_bwd(name, proj, dn, w, dproj, blk0, tm=1024):
    s = proj.shape[0]
    nblk = dn.shape[1] // COL

    def body(p_ref, d_ref, w_ref, _, o_ref, acc_ref):
        pv = p_ref[...].astype(F32)
        dv = d_ref[...]
        wv = w_ref[...]
        sums = []

        def one(qh, h):
            dn = dv[:, h * HD:(h + 1) * HD]
            r = lax.rsqrt(jnp.mean(qh * qh, axis=-1, keepdims=True) + EPS)
            nh = qh * r
            sums.append(jnp.sum(dn * nh, axis=0, keepdims=True))
            dnw = dn * wv[:, h * HD:(h + 1) * HD]
            return r * (dnw - nh * jnp.mean(dnw * nh, axis=-1, keepdims=True))

        o_ref[...] = _heads(pv, one).astype(BF16)

        @pl.when(pl.program_id(1) == 0)
        def _():
            acc_ref[...] = jnp.zeros_like(acc_ref)

        acc_ref[0:1, :] += jnp.concatenate(sums, axis=1)

    return pl.pallas_call(
        body, name=name, grid=(nblk, s // tm),
        out_shape=[jax.ShapeDtypeStruct((s, IN_W), BF16), jax.ShapeDtypeStruct((8, nblk * COL), F32)],
        in_specs=[pl.BlockSpec((tm, COL), lambda j, i: (i, blk0 + j)), pl.BlockSpec((tm, COL), lambda j, i: (i, j)),
                  pl.BlockSpec((1, COL), lambda j, i: (0, j)), pl.BlockSpec(memory_space=pl.ANY)],
        out_specs=[pl.BlockSpec((tm, COL), lambda j, i: (i, blk0 + j)),
                   pl.BlockSpec((8, COL), lambda j, i: (0, j))],
        input_output_aliases={3: 0},
        compiler_params=_params("arbitrary", "arbitrary"),
    )(proj, dn, w, dproj)


def _attn_shapes(s, g):
    d = DILATIONS[g]
    tb = min(s, max(2048, 256 * d))
    sb = min(256, tb // d)
    pb = BAND * d
    assert s % tb == 0 and tb % pb == 0 and (tb // d) % sb == 0 and sb % BAND == 0
    return d, tb, sb, pb


def _lanes(x, width):
    return jnp.concatenate([x] * (width // HD), axis=1)


def _every(start, size, d):
    return pl.ds(start, size, stride=d) if d > 1 else pl.ds(start, size)


def _attn_specs(g, tb, pb, s, ahead):
    ratio = tb // pb
    if ahead:
        nbr = lambda n: jnp.minimum((n + 1) * ratio, s // pb - 1)
    else:
        nbr = lambda n: jnp.maximum(n * ratio - 1, 0)
    cur = lambda base: pl.BlockSpec((tb, HD), lambda h, n: (n, base + g * N_HEADS + h))
    side = lambda base: pl.BlockSpec((pb, HD), lambda h, n: (nbr(n), base + g * N_HEADS + h))
    tok = pl.BlockSpec((tb, HD), lambda h, n: (n, h))
    tok_side = pl.BlockSpec((pb, HD), lambda h, n: (nbr(n), h))
    return cur, side, tok, tok_side


Q_COL, K_COL, V_COL = 0, 12, 24


def _attn_fwd(g, qkn, proj):
    s = qkn.shape[0]
    d, tb, sb, pb = _attn_shapes(s, g)
    ft = F32 if d > 1 else BF16
    nj = tb // d // sb
    scale = HD ** -0.5

    def body(q_ref, kc_ref, kp_ref, vc_ref, vp_ref, o_ref, lse_ref, qf, kf, vf):
        n = pl.program_id(1)
        qf[...] = q_ref[...].astype(ft)
        kf[0:pb] = kp_ref[...].astype(ft)
        kf[pb:] = kc_ref[...].astype(ft)
        vf[0:pb] = vp_ref[...].astype(ft)
        vf[pb:] = vc_ref[...].astype(ft)
        for r in range(d):
            for j in range(nj):
                at = j * sb * d + r
                q = qf[_every(at, sb, d), :].astype(BF16)
                k = kf[_every(at, sb + BAND, d), :].astype(BF16)
                v = vf[_every(at, sb + BAND, d), :].astype(BF16)
                sc = lax.dot_general(q, k, NT_DIMS, preferred_element_type=F32) * scale
                qi = lax.broadcasted_iota(jnp.int32, sc.shape, 0)
                kj = lax.broadcasted_iota(jnp.int32, sc.shape, 1)
                valid = (kj >= qi) & (kj <= qi + BAND)
                if j == 0:
                    valid = valid & ((kj >= BAND) | (n > 0))
                sc = jnp.where(valid, sc, -1e30)
                m = jnp.max(sc, axis=-1, keepdims=True)
                p = jnp.exp(sc - m)
                l = jnp.sum(p, axis=-1, keepdims=True)
                o = lax.dot_general(p.astype(BF16), v, NN_DIMS, preferred_element_type=F32)
                o_ref[_every(at, sb, d), :] = o / l
                lse_ref[_every(at, sb, d), :] = jnp.broadcast_to(m + jnp.log(l), (sb, HD))

    cur, side, tok, _ = _attn_specs(g, tb, pb, s, ahead=False)
    return pl.pallas_call(
        body, name=f"attn_fwd_g{g}", grid=(N_HEADS, s // tb),
        out_shape=[jax.ShapeDtypeStruct((s, COL), F32)] * 2,
        in_specs=[cur(Q_COL), cur(K_COL), side(K_COL), cur(V_COL), side(V_COL)],
        out_specs=[tok, tok],
        scratch_shapes=[pltpu.VMEM((tb, HD), ft), pltpu.VMEM((tb + pb, HD), ft),
                        pltpu.VMEM((tb + pb, HD), ft)],
        compiler_params=_params("parallel", "arbitrary"),
    )(qkn, qkn, qkn, proj, proj)


def _attn_combine(os_, lses, tm=512):
    s = os_[0].shape[0]

    def body(o0, o1, o2, l0, l1, l2, o_ref, lse_ref):
        a, b, c = l0[...], l1[...], l2[...]
        m = jnp.maximum(jnp.maximum(a, b), c)
        ea, eb, ec = jnp.exp(a - m), jnp.exp(b - m), jnp.exp(c - m)
        tot = ea + eb + ec
        o_ref[...] = ((ea * o0[...] + eb * o1[...] + ec * o2[...]) / tot).astype(BF16)
        lse_ref[...] = m + jnp.log(tot)

    return pl.pallas_call(
        body, name="attn_combine", grid=(s // tm,),
        out_shape=[jax.ShapeDtypeStruct((s, COL), BF16), jax.ShapeDtypeStruct((s, COL), F32)],
        in_specs=[_row(tm, COL)] * 6, out_specs=[_row(tm, COL)] * 2,
        compiler_params=_params("parallel"),
    )(*os_, *lses)


def _attn_delta(do, o, tm=512):
    s = do.shape[0]

    def body(do_ref, o_ref, del_ref):
        prod = do_ref[...] * o_ref[...].astype(F32)
        del_ref[...] = _heads(prod, lambda ph, h: jnp.broadcast_to(
            jnp.sum(ph, axis=-1, keepdims=True), ph.shape))

    return pl.pallas_call(
        body, name="attn_delta", grid=(s // tm,),
        out_shape=jax.ShapeDtypeStruct((s, COL), F32),
        in_specs=[_row(tm, COL)] * 2, out_specs=_row(tm, COL),
        compiler_params=_params("parallel"),
    )(do, o)


def _attn_bwd(g, qkn, proj, do, lse, delta, dqn, dkn, dproj):
    s = qkn.shape[0]
    d, tb, sb, pb = _attn_shapes(s, g)
    ft = F32 if d > 1 else BF16
    nj = tb // d // sb
    nt = s // tb
    scale = HD ** -0.5
    chained = dqn is not None

    def body(k_ref, v_ref, qc_ref, qn_ref, doc_ref, don_ref, lc_ref, ln_ref, dc_ref, dn_ref, *rest):
        dq_ref, dk_ref, dv_ref, kf, vf, qf, dvf, later = rest[-8:]
        n = pl.program_id(1)
        kf[...] = k_ref[...].astype(ft)
        vf[...] = v_ref[...].astype(ft)
        qf[0:tb] = qc_ref[...].astype(ft)
        qf[tb:] = qn_ref[...].astype(ft)

        @pl.when(n == 0)
        def _():
            later[...] = jnp.zeros_like(later)

        def window(c_ref, n_ref, r, j):
            at = j * sb * d + r
            if j < nj - 1:
                return c_ref[_every(at, sb + BAND, d), :]
            return jnp.concatenate([c_ref[_every(at, sb, d), :], n_ref[_every(r, BAND, d), :]], axis=0)

        for r in range(d):
            tail = later[r]
            for j in range(nj):
                at = j * sb * d + r
                rows = _every(at, sb, d)
                k = kf[rows, :].astype(BF16)
                v = vf[rows, :].astype(BF16)
                q = qf[_every(at, sb + BAND, d), :].astype(BF16)
                dov = window(doc_ref, don_ref, r, j).astype(BF16)
                sc = lax.dot_general(q, k, NT_DIMS, preferred_element_type=F32) * scale
                qi = lax.broadcasted_iota(jnp.int32, sc.shape, 0)
                kj = lax.broadcasted_iota(jnp.int32, sc.shape, 1)
                valid = (qi >= kj) & (qi <= kj + BAND)
                if j == nj - 1:
                    valid = valid & ((qi < sb) | (n < nt - 1))
                p = jnp.exp(jnp.where(valid, sc - _lanes(window(lc_ref, ln_ref, r, j), sb), -1e30))
                dp = lax.dot_general(dov, v, NT_DIMS, preferred_element_type=F32)
                ds = (p * (dp - _lanes(window(dc_ref, dn_ref, r, j), sb)) * scale).astype(BF16)
                dvf[rows, :] = lax.dot_general(p.astype(BF16), dov, TN_DIMS, preferred_element_type=F32)
                dk_ref[rows, :] = lax.dot_general(ds, q, TN_DIMS, preferred_element_type=F32)
                dqw = lax.dot_general(ds, k, NN_DIMS, preferred_element_type=F32)
                first = dqw[:BAND] + tail
                dq_ref[rows, :] = first if sb == BAND else jnp.concatenate([first, dqw[BAND:sb]], axis=0)
                tail = dqw[sb:]
            later[r] = tail
        dv_ref[...] = dvf[...].astype(BF16)

    cur, side, tok, tok_side = _attn_specs(g, tb, pb, s, ahead=True)
    anyspec = pl.BlockSpec(memory_space=pl.ANY)
    n_heads_cols = 3 * N_HEADS * HD
    return pl.pallas_call(
        body, name=f"attn_bwd_g{g}", grid=(N_HEADS, nt),
        out_shape=[jax.ShapeDtypeStruct((s, n_heads_cols), F32), jax.ShapeDtypeStruct((s, n_heads_cols), F32),
                   jax.ShapeDtypeStruct((s, IN_W), BF16)],
        in_specs=[cur(K_COL), cur(V_COL), cur(Q_COL), side(Q_COL), tok, tok_side, tok, tok_side,
                  tok, tok_side] + ([anyspec, anyspec] if chained else []) + [anyspec],
        out_specs=[cur(0), cur(0), cur(V_COL)],
        input_output_aliases={10: 0, 11: 1, 12: 2} if chained else {10: 2},
        scratch_shapes=[pltpu.VMEM((tb, HD), ft), pltpu.VMEM((tb, HD), ft),
                        pltpu.VMEM((tb + pb, HD), ft), pltpu.VMEM((tb, HD), F32),
                        pltpu.VMEM((d, BAND, HD), F32)],
        compiler_params=_params("arbitrary", "arbitrary"),
    )(qkn, proj, qkn, qkn, do, do, lse, lse, delta, delta, *([dqn, dkn] if chained else []), dproj)


def _shift_down(x, before, k):
    rolled = pltpu.roll(x, k, 0)
    head = jnp.where(lax.broadcasted_iota(jnp.int32, before.shape, 0) < k, pltpu.roll(before, k, 0), rolled[:8])
    return jnp.concatenate([head, rolled[8:]], axis=0)


def _shift_up(x, after, k):
    rows = x.shape[0]
    rolled = pltpu.roll(x, rows - k, 0)
    tail = jnp.where(lax.broadcasted_iota(jnp.int32, after.shape, 0) >= 8 - k,
                     pltpu.roll(after, 8 - k, 0), rolled[rows - 8:])
    return jnp.concatenate([rolled[:rows - 8], tail], axis=0)


def _conv_fwd(proj, cw, tm=1024):
    s = proj.shape[0]
    r16 = tm // 16

    def body(u_ref, b_ref, c_ref, up_ref, cp_ref, w_ref, z_ref):
        i = pl.program_id(1)
        xc = c_ref[...].astype(F32) * u_ref[...].astype(F32)
        xp = jnp.where(i > 0, cp_ref[8:16, :].astype(F32) * up_ref[8:16, :].astype(F32), 0.0)
        w = w_ref[...]
        conv = _shift_down(xc, xp, 2) * w[0:1] + _shift_down(xc, xp, 1) * w[1:2] + xc * w[2:3]
        z_ref[...] = (b_ref[...].astype(F32) * conv).astype(BF16)

    tile = lambda blk: pl.BlockSpec((tm, COL), lambda j, i: (i, blk + j))
    before = lambda blk: pl.BlockSpec((16, COL), lambda j, i: (jnp.maximum(i * r16 - 1, 0), blk + j))
    return pl.pallas_call(
        body, name="conv_fwd", grid=(D // COL, s // tm),
        out_shape=jax.ShapeDtypeStruct((s, D), BF16),
        in_specs=[tile(U_BLK), tile(B_BLK), tile(C_BLK), before(U_BLK), before(C_BLK),
                  pl.BlockSpec((3, COL), lambda j, i: (0, j))],
        out_specs=pl.BlockSpec((tm, COL), lambda j, i: (i, j)),
        compiler_params=_params("parallel", "parallel"),
    )(proj, proj, proj, proj, proj, cw)


def _conv_bwd(dz, proj, cw, dproj, tm=1024):
    s = proj.shape[0]
    r16 = tm // 16
    nrow = s // tm

    def body(dz_ref, u_ref, b_ref, c_ref, up_ref, cp_ref, dzn_ref, bn_ref, w_ref, _, o_ref, dc_ref, acc_ref):
        piece, i = pl.program_id(1), pl.program_id(2)
        u, c = u_ref[...].astype(F32), c_ref[...].astype(F32)
        bv = b_ref[...].astype(F32)
        dzv = dz_ref[...]
        w = w_ref[...]

        @pl.when((piece == 0) & (i == 0))
        def _():
            acc_ref[...] = jnp.zeros_like(acc_ref)

        @pl.when(piece == 0)
        def _():
            xc = c * u
            xp = jnp.where(i > 0, cp_ref[8:16, :].astype(F32) * up_ref[8:16, :].astype(F32), 0.0)
            x2, x1 = _shift_down(xc, xp, 2), _shift_down(xc, xp, 1)
            o_ref[...] = (dzv * (x2 * w[0:1] + x1 * w[1:2] + xc * w[2:3])).astype(BF16)
            dc_ref[...] = jnp.zeros_like(dc_ref)
            dconv = dzv * bv
            acc_ref[0:1, :] += jnp.sum(dconv * x2, axis=0, keepdims=True)
            acc_ref[1:2, :] += jnp.sum(dconv * x1, axis=0, keepdims=True)
            acc_ref[2:3, :] += jnp.sum(dconv * xc, axis=0, keepdims=True)

        @pl.when(piece == 1)
        def _():
            dconv = dzv * bv
            dn = jnp.where(i < nrow - 1, dzn_ref[...] * bn_ref[0:8, :].astype(F32), 0.0)
            dxc = dconv * w[2:3] + _shift_up(dconv, dn, 1) * w[1:2] + _shift_up(dconv, dn, 2) * w[0:1]
            o_ref[...] = (dxc * c).astype(BF16)
            dc_ref[...] = (dxc * u).astype(BF16)

    tile = lambda blk: pl.BlockSpec((tm, COL), lambda j, p, i: (i, blk + j))
    before = lambda blk: pl.BlockSpec((16, COL), lambda j, p, i: (jnp.maximum(i * r16 - 1, 0), blk + j))
    after = lambda rows, blk: pl.BlockSpec(
        (rows, COL), lambda j, p, i: (jnp.minimum((i + 1) * (tm // rows), s // rows - 1), blk + j))
    return pl.pallas_call(
        body, name="conv_bwd", grid=(D // COL, 2, nrow),
        out_shape=[jax.ShapeDtypeStruct((s, IN_W), BF16), jax.ShapeDtypeStruct((s + tm, D), BF16),
                   jax.ShapeDtypeStruct((8, D), F32)],
        in_specs=[tile(0), tile(U_BLK), tile(B_BLK), tile(C_BLK), before(U_BLK), before(C_BLK),
                  after(8, 0), after(16, B_BLK), pl.BlockSpec((3, COL), lambda j, p, i: (0, j)),
                  pl.BlockSpec(memory_space=pl.ANY)],
        out_specs=[pl.BlockSpec((tm, COL), lambda j, p, i: (i, jnp.where(p == 0, B_BLK, U_BLK) + j)),
                   pl.BlockSpec((tm, COL), lambda j, p, i: (jnp.where(p == 0, nrow, i), j)),
                   pl.BlockSpec((8, COL), lambda j, p, i: (0, j))],
        input_output_aliases={9: 0},
        compiler_params=_params("arbitrary", "arbitrary", "arbitrary"),
    )(dz, proj, proj, proj, proj, proj, dz, proj, cw, dproj)


def _copy_columns(name, src, dst, blk0, tm=1024):
    s, w = dst.shape[0], src.shape[1]

    def body(x_ref, _, o_ref):
        o_ref[...] = x_ref[...]

    return pl.pallas_call(
        body, name=name, grid=(w // COL, s // tm),
        out_shape=jax.ShapeDtypeStruct(dst.shape, dst.dtype),
        in_specs=[pl.BlockSpec((tm, COL), lambda j, i: (i, j)), pl.BlockSpec(memory_space=pl.ANY)],
        out_specs=pl.BlockSpec((tm, COL), lambda j, i: (i, blk0 + j)),
        input_output_aliases={1: 0},
        compiler_params=_params("parallel", "parallel"),
    )(src, dst)


def _merge_fwd(ya, yc, proj, tm=512):
    s = proj.shape[0]

    def body(ya_ref, yc_ref, ga_ref, gc_ref, o_ref):
        o_ref[...] = (_sigmoid(ga_ref[...].astype(F32)) * ya_ref[...].astype(F32)
                      + _sigmoid(gc_ref[...].astype(F32)) * yc_ref[...].astype(F32)).astype(BF16)

    tile = lambda blk: pl.BlockSpec((tm, COL), lambda j, i: (i, blk + j))
    return pl.pallas_call(
        body, name="merge_fwd", grid=(D // COL, s // tm),
        out_shape=jax.ShapeDtypeStruct((s, D), BF16),
        in_specs=[tile(0), tile(0), tile(GA_BLK), tile(GC_BLK)], out_specs=tile(0),
        compiler_params=_params("parallel", "parallel"),
    )(ya, yc, proj, proj)


def _merge_bwd_branches(dm, proj, tm=512):
    s = proj.shape[0]

    def body(dm_ref, ga_ref, gc_ref, dya_ref, dyc_ref):
        dmv = dm_ref[...]
        dya_ref[...] = (dmv * _sigmoid(ga_ref[...].astype(F32))).astype(BF16)
        dyc_ref[...] = (dmv * _sigmoid(gc_ref[...].astype(F32))).astype(BF16)

    tile = lambda blk: pl.BlockSpec((tm, COL), lambda j, i: (i, blk + j))
    return pl.pallas_call(
        body, name="merge_bwd_branches", grid=(D // COL, s // tm),
        out_shape=[jax.ShapeDtypeStruct((s, D), BF16)] * 2,
        in_specs=[tile(0), tile(GA_BLK), tile(GC_BLK)], out_specs=[tile(0)] * 2,
        compiler_params=_params("parallel", "parallel"),
    )(dm, proj, proj)


def _merge_bwd_gates(dm, ya, yc, proj, tm=1024):
    s = proj.shape[0]
    half = D // COL

    def body(dm_ref, ya_ref, yc_ref, g_ref, o_ref):
        y = jnp.where(pl.program_id(0) < half, ya_ref[...].astype(F32), yc_ref[...].astype(F32))
        sig = _sigmoid(g_ref[...].astype(F32))
        o_ref[...] = (dm_ref[...] * y * sig * (1.0 - sig)).astype(BF16)

    chan = pl.BlockSpec((tm, COL), lambda jj, i: (i, jj % half))
    gate = pl.BlockSpec((tm, COL), lambda jj, i: (i, GA_BLK + jj))
    return pl.pallas_call(
        body, name="merge_bwd_gates", grid=(2 * half, s // tm),
        out_shape=jax.ShapeDtypeStruct((s, IN_W), BF16),
        in_specs=[chan, chan, chan, gate], out_specs=gate,
        compiler_params=_params("parallel", "parallel"),
    )(dm, ya, yc, proj)


def _mod_part(c_all, w_ada, b_part):
    def body(c_ref, w_ref, b_ref, o_ref):
        cv = c_ref[...]
        act = cv * _sigmoid(cv)
        o_ref[...] = jnp.dot(act, w_ref[...], preferred_element_type=F32,
                             precision=lax.Precision.HIGHEST) + b_ref[...]

    return pl.pallas_call(
        body, name="mod_part", out_shape=jax.ShapeDtypeStruct((N_DEV, w_ada.shape[1]), F32),
    )(c_all, w_ada, b_part)


def _w_ada_grad(c_all_t, dmod_part):
    def body(c_ref, d_ref, o_ref):
        cv = c_ref[...]
        act = cv * _sigmoid(cv)
        dv = d_ref[...]
        acc = act[:, 0:1] * dv[0:1, :]
        for b in range(1, N_DEV):
            acc = acc + act[:, b:b + 1] * dv[b:b + 1, :]
        o_ref[...] = acc

    return pl.pallas_call(
        body, name="w_ada_grad", out_shape=jax.ShapeDtypeStruct((D, dmod_part.shape[1]), F32),
    )(c_all_t, dmod_part)


def _sum_rows(name, v):
    def body(v_ref, o_ref):
        acc = v_ref[0]
        for k in range(1, N_DEV):
            acc = acc + v_ref[k]
        o_ref[...] = acc

    return pl.pallas_call(body, name=name, out_shape=jax.ShapeDtypeStruct(v.shape[1:], F32))(v)


def _adamw(name, w, g, m, v):
    rows, cols = w.shape
    limit = max(16, (1 << 20) // (4 * cols))
    tr = rows if rows <= limit else next((t for t in range(limit - limit % 16, 15, -16) if rows % t == 0), rows)
    c1 = 1.0 - ADAM_B1 ** ADAM_STEP
    c2 = 1.0 - ADAM_B2 ** ADAM_STEP
    parts = g.ndim == 3

    def body(w_ref, g_ref, m_ref, v_ref, go_ref, d_ref, nm_ref, nv_ref):
        if parts:
            gv = g_ref[0].astype(F32)
            for k in range(1, N_DEV):
                gv = gv + g_ref[k].astype(F32)
        else:
            gv = g_ref[...]
        go_ref[...] = gv
        nm = ADAM_B1 * m_ref[...] + (1.0 - ADAM_B1) * gv
        nv = ADAM_B2 * v_ref[...] + (1.0 - ADAM_B2) * (gv * gv)
        nm_ref[...] = nm
        nv_ref[...] = nv
        d_ref[...] = -ADAM_LR * ((nm / c1) / (jnp.sqrt(nv / c2) + ADAM_EPS) + ADAM_WD * w_ref[...])

    spec = pl.BlockSpec((tr, cols), lambda i: (i, 0))
    g_spec = pl.BlockSpec((N_DEV, tr, cols), lambda i: (0, i, 0)) if parts else spec
    return pl.pallas_call(
        body, name=name, grid=(rows // tr,),
        out_shape=[jax.ShapeDtypeStruct((rows, cols), F32)] * 4,
        in_specs=[spec, g_spec, spec, spec], out_specs=[spec] * 4,
        compiler_params=_params("parallel"),
    )(w, g, m, v)


def _adamw_small(ws, gs, ms, vs):
    n = len(ws)
    c1 = 1.0 - ADAM_B1 ** ADAM_STEP
    c2 = 1.0 - ADAM_B2 ** ADAM_STEP

    def body(*refs):
        for i in range(n):
            w_ref, g_ref, m_ref, v_ref = refs[i], refs[n + i], refs[2 * n + i], refs[3 * n + i]
            d_ref, nm_ref, nv_ref = refs[4 * n + 3 * i:4 * n + 3 * i + 3]
            gv = g_ref[...]
            nm = ADAM_B1 * m_ref[...] + (1.0 - ADAM_B1) * gv
            nv = ADAM_B2 * v_ref[...] + (1.0 - ADAM_B2) * (gv * gv)
            nm_ref[...] = nm
            nv_ref[...] = nv
            d_ref[...] = -ADAM_LR * ((nm / c1) / (jnp.sqrt(nv / c2) + ADAM_EPS) + ADAM_WD * w_ref[...])

    outs = pl.pallas_call(
        body, name="adamw_small",
        out_shape=[jax.ShapeDtypeStruct(w.shape, F32) for w in ws for _ in range(3)],
    )(*ws, *gs, *ms, *vs)
    return [tuple(outs[3 * i:3 * i + 3]) for i in range(n)]


HALF = FF // 2


def _sds(shape, dtype):
    return jax.ShapeDtypeStruct(shape, dtype)


def _row_tile(w):
    return lambda tm: ((tm, w), lambda i, j: (i, 0))


def _one(w):
    return lambda rows: ((rows, w), lambda i, j: (0, 0))


def _gate_up_swiglu(name, h, wgu, carry=None, tm=512):
    s = h.shape[0]
    tm = min(tm, s)

    def epilogue(prod, first, tin, tout):
        pq_ref, s_ref = tout
        a, b = prod[:, :HALF], prod[:, HALF:]
        sig = _sigmoid(a)
        act = a * sig
        pq_ref[:, :HALF] = (b * (sig * (1.0 + a * (1.0 - sig)))).astype(BF16)
        pq_ref[:, HALF:] = act.astype(BF16)
        s_ref[...] = (act * b).astype(BF16)

    return _mm(name, h, wgu, "NT", None, tm, FF, D, carry=carry, n_outer=True, epilogue=epilogue,
               tiles_out=[(_sds((s, 2 * FF), BF16), (tm, FF), lambda i, j: (i, j)),
                          (_sds((s, FF), BF16), (tm, HALF), lambda i, j: (i, j))])


def _d_hidden_swiglu(name, df, wd, ab, tm=512):
    s = df.shape[0]
    tm = min(tm, s)

    def epilogue(prod, first, tin, tout, cols):
        da_cols = slice(cols[0], cols[0] + cols[1])
        db_cols = slice(HALF + cols[0], HALF + cols[0] + cols[1])
        tout[0][:, da_cols] = (prod * tin[0][:, da_cols].astype(F32)).astype(BF16)
        tout[0][:, db_cols] = (prod * tin[0][:, db_cols].astype(F32)).astype(BF16)

    chunks = [(c0, min(384, HALF - c0)) for c0 in range(0, HALF, 384)]
    return _mm(name, df, wd, "NT", None, tm, HALF, D, n_outer=True, epilogue=epilogue, col_chunks=chunks,
               tiles_in=[(ab, (tm, FF), lambda i, j: (i, j))],
               tiles_out=[(_sds((s, 2 * FF), BF16), (tm, FF), lambda i, j: (i, j))])[0]


def _out_residual(name, a, w, x, gt, coef, nxt, tm=512, tk=FF):
    s = a.shape[0]
    tm = min(tm, s)

    def epilogue(prod, first, tin, tout):
        x_ref, gt_ref, g_ref, sc_ref, sh_ref = tin
        f_ref, xn_ref, h_ref = tout
        f_ref[...] = prod
        xn = x_ref[...] + (coef * gt_ref[...]) * prod
        xn_ref[...] = xn
        r = lax.rsqrt(jnp.mean(xn * xn, axis=-1, keepdims=True) + EPS)
        h_ref[...] = ((xn * r) * g_ref[...] * (1.0 + sc_ref[...]) + sh_ref[...]).astype(BF16)

    row, vec = _row_tile(D)(tm), _one(D)(1)
    return _mm(name, a, w, "NN", None, tm, D, tk, epilogue=epilogue,
               tiles_in=[(x, *row), (gt, *vec)] + [(v, *vec) for v in nxt],
               tiles_out=[(_sds((s, D), F32), *row), (_sds((s, D), F32), *row), (_sds((s, D), BF16), *row)])


def _out_loss(name, a, w, x, gt, coef, target, tm=512):
    s = a.shape[0]
    tm = min(tm, s)

    def epilogue(prod, first, tin, tout):
        x_ref, gt_ref, t_ref = tin
        f_ref, g_ref, df_ref, acc_ref = tout
        f_ref[...] = prod
        cg = coef * gt_ref[...]
        e = x_ref[...] + cg * prod - t_ref[...]
        gv = e * (1.0 / D)
        g_ref[...] = gv
        df_ref[...] = (cg * gv).astype(BF16)

        @pl.when(first)
        def _():
            acc_ref[...] = jnp.zeros_like(acc_ref)

        acc_ref[0:1, :] += coef * jnp.sum(gv * prod, axis=0, keepdims=True)
        acc_ref[1:2, :] += (0.5 / D) * jnp.sum(e * e, axis=0, keepdims=True)

    row, vec = _row_tile(D)(tm), _one(D)(1)
    return _mm(name, a, w, "NN", None, tm, D, FF, epilogue=epilogue,
               tiles_in=[(x, *row), (gt, *vec), (target, *row)],
               tiles_out=[(_sds((s, D), F32), *row), (_sds((s, D), F32), *row), (_sds((s, D), BF16), *row),
                          (_sds((8, D), F32), *_one(D)(8))])


def _d_h_norm_bwd(name, da, w, x, gin, g, sc, sh, before=None, carry=None, tm=256):
    s = da.shape[0]
    tm = min(tm, s)
    coef = before[2] if before else None

    def epilogue(prod, first, tin, tout):
        x_ref, gin_ref, g_ref, sc_ref, sh_ref = tin[:5]
        gout_ref, acc_ref = tout[:2]
        xv = x_ref[...]
        r = lax.rsqrt(jnp.mean(xv * xv, axis=-1, keepdims=True) + EPS)
        nv = xv * r
        gv, one_sc = g_ref[...], 1.0 + sc_ref[...]
        dn = prod * gv * one_sc
        gout = gin_ref[...] + r * (dn - nv * jnp.mean(dn * nv, axis=-1, keepdims=True))
        gout_ref[...] = gout

        @pl.when(first)
        def _():
            acc_ref[...] = jnp.zeros_like(acc_ref)

        dhn = prod * nv
        acc_ref[0:1, :] += jnp.sum(prod, axis=0, keepdims=True)
        acc_ref[1:2, :] += jnp.sum(dhn * gv, axis=0, keepdims=True)
        acc_ref[2:3, :] += jnp.sum(dhn * one_sc, axis=0, keepdims=True)
        if before:
            f_ref, gt_ref = tin[5:]
            tout[2][...] = ((coef * gt_ref[...]) * gout).astype(BF16)
            acc_ref[3:4, :] += coef * jnp.sum(gout * f_ref[...], axis=0, keepdims=True)

    row, vec = _row_tile(D)(tm), _one(D)(1)
    tiles_in = [(x, *row), (gin, *row), (g, *vec), (sc, *vec), (sh, *vec)]
    tiles_out = [(_sds((s, D), F32), *row), (_sds((8, D), F32), *_one(D)(8))]
    if before:
        tiles_in += [(before[0], *row), (before[1], *vec)]
        tiles_out.append((_sds((s, D), BF16), *row))
    return _mm(name, da, w, "NN", None, tm, D, da.shape[1], epilogue=epilogue, carry=carry, keep_b=True,
               tiles_in=tiles_in, tiles_out=tiles_out)


def _gate_tiles(proj, tm):
    return [(proj, (tm, COL), (lambda i, j, blk=blk: (i, blk))) for blk in (GA_BLK, GA_BLK + 1, GC_BLK, GC_BLK + 1)]


def _conv_branch_merge(z, wc, ya, proj, tm=512):
    s = z.shape[0]
    tm = min(tm, s)

    def epilogue(prod, first, tin, tout):
        ya_ref, ga0, ga1, gc0, gc1 = tin
        tout[0][...] = prod.astype(BF16)
        for half, (ga, gc) in enumerate(((ga0, gc0), (ga1, gc1))):
            cols = slice(half * COL, (half + 1) * COL)
            tout[1][:, cols] = (_sigmoid(ga[...].astype(F32)) * ya_ref[:, cols].astype(F32)
                                + _sigmoid(gc[...].astype(F32)) * prod[:, cols]).astype(BF16)

    row = _row_tile(D)(tm)
    return _mm("mix_conv_branch", z, wc, "NN", None, tm, D, D, epilogue=epilogue,
               tiles_in=[(ya, *row)] + _gate_tiles(proj, tm),
               tiles_out=[(_sds((s, D), BF16), *row), (_sds((s, D), BF16), *row)])


def _d_merged_branches(dmix, wo, proj, tm=512):
    s = dmix.shape[0]
    tm = min(tm, s)

    def epilogue(prod, first, tin, tout):
        ga0, ga1, gc0, gc1 = tin
        tout[0][...] = prod
        for half, (ga, gc) in enumerate(((ga0, gc0), (ga1, gc1))):
            cols = slice(half * COL, (half + 1) * COL)
            tout[1][:, cols] = (prod[:, cols] * _sigmoid(ga[...].astype(F32))).astype(BF16)
            tout[2][:, cols] = (prod[:, cols] * _sigmoid(gc[...].astype(F32))).astype(BF16)

    row = _row_tile(D)(tm)
    return _mm("mix_d_merged", dmix, wo, "NT", None, tm, D, D, epilogue=epilogue,
               tiles_in=_gate_tiles(proj, tm),
               tiles_out=[(_sds((s, D), F32), *row), (_sds((s, D), BF16), *row), (_sds((s, D), BF16), *row)])


def _d_o_delta(dya, wa_t, o, tm=1024):
    s = dya.shape[0]
    tm = min(tm, s)

    def epilogue(prod, first, tin, tout):
        tout[0][...] = prod
        tout[1][...] = _heads(prod * tin[0][...].astype(F32), lambda ph, h: jnp.broadcast_to(
            jnp.sum(ph, axis=-1, keepdims=True), ph.shape))

    row = _row_tile(COL)(tm)
    return _mm("mix_d_o", dya, wa_t, "NN", None, tm, COL, D, epilogue=epilogue,
               tiles_in=[(o, *row)], tiles_out=[(_sds((s, COL), F32), *row), (_sds((s, COL), F32), *row)])


def _ffn_bwd(tag, df, x, gin, h, ab, sw, g, sc, sh, wgu, wd, before=None, tk_dw=2048):
    dwd = _mm(f"{tag}_dw_down", sw, df, "TN", BF16, HALF, D, tk_dw)
    dab = _d_hidden_swiglu(f"{tag}_d_hidden", df, wd, ab)
    dwgu = _mm(f"{tag}_dw_gate_up", dab, h, "TN", BF16, HALF, D, tk_dw)
    res = _d_h_norm_bwd(f"{tag}_d_h", dab, wgu, x, gin, g, sc, sh, before=before)
    return res, dwgu, dwd


def kernel(x, c, w_ada, b_ada, norm_ffn1, ffn1_w_gate, ffn1_w_up, ffn1_w_down, norm_mix, w_in, q_norm, k_norm, conv_w, w_attn_branch, w_conv_branch, w_out, norm_ffn2, ffn2_w_gate, ffn2_w_up, ffn2_w_down, loss_target, m_w_ada, m_b_ada, m_norm_ffn1, m_ffn1_w_gate, m_ffn1_w_up, m_ffn1_w_down, m_norm_mix, m_w_in, m_q_norm, m_k_norm, m_conv_w, m_w_attn_branch, m_w_conv_branch, m_w_out, m_norm_ffn2, m_ffn2_w_gate, m_ffn2_w_up, m_ffn2_w_down, v_w_ada, v_b_ada, v_norm_ffn1, v_ffn1_w_gate, v_ffn1_w_up, v_ffn1_w_down, v_norm_mix, v_w_in, v_q_norm, v_k_norm, v_conv_w, v_w_attn_branch, v_w_conv_branch, v_w_out, v_norm_ffn2, v_ffn2_w_gate, v_ffn2_w_up, v_ffn2_w_down):
    me = 4 * lax.axis_index("x") + 2 * lax.axis_index("y") + lax.axis_index("c")
    x0, target = x[0], loss_target[0]
    s = x0.shape[0]
    ada_cols = w_ada.shape[2]
    cw_cols = conv_w.shape[2]

    gathered = _small_allgather(
        "gather_c_conv", jnp.concatenate([c, conv_w[0].reshape(1, 3 * cw_cols)], axis=1))[:, 0]
    c_all = gathered[:, :D]
    cw = gathered[:, D:].reshape(N_DEV, 3, cw_cols).transpose(1, 0, 2).reshape(3, D)
    b_part = lax.dynamic_slice(b_ada, (0, me * ada_cols), (1, ada_cols))
    mod_part = _mod_part(c_all, w_ada[0], b_part)
    mod_all = _small_allgather("gather_mod", mod_part.reshape(1, N_DEV * ada_cols))
    mod = lax.dynamic_slice(mod_all.reshape(N_DEV, N_DEV, ada_cols), (0, me, 0), (N_DEV, 1, ada_cols))
    mod = mod.reshape(N_MOD, 1, D)
    sh1, sc1, gt1, sh2, sc2, gt2, sh3, sc3, gt3 = [mod[i] for i in range(N_MOD)]

    tb = lambda w: w[0].T.astype(BF16)
    nb = lambda w: w[0].astype(BF16)
    ffn1_shards = [tb(ffn1_w_gate), tb(ffn1_w_up), nb(ffn1_w_down)]
    ffn2_shards = [tb(ffn2_w_gate), tb(ffn2_w_up), nb(ffn2_w_down)]
    mix_shards = [tb(w_in), tb(w_attn_branch), nb(w_conv_branch), nb(w_out)]
    ffn_dst, ffn_base, ffn_jump, ffn_shapes = [0, 0, 1], [0, HALF, 0], [HALF, HALF, 0], [(2 * FF, D), (FF, D)]
    mix_dst, mix_base, mix_shapes = [0, 1, 2, 3], [0, 0, 0, 0], [(IN_W, D), (D, COL), (D, D), (D, D)]
    (wgu1,) = _run_plan_on_sequencer(
        "gather_ffn1_gate_up", _gather_plan(ffn1_shards[:2], ffn_dst[:2], ffn_base[:2], ffn_shapes[:1], ffn_jump[:2]), 1)
    (wd1,) = _run_plan_on_sequencer(
        "gather_ffn1_down", _gather_plan(ffn1_shards[2:], [0], [0], ffn_shapes[1:]), 8)
    win_t, wa_t, wc, wo = _run_plan_on_sequencer(
        "gather_mix_weights", _gather_plan(mix_shards, mix_dst, mix_base, mix_shapes), 2)
    wgu2, wd2 = _run_plan_on_sequencer(
        "gather_ffn2_weights", _gather_plan(ffn2_shards, ffn_dst, ffn_base, ffn_shapes, ffn_jump), 3)

    h1 = _normmod("ffn1_normmod", x0, norm_ffn1, sc1, sh1)
    ab1, s1 = _gate_up_swiglu("ffn1_gate_up", h1, wgu1)
    f1, x1, h2 = _out_residual("ffn1_down", s1, wd1, x0, gt1, 0.5, (norm_mix, sc2, sh2))
    proj = _mm("mix_in_proj", h2, win_t, "NT", BF16, 1024, IN_W // 4, D, n_outer=True)
    wqk = jnp.concatenate([jnp.tile(q_norm, (1, 12)), jnp.tile(k_norm, (1, 12))], axis=1)
    qkn = _qknorm(proj, wqk)
    group_out = [_attn_fwd(g, qkn, proj) for g in range(3)]
    o, lse = _attn_combine([go[0] for go in group_out], [go[1] for go in group_out])
    ya = _mm("mix_attn_branch", o, wa_t, "NT", BF16, 1024, 1024, COL)
    z = _conv_fwd(proj, cw)
    yc, merged = _conv_branch_merge(z, wc, ya, proj)
    mix, x2, h3 = _out_residual("mix_out_proj", merged, wo, x1, gt2, 1.0, (norm_ffn2, sc3, sh3), tk=D)
    ab3, s3 = _gate_up_swiglu("ffn2_gate_up", h3, wgu2)
    f3, g3, df3, acc_out = _out_loss("ffn2_down", s3, wd2, x2, gt3, 0.5, target)
    loss_part = jnp.sum(acc_out[1])

    ffn_rows = [sh_.shape[0] for sh_ in ffn1_shards]
    mix_rows = [sh_.shape[0] for sh_ in mix_shards]
    (g2, acc3, dmix), dwgu2, dwd2 = _ffn_bwd(
        "ffn2", df3, x2, g3, h3, ab3, s3, norm_ffn2, sc3, sh3, wgu2, wd2, before=(mix, gt2, 1.0))
    dmerged, dya, dyc = _d_merged_branches(dmix, wo, proj)
    dwo = _mm("mix_dw_out", merged, dmix, "TN", BF16, 1024, 1024, 2048)
    dproj = _merge_bwd_gates(dmerged, ya, yc, proj)
    dwc = _mm("mix_dw_conv_branch", z, dyc, "TN", BF16, 1024, 1024, 2048)
    dz = _mm("mix_d_z", dyc, wc, "NT", F32, 1024, 1024, D)
    dproj, d_c, cw_acc = _conv_bwd(dz, proj, cw, dproj)
    dproj = _copy_columns("copy_d_c", d_c, dproj, C_BLK)
    dwa_t = _mm("mix_dw_attn_branch", dya, o, "TN", BF16, 1024, COL, 2048)
    do, delta = _d_o_delta(dya, wa_t, o)
    dqn = dkn = None
    for g in range(3):
        dqn, dkn, dproj = _attn_bwd(g, qkn, proj, do, lse, delta, dqn, dkn, dproj)
    dproj, wq_acc = _qknorm_bwd("qnorm_bwd", proj, dqn, wqk[:, :QKW // 2], dproj, 0)
    dproj, wk_acc = _qknorm_bwd("knorm_bwd", proj, dkn, wqk[:, QKW // 2:], dproj, QKW // 2 // COL)
    r_f2g, r_f2u, r_f2d, r_wa, r_wc, r_wo = _run_plan_on_sequencer(
        "scatter_ffn2_and_branch_grads",
        _scatter_plan([dwgu2, dwd2, dwa_t, dwc, dwo], [0, 0, 1, 2, 3, 4], [0, HALF, 0, 0, 0, 0],
                      ffn_rows + mix_rows[1:], [D, D, D, COL, D, D], [HALF, HALF, 0, 0, 0, 0]), 4)
    dwin_t = _mm("mix_dw_in", dproj, h2, "TN", BF16, IN_W // 4, COL, 2048)
    (r_win,) = _run_plan_on_sequencer(
        "scatter_w_in_grad", _scatter_plan([dwin_t], [0], [0], mix_rows[:1], [D]), 5)
    g1, acc2, df1 = _d_h_norm_bwd("mix_d_h", dproj, win_t, x1, g2, norm_mix, sc2, sh2, before=(f1, gt1, 0.5))
    (g0, acc1), dwgu1, dwd1 = _ffn_bwd(
        "ffn1", df1, x0, g1, h1, ab1, s1, norm_ffn1, sc1, sh1, wgu1, wd1)
    (r_f1d,) = _run_plan_on_sequencer(
        "scatter_ffn1_down_grad", _scatter_plan([dwd1], [0], [0], ffn_rows[2:], [D]), 6)
    r_f1g, r_f1u = _run_plan_on_sequencer(
        "scatter_ffn1_gate_up_grads",
        _scatter_plan([dwgu1], [0, 0], [0, HALF], ffn_rows[:2], [D, D], [HALF, HALF]), 7)

    dqw = jnp.sum(wq_acc[0].reshape(12, HD), axis=0)
    dkw = jnp.sum(wk_acc[0].reshape(12, HD), axis=0)
    small = jnp.concatenate([
        acc1[0], acc1[1], acc2[3], acc2[0], acc2[1], acc3[3], acc3[0], acc3[1], acc_out[0],
        acc1[2], acc2[2], acc3[2], dqw, dkw, cw_acc[0:3].reshape(3 * D),
        jnp.zeros((HD,), F32).at[0].set(loss_part)]).reshape(1, -1)
    small_all = _small_allgather("gather_small_grads", small)
    small_sum = _sum_rows("sum_small_grads", small_all)[0]
    n_mod = N_MOD * D
    g_b_ada = small_sum[:n_mod].reshape(1, n_mod)
    g_norm1, g_norm2, g_norm3 = [small_sum[n_mod + i * D:n_mod + (i + 1) * D].reshape(1, D) for i in range(3)]
    off = n_mod + 3 * D
    g_qn, g_kn = small_sum[off:off + HD].reshape(1, HD), small_sum[off + HD:off + 2 * HD].reshape(1, HD)
    g_cw_full = small_sum[off + 2 * HD:off + 2 * HD + 3 * D].reshape(3, D)
    loss = small_sum[off + 2 * HD + 3 * D]
    g_cw = lax.dynamic_slice(g_cw_full, (0, me * cw_cols), (3, cw_cols))
    dmod_part = lax.dynamic_slice(small_all[:, 0, :n_mod], (0, me * ada_cols), (N_DEV, ada_cols))
    g_w_ada = _w_ada_grad(c_all.T, dmod_part)

    as_rows = {"ffn1_w_gate", "ffn1_w_up", "w_in", "w_attn_branch", "ffn2_w_gate", "ffn2_w_up"}
    grad_list = [g_w_ada, g_b_ada, g_norm1, r_f1g, r_f1u, r_f1d, g_norm2, r_win,
                 g_qn, g_kn, g_cw, r_wa, r_wc, r_wo, g_norm3, r_f2g, r_f2u, r_f2d]
    weights = [w_ada, b_ada, norm_ffn1, ffn1_w_gate, ffn1_w_up, ffn1_w_down, norm_mix, w_in, q_norm, k_norm,
               conv_w, w_attn_branch, w_conv_branch, w_out, norm_ffn2, ffn2_w_gate, ffn2_w_up, ffn2_w_down]
    ms = [m_w_ada, m_b_ada, m_norm_ffn1, m_ffn1_w_gate, m_ffn1_w_up, m_ffn1_w_down, m_norm_mix, m_w_in, m_q_norm,
          m_k_norm, m_conv_w, m_w_attn_branch, m_w_conv_branch, m_w_out, m_norm_ffn2, m_ffn2_w_gate,
          m_ffn2_w_up, m_ffn2_w_down]
    vs = [v_w_ada, v_b_ada, v_norm_ffn1, v_ffn1_w_gate, v_ffn1_w_up, v_ffn1_w_down, v_norm_mix, v_w_in, v_q_norm,
          v_k_norm, v_conv_w, v_w_attn_branch, v_w_conv_branch, v_w_out, v_norm_ffn2, v_ffn2_w_gate,
          v_ffn2_w_up, v_ffn2_w_down]
    wnames = ["w_ada", "b_ada", "norm_ffn1", "ffn1_w_gate", "ffn1_w_up", "ffn1_w_down", "norm_mix", "w_in",
              "q_norm", "k_norm", "conv_w", "w_attn_branch", "w_conv_branch", "w_out", "norm_ffn2",
              "ffn2_w_gate", "ffn2_w_up", "ffn2_w_down"]
    small = [i for i, gr in enumerate(grad_list) if gr.ndim == 2 and gr.size <= 16384]
    flat = lambda a, i: a.reshape(-1, weights[i].shape[-1])
    small_res = dict(zip(small, _adamw_small(
        [flat(weights[i], i) for i in small], [flat(grad_list[i], i) for i in small],
        [flat(ms[i], i) for i in small], [flat(vs[i], i) for i in small])))
    grad_out, deltas, new_ms, new_vs = [], [], [], []
    for idx, (nm, w, gr, m_, v_) in enumerate(zip(wnames, weights, grad_list, ms, vs)):
        if idx in small_res:
            gr, dl, nm_, nv_ = [r.reshape(w.shape) for r in (gr, *small_res[idx])]
        elif nm in as_rows:
            res = _adamw(f"adamw_{nm}", w[0].T, gr, m_[0].T, v_[0].T)
            gr, dl, nm_, nv_ = [r.T[None] for r in res]
        else:
            two_d = (-1, w.shape[-1])
            res = _adamw(f"adamw_{nm}", w.reshape(two_d), gr if gr.ndim == 3 else gr.reshape(two_d),
                         m_.reshape(two_d), v_.reshape(two_d))
            gr, dl, nm_, nv_ = [r.reshape(w.shape) for r in res]
        grad_out.append(gr)
        deltas.append(dl)
        new_ms.append(nm_)
        new_vs.append(nv_)
    return (loss, g0[None], *grad_out, *deltas, *new_ms, *new_vs)
```

```python
import functools

import jax
import jax.numpy as jnp
from jax import lax
from jax.experimental import pallas as pl
from jax.experimental.pallas import tpu as pltpu
from jax.experimental.pallas import tpu_sc as plsc

F32 = jnp.float32
BF16 = jnp.bfloat16
MESH = pl.DeviceIdType.MESH

N_DEV = 8
D = 1024
FF = 2816
HD = 128
N_HEADS = 4
DILATIONS = (1, 4, 16)
BAND = 128
QKW = 2 * 3 * N_HEADS * HD
IN_W = 9728
COL = 512
V_BLK, U_BLK, B_BLK, C_BLK, GA_BLK, GC_BLK = 6, 9, 11, 13, 15, 17
EPS = 1e-6
N_MOD = 9
ADAM_LR, ADAM_B1, ADAM_B2, ADAM_EPS, ADAM_WD, ADAM_STEP = 0.001, 0.9, 0.999, 1e-08, 0.01, 10

NT_DIMS = (((1,), (1,)), ((), ()))
TN_DIMS = (((0,), (0,)), ((), ()))
NN_DIMS = (((1,), (0,)), ((), ()))


def _place():
    return lax.axis_index("x"), lax.axis_index("y"), lax.axis_index("c")


def _flip(coord, bit):
    return 1 - coord if bit else coord


def _params(*sem):
    return pltpu.CompilerParams(dimension_semantics=sem)


def _small_allgather(name, v):
    n = v.shape[-1]

    def body(v_ref, out_ref, send_sems, recv_sems):
        x, y, c = _place()
        me = 4 * x + 2 * y + c
        out_ref[me] = v_ref[...]
        copies = []
        for k in range(1, N_DEV):
            peer = (_flip(x, (k >> 2) & 1), _flip(y, (k >> 1) & 1), _flip(c, k & 1))
            cp = pltpu.make_async_remote_copy(
                src_ref=v_ref, dst_ref=out_ref.at[me], send_sem=send_sems.at[k - 1],
                recv_sem=recv_sems.at[k - 1], device_id=peer, device_id_type=MESH)
            cp.start()
            copies.append(cp)
        for cp in copies:
            cp.wait()

    return pl.pallas_call(
        body, name=name,
        out_shape=jax.ShapeDtypeStruct((N_DEV, 1, n), F32),
        in_specs=[pl.BlockSpec(memory_space=pltpu.VMEM)],
        out_specs=pl.BlockSpec(memory_space=pltpu.VMEM),
        scratch_shapes=[pltpu.SemaphoreType.DMA((N_DEV - 1,)), pltpu.SemaphoreType.DMA((N_DEV - 1,))],
    )(v)


class _Plan:
    def __init__(self, operands, out_shapes, sems, phases):
        self.operands, self.out_shapes, self.sems, self.phases = operands, out_shapes, sems, phases


def _slab_start(base, rows, jump, idx):
    return pl.multiple_of(base + idx * rows + (idx // 4) * jump, 16)


def _gather_plan(shards, dst_of, base_of, dst_shapes, jump_of=None):
    n = len(shards)
    rows = [s.shape[0] for s in shards]
    jump_of = jump_of or [0] * n

    def phases(srcs, dsts, sems):
        send_sems, recv_sems, local_sems = sems
        x, y, c = _place()
        me, sibling = (x, y, c), (x, y, 1 - c)
        chips = [(1 - x, y), (x, 1 - y), (1 - x, 1 - y)]

        def slab(i, px, py, pc):
            start = _slab_start(base_of[i], rows[i], jump_of[i], 4 * px + 2 * py + pc)
            return dsts[dst_of[i]].at[pl.ds(start, rows[i])]

        def copy(i, k, block, to, src=None):
            return pltpu.make_async_remote_copy(
                src_ref=slab(i, *block) if src is None else src, dst_ref=slab(i, *block),
                send_sem=send_sems.at[i, k], recv_sem=recv_sems.at[i, k],
                device_id=to, device_id_type=MESH)

        def mine():
            return [pltpu.make_async_copy(srcs[i], slab(i, *me), local_sems.at[i]) for i in range(n)]

        def first():
            out = []
            for i in range(n):
                out.append(copy(i, 0, me, sibling, src=srcs[i]))
                out += [copy(i, 1 + j, me, (*chip, c), src=srcs[i]) for j, chip in enumerate(chips)]
            return out

        def passed():
            return [(copy(i, 1 + j, (*chip, c), me), copy(i, 4 + j, (*chip, c), sibling))
                    for j, chip in enumerate(chips) for i in range(n)]

        def start():
            for cp in mine() + first():
                cp.start()

        def middle():
            for landed, onward in passed():
                landed.wait_recv()
                onward.start()

        def finish():
            for i in range(n):
                copy(i, 0, sibling, me).wait_recv()
                for j, chip in enumerate(chips):
                    copy(i, 4 + j, (*chip, 1 - c), me).wait_recv()
            for cp in first() + [onward for _, onward in passed()]:
                cp.wait_send()
            for cp in mine():
                cp.wait()

        return start, middle, finish

    sems = [pltpu.SemaphoreType.DMA((n, 7)), pltpu.SemaphoreType.DMA((n, 7)), pltpu.SemaphoreType.DMA((n,))]
    return _Plan(list(shards), [jax.ShapeDtypeStruct(s, BF16) for s in dst_shapes], sems, phases)


def _scatter_plan(grads, src_of, base_of, rows, cols, jump_of=None):
    n = len(rows)
    jump_of = jump_of or [0] * n

    def phases(srcs, recvs, sems):
        send_sems, recv_sems, local_sems = sems
        x, y, c = _place()
        me = 4 * x + 2 * y + c

        def slab(i, idx):
            start = _slab_start(base_of[i], rows[i], jump_of[i], idx)
            return srcs[src_of[i]].at[pl.ds(start, rows[i])]

        def copies():
            out = [pltpu.make_async_copy(slab(i, me), recvs[i].at[me], local_sems.at[i]) for i in range(n)]
            for k in range(1, N_DEV):
                px, py, pc = _flip(x, (k >> 2) & 1), _flip(y, (k >> 1) & 1), _flip(c, k & 1)
                out += [pltpu.make_async_remote_copy(
                    src_ref=slab(i, 4 * px + 2 * py + pc), dst_ref=recvs[i].at[me],
                    send_sem=send_sems.at[i, k - 1], recv_sem=recv_sems.at[i, k - 1],
                    device_id=(px, py, pc), device_id_type=MESH) for i in range(n)]
            return out

        def start():
            for cp in copies():
                cp.start()

        def finish():
            for cp in copies():
                cp.wait()

        return start, None, finish

    sems = [pltpu.SemaphoreType.DMA((n, 7)), pltpu.SemaphoreType.DMA((n, 7)), pltpu.SemaphoreType.DMA((n,))]
    out_shapes = [jax.ShapeDtypeStruct((N_DEV, rows[i], cols[i]), BF16) for i in range(n)]
    return _Plan(list(grads), out_shapes, sems, phases)


def _run_plan(name, plan):
    n_in, n_out = len(plan.operands), len(plan.out_shapes)

    def body(*refs):
        for phase in plan.phases(refs[:n_in], refs[n_in:n_in + n_out], refs[n_in + n_out:]):
            if phase is not None:
                phase()

    hbm = pl.BlockSpec(memory_space=pltpu.HBM)
    return pl.pallas_call(
        body, name=name, out_shape=plan.out_shapes,
        in_specs=[hbm] * n_in, out_specs=[hbm] * n_out, scratch_shapes=plan.sems,
    )(*plan.operands)


def _run_plan_on_sequencer(name, plan, collective_id):
    src_refs = [jax.new_ref(a, memory_space=pltpu.MemorySpace.HBM) for a in plan.operands]
    dst_refs = [jax.empty_ref(s, memory_space=pltpu.MemorySpace.HBM) for s in plan.out_shapes]

    @pl.kernel(mesh=plsc.ScalarSubcoreMesh(axis_name="sequencer", num_cores=1), name=name,
               scratch_types=tuple(plan.sems),
               compiler_params=pltpu.CompilerParams(collective_id=collective_id))
    def launch(*sems):
        x, y, c = _place()
        barrier = pltpu.get_barrier_semaphore()
        for k in range(1, N_DEV):
            peer = (_flip(x, (k >> 2) & 1), _flip(y, (k >> 1) & 1), _flip(c, k & 1))
            pl.semaphore_signal(barrier, inc=1, device_id=peer, device_id_type=MESH)
        pl.semaphore_wait(barrier, N_DEV - 1)
        for phase in plan.phases(src_refs, dst_refs, sems):
            if phase is not None:
                phase()

    launch()
    return [r[...] for r in dst_refs]


def _sum_contributions(name, recv):
    _, rows, cols = recv.shape
    tr = rows if rows <= 512 else 304 if rows % 304 == 0 else 256

    def body(r_ref, o_ref):
        acc = r_ref[0].astype(F32)
        for k in range(1, N_DEV):
            acc = acc + r_ref[k].astype(F32)
        o_ref[...] = acc

    return pl.pallas_call(
        body, name=name, grid=(rows // tr,),
        out_shape=jax.ShapeDtypeStruct((rows, cols), F32),
        in_specs=[pl.BlockSpec((N_DEV, tr, cols), lambda i: (0, i, 0))],
        out_specs=pl.BlockSpec((tr, cols), lambda i: (i, 0)),
        compiler_params=_params("parallel"),
    )(recv)


def _mm(name, a, b, mode, out_dtype, tm, tn, tk, *, carry=None, tiles_in=(), tiles_out=(), epilogue=None,
        n_outer=False, keep_b=False, col_chunks=None):
    if mode == "TN":
        kk, m = a.shape
    else:
        m, kk = a.shape
    n = b.shape[0] if mode == "NT" else b.shape[1]
    tm, tn, tk = min(tm, m), min(tn, n), min(tk, kk)
    assert m % tm == 0 and n % tn == 0 and kk % tk == 0, (name, m, n, kk, tm, tn, tk)
    ni, nj, nk = m // tm, n // tn, kk // tk
    steps = ni * nj * nk
    dims = {"NN": NN_DIMS, "NT": NT_DIMS, "TN": TN_DIMS}[mode]
    if epilogue is None:
        tiles_out = [(jax.ShapeDtypeStruct((m, n), out_dtype), (tm, tn), lambda i, j: (i, j))]
    n_tin, n_tout = len(tiles_in), len(tiles_out)
    n_in = len(carry.operands) if carry else 0
    n_out = len(carry.out_shapes) if carry else 0
    n_acc = 1 if nk > 1 else 0
    n_keep = 2 if keep_b else 0
    assert not carry or steps >= 3
    assert not keep_b or (nk == 1 and nj == 1)
    assert not col_chunks or (epilogue is not None and nk == 1 and mode != "TN")
    ij = (lambda p, q: (q, p)) if n_outer else (lambda p, q: (p, q))
    inner = ni if n_outer else nj

    def body(a_ref, b_ref, *rest):
        tin = rest[:n_tin]
        cin = rest[n_tin:n_tin + n_in]
        tout = rest[n_tin + n_in:n_tin + n_in + n_tout]
        cout = rest[n_tin + n_in + n_tout:n_tin + n_in + n_tout + n_out]
        scratch = rest[n_tin + n_in + n_tout + n_out:]
        k = pl.program_id(2)
        visit = pl.program_id(0) * inner + pl.program_id(1)
        step = visit * nk + k
        if keep_b:
            b_kept, b_sem = scratch[n_acc:n_acc + 2]

            @pl.when(step == 0)
            def _():
                cp = pltpu.make_async_copy(b_ref, b_kept, b_sem)
                cp.start()
                cp.wait()

            b_ref = b_kept
        if carry:
            start, middle, finish = carry.phases(cin, cout, scratch[n_acc + n_keep:])
            pl.when(step == 0)(start)

        def store(prod, c=0, cols=()):
            if epilogue is None:
                tout[0][...] = prod.astype(out_dtype)
            else:
                epilogue(prod, jnp.logical_and(visit == 0, c == 0), tin, tout, *cols)

        if col_chunks:
            for c, (c0, cw) in enumerate(col_chunks):
                b_part = b_ref[pl.ds(c0, cw), :] if mode == "NT" else b_ref[:, pl.ds(c0, cw)]
                store(lax.dot_general(a_ref[...], b_part, dims, preferred_element_type=F32), c, ((c0, cw),))
        else:
            part = lax.dot_general(a_ref[...], b_ref[...], dims, preferred_element_type=F32)
            if nk == 1:
                store(part)
            else:
                acc_ref = scratch[0]

                @pl.when(k == 0)
                def _():
                    acc_ref[...] = part

                @pl.when((k > 0) & (k < nk - 1))
                def _():
                    acc_ref[...] += part

                @pl.when(k == nk - 1)
                def _():
                    store(acc_ref[...] + part)

        if carry:
            if middle is not None:
                pl.when(step == (steps * 3) // 5)(middle)
            pl.when(step == steps - 1)(finish)

    def spec(shape, fn):
        return pl.BlockSpec(shape, lambda p, q, k: fn(*ij(p, q)))

    a_spec = (pl.BlockSpec((tk, tm), lambda p, q, k: (k, ij(p, q)[0])) if mode == "TN"
              else pl.BlockSpec((tm, tk), lambda p, q, k: (ij(p, q)[0], k)))
    if keep_b:
        b_spec = pl.BlockSpec(memory_space=pl.ANY)
    elif mode == "NT":
        b_spec = pl.BlockSpec((tn, tk), lambda p, q, k: (ij(p, q)[1], k))
    else:
        b_spec = pl.BlockSpec((tk, tn), lambda p, q, k: (k, ij(p, q)[1]))
    hbm = pl.BlockSpec(memory_space=pltpu.HBM)
    sequential = carry or epilogue or keep_b
    out = pl.pallas_call(
        body, name=name, grid=(nj, ni, nk) if n_outer else (ni, nj, nk),
        out_shape=[t[0] for t in tiles_out] + (carry.out_shapes if carry else []),
        in_specs=[a_spec, b_spec] + [spec(t[1], t[2]) for t in tiles_in] + [hbm] * n_in,
        out_specs=[spec(t[1], t[2]) for t in tiles_out] + [hbm] * n_out,
        scratch_shapes=([pltpu.VMEM((tm, tn), F32)] * n_acc
                        + ([pltpu.VMEM(b.shape, b.dtype), pltpu.SemaphoreType.DMA] if keep_b else [])
                        + (carry.sems if carry else [])),
        compiler_params=(_params("arbitrary", "arbitrary", "arbitrary") if sequential
                         else _params("parallel", "parallel", "arbitrary")),
    )(a, b, *[t[0] for t in tiles_in], *(carry.operands if carry else []))
    return out if (carry or epilogue) else out[0]


def _row(tm, w, off=0):
    return pl.BlockSpec((tm, w), lambda i: (i, off))


def _vec(w):
    return pl.BlockSpec((1, w), lambda i: (0, 0))


def _sigmoid(x):
    return 0.5 * jnp.tanh(0.5 * x) + 0.5


def _normmod(name, x, g, sc, sh, tm=512):
    s = x.shape[0]

    def body(x_ref, g_ref, sc_ref, sh_ref, h_ref):
        xv = x_ref[...]
        r = lax.rsqrt(jnp.mean(xv * xv, axis=-1, keepdims=True) + EPS)
        h_ref[...] = ((xv * r) * g_ref[...] * (1.0 + sc_ref[...]) + sh_ref[...]).astype(BF16)

    return pl.pallas_call(
        body, name=name, grid=(s // tm,),
        out_shape=jax.ShapeDtypeStruct((s, D), BF16),
        in_specs=[_row(tm, D), _vec(D), _vec(D), _vec(D)], out_specs=_row(tm, D),
        compiler_params=_params("parallel"),
    )(x, g, sc, sh)


def _normmod_bwd(name, dh, x, gin, g, sc, sh, tm=512):
    s = x.shape[0]

    def body(dh_ref, x_ref, gin_ref, g_ref, sc_ref, sh_ref, gout_ref, acc_ref):
        xv, dhv = x_ref[...], dh_ref[...]
        r = lax.rsqrt(jnp.mean(xv * xv, axis=-1, keepdims=True) + EPS)
        nv = xv * r
        gv, one_sc = g_ref[...], 1.0 + sc_ref[...]
        dn = dhv * gv * one_sc
        dx = r * (dn - nv * jnp.mean(dn * nv, axis=-1, keepdims=True))
        gout_ref[...] = gin_ref[...] + dx

        @pl.when(pl.program_id(0) == 0)
        def _():
            acc_ref[...] = jnp.zeros_like(acc_ref)

        dhn = dhv * nv
        acc_ref[0:1, :] += jnp.sum(dhv, axis=0, keepdims=True)
        acc_ref[1:2, :] += jnp.sum(dhn * gv, axis=0, keepdims=True)
        acc_ref[2:3, :] += jnp.sum(dhn * one_sc, axis=0, keepdims=True)

    return pl.pallas_call(
        body, name=name, grid=(s // tm,),
        out_shape=[jax.ShapeDtypeStruct((s, D), F32), jax.ShapeDtypeStruct((8, D), F32)],
        in_specs=[_row(tm, D), _row(tm, D), _row(tm, D), _vec(D), _vec(D), _vec(D)],
        out_specs=[_row(tm, D), pl.BlockSpec((8, D), lambda i: (0, 0))],
        compiler_params=_params("arbitrary"),
    )(dh, x, gin, g, sc, sh)


def _swiglu(name, ab, tm=512):
    s = ab.shape[0]

    def body(ab_ref, s_ref):
        a = ab_ref[:, :FF].astype(F32)
        b = ab_ref[:, FF:].astype(F32)
        s_ref[...] = (a * _sigmoid(a) * b).astype(BF16)

    return pl.pallas_call(
        body, name=name, grid=(s // tm,),
        out_shape=jax.ShapeDtypeStruct((s, FF), BF16),
        in_specs=[_row(tm, 2 * FF)], out_specs=_row(tm, FF),
        compiler_params=_params("parallel"),
    )(ab)


def _swiglu_bwd(name, ds, ab, tm=256):
    s = ab.shape[0]

    def body(ds_ref, ab_ref, dab_ref):
        a = ab_ref[:, :FF].astype(F32)
        b = ab_ref[:, FF:].astype(F32)
        dsv = ds_ref[...].astype(F32)
        sig = _sigmoid(a)
        dab_ref[:, :FF] = (dsv * b * (sig * (1.0 + a * (1.0 - sig)))).astype(BF16)
        dab_ref[:, FF:] = (dsv * (a * sig)).astype(BF16)

    return pl.pallas_call(
        body, name=name, grid=(s // tm,),
        out_shape=jax.ShapeDtypeStruct((s, 2 * FF), BF16),
        in_specs=[_row(tm, FF), _row(tm, 2 * FF)], out_specs=_row(tm, 2 * FF),
        compiler_params=_params("parallel"),
    )(ds, ab)


def _residual(name, x, f, gt, coef, tm=512):
    s = x.shape[0]

    def body(x_ref, f_ref, gt_ref, o_ref):
        o_ref[...] = x_ref[...] + (coef * gt_ref[...]) * f_ref[...]

    return pl.pallas_call(
        body, name=name, grid=(s // tm,),
        out_shape=jax.ShapeDtypeStruct((s, D), F32),
        in_specs=[_row(tm, D), _row(tm, D), _vec(D)], out_specs=_row(tm, D),
        compiler_params=_params("parallel"),
    )(x, f, gt)


def _gate_bwd(name, gin, f, gt, coef, tm=512):
    s = gin.shape[0]

    def body(g_ref, f_ref, gt_ref, df_ref, acc_ref):
        gv = g_ref[...]
        df_ref[...] = ((coef * gt_ref[...]) * gv).astype(BF16)

        @pl.when(pl.program_id(0) == 0)
        def _():
            acc_ref[...] = jnp.zeros_like(acc_ref)

        acc_ref[0:1, :] += coef * jnp.sum(gv * f_ref[...], axis=0, keepdims=True)

    return pl.pallas_call(
        body, name=name, grid=(s // tm,),
        out_shape=[jax.ShapeDtypeStruct((s, D), BF16), jax.ShapeDtypeStruct((8, D), F32)],
        in_specs=[_row(tm, D), _row(tm, D), _vec(D)],
        out_specs=[_row(tm, D), pl.BlockSpec((8, D), lambda i: (0, 0))],
        compiler_params=_params("arbitrary"),
    )(gin, f, gt)


def _loss_grad(x3, target, tm=512):
    s = x3.shape[0]

    def body(y_ref, t_ref, g_ref, l_ref):
        e = y_ref[...] - t_ref[...]
        g_ref[...] = e * (1.0 / D)

        @pl.when(pl.program_id(0) == 0)
        def _():
            l_ref[...] = jnp.zeros_like(l_ref)

        l_ref[...] += jnp.sum(jnp.mean(e * e, axis=-1, keepdims=True), axis=0, keepdims=True) * 0.5

    return pl.pallas_call(
        body, name="loss_grad", grid=(s // tm,),
        out_shape=[jax.ShapeDtypeStruct((s, D), F32), jax.ShapeDtypeStruct((8, 128), F32)],
        in_specs=[_row(tm, D), _row(tm, D)],
        out_specs=[_row(tm, D), pl.BlockSpec((8, 128), lambda i: (0, 0))],
        compiler_params=_params("arbitrary"),
    )(x3, target)


def _heads(x, fn):
    return jnp.concatenate([fn(x[:, h * HD:(h + 1) * HD], h) for h in range(COL // HD)], axis=1)


def _qknorm(proj, wqk, tm=1024):
    s = proj.shape[0]

    def body(p_ref, w_ref, o_ref):
        pv = p_ref[...].astype(F32)
        wv = w_ref[...]

        def one(qh, h):
            r = lax.rsqrt(jnp.mean(qh * qh, axis=-1, keepdims=True) + EPS)
            return (qh * r) * wv[:, h * HD:(h + 1) * HD]

        o_ref[...] = _heads(pv, one).astype(BF16)

    return pl.pallas_call(
        body, name="qknorm", grid=(s // tm, QKW // COL),
        out_shape=jax.ShapeDtypeStruct((s, QKW), BF16),
        in_specs=[pl.BlockSpec((tm, COL), lambda i, j: (i, j)), pl.BlockSpec((1, COL), lambda i, j: (0, j))],
        out_specs=pl.BlockSpec((tm, COL), lambda i, j: (i, j)),
        compiler_params=_params("parallel", "parallel"),
    )(proj, wqk)


def _qknorm_bwd(name, proj, dn, w, dproj, blk0, tm=1024):
    s = proj.shape[0]
    nblk = dn.shape[1] // COL

    def body(p_ref, d_ref, w_ref, _, o_ref, acc_ref):
        pv = p_ref[...].astype(F32)
        dv = d_ref[...]
        wv = w_ref[...]
        sums = []

        def one(qh, h):
            dn = dv[:, h * HD:(h + 1) * HD]
            r = lax.rsqrt(jnp.mean(qh * qh, axis=-1, keepdims=True) + EPS)
            nh = qh * r
            sums.append(jnp.sum(dn * nh, axis=0, keepdims=True))
            dnw = dn * wv[:, h * HD:(h + 1) * HD]
            return r * (dnw - nh * jnp.mean(dnw * nh, axis=-1, keepdims=True))

        o_ref[...] = _heads(pv, one).astype(BF16)

        @pl.when(pl.program_id(1) == 0)
        def _():
            acc_ref[...] = jnp.zeros_like(acc_ref)

        acc_ref[0:1, :] += jnp.concatenate(sums, axis=1)

    return pl.pallas_call(
        body, name=name, grid=(nblk, s // tm),
        out_shape=[jax.ShapeDtypeStruct((s, IN_W), BF16), jax.ShapeDtypeStruct((8, nblk * COL), F32)],
        in_specs=[pl.BlockSpec((tm, COL), lambda j, i: (i, blk0 + j)), pl.BlockSpec((tm, COL), lambda j, i: (i, j)),
                  pl.BlockSpec((1, COL), lambda j, i: (0, j)), pl.BlockSpec(memory_space=pl.ANY)],
        out_specs=[pl.BlockSpec((tm, COL), lambda j, i: (i, blk0 + j)),
                   pl.BlockSpec((8, COL), lambda j, i: (0, j))],
        input_output_aliases={3: 0},
        compiler_params=_params("arbitrary", "arbitrary"),
    )(proj, dn, w, dproj)


def _attn_shapes(s, g):
    d = DILATIONS[g]
    tb = min(s, max(2048, 256 * d))
    sb = min(256, tb // d)
    pb = BAND * d
    assert s % tb == 0 and tb % pb == 0 and (tb // d) % sb == 0 and sb % BAND == 0
    return d, tb, sb, pb


def _lanes(x, width):
    return jnp.concatenate([x] * (width // HD), axis=1)


def _every(start, size, d):
    return pl.ds(start, size, stride=d) if d > 1 else pl.ds(start, size)


def _attn_specs(g, tb, pb, s, ahead):
    ratio = tb // pb
    if ahead:
        nbr = lambda n: jnp.minimum((n + 1) * ratio, s // pb - 1)
    else:
        nbr = lambda n: jnp.maximum(n * ratio - 1, 0)
    cur = lambda base: pl.BlockSpec((tb, HD), lambda h, n: (n, base + g * N_HEADS + h))
    side = lambda base: pl.BlockSpec((pb, HD), lambda h, n: (nbr(n), base + g * N_HEADS + h))
    tok = pl.BlockSpec((tb, HD), lambda h, n: (n, h))
    tok_side = pl.BlockSpec((pb, HD), lambda h, n: (nbr(n), h))
    return cur, side, tok, tok_side


Q_COL, K_COL, V_COL = 0, 12, 24


def _attn_fwd(g, qkn, proj):
    s = qkn.shape[0]
    d, tb, sb, pb = _attn_shapes(s, g)
    ft = F32 if d > 1 else BF16
    nj = tb // d // sb
    scale = HD ** -0.5

    def body(q_ref, kc_ref, kp_ref, vc_ref, vp_ref, o_ref, lse_ref, qf, kf, vf):
        n = pl.program_id(1)
        qf[...] = q_ref[...].astype(ft)
        kf[0:pb] = kp_ref[...].astype(ft)
        kf[pb:] = kc_ref[...].astype(ft)
        vf[0:pb] = vp_ref[...].astype(ft)
        vf[pb:] = vc_ref[...].astype(ft)
        for r in range(d):
            for j in range(nj):
                at = j * sb * d + r
                q = qf[_every(at, sb, d), :].astype(BF16)
                k = kf[_every(at, sb + BAND, d), :].astype(BF16)
                v = vf[_every(at, sb + BAND, d), :].astype(BF16)
                sc = lax.dot_general(q, k, NT_DIMS, preferred_element_type=F32) * scale
                qi = lax.broadcasted_iota(jnp.int32, sc.shape, 0)
                kj = lax.broadcasted_iota(jnp.int32, sc.shape, 1)
                valid = (kj >= qi) & (kj <= qi + BAND)
                if j == 0:
                    valid = valid & ((kj >= BAND) | (n > 0))
                sc = jnp.where(valid, sc, -1e30)
                m = jnp.max(sc, axis=-1, keepdims=True)
                p = jnp.exp(sc - m)
                l = jnp.sum(p, axis=-1, keepdims=True)
                o = lax.dot_general(p.astype(BF16), v, NN_DIMS, preferred_element_type=F32)
                o_ref[_every(at, sb, d), :] = o / l
                lse_ref[_every(at, sb, d), :] = jnp.broadcast_to(m + jnp.log(l), (sb, HD))

    cur, side, tok, _ = _attn_specs(g, tb, pb, s, ahead=False)
    return pl.pallas_call(
        body, name=f"attn_fwd_g{g}", grid=(N_HEADS, s // tb),
        out_shape=[jax.ShapeDtypeStruct((s, COL), F32)] * 2,
        in_specs=[cur(Q_COL), cur(K_COL), side(K_COL), cur(V_COL), side(V_COL)],
        out_specs=[tok, tok],
        scratch_shapes=[pltpu.VMEM((tb, HD), ft), pltpu.VMEM((tb + pb, HD), ft),
                        pltpu.VMEM((tb + pb, HD), ft)],
        compiler_params=_params("parallel", "arbitrary"),
    )(qkn, qkn, qkn, proj, proj)


def _attn_combine(os_, lses, tm=512):
    s = os_[0].shape[0]

    def body(o0, o1, o2, l0, l1, l2, o_ref, lse_ref):
        a, b, c = l0[...], l1[...], l2[...]
        m = jnp.maximum(jnp.maximum(a, b), c)
        ea, eb, ec = jnp.exp(a - m), jnp.exp(b - m), jnp.exp(c - m)
        tot = ea + eb + ec
        o_ref[...] = ((ea * o0[...] + eb * o1[...] + ec * o2[...]) / tot).astype(BF16)
        lse_ref[...] = m + jnp.log(tot)

    return pl.pallas_call(
        body, name="attn_combine", grid=(s // tm,),
        out_shape=[jax.ShapeDtypeStruct((s, COL), BF16), jax.ShapeDtypeStruct((s, COL), F32)],
        in_specs=[_row(tm, COL)] * 6, out_specs=[_row(tm, COL)] * 2,
        compiler_params=_params("parallel"),
    )(*os_, *lses)


def _attn_delta(do, o, tm=512):
    s = do.shape[0]

    def body(do_ref, o_ref, del_ref):
        prod = do_ref[...] * o_ref[...].astype(F32)
        del_ref[...] = _heads(prod, lambda ph, h: jnp.broadcast_to(
            jnp.sum(ph, axis=-1, keepdims=True), ph.shape))

    return pl.pallas_call(
        body, name="attn_delta", grid=(s // tm,),
        out_shape=jax.ShapeDtypeStruct((s, COL), F32),
        in_specs=[_row(tm, COL)] * 2, out_specs=_row(tm, COL),
        compiler_params=_params("parallel"),
    )(do, o)


def _attn_bwd(g, qkn, proj, do, lse, delta, dqn, dkn, dproj):
    s = qkn.shape[0]
    d, tb, sb, pb = _attn_shapes(s, g)
    ft = F32 if d > 1 else BF16
    nj = tb // d // sb
    nt = s // tb
    scale = HD ** -0.5
    chained = dqn is not None

    def body(k_ref, v_ref, qc_ref, qn_ref, doc_ref, don_ref, lc_ref, ln_ref, dc_ref, dn_ref, *rest):
        dq_ref, dk_ref, dv_ref, kf, vf, qf, dvf, later = rest[-8:]
        n = pl.program_id(1)
        kf[...] = k_ref[...].astype(ft)
        vf[...] = v_ref[...].astype(ft)
        qf[0:tb] = qc_ref[...].astype(ft)
        qf[tb:] = qn_ref[...].astype(ft)

        @pl.when(n == 0)
        def _():
            later[...] = jnp.zeros_like(later)

        def window(c_ref, n_ref, r, j):
            at = j * sb * d + r
            if j < nj - 1:
                return c_ref[_every(at, sb + BAND, d), :]
            return jnp.concatenate([c_ref[_every(at, sb, d), :], n_ref[_every(r, BAND, d), :]], axis=0)

        for r in range(d):
            tail = later[r]
            for j in range(nj):
                at = j * sb * d + r
                rows = _every(at, sb, d)
                k = kf[rows, :].astype(BF16)
                v = vf[rows, :].astype(BF16)
                q = qf[_every(at, sb + BAND, d), :].astype(BF16)
                dov = window(doc_ref, don_ref, r, j).astype(BF16)
                sc = lax.dot_general(q, k, NT_DIMS, preferred_element_type=F32) * scale
                qi = lax.broadcasted_iota(jnp.int32, sc.shape, 0)
                kj = lax.broadcasted_iota(jnp.int32, sc.shape, 1)
                valid = (qi >= kj) & (qi <= kj + BAND)
                if j == nj - 1:
                    valid = valid & ((qi < sb) | (n < nt - 1))
                p = jnp.exp(jnp.where(valid, sc - _lanes(window(lc_ref, ln_ref, r, j), sb), -1e30))
                dp = lax.dot_general(dov, v, NT_DIMS, preferred_element_type=F32)
                ds = (p * (dp - _lanes(window(dc_ref, dn_ref, r, j), sb)) * scale).astype(BF16)
                dvf[rows, :] = lax.dot_general(p.astype(BF16), dov, TN_DIMS, preferred_element_type=F32)
                dk_ref[rows, :] = lax.dot_general(ds, q, TN_DIMS, preferred_element_type=F32)
                dqw = lax.dot_general(ds, k, NN_DIMS, preferred_element_type=F32)
                first = dqw[:BAND] + tail
                dq_ref[rows, :] = first if sb == BAND else jnp.concatenate([first, dqw[BAND:sb]], axis=0)
                tail = dqw[sb:]
            later[r] = tail
        dv_ref[...] = dvf[...].astype(BF16)

    cur, side, tok, tok_side = _attn_specs(g, tb, pb, s, ahead=True)
    anyspec = pl.BlockSpec(memory_space=pl.ANY)
    n_heads_cols = 3 * N_HEADS * HD
    return pl.pallas_call(
        body, name=f"attn_bwd_g{g}", grid=(N_HEADS, nt),
        out_shape=[jax.ShapeDtypeStruct((s, n_heads_cols), F32), jax.ShapeDtypeStruct((s, n_heads_cols), F32),
                   jax.ShapeDtypeStruct((s, IN_W), BF16)],
        in_specs=[cur(K_COL), cur(V_COL), cur(Q_COL), side(Q_COL), tok, tok_side, tok, tok_side,
                  tok, tok_side] + ([anyspec, anyspec] if chained else []) + [anyspec],
        out_specs=[cur(0), cur(0), cur(V_COL)],
        input_output_aliases={10: 0, 11: 1, 12: 2} if chained else {10: 2},
        scratch_shapes=[pltpu.VMEM((tb, HD), ft), pltpu.VMEM((tb, HD), ft),
                        pltpu.VMEM((tb + pb, HD), ft), pltpu.VMEM((tb, HD), F32),
                        pltpu.VMEM((d, BAND, HD), F32)],
        compiler_params=_params("arbitrary", "arbitrary"),
    )(qkn, proj, qkn, qkn, do, do, lse, lse, delta, delta, *([dqn, dkn] if chained else []), dproj)


def _shift_down(x, before, k):
    rolled = pltpu.roll(x, k, 0)
    head = jnp.where(lax.broadcasted_iota(jnp.int32, before.shape, 0) < k, pltpu.roll(before, k, 0), rolled[:8])
    return jnp.concatenate([head, rolled[8:]], axis=0)


def _shift_up(x, after, k):
    rows = x.shape[0]
    rolled = pltpu.roll(x, rows - k, 0)
    tail = jnp.where(lax.broadcasted_iota(jnp.int32, after.shape, 0) >= 8 - k,
                     pltpu.roll(after, 8 - k, 0), rolled[rows - 8:])
    return jnp.concatenate([rolled[:rows - 8], tail], axis=0)


def _conv_fwd(proj, cw, tm=1024):
    s = proj.shape[0]
    r16 = tm // 16

    def body(u_ref, b_ref, c_ref, up_ref, cp_ref, w_ref, z_ref):
        i = pl.program_id(1)
        xc = c_ref[...].astype(F32) * u_ref[...].astype(F32)
        xp = jnp.where(i > 0, cp_ref[8:16, :].astype(F32) * up_ref[8:16, :].astype(F32), 0.0)
        w = w_ref[...]
        conv = _shift_down(xc, xp, 2) * w[0:1] + _shift_down(xc, xp, 1) * w[1:2] + xc * w[2:3]
        z_ref[...] = (b_ref[...].astype(F32) * conv).astype(BF16)

    tile = lambda blk: pl.BlockSpec((tm, COL), lambda j, i: (i, blk + j))
    before = lambda blk: pl.BlockSpec((16, COL), lambda j, i: (jnp.maximum(i * r16 - 1, 0), blk + j))
    return pl.pallas_call(
        body, name="conv_fwd", grid=(D // COL, s // tm),
        out_shape=jax.ShapeDtypeStruct((s, D), BF16),
        in_specs=[tile(U_BLK), tile(B_BLK), tile(C_BLK), before(U_BLK), before(C_BLK),
                  pl.BlockSpec((3, COL), lambda j, i: (0, j))],
        out_specs=pl.BlockSpec((tm, COL), lambda j, i: (i, j)),
        compiler_params=_params("parallel", "parallel"),
    )(proj, proj, proj, proj, proj, cw)


def _conv_bwd(dz, proj, cw, dproj, tm=1024):
    s = proj.shape[0]
    r16 = tm // 16
    nrow = s // tm

    def body(dz_ref, u_ref, b_ref, c_ref, up_ref, cp_ref, dzn_ref, bn_ref, w_ref, _, o_ref, dc_ref, acc_ref):
        piece, i = pl.program_id(1), pl.program_id(2)
        u, c = u_ref[...].astype(F32), c_ref[...].astype(F32)
        bv = b_ref[...].astype(F32)
        dzv = dz_ref[...]
        w = w_ref[...]

        @pl.when((piece == 0) & (i == 0))
        def _():
            acc_ref[...] = jnp.zeros_like(acc_ref)

        @pl.when(piece == 0)
        def _():
            xc = c * u
            xp = jnp.where(i > 0, cp_ref[8:16, :].astype(F32) * up_ref[8:16, :].astype(F32), 0.0)
            x2, x1 = _shift_down(xc, xp, 2), _shift_down(xc, xp, 1)
            o_ref[...] = (dzv * (x2 * w[0:1] + x1 * w[1:2] + xc * w[2:3])).astype(BF16)
            dc_ref[...] = jnp.zeros_like(dc_ref)
            dconv = dzv * bv
            acc_ref[0:1, :] += jnp.sum(dconv * x2, axis=0, keepdims=True)
            acc_ref[1:2, :] += jnp.sum(dconv * x1, axis=0, keepdims=True)
            acc_ref[2:3, :] += jnp.sum(dconv * xc, axis=0, keepdims=True)

        @pl.when(piece == 1)
        def _():
            dconv = dzv * bv
            dn = jnp.where(i < nrow - 1, dzn_ref[...] * bn_ref[0:8, :].astype(F32), 0.0)
            dxc = dconv * w[2:3] + _shift_up(dconv, dn, 1) * w[1:2] + _shift_up(dconv, dn, 2) * w[0:1]
            o_ref[...] = (dxc * c).astype(BF16)
            dc_ref[...] = (dxc * u).astype(BF16)

    tile = lambda blk: pl.BlockSpec((tm, COL), lambda j, p, i: (i, blk + j))
    before = lambda blk: pl.BlockSpec((16, COL), lambda j, p, i: (jnp.maximum(i * r16 - 1, 0), blk + j))
    after = lambda rows, blk: pl.BlockSpec(
        (rows, COL), lambda j, p, i: (jnp.minimum((i + 1) * (tm // rows), s // rows - 1), blk + j))
    return pl.pallas_call(
        body, name="conv_bwd", grid=(D // COL, 2, nrow),
        out_shape=[jax.ShapeDtypeStruct((s, IN_W), BF16), jax.ShapeDtypeStruct((s + tm, D), BF16),
                   jax.ShapeDtypeStruct((8, D), F32)],
        in_specs=[tile(0), tile(U_BLK), tile(B_BLK), tile(C_BLK), before(U_BLK), before(C_BLK),
                  after(8, 0), after(16, B_BLK), pl.BlockSpec((3, COL), lambda j, p, i: (0, j)),
                  pl.BlockSpec(memory_space=pl.ANY)],
        out_specs=[pl.BlockSpec((tm, COL), lambda j, p, i: (i, jnp.where(p == 0, B_BLK, U_BLK) + j)),
                   pl.BlockSpec((tm, COL), lambda j, p, i: (jnp.where(p == 0, nrow, i), j)),
                   pl.BlockSpec((8, COL), lambda j, p, i: (0, j))],
        input_output_aliases={9: 0},
        compiler_params=_params("arbitrary", "arbitrary", "arbitrary"),
    )(dz, proj, proj, proj, proj, proj, dz, proj, cw, dproj)


def _copy_columns(name, src, dst, blk0, tm=1024):
    s, w = dst.shape[0], src.shape[1]
    fresh = isinstance(dst, jax.ShapeDtypeStruct)

    def body(x_ref, *rest):
        rest[-1][...] = x_ref[...]

    return pl.pallas_call(
        body, name=name, grid=(w // COL, s // tm),
        out_shape=jax.ShapeDtypeStruct(dst.shape, dst.dtype),
        in_specs=[pl.BlockSpec((tm, COL), lambda j, i: (i, j))] + ([] if fresh else [pl.BlockSpec(memory_space=pl.ANY)]),
        out_specs=pl.BlockSpec((tm, COL), lambda j, i: (i, blk0 + j)),
        input_output_aliases={} if fresh else {1: 0},
        compiler_params=_params("parallel", "parallel"),
    )(src, *([] if fresh else [dst]))


def _merge_fwd(ya, yc, proj, tm=512):
    s = proj.shape[0]

    def body(ya_ref, yc_ref, ga_ref, gc_ref, o_ref):
        o_ref[...] = (_sigmoid(ga_ref[...].astype(F32)) * ya_ref[...].astype(F32)
                      + _sigmoid(gc_ref[...].astype(F32)) * yc_ref[...].astype(F32)).astype(BF16)

    tile = lambda blk: pl.BlockSpec((tm, COL), lambda j, i: (i, blk + j))
    return pl.pallas_call(
        body, name="merge_fwd", grid=(D // COL, s // tm),
        out_shape=jax.ShapeDtypeStruct((s, D), BF16),
        in_specs=[tile(0), tile(0), tile(GA_BLK), tile(GC_BLK)], out_specs=tile(0),
        compiler_params=_params("parallel", "parallel"),
    )(ya, yc, proj, proj)


def _merge_bwd_branches(dm, proj, tm=512):
    s = proj.shape[0]

    def body(dm_ref, ga_ref, gc_ref, dya_ref, dyc_ref):
        dmv = dm_ref[...]
        dya_ref[...] = (dmv * _sigmoid(ga_ref[...].astype(F32))).astype(BF16)
        dyc_ref[...] = (dmv * _sigmoid(gc_ref[...].astype(F32))).astype(BF16)

    tile = lambda blk: pl.BlockSpec((tm, COL), lambda j, i: (i, blk + j))
    return pl.pallas_call(
        body, name="merge_bwd_branches", grid=(D // COL, s // tm),
        out_shape=[jax.ShapeDtypeStruct((s, D), BF16)] * 2,
        in_specs=[tile(0), tile(GA_BLK), tile(GC_BLK)], out_specs=[tile(0)] * 2,
        compiler_params=_params("parallel", "parallel"),
    )(dm, proj, proj)


def _merge_bwd_gates(dm, ya, yc, proj, tm=1024):
    s = proj.shape[0]
    half = D // COL

    def body(dm_ref, ya_ref, yc_ref, g_ref, o_ref):
        y = jnp.where(pl.program_id(0) < half, ya_ref[...].astype(F32), yc_ref[...].astype(F32))
        sig = _sigmoid(g_ref[...].astype(F32))
        o_ref[...] = (dm_ref[...] * y * sig * (1.0 - sig)).astype(BF16)

    chan = pl.BlockSpec((tm, COL), lambda jj, i: (i, jj % half))
    gate = pl.BlockSpec((tm, COL), lambda jj, i: (i, GA_BLK + jj))
    return pl.pallas_call(
        body, name="merge_bwd_gates", grid=(2 * half, s // tm),
        out_shape=jax.ShapeDtypeStruct((s, IN_W), BF16),
        in_specs=[chan, chan, chan, gate], out_specs=gate,
        compiler_params=_params("parallel", "parallel"),
    )(dm, ya, yc, proj)


def _mod_part(c_all, w_ada, b_part):
    def body(c_ref, w_ref, b_ref, o_ref):
        cv = c_ref[...]
        act = cv * _sigmoid(cv)
        o_ref[...] = jnp.dot(act, w_ref[...], preferred_element_type=F32,
                             precision=lax.Precision.HIGHEST) + b_ref[...]

    return pl.pallas_call(
        body, name="mod_part", out_shape=jax.ShapeDtypeStruct((N_DEV, w_ada.shape[1]), F32),
    )(c_all, w_ada, b_part)


def _w_ada_grad(c_all_t, dmod_part):
    def body(c_ref, d_ref, o_ref):
        cv = c_ref[...]
        act = cv * _sigmoid(cv)
        dv = d_ref[...]
        acc = act[:, 0:1] * dv[0:1, :]
        for b in range(1, N_DEV):
            acc = acc + act[:, b:b + 1] * dv[b:b + 1, :]
        o_ref[...] = acc

    return pl.pallas_call(
        body, name="w_ada_grad", out_shape=jax.ShapeDtypeStruct((D, dmod_part.shape[1]), F32),
    )(c_all_t, dmod_part)


def _sum_rows(name, v):
    def body(v_ref, o_ref):
        acc = v_ref[0]
        for k in range(1, N_DEV):
            acc = acc + v_ref[k]
        o_ref[...] = acc

    return pl.pallas_call(body, name=name, out_shape=jax.ShapeDtypeStruct(v.shape[1:], F32))(v)


def _adamw(name, w, g, m, v):
    rows, cols = w.shape
    limit = max(16, (1 << 20) // (4 * cols))
    tr = rows if rows <= limit else next((t for t in range(limit - limit % 16, 15, -16) if rows % t == 0), rows)
    c1 = 1.0 - ADAM_B1 ** ADAM_STEP
    c2 = 1.0 - ADAM_B2 ** ADAM_STEP
    parts = g.ndim == 3

    def body(w_ref, g_ref, m_ref, v_ref, go_ref, d_ref, nm_ref, nv_ref):
        if parts:
            gv = g_ref[0].astype(F32)
            for k in range(1, N_DEV):
                gv = gv + g_ref[k].astype(F32)
        else:
            gv = g_ref[...]
        go_ref[...] = gv
        nm = ADAM_B1 * m_ref[...] + (1.0 - ADAM_B1) * gv
        nv = ADAM_B2 * v_ref[...] + (1.0 - ADAM_B2) * (gv * gv)
        nm_ref[...] = nm
        nv_ref[...] = nv
        d_ref[...] = -ADAM_LR * ((nm / c1) / (jnp.sqrt(nv / c2) + ADAM_EPS) + ADAM_WD * w_ref[...])

    spec = pl.BlockSpec((tr, cols), lambda i: (i, 0))
    g_spec = pl.BlockSpec((N_DEV, tr, cols), lambda i: (0, i, 0)) if parts else spec
    return pl.pallas_call(
        body, name=name, grid=(rows // tr,),
        out_shape=[jax.ShapeDtypeStruct((rows, cols), F32)] * 4,
        in_specs=[spec, g_spec, spec, spec], out_specs=[spec] * 4,
        compiler_params=_params("parallel"),
    )(w, g, m, v)


def _adamw_small(ws, gs, ms, vs):
    n = len(ws)
    c1 = 1.0 - ADAM_B1 ** ADAM_STEP
    c2 = 1.0 - ADAM_B2 ** ADAM_STEP

    def body(*refs):
        for i in range(n):
            w_ref, g_ref, m_ref, v_ref = refs[i], refs[n + i], refs[2 * n + i], refs[3 * n + i]
            d_ref, nm_ref, nv_ref = refs[4 * n + 3 * i:4 * n + 3 * i + 3]
            gv = g_ref[...]
            nm = ADAM_B1 * m_ref[...] + (1.0 - ADAM_B1) * gv
            nv = ADAM_B2 * v_ref[...] + (1.0 - ADAM_B2) * (gv * gv)
            nm_ref[...] = nm
            nv_ref[...] = nv
            d_ref[...] = -ADAM_LR * ((nm / c1) / (jnp.sqrt(nv / c2) + ADAM_EPS) + ADAM_WD * w_ref[...])

    outs = pl.pallas_call(
        body, name="adamw_small",
        out_shape=[jax.ShapeDtypeStruct(w.shape, F32) for w in ws for _ in range(3)],
    )(*ws, *gs, *ms, *vs)
    return [tuple(outs[3 * i:3 * i + 3]) for i in range(n)]


HALF = FF // 2


def _sds(shape, dtype):
    return jax.ShapeDtypeStruct(shape, dtype)


def _row_tile(w):
    return lambda tm: ((tm, w), lambda i, j: (i, 0))


def _one(w):
    return lambda rows: ((rows, w), lambda i, j: (0, 0))


def _gate_up_swiglu(name, h, wgu, carry=None, tm=512):
    s = h.shape[0]
    tm = min(tm, s)

    def epilogue(prod, first, tin, tout):
        pq_ref, s_ref = tout
        a, b = prod[:, :HALF], prod[:, HALF:]
        sig = _sigmoid(a)
        act = a * sig
        pq_ref[:, :HALF] = (b * (sig * (1.0 + a * (1.0 - sig)))).astype(BF16)
        pq_ref[:, HALF:] = act.astype(BF16)
        s_ref[...] = (act * b).astype(BF16)

    return _mm(name, h, wgu, "NT", None, tm, FF, D, carry=carry, n_outer=True, epilogue=epilogue,
               tiles_out=[(_sds((s, 2 * FF), BF16), (tm, FF), lambda i, j: (i, j)),
                          (_sds((s, FF), BF16), (tm, HALF), lambda i, j: (i, j))])


def _d_hidden_swiglu(name, df, wd, ab, tm=512):
    s = df.shape[0]
    tm = min(tm, s)

    def epilogue(prod, first, tin, tout, cols):
        da_cols = slice(cols[0], cols[0] + cols[1])
        db_cols = slice(HALF + cols[0], HALF + cols[0] + cols[1])
        tout[0][:, da_cols] = (prod * tin[0][:, da_cols].astype(F32)).astype(BF16)
        tout[0][:, db_cols] = (prod * tin[0][:, db_cols].astype(F32)).astype(BF16)

    chunks = [(c0, min(384, HALF - c0)) for c0 in range(0, HALF, 384)]
    return _mm(name, df, wd, "NT", None, tm, HALF, D, n_outer=True, epilogue=epilogue, col_chunks=chunks,
               tiles_in=[(ab, (tm, FF), lambda i, j: (i, j))],
               tiles_out=[(_sds((s, 2 * FF), BF16), (tm, FF), lambda i, j: (i, j))])[0]


def _out_residual(name, a, w, x, gt, coef, nxt, tm=512, tk=FF):
    s = a.shape[0]
    tm = min(tm, s)

    def epilogue(prod, first, tin, tout):
        x_ref, gt_ref, g_ref, sc_ref, sh_ref = tin
        f_ref, xn_ref, h_ref = tout
        f_ref[...] = prod
        xn = x_ref[...] + (coef * gt_ref[...]) * prod
        xn_ref[...] = xn
        r = lax.rsqrt(jnp.mean(xn * xn, axis=-1, keepdims=True) + EPS)
        h_ref[...] = ((xn * r) * g_ref[...] * (1.0 + sc_ref[...]) + sh_ref[...]).astype(BF16)

    row, vec = _row_tile(D)(tm), _one(D)(1)
    return _mm(name, a, w, "NN", None, tm, D, tk, epilogue=epilogue,
               tiles_in=[(x, *row), (gt, *vec)] + [(v, *vec) for v in nxt],
               tiles_out=[(_sds((s, D), F32), *row), (_sds((s, D), F32), *row), (_sds((s, D), BF16), *row)])


def _out_loss(name, a, w, x, gt, coef, target, tm=512):
    s = a.shape[0]
    tm = min(tm, s)

    def epilogue(prod, first, tin, tout):
        x_ref, gt_ref, t_ref = tin
        f_ref, g_ref, df_ref, acc_ref = tout
        f_ref[...] = prod
        cg = coef * gt_ref[...]
        e = x_ref[...] + cg * prod - t_ref[...]
        gv = e * (1.0 / D)
        g_ref[...] = gv
        df_ref[...] = (cg * gv).astype(BF16)

        @pl.when(first)
        def _():
            acc_ref[...] = jnp.zeros_like(acc_ref)

        acc_ref[0:1, :] += coef * jnp.sum(gv * prod, axis=0, keepdims=True)
        acc_ref[1:2, :] += (0.5 / D) * jnp.sum(e * e, axis=0, keepdims=True)

    row, vec = _row_tile(D)(tm), _one(D)(1)
    return _mm(name, a, w, "NN", None, tm, D, FF, epilogue=epilogue,
               tiles_in=[(x, *row), (gt, *vec), (target, *row)],
               tiles_out=[(_sds((s, D), F32), *row), (_sds((s, D), F32), *row), (_sds((s, D), BF16), *row),
                          (_sds((8, D), F32), *_one(D)(8))])


def _d_h_norm_bwd(name, da, w, x, gin, g, sc, sh, before=None, carry=None, tm=256):
    s = da.shape[0]
    tm = min(tm, s)
    coef = before[2] if before else None

    def epilogue(prod, first, tin, tout):
        x_ref, gin_ref, g_ref, sc_ref, sh_ref = tin[:5]
        gout_ref, acc_ref = tout[:2]
        xv = x_ref[...]
        r = lax.rsqrt(jnp.mean(xv * xv, axis=-1, keepdims=True) + EPS)
        nv = xv * r
        gv, one_sc = g_ref[...], 1.0 + sc_ref[...]
        dn = prod * gv * one_sc
        gout = gin_ref[...] + r * (dn - nv * jnp.mean(dn * nv, axis=-1, keepdims=True))
        gout_ref[...] = gout

        @pl.when(first)
        def _():
            acc_ref[...] = jnp.zeros_like(acc_ref)

        dhn = prod * nv
        acc_ref[0:1, :] += jnp.sum(prod, axis=0, keepdims=True)
        acc_ref[1:2, :] += jnp.sum(dhn * gv, axis=0, keepdims=True)
        acc_ref[2:3, :] += jnp.sum(dhn * one_sc, axis=0, keepdims=True)
        if before:
            f_ref, gt_ref = tin[5:]
            tout[2][...] = ((coef * gt_ref[...]) * gout).astype(BF16)
            acc_ref[3:4, :] += coef * jnp.sum(gout * f_ref[...], axis=0, keepdims=True)

    row, vec = _row_tile(D)(tm), _one(D)(1)
    tiles_in = [(x, *row), (gin, *row), (g, *vec), (sc, *vec), (sh, *vec)]
    tiles_out = [(_sds((s, D), F32), *row), (_sds((8, D), F32), *_one(D)(8))]
    if before:
        tiles_in += [(before[0], *row), (before[1], *vec)]
        tiles_out.append((_sds((s, D), BF16), *row))
    return _mm(name, da, w, "NN", None, tm, D, da.shape[1], epilogue=epilogue, carry=carry, keep_b=True,
               tiles_in=tiles_in, tiles_out=tiles_out)


def _gate_tiles(proj, tm):
    return [(proj, (tm, COL), (lambda i, j, blk=blk: (i, blk))) for blk in (GA_BLK, GA_BLK + 1, GC_BLK, GC_BLK + 1)]


def _conv_branch_merge(z, wc, ya, proj, tm=512):
    s = z.shape[0]
    tm = min(tm, s)

    def epilogue(prod, first, tin, tout):
        ya_ref, ga0, ga1, gc0, gc1 = tin
        tout[0][...] = prod.astype(BF16)
        for half, (ga, gc) in enumerate(((ga0, gc0), (ga1, gc1))):
            cols = slice(half * COL, (half + 1) * COL)
            tout[1][:, cols] = (_sigmoid(ga[...].astype(F32)) * ya_ref[:, cols].astype(F32)
                                + _sigmoid(gc[...].astype(F32)) * prod[:, cols]).astype(BF16)

    row = _row_tile(D)(tm)
    return _mm("mix_conv_branch", z, wc, "NN", None, tm, D, D, epilogue=epilogue,
               tiles_in=[(ya, *row)] + _gate_tiles(proj, tm),
               tiles_out=[(_sds((s, D), BF16), *row), (_sds((s, D), BF16), *row)])


def _d_merged_branches(dmix, wo, ya, yc, proj, tm=512):
    s = dmix.shape[0]
    tm = min(tm, s)

    def epilogue(prod, first, tin, tout):
        ya_ref, yc_ref, ga0, ga1, gc0, gc1 = tin
        dya_ref, dyc_ref, dg_ref = tout
        for half, (ga, gc) in enumerate(((ga0, gc0), (ga1, gc1))):
            cols = slice(half * COL, (half + 1) * COL)
            dm = prod[:, cols]
            for y_ref, g_ref, dy_ref, off in ((ya_ref, ga, dya_ref, 0), (yc_ref, gc, dyc_ref, D)):
                sig = _sigmoid(g_ref[...].astype(F32))
                dms = dm * sig
                dy_ref[:, cols] = dms.astype(BF16)
                dg_ref[:, off + half * COL:off + (half + 1) * COL] = (
                    dms * y_ref[:, cols].astype(F32) * (1.0 - sig)).astype(BF16)

    row = _row_tile(D)(tm)
    return _mm("mix_d_merged", dmix, wo, "NT", None, tm, D, D, epilogue=epilogue,
               tiles_in=[(ya, *row), (yc, *row)] + _gate_tiles(proj, tm),
               tiles_out=[(_sds((s, D), BF16), *row), (_sds((s, D), BF16), *row),
                          (_sds((s, 2 * D), BF16), *_row_tile(2 * D)(tm))])


def _d_o_delta(dya, wa_t, o, tm=1024):
    s = dya.shape[0]
    tm = min(tm, s)

    def epilogue(prod, first, tin, tout):
        tout[0][...] = prod
        tout[1][...] = _heads(prod * tin[0][...].astype(F32), lambda ph, h: jnp.broadcast_to(
            jnp.sum(ph, axis=-1, keepdims=True), ph.shape))

    row = _row_tile(COL)(tm)
    return _mm("mix_d_o", dya, wa_t, "NN", None, tm, COL, D, epilogue=epilogue,
               tiles_in=[(o, *row)], tiles_out=[(_sds((s, COL), F32), *row), (_sds((s, COL), F32), *row)])


def _ffn_bwd(tag, df, x, gin, h, ab, sw, g, sc, sh, wgu, wd, before=None, tk_dw=2048):
    dwd = _mm(f"{tag}_dw_down", sw, df, "TN", BF16, HALF, D, tk_dw)
    dab = _d_hidden_swiglu(f"{tag}_d_hidden", df, wd, ab)
    dwgu = _mm(f"{tag}_dw_gate_up", dab, h, "TN", BF16, HALF, D, tk_dw)
    res = _d_h_norm_bwd(f"{tag}_d_h", dab, wgu, x, gin, g, sc, sh, before=before)
    return res, dwgu, dwd


def kernel(x, c, w_ada, b_ada, norm_ffn1, ffn1_w_gate, ffn1_w_up, ffn1_w_down, norm_mix, w_in, q_norm, k_norm, conv_w, w_attn_branch, w_conv_branch, w_out, norm_ffn2, ffn2_w_gate, ffn2_w_up, ffn2_w_down, loss_target, m_w_ada, m_b_ada, m_norm_ffn1, m_ffn1_w_gate, m_ffn1_w_up, m_ffn1_w_down, m_norm_mix, m_w_in, m_q_norm, m_k_norm, m_conv_w, m_w_attn_branch, m_w_conv_branch, m_w_out, m_norm_ffn2, m_ffn2_w_gate, m_ffn2_w_up, m_ffn2_w_down, v_w_ada, v_b_ada, v_norm_ffn1, v_ffn1_w_gate, v_ffn1_w_up, v_ffn1_w_down, v_norm_mix, v_w_in, v_q_norm, v_k_norm, v_conv_w, v_w_attn_branch, v_w_conv_branch, v_w_out, v_norm_ffn2, v_ffn2_w_gate, v_ffn2_w_up, v_ffn2_w_down):
    me = 4 * lax.axis_index("x") + 2 * lax.axis_index("y") + lax.axis_index("c")
    x0, target = x[0], loss_target[0]
    s = x0.shape[0]
    ada_cols = w_ada.shape[2]
    cw_cols = conv_w.shape[2]

    gathered = _small_allgather(
        "gather_c_conv", jnp.concatenate([c, conv_w[0].reshape(1, 3 * cw_cols)], axis=1))[:, 0]
    c_all = gathered[:, :D]
    cw = gathered[:, D:].reshape(N_DEV, 3, cw_cols).transpose(1, 0, 2).reshape(3, D)
    b_part = lax.dynamic_slice(b_ada, (0, me * ada_cols), (1, ada_cols))
    mod_part = _mod_part(c_all, w_ada[0], b_part)
    mod_all = _small_allgather("gather_mod", mod_part.reshape(1, N_DEV * ada_cols))
    mod = lax.dynamic_slice(mod_all.reshape(N_DEV, N_DEV, ada_cols), (0, me, 0), (N_DEV, 1, ada_cols))
    mod = mod.reshape(N_MOD, 1, D)
    sh1, sc1, gt1, sh2, sc2, gt2, sh3, sc3, gt3 = [mod[i] for i in range(N_MOD)]

    tb = lambda w: w[0].T.astype(BF16)
    nb = lambda w: w[0].astype(BF16)
    ffn1_shards = [tb(ffn1_w_gate), tb(ffn1_w_up), nb(ffn1_w_down)]
    ffn2_shards = [tb(ffn2_w_gate), tb(ffn2_w_up), nb(ffn2_w_down)]
    mix_shards = [tb(w_in), tb(w_attn_branch), nb(w_conv_branch), nb(w_out)]
    ffn_dst, ffn_base, ffn_jump, ffn_shapes = [0, 0, 1], [0, HALF, 0], [HALF, HALF, 0], [(2 * FF, D), (FF, D)]
    mix_dst, mix_base, mix_shapes = [0, 1, 2, 3], [0, 0, 0, 0], [(IN_W, D), (D, COL), (D, D), (D, D)]
    (wgu1,) = _run_plan_on_sequencer(
        "gather_ffn1_gate_up", _gather_plan(ffn1_shards[:2], ffn_dst[:2], ffn_base[:2], ffn_shapes[:1], ffn_jump[:2]), 1)
    (wd1,) = _run_plan_on_sequencer(
        "gather_ffn1_down", _gather_plan(ffn1_shards[2:], [0], [0], ffn_shapes[1:]), 8)
    win_t, wa_t, wc, wo = _run_plan_on_sequencer(
        "gather_mix_weights", _gather_plan(mix_shards, mix_dst, mix_base, mix_shapes), 2)
    wgu2, wd2 = _run_plan_on_sequencer(
        "gather_ffn2_weights", _gather_plan(ffn2_shards, ffn_dst, ffn_base, ffn_shapes, ffn_jump), 3)

    h1 = _normmod("ffn1_normmod", x0, norm_ffn1, sc1, sh1)
    ab1, s1 = _gate_up_swiglu("ffn1_gate_up", h1, wgu1)
    f1, x1, h2 = _out_residual("ffn1_down", s1, wd1, x0, gt1, 0.5, (norm_mix, sc2, sh2))
    proj = _mm("mix_in_proj", h2, win_t, "NT", BF16, 1024, IN_W // 4, D, n_outer=True)
    wqk = jnp.concatenate([jnp.tile(q_norm, (1, 12)), jnp.tile(k_norm, (1, 12))], axis=1)
    qkn = _qknorm(proj, wqk)
    group_out = [_attn_fwd(g, qkn, proj) for g in range(3)]
    o, lse = _attn_combine([go[0] for go in group_out], [go[1] for go in group_out])
    ya = _mm("mix_attn_branch", o, wa_t, "NT", BF16, 1024, 1024, COL)
    z = _conv_fwd(proj, cw)
    yc, merged = _conv_branch_merge(z, wc, ya, proj)
    mix, x2, h3 = _out_residual("mix_out_proj", merged, wo, x1, gt2, 1.0, (norm_ffn2, sc3, sh3), tk=D)
    ab3, s3 = _gate_up_swiglu("ffn2_gate_up", h3, wgu2)
    f3, g3, df3, acc_out = _out_loss("ffn2_down", s3, wd2, x2, gt3, 0.5, target)
    loss_part = jnp.sum(acc_out[1])

    ffn_rows = [sh_.shape[0] for sh_ in ffn1_shards]
    mix_rows = [sh_.shape[0] for sh_ in mix_shards]
    (g2, acc3, dmix), dwgu2, dwd2 = _ffn_bwd(
        "ffn2", df3, x2, g3, h3, ab3, s3, norm_ffn2, sc3, sh3, wgu2, wd2, before=(mix, gt2, 1.0))
    dya, dyc, dgates = _d_merged_branches(dmix, wo, ya, yc, proj)
    dwo = _mm("mix_dw_out", merged, dmix, "TN", BF16, 1024, 1024, 2048)
    dproj = _copy_columns("dproj_gates", dgates, jax.ShapeDtypeStruct((s, IN_W), BF16), GA_BLK)
    dwc = _mm("mix_dw_conv_branch", z, dyc, "TN", BF16, 1024, 1024, 2048)
    dz = _mm("mix_d_z", dyc, wc, "NT", F32, 1024, 1024, D)
    dproj, d_c, cw_acc = _conv_bwd(dz, proj, cw, dproj)
    dproj = _copy_columns("copy_d_c", d_c, dproj, C_BLK)
    dwa_t = _mm("mix_dw_attn_branch", dya, o, "TN", BF16, 1024, COL, 2048)
    do, delta = _d_o_delta(dya, wa_t, o)
    dqn = dkn = None
    for g in range(3):
        dqn, dkn, dproj = _attn_bwd(g, qkn, proj, do, lse, delta, dqn, dkn, dproj)
    dproj, wq_acc = _qknorm_bwd("qnorm_bwd", proj, dqn, wqk[:, :QKW // 2], dproj, 0)
    dproj, wk_acc = _qknorm_bwd("knorm_bwd", proj, dkn, wqk[:, QKW // 2:], dproj, QKW // 2 // COL)
    r_f2g, r_f2u, r_f2d, r_wa, r_wc, r_wo = _run_plan_on_sequencer(
        "scatter_ffn2_and_branch_grads",
        _scatter_plan([dwgu2, dwd2, dwa_t, dwc, dwo], [0, 0, 1, 2, 3, 4], [0, HALF, 0, 0, 0, 0],
                      ffn_rows + mix_rows[1:], [D, D, D, COL, D, D], [HALF, HALF, 0, 0, 0, 0]), 4)
    dwin_t = _mm("mix_dw_in", dproj, h2, "TN", BF16, IN_W // 4, COL, 2048)
    (r_win,) = _run_plan_on_sequencer(
        "scatter_w_in_grad", _scatter_plan([dwin_t], [0], [0], mix_rows[:1], [D]), 5)
    g1, acc2, df1 = _d_h_norm_bwd("mix_d_h", dproj, win_t, x1, g2, norm_mix, sc2, sh2, before=(f1, gt1, 0.5))
    (g0, acc1), dwgu1, dwd1 = _ffn_bwd(
        "ffn1", df1, x0, g1, h1, ab1, s1, norm_ffn1, sc1, sh1, wgu1, wd1)
    (r_f1d,) = _run_plan_on_sequencer(
        "scatter_ffn1_down_grad", _scatter_plan([dwd1], [0], [0], ffn_rows[2:], [D]), 6)
    r_f1g, r_f1u = _run_plan_on_sequencer(
        "scatter_ffn1_gate_up_grads",
        _scatter_plan([dwgu1], [0, 0], [0, HALF], ffn_rows[:2], [D, D], [HALF, HALF]), 7)

    dqw = jnp.sum(wq_acc[0].reshape(12, HD), axis=0)
    dkw = jnp.sum(wk_acc[0].reshape(12, HD), axis=0)
    small = jnp.concatenate([
        acc1[0], acc1[1], acc2[3], acc2[0], acc2[1], acc3[3], acc3[0], acc3[1], acc_out[0],
        acc1[2], acc2[2], acc3[2], dqw, dkw, cw_acc[0:3].reshape(3 * D),
        jnp.zeros((HD,), F32).at[0].set(loss_part)]).reshape(1, -1)
    small_all = _small_allgather("gather_small_grads", small)
    small_sum = _sum_rows("sum_small_grads", small_all)[0]
    n_mod = N_MOD * D
    g_b_ada = small_sum[:n_mod].reshape(1, n_mod)
    g_norm1, g_norm2, g_norm3 = [small_sum[n_mod + i * D:n_mod + (i + 1) * D].reshape(1, D) for i in range(3)]
    off = n_mod + 3 * D
    g_qn, g_kn = small_sum[off:off + HD].reshape(1, HD), small_sum[off + HD:off + 2 * HD].reshape(1, HD)
    g_cw_full = small_sum[off + 2 * HD:off + 2 * HD + 3 * D].reshape(3, D)
    loss = small_sum[off + 2 * HD + 3 * D]
    g_cw = lax.dynamic_slice(g_cw_full, (0, me * cw_cols), (3, cw_cols))
    dmod_part = lax.dynamic_slice(small_all[:, 0, :n_mod], (0, me * ada_cols), (N_DEV, ada_cols))
    g_w_ada = _w_ada_grad(c_all.T, dmod_part)

    as_rows = {"ffn1_w_gate", "ffn1_w_up", "w_in", "w_attn_branch", "ffn2_w_gate", "ffn2_w_up"}
    grad_list = [g_w_ada, g_b_ada, g_norm1, r_f1g, r_f1u, r_f1d, g_norm2, r_win,
                 g_qn, g_kn, g_cw, r_wa, r_wc, r_wo, g_norm3, r_f2g, r_f2u, r_f2d]
    weights = [w_ada, b_ada, norm_ffn1, ffn1_w_gate, ffn1_w_up, ffn1_w_down, norm_mix, w_in, q_norm, k_norm,
               conv_w, w_attn_branch, w_conv_branch, w_out, norm_ffn2, ffn2_w_gate, ffn2_w_up, ffn2_w_down]
    ms = [m_w_ada, m_b_ada, m_norm_ffn1, m_ffn1_w_gate, m_ffn1_w_up, m_ffn1_w_down, m_norm_mix, m_w_in, m_q_norm,
          m_k_norm, m_conv_w, m_w_attn_branch, m_w_conv_branch, m_w_out, m_norm_ffn2, m_ffn2_w_gate,
          m_ffn2_w_up, m_ffn2_w_down]
    vs = [v_w_ada, v_b_ada, v_norm_ffn1, v_ffn1_w_gate, v_ffn1_w_up, v_ffn1_w_down, v_norm_mix, v_w_in, v_q_norm,
          v_k_norm, v_conv_w, v_w_attn_branch, v_w_conv_branch, v_w_out, v_norm_ffn2, v_ffn2_w_gate,
          v_ffn2_w_up, v_ffn2_w_down]
    wnames = ["w_ada", "b_ada", "norm_ffn1", "ffn1_w_gate", "ffn1_w_up", "ffn1_w_down", "norm_mix", "w_in",
              "q_norm", "k_norm", "conv_w", "w_attn_branch", "w_conv_branch", "w_out", "norm_ffn2",
              "ffn2_w_gate", "ffn2_w_up", "ffn2_w_down"]
    small = [i for i, gr in enumerate(grad_list) if gr.ndim == 2 and gr.size <= 16384]
    flat = lambda a, i: a.reshape(-1, weights[i].shape[-1])
    small_res = dict(zip(small, _adamw_small(
        [flat(weights[i], i) for i in small], [flat(grad_list[i], i) for i in small],
        [flat(ms[i], i) for i in small], [flat(vs[i], i) for i in small])))
    grad_out, deltas, new_ms, new_vs = [], [], [], []
    for idx, (nm, w, gr, m_, v_) in enumerate(zip(wnames, weights, grad_list, ms, vs)):
        if idx in small_res:
            gr, dl, nm_, nv_ = [r.reshape(w.shape) for r in (gr, *small_res[idx])]
        elif nm in as_rows:
            res = _adamw(f"adamw_{nm}", w[0].T, gr, m_[0].T, v_[0].T)
            gr, dl, nm_, nv_ = [r.T[None] for r in res]
        else:
            two_d = (-1, w.shape[-1])
            res = _adamw(f"adamw_{nm}", w.reshape(two_d), gr if gr.ndim == 3 else gr.reshape(two_d),
                         m_.reshape(two_d), v_.reshape(two_d))
            gr, dl, nm_, nv_ = [r.reshape(w.shape) for r in res]
        grad_out.append(gr)
        deltas.append(dl)
        new_ms.append(nm_)
        new_vs.append(nv_)
    return (loss, g0[None], *grad_out, *deltas, *new_ms, *new_vs)
```

```python
import functools

import jax
import jax.numpy as jnp
from jax import lax
from jax.experimental import pallas as pl
from jax.experimental.pallas import tpu as pltpu
from jax.experimental.pallas import tpu_sc as plsc

F32 = jnp.float32
BF16 = jnp.bfloat16
MESH = pl.DeviceIdType.MESH

N_DEV = 8
D = 1024
FF = 2816
HD = 128
N_HEADS = 4
DILATIONS = (1, 4, 16)
BAND = 128
QKW = 2 * 3 * N_HEADS * HD
IN_W = 9728
COL = 512
V_BLK, U_BLK, B_BLK, C_BLK, GA_BLK, GC_BLK = 6, 9, 11, 13, 15, 17
EPS = 1e-6
N_MOD = 9
ADAM_LR, ADAM_B1, ADAM_B2, ADAM_EPS, ADAM_WD, ADAM_STEP = 0.001, 0.9, 0.999, 1e-08, 0.01, 10

NT_DIMS = (((1,), (1,)), ((), ()))
TN_DIMS = (((0,), (0,)), ((), ()))
NN_DIMS = (((1,), (0,)), ((), ()))


def _place():
    return lax.axis_index("x"), lax.axis_index("y"), lax.axis_index("c")


def _flip(coord, bit):
    return 1 - coord if bit else coord


def _params(*sem):
    return pltpu.CompilerParams(dimension_semantics=sem)


def _small_allgather(name, v):
    n = v.shape[-1]

    def body(v_ref, out_ref, send_sems, recv_sems):
        x, y, c = _place()
        me = 4 * x + 2 * y + c
        out_ref[me] = v_ref[...]
        copies = []
        for k in range(1, N_DEV):
            peer = (_flip(x, (k >> 2) & 1), _flip(y, (k >> 1) & 1), _flip(c, k & 1))
            cp = pltpu.make_async_remote_copy(
                src_ref=v_ref, dst_ref=out_ref.at[me], send_sem=send_sems.at[k - 1],
                recv_sem=recv_sems.at[k - 1], device_id=peer, device_id_type=MESH)
            cp.start()
            copies.append(cp)
        for cp in copies:
            cp.wait()

    return pl.pallas_call(
        body, name=name,
        out_shape=jax.ShapeDtypeStruct((N_DEV, 1, n), F32),
        in_specs=[pl.BlockSpec(memory_space=pltpu.VMEM)],
        out_specs=pl.BlockSpec(memory_space=pltpu.VMEM),
        scratch_shapes=[pltpu.SemaphoreType.DMA((N_DEV - 1,)), pltpu.SemaphoreType.DMA((N_DEV - 1,))],
    )(v)


class _Plan:
    def __init__(self, operands, out_shapes, sems, phases):
        self.operands, self.out_shapes, self.sems, self.phases = operands, out_shapes, sems, phases


def _slab_start(base, rows, jump, idx):
    return pl.multiple_of(base + idx * rows + (idx // 4) * jump, 16)


def _gather_plan(shards, dst_of, base_of, dst_shapes, jump_of=None):
    n = len(shards)
    rows = [s.shape[0] for s in shards]
    jump_of = jump_of or [0] * n

    def phases(srcs, dsts, sems):
        send_sems, recv_sems, local_sems = sems
        x, y, c = _place()
        me, sibling = (x, y, c), (x, y, 1 - c)
        chips = [(1 - x, y), (x, 1 - y), (1 - x, 1 - y)]

        def slab(i, px, py, pc):
            start = _slab_start(base_of[i], rows[i], jump_of[i], 4 * px + 2 * py + pc)
            return dsts[dst_of[i]].at[pl.ds(start, rows[i])]

        def copy(i, k, block, to, src=None):
            return pltpu.make_async_remote_copy(
                src_ref=slab(i, *block) if src is None else src, dst_ref=slab(i, *block),
                send_sem=send_sems.at[i, k], recv_sem=recv_sems.at[i, k],
                device_id=to, device_id_type=MESH)

        def mine():
            return [pltpu.make_async_copy(srcs[i], slab(i, *me), local_sems.at[i]) for i in range(n)]

        def first():
            out = []
            for i in range(n):
                out.append(copy(i, 0, me, sibling, src=srcs[i]))
                out += [copy(i, 1 + j, me, (*chip, c), src=srcs[i]) for j, chip in enumerate(chips)]
            return out

        def passed():
            return [(copy(i, 1 + j, (*chip, c), me), copy(i, 4 + j, (*chip, c), sibling))
                    for j, chip in enumerate(chips) for i in range(n)]

        def start():
            for cp in mine() + first():
                cp.start()

        def middle():
            for landed, onward in passed():
                landed.wait_recv()
                onward.start()

        def finish():
            for i in range(n):
                copy(i, 0, sibling, me).wait_recv()
                for j, chip in enumerate(chips):
                    copy(i, 4 + j, (*chip, 1 - c), me).wait_recv()
            for cp in first() + [onward for _, onward in passed()]:
                cp.wait_send()
            for cp in mine():
                cp.wait()

        return start, middle, finish

    sems = [pltpu.SemaphoreType.DMA((n, 7)), pltpu.SemaphoreType.DMA((n, 7)), pltpu.SemaphoreType.DMA((n,))]
    return _Plan(list(shards), [jax.ShapeDtypeStruct(s, BF16) for s in dst_shapes], sems, phases)


def _scatter_plan(grads, src_of, base_of, rows, cols, jump_of=None):
    n = len(rows)
    jump_of = jump_of or [0] * n

    def phases(srcs, recvs, sems):
        send_sems, recv_sems, local_sems = sems
        x, y, c = _place()
        me = 4 * x + 2 * y + c

        def slab(i, idx):
            start = _slab_start(base_of[i], rows[i], jump_of[i], idx)
            return srcs[src_of[i]].at[pl.ds(start, rows[i])]

        def copies():
            out = [pltpu.make_async_copy(slab(i, me), recvs[i].at[me], local_sems.at[i]) for i in range(n)]
            for k in range(1, N_DEV):
                px, py, pc = _flip(x, (k >> 2) & 1), _flip(y, (k >> 1) & 1), _flip(c, k & 1)
                out += [pltpu.make_async_remote_copy(
                    src_ref=slab(i, 4 * px + 2 * py + pc), dst_ref=recvs[i].at[me],
                    send_sem=send_sems.at[i, k - 1], recv_sem=recv_sems.at[i, k - 1],
                    device_id=(px, py, pc), device_id_type=MESH) for i in range(n)]
            return out

        def start():
            for cp in copies():
                cp.start()

        def finish():
            for cp in copies():
                cp.wait()

        return start, None, finish

    sems = [pltpu.SemaphoreType.DMA((n, 7)), pltpu.SemaphoreType.DMA((n, 7)), pltpu.SemaphoreType.DMA((n,))]
    out_shapes = [jax.ShapeDtypeStruct((N_DEV, rows[i], cols[i]), BF16) for i in range(n)]
    return _Plan(list(grads), out_shapes, sems, phases)


def _run_plan(name, plan):
    n_in, n_out = len(plan.operands), len(plan.out_shapes)

    def body(*refs):
        for phase in plan.phases(refs[:n_in], refs[n_in:n_in + n_out], refs[n_in + n_out:]):
            if phase is not None:
                phase()

    hbm = pl.BlockSpec(memory_space=pltpu.HBM)
    return pl.pallas_call(
        body, name=name, out_shape=plan.out_shapes,
        in_specs=[hbm] * n_in, out_specs=[hbm] * n_out, scratch_shapes=plan.sems,
    )(*plan.operands)


def _run_plan_on_sequencer(name, plan, collective_id):
    src_refs = [jax.new_ref(a, memory_space=pltpu.MemorySpace.HBM) for a in plan.operands]
    dst_refs = [jax.empty_ref(s, memory_space=pltpu.MemorySpace.HBM) for s in plan.out_shapes]

    @pl.kernel(mesh=plsc.ScalarSubcoreMesh(axis_name="sequencer", num_cores=1), name=name,
               scratch_types=tuple(plan.sems),
               compiler_params=pltpu.CompilerParams(collective_id=collective_id))
    def launch(*sems):
        x, y, c = _place()
        barrier = pltpu.get_barrier_semaphore()
        for k in range(1, N_DEV):
            peer = (_flip(x, (k >> 2) & 1), _flip(y, (k >> 1) & 1), _flip(c, k & 1))
            pl.semaphore_signal(barrier, inc=1, device_id=peer, device_id_type=MESH)
        pl.semaphore_wait(barrier, N_DEV - 1)
        for phase in plan.phases(src_refs, dst_refs, sems):
            if phase is not None:
                phase()

    launch()
    return [r[...] for r in dst_refs]


def _sum_contributions(name, recv):
    _, rows, cols = recv.shape
    tr = rows if rows <= 512 else 304 if rows % 304 == 0 else 256

    def body(r_ref, o_ref):
        acc = r_ref[0].astype(F32)
        for k in range(1, N_DEV):
            acc = acc + r_ref[k].astype(F32)
        o_ref[...] = acc

    return pl.pallas_call(
        body, name=name, grid=(rows // tr,),
        out_shape=jax.ShapeDtypeStruct((rows, cols), F32),
        in_specs=[pl.BlockSpec((N_DEV, tr, cols), lambda i: (0, i, 0))],
        out_specs=pl.BlockSpec((tr, cols), lambda i: (i, 0)),
        compiler_params=_params("parallel"),
    )(recv)


def _mm(name, a, b, mode, out_dtype, tm, tn, tk, *, carry=None, tiles_in=(), tiles_out=(), epilogue=None,
        n_outer=False, keep_b=False, col_chunks=None, after=()):
    if mode == "TN":
        kk, m = a.shape
    else:
        m, kk = a.shape
    n = b.shape[0] if mode == "NT" else b.shape[1]
    tm, tn, tk = min(tm, m), min(tn, n), min(tk, kk)
    assert m % tm == 0 and n % tn == 0 and kk % tk == 0, (name, m, n, kk, tm, tn, tk)
    ni, nj, nk = m // tm, n // tn, kk // tk
    steps = ni * nj * nk
    dims = {"NN": NN_DIMS, "NT": NT_DIMS, "TN": TN_DIMS}[mode]
    if epilogue is None:
        tiles_out = [(jax.ShapeDtypeStruct((m, n), out_dtype), (tm, tn), lambda i, j: (i, j))]
    n_tin, n_tout = len(tiles_in), len(tiles_out)
    n_in = len(carry.operands) if carry else 0
    n_out = len(carry.out_shapes) if carry else 0
    n_acc = 1 if nk > 1 else 0
    n_keep = 2 if keep_b else 0
    n_after = len(after)
    assert not carry or steps >= 3
    assert not keep_b or (nk == 1 and nj == 1)
    assert not col_chunks or (epilogue is not None and nk == 1 and mode != "TN")
    ij = (lambda p, q: (q, p)) if n_outer else (lambda p, q: (p, q))
    inner = ni if n_outer else nj

    def body(a_ref, b_ref, *rest):
        tin = rest[:n_tin]
        cin = rest[n_tin:n_tin + n_in]
        rest = rest[n_tin + n_in + n_after:]
        tout = rest[:n_tout]
        cout = rest[n_tout:n_tout + n_out]
        scratch = rest[n_tout + n_out:]
        k = pl.program_id(2)
        visit = pl.program_id(0) * inner + pl.program_id(1)
        step = visit * nk + k
        if keep_b:
            b_kept, b_sem = scratch[n_acc:n_acc + 2]

            @pl.when(step == 0)
            def _():
                cp = pltpu.make_async_copy(b_ref, b_kept, b_sem)
                cp.start()
                cp.wait()

            b_ref = b_kept
        if carry:
            start, middle, finish = carry.phases(cin, cout, scratch[n_acc + n_keep:])
            pl.when(step == 0)(start)

        def store(prod, c=0, cols=()):
            if epilogue is None:
                tout[0][...] = prod.astype(out_dtype)
            else:
                epilogue(prod, jnp.logical_and(visit == 0, c == 0), tin, tout, *cols)

        if col_chunks:
            for c, (c0, cw) in enumerate(col_chunks):
                b_part = b_ref[pl.ds(c0, cw), :] if mode == "NT" else b_ref[:, pl.ds(c0, cw)]
                store(lax.dot_general(a_ref[...], b_part, dims, preferred_element_type=F32), c, ((c0, cw),))
        else:
            part = lax.dot_general(a_ref[...], b_ref[...], dims, preferred_element_type=F32)
            if nk == 1:
                store(part)
            else:
                acc_ref = scratch[0]

                @pl.when(k == 0)
                def _():
                    acc_ref[...] = part

                @pl.when((k > 0) & (k < nk - 1))
                def _():
                    acc_ref[...] += part

                @pl.when(k == nk - 1)
                def _():
                    store(acc_ref[...] + part)

        if carry:
            if middle is not None:
                pl.when(step == (steps * 3) // 5)(middle)
            pl.when(step == steps - 1)(finish)

    def spec(shape, fn):
        return pl.BlockSpec(shape, lambda p, q, k: fn(*ij(p, q)))

    a_spec = (pl.BlockSpec((tk, tm), lambda p, q, k: (k, ij(p, q)[0])) if mode == "TN"
              else pl.BlockSpec((tm, tk), lambda p, q, k: (ij(p, q)[0], k)))
    if keep_b:
        b_spec = pl.BlockSpec(memory_space=pl.ANY)
    elif mode == "NT":
        b_spec = pl.BlockSpec((tn, tk), lambda p, q, k: (ij(p, q)[1], k))
    else:
        b_spec = pl.BlockSpec((tk, tn), lambda p, q, k: (k, ij(p, q)[1]))
    hbm = pl.BlockSpec(memory_space=pltpu.HBM)
    sequential = carry or epilogue or keep_b
    out = pl.pallas_call(
        body, name=name, grid=(nj, ni, nk) if n_outer else (ni, nj, nk),
        out_shape=[t[0] for t in tiles_out] + (carry.out_shapes if carry else []),
        in_specs=([a_spec, b_spec] + [spec(t[1], t[2]) for t in tiles_in] + [hbm] * n_in
                  + [pl.BlockSpec(memory_space=pl.ANY)] * n_after),
        out_specs=[spec(t[1], t[2]) for t in tiles_out] + [hbm] * n_out,
        scratch_shapes=([pltpu.VMEM((tm, tn), F32)] * n_acc
                        + ([pltpu.VMEM(b.shape, b.dtype), pltpu.SemaphoreType.DMA] if keep_b else [])
                        + (carry.sems if carry else [])),
        compiler_params=(_params("arbitrary", "arbitrary", "arbitrary") if sequential
                         else _params("parallel", "parallel", "arbitrary")),
    )(a, b, *[t[0] for t in tiles_in], *(carry.operands if carry else []), *after)
    return out if (carry or epilogue) else out[0]


def _row(tm, w, off=0):
    return pl.BlockSpec((tm, w), lambda i: (i, off))


def _vec(w):
    return pl.BlockSpec((1, w), lambda i: (0, 0))


def _sigmoid(x):
    return 0.5 * jnp.tanh(0.5 * x) + 0.5


def _normmod(name, x, g, sc, sh, tm=512):
    s = x.shape[0]

    def body(x_ref, g_ref, sc_ref, sh_ref, h_ref):
        xv = x_ref[...]
        r = lax.rsqrt(jnp.mean(xv * xv, axis=-1, keepdims=True) + EPS)
        h_ref[...] = ((xv * r) * g_ref[...] * (1.0 + sc_ref[...]) + sh_ref[...]).astype(BF16)

    return pl.pallas_call(
        body, name=name, grid=(s // tm,),
        out_shape=jax.ShapeDtypeStruct((s, D), BF16),
        in_specs=[_row(tm, D), _vec(D), _vec(D), _vec(D)], out_specs=_row(tm, D),
        compiler_params=_params("parallel"),
    )(x, g, sc, sh)


def _normmod_bwd(name, dh, x, gin, g, sc, sh, tm=512):
    s = x.shape[0]

    def body(dh_ref, x_ref, gin_ref, g_ref, sc_ref, sh_ref, gout_ref, acc_ref):
        xv, dhv = x_ref[...], dh_ref[...]
        r = lax.rsqrt(jnp.mean(xv * xv, axis=-1, keepdims=True) + EPS)
        nv = xv * r
        gv, one_sc = g_ref[...], 1.0 + sc_ref[...]
        dn = dhv * gv * one_sc
        dx = r * (dn - nv * jnp.mean(dn * nv, axis=-1, keepdims=True))
        gout_ref[...] = gin_ref[...] + dx

        @pl.when(pl.program_id(0) == 0)
        def _():
            acc_ref[...] = jnp.zeros_like(acc_ref)

        dhn = dhv * nv
        acc_ref[0:1, :] += jnp.sum(dhv, axis=0, keepdims=True)
        acc_ref[1:2, :] += jnp.sum(dhn * gv, axis=0, keepdims=True)
        acc_ref[2:3, :] += jnp.sum(dhn * one_sc, axis=0, keepdims=True)

    return pl.pallas_call(
        body, name=name, grid=(s // tm,),
        out_shape=[jax.ShapeDtypeStruct((s, D), F32), jax.ShapeDtypeStruct((8, D), F32)],
        in_specs=[_row(tm, D), _row(tm, D), _row(tm, D), _vec(D), _vec(D), _vec(D)],
        out_specs=[_row(tm, D), pl.BlockSpec((8, D), lambda i: (0, 0))],
        compiler_params=_params("arbitrary"),
    )(dh, x, gin, g, sc, sh)


def _swiglu(name, ab, tm=512):
    s = ab.shape[0]

    def body(ab_ref, s_ref):
        a = ab_ref[:, :FF].astype(F32)
        b = ab_ref[:, FF:].astype(F32)
        s_ref[...] = (a * _sigmoid(a) * b).astype(BF16)

    return pl.pallas_call(
        body, name=name, grid=(s // tm,),
        out_shape=jax.ShapeDtypeStruct((s, FF), BF16),
        in_specs=[_row(tm, 2 * FF)], out_specs=_row(tm, FF),
        compiler_params=_params("parallel"),
    )(ab)


def _swiglu_bwd(name, ds, ab, tm=256):
    s = ab.shape[0]

    def body(ds_ref, ab_ref, dab_ref):
        a = ab_ref[:, :FF].astype(F32)
        b = ab_ref[:, FF:].astype(F32)
        dsv = ds_ref[...].astype(F32)
        sig = _sigmoid(a)
        dab_ref[:, :FF] = (dsv * b * (sig * (1.0 + a * (1.0 - sig)))).astype(BF16)
        dab_ref[:, FF:] = (dsv * (a * sig)).astype(BF16)

    return pl.pallas_call(
        body, name=name, grid=(s // tm,),
        out_shape=jax.ShapeDtypeStruct((s, 2 * FF), BF16),
        in_specs=[_row(tm, FF), _row(tm, 2 * FF)], out_specs=_row(tm, 2 * FF),
        compiler_params=_params("parallel"),
    )(ds, ab)


def _residual(name, x, f, gt, coef, tm=512):
    s = x.shape[0]

    def body(x_ref, f_ref, gt_ref, o_ref):
        o_ref[...] = x_ref[...] + (coef * gt_ref[...]) * f_ref[...]

    return pl.pallas_call(
        body, name=name, grid=(s // tm,),
        out_shape=jax.ShapeDtypeStruct((s, D), F32),
        in_specs=[_row(tm, D), _row(tm, D), _vec(D)], out_specs=_row(tm, D),
        compiler_params=_params("parallel"),
    )(x, f, gt)


def _gate_bwd(name, gin, f, gt, coef, tm=512):
    s = gin.shape[0]

    def body(g_ref, f_ref, gt_ref, df_ref, acc_ref):
        gv = g_ref[...]
        df_ref[...] = ((coef * gt_ref[...]) * gv).astype(BF16)

        @pl.when(pl.program_id(0) == 0)
        def _():
            acc_ref[...] = jnp.zeros_like(acc_ref)

        acc_ref[0:1, :] += coef * jnp.sum(gv * f_ref[...], axis=0, keepdims=True)

    return pl.pallas_call(
        body, name=name, grid=(s // tm,),
        out_shape=[jax.ShapeDtypeStruct((s, D), BF16), jax.ShapeDtypeStruct((8, D), F32)],
        in_specs=[_row(tm, D), _row(tm, D), _vec(D)],
        out_specs=[_row(tm, D), pl.BlockSpec((8, D), lambda i: (0, 0))],
        compiler_params=_params("arbitrary"),
    )(gin, f, gt)


def _loss_grad(x3, target, tm=512):
    s = x3.shape[0]

    def body(y_ref, t_ref, g_ref, l_ref):
        e = y_ref[...] - t_ref[...]
        g_ref[...] = e * (1.0 / D)

        @pl.when(pl.program_id(0) == 0)
        def _():
            l_ref[...] = jnp.zeros_like(l_ref)

        l_ref[...] += jnp.sum(jnp.mean(e * e, axis=-1, keepdims=True), axis=0, keepdims=True) * 0.5

    return pl.pallas_call(
        body, name="loss_grad", grid=(s // tm,),
        out_shape=[jax.ShapeDtypeStruct((s, D), F32), jax.ShapeDtypeStruct((8, 128), F32)],
        in_specs=[_row(tm, D), _row(tm, D)],
        out_specs=[_row(tm, D), pl.BlockSpec((8, 128), lambda i: (0, 0))],
        compiler_params=_params("arbitrary"),
    )(x3, target)


def _heads(x, fn):
    return jnp.concatenate([fn(x[:, h * HD:(h + 1) * HD], h) for h in range(COL // HD)], axis=1)


def _qknorm(proj, wqk, tm=1024):
    s = proj.shape[0]

    def body(p_ref, w_ref, o_ref):
        pv = p_ref[...].astype(F32)
        wv = w_ref[...]

        def one(qh, h):
            r = lax.rsqrt(jnp.mean(qh * qh, axis=-1, keepdims=True) + EPS)
            return (qh * r) * wv[:, h * HD:(h + 1) * HD]

        o_ref[...] = _heads(pv, one).astype(BF16)

    return pl.pallas_call(
        body, name="qknorm", grid=(s // tm, QKW // COL),
        out_shape=jax.ShapeDtypeStruct((s, QKW), BF16),
        in_specs=[pl.BlockSpec((tm, COL), lambda i, j: (i, j)), pl.BlockSpec((1, COL), lambda i, j: (0, j))],
        out_specs=pl.BlockSpec((tm, COL), lambda i, j: (i, j)),
        compiler_params=_params("parallel", "parallel"),
    )(proj, wqk)


def _qknorm_bwd(name, proj, dn, w, dproj, blk0, tm=1024):
    s = proj.shape[0]
    nblk = dn.shape[1] // COL

    def body(p_ref, d_ref, w_ref, _, o_ref, acc_ref):
        pv = p_ref[...].astype(F32)
        dv = d_ref[...]
        wv = w_ref[...]
        sums = []

        def one(qh, h):
            dn = dv[:, h * HD:(h + 1) * HD]
            r = lax.rsqrt(jnp.mean(qh * qh, axis=-1, keepdims=True) + EPS)
            nh = qh * r
            sums.append(jnp.sum(dn * nh, axis=0, keepdims=True))
            dnw = dn * wv[:, h * HD:(h + 1) * HD]
            return r * (dnw - nh * jnp.mean(dnw * nh, axis=-1, keepdims=True))

        o_ref[...] = _heads(pv, one).astype(BF16)

        @pl.when(pl.program_id(1) == 0)
        def _():
            acc_ref[...] = jnp.zeros_like(acc_ref)

        acc_ref[0:1, :] += jnp.concatenate(sums, axis=1)

    return pl.pallas_call(
        body, name=name, grid=(nblk, s // tm),
        out_shape=[jax.ShapeDtypeStruct((s, IN_W), BF16), jax.ShapeDtypeStruct((8, nblk * COL), F32)],
        in_specs=[pl.BlockSpec((tm, COL), lambda j, i: (i, blk0 + j)), pl.BlockSpec((tm, COL), lambda j, i: (i, j)),
                  pl.BlockSpec((1, COL), lambda j, i: (0, j)), pl.BlockSpec(memory_space=pl.ANY)],
        out_specs=[pl.BlockSpec((tm, COL), lambda j, i: (i, blk0 + j)),
                   pl.BlockSpec((8, COL), lambda j, i: (0, j))],
        input_output_aliases={3: 0},
        compiler_params=_params("arbitrary", "arbitrary"),
    )(proj, dn, w, dproj)


def _attn_shapes(s, g):
    d = DILATIONS[g]
    tb = min(s, max(2048, 256 * d))
    sb = min(256, tb // d)
    pb = BAND * d
    assert s % tb == 0 and tb % pb == 0 and (tb // d) % sb == 0 and sb % BAND == 0
    return d, tb, sb, pb


def _lanes(x, width):
    return jnp.concatenate([x] * (width // HD), axis=1)


def _every(start, size, d):
    return pl.ds(start, size, stride=d) if d > 1 else pl.ds(start, size)


def _attn_specs(g, tb, pb, s, ahead):
    ratio = tb // pb
    if ahead:
        nbr = lambda n: jnp.minimum((n + 1) * ratio, s // pb - 1)
    else:
        nbr = lambda n: jnp.maximum(n * ratio - 1, 0)
    cur = lambda base: pl.BlockSpec((tb, HD), lambda h, n: (n, base + g * N_HEADS + h))
    side = lambda base: pl.BlockSpec((pb, HD), lambda h, n: (nbr(n), base + g * N_HEADS + h))
    tok = pl.BlockSpec((tb, HD), lambda h, n: (n, h))
    tok_side = pl.BlockSpec((pb, HD), lambda h, n: (nbr(n), h))
    return cur, side, tok, tok_side


Q_COL, K_COL, V_COL = 0, 12, 24


def _attn_fwd(g, qkn, proj):
    s = qkn.shape[0]
    d, tb, sb, pb = _attn_shapes(s, g)
    ft = F32 if d > 1 else BF16
    nj = tb // d // sb
    scale = HD ** -0.5

    def body(q_ref, kc_ref, kp_ref, vc_ref, vp_ref, o_ref, lse_ref, qf, kf, vf):
        n = pl.program_id(1)
        qf[...] = q_ref[...].astype(ft)
        kf[0:pb] = kp_ref[...].astype(ft)
        kf[pb:] = kc_ref[...].astype(ft)
        vf[0:pb] = vp_ref[...].astype(ft)
        vf[pb:] = vc_ref[...].astype(ft)
        for r in range(d):
            for j in range(nj):
                at = j * sb * d + r
                q = qf[_every(at, sb, d), :].astype(BF16)
                k = kf[_every(at, sb + BAND, d), :].astype(BF16)
                v = vf[_every(at, sb + BAND, d), :].astype(BF16)
                sc = lax.dot_general(q, k, NT_DIMS, preferred_element_type=F32) * scale
                qi = lax.broadcasted_iota(jnp.int32, sc.shape, 0)
                kj = lax.broadcasted_iota(jnp.int32, sc.shape, 1)
                valid = (kj >= qi) & (kj <= qi + BAND)
                if j == 0:
                    valid = valid & ((kj >= BAND) | (n > 0))
                sc = jnp.where(valid, sc, -1e30)
                m = jnp.max(sc, axis=-1, keepdims=True)
                p = jnp.exp(sc - m)
                l = jnp.sum(p, axis=-1, keepdims=True)
                o = lax.dot_general(p.astype(BF16), v, NN_DIMS, preferred_element_type=F32)
                o_ref[_every(at, sb, d), :] = o / l
                lse_ref[_every(at, sb, d), :] = jnp.broadcast_to(m + jnp.log(l), (sb, HD))

    cur, side, tok, _ = _attn_specs(g, tb, pb, s, ahead=False)
    return pl.pallas_call(
        body, name=f"attn_fwd_g{g}", grid=(N_HEADS, s // tb),
        out_shape=[jax.ShapeDtypeStruct((s, COL), F32)] * 2,
        in_specs=[cur(Q_COL), cur(K_COL), side(K_COL), cur(V_COL), side(V_COL)],
        out_specs=[tok, tok],
        scratch_shapes=[pltpu.VMEM((tb, HD), ft), pltpu.VMEM((tb + pb, HD), ft),
                        pltpu.VMEM((tb + pb, HD), ft)],
        compiler_params=_params("parallel", "arbitrary"),
    )(qkn, qkn, qkn, proj, proj)


def _attn_combine(os_, lses, tm=512):
    s = os_[0].shape[0]

    def body(o0, o1, o2, l0, l1, l2, o_ref, lse_ref):
        a, b, c = l0[...], l1[...], l2[...]
        m = jnp.maximum(jnp.maximum(a, b), c)
        ea, eb, ec = jnp.exp(a - m), jnp.exp(b - m), jnp.exp(c - m)
        tot = ea + eb + ec
        o_ref[...] = ((ea * o0[...] + eb * o1[...] + ec * o2[...]) / tot).astype(BF16)
        lse_ref[...] = m + jnp.log(tot)

    return pl.pallas_call(
        body, name="attn_combine", grid=(s // tm,),
        out_shape=[jax.ShapeDtypeStruct((s, COL), BF16), jax.ShapeDtypeStruct((s, COL), F32)],
        in_specs=[_row(tm, COL)] * 6, out_specs=[_row(tm, COL)] * 2,
        compiler_params=_params("parallel"),
    )(*os_, *lses)


def _attn_delta(do, o, tm=512):
    s = do.shape[0]

    def body(do_ref, o_ref, del_ref):
        prod = do_ref[...] * o_ref[...].astype(F32)
        del_ref[...] = _heads(prod, lambda ph, h: jnp.broadcast_to(
            jnp.sum(ph, axis=-1, keepdims=True), ph.shape))

    return pl.pallas_call(
        body, name="attn_delta", grid=(s // tm,),
        out_shape=jax.ShapeDtypeStruct((s, COL), F32),
        in_specs=[_row(tm, COL)] * 2, out_specs=_row(tm, COL),
        compiler_params=_params("parallel"),
    )(do, o)


def _attn_bwd(g, qkn, proj, do, lse, delta, dqn, dkn, dproj):
    s = qkn.shape[0]
    d, tb, sb, pb = _attn_shapes(s, g)
    ft = F32 if d > 1 else BF16
    nj = tb // d // sb
    nt = s // tb
    scale = HD ** -0.5
    chained = dqn is not None

    def body(k_ref, v_ref, qc_ref, qn_ref, doc_ref, don_ref, lc_ref, ln_ref, dc_ref, dn_ref, *rest):
        dq_ref, dk_ref, dv_ref, kf, vf, qf, dvf, later = rest[-8:]
        n = pl.program_id(1)
        kf[...] = k_ref[...].astype(ft)
        vf[...] = v_ref[...].astype(ft)
        qf[0:tb] = qc_ref[...].astype(ft)
        qf[tb:] = qn_ref[...].astype(ft)

        @pl.when(n == 0)
        def _():
            later[...] = jnp.zeros_like(later)

        def window(c_ref, n_ref, r, j):
            at = j * sb * d + r
            if j < nj - 1:
                return c_ref[_every(at, sb + BAND, d), :]
            return jnp.concatenate([c_ref[_every(at, sb, d), :], n_ref[_every(r, BAND, d), :]], axis=0)

        for r in range(d):
            tail = later[r]
            for j in range(nj):
                at = j * sb * d + r
                rows = _every(at, sb, d)
                k = kf[rows, :].astype(BF16)
                v = vf[rows, :].astype(BF16)
                q = qf[_every(at, sb + BAND, d), :].astype(BF16)
                dov = window(doc_ref, don_ref, r, j).astype(BF16)
                sc = lax.dot_general(q, k, NT_DIMS, preferred_element_type=F32) * scale
                qi = lax.broadcasted_iota(jnp.int32, sc.shape, 0)
                kj = lax.broadcasted_iota(jnp.int32, sc.shape, 1)
                valid = (qi >= kj) & (qi <= kj + BAND)
                if j == nj - 1:
                    valid = valid & ((qi < sb) | (n < nt - 1))
                p = jnp.exp(jnp.where(valid, sc - _lanes(window(lc_ref, ln_ref, r, j), sb), -1e30))
                dp = lax.dot_general(dov, v, NT_DIMS, preferred_element_type=F32)
                ds = (p * (dp - _lanes(window(dc_ref, dn_ref, r, j), sb)) * scale).astype(BF16)
                dvf[rows, :] = lax.dot_general(p.astype(BF16), dov, TN_DIMS, preferred_element_type=F32)
                dk_ref[rows, :] = lax.dot_general(ds, q, TN_DIMS, preferred_element_type=F32)
                dqw = lax.dot_general(ds, k, NN_DIMS, preferred_element_type=F32)
                first = dqw[:BAND] + tail
                dq_ref[rows, :] = first if sb == BAND else jnp.concatenate([first, dqw[BAND:sb]], axis=0)
                tail = dqw[sb:]
            later[r] = tail
        dv_ref[...] = dvf[...].astype(BF16)

    cur, side, tok, tok_side = _attn_specs(g, tb, pb, s, ahead=True)
    anyspec = pl.BlockSpec(memory_space=pl.ANY)
    n_heads_cols = 3 * N_HEADS * HD
    return pl.pallas_call(
        body, name=f"attn_bwd_g{g}", grid=(N_HEADS, nt),
        out_shape=[jax.ShapeDtypeStruct((s, n_heads_cols), F32), jax.ShapeDtypeStruct((s, n_heads_cols), F32),
                   jax.ShapeDtypeStruct((s, IN_W), BF16)],
        in_specs=[cur(K_COL), cur(V_COL), cur(Q_COL), side(Q_COL), tok, tok_side, tok, tok_side,
                  tok, tok_side] + ([anyspec, anyspec] if chained else []) + [anyspec],
        out_specs=[cur(0), cur(0), cur(V_COL)],
        input_output_aliases={10: 0, 11: 1, 12: 2} if chained else {10: 2},
        scratch_shapes=[pltpu.VMEM((tb, HD), ft), pltpu.VMEM((tb, HD), ft),
                        pltpu.VMEM((tb + pb, HD), ft), pltpu.VMEM((tb, HD), F32),
                        pltpu.VMEM((d, BAND, HD), F32)],
        compiler_params=_params("arbitrary", "arbitrary"),
    )(qkn, proj, qkn, qkn, do, do, lse, lse, delta, delta, *([dqn, dkn] if chained else []), dproj)


def _shift_down(x, before, k):
    rolled = pltpu.roll(x, k, 0)
    head = jnp.where(lax.broadcasted_iota(jnp.int32, before.shape, 0) < k, pltpu.roll(before, k, 0), rolled[:8])
    return jnp.concatenate([head, rolled[8:]], axis=0)


def _shift_up(x, after, k):
    rows = x.shape[0]
    rolled = pltpu.roll(x, rows - k, 0)
    tail = jnp.where(lax.broadcasted_iota(jnp.int32, after.shape, 0) >= 8 - k,
                     pltpu.roll(after, 8 - k, 0), rolled[rows - 8:])
    return jnp.concatenate([rolled[:rows - 8], tail], axis=0)


def _conv_fwd(proj, cw, tm=1024):
    s = proj.shape[0]
    r16 = tm // 16

    def body(u_ref, b_ref, c_ref, up_ref, cp_ref, w_ref, z_ref):
        i = pl.program_id(1)
        xc = c_ref[...].astype(F32) * u_ref[...].astype(F32)
        xp = jnp.where(i > 0, cp_ref[8:16, :].astype(F32) * up_ref[8:16, :].astype(F32), 0.0)
        w = w_ref[...]
        conv = _shift_down(xc, xp, 2) * w[0:1] + _shift_down(xc, xp, 1) * w[1:2] + xc * w[2:3]
        z_ref[...] = (b_ref[...].astype(F32) * conv).astype(BF16)

    tile = lambda blk: pl.BlockSpec((tm, COL), lambda j, i: (i, blk + j))
    before = lambda blk: pl.BlockSpec((16, COL), lambda j, i: (jnp.maximum(i * r16 - 1, 0), blk + j))
    return pl.pallas_call(
        body, name="conv_fwd", grid=(D // COL, s // tm),
        out_shape=jax.ShapeDtypeStruct((s, D), BF16),
        in_specs=[tile(U_BLK), tile(B_BLK), tile(C_BLK), before(U_BLK), before(C_BLK),
                  pl.BlockSpec((3, COL), lambda j, i: (0, j))],
        out_specs=pl.BlockSpec((tm, COL), lambda j, i: (i, j)),
        compiler_params=_params("parallel", "parallel"),
    )(proj, proj, proj, proj, proj, cw)


def _conv_bwd(dz, proj, cw, dproj, tm=1024):
    s = proj.shape[0]
    r16 = tm // 16
    nrow = s // tm

    def body(dz_ref, u_ref, b_ref, c_ref, up_ref, cp_ref, dzn_ref, bn_ref, w_ref, _, o_ref, dc_ref, acc_ref):
        piece, i = pl.program_id(1), pl.program_id(2)
        u, c = u_ref[...].astype(F32), c_ref[...].astype(F32)
        bv = b_ref[...].astype(F32)
        dzv = dz_ref[...]
        w = w_ref[...]

        @pl.when((piece == 0) & (i == 0))
        def _():
            acc_ref[...] = jnp.zeros_like(acc_ref)

        @pl.when(piece == 0)
        def _():
            xc = c * u
            xp = jnp.where(i > 0, cp_ref[8:16, :].astype(F32) * up_ref[8:16, :].astype(F32), 0.0)
            x2, x1 = _shift_down(xc, xp, 2), _shift_down(xc, xp, 1)
            o_ref[...] = (dzv * (x2 * w[0:1] + x1 * w[1:2] + xc * w[2:3])).astype(BF16)
            dc_ref[...] = jnp.zeros_like(dc_ref)
            dconv = dzv * bv
            acc_ref[0:1, :] += jnp.sum(dconv * x2, axis=0, keepdims=True)
            acc_ref[1:2, :] += jnp.sum(dconv * x1, axis=0, keepdims=True)
            acc_ref[2:3, :] += jnp.sum(dconv * xc, axis=0, keepdims=True)

        @pl.when(piece == 1)
        def _():
            dconv = dzv * bv
            dn = jnp.where(i < nrow - 1, dzn_ref[...] * bn_ref[0:8, :].astype(F32), 0.0)
            dxc = dconv * w[2:3] + _shift_up(dconv, dn, 1) * w[1:2] + _shift_up(dconv, dn, 2) * w[0:1]
            o_ref[...] = (dxc * c).astype(BF16)
            dc_ref[...] = (dxc * u).astype(BF16)

    tile = lambda blk: pl.BlockSpec((tm, COL), lambda j, p, i: (i, blk + j))
    before = lambda blk: pl.BlockSpec((16, COL), lambda j, p, i: (jnp.maximum(i * r16 - 1, 0), blk + j))
    after = lambda rows, blk: pl.BlockSpec(
        (rows, COL), lambda j, p, i: (jnp.minimum((i + 1) * (tm // rows), s // rows - 1), blk + j))
    return pl.pallas_call(
        body, name="conv_bwd", grid=(D // COL, 2, nrow),
        out_shape=[jax.ShapeDtypeStruct((s, IN_W), BF16), jax.ShapeDtypeStruct((s + tm, D), BF16),
                   jax.ShapeDtypeStruct((8, D), F32)],
        in_specs=[tile(0), tile(U_BLK), tile(B_BLK), tile(C_BLK), before(U_BLK), before(C_BLK),
                  after(8, 0), after(16, B_BLK), pl.BlockSpec((3, COL), lambda j, p, i: (0, j)),
                  pl.BlockSpec(memory_space=pl.ANY)],
        out_specs=[pl.BlockSpec((tm, COL), lambda j, p, i: (i, jnp.where(p == 0, B_BLK, U_BLK) + j)),
                   pl.BlockSpec((tm, COL), lambda j, p, i: (jnp.where(p == 0, nrow, i), j)),
                   pl.BlockSpec((8, COL), lambda j, p, i: (0, j))],
        input_output_aliases={9: 0},
        compiler_params=_params("arbitrary", "arbitrary", "arbitrary"),
    )(dz, proj, proj, proj, proj, proj, dz, proj, cw, dproj)


def _copy_columns(name, src, dst, blk0, tm=1024):
    s, w = dst.shape[0], src.shape[1]
    fresh = isinstance(dst, jax.ShapeDtypeStruct)

    def body(x_ref, *rest):
        rest[-1][...] = x_ref[...]

    return pl.pallas_call(
        body, name=name, grid=(w // COL, s // tm),
        out_shape=jax.ShapeDtypeStruct(dst.shape, dst.dtype),
        in_specs=[pl.BlockSpec((tm, COL), lambda j, i: (i, j))] + ([] if fresh else [pl.BlockSpec(memory_space=pl.ANY)]),
        out_specs=pl.BlockSpec((tm, COL), lambda j, i: (i, blk0 + j)),
        input_output_aliases={} if fresh else {1: 0},
        compiler_params=_params("parallel", "parallel"),
    )(src, *([] if fresh else [dst]))


def _merge_fwd(ya, yc, proj, tm=512):
    s = proj.shape[0]

    def body(ya_ref, yc_ref, ga_ref, gc_ref, o_ref):
        o_ref[...] = (_sigmoid(ga_ref[...].astype(F32)) * ya_ref[...].astype(F32)
                      + _sigmoid(gc_ref[...].astype(F32)) * yc_ref[...].astype(F32)).astype(BF16)

    tile = lambda blk: pl.BlockSpec((tm, COL), lambda j, i: (i, blk + j))
    return pl.pallas_call(
        body, name="merge_fwd", grid=(D // COL, s // tm),
        out_shape=jax.ShapeDtypeStruct((s, D), BF16),
        in_specs=[tile(0), tile(0), tile(GA_BLK), tile(GC_BLK)], out_specs=tile(0),
        compiler_params=_params("parallel", "parallel"),
    )(ya, yc, proj, proj)


def _merge_bwd_branches(dm, proj, tm=512):
    s = proj.shape[0]

    def body(dm_ref, ga_ref, gc_ref, dya_ref, dyc_ref):
        dmv = dm_ref[...]
        dya_ref[...] = (dmv * _sigmoid(ga_ref[...].astype(F32))).astype(BF16)
        dyc_ref[...] = (dmv * _sigmoid(gc_ref[...].astype(F32))).astype(BF16)

    tile = lambda blk: pl.BlockSpec((tm, COL), lambda j, i: (i, blk + j))
    return pl.pallas_call(
        body, name="merge_bwd_branches", grid=(D // COL, s // tm),
        out_shape=[jax.ShapeDtypeStruct((s, D), BF16)] * 2,
        in_specs=[tile(0), tile(GA_BLK), tile(GC_BLK)], out_specs=[tile(0)] * 2,
        compiler_params=_params("parallel", "parallel"),
    )(dm, proj, proj)


def _merge_bwd_gates(dm, ya, yc, proj, tm=1024):
    s = proj.shape[0]
    half = D // COL

    def body(dm_ref, ya_ref, yc_ref, g_ref, o_ref):
        y = jnp.where(pl.program_id(0) < half, ya_ref[...].astype(F32), yc_ref[...].astype(F32))
        sig = _sigmoid(g_ref[...].astype(F32))
        o_ref[...] = (dm_ref[...] * y * sig * (1.0 - sig)).astype(BF16)

    chan = pl.BlockSpec((tm, COL), lambda jj, i: (i, jj % half))
    gate = pl.BlockSpec((tm, COL), lambda jj, i: (i, GA_BLK + jj))
    return pl.pallas_call(
        body, name="merge_bwd_gates", grid=(2 * half, s // tm),
        out_shape=jax.ShapeDtypeStruct((s, IN_W), BF16),
        in_specs=[chan, chan, chan, gate], out_specs=gate,
        compiler_params=_params("parallel", "parallel"),
    )(dm, ya, yc, proj)


def _mod_part(c_all, w_ada, b_part):
    def body(c_ref, w_ref, b_ref, o_ref):
        cv = c_ref[...]
        act = cv * _sigmoid(cv)
        o_ref[...] = jnp.dot(act, w_ref[...], preferred_element_type=F32,
                             precision=lax.Precision.HIGHEST) + b_ref[...]

    return pl.pallas_call(
        body, name="mod_part", out_shape=jax.ShapeDtypeStruct((N_DEV, w_ada.shape[1]), F32),
    )(c_all, w_ada, b_part)


def _w_ada_grad(c_all_t, dmod_part):
    def body(c_ref, d_ref, o_ref):
        cv = c_ref[...]
        act = cv * _sigmoid(cv)
        dv = d_ref[...]
        acc = act[:, 0:1] * dv[0:1, :]
        for b in range(1, N_DEV):
            acc = acc + act[:, b:b + 1] * dv[b:b + 1, :]
        o_ref[...] = acc

    return pl.pallas_call(
        body, name="w_ada_grad", out_shape=jax.ShapeDtypeStruct((D, dmod_part.shape[1]), F32),
    )(c_all_t, dmod_part)


def _sum_rows(name, v):
    def body(v_ref, o_ref):
        acc = v_ref[0]
        for k in range(1, N_DEV):
            acc = acc + v_ref[k]
        o_ref[...] = acc

    return pl.pallas_call(body, name=name, out_shape=jax.ShapeDtypeStruct(v.shape[1:], F32))(v)


def _adamw(name, w, g, m, v):
    rows, cols = w.shape
    limit = max(16, (1 << 20) // (4 * cols))
    tr = rows if rows <= limit else next((t for t in range(limit - limit % 16, 15, -16) if rows % t == 0), rows)
    c1 = 1.0 - ADAM_B1 ** ADAM_STEP
    c2 = 1.0 - ADAM_B2 ** ADAM_STEP
    parts = g.ndim == 3

    def body(w_ref, g_ref, m_ref, v_ref, go_ref, d_ref, nm_ref, nv_ref):
        if parts:
            gv = g_ref[0].astype(F32)
            for k in range(1, N_DEV):
                gv = gv + g_ref[k].astype(F32)
        else:
            gv = g_ref[...]
        go_ref[...] = gv
        nm = ADAM_B1 * m_ref[...] + (1.0 - ADAM_B1) * gv
        nv = ADAM_B2 * v_ref[...] + (1.0 - ADAM_B2) * (gv * gv)
        nm_ref[...] = nm
        nv_ref[...] = nv
        d_ref[...] = -ADAM_LR * ((nm / c1) / (jnp.sqrt(nv / c2) + ADAM_EPS) + ADAM_WD * w_ref[...])

    spec = pl.BlockSpec((tr, cols), lambda i: (i, 0))
    g_spec = pl.BlockSpec((N_DEV, tr, cols), lambda i: (0, i, 0)) if parts else spec
    return pl.pallas_call(
        body, name=name, grid=(rows // tr,),
        out_shape=[jax.ShapeDtypeStruct((rows, cols), F32)] * 4,
        in_specs=[spec, g_spec, spec, spec], out_specs=[spec] * 4,
        compiler_params=_params("parallel"),
    )(w, g, m, v)


def _adamw_small(ws, gs, ms, vs):
    n = len(ws)
    c1 = 1.0 - ADAM_B1 ** ADAM_STEP
    c2 = 1.0 - ADAM_B2 ** ADAM_STEP

    def body(*refs):
        for i in range(n):
            w_ref, g_ref, m_ref, v_ref = refs[i], refs[n + i], refs[2 * n + i], refs[3 * n + i]
            d_ref, nm_ref, nv_ref = refs[4 * n + 3 * i:4 * n + 3 * i + 3]
            gv = g_ref[...]
            nm = ADAM_B1 * m_ref[...] + (1.0 - ADAM_B1) * gv
            nv = ADAM_B2 * v_ref[...] + (1.0 - ADAM_B2) * (gv * gv)
            nm_ref[...] = nm
            nv_ref[...] = nv
            d_ref[...] = -ADAM_LR * ((nm / c1) / (jnp.sqrt(nv / c2) + ADAM_EPS) + ADAM_WD * w_ref[...])

    outs = pl.pallas_call(
        body, name="adamw_small",
        out_shape=[jax.ShapeDtypeStruct(w.shape, F32) for w in ws for _ in range(3)],
    )(*ws, *gs, *ms, *vs)
    return [tuple(outs[3 * i:3 * i + 3]) for i in range(n)]


HALF = FF // 2


def _sds(shape, dtype):
    return jax.ShapeDtypeStruct(shape, dtype)


def _row_tile(w):
    return lambda tm: ((tm, w), lambda i, j: (i, 0))


def _one(w):
    return lambda rows: ((rows, w), lambda i, j: (0, 0))


def _gate_up_swiglu(name, h, wgu, carry=None, tm=512):
    s = h.shape[0]
    tm = min(tm, s)

    def epilogue(prod, first, tin, tout):
        pq_ref, s_ref = tout
        a, b = prod[:, :HALF], prod[:, HALF:]
        sig = _sigmoid(a)
        act = a * sig
        pq_ref[:, :HALF] = (b * (sig * (1.0 + a * (1.0 - sig)))).astype(BF16)
        pq_ref[:, HALF:] = act.astype(BF16)
        s_ref[...] = (act * b).astype(BF16)

    return _mm(name, h, wgu, "NT", None, tm, FF, D, carry=carry, n_outer=True, epilogue=epilogue,
               tiles_out=[(_sds((s, 2 * FF), BF16), (tm, FF), lambda i, j: (i, j)),
                          (_sds((s, FF), BF16), (tm, HALF), lambda i, j: (i, j))])


def _d_hidden_swiglu(name, df, wd, ab, after=(), tm=512):
    s = df.shape[0]
    tm = min(tm, s)

    def epilogue(prod, first, tin, tout, cols):
        da_cols = slice(cols[0], cols[0] + cols[1])
        db_cols = slice(HALF + cols[0], HALF + cols[0] + cols[1])
        tout[0][:, da_cols] = (prod * tin[0][:, da_cols].astype(F32)).astype(BF16)
        tout[0][:, db_cols] = (prod * tin[0][:, db_cols].astype(F32)).astype(BF16)

    chunks = [(c0, min(384, HALF - c0)) for c0 in range(0, HALF, 384)]
    return _mm(name, df, wd, "NT", None, tm, HALF, D, n_outer=True, epilogue=epilogue, col_chunks=chunks, after=after,
               tiles_in=[(ab, (tm, FF), lambda i, j: (i, j))],
               tiles_out=[(_sds((s, 2 * FF), BF16), (tm, FF), lambda i, j: (i, j))])[0]


def _out_residual(name, a, w, x, gt, coef, nxt, tm=512, tk=FF):
    s = a.shape[0]
    tm = min(tm, s)

    def epilogue(prod, first, tin, tout):
        x_ref, gt_ref, g_ref, sc_ref, sh_ref = tin
        f_ref, xn_ref, h_ref = tout
        f_ref[...] = prod
        xn = x_ref[...] + (coef * gt_ref[...]) * prod
        xn_ref[...] = xn
        r = lax.rsqrt(jnp.mean(xn * xn, axis=-1, keepdims=True) + EPS)
        h_ref[...] = ((xn * r) * g_ref[...] * (1.0 + sc_ref[...]) + sh_ref[...]).astype(BF16)

    row, vec = _row_tile(D)(tm), _one(D)(1)
    return _mm(name, a, w, "NN", None, tm, D, tk, epilogue=epilogue,
               tiles_in=[(x, *row), (gt, *vec)] + [(v, *vec) for v in nxt],
               tiles_out=[(_sds((s, D), F32), *row), (_sds((s, D), F32), *row), (_sds((s, D), BF16), *row)])


def _out_loss(name, a, w, x, gt, coef, target, tm=512):
    s = a.shape[0]
    tm = min(tm, s)

    def epilogue(prod, first, tin, tout):
        x_ref, gt_ref, t_ref = tin
        f_ref, g_ref, df_ref, acc_ref = tout
        f_ref[...] = prod
        cg = coef * gt_ref[...]
        e = x_ref[...] + cg * prod - t_ref[...]
        gv = e * (1.0 / D)
        g_ref[...] = gv
        df_ref[...] = (cg * gv).astype(BF16)

        @pl.when(first)
        def _():
            acc_ref[...] = jnp.zeros_like(acc_ref)

        acc_ref[0:1, :] += coef * jnp.sum(gv * prod, axis=0, keepdims=True)
        acc_ref[1:2, :] += (0.5 / D) * jnp.sum(e * e, axis=0, keepdims=True)

    row, vec = _row_tile(D)(tm), _one(D)(1)
    return _mm(name, a, w, "NN", None, tm, D, FF, epilogue=epilogue,
               tiles_in=[(x, *row), (gt, *vec), (target, *row)],
               tiles_out=[(_sds((s, D), F32), *row), (_sds((s, D), F32), *row), (_sds((s, D), BF16), *row),
                          (_sds((8, D), F32), *_one(D)(8))])


def _d_h_norm_bwd(name, da, w, x, gin, g, sc, sh, before=None, carry=None, after=(), tm=256):
    s = da.shape[0]
    tm = min(tm, s)
    coef = before[2] if before else None

    def epilogue(prod, first, tin, tout):
        x_ref, gin_ref, g_ref, sc_ref, sh_ref = tin[:5]
        gout_ref, acc_ref = tout[:2]
        xv = x_ref[...]
        r = lax.rsqrt(jnp.mean(xv * xv, axis=-1, keepdims=True) + EPS)
        nv = xv * r
        gv, one_sc = g_ref[...], 1.0 + sc_ref[...]
        dn = prod * gv * one_sc
        gout = gin_ref[...] + r * (dn - nv * jnp.mean(dn * nv, axis=-1, keepdims=True))
        gout_ref[...] = gout

        @pl.when(first)
        def _():
            acc_ref[...] = jnp.zeros_like(acc_ref)

        dhn = prod * nv
        acc_ref[0:1, :] += jnp.sum(prod, axis=0, keepdims=True)
        acc_ref[1:2, :] += jnp.sum(dhn * gv, axis=0, keepdims=True)
        acc_ref[2:3, :] += jnp.sum(dhn * one_sc, axis=0, keepdims=True)
        if before:
            f_ref, gt_ref = tin[5:]
            tout[2][...] = ((coef * gt_ref[...]) * gout).astype(BF16)
            acc_ref[3:4, :] += coef * jnp.sum(gout * f_ref[...], axis=0, keepdims=True)

    row, vec = _row_tile(D)(tm), _one(D)(1)
    tiles_in = [(x, *row), (gin, *row), (g, *vec), (sc, *vec), (sh, *vec)]
    tiles_out = [(_sds((s, D), F32), *row), (_sds((8, D), F32), *_one(D)(8))]
    if before:
        tiles_in += [(before[0], *row), (before[1], *vec)]
        tiles_out.append((_sds((s, D), BF16), *row))
    return _mm(name, da, w, "NN", None, tm, D, da.shape[1], epilogue=epilogue, carry=carry, keep_b=True, after=after,
               tiles_in=tiles_in, tiles_out=tiles_out)


def _gate_tiles(proj, tm):
    return [(proj, (tm, COL), (lambda i, j, blk=blk: (i, blk))) for blk in (GA_BLK, GA_BLK + 1, GC_BLK, GC_BLK + 1)]


def _conv_branch_merge(z, wc, ya, proj, tm=512):
    s = z.shape[0]
    tm = min(tm, s)

    def epilogue(prod, first, tin, tout):
        ya_ref, ga0, ga1, gc0, gc1 = tin
        tout[0][...] = prod.astype(BF16)
        for half, (ga, gc) in enumerate(((ga0, gc0), (ga1, gc1))):
            cols = slice(half * COL, (half + 1) * COL)
            tout[1][:, cols] = (_sigmoid(ga[...].astype(F32)) * ya_ref[:, cols].astype(F32)
                                + _sigmoid(gc[...].astype(F32)) * prod[:, cols]).astype(BF16)

    row = _row_tile(D)(tm)
    return _mm("mix_conv_branch", z, wc, "NN", None, tm, D, D, epilogue=epilogue,
               tiles_in=[(ya, *row)] + _gate_tiles(proj, tm),
               tiles_out=[(_sds((s, D), BF16), *row), (_sds((s, D), BF16), *row)])


def _d_merged_branches(dmix, wo, ya, yc, proj, tm=512):
    s = dmix.shape[0]
    tm = min(tm, s)

    def epilogue(prod, first, tin, tout):
        ya_ref, yc_ref, ga0, ga1, gc0, gc1 = tin
        dya_ref, dyc_ref, dg_ref = tout
        for half, (ga, gc) in enumerate(((ga0, gc0), (ga1, gc1))):
            cols = slice(half * COL, (half + 1) * COL)
            dm = prod[:, cols]
            for y_ref, g_ref, dy_ref, off in ((ya_ref, ga, dya_ref, 0), (yc_ref, gc, dyc_ref, D)):
                sig = _sigmoid(g_ref[...].astype(F32))
                dms = dm * sig
                dy_ref[:, cols] = dms.astype(BF16)
                dg_ref[:, off + half * COL:off + (half + 1) * COL] = (
                    dms * y_ref[:, cols].astype(F32) * (1.0 - sig)).astype(BF16)

    row = _row_tile(D)(tm)
    return _mm("mix_d_merged", dmix, wo, "NT", None, tm, D, D, epilogue=epilogue,
               tiles_in=[(ya, *row), (yc, *row)] + _gate_tiles(proj, tm),
               tiles_out=[(_sds((s, D), BF16), *row), (_sds((s, D), BF16), *row),
                          (_sds((s, 2 * D), BF16), *_row_tile(2 * D)(tm))])


def _d_o_delta(dya, wa_t, o, tm=1024):
    s = dya.shape[0]
    tm = min(tm, s)

    def epilogue(prod, first, tin, tout):
        tout[0][...] = prod
        tout[1][...] = _heads(prod * tin[0][...].astype(F32), lambda ph, h: jnp.broadcast_to(
            jnp.sum(ph, axis=-1, keepdims=True), ph.shape))

    row = _row_tile(COL)(tm)
    return _mm("mix_d_o", dya, wa_t, "NN", None, tm, COL, D, epilogue=epilogue,
               tiles_in=[(o, *row)], tiles_out=[(_sds((s, COL), F32), *row), (_sds((s, COL), F32), *row)])


def _ffn_bwd(tag, df, x, gin, h, ab, sw, g, sc, sh, wgu, wd, before=None, tk_dw=2048):
    dwd = _mm(f"{tag}_dw_down", sw, df, "TN", BF16, HALF, D, tk_dw)
    dab = _d_hidden_swiglu(f"{tag}_d_hidden", df, wd, ab, after=[dwd])
    dwgu = _mm(f"{tag}_dw_gate_up", dab, h, "TN", BF16, HALF, D, tk_dw)
    res = _d_h_norm_bwd(f"{tag}_d_h", dab, wgu, x, gin, g, sc, sh, before=before, after=[dwgu])
    return res, dwgu, dwd


def kernel(x, c, w_ada, b_ada, norm_ffn1, ffn1_w_gate, ffn1_w_up, ffn1_w_down, norm_mix, w_in, q_norm, k_norm, conv_w, w_attn_branch, w_conv_branch, w_out, norm_ffn2, ffn2_w_gate, ffn2_w_up, ffn2_w_down, loss_target, m_w_ada, m_b_ada, m_norm_ffn1, m_ffn1_w_gate, m_ffn1_w_up, m_ffn1_w_down, m_norm_mix, m_w_in, m_q_norm, m_k_norm, m_conv_w, m_w_attn_branch, m_w_conv_branch, m_w_out, m_norm_ffn2, m_ffn2_w_gate, m_ffn2_w_up, m_ffn2_w_down, v_w_ada, v_b_ada, v_norm_ffn1, v_ffn1_w_gate, v_ffn1_w_up, v_ffn1_w_down, v_norm_mix, v_w_in, v_q_norm, v_k_norm, v_conv_w, v_w_attn_branch, v_w_conv_branch, v_w_out, v_norm_ffn2, v_ffn2_w_gate, v_ffn2_w_up, v_ffn2_w_down):
    me = 4 * lax.axis_index("x") + 2 * lax.axis_index("y") + lax.axis_index("c")
    x0, target = x[0], loss_target[0]
    s = x0.shape[0]
    ada_cols = w_ada.shape[2]
    cw_cols = conv_w.shape[2]

    gathered = _small_allgather(
        "gather_c_conv", jnp.concatenate([c, conv_w[0].reshape(1, 3 * cw_cols)], axis=1))[:, 0]
    c_all = gathered[:, :D]
    cw = gathered[:, D:].reshape(N_DEV, 3, cw_cols).transpose(1, 0, 2).reshape(3, D)
    b_part = lax.dynamic_slice(b_ada, (0, me * ada_cols), (1, ada_cols))
    mod_part = _mod_part(c_all, w_ada[0], b_part)
    mod_all = _small_allgather("gather_mod", mod_part.reshape(1, N_DEV * ada_cols))
    mod = lax.dynamic_slice(mod_all.reshape(N_DEV, N_DEV, ada_cols), (0, me, 0), (N_DEV, 1, ada_cols))
    mod = mod.reshape(N_MOD, 1, D)
    sh1, sc1, gt1, sh2, sc2, gt2, sh3, sc3, gt3 = [mod[i] for i in range(N_MOD)]

    tb = lambda w: w[0].T.astype(BF16)
    nb = lambda w: w[0].astype(BF16)
    ffn1_shards = [tb(ffn1_w_gate), tb(ffn1_w_up), nb(ffn1_w_down)]
    ffn2_shards = [tb(ffn2_w_gate), tb(ffn2_w_up), nb(ffn2_w_down)]
    mix_shards = [tb(w_in), tb(w_attn_branch), nb(w_conv_branch), nb(w_out)]
    ffn_dst, ffn_base, ffn_jump, ffn_shapes = [0, 0, 1], [0, HALF, 0], [HALF, HALF, 0], [(2 * FF, D), (FF, D)]
    mix_dst, mix_base, mix_shapes = [0, 1, 2, 3], [0, 0, 0, 0], [(IN_W, D), (D, COL), (D, D), (D, D)]
    (wgu1,) = _run_plan_on_sequencer(
        "gather_ffn1_gate_up", _gather_plan(ffn1_shards[:2], ffn_dst[:2], ffn_base[:2], ffn_shapes[:1], ffn_jump[:2]), 1)
    (wd1,) = _run_plan_on_sequencer(
        "gather_ffn1_down", _gather_plan(ffn1_shards[2:], [0], [0], ffn_shapes[1:]), 8)
    win_t, wa_t, wc, wo = _run_plan_on_sequencer(
        "gather_mix_weights", _gather_plan(mix_shards, mix_dst, mix_base, mix_shapes), 2)
    wgu2, wd2 = _run_plan_on_sequencer(
        "gather_ffn2_weights", _gather_plan(ffn2_shards, ffn_dst, ffn_base, ffn_shapes, ffn_jump), 3)

    h1 = _normmod("ffn1_normmod", x0, norm_ffn1, sc1, sh1)
    ab1, s1 = _gate_up_swiglu("ffn1_gate_up", h1, wgu1)
    f1, x1, h2 = _out_residual("ffn1_down", s1, wd1, x0, gt1, 0.5, (norm_mix, sc2, sh2))
    proj = _mm("mix_in_proj", h2, win_t, "NT", BF16, 1024, IN_W // 4, D, n_outer=True)
    wqk = jnp.concatenate([jnp.tile(q_norm, (1, 12)), jnp.tile(k_norm, (1, 12))], axis=1)
    qkn = _qknorm(proj, wqk)
    group_out = [_attn_fwd(g, qkn, proj) for g in range(3)]
    o, lse = _attn_combine([go[0] for go in group_out], [go[1] for go in group_out])
    ya = _mm("mix_attn_branch", o, wa_t, "NT", BF16, 1024, 1024, COL)
    z = _conv_fwd(proj, cw)
    yc, merged = _conv_branch_merge(z, wc, ya, proj)
    mix, x2, h3 = _out_residual("mix_out_proj", merged, wo, x1, gt2, 1.0, (norm_ffn2, sc3, sh3), tk=D)
    ab3, s3 = _gate_up_swiglu("ffn2_gate_up", h3, wgu2)
    f3, g3, df3, acc_out = _out_loss("ffn2_down", s3, wd2, x2, gt3, 0.5, target)
    loss_part = jnp.sum(acc_out[1])

    ffn_rows = [sh_.shape[0] for sh_ in ffn1_shards]
    mix_rows = [sh_.shape[0] for sh_ in mix_shards]
    (g2, acc3, dmix), dwgu2, dwd2 = _ffn_bwd(
        "ffn2", df3, x2, g3, h3, ab3, s3, norm_ffn2, sc3, sh3, wgu2, wd2, before=(mix, gt2, 1.0))
    dya, dyc, dgates = _d_merged_branches(dmix, wo, ya, yc, proj)
    dwo = _mm("mix_dw_out", merged, dmix, "TN", BF16, 1024, 1024, 2048)
    dproj = _copy_columns("dproj_gates", dgates, jax.ShapeDtypeStruct((s, IN_W), BF16), GA_BLK)
    dwc = _mm("mix_dw_conv_branch", z, dyc, "TN", BF16, 1024, 1024, 2048)
    dz = _mm("mix_d_z", dyc, wc, "NT", F32, 1024, 1024, D)
    dproj, d_c, cw_acc = _conv_bwd(dz, proj, cw, dproj)
    dproj = _copy_columns("copy_d_c", d_c, dproj, C_BLK)
    dwa_t = _mm("mix_dw_attn_branch", dya, o, "TN", BF16, 1024, COL, 2048)
    do, delta = _d_o_delta(dya, wa_t, o)
    dqn = dkn = None
    for g in range(3):
        dqn, dkn, dproj = _attn_bwd(g, qkn, proj, do, lse, delta, dqn, dkn, dproj)
    dproj, wq_acc = _qknorm_bwd("qnorm_bwd", proj, dqn, wqk[:, :QKW // 2], dproj, 0)
    dproj, wk_acc = _qknorm_bwd("knorm_bwd", proj, dkn, wqk[:, QKW // 2:], dproj, QKW // 2 // COL)
    r_f2g, r_f2u, r_f2d, r_wa, r_wc, r_wo = _run_plan_on_sequencer(
        "scatter_ffn2_and_branch_grads",
        _scatter_plan([dwgu2, dwd2, dwa_t, dwc, dwo], [0, 0, 1, 2, 3, 4], [0, HALF, 0, 0, 0, 0],
                      ffn_rows + mix_rows[1:], [D, D, D, COL, D, D], [HALF, HALF, 0, 0, 0, 0]), 4)
    dwin_t = _mm("mix_dw_in", dproj, h2, "TN", BF16, IN_W // 4, COL, 2048)
    (r_win,) = _run_plan_on_sequencer(
        "scatter_w_in_grad", _scatter_plan([dwin_t], [0], [0], mix_rows[:1], [D]), 5)
    g1, acc2, df1 = _d_h_norm_bwd("mix_d_h", dproj, win_t, x1, g2, norm_mix, sc2, sh2, before=(f1, gt1, 0.5),
                                  after=[dwin_t])
    (g0, acc1), dwgu1, dwd1 = _ffn_bwd(
        "ffn1", df1, x0, g1, h1, ab1, s1, norm_ffn1, sc1, sh1, wgu1, wd1)
    (r_f1d,) = _run_plan_on_sequencer(
        "scatter_ffn1_down_grad", _scatter_plan([dwd1], [0], [0], ffn_rows[2:], [D]), 6)
    r_f1g, r_f1u = _run_plan_on_sequencer(
        "scatter_ffn1_gate_up_grads",
        _scatter_plan([dwgu1], [0, 0], [0, HALF], ffn_rows[:2], [D, D], [HALF, HALF]), 7)

    dqw = jnp.sum(wq_acc[0].reshape(12, HD), axis=0)
    dkw = jnp.sum(wk_acc[0].reshape(12, HD), axis=0)
    small = jnp.concatenate([
        acc1[0], acc1[1], acc2[3], acc2[0], acc2[1], acc3[3], acc3[0], acc3[1], acc_out[0],
        acc1[2], acc2[2], acc3[2], dqw, dkw, cw_acc[0:3].reshape(3 * D),
        jnp.zeros((HD,), F32).at[0].set(loss_part)]).reshape(1, -1)
    small_all = _small_allgather("gather_small_grads", small)
    small_sum = _sum_rows("sum_small_grads", small_all)[0]
    n_mod = N_MOD * D
    g_b_ada = small_sum[:n_mod].reshape(1, n_mod)
    g_norm1, g_norm2, g_norm3 = [small_sum[n_mod + i * D:n_mod + (i + 1) * D].reshape(1, D) for i in range(3)]
    off = n_mod + 3 * D
    g_qn, g_kn = small_sum[off:off + HD].reshape(1, HD), small_sum[off + HD:off + 2 * HD].reshape(1, HD)
    g_cw_full = small_sum[off + 2 * HD:off + 2 * HD + 3 * D].reshape(3, D)
    loss = small_sum[off + 2 * HD + 3 * D]
    g_cw = lax.dynamic_slice(g_cw_full, (0, me * cw_cols), (3, cw_cols))
    dmod_part = lax.dynamic_slice(small_all[:, 0, :n_mod], (0, me * ada_cols), (N_DEV, ada_cols))
    g_w_ada = _w_ada_grad(c_all.T, dmod_part)

    as_rows = {"ffn1_w_gate", "ffn1_w_up", "w_in", "w_attn_branch", "ffn2_w_gate", "ffn2_w_up"}
    grad_list = [g_w_ada, g_b_ada, g_norm1, r_f1g, r_f1u, r_f1d, g_norm2, r_win,
                 g_qn, g_kn, g_cw, r_wa, r_wc, r_wo, g_norm3, r_f2g, r_f2u, r_f2d]
    weights = [w_ada, b_ada, norm_ffn1, ffn1_w_gate, ffn1_w_up, ffn1_w_down, norm_mix, w_in, q_norm, k_norm,
               conv_w, w_attn_branch, w_conv_branch, w_out, norm_ffn2, ffn2_w_gate, ffn2_w_up, ffn2_w_down]
    ms = [m_w_ada, m_b_ada, m_norm_ffn1, m_ffn1_w_gate, m_ffn1_w_up, m_ffn1_w_down, m_norm_mix, m_w_in, m_q_norm,
          m_k_norm, m_conv_w, m_w_attn_branch, m_w_conv_branch, m_w_out, m_norm_ffn2, m_ffn2_w_gate,
          m_ffn2_w_up, m_ffn2_w_down]
    vs = [v_w_ada, v_b_ada, v_norm_ffn1, v_ffn1_w_gate, v_ffn1_w_up, v_ffn1_w_down, v_norm_mix, v_w_in, v_q_norm,
          v_k_norm, v_conv_w, v_w_attn_branch, v_w_conv_branch, v_w_out, v_norm_ffn2, v_ffn2_w_gate,
          v_ffn2_w_up, v_ffn2_w_down]
    wnames = ["w_ada", "b_ada", "norm_ffn1", "ffn1_w_gate", "ffn1_w_up", "ffn1_w_down", "norm_mix", "w_in",
              "q_norm", "k_norm", "conv_w", "w_attn_branch", "w_conv_branch", "w_out", "norm_ffn2",
              "ffn2_w_gate", "ffn2_w_up", "ffn2_w_down"]
    small = [i for i, gr in enumerate(grad_list) if gr.ndim == 2 and gr.size <= 16384]
    flat = lambda a, i: a.reshape(-1, weights[i].shape[-1])
    small_res = dict(zip(small, _adamw_small(
        [flat(weights[i], i) for i in small], [flat(grad_list[i], i) for i in small],
        [flat(ms[i], i) for i in small], [flat(vs[i], i) for i in small])))
    grad_out, deltas, new_ms, new_vs = [], [], [], []
    for idx, (nm, w, gr, m_, v_) in enumerate(zip(wnames, weights, grad_list, ms, vs)):
        if idx in small_res:
            gr, dl, nm_, nv_ = [r.reshape(w.shape) for r in (gr, *small_res[idx])]
        elif nm in as_rows:
            res = _adamw(f"adamw_{nm}", w[0].T, gr, m_[0].T, v_[0].T)
            gr, dl, nm_, nv_ = [r.T[None] for r in res]
        else:
            two_d = (-1, w.shape[-1])
            res = _adamw(f"adamw_{nm}", w.reshape(two_d), gr if gr.ndim == 3 else gr.reshape(two_d),
                         m_.reshape(two_d), v_.reshape(two_d))
            gr, dl, nm_, nv_ = [r.reshape(w.shape) for r in res]
        grad_out.append(gr)
        deltas.append(dl)
        new_ms.append(nm_)
        new_vs.append(nv_)
    return (loss, g0[None], *grad_out, *deltas, *new_ms, *new_vs)
```

```python
import functools

import jax
import jax.numpy as jnp
from jax import lax
from jax.experimental import pallas as pl
from jax.experimental.pallas import tpu as pltpu
from jax.experimental.pallas import tpu_sc as plsc

F32 = jnp.float32
BF16 = jnp.bfloat16
MESH = pl.DeviceIdType.MESH

N_DEV = 8
D = 1024
FF = 2816
HD = 128
N_HEADS = 4
DILATIONS = (1, 4, 16)
BAND = 128
QKW = 2 * 3 * N_HEADS * HD
IN_W = 9728
COL = 512
V_BLK, U_BLK, B_BLK, C_BLK, GA_BLK, GC_BLK = 6, 9, 11, 13, 15, 17
EPS = 1e-6
N_MOD = 9
ADAM_LR, ADAM_B1, ADAM_B2, ADAM_EPS, ADAM_WD, ADAM_STEP = 0.001, 0.9, 0.999, 1e-08, 0.01, 10

NT_DIMS = (((1,), (1,)), ((), ()))
TN_DIMS = (((0,), (0,)), ((), ()))
NN_DIMS = (((1,), (0,)), ((), ()))


def _place():
    return lax.axis_index("x"), lax.axis_index("y"), lax.axis_index("c")


def _flip(coord, bit):
    return 1 - coord if bit else coord


def _params(*sem):
    return pltpu.CompilerParams(dimension_semantics=sem)


def _small_allgather(name, v):
    n = v.shape[-1]

    def body(v_ref, out_ref, send_sems, recv_sems):
        x, y, c = _place()
        me = 4 * x + 2 * y + c
        out_ref[me] = v_ref[...]
        copies = []
        for k in range(1, N_DEV):
            peer = (_flip(x, (k >> 2) & 1), _flip(y, (k >> 1) & 1), _flip(c, k & 1))
            cp = pltpu.make_async_remote_copy(
                src_ref=v_ref, dst_ref=out_ref.at[me], send_sem=send_sems.at[k - 1],
                recv_sem=recv_sems.at[k - 1], device_id=peer, device_id_type=MESH)
            cp.start()
            copies.append(cp)
        for cp in copies:
            cp.wait()

    return pl.pallas_call(
        body, name=name,
        out_shape=jax.ShapeDtypeStruct((N_DEV, 1, n), F32),
        in_specs=[pl.BlockSpec(memory_space=pltpu.VMEM)],
        out_specs=pl.BlockSpec(memory_space=pltpu.VMEM),
        scratch_shapes=[pltpu.SemaphoreType.DMA((N_DEV - 1,)), pltpu.SemaphoreType.DMA((N_DEV - 1,))],
    )(v)


class _Plan:
    def __init__(self, operands, out_shapes, sems, phases):
        self.operands, self.out_shapes, self.sems, self.phases = operands, out_shapes, sems, phases


def _slab_start(base, rows, jump, idx):
    return pl.multiple_of(base + idx * rows + (idx // 4) * jump, 16)


def _gather_plan(shards, dst_of, base_of, dst_shapes, jump_of=None):
    n = len(shards)
    rows = [s.shape[0] for s in shards]
    jump_of = jump_of or [0] * n

    def phases(srcs, dsts, sems):
        send_sems, recv_sems, local_sems = sems
        x, y, c = _place()
        me, sibling = (x, y, c), (x, y, 1 - c)
        chips = [(1 - x, y), (x, 1 - y), (1 - x, 1 - y)]

        def slab(i, px, py, pc):
            start = _slab_start(base_of[i], rows[i], jump_of[i], 4 * px + 2 * py + pc)
            return dsts[dst_of[i]].at[pl.ds(start, rows[i])]

        def copy(i, k, block, to, src=None):
            return pltpu.make_async_remote_copy(
                src_ref=slab(i, *block) if src is None else src, dst_ref=slab(i, *block),
                send_sem=send_sems.at[i, k], recv_sem=recv_sems.at[i, k],
                device_id=to, device_id_type=MESH)

        def mine():
            return [pltpu.make_async_copy(srcs[i], slab(i, *me), local_sems.at[i]) for i in range(n)]

        def first():
            out = []
            for i in range(n):
                out.append(copy(i, 0, me, sibling, src=srcs[i]))
                out += [copy(i, 1 + j, me, (*chip, c), src=srcs[i]) for j, chip in enumerate(chips)]
            return out

        def passed():
            return [(copy(i, 1 + j, (*chip, c), me), copy(i, 4 + j, (*chip, c), sibling))
                    for j, chip in enumerate(chips) for i in range(n)]

        def start():
            for cp in mine() + first():
                cp.start()

        def middle():
            for landed, onward in passed():
                landed.wait_recv()
                onward.start()

        def finish():
            for i in range(n):
                copy(i, 0, sibling, me).wait_recv()
                for j, chip in enumerate(chips):
                    copy(i, 4 + j, (*chip, 1 - c), me).wait_recv()
            for cp in first() + [onward for _, onward in passed()]:
                cp.wait_send()
            for cp in mine():
                cp.wait()

        return start, middle, finish

    sems = [pltpu.SemaphoreType.DMA((n, 7)), pltpu.SemaphoreType.DMA((n, 7)), pltpu.SemaphoreType.DMA((n,))]
    return _Plan(list(shards), [jax.ShapeDtypeStruct(s, BF16) for s in dst_shapes], sems, phases)


def _scatter_plan(grads, src_of, base_of, rows, cols, jump_of=None):
    n = len(rows)
    jump_of = jump_of or [0] * n

    def phases(srcs, recvs, sems):
        send_sems, recv_sems, local_sems = sems
        x, y, c = _place()
        me = 4 * x + 2 * y + c

        def slab(i, idx):
            start = _slab_start(base_of[i], rows[i], jump_of[i], idx)
            return srcs[src_of[i]].at[pl.ds(start, rows[i])]

        def copies():
            out = [pltpu.make_async_copy(slab(i, me), recvs[i].at[me], local_sems.at[i]) for i in range(n)]
            for k in range(1, N_DEV):
                px, py, pc = _flip(x, (k >> 2) & 1), _flip(y, (k >> 1) & 1), _flip(c, k & 1)
                out += [pltpu.make_async_remote_copy(
                    src_ref=slab(i, 4 * px + 2 * py + pc), dst_ref=recvs[i].at[me],
                    send_sem=send_sems.at[i, k - 1], recv_sem=recv_sems.at[i, k - 1],
                    device_id=(px, py, pc), device_id_type=MESH) for i in range(n)]
            return out

        def start():
            for cp in copies():
                cp.start()

        def finish():
            for cp in copies():
                cp.wait()

        return start, None, finish

    sems = [pltpu.SemaphoreType.DMA((n, 7)), pltpu.SemaphoreType.DMA((n, 7)), pltpu.SemaphoreType.DMA((n,))]
    out_shapes = [jax.ShapeDtypeStruct((N_DEV, rows[i], cols[i]), BF16) for i in range(n)]
    return _Plan(list(grads), out_shapes, sems, phases)


def _run_plan(name, plan):
    n_in, n_out = len(plan.operands), len(plan.out_shapes)

    def body(*refs):
        for phase in plan.phases(refs[:n_in], refs[n_in:n_in + n_out], refs[n_in + n_out:]):
            if phase is not None:
                phase()

    hbm = pl.BlockSpec(memory_space=pltpu.HBM)
    return pl.pallas_call(
        body, name=name, out_shape=plan.out_shapes,
        in_specs=[hbm] * n_in, out_specs=[hbm] * n_out, scratch_shapes=plan.sems,
    )(*plan.operands)


def _run_plan_on_sequencer(name, plan, collective_id):
    src_refs = [jax.new_ref(a, memory_space=pltpu.MemorySpace.HBM) for a in plan.operands]
    dst_refs = [jax.empty_ref(s, memory_space=pltpu.MemorySpace.HBM) for s in plan.out_shapes]

    @pl.kernel(mesh=plsc.ScalarSubcoreMesh(axis_name="sequencer", num_cores=1), name=name,
               scratch_types=tuple(plan.sems),
               compiler_params=pltpu.CompilerParams(collective_id=collective_id))
    def launch(*sems):
        x, y, c = _place()
        barrier = pltpu.get_barrier_semaphore()
        for k in range(1, N_DEV):
            peer = (_flip(x, (k >> 2) & 1), _flip(y, (k >> 1) & 1), _flip(c, k & 1))
            pl.semaphore_signal(barrier, inc=1, device_id=peer, device_id_type=MESH)
        pl.semaphore_wait(barrier, N_DEV - 1)
        for phase in plan.phases(src_refs, dst_refs, sems):
            if phase is not None:
                phase()

    launch()
    return [r[...] for r in dst_refs]


def _sum_contributions(name, recv):
    _, rows, cols = recv.shape
    tr = rows if rows <= 512 else 304 if rows % 304 == 0 else 256

    def body(r_ref, o_ref):
        acc = r_ref[0].astype(F32)
        for k in range(1, N_DEV):
            acc = acc + r_ref[k].astype(F32)
        o_ref[...] = acc

    return pl.pallas_call(
        body, name=name, grid=(rows // tr,),
        out_shape=jax.ShapeDtypeStruct((rows, cols), F32),
        in_specs=[pl.BlockSpec((N_DEV, tr, cols), lambda i: (0, i, 0))],
        out_specs=pl.BlockSpec((tr, cols), lambda i: (i, 0)),
        compiler_params=_params("parallel"),
    )(recv)


def _mm(name, a, b, mode, out_dtype, tm, tn, tk, *, carry=None, tiles_in=(), tiles_out=(), epilogue=None,
        n_outer=False, keep_b=False, col_chunks=None, after=()):
    if mode == "TN":
        kk, m = a.shape
    else:
        m, kk = a.shape
    n = b.shape[0] if mode == "NT" else b.shape[1]
    tm, tn, tk = min(tm, m), min(tn, n), min(tk, kk)
    assert m % tm == 0 and n % tn == 0 and kk % tk == 0, (name, m, n, kk, tm, tn, tk)
    ni, nj, nk = m // tm, n // tn, kk // tk
    steps = ni * nj * nk
    dims = {"NN": NN_DIMS, "NT": NT_DIMS, "TN": TN_DIMS}[mode]
    if epilogue is None:
        tiles_out = [(jax.ShapeDtypeStruct((m, n), out_dtype), (tm, tn), lambda i, j: (i, j))]
    n_tin, n_tout = len(tiles_in), len(tiles_out)
    n_in = len(carry.operands) if carry else 0
    n_out = len(carry.out_shapes) if carry else 0
    n_acc = 1 if nk > 1 else 0
    n_keep = 2 if keep_b else 0
    n_after = len(after)
    assert not carry or steps >= 3
    assert not keep_b or (nk == 1 and nj == 1)
    assert not col_chunks or (epilogue is not None and nk == 1 and mode != "TN")
    ij = (lambda p, q: (q, p)) if n_outer else (lambda p, q: (p, q))
    inner = ni if n_outer else nj

    def body(a_ref, b_ref, *rest):
        tin = rest[:n_tin]
        cin = rest[n_tin:n_tin + n_in]
        rest = rest[n_tin + n_in + n_after:]
        tout = rest[:n_tout]
        cout = rest[n_tout:n_tout + n_out]
        scratch = rest[n_tout + n_out:]
        k = pl.program_id(2)
        visit = pl.program_id(0) * inner + pl.program_id(1)
        step = visit * nk + k
        if keep_b:
            b_kept, b_sem = scratch[n_acc:n_acc + 2]

            @pl.when(step == 0)
            def _():
                cp = pltpu.make_async_copy(b_ref, b_kept, b_sem)
                cp.start()
                cp.wait()

            b_ref = b_kept
        if carry:
            start, middle, finish = carry.phases(cin, cout, scratch[n_acc + n_keep:])
            pl.when(step == 0)(start)

        def store(prod, c=0, cols=()):
            if epilogue is None:
                tout[0][...] = prod.astype(out_dtype)
            else:
                epilogue(prod, jnp.logical_and(visit == 0, c == 0), tin, tout, *cols)

        if col_chunks:
            for c, (c0, cw) in enumerate(col_chunks):
                b_part = b_ref[pl.ds(c0, cw), :] if mode == "NT" else b_ref[:, pl.ds(c0, cw)]
                store(lax.dot_general(a_ref[...], b_part, dims, preferred_element_type=F32), c, ((c0, cw),))
        else:
            part = lax.dot_general(a_ref[...], b_ref[...], dims, preferred_element_type=F32)
            if nk == 1:
                store(part)
            else:
                acc_ref = scratch[0]

                @pl.when(k == 0)
                def _():
                    acc_ref[...] = part

                @pl.when((k > 0) & (k < nk - 1))
                def _():
                    acc_ref[...] += part

                @pl.when(k == nk - 1)
                def _():
                    store(acc_ref[...] + part)

        if carry:
            if middle is not None:
                pl.when(step == (steps * 3) // 5)(middle)
            pl.when(step == steps - 1)(finish)

    def spec(shape, fn):
        return pl.BlockSpec(shape, lambda p, q, k: fn(*ij(p, q)))

    a_spec = (pl.BlockSpec((tk, tm), lambda p, q, k: (k, ij(p, q)[0])) if mode == "TN"
              else pl.BlockSpec((tm, tk), lambda p, q, k: (ij(p, q)[0], k)))
    if keep_b:
        b_spec = pl.BlockSpec(memory_space=pl.ANY)
    elif mode == "NT":
        b_spec = pl.BlockSpec((tn, tk), lambda p, q, k: (ij(p, q)[1], k))
    else:
        b_spec = pl.BlockSpec((tk, tn), lambda p, q, k: (k, ij(p, q)[1]))
    hbm = pl.BlockSpec(memory_space=pltpu.HBM)
    sequential = carry or epilogue or keep_b
    out = pl.pallas_call(
        body, name=name, grid=(nj, ni, nk) if n_outer else (ni, nj, nk),
        out_shape=[t[0] for t in tiles_out] + (carry.out_shapes if carry else []),
        in_specs=([a_spec, b_spec] + [spec(t[1], t[2]) for t in tiles_in] + [hbm] * n_in
                  + [pl.BlockSpec(memory_space=pl.ANY)] * n_after),
        out_specs=[spec(t[1], t[2]) for t in tiles_out] + [hbm] * n_out,
        scratch_shapes=([pltpu.VMEM((tm, tn), F32)] * n_acc
                        + ([pltpu.VMEM(b.shape, b.dtype), pltpu.SemaphoreType.DMA] if keep_b else [])
                        + (carry.sems if carry else [])),
        compiler_params=(_params("arbitrary", "arbitrary", "arbitrary") if sequential
                         else _params("parallel", "parallel", "arbitrary")),
    )(a, b, *[t[0] for t in tiles_in], *(carry.operands if carry else []), *after)
    return out if (carry or epilogue) else out[0]


def _row(tm, w, off=0):
    return pl.BlockSpec((tm, w), lambda i: (i, off))


def _vec(w):
    return pl.BlockSpec((1, w), lambda i: (0, 0))


def _sigmoid(x):
    return 0.5 * jnp.tanh(0.5 * x) + 0.5


def _normmod(name, x, g, sc, sh, tm=512):
    s = x.shape[0]

    def body(x_ref, g_ref, sc_ref, sh_ref, h_ref):
        xv = x_ref[...]
        r = lax.rsqrt(jnp.mean(xv * xv, axis=-1, keepdims=True) + EPS)
        h_ref[...] = ((xv * r) * g_ref[...] * (1.0 + sc_ref[...]) + sh_ref[...]).astype(BF16)

    return pl.pallas_call(
        body, name=name, grid=(s // tm,),
        out_shape=jax.ShapeDtypeStruct((s, D), BF16),
        in_specs=[_row(tm, D), _vec(D), _vec(D), _vec(D)], out_specs=_row(tm, D),
        compiler_params=_params("parallel"),
    )(x, g, sc, sh)


def _normmod_bwd(name, dh, x, gin, g, sc, sh, tm=512):
    s = x.shape[0]

    def body(dh_ref, x_ref, gin_ref, g_ref, sc_ref, sh_ref, gout_ref, acc_ref):
        xv, dhv = x_ref[...], dh_ref[...]
        r = lax.rsqrt(jnp.mean(xv * xv, axis=-1, keepdims=True) + EPS)
        nv = xv * r
        gv, one_sc = g_ref[...], 1.0 + sc_ref[...]
        dn = dhv * gv * one_sc
        dx = r * (dn - nv * jnp.mean(dn * nv, axis=-1, keepdims=True))
        gout_ref[...] = gin_ref[...] + dx

        @pl.when(pl.program_id(0) == 0)
        def _():
            acc_ref[...] = jnp.zeros_like(acc_ref)

        dhn = dhv * nv
        acc_ref[0:1, :] += jnp.sum(dhv, axis=0, keepdims=True)
        acc_ref[1:2, :] += jnp.sum(dhn * gv, axis=0, keepdims=True)
        acc_ref[2:3, :] += jnp.sum(dhn * one_sc, axis=0, keepdims=True)

    return pl.pallas_call(
        body, name=name, grid=(s // tm,),
        out_shape=[jax.ShapeDtypeStruct((s, D), F32), jax.ShapeDtypeStruct((8, D), F32)],
        in_specs=[_row(tm, D), _row(tm, D), _row(tm, D), _vec(D), _vec(D), _vec(D)],
        out_specs=[_row(tm, D), pl.BlockSpec((8, D), lambda i: (0, 0))],
        compiler_params=_params("arbitrary"),
    )(dh, x, gin, g, sc, sh)


def _swiglu(name, ab, tm=512):
    s = ab.shape[0]

    def body(ab_ref, s_ref):
        a = ab_ref[:, :FF].astype(F32)
        b = ab_ref[:, FF:].astype(F32)
        s_ref[...] = (a * _sigmoid(a) * b).astype(BF16)

    return pl.pallas_call(
        body, name=name, grid=(s // tm,),
        out_shape=jax.ShapeDtypeStruct((s, FF), BF16),
        in_specs=[_row(tm, 2 * FF)], out_specs=_row(tm, FF),
        compiler_params=_params("parallel"),
    )(ab)


def _swiglu_bwd(name, ds, ab, tm=256):
    s = ab.shape[0]

    def body(ds_ref, ab_ref, dab_ref):
        a = ab_ref[:, :FF].astype(F32)
        b = ab_ref[:, FF:].astype(F32)
        dsv = ds_ref[...].astype(F32)
        sig = _sigmoid(a)
        dab_ref[:, :FF] = (dsv * b * (sig * (1.0 + a * (1.0 - sig)))).astype(BF16)
        dab_ref[:, FF:] = (dsv * (a * sig)).astype(BF16)

    return pl.pallas_call(
        body, name=name, grid=(s // tm,),
        out_shape=jax.ShapeDtypeStruct((s, 2 * FF), BF16),
        in_specs=[_row(tm, FF), _row(tm, 2 * FF)], out_specs=_row(tm, 2 * FF),
        compiler_params=_params("parallel"),
    )(ds, ab)


def _residual(name, x, f, gt, coef, tm=512):
    s = x.shape[0]

    def body(x_ref, f_ref, gt_ref, o_ref):
        o_ref[...] = x_ref[...] + (coef * gt_ref[...]) * f_ref[...]

    return pl.pallas_call(
        body, name=name, grid=(s // tm,),
        out_shape=jax.ShapeDtypeStruct((s, D), F32),
        in_specs=[_row(tm, D), _row(tm, D), _vec(D)], out_specs=_row(tm, D),
        compiler_params=_params("parallel"),
    )(x, f, gt)


def _gate_bwd(name, gin, f, gt, coef, tm=512):
    s = gin.shape[0]

    def body(g_ref, f_ref, gt_ref, df_ref, acc_ref):
        gv = g_ref[...]
        df_ref[...] = ((coef * gt_ref[...]) * gv).astype(BF16)

        @pl.when(pl.program_id(0) == 0)
        def _():
            acc_ref[...] = jnp.zeros_like(acc_ref)

        acc_ref[0:1, :] += coef * jnp.sum(gv * f_ref[...], axis=0, keepdims=True)

    return pl.pallas_call(
        body, name=name, grid=(s // tm,),
        out_shape=[jax.ShapeDtypeStruct((s, D), BF16), jax.ShapeDtypeStruct((8, D), F32)],
        in_specs=[_row(tm, D), _row(tm, D), _vec(D)],
        out_specs=[_row(tm, D), pl.BlockSpec((8, D), lambda i: (0, 0))],
        compiler_params=_params("arbitrary"),
    )(gin, f, gt)


def _loss_grad(x3, target, tm=512):
    s = x3.shape[0]

    def body(y_ref, t_ref, g_ref, l_ref):
        e = y_ref[...] - t_ref[...]
        g_ref[...] = e * (1.0 / D)

        @pl.when(pl.program_id(0) == 0)
        def _():
            l_ref[...] = jnp.zeros_like(l_ref)

        l_ref[...] += jnp.sum(jnp.mean(e * e, axis=-1, keepdims=True), axis=0, keepdims=True) * 0.5

    return pl.pallas_call(
        body, name="loss_grad", grid=(s // tm,),
        out_shape=[jax.ShapeDtypeStruct((s, D), F32), jax.ShapeDtypeStruct((8, 128), F32)],
        in_specs=[_row(tm, D), _row(tm, D)],
        out_specs=[_row(tm, D), pl.BlockSpec((8, 128), lambda i: (0, 0))],
        compiler_params=_params("arbitrary"),
    )(x3, target)


def _heads(x, fn):
    return jnp.concatenate([fn(x[:, h * HD:(h + 1) * HD], h) for h in range(COL // HD)], axis=1)


def _qknorm(proj, wqk, tm=1024):
    s = proj.shape[0]

    def body(p_ref, w_ref, o_ref):
        pv = p_ref[...].astype(F32)
        wv = w_ref[...]

        def one(qh, h):
            r = lax.rsqrt(jnp.mean(qh * qh, axis=-1, keepdims=True) + EPS)
            return (qh * r) * wv[:, h * HD:(h + 1) * HD]

        o_ref[...] = _heads(pv, one).astype(BF16)

    return pl.pallas_call(
        body, name="qknorm", grid=(s // tm, QKW // COL),
        out_shape=jax.ShapeDtypeStruct((s, QKW), BF16),
        in_specs=[pl.BlockSpec((tm, COL), lambda i, j: (i, j)), pl.BlockSpec((1, COL), lambda i, j: (0, j))],
        out_specs=pl.BlockSpec((tm, COL), lambda i, j: (i, j)),
        compiler_params=_params("parallel", "parallel"),
    )(proj, wqk)


def _qknorm_bwd(name, proj, dn, w, dproj, blk0, tm=1024):
    s = proj.shape[0]
    nblk = dn.shape[1] // COL

    def body(p_ref, d_ref, w_ref, _, o_ref, acc_ref):
        pv = p_ref[...].astype(F32)
        dv = d_ref[...]
        wv = w_ref[...]
        sums = []

        def one(qh, h):
            dn = dv[:, h * HD:(h + 1) * HD]
            r = lax.rsqrt(jnp.mean(qh * qh, axis=-1, keepdims=True) + EPS)
            nh = qh * r
            sums.append(jnp.sum(dn * nh, axis=0, keepdims=True))
            dnw = dn * wv[:, h * HD:(h + 1) * HD]
            return r * (dnw - nh * jnp.mean(dnw * nh, axis=-1, keepdims=True))

        o_ref[...] = _heads(pv, one).astype(BF16)

        @pl.when(pl.program_id(1) == 0)
        def _():
            acc_ref[...] = jnp.zeros_like(acc_ref)

        acc_ref[0:1, :] += jnp.concatenate(sums, axis=1)

    return pl.pallas_call(
        body, name=name, grid=(nblk, s // tm),
        out_shape=[jax.ShapeDtypeStruct((s, IN_W), BF16), jax.ShapeDtypeStruct((8, nblk * COL), F32)],
        in_specs=[pl.BlockSpec((tm, COL), lambda j, i: (i, blk0 + j)), pl.BlockSpec((tm, COL), lambda j, i: (i, j)),
                  pl.BlockSpec((1, COL), lambda j, i: (0, j)), pl.BlockSpec(memory_space=pl.ANY)],
        out_specs=[pl.BlockSpec((tm, COL), lambda j, i: (i, blk0 + j)),
                   pl.BlockSpec((8, COL), lambda j, i: (0, j))],
        input_output_aliases={3: 0},
        compiler_params=_params("arbitrary", "arbitrary"),
    )(proj, dn, w, dproj)


def _attn_shapes(s, g):
    d = DILATIONS[g]
    tb = min(s, max(2048, 256 * d))
    sb = min(256, tb // d)
    pb = BAND * d
    assert s % tb == 0 and tb % pb == 0 and (tb // d) % sb == 0 and sb % BAND == 0
    return d, tb, sb, pb


def _lanes(x, width):
    return jnp.concatenate([x] * (width // HD), axis=1)


def _every(start, size, d):
    return pl.ds(start, size, stride=d) if d > 1 else pl.ds(start, size)


def _attn_specs(g, tb, pb, s, ahead):
    ratio = tb // pb
    if ahead:
        nbr = lambda n: jnp.minimum((n + 1) * ratio, s // pb - 1)
    else:
        nbr = lambda n: jnp.maximum(n * ratio - 1, 0)
    cur = lambda base: pl.BlockSpec((tb, HD), lambda h, n: (n, base + g * N_HEADS + h))
    side = lambda base: pl.BlockSpec((pb, HD), lambda h, n: (nbr(n), base + g * N_HEADS + h))
    tok = pl.BlockSpec((tb, HD), lambda h, n: (n, h))
    tok_side = pl.BlockSpec((pb, HD), lambda h, n: (nbr(n), h))
    return cur, side, tok, tok_side


Q_COL, K_COL, V_COL = 0, 12, 24


def _attn_fwd(g, qkn, proj):
    s = qkn.shape[0]
    d, tb, sb, pb = _attn_shapes(s, g)
    ft = F32 if d > 1 else BF16
    nj = tb // d // sb
    scale = HD ** -0.5

    def body(q_ref, kc_ref, kp_ref, vc_ref, vp_ref, o_ref, lse_ref, qf, kf, vf):
        n = pl.program_id(1)
        qf[...] = q_ref[...].astype(ft)
        kf[0:pb] = kp_ref[...].astype(ft)
        kf[pb:] = kc_ref[...].astype(ft)
        vf[0:pb] = vp_ref[...].astype(ft)
        vf[pb:] = vc_ref[...].astype(ft)
        for r in range(d):
            for j in range(nj):
                at = j * sb * d + r
                q = qf[_every(at, sb, d), :].astype(BF16)
                k = kf[_every(at, sb + BAND, d), :].astype(BF16)
                v = vf[_every(at, sb + BAND, d), :].astype(BF16)
                sc = lax.dot_general(q, k, NT_DIMS, preferred_element_type=F32) * scale
                qi = lax.broadcasted_iota(jnp.int32, sc.shape, 0)
                kj = lax.broadcasted_iota(jnp.int32, sc.shape, 1)
                valid = (kj >= qi) & (kj <= qi + BAND)
                if j == 0:
                    valid = valid & ((kj >= BAND) | (n > 0))
                sc = jnp.where(valid, sc, -1e30)
                m = jnp.max(sc, axis=-1, keepdims=True)
                p = jnp.exp(sc - m)
                l = jnp.sum(p, axis=-1, keepdims=True)
                o = lax.dot_general(p.astype(BF16), v, NN_DIMS, preferred_element_type=F32)
                o_ref[_every(at, sb, d), :] = o / l
                lse_ref[_every(at, sb, d), :] = jnp.broadcast_to(m + jnp.log(l), (sb, HD))

    cur, side, tok, _ = _attn_specs(g, tb, pb, s, ahead=False)
    return pl.pallas_call(
        body, name=f"attn_fwd_g{g}", grid=(N_HEADS, s // tb),
        out_shape=[jax.ShapeDtypeStruct((s, COL), F32)] * 2,
        in_specs=[cur(Q_COL), cur(K_COL), side(K_COL), cur(V_COL), side(V_COL)],
        out_specs=[tok, tok],
        scratch_shapes=[pltpu.VMEM((tb, HD), ft), pltpu.VMEM((tb + pb, HD), ft),
                        pltpu.VMEM((tb + pb, HD), ft)],
        compiler_params=_params("parallel", "arbitrary"),
    )(qkn, qkn, qkn, proj, proj)


def _attn_combine(os_, lses, tm=512):
    s = os_[0].shape[0]

    def body(o0, o1, o2, l0, l1, l2, o_ref, lse_ref):
        a, b, c = l0[...], l1[...], l2[...]
        m = jnp.maximum(jnp.maximum(a, b), c)
        ea, eb, ec = jnp.exp(a - m), jnp.exp(b - m), jnp.exp(c - m)
        tot = ea + eb + ec
        o_ref[...] = ((ea * o0[...] + eb * o1[...] + ec * o2[...]) / tot).astype(BF16)
        lse_ref[...] = m + jnp.log(tot)

    return pl.pallas_call(
        body, name="attn_combine", grid=(s // tm,),
        out_shape=[jax.ShapeDtypeStruct((s, COL), BF16), jax.ShapeDtypeStruct((s, COL), F32)],
        in_specs=[_row(tm, COL)] * 6, out_specs=[_row(tm, COL)] * 2,
        compiler_params=_params("parallel"),
    )(*os_, *lses)


def _attn_delta(do, o, tm=512):
    s = do.shape[0]

    def body(do_ref, o_ref, del_ref):
        prod = do_ref[...] * o_ref[...].astype(F32)
        del_ref[...] = _heads(prod, lambda ph, h: jnp.broadcast_to(
            jnp.sum(ph, axis=-1, keepdims=True), ph.shape))

    return pl.pallas_call(
        body, name="attn_delta", grid=(s // tm,),
        out_shape=jax.ShapeDtypeStruct((s, COL), F32),
        in_specs=[_row(tm, COL)] * 2, out_specs=_row(tm, COL),
        compiler_params=_params("parallel"),
    )(do, o)


def _attn_bwd(g, qkn, proj, do, lse, delta, dqn, dkn, dproj):
    s = qkn.shape[0]
    d, tb, sb, pb = _attn_shapes(s, g)
    ft = F32 if d > 1 else BF16
    nj = tb // d // sb
    nt = s // tb
    scale = HD ** -0.5
    chained = dqn is not None

    def body(k_ref, v_ref, qc_ref, qn_ref, doc_ref, don_ref, lc_ref, ln_ref, dc_ref, dn_ref, *rest):
        dq_ref, dk_ref, dv_ref, kf, vf, qf, dvf, later = rest[-8:]
        n = pl.program_id(1)
        kf[...] = k_ref[...].astype(ft)
        vf[...] = v_ref[...].astype(ft)
        qf[0:tb] = qc_ref[...].astype(ft)
        qf[tb:] = qn_ref[...].astype(ft)

        @pl.when(n == 0)
        def _():
            later[...] = jnp.zeros_like(later)

        def window(c_ref, n_ref, r, j):
            at = j * sb * d + r
            if j < nj - 1:
                return c_ref[_every(at, sb + BAND, d), :]
            return jnp.concatenate([c_ref[_every(at, sb, d), :], n_ref[_every(r, BAND, d), :]], axis=0)

        for r in range(d):
            tail = later[r]
            for j in range(nj):
                at = j * sb * d + r
                rows = _every(at, sb, d)
                k = kf[rows, :].astype(BF16)
                v = vf[rows, :].astype(BF16)
                q = qf[_every(at, sb + BAND, d), :].astype(BF16)
                dov = window(doc_ref, don_ref, r, j).astype(BF16)
                sc = lax.dot_general(q, k, NT_DIMS, preferred_element_type=F32) * scale
                qi = lax.broadcasted_iota(jnp.int32, sc.shape, 0)
                kj = lax.broadcasted_iota(jnp.int32, sc.shape, 1)
                valid = (qi >= kj) & (qi <= kj + BAND)
                if j == nj - 1:
                    valid = valid & ((qi < sb) | (n < nt - 1))
                p = jnp.exp(jnp.where(valid, sc - _lanes(window(lc_ref, ln_ref, r, j), sb), -1e30))
                dp = lax.dot_general(dov, v, NT_DIMS, preferred_element_type=F32)
                ds = (p * (dp - _lanes(window(dc_ref, dn_ref, r, j), sb)) * scale).astype(BF16)
                dvf[rows, :] = lax.dot_general(p.astype(BF16), dov, TN_DIMS, preferred_element_type=F32)
                dk_ref[rows, :] = lax.dot_general(ds, q, TN_DIMS, preferred_element_type=F32)
                dqw = lax.dot_general(ds, k, NN_DIMS, preferred_element_type=F32)
                first = dqw[:BAND] + tail
                dq_ref[rows, :] = first if sb == BAND else jnp.concatenate([first, dqw[BAND:sb]], axis=0)
                tail = dqw[sb:]
            later[r] = tail
        dv_ref[...] = dvf[...].astype(BF16)

    cur, side, tok, tok_side = _attn_specs(g, tb, pb, s, ahead=True)
    anyspec = pl.BlockSpec(memory_space=pl.ANY)
    n_heads_cols = 3 * N_HEADS * HD
    return pl.pallas_call(
        body, name=f"attn_bwd_g{g}", grid=(N_HEADS, nt),
        out_shape=[jax.ShapeDtypeStruct((s, n_heads_cols), F32), jax.ShapeDtypeStruct((s, n_heads_cols), F32),
                   jax.ShapeDtypeStruct((s, IN_W), BF16)],
        in_specs=[cur(K_COL), cur(V_COL), cur(Q_COL), side(Q_COL), tok, tok_side, tok, tok_side,
                  tok, tok_side] + ([anyspec, anyspec] if chained else []) + [anyspec],
        out_specs=[cur(0), cur(0), cur(V_COL)],
        input_output_aliases={10: 0, 11: 1, 12: 2} if chained else {10: 2},
        scratch_shapes=[pltpu.VMEM((tb, HD), ft), pltpu.VMEM((tb, HD), ft),
                        pltpu.VMEM((tb + pb, HD), ft), pltpu.VMEM((tb, HD), F32),
                        pltpu.VMEM((d, BAND, HD), F32)],
        compiler_params=_params("arbitrary", "arbitrary"),
    )(qkn, proj, qkn, qkn, do, do, lse, lse, delta, delta, *([dqn, dkn] if chained else []), dproj)


def _shift_down(x, before, k):
    rolled = pltpu.roll(x, k, 0)
    head = jnp.where(lax.broadcasted_iota(jnp.int32, before.shape, 0) < k, pltpu.roll(before, k, 0), rolled[:8])
    return jnp.concatenate([head, rolled[8:]], axis=0)


def _shift_up(x, after, k):
    rows = x.shape[0]
    rolled = pltpu.roll(x, rows - k, 0)
    tail = jnp.where(lax.broadcasted_iota(jnp.int32, after.shape, 0) >= 8 - k,
                     pltpu.roll(after, 8 - k, 0), rolled[rows - 8:])
    return jnp.concatenate([rolled[:rows - 8], tail], axis=0)


def _conv_fwd(proj, cw, tm=1024):
    s = proj.shape[0]
    r16 = tm // 16

    def body(u_ref, b_ref, c_ref, up_ref, cp_ref, w_ref, z_ref):
        i = pl.program_id(1)
        xc = c_ref[...].astype(F32) * u_ref[...].astype(F32)
        xp = jnp.where(i > 0, cp_ref[8:16, :].astype(F32) * up_ref[8:16, :].astype(F32), 0.0)
        w = w_ref[...]
        conv = _shift_down(xc, xp, 2) * w[0:1] + _shift_down(xc, xp, 1) * w[1:2] + xc * w[2:3]
        z_ref[...] = (b_ref[...].astype(F32) * conv).astype(BF16)

    tile = lambda blk: pl.BlockSpec((tm, COL), lambda j, i: (i, blk + j))
    before = lambda blk: pl.BlockSpec((16, COL), lambda j, i: (jnp.maximum(i * r16 - 1, 0), blk + j))
    return pl.pallas_call(
        body, name="conv_fwd", grid=(D // COL, s // tm),
        out_shape=jax.ShapeDtypeStruct((s, D), BF16),
        in_specs=[tile(U_BLK), tile(B_BLK), tile(C_BLK), before(U_BLK), before(C_BLK),
                  pl.BlockSpec((3, COL), lambda j, i: (0, j))],
        out_specs=pl.BlockSpec((tm, COL), lambda j, i: (i, j)),
        compiler_params=_params("parallel", "parallel"),
    )(proj, proj, proj, proj, proj, cw)


def _conv_bwd(dz, proj, cw, dproj, tm=1024):
    s = proj.shape[0]
    r16 = tm // 16
    nrow = s // tm

    def body(dz_ref, u_ref, b_ref, c_ref, up_ref, cp_ref, dzn_ref, bn_ref, w_ref, _, o_ref, dc_ref, acc_ref):
        piece, i = pl.program_id(1), pl.program_id(2)
        u, c = u_ref[...].astype(F32), c_ref[...].astype(F32)
        bv = b_ref[...].astype(F32)
        dzv = dz_ref[...]
        w = w_ref[...]

        @pl.when((piece == 0) & (i == 0))
        def _():
            acc_ref[...] = jnp.zeros_like(acc_ref)

        @pl.when(piece == 0)
        def _():
            xc = c * u
            xp = jnp.where(i > 0, cp_ref[8:16, :].astype(F32) * up_ref[8:16, :].astype(F32), 0.0)
            x2, x1 = _shift_down(xc, xp, 2), _shift_down(xc, xp, 1)
            o_ref[...] = (dzv * (x2 * w[0:1] + x1 * w[1:2] + xc * w[2:3])).astype(BF16)
            dc_ref[...] = jnp.zeros_like(dc_ref)
            dconv = dzv * bv
            acc_ref[0:1, :] += jnp.sum(dconv * x2, axis=0, keepdims=True)
            acc_ref[1:2, :] += jnp.sum(dconv * x1, axis=0, keepdims=True)
            acc_ref[2:3, :] += jnp.sum(dconv * xc, axis=0, keepdims=True)

        @pl.when(piece == 1)
        def _():
            dconv = dzv * bv
            dn = jnp.where(i < nrow - 1, dzn_ref[...] * bn_ref[0:8, :].astype(F32), 0.0)
            dxc = dconv * w[2:3] + _shift_up(dconv, dn, 1) * w[1:2] + _shift_up(dconv, dn, 2) * w[0:1]
            o_ref[...] = (dxc * c).astype(BF16)
            dc_ref[...] = (dxc * u).astype(BF16)

    tile = lambda blk: pl.BlockSpec((tm, COL), lambda j, p, i: (i, blk + j))
    before = lambda blk: pl.BlockSpec((16, COL), lambda j, p, i: (jnp.maximum(i * r16 - 1, 0), blk + j))
    after = lambda rows, blk: pl.BlockSpec(
        (rows, COL), lambda j, p, i: (jnp.minimum((i + 1) * (tm // rows), s // rows - 1), blk + j))
    return pl.pallas_call(
        body, name="conv_bwd", grid=(D // COL, 2, nrow),
        out_shape=[jax.ShapeDtypeStruct((s, IN_W), BF16), jax.ShapeDtypeStruct((s + tm, D), BF16),
                   jax.ShapeDtypeStruct((8, D), F32)],
        in_specs=[tile(0), tile(U_BLK), tile(B_BLK), tile(C_BLK), before(U_BLK), before(C_BLK),
                  after(8, 0), after(16, B_BLK), pl.BlockSpec((3, COL), lambda j, p, i: (0, j)),
                  pl.BlockSpec(memory_space=pl.ANY)],
        out_specs=[pl.BlockSpec((tm, COL), lambda j, p, i: (i, jnp.where(p == 0, B_BLK, U_BLK) + j)),
                   pl.BlockSpec((tm, COL), lambda j, p, i: (jnp.where(p == 0, nrow, i), j)),
                   pl.BlockSpec((8, COL), lambda j, p, i: (0, j))],
        input_output_aliases={9: 0},
        compiler_params=_params("arbitrary", "arbitrary", "arbitrary"),
    )(dz, proj, proj, proj, proj, proj, dz, proj, cw, dproj)


def _copy_columns(name, src, dst, blk0, tm=1024):
    s, w = dst.shape[0], src.shape[1]
    fresh = isinstance(dst, jax.ShapeDtypeStruct)

    def body(x_ref, *rest):
        rest[-1][...] = x_ref[...]

    return pl.pallas_call(
        body, name=name, grid=(w // COL, s // tm),
        out_shape=jax.ShapeDtypeStruct(dst.shape, dst.dtype),
        in_specs=[pl.BlockSpec((tm, COL), lambda j, i: (i, j))] + ([] if fresh else [pl.BlockSpec(memory_space=pl.ANY)]),
        out_specs=pl.BlockSpec((tm, COL), lambda j, i: (i, blk0 + j)),
        input_output_aliases={} if fresh else {1: 0},
        compiler_params=_params("parallel", "parallel"),
    )(src, *([] if fresh else [dst]))


def _merge_fwd(ya, yc, proj, tm=512):
    s = proj.shape[0]

    def body(ya_ref, yc_ref, ga_ref, gc_ref, o_ref):
        o_ref[...] = (_sigmoid(ga_ref[...].astype(F32)) * ya_ref[...].astype(F32)
                      + _sigmoid(gc_ref[...].astype(F32)) * yc_ref[...].astype(F32)).astype(BF16)

    tile = lambda blk: pl.BlockSpec((tm, COL), lambda j, i: (i, blk + j))
    return pl.pallas_call(
        body, name="merge_fwd", grid=(D // COL, s // tm),
        out_shape=jax.ShapeDtypeStruct((s, D), BF16),
        in_specs=[tile(0), tile(0), tile(GA_BLK), tile(GC_BLK)], out_specs=tile(0),
        compiler_params=_params("parallel", "parallel"),
    )(ya, yc, proj, proj)


def _merge_bwd_branches(dm, proj, tm=512):
    s = proj.shape[0]

    def body(dm_ref, ga_ref, gc_ref, dya_ref, dyc_ref):
        dmv = dm_ref[...]
        dya_ref[...] = (dmv * _sigmoid(ga_ref[...].astype(F32))).astype(BF16)
        dyc_ref[...] = (dmv * _sigmoid(gc_ref[...].astype(F32))).astype(BF16)

    tile = lambda blk: pl.BlockSpec((tm, COL), lambda j, i: (i, blk + j))
    return pl.pallas_call(
        body, name="merge_bwd_branches", grid=(D // COL, s // tm),
        out_shape=[jax.ShapeDtypeStruct((s, D), BF16)] * 2,
        in_specs=[tile(0), tile(GA_BLK), tile(GC_BLK)], out_specs=[tile(0)] * 2,
        compiler_params=_params("parallel", "parallel"),
    )(dm, proj, proj)


def _merge_bwd_gates(dm, ya, yc, proj, tm=1024):
    s = proj.shape[0]
    half = D // COL

    def body(dm_ref, ya_ref, yc_ref, g_ref, o_ref):
        y = jnp.where(pl.program_id(0) < half, ya_ref[...].astype(F32), yc_ref[...].astype(F32))
        sig = _sigmoid(g_ref[...].astype(F32))
        o_ref[...] = (dm_ref[...] * y * sig * (1.0 - sig)).astype(BF16)

    chan = pl.BlockSpec((tm, COL), lambda jj, i: (i, jj % half))
    gate = pl.BlockSpec((tm, COL), lambda jj, i: (i, GA_BLK + jj))
    return pl.pallas_call(
        body, name="merge_bwd_gates", grid=(2 * half, s // tm),
        out_shape=jax.ShapeDtypeStruct((s, IN_W), BF16),
        in_specs=[chan, chan, chan, gate], out_specs=gate,
        compiler_params=_params("parallel", "parallel"),
    )(dm, ya, yc, proj)


def _mod_part(c_all, w_ada, b_part):
    def body(c_ref, w_ref, b_ref, o_ref):
        cv = c_ref[...]
        act = cv * _sigmoid(cv)
        o_ref[...] = jnp.dot(act, w_ref[...], preferred_element_type=F32,
                             precision=lax.Precision.HIGHEST) + b_ref[...]

    return pl.pallas_call(
        body, name="mod_part", out_shape=jax.ShapeDtypeStruct((N_DEV, w_ada.shape[1]), F32),
    )(c_all, w_ada, b_part)


def _w_ada_grad(c_all_t, dmod_part):
    def body(c_ref, d_ref, o_ref):
        cv = c_ref[...]
        act = cv * _sigmoid(cv)
        dv = d_ref[...]
        acc = act[:, 0:1] * dv[0:1, :]
        for b in range(1, N_DEV):
            acc = acc + act[:, b:b + 1] * dv[b:b + 1, :]
        o_ref[...] = acc

    return pl.pallas_call(
        body, name="w_ada_grad", out_shape=jax.ShapeDtypeStruct((D, dmod_part.shape[1]), F32),
    )(c_all_t, dmod_part)


def _sum_rows(name, v):
    def body(v_ref, o_ref):
        acc = v_ref[0]
        for k in range(1, N_DEV):
            acc = acc + v_ref[k]
        o_ref[...] = acc

    return pl.pallas_call(body, name=name, out_shape=jax.ShapeDtypeStruct(v.shape[1:], F32))(v)


def _adamw(name, w, g, m, v):
    rows, cols = w.shape
    limit = max(16, (1 << 20) // (4 * cols))
    tr = rows if rows <= limit else next((t for t in range(limit - limit % 16, 15, -16) if rows % t == 0), rows)
    c1 = 1.0 - ADAM_B1 ** ADAM_STEP
    c2 = 1.0 - ADAM_B2 ** ADAM_STEP
    parts = g.ndim == 3

    def body(w_ref, g_ref, m_ref, v_ref, go_ref, d_ref, nm_ref, nv_ref):
        if parts:
            gv = g_ref[0].astype(F32)
            for k in range(1, N_DEV):
                gv = gv + g_ref[k].astype(F32)
        else:
            gv = g_ref[...]
        go_ref[...] = gv
        nm = ADAM_B1 * m_ref[...] + (1.0 - ADAM_B1) * gv
        nv = ADAM_B2 * v_ref[...] + (1.0 - ADAM_B2) * (gv * gv)
        nm_ref[...] = nm
        nv_ref[...] = nv
        d_ref[...] = -ADAM_LR * ((nm / c1) / (jnp.sqrt(nv / c2) + ADAM_EPS) + ADAM_WD * w_ref[...])

    spec = pl.BlockSpec((tr, cols), lambda i: (i, 0))
    g_spec = pl.BlockSpec((N_DEV, tr, cols), lambda i: (0, i, 0)) if parts else spec
    return pl.pallas_call(
        body, name=name, grid=(rows // tr,),
        out_shape=[jax.ShapeDtypeStruct((rows, cols), F32)] * 4,
        in_specs=[spec, g_spec, spec, spec], out_specs=[spec] * 4,
        compiler_params=_params("parallel"),
    )(w, g, m, v)


def _adamw_small(ws, gs, ms, vs):
    n = len(ws)
    c1 = 1.0 - ADAM_B1 ** ADAM_STEP
    c2 = 1.0 - ADAM_B2 ** ADAM_STEP

    def body(*refs):
        for i in range(n):
            w_ref, g_ref, m_ref, v_ref = refs[i], refs[n + i], refs[2 * n + i], refs[3 * n + i]
            d_ref, nm_ref, nv_ref = refs[4 * n + 3 * i:4 * n + 3 * i + 3]
            gv = g_ref[...]
            nm = ADAM_B1 * m_ref[...] + (1.0 - ADAM_B1) * gv
            nv = ADAM_B2 * v_ref[...] + (1.0 - ADAM_B2) * (gv * gv)
            nm_ref[...] = nm
            nv_ref[...] = nv
            d_ref[...] = -ADAM_LR * ((nm / c1) / (jnp.sqrt(nv / c2) + ADAM_EPS) + ADAM_WD * w_ref[...])

    outs = pl.pallas_call(
        body, name="adamw_small",
        out_shape=[jax.ShapeDtypeStruct(w.shape, F32) for w in ws for _ in range(3)],
    )(*ws, *gs, *ms, *vs)
    return [tuple(outs[3 * i:3 * i + 3]) for i in range(n)]


HALF = FF // 2


def _sds(shape, dtype):
    return jax.ShapeDtypeStruct(shape, dtype)


def _row_tile(w):
    return lambda tm: ((tm, w), lambda i, j: (i, 0))


def _one(w):
    return lambda rows: ((rows, w), lambda i, j: (0, 0))


def _gate_up_swiglu(name, h, wgu, carry=None, tm=512):
    s = h.shape[0]
    tm = min(tm, s)

    def epilogue(prod, first, tin, tout):
        pq_ref, s_ref = tout
        a, b = prod[:, :HALF], prod[:, HALF:]
        sig = _sigmoid(a)
        act = a * sig
        pq_ref[:, :HALF] = (b * (sig * (1.0 + a * (1.0 - sig)))).astype(BF16)
        pq_ref[:, HALF:] = act.astype(BF16)
        s_ref[...] = (act * b).astype(BF16)

    return _mm(name, h, wgu, "NT", None, tm, FF, D, carry=carry, n_outer=True, epilogue=epilogue,
               tiles_out=[(_sds((s, 2 * FF), BF16), (tm, FF), lambda i, j: (i, j)),
                          (_sds((s, FF), BF16), (tm, HALF), lambda i, j: (i, j))])


def _d_hidden_swiglu(name, df, wd, ab, after=(), tm=512):
    s = df.shape[0]
    tm = min(tm, s)

    def epilogue(prod, first, tin, tout, cols):
        da_cols = slice(cols[0], cols[0] + cols[1])
        db_cols = slice(HALF + cols[0], HALF + cols[0] + cols[1])
        tout[0][:, da_cols] = (prod * tin[0][:, da_cols].astype(F32)).astype(BF16)
        tout[0][:, db_cols] = (prod * tin[0][:, db_cols].astype(F32)).astype(BF16)

    chunks = [(c0, min(384, HALF - c0)) for c0 in range(0, HALF, 384)]
    return _mm(name, df, wd, "NT", None, tm, HALF, D, n_outer=True, epilogue=epilogue, col_chunks=chunks, after=after,
               tiles_in=[(ab, (tm, FF), lambda i, j: (i, j))],
               tiles_out=[(_sds((s, 2 * FF), BF16), (tm, FF), lambda i, j: (i, j))])[0]


def _out_residual(name, a, w, x, gt, coef, nxt, tm=512, tk=FF):
    s = a.shape[0]
    tm = min(tm, s)

    def epilogue(prod, first, tin, tout):
        x_ref, gt_ref, g_ref, sc_ref, sh_ref = tin
        f_ref, xn_ref, h_ref = tout
        f_ref[...] = prod
        xn = x_ref[...] + (coef * gt_ref[...]) * prod
        xn_ref[...] = xn
        r = lax.rsqrt(jnp.mean(xn * xn, axis=-1, keepdims=True) + EPS)
        h_ref[...] = ((xn * r) * g_ref[...] * (1.0 + sc_ref[...]) + sh_ref[...]).astype(BF16)

    row, vec = _row_tile(D)(tm), _one(D)(1)
    return _mm(name, a, w, "NN", None, tm, D, tk, epilogue=epilogue,
               tiles_in=[(x, *row), (gt, *vec)] + [(v, *vec) for v in nxt],
               tiles_out=[(_sds((s, D), F32), *row), (_sds((s, D), F32), *row), (_sds((s, D), BF16), *row)])


def _out_loss(name, a, w, x, gt, coef, target, tm=512):
    s = a.shape[0]
    tm = min(tm, s)

    def epilogue(prod, first, tin, tout):
        x_ref, gt_ref, t_ref = tin
        f_ref, g_ref, df_ref, acc_ref = tout
        f_ref[...] = prod
        cg = coef * gt_ref[...]
        e = x_ref[...] + cg * prod - t_ref[...]
        gv = e * (1.0 / D)
        g_ref[...] = gv
        df_ref[...] = (cg * gv).astype(BF16)

        @pl.when(first)
        def _():
            acc_ref[...] = jnp.zeros_like(acc_ref)

        acc_ref[0:1, :] += coef * jnp.sum(gv * prod, axis=0, keepdims=True)
        acc_ref[1:2, :] += (0.5 / D) * jnp.sum(e * e, axis=0, keepdims=True)

    row, vec = _row_tile(D)(tm), _one(D)(1)
    return _mm(name, a, w, "NN", None, tm, D, FF, epilogue=epilogue,
               tiles_in=[(x, *row), (gt, *vec), (target, *row)],
               tiles_out=[(_sds((s, D), F32), *row), (_sds((s, D), F32), *row), (_sds((s, D), BF16), *row),
                          (_sds((8, D), F32), *_one(D)(8))])


def _d_h_norm_bwd(name, da, w, x, gin, g, sc, sh, before=None, carry=None, after=(), tm=256):
    s = da.shape[0]
    tm = min(tm, s)
    coef = before[2] if before else None

    def epilogue(prod, first, tin, tout):
        x_ref, gin_ref, g_ref, sc_ref, sh_ref = tin[:5]
        gout_ref, acc_ref = tout[:2]
        xv = x_ref[...]
        r = lax.rsqrt(jnp.mean(xv * xv, axis=-1, keepdims=True) + EPS)
        nv = xv * r
        gv, one_sc = g_ref[...], 1.0 + sc_ref[...]
        dn = prod * gv * one_sc
        gout = gin_ref[...] + r * (dn - nv * jnp.mean(dn * nv, axis=-1, keepdims=True))
        gout_ref[...] = gout

        @pl.when(first)
        def _():
            acc_ref[...] = jnp.zeros_like(acc_ref)

        dhn = prod * nv
        acc_ref[0:1, :] += jnp.sum(prod, axis=0, keepdims=True)
        acc_ref[1:2, :] += jnp.sum(dhn * gv, axis=0, keepdims=True)
        acc_ref[2:3, :] += jnp.sum(dhn * one_sc, axis=0, keepdims=True)
        if before:
            f_ref, gt_ref = tin[5:]
            tout[2][...] = ((coef * gt_ref[...]) * gout).astype(BF16)
            acc_ref[3:4, :] += coef * jnp.sum(gout * f_ref[...], axis=0, keepdims=True)

    row, vec = _row_tile(D)(tm), _one(D)(1)
    tiles_in = [(x, *row), (gin, *row), (g, *vec), (sc, *vec), (sh, *vec)]
    tiles_out = [(_sds((s, D), F32), *row), (_sds((8, D), F32), *_one(D)(8))]
    if before:
        tiles_in += [(before[0], *row), (before[1], *vec)]
        tiles_out.append((_sds((s, D), BF16), *row))
    return _mm(name, da, w, "NN", None, tm, D, da.shape[1], epilogue=epilogue, carry=carry, keep_b=True, after=after,
               tiles_in=tiles_in, tiles_out=tiles_out)


def _gate_tiles(proj, tm):
    return [(proj, (tm, COL), (lambda i, j, blk=blk: (i, blk))) for blk in (GA_BLK, GA_BLK + 1, GC_BLK, GC_BLK + 1)]


def _conv_branch_merge(z, wc, ya, proj, tm=512):
    s = z.shape[0]
    tm = min(tm, s)

    def epilogue(prod, first, tin, tout):
        ya_ref, ga0, ga1, gc0, gc1 = tin
        tout[0][...] = prod.astype(BF16)
        for half, (ga, gc) in enumerate(((ga0, gc0), (ga1, gc1))):
            cols = slice(half * COL, (half + 1) * COL)
            tout[1][:, cols] = (_sigmoid(ga[...].astype(F32)) * ya_ref[:, cols].astype(F32)
                                + _sigmoid(gc[...].astype(F32)) * prod[:, cols]).astype(BF16)

    row = _row_tile(D)(tm)
    return _mm("mix_conv_branch", z, wc, "NN", None, tm, D, D, epilogue=epilogue,
               tiles_in=[(ya, *row)] + _gate_tiles(proj, tm),
               tiles_out=[(_sds((s, D), BF16), *row), (_sds((s, D), BF16), *row)])


def _d_merged_branches(dmix, wo, ya, yc, proj, tm=512):
    s = dmix.shape[0]
    tm = min(tm, s)

    def epilogue(prod, first, tin, tout):
        ya_ref, yc_ref, ga0, ga1, gc0, gc1 = tin
        dya_ref, dyc_ref, dg_ref = tout
        for half, (ga, gc) in enumerate(((ga0, gc0), (ga1, gc1))):
            cols = slice(half * COL, (half + 1) * COL)
            dm = prod[:, cols]
            for y_ref, g_ref, dy_ref, off in ((ya_ref, ga, dya_ref, 0), (yc_ref, gc, dyc_ref, D)):
                sig = _sigmoid(g_ref[...].astype(F32))
                dms = dm * sig
                dy_ref[:, cols] = dms.astype(BF16)
                dg_ref[:, off + half * COL:off + (half + 1) * COL] = (
                    dms * y_ref[:, cols].astype(F32) * (1.0 - sig)).astype(BF16)

    row = _row_tile(D)(tm)
    return _mm("mix_d_merged", dmix, wo, "NT", None, tm, D, D, epilogue=epilogue,
               tiles_in=[(ya, *row), (yc, *row)] + _gate_tiles(proj, tm),
               tiles_out=[(_sds((s, D), BF16), *row), (_sds((s, D), BF16), *row),
                          (_sds((s, 2 * D), BF16), *_row_tile(2 * D)(tm))])


def _d_o_delta(dya, wa_t, o, tm=1024):
    s = dya.shape[0]
    tm = min(tm, s)

    def epilogue(prod, first, tin, tout):
        tout[0][...] = prod
        tout[1][...] = _heads(prod * tin[0][...].astype(F32), lambda ph, h: jnp.broadcast_to(
            jnp.sum(ph, axis=-1, keepdims=True), ph.shape))

    row = _row_tile(COL)(tm)
    return _mm("mix_d_o", dya, wa_t, "NN", None, tm, COL, D, epilogue=epilogue,
               tiles_in=[(o, *row)], tiles_out=[(_sds((s, COL), F32), *row), (_sds((s, COL), F32), *row)])


def _ffn_bwd(tag, df, x, gin, h, ab, sw, g, sc, sh, wgu, wd, before=None, tk_dw=2048):
    dwd = _mm(f"{tag}_dw_down", sw, df, "TN", BF16, HALF, D, tk_dw)
    dab = _d_hidden_swiglu(f"{tag}_d_hidden", df, wd, ab, after=[dwd])
    dwgu = _mm(f"{tag}_dw_gate_up", dab, h, "TN", BF16, HALF, D, tk_dw)
    res = _d_h_norm_bwd(f"{tag}_d_h", dab, wgu, x, gin, g, sc, sh, before=before, after=[dwgu])
    return res, dwgu, dwd


def kernel(x, c, w_ada, b_ada, norm_ffn1, ffn1_w_gate, ffn1_w_up, ffn1_w_down, norm_mix, w_in, q_norm, k_norm, conv_w, w_attn_branch, w_conv_branch, w_out, norm_ffn2, ffn2_w_gate, ffn2_w_up, ffn2_w_down, loss_target, m_w_ada, m_b_ada, m_norm_ffn1, m_ffn1_w_gate, m_ffn1_w_up, m_ffn1_w_down, m_norm_mix, m_w_in, m_q_norm, m_k_norm, m_conv_w, m_w_attn_branch, m_w_conv_branch, m_w_out, m_norm_ffn2, m_ffn2_w_gate, m_ffn2_w_up, m_ffn2_w_down, v_w_ada, v_b_ada, v_norm_ffn1, v_ffn1_w_gate, v_ffn1_w_up, v_ffn1_w_down, v_norm_mix, v_w_in, v_q_norm, v_k_norm, v_conv_w, v_w_attn_branch, v_w_conv_branch, v_w_out, v_norm_ffn2, v_ffn2_w_gate, v_ffn2_w_up, v_ffn2_w_down):
    me = 4 * lax.axis_index("x") + 2 * lax.axis_index("y") + lax.axis_index("c")
    x0, target = x[0], loss_target[0]
    s = x0.shape[0]
    ada_cols = w_ada.shape[2]
    cw_cols = conv_w.shape[2]

    gathered = _small_allgather(
        "gather_c_conv", jnp.concatenate([c, conv_w[0].reshape(1, 3 * cw_cols)], axis=1))[:, 0]
    c_all = gathered[:, :D]
    cw = gathered[:, D:].reshape(N_DEV, 3, cw_cols).transpose(1, 0, 2).reshape(3, D)
    b_part = lax.dynamic_slice(b_ada, (0, me * ada_cols), (1, ada_cols))
    mod_part = _mod_part(c_all, w_ada[0], b_part)
    mod_all = _small_allgather("gather_mod", mod_part.reshape(1, N_DEV * ada_cols))
    mod = lax.dynamic_slice(mod_all.reshape(N_DEV, N_DEV, ada_cols), (0, me, 0), (N_DEV, 1, ada_cols))
    mod = mod.reshape(N_MOD, 1, D)
    sh1, sc1, gt1, sh2, sc2, gt2, sh3, sc3, gt3 = [mod[i] for i in range(N_MOD)]

    tb = lambda w: w[0].T.astype(BF16)
    nb = lambda w: w[0].astype(BF16)
    ffn1_shards = [tb(ffn1_w_gate), tb(ffn1_w_up), nb(ffn1_w_down)]
    ffn2_shards = [tb(ffn2_w_gate), tb(ffn2_w_up), nb(ffn2_w_down)]
    mix_shards = [tb(w_in), tb(w_attn_branch), nb(w_conv_branch), nb(w_out)]
    ffn_dst, ffn_base, ffn_jump, ffn_shapes = [0, 0, 1], [0, HALF, 0], [HALF, HALF, 0], [(2 * FF, D), (FF, D)]
    mix_dst, mix_base, mix_shapes = [0, 1, 2, 3], [0, 0, 0, 0], [(IN_W, D), (D, COL), (D, D), (D, D)]
    (wgu1,) = _run_plan_on_sequencer(
        "gather_ffn1_gate_up", _gather_plan(ffn1_shards[:2], ffn_dst[:2], ffn_base[:2], ffn_shapes[:1], ffn_jump[:2]), 1)
    (wd1,) = _run_plan_on_sequencer(
        "gather_ffn1_down", _gather_plan(ffn1_shards[2:], [0], [0], ffn_shapes[1:]), 8)
    win_t, wa_t, wc, wo = _run_plan_on_sequencer(
        "gather_mix_weights", _gather_plan(mix_shards, mix_dst, mix_base, mix_shapes), 2)
    wgu2, wd2 = _run_plan_on_sequencer(
        "gather_ffn2_weights", _gather_plan(ffn2_shards, ffn_dst, ffn_base, ffn_shapes, ffn_jump), 3)

    h1 = _normmod("ffn1_normmod", x0, norm_ffn1, sc1, sh1)
    ab1, s1 = _gate_up_swiglu("ffn1_gate_up", h1, wgu1)
    f1, x1, h2 = _out_residual("ffn1_down", s1, wd1, x0, gt1, 0.5, (norm_mix, sc2, sh2))
    proj = _mm("mix_in_proj", h2, win_t, "NT", BF16, 1024, IN_W // 4, D, n_outer=True)
    wqk = jnp.concatenate([jnp.tile(q_norm, (1, 12)), jnp.tile(k_norm, (1, 12))], axis=1)
    qkn = _qknorm(proj, wqk)
    group_out = [_attn_fwd(g, qkn, proj) for g in range(3)]
    o, lse = _attn_combine([go[0] for go in group_out], [go[1] for go in group_out])
    ya = _mm("mix_attn_branch", o, wa_t, "NT", BF16, 1024, 1024, COL)
    z = _conv_fwd(proj, cw)
    yc, merged = _conv_branch_merge(z, wc, ya, proj)
    mix, x2, h3 = _out_residual("mix_out_proj", merged, wo, x1, gt2, 1.0, (norm_ffn2, sc3, sh3), tk=D)
    ab3, s3 = _gate_up_swiglu("ffn2_gate_up", h3, wgu2)
    f3, g3, df3, acc_out = _out_loss("ffn2_down", s3, wd2, x2, gt3, 0.5, target)
    loss_part = jnp.sum(acc_out[1])

    ffn_rows = [sh_.shape[0] for sh_ in ffn1_shards]
    mix_rows = [sh_.shape[0] for sh_ in mix_shards]
    (g2, acc3, dmix), dwgu2, dwd2 = _ffn_bwd(
        "ffn2", df3, x2, g3, h3, ab3, s3, norm_ffn2, sc3, sh3, wgu2, wd2, before=(mix, gt2, 1.0))
    dya, dyc, dgates = _d_merged_branches(dmix, wo, ya, yc, proj)
    dwo = _mm("mix_dw_out", merged, dmix, "TN", BF16, 1024, 1024, 2048)
    dproj = _copy_columns("dproj_gates", dgates, jax.ShapeDtypeStruct((s, IN_W), BF16), GA_BLK)
    dwc = _mm("mix_dw_conv_branch", z, dyc, "TN", BF16, 1024, 1024, 2048)
    dz = _mm("mix_d_z", dyc, wc, "NT", F32, 1024, 1024, D)
    dproj, d_c, cw_acc = _conv_bwd(dz, proj, cw, dproj)
    dproj = _copy_columns("copy_d_c", d_c, dproj, C_BLK)
    dwa_t = _mm("mix_dw_attn_branch", dya, o, "TN", BF16, 1024, COL, 2048)
    do, delta = _d_o_delta(dya, wa_t, o)
    dqn = dkn = None
    for g in range(3):
        dqn, dkn, dproj = _attn_bwd(g, qkn, proj, do, lse, delta, dqn, dkn, dproj)
    dproj, wq_acc = _qknorm_bwd("qnorm_bwd", proj, dqn, wqk[:, :QKW // 2], dproj, 0)
    dproj, wk_acc = _qknorm_bwd("knorm_bwd", proj, dkn, wqk[:, QKW // 2:], dproj, QKW // 2 // COL)
    r_f2g, r_f2u, r_f2d, r_wa, r_wc, r_wo = _run_plan_on_sequencer(
        "scatter_ffn2_and_branch_grads",
        _scatter_plan([dwgu2, dwd2, dwa_t, dwc, dwo], [0, 0, 1, 2, 3, 4], [0, HALF, 0, 0, 0, 0],
                      ffn_rows + mix_rows[1:], [D, D, D, COL, D, D], [HALF, HALF, 0, 0, 0, 0]), 4)
    dwin_t = _mm("mix_dw_in", dproj, h2, "TN", BF16, IN_W // 4, COL, 2048)
    (r_win,) = _run_plan_on_sequencer(
        "scatter_w_in_grad", _scatter_plan([dwin_t], [0], [0], mix_rows[:1], [D]), 5)
    g1, acc2, df1 = _d_h_norm_bwd("mix_d_h", dproj, win_t, x1, g2, norm_mix, sc2, sh2, before=(f1, gt1, 0.5),
                                  after=[dwin_t])
    dwd1 = _mm("ffn1_dw_down", s1, df1, "TN", BF16, HALF, D, 2048)
    (r_f1d,) = _run_plan_on_sequencer(
        "scatter_ffn1_down_grad", _scatter_plan([dwd1], [0], [0], ffn_rows[2:], [D]), 6)
    dab1 = _d_hidden_swiglu("ffn1_d_hidden", df1, wd1, ab1, after=[dwd1, r_win])
    dwgu1 = _mm("ffn1_dw_gate_up", dab1, h1, "TN", BF16, HALF, D, 2048)
    r_f1g, r_f1u = _run_plan_on_sequencer(
        "scatter_ffn1_gate_up_grads",
        _scatter_plan([dwgu1], [0, 0], [0, HALF], ffn_rows[:2], [D, D], [HALF, HALF]), 7)
    g0, acc1 = _d_h_norm_bwd("ffn1_d_h", dab1, wgu1, x0, g1, norm_ffn1, sc1, sh1, after=[dwgu1, r_f1d])

    dqw = jnp.sum(wq_acc[0].reshape(12, HD), axis=0)
    dkw = jnp.sum(wk_acc[0].reshape(12, HD), axis=0)
    small = jnp.concatenate([
        acc1[0], acc1[1], acc2[3], acc2[0], acc2[1], acc3[3], acc3[0], acc3[1], acc_out[0],
        acc1[2], acc2[2], acc3[2], dqw, dkw, cw_acc[0:3].reshape(3 * D),
        jnp.zeros((HD,), F32).at[0].set(loss_part)]).reshape(1, -1)
    small_all = _small_allgather("gather_small_grads", small)
    small_sum = _sum_rows("sum_small_grads", small_all)[0]
    n_mod = N_MOD * D
    g_b_ada = small_sum[:n_mod].reshape(1, n_mod)
    g_norm1, g_norm2, g_norm3 = [small_sum[n_mod + i * D:n_mod + (i + 1) * D].reshape(1, D) for i in range(3)]
    off = n_mod + 3 * D
    g_qn, g_kn = small_sum[off:off + HD].reshape(1, HD), small_sum[off + HD:off + 2 * HD].reshape(1, HD)
    g_cw_full = small_sum[off + 2 * HD:off + 2 * HD + 3 * D].reshape(3, D)
    loss = small_sum[off + 2 * HD + 3 * D]
    g_cw = lax.dynamic_slice(g_cw_full, (0, me * cw_cols), (3, cw_cols))
    dmod_part = lax.dynamic_slice(small_all[:, 0, :n_mod], (0, me * ada_cols), (N_DEV, ada_cols))
    g_w_ada = _w_ada_grad(c_all.T, dmod_part)

    as_rows = {"ffn1_w_gate", "ffn1_w_up", "w_in", "w_attn_branch", "ffn2_w_gate", "ffn2_w_up"}
    grad_list = [g_w_ada, g_b_ada, g_norm1, r_f1g, r_f1u, r_f1d, g_norm2, r_win,
                 g_qn, g_kn, g_cw, r_wa, r_wc, r_wo, g_norm3, r_f2g, r_f2u, r_f2d]
    weights = [w_ada, b_ada, norm_ffn1, ffn1_w_gate, ffn1_w_up, ffn1_w_down, norm_mix, w_in, q_norm, k_norm,
               conv_w, w_attn_branch, w_conv_branch, w_out, norm_ffn2, ffn2_w_gate, ffn2_w_up, ffn2_w_down]
    ms = [m_w_ada, m_b_ada, m_norm_ffn1, m_ffn1_w_gate, m_ffn1_w_up, m_ffn1_w_down, m_norm_mix, m_w_in, m_q_norm,
          m_k_norm, m_conv_w, m_w_attn_branch, m_w_conv_branch, m_w_out, m_norm_ffn2, m_ffn2_w_gate,
          m_ffn2_w_up, m_ffn2_w_down]
    vs = [v_w_ada, v_b_ada, v_norm_ffn1, v_ffn1_w_gate, v_ffn1_w_up, v_ffn1_w_down, v_norm_mix, v_w_in, v_q_norm,
          v_k_norm, v_conv_w, v_w_attn_branch, v_w_conv_branch, v_w_out, v_norm_ffn2, v_ffn2_w_gate,
          v_ffn2_w_up, v_ffn2_w_down]
    wnames = ["w_ada", "b_ada", "norm_ffn1", "ffn1_w_gate", "ffn1_w_up", "ffn1_w_down", "norm_mix", "w_in",
              "q_norm", "k_norm", "conv_w", "w_attn_branch", "w_conv_branch", "w_out", "norm_ffn2",
              "ffn2_w_gate", "ffn2_w_up", "ffn2_w_down"]
    small = [i for i, gr in enumerate(grad_list) if gr.ndim == 2 and gr.size <= 16384]
    flat = lambda a, i: a.reshape(-1, weights[i].shape[-1])
    small_res = dict(zip(small, _adamw_small(
        [flat(weights[i], i) for i in small], [flat(grad_list[i], i) for i in small],
        [flat(ms[i], i) for i in small], [flat(vs[i], i) for i in small])))
    grad_out, deltas, new_ms, new_vs = [], [], [], []
    for idx, (nm, w, gr, m_, v_) in enumerate(zip(wnames, weights, grad_list, ms, vs)):
        if idx in small_res:
            gr, dl, nm_, nv_ = [r.reshape(w.shape) for r in (gr, *small_res[idx])]
        elif nm in as_rows:
            res = _adamw(f"adamw_{nm}", w[0].T, gr, m_[0].T, v_[0].T)
            gr, dl, nm_, nv_ = [r.T[None] for r in res]
        else:
            two_d = (-1, w.shape[-1])
            res = _adamw(f"adamw_{nm}", w.reshape(two_d), gr if gr.ndim == 3 else gr.reshape(two_d),
                         m_.reshape(two_d), v_.reshape(two_d))
            gr, dl, nm_, nv_ = [r.reshape(w.shape) for r in res]
        grad_out.append(gr)
        deltas.append(dl)
        new_ms.append(nm_)
        new_vs.append(nv_)
    return (loss, g0[None], *grad_out, *deltas, *new_ms, *new_vs)
```

```python
import functools

import jax
import jax.numpy as jnp
from jax import lax
from jax.experimental import pallas as pl
from jax.experimental.pallas import tpu as pltpu
from jax.experimental.pallas import tpu_sc as plsc

F32 = jnp.float32
BF16 = jnp.bfloat16
MESH = pl.DeviceIdType.MESH

N_DEV = 8
D = 1024
FF = 2816
HD = 128
N_HEADS = 4
DILATIONS = (1, 4, 16)
BAND = 128
QKW = 2 * 3 * N_HEADS * HD
IN_W = 9728
COL = 512
V_BLK, U_BLK, B_BLK, C_BLK, GA_BLK, GC_BLK = 6, 9, 11, 13, 15, 17
EPS = 1e-6
N_MOD = 9
ADAM_LR, ADAM_B1, ADAM_B2, ADAM_EPS, ADAM_WD, ADAM_STEP = 0.001, 0.9, 0.999, 1e-08, 0.01, 10

NT_DIMS = (((1,), (1,)), ((), ()))
TN_DIMS = (((0,), (0,)), ((), ()))
NN_DIMS = (((1,), (0,)), ((), ()))


def _place():
    return lax.axis_index("x"), lax.axis_index("y"), lax.axis_index("c")


def _flip(coord, bit):
    return 1 - coord if bit else coord


def _params(*sem):
    return pltpu.CompilerParams(dimension_semantics=sem)


def _small_allgather(name, v):
    n = v.shape[-1]

    def body(v_ref, out_ref, send_sems, recv_sems):
        x, y, c = _place()
        me = 4 * x + 2 * y + c
        out_ref[me] = v_ref[...]
        copies = []
        for k in range(1, N_DEV):
            peer = (_flip(x, (k >> 2) & 1), _flip(y, (k >> 1) & 1), _flip(c, k & 1))
            cp = pltpu.make_async_remote_copy(
                src_ref=v_ref, dst_ref=out_ref.at[me], send_sem=send_sems.at[k - 1],
                recv_sem=recv_sems.at[k - 1], device_id=peer, device_id_type=MESH)
            cp.start()
            copies.append(cp)
        for cp in copies:
            cp.wait()

    return pl.pallas_call(
        body, name=name,
        out_shape=jax.ShapeDtypeStruct((N_DEV, 1, n), F32),
        in_specs=[pl.BlockSpec(memory_space=pltpu.VMEM)],
        out_specs=pl.BlockSpec(memory_space=pltpu.VMEM),
        scratch_shapes=[pltpu.SemaphoreType.DMA((N_DEV - 1,)), pltpu.SemaphoreType.DMA((N_DEV - 1,))],
    )(v)


class _Plan:
    def __init__(self, operands, out_shapes, sems, phases):
        self.operands, self.out_shapes, self.sems, self.phases = operands, out_shapes, sems, phases


def _slab_start(base, rows, jump, idx):
    return pl.multiple_of(base + idx * rows + (idx // 4) * jump, 16)


def _gather_plan(shards, dst_of, base_of, dst_shapes, jump_of=None):
    n = len(shards)
    rows = [s.shape[0] for s in shards]
    jump_of = jump_of or [0] * n

    def phases(srcs, dsts, sems):
        send_sems, recv_sems, local_sems = sems
        x, y, c = _place()
        me, sibling = (x, y, c), (x, y, 1 - c)
        chips = [(1 - x, y), (x, 1 - y), (1 - x, 1 - y)]

        def slab(i, px, py, pc):
            start = _slab_start(base_of[i], rows[i], jump_of[i], 4 * px + 2 * py + pc)
            return dsts[dst_of[i]].at[pl.ds(start, rows[i])]

        def copy(i, k, block, to, src=None):
            return pltpu.make_async_remote_copy(
                src_ref=slab(i, *block) if src is None else src, dst_ref=slab(i, *block),
                send_sem=send_sems.at[i, k], recv_sem=recv_sems.at[i, k],
                device_id=to, device_id_type=MESH)

        def mine():
            return [pltpu.make_async_copy(srcs[i], slab(i, *me), local_sems.at[i]) for i in range(n)]

        def first():
            out = []
            for i in range(n):
                out.append(copy(i, 0, me, sibling, src=srcs[i]))
                out += [copy(i, 1 + j, me, (*chip, c), src=srcs[i]) for j, chip in enumerate(chips)]
            return out

        def passed():
            return [(copy(i, 1 + j, (*chip, c), me), copy(i, 4 + j, (*chip, c), sibling))
                    for j, chip in enumerate(chips) for i in range(n)]

        def start():
            for cp in mine() + first():
                cp.start()

        def middle():
            for landed, onward in passed():
                landed.wait_recv()
                onward.start()

        def finish():
            for i in range(n):
                copy(i, 0, sibling, me).wait_recv()
                for j, chip in enumerate(chips):
                    copy(i, 4 + j, (*chip, 1 - c), me).wait_recv()
            for cp in first() + [onward for _, onward in passed()]:
                cp.wait_send()
            for cp in mine():
                cp.wait()

        return start, middle, finish

    sems = [pltpu.SemaphoreType.DMA((n, 7)), pltpu.SemaphoreType.DMA((n, 7)), pltpu.SemaphoreType.DMA((n,))]
    return _Plan(list(shards), [jax.ShapeDtypeStruct(s, BF16) for s in dst_shapes], sems, phases)


def _scatter_plan(grads, src_of, base_of, rows, cols, jump_of=None):
    n = len(rows)
    jump_of = jump_of or [0] * n

    def phases(srcs, recvs, sems):
        send_sems, recv_sems, local_sems = sems
        x, y, c = _place()
        me = 4 * x + 2 * y + c

        def slab(i, idx):
            start = _slab_start(base_of[i], rows[i], jump_of[i], idx)
            return srcs[src_of[i]].at[pl.ds(start, rows[i])]

        def copies():
            out = [pltpu.make_async_copy(slab(i, me), recvs[i].at[me], local_sems.at[i]) for i in range(n)]
            for k in range(1, N_DEV):
                px, py, pc = _flip(x, (k >> 2) & 1), _flip(y, (k >> 1) & 1), _flip(c, k & 1)
                out += [pltpu.make_async_remote_copy(
                    src_ref=slab(i, 4 * px + 2 * py + pc), dst_ref=recvs[i].at[me],
                    send_sem=send_sems.at[i, k - 1], recv_sem=recv_sems.at[i, k - 1],
                    device_id=(px, py, pc), device_id_type=MESH) for i in range(n)]
            return out

        def start():
            for cp in copies():
                cp.start()

        def finish():
            for cp in copies():
                cp.wait()

        return start, None, finish

    sems = [pltpu.SemaphoreType.DMA((n, 7)), pltpu.SemaphoreType.DMA((n, 7)), pltpu.SemaphoreType.DMA((n,))]
    out_shapes = [jax.ShapeDtypeStruct((N_DEV, rows[i], cols[i]), BF16) for i in range(n)]
    return _Plan(list(grads), out_shapes, sems, phases)


def _run_plan(name, plan):
    n_in, n_out = len(plan.operands), len(plan.out_shapes)

    def body(*refs):
        for phase in plan.phases(refs[:n_in], refs[n_in:n_in + n_out], refs[n_in + n_out:]):
            if phase is not None:
                phase()

    hbm = pl.BlockSpec(memory_space=pltpu.HBM)
    return pl.pallas_call(
        body, name=name, out_shape=plan.out_shapes,
        in_specs=[hbm] * n_in, out_specs=[hbm] * n_out, scratch_shapes=plan.sems,
    )(*plan.operands)


def _run_plan_on_sequencer(name, plan, collective_id):
    src_refs = [jax.new_ref(a, memory_space=pltpu.MemorySpace.HBM) for a in plan.operands]
    dst_refs = [jax.empty_ref(s, memory_space=pltpu.MemorySpace.HBM) for s in plan.out_shapes]

    @pl.kernel(mesh=plsc.ScalarSubcoreMesh(axis_name="sequencer", num_cores=1), name=name,
               scratch_types=tuple(plan.sems),
               compiler_params=pltpu.CompilerParams(collective_id=collective_id))
    def launch(*sems):
        x, y, c = _place()
        barrier = pltpu.get_barrier_semaphore()
        for k in range(1, N_DEV):
            peer = (_flip(x, (k >> 2) & 1), _flip(y, (k >> 1) & 1), _flip(c, k & 1))
            pl.semaphore_signal(barrier, inc=1, device_id=peer, device_id_type=MESH)
        pl.semaphore_wait(barrier, N_DEV - 1)
        for phase in plan.phases(src_refs, dst_refs, sems):
            if phase is not None:
                phase()

    launch()
    return [r[...] for r in dst_refs]


def _sum_contributions(name, recv):
    _, rows, cols = recv.shape
    tr = rows if rows <= 512 else 304 if rows % 304 == 0 else 256

    def body(r_ref, o_ref):
        acc = r_ref[0].astype(F32)
        for k in range(1, N_DEV):
            acc = acc + r_ref[k].astype(F32)
        o_ref[...] = acc

    return pl.pallas_call(
        body, name=name, grid=(rows // tr,),
        out_shape=jax.ShapeDtypeStruct((rows, cols), F32),
        in_specs=[pl.BlockSpec((N_DEV, tr, cols), lambda i: (0, i, 0))],
        out_specs=pl.BlockSpec((tr, cols), lambda i: (i, 0)),
        compiler_params=_params("parallel"),
    )(recv)


def _mm(name, a, b, mode, out_dtype, tm, tn, tk, *, carry=None, tiles_in=(), tiles_out=(), epilogue=None,
        n_outer=False, keep_b=False, col_chunks=None, after=()):
    if mode == "TN":
        kk, m = a.shape
    else:
        m, kk = a.shape
    n = b.shape[0] if mode == "NT" else b.shape[1]
    tm, tn, tk = min(tm, m), min(tn, n), min(tk, kk)
    assert m % tm == 0 and n % tn == 0 and kk % tk == 0, (name, m, n, kk, tm, tn, tk)
    ni, nj, nk = m // tm, n // tn, kk // tk
    steps = ni * nj * nk
    dims = {"NN": NN_DIMS, "NT": NT_DIMS, "TN": TN_DIMS}[mode]
    if epilogue is None:
        tiles_out = [(jax.ShapeDtypeStruct((m, n), out_dtype), (tm, tn), lambda i, j: (i, j))]
    n_tin, n_tout = len(tiles_in), len(tiles_out)
    n_in = len(carry.operands) if carry else 0
    n_out = len(carry.out_shapes) if carry else 0
    n_acc = 1 if nk > 1 else 0
    n_keep = 2 if keep_b else 0
    n_after = len(after)
    assert not carry or steps >= 3
    assert not keep_b or (nk == 1 and nj == 1)
    assert not col_chunks or (epilogue is not None and nk == 1 and mode != "TN")
    ij = (lambda p, q: (q, p)) if n_outer else (lambda p, q: (p, q))
    inner = ni if n_outer else nj

    def body(a_ref, b_ref, *rest):
        tin = rest[:n_tin]
        cin = rest[n_tin:n_tin + n_in]
        rest = rest[n_tin + n_in + n_after:]
        tout = rest[:n_tout]
        cout = rest[n_tout:n_tout + n_out]
        scratch = rest[n_tout + n_out:]
        k = pl.program_id(2)
        visit = pl.program_id(0) * inner + pl.program_id(1)
        step = visit * nk + k
        if keep_b:
            b_kept, b_sem = scratch[n_acc:n_acc + 2]

            @pl.when(step == 0)
            def _():
                cp = pltpu.make_async_copy(b_ref, b_kept, b_sem)
                cp.start()
                cp.wait()

            b_ref = b_kept
        if carry:
            start, middle, finish = carry.phases(cin, cout, scratch[n_acc + n_keep:])
            pl.when(step == 0)(start)

        def store(prod, c=0, cols=()):
            if epilogue is None:
                tout[0][...] = prod.astype(out_dtype)
            else:
                epilogue(prod, jnp.logical_and(visit == 0, c == 0), tin, tout, *cols)

        if col_chunks:
            for c, (c0, cw) in enumerate(col_chunks):
                b_part = b_ref[pl.ds(c0, cw), :] if mode == "NT" else b_ref[:, pl.ds(c0, cw)]
                store(lax.dot_general(a_ref[...], b_part, dims, preferred_element_type=F32), c, ((c0, cw),))
        else:
            part = lax.dot_general(a_ref[...], b_ref[...], dims, preferred_element_type=F32)
            if nk == 1:
                store(part)
            else:
                acc_ref = scratch[0]

                @pl.when(k == 0)
                def _():
                    acc_ref[...] = part

                @pl.when((k > 0) & (k < nk - 1))
                def _():
                    acc_ref[...] += part

                @pl.when(k == nk - 1)
                def _():
                    store(acc_ref[...] + part)

        if carry:
            if middle is not None:
                pl.when(step == (steps * 3) // 5)(middle)
            pl.when(step == steps - 1)(finish)

    def spec(shape, fn):
        return pl.BlockSpec(shape, lambda p, q, k: fn(*ij(p, q)))

    a_spec = (pl.BlockSpec((tk, tm), lambda p, q, k: (k, ij(p, q)[0])) if mode == "TN"
              else pl.BlockSpec((tm, tk), lambda p, q, k: (ij(p, q)[0], k)))
    if keep_b:
        b_spec = pl.BlockSpec(memory_space=pl.ANY)
    elif mode == "NT":
        b_spec = pl.BlockSpec((tn, tk), lambda p, q, k: (ij(p, q)[1], k))
    else:
        b_spec = pl.BlockSpec((tk, tn), lambda p, q, k: (k, ij(p, q)[1]))
    hbm = pl.BlockSpec(memory_space=pltpu.HBM)
    sequential = carry or epilogue or keep_b
    out = pl.pallas_call(
        body, name=name, grid=(nj, ni, nk) if n_outer else (ni, nj, nk),
        out_shape=[t[0] for t in tiles_out] + (carry.out_shapes if carry else []),
        in_specs=([a_spec, b_spec] + [spec(t[1], t[2]) for t in tiles_in] + [hbm] * n_in
                  + [pl.BlockSpec(memory_space=pl.ANY)] * n_after),
        out_specs=[spec(t[1], t[2]) for t in tiles_out] + [hbm] * n_out,
        scratch_shapes=([pltpu.VMEM((tm, tn), F32)] * n_acc
                        + ([pltpu.VMEM(b.shape, b.dtype), pltpu.SemaphoreType.DMA] if keep_b else [])
                        + (carry.sems if carry else [])),
        compiler_params=(_params("arbitrary", "arbitrary", "arbitrary") if sequential
                         else _params("parallel", "parallel", "arbitrary")),
    )(a, b, *[t[0] for t in tiles_in], *(carry.operands if carry else []), *after)
    return out if (carry or epilogue) else out[0]


def _row(tm, w, off=0):
    return pl.BlockSpec((tm, w), lambda i: (i, off))


def _vec(w):
    return pl.BlockSpec((1, w), lambda i: (0, 0))


def _sigmoid(x):
    return 0.5 * jnp.tanh(0.5 * x) + 0.5


def _normmod(name, x, g, sc, sh, tm=512):
    s = x.shape[0]

    def body(x_ref, g_ref, sc_ref, sh_ref, h_ref):
        xv = x_ref[...]
        r = lax.rsqrt(jnp.mean(xv * xv, axis=-1, keepdims=True) + EPS)
        h_ref[...] = ((xv * r) * g_ref[...] * (1.0 + sc_ref[...]) + sh_ref[...]).astype(BF16)

    return pl.pallas_call(
        body, name=name, grid=(s // tm,),
        out_shape=jax.ShapeDtypeStruct((s, D), BF16),
        in_specs=[_row(tm, D), _vec(D), _vec(D), _vec(D)], out_specs=_row(tm, D),
        compiler_params=_params("parallel"),
    )(x, g, sc, sh)


def _normmod_bwd(name, dh, x, gin, g, sc, sh, tm=512):
    s = x.shape[0]

    def body(dh_ref, x_ref, gin_ref, g_ref, sc_ref, sh_ref, gout_ref, acc_ref):
        xv, dhv = x_ref[...], dh_ref[...]
        r = lax.rsqrt(jnp.mean(xv * xv, axis=-1, keepdims=True) + EPS)
        nv = xv * r
        gv, one_sc = g_ref[...], 1.0 + sc_ref[...]
        dn = dhv * gv * one_sc
        dx = r * (dn - nv * jnp.mean(dn * nv, axis=-1, keepdims=True))
        gout_ref[...] = gin_ref[...] + dx

        @pl.when(pl.program_id(0) == 0)
        def _():
            acc_ref[...] = jnp.zeros_like(acc_ref)

        dhn = dhv * nv
        acc_ref[0:1, :] += jnp.sum(dhv, axis=0, keepdims=True)
        acc_ref[1:2, :] += jnp.sum(dhn * gv, axis=0, keepdims=True)
        acc_ref[2:3, :] += jnp.sum(dhn * one_sc, axis=0, keepdims=True)

    return pl.pallas_call(
        body, name=name, grid=(s // tm,),
        out_shape=[jax.ShapeDtypeStruct((s, D), F32), jax.ShapeDtypeStruct((8, D), F32)],
        in_specs=[_row(tm, D), _row(tm, D), _row(tm, D), _vec(D), _vec(D), _vec(D)],
        out_specs=[_row(tm, D), pl.BlockSpec((8, D), lambda i: (0, 0))],
        compiler_params=_params("arbitrary"),
    )(dh, x, gin, g, sc, sh)


def _swiglu(name, ab, tm=512):
    s = ab.shape[0]

    def body(ab_ref, s_ref):
        a = ab_ref[:, :FF].astype(F32)
        b = ab_ref[:, FF:].astype(F32)
        s_ref[...] = (a * _sigmoid(a) * b).astype(BF16)

    return pl.pallas_call(
        body, name=name, grid=(s // tm,),
        out_shape=jax.ShapeDtypeStruct((s, FF), BF16),
        in_specs=[_row(tm, 2 * FF)], out_specs=_row(tm, FF),
        compiler_params=_params("parallel"),
    )(ab)


def _swiglu_bwd(name, ds, ab, tm=256):
    s = ab.shape[0]

    def body(ds_ref, ab_ref, dab_ref):
        a = ab_ref[:, :FF].astype(F32)
        b = ab_ref[:, FF:].astype(F32)
        dsv = ds_ref[...].astype(F32)
        sig = _sigmoid(a)
        dab_ref[:, :FF] = (dsv * b * (sig * (1.0 + a * (1.0 - sig)))).astype(BF16)
        dab_ref[:, FF:] = (dsv * (a * sig)).astype(BF16)

    return pl.pallas_call(
        body, name=name, grid=(s // tm,),
        out_shape=jax.ShapeDtypeStruct((s, 2 * FF), BF16),
        in_specs=[_row(tm, FF), _row(tm, 2 * FF)], out_specs=_row(tm, 2 * FF),
        compiler_params=_params("parallel"),
    )(ds, ab)


def _residual(name, x, f, gt, coef, tm=512):
    s = x.shape[0]

    def body(x_ref, f_ref, gt_ref, o_ref):
        o_ref[...] = x_ref[...] + (coef * gt_ref[...]) * f_ref[...]

    return pl.pallas_call(
        body, name=name, grid=(s // tm,),
        out_shape=jax.ShapeDtypeStruct((s, D), F32),
        in_specs=[_row(tm, D), _row(tm, D), _vec(D)], out_specs=_row(tm, D),
        compiler_params=_params("parallel"),
    )(x, f, gt)


def _gate_bwd(name, gin, f, gt, coef, tm=512):
    s = gin.shape[0]

    def body(g_ref, f_ref, gt_ref, df_ref, acc_ref):
        gv = g_ref[...]
        df_ref[...] = ((coef * gt_ref[...]) * gv).astype(BF16)

        @pl.when(pl.program_id(0) == 0)
        def _():
            acc_ref[...] = jnp.zeros_like(acc_ref)

        acc_ref[0:1, :] += coef * jnp.sum(gv * f_ref[...], axis=0, keepdims=True)

    return pl.pallas_call(
        body, name=name, grid=(s // tm,),
        out_shape=[jax.ShapeDtypeStruct((s, D), BF16), jax.ShapeDtypeStruct((8, D), F32)],
        in_specs=[_row(tm, D), _row(tm, D), _vec(D)],
        out_specs=[_row(tm, D), pl.BlockSpec((8, D), lambda i: (0, 0))],
        compiler_params=_params("arbitrary"),
    )(gin, f, gt)


def _loss_grad(x3, target, tm=512):
    s = x3.shape[0]

    def body(y_ref, t_ref, g_ref, l_ref):
        e = y_ref[...] - t_ref[...]
        g_ref[...] = e * (1.0 / D)

        @pl.when(pl.program_id(0) == 0)
        def _():
            l_ref[...] = jnp.zeros_like(l_ref)

        l_ref[...] += jnp.sum(jnp.mean(e * e, axis=-1, keepdims=True), axis=0, keepdims=True) * 0.5

    return pl.pallas_call(
        body, name="loss_grad", grid=(s // tm,),
        out_shape=[jax.ShapeDtypeStruct((s, D), F32), jax.ShapeDtypeStruct((8, 128), F32)],
        in_specs=[_row(tm, D), _row(tm, D)],
        out_specs=[_row(tm, D), pl.BlockSpec((8, 128), lambda i: (0, 0))],
        compiler_params=_params("arbitrary"),
    )(x3, target)


def _heads(x, fn):
    return jnp.concatenate([fn(x[:, h * HD:(h + 1) * HD], h) for h in range(COL // HD)], axis=1)


def _qknorm(proj, wqk, tm=1024):
    s = proj.shape[0]

    def body(p_ref, w_ref, o_ref):
        pv = p_ref[...].astype(F32)
        wv = w_ref[...]

        def one(qh, h):
            r = lax.rsqrt(jnp.mean(qh * qh, axis=-1, keepdims=True) + EPS)
            return (qh * r) * wv[:, h * HD:(h + 1) * HD]

        o_ref[...] = _heads(pv, one).astype(BF16)

    return pl.pallas_call(
        body, name="qknorm", grid=(s // tm, QKW // COL),
        out_shape=jax.ShapeDtypeStruct((s, QKW), BF16),
        in_specs=[pl.BlockSpec((tm, COL), lambda i, j: (i, j)), pl.BlockSpec((1, COL), lambda i, j: (0, j))],
        out_specs=pl.BlockSpec((tm, COL), lambda i, j: (i, j)),
        compiler_params=_params("parallel", "parallel"),
    )(proj, wqk)


def _qknorm_bwd(name, proj, dn, w, dproj, blk0, tm=1024):
    s = proj.shape[0]
    nblk = dn.shape[1] // COL

    def body(p_ref, d_ref, w_ref, _, o_ref, acc_ref):
        pv = p_ref[...].astype(F32)
        dv = d_ref[...]
        wv = w_ref[...]
        sums = []

        def one(qh, h):
            dn = dv[:, h * HD:(h + 1) * HD]
            r = lax.rsqrt(jnp.mean(qh * qh, axis=-1, keepdims=True) + EPS)
            nh = qh * r
            sums.append(jnp.sum(dn * nh, axis=0, keepdims=True))
            dnw = dn * wv[:, h * HD:(h + 1) * HD]
            return r * (dnw - nh * jnp.mean(dnw * nh, axis=-1, keepdims=True))

        o_ref[...] = _heads(pv, one).astype(BF16)

        @pl.when(pl.program_id(1) == 0)
        def _():
            acc_ref[...] = jnp.zeros_like(acc_ref)

        acc_ref[0:1, :] += jnp.concatenate(sums, axis=1)

    return pl.pallas_call(
        body, name=name, grid=(nblk, s // tm),
        out_shape=[jax.ShapeDtypeStruct((s, IN_W), BF16), jax.ShapeDtypeStruct((8, nblk * COL), F32)],
        in_specs=[pl.BlockSpec((tm, COL), lambda j, i: (i, blk0 + j)), pl.BlockSpec((tm, COL), lambda j, i: (i, j)),
                  pl.BlockSpec((1, COL), lambda j, i: (0, j)), pl.BlockSpec(memory_space=pl.ANY)],
        out_specs=[pl.BlockSpec((tm, COL), lambda j, i: (i, blk0 + j)),
                   pl.BlockSpec((8, COL), lambda j, i: (0, j))],
        input_output_aliases={3: 0},
        compiler_params=_params("arbitrary", "arbitrary"),
    )(proj, dn, w, dproj)


def _attn_shapes(s, g):
    d = DILATIONS[g]
    tb = min(s, max(2048, 256 * d))
    sb = min(256, tb // d)
    pb = BAND * d
    assert s % tb == 0 and tb % pb == 0 and (tb // d) % sb == 0 and sb % BAND == 0
    return d, tb, sb, pb


def _lanes(x, width):
    return jnp.concatenate([x] * (width // HD), axis=1)


def _every(start, size, d):
    return pl.ds(start, size, stride=d) if d > 1 else pl.ds(start, size)


def _attn_specs(g, tb, pb, s, ahead):
    ratio = tb // pb
    if ahead:
        nbr = lambda n: jnp.minimum((n + 1) * ratio, s // pb - 1)
    else:
        nbr = lambda n: jnp.maximum(n * ratio - 1, 0)
    cur = lambda base: pl.BlockSpec((tb, HD), lambda h, n: (n, base + g * N_HEADS + h))
    side = lambda base: pl.BlockSpec((pb, HD), lambda h, n: (nbr(n), base + g * N_HEADS + h))
    tok = pl.BlockSpec((tb, HD), lambda h, n: (n, h))
    tok_side = pl.BlockSpec((pb, HD), lambda h, n: (nbr(n), h))
    return cur, side, tok, tok_side


Q_COL, K_COL, V_COL = 0, 12, 24


def _attn_fwd(g, qkn, proj):
    s = qkn.shape[0]
    d, tb, sb, pb = _attn_shapes(s, g)
    ft = F32 if d > 1 else BF16
    nj = tb // d // sb
    scale = HD ** -0.5

    def body(q_ref, kc_ref, kp_ref, vc_ref, vp_ref, o_ref, lse_ref, qf, kf, vf):
        n = pl.program_id(1)
        qf[...] = q_ref[...].astype(ft)
        kf[0:pb] = kp_ref[...].astype(ft)
        kf[pb:] = kc_ref[...].astype(ft)
        vf[0:pb] = vp_ref[...].astype(ft)
        vf[pb:] = vc_ref[...].astype(ft)
        for r in range(d):
            for j in range(nj):
                at = j * sb * d + r
                q = qf[_every(at, sb, d), :].astype(BF16)
                k = kf[_every(at, sb + BAND, d), :].astype(BF16)
                v = vf[_every(at, sb + BAND, d), :].astype(BF16)
                sc = lax.dot_general(q, k, NT_DIMS, preferred_element_type=F32) * scale
                qi = lax.broadcasted_iota(jnp.int32, sc.shape, 0)
                kj = lax.broadcasted_iota(jnp.int32, sc.shape, 1)
                valid = (kj >= qi) & (kj <= qi + BAND)
                if j == 0:
                    valid = valid & ((kj >= BAND) | (n > 0))
                sc = jnp.where(valid, sc, -1e30)
                m = jnp.max(sc, axis=-1, keepdims=True)
                p = jnp.exp(sc - m)
                l = jnp.sum(p, axis=-1, keepdims=True)
                o = lax.dot_general(p.astype(BF16), v, NN_DIMS, preferred_element_type=F32)
                o_ref[_every(at, sb, d), :] = o / l
                lse_ref[_every(at, sb, d), :] = jnp.broadcast_to(m + jnp.log(l), (sb, HD))

    cur, side, tok, _ = _attn_specs(g, tb, pb, s, ahead=False)
    return pl.pallas_call(
        body, name=f"attn_fwd_g{g}", grid=(N_HEADS, s // tb),
        out_shape=[jax.ShapeDtypeStruct((s, COL), F32)] * 2,
        in_specs=[cur(Q_COL), cur(K_COL), side(K_COL), cur(V_COL), side(V_COL)],
        out_specs=[tok, tok],
        scratch_shapes=[pltpu.VMEM((tb, HD), ft), pltpu.VMEM((tb + pb, HD), ft),
                        pltpu.VMEM((tb + pb, HD), ft)],
        compiler_params=_params("parallel", "arbitrary"),
    )(qkn, qkn, qkn, proj, proj)


def _attn_combine(os_, lses, tm=512):
    s = os_[0].shape[0]

    def body(o0, o1, o2, l0, l1, l2, o_ref, lse_ref):
        a, b, c = l0[...], l1[...], l2[...]
        m = jnp.maximum(jnp.maximum(a, b), c)
        ea, eb, ec = jnp.exp(a - m), jnp.exp(b - m), jnp.exp(c - m)
        tot = ea + eb + ec
        o_ref[...] = ((ea * o0[...] + eb * o1[...] + ec * o2[...]) / tot).astype(BF16)
        lse_ref[...] = m + jnp.log(tot)

    return pl.pallas_call(
        body, name="attn_combine", grid=(s // tm,),
        out_shape=[jax.ShapeDtypeStruct((s, COL), BF16), jax.ShapeDtypeStruct((s, COL), F32)],
        in_specs=[_row(tm, COL)] * 6, out_specs=[_row(tm, COL)] * 2,
        compiler_params=_params("parallel"),
    )(*os_, *lses)


def _attn_delta(do, o, tm=512):
    s = do.shape[0]

    def body(do_ref, o_ref, del_ref):
        prod = do_ref[...] * o_ref[...].astype(F32)
        del_ref[...] = _heads(prod, lambda ph, h: jnp.broadcast_to(
            jnp.sum(ph, axis=-1, keepdims=True), ph.shape))

    return pl.pallas_call(
        body, name="attn_delta", grid=(s // tm,),
        out_shape=jax.ShapeDtypeStruct((s, COL), F32),
        in_specs=[_row(tm, COL)] * 2, out_specs=_row(tm, COL),
        compiler_params=_params("parallel"),
    )(do, o)


def _attn_bwd(g, qkn, proj, do, lse, delta, dqn, dkn, dproj):
    s = qkn.shape[0]
    d, tb, sb, pb = _attn_shapes(s, g)
    ft = F32 if d > 1 else BF16
    nj = tb // d // sb
    nt = s // tb
    scale = HD ** -0.5
    chained = dqn is not None

    def body(k_ref, v_ref, qc_ref, qn_ref, doc_ref, don_ref, lc_ref, ln_ref, dc_ref, dn_ref, *rest):
        dq_ref, dk_ref, dv_ref, kf, vf, qf, dvf, later = rest[-8:]
        n = pl.program_id(1)
        kf[...] = k_ref[...].astype(ft)
        vf[...] = v_ref[...].astype(ft)
        qf[0:tb] = qc_ref[...].astype(ft)
        qf[tb:] = qn_ref[...].astype(ft)

        @pl.when(n == 0)
        def _():
            later[...] = jnp.zeros_like(later)

        def window(c_ref, n_ref, r, j):
            at = j * sb * d + r
            if j < nj - 1:
                return c_ref[_every(at, sb + BAND, d), :]
            return jnp.concatenate([c_ref[_every(at, sb, d), :], n_ref[_every(r, BAND, d), :]], axis=0)

        for r in range(d):
            tail = later[r]
            for j in range(nj):
                at = j * sb * d + r
                rows = _every(at, sb, d)
                k = kf[rows, :].astype(BF16)
                v = vf[rows, :].astype(BF16)
                q = qf[_every(at, sb + BAND, d), :].astype(BF16)
                dov = window(doc_ref, don_ref, r, j).astype(BF16)
                sc = lax.dot_general(q, k, NT_DIMS, preferred_element_type=F32) * scale
                qi = lax.broadcasted_iota(jnp.int32, sc.shape, 0)
                kj = lax.broadcasted_iota(jnp.int32, sc.shape, 1)
                valid = (qi >= kj) & (qi <= kj + BAND)
                if j == nj - 1:
                    valid = valid & ((qi < sb) | (n < nt - 1))
                p = jnp.exp(jnp.where(valid, sc - _lanes(window(lc_ref, ln_ref, r, j), sb), -1e30))
                dp = lax.dot_general(dov, v, NT_DIMS, preferred_element_type=F32)
                ds = (p * (dp - _lanes(window(dc_ref, dn_ref, r, j), sb)) * scale).astype(BF16)
                dvf[rows, :] = lax.dot_general(p.astype(BF16), dov, TN_DIMS, preferred_element_type=F32)
                dk_ref[rows, :] = lax.dot_general(ds, q, TN_DIMS, preferred_element_type=F32)
                dqw = lax.dot_general(ds, k, NN_DIMS, preferred_element_type=F32)
                first = dqw[:BAND] + tail
                dq_ref[rows, :] = first if sb == BAND else jnp.concatenate([first, dqw[BAND:sb]], axis=0)
                tail = dqw[sb:]
            later[r] = tail
        dv_ref[...] = dvf[...].astype(BF16)

    cur, side, tok, tok_side = _attn_specs(g, tb, pb, s, ahead=True)
    anyspec = pl.BlockSpec(memory_space=pl.ANY)
    n_heads_cols = 3 * N_HEADS * HD
    return pl.pallas_call(
        body, name=f"attn_bwd_g{g}", grid=(N_HEADS, nt),
        out_shape=[jax.ShapeDtypeStruct((s, n_heads_cols), F32), jax.ShapeDtypeStruct((s, n_heads_cols), F32),
                   jax.ShapeDtypeStruct((s, IN_W), BF16)],
        in_specs=[cur(K_COL), cur(V_COL), cur(Q_COL), side(Q_COL), tok, tok_side, tok, tok_side,
                  tok, tok_side] + ([anyspec, anyspec] if chained else []) + [anyspec],
        out_specs=[cur(0), cur(0), cur(V_COL)],
        input_output_aliases={10: 0, 11: 1, 12: 2} if chained else {10: 2},
        scratch_shapes=[pltpu.VMEM((tb, HD), ft), pltpu.VMEM((tb, HD), ft),
                        pltpu.VMEM((tb + pb, HD), ft), pltpu.VMEM((tb, HD), F32),
                        pltpu.VMEM((d, BAND, HD), F32)],
        compiler_params=_params("arbitrary", "arbitrary"),
    )(qkn, proj, qkn, qkn, do, do, lse, lse, delta, delta, *([dqn, dkn] if chained else []), dproj)


def _shift_down(x, before, k):
    rolled = pltpu.roll(x, k, 0)
    head = jnp.where(lax.broadcasted_iota(jnp.int32, before.shape, 0) < k, pltpu.roll(before, k, 0), rolled[:8])
    return jnp.concatenate([head, rolled[8:]], axis=0)


def _shift_up(x, after, k):
    rows = x.shape[0]
    rolled = pltpu.roll(x, rows - k, 0)
    tail = jnp.where(lax.broadcasted_iota(jnp.int32, after.shape, 0) >= 8 - k,
                     pltpu.roll(after, 8 - k, 0), rolled[rows - 8:])
    return jnp.concatenate([rolled[:rows - 8], tail], axis=0)


def _conv_fwd(proj, cw, tm=1024):
    s = proj.shape[0]
    r16 = tm // 16

    def body(u_ref, b_ref, c_ref, up_ref, cp_ref, w_ref, z_ref):
        i = pl.program_id(1)
        xc = c_ref[...].astype(F32) * u_ref[...].astype(F32)
        xp = jnp.where(i > 0, cp_ref[8:16, :].astype(F32) * up_ref[8:16, :].astype(F32), 0.0)
        w = w_ref[...]
        conv = _shift_down(xc, xp, 2) * w[0:1] + _shift_down(xc, xp, 1) * w[1:2] + xc * w[2:3]
        z_ref[...] = (b_ref[...].astype(F32) * conv).astype(BF16)

    tile = lambda blk: pl.BlockSpec((tm, COL), lambda j, i: (i, blk + j))
    before = lambda blk: pl.BlockSpec((16, COL), lambda j, i: (jnp.maximum(i * r16 - 1, 0), blk + j))
    return pl.pallas_call(
        body, name="conv_fwd", grid=(D // COL, s // tm),
        out_shape=jax.ShapeDtypeStruct((s, D), BF16),
        in_specs=[tile(U_BLK), tile(B_BLK), tile(C_BLK), before(U_BLK), before(C_BLK),
                  pl.BlockSpec((3, COL), lambda j, i: (0, j))],
        out_specs=pl.BlockSpec((tm, COL), lambda j, i: (i, j)),
        compiler_params=_params("parallel", "parallel"),
    )(proj, proj, proj, proj, proj, cw)


def _conv_bwd(dz, proj, cw, dproj, tm=1024):
    s = proj.shape[0]
    r16 = tm // 16
    nrow = s // tm

    def body(dz_ref, u_ref, b_ref, c_ref, up_ref, cp_ref, dzn_ref, bn_ref, w_ref, _, o_ref, dc_ref, acc_ref):
        piece, i = pl.program_id(1), pl.program_id(2)
        u, c = u_ref[...].astype(F32), c_ref[...].astype(F32)
        bv = b_ref[...].astype(F32)
        dzv = dz_ref[...]
        w = w_ref[...]

        @pl.when((piece == 0) & (i == 0))
        def _():
            acc_ref[...] = jnp.zeros_like(acc_ref)

        @pl.when(piece == 0)
        def _():
            xc = c * u
            xp = jnp.where(i > 0, cp_ref[8:16, :].astype(F32) * up_ref[8:16, :].astype(F32), 0.0)
            x2, x1 = _shift_down(xc, xp, 2), _shift_down(xc, xp, 1)
            o_ref[...] = (dzv * (x2 * w[0:1] + x1 * w[1:2] + xc * w[2:3])).astype(BF16)
            dc_ref[...] = jnp.zeros_like(dc_ref)
            dconv = dzv * bv
            acc_ref[0:1, :] += jnp.sum(dconv * x2, axis=0, keepdims=True)
            acc_ref[1:2, :] += jnp.sum(dconv * x1, axis=0, keepdims=True)
            acc_ref[2:3, :] += jnp.sum(dconv * xc, axis=0, keepdims=True)

        @pl.when(piece == 1)
        def _():
            dconv = dzv * bv
            dn = jnp.where(i < nrow - 1, dzn_ref[...] * bn_ref[0:8, :].astype(F32), 0.0)
            dxc = dconv * w[2:3] + _shift_up(dconv, dn, 1) * w[1:2] + _shift_up(dconv, dn, 2) * w[0:1]
            o_ref[...] = (dxc * c).astype(BF16)
            dc_ref[...] = (dxc * u).astype(BF16)

    tile = lambda blk: pl.BlockSpec((tm, COL), lambda j, p, i: (i, blk + j))
    before = lambda blk: pl.BlockSpec((16, COL), lambda j, p, i: (jnp.maximum(i * r16 - 1, 0), blk + j))
    after = lambda rows, blk: pl.BlockSpec(
        (rows, COL), lambda j, p, i: (jnp.minimum((i + 1) * (tm // rows), s // rows - 1), blk + j))
    return pl.pallas_call(
        body, name="conv_bwd", grid=(D // COL, 2, nrow),
        out_shape=[jax.ShapeDtypeStruct((s, IN_W), BF16), jax.ShapeDtypeStruct((s + tm, D), BF16),
                   jax.ShapeDtypeStruct((8, D), F32)],
        in_specs=[tile(0), tile(U_BLK), tile(B_BLK), tile(C_BLK), before(U_BLK), before(C_BLK),
                  after(8, 0), after(16, B_BLK), pl.BlockSpec((3, COL), lambda j, p, i: (0, j)),
                  pl.BlockSpec(memory_space=pl.ANY)],
        out_specs=[pl.BlockSpec((tm, COL), lambda j, p, i: (i, jnp.where(p == 0, B_BLK, U_BLK) + j)),
                   pl.BlockSpec((tm, COL), lambda j, p, i: (jnp.where(p == 0, nrow, i), j)),
                   pl.BlockSpec((8, COL), lambda j, p, i: (0, j))],
        input_output_aliases={9: 0},
        compiler_params=_params("arbitrary", "arbitrary", "arbitrary"),
    )(dz, proj, proj, proj, proj, proj, dz, proj, cw, dproj)


def _copy_columns(name, src, dst, blk0, tm=1024):
    s, w = dst.shape[0], src.shape[1]
    fresh = isinstance(dst, jax.ShapeDtypeStruct)

    def body(x_ref, *rest):
        rest[-1][...] = x_ref[...]

    return pl.pallas_call(
        body, name=name, grid=(w // COL, s // tm),
        out_shape=jax.ShapeDtypeStruct(dst.shape, dst.dtype),
        in_specs=[pl.BlockSpec((tm, COL), lambda j, i: (i, j))] + ([] if fresh else [pl.BlockSpec(memory_space=pl.ANY)]),
        out_specs=pl.BlockSpec((tm, COL), lambda j, i: (i, blk0 + j)),
        input_output_aliases={} if fresh else {1: 0},
        compiler_params=_params("parallel", "parallel"),
    )(src, *([] if fresh else [dst]))


def _merge_fwd(ya, yc, proj, tm=512):
    s = proj.shape[0]

    def body(ya_ref, yc_ref, ga_ref, gc_ref, o_ref):
        o_ref[...] = (_sigmoid(ga_ref[...].astype(F32)) * ya_ref[...].astype(F32)
                      + _sigmoid(gc_ref[...].astype(F32)) * yc_ref[...].astype(F32)).astype(BF16)

    tile = lambda blk: pl.BlockSpec((tm, COL), lambda j, i: (i, blk + j))
    return pl.pallas_call(
        body, name="merge_fwd", grid=(D // COL, s // tm),
        out_shape=jax.ShapeDtypeStruct((s, D), BF16),
        in_specs=[tile(0), tile(0), tile(GA_BLK), tile(GC_BLK)], out_specs=tile(0),
        compiler_params=_params("parallel", "parallel"),
    )(ya, yc, proj, proj)


def _merge_bwd_branches(dm, proj, tm=512):
    s = proj.shape[0]

    def body(dm_ref, ga_ref, gc_ref, dya_ref, dyc_ref):
        dmv = dm_ref[...]
        dya_ref[...] = (dmv * _sigmoid(ga_ref[...].astype(F32))).astype(BF16)
        dyc_ref[...] = (dmv * _sigmoid(gc_ref[...].astype(F32))).astype(BF16)

    tile = lambda blk: pl.BlockSpec((tm, COL), lambda j, i: (i, blk + j))
    return pl.pallas_call(
        body, name="merge_bwd_branches", grid=(D // COL, s // tm),
        out_shape=[jax.ShapeDtypeStruct((s, D), BF16)] * 2,
        in_specs=[tile(0), tile(GA_BLK), tile(GC_BLK)], out_specs=[tile(0)] * 2,
        compiler_params=_params("parallel", "parallel"),
    )(dm, proj, proj)


def _merge_bwd_gates(dm, ya, yc, proj, tm=1024):
    s = proj.shape[0]
    half = D // COL

    def body(dm_ref, ya_ref, yc_ref, g_ref, o_ref):
        y = jnp.where(pl.program_id(0) < half, ya_ref[...].astype(F32), yc_ref[...].astype(F32))
        sig = _sigmoid(g_ref[...].astype(F32))
        o_ref[...] = (dm_ref[...] * y * sig * (1.0 - sig)).astype(BF16)

    chan = pl.BlockSpec((tm, COL), lambda jj, i: (i, jj % half))
    gate = pl.BlockSpec((tm, COL), lambda jj, i: (i, GA_BLK + jj))
    return pl.pallas_call(
        body, name="merge_bwd_gates", grid=(2 * half, s // tm),
        out_shape=jax.ShapeDtypeStruct((s, IN_W), BF16),
        in_specs=[chan, chan, chan, gate], out_specs=gate,
        compiler_params=_params("parallel", "parallel"),
    )(dm, ya, yc, proj)


def _mod_part(c_all, w_ada, b_part):
    def body(c_ref, w_ref, b_ref, o_ref):
        cv = c_ref[...]
        act = cv * _sigmoid(cv)
        o_ref[...] = jnp.dot(act, w_ref[...], preferred_element_type=F32,
                             precision=lax.Precision.HIGHEST) + b_ref[...]

    return pl.pallas_call(
        body, name="mod_part", out_shape=jax.ShapeDtypeStruct((N_DEV, w_ada.shape[1]), F32),
    )(c_all, w_ada, b_part)


def _w_ada_grad(c_all_t, dmod_part):
    def body(c_ref, d_ref, o_ref):
        cv = c_ref[...]
        act = cv * _sigmoid(cv)
        dv = d_ref[...]
        acc = act[:, 0:1] * dv[0:1, :]
        for b in range(1, N_DEV):
            acc = acc + act[:, b:b + 1] * dv[b:b + 1, :]
        o_ref[...] = acc

    return pl.pallas_call(
        body, name="w_ada_grad", out_shape=jax.ShapeDtypeStruct((D, dmod_part.shape[1]), F32),
    )(c_all_t, dmod_part)


def _sum_rows(name, v):
    def body(v_ref, o_ref):
        acc = v_ref[0]
        for k in range(1, N_DEV):
            acc = acc + v_ref[k]
        o_ref[...] = acc

    return pl.pallas_call(body, name=name, out_shape=jax.ShapeDtypeStruct(v.shape[1:], F32))(v)


def _adamw(name, w, g, m, v):
    rows, cols = w.shape
    limit = max(16, (1 << 20) // (4 * cols))
    tr = rows if rows <= limit else next((t for t in range(limit - limit % 16, 15, -16) if rows % t == 0), rows)
    c1 = 1.0 - ADAM_B1 ** ADAM_STEP
    c2 = 1.0 - ADAM_B2 ** ADAM_STEP
    parts = g.ndim == 3

    def body(w_ref, g_ref, m_ref, v_ref, go_ref, d_ref, nm_ref, nv_ref):
        if parts:
            gv = g_ref[0].astype(F32)
            for k in range(1, N_DEV):
                gv = gv + g_ref[k].astype(F32)
        else:
            gv = g_ref[...]
        go_ref[...] = gv
        nm = ADAM_B1 * m_ref[...] + (1.0 - ADAM_B1) * gv
        nv = ADAM_B2 * v_ref[...] + (1.0 - ADAM_B2) * (gv * gv)
        nm_ref[...] = nm
        nv_ref[...] = nv
        d_ref[...] = -ADAM_LR * ((nm / c1) / (jnp.sqrt(nv / c2) + ADAM_EPS) + ADAM_WD * w_ref[...])

    spec = pl.BlockSpec((tr, cols), lambda i: (i, 0))
    g_spec = pl.BlockSpec((N_DEV, tr, cols), lambda i: (0, i, 0)) if parts else spec
    return pl.pallas_call(
        body, name=name, grid=(rows // tr,),
        out_shape=[jax.ShapeDtypeStruct((rows, cols), F32)] * 4,
        in_specs=[spec, g_spec, spec, spec], out_specs=[spec] * 4,
        compiler_params=_params("parallel"),
    )(w, g, m, v)


def _adamw_small(ws, gs, ms, vs):
    n = len(ws)
    c1 = 1.0 - ADAM_B1 ** ADAM_STEP
    c2 = 1.0 - ADAM_B2 ** ADAM_STEP

    def body(*refs):
        for i in range(n):
            w_ref, g_ref, m_ref, v_ref = refs[i], refs[n + i], refs[2 * n + i], refs[3 * n + i]
            d_ref, nm_ref, nv_ref = refs[4 * n + 3 * i:4 * n + 3 * i + 3]
            gv = g_ref[...]
            nm = ADAM_B1 * m_ref[...] + (1.0 - ADAM_B1) * gv
            nv = ADAM_B2 * v_ref[...] + (1.0 - ADAM_B2) * (gv * gv)
            nm_ref[...] = nm
            nv_ref[...] = nv
            d_ref[...] = -ADAM_LR * ((nm / c1) / (jnp.sqrt(nv / c2) + ADAM_EPS) + ADAM_WD * w_ref[...])

    outs = pl.pallas_call(
        body, name="adamw_small",
        out_shape=[jax.ShapeDtypeStruct(w.shape, F32) for w in ws for _ in range(3)],
    )(*ws, *gs, *ms, *vs)
    return [tuple(outs[3 * i:3 * i + 3]) for i in range(n)]


HALF = FF // 2


def _sds(shape, dtype):
    return jax.ShapeDtypeStruct(shape, dtype)


def _row_tile(w):
    return lambda tm: ((tm, w), lambda i, j: (i, 0))


def _one(w):
    return lambda rows: ((rows, w), lambda i, j: (0, 0))


def _gate_up_swiglu(name, h, wgu, carry=None, tm=512):
    s = h.shape[0]
    tm = min(tm, s)

    def epilogue(prod, first, tin, tout):
        pq_ref, s_ref = tout
        a, b = prod[:, :HALF], prod[:, HALF:]
        sig = _sigmoid(a)
        act = a * sig
        pq_ref[:, :HALF] = (b * (sig * (1.0 + a * (1.0 - sig)))).astype(BF16)
        pq_ref[:, HALF:] = act.astype(BF16)
        s_ref[...] = (act * b).astype(BF16)

    return _mm(name, h, wgu, "NT", None, tm, FF, D, carry=carry, n_outer=True, epilogue=epilogue,
               tiles_out=[(_sds((s, 2 * FF), BF16), (tm, FF), lambda i, j: (i, j)),
                          (_sds((s, FF), BF16), (tm, HALF), lambda i, j: (i, j))])


def _d_hidden_swiglu(name, df, wd, ab, after=(), tm=512):
    s = df.shape[0]
    tm = min(tm, s)

    def epilogue(prod, first, tin, tout, cols):
        da_cols = slice(cols[0], cols[0] + cols[1])
        db_cols = slice(HALF + cols[0], HALF + cols[0] + cols[1])
        tout[0][:, da_cols] = (prod * tin[0][:, da_cols].astype(F32)).astype(BF16)
        tout[0][:, db_cols] = (prod * tin[0][:, db_cols].astype(F32)).astype(BF16)

    chunks = [(c0, min(384, HALF - c0)) for c0 in range(0, HALF, 384)]
    return _mm(name, df, wd, "NT", None, tm, HALF, D, n_outer=True, epilogue=epilogue, col_chunks=chunks, after=after,
               tiles_in=[(ab, (tm, FF), lambda i, j: (i, j))],
               tiles_out=[(_sds((s, 2 * FF), BF16), (tm, FF), lambda i, j: (i, j))])[0]


def _out_residual(name, a, w, x, gt, coef, nxt, tm=512, tk=FF):
    s = a.shape[0]
    tm = min(tm, s)

    def epilogue(prod, first, tin, tout):
        x_ref, gt_ref, g_ref, sc_ref, sh_ref = tin
        f_ref, xn_ref, h_ref = tout
        f_ref[...] = prod
        xn = x_ref[...] + (coef * gt_ref[...]) * prod
        xn_ref[...] = xn
        r = lax.rsqrt(jnp.mean(xn * xn, axis=-1, keepdims=True) + EPS)
        h_ref[...] = ((xn * r) * g_ref[...] * (1.0 + sc_ref[...]) + sh_ref[...]).astype(BF16)

    row, vec = _row_tile(D)(tm), _one(D)(1)
    return _mm(name, a, w, "NN", None, tm, D, tk, epilogue=epilogue,
               tiles_in=[(x, *row), (gt, *vec)] + [(v, *vec) for v in nxt],
               tiles_out=[(_sds((s, D), F32), *row), (_sds((s, D), F32), *row), (_sds((s, D), BF16), *row)])


def _out_loss(name, a, w, x, gt, coef, target, tm=512):
    s = a.shape[0]
    tm = min(tm, s)

    def epilogue(prod, first, tin, tout):
        x_ref, gt_ref, t_ref = tin
        f_ref, g_ref, df_ref, acc_ref = tout
        f_ref[...] = prod
        cg = coef * gt_ref[...]
        e = x_ref[...] + cg * prod - t_ref[...]
        gv = e * (1.0 / D)
        g_ref[...] = gv
        df_ref[...] = (cg * gv).astype(BF16)

        @pl.when(first)
        def _():
            acc_ref[...] = jnp.zeros_like(acc_ref)

        acc_ref[0:1, :] += coef * jnp.sum(gv * prod, axis=0, keepdims=True)
        acc_ref[1:2, :] += (0.5 / D) * jnp.sum(e * e, axis=0, keepdims=True)

    row, vec = _row_tile(D)(tm), _one(D)(1)
    return _mm(name, a, w, "NN", None, tm, D, FF, epilogue=epilogue,
               tiles_in=[(x, *row), (gt, *vec), (target, *row)],
               tiles_out=[(_sds((s, D), F32), *row), (_sds((s, D), F32), *row), (_sds((s, D), BF16), *row),
                          (_sds((8, D), F32), *_one(D)(8))])


def _d_h_norm_bwd(name, da, w, x, gin, g, sc, sh, before=None, carry=None, after=(), tm=256):
    s = da.shape[0]
    tm = min(tm, s)
    coef = before[2] if before else None

    def epilogue(prod, first, tin, tout):
        x_ref, gin_ref, g_ref, sc_ref, sh_ref = tin[:5]
        gout_ref, acc_ref = tout[:2]
        xv = x_ref[...]
        r = lax.rsqrt(jnp.mean(xv * xv, axis=-1, keepdims=True) + EPS)
        nv = xv * r
        gv, one_sc = g_ref[...], 1.0 + sc_ref[...]
        dn = prod * gv * one_sc
        gout = gin_ref[...] + r * (dn - nv * jnp.mean(dn * nv, axis=-1, keepdims=True))
        gout_ref[...] = gout

        @pl.when(first)
        def _():
            acc_ref[...] = jnp.zeros_like(acc_ref)

        dhn = prod * nv
        acc_ref[0:1, :] += jnp.sum(prod, axis=0, keepdims=True)
        acc_ref[1:2, :] += jnp.sum(dhn * gv, axis=0, keepdims=True)
        acc_ref[2:3, :] += jnp.sum(dhn * one_sc, axis=0, keepdims=True)
        if before:
            f_ref, gt_ref = tin[5:]
            tout[2][...] = ((coef * gt_ref[...]) * gout).astype(BF16)
            acc_ref[3:4, :] += coef * jnp.sum(gout * f_ref[...], axis=0, keepdims=True)

    row, vec = _row_tile(D)(tm), _one(D)(1)
    tiles_in = [(x, *row), (gin, *row), (g, *vec), (sc, *vec), (sh, *vec)]
    tiles_out = [(_sds((s, D), F32), *row), (_sds((8, D), F32), *_one(D)(8))]
    if before:
        tiles_in += [(before[0], *row), (before[1], *vec)]
        tiles_out.append((_sds((s, D), BF16), *row))
    return _mm(name, da, w, "NN", None, tm, D, da.shape[1], epilogue=epilogue, carry=carry, keep_b=True, after=after,
               tiles_in=tiles_in, tiles_out=tiles_out)


def _gate_tiles(proj, tm):
    return [(proj, (tm, COL), (lambda i, j, blk=blk: (i, blk))) for blk in (GA_BLK, GA_BLK + 1, GC_BLK, GC_BLK + 1)]


def _conv_branch_merge(z, wc, ya, proj, tm=1024):
    s = z.shape[0]
    tm = min(tm, s)

    def epilogue(prod, first, tin, tout):
        ya_ref, ga0, ga1, gc0, gc1 = tin
        tout[0][...] = prod.astype(BF16)
        for half, (ga, gc) in enumerate(((ga0, gc0), (ga1, gc1))):
            cols = slice(half * COL, (half + 1) * COL)
            tout[1][:, cols] = (_sigmoid(ga[...].astype(F32)) * ya_ref[:, cols].astype(F32)
                                + _sigmoid(gc[...].astype(F32)) * prod[:, cols]).astype(BF16)

    row = _row_tile(D)(tm)
    return _mm("mix_conv_branch", z, wc, "NN", None, tm, D, D, epilogue=epilogue,
               tiles_in=[(ya, *row)] + _gate_tiles(proj, tm),
               tiles_out=[(_sds((s, D), BF16), *row), (_sds((s, D), BF16), *row)])


def _d_merged_branches(dmix, wo, ya, yc, proj, tm=1024):
    s = dmix.shape[0]
    tm = min(tm, s)

    def epilogue(prod, first, tin, tout):
        ya_ref, yc_ref, ga0, ga1, gc0, gc1 = tin
        dya_ref, dyc_ref, dg_ref = tout
        for half, (ga, gc) in enumerate(((ga0, gc0), (ga1, gc1))):
            cols = slice(half * COL, (half + 1) * COL)
            dm = prod[:, cols]
            for y_ref, g_ref, dy_ref, off in ((ya_ref, ga, dya_ref, 0), (yc_ref, gc, dyc_ref, D)):
                sig = _sigmoid(g_ref[...].astype(F32))
                dms = dm * sig
                dy_ref[:, cols] = dms.astype(BF16)
                dg_ref[:, off + half * COL:off + (half + 1) * COL] = (
                    dms * y_ref[:, cols].astype(F32) * (1.0 - sig)).astype(BF16)

    row = _row_tile(D)(tm)
    return _mm("mix_d_merged", dmix, wo, "NT", None, tm, D, D, epilogue=epilogue,
               tiles_in=[(ya, *row), (yc, *row)] + _gate_tiles(proj, tm),
               tiles_out=[(_sds((s, D), BF16), *row), (_sds((s, D), BF16), *row),
                          (_sds((s, 2 * D), BF16), *_row_tile(2 * D)(tm))])


def _d_o_delta(dya, wa_t, o, tm=1024):
    s = dya.shape[0]
    tm = min(tm, s)

    def epilogue(prod, first, tin, tout):
        tout[0][...] = prod
        tout[1][...] = _heads(prod * tin[0][...].astype(F32), lambda ph, h: jnp.broadcast_to(
            jnp.sum(ph, axis=-1, keepdims=True), ph.shape))

    row = _row_tile(COL)(tm)
    return _mm("mix_d_o", dya, wa_t, "NN", None, tm, COL, D, epilogue=epilogue,
               tiles_in=[(o, *row)], tiles_out=[(_sds((s, COL), F32), *row), (_sds((s, COL), F32), *row)])


def _ffn_bwd(tag, df, x, gin, h, ab, sw, g, sc, sh, wgu, wd, before=None, tk_dw=2048):
    dwd = _mm(f"{tag}_dw_down", sw, df, "TN", BF16, HALF, D, tk_dw)
    dab = _d_hidden_swiglu(f"{tag}_d_hidden", df, wd, ab, after=[dwd])
    dwgu = _mm(f"{tag}_dw_gate_up", dab, h, "TN", BF16, HALF, D, tk_dw)
    res = _d_h_norm_bwd(f"{tag}_d_h", dab, wgu, x, gin, g, sc, sh, before=before, after=[dwgu], tm=512)
    return res, dwgu, dwd


def kernel(x, c, w_ada, b_ada, norm_ffn1, ffn1_w_gate, ffn1_w_up, ffn1_w_down, norm_mix, w_in, q_norm, k_norm, conv_w, w_attn_branch, w_conv_branch, w_out, norm_ffn2, ffn2_w_gate, ffn2_w_up, ffn2_w_down, loss_target, m_w_ada, m_b_ada, m_norm_ffn1, m_ffn1_w_gate, m_ffn1_w_up, m_ffn1_w_down, m_norm_mix, m_w_in, m_q_norm, m_k_norm, m_conv_w, m_w_attn_branch, m_w_conv_branch, m_w_out, m_norm_ffn2, m_ffn2_w_gate, m_ffn2_w_up, m_ffn2_w_down, v_w_ada, v_b_ada, v_norm_ffn1, v_ffn1_w_gate, v_ffn1_w_up, v_ffn1_w_down, v_norm_mix, v_w_in, v_q_norm, v_k_norm, v_conv_w, v_w_attn_branch, v_w_conv_branch, v_w_out, v_norm_ffn2, v_ffn2_w_gate, v_ffn2_w_up, v_ffn2_w_down):
    me = 4 * lax.axis_index("x") + 2 * lax.axis_index("y") + lax.axis_index("c")
    x0, target = x[0], loss_target[0]
    s = x0.shape[0]
    ada_cols = w_ada.shape[2]
    cw_cols = conv_w.shape[2]

    gathered = _small_allgather(
        "gather_c_conv", jnp.concatenate([c, conv_w[0].reshape(1, 3 * cw_cols)], axis=1))[:, 0]
    c_all = gathered[:, :D]
    cw = gathered[:, D:].reshape(N_DEV, 3, cw_cols).transpose(1, 0, 2).reshape(3, D)
    b_part = lax.dynamic_slice(b_ada, (0, me * ada_cols), (1, ada_cols))
    mod_part = _mod_part(c_all, w_ada[0], b_part)
    mod_all = _small_allgather("gather_mod", mod_part.reshape(1, N_DEV * ada_cols))
    mod = lax.dynamic_slice(mod_all.reshape(N_DEV, N_DEV, ada_cols), (0, me, 0), (N_DEV, 1, ada_cols))
    mod = mod.reshape(N_MOD, 1, D)
    sh1, sc1, gt1, sh2, sc2, gt2, sh3, sc3, gt3 = [mod[i] for i in range(N_MOD)]

    tb = lambda w: w[0].T.astype(BF16)
    nb = lambda w: w[0].astype(BF16)
    ffn1_shards = [tb(ffn1_w_gate), tb(ffn1_w_up), nb(ffn1_w_down)]
    ffn2_shards = [tb(ffn2_w_gate), tb(ffn2_w_up), nb(ffn2_w_down)]
    mix_shards = [tb(w_in), tb(w_attn_branch), nb(w_conv_branch), nb(w_out)]
    ffn_dst, ffn_base, ffn_jump, ffn_shapes = [0, 0, 1], [0, HALF, 0], [HALF, HALF, 0], [(2 * FF, D), (FF, D)]
    mix_dst, mix_base, mix_shapes = [0, 1, 2, 3], [0, 0, 0, 0], [(IN_W, D), (D, COL), (D, D), (D, D)]
    (wgu1,) = _run_plan_on_sequencer(
        "gather_ffn1_gate_up", _gather_plan(ffn1_shards[:2], ffn_dst[:2], ffn_base[:2], ffn_shapes[:1], ffn_jump[:2]), 1)
    (wd1,) = _run_plan_on_sequencer(
        "gather_ffn1_down", _gather_plan(ffn1_shards[2:], [0], [0], ffn_shapes[1:]), 8)
    win_t, wa_t, wc, wo = _run_plan_on_sequencer(
        "gather_mix_weights", _gather_plan(mix_shards, mix_dst, mix_base, mix_shapes), 2)
    wgu2, wd2 = _run_plan_on_sequencer(
        "gather_ffn2_weights", _gather_plan(ffn2_shards, ffn_dst, ffn_base, ffn_shapes, ffn_jump), 3)

    h1 = _normmod("ffn1_normmod", x0, norm_ffn1, sc1, sh1)
    ab1, s1 = _gate_up_swiglu("ffn1_gate_up", h1, wgu1)
    f1, x1, h2 = _out_residual("ffn1_down", s1, wd1, x0, gt1, 0.5, (norm_mix, sc2, sh2))
    proj = _mm("mix_in_proj", h2, win_t, "NT", BF16, 1024, IN_W // 4, D, n_outer=True)
    wqk = jnp.concatenate([jnp.tile(q_norm, (1, 12)), jnp.tile(k_norm, (1, 12))], axis=1)
    qkn = _qknorm(proj, wqk)
    group_out = [_attn_fwd(g, qkn, proj) for g in range(3)]
    o, lse = _attn_combine([go[0] for go in group_out], [go[1] for go in group_out])
    ya = _mm("mix_attn_branch", o, wa_t, "NT", BF16, 1024, 1024, COL)
    z = _conv_fwd(proj, cw)
    yc, merged = _conv_branch_merge(z, wc, ya, proj)
    mix, x2, h3 = _out_residual("mix_out_proj", merged, wo, x1, gt2, 1.0, (norm_ffn2, sc3, sh3), tm=1024, tk=D)
    ab3, s3 = _gate_up_swiglu("ffn2_gate_up", h3, wgu2)
    f3, g3, df3, acc_out = _out_loss("ffn2_down", s3, wd2, x2, gt3, 0.5, target)
    loss_part = jnp.sum(acc_out[1])

    ffn_rows = [sh_.shape[0] for sh_ in ffn1_shards]
    mix_rows = [sh_.shape[0] for sh_ in mix_shards]
    (g2, acc3, dmix), dwgu2, dwd2 = _ffn_bwd(
        "ffn2", df3, x2, g3, h3, ab3, s3, norm_ffn2, sc3, sh3, wgu2, wd2, before=(mix, gt2, 1.0))
    dya, dyc, dgates = _d_merged_branches(dmix, wo, ya, yc, proj)
    dwo = _mm("mix_dw_out", merged, dmix, "TN", BF16, 1024, 1024, 2048)
    dproj = _copy_columns("dproj_gates", dgates, jax.ShapeDtypeStruct((s, IN_W), BF16), GA_BLK)
    dwc = _mm("mix_dw_conv_branch", z, dyc, "TN", BF16, 1024, 1024, 2048)
    dz = _mm("mix_d_z", dyc, wc, "NT", F32, 1024, 1024, D)
    dproj, d_c, cw_acc = _conv_bwd(dz, proj, cw, dproj)
    dproj = _copy_columns("copy_d_c", d_c, dproj, C_BLK)
    dwa_t = _mm("mix_dw_attn_branch", dya, o, "TN", BF16, 1024, COL, 2048)
    do, delta = _d_o_delta(dya, wa_t, o)
    dqn = dkn = None
    for g in range(3):
        dqn, dkn, dproj = _attn_bwd(g, qkn, proj, do, lse, delta, dqn, dkn, dproj)
    dproj, wq_acc = _qknorm_bwd("qnorm_bwd", proj, dqn, wqk[:, :QKW // 2], dproj, 0)
    dproj, wk_acc = _qknorm_bwd("knorm_bwd", proj, dkn, wqk[:, QKW // 2:], dproj, QKW // 2 // COL)
    r_f2g, r_f2u, r_f2d, r_wa, r_wc, r_wo = _run_plan_on_sequencer(
        "scatter_ffn2_and_branch_grads",
        _scatter_plan([dwgu2, dwd2, dwa_t, dwc, dwo], [0, 0, 1, 2, 3, 4], [0, HALF, 0, 0, 0, 0],
                      ffn_rows + mix_rows[1:], [D, D, D, COL, D, D], [HALF, HALF, 0, 0, 0, 0]), 4)
    dwin_t = _mm("mix_dw_in", dproj, h2, "TN", BF16, IN_W // 4, COL, 2048)
    (r_win,) = _run_plan_on_sequencer(
        "scatter_w_in_grad", _scatter_plan([dwin_t], [0], [0], mix_rows[:1], [D]), 5)
    g1, acc2, df1 = _d_h_norm_bwd("mix_d_h", dproj, win_t, x1, g2, norm_mix, sc2, sh2, before=(f1, gt1, 0.5),
                                  after=[dwin_t])
    dwd1 = _mm("ffn1_dw_down", s1, df1, "TN", BF16, HALF, D, 2048)
    (r_f1d,) = _run_plan_on_sequencer(
        "scatter_ffn1_down_grad", _scatter_plan([dwd1], [0], [0], ffn_rows[2:], [D]), 6)
    dab1 = _d_hidden_swiglu("ffn1_d_hidden", df1, wd1, ab1, after=[dwd1, r_win])
    dwgu1 = _mm("ffn1_dw_gate_up", dab1, h1, "TN", BF16, HALF, D, 2048)
    r_f1g, r_f1u = _run_plan_on_sequencer(
        "scatter_ffn1_gate_up_grads",
        _scatter_plan([dwgu1], [0, 0], [0, HALF], ffn_rows[:2], [D, D], [HALF, HALF]), 7)
    g0, acc1 = _d_h_norm_bwd("ffn1_d_h", dab1, wgu1, x0, g1, norm_ffn1, sc1, sh1, after=[dwgu1, r_f1d], tm=512)

    dqw = jnp.sum(wq_acc[0].reshape(12, HD), axis=0)
    dkw = jnp.sum(wk_acc[0].reshape(12, HD), axis=0)
    small = jnp.concatenate([
        acc1[0], acc1[1], acc2[3], acc2[0], acc2[1], acc3[3], acc3[0], acc3[1], acc_out[0],
        acc1[2], acc2[2], acc3[2], dqw, dkw, cw_acc[0:3].reshape(3 * D),
        jnp.zeros((HD,), F32).at[0].set(loss_part)]).reshape(1, -1)
    small_all = _small_allgather("gather_small_grads", small)
    small_sum = _sum_rows("sum_small_grads", small_all)[0]
    n_mod = N_MOD * D
    g_b_ada = small_sum[:n_mod].reshape(1, n_mod)
    g_norm1, g_norm2, g_norm3 = [small_sum[n_mod + i * D:n_mod + (i + 1) * D].reshape(1, D) for i in range(3)]
    off = n_mod + 3 * D
    g_qn, g_kn = small_sum[off:off + HD].reshape(1, HD), small_sum[off + HD:off + 2 * HD].reshape(1, HD)
    g_cw_full = small_sum[off + 2 * HD:off + 2 * HD + 3 * D].reshape(3, D)
    loss = small_sum[off + 2 * HD + 3 * D]
    g_cw = lax.dynamic_slice(g_cw_full, (0, me * cw_cols), (3, cw_cols))
    dmod_part = lax.dynamic_slice(small_all[:, 0, :n_mod], (0, me * ada_cols), (N_DEV, ada_cols))
    g_w_ada = _w_ada_grad(c_all.T, dmod_part)

    as_rows = {"ffn1_w_gate", "ffn1_w_up", "w_in", "w_attn_branch", "ffn2_w_gate", "ffn2_w_up"}
    grad_list = [g_w_ada, g_b_ada, g_norm1, r_f1g, r_f1u, r_f1d, g_norm2, r_win,
                 g_qn, g_kn, g_cw, r_wa, r_wc, r_wo, g_norm3, r_f2g, r_f2u, r_f2d]
    weights = [w_ada, b_ada, norm_ffn1, ffn1_w_gate, ffn1_w_up, ffn1_w_down, norm_mix, w_in, q_norm, k_norm,
               conv_w, w_attn_branch, w_conv_branch, w_out, norm_ffn2, ffn2_w_gate, ffn2_w_up, ffn2_w_down]
    ms = [m_w_ada, m_b_ada, m_norm_ffn1, m_ffn1_w_gate, m_ffn1_w_up, m_ffn1_w_down, m_norm_mix, m_w_in, m_q_norm,
          m_k_norm, m_conv_w, m_w_attn_branch, m_w_conv_branch, m_w_out, m_norm_ffn2, m_ffn2_w_gate,
          m_ffn2_w_up, m_ffn2_w_down]
    vs = [v_w_ada, v_b_ada, v_norm_ffn1, v_ffn1_w_gate, v_ffn1_w_up, v_ffn1_w_down, v_norm_mix, v_w_in, v_q_norm,
          v_k_norm, v_conv_w, v_w_attn_branch, v_w_conv_branch, v_w_out, v_norm_ffn2, v_ffn2_w_gate,
          v_ffn2_w_up, v_ffn2_w_down]
    wnames = ["w_ada", "b_ada", "norm_ffn1", "ffn1_w_gate", "ffn1_w_up", "ffn1_w_down", "norm_mix", "w_in",
              "q_norm", "k_norm", "conv_w", "w_attn_branch", "w_conv_branch", "w_out", "norm_ffn2",
              "ffn2_w_gate", "ffn2_w_up", "ffn2_w_down"]
    small = [i for i, gr in enumerate(grad_list) if gr.ndim == 2 and gr.size <= 16384]
    flat = lambda a, i: a.reshape(-1, weights[i].shape[-1])
    small_res = dict(zip(small, _adamw_small(
        [flat(weights[i], i) for i in small], [flat(grad_list[i], i) for i in small],
        [flat(ms[i], i) for i in small], [flat(vs[i], i) for i in small])))
    grad_out, deltas, new_ms, new_vs = [], [], [], []
    for idx, (nm, w, gr, m_, v_) in enumerate(zip(wnames, weights, grad_list, ms, vs)):
        if idx in small_res:
            gr, dl, nm_, nv_ = [r.reshape(w.shape) for r in (gr, *small_res[idx])]
        elif nm in as_rows:
            res = _adamw(f"adamw_{nm}", w[0].T, gr, m_[0].T, v_[0].T)
            gr, dl, nm_, nv_ = [r.T[None] for r in res]
        else:
            two_d = (-1, w.shape[-1])
            res = _adamw(f"adamw_{nm}", w.reshape(two_d), gr if gr.ndim == 3 else gr.reshape(two_d),
                         m_.reshape(two_d), v_.reshape(two_d))
            gr, dl, nm_, nv_ = [r.reshape(w.shape) for r in res]
        grad_out.append(gr)
        deltas.append(dl)
        new_ms.append(nm_)
        new_vs.append(nv_)
    return (loss, g0[None], *grad_out, *deltas, *new_ms, *new_vs)
```

```python
import jax
import jax.numpy as jnp
from jax import lax
from jax.experimental import pallas as pl
from jax.experimental.pallas import tpu as pltpu
from jax.experimental.pallas import tpu_sc as plsc

F32 = jnp.float32
BF16 = jnp.bfloat16
MESH = pl.DeviceIdType.MESH

N_DEV = 8
D = 1024
FF = 2816
HD = 128
N_HEADS = 4
DILATIONS = (1, 4, 16)
BAND = 128
QKW = 2 * 3 * N_HEADS * HD
IN_W = 9728
COL = 512
V_BLK, U_BLK, B_BLK, C_BLK, GA_BLK, GC_BLK = 6, 9, 11, 13, 15, 17
EPS = 1e-6
N_MOD = 9
ADAM_LR, ADAM_B1, ADAM_B2, ADAM_EPS, ADAM_WD, ADAM_STEP = 0.001, 0.9, 0.999, 1e-08, 0.01, 10

NT_DIMS = (((1,), (1,)), ((), ()))
TN_DIMS = (((0,), (0,)), ((), ()))
NN_DIMS = (((1,), (0,)), ((), ()))


def _place():
    return lax.axis_index("x"), lax.axis_index("y"), lax.axis_index("c")


def _flip(coord, bit):
    return 1 - coord if bit else coord


def _params(*sem):
    return pltpu.CompilerParams(dimension_semantics=sem)


def _small_allgather(name, v):
    n = v.shape[-1]

    def body(v_ref, out_ref, send_sems, recv_sems):
        x, y, c = _place()
        me = 4 * x + 2 * y + c
        out_ref[me] = v_ref[...]
        copies = []
        for k in range(1, N_DEV):
            peer = (_flip(x, (k >> 2) & 1), _flip(y, (k >> 1) & 1), _flip(c, k & 1))
            cp = pltpu.make_async_remote_copy(
                src_ref=v_ref, dst_ref=out_ref.at[me], send_sem=send_sems.at[k - 1],
                recv_sem=recv_sems.at[k - 1], device_id=peer, device_id_type=MESH)
            cp.start()
            copies.append(cp)
        for cp in copies:
            cp.wait()

    return pl.pallas_call(
        body, name=name,
        out_shape=jax.ShapeDtypeStruct((N_DEV, 1, n), F32),
        in_specs=[pl.BlockSpec(memory_space=pltpu.VMEM)],
        out_specs=pl.BlockSpec(memory_space=pltpu.VMEM),
        scratch_shapes=[pltpu.SemaphoreType.DMA((N_DEV - 1,)), pltpu.SemaphoreType.DMA((N_DEV - 1,))],
    )(v)


class _Plan:
    def __init__(self, operands, out_shapes, sems, phases):
        self.operands, self.out_shapes, self.sems, self.phases = operands, out_shapes, sems, phases


def _slab_start(base, rows, jump, idx):
    return pl.multiple_of(base + idx * rows + (idx // 4) * jump, 16)


def _gather_plan(shards, dst_of, base_of, dst_shapes, jump_of=None):
    n = len(shards)
    rows = [s.shape[0] for s in shards]
    jump_of = jump_of or [0] * n

    def phases(srcs, dsts, sems):
        send_sems, recv_sems, local_sems = sems
        x, y, c = _place()
        me, sibling = (x, y, c), (x, y, 1 - c)
        chips = [(1 - x, y), (x, 1 - y), (1 - x, 1 - y)]

        def slab(i, px, py, pc):
            start = _slab_start(base_of[i], rows[i], jump_of[i], 4 * px + 2 * py + pc)
            return dsts[dst_of[i]].at[pl.ds(start, rows[i])]

        def copy(i, k, block, to, src=None):
            return pltpu.make_async_remote_copy(
                src_ref=slab(i, *block) if src is None else src, dst_ref=slab(i, *block),
                send_sem=send_sems.at[i, k], recv_sem=recv_sems.at[i, k],
                device_id=to, device_id_type=MESH)

        def mine():
            return [pltpu.make_async_copy(srcs[i], slab(i, *me), local_sems.at[i]) for i in range(n)]

        def first():
            out = []
            for i in range(n):
                out.append(copy(i, 0, me, sibling, src=srcs[i]))
                out += [copy(i, 1 + j, me, (*chip, c), src=srcs[i]) for j, chip in enumerate(chips)]
            return out

        def passed():
            return [(copy(i, 1 + j, (*chip, c), me), copy(i, 4 + j, (*chip, c), sibling))
                    for j, chip in enumerate(chips) for i in range(n)]

        def start():
            for cp in mine() + first():
                cp.start()

        def middle():
            for landed, onward in passed():
                landed.wait_recv()
                onward.start()

        def finish():
            for i in range(n):
                copy(i, 0, sibling, me).wait_recv()
                for j, chip in enumerate(chips):
                    copy(i, 4 + j, (*chip, 1 - c), me).wait_recv()
            for cp in first() + [onward for _, onward in passed()]:
                cp.wait_send()
            for cp in mine():
                cp.wait()

        return start, middle, finish

    sems = [pltpu.SemaphoreType.DMA((n, 7)), pltpu.SemaphoreType.DMA((n, 7)), pltpu.SemaphoreType.DMA((n,))]
    return _Plan(list(shards), [jax.ShapeDtypeStruct(s, BF16) for s in dst_shapes], sems, phases)


def _scatter_plan(grads, src_of, base_of, rows, cols, jump_of=None):
    n = len(rows)
    jump_of = jump_of or [0] * n

    def phases(srcs, recvs, sems):
        send_sems, recv_sems, local_sems = sems
        x, y, c = _place()
        me = 4 * x + 2 * y + c

        def slab(i, idx):
            start = _slab_start(base_of[i], rows[i], jump_of[i], idx)
            return srcs[src_of[i]].at[pl.ds(start, rows[i])]

        def copies():
            out = [pltpu.make_async_copy(slab(i, me), recvs[i].at[me], local_sems.at[i]) for i in range(n)]
            for k in range(1, N_DEV):
                px, py, pc = _flip(x, (k >> 2) & 1), _flip(y, (k >> 1) & 1), _flip(c, k & 1)
                out += [pltpu.make_async_remote_copy(
                    src_ref=slab(i, 4 * px + 2 * py + pc), dst_ref=recvs[i].at[me],
                    send_sem=send_sems.at[i, k - 1], recv_sem=recv_sems.at[i, k - 1],
                    device_id=(px, py, pc), device_id_type=MESH) for i in range(n)]
            return out

        def start():
            for cp in copies():
                cp.start()

        def finish():
            for cp in copies():
                cp.wait()

        return start, None, finish

    sems = [pltpu.SemaphoreType.DMA((n, 7)), pltpu.SemaphoreType.DMA((n, 7)), pltpu.SemaphoreType.DMA((n,))]
    out_shapes = [jax.ShapeDtypeStruct((N_DEV, rows[i], cols[i]), BF16) for i in range(n)]
    return _Plan(list(grads), out_shapes, sems, phases)


def _run_plan_on_sequencer(name, plan, collective_id):
    src_refs = [jax.new_ref(a, memory_space=pltpu.MemorySpace.HBM) for a in plan.operands]
    dst_refs = [jax.empty_ref(s, memory_space=pltpu.MemorySpace.HBM) for s in plan.out_shapes]

    @pl.kernel(mesh=plsc.ScalarSubcoreMesh(axis_name="sequencer", num_cores=1), name=name,
               scratch_types=tuple(plan.sems),
               compiler_params=pltpu.CompilerParams(collective_id=collective_id))
    def launch(*sems):
        x, y, c = _place()
        barrier = pltpu.get_barrier_semaphore()
        for k in range(1, N_DEV):
            peer = (_flip(x, (k >> 2) & 1), _flip(y, (k >> 1) & 1), _flip(c, k & 1))
            pl.semaphore_signal(barrier, inc=1, device_id=peer, device_id_type=MESH)
        pl.semaphore_wait(barrier, N_DEV - 1)
        for phase in plan.phases(src_refs, dst_refs, sems):
            if phase is not None:
                phase()

    launch()
    return [r[...] for r in dst_refs]


def _mm(name, a, b, mode, out_dtype, tm, tn, tk, *, tiles_in=(), tiles_out=(), epilogue=None,
        n_outer=False, keep_b=False, col_chunks=None, after=()):
    if mode == "TN":
        kk, m = a.shape
    else:
        m, kk = a.shape
    n = b.shape[0] if mode == "NT" else b.shape[1]
    tm, tn, tk = min(tm, m), min(tn, n), min(tk, kk)
    assert m % tm == 0 and n % tn == 0 and kk % tk == 0, (name, m, n, kk, tm, tn, tk)
    ni, nj, nk = m // tm, n // tn, kk // tk
    dims = {"NN": NN_DIMS, "NT": NT_DIMS, "TN": TN_DIMS}[mode]
    if epilogue is None:
        tiles_out = [(jax.ShapeDtypeStruct((m, n), out_dtype), (tm, tn), lambda i, j: (i, j))]
    n_tin, n_tout = len(tiles_in), len(tiles_out)
    n_acc = 1 if nk > 1 else 0
    n_after = len(after)
    assert not keep_b or (nk == 1 and nj == 1)
    assert not col_chunks or (epilogue is not None and nk == 1 and mode != "TN")
    ij = (lambda p, q: (q, p)) if n_outer else (lambda p, q: (p, q))
    inner = ni if n_outer else nj

    def body(a_ref, b_ref, *rest):
        tin = rest[:n_tin]
        tout = rest[n_tin + n_after:n_tin + n_after + n_tout]
        scratch = rest[n_tin + n_after + n_tout:]
        k = pl.program_id(2)
        visit = pl.program_id(0) * inner + pl.program_id(1)
        if keep_b:
            b_kept, b_sem = scratch[n_acc:n_acc + 2]

            @pl.when((visit == 0) & (k == 0))
            def _():
                cp = pltpu.make_async_copy(b_ref, b_kept, b_sem)
                cp.start()
                cp.wait()

            b_ref = b_kept

        def store(prod, c=0, cols=()):
            if epilogue is None:
                tout[0][...] = prod.astype(out_dtype)
            else:
                epilogue(prod, jnp.logical_and(visit == 0, c == 0), tin, tout, *cols)

        if col_chunks:
            for c, (c0, cw) in enumerate(col_chunks):
                b_part = b_ref[pl.ds(c0, cw), :] if mode == "NT" else b_ref[:, pl.ds(c0, cw)]
                store(lax.dot_general(a_ref[...], b_part, dims, preferred_element_type=F32), c, ((c0, cw),))
        else:
            part = lax.dot_general(a_ref[...], b_ref[...], dims, preferred_element_type=F32)
            if nk == 1:
                store(part)
            else:
                acc_ref = scratch[0]

                @pl.when(k == 0)
                def _():
                    acc_ref[...] = part

                @pl.when((k > 0) & (k < nk - 1))
                def _():
                    acc_ref[...] += part

                @pl.when(k == nk - 1)
                def _():
                    store(acc_ref[...] + part)

    def spec(shape, fn):
        return pl.BlockSpec(shape, lambda p, q, k: fn(*ij(p, q)))

    a_spec = (pl.BlockSpec((tk, tm), lambda p, q, k: (k, ij(p, q)[0])) if mode == "TN"
              else pl.BlockSpec((tm, tk), lambda p, q, k: (ij(p, q)[0], k)))
    if keep_b:
        b_spec = pl.BlockSpec(memory_space=pl.ANY)
    elif mode == "NT":
        b_spec = pl.BlockSpec((tn, tk), lambda p, q, k: (ij(p, q)[1], k))
    else:
        b_spec = pl.BlockSpec((tk, tn), lambda p, q, k: (k, ij(p, q)[1]))
    sequential = epilogue or keep_b
    out = pl.pallas_call(
        body, name=name, grid=(nj, ni, nk) if n_outer else (ni, nj, nk),
        out_shape=[t[0] for t in tiles_out],
        in_specs=([a_spec, b_spec] + [spec(t[1], t[2]) for t in tiles_in]
                  + [pl.BlockSpec(memory_space=pl.ANY)] * n_after),
        out_specs=[spec(t[1], t[2]) for t in tiles_out],
        scratch_shapes=([pltpu.VMEM((tm, tn), F32)] * n_acc
                        + ([pltpu.VMEM(b.shape, b.dtype), pltpu.SemaphoreType.DMA] if keep_b else [])),
        compiler_params=(_params("arbitrary", "arbitrary", "arbitrary") if sequential
                         else _params("parallel", "parallel", "arbitrary")),
    )(a, b, *[t[0] for t in tiles_in], *after)
    return out if epilogue else out[0]


def _row(tm, w, off=0):
    return pl.BlockSpec((tm, w), lambda i: (i, off))


def _vec(w):
    return pl.BlockSpec((1, w), lambda i: (0, 0))


def _sigmoid(x):
    return 0.5 * jnp.tanh(0.5 * x) + 0.5


def _normmod(name, x, g, sc, sh, tm=512):
    s = x.shape[0]

    def body(x_ref, g_ref, sc_ref, sh_ref, h_ref):
        xv = x_ref[...]
        r = lax.rsqrt(jnp.mean(xv * xv, axis=-1, keepdims=True) + EPS)
        h_ref[...] = ((xv * r) * g_ref[...] * (1.0 + sc_ref[...]) + sh_ref[...]).astype(BF16)

    return pl.pallas_call(
        body, name=name, grid=(s // tm,),
        out_shape=jax.ShapeDtypeStruct((s, D), BF16),
        in_specs=[_row(tm, D), _vec(D), _vec(D), _vec(D)], out_specs=_row(tm, D),
        compiler_params=_params("parallel"),
    )(x, g, sc, sh)


def _heads(x, fn):
    return jnp.concatenate([fn(x[:, h * HD:(h + 1) * HD], h) for h in range(COL // HD)], axis=1)


def _qknorm(proj, wqk, tm=1024):
    s = proj.shape[0]

    def body(p_ref, w_ref, o_ref):
        pv = p_ref[...].astype(F32)
        wv = w_ref[...]

        def one(qh, h):
            r = lax.rsqrt(jnp.mean(qh * qh, axis=-1, keepdims=True) + EPS)
            return (qh * r) * wv[:, h * HD:(h + 1) * HD]

        o_ref[...] = _heads(pv, one).astype(BF16)

    return pl.pallas_call(
        body, name="qknorm", grid=(s // tm, QKW // COL),
        out_shape=jax.ShapeDtypeStruct((s, QKW), BF16),
        in_specs=[pl.BlockSpec((tm, COL), lambda i, j: (i, j)), pl.BlockSpec((1, COL), lambda i, j: (0, j))],
        out_specs=pl.BlockSpec((tm, COL), lambda i, j: (i, j)),
        compiler_params=_params("parallel", "parallel"),
    )(proj, wqk)


def _qknorm_bwd(name, proj, dn, w, dproj, blk0, tm=1024):
    s = proj.shape[0]
    nblk = dn.shape[1] // COL

    def body(p_ref, d_ref, w_ref, _, o_ref, acc_ref):
        pv = p_ref[...].astype(F32)
        dv = d_ref[...]
        wv = w_ref[...]
        sums = []

        def one(qh, h):
            dn = dv[:, h * HD:(h + 1) * HD]
            r = lax.rsqrt(jnp.mean(qh * qh, axis=-1, keepdims=True) + EPS)
            nh = qh * r
            sums.append(jnp.sum(dn * nh, axis=0, keepdims=True))
            dnw = dn * wv[:, h * HD:(h + 1) * HD]
            return r * (dnw - nh * jnp.mean(dnw * nh, axis=-1, keepdims=True))

        o_ref[...] = _heads(pv, one).astype(BF16)

        @pl.when(pl.program_id(1) == 0)
        def _():
            acc_ref[...] = jnp.zeros_like(acc_ref)

        acc_ref[0:1, :] += jnp.concatenate(sums, axis=1)

    return pl.pallas_call(
        body, name=name, grid=(nblk, s // tm),
        out_shape=[jax.ShapeDtypeStruct((s, IN_W), BF16), jax.ShapeDtypeStruct((8, nblk * COL), F32)],
        in_specs=[pl.BlockSpec((tm, COL), lambda j, i: (i, blk0 + j)), pl.BlockSpec((tm, COL), lambda j, i: (i, j)),
                  pl.BlockSpec((1, COL), lambda j, i: (0, j)), pl.BlockSpec(memory_space=pl.ANY)],
        out_specs=[pl.BlockSpec((tm, COL), lambda j, i: (i, blk0 + j)),
                   pl.BlockSpec((8, COL), lambda j, i: (0, j))],
        input_output_aliases={3: 0},
        compiler_params=_params("arbitrary", "arbitrary"),
    )(proj, dn, w, dproj)


def _attn_shapes(s, g):
    d = DILATIONS[g]
    tb = min(s, max(2048, 256 * d))
    sb = min(256, tb // d)
    pb = BAND * d
    assert s % tb == 0 and tb % pb == 0 and (tb // d) % sb == 0 and sb % BAND == 0
    return d, tb, sb, pb


def _lanes(x, width):
    return jnp.concatenate([x] * (width // HD), axis=1)


def _every(start, size, d):
    return pl.ds(start, size, stride=d) if d > 1 else pl.ds(start, size)


def _attn_specs(g, tb, pb, s, ahead):
    ratio = tb // pb
    if ahead:
        nbr = lambda n: jnp.minimum((n + 1) * ratio, s // pb - 1)
    else:
        nbr = lambda n: jnp.maximum(n * ratio - 1, 0)
    cur = lambda base: pl.BlockSpec((tb, HD), lambda h, n: (n, base + g * N_HEADS + h))
    side = lambda base: pl.BlockSpec((pb, HD), lambda h, n: (nbr(n), base + g * N_HEADS + h))
    tok = pl.BlockSpec((tb, HD), lambda h, n: (n, h))
    tok_side = pl.BlockSpec((pb, HD), lambda h, n: (nbr(n), h))
    return cur, side, tok, tok_side


Q_COL, K_COL, V_COL = 0, 12, 24


def _attn_fwd(g, qkn, proj):
    s = qkn.shape[0]
    d, tb, sb, pb = _attn_shapes(s, g)
    ft = F32 if d > 1 else BF16
    nj = tb // d // sb
    scale = HD ** -0.5

    def body(q_ref, kc_ref, kp_ref, vc_ref, vp_ref, o_ref, lse_ref, qf, kf, vf):
        n = pl.program_id(1)
        qf[...] = q_ref[...].astype(ft)
        kf[0:pb] = kp_ref[...].astype(ft)
        kf[pb:] = kc_ref[...].astype(ft)
        vf[0:pb] = vp_ref[...].astype(ft)
        vf[pb:] = vc_ref[...].astype(ft)
        for r in range(d):
            for j in range(nj):
                at = j * sb * d + r
                q = qf[_every(at, sb, d), :].astype(BF16)
                k = kf[_every(at, sb + BAND, d), :].astype(BF16)
                v = vf[_every(at, sb + BAND, d), :].astype(BF16)
                sc = lax.dot_general(q, k, NT_DIMS, preferred_element_type=F32) * scale
                qi = lax.broadcasted_iota(jnp.int32, sc.shape, 0)
                kj = lax.broadcasted_iota(jnp.int32, sc.shape, 1)
                valid = (kj >= qi) & (kj <= qi + BAND)
                if j == 0:
                    valid = valid & ((kj >= BAND) | (n > 0))
                sc = jnp.where(valid, sc, -1e30)
                m = jnp.max(sc, axis=-1, keepdims=True)
                p = jnp.exp(sc - m)
                l = jnp.sum(p, axis=-1, keepdims=True)
                o = lax.dot_general(p.astype(BF16), v, NN_DIMS, preferred_element_type=F32)
                o_ref[_every(at, sb, d), :] = o / l
                lse_ref[_every(at, sb, d), :] = jnp.broadcast_to(m + jnp.log(l), (sb, HD))

    cur, side, tok, _ = _attn_specs(g, tb, pb, s, ahead=False)
    return pl.pallas_call(
        body, name=f"attn_fwd_g{g}", grid=(N_HEADS, s // tb),
        out_shape=[jax.ShapeDtypeStruct((s, COL), F32)] * 2,
        in_specs=[cur(Q_COL), cur(K_COL), side(K_COL), cur(V_COL), side(V_COL)],
        out_specs=[tok, tok],
        scratch_shapes=[pltpu.VMEM((tb, HD), ft), pltpu.VMEM((tb + pb, HD), ft),
                        pltpu.VMEM((tb + pb, HD), ft)],
        compiler_params=_params("parallel", "arbitrary"),
    )(qkn, qkn, qkn, proj, proj)


def _attn_combine(os_, lses, tm=512):
    s = os_[0].shape[0]

    def body(o0, o1, o2, l0, l1, l2, o_ref, lse_ref):
        a, b, c = l0[...], l1[...], l2[...]
        m = jnp.maximum(jnp.maximum(a, b), c)
        ea, eb, ec = jnp.exp(a - m), jnp.exp(b - m), jnp.exp(c - m)
        tot = ea + eb + ec
        o_ref[...] = ((ea * o0[...] + eb * o1[...] + ec * o2[...]) / tot).astype(BF16)
        lse_ref[...] = m + jnp.log(tot)

    return pl.pallas_call(
        body, name="attn_combine", grid=(s // tm,),
        out_shape=[jax.ShapeDtypeStruct((s, COL), BF16), jax.ShapeDtypeStruct((s, COL), F32)],
        in_specs=[_row(tm, COL)] * 6, out_specs=[_row(tm, COL)] * 2,
        compiler_params=_params("parallel"),
    )(*os_, *lses)


def _attn_bwd(g, qkn, proj, do, lse, delta, dqn, dkn, dproj):
    s = qkn.shape[0]
    d, tb, sb, pb = _attn_shapes(s, g)
    ft = F32 if d > 1 else BF16
    nj = tb // d // sb
    nt = s // tb
    scale = HD ** -0.5
    chained = dqn is not None

    def body(k_ref, v_ref, qc_ref, qn_ref, doc_ref, don_ref, lc_ref, ln_ref, dc_ref, dn_ref, *rest):
        dq_ref, dk_ref, dv_ref, kf, vf, qf, dvf, later = rest[-8:]
        n = pl.program_id(1)
        kf[...] = k_ref[...].astype(ft)
        vf[...] = v_ref[...].astype(ft)
        qf[0:tb] = qc_ref[...].astype(ft)
        qf[tb:] = qn_ref[...].astype(ft)

        @pl.when(n == 0)
        def _():
            later[...] = jnp.zeros_like(later)

        def window(c_ref, n_ref, r, j):
            at = j * sb * d + r
            if j < nj - 1:
                return c_ref[_every(at, sb + BAND, d), :]
            return jnp.concatenate([c_ref[_every(at, sb, d), :], n_ref[_every(r, BAND, d), :]], axis=0)

        for r in range(d):
            tail = later[r]
            for j in range(nj):
                at = j * sb * d + r
                rows = _every(at, sb, d)
                k = kf[rows, :].astype(BF16)
                v = vf[rows, :].astype(BF16)
                q = qf[_every(at, sb + BAND, d), :].astype(BF16)
                dov = window(doc_ref, don_ref, r, j).astype(BF16)
                sc = lax.dot_general(q, k, NT_DIMS, preferred_element_type=F32) * scale
                qi = lax.broadcasted_iota(jnp.int32, sc.shape, 0)
                kj = lax.broadcasted_iota(jnp.int32, sc.shape, 1)
                valid = (qi >= kj) & (qi <= kj + BAND)
                if j == nj - 1:
                    valid = valid & ((qi < sb) | (n < nt - 1))
                p = jnp.exp(jnp.where(valid, sc - _lanes(window(lc_ref, ln_ref, r, j), sb), -1e30))
                dp = lax.dot_general(dov, v, NT_DIMS, preferred_element_type=F32)
                ds = (p * (dp - _lanes(window(dc_ref, dn_ref, r, j), sb)) * scale).astype(BF16)
                dvf[rows, :] = lax.dot_general(p.astype(BF16), dov, TN_DIMS, preferred_element_type=F32)
                dk_ref[rows, :] = lax.dot_general(ds, q, TN_DIMS, preferred_element_type=F32)
                dqw = lax.dot_general(ds, k, NN_DIMS, preferred_element_type=F32)
                first = dqw[:BAND] + tail
                dq_ref[rows, :] = first if sb == BAND else jnp.concatenate([first, dqw[BAND:sb]], axis=0)
                tail = dqw[sb:]
            later[r] = tail
        dv_ref[...] = dvf[...].astype(BF16)

    cur, side, tok, tok_side = _attn_specs(g, tb, pb, s, ahead=True)
    anyspec = pl.BlockSpec(memory_space=pl.ANY)
    n_heads_cols = 3 * N_HEADS * HD
    return pl.pallas_call(
        body, name=f"attn_bwd_g{g}", grid=(N_HEADS, nt),
        out_shape=[jax.ShapeDtypeStruct((s, n_heads_cols), F32), jax.ShapeDtypeStruct((s, n_heads_cols), F32),
                   jax.ShapeDtypeStruct((s, IN_W), BF16)],
        in_specs=[cur(K_COL), cur(V_COL), cur(Q_COL), side(Q_COL), tok, tok_side, tok, tok_side,
                  tok, tok_side] + ([anyspec, anyspec] if chained else []) + [anyspec],
        out_specs=[cur(0), cur(0), cur(V_COL)],
        input_output_aliases={10: 0, 11: 1, 12: 2} if chained else {10: 2},
        scratch_shapes=[pltpu.VMEM((tb, HD), ft), pltpu.VMEM((tb, HD), ft),
                        pltpu.VMEM((tb + pb, HD), ft), pltpu.VMEM((tb, HD), F32),
                        pltpu.VMEM((d, BAND, HD), F32)],
        compiler_params=_params("arbitrary", "arbitrary"),
    )(qkn, proj, qkn, qkn, do, do, lse, lse, delta, delta, *([dqn, dkn] if chained else []), dproj)


def _shift_down(x, before, k):
    rolled = pltpu.roll(x, k, 0)
    head = jnp.where(lax.broadcasted_iota(jnp.int32, before.shape, 0) < k, pltpu.roll(before, k, 0), rolled[:8])
    return jnp.concatenate([head, rolled[8:]], axis=0)


def _shift_up(x, after, k):
    rows = x.shape[0]
    rolled = pltpu.roll(x, rows - k, 0)
    tail = jnp.where(lax.broadcasted_iota(jnp.int32, after.shape, 0) >= 8 - k,
                     pltpu.roll(after, 8 - k, 0), rolled[rows - 8:])
    return jnp.concatenate([rolled[:rows - 8], tail], axis=0)


def _conv_fwd(proj, cw, tm=1024):
    s = proj.shape[0]
    r16 = tm // 16

    def body(u_ref, b_ref, c_ref, up_ref, cp_ref, w_ref, z_ref):
        i = pl.program_id(1)
        xc = c_ref[...].astype(F32) * u_ref[...].astype(F32)
        xp = jnp.where(i > 0, cp_ref[8:16, :].astype(F32) * up_ref[8:16, :].astype(F32), 0.0)
        w = w_ref[...]
        conv = _shift_down(xc, xp, 2) * w[0:1] + _shift_down(xc, xp, 1) * w[1:2] + xc * w[2:3]
        z_ref[...] = (b_ref[...].astype(F32) * conv).astype(BF16)

    tile = lambda blk: pl.BlockSpec((tm, COL), lambda j, i: (i, blk + j))
    before = lambda blk: pl.BlockSpec((16, COL), lambda j, i: (jnp.maximum(i * r16 - 1, 0), blk + j))
    return pl.pallas_call(
        body, name="conv_fwd", grid=(D // COL, s // tm),
        out_shape=jax.ShapeDtypeStruct((s, D), BF16),
        in_specs=[tile(U_BLK), tile(B_BLK), tile(C_BLK), before(U_BLK), before(C_BLK),
                  pl.BlockSpec((3, COL), lambda j, i: (0, j))],
        out_specs=pl.BlockSpec((tm, COL), lambda j, i: (i, j)),
        compiler_params=_params("parallel", "parallel"),
    )(proj, proj, proj, proj, proj, cw)


def _conv_bwd(dz, proj, cw, dproj, tm=1024):
    s = proj.shape[0]
    r16 = tm // 16
    nrow = s // tm

    def body(dz_ref, u_ref, b_ref, c_ref, up_ref, cp_ref, dzn_ref, bn_ref, w_ref, _, o_ref, dc_ref, acc_ref):
        piece, i = pl.program_id(1), pl.program_id(2)
        u, c = u_ref[...].astype(F32), c_ref[...].astype(F32)
        bv = b_ref[...].astype(F32)
        dzv = dz_ref[...]
        w = w_ref[...]

        @pl.when((piece == 0) & (i == 0))
        def _():
            acc_ref[...] = jnp.zeros_like(acc_ref)

        @pl.when(piece == 0)
        def _():
            xc = c * u
            xp = jnp.where(i > 0, cp_ref[8:16, :].astype(F32) * up_ref[8:16, :].astype(F32), 0.0)
            x2, x1 = _shift_down(xc, xp, 2), _shift_down(xc, xp, 1)
            o_ref[...] = (dzv * (x2 * w[0:1] + x1 * w[1:2] + xc * w[2:3])).astype(BF16)
            dc_ref[...] = jnp.zeros_like(dc_ref)
            dconv = dzv * bv
            acc_ref[0:1, :] += jnp.sum(dconv * x2, axis=0, keepdims=True)
            acc_ref[1:2, :] += jnp.sum(dconv * x1, axis=0, keepdims=True)
            acc_ref[2:3, :] += jnp.sum(dconv * xc, axis=0, keepdims=True)

        @pl.when(piece == 1)
        def _():
            dconv = dzv * bv
            dn = jnp.where(i < nrow - 1, dzn_ref[...] * bn_ref[0:8, :].astype(F32), 0.0)
            dxc = dconv * w[2:3] + _shift_up(dconv, dn, 1) * w[1:2] + _shift_up(dconv, dn, 2) * w[0:1]
            o_ref[...] = (dxc * c).astype(BF16)
            dc_ref[...] = (dxc * u).astype(BF16)

    tile = lambda blk: pl.BlockSpec((tm, COL), lambda j, p, i: (i, blk + j))
    before = lambda blk: pl.BlockSpec((16, COL), lambda j, p, i: (jnp.maximum(i * r16 - 1, 0), blk + j))
    after = lambda rows, blk: pl.BlockSpec(
        (rows, COL), lambda j, p, i: (jnp.minimum((i + 1) * (tm // rows), s // rows - 1), blk + j))
    return pl.pallas_call(
        body, name="conv_bwd", grid=(D // COL, 2, nrow),
        out_shape=[jax.ShapeDtypeStruct((s, IN_W), BF16), jax.ShapeDtypeStruct((s + tm, D), BF16),
                   jax.ShapeDtypeStruct((8, D), F32)],
        in_specs=[tile(0), tile(U_BLK), tile(B_BLK), tile(C_BLK), before(U_BLK), before(C_BLK),
                  after(8, 0), after(16, B_BLK), pl.BlockSpec((3, COL), lambda j, p, i: (0, j)),
                  pl.BlockSpec(memory_space=pl.ANY)],
        out_specs=[pl.BlockSpec((tm, COL), lambda j, p, i: (i, jnp.where(p == 0, B_BLK, U_BLK) + j)),
                   pl.BlockSpec((tm, COL), lambda j, p, i: (jnp.where(p == 0, nrow, i), j)),
                   pl.BlockSpec((8, COL), lambda j, p, i: (0, j))],
        input_output_aliases={9: 0},
        compiler_params=_params("arbitrary", "arbitrary", "arbitrary"),
    )(dz, proj, proj, proj, proj, proj, dz, proj, cw, dproj)


def _copy_columns(name, src, dst, blk0, tm=1024):
    s, w = dst.shape[0], src.shape[1]
    fresh = isinstance(dst, jax.ShapeDtypeStruct)

    def body(x_ref, *rest):
        rest[-1][...] = x_ref[...]

    return pl.pallas_call(
        body, name=name, grid=(w // COL, s // tm),
        out_shape=jax.ShapeDtypeStruct(dst.shape, dst.dtype),
        in_specs=[pl.BlockSpec((tm, COL), lambda j, i: (i, j))] + ([] if fresh else [pl.BlockSpec(memory_space=pl.ANY)]),
        out_specs=pl.BlockSpec((tm, COL), lambda j, i: (i, blk0 + j)),
        input_output_aliases={} if fresh else {1: 0},
        compiler_params=_params("parallel", "parallel"),
    )(src, *([] if fresh else [dst]))


def _mod_part(c_all, w_ada, b_part):
    def body(c_ref, w_ref, b_ref, o_ref):
        cv = c_ref[...]
        act = cv * _sigmoid(cv)
        o_ref[...] = jnp.dot(act, w_ref[...], preferred_element_type=F32,
                             precision=lax.Precision.HIGHEST) + b_ref[...]

    return pl.pallas_call(
        body, name="mod_part", out_shape=jax.ShapeDtypeStruct((N_DEV, w_ada.shape[1]), F32),
    )(c_all, w_ada, b_part)


def _w_ada_grad(c_all_t, dmod_part):
    def body(c_ref, d_ref, o_ref):
        cv = c_ref[...]
        act = cv * _sigmoid(cv)
        dv = d_ref[...]
        acc = act[:, 0:1] * dv[0:1, :]
        for b in range(1, N_DEV):
            acc = acc + act[:, b:b + 1] * dv[b:b + 1, :]
        o_ref[...] = acc

    return pl.pallas_call(
        body, name="w_ada_grad", out_shape=jax.ShapeDtypeStruct((D, dmod_part.shape[1]), F32),
    )(c_all_t, dmod_part)


def _sum_rows(name, v):
    def body(v_ref, o_ref):
        acc = v_ref[0]
        for k in range(1, N_DEV):
            acc = acc + v_ref[k]
        o_ref[...] = acc

    return pl.pallas_call(body, name=name, out_shape=jax.ShapeDtypeStruct(v.shape[1:], F32))(v)


def _adamw(name, w, g, m, v):
    rows, cols = w.shape
    limit = max(16, (1 << 20) // (4 * cols))
    tr = rows if rows <= limit else next((t for t in range(limit - limit % 16, 15, -16) if rows % t == 0), rows)
    c1 = 1.0 - ADAM_B1 ** ADAM_STEP
    c2 = 1.0 - ADAM_B2 ** ADAM_STEP
    parts = g.ndim == 3

    def body(w_ref, g_ref, m_ref, v_ref, go_ref, d_ref, nm_ref, nv_ref):
        if parts:
            gv = g_ref[0].astype(F32)
            for k in range(1, N_DEV):
                gv = gv + g_ref[k].astype(F32)
        else:
            gv = g_ref[...]
        go_ref[...] = gv
        nm = ADAM_B1 * m_ref[...] + (1.0 - ADAM_B1) * gv
        nv = ADAM_B2 * v_ref[...] + (1.0 - ADAM_B2) * (gv * gv)
        nm_ref[...] = nm
        nv_ref[...] = nv
        d_ref[...] = -ADAM_LR * ((nm / c1) / (jnp.sqrt(nv / c2) + ADAM_EPS) + ADAM_WD * w_ref[...])

    spec = pl.BlockSpec((tr, cols), lambda i: (i, 0))
    g_spec = pl.BlockSpec((N_DEV, tr, cols), lambda i: (0, i, 0)) if parts else spec
    return pl.pallas_call(
        body, name=name, grid=(rows // tr,),
        out_shape=[jax.ShapeDtypeStruct((rows, cols), F32)] * 4,
        in_specs=[spec, g_spec, spec, spec], out_specs=[spec] * 4,
        compiler_params=_params("parallel"),
    )(w, g, m, v)


def _adamw_small(ws, gs, ms, vs):
    n = len(ws)
    c1 = 1.0 - ADAM_B1 ** ADAM_STEP
    c2 = 1.0 - ADAM_B2 ** ADAM_STEP

    def body(*refs):
        for i in range(n):
            w_ref, g_ref, m_ref, v_ref = refs[i], refs[n + i], refs[2 * n + i], refs[3 * n + i]
            d_ref, nm_ref, nv_ref = refs[4 * n + 3 * i:4 * n + 3 * i + 3]
            gv = g_ref[...]
            nm = ADAM_B1 * m_ref[...] + (1.0 - ADAM_B1) * gv
            nv = ADAM_B2 * v_ref[...] + (1.0 - ADAM_B2) * (gv * gv)
            nm_ref[...] = nm
            nv_ref[...] = nv
            d_ref[...] = -ADAM_LR * ((nm / c1) / (jnp.sqrt(nv / c2) + ADAM_EPS) + ADAM_WD * w_ref[...])

    outs = pl.pallas_call(
        body, name="adamw_small",
        out_shape=[jax.ShapeDtypeStruct(w.shape, F32) for w in ws for _ in range(3)],
    )(*ws, *gs, *ms, *vs)
    return [tuple(outs[3 * i:3 * i + 3]) for i in range(n)]


HALF = FF // 2


def _sds(shape, dtype):
    return jax.ShapeDtypeStruct(shape, dtype)


def _row_tile(w):
    return lambda tm: ((tm, w), lambda i, j: (i, 0))


def _one(w):
    return lambda rows: ((rows, w), lambda i, j: (0, 0))


def _gate_up_swiglu(name, h, wgu, tm=512):
    s = h.shape[0]
    tm = min(tm, s)

    def epilogue(prod, first, tin, tout):
        pq_ref, s_ref = tout
        a, b = prod[:, :HALF], prod[:, HALF:]
        sig = _sigmoid(a)
        act = a * sig
        pq_ref[:, :HALF] = (b * (sig * (1.0 + a * (1.0 - sig)))).astype(BF16)
        pq_ref[:, HALF:] = act.astype(BF16)
        s_ref[...] = (act * b).astype(BF16)

    return _mm(name, h, wgu, "NT", None, tm, FF, D, n_outer=True, epilogue=epilogue,
               tiles_out=[(_sds((s, 2 * FF), BF16), (tm, FF), lambda i, j: (i, j)),
                          (_sds((s, FF), BF16), (tm, HALF), lambda i, j: (i, j))])


def _d_hidden_swiglu(name, df, wd, ab, after=(), tm=512):
    s = df.shape[0]
    tm = min(tm, s)

    def epilogue(prod, first, tin, tout, cols):
        da_cols = slice(cols[0], cols[0] + cols[1])
        db_cols = slice(HALF + cols[0], HALF + cols[0] + cols[1])
        tout[0][:, da_cols] = (prod * tin[0][:, da_cols].astype(F32)).astype(BF16)
        tout[0][:, db_cols] = (prod * tin[0][:, db_cols].astype(F32)).astype(BF16)

    chunks = [(c0, min(384, HALF - c0)) for c0 in range(0, HALF, 384)]
    return _mm(name, df, wd, "NT", None, tm, HALF, D, n_outer=True, epilogue=epilogue, col_chunks=chunks, after=after,
               tiles_in=[(ab, (tm, FF), lambda i, j: (i, j))],
               tiles_out=[(_sds((s, 2 * FF), BF16), (tm, FF), lambda i, j: (i, j))])[0]


def _out_residual(name, a, w, x, gt, coef, nxt, tm=512, tk=FF):
    s = a.shape[0]
    tm = min(tm, s)

    def epilogue(prod, first, tin, tout):
        x_ref, gt_ref, g_ref, sc_ref, sh_ref = tin
        f_ref, xn_ref, h_ref = tout
        f_ref[...] = prod
        xn = x_ref[...] + (coef * gt_ref[...]) * prod
        xn_ref[...] = xn
        r = lax.rsqrt(jnp.mean(xn * xn, axis=-1, keepdims=True) + EPS)
        h_ref[...] = ((xn * r) * g_ref[...] * (1.0 + sc_ref[...]) + sh_ref[...]).astype(BF16)

    row, vec = _row_tile(D)(tm), _one(D)(1)
    return _mm(name, a, w, "NN", None, tm, D, tk, epilogue=epilogue,
               tiles_in=[(x, *row), (gt, *vec)] + [(v, *vec) for v in nxt],
               tiles_out=[(_sds((s, D), F32), *row), (_sds((s, D), F32), *row), (_sds((s, D), BF16), *row)])


def _out_loss(name, a, w, x, gt, coef, target, tm=512):
    s = a.shape[0]
    tm = min(tm, s)

    def epilogue(prod, first, tin, tout):
        x_ref, gt_ref, t_ref = tin
        f_ref, g_ref, df_ref, acc_ref = tout
        f_ref[...] = prod
        cg = coef * gt_ref[...]
        e = x_ref[...] + cg * prod - t_ref[...]
        gv = e * (1.0 / D)
        g_ref[...] = gv
        df_ref[...] = (cg * gv).astype(BF16)

        @pl.when(first)
        def _():
            acc_ref[...] = jnp.zeros_like(acc_ref)

        acc_ref[0:1, :] += coef * jnp.sum(gv * prod, axis=0, keepdims=True)
        acc_ref[1:2, :] += (0.5 / D) * jnp.sum(e * e, axis=0, keepdims=True)

    row, vec = _row_tile(D)(tm), _one(D)(1)
    return _mm(name, a, w, "NN", None, tm, D, FF, epilogue=epilogue,
               tiles_in=[(x, *row), (gt, *vec), (target, *row)],
               tiles_out=[(_sds((s, D), F32), *row), (_sds((s, D), F32), *row), (_sds((s, D), BF16), *row),
                          (_sds((8, D), F32), *_one(D)(8))])


def _d_h_norm_bwd(name, da, w, x, gin, g, sc, sh, before=None, after=(), tm=256):
    s = da.shape[0]
    tm = min(tm, s)
    coef = before[2] if before else None

    def epilogue(prod, first, tin, tout):
        x_ref, gin_ref, g_ref, sc_ref, sh_ref = tin[:5]
        gout_ref, acc_ref = tout[:2]
        xv = x_ref[...]
        r = lax.rsqrt(jnp.mean(xv * xv, axis=-1, keepdims=True) + EPS)
        nv = xv * r
        gv, one_sc = g_ref[...], 1.0 + sc_ref[...]
        dn = prod * gv * one_sc
        gout = gin_ref[...] + r * (dn - nv * jnp.mean(dn * nv, axis=-1, keepdims=True))
        gout_ref[...] = gout

        @pl.when(first)
        def _():
            acc_ref[...] = jnp.zeros_like(acc_ref)

        dhn = prod * nv
        acc_ref[0:1, :] += jnp.sum(prod, axis=0, keepdims=True)
        acc_ref[1:2, :] += jnp.sum(dhn * gv, axis=0, keepdims=True)
        acc_ref[2:3, :] += jnp.sum(dhn * one_sc, axis=0, keepdims=True)
        if before:
            f_ref, gt_ref = tin[5:]
            tout[2][...] = ((coef * gt_ref[...]) * gout).astype(BF16)
            acc_ref[3:4, :] += coef * jnp.sum(gout * f_ref[...], axis=0, keepdims=True)

    row, vec = _row_tile(D)(tm), _one(D)(1)
    tiles_in = [(x, *row), (gin, *row), (g, *vec), (sc, *vec), (sh, *vec)]
    tiles_out = [(_sds((s, D), F32), *row), (_sds((8, D), F32), *_one(D)(8))]
    if before:
        tiles_in += [(before[0], *row), (before[1], *vec)]
        tiles_out.append((_sds((s, D), BF16), *row))
    return _mm(name, da, w, "NN", None, tm, D, da.shape[1], epilogue=epilogue, keep_b=True, after=after,
               tiles_in=tiles_in, tiles_out=tiles_out)


def _gate_tiles(proj, tm):
    return [(proj, (tm, COL), (lambda i, j, blk=blk: (i, blk))) for blk in (GA_BLK, GA_BLK + 1, GC_BLK, GC_BLK + 1)]


def _conv_branch_merge(z, wc, ya, proj, tm=1024):
    s = z.shape[0]
    tm = min(tm, s)

    def epilogue(prod, first, tin, tout):
        ya_ref, ga0, ga1, gc0, gc1 = tin
        tout[0][...] = prod.astype(BF16)
        for half, (ga, gc) in enumerate(((ga0, gc0), (ga1, gc1))):
            cols = slice(half * COL, (half + 1) * COL)
            tout[1][:, cols] = (_sigmoid(ga[...].astype(F32)) * ya_ref[:, cols].astype(F32)
                                + _sigmoid(gc[...].astype(F32)) * prod[:, cols]).astype(BF16)

    row = _row_tile(D)(tm)
    return _mm("mix_conv_branch", z, wc, "NN", None, tm, D, D, epilogue=epilogue,
               tiles_in=[(ya, *row)] + _gate_tiles(proj, tm),
               tiles_out=[(_sds((s, D), BF16), *row), (_sds((s, D), BF16), *row)])


def _d_merged_branches(dmix, wo, ya, yc, proj, tm=1024):
    s = dmix.shape[0]
    tm = min(tm, s)

    def epilogue(prod, first, tin, tout):
        ya_ref, yc_ref, ga0, ga1, gc0, gc1 = tin
        dya_ref, dyc_ref, dg_ref = tout
        for half, (ga, gc) in enumerate(((ga0, gc0), (ga1, gc1))):
            cols = slice(half * COL, (half + 1) * COL)
            dm = prod[:, cols]
            for y_ref, g_ref, dy_ref, off in ((ya_ref, ga, dya_ref, 0), (yc_ref, gc, dyc_ref, D)):
                sig = _sigmoid(g_ref[...].astype(F32))
                dms = dm * sig
                dy_ref[:, cols] = dms.astype(BF16)
                dg_ref[:, off + half * COL:off + (half + 1) * COL] = (
                    dms * y_ref[:, cols].astype(F32) * (1.0 - sig)).astype(BF16)

    row = _row_tile(D)(tm)
    return _mm("mix_d_merged", dmix, wo, "NT", None, tm, D, D, epilogue=epilogue,
               tiles_in=[(ya, *row), (yc, *row)] + _gate_tiles(proj, tm),
               tiles_out=[(_sds((s, D), BF16), *row), (_sds((s, D), BF16), *row),
                          (_sds((s, 2 * D), BF16), *_row_tile(2 * D)(tm))])


def _d_o_delta(dya, wa_t, o, tm=1024):
    s = dya.shape[0]
    tm = min(tm, s)

    def epilogue(prod, first, tin, tout):
        tout[0][...] = prod
        tout[1][...] = _heads(prod * tin[0][...].astype(F32), lambda ph, h: jnp.broadcast_to(
            jnp.sum(ph, axis=-1, keepdims=True), ph.shape))

    row = _row_tile(COL)(tm)
    return _mm("mix_d_o", dya, wa_t, "NN", None, tm, COL, D, epilogue=epilogue,
               tiles_in=[(o, *row)], tiles_out=[(_sds((s, COL), F32), *row), (_sds((s, COL), F32), *row)])


def _ffn_bwd(tag, df, x, gin, h, ab, sw, g, sc, sh, wgu, wd, before=None, tk_dw=2048):
    dwd = _mm(f"{tag}_dw_down", sw, df, "TN", BF16, HALF, D, tk_dw)
    dab = _d_hidden_swiglu(f"{tag}_d_hidden", df, wd, ab, after=[dwd])
    dwgu = _mm(f"{tag}_dw_gate_up", dab, h, "TN", BF16, HALF, D, tk_dw)
    res = _d_h_norm_bwd(f"{tag}_d_h", dab, wgu, x, gin, g, sc, sh, before=before, after=[dwgu], tm=512)
    return res, dwgu, dwd


def kernel(x, c, w_ada, b_ada, norm_ffn1, ffn1_w_gate, ffn1_w_up, ffn1_w_down, norm_mix, w_in, q_norm, k_norm, conv_w, w_attn_branch, w_conv_branch, w_out, norm_ffn2, ffn2_w_gate, ffn2_w_up, ffn2_w_down, loss_target, m_w_ada, m_b_ada, m_norm_ffn1, m_ffn1_w_gate, m_ffn1_w_up, m_ffn1_w_down, m_norm_mix, m_w_in, m_q_norm, m_k_norm, m_conv_w, m_w_attn_branch, m_w_conv_branch, m_w_out, m_norm_ffn2, m_ffn2_w_gate, m_ffn2_w_up, m_ffn2_w_down, v_w_ada, v_b_ada, v_norm_ffn1, v_ffn1_w_gate, v_ffn1_w_up, v_ffn1_w_down, v_norm_mix, v_w_in, v_q_norm, v_k_norm, v_conv_w, v_w_attn_branch, v_w_conv_branch, v_w_out, v_norm_ffn2, v_ffn2_w_gate, v_ffn2_w_up, v_ffn2_w_down):
    me = 4 * lax.axis_index("x") + 2 * lax.axis_index("y") + lax.axis_index("c")
    x0, target = x[0], loss_target[0]
    s = x0.shape[0]
    ada_cols = w_ada.shape[2]
    cw_cols = conv_w.shape[2]

    gathered = _small_allgather(
        "gather_c_conv", jnp.concatenate([c, conv_w[0].reshape(1, 3 * cw_cols)], axis=1))[:, 0]
    c_all = gathered[:, :D]
    cw = gathered[:, D:].reshape(N_DEV, 3, cw_cols).transpose(1, 0, 2).reshape(3, D)
    b_part = lax.dynamic_slice(b_ada, (0, me * ada_cols), (1, ada_cols))
    mod_part = _mod_part(c_all, w_ada[0], b_part)
    mod_all = _small_allgather("gather_mod", mod_part.reshape(1, N_DEV * ada_cols))
    mod = lax.dynamic_slice(mod_all.reshape(N_DEV, N_DEV, ada_cols), (0, me, 0), (N_DEV, 1, ada_cols))
    mod = mod.reshape(N_MOD, 1, D)
    sh1, sc1, gt1, sh2, sc2, gt2, sh3, sc3, gt3 = [mod[i] for i in range(N_MOD)]

    tb = lambda w: w[0].T.astype(BF16)
    nb = lambda w: w[0].astype(BF16)
    ffn1_shards = [tb(ffn1_w_gate), tb(ffn1_w_up), nb(ffn1_w_down)]
    ffn2_shards = [tb(ffn2_w_gate), tb(ffn2_w_up), nb(ffn2_w_down)]
    mix_shards = [tb(w_in), tb(w_attn_branch), nb(w_conv_branch), nb(w_out)]
    ffn_dst, ffn_base, ffn_jump, ffn_shapes = [0, 0, 1], [0, HALF, 0], [HALF, HALF, 0], [(2 * FF, D), (FF, D)]
    mix_dst, mix_base, mix_shapes = [0, 1, 2, 3], [0, 0, 0, 0], [(IN_W, D), (D, COL), (D, D), (D, D)]
    (wgu1,) = _run_plan_on_sequencer(
        "gather_ffn1_gate_up", _gather_plan(ffn1_shards[:2], ffn_dst[:2], ffn_base[:2], ffn_shapes[:1], ffn_jump[:2]), 1)
    (wd1,) = _run_plan_on_sequencer(
        "gather_ffn1_down", _gather_plan(ffn1_shards[2:], [0], [0], ffn_shapes[1:]), 8)
    win_t, wa_t, wc, wo = _run_plan_on_sequencer(
        "gather_mix_weights", _gather_plan(mix_shards, mix_dst, mix_base, mix_shapes), 2)
    wgu2, wd2 = _run_plan_on_sequencer(
        "gather_ffn2_weights", _gather_plan(ffn2_shards, ffn_dst, ffn_base, ffn_shapes, ffn_jump), 3)

    h1 = _normmod("ffn1_normmod", x0, norm_ffn1, sc1, sh1)
    ab1, s1 = _gate_up_swiglu("ffn1_gate_up", h1, wgu1)
    f1, x1, h2 = _out_residual("ffn1_down", s1, wd1, x0, gt1, 0.5, (norm_mix, sc2, sh2))
    proj = _mm("mix_in_proj", h2, win_t, "NT", BF16, 1024, IN_W // 4, D, n_outer=True)
    wqk = jnp.concatenate([jnp.tile(q_norm, (1, 12)), jnp.tile(k_norm, (1, 12))], axis=1)
    qkn = _qknorm(proj, wqk)
    group_out = [_attn_fwd(g, qkn, proj) for g in range(3)]
    o, lse = _attn_combine([go[0] for go in group_out], [go[1] for go in group_out])
    ya = _mm("mix_attn_branch", o, wa_t, "NT", BF16, 1024, 1024, COL)
    z = _conv_fwd(proj, cw)
    yc, merged = _conv_branch_merge(z, wc, ya, proj)
    mix, x2, h3 = _out_residual("mix_out_proj", merged, wo, x1, gt2, 1.0, (norm_ffn2, sc3, sh3), tm=1024, tk=D)
    ab3, s3 = _gate_up_swiglu("ffn2_gate_up", h3, wgu2)
    f3, g3, df3, acc_out = _out_loss("ffn2_down", s3, wd2, x2, gt3, 0.5, target)
    loss_part = jnp.sum(acc_out[1])

    ffn_rows = [sh_.shape[0] for sh_ in ffn1_shards]
    mix_rows = [sh_.shape[0] for sh_ in mix_shards]
    (g2, acc3, dmix), dwgu2, dwd2 = _ffn_bwd(
        "ffn2", df3, x2, g3, h3, ab3, s3, norm_ffn2, sc3, sh3, wgu2, wd2, before=(mix, gt2, 1.0))
    dya, dyc, dgates = _d_merged_branches(dmix, wo, ya, yc, proj)
    dwo = _mm("mix_dw_out", merged, dmix, "TN", BF16, 1024, 1024, 2048)
    dproj = _copy_columns("dproj_gates", dgates, jax.ShapeDtypeStruct((s, IN_W), BF16), GA_BLK)
    dwc = _mm("mix_dw_conv_branch", z, dyc, "TN", BF16, 1024, 1024, 2048)
    dz = _mm("mix_d_z", dyc, wc, "NT", F32, 1024, 1024, D)
    dproj, d_c, cw_acc = _conv_bwd(dz, proj, cw, dproj)
    dproj = _copy_columns("copy_d_c", d_c, dproj, C_BLK)
    dwa_t = _mm("mix_dw_attn_branch", dya, o, "TN", BF16, 1024, COL, 2048)
    do, delta = _d_o_delta(dya, wa_t, o)
    dqn = dkn = None
    for g in range(3):
        dqn, dkn, dproj = _attn_bwd(g, qkn, proj, do, lse, delta, dqn, dkn, dproj)
    dproj, wq_acc = _qknorm_bwd("qnorm_bwd", proj, dqn, wqk[:, :QKW // 2], dproj, 0)
    dproj, wk_acc = _qknorm_bwd("knorm_bwd", proj, dkn, wqk[:, QKW // 2:], dproj, QKW // 2 // COL)
    r_f2g, r_f2u, r_f2d, r_wa, r_wc, r_wo = _run_plan_on_sequencer(
        "scatter_ffn2_and_branch_grads",
        _scatter_plan([dwgu2, dwd2, dwa_t, dwc, dwo], [0, 0, 1, 2, 3, 4], [0, HALF, 0, 0, 0, 0],
                      ffn_rows + mix_rows[1:], [D, D, D, COL, D, D], [HALF, HALF, 0, 0, 0, 0]), 4)
    dwin_t = _mm("mix_dw_in", dproj, h2, "TN", BF16, IN_W // 4, COL, 2048)
    (r_win,) = _run_plan_on_sequencer(
        "scatter_w_in_grad", _scatter_plan([dwin_t], [0], [0], mix_rows[:1], [D]), 5)
    g1, acc2, df1 = _d_h_norm_bwd("mix_d_h", dproj, win_t, x1, g2, norm_mix, sc2, sh2, before=(f1, gt1, 0.5),
                                  after=[dwin_t])
    dwd1 = _mm("ffn1_dw_down", s1, df1, "TN", BF16, HALF, D, 2048)
    (r_f1d,) = _run_plan_on_sequencer(
        "scatter_ffn1_down_grad", _scatter_plan([dwd1], [0], [0], ffn_rows[2:], [D]), 6)
    dab1 = _d_hidden_swiglu("ffn1_d_hidden", df1, wd1, ab1, after=[dwd1, r_win])
    dwgu1 = _mm("ffn1_dw_gate_up", dab1, h1, "TN", BF16, HALF, D, 2048)
    r_f1g, r_f1u = _run_plan_on_sequencer(
        "scatter_ffn1_gate_up_grads",
        _scatter_plan([dwgu1], [0, 0], [0, HALF], ffn_rows[:2], [D, D], [HALF, HALF]), 7)
    g0, acc1 = _d_h_norm_bwd("ffn1_d_h", dab1, wgu1, x0, g1, norm_ffn1, sc1, sh1, after=[dwgu1, r_f1d], tm=512)

    dqw = jnp.sum(wq_acc[0].reshape(12, HD), axis=0)
    dkw = jnp.sum(wk_acc[0].reshape(12, HD), axis=0)
    small = jnp.concatenate([
        acc1[0], acc1[1], acc2[3], acc2[0], acc2[1], acc3[3], acc3[0], acc3[1], acc_out[0],
        acc1[2], acc2[2], acc3[2], dqw, dkw, cw_acc[0:3].reshape(3 * D),
        jnp.zeros((HD,), F32).at[0].set(loss_part)]).reshape(1, -1)
    small_all = _small_allgather("gather_small_grads", small)
    small_sum = _sum_rows("sum_small_grads", small_all)[0]
    n_mod = N_MOD * D
    g_b_ada = small_sum[:n_mod].reshape(1, n_mod)
    g_norm1, g_norm2, g_norm3 = [small_sum[n_mod + i * D:n_mod + (i + 1) * D].reshape(1, D) for i in range(3)]
    off = n_mod + 3 * D
    g_qn, g_kn = small_sum[off:off + HD].reshape(1, HD), small_sum[off + HD:off + 2 * HD].reshape(1, HD)
    g_cw_full = small_sum[off + 2 * HD:off + 2 * HD + 3 * D].reshape(3, D)
    loss = small_sum[off + 2 * HD + 3 * D]
    g_cw = lax.dynamic_slice(g_cw_full, (0, me * cw_cols), (3, cw_cols))
    dmod_part = lax.dynamic_slice(small_all[:, 0, :n_mod], (0, me * ada_cols), (N_DEV, ada_cols))
    g_w_ada = _w_ada_grad(c_all.T, dmod_part)

    as_rows = {"ffn1_w_gate", "ffn1_w_up", "w_in", "w_attn_branch", "ffn2_w_gate", "ffn2_w_up"}
    grad_list = [g_w_ada, g_b_ada, g_norm1, r_f1g, r_f1u, r_f1d, g_norm2, r_win,
                 g_qn, g_kn, g_cw, r_wa, r_wc, r_wo, g_norm3, r_f2g, r_f2u, r_f2d]
    weights = [w_ada, b_ada, norm_ffn1, ffn1_w_gate, ffn1_w_up, ffn1_w_down, norm_mix, w_in, q_norm, k_norm,
               conv_w, w_attn_branch, w_conv_branch, w_out, norm_ffn2, ffn2_w_gate, ffn2_w_up, ffn2_w_down]
    ms = [m_w_ada, m_b_ada, m_norm_ffn1, m_ffn1_w_gate, m_ffn1_w_up, m_ffn1_w_down, m_norm_mix, m_w_in, m_q_norm,
          m_k_norm, m_conv_w, m_w_attn_branch, m_w_conv_branch, m_w_out, m_norm_ffn2, m_ffn2_w_gate,
          m_ffn2_w_up, m_ffn2_w_down]
    vs = [v_w_ada, v_b_ada, v_norm_ffn1, v_ffn1_w_gate, v_ffn1_w_up, v_ffn1_w_down, v_norm_mix, v_w_in, v_q_norm,
          v_k_norm, v_conv_w, v_w_attn_branch, v_w_conv_branch, v_w_out, v_norm_ffn2, v_ffn2_w_gate,
          v_ffn2_w_up, v_ffn2_w_down]
    wnames = ["w_ada", "b_ada", "norm_ffn1", "ffn1_w_gate", "ffn1_w_up", "ffn1_w_down", "norm_mix", "w_in",
              "q_norm", "k_norm", "conv_w", "w_attn_branch", "w_conv_branch", "w_out", "norm_ffn2",
              "ffn2_w_gate", "ffn2_w_up", "ffn2_w_down"]
    small = [i for i, gr in enumerate(grad_list) if gr.ndim == 2 and gr.size <= 16384]
    flat = lambda a, i: a.reshape(-1, weights[i].shape[-1])
    small_res = dict(zip(small, _adamw_small(
        [flat(weights[i], i) for i in small], [flat(grad_list[i], i) for i in small],
        [flat(ms[i], i) for i in small], [flat(vs[i], i) for i in small])))
    grad_out, deltas, new_ms, new_vs = [], [], [], []
    for idx, (nm, w, gr, m_, v_) in enumerate(zip(wnames, weights, grad_list, ms, vs)):
        if idx in small_res:
            gr, dl, nm_, nv_ = [r.reshape(w.shape) for r in (gr, *small_res[idx])]
        elif nm in as_rows:
            res = _adamw(f"adamw_{nm}", w[0].T, gr, m_[0].T, v_[0].T)
            gr, dl, nm_, nv_ = [r.T[None] for r in res]
        else:
            two_d = (-1, w.shape[-1])
            res = _adamw(f"adamw_{nm}", w.reshape(two_d), gr if gr.ndim == 3 else gr.reshape(two_d),
                         m_.reshape(two_d), v_.reshape(two_d))
            gr, dl, nm_, nv_ = [r.reshape(w.shape) for r in res]
        grad_out.append(gr)
        deltas.append(dl)
        new_ms.append(nm_)
        new_vs.append(nv_)
    return (loss, g0[None], *grad_out, *deltas, *new_ms, *new_vs)
```

```python
import jax
import jax.numpy as jnp
from jax import lax
from jax.experimental import pallas as pl
from jax.experimental.pallas import tpu as pltpu
from jax.experimental.pallas import tpu_sc as plsc

F32 = jnp.float32
BF16 = jnp.bfloat16
MESH = pl.DeviceIdType.MESH

N_DEV = 8
D = 1024
FF = 2816
HD = 128
N_HEADS = 4
DILATIONS = (1, 4, 16)
BAND = 128
QKW = 2 * 3 * N_HEADS * HD
IN_W = 9728
COL = 512
V_BLK, U_BLK, B_BLK, C_BLK, GA_BLK, GC_BLK = 6, 9, 11, 13, 15, 17
EPS = 1e-6
N_MOD = 9
ADAM_LR, ADAM_B1, ADAM_B2, ADAM_EPS, ADAM_WD, ADAM_STEP = 0.001, 0.9, 0.999, 1e-08, 0.01, 10

NT_DIMS = (((1,), (1,)), ((), ()))
TN_DIMS = (((0,), (0,)), ((), ()))
NN_DIMS = (((1,), (0,)), ((), ()))


def _place():
    return lax.axis_index("x"), lax.axis_index("y"), lax.axis_index("c")


def _flip(coord, bit):
    return 1 - coord if bit else coord


def _params(*sem):
    return pltpu.CompilerParams(dimension_semantics=sem)


def _small_allgather(name, v):
    n = v.shape[-1]

    def body(v_ref, out_ref, send_sems, recv_sems):
        x, y, c = _place()
        me = 4 * x + 2 * y + c
        out_ref[me] = v_ref[...]
        copies = []
        for k in range(1, N_DEV):
            peer = (_flip(x, (k >> 2) & 1), _flip(y, (k >> 1) & 1), _flip(c, k & 1))
            cp = pltpu.make_async_remote_copy(
                src_ref=v_ref, dst_ref=out_ref.at[me], send_sem=send_sems.at[k - 1],
                recv_sem=recv_sems.at[k - 1], device_id=peer, device_id_type=MESH)
            cp.start()
            copies.append(cp)
        for cp in copies:
            cp.wait()

    return pl.pallas_call(
        body, name=name,
        out_shape=jax.ShapeDtypeStruct((N_DEV, 1, n), F32),
        in_specs=[pl.BlockSpec(memory_space=pltpu.VMEM)],
        out_specs=pl.BlockSpec(memory_space=pltpu.VMEM),
        scratch_shapes=[pltpu.SemaphoreType.DMA((N_DEV - 1,)), pltpu.SemaphoreType.DMA((N_DEV - 1,))],
    )(v)


class _Plan:
    def __init__(self, operands, out_shapes, sems, phases):
        self.operands, self.out_shapes, self.sems, self.phases = operands, out_shapes, sems, phases


def _slab_start(base, rows, jump, idx):
    return pl.multiple_of(base + idx * rows + (idx // 4) * jump, 16)


def _gather_plan(shards, dst_of, base_of, dst_shapes, jump_of=None):
    n = len(shards)
    rows = [s.shape[0] for s in shards]
    jump_of = jump_of or [0] * n

    def phases(srcs, dsts, sems):
        send_sems, recv_sems, local_sems = sems
        x, y, c = _place()
        me, sibling = (x, y, c), (x, y, 1 - c)
        chips = [(1 - x, y), (x, 1 - y), (1 - x, 1 - y)]

        def slab(i, px, py, pc):
            start = _slab_start(base_of[i], rows[i], jump_of[i], 4 * px + 2 * py + pc)
            return dsts[dst_of[i]].at[pl.ds(start, rows[i])]

        def copy(i, k, block, to, src=None):
            return pltpu.make_async_remote_copy(
                src_ref=slab(i, *block) if src is None else src, dst_ref=slab(i, *block),
                send_sem=send_sems.at[i, k], recv_sem=recv_sems.at[i, k],
                device_id=to, device_id_type=MESH)

        def mine():
            return [pltpu.make_async_copy(srcs[i], slab(i, *me), local_sems.at[i]) for i in range(n)]

        def first():
            out = []
            for i in range(n):
                out.append(copy(i, 0, me, sibling, src=srcs[i]))
                out += [copy(i, 1 + j, me, (*chip, c), src=srcs[i]) for j, chip in enumerate(chips)]
            return out

        def passed():
            return [(copy(i, 1 + j, (*chip, c), me), copy(i, 4 + j, (*chip, c), sibling))
                    for j, chip in enumerate(chips) for i in range(n)]

        def start():
            for cp in mine() + first():
                cp.start()

        def middle():
            for landed, onward in passed():
                landed.wait_recv()
                onward.start()

        def finish():
            for i in range(n):
                copy(i, 0, sibling, me).wait_recv()
                for j, chip in enumerate(chips):
                    copy(i, 4 + j, (*chip, 1 - c), me).wait_recv()
            for cp in first() + [onward for _, onward in passed()]:
                cp.wait_send()
            for cp in mine():
                cp.wait()

        return start, middle, finish

    sems = [pltpu.SemaphoreType.DMA((n, 7)), pltpu.SemaphoreType.DMA((n, 7)), pltpu.SemaphoreType.DMA((n,))]
    return _Plan(list(shards), [jax.ShapeDtypeStruct(s, BF16) for s in dst_shapes], sems, phases)


def _scatter_plan(grads, src_of, base_of, rows, cols, jump_of=None):
    n = len(rows)
    jump_of = jump_of or [0] * n

    def phases(srcs, recvs, sems):
        send_sems, recv_sems, local_sems = sems
        x, y, c = _place()
        me = 4 * x + 2 * y + c

        def slab(i, idx):
            start = _slab_start(base_of[i], rows[i], jump_of[i], idx)
            return srcs[src_of[i]].at[pl.ds(start, rows[i])]

        def copies():
            out = [pltpu.make_async_copy(slab(i, me), recvs[i].at[me], local_sems.at[i]) for i in range(n)]
            for k in range(1, N_DEV):
                px, py, pc = _flip(x, (k >> 2) & 1), _flip(y, (k >> 1) & 1), _flip(c, k & 1)
                out += [pltpu.make_async_remote_copy(
                    src_ref=slab(i, 4 * px + 2 * py + pc), dst_ref=recvs[i].at[me],
                    send_sem=send_sems.at[i, k - 1], recv_sem=recv_sems.at[i, k - 1],
                    device_id=(px, py, pc), device_id_type=MESH) for i in range(n)]
            return out

        def start():
            for cp in copies():
                cp.start()

        def finish():
            for cp in copies():
                cp.wait()

        return start, None, finish

    sems = [pltpu.SemaphoreType.DMA((n, 7)), pltpu.SemaphoreType.DMA((n, 7)), pltpu.SemaphoreType.DMA((n,))]
    out_shapes = [jax.ShapeDtypeStruct((N_DEV, rows[i], cols[i]), BF16) for i in range(n)]
    return _Plan(list(grads), out_shapes, sems, phases)


def _run_plan_on_sequencer(name, plan, collective_id):
    src_refs = [jax.new_ref(a, memory_space=pltpu.MemorySpace.HBM) for a in plan.operands]
    dst_refs = [jax.empty_ref(s, memory_space=pltpu.MemorySpace.HBM) for s in plan.out_shapes]

    @pl.kernel(mesh=plsc.ScalarSubcoreMesh(axis_name="sequencer", num_cores=1), name=name,
               scratch_types=tuple(plan.sems),
               compiler_params=pltpu.CompilerParams(collective_id=collective_id))
    def launch(*sems):
        x, y, c = _place()
        barrier = pltpu.get_barrier_semaphore()
        for k in range(1, N_DEV):
            peer = (_flip(x, (k >> 2) & 1), _flip(y, (k >> 1) & 1), _flip(c, k & 1))
            pl.semaphore_signal(barrier, inc=1, device_id=peer, device_id_type=MESH)
        pl.semaphore_wait(barrier, N_DEV - 1)
        for phase in plan.phases(src_refs, dst_refs, sems):
            if phase is not None:
                phase()

    launch()
    return [r[...] for r in dst_refs]


def _mm(name, a, b, mode, out_dtype, tm, tn, tk, *, tiles_in=(), tiles_out=(), epilogue=None,
        n_outer=False, keep_b=False, col_chunks=None, after=()):
    if mode == "TN":
        kk, m = a.shape
    else:
        m, kk = a.shape
    n = b.shape[0] if mode == "NT" else b.shape[1]
    tm, tn, tk = min(tm, m), min(tn, n), min(tk, kk)
    assert m % tm == 0 and n % tn == 0 and kk % tk == 0, (name, m, n, kk, tm, tn, tk)
    ni, nj, nk = m // tm, n // tn, kk // tk
    dims = {"NN": NN_DIMS, "NT": NT_DIMS, "TN": TN_DIMS}[mode]
    if epilogue is None:
        tiles_out = [(jax.ShapeDtypeStruct((m, n), out_dtype), (tm, tn), lambda i, j: (i, j))]
    n_tin, n_tout = len(tiles_in), len(tiles_out)
    n_acc = 1 if nk > 1 else 0
    n_after = len(after)
    assert not keep_b or (nk == 1 and nj == 1)
    assert not col_chunks or (epilogue is not None and nk == 1 and mode != "TN")
    ij = (lambda p, q: (q, p)) if n_outer else (lambda p, q: (p, q))
    inner = ni if n_outer else nj

    def body(a_ref, b_ref, *rest):
        tin = rest[:n_tin]
        tout = rest[n_tin + n_after:n_tin + n_after + n_tout]
        scratch = rest[n_tin + n_after + n_tout:]
        k = pl.program_id(2)
        visit = pl.program_id(0) * inner + pl.program_id(1)
        if keep_b:
            b_kept, b_sem = scratch[n_acc:n_acc + 2]

            @pl.when((visit == 0) & (k == 0))
            def _():
                cp = pltpu.make_async_copy(b_ref, b_kept, b_sem)
                cp.start()
                cp.wait()

            b_ref = b_kept

        def store(prod, c=0, cols=()):
            if epilogue is None:
                tout[0][...] = prod.astype(out_dtype)
            else:
                epilogue(prod, jnp.logical_and(visit == 0, c == 0), tin, tout, *cols)

        if col_chunks:
            for c, (c0, cw) in enumerate(col_chunks):
                b_part = b_ref[pl.ds(c0, cw), :] if mode == "NT" else b_ref[:, pl.ds(c0, cw)]
                store(lax.dot_general(a_ref[...], b_part, dims, preferred_element_type=F32), c, ((c0, cw),))
        else:
            part = lax.dot_general(a_ref[...], b_ref[...], dims, preferred_element_type=F32)
            if nk == 1:
                store(part)
            else:
                acc_ref = scratch[0]

                @pl.when(k == 0)
                def _():
                    acc_ref[...] = part

                @pl.when((k > 0) & (k < nk - 1))
                def _():
                    acc_ref[...] += part

                @pl.when(k == nk - 1)
                def _():
                    store(acc_ref[...] + part)

    def spec(shape, fn):
        return pl.BlockSpec(shape, lambda p, q, k: fn(*ij(p, q)))

    a_spec = (pl.BlockSpec((tk, tm), lambda p, q, k: (k, ij(p, q)[0])) if mode == "TN"
              else pl.BlockSpec((tm, tk), lambda p, q, k: (ij(p, q)[0], k)))
    if keep_b:
        b_spec = pl.BlockSpec(memory_space=pl.ANY)
    elif mode == "NT":
        b_spec = pl.BlockSpec((tn, tk), lambda p, q, k: (ij(p, q)[1], k))
    else:
        b_spec = pl.BlockSpec((tk, tn), lambda p, q, k: (k, ij(p, q)[1]))
    sequential = epilogue or keep_b
    out = pl.pallas_call(
        body, name=name, grid=(nj, ni, nk) if n_outer else (ni, nj, nk),
        out_shape=[t[0] for t in tiles_out],
        in_specs=([a_spec, b_spec] + [spec(t[1], t[2]) for t in tiles_in]
                  + [pl.BlockSpec(memory_space=pl.ANY)] * n_after),
        out_specs=[spec(t[1], t[2]) for t in tiles_out],
        scratch_shapes=([pltpu.VMEM((tm, tn), F32)] * n_acc
                        + ([pltpu.VMEM(b.shape, b.dtype), pltpu.SemaphoreType.DMA] if keep_b else [])),
        compiler_params=(_params("arbitrary", "arbitrary", "arbitrary") if sequential
                         else _params("parallel", "parallel", "arbitrary")),
    )(a, b, *[t[0] for t in tiles_in], *after)
    return out if epilogue else out[0]


def _row(tm, w, off=0):
    return pl.BlockSpec((tm, w), lambda i: (i, off))


def _vec(w):
    return pl.BlockSpec((1, w), lambda i: (0, 0))


def _sigmoid(x):
    return 0.5 * jnp.tanh(0.5 * x) + 0.5


def _normmod(name, x, g, sc, sh, tm=512):
    s = x.shape[0]

    def body(x_ref, g_ref, sc_ref, sh_ref, h_ref):
        xv = x_ref[...]
        r = lax.rsqrt(jnp.mean(xv * xv, axis=-1, keepdims=True) + EPS)
        h_ref[...] = ((xv * r) * g_ref[...] * (1.0 + sc_ref[...]) + sh_ref[...]).astype(BF16)

    return pl.pallas_call(
        body, name=name, grid=(s // tm,),
        out_shape=jax.ShapeDtypeStruct((s, D), BF16),
        in_specs=[_row(tm, D), _vec(D), _vec(D), _vec(D)], out_specs=_row(tm, D),
        compiler_params=_params("parallel"),
    )(x, g, sc, sh)


def _heads(x, fn):
    return jnp.concatenate([fn(x[:, h * HD:(h + 1) * HD], h) for h in range(COL // HD)], axis=1)


def _head_mean(x):
    return jnp.dot(x.astype(BF16), jnp.full((HD, HD), 1.0 / HD, BF16), preferred_element_type=F32)


def _qknorm(proj, wqk, tm=1024):
    s = proj.shape[0]

    def body(p_ref, w_ref, o_ref):
        pv = p_ref[...].astype(F32)
        wv = w_ref[...]

        def one(qh, h):
            r = lax.rsqrt(_head_mean(qh * qh) + EPS)
            return (qh * r) * wv[:, h * HD:(h + 1) * HD]

        o_ref[...] = _heads(pv, one).astype(BF16)

    return pl.pallas_call(
        body, name="qknorm", grid=(s // tm, QKW // COL),
        out_shape=jax.ShapeDtypeStruct((s, QKW), BF16),
        in_specs=[pl.BlockSpec((tm, COL), lambda i, j: (i, j)), pl.BlockSpec((1, COL), lambda i, j: (0, j))],
        out_specs=pl.BlockSpec((tm, COL), lambda i, j: (i, j)),
        compiler_params=_params("parallel", "parallel"),
    )(proj, wqk)


def _qknorm_bwd(name, proj, dn, w, dproj, blk0, tm=1024):
    s = proj.shape[0]
    nblk = dn.shape[1] // COL

    def body(p_ref, d_ref, w_ref, _, o_ref, acc_ref):
        pv = p_ref[...].astype(F32)
        dv = d_ref[...]
        wv = w_ref[...]
        sums = []

        def one(qh, h):
            dn = dv[:, h * HD:(h + 1) * HD]
            r = lax.rsqrt(_head_mean(qh * qh) + EPS)
            nh = qh * r
            sums.append(jnp.sum(dn * nh, axis=0, keepdims=True))
            dnw = dn * wv[:, h * HD:(h + 1) * HD]
            return r * (dnw - nh * _head_mean(dnw * nh))

        o_ref[...] = _heads(pv, one).astype(BF16)

        @pl.when(pl.program_id(1) == 0)
        def _():
            acc_ref[...] = jnp.zeros_like(acc_ref)

        acc_ref[0:1, :] += jnp.concatenate(sums, axis=1)

    return pl.pallas_call(
        body, name=name, grid=(nblk, s // tm),
        out_shape=[jax.ShapeDtypeStruct((s, IN_W), BF16), jax.ShapeDtypeStruct((8, nblk * COL), F32)],
        in_specs=[pl.BlockSpec((tm, COL), lambda j, i: (i, blk0 + j)), pl.BlockSpec((tm, COL), lambda j, i: (i, j)),
                  pl.BlockSpec((1, COL), lambda j, i: (0, j)), pl.BlockSpec(memory_space=pl.ANY)],
        out_specs=[pl.BlockSpec((tm, COL), lambda j, i: (i, blk0 + j)),
                   pl.BlockSpec((8, COL), lambda j, i: (0, j))],
        input_output_aliases={3: 0},
        compiler_params=_params("arbitrary", "arbitrary"),
    )(proj, dn, w, dproj)


def _attn_shapes(s, g):
    d = DILATIONS[g]
    tb = min(s, max(2048, 256 * d))
    sb = min(256, tb // d)
    pb = BAND * d
    assert s % tb == 0 and tb % pb == 0 and (tb // d) % sb == 0 and sb % BAND == 0
    return d, tb, sb, pb


def _lanes(x, width):
    return jnp.concatenate([x] * (width // HD), axis=1)


def _every(start, size, d):
    return pl.ds(start, size, stride=d) if d > 1 else pl.ds(start, size)


def _attn_specs(g, tb, pb, s, ahead):
    ratio = tb // pb
    if ahead:
        nbr = lambda n: jnp.minimum((n + 1) * ratio, s // pb - 1)
    else:
        nbr = lambda n: jnp.maximum(n * ratio - 1, 0)
    cur = lambda base: pl.BlockSpec((tb, HD), lambda h, n: (n, base + g * N_HEADS + h))
    side = lambda base: pl.BlockSpec((pb, HD), lambda h, n: (nbr(n), base + g * N_HEADS + h))
    tok = pl.BlockSpec((tb, HD), lambda h, n: (n, h))
    tok_side = pl.BlockSpec((pb, HD), lambda h, n: (nbr(n), h))
    return cur, side, tok, tok_side


Q_COL, K_COL, V_COL = 0, 12, 24


def _attn_fwd(g, qkn, proj):
    s = qkn.shape[0]
    d, tb, sb, pb = _attn_shapes(s, g)
    ft = F32 if d > 1 else BF16
    nj = tb // d // sb
    scale = HD ** -0.5

    def body(q_ref, kc_ref, kp_ref, vc_ref, vp_ref, o_ref, lse_ref, qf, kf, vf):
        n = pl.program_id(1)
        qf[...] = q_ref[...].astype(ft)
        kf[0:pb] = kp_ref[...].astype(ft)
        kf[pb:] = kc_ref[...].astype(ft)
        vf[0:pb] = vp_ref[...].astype(ft)
        vf[pb:] = vc_ref[...].astype(ft)
        for r in range(d):
            for j in range(nj):
                at = j * sb * d + r
                q = qf[_every(at, sb, d), :].astype(BF16)
                k = kf[_every(at, sb + BAND, d), :].astype(BF16)
                v = vf[_every(at, sb + BAND, d), :].astype(BF16)
                sc = lax.dot_general(q, k, NT_DIMS, preferred_element_type=F32) * scale
                qi = lax.broadcasted_iota(jnp.int32, sc.shape, 0)
                kj = lax.broadcasted_iota(jnp.int32, sc.shape, 1)
                valid = (kj >= qi) & (kj <= qi + BAND)
                if j == 0:
                    valid = valid & ((kj >= BAND) | (n > 0))
                sc = jnp.where(valid, sc, -1e30)
                m = jnp.max(sc, axis=-1, keepdims=True)
                p = jnp.exp(sc - m)
                l = jnp.sum(p, axis=-1, keepdims=True)
                o = lax.dot_general(p.astype(BF16), v, NN_DIMS, preferred_element_type=F32)
                o_ref[_every(at, sb, d), :] = o / l
                lse_ref[_every(at, sb, d), :] = jnp.broadcast_to(m + jnp.log(l), (sb, HD))

    cur, side, tok, _ = _attn_specs(g, tb, pb, s, ahead=False)
    return pl.pallas_call(
        body, name=f"attn_fwd_g{g}", grid=(N_HEADS, s // tb),
        out_shape=[jax.ShapeDtypeStruct((s, COL), F32)] * 2,
        in_specs=[cur(Q_COL), cur(K_COL), side(K_COL), cur(V_COL), side(V_COL)],
        out_specs=[tok, tok],
        scratch_shapes=[pltpu.VMEM((tb, HD), ft), pltpu.VMEM((tb + pb, HD), ft),
                        pltpu.VMEM((tb + pb, HD), ft)],
        compiler_params=_params("parallel", "arbitrary"),
    )(qkn, qkn, qkn, proj, proj)


def _attn_combine(os_, lses, tm=512):
    s = os_[0].shape[0]

    def body(o0, o1, o2, l0, l1, l2, o_ref, lse_ref):
        a, b, c = l0[...], l1[...], l2[...]
        m = jnp.maximum(jnp.maximum(a, b), c)
        ea, eb, ec = jnp.exp(a - m), jnp.exp(b - m), jnp.exp(c - m)
        tot = ea + eb + ec
        o_ref[...] = ((ea * o0[...] + eb * o1[...] + ec * o2[...]) / tot).astype(BF16)
        lse_ref[...] = m + jnp.log(tot)

    return pl.pallas_call(
        body, name="attn_combine", grid=(s // tm,),
        out_shape=[jax.ShapeDtypeStruct((s, COL), BF16), jax.ShapeDtypeStruct((s, COL), F32)],
        in_specs=[_row(tm, COL)] * 6, out_specs=[_row(tm, COL)] * 2,
        compiler_params=_params("parallel"),
    )(*os_, *lses)


def _attn_bwd(g, qkn, proj, do, lse, delta, dqn, dkn, dproj):
    s = qkn.shape[0]
    d, tb, sb, pb = _attn_shapes(s, g)
    ft = F32 if d > 1 else BF16
    nj = tb // d // sb
    nt = s // tb
    scale = HD ** -0.5
    chained = dqn is not None

    def body(k_ref, v_ref, qc_ref, qn_ref, doc_ref, don_ref, lc_ref, ln_ref, dc_ref, dn_ref, *rest):
        dq_ref, dk_ref, dv_ref, kf, vf, qf, dvf, later = rest[-8:]
        n = pl.program_id(1)
        kf[...] = k_ref[...].astype(ft)
        vf[...] = v_ref[...].astype(ft)
        qf[0:tb] = qc_ref[...].astype(ft)
        qf[tb:] = qn_ref[...].astype(ft)

        @pl.when(n == 0)
        def _():
            later[...] = jnp.zeros_like(later)

        def window(c_ref, n_ref, r, j):
            at = j * sb * d + r
            if j < nj - 1:
                return c_ref[_every(at, sb + BAND, d), :]
            return jnp.concatenate([c_ref[_every(at, sb, d), :], n_ref[_every(r, BAND, d), :]], axis=0)

        for r in range(d):
            tail = later[r]
            for j in range(nj):
                at = j * sb * d + r
                rows = _every(at, sb, d)
                k = kf[rows, :].astype(BF16)
                v = vf[rows, :].astype(BF16)
                q = qf[_every(at, sb + BAND, d), :].astype(BF16)
                dov = window(doc_ref, don_ref, r, j).astype(BF16)
                sc = lax.dot_general(q, k, NT_DIMS, preferred_element_type=F32) * scale
                qi = lax.broadcasted_iota(jnp.int32, sc.shape, 0)
                kj = lax.broadcasted_iota(jnp.int32, sc.shape, 1)
                valid = (qi >= kj) & (qi <= kj + BAND)
                if j == nj - 1:
                    valid = valid & ((qi < sb) | (n < nt - 1))
                p = jnp.exp(jnp.where(valid, sc - _lanes(window(lc_ref, ln_ref, r, j), sb), -1e30))
                dp = lax.dot_general(dov, v, NT_DIMS, preferred_element_type=F32)
                ds = (p * (dp - _lanes(window(dc_ref, dn_ref, r, j), sb)) * scale).astype(BF16)
                dvf[rows, :] = lax.dot_general(p.astype(BF16), dov, TN_DIMS, preferred_element_type=F32)
                dk_ref[rows, :] = lax.dot_general(ds, q, TN_DIMS, preferred_element_type=F32)
                dqw = lax.dot_general(ds, k, NN_DIMS, preferred_element_type=F32)
                first = dqw[:BAND] + tail
                dq_ref[rows, :] = first if sb == BAND else jnp.concatenate([first, dqw[BAND:sb]], axis=0)
                tail = dqw[sb:]
            later[r] = tail
        dv_ref[...] = dvf[...].astype(BF16)

    cur, side, tok, tok_side = _attn_specs(g, tb, pb, s, ahead=True)
    anyspec = pl.BlockSpec(memory_space=pl.ANY)
    n_heads_cols = 3 * N_HEADS * HD
    return pl.pallas_call(
        body, name=f"attn_bwd_g{g}", grid=(N_HEADS, nt),
        out_shape=[jax.ShapeDtypeStruct((s, n_heads_cols), F32), jax.ShapeDtypeStruct((s, n_heads_cols), F32),
                   jax.ShapeDtypeStruct((s, IN_W), BF16)],
        in_specs=[cur(K_COL), cur(V_COL), cur(Q_COL), side(Q_COL), tok, tok_side, tok, tok_side,
                  tok, tok_side] + ([anyspec, anyspec] if chained else []) + [anyspec],
        out_specs=[cur(0), cur(0), cur(V_COL)],
        input_output_aliases={10: 0, 11: 1, 12: 2} if chained else {10: 2},
        scratch_shapes=[pltpu.VMEM((tb, HD), ft), pltpu.VMEM((tb, HD), ft),
                        pltpu.VMEM((tb + pb, HD), ft), pltpu.VMEM((tb, HD), F32),
                        pltpu.VMEM((d, BAND, HD), F32)],
        compiler_params=_params("arbitrary", "arbitrary"),
    )(qkn, proj, qkn, qkn, do, do, lse, lse, delta, delta, *([dqn, dkn] if chained else []), dproj)


def _shift_down(x, before, k):
    rolled = pltpu.roll(x, k, 0)
    head = jnp.where(lax.broadcasted_iota(jnp.int32, before.shape, 0) < k, pltpu.roll(before, k, 0), rolled[:8])
    return jnp.concatenate([head, rolled[8:]], axis=0)


def _shift_up(x, after, k):
    rows = x.shape[0]
    rolled = pltpu.roll(x, rows - k, 0)
    tail = jnp.where(lax.broadcasted_iota(jnp.int32, after.shape, 0) >= 8 - k,
                     pltpu.roll(after, 8 - k, 0), rolled[rows - 8:])
    return jnp.concatenate([rolled[:rows - 8], tail], axis=0)


def _conv_fwd(proj, cw, tm=1024):
    s = proj.shape[0]
    r16 = tm // 16

    def body(u_ref, b_ref, c_ref, up_ref, cp_ref, w_ref, z_ref):
        i = pl.program_id(1)
        xc = c_ref[...].astype(F32) * u_ref[...].astype(F32)
        xp = jnp.where(i > 0, cp_ref[8:16, :].astype(F32) * up_ref[8:16, :].astype(F32), 0.0)
        w = w_ref[...]
        conv = _shift_down(xc, xp, 2) * w[0:1] + _shift_down(xc, xp, 1) * w[1:2] + xc * w[2:3]
        z_ref[...] = (b_ref[...].astype(F32) * conv).astype(BF16)

    tile = lambda blk: pl.BlockSpec((tm, COL), lambda j, i: (i, blk + j))
    before = lambda blk: pl.BlockSpec((16, COL), lambda j, i: (jnp.maximum(i * r16 - 1, 0), blk + j))
    return pl.pallas_call(
        body, name="conv_fwd", grid=(D // COL, s // tm),
        out_shape=jax.ShapeDtypeStruct((s, D), BF16),
        in_specs=[tile(U_BLK), tile(B_BLK), tile(C_BLK), before(U_BLK), before(C_BLK),
                  pl.BlockSpec((3, COL), lambda j, i: (0, j))],
        out_specs=pl.BlockSpec((tm, COL), lambda j, i: (i, j)),
        compiler_params=_params("parallel", "parallel"),
    )(proj, proj, proj, proj, proj, cw)


def _conv_bwd(dz, proj, cw, dproj, tm=1024):
    s = proj.shape[0]
    r16 = tm // 16
    nrow = s // tm

    def body(dz_ref, u_ref, b_ref, c_ref, up_ref, cp_ref, dzn_ref, bn_ref, w_ref, _, o_ref, dc_ref, acc_ref):
        piece, i = pl.program_id(1), pl.program_id(2)
        u, c = u_ref[...].astype(F32), c_ref[...].astype(F32)
        bv = b_ref[...].astype(F32)
        dzv = dz_ref[...]
        w = w_ref[...]

        @pl.when((piece == 0) & (i == 0))
        def _():
            acc_ref[...] = jnp.zeros_like(acc_ref)

        @pl.when(piece == 0)
        def _():
            xc = c * u
            xp = jnp.where(i > 0, cp_ref[8:16, :].astype(F32) * up_ref[8:16, :].astype(F32), 0.0)
            x2, x1 = _shift_down(xc, xp, 2), _shift_down(xc, xp, 1)
            o_ref[...] = (dzv * (x2 * w[0:1] + x1 * w[1:2] + xc * w[2:3])).astype(BF16)
            dc_ref[...] = jnp.zeros_like(dc_ref)
            dconv = dzv * bv
            acc_ref[0:1, :] += jnp.sum(dconv * x2, axis=0, keepdims=True)
            acc_ref[1:2, :] += jnp.sum(dconv * x1, axis=0, keepdims=True)
            acc_ref[2:3, :] += jnp.sum(dconv * xc, axis=0, keepdims=True)

        @pl.when(piece == 1)
        def _():
            dconv = dzv * bv
            dn = jnp.where(i < nrow - 1, dzn_ref[...] * bn_ref[0:8, :].astype(F32), 0.0)
            dxc = dconv * w[2:3] + _shift_up(dconv, dn, 1) * w[1:2] + _shift_up(dconv, dn, 2) * w[0:1]
            o_ref[...] = (dxc * c).astype(BF16)
            dc_ref[...] = (dxc * u).astype(BF16)

    tile = lambda blk: pl.BlockSpec((tm, COL), lambda j, p, i: (i, blk + j))
    before = lambda blk: pl.BlockSpec((16, COL), lambda j, p, i: (jnp.maximum(i * r16 - 1, 0), blk + j))
    after = lambda rows, blk: pl.BlockSpec(
        (rows, COL), lambda j, p, i: (jnp.minimum((i + 1) * (tm // rows), s // rows - 1), blk + j))
    return pl.pallas_call(
        body, name="conv_bwd", grid=(D // COL, 2, nrow),
        out_shape=[jax.ShapeDtypeStruct((s, IN_W), BF16), jax.ShapeDtypeStruct((s + tm, D), BF16),
                   jax.ShapeDtypeStruct((8, D), F32)],
        in_specs=[tile(0), tile(U_BLK), tile(B_BLK), tile(C_BLK), before(U_BLK), before(C_BLK),
                  after(8, 0), after(16, B_BLK), pl.BlockSpec((3, COL), lambda j, p, i: (0, j)),
                  pl.BlockSpec(memory_space=pl.ANY)],
        out_specs=[pl.BlockSpec((tm, COL), lambda j, p, i: (i, jnp.where(p == 0, B_BLK, U_BLK) + j)),
                   pl.BlockSpec((tm, COL), lambda j, p, i: (jnp.where(p == 0, nrow, i), j)),
                   pl.BlockSpec((8, COL), lambda j, p, i: (0, j))],
        input_output_aliases={9: 0},
        compiler_params=_params("arbitrary", "arbitrary", "arbitrary"),
    )(dz, proj, proj, proj, proj, proj, dz, proj, cw, dproj)


def _copy_columns(name, src, dst, blk0, tm=1024):
    s, w = dst.shape[0], src.shape[1]
    fresh = isinstance(dst, jax.ShapeDtypeStruct)

    def body(x_ref, *rest):
        rest[-1][...] = x_ref[...]

    return pl.pallas_call(
        body, name=name, grid=(w // COL, s // tm),
        out_shape=jax.ShapeDtypeStruct(dst.shape, dst.dtype),
        in_specs=[pl.BlockSpec((tm, COL), lambda j, i: (i, j))] + ([] if fresh else [pl.BlockSpec(memory_space=pl.ANY)]),
        out_specs=pl.BlockSpec((tm, COL), lambda j, i: (i, blk0 + j)),
        input_output_aliases={} if fresh else {1: 0},
        compiler_params=_params("parallel", "parallel"),
    )(src, *([] if fresh else [dst]))


def _mod_part(c_all, w_ada, b_part):
    def body(c_ref, w_ref, b_ref, o_ref):
        cv = c_ref[...]
        act = cv * _sigmoid(cv)
        o_ref[...] = jnp.dot(act, w_ref[...], preferred_element_type=F32,
                             precision=lax.Precision.HIGHEST) + b_ref[...]

    return pl.pallas_call(
        body, name="mod_part", out_shape=jax.ShapeDtypeStruct((N_DEV, w_ada.shape[1]), F32),
    )(c_all, w_ada, b_part)


def _w_ada_grad(c_all_t, dmod_part):
    def body(c_ref, d_ref, o_ref):
        cv = c_ref[...]
        act = cv * _sigmoid(cv)
        dv = d_ref[...]
        acc = act[:, 0:1] * dv[0:1, :]
        for b in range(1, N_DEV):
            acc = acc + act[:, b:b + 1] * dv[b:b + 1, :]
        o_ref[...] = acc

    return pl.pallas_call(
        body, name="w_ada_grad", out_shape=jax.ShapeDtypeStruct((D, dmod_part.shape[1]), F32),
    )(c_all_t, dmod_part)


def _sum_rows(name, v):
    def body(v_ref, o_ref):
        acc = v_ref[0]
        for k in range(1, N_DEV):
            acc = acc + v_ref[k]
        o_ref[...] = acc

    return pl.pallas_call(body, name=name, out_shape=jax.ShapeDtypeStruct(v.shape[1:], F32))(v)


def _adamw(name, w, g, m, v):
    rows, cols = w.shape
    limit = max(16, (1 << 20) // (4 * cols))
    tr = rows if rows <= limit else next((t for t in range(limit - limit % 16, 15, -16) if rows % t == 0), rows)
    c1 = 1.0 - ADAM_B1 ** ADAM_STEP
    c2 = 1.0 - ADAM_B2 ** ADAM_STEP
    parts = g.ndim == 3

    def body(w_ref, g_ref, m_ref, v_ref, go_ref, d_ref, nm_ref, nv_ref):
        if parts:
            gv = g_ref[0].astype(F32)
            for k in range(1, N_DEV):
                gv = gv + g_ref[k].astype(F32)
        else:
            gv = g_ref[...]
        go_ref[...] = gv
        nm = ADAM_B1 * m_ref[...] + (1.0 - ADAM_B1) * gv
        nv = ADAM_B2 * v_ref[...] + (1.0 - ADAM_B2) * (gv * gv)
        nm_ref[...] = nm
        nv_ref[...] = nv
        d_ref[...] = -ADAM_LR * ((nm / c1) / (jnp.sqrt(nv / c2) + ADAM_EPS) + ADAM_WD * w_ref[...])

    spec = pl.BlockSpec((tr, cols), lambda i: (i, 0))
    g_spec = pl.BlockSpec((N_DEV, tr, cols), lambda i: (0, i, 0)) if parts else spec
    return pl.pallas_call(
        body, name=name, grid=(rows // tr,),
        out_shape=[jax.ShapeDtypeStruct((rows, cols), F32)] * 4,
        in_specs=[spec, g_spec, spec, spec], out_specs=[spec] * 4,
        compiler_params=_params("parallel"),
    )(w, g, m, v)


def _adamw_small(ws, gs, ms, vs):
    n = len(ws)
    c1 = 1.0 - ADAM_B1 ** ADAM_STEP
    c2 = 1.0 - ADAM_B2 ** ADAM_STEP

    def body(*refs):
        for i in range(n):
            w_ref, g_ref, m_ref, v_ref = refs[i], refs[n + i], refs[2 * n + i], refs[3 * n + i]
            d_ref, nm_ref, nv_ref = refs[4 * n + 3 * i:4 * n + 3 * i + 3]
            gv = g_ref[...]
            nm = ADAM_B1 * m_ref[...] + (1.0 - ADAM_B1) * gv
            nv = ADAM_B2 * v_ref[...] + (1.0 - ADAM_B2) * (gv * gv)
            nm_ref[...] = nm
            nv_ref[...] = nv
            d_ref[...] = -ADAM_LR * ((nm / c1) / (jnp.sqrt(nv / c2) + ADAM_EPS) + ADAM_WD * w_ref[...])

    outs = pl.pallas_call(
        body, name="adamw_small",
        out_shape=[jax.ShapeDtypeStruct(w.shape, F32) for w in ws for _ in range(3)],
    )(*ws, *gs, *ms, *vs)
    return [tuple(outs[3 * i:3 * i + 3]) for i in range(n)]


HALF = FF // 2


def _sds(shape, dtype):
    return jax.ShapeDtypeStruct(shape, dtype)


def _row_tile(w):
    return lambda tm: ((tm, w), lambda i, j: (i, 0))


def _one(w):
    return lambda rows: ((rows, w), lambda i, j: (0, 0))


def _gate_up_swiglu(name, h, wgu, tm=512):
    s = h.shape[0]
    tm = min(tm, s)

    def epilogue(prod, first, tin, tout):
        pq_ref, s_ref = tout
        a, b = prod[:, :HALF], prod[:, HALF:]
        sig = _sigmoid(a)
        act = a * sig
        pq_ref[:, :HALF] = (b * (sig * (1.0 + a * (1.0 - sig)))).astype(BF16)
        pq_ref[:, HALF:] = act.astype(BF16)
        s_ref[...] = (act * b).astype(BF16)

    return _mm(name, h, wgu, "NT", None, tm, FF, D, n_outer=True, epilogue=epilogue,
               tiles_out=[(_sds((s, 2 * FF), BF16), (tm, FF), lambda i, j: (i, j)),
                          (_sds((s, FF), BF16), (tm, HALF), lambda i, j: (i, j))])


def _d_hidden_swiglu(name, df, wd, ab, after=(), tm=512):
    s = df.shape[0]
    tm = min(tm, s)

    def epilogue(prod, first, tin, tout, cols):
        da_cols = slice(cols[0], cols[0] + cols[1])
        db_cols = slice(HALF + cols[0], HALF + cols[0] + cols[1])
        tout[0][:, da_cols] = (prod * tin[0][:, da_cols].astype(F32)).astype(BF16)
        tout[0][:, db_cols] = (prod * tin[0][:, db_cols].astype(F32)).astype(BF16)

    chunks = [(c0, min(384, HALF - c0)) for c0 in range(0, HALF, 384)]
    return _mm(name, df, wd, "NT", None, tm, HALF, D, n_outer=True, epilogue=epilogue, col_chunks=chunks, after=after,
               tiles_in=[(ab, (tm, FF), lambda i, j: (i, j))],
               tiles_out=[(_sds((s, 2 * FF), BF16), (tm, FF), lambda i, j: (i, j))])[0]


def _out_residual(name, a, w, x, gt, coef, nxt, tm=512, tk=FF):
    s = a.shape[0]
    tm = min(tm, s)

    def epilogue(prod, first, tin, tout):
        x_ref, gt_ref, g_ref, sc_ref, sh_ref = tin
        f_ref, xn_ref, h_ref = tout
        f_ref[...] = prod
        xn = x_ref[...] + (coef * gt_ref[...]) * prod
        xn_ref[...] = xn
        r = lax.rsqrt(jnp.mean(xn * xn, axis=-1, keepdims=True) + EPS)
        h_ref[...] = ((xn * r) * g_ref[...] * (1.0 + sc_ref[...]) + sh_ref[...]).astype(BF16)

    row, vec = _row_tile(D)(tm), _one(D)(1)
    return _mm(name, a, w, "NN", None, tm, D, tk, epilogue=epilogue,
               tiles_in=[(x, *row), (gt, *vec)] + [(v, *vec) for v in nxt],
               tiles_out=[(_sds((s, D), F32), *row), (_sds((s, D), F32), *row), (_sds((s, D), BF16), *row)])


def _out_loss(name, a, w, x, gt, coef, target, tm=512):
    s = a.shape[0]
    tm = min(tm, s)

    def epilogue(prod, first, tin, tout):
        x_ref, gt_ref, t_ref = tin
        f_ref, g_ref, df_ref, acc_ref = tout
        f_ref[...] = prod
        cg = coef * gt_ref[...]
        e = x_ref[...] + cg * prod - t_ref[...]
        gv = e * (1.0 / D)
        g_ref[...] = gv
        df_ref[...] = (cg * gv).astype(BF16)

        @pl.when(first)
        def _():
            acc_ref[...] = jnp.zeros_like(acc_ref)

        acc_ref[0:1, :] += coef * jnp.sum(gv * prod, axis=0, keepdims=True)
        acc_ref[1:2, :] += (0.5 / D) * jnp.sum(e * e, axis=0, keepdims=True)

    row, vec = _row_tile(D)(tm), _one(D)(1)
    return _mm(name, a, w, "NN", None, tm, D, FF, epilogue=epilogue,
               tiles_in=[(x, *row), (gt, *vec), (target, *row)],
               tiles_out=[(_sds((s, D), F32), *row), (_sds((s, D), F32), *row), (_sds((s, D), BF16), *row),
                          (_sds((8, D), F32), *_one(D)(8))])


def _d_h_norm_bwd(name, da, w, x, gin, g, sc, sh, before=None, after=(), tm=256):
    s = da.shape[0]
    tm = min(tm, s)
    coef = before[2] if before else None

    def epilogue(prod, first, tin, tout):
        x_ref, gin_ref, g_ref, sc_ref, sh_ref = tin[:5]
        gout_ref, acc_ref = tout[:2]
        xv = x_ref[...]
        r = lax.rsqrt(jnp.mean(xv * xv, axis=-1, keepdims=True) + EPS)
        nv = xv * r
        gv, one_sc = g_ref[...], 1.0 + sc_ref[...]
        dn = prod * gv * one_sc
        gout = gin_ref[...] + r * (dn - nv * jnp.mean(dn * nv, axis=-1, keepdims=True))
        gout_ref[...] = gout

        @pl.when(first)
        def _():
            acc_ref[...] = jnp.zeros_like(acc_ref)

        dhn = prod * nv
        acc_ref[0:1, :] += jnp.sum(prod, axis=0, keepdims=True)
        acc_ref[1:2, :] += jnp.sum(dhn * gv, axis=0, keepdims=True)
        acc_ref[2:3, :] += jnp.sum(dhn * one_sc, axis=0, keepdims=True)
        if before:
            f_ref, gt_ref = tin[5:]
            tout[2][...] = ((coef * gt_ref[...]) * gout).astype(BF16)
            acc_ref[3:4, :] += coef * jnp.sum(gout * f_ref[...], axis=0, keepdims=True)

    row, vec = _row_tile(D)(tm), _one(D)(1)
    tiles_in = [(x, *row), (gin, *row), (g, *vec), (sc, *vec), (sh, *vec)]
    tiles_out = [(_sds((s, D), F32), *row), (_sds((8, D), F32), *_one(D)(8))]
    if before:
        tiles_in += [(before[0], *row), (before[1], *vec)]
        tiles_out.append((_sds((s, D), BF16), *row))
    return _mm(name, da, w, "NN", None, tm, D, da.shape[1], epilogue=epilogue, keep_b=True, after=after,
               tiles_in=tiles_in, tiles_out=tiles_out)


def _gate_tiles(proj, tm):
    return [(proj, (tm, COL), (lambda i, j, blk=blk: (i, blk))) for blk in (GA_BLK, GA_BLK + 1, GC_BLK, GC_BLK + 1)]


def _conv_branch_merge(z, wc, ya, proj, tm=1024):
    s = z.shape[0]
    tm = min(tm, s)

    def epilogue(prod, first, tin, tout):
        ya_ref, ga0, ga1, gc0, gc1 = tin
        tout[0][...] = prod.astype(BF16)
        for half, (ga, gc) in enumerate(((ga0, gc0), (ga1, gc1))):
            cols = slice(half * COL, (half + 1) * COL)
            tout[1][:, cols] = (_sigmoid(ga[...].astype(F32)) * ya_ref[:, cols].astype(F32)
                                + _sigmoid(gc[...].astype(F32)) * prod[:, cols]).astype(BF16)

    row = _row_tile(D)(tm)
    return _mm("mix_conv_branch", z, wc, "NN", None, tm, D, D, epilogue=epilogue,
               tiles_in=[(ya, *row)] + _gate_tiles(proj, tm),
               tiles_out=[(_sds((s, D), BF16), *row), (_sds((s, D), BF16), *row)])


def _d_merged_branches(dmix, wo, ya, yc, proj, tm=1024):
    s = dmix.shape[0]
    tm = min(tm, s)

    def epilogue(prod, first, tin, tout):
        ya_ref, yc_ref, ga0, ga1, gc0, gc1 = tin
        dya_ref, dyc_ref, dg_ref = tout
        for half, (ga, gc) in enumerate(((ga0, gc0), (ga1, gc1))):
            cols = slice(half * COL, (half + 1) * COL)
            dm = prod[:, cols]
            for y_ref, g_ref, dy_ref, off in ((ya_ref, ga, dya_ref, 0), (yc_ref, gc, dyc_ref, D)):
                sig = _sigmoid(g_ref[...].astype(F32))
                dms = dm * sig
                dy_ref[:, cols] = dms.astype(BF16)
                dg_ref[:, off + half * COL:off + (half + 1) * COL] = (
                    dms * y_ref[:, cols].astype(F32) * (1.0 - sig)).astype(BF16)

    row = _row_tile(D)(tm)
    return _mm("mix_d_merged", dmix, wo, "NT", None, tm, D, D, epilogue=epilogue,
               tiles_in=[(ya, *row), (yc, *row)] + _gate_tiles(proj, tm),
               tiles_out=[(_sds((s, D), BF16), *row), (_sds((s, D), BF16), *row),
                          (_sds((s, 2 * D), BF16), *_row_tile(2 * D)(tm))])


def _d_o_delta(dya, wa_t, o, tm=1024):
    s = dya.shape[0]
    tm = min(tm, s)

    def epilogue(prod, first, tin, tout):
        tout[0][...] = prod
        tout[1][...] = _heads(prod * tin[0][...].astype(F32), lambda ph, h: jnp.broadcast_to(
            jnp.sum(ph, axis=-1, keepdims=True), ph.shape))

    row = _row_tile(COL)(tm)
    return _mm("mix_d_o", dya, wa_t, "NN", None, tm, COL, D, epilogue=epilogue,
               tiles_in=[(o, *row)], tiles_out=[(_sds((s, COL), F32), *row), (_sds((s, COL), F32), *row)])


def _ffn_bwd(tag, df, x, gin, h, ab, sw, g, sc, sh, wgu, wd, before=None, tk_dw=2048):
    dwd = _mm(f"{tag}_dw_down", sw, df, "TN", BF16, HALF, D, tk_dw)
    dab = _d_hidden_swiglu(f"{tag}_d_hidden", df, wd, ab, after=[dwd])
    dwgu = _mm(f"{tag}_dw_gate_up", dab, h, "TN", BF16, HALF, D, tk_dw)
    res = _d_h_norm_bwd(f"{tag}_d_h", dab, wgu, x, gin, g, sc, sh, before=before, after=[dwgu], tm=512)
    return res, dwgu, dwd


def kernel(x, c, w_ada, b_ada, norm_ffn1, ffn1_w_gate, ffn1_w_up, ffn1_w_down, norm_mix, w_in, q_norm, k_norm, conv_w, w_attn_branch, w_conv_branch, w_out, norm_ffn2, ffn2_w_gate, ffn2_w_up, ffn2_w_down, loss_target, m_w_ada, m_b_ada, m_norm_ffn1, m_ffn1_w_gate, m_ffn1_w_up, m_ffn1_w_down, m_norm_mix, m_w_in, m_q_norm, m_k_norm, m_conv_w, m_w_attn_branch, m_w_conv_branch, m_w_out, m_norm_ffn2, m_ffn2_w_gate, m_ffn2_w_up, m_ffn2_w_down, v_w_ada, v_b_ada, v_norm_ffn1, v_ffn1_w_gate, v_ffn1_w_up, v_ffn1_w_down, v_norm_mix, v_w_in, v_q_norm, v_k_norm, v_conv_w, v_w_attn_branch, v_w_conv_branch, v_w_out, v_norm_ffn2, v_ffn2_w_gate, v_ffn2_w_up, v_ffn2_w_down):
    me = 4 * lax.axis_index("x") + 2 * lax.axis_index("y") + lax.axis_index("c")
    x0, target = x[0], loss_target[0]
    s = x0.shape[0]
    ada_cols = w_ada.shape[2]
    cw_cols = conv_w.shape[2]

    gathered = _small_allgather(
        "gather_c_conv", jnp.concatenate([c, conv_w[0].reshape(1, 3 * cw_cols)], axis=1))[:, 0]
    c_all = gathered[:, :D]
    cw = gathered[:, D:].reshape(N_DEV, 3, cw_cols).transpose(1, 0, 2).reshape(3, D)
    b_part = lax.dynamic_slice(b_ada, (0, me * ada_cols), (1, ada_cols))
    mod_part = _mod_part(c_all, w_ada[0], b_part)
    mod_all = _small_allgather("gather_mod", mod_part.reshape(1, N_DEV * ada_cols))
    mod = lax.dynamic_slice(mod_all.reshape(N_DEV, N_DEV, ada_cols), (0, me, 0), (N_DEV, 1, ada_cols))
    mod = mod.reshape(N_MOD, 1, D)
    sh1, sc1, gt1, sh2, sc2, gt2, sh3, sc3, gt3 = [mod[i] for i in range(N_MOD)]

    tb = lambda w: w[0].T.astype(BF16)
    nb = lambda w: w[0].astype(BF16)
    ffn1_shards = [tb(ffn1_w_gate), tb(ffn1_w_up), nb(ffn1_w_down)]
    ffn2_shards = [tb(ffn2_w_gate), tb(ffn2_w_up), nb(ffn2_w_down)]
    mix_shards = [tb(w_in), tb(w_attn_branch), nb(w_conv_branch), nb(w_out)]
    ffn_dst, ffn_base, ffn_jump, ffn_shapes = [0, 0, 1], [0, HALF, 0], [HALF, HALF, 0], [(2 * FF, D), (FF, D)]
    mix_dst, mix_base, mix_shapes = [0, 1, 2, 3], [0, 0, 0, 0], [(IN_W, D), (D, COL), (D, D), (D, D)]
    (wgu1,) = _run_plan_on_sequencer(
        "gather_ffn1_gate_up", _gather_plan(ffn1_shards[:2], ffn_dst[:2], ffn_base[:2], ffn_shapes[:1], ffn_jump[:2]), 1)
    (wd1,) = _run_plan_on_sequencer(
        "gather_ffn1_down", _gather_plan(ffn1_shards[2:], [0], [0], ffn_shapes[1:]), 8)
    win_t, wa_t, wc, wo = _run_plan_on_sequencer(
        "gather_mix_weights", _gather_plan(mix_shards, mix_dst, mix_base, mix_shapes), 2)
    wgu2, wd2 = _run_plan_on_sequencer(
        "gather_ffn2_weights", _gather_plan(ffn2_shards, ffn_dst, ffn_base, ffn_shapes, ffn_jump), 3)

    h1 = _normmod("ffn1_normmod", x0, norm_ffn1, sc1, sh1)
    ab1, s1 = _gate_up_swiglu("ffn1_gate_up", h1, wgu1)
    f1, x1, h2 = _out_residual("ffn1_down", s1, wd1, x0, gt1, 0.5, (norm_mix, sc2, sh2))
    proj = _mm("mix_in_proj", h2, win_t, "NT", BF16, 1024, IN_W // 4, D, n_outer=True)
    wqk = jnp.concatenate([jnp.tile(q_norm, (1, 12)), jnp.tile(k_norm, (1, 12))], axis=1)
    qkn = _qknorm(proj, wqk)
    group_out = [_attn_fwd(g, qkn, proj) for g in range(3)]
    o, lse = _attn_combine([go[0] for go in group_out], [go[1] for go in group_out])
    ya = _mm("mix_attn_branch", o, wa_t, "NT", BF16, 1024, 1024, COL)
    z = _conv_fwd(proj, cw)
    yc, merged = _conv_branch_merge(z, wc, ya, proj)
    mix, x2, h3 = _out_residual("mix_out_proj", merged, wo, x1, gt2, 1.0, (norm_ffn2, sc3, sh3), tm=1024, tk=D)
    ab3, s3 = _gate_up_swiglu("ffn2_gate_up", h3, wgu2)
    f3, g3, df3, acc_out = _out_loss("ffn2_down", s3, wd2, x2, gt3, 0.5, target)
    loss_part = jnp.sum(acc_out[1])

    ffn_rows = [sh_.shape[0] for sh_ in ffn1_shards]
    mix_rows = [sh_.shape[0] for sh_ in mix_shards]
    (g2, acc3, dmix), dwgu2, dwd2 = _ffn_bwd(
        "ffn2", df3, x2, g3, h3, ab3, s3, norm_ffn2, sc3, sh3, wgu2, wd2, before=(mix, gt2, 1.0))
    dya, dyc, dgates = _d_merged_branches(dmix, wo, ya, yc, proj)
    dwo = _mm("mix_dw_out", merged, dmix, "TN", BF16, 1024, 1024, 2048)
    dproj = _copy_columns("dproj_gates", dgates, jax.ShapeDtypeStruct((s, IN_W), BF16), GA_BLK)
    dwc = _mm("mix_dw_conv_branch", z, dyc, "TN", BF16, 1024, 1024, 2048)
    dz = _mm("mix_d_z", dyc, wc, "NT", F32, 1024, 1024, D)
    dproj, d_c, cw_acc = _conv_bwd(dz, proj, cw, dproj)
    dproj = _copy_columns("copy_d_c", d_c, dproj, C_BLK)
    dwa_t = _mm("mix_dw_attn_branch", dya, o, "TN", BF16, 1024, COL, 2048)
    do, delta = _d_o_delta(dya, wa_t, o)
    dqn = dkn = None
    for g in range(3):
        dqn, dkn, dproj = _attn_bwd(g, qkn, proj, do, lse, delta, dqn, dkn, dproj)
    dproj, wq_acc = _qknorm_bwd("qnorm_bwd", proj, dqn, wqk[:, :QKW // 2], dproj, 0)
    dproj, wk_acc = _qknorm_bwd("knorm_bwd", proj, dkn, wqk[:, QKW // 2:], dproj, QKW // 2 // COL)
    r_f2g, r_f2u, r_f2d, r_wa, r_wc, r_wo = _run_plan_on_sequencer(
        "scatter_ffn2_and_branch_grads",
        _scatter_plan([dwgu2, dwd2, dwa_t, dwc, dwo], [0, 0, 1, 2, 3, 4], [0, HALF, 0, 0, 0, 0],
                      ffn_rows + mix_rows[1:], [D, D, D, COL, D, D], [HALF, HALF, 0, 0, 0, 0]), 4)
    dwin_t = _mm("mix_dw_in", dproj, h2, "TN", BF16, IN_W // 4, COL, 2048)
    (r_win,) = _run_plan_on_sequencer(
        "scatter_w_in_grad", _scatter_plan([dwin_t], [0], [0], mix_rows[:1], [D]), 5)
    g1, acc2, df1 = _d_h_norm_bwd("mix_d_h", dproj, win_t, x1, g2, norm_mix, sc2, sh2, before=(f1, gt1, 0.5),
                                  after=[dwin_t])
    dwd1 = _mm("ffn1_dw_down", s1, df1, "TN", BF16, HALF, D, 2048)
    (r_f1d,) = _run_plan_on_sequencer(
        "scatter_ffn1_down_grad", _scatter_plan([dwd1], [0], [0], ffn_rows[2:], [D]), 6)
    dab1 = _d_hidden_swiglu("ffn1_d_hidden", df1, wd1, ab1, after=[dwd1, r_win])
    dwgu1 = _mm("ffn1_dw_gate_up", dab1, h1, "TN", BF16, HALF, D, 2048)
    r_f1g, r_f1u = _run_plan_on_sequencer(
        "scatter_ffn1_gate_up_grads",
        _scatter_plan([dwgu1], [0, 0], [0, HALF], ffn_rows[:2], [D, D], [HALF, HALF]), 7)
    g0, acc1 = _d_h_norm_bwd("ffn1_d_h", dab1, wgu1, x0, g1, norm_ffn1, sc1, sh1, after=[dwgu1, r_f1d], tm=512)

    dqw = jnp.sum(wq_acc[0].reshape(12, HD), axis=0)
    dkw = jnp.sum(wk_acc[0].reshape(12, HD), axis=0)
    small = jnp.concatenate([
        acc1[0], acc1[1], acc2[3], acc2[0], acc2[1], acc3[3], acc3[0], acc3[1], acc_out[0],
        acc1[2], acc2[2], acc3[2], dqw, dkw, cw_acc[0:3].reshape(3 * D),
        jnp.zeros((HD,), F32).at[0].set(loss_part)]).reshape(1, -1)
    small_all = _small_allgather("gather_small_grads", small)
    small_sum = _sum_rows("sum_small_grads", small_all)[0]
    n_mod = N_MOD * D
    g_b_ada = small_sum[:n_mod].reshape(1, n_mod)
    g_norm1, g_norm2, g_norm3 = [small_sum[n_mod + i * D:n_mod + (i + 1) * D].reshape(1, D) for i in range(3)]
    off = n_mod + 3 * D
    g_qn, g_kn = small_sum[off:off + HD].reshape(1, HD), small_sum[off + HD:off + 2 * HD].reshape(1, HD)
    g_cw_full = small_sum[off + 2 * HD:off + 2 * HD + 3 * D].reshape(3, D)
    loss = small_sum[off + 2 * HD + 3 * D]
    g_cw = lax.dynamic_slice(g_cw_full, (0, me * cw_cols), (3, cw_cols))
    dmod_part = lax.dynamic_slice(small_all[:, 0, :n_mod], (0, me * ada_cols), (N_DEV, ada_cols))
    g_w_ada = _w_ada_grad(c_all.T, dmod_part)

    as_rows = {"ffn1_w_gate", "ffn1_w_up", "w_in", "w_attn_branch", "ffn2_w_gate", "ffn2_w_up"}
    grad_list = [g_w_ada, g_b_ada, g_norm1, r_f1g, r_f1u, r_f1d, g_norm2, r_win,
                 g_qn, g_kn, g_cw, r_wa, r_wc, r_wo, g_norm3, r_f2g, r_f2u, r_f2d]
    weights = [w_ada, b_ada, norm_ffn1, ffn1_w_gate, ffn1_w_up, ffn1_w_down, norm_mix, w_in, q_norm, k_norm,
               conv_w, w_attn_branch, w_conv_branch, w_out, norm_ffn2, ffn2_w_gate, ffn2_w_up, ffn2_w_down]
    ms = [m_w_ada, m_b_ada, m_norm_ffn1, m_ffn1_w_gate, m_ffn1_w_up, m_ffn1_w_down, m_norm_mix, m_w_in, m_q_norm,
          m_k_norm, m_conv_w, m_w_attn_branch, m_w_conv_branch, m_w_out, m_norm_ffn2, m_ffn2_w_gate,
          m_ffn2_w_up, m_ffn2_w_down]
    vs = [v_w_ada, v_b_ada, v_norm_ffn1, v_ffn1_w_gate, v_ffn1_w_up, v_ffn1_w_down, v_norm_mix, v_w_in, v_q_norm,
          v_k_norm, v_conv_w, v_w_attn_branch, v_w_conv_branch, v_w_out, v_norm_ffn2, v_ffn2_w_gate,
          v_ffn2_w_up, v_ffn2_w_down]
    wnames = ["w_ada", "b_ada", "norm_ffn1", "ffn1_w_gate", "ffn1_w_up", "ffn1_w_down", "norm_mix", "w_in",
              "q_norm", "k_norm", "conv_w", "w_attn_branch", "w_conv_branch", "w_out", "norm_ffn2",
              "ffn2_w_gate", "ffn2_w_up", "ffn2_w_down"]
    small = [i for i, gr in enumerate(grad_list) if gr.ndim == 2 and gr.size <= 16384]
    flat = lambda a, i: a.reshape(-1, weights[i].shape[-1])
    small_res = dict(zip(small, _adamw_small(
        [flat(weights[i], i) for i in small], [flat(grad_list[i], i) for i in small],
        [flat(ms[i], i) for i in small], [flat(vs[i], i) for i in small])))
    grad_out, deltas, new_ms, new_vs = [], [], [], []
    for idx, (nm, w, gr, m_, v_) in enumerate(zip(wnames, weights, grad_list, ms, vs)):
        if idx in small_res:
            gr, dl, nm_, nv_ = [r.reshape(w.shape) for r in (gr, *small_res[idx])]
        elif nm in as_rows:
            res = _adamw(f"adamw_{nm}", w[0].T, gr, m_[0].T, v_[0].T)
            gr, dl, nm_, nv_ = [r.T[None] for r in res]
        else:
            two_d = (-1, w.shape[-1])
            res = _adamw(f"adamw_{nm}", w.reshape(two_d), gr if gr.ndim == 3 else gr.reshape(two_d),
                         m_.reshape(two_d), v_.reshape(two_d))
            gr, dl, nm_, nv_ = [r.reshape(w.shape) for r in res]
        grad_out.append(gr)
        deltas.append(dl)
        new_ms.append(nm_)
        new_vs.append(nv_)
    return (loss, g0[None], *grad_out, *deltas, *new_ms, *new_vs)
```

```python
import jax
import jax.numpy as jnp
from jax import lax
from jax.experimental import pallas as pl
from jax.experimental.pallas import tpu as pltpu
from jax.experimental.pallas import tpu_sc as plsc

F32 = jnp.float32
BF16 = jnp.bfloat16
MESH = pl.DeviceIdType.MESH

N_DEV = 8
D = 1024
FF = 2816
HD = 128
N_HEADS = 4
DILATIONS = (1, 4, 16)
BAND = 128
QKW = 2 * 3 * N_HEADS * HD
IN_W = 9728
COL = 512
V_BLK, U_BLK, B_BLK, C_BLK, GA_BLK, GC_BLK = 6, 9, 11, 13, 15, 17
EPS = 1e-6
N_MOD = 9
ADAM_LR, ADAM_B1, ADAM_B2, ADAM_EPS, ADAM_WD, ADAM_STEP = 0.001, 0.9, 0.999, 1e-08, 0.01, 10

NT_DIMS = (((1,), (1,)), ((), ()))
TN_DIMS = (((0,), (0,)), ((), ()))
NN_DIMS = (((1,), (0,)), ((), ()))


def _place():
    return lax.axis_index("x"), lax.axis_index("y"), lax.axis_index("c")


def _flip(coord, bit):
    return 1 - coord if bit else coord


def _params(*sem):
    return pltpu.CompilerParams(dimension_semantics=sem)


def _small_allgather(name, v):
    n = v.shape[-1]

    def body(v_ref, out_ref, send_sems, recv_sems):
        x, y, c = _place()
        me = 4 * x + 2 * y + c
        out_ref[me] = v_ref[...]
        copies = []
        for k in range(1, N_DEV):
            peer = (_flip(x, (k >> 2) & 1), _flip(y, (k >> 1) & 1), _flip(c, k & 1))
            cp = pltpu.make_async_remote_copy(
                src_ref=v_ref, dst_ref=out_ref.at[me], send_sem=send_sems.at[k - 1],
                recv_sem=recv_sems.at[k - 1], device_id=peer, device_id_type=MESH)
            cp.start()
            copies.append(cp)
        for cp in copies:
            cp.wait()

    return pl.pallas_call(
        body, name=name,
        out_shape=jax.ShapeDtypeStruct((N_DEV, 1, n), F32),
        in_specs=[pl.BlockSpec(memory_space=pltpu.VMEM)],
        out_specs=pl.BlockSpec(memory_space=pltpu.VMEM),
        scratch_shapes=[pltpu.SemaphoreType.DMA((N_DEV - 1,)), pltpu.SemaphoreType.DMA((N_DEV - 1,))],
    )(v)


class _Plan:
    def __init__(self, operands, out_shapes, sems, phases):
        self.operands, self.out_shapes, self.sems, self.phases = operands, out_shapes, sems, phases


def _slab_start(base, rows, jump, idx):
    return pl.multiple_of(base + idx * rows + (idx // 4) * jump, 16)


def _gather_plan(shards, dst_of, base_of, dst_shapes, jump_of=None):
    n = len(shards)
    rows = [s.shape[0] for s in shards]
    jump_of = jump_of or [0] * n

    def phases(srcs, dsts, sems):
        send_sems, recv_sems, local_sems = sems
        x, y, c = _place()
        me, sibling = (x, y, c), (x, y, 1 - c)
        chips = [(1 - x, y), (x, 1 - y), (1 - x, 1 - y)]

        def slab(i, px, py, pc):
            start = _slab_start(base_of[i], rows[i], jump_of[i], 4 * px + 2 * py + pc)
            return dsts[dst_of[i]].at[pl.ds(start, rows[i])]

        def copy(i, k, block, to, src=None):
            return pltpu.make_async_remote_copy(
                src_ref=slab(i, *block) if src is None else src, dst_ref=slab(i, *block),
                send_sem=send_sems.at[i, k], recv_sem=recv_sems.at[i, k],
                device_id=to, device_id_type=MESH)

        def mine():
            return [pltpu.make_async_copy(srcs[i], slab(i, *me), local_sems.at[i]) for i in range(n)]

        def first():
            out = []
            for i in range(n):
                out.append(copy(i, 0, me, sibling, src=srcs[i]))
                out += [copy(i, 1 + j, me, (*chip, c), src=srcs[i]) for j, chip in enumerate(chips)]
            return out

        def passed():
            return [(copy(i, 1 + j, (*chip, c), me), copy(i, 4 + j, (*chip, c), sibling))
                    for j, chip in enumerate(chips) for i in range(n)]

        def start():
            for cp in mine() + first():
                cp.start()

        def middle():
            for landed, onward in passed():
                landed.wait_recv()
                onward.start()

        def finish():
            for i in range(n):
                copy(i, 0, sibling, me).wait_recv()
                for j, chip in enumerate(chips):
                    copy(i, 4 + j, (*chip, 1 - c), me).wait_recv()
            for cp in first() + [onward for _, onward in passed()]:
                cp.wait_send()
            for cp in mine():
                cp.wait()

        return start, middle, finish

    sems = [pltpu.SemaphoreType.DMA((n, 7)), pltpu.SemaphoreType.DMA((n, 7)), pltpu.SemaphoreType.DMA((n,))]
    return _Plan(list(shards), [jax.ShapeDtypeStruct(s, BF16) for s in dst_shapes], sems, phases)


def _scatter_plan(grads, src_of, base_of, rows, cols, jump_of=None):
    n = len(rows)
    jump_of = jump_of or [0] * n

    def phases(srcs, recvs, sems):
        send_sems, recv_sems, local_sems = sems
        x, y, c = _place()
        me = 4 * x + 2 * y + c

        def slab(i, idx):
            start = _slab_start(base_of[i], rows[i], jump_of[i], idx)
            return srcs[src_of[i]].at[pl.ds(start, rows[i])]

        def copies():
            out = [pltpu.make_async_copy(slab(i, me), recvs[i].at[me], local_sems.at[i]) for i in range(n)]
            for k in range(1, N_DEV):
                px, py, pc = _flip(x, (k >> 2) & 1), _flip(y, (k >> 1) & 1), _flip(c, k & 1)
                out += [pltpu.make_async_remote_copy(
                    src_ref=slab(i, 4 * px + 2 * py + pc), dst_ref=recvs[i].at[me],
                    send_sem=send_sems.at[i, k - 1], recv_sem=recv_sems.at[i, k - 1],
                    device_id=(px, py, pc), device_id_type=MESH) for i in range(n)]
            return out

        def start():
            for cp in copies():
                cp.start()

        def finish():
            for cp in copies():
                cp.wait()

        return start, None, finish

    sems = [pltpu.SemaphoreType.DMA((n, 7)), pltpu.SemaphoreType.DMA((n, 7)), pltpu.SemaphoreType.DMA((n,))]
    out_shapes = [jax.ShapeDtypeStruct((N_DEV, rows[i], cols[i]), BF16) for i in range(n)]
    return _Plan(list(grads), out_shapes, sems, phases)


def _run_plan_on_sequencer(name, plan, collective_id):
    src_refs = [jax.new_ref(a, memory_space=pltpu.MemorySpace.HBM) for a in plan.operands]
    dst_refs = [jax.empty_ref(s, memory_space=pltpu.MemorySpace.HBM) for s in plan.out_shapes]

    @pl.kernel(mesh=plsc.ScalarSubcoreMesh(axis_name="sequencer", num_cores=1), name=name,
               scratch_types=tuple(plan.sems),
               compiler_params=pltpu.CompilerParams(collective_id=collective_id))
    def launch(*sems):
        x, y, c = _place()
        barrier = pltpu.get_barrier_semaphore()
        for k in range(1, N_DEV):
            peer = (_flip(x, (k >> 2) & 1), _flip(y, (k >> 1) & 1), _flip(c, k & 1))
            pl.semaphore_signal(barrier, inc=1, device_id=peer, device_id_type=MESH)
        pl.semaphore_wait(barrier, N_DEV - 1)
        for phase in plan.phases(src_refs, dst_refs, sems):
            if phase is not None:
                phase()

    launch()
    return [r[...] for r in dst_refs]


def _mm(name, a, b, mode, out_dtype, tm, tn, tk, *, tiles_in=(), tiles_out=(), epilogue=None,
        n_outer=False, keep_b=False, col_chunks=None, after=()):
    if mode == "TN":
        kk, m = a.shape
    else:
        m, kk = a.shape
    n = b.shape[0] if mode == "NT" else b.shape[1]
    tm, tn, tk = min(tm, m), min(tn, n), min(tk, kk)
    assert m % tm == 0 and n % tn == 0 and kk % tk == 0, (name, m, n, kk, tm, tn, tk)
    ni, nj, nk = m // tm, n // tn, kk // tk
    dims = {"NN": NN_DIMS, "NT": NT_DIMS, "TN": TN_DIMS}[mode]
    if epilogue is None:
        tiles_out = [(jax.ShapeDtypeStruct((m, n), out_dtype), (tm, tn), lambda i, j: (i, j))]
    n_tin, n_tout = len(tiles_in), len(tiles_out)
    n_acc = 1 if nk > 1 else 0
    n_after = len(after)
    assert not keep_b or (nk == 1 and nj == 1)
    assert not col_chunks or (epilogue is not None and nk == 1 and mode != "TN")
    ij = (lambda p, q: (q, p)) if n_outer else (lambda p, q: (p, q))
    inner = ni if n_outer else nj

    def body(a_ref, b_ref, *rest):
        tin = rest[:n_tin]
        tout = rest[n_tin + n_after:n_tin + n_after + n_tout]
        scratch = rest[n_tin + n_after + n_tout:]
        k = pl.program_id(2)
        visit = pl.program_id(0) * inner + pl.program_id(1)
        if keep_b:
            b_kept, b_sem = scratch[n_acc:n_acc + 2]

            @pl.when((visit == 0) & (k == 0))
            def _():
                cp = pltpu.make_async_copy(b_ref, b_kept, b_sem)
                cp.start()
                cp.wait()

            b_ref = b_kept

        def store(prod, c=0, cols=()):
            if epilogue is None:
                tout[0][...] = prod.astype(out_dtype)
            else:
                epilogue(prod, jnp.logical_and(visit == 0, c == 0), tin, tout, *cols)

        if col_chunks:
            for c, (c0, cw) in enumerate(col_chunks):
                b_part = b_ref[pl.ds(c0, cw), :] if mode == "NT" else b_ref[:, pl.ds(c0, cw)]
                store(lax.dot_general(a_ref[...], b_part, dims, preferred_element_type=F32), c, ((c0, cw),))
        else:
            part = lax.dot_general(a_ref[...], b_ref[...], dims, preferred_element_type=F32)
            if nk == 1:
                store(part)
            else:
                acc_ref = scratch[0]

                @pl.when(k == 0)
                def _():
                    acc_ref[...] = part

                @pl.when((k > 0) & (k < nk - 1))
                def _():
                    acc_ref[...] += part

                @pl.when(k == nk - 1)
                def _():
                    store(acc_ref[...] + part)

    def spec(shape, fn):
        return pl.BlockSpec(shape, lambda p, q, k: fn(*ij(p, q)))

    a_spec = (pl.BlockSpec((tk, tm), lambda p, q, k: (k, ij(p, q)[0])) if mode == "TN"
              else pl.BlockSpec((tm, tk), lambda p, q, k: (ij(p, q)[0], k)))
    if keep_b:
        b_spec = pl.BlockSpec(memory_space=pl.ANY)
    elif mode == "NT":
        b_spec = pl.BlockSpec((tn, tk), lambda p, q, k: (ij(p, q)[1], k))
    else:
        b_spec = pl.BlockSpec((tk, tn), lambda p, q, k: (k, ij(p, q)[1]))
    sequential = epilogue or keep_b
    out = pl.pallas_call(
        body, name=name, grid=(nj, ni, nk) if n_outer else (ni, nj, nk),
        out_shape=[t[0] for t in tiles_out],
        in_specs=([a_spec, b_spec] + [spec(t[1], t[2]) for t in tiles_in]
                  + [pl.BlockSpec(memory_space=pl.ANY)] * n_after),
        out_specs=[spec(t[1], t[2]) for t in tiles_out],
        scratch_shapes=([pltpu.VMEM((tm, tn), F32)] * n_acc
                        + ([pltpu.VMEM(b.shape, b.dtype), pltpu.SemaphoreType.DMA] if keep_b else [])),
        compiler_params=(_params("arbitrary", "arbitrary", "arbitrary") if sequential
                         else _params("parallel", "parallel", "arbitrary")),
    )(a, b, *[t[0] for t in tiles_in], *after)
    return out if epilogue else out[0]


def _row(tm, w, off=0):
    return pl.BlockSpec((tm, w), lambda i: (i, off))


def _vec(w):
    return pl.BlockSpec((1, w), lambda i: (0, 0))


def _sigmoid(x):
    return 0.5 * jnp.tanh(0.5 * x) + 0.5


def _normmod(name, x, g, sc, sh, tm=512):
    s = x.shape[0]

    def body(x_ref, g_ref, sc_ref, sh_ref, h_ref):
        xv = x_ref[...]
        r = lax.rsqrt(jnp.mean(xv * xv, axis=-1, keepdims=True) + EPS)
        h_ref[...] = ((xv * r) * g_ref[...] * (1.0 + sc_ref[...]) + sh_ref[...]).astype(BF16)

    return pl.pallas_call(
        body, name=name, grid=(s // tm,),
        out_shape=jax.ShapeDtypeStruct((s, D), BF16),
        in_specs=[_row(tm, D), _vec(D), _vec(D), _vec(D)], out_specs=_row(tm, D),
        compiler_params=_params("parallel"),
    )(x, g, sc, sh)


def _heads(x, fn):
    return jnp.concatenate([fn(x[:, h * HD:(h + 1) * HD], h) for h in range(x.shape[1] // HD)], axis=1)


def _qknorm(proj, wqk, tm=512):
    s = proj.shape[0]

    def body(p_ref, w_ref, o_ref):
        pv = p_ref[...].astype(F32)
        wv = w_ref[...]

        def one(qh, h):
            r = lax.rsqrt(jnp.mean(qh * qh, axis=-1, keepdims=True) + EPS)
            return (qh * r) * wv[:, h * HD:(h + 1) * HD]

        o_ref[...] = _heads(pv, one).astype(BF16)

    return pl.pallas_call(
        body, name="qknorm", grid=(s // tm,),
        out_shape=jax.ShapeDtypeStruct((s, QKW), BF16),
        in_specs=[pl.BlockSpec((tm, QKW), lambda i: (i, 0)), pl.BlockSpec((1, QKW), lambda i: (0, 0))],
        out_specs=pl.BlockSpec((tm, QKW), lambda i: (i, 0)),
        compiler_params=_params("parallel"),
    )(proj, wqk)


def _qknorm_bwd(name, proj, dn, w, dproj, blk0, tm=512):
    s, width = dn.shape

    def body(p_ref, d_ref, w_ref, _, o_ref, acc_ref):
        pv = p_ref[...].astype(F32)
        dv = d_ref[...]
        wv = w_ref[...]
        sums = []

        def one(qh, h):
            dn = dv[:, h * HD:(h + 1) * HD]
            r = lax.rsqrt(jnp.mean(qh * qh, axis=-1, keepdims=True) + EPS)
            nh = qh * r
            sums.append(jnp.sum(dn * nh, axis=0, keepdims=True))
            dnw = dn * wv[:, h * HD:(h + 1) * HD]
            return r * (dnw - nh * jnp.mean(dnw * nh, axis=-1, keepdims=True))

        o_ref[...] = _heads(pv, one).astype(BF16)

        @pl.when(pl.program_id(0) == 0)
        def _():
            acc_ref[...] = jnp.zeros_like(acc_ref)

        acc_ref[0:1, :] += jnp.concatenate(sums, axis=1)

    return pl.pallas_call(
        body, name=name, grid=(s // tm,),
        out_shape=[jax.ShapeDtypeStruct((s, IN_W), BF16), jax.ShapeDtypeStruct((8, width), F32)],
        in_specs=[pl.BlockSpec((tm, width), lambda i: (i, blk0)), pl.BlockSpec((tm, width), lambda i: (i, 0)),
                  pl.BlockSpec((1, width), lambda i: (0, 0)), pl.BlockSpec(memory_space=pl.ANY)],
        out_specs=[pl.BlockSpec((tm, width), lambda i: (i, blk0)), pl.BlockSpec((8, width), lambda i: (0, 0))],
        input_output_aliases={3: 0},
        compiler_params=_params("arbitrary"),
    )(proj, dn, w, dproj)


def _attn_shapes(s, g):
    d = DILATIONS[g]
    tb = min(s, max(2048, 256 * d))
    sb = min(256, tb // d)
    pb = BAND * d
    assert s % tb == 0 and tb % pb == 0 and (tb // d) % sb == 0 and sb % BAND == 0
    return d, tb, sb, pb


def _lanes(x, width):
    return jnp.concatenate([x] * (width // HD), axis=1)


def _every(start, size, d):
    return pl.ds(start, size, stride=d) if d > 1 else pl.ds(start, size)


def _attn_specs(g, tb, pb, s, ahead):
    ratio = tb // pb
    if ahead:
        nbr = lambda n: jnp.minimum((n + 1) * ratio, s // pb - 1)
    else:
        nbr = lambda n: jnp.maximum(n * ratio - 1, 0)
    cur = lambda base: pl.BlockSpec((tb, HD), lambda h, n: (n, base + g * N_HEADS + h))
    side = lambda base: pl.BlockSpec((pb, HD), lambda h, n: (nbr(n), base + g * N_HEADS + h))
    tok = pl.BlockSpec((tb, HD), lambda h, n: (n, h))
    tok_side = pl.BlockSpec((pb, HD), lambda h, n: (nbr(n), h))
    return cur, side, tok, tok_side


Q_COL, K_COL, V_COL = 0, 12, 24


def _attn_fwd(g, qkn, proj):
    s = qkn.shape[0]
    d, tb, sb, pb = _attn_shapes(s, g)
    ft = F32 if d > 1 else BF16
    nj = tb // d // sb
    scale = HD ** -0.5

    def body(q_ref, kc_ref, kp_ref, vc_ref, vp_ref, o_ref, lse_ref, qf, kf, vf):
        n = pl.program_id(1)
        qf[...] = q_ref[...].astype(ft)
        kf[0:pb] = kp_ref[...].astype(ft)
        kf[pb:] = kc_ref[...].astype(ft)
        vf[0:pb] = vp_ref[...].astype(ft)
        vf[pb:] = vc_ref[...].astype(ft)
        for r in range(d):
            for j in range(nj):
                at = j * sb * d + r
                q = qf[_every(at, sb, d), :].astype(BF16)
                k = kf[_every(at, sb + BAND, d), :].astype(BF16)
                v = vf[_every(at, sb + BAND, d), :].astype(BF16)
                sc = lax.dot_general(q, k, NT_DIMS, preferred_element_type=F32) * scale
                qi = lax.broadcasted_iota(jnp.int32, sc.shape, 0)
                kj = lax.broadcasted_iota(jnp.int32, sc.shape, 1)
                valid = (kj >= qi) & (kj <= qi + BAND)
                if j == 0:
                    valid = valid & ((kj >= BAND) | (n > 0))
                sc = jnp.where(valid, sc, -1e30)
                m = jnp.max(sc, axis=-1, keepdims=True)
                p = jnp.exp(sc - m)
                l = jnp.sum(p, axis=-1, keepdims=True)
                o = lax.dot_general(p.astype(BF16), v, NN_DIMS, preferred_element_type=F32)
                o_ref[_every(at, sb, d), :] = o / l
                lse_ref[_every(at, sb, d), :] = jnp.broadcast_to(m + jnp.log(l), (sb, HD))

    cur, side, tok, _ = _attn_specs(g, tb, pb, s, ahead=False)
    return pl.pallas_call(
        body, name=f"attn_fwd_g{g}", grid=(N_HEADS, s // tb),
        out_shape=[jax.ShapeDtypeStruct((s, COL), F32)] * 2,
        in_specs=[cur(Q_COL), cur(K_COL), side(K_COL), cur(V_COL), side(V_COL)],
        out_specs=[tok, tok],
        scratch_shapes=[pltpu.VMEM((tb, HD), ft), pltpu.VMEM((tb + pb, HD), ft),
                        pltpu.VMEM((tb + pb, HD), ft)],
        compiler_params=_params("parallel", "arbitrary"),
    )(qkn, qkn, qkn, proj, proj)


def _attn_combine(os_, lses, tm=512):
    s = os_[0].shape[0]

    def body(o0, o1, o2, l0, l1, l2, o_ref, lse_ref):
        a, b, c = l0[...], l1[...], l2[...]
        m = jnp.maximum(jnp.maximum(a, b), c)
        ea, eb, ec = jnp.exp(a - m), jnp.exp(b - m), jnp.exp(c - m)
        tot = ea + eb + ec
        o_ref[...] = ((ea * o0[...] + eb * o1[...] + ec * o2[...]) / tot).astype(BF16)
        lse_ref[...] = m + jnp.log(tot)

    return pl.pallas_call(
        body, name="attn_combine", grid=(s // tm,),
        out_shape=[jax.ShapeDtypeStruct((s, COL), BF16), jax.ShapeDtypeStruct((s, COL), F32)],
        in_specs=[_row(tm, COL)] * 6, out_specs=[_row(tm, COL)] * 2,
        compiler_params=_params("parallel"),
    )(*os_, *lses)


def _attn_bwd(g, qkn, proj, do, lse, delta, dqn, dkn, dproj):
    s = qkn.shape[0]
    d, tb, sb, pb = _attn_shapes(s, g)
    ft = F32 if d > 1 else BF16
    nj = tb // d // sb
    nt = s // tb
    scale = HD ** -0.5
    chained = dqn is not None

    def body(k_ref, v_ref, qc_ref, qn_ref, doc_ref, don_ref, lc_ref, ln_ref, dc_ref, dn_ref, *rest):
        dq_ref, dk_ref, dv_ref, kf, vf, qf, dvf, later = rest[-8:]
        n = pl.program_id(1)
        kf[...] = k_ref[...].astype(ft)
        vf[...] = v_ref[...].astype(ft)
        qf[0:tb] = qc_ref[...].astype(ft)
        qf[tb:] = qn_ref[...].astype(ft)

        @pl.when(n == 0)
        def _():
            later[...] = jnp.zeros_like(later)

        def window(c_ref, n_ref, r, j):
            at = j * sb * d + r
            if j < nj - 1:
                return c_ref[_every(at, sb + BAND, d), :]
            return jnp.concatenate([c_ref[_every(at, sb, d), :], n_ref[_every(r, BAND, d), :]], axis=0)

        for r in range(d):
            tail = later[r]
            for j in range(nj):
                at = j * sb * d + r
                rows = _every(at, sb, d)
                k = kf[rows, :].astype(BF16)
                v = vf[rows, :].astype(BF16)
                q = qf[_every(at, sb + BAND, d), :].astype(BF16)
                dov = window(doc_ref, don_ref, r, j).astype(BF16)
                sc = lax.dot_general(q, k, NT_DIMS, preferred_element_type=F32) * scale
                qi = lax.broadcasted_iota(jnp.int32, sc.shape, 0)
                kj = lax.broadcasted_iota(jnp.int32, sc.shape, 1)
                valid = (qi >= kj) & (qi <= kj + BAND)
                if j == nj - 1:
                    valid = valid & ((qi < sb) | (n < nt - 1))
                p = jnp.exp(jnp.where(valid, sc - _lanes(window(lc_ref, ln_ref, r, j), sb), -1e30))
                dp = lax.dot_general(dov, v, NT_DIMS, preferred_element_type=F32)
                ds = (p * (dp - _lanes(window(dc_ref, dn_ref, r, j), sb)) * scale).astype(BF16)
                dvf[rows, :] = lax.dot_general(p.astype(BF16), dov, TN_DIMS, preferred_element_type=F32)
                dk_ref[rows, :] = lax.dot_general(ds, q, TN_DIMS, preferred_element_type=F32)
                dqw = lax.dot_general(ds, k, NN_DIMS, preferred_element_type=F32)
                first = dqw[:BAND] + tail
                dq_ref[rows, :] = first if sb == BAND else jnp.concatenate([first, dqw[BAND:sb]], axis=0)
                tail = dqw[sb:]
            later[r] = tail
        dv_ref[...] = dvf[...].astype(BF16)

    cur, side, tok, tok_side = _attn_specs(g, tb, pb, s, ahead=True)
    anyspec = pl.BlockSpec(memory_space=pl.ANY)
    n_heads_cols = 3 * N_HEADS * HD
    return pl.pallas_call(
        body, name=f"attn_bwd_g{g}", grid=(N_HEADS, nt),
        out_shape=[jax.ShapeDtypeStruct((s, n_heads_cols), F32), jax.ShapeDtypeStruct((s, n_heads_cols), F32),
                   jax.ShapeDtypeStruct((s, IN_W), BF16)],
        in_specs=[cur(K_COL), cur(V_COL), cur(Q_COL), side(Q_COL), tok, tok_side, tok, tok_side,
                  tok, tok_side] + ([anyspec, anyspec] if chained else []) + [anyspec],
        out_specs=[cur(0), cur(0), cur(V_COL)],
        input_output_aliases={10: 0, 11: 1, 12: 2} if chained else {10: 2},
        scratch_shapes=[pltpu.VMEM((tb, HD), ft), pltpu.VMEM((tb, HD), ft),
                        pltpu.VMEM((tb + pb, HD), ft), pltpu.VMEM((tb, HD), F32),
                        pltpu.VMEM((d, BAND, HD), F32)],
        compiler_params=_params("arbitrary", "arbitrary"),
    )(qkn, proj, qkn, qkn, do, do, lse, lse, delta, delta, *([dqn, dkn] if chained else []), dproj)


def _shift_down(x, before, k):
    rolled = pltpu.roll(x, k, 0)
    head = jnp.where(lax.broadcasted_iota(jnp.int32, before.shape, 0) < k, pltpu.roll(before, k, 0), rolled[:8])
    return jnp.concatenate([head, rolled[8:]], axis=0)


def _shift_up(x, after, k):
    rows = x.shape[0]
    rolled = pltpu.roll(x, rows - k, 0)
    tail = jnp.where(lax.broadcasted_iota(jnp.int32, after.shape, 0) >= 8 - k,
                     pltpu.roll(after, 8 - k, 0), rolled[rows - 8:])
    return jnp.concatenate([rolled[:rows - 8], tail], axis=0)


def _conv_fwd(proj, cw, tm=1024):
    s = proj.shape[0]
    r16 = tm // 16

    def body(u_ref, b_ref, c_ref, up_ref, cp_ref, w_ref, z_ref):
        i = pl.program_id(1)
        xc = c_ref[...].astype(F32) * u_ref[...].astype(F32)
        xp = jnp.where(i > 0, cp_ref[8:16, :].astype(F32) * up_ref[8:16, :].astype(F32), 0.0)
        w = w_ref[...]
        conv = _shift_down(xc, xp, 2) * w[0:1] + _shift_down(xc, xp, 1) * w[1:2] + xc * w[2:3]
        z_ref[...] = (b_ref[...].astype(F32) * conv).astype(BF16)

    tile = lambda blk: pl.BlockSpec((tm, COL), lambda j, i: (i, blk + j))
    before = lambda blk: pl.BlockSpec((16, COL), lambda j, i: (jnp.maximum(i * r16 - 1, 0), blk + j))
    return pl.pallas_call(
        body, name="conv_fwd", grid=(D // COL, s // tm),
        out_shape=jax.ShapeDtypeStruct((s, D), BF16),
        in_specs=[tile(U_BLK), tile(B_BLK), tile(C_BLK), before(U_BLK), before(C_BLK),
                  pl.BlockSpec((3, COL), lambda j, i: (0, j))],
        out_specs=pl.BlockSpec((tm, COL), lambda j, i: (i, j)),
        compiler_params=_params("parallel", "parallel"),
    )(proj, proj, proj, proj, proj, cw)


def _conv_bwd(dz, proj, cw, dproj, tm=1024):
    s = proj.shape[0]
    r16 = tm // 16
    nrow = s // tm

    def body(dz_ref, u_ref, b_ref, c_ref, up_ref, cp_ref, dzn_ref, bn_ref, w_ref, _, o_ref, dc_ref, acc_ref):
        piece, i = pl.program_id(1), pl.program_id(2)
        u, c = u_ref[...].astype(F32), c_ref[...].astype(F32)
        bv = b_ref[...].astype(F32)
        dzv = dz_ref[...]
        w = w_ref[...]

        @pl.when((piece == 0) & (i == 0))
        def _():
            acc_ref[...] = jnp.zeros_like(acc_ref)

        @pl.when(piece == 0)
        def _():
            xc = c * u
            xp = jnp.where(i > 0, cp_ref[8:16, :].astype(F32) * up_ref[8:16, :].astype(F32), 0.0)
            x2, x1 = _shift_down(xc, xp, 2), _shift_down(xc, xp, 1)
            o_ref[...] = (dzv * (x2 * w[0:1] + x1 * w[1:2] + xc * w[2:3])).astype(BF16)
            dc_ref[...] = jnp.zeros_like(dc_ref)
            dconv = dzv * bv
            acc_ref[0:1, :] += jnp.sum(dconv * x2, axis=0, keepdims=True)
            acc_ref[1:2, :] += jnp.sum(dconv * x1, axis=0, keepdims=True)
            acc_ref[2:3, :] += jnp.sum(dconv * xc, axis=0, keepdims=True)

        @pl.when(piece == 1)
        def _():
            dconv = dzv * bv
            dn = jnp.where(i < nrow - 1, dzn_ref[...] * bn_ref[0:8, :].astype(F32), 0.0)
            dxc = dconv * w[2:3] + _shift_up(dconv, dn, 1) * w[1:2] + _shift_up(dconv, dn, 2) * w[0:1]
            o_ref[...] = (dxc * c).astype(BF16)
            dc_ref[...] = (dxc * u).astype(BF16)

    tile = lambda blk: pl.BlockSpec((tm, COL), lambda j, p, i: (i, blk + j))
    before = lambda blk: pl.BlockSpec((16, COL), lambda j, p, i: (jnp.maximum(i * r16 - 1, 0), blk + j))
    after = lambda rows, blk: pl.BlockSpec(
        (rows, COL), lambda j, p, i: (jnp.minimum((i + 1) * (tm // rows), s // rows - 1), blk + j))
    return pl.pallas_call(
        body, name="conv_bwd", grid=(D // COL, 2, nrow),
        out_shape=[jax.ShapeDtypeStruct((s, IN_W), BF16), jax.ShapeDtypeStruct((s + tm, D), BF16),
                   jax.ShapeDtypeStruct((8, D), F32)],
        in_specs=[tile(0), tile(U_BLK), tile(B_BLK), tile(C_BLK), before(U_BLK), before(C_BLK),
                  after(8, 0), after(16, B_BLK), pl.BlockSpec((3, COL), lambda j, p, i: (0, j)),
                  pl.BlockSpec(memory_space=pl.ANY)],
        out_specs=[pl.BlockSpec((tm, COL), lambda j, p, i: (i, jnp.where(p == 0, B_BLK, U_BLK) + j)),
                   pl.BlockSpec((tm, COL), lambda j, p, i: (jnp.where(p == 0, nrow, i), j)),
                   pl.BlockSpec((8, COL), lambda j, p, i: (0, j))],
        input_output_aliases={9: 0},
        compiler_params=_params("arbitrary", "arbitrary", "arbitrary"),
    )(dz, proj, proj, proj, proj, proj, dz, proj, cw, dproj)


def _copy_columns(name, src, dst, blk0, tm=1024):
    s, w = dst.shape[0], src.shape[1]
    fresh = isinstance(dst, jax.ShapeDtypeStruct)

    def body(x_ref, *rest):
        rest[-1][...] = x_ref[...]

    return pl.pallas_call(
        body, name=name, grid=(w // COL, s // tm),
        out_shape=jax.ShapeDtypeStruct(dst.shape, dst.dtype),
        in_specs=[pl.BlockSpec((tm, COL), lambda j, i: (i, j))] + ([] if fresh else [pl.BlockSpec(memory_space=pl.ANY)]),
        out_specs=pl.BlockSpec((tm, COL), lambda j, i: (i, blk0 + j)),
        input_output_aliases={} if fresh else {1: 0},
        compiler_params=_params("parallel", "parallel"),
    )(src, *([] if fresh else [dst]))


def _mod_part(c_all, w_ada, b_part):
    def body(c_ref, w_ref, b_ref, o_ref):
        cv = c_ref[...]
        act = cv * _sigmoid(cv)
        o_ref[...] = jnp.dot(act, w_ref[...], preferred_element_type=F32,
                             precision=lax.Precision.HIGHEST) + b_ref[...]

    return pl.pallas_call(
        body, name="mod_part", out_shape=jax.ShapeDtypeStruct((N_DEV, w_ada.shape[1]), F32),
    )(c_all, w_ada, b_part)


def _w_ada_grad(c_all_t, dmod_part):
    def body(c_ref, d_ref, o_ref):
        cv = c_ref[...]
        act = cv * _sigmoid(cv)
        dv = d_ref[...]
        acc = act[:, 0:1] * dv[0:1, :]
        for b in range(1, N_DEV):
            acc = acc + act[:, b:b + 1] * dv[b:b + 1, :]
        o_ref[...] = acc

    return pl.pallas_call(
        body, name="w_ada_grad", out_shape=jax.ShapeDtypeStruct((D, dmod_part.shape[1]), F32),
    )(c_all_t, dmod_part)


def _sum_rows(name, v):
    def body(v_ref, o_ref):
        acc = v_ref[0]
        for k in range(1, N_DEV):
            acc = acc + v_ref[k]
        o_ref[...] = acc

    return pl.pallas_call(body, name=name, out_shape=jax.ShapeDtypeStruct(v.shape[1:], F32))(v)


def _adamw(name, w, g, m, v):
    rows, cols = w.shape
    limit = max(16, (1 << 20) // (4 * cols))
    tr = rows if rows <= limit else next((t for t in range(limit - limit % 16, 15, -16) if rows % t == 0), rows)
    c1 = 1.0 - ADAM_B1 ** ADAM_STEP
    c2 = 1.0 - ADAM_B2 ** ADAM_STEP
    parts = g.ndim == 3

    def body(w_ref, g_ref, m_ref, v_ref, go_ref, d_ref, nm_ref, nv_ref):
        if parts:
            gv = g_ref[0].astype(F32)
            for k in range(1, N_DEV):
                gv = gv + g_ref[k].astype(F32)
        else:
            gv = g_ref[...]
        go_ref[...] = gv
        nm = ADAM_B1 * m_ref[...] + (1.0 - ADAM_B1) * gv
        nv = ADAM_B2 * v_ref[...] + (1.0 - ADAM_B2) * (gv * gv)
        nm_ref[...] = nm
        nv_ref[...] = nv
        d_ref[...] = -ADAM_LR * ((nm / c1) / (jnp.sqrt(nv / c2) + ADAM_EPS) + ADAM_WD * w_ref[...])

    spec = pl.BlockSpec((tr, cols), lambda i: (i, 0))
    g_spec = pl.BlockSpec((N_DEV, tr, cols), lambda i: (0, i, 0)) if parts else spec
    return pl.pallas_call(
        body, name=name, grid=(rows // tr,),
        out_shape=[jax.ShapeDtypeStruct((rows, cols), F32)] * 4,
        in_specs=[spec, g_spec, spec, spec], out_specs=[spec] * 4,
        compiler_params=_params("parallel"),
    )(w, g, m, v)


def _adamw_small(ws, gs, ms, vs):
    n = len(ws)
    c1 = 1.0 - ADAM_B1 ** ADAM_STEP
    c2 = 1.0 - ADAM_B2 ** ADAM_STEP

    def body(*refs):
        for i in range(n):
            w_ref, g_ref, m_ref, v_ref = refs[i], refs[n + i], refs[2 * n + i], refs[3 * n + i]
            d_ref, nm_ref, nv_ref = refs[4 * n + 3 * i:4 * n + 3 * i + 3]
            gv = g_ref[...]
            nm = ADAM_B1 * m_ref[...] + (1.0 - ADAM_B1) * gv
            nv = ADAM_B2 * v_ref[...] + (1.0 - ADAM_B2) * (gv * gv)
            nm_ref[...] = nm
            nv_ref[...] = nv
            d_ref[...] = -ADAM_LR * ((nm / c1) / (jnp.sqrt(nv / c2) + ADAM_EPS) + ADAM_WD * w_ref[...])

    outs = pl.pallas_call(
        body, name="adamw_small",
        out_shape=[jax.ShapeDtypeStruct(w.shape, F32) for w in ws for _ in range(3)],
    )(*ws, *gs, *ms, *vs)
    return [tuple(outs[3 * i:3 * i + 3]) for i in range(n)]


HALF = FF // 2


def _sds(shape, dtype):
    return jax.ShapeDtypeStruct(shape, dtype)


def _row_tile(w):
    return lambda tm: ((tm, w), lambda i, j: (i, 0))


def _one(w):
    return lambda rows: ((rows, w), lambda i, j: (0, 0))


def _gate_up_swiglu(name, h, wgu, tm=512):
    s = h.shape[0]
    tm = min(tm, s)

    def epilogue(prod, first, tin, tout):
        pq_ref, s_ref = tout
        a, b = prod[:, :HALF], prod[:, HALF:]
        sig = _sigmoid(a)
        act = a * sig
        pq_ref[:, :HALF] = (b * (sig * (1.0 + a * (1.0 - sig)))).astype(BF16)
        pq_ref[:, HALF:] = act.astype(BF16)
        s_ref[...] = (act * b).astype(BF16)

    return _mm(name, h, wgu, "NT", None, tm, FF, D, n_outer=True, epilogue=epilogue,
               tiles_out=[(_sds((s, 2 * FF), BF16), (tm, FF), lambda i, j: (i, j)),
                          (_sds((s, FF), BF16), (tm, HALF), lambda i, j: (i, j))])


def _d_hidden_swiglu(name, df, wd, ab, after=(), tm=512):
    s = df.shape[0]
    tm = min(tm, s)

    def epilogue(prod, first, tin, tout, cols):
        da_cols = slice(cols[0], cols[0] + cols[1])
        db_cols = slice(HALF + cols[0], HALF + cols[0] + cols[1])
        tout[0][:, da_cols] = (prod * tin[0][:, da_cols].astype(F32)).astype(BF16)
        tout[0][:, db_cols] = (prod * tin[0][:, db_cols].astype(F32)).astype(BF16)

    chunks = [(c0, min(384, HALF - c0)) for c0 in range(0, HALF, 384)]
    return _mm(name, df, wd, "NT", None, tm, HALF, D, n_outer=True, epilogue=epilogue, col_chunks=chunks, after=after,
               tiles_in=[(ab, (tm, FF), lambda i, j: (i, j))],
               tiles_out=[(_sds((s, 2 * FF), BF16), (tm, FF), lambda i, j: (i, j))])[0]


def _out_residual(name, a, w, x, gt, coef, nxt, tm=512, tk=FF):
    s = a.shape[0]
    tm = min(tm, s)

    def epilogue(prod, first, tin, tout):
        x_ref, gt_ref, g_ref, sc_ref, sh_ref = tin
        f_ref, xn_ref, h_ref = tout
        f_ref[...] = prod
        xn = x_ref[...] + (coef * gt_ref[...]) * prod
        xn_ref[...] = xn
        r = lax.rsqrt(jnp.mean(xn * xn, axis=-1, keepdims=True) + EPS)
        h_ref[...] = ((xn * r) * g_ref[...] * (1.0 + sc_ref[...]) + sh_ref[...]).astype(BF16)

    row, vec = _row_tile(D)(tm), _one(D)(1)
    return _mm(name, a, w, "NN", None, tm, D, tk, epilogue=epilogue,
               tiles_in=[(x, *row), (gt, *vec)] + [(v, *vec) for v in nxt],
               tiles_out=[(_sds((s, D), F32), *row), (_sds((s, D), F32), *row), (_sds((s, D), BF16), *row)])


def _out_loss(name, a, w, x, gt, coef, target, tm=512):
    s = a.shape[0]
    tm = min(tm, s)

    def epilogue(prod, first, tin, tout):
        x_ref, gt_ref, t_ref = tin
        f_ref, g_ref, df_ref, acc_ref = tout
        f_ref[...] = prod
        cg = coef * gt_ref[...]
        e = x_ref[...] + cg * prod - t_ref[...]
        gv = e * (1.0 / D)
        g_ref[...] = gv
        df_ref[...] = (cg * gv).astype(BF16)

        @pl.when(first)
        def _():
            acc_ref[...] = jnp.zeros_like(acc_ref)

        acc_ref[0:1, :] += coef * jnp.sum(gv * prod, axis=0, keepdims=True)
        acc_ref[1:2, :] += (0.5 / D) * jnp.sum(e * e, axis=0, keepdims=True)

    row, vec = _row_tile(D)(tm), _one(D)(1)
    return _mm(name, a, w, "NN", None, tm, D, FF, epilogue=epilogue,
               tiles_in=[(x, *row), (gt, *vec), (target, *row)],
               tiles_out=[(_sds((s, D), F32), *row), (_sds((s, D), F32), *row), (_sds((s, D), BF16), *row),
                          (_sds((8, D), F32), *_one(D)(8))])


def _d_h_norm_bwd(name, da, w, x, gin, g, sc, sh, before=None, after=(), tm=256):
    s = da.shape[0]
    tm = min(tm, s)
    coef = before[2] if before else None

    def epilogue(prod, first, tin, tout):
        x_ref, gin_ref, g_ref, sc_ref, sh_ref = tin[:5]
        gout_ref, acc_ref = tout[:2]
        xv = x_ref[...]
        r = lax.rsqrt(jnp.mean(xv * xv, axis=-1, keepdims=True) + EPS)
        nv = xv * r
        gv, one_sc = g_ref[...], 1.0 + sc_ref[...]
        dn = prod * gv * one_sc
        gout = gin_ref[...] + r * (dn - nv * jnp.mean(dn * nv, axis=-1, keepdims=True))
        gout_ref[...] = gout

        @pl.when(first)
        def _():
            acc_ref[...] = jnp.zeros_like(acc_ref)

        dhn = prod * nv
        acc_ref[0:1, :] += jnp.sum(prod, axis=0, keepdims=True)
        acc_ref[1:2, :] += jnp.sum(dhn * gv, axis=0, keepdims=True)
        acc_ref[2:3, :] += jnp.sum(dhn * one_sc, axis=0, keepdims=True)
        if before:
            f_ref, gt_ref = tin[5:]
            tout[2][...] = ((coef * gt_ref[...]) * gout).astype(BF16)
            acc_ref[3:4, :] += coef * jnp.sum(gout * f_ref[...], axis=0, keepdims=True)

    row, vec = _row_tile(D)(tm), _one(D)(1)
    tiles_in = [(x, *row), (gin, *row), (g, *vec), (sc, *vec), (sh, *vec)]
    tiles_out = [(_sds((s, D), F32), *row), (_sds((8, D), F32), *_one(D)(8))]
    if before:
        tiles_in += [(before[0], *row), (before[1], *vec)]
        tiles_out.append((_sds((s, D), BF16), *row))
    return _mm(name, da, w, "NN", None, tm, D, da.shape[1], epilogue=epilogue, keep_b=True, after=after,
               tiles_in=tiles_in, tiles_out=tiles_out)


def _gate_tiles(proj, tm):
    return [(proj, (tm, COL), (lambda i, j, blk=blk: (i, blk))) for blk in (GA_BLK, GA_BLK + 1, GC_BLK, GC_BLK + 1)]


def _conv_branch_merge(z, wc, ya, proj, tm=1024):
    s = z.shape[0]
    tm = min(tm, s)

    def epilogue(prod, first, tin, tout):
        ya_ref, ga0, ga1, gc0, gc1 = tin
        tout[0][...] = prod.astype(BF16)
        for half, (ga, gc) in enumerate(((ga0, gc0), (ga1, gc1))):
            cols = slice(half * COL, (half + 1) * COL)
            tout[1][:, cols] = (_sigmoid(ga[...].astype(F32)) * ya_ref[:, cols].astype(F32)
                                + _sigmoid(gc[...].astype(F32)) * prod[:, cols]).astype(BF16)

    row = _row_tile(D)(tm)
    return _mm("mix_conv_branch", z, wc, "NN", None, tm, D, D, epilogue=epilogue,
               tiles_in=[(ya, *row)] + _gate_tiles(proj, tm),
               tiles_out=[(_sds((s, D), BF16), *row), (_sds((s, D), BF16), *row)])


def _d_merged_branches(dmix, wo, ya, yc, proj, tm=1024):
    s = dmix.shape[0]
    tm = min(tm, s)

    def epilogue(prod, first, tin, tout):
        ya_ref, yc_ref, ga0, ga1, gc0, gc1 = tin
        dya_ref, dyc_ref, dg_ref = tout
        for half, (ga, gc) in enumerate(((ga0, gc0), (ga1, gc1))):
            cols = slice(half * COL, (half + 1) * COL)
            dm = prod[:, cols]
            for y_ref, g_ref, dy_ref, off in ((ya_ref, ga, dya_ref, 0), (yc_ref, gc, dyc_ref, D)):
                sig = _sigmoid(g_ref[...].astype(F32))
                dms = dm * sig
                dy_ref[:, cols] = dms.astype(BF16)
                dg_ref[:, off + half * COL:off + (half + 1) * COL] = (
                    dms * y_ref[:, cols].astype(F32) * (1.0 - sig)).astype(BF16)

    row = _row_tile(D)(tm)
    return _mm("mix_d_merged", dmix, wo, "NT", None, tm, D, D, epilogue=epilogue,
               tiles_in=[(ya, *row), (yc, *row)] + _gate_tiles(proj, tm),
               tiles_out=[(_sds((s, D), BF16), *row), (_sds((s, D), BF16), *row),
                          (_sds((s, 2 * D), BF16), *_row_tile(2 * D)(tm))])


def _d_o_delta(dya, wa_t, o, tm=1024):
    s = dya.shape[0]
    tm = min(tm, s)

    def epilogue(prod, first, tin, tout):
        tout[0][...] = prod
        tout[1][...] = _heads(prod * tin[0][...].astype(F32), lambda ph, h: jnp.broadcast_to(
            jnp.sum(ph, axis=-1, keepdims=True), ph.shape))

    row = _row_tile(COL)(tm)
    return _mm("mix_d_o", dya, wa_t, "NN", None, tm, COL, D, epilogue=epilogue,
               tiles_in=[(o, *row)], tiles_out=[(_sds((s, COL), F32), *row), (_sds((s, COL), F32), *row)])


def _ffn_bwd(tag, df, x, gin, h, ab, sw, g, sc, sh, wgu, wd, before=None, tk_dw=2048):
    dwd = _mm(f"{tag}_dw_down", sw, df, "TN", BF16, HALF, D, tk_dw)
    dab = _d_hidden_swiglu(f"{tag}_d_hidden", df, wd, ab, after=[dwd])
    dwgu = _mm(f"{tag}_dw_gate_up", dab, h, "TN", BF16, HALF, D, tk_dw)
    res = _d_h_norm_bwd(f"{tag}_d_h", dab, wgu, x, gin, g, sc, sh, before=before, after=[dwgu], tm=512)
    return res, dwgu, dwd


def kernel(x, c, w_ada, b_ada, norm_ffn1, ffn1_w_gate, ffn1_w_up, ffn1_w_down, norm_mix, w_in, q_norm, k_norm, conv_w, w_attn_branch, w_conv_branch, w_out, norm_ffn2, ffn2_w_gate, ffn2_w_up, ffn2_w_down, loss_target, m_w_ada, m_b_ada, m_norm_ffn1, m_ffn1_w_gate, m_ffn1_w_up, m_ffn1_w_down, m_norm_mix, m_w_in, m_q_norm, m_k_norm, m_conv_w, m_w_attn_branch, m_w_conv_branch, m_w_out, m_norm_ffn2, m_ffn2_w_gate, m_ffn2_w_up, m_ffn2_w_down, v_w_ada, v_b_ada, v_norm_ffn1, v_ffn1_w_gate, v_ffn1_w_up, v_ffn1_w_down, v_norm_mix, v_w_in, v_q_norm, v_k_norm, v_conv_w, v_w_attn_branch, v_w_conv_branch, v_w_out, v_norm_ffn2, v_ffn2_w_gate, v_ffn2_w_up, v_ffn2_w_down):
    me = 4 * lax.axis_index("x") + 2 * lax.axis_index("y") + lax.axis_index("c")
    x0, target = x[0], loss_target[0]
    s = x0.shape[0]
    ada_cols = w_ada.shape[2]
    cw_cols = conv_w.shape[2]

    gathered = _small_allgather(
        "gather_c_conv", jnp.concatenate([c, conv_w[0].reshape(1, 3 * cw_cols)], axis=1))[:, 0]
    c_all = gathered[:, :D]
    cw = gathered[:, D:].reshape(N_DEV, 3, cw_cols).transpose(1, 0, 2).reshape(3, D)
    b_part = lax.dynamic_slice(b_ada, (0, me * ada_cols), (1, ada_cols))
    mod_part = _mod_part(c_all, w_ada[0], b_part)
    mod_all = _small_allgather("gather_mod", mod_part.reshape(1, N_DEV * ada_cols))
    mod = lax.dynamic_slice(mod_all.reshape(N_DEV, N_DEV, ada_cols), (0, me, 0), (N_DEV, 1, ada_cols))
    mod = mod.reshape(N_MOD, 1, D)
    sh1, sc1, gt1, sh2, sc2, gt2, sh3, sc3, gt3 = [mod[i] for i in range(N_MOD)]

    tb = lambda w: w[0].T.astype(BF16)
    nb = lambda w: w[0].astype(BF16)
    ffn1_shards = [tb(ffn1_w_gate), tb(ffn1_w_up), nb(ffn1_w_down)]
    ffn2_shards = [tb(ffn2_w_gate), tb(ffn2_w_up), nb(ffn2_w_down)]
    mix_shards = [tb(w_in), tb(w_attn_branch), nb(w_conv_branch), nb(w_out)]
    ffn_dst, ffn_base, ffn_jump, ffn_shapes = [0, 0, 1], [0, HALF, 0], [HALF, HALF, 0], [(2 * FF, D), (FF, D)]
    mix_dst, mix_base, mix_shapes = [0, 1, 2, 3], [0, 0, 0, 0], [(IN_W, D), (D, COL), (D, D), (D, D)]
    (wgu1,) = _run_plan_on_sequencer(
        "gather_ffn1_gate_up", _gather_plan(ffn1_shards[:2], ffn_dst[:2], ffn_base[:2], ffn_shapes[:1], ffn_jump[:2]), 1)
    (wd1,) = _run_plan_on_sequencer(
        "gather_ffn1_down", _gather_plan(ffn1_shards[2:], [0], [0], ffn_shapes[1:]), 8)
    win_t, wa_t, wc, wo = _run_plan_on_sequencer(
        "gather_mix_weights", _gather_plan(mix_shards, mix_dst, mix_base, mix_shapes), 2)
    wgu2, wd2 = _run_plan_on_sequencer(
        "gather_ffn2_weights", _gather_plan(ffn2_shards, ffn_dst, ffn_base, ffn_shapes, ffn_jump), 3)

    h1 = _normmod("ffn1_normmod", x0, norm_ffn1, sc1, sh1)
    ab1, s1 = _gate_up_swiglu("ffn1_gate_up", h1, wgu1)
    f1, x1, h2 = _out_residual("ffn1_down", s1, wd1, x0, gt1, 0.5, (norm_mix, sc2, sh2))
    proj = _mm("mix_in_proj", h2, win_t, "NT", BF16, 1024, IN_W // 4, D, n_outer=True)
    wqk = jnp.concatenate([jnp.tile(q_norm, (1, 12)), jnp.tile(k_norm, (1, 12))], axis=1)
    qkn = _qknorm(proj, wqk)
    group_out = [_attn_fwd(g, qkn, proj) for g in range(3)]
    o, lse = _attn_combine([go[0] for go in group_out], [go[1] for go in group_out])
    ya = _mm("mix_attn_branch", o, wa_t, "NT", BF16, 1024, 1024, COL)
    z = _conv_fwd(proj, cw)
    yc, merged = _conv_branch_merge(z, wc, ya, proj)
    mix, x2, h3 = _out_residual("mix_out_proj", merged, wo, x1, gt2, 1.0, (norm_ffn2, sc3, sh3), tm=1024, tk=D)
    ab3, s3 = _gate_up_swiglu("ffn2_gate_up", h3, wgu2)
    f3, g3, df3, acc_out = _out_loss("ffn2_down", s3, wd2, x2, gt3, 0.5, target)
    loss_part = jnp.sum(acc_out[1])

    ffn_rows = [sh_.shape[0] for sh_ in ffn1_shards]
    mix_rows = [sh_.shape[0] for sh_ in mix_shards]
    (g2, acc3, dmix), dwgu2, dwd2 = _ffn_bwd(
        "ffn2", df3, x2, g3, h3, ab3, s3, norm_ffn2, sc3, sh3, wgu2, wd2, before=(mix, gt2, 1.0))
    dya, dyc, dgates = _d_merged_branches(dmix, wo, ya, yc, proj)
    dwo = _mm("mix_dw_out", merged, dmix, "TN", BF16, 1024, 1024, 2048)
    dproj = _copy_columns("dproj_gates", dgates, jax.ShapeDtypeStruct((s, IN_W), BF16), GA_BLK)
    dwc = _mm("mix_dw_conv_branch", z, dyc, "TN", BF16, 1024, 1024, 2048)
    dz = _mm("mix_d_z", dyc, wc, "NT", F32, 1024, 1024, D)
    dproj, d_c, cw_acc = _conv_bwd(dz, proj, cw, dproj)
    dproj = _copy_columns("copy_d_c", d_c, dproj, C_BLK)
    dwa_t = _mm("mix_dw_attn_branch", dya, o, "TN", BF16, 1024, COL, 2048)
    do, delta = _d_o_delta(dya, wa_t, o)
    dqn = dkn = None
    for g in range(3):
        dqn, dkn, dproj = _attn_bwd(g, qkn, proj, do, lse, delta, dqn, dkn, dproj)
    dproj, wq_acc = _qknorm_bwd("qnorm_bwd", proj, dqn, wqk[:, :QKW // 2], dproj, 0)
    dproj, wk_acc = _qknorm_bwd("knorm_bwd", proj, dkn, wqk[:, QKW // 2:], dproj, 1)
    r_f2g, r_f2u, r_f2d, r_wa, r_wc, r_wo = _run_plan_on_sequencer(
        "scatter_ffn2_and_branch_grads",
        _scatter_plan([dwgu2, dwd2, dwa_t, dwc, dwo], [0, 0, 1, 2, 3, 4], [0, HALF, 0, 0, 0, 0],
                      ffn_rows + mix_rows[1:], [D, D, D, COL, D, D], [HALF, HALF, 0, 0, 0, 0]), 4)
    dwin_t = _mm("mix_dw_in", dproj, h2, "TN", BF16, IN_W // 4, COL, 2048)
    (r_win,) = _run_plan_on_sequencer(
        "scatter_w_in_grad", _scatter_plan([dwin_t], [0], [0], mix_rows[:1], [D]), 5)
    g1, acc2, df1 = _d_h_norm_bwd("mix_d_h", dproj, win_t, x1, g2, norm_mix, sc2, sh2, before=(f1, gt1, 0.5),
                                  after=[dwin_t])
    dwd1 = _mm("ffn1_dw_down", s1, df1, "TN", BF16, HALF, D, 2048)
    (r_f1d,) = _run_plan_on_sequencer(
        "scatter_ffn1_down_grad", _scatter_plan([dwd1], [0], [0], ffn_rows[2:], [D]), 6)
    dab1 = _d_hidden_swiglu("ffn1_d_hidden", df1, wd1, ab1, after=[dwd1, r_win])
    dwgu1 = _mm("ffn1_dw_gate_up", dab1, h1, "TN", BF16, HALF, D, 2048)
    r_f1g, r_f1u = _run_plan_on_sequencer(
        "scatter_ffn1_gate_up_grads",
        _scatter_plan([dwgu1], [0, 0], [0, HALF], ffn_rows[:2], [D, D], [HALF, HALF]), 7)
    g0, acc1 = _d_h_norm_bwd("ffn1_d_h", dab1, wgu1, x0, g1, norm_ffn1, sc1, sh1, after=[dwgu1, r_f1d], tm=512)

    dqw = jnp.sum(wq_acc[0].reshape(12, HD), axis=0)
    dkw = jnp.sum(wk_acc[0].reshape(12, HD), axis=0)
    small = jnp.concatenate([
        acc1[0], acc1[1], acc2[3], acc2[0], acc2[1], acc3[3], acc3[0], acc3[1], acc_out[0],
        acc1[2], acc2[2], acc3[2], dqw, dkw, cw_acc[0:3].reshape(3 * D),
        jnp.zeros((HD,), F32).at[0].set(loss_part)]).reshape(1, -1)
    small_all = _small_allgather("gather_small_grads", small)
    small_sum = _sum_rows("sum_small_grads", small_all)[0]
    n_mod = N_MOD * D
    g_b_ada = small_sum[:n_mod].reshape(1, n_mod)
    g_norm1, g_norm2, g_norm3 = [small_sum[n_mod + i * D:n_mod + (i + 1) * D].reshape(1, D) for i in range(3)]
    off = n_mod + 3 * D
    g_qn, g_kn = small_sum[off:off + HD].reshape(1, HD), small_sum[off + HD:off + 2 * HD].reshape(1, HD)
    g_cw_full = small_sum[off + 2 * HD:off + 2 * HD + 3 * D].reshape(3, D)
    loss = small_sum[off + 2 * HD + 3 * D]
    g_cw = lax.dynamic_slice(g_cw_full, (0, me * cw_cols), (3, cw_cols))
    dmod_part = lax.dynamic_slice(small_all[:, 0, :n_mod], (0, me * ada_cols), (N_DEV, ada_cols))
    g_w_ada = _w_ada_grad(c_all.T, dmod_part)

    as_rows = {"ffn1_w_gate", "ffn1_w_up", "w_in", "w_attn_branch", "ffn2_w_gate", "ffn2_w_up"}
    grad_list = [g_w_ada, g_b_ada, g_norm1, r_f1g, r_f1u, r_f1d, g_norm2, r_win,
                 g_qn, g_kn, g_cw, r_wa, r_wc, r_wo, g_norm3, r_f2g, r_f2u, r_f2d]
    weights = [w_ada, b_ada, norm_ffn1, ffn1_w_gate, ffn1_w_up, ffn1_w_down, norm_mix, w_in, q_norm, k_norm,
               conv_w, w_attn_branch, w_conv_branch, w_out, norm_ffn2, ffn2_w_gate, ffn2_w_up, ffn2_w_down]
    ms = [m_w_ada, m_b_ada, m_norm_ffn1, m_ffn1_w_gate, m_ffn1_w_up, m_ffn1_w_down, m_norm_mix, m_w_in, m_q_norm,
          m_k_norm, m_conv_w, m_w_attn_branch, m_w_conv_branch, m_w_out, m_norm_ffn2, m_ffn2_w_gate,
          m_ffn2_w_up, m_ffn2_w_down]
    vs = [v_w_ada, v_b_ada, v_norm_ffn1, v_ffn1_w_gate, v_ffn1_w_up, v_ffn1_w_down, v_norm_mix, v_w_in, v_q_norm,
          v_k_norm, v_conv_w, v_w_attn_branch, v_w_conv_branch, v_w_out, v_norm_ffn2, v_ffn2_w_gate,
          v_ffn2_w_up, v_ffn2_w_down]
    wnames = ["w_ada", "b_ada", "norm_ffn1", "ffn1_w_gate", "ffn1_w_up", "ffn1_w_down", "norm_mix", "w_in",
              "q_norm", "k_norm", "conv_w", "w_attn_branch", "w_conv_branch", "w_out", "norm_ffn2",
              "ffn2_w_gate", "ffn2_w_up", "ffn2_w_down"]
    small = [i for i, gr in enumerate(grad_list) if gr.ndim == 2 and gr.size <= 16384]
    flat = lambda a, i: a.reshape(-1, weights[i].shape[-1])
    small_res = dict(zip(small, _adamw_small(
        [flat(weights[i], i) for i in small], [flat(grad_list[i], i) for i in small],
        [flat(ms[i], i) for i in small], [flat(vs[i], i) for i in small])))
    grad_out, deltas, new_ms, new_vs = [], [], [], []
    for idx, (nm, w, gr, m_, v_) in enumerate(zip(wnames, weights, grad_list, ms, vs)):
        if idx in small_res:
            gr, dl, nm_, nv_ = [r.reshape(w.shape) for r in (gr, *small_res[idx])]
        elif nm in as_rows:
            res = _adamw(f"adamw_{nm}", w[0].T, gr, m_[0].T, v_[0].T)
            gr, dl, nm_, nv_ = [r.T[None] for r in res]
        else:
            two_d = (-1, w.shape[-1])
            res = _adamw(f"adamw_{nm}", w.reshape(two_d), gr if gr.ndim == 3 else gr.reshape(two_d),
                         m_.reshape(two_d), v_.reshape(two_d))
            gr, dl, nm_, nv_ = [r.reshape(w.shape) for r in res]
        grad_out.append(gr)
        deltas.append(dl)
        new_ms.append(nm_)
        new_vs.append(nv_)
    return (loss, g0[None], *grad_out, *deltas, *new_ms, *new_vs)
```

```python
import jax
import jax.numpy as jnp
from jax import lax
from jax.experimental import pallas as pl
from jax.experimental.pallas import tpu as pltpu
from jax.experimental.pallas import tpu_sc as plsc

F32 = jnp.float32
BF16 = jnp.bfloat16
MESH = pl.DeviceIdType.MESH

N_DEV = 8
D = 1024
FF = 2816
HD = 128
N_HEADS = 4
DILATIONS = (1, 4, 16)
BAND = 128
QKW = 2 * 3 * N_HEADS * HD
IN_W = 9728
COL = 512
V_BLK, U_BLK, B_BLK, C_BLK, GA_BLK, GC_BLK = 6, 9, 11, 13, 15, 17
EPS = 1e-6
N_MOD = 9
ADAM_LR, ADAM_B1, ADAM_B2, ADAM_EPS, ADAM_WD, ADAM_STEP = 0.001, 0.9, 0.999, 1e-08, 0.01, 10

NT_DIMS = (((1,), (1,)), ((), ()))
TN_DIMS = (((0,), (0,)), ((), ()))
NN_DIMS = (((1,), (0,)), ((), ()))


def _place():
    return lax.axis_index("x"), lax.axis_index("y"), lax.axis_index("c")


def _flip(coord, bit):
    return 1 - coord if bit else coord


def _params(*sem):
    return pltpu.CompilerParams(dimension_semantics=sem)


def _small_allgather(name, v):
    n = v.shape[-1]

    def body(v_ref, out_ref, send_sems, recv_sems):
        x, y, c = _place()
        me = 4 * x + 2 * y + c
        out_ref[me] = v_ref[...]
        copies = []
        for k in range(1, N_DEV):
            peer = (_flip(x, (k >> 2) & 1), _flip(y, (k >> 1) & 1), _flip(c, k & 1))
            cp = pltpu.make_async_remote_copy(
                src_ref=v_ref, dst_ref=out_ref.at[me], send_sem=send_sems.at[k - 1],
                recv_sem=recv_sems.at[k - 1], device_id=peer, device_id_type=MESH)
            cp.start()
            copies.append(cp)
        for cp in copies:
            cp.wait()

    return pl.pallas_call(
        body, name=name,
        out_shape=jax.ShapeDtypeStruct((N_DEV, 1, n), F32),
        in_specs=[pl.BlockSpec(memory_space=pltpu.VMEM)],
        out_specs=pl.BlockSpec(memory_space=pltpu.VMEM),
        scratch_shapes=[pltpu.SemaphoreType.DMA((N_DEV - 1,)), pltpu.SemaphoreType.DMA((N_DEV - 1,))],
    )(v)


class _Plan:
    def __init__(self, operands, out_shapes, sems, phases):
        self.operands, self.out_shapes, self.sems, self.phases = operands, out_shapes, sems, phases


def _slab_start(base, rows, jump, idx):
    return pl.multiple_of(base + idx * rows + (idx // 4) * jump, 16)


def _gather_plan(shards, dst_of, base_of, dst_shapes, jump_of=None):
    n = len(shards)
    rows = [s.shape[0] for s in shards]
    jump_of = jump_of or [0] * n

    def phases(srcs, dsts, sems):
        send_sems, recv_sems, local_sems = sems
        x, y, c = _place()
        me, sibling = (x, y, c), (x, y, 1 - c)
        chips = [(1 - x, y), (x, 1 - y), (1 - x, 1 - y)]

        def slab(i, px, py, pc):
            start = _slab_start(base_of[i], rows[i], jump_of[i], 4 * px + 2 * py + pc)
            return dsts[dst_of[i]].at[pl.ds(start, rows[i])]

        def copy(i, k, block, to, src=None):
            return pltpu.make_async_remote_copy(
                src_ref=slab(i, *block) if src is None else src, dst_ref=slab(i, *block),
                send_sem=send_sems.at[i, k], recv_sem=recv_sems.at[i, k],
                device_id=to, device_id_type=MESH)

        def mine():
            return [pltpu.make_async_copy(srcs[i], slab(i, *me), local_sems.at[i]) for i in range(n)]

        def first():
            out = []
            for i in range(n):
                out.append(copy(i, 0, me, sibling, src=srcs[i]))
                out += [copy(i, 1 + j, me, (*chip, c), src=srcs[i]) for j, chip in enumerate(chips)]
            return out

        def passed():
            return [(copy(i, 1 + j, (*chip, c), me), copy(i, 4 + j, (*chip, c), sibling))
                    for j, chip in enumerate(chips) for i in range(n)]

        def start():
            for cp in mine() + first():
                cp.start()

        def middle():
            for landed, onward in passed():
                landed.wait_recv()
                onward.start()

        def finish():
            for i in range(n):
                copy(i, 0, sibling, me).wait_recv()
                for j, chip in enumerate(chips):
                    copy(i, 4 + j, (*chip, 1 - c), me).wait_recv()
            for cp in first() + [onward for _, onward in passed()]:
                cp.wait_send()
            for cp in mine():
                cp.wait()

        return start, middle, finish

    sems = [pltpu.SemaphoreType.DMA((n, 7)), pltpu.SemaphoreType.DMA((n, 7)), pltpu.SemaphoreType.DMA((n,))]
    return _Plan(list(shards), [jax.ShapeDtypeStruct(s, BF16) for s in dst_shapes], sems, phases)


def _scatter_plan(grads, src_of, base_of, rows, cols, jump_of=None):
    n = len(rows)
    jump_of = jump_of or [0] * n

    def phases(srcs, recvs, sems):
        send_sems, recv_sems, local_sems = sems
        x, y, c = _place()
        me = 4 * x + 2 * y + c

        def slab(i, idx):
            start = _slab_start(base_of[i], rows[i], jump_of[i], idx)
            return srcs[src_of[i]].at[pl.ds(start, rows[i])]

        def copies():
            out = [pltpu.make_async_copy(slab(i, me), recvs[i].at[me], local_sems.at[i]) for i in range(n)]
            for k in range(1, N_DEV):
                px, py, pc = _flip(x, (k >> 2) & 1), _flip(y, (k >> 1) & 1), _flip(c, k & 1)
                out += [pltpu.make_async_remote_copy(
                    src_ref=slab(i, 4 * px + 2 * py + pc), dst_ref=recvs[i].at[me],
                    send_sem=send_sems.at[i, k - 1], recv_sem=recv_sems.at[i, k - 1],
                    device_id=(px, py, pc), device_id_type=MESH) for i in range(n)]
            return out

        def start():
            for cp in copies():
                cp.start()

        def finish():
            for cp in copies():
                cp.wait()

        return start, None, finish

    sems = [pltpu.SemaphoreType.DMA((n, 7)), pltpu.SemaphoreType.DMA((n, 7)), pltpu.SemaphoreType.DMA((n,))]
    out_shapes = [jax.ShapeDtypeStruct((N_DEV, rows[i], cols[i]), BF16) for i in range(n)]
    return _Plan(list(grads), out_shapes, sems, phases)


def _run_plan_on_sequencer(name, plan, collective_id):
    src_refs = [jax.new_ref(a, memory_space=pltpu.MemorySpace.HBM) for a in plan.operands]
    dst_refs = [jax.empty_ref(s, memory_space=pltpu.MemorySpace.HBM) for s in plan.out_shapes]

    @pl.kernel(mesh=plsc.ScalarSubcoreMesh(axis_name="sequencer", num_cores=1), name=name,
               scratch_types=tuple(plan.sems),
               compiler_params=pltpu.CompilerParams(collective_id=collective_id))
    def launch(*sems):
        x, y, c = _place()
        barrier = pltpu.get_barrier_semaphore()
        for k in range(1, N_DEV):
            peer = (_flip(x, (k >> 2) & 1), _flip(y, (k >> 1) & 1), _flip(c, k & 1))
            pl.semaphore_signal(barrier, inc=1, device_id=peer, device_id_type=MESH)
        pl.semaphore_wait(barrier, N_DEV - 1)
        for phase in plan.phases(src_refs, dst_refs, sems):
            if phase is not None:
                phase()

    launch()
    return [r[...] for r in dst_refs]


def _mm(name, a, b, mode, out_dtype, tm, tn, tk, *, tiles_in=(), tiles_out=(), epilogue=None,
        n_outer=False, keep_b=False, col_chunks=None, after=()):
    if mode == "TN":
        kk, m = a.shape
    else:
        m, kk = a.shape
    n = b.shape[0] if mode == "NT" else b.shape[1]
    tm, tn, tk = min(tm, m), min(tn, n), min(tk, kk)
    assert m % tm == 0 and n % tn == 0 and kk % tk == 0, (name, m, n, kk, tm, tn, tk)
    ni, nj, nk = m // tm, n // tn, kk // tk
    dims = {"NN": NN_DIMS, "NT": NT_DIMS, "TN": TN_DIMS}[mode]
    if epilogue is None:
        tiles_out = [(jax.ShapeDtypeStruct((m, n), out_dtype), (tm, tn), lambda i, j: (i, j))]
    n_tin, n_tout = len(tiles_in), len(tiles_out)
    n_acc = 1 if nk > 1 else 0
    n_after = len(after)
    assert not keep_b or (nk == 1 and nj == 1)
    assert not col_chunks or (epilogue is not None and nk == 1 and mode != "TN")
    ij = (lambda p, q: (q, p)) if n_outer else (lambda p, q: (p, q))
    inner = ni if n_outer else nj

    def body(a_ref, b_ref, *rest):
        tin = rest[:n_tin]
        tout = rest[n_tin + n_after:n_tin + n_after + n_tout]
        scratch = rest[n_tin + n_after + n_tout:]
        k = pl.program_id(2)
        visit = pl.program_id(0) * inner + pl.program_id(1)
        if keep_b:
            b_kept, b_sem = scratch[n_acc:n_acc + 2]

            @pl.when((visit == 0) & (k == 0))
            def _():
                cp = pltpu.make_async_copy(b_ref, b_kept, b_sem)
                cp.start()
                cp.wait()

            b_ref = b_kept

        def store(prod, c=0, cols=()):
            if epilogue is None:
                tout[0][...] = prod.astype(out_dtype)
            else:
                epilogue(prod, jnp.logical_and(visit == 0, c == 0), tin, tout, *cols)

        if col_chunks:
            for c, (c0, cw) in enumerate(col_chunks):
                b_part = b_ref[pl.ds(c0, cw), :] if mode == "NT" else b_ref[:, pl.ds(c0, cw)]
                store(lax.dot_general(a_ref[...], b_part, dims, preferred_element_type=F32), c, ((c0, cw),))
        else:
            part = lax.dot_general(a_ref[...], b_ref[...], dims, preferred_element_type=F32)
            if nk == 1:
                store(part)
            else:
                acc_ref = scratch[0]

                @pl.when(k == 0)
                def _():
                    acc_ref[...] = part

                @pl.when((k > 0) & (k < nk - 1))
                def _():
                    acc_ref[...] += part

                @pl.when(k == nk - 1)
                def _():
                    store(acc_ref[...] + part)

    def spec(shape, fn):
        return pl.BlockSpec(shape, lambda p, q, k: fn(*ij(p, q)))

    a_spec = (pl.BlockSpec((tk, tm), lambda p, q, k: (k, ij(p, q)[0])) if mode == "TN"
              else pl.BlockSpec((tm, tk), lambda p, q, k: (ij(p, q)[0], k)))
    if keep_b:
        b_spec = pl.BlockSpec(memory_space=pl.ANY)
    elif mode == "NT":
        b_spec = pl.BlockSpec((tn, tk), lambda p, q, k: (ij(p, q)[1], k))
    else:
        b_spec = pl.BlockSpec((tk, tn), lambda p, q, k: (k, ij(p, q)[1]))
    sequential = epilogue or keep_b
    out = pl.pallas_call(
        body, name=name, grid=(nj, ni, nk) if n_outer else (ni, nj, nk),
        out_shape=[t[0] for t in tiles_out],
        in_specs=([a_spec, b_spec] + [spec(t[1], t[2]) for t in tiles_in]
                  + [pl.BlockSpec(memory_space=pl.ANY)] * n_after),
        out_specs=[spec(t[1], t[2]) for t in tiles_out],
        scratch_shapes=([pltpu.VMEM((tm, tn), F32)] * n_acc
                        + ([pltpu.VMEM(b.shape, b.dtype), pltpu.SemaphoreType.DMA] if keep_b else [])),
        compiler_params=(_params("arbitrary", "arbitrary", "arbitrary") if sequential
                         else _params("parallel", "parallel", "arbitrary")),
    )(a, b, *[t[0] for t in tiles_in], *after)
    return out if epilogue else out[0]


def _row(tm, w, off=0):
    return pl.BlockSpec((tm, w), lambda i: (i, off))


def _vec(w):
    return pl.BlockSpec((1, w), lambda i: (0, 0))


def _sigmoid(x):
    return 0.5 * jnp.tanh(0.5 * x) + 0.5


def _normmod(name, x, g, sc, sh, tm=512):
    s = x.shape[0]

    def body(x_ref, g_ref, sc_ref, sh_ref, h_ref):
        xv = x_ref[...]
        r = lax.rsqrt(jnp.mean(xv * xv, axis=-1, keepdims=True) + EPS)
        h_ref[...] = ((xv * r) * g_ref[...] * (1.0 + sc_ref[...]) + sh_ref[...]).astype(BF16)

    return pl.pallas_call(
        body, name=name, grid=(s // tm,),
        out_shape=jax.ShapeDtypeStruct((s, D), BF16),
        in_specs=[_row(tm, D), _vec(D), _vec(D), _vec(D)], out_specs=_row(tm, D),
        compiler_params=_params("parallel"),
    )(x, g, sc, sh)


def _heads(x, fn):
    return jnp.concatenate([fn(x[:, h * HD:(h + 1) * HD], h) for h in range(x.shape[1] // HD)], axis=1)


def _qknorm(proj, wqk, tm=512):
    s = proj.shape[0]

    def body(p_ref, w_ref, o_ref):
        pv = p_ref[...].astype(F32)
        wv = w_ref[...]

        def one(qh, h):
            r = lax.rsqrt(jnp.mean(qh * qh, axis=-1, keepdims=True) + EPS)
            return (qh * r) * wv[:, h * HD:(h + 1) * HD]

        o_ref[...] = _heads(pv, one).astype(BF16)

    return pl.pallas_call(
        body, name="qknorm", grid=(s // tm,),
        out_shape=jax.ShapeDtypeStruct((s, QKW), BF16),
        in_specs=[pl.BlockSpec((tm, QKW), lambda i: (i, 0)), pl.BlockSpec((1, QKW), lambda i: (0, 0))],
        out_specs=pl.BlockSpec((tm, QKW), lambda i: (i, 0)),
        compiler_params=_params("parallel"),
    )(proj, wqk)


def _qknorm_bwd(name, proj, dn, w, dproj, blk0, tm=512):
    s, width = dn.shape

    def body(p_ref, d_ref, w_ref, _, o_ref, acc_ref):
        pv = p_ref[...].astype(F32)
        dv = d_ref[...]
        wv = w_ref[...]
        sums = []

        def one(qh, h):
            dn = dv[:, h * HD:(h + 1) * HD]
            r = lax.rsqrt(jnp.mean(qh * qh, axis=-1, keepdims=True) + EPS)
            nh = qh * r
            sums.append(jnp.sum(dn * nh, axis=0, keepdims=True))
            dnw = dn * wv[:, h * HD:(h + 1) * HD]
            return r * (dnw - nh * jnp.mean(dnw * nh, axis=-1, keepdims=True))

        o_ref[...] = _heads(pv, one).astype(BF16)

        @pl.when(pl.program_id(0) == 0)
        def _():
            acc_ref[...] = jnp.zeros_like(acc_ref)

        acc_ref[0:1, :] += jnp.concatenate(sums, axis=1)

    return pl.pallas_call(
        body, name=name, grid=(s // tm,),
        out_shape=[jax.ShapeDtypeStruct((s, IN_W), BF16), jax.ShapeDtypeStruct((8, width), F32)],
        in_specs=[pl.BlockSpec((tm, width), lambda i: (i, blk0)), pl.BlockSpec((tm, width), lambda i: (i, 0)),
                  pl.BlockSpec((1, width), lambda i: (0, 0)), pl.BlockSpec(memory_space=pl.ANY)],
        out_specs=[pl.BlockSpec((tm, width), lambda i: (i, blk0)), pl.BlockSpec((8, width), lambda i: (0, 0))],
        input_output_aliases={3: 0},
        compiler_params=_params("arbitrary"),
    )(proj, dn, w, dproj)


def _attn_shapes(s, g):
    d = DILATIONS[g]
    tb = min(s, max(2048, 256 * d))
    sb = min(256, tb // d)
    pb = BAND * d
    assert s % tb == 0 and tb % pb == 0 and (tb // d) % sb == 0 and sb % BAND == 0
    return d, tb, sb, pb


def _lanes(x, width):
    return jnp.concatenate([x] * (width // HD), axis=1)


def _every(start, size, d):
    return pl.ds(start, size, stride=d) if d > 1 else pl.ds(start, size)


def _attn_specs(g, tb, pb, s, ahead):
    ratio = tb // pb
    if ahead:
        nbr = lambda n: jnp.minimum((n + 1) * ratio, s // pb - 1)
    else:
        nbr = lambda n: jnp.maximum(n * ratio - 1, 0)
    cur = lambda base: pl.BlockSpec((tb, HD), lambda h, n: (n, base + g * N_HEADS + h))
    side = lambda base: pl.BlockSpec((pb, HD), lambda h, n: (nbr(n), base + g * N_HEADS + h))
    tok = pl.BlockSpec((tb, HD), lambda h, n: (n, h))
    tok_side = pl.BlockSpec((pb, HD), lambda h, n: (nbr(n), h))
    return cur, side, tok, tok_side


Q_COL, K_COL, V_COL = 0, 12, 24


def _attn_fwd(g, qkn, proj):
    s = qkn.shape[0]
    d, tb, sb, pb = _attn_shapes(s, g)
    ft = F32 if d > 1 else BF16
    nj = tb // d // sb
    scale = HD ** -0.5

    def body(q_ref, kc_ref, kp_ref, vc_ref, vp_ref, o_ref, lse_ref, qf, kf, vf):
        n = pl.program_id(1)
        qf[...] = q_ref[...].astype(ft)
        kf[0:pb] = kp_ref[...].astype(ft)
        kf[pb:] = kc_ref[...].astype(ft)
        vf[0:pb] = vp_ref[...].astype(ft)
        vf[pb:] = vc_ref[...].astype(ft)
        for r in range(d):
            for j in range(nj):
                at = j * sb * d + r
                q = qf[_every(at, sb, d), :].astype(BF16)
                k = kf[_every(at, sb + BAND, d), :].astype(BF16)
                v = vf[_every(at, sb + BAND, d), :].astype(BF16)
                sc = lax.dot_general(q, k, NT_DIMS, preferred_element_type=F32) * scale
                qi = lax.broadcasted_iota(jnp.int32, sc.shape, 0)
                kj = lax.broadcasted_iota(jnp.int32, sc.shape, 1)
                valid = (kj >= qi) & (kj <= qi + BAND)
                if j == 0:
                    valid = valid & ((kj >= BAND) | (n > 0))
                sc = jnp.where(valid, sc, -1e30)
                m = jnp.max(sc, axis=-1, keepdims=True)
                p = jnp.exp(sc - m)
                l = jnp.sum(p, axis=-1, keepdims=True)
                o = lax.dot_general(p.astype(BF16), v, NN_DIMS, preferred_element_type=F32)
                o_ref[_every(at, sb, d), :] = o / l
                lse_ref[_every(at, sb, d), :] = jnp.broadcast_to(m + jnp.log(l), (sb, HD))

    cur, side, tok, _ = _attn_specs(g, tb, pb, s, ahead=False)
    return pl.pallas_call(
        body, name=f"attn_fwd_g{g}", grid=(N_HEADS, s // tb),
        out_shape=[jax.ShapeDtypeStruct((s, COL), F32)] * 2,
        in_specs=[cur(Q_COL), cur(K_COL), side(K_COL), cur(V_COL), side(V_COL)],
        out_specs=[tok, tok],
        scratch_shapes=[pltpu.VMEM((tb, HD), ft), pltpu.VMEM((tb + pb, HD), ft),
                        pltpu.VMEM((tb + pb, HD), ft)],
        compiler_params=_params("parallel", "arbitrary"),
    )(qkn, qkn, qkn, proj, proj)


def _attn_combine(os_, lses, tm=1024):
    s = os_[0].shape[0]

    def body(o0, o1, o2, l0, l1, l2, o_ref, lse_ref):
        a, b, c = l0[...], l1[...], l2[...]
        m = jnp.maximum(jnp.maximum(a, b), c)
        ea, eb, ec = jnp.exp(a - m), jnp.exp(b - m), jnp.exp(c - m)
        tot = ea + eb + ec
        o_ref[...] = ((ea * o0[...] + eb * o1[...] + ec * o2[...]) / tot).astype(BF16)
        lse_ref[...] = m + jnp.log(tot)

    return pl.pallas_call(
        body, name="attn_combine", grid=(s // tm,),
        out_shape=[jax.ShapeDtypeStruct((s, COL), BF16), jax.ShapeDtypeStruct((s, COL), F32)],
        in_specs=[_row(tm, COL)] * 6, out_specs=[_row(tm, COL)] * 2,
        compiler_params=_params("parallel"),
    )(*os_, *lses)


def _attn_bwd(g, qkn, proj, do, lse, delta, dqn, dkn, dproj):
    s = qkn.shape[0]
    d, tb, sb, pb = _attn_shapes(s, g)
    ft = F32 if d > 1 else BF16
    nj = tb // d // sb
    nt = s // tb
    scale = HD ** -0.5
    chained = dqn is not None

    def body(k_ref, v_ref, qc_ref, qn_ref, doc_ref, don_ref, lc_ref, ln_ref, dc_ref, dn_ref, *rest):
        dq_ref, dk_ref, dv_ref, kf, vf, qf, dvf, later = rest[-8:]
        n = pl.program_id(1)
        kf[...] = k_ref[...].astype(ft)
        vf[...] = v_ref[...].astype(ft)
        qf[0:tb] = qc_ref[...].astype(ft)
        qf[tb:] = qn_ref[...].astype(ft)

        @pl.when(n == 0)
        def _():
            later[...] = jnp.zeros_like(later)

        def window(c_ref, n_ref, r, j):
            at = j * sb * d + r
            if j < nj - 1:
                return c_ref[_every(at, sb + BAND, d), :]
            return jnp.concatenate([c_ref[_every(at, sb, d), :], n_ref[_every(r, BAND, d), :]], axis=0)

        for r in range(d):
            tail = later[r]
            for j in range(nj):
                at = j * sb * d + r
                rows = _every(at, sb, d)
                k = kf[rows, :].astype(BF16)
                v = vf[rows, :].astype(BF16)
                q = qf[_every(at, sb + BAND, d), :].astype(BF16)
                dov = window(doc_ref, don_ref, r, j).astype(BF16)
                sc = lax.dot_general(q, k, NT_DIMS, preferred_element_type=F32) * scale
                qi = lax.broadcasted_iota(jnp.int32, sc.shape, 0)
                kj = lax.broadcasted_iota(jnp.int32, sc.shape, 1)
                valid = (qi >= kj) & (qi <= kj + BAND)
                if j == nj - 1:
                    valid = valid & ((qi < sb) | (n < nt - 1))
                p = jnp.exp(jnp.where(valid, sc - _lanes(window(lc_ref, ln_ref, r, j), sb), -1e30))
                dp = lax.dot_general(dov, v, NT_DIMS, preferred_element_type=F32)
                ds = (p * (dp - _lanes(window(dc_ref, dn_ref, r, j), sb)) * scale).astype(BF16)
                dvf[rows, :] = lax.dot_general(p.astype(BF16), dov, TN_DIMS, preferred_element_type=F32)
                dk_ref[rows, :] = lax.dot_general(ds, q, TN_DIMS, preferred_element_type=F32)
                dqw = lax.dot_general(ds, k, NN_DIMS, preferred_element_type=F32)
                first = dqw[:BAND] + tail
                dq_ref[rows, :] = first if sb == BAND else jnp.concatenate([first, dqw[BAND:sb]], axis=0)
                tail = dqw[sb:]
            later[r] = tail
        dv_ref[...] = dvf[...].astype(BF16)

    cur, side, tok, tok_side = _attn_specs(g, tb, pb, s, ahead=True)
    anyspec = pl.BlockSpec(memory_space=pl.ANY)
    n_heads_cols = 3 * N_HEADS * HD
    return pl.pallas_call(
        body, name=f"attn_bwd_g{g}", grid=(N_HEADS, nt),
        out_shape=[jax.ShapeDtypeStruct((s, n_heads_cols), F32), jax.ShapeDtypeStruct((s, n_heads_cols), F32),
                   jax.ShapeDtypeStruct((s, IN_W), BF16)],
        in_specs=[cur(K_COL), cur(V_COL), cur(Q_COL), side(Q_COL), tok, tok_side, tok, tok_side,
                  tok, tok_side] + ([anyspec, anyspec] if chained else []) + [anyspec],
        out_specs=[cur(0), cur(0), cur(V_COL)],
        input_output_aliases={10: 0, 11: 1, 12: 2} if chained else {10: 2},
        scratch_shapes=[pltpu.VMEM((tb, HD), ft), pltpu.VMEM((tb, HD), ft),
                        pltpu.VMEM((tb + pb, HD), ft), pltpu.VMEM((tb, HD), F32),
                        pltpu.VMEM((d, BAND, HD), F32)],
        compiler_params=_params("arbitrary", "arbitrary"),
    )(qkn, proj, qkn, qkn, do, do, lse, lse, delta, delta, *([dqn, dkn] if chained else []), dproj)


def _shift_down(x, before, k):
    rolled = pltpu.roll(x, k, 0)
    head = jnp.where(lax.broadcasted_iota(jnp.int32, before.shape, 0) < k, pltpu.roll(before, k, 0), rolled[:8])
    return jnp.concatenate([head, rolled[8:]], axis=0)


def _shift_up(x, after, k):
    rows = x.shape[0]
    rolled = pltpu.roll(x, rows - k, 0)
    tail = jnp.where(lax.broadcasted_iota(jnp.int32, after.shape, 0) >= 8 - k,
                     pltpu.roll(after, 8 - k, 0), rolled[rows - 8:])
    return jnp.concatenate([rolled[:rows - 8], tail], axis=0)


def _conv_fwd(proj, cw, tm=1024):
    s = proj.shape[0]
    r16 = tm // 16

    def body(u_ref, b_ref, c_ref, up_ref, cp_ref, w_ref, z_ref):
        i = pl.program_id(1)
        xc = c_ref[...].astype(F32) * u_ref[...].astype(F32)
        xp = jnp.where(i > 0, cp_ref[8:16, :].astype(F32) * up_ref[8:16, :].astype(F32), 0.0)
        w = w_ref[...]
        conv = _shift_down(xc, xp, 2) * w[0:1] + _shift_down(xc, xp, 1) * w[1:2] + xc * w[2:3]
        z_ref[...] = (b_ref[...].astype(F32) * conv).astype(BF16)

    tile = lambda blk: pl.BlockSpec((tm, COL), lambda j, i: (i, blk + j))
    before = lambda blk: pl.BlockSpec((16, COL), lambda j, i: (jnp.maximum(i * r16 - 1, 0), blk + j))
    return pl.pallas_call(
        body, name="conv_fwd", grid=(D // COL, s // tm),
        out_shape=jax.ShapeDtypeStruct((s, D), BF16),
        in_specs=[tile(U_BLK), tile(B_BLK), tile(C_BLK), before(U_BLK), before(C_BLK),
                  pl.BlockSpec((3, COL), lambda j, i: (0, j))],
        out_specs=pl.BlockSpec((tm, COL), lambda j, i: (i, j)),
        compiler_params=_params("parallel", "parallel"),
    )(proj, proj, proj, proj, proj, cw)


def _conv_bwd(dz, proj, cw, dproj, tm=1024):
    s = proj.shape[0]
    r16 = tm // 16
    nrow = s // tm

    def body(dz_ref, u_ref, b_ref, c_ref, up_ref, cp_ref, dzn_ref, bn_ref, w_ref, _, o_ref, dc_ref, acc_ref):
        piece, i = pl.program_id(1), pl.program_id(2)
        u, c = u_ref[...].astype(F32), c_ref[...].astype(F32)
        bv = b_ref[...].astype(F32)
        dzv = dz_ref[...]
        w = w_ref[...]

        @pl.when((piece == 0) & (i == 0))
        def _():
            acc_ref[...] = jnp.zeros_like(acc_ref)

        @pl.when(piece == 0)
        def _():
            xc = c * u
            xp = jnp.where(i > 0, cp_ref[8:16, :].astype(F32) * up_ref[8:16, :].astype(F32), 0.0)
            x2, x1 = _shift_down(xc, xp, 2), _shift_down(xc, xp, 1)
            o_ref[...] = (dzv * (x2 * w[0:1] + x1 * w[1:2] + xc * w[2:3])).astype(BF16)
            dc_ref[...] = jnp.zeros_like(dc_ref)
            dconv = dzv * bv
            acc_ref[0:1, :] += jnp.sum(dconv * x2, axis=0, keepdims=True)
            acc_ref[1:2, :] += jnp.sum(dconv * x1, axis=0, keepdims=True)
            acc_ref[2:3, :] += jnp.sum(dconv * xc, axis=0, keepdims=True)

        @pl.when(piece == 1)
        def _():
            dconv = dzv * bv
            dn = jnp.where(i < nrow - 1, dzn_ref[...] * bn_ref[0:8, :].astype(F32), 0.0)
            dxc = dconv * w[2:3] + _shift_up(dconv, dn, 1) * w[1:2] + _shift_up(dconv, dn, 2) * w[0:1]
            o_ref[...] = (dxc * c).astype(BF16)
            dc_ref[...] = (dxc * u).astype(BF16)

    tile = lambda blk: pl.BlockSpec((tm, COL), lambda j, p, i: (i, blk + j))
    before = lambda blk: pl.BlockSpec((16, COL), lambda j, p, i: (jnp.maximum(i * r16 - 1, 0), blk + j))
    after = lambda rows, blk: pl.BlockSpec(
        (rows, COL), lambda j, p, i: (jnp.minimum((i + 1) * (tm // rows), s // rows - 1), blk + j))
    return pl.pallas_call(
        body, name="conv_bwd", grid=(D // COL, 2, nrow),
        out_shape=[jax.ShapeDtypeStruct((s, IN_W), BF16), jax.ShapeDtypeStruct((s + tm, D), BF16),
                   jax.ShapeDtypeStruct((8, D), F32)],
        in_specs=[tile(0), tile(U_BLK), tile(B_BLK), tile(C_BLK), before(U_BLK), before(C_BLK),
                  after(8, 0), after(16, B_BLK), pl.BlockSpec((3, COL), lambda j, p, i: (0, j)),
                  pl.BlockSpec(memory_space=pl.ANY)],
        out_specs=[pl.BlockSpec((tm, COL), lambda j, p, i: (i, jnp.where(p == 0, B_BLK, U_BLK) + j)),
                   pl.BlockSpec((tm, COL), lambda j, p, i: (jnp.where(p == 0, nrow, i), j)),
                   pl.BlockSpec((8, COL), lambda j, p, i: (0, j))],
        input_output_aliases={9: 0},
        compiler_params=_params("arbitrary", "arbitrary", "arbitrary"),
    )(dz, proj, proj, proj, proj, proj, dz, proj, cw, dproj)


def _copy_columns(name, src, dst, blk0, tm=2048):
    s, w = dst.shape[0], src.shape[1]
    fresh = isinstance(dst, jax.ShapeDtypeStruct)

    def body(x_ref, *rest):
        rest[-1][...] = x_ref[...]

    return pl.pallas_call(
        body, name=name, grid=(w // COL, s // tm),
        out_shape=jax.ShapeDtypeStruct(dst.shape, dst.dtype),
        in_specs=[pl.BlockSpec((tm, COL), lambda j, i: (i, j))] + ([] if fresh else [pl.BlockSpec(memory_space=pl.ANY)]),
        out_specs=pl.BlockSpec((tm, COL), lambda j, i: (i, blk0 + j)),
        input_output_aliases={} if fresh else {1: 0},
        compiler_params=_params("parallel", "parallel"),
    )(src, *([] if fresh else [dst]))


def _mod_part(c_all, w_ada, b_part):
    def body(c_ref, w_ref, b_ref, o_ref):
        cv = c_ref[...]
        act = cv * _sigmoid(cv)
        o_ref[...] = jnp.dot(act, w_ref[...], preferred_element_type=F32,
                             precision=lax.Precision.HIGHEST) + b_ref[...]

    return pl.pallas_call(
        body, name="mod_part", out_shape=jax.ShapeDtypeStruct((N_DEV, w_ada.shape[1]), F32),
    )(c_all, w_ada, b_part)


def _w_ada_grad(c_all_t, dmod_part):
    def body(c_ref, d_ref, o_ref):
        cv = c_ref[...]
        act = cv * _sigmoid(cv)
        dv = d_ref[...]
        acc = act[:, 0:1] * dv[0:1, :]
        for b in range(1, N_DEV):
            acc = acc + act[:, b:b + 1] * dv[b:b + 1, :]
        o_ref[...] = acc

    return pl.pallas_call(
        body, name="w_ada_grad", out_shape=jax.ShapeDtypeStruct((D, dmod_part.shape[1]), F32),
    )(c_all_t, dmod_part)


def _sum_rows(name, v):
    def body(v_ref, o_ref):
        acc = v_ref[0]
        for k in range(1, N_DEV):
            acc = acc + v_ref[k]
        o_ref[...] = acc

    return pl.pallas_call(body, name=name, out_shape=jax.ShapeDtypeStruct(v.shape[1:], F32))(v)


def _adamw(name, w, g, m, v):
    rows, cols = w.shape
    limit = max(16, (1 << 20) // (4 * cols))
    tr = rows if rows <= limit else next((t for t in range(limit - limit % 16, 15, -16) if rows % t == 0), rows)
    c1 = 1.0 - ADAM_B1 ** ADAM_STEP
    c2 = 1.0 - ADAM_B2 ** ADAM_STEP
    parts = g.ndim == 3

    def body(w_ref, g_ref, m_ref, v_ref, go_ref, d_ref, nm_ref, nv_ref):
        if parts:
            gv = g_ref[0].astype(F32)
            for k in range(1, N_DEV):
                gv = gv + g_ref[k].astype(F32)
        else:
            gv = g_ref[...]
        go_ref[...] = gv
        nm = ADAM_B1 * m_ref[...] + (1.0 - ADAM_B1) * gv
        nv = ADAM_B2 * v_ref[...] + (1.0 - ADAM_B2) * (gv * gv)
        nm_ref[...] = nm
        nv_ref[...] = nv
        d_ref[...] = -ADAM_LR * ((nm / c1) / (jnp.sqrt(nv / c2) + ADAM_EPS) + ADAM_WD * w_ref[...])

    spec = pl.BlockSpec((tr, cols), lambda i: (i, 0))
    g_spec = pl.BlockSpec((N_DEV, tr, cols), lambda i: (0, i, 0)) if parts else spec
    return pl.pallas_call(
        body, name=name, grid=(rows // tr,),
        out_shape=[jax.ShapeDtypeStruct((rows, cols), F32)] * 4,
        in_specs=[spec, g_spec, spec, spec], out_specs=[spec] * 4,
        compiler_params=_params("parallel"),
    )(w, g, m, v)


def _adamw_small(ws, gs, ms, vs):
    n = len(ws)
    c1 = 1.0 - ADAM_B1 ** ADAM_STEP
    c2 = 1.0 - ADAM_B2 ** ADAM_STEP

    def body(*refs):
        for i in range(n):
            w_ref, g_ref, m_ref, v_ref = refs[i], refs[n + i], refs[2 * n + i], refs[3 * n + i]
            d_ref, nm_ref, nv_ref = refs[4 * n + 3 * i:4 * n + 3 * i + 3]
            gv = g_ref[...]
            nm = ADAM_B1 * m_ref[...] + (1.0 - ADAM_B1) * gv
            nv = ADAM_B2 * v_ref[...] + (1.0 - ADAM_B2) * (gv * gv)
            nm_ref[...] = nm
            nv_ref[...] = nv
            d_ref[...] = -ADAM_LR * ((nm / c1) / (jnp.sqrt(nv / c2) + ADAM_EPS) + ADAM_WD * w_ref[...])

    outs = pl.pallas_call(
        body, name="adamw_small",
        out_shape=[jax.ShapeDtypeStruct(w.shape, F32) for w in ws for _ in range(3)],
    )(*ws, *gs, *ms, *vs)
    return [tuple(outs[3 * i:3 * i + 3]) for i in range(n)]


HALF = FF // 2


def _sds(shape, dtype):
    return jax.ShapeDtypeStruct(shape, dtype)


def _row_tile(w):
    return lambda tm: ((tm, w), lambda i, j: (i, 0))


def _one(w):
    return lambda rows: ((rows, w), lambda i, j: (0, 0))


def _gate_up_swiglu(name, h, wgu, tm=512):
    s = h.shape[0]
    tm = min(tm, s)

    def epilogue(prod, first, tin, tout):
        pq_ref, s_ref = tout
        a, b = prod[:, :HALF], prod[:, HALF:]
        sig = _sigmoid(a)
        act = a * sig
        pq_ref[:, :HALF] = (b * (sig * (1.0 + a * (1.0 - sig)))).astype(BF16)
        pq_ref[:, HALF:] = act.astype(BF16)
        s_ref[...] = (act * b).astype(BF16)

    return _mm(name, h, wgu, "NT", None, tm, FF, D, n_outer=True, epilogue=epilogue,
               tiles_out=[(_sds((s, 2 * FF), BF16), (tm, FF), lambda i, j: (i, j)),
                          (_sds((s, FF), BF16), (tm, HALF), lambda i, j: (i, j))])


def _d_hidden_swiglu(name, df, wd, ab, after=(), tm=1024):
    s = df.shape[0]
    tm = min(tm, s)

    def epilogue(prod, first, tin, tout, cols):
        da_cols = slice(cols[0], cols[0] + cols[1])
        db_cols = slice(HALF + cols[0], HALF + cols[0] + cols[1])
        tout[0][:, da_cols] = (prod * tin[0][:, da_cols].astype(F32)).astype(BF16)
        tout[0][:, db_cols] = (prod * tin[0][:, db_cols].astype(F32)).astype(BF16)

    chunks = [(c0, min(384, HALF - c0)) for c0 in range(0, HALF, 384)]
    return _mm(name, df, wd, "NT", None, tm, HALF, D, n_outer=True, epilogue=epilogue, col_chunks=chunks, after=after,
               tiles_in=[(ab, (tm, FF), lambda i, j: (i, j))],
               tiles_out=[(_sds((s, 2 * FF), BF16), (tm, FF), lambda i, j: (i, j))])[0]


def _out_residual(name, a, w, x, gt, coef, nxt, tm=512, tk=FF):
    s = a.shape[0]
    tm = min(tm, s)

    def epilogue(prod, first, tin, tout):
        x_ref, gt_ref, g_ref, sc_ref, sh_ref = tin
        f_ref, xn_ref, h_ref = tout
        f_ref[...] = prod
        xn = x_ref[...] + (coef * gt_ref[...]) * prod
        xn_ref[...] = xn
        r = lax.rsqrt(jnp.mean(xn * xn, axis=-1, keepdims=True) + EPS)
        h_ref[...] = ((xn * r) * g_ref[...] * (1.0 + sc_ref[...]) + sh_ref[...]).astype(BF16)

    row, vec = _row_tile(D)(tm), _one(D)(1)
    return _mm(name, a, w, "NN", None, tm, D, tk, epilogue=epilogue,
               tiles_in=[(x, *row), (gt, *vec)] + [(v, *vec) for v in nxt],
               tiles_out=[(_sds((s, D), F32), *row), (_sds((s, D), F32), *row), (_sds((s, D), BF16), *row)])


def _out_loss(name, a, w, x, gt, coef, target, tm=512):
    s = a.shape[0]
    tm = min(tm, s)

    def epilogue(prod, first, tin, tout):
        x_ref, gt_ref, t_ref = tin
        f_ref, g_ref, df_ref, acc_ref = tout
        f_ref[...] = prod
        cg = coef * gt_ref[...]
        e = x_ref[...] + cg * prod - t_ref[...]
        gv = e * (1.0 / D)
        g_ref[...] = gv
        df_ref[...] = (cg * gv).astype(BF16)

        @pl.when(first)
        def _():
            acc_ref[...] = jnp.zeros_like(acc_ref)

        acc_ref[0:1, :] += coef * jnp.sum(gv * prod, axis=0, keepdims=True)
        acc_ref[1:2, :] += (0.5 / D) * jnp.sum(e * e, axis=0, keepdims=True)

    row, vec = _row_tile(D)(tm), _one(D)(1)
    return _mm(name, a, w, "NN", None, tm, D, FF, epilogue=epilogue,
               tiles_in=[(x, *row), (gt, *vec), (target, *row)],
               tiles_out=[(_sds((s, D), F32), *row), (_sds((s, D), F32), *row), (_sds((s, D), BF16), *row),
                          (_sds((8, D), F32), *_one(D)(8))])


def _d_h_norm_bwd(name, da, w, x, gin, g, sc, sh, before=None, after=(), tm=256):
    s = da.shape[0]
    tm = min(tm, s)
    coef = before[2] if before else None

    def epilogue(prod, first, tin, tout):
        x_ref, gin_ref, g_ref, sc_ref, sh_ref = tin[:5]
        gout_ref, acc_ref = tout[:2]
        xv = x_ref[...]
        r = lax.rsqrt(jnp.mean(xv * xv, axis=-1, keepdims=True) + EPS)
        nv = xv * r
        gv, one_sc = g_ref[...], 1.0 + sc_ref[...]
        dn = prod * gv * one_sc
        gout = gin_ref[...] + r * (dn - nv * jnp.mean(dn * nv, axis=-1, keepdims=True))
        gout_ref[...] = gout

        @pl.when(first)
        def _():
            acc_ref[...] = jnp.zeros_like(acc_ref)

        dhn = prod * nv
        acc_ref[0:1, :] += jnp.sum(prod, axis=0, keepdims=True)
        acc_ref[1:2, :] += jnp.sum(dhn * gv, axis=0, keepdims=True)
        acc_ref[2:3, :] += jnp.sum(dhn * one_sc, axis=0, keepdims=True)
        if before:
            f_ref, gt_ref = tin[5:]
            tout[2][...] = ((coef * gt_ref[...]) * gout).astype(BF16)
            acc_ref[3:4, :] += coef * jnp.sum(gout * f_ref[...], axis=0, keepdims=True)

    row, vec = _row_tile(D)(tm), _one(D)(1)
    tiles_in = [(x, *row), (gin, *row), (g, *vec), (sc, *vec), (sh, *vec)]
    tiles_out = [(_sds((s, D), F32), *row), (_sds((8, D), F32), *_one(D)(8))]
    if before:
        tiles_in += [(before[0], *row), (before[1], *vec)]
        tiles_out.append((_sds((s, D), BF16), *row))
    return _mm(name, da, w, "NN", None, tm, D, da.shape[1], epilogue=epilogue, keep_b=True, after=after,
               tiles_in=tiles_in, tiles_out=tiles_out)


def _gate_tiles(proj, tm):
    return [(proj, (tm, COL), (lambda i, j, blk=blk: (i, blk))) for blk in (GA_BLK, GA_BLK + 1, GC_BLK, GC_BLK + 1)]


def _conv_branch_merge(z, wc, ya, proj, tm=1024):
    s = z.shape[0]
    tm = min(tm, s)

    def epilogue(prod, first, tin, tout):
        ya_ref, ga0, ga1, gc0, gc1 = tin
        tout[0][...] = prod.astype(BF16)
        for half, (ga, gc) in enumerate(((ga0, gc0), (ga1, gc1))):
            cols = slice(half * COL, (half + 1) * COL)
            tout[1][:, cols] = (_sigmoid(ga[...].astype(F32)) * ya_ref[:, cols].astype(F32)
                                + _sigmoid(gc[...].astype(F32)) * prod[:, cols]).astype(BF16)

    row = _row_tile(D)(tm)
    return _mm("mix_conv_branch", z, wc, "NN", None, tm, D, D, epilogue=epilogue,
               tiles_in=[(ya, *row)] + _gate_tiles(proj, tm),
               tiles_out=[(_sds((s, D), BF16), *row), (_sds((s, D), BF16), *row)])


def _d_merged_branches(dmix, wo, ya, yc, proj, tm=1024):
    s = dmix.shape[0]
    tm = min(tm, s)

    def epilogue(prod, first, tin, tout):
        ya_ref, yc_ref, ga0, ga1, gc0, gc1 = tin
        dya_ref, dyc_ref, dg_ref = tout
        for half, (ga, gc) in enumerate(((ga0, gc0), (ga1, gc1))):
            cols = slice(half * COL, (half + 1) * COL)
            dm = prod[:, cols]
            for y_ref, g_ref, dy_ref, off in ((ya_ref, ga, dya_ref, 0), (yc_ref, gc, dyc_ref, D)):
                sig = _sigmoid(g_ref[...].astype(F32))
                dms = dm * sig
                dy_ref[:, cols] = dms.astype(BF16)
                dg_ref[:, off + half * COL:off + (half + 1) * COL] = (
                    dms * y_ref[:, cols].astype(F32) * (1.0 - sig)).astype(BF16)

    row = _row_tile(D)(tm)
    return _mm("mix_d_merged", dmix, wo, "NT", None, tm, D, D, epilogue=epilogue,
               tiles_in=[(ya, *row), (yc, *row)] + _gate_tiles(proj, tm),
               tiles_out=[(_sds((s, D), BF16), *row), (_sds((s, D), BF16), *row),
                          (_sds((s, 2 * D), BF16), *_row_tile(2 * D)(tm))])


def _d_o_delta(dya, wa_t, o, tm=1024):
    s = dya.shape[0]
    tm = min(tm, s)

    def epilogue(prod, first, tin, tout):
        tout[0][...] = prod
        tout[1][...] = _heads(prod * tin[0][...].astype(F32), lambda ph, h: jnp.broadcast_to(
            jnp.sum(ph, axis=-1, keepdims=True), ph.shape))

    row = _row_tile(COL)(tm)
    return _mm("mix_d_o", dya, wa_t, "NN", None, tm, COL, D, epilogue=epilogue,
               tiles_in=[(o, *row)], tiles_out=[(_sds((s, COL), F32), *row), (_sds((s, COL), F32), *row)])


def _ffn_bwd(tag, df, x, gin, h, ab, sw, g, sc, sh, wgu, wd, before=None, tk_dw=2048):
    dwd = _mm(f"{tag}_dw_down", sw, df, "TN", BF16, HALF, D, tk_dw)
    dab = _d_hidden_swiglu(f"{tag}_d_hidden", df, wd, ab, after=[dwd])
    dwgu = _mm(f"{tag}_dw_gate_up", dab, h, "TN", BF16, HALF, D, tk_dw)
    res = _d_h_norm_bwd(f"{tag}_d_h", dab, wgu, x, gin, g, sc, sh, before=before, after=[dwgu], tm=512)
    return res, dwgu, dwd


def kernel(x, c, w_ada, b_ada, norm_ffn1, ffn1_w_gate, ffn1_w_up, ffn1_w_down, norm_mix, w_in, q_norm, k_norm, conv_w, w_attn_branch, w_conv_branch, w_out, norm_ffn2, ffn2_w_gate, ffn2_w_up, ffn2_w_down, loss_target, m_w_ada, m_b_ada, m_norm_ffn1, m_ffn1_w_gate, m_ffn1_w_up, m_ffn1_w_down, m_norm_mix, m_w_in, m_q_norm, m_k_norm, m_conv_w, m_w_attn_branch, m_w_conv_branch, m_w_out, m_norm_ffn2, m_ffn2_w_gate, m_ffn2_w_up, m_ffn2_w_down, v_w_ada, v_b_ada, v_norm_ffn1, v_ffn1_w_gate, v_ffn1_w_up, v_ffn1_w_down, v_norm_mix, v_w_in, v_q_norm, v_k_norm, v_conv_w, v_w_attn_branch, v_w_conv_branch, v_w_out, v_norm_ffn2, v_ffn2_w_gate, v_ffn2_w_up, v_ffn2_w_down):
    me = 4 * lax.axis_index("x") + 2 * lax.axis_index("y") + lax.axis_index("c")
    x0, target = x[0], loss_target[0]
    s = x0.shape[0]
    ada_cols = w_ada.shape[2]
    cw_cols = conv_w.shape[2]

    gathered = _small_allgather(
        "gather_c_conv", jnp.concatenate([c, conv_w[0].reshape(1, 3 * cw_cols)], axis=1))[:, 0]
    c_all = gathered[:, :D]
    cw = gathered[:, D:].reshape(N_DEV, 3, cw_cols).transpose(1, 0, 2).reshape(3, D)
    b_part = lax.dynamic_slice(b_ada, (0, me * ada_cols), (1, ada_cols))
    mod_part = _mod_part(c_all, w_ada[0], b_part)
    mod_all = _small_allgather("gather_mod", mod_part.reshape(1, N_DEV * ada_cols))
    mod = lax.dynamic_slice(mod_all.reshape(N_DEV, N_DEV, ada_cols), (0, me, 0), (N_DEV, 1, ada_cols))
    mod = mod.reshape(N_MOD, 1, D)
    sh1, sc1, gt1, sh2, sc2, gt2, sh3, sc3, gt3 = [mod[i] for i in range(N_MOD)]

    tb = lambda w: w[0].T.astype(BF16)
    nb = lambda w: w[0].astype(BF16)
    ffn1_shards = [tb(ffn1_w_gate), tb(ffn1_w_up), nb(ffn1_w_down)]
    ffn2_shards = [tb(ffn2_w_gate), tb(ffn2_w_up), nb(ffn2_w_down)]
    mix_shards = [tb(w_in), tb(w_attn_branch), nb(w_conv_branch), nb(w_out)]
    ffn_dst, ffn_base, ffn_jump, ffn_shapes = [0, 0, 1], [0, HALF, 0], [HALF, HALF, 0], [(2 * FF, D), (FF, D)]
    mix_dst, mix_base, mix_shapes = [0, 1, 2, 3], [0, 0, 0, 0], [(IN_W, D), (D, COL), (D, D), (D, D)]
    (wgu1,) = _run_plan_on_sequencer(
        "gather_ffn1_gate_up", _gather_plan(ffn1_shards[:2], ffn_dst[:2], ffn_base[:2], ffn_shapes[:1], ffn_jump[:2]), 1)
    (wd1,) = _run_plan_on_sequencer(
        "gather_ffn1_down", _gather_plan(ffn1_shards[2:], [0], [0], ffn_shapes[1:]), 8)
    win_t, wa_t, wc, wo = _run_plan_on_sequencer(
        "gather_mix_weights", _gather_plan(mix_shards, mix_dst, mix_base, mix_shapes), 2)
    wgu2, wd2 = _run_plan_on_sequencer(
        "gather_ffn2_weights", _gather_plan(ffn2_shards, ffn_dst, ffn_base, ffn_shapes, ffn_jump), 3)

    h1 = _normmod("ffn1_normmod", x0, norm_ffn1, sc1, sh1)
    ab1, s1 = _gate_up_swiglu("ffn1_gate_up", h1, wgu1)
    f1, x1, h2 = _out_residual("ffn1_down", s1, wd1, x0, gt1, 0.5, (norm_mix, sc2, sh2))
    proj = _mm("mix_in_proj", h2, win_t, "NT", BF16, 1024, IN_W // 4, D, n_outer=True)
    wqk = jnp.concatenate([jnp.tile(q_norm, (1, 12)), jnp.tile(k_norm, (1, 12))], axis=1)
    qkn = _qknorm(proj, wqk)
    group_out = [_attn_fwd(g, qkn, proj) for g in range(3)]
    o, lse = _attn_combine([go[0] for go in group_out], [go[1] for go in group_out])
    ya = _mm("mix_attn_branch", o, wa_t, "NT", BF16, 1024, 1024, COL)
    z = _conv_fwd(proj, cw)
    yc, merged = _conv_branch_merge(z, wc, ya, proj)
    mix, x2, h3 = _out_residual("mix_out_proj", merged, wo, x1, gt2, 1.0, (norm_ffn2, sc3, sh3), tm=1024, tk=D)
    ab3, s3 = _gate_up_swiglu("ffn2_gate_up", h3, wgu2)
    f3, g3, df3, acc_out = _out_loss("ffn2_down", s3, wd2, x2, gt3, 0.5, target)
    loss_part = jnp.sum(acc_out[1])

    ffn_rows = [sh_.shape[0] for sh_ in ffn1_shards]
    mix_rows = [sh_.shape[0] for sh_ in mix_shards]
    (g2, acc3, dmix), dwgu2, dwd2 = _ffn_bwd(
        "ffn2", df3, x2, g3, h3, ab3, s3, norm_ffn2, sc3, sh3, wgu2, wd2, before=(mix, gt2, 1.0))
    dya, dyc, dgates = _d_merged_branches(dmix, wo, ya, yc, proj)
    dwo = _mm("mix_dw_out", merged, dmix, "TN", BF16, 1024, 1024, 2048)
    dproj = _copy_columns("dproj_gates", dgates, jax.ShapeDtypeStruct((s, IN_W), BF16), GA_BLK)
    dwc = _mm("mix_dw_conv_branch", z, dyc, "TN", BF16, 1024, 1024, 2048)
    dz = _mm("mix_d_z", dyc, wc, "NT", F32, 1024, 1024, D)
    dproj, d_c, cw_acc = _conv_bwd(dz, proj, cw, dproj)
    dproj = _copy_columns("copy_d_c", d_c, dproj, C_BLK)
    dwa_t = _mm("mix_dw_attn_branch", dya, o, "TN", BF16, 1024, COL, 2048)
    do, delta = _d_o_delta(dya, wa_t, o)
    dqn = dkn = None
    for g in range(3):
        dqn, dkn, dproj = _attn_bwd(g, qkn, proj, do, lse, delta, dqn, dkn, dproj)
    dproj, wq_acc = _qknorm_bwd("qnorm_bwd", proj, dqn, wqk[:, :QKW // 2], dproj, 0)
    dproj, wk_acc = _qknorm_bwd("knorm_bwd", proj, dkn, wqk[:, QKW // 2:], dproj, 1)
    r_f2g, r_f2u, r_f2d, r_wa, r_wc, r_wo = _run_plan_on_sequencer(
        "scatter_ffn2_and_branch_grads",
        _scatter_plan([dwgu2, dwd2, dwa_t, dwc, dwo], [0, 0, 1, 2, 3, 4], [0, HALF, 0, 0, 0, 0],
                      ffn_rows + mix_rows[1:], [D, D, D, COL, D, D], [HALF, HALF, 0, 0, 0, 0]), 4)
    dwin_t = _mm("mix_dw_in", dproj, h2, "TN", BF16, IN_W // 4, COL, 2048)
    (r_win,) = _run_plan_on_sequencer(
        "scatter_w_in_grad", _scatter_plan([dwin_t], [0], [0], mix_rows[:1], [D]), 5)
    g1, acc2, df1 = _d_h_norm_bwd("mix_d_h", dproj, win_t, x1, g2, norm_mix, sc2, sh2, before=(f1, gt1, 0.5),
                                  after=[dwin_t])
    dwd1 = _mm("ffn1_dw_down", s1, df1, "TN", BF16, HALF, D, 2048)
    (r_f1d,) = _run_plan_on_sequencer(
        "scatter_ffn1_down_grad", _scatter_plan([dwd1], [0], [0], ffn_rows[2:], [D]), 6)
    dab1 = _d_hidden_swiglu("ffn1_d_hidden", df1, wd1, ab1, after=[dwd1, r_win])
    dwgu1 = _mm("ffn1_dw_gate_up", dab1, h1, "TN", BF16, HALF, D, 2048)
    r_f1g, r_f1u = _run_plan_on_sequencer(
        "scatter_ffn1_gate_up_grads",
        _scatter_plan([dwgu1], [0, 0], [0, HALF], ffn_rows[:2], [D, D], [HALF, HALF]), 7)
    g0, acc1 = _d_h_norm_bwd("ffn1_d_h", dab1, wgu1, x0, g1, norm_ffn1, sc1, sh1, after=[dwgu1, r_f1d], tm=512)

    dqw = jnp.sum(wq_acc[0].reshape(12, HD), axis=0)
    dkw = jnp.sum(wk_acc[0].reshape(12, HD), axis=0)
    small = jnp.concatenate([
        acc1[0], acc1[1], acc2[3], acc2[0], acc2[1], acc3[3], acc3[0], acc3[1], acc_out[0],
        acc1[2], acc2[2], acc3[2], dqw, dkw, cw_acc[0:3].reshape(3 * D),
        jnp.zeros((HD,), F32).at[0].set(loss_part)]).reshape(1, -1)
    small_all = _small_allgather("gather_small_grads", small)
    small_sum = _sum_rows("sum_small_grads", small_all)[0]
    n_mod = N_MOD * D
    g_b_ada = small_sum[:n_mod].reshape(1, n_mod)
    g_norm1, g_norm2, g_norm3 = [small_sum[n_mod + i * D:n_mod + (i + 1) * D].reshape(1, D) for i in range(3)]
    off = n_mod + 3 * D
    g_qn, g_kn = small_sum[off:off + HD].reshape(1, HD), small_sum[off + HD:off + 2 * HD].reshape(1, HD)
    g_cw_full = small_sum[off + 2 * HD:off + 2 * HD + 3 * D].reshape(3, D)
    loss = small_sum[off + 2 * HD + 3 * D]
    g_cw = lax.dynamic_slice(g_cw_full, (0, me * cw_cols), (3, cw_cols))
    dmod_part = lax.dynamic_slice(small_all[:, 0, :n_mod], (0, me * ada_cols), (N_DEV, ada_cols))
    g_w_ada = _w_ada_grad(c_all.T, dmod_part)

    as_rows = {"ffn1_w_gate", "ffn1_w_up", "w_in", "w_attn_branch", "ffn2_w_gate", "ffn2_w_up"}
    grad_list = [g_w_ada, g_b_ada, g_norm1, r_f1g, r_f1u, r_f1d, g_norm2, r_win,
                 g_qn, g_kn, g_cw, r_wa, r_wc, r_wo, g_norm3, r_f2g, r_f2u, r_f2d]
    weights = [w_ada, b_ada, norm_ffn1, ffn1_w_gate, ffn1_w_up, ffn1_w_down, norm_mix, w_in, q_norm, k_norm,
               conv_w, w_attn_branch, w_conv_branch, w_out, norm_ffn2, ffn2_w_gate, ffn2_w_up, ffn2_w_down]
    ms = [m_w_ada, m_b_ada, m_norm_ffn1, m_ffn1_w_gate, m_ffn1_w_up, m_ffn1_w_down, m_norm_mix, m_w_in, m_q_norm,
          m_k_norm, m_conv_w, m_w_attn_branch, m_w_conv_branch, m_w_out, m_norm_ffn2, m_ffn2_w_gate,
          m_ffn2_w_up, m_ffn2_w_down]
    vs = [v_w_ada, v_b_ada, v_norm_ffn1, v_ffn1_w_gate, v_ffn1_w_up, v_ffn1_w_down, v_norm_mix, v_w_in, v_q_norm,
          v_k_norm, v_conv_w, v_w_attn_branch, v_w_conv_branch, v_w_out, v_norm_ffn2, v_ffn2_w_gate,
          v_ffn2_w_up, v_ffn2_w_down]
    wnames = ["w_ada", "b_ada", "norm_ffn1", "ffn1_w_gate", "ffn1_w_up", "ffn1_w_down", "norm_mix", "w_in",
              "q_norm", "k_norm", "conv_w", "w_attn_branch", "w_conv_branch", "w_out", "norm_ffn2",
              "ffn2_w_gate", "ffn2_w_up", "ffn2_w_down"]
    small = [i for i, gr in enumerate(grad_list) if gr.ndim == 2 and gr.size <= 16384]
    flat = lambda a, i: a.reshape(-1, weights[i].shape[-1])
    small_res = dict(zip(small, _adamw_small(
        [flat(weights[i], i) for i in small], [flat(grad_list[i], i) for i in small],
        [flat(ms[i], i) for i in small], [flat(vs[i], i) for i in small])))
    grad_out, deltas, new_ms, new_vs = [], [], [], []
    for idx, (nm, w, gr, m_, v_) in enumerate(zip(wnames, weights, grad_list, ms, vs)):
        if idx in small_res:
            gr, dl, nm_, nv_ = [r.reshape(w.shape) for r in (gr, *small_res[idx])]
        elif nm in as_rows:
            res = _adamw(f"adamw_{nm}", w[0].T, gr, m_[0].T, v_[0].T)
            gr, dl, nm_, nv_ = [r.T[None] for r in res]
        else:
            two_d = (-1, w.shape[-1])
            res = _adamw(f"adamw_{nm}", w.reshape(two_d), gr if gr.ndim == 3 else gr.reshape(two_d),
                         m_.reshape(two_d), v_.reshape(two_d))
            gr, dl, nm_, nv_ = [r.reshape(w.shape) for r in res]
        grad_out.append(gr)
        deltas.append(dl)
        new_ms.append(nm_)
        new_vs.append(nv_)
    return (loss, g0[None], *grad_out, *deltas, *new_ms, *new_vs)
```

```python
import jax
import jax.numpy as jnp
from jax import lax
from jax.experimental import pallas as pl
from jax.experimental.pallas import tpu as pltpu
from jax.experimental.pallas import tpu_sc as plsc

F32 = jnp.float32
BF16 = jnp.bfloat16
MESH = pl.DeviceIdType.MESH

N_DEV = 8
D = 1024
FF = 2816
HD = 128
N_HEADS = 4
DILATIONS = (1, 4, 16)
BAND = 128
QKW = 2 * 3 * N_HEADS * HD
IN_W = 9728
COL = 512
V_BLK, U_BLK, B_BLK, C_BLK, GA_BLK, GC_BLK = 6, 9, 11, 13, 15, 17
EPS = 1e-6
N_MOD = 9
ADAM_LR, ADAM_B1, ADAM_B2, ADAM_EPS, ADAM_WD, ADAM_STEP = 0.001, 0.9, 0.999, 1e-08, 0.01, 10

NT_DIMS = (((1,), (1,)), ((), ()))
TN_DIMS = (((0,), (0,)), ((), ()))
NN_DIMS = (((1,), (0,)), ((), ()))


def _place():
    return lax.axis_index("x"), lax.axis_index("y"), lax.axis_index("c")


def _flip(coord, bit):
    return 1 - coord if bit else coord


def _params(*sem):
    return pltpu.CompilerParams(dimension_semantics=sem)


def _small_allgather(name, v):
    n = v.shape[-1]

    def body(v_ref, out_ref, send_sems, recv_sems):
        x, y, c = _place()
        me = 4 * x + 2 * y + c
        out_ref[me] = v_ref[...]
        copies = []
        for k in range(1, N_DEV):
            peer = (_flip(x, (k >> 2) & 1), _flip(y, (k >> 1) & 1), _flip(c, k & 1))
            cp = pltpu.make_async_remote_copy(
                src_ref=v_ref, dst_ref=out_ref.at[me], send_sem=send_sems.at[k - 1],
                recv_sem=recv_sems.at[k - 1], device_id=peer, device_id_type=MESH)
            cp.start()
            copies.append(cp)
        for cp in copies:
            cp.wait()

    return pl.pallas_call(
        body, name=name,
        out_shape=jax.ShapeDtypeStruct((N_DEV, 1, n), F32),
        in_specs=[pl.BlockSpec(memory_space=pltpu.VMEM)],
        out_specs=pl.BlockSpec(memory_space=pltpu.VMEM),
        scratch_shapes=[pltpu.SemaphoreType.DMA((N_DEV - 1,)), pltpu.SemaphoreType.DMA((N_DEV - 1,))],
    )(v)


class _Plan:
    def __init__(self, operands, out_shapes, sems, phases):
        self.operands, self.out_shapes, self.sems, self.phases = operands, out_shapes, sems, phases


def _slab_start(base, rows, jump, idx):
    return pl.multiple_of(base + idx * rows + (idx // 4) * jump, 16)


def _gather_plan(shards, dst_of, base_of, dst_shapes, jump_of=None):
    n = len(shards)
    rows = [s.shape[0] for s in shards]
    jump_of = jump_of or [0] * n

    def phases(srcs, dsts, sems):
        send_sems, recv_sems, local_sems = sems
        x, y, c = _place()
        me, sibling = (x, y, c), (x, y, 1 - c)
        chips = [(1 - x, y), (x, 1 - y), (1 - x, 1 - y)]

        def slab(i, px, py, pc):
            start = _slab_start(base_of[i], rows[i], jump_of[i], 4 * px + 2 * py + pc)
            return dsts[dst_of[i]].at[pl.ds(start, rows[i])]

        def copy(i, k, block, to, src=None):
            return pltpu.make_async_remote_copy(
                src_ref=slab(i, *block) if src is None else src, dst_ref=slab(i, *block),
                send_sem=send_sems.at[i, k], recv_sem=recv_sems.at[i, k],
                device_id=to, device_id_type=MESH)

        def mine():
            return [pltpu.make_async_copy(srcs[i], slab(i, *me), local_sems.at[i]) for i in range(n)]

        def first():
            out = []
            for i in range(n):
                out.append(copy(i, 0, me, sibling, src=srcs[i]))
                out += [copy(i, 1 + j, me, (*chip, c), src=srcs[i]) for j, chip in enumerate(chips)]
            return out

        def passed():
            return [(copy(i, 1 + j, (*chip, c), me), copy(i, 4 + j, (*chip, c), sibling))
                    for j, chip in enumerate(chips) for i in range(n)]

        def start():
            for cp in mine() + first():
                cp.start()

        def middle():
            for landed, onward in passed():
                landed.wait_recv()
                onward.start()

        def finish():
            for i in range(n):
                copy(i, 0, sibling, me).wait_recv()
                for j, chip in enumerate(chips):
                    copy(i, 4 + j, (*chip, 1 - c), me).wait_recv()
            for cp in first() + [onward for _, onward in passed()]:
                cp.wait_send()
            for cp in mine():
                cp.wait()

        return start, middle, finish

    sems = [pltpu.SemaphoreType.DMA((n, 7)), pltpu.SemaphoreType.DMA((n, 7)), pltpu.SemaphoreType.DMA((n,))]
    return _Plan(list(shards), [jax.ShapeDtypeStruct(s, BF16) for s in dst_shapes], sems, phases)


def _scatter_plan(grads, src_of, base_of, rows, cols, jump_of=None):
    n = len(rows)
    jump_of = jump_of or [0] * n

    def phases(srcs, recvs, sems):
        send_sems, recv_sems, local_sems = sems
        x, y, c = _place()
        me = 4 * x + 2 * y + c

        def slab(i, idx):
            start = _slab_start(base_of[i], rows[i], jump_of[i], idx)
            return srcs[src_of[i]].at[pl.ds(start, rows[i])]

        def copies():
            out = [pltpu.make_async_copy(slab(i, me), recvs[i].at[me], local_sems.at[i]) for i in range(n)]
            for k in range(1, N_DEV):
                px, py, pc = _flip(x, (k >> 2) & 1), _flip(y, (k >> 1) & 1), _flip(c, k & 1)
                out += [pltpu.make_async_remote_copy(
                    src_ref=slab(i, 4 * px + 2 * py + pc), dst_ref=recvs[i].at[me],
                    send_sem=send_sems.at[i, k - 1], recv_sem=recv_sems.at[i, k - 1],
                    device_id=(px, py, pc), device_id_type=MESH) for i in range(n)]
            return out

        def start():
            for cp in copies():
                cp.start()

        def finish():
            for cp in copies():
                cp.wait()

        return start, None, finish

    sems = [pltpu.SemaphoreType.DMA((n, 7)), pltpu.SemaphoreType.DMA((n, 7)), pltpu.SemaphoreType.DMA((n,))]
    out_shapes = [jax.ShapeDtypeStruct((N_DEV, rows[i], cols[i]), BF16) for i in range(n)]
    return _Plan(list(grads), out_shapes, sems, phases)


def _run_plan_on_sequencer(name, plan, collective_id):
    src_refs = [jax.new_ref(a, memory_space=pltpu.MemorySpace.HBM) for a in plan.operands]
    dst_refs = [jax.empty_ref(s, memory_space=pltpu.MemorySpace.HBM) for s in plan.out_shapes]

    @pl.kernel(mesh=plsc.ScalarSubcoreMesh(axis_name="sequencer", num_cores=1), name=name,
               scratch_types=tuple(plan.sems),
               compiler_params=pltpu.CompilerParams(collective_id=collective_id))
    def launch(*sems):
        x, y, c = _place()
        barrier = pltpu.get_barrier_semaphore()
        for k in range(1, N_DEV):
            peer = (_flip(x, (k >> 2) & 1), _flip(y, (k >> 1) & 1), _flip(c, k & 1))
            pl.semaphore_signal(barrier, inc=1, device_id=peer, device_id_type=MESH)
        pl.semaphore_wait(barrier, N_DEV - 1)
        for phase in plan.phases(src_refs, dst_refs, sems):
            if phase is not None:
                phase()

    launch()
    return [r[...] for r in dst_refs]


def _mm(name, a, b, mode, out_dtype, tm, tn, tk, *, tiles_in=(), tiles_out=(), epilogue=None,
        n_outer=False, keep_b=False, col_chunks=None, after=()):
    if mode == "TN":
        kk, m = a.shape
    else:
        m, kk = a.shape
    n = b.shape[0] if mode == "NT" else b.shape[1]
    tm, tn, tk = min(tm, m), min(tn, n), min(tk, kk)
    assert m % tm == 0 and n % tn == 0 and kk % tk == 0, (name, m, n, kk, tm, tn, tk)
    ni, nj, nk = m // tm, n // tn, kk // tk
    dims = {"NN": NN_DIMS, "NT": NT_DIMS, "TN": TN_DIMS}[mode]
    if epilogue is None:
        tiles_out = [(jax.ShapeDtypeStruct((m, n), out_dtype), (tm, tn), lambda i, j: (i, j))]
    n_tin, n_tout = len(tiles_in), len(tiles_out)
    n_acc = 1 if nk > 1 else 0
    n_after = len(after)
    assert not keep_b or (nk == 1 and nj == 1)
    assert not col_chunks or (epilogue is not None and nk == 1 and mode != "TN")
    ij = (lambda p, q: (q, p)) if n_outer else (lambda p, q: (p, q))
    inner = ni if n_outer else nj

    def body(a_ref, b_ref, *rest):
        tin = rest[:n_tin]
        tout = rest[n_tin + n_after:n_tin + n_after + n_tout]
        scratch = rest[n_tin + n_after + n_tout:]
        k = pl.program_id(2)
        visit = pl.program_id(0) * inner + pl.program_id(1)
        if keep_b:
            b_kept, b_sem = scratch[n_acc:n_acc + 2]

            @pl.when((visit == 0) & (k == 0))
            def _():
                cp = pltpu.make_async_copy(b_ref, b_kept, b_sem)
                cp.start()
                cp.wait()

            b_ref = b_kept

        def store(prod, c=0, cols=()):
            if epilogue is None:
                tout[0][...] = prod.astype(out_dtype)
            else:
                epilogue(prod, jnp.logical_and(visit == 0, c == 0), tin, tout, *cols)

        if col_chunks:
            for c, (c0, cw) in enumerate(col_chunks):
                b_part = b_ref[pl.ds(c0, cw), :] if mode == "NT" else b_ref[:, pl.ds(c0, cw)]
                store(lax.dot_general(a_ref[...], b_part, dims, preferred_element_type=F32), c, ((c0, cw),))
        else:
            part = lax.dot_general(a_ref[...], b_ref[...], dims, preferred_element_type=F32)
            if nk == 1:
                store(part)
            else:
                acc_ref = scratch[0]

                @pl.when(k == 0)
                def _():
                    acc_ref[...] = part

                @pl.when((k > 0) & (k < nk - 1))
                def _():
                    acc_ref[...] += part

                @pl.when(k == nk - 1)
                def _():
                    store(acc_ref[...] + part)

    def spec(shape, fn):
        return pl.BlockSpec(shape, lambda p, q, k: fn(*ij(p, q)))

    a_spec = (pl.BlockSpec((tk, tm), lambda p, q, k: (k, ij(p, q)[0])) if mode == "TN"
              else pl.BlockSpec((tm, tk), lambda p, q, k: (ij(p, q)[0], k)))
    if keep_b:
        b_spec = pl.BlockSpec(memory_space=pl.ANY)
    elif mode == "NT":
        b_spec = pl.BlockSpec((tn, tk), lambda p, q, k: (ij(p, q)[1], k))
    else:
        b_spec = pl.BlockSpec((tk, tn), lambda p, q, k: (k, ij(p, q)[1]))
    sequential = epilogue or keep_b
    out = pl.pallas_call(
        body, name=name, grid=(nj, ni, nk) if n_outer else (ni, nj, nk),
        out_shape=[t[0] for t in tiles_out],
        in_specs=([a_spec, b_spec] + [spec(t[1], t[2]) for t in tiles_in]
                  + [pl.BlockSpec(memory_space=pl.ANY)] * n_after),
        out_specs=[spec(t[1], t[2]) for t in tiles_out],
        scratch_shapes=([pltpu.VMEM((tm, tn), F32)] * n_acc
                        + ([pltpu.VMEM(b.shape, b.dtype), pltpu.SemaphoreType.DMA] if keep_b else [])),
        compiler_params=(_params("arbitrary", "arbitrary", "arbitrary") if sequential
                         else _params("parallel", "parallel", "arbitrary")),
    )(a, b, *[t[0] for t in tiles_in], *after)
    return out if epilogue else out[0]


def _row(tm, w, off=0):
    return pl.BlockSpec((tm, w), lambda i: (i, off))


def _vec(w):
    return pl.BlockSpec((1, w), lambda i: (0, 0))


def _sigmoid(x):
    return 0.5 * jnp.tanh(0.5 * x) + 0.5


def _normmod(name, x, g, sc, sh, tm=1024):
    s = x.shape[0]

    def body(x_ref, g_ref, sc_ref, sh_ref, h_ref):
        xv = x_ref[...]
        r = lax.rsqrt(jnp.mean(xv * xv, axis=-1, keepdims=True) + EPS)
        h_ref[...] = ((xv * r) * g_ref[...] * (1.0 + sc_ref[...]) + sh_ref[...]).astype(BF16)

    return pl.pallas_call(
        body, name=name, grid=(s // tm,),
        out_shape=jax.ShapeDtypeStruct((s, D), BF16),
        in_specs=[_row(tm, D), _vec(D), _vec(D), _vec(D)], out_specs=_row(tm, D),
        compiler_params=_params("parallel"),
    )(x, g, sc, sh)


def _heads(x, fn):
    return jnp.concatenate([fn(x[:, h * HD:(h + 1) * HD], h) for h in range(x.shape[1] // HD)], axis=1)


def _qknorm(proj, wqk, tm=512):
    s = proj.shape[0]

    def body(p_ref, w_ref, o_ref):
        pv = p_ref[...].astype(F32)
        wv = w_ref[...]

        def one(qh, h):
            r = lax.rsqrt(jnp.mean(qh * qh, axis=-1, keepdims=True) + EPS)
            return (qh * r) * wv[:, h * HD:(h + 1) * HD]

        o_ref[...] = _heads(pv, one).astype(BF16)

    return pl.pallas_call(
        body, name="qknorm", grid=(s // tm,),
        out_shape=jax.ShapeDtypeStruct((s, QKW), BF16),
        in_specs=[pl.BlockSpec((tm, QKW), lambda i: (i, 0)), pl.BlockSpec((1, QKW), lambda i: (0, 0))],
        out_specs=pl.BlockSpec((tm, QKW), lambda i: (i, 0)),
        compiler_params=_params("parallel"),
    )(proj, wqk)


def _qknorm_bwd(name, proj, dn, w, dproj, blk0, tm=512):
    s, width = dn.shape

    def body(p_ref, d_ref, w_ref, _, o_ref, acc_ref):
        pv = p_ref[...].astype(F32)
        dv = d_ref[...]
        wv = w_ref[...]
        sums = []

        def one(qh, h):
            dn = dv[:, h * HD:(h + 1) * HD]
            r = lax.rsqrt(jnp.mean(qh * qh, axis=-1, keepdims=True) + EPS)
            nh = qh * r
            sums.append(jnp.sum(dn * nh, axis=0, keepdims=True))
            dnw = dn * wv[:, h * HD:(h + 1) * HD]
            return r * (dnw - nh * jnp.mean(dnw * nh, axis=-1, keepdims=True))

        o_ref[...] = _heads(pv, one).astype(BF16)

        @pl.when(pl.program_id(0) == 0)
        def _():
            acc_ref[...] = jnp.zeros_like(acc_ref)

        acc_ref[0:1, :] += jnp.concatenate(sums, axis=1)

    return pl.pallas_call(
        body, name=name, grid=(s // tm,),
        out_shape=[jax.ShapeDtypeStruct((s, IN_W), BF16), jax.ShapeDtypeStruct((8, width), F32)],
        in_specs=[pl.BlockSpec((tm, width), lambda i: (i, blk0)), pl.BlockSpec((tm, width), lambda i: (i, 0)),
                  pl.BlockSpec((1, width), lambda i: (0, 0)), pl.BlockSpec(memory_space=pl.ANY)],
        out_specs=[pl.BlockSpec((tm, width), lambda i: (i, blk0)), pl.BlockSpec((8, width), lambda i: (0, 0))],
        input_output_aliases={3: 0},
        compiler_params=_params("arbitrary"),
    )(proj, dn, w, dproj)


def _attn_shapes(s, g):
    d = DILATIONS[g]
    tb = min(s, max(2048, 256 * d))
    sb = min(256, tb // d)
    pb = BAND * d
    assert s % tb == 0 and tb % pb == 0 and (tb // d) % sb == 0 and sb % BAND == 0
    return d, tb, sb, pb


def _lanes(x, width):
    return jnp.concatenate([x] * (width // HD), axis=1)


def _every(start, size, d):
    return pl.ds(start, size, stride=d) if d > 1 else pl.ds(start, size)


def _attn_specs(g, tb, pb, s, ahead):
    ratio = tb // pb
    if ahead:
        nbr = lambda n: jnp.minimum((n + 1) * ratio, s // pb - 1)
    else:
        nbr = lambda n: jnp.maximum(n * ratio - 1, 0)
    cur = lambda base: pl.BlockSpec((tb, HD), lambda h, n: (n, base + g * N_HEADS + h))
    side = lambda base: pl.BlockSpec((pb, HD), lambda h, n: (nbr(n), base + g * N_HEADS + h))
    tok = pl.BlockSpec((tb, HD), lambda h, n: (n, h))
    tok_side = pl.BlockSpec((pb, HD), lambda h, n: (nbr(n), h))
    return cur, side, tok, tok_side


Q_COL, K_COL, V_COL = 0, 12, 24


def _attn_fwd(g, qkn, proj):
    s = qkn.shape[0]
    d, tb, sb, pb = _attn_shapes(s, g)
    ft = F32 if d > 1 else BF16
    nj = tb // d // sb
    scale = HD ** -0.5

    def body(q_ref, kc_ref, kp_ref, vc_ref, vp_ref, o_ref, lse_ref, qf, kf, vf):
        n = pl.program_id(1)
        qf[...] = q_ref[...].astype(ft)
        kf[0:pb] = kp_ref[...].astype(ft)
        kf[pb:] = kc_ref[...].astype(ft)
        vf[0:pb] = vp_ref[...].astype(ft)
        vf[pb:] = vc_ref[...].astype(ft)
        for r in range(d):
            for j in range(nj):
                at = j * sb * d + r
                q = qf[_every(at, sb, d), :].astype(BF16)
                k = kf[_every(at, sb + BAND, d), :].astype(BF16)
                v = vf[_every(at, sb + BAND, d), :].astype(BF16)
                sc = lax.dot_general(q, k, NT_DIMS, preferred_element_type=F32) * scale
                qi = lax.broadcasted_iota(jnp.int32, sc.shape, 0)
                kj = lax.broadcasted_iota(jnp.int32, sc.shape, 1)
                valid = (kj >= qi) & (kj <= qi + BAND)
                if j == 0:
                    valid = valid & ((kj >= BAND) | (n > 0))
                sc = jnp.where(valid, sc, -1e30)
                m = jnp.max(sc, axis=-1, keepdims=True)
                p = jnp.exp(sc - m)
                l = jnp.sum(p, axis=-1, keepdims=True)
                o = lax.dot_general(p.astype(BF16), v, NN_DIMS, preferred_element_type=F32)
                o_ref[_every(at, sb, d), :] = o / l
                lse_ref[_every(at, sb, d), :] = jnp.broadcast_to(m + jnp.log(l), (sb, HD))

    cur, side, tok, _ = _attn_specs(g, tb, pb, s, ahead=False)
    return pl.pallas_call(
        body, name=f"attn_fwd_g{g}", grid=(N_HEADS, s // tb),
        out_shape=[jax.ShapeDtypeStruct((s, COL), F32)] * 2,
        in_specs=[cur(Q_COL), cur(K_COL), side(K_COL), cur(V_COL), side(V_COL)],
        out_specs=[tok, tok],
        scratch_shapes=[pltpu.VMEM((tb, HD), ft), pltpu.VMEM((tb + pb, HD), ft),
                        pltpu.VMEM((tb + pb, HD), ft)],
        compiler_params=_params("parallel", "arbitrary"),
    )(qkn, qkn, qkn, proj, proj)


def _attn_combine(os_, lses, tm=1024):
    s = os_[0].shape[0]

    def body(o0, o1, o2, l0, l1, l2, o_ref, lse_ref):
        a, b, c = l0[...], l1[...], l2[...]
        m = jnp.maximum(jnp.maximum(a, b), c)
        ea, eb, ec = jnp.exp(a - m), jnp.exp(b - m), jnp.exp(c - m)
        tot = ea + eb + ec
        o_ref[...] = ((ea * o0[...] + eb * o1[...] + ec * o2[...]) / tot).astype(BF16)
        lse_ref[...] = m + jnp.log(tot)

    return pl.pallas_call(
        body, name="attn_combine", grid=(s // tm,),
        out_shape=[jax.ShapeDtypeStruct((s, COL), BF16), jax.ShapeDtypeStruct((s, COL), F32)],
        in_specs=[_row(tm, COL)] * 6, out_specs=[_row(tm, COL)] * 2,
        compiler_params=_params("parallel"),
    )(*os_, *lses)


def _attn_bwd(g, qkn, proj, do, lse, delta, dqn, dkn, dproj):
    s = qkn.shape[0]
    d, tb, sb, pb = _attn_shapes(s, g)
    ft = F32 if d > 1 else BF16
    nj = tb // d // sb
    nt = s // tb
    scale = HD ** -0.5
    chained = dqn is not None

    def body(k_ref, v_ref, qc_ref, qn_ref, doc_ref, don_ref, lc_ref, ln_ref, dc_ref, dn_ref, *rest):
        dq_ref, dk_ref, dv_ref, kf, vf, qf, dvf, later = rest[-8:]
        n = pl.program_id(1)
        kf[...] = k_ref[...].astype(ft)
        vf[...] = v_ref[...].astype(ft)
        qf[0:tb] = qc_ref[...].astype(ft)
        qf[tb:] = qn_ref[...].astype(ft)

        @pl.when(n == 0)
        def _():
            later[...] = jnp.zeros_like(later)

        def window(c_ref, n_ref, r, j):
            at = j * sb * d + r
            if j < nj - 1:
                return c_ref[_every(at, sb + BAND, d), :]
            return jnp.concatenate([c_ref[_every(at, sb, d), :], n_ref[_every(r, BAND, d), :]], axis=0)

        for r in range(d):
            tail = later[r]
            for j in range(nj):
                at = j * sb * d + r
                rows = _every(at, sb, d)
                k = kf[rows, :].astype(BF16)
                v = vf[rows, :].astype(BF16)
                q = qf[_every(at, sb + BAND, d), :].astype(BF16)
                dov = window(doc_ref, don_ref, r, j).astype(BF16)
                sc = lax.dot_general(q, k, NT_DIMS, preferred_element_type=F32) * scale
                qi = lax.broadcasted_iota(jnp.int32, sc.shape, 0)
                kj = lax.broadcasted_iota(jnp.int32, sc.shape, 1)
                valid = (qi >= kj) & (qi <= kj + BAND)
                if j == nj - 1:
                    valid = valid & ((qi < sb) | (n < nt - 1))
                p = jnp.exp(jnp.where(valid, sc - _lanes(window(lc_ref, ln_ref, r, j), sb), -1e30))
                dp = lax.dot_general(dov, v, NT_DIMS, preferred_element_type=F32)
                ds = (p * (dp - _lanes(window(dc_ref, dn_ref, r, j), sb)) * scale).astype(BF16)
                dvf[rows, :] = lax.dot_general(p.astype(BF16), dov, TN_DIMS, preferred_element_type=F32)
                dk_ref[rows, :] = lax.dot_general(ds, q, TN_DIMS, preferred_element_type=F32)
                dqw = lax.dot_general(ds, k, NN_DIMS, preferred_element_type=F32)
                first = dqw[:BAND] + tail
                dq_ref[rows, :] = first if sb == BAND else jnp.concatenate([first, dqw[BAND:sb]], axis=0)
                tail = dqw[sb:]
            later[r] = tail
        dv_ref[...] = dvf[...].astype(BF16)

    cur, side, tok, tok_side = _attn_specs(g, tb, pb, s, ahead=True)
    anyspec = pl.BlockSpec(memory_space=pl.ANY)
    n_heads_cols = 3 * N_HEADS * HD
    return pl.pallas_call(
        body, name=f"attn_bwd_g{g}", grid=(N_HEADS, nt),
        out_shape=[jax.ShapeDtypeStruct((s, n_heads_cols), F32), jax.ShapeDtypeStruct((s, n_heads_cols), F32),
                   jax.ShapeDtypeStruct((s, IN_W), BF16)],
        in_specs=[cur(K_COL), cur(V_COL), cur(Q_COL), side(Q_COL), tok, tok_side, tok, tok_side,
                  tok, tok_side] + ([anyspec, anyspec] if chained else []) + [anyspec],
        out_specs=[cur(0), cur(0), cur(V_COL)],
        input_output_aliases={10: 0, 11: 1, 12: 2} if chained else {10: 2},
        scratch_shapes=[pltpu.VMEM((tb, HD), ft), pltpu.VMEM((tb, HD), ft),
                        pltpu.VMEM((tb + pb, HD), ft), pltpu.VMEM((tb, HD), F32),
                        pltpu.VMEM((d, BAND, HD), F32)],
        compiler_params=_params("arbitrary", "arbitrary"),
    )(qkn, proj, qkn, qkn, do, do, lse, lse, delta, delta, *([dqn, dkn] if chained else []), dproj)


def _shift_down(x, before, k):
    rolled = pltpu.roll(x, k, 0)
    head = jnp.where(lax.broadcasted_iota(jnp.int32, before.shape, 0) < k, pltpu.roll(before, k, 0), rolled[:8])
    return jnp.concatenate([head, rolled[8:]], axis=0)


def _shift_up(x, after, k):
    rows = x.shape[0]
    rolled = pltpu.roll(x, rows - k, 0)
    tail = jnp.where(lax.broadcasted_iota(jnp.int32, after.shape, 0) >= 8 - k,
                     pltpu.roll(after, 8 - k, 0), rolled[rows - 8:])
    return jnp.concatenate([rolled[:rows - 8], tail], axis=0)


def _conv_fwd(proj, cw, tm=1024):
    s = proj.shape[0]
    r16 = tm // 16

    def body(u_ref, b_ref, c_ref, up_ref, cp_ref, w_ref, z_ref):
        i = pl.program_id(1)
        xc = c_ref[...].astype(F32) * u_ref[...].astype(F32)
        xp = jnp.where(i > 0, cp_ref[8:16, :].astype(F32) * up_ref[8:16, :].astype(F32), 0.0)
        w = w_ref[...]
        conv = _shift_down(xc, xp, 2) * w[0:1] + _shift_down(xc, xp, 1) * w[1:2] + xc * w[2:3]
        z_ref[...] = (b_ref[...].astype(F32) * conv).astype(BF16)

    tile = lambda blk: pl.BlockSpec((tm, COL), lambda j, i: (i, blk + j))
    before = lambda blk: pl.BlockSpec((16, COL), lambda j, i: (jnp.maximum(i * r16 - 1, 0), blk + j))
    return pl.pallas_call(
        body, name="conv_fwd", grid=(D // COL, s // tm),
        out_shape=jax.ShapeDtypeStruct((s, D), BF16),
        in_specs=[tile(U_BLK), tile(B_BLK), tile(C_BLK), before(U_BLK), before(C_BLK),
                  pl.BlockSpec((3, COL), lambda j, i: (0, j))],
        out_specs=pl.BlockSpec((tm, COL), lambda j, i: (i, j)),
        compiler_params=_params("parallel", "parallel"),
    )(proj, proj, proj, proj, proj, cw)


def _conv_bwd(dz, proj, cw, dproj, tm=1024):
    s = proj.shape[0]
    r16 = tm // 16
    nrow = s // tm

    def body(dz_ref, u_ref, b_ref, c_ref, up_ref, cp_ref, dzn_ref, bn_ref, w_ref, _, o_ref, dc_ref, acc_ref):
        piece, i = pl.program_id(1), pl.program_id(2)
        u, c = u_ref[...].astype(F32), c_ref[...].astype(F32)
        bv = b_ref[...].astype(F32)
        dzv = dz_ref[...]
        w = w_ref[...]

        @pl.when((piece == 0) & (i == 0))
        def _():
            acc_ref[...] = jnp.zeros_like(acc_ref)

        @pl.when(piece == 0)
        def _():
            xc = c * u
            xp = jnp.where(i > 0, cp_ref[8:16, :].astype(F32) * up_ref[8:16, :].astype(F32), 0.0)
            x2, x1 = _shift_down(xc, xp, 2), _shift_down(xc, xp, 1)
            o_ref[...] = (dzv * (x2 * w[0:1] + x1 * w[1:2] + xc * w[2:3])).astype(BF16)
            dc_ref[...] = jnp.zeros_like(dc_ref)
            dconv = dzv * bv
            acc_ref[0:1, :] += jnp.sum(dconv * x2, axis=0, keepdims=True)
            acc_ref[1:2, :] += jnp.sum(dconv * x1, axis=0, keepdims=True)
            acc_ref[2:3, :] += jnp.sum(dconv * xc, axis=0, keepdims=True)

        @pl.when(piece == 1)
        def _():
            dconv = dzv * bv
            dn = jnp.where(i < nrow - 1, dzn_ref[...] * bn_ref[0:8, :].astype(F32), 0.0)
            dxc = dconv * w[2:3] + _shift_up(dconv, dn, 1) * w[1:2] + _shift_up(dconv, dn, 2) * w[0:1]
            o_ref[...] = (dxc * c).astype(BF16)
            dc_ref[...] = (dxc * u).astype(BF16)

    tile = lambda blk: pl.BlockSpec((tm, COL), lambda j, p, i: (i, blk + j))
    before = lambda blk: pl.BlockSpec((16, COL), lambda j, p, i: (jnp.maximum(i * r16 - 1, 0), blk + j))
    after = lambda rows, blk: pl.BlockSpec(
        (rows, COL), lambda j, p, i: (jnp.minimum((i + 1) * (tm // rows), s // rows - 1), blk + j))
    return pl.pallas_call(
        body, name="conv_bwd", grid=(D // COL, 2, nrow),
        out_shape=[jax.ShapeDtypeStruct((s, IN_W), BF16), jax.ShapeDtypeStruct((s + tm, D), BF16),
                   jax.ShapeDtypeStruct((8, D), F32)],
        in_specs=[tile(0), tile(U_BLK), tile(B_BLK), tile(C_BLK), before(U_BLK), before(C_BLK),
                  after(8, 0), after(16, B_BLK), pl.BlockSpec((3, COL), lambda j, p, i: (0, j)),
                  pl.BlockSpec(memory_space=pl.ANY)],
        out_specs=[pl.BlockSpec((tm, COL), lambda j, p, i: (i, jnp.where(p == 0, B_BLK, U_BLK) + j)),
                   pl.BlockSpec((tm, COL), lambda j, p, i: (jnp.where(p == 0, nrow, i), j)),
                   pl.BlockSpec((8, COL), lambda j, p, i: (0, j))],
        input_output_aliases={9: 0},
        compiler_params=_params("arbitrary", "arbitrary", "arbitrary"),
    )(dz, proj, proj, proj, proj, proj, dz, proj, cw, dproj)


def _copy_columns(name, src, dst, blk0, tm=2048):
    s, w = dst.shape[0], src.shape[1]
    fresh = isinstance(dst, jax.ShapeDtypeStruct)

    def body(x_ref, *rest):
        rest[-1][...] = x_ref[...]

    return pl.pallas_call(
        body, name=name, grid=(w // COL, s // tm),
        out_shape=jax.ShapeDtypeStruct(dst.shape, dst.dtype),
        in_specs=[pl.BlockSpec((tm, COL), lambda j, i: (i, j))] + ([] if fresh else [pl.BlockSpec(memory_space=pl.ANY)]),
        out_specs=pl.BlockSpec((tm, COL), lambda j, i: (i, blk0 + j)),
        input_output_aliases={} if fresh else {1: 0},
        compiler_params=_params("parallel", "parallel"),
    )(src, *([] if fresh else [dst]))


def _mod_part(c_all, w_ada, b_part):
    def body(c_ref, w_ref, b_ref, o_ref):
        cv = c_ref[...]
        act = cv * _sigmoid(cv)
        o_ref[...] = jnp.dot(act, w_ref[...], preferred_element_type=F32,
                             precision=lax.Precision.HIGHEST) + b_ref[...]

    return pl.pallas_call(
        body, name="mod_part", out_shape=jax.ShapeDtypeStruct((N_DEV, w_ada.shape[1]), F32),
    )(c_all, w_ada, b_part)


def _w_ada_grad(c_all_t, dmod_part):
    def body(c_ref, d_ref, o_ref):
        cv = c_ref[...]
        act = cv * _sigmoid(cv)
        dv = d_ref[...]
        acc = act[:, 0:1] * dv[0:1, :]
        for b in range(1, N_DEV):
            acc = acc + act[:, b:b + 1] * dv[b:b + 1, :]
        o_ref[...] = acc

    return pl.pallas_call(
        body, name="w_ada_grad", out_shape=jax.ShapeDtypeStruct((D, dmod_part.shape[1]), F32),
    )(c_all_t, dmod_part)


def _sum_rows(name, v):
    def body(v_ref, o_ref):
        acc = v_ref[0]
        for k in range(1, N_DEV):
            acc = acc + v_ref[k]
        o_ref[...] = acc

    return pl.pallas_call(body, name=name, out_shape=jax.ShapeDtypeStruct(v.shape[1:], F32))(v)


def _adamw(name, w, g, m, v):
    rows, cols = w.shape
    limit = max(16, (1 << 20) // (4 * cols))
    tr = rows if rows <= limit else next((t for t in range(limit - limit % 16, 15, -16) if rows % t == 0), rows)
    c1 = 1.0 - ADAM_B1 ** ADAM_STEP
    c2 = 1.0 - ADAM_B2 ** ADAM_STEP
    parts = g.ndim == 3

    def body(w_ref, g_ref, m_ref, v_ref, go_ref, d_ref, nm_ref, nv_ref):
        if parts:
            gv = g_ref[0].astype(F32)
            for k in range(1, N_DEV):
                gv = gv + g_ref[k].astype(F32)
        else:
            gv = g_ref[...]
        go_ref[...] = gv
        nm = ADAM_B1 * m_ref[...] + (1.0 - ADAM_B1) * gv
        nv = ADAM_B2 * v_ref[...] + (1.0 - ADAM_B2) * (gv * gv)
        nm_ref[...] = nm
        nv_ref[...] = nv
        d_ref[...] = -ADAM_LR * ((nm / c1) / (jnp.sqrt(nv / c2) + ADAM_EPS) + ADAM_WD * w_ref[...])

    spec = pl.BlockSpec((tr, cols), lambda i: (i, 0))
    g_spec = pl.BlockSpec((N_DEV, tr, cols), lambda i: (0, i, 0)) if parts else spec
    return pl.pallas_call(
        body, name=name, grid=(rows // tr,),
        out_shape=[jax.ShapeDtypeStruct((rows, cols), F32)] * 4,
        in_specs=[spec, g_spec, spec, spec], out_specs=[spec] * 4,
        compiler_params=_params("parallel"),
    )(w, g, m, v)


def _adamw_small(ws, gs, ms, vs):
    n = len(ws)
    c1 = 1.0 - ADAM_B1 ** ADAM_STEP
    c2 = 1.0 - ADAM_B2 ** ADAM_STEP

    def body(*refs):
        for i in range(n):
            w_ref, g_ref, m_ref, v_ref = refs[i], refs[n + i], refs[2 * n + i], refs[3 * n + i]
            d_ref, nm_ref, nv_ref = refs[4 * n + 3 * i:4 * n + 3 * i + 3]
            gv = g_ref[...]
            nm = ADAM_B1 * m_ref[...] + (1.0 - ADAM_B1) * gv
            nv = ADAM_B2 * v_ref[...] + (1.0 - ADAM_B2) * (gv * gv)
            nm_ref[...] = nm
            nv_ref[...] = nv
            d_ref[...] = -ADAM_LR * ((nm / c1) / (jnp.sqrt(nv / c2) + ADAM_EPS) + ADAM_WD * w_ref[...])

    outs = pl.pallas_call(
        body, name="adamw_small",
        out_shape=[jax.ShapeDtypeStruct(w.shape, F32) for w in ws for _ in range(3)],
    )(*ws, *gs, *ms, *vs)
    return [tuple(outs[3 * i:3 * i + 3]) for i in range(n)]


HALF = FF // 2


def _sds(shape, dtype):
    return jax.ShapeDtypeStruct(shape, dtype)


def _row_tile(w):
    return lambda tm: ((tm, w), lambda i, j: (i, 0))


def _one(w):
    return lambda rows: ((rows, w), lambda i, j: (0, 0))


def _gate_up_swiglu(name, h, wgu, tm=1024):
    s = h.shape[0]
    tm = min(tm, s)

    def epilogue(prod, first, tin, tout):
        pq_ref, s_ref = tout
        a, b = prod[:, :HALF], prod[:, HALF:]
        sig = _sigmoid(a)
        act = a * sig
        pq_ref[:, :HALF] = (b * (sig * (1.0 + a * (1.0 - sig)))).astype(BF16)
        pq_ref[:, HALF:] = act.astype(BF16)
        s_ref[...] = (act * b).astype(BF16)

    return _mm(name, h, wgu, "NT", None, tm, FF, D, n_outer=True, epilogue=epilogue,
               tiles_out=[(_sds((s, 2 * FF), BF16), (tm, FF), lambda i, j: (i, j)),
                          (_sds((s, FF), BF16), (tm, HALF), lambda i, j: (i, j))])


def _d_hidden_swiglu(name, df, wd, ab, after=(), tm=1024):
    s = df.shape[0]
    tm = min(tm, s)

    def epilogue(prod, first, tin, tout, cols):
        da_cols = slice(cols[0], cols[0] + cols[1])
        db_cols = slice(HALF + cols[0], HALF + cols[0] + cols[1])
        tout[0][:, da_cols] = (prod * tin[0][:, da_cols].astype(F32)).astype(BF16)
        tout[0][:, db_cols] = (prod * tin[0][:, db_cols].astype(F32)).astype(BF16)

    chunks = [(c0, min(384, HALF - c0)) for c0 in range(0, HALF, 384)]
    return _mm(name, df, wd, "NT", None, tm, HALF, D, n_outer=True, epilogue=epilogue, col_chunks=chunks, after=after,
               tiles_in=[(ab, (tm, FF), lambda i, j: (i, j))],
               tiles_out=[(_sds((s, 2 * FF), BF16), (tm, FF), lambda i, j: (i, j))])[0]


def _out_residual(name, a, w, x, gt, coef, nxt, tm=512, tk=FF):
    s = a.shape[0]
    tm = min(tm, s)

    def epilogue(prod, first, tin, tout):
        x_ref, gt_ref, g_ref, sc_ref, sh_ref = tin
        f_ref, xn_ref, h_ref = tout
        f_ref[...] = prod
        xn = x_ref[...] + (coef * gt_ref[...]) * prod
        xn_ref[...] = xn
        r = lax.rsqrt(jnp.mean(xn * xn, axis=-1, keepdims=True) + EPS)
        h_ref[...] = ((xn * r) * g_ref[...] * (1.0 + sc_ref[...]) + sh_ref[...]).astype(BF16)

    row, vec = _row_tile(D)(tm), _one(D)(1)
    return _mm(name, a, w, "NN", None, tm, D, tk, epilogue=epilogue,
               tiles_in=[(x, *row), (gt, *vec)] + [(v, *vec) for v in nxt],
               tiles_out=[(_sds((s, D), F32), *row), (_sds((s, D), F32), *row), (_sds((s, D), BF16), *row)])


def _out_loss(name, a, w, x, gt, coef, target, tm=512):
    s = a.shape[0]
    tm = min(tm, s)

    def epilogue(prod, first, tin, tout):
        x_ref, gt_ref, t_ref = tin
        f_ref, g_ref, df_ref, acc_ref = tout
        f_ref[...] = prod
        cg = coef * gt_ref[...]
        e = x_ref[...] + cg * prod - t_ref[...]
        gv = e * (1.0 / D)
        g_ref[...] = gv
        df_ref[...] = (cg * gv).astype(BF16)

        @pl.when(first)
        def _():
            acc_ref[...] = jnp.zeros_like(acc_ref)

        acc_ref[0:1, :] += coef * jnp.sum(gv * prod, axis=0, keepdims=True)
        acc_ref[1:2, :] += (0.5 / D) * jnp.sum(e * e, axis=0, keepdims=True)

    row, vec = _row_tile(D)(tm), _one(D)(1)
    return _mm(name, a, w, "NN", None, tm, D, FF, epilogue=epilogue,
               tiles_in=[(x, *row), (gt, *vec), (target, *row)],
               tiles_out=[(_sds((s, D), F32), *row), (_sds((s, D), F32), *row), (_sds((s, D), BF16), *row),
                          (_sds((8, D), F32), *_one(D)(8))])


def _d_h_norm_bwd(name, da, w, x, gin, g, sc, sh, before=None, after=(), tm=256):
    s = da.shape[0]
    tm = min(tm, s)
    coef = before[2] if before else None

    def epilogue(prod, first, tin, tout):
        x_ref, gin_ref, g_ref, sc_ref, sh_ref = tin[:5]
        gout_ref, acc_ref = tout[:2]
        xv = x_ref[...]
        r = lax.rsqrt(jnp.mean(xv * xv, axis=-1, keepdims=True) + EPS)
        nv = xv * r
        gv, one_sc = g_ref[...], 1.0 + sc_ref[...]
        dn = prod * gv * one_sc
        gout = gin_ref[...] + r * (dn - nv * jnp.mean(dn * nv, axis=-1, keepdims=True))
        gout_ref[...] = gout

        @pl.when(first)
        def _():
            acc_ref[...] = jnp.zeros_like(acc_ref)

        dhn = prod * nv
        acc_ref[0:1, :] += jnp.sum(prod, axis=0, keepdims=True)
        acc_ref[1:2, :] += jnp.sum(dhn * gv, axis=0, keepdims=True)
        acc_ref[2:3, :] += jnp.sum(dhn * one_sc, axis=0, keepdims=True)
        if before:
            f_ref, gt_ref = tin[5:]
            tout[2][...] = ((coef * gt_ref[...]) * gout).astype(BF16)
            acc_ref[3:4, :] += coef * jnp.sum(gout * f_ref[...], axis=0, keepdims=True)

    row, vec = _row_tile(D)(tm), _one(D)(1)
    tiles_in = [(x, *row), (gin, *row), (g, *vec), (sc, *vec), (sh, *vec)]
    tiles_out = [(_sds((s, D), F32), *row), (_sds((8, D), F32), *_one(D)(8))]
    if before:
        tiles_in += [(before[0], *row), (before[1], *vec)]
        tiles_out.append((_sds((s, D), BF16), *row))
    return _mm(name, da, w, "NN", None, tm, D, da.shape[1], epilogue=epilogue, keep_b=True, after=after,
               tiles_in=tiles_in, tiles_out=tiles_out)


def _gate_tiles(proj, tm):
    return [(proj, (tm, COL), (lambda i, j, blk=blk: (i, blk))) for blk in (GA_BLK, GA_BLK + 1, GC_BLK, GC_BLK + 1)]


def _conv_branch_merge(z, wc, ya, proj, tm=1024):
    s = z.shape[0]
    tm = min(tm, s)

    def epilogue(prod, first, tin, tout):
        ya_ref, ga0, ga1, gc0, gc1 = tin
        tout[0][...] = prod.astype(BF16)
        for half, (ga, gc) in enumerate(((ga0, gc0), (ga1, gc1))):
            cols = slice(half * COL, (half + 1) * COL)
            tout[1][:, cols] = (_sigmoid(ga[...].astype(F32)) * ya_ref[:, cols].astype(F32)
                                + _sigmoid(gc[...].astype(F32)) * prod[:, cols]).astype(BF16)

    row = _row_tile(D)(tm)
    return _mm("mix_conv_branch", z, wc, "NN", None, tm, D, D, epilogue=epilogue,
               tiles_in=[(ya, *row)] + _gate_tiles(proj, tm),
               tiles_out=[(_sds((s, D), BF16), *row), (_sds((s, D), BF16), *row)])


def _d_merged_branches(dmix, wo, ya, yc, proj, tm=1024):
    s = dmix.shape[0]
    tm = min(tm, s)

    def epilogue(prod, first, tin, tout):
        ya_ref, yc_ref, ga0, ga1, gc0, gc1 = tin
        dya_ref, dyc_ref, dg_ref = tout
        for half, (ga, gc) in enumerate(((ga0, gc0), (ga1, gc1))):
            cols = slice(half * COL, (half + 1) * COL)
            dm = prod[:, cols]
            for y_ref, g_ref, dy_ref, off in ((ya_ref, ga, dya_ref, 0), (yc_ref, gc, dyc_ref, D)):
                sig = _sigmoid(g_ref[...].astype(F32))
                dms = dm * sig
                dy_ref[:, cols] = dms.astype(BF16)
                dg_ref[:, off + half * COL:off + (half + 1) * COL] = (
                    dms * y_ref[:, cols].astype(F32) * (1.0 - sig)).astype(BF16)

    row = _row_tile(D)(tm)
    return _mm("mix_d_merged", dmix, wo, "NT", None, tm, D, D, epilogue=epilogue,
               tiles_in=[(ya, *row), (yc, *row)] + _gate_tiles(proj, tm),
               tiles_out=[(_sds((s, D), BF16), *row), (_sds((s, D), BF16), *row),
                          (_sds((s, 2 * D), BF16), *_row_tile(2 * D)(tm))])


def _d_o_delta(dya, wa_t, o, tm=1024):
    s = dya.shape[0]
    tm = min(tm, s)

    def epilogue(prod, first, tin, tout):
        tout[0][...] = prod
        tout[1][...] = _heads(prod * tin[0][...].astype(F32), lambda ph, h: jnp.broadcast_to(
            jnp.sum(ph, axis=-1, keepdims=True), ph.shape))

    row = _row_tile(COL)(tm)
    return _mm("mix_d_o", dya, wa_t, "NN", None, tm, COL, D, epilogue=epilogue,
               tiles_in=[(o, *row)], tiles_out=[(_sds((s, COL), F32), *row), (_sds((s, COL), F32), *row)])


def _ffn_bwd(tag, df, x, gin, h, ab, sw, g, sc, sh, wgu, wd, before=None, tk_dw=2048):
    dwd = _mm(f"{tag}_dw_down", sw, df, "TN", BF16, HALF, D, tk_dw)
    dab = _d_hidden_swiglu(f"{tag}_d_hidden", df, wd, ab, after=[dwd])
    dwgu = _mm(f"{tag}_dw_gate_up", dab, h, "TN", BF16, HALF, D, tk_dw)
    res = _d_h_norm_bwd(f"{tag}_d_h", dab, wgu, x, gin, g, sc, sh, before=before, after=[dwgu], tm=512)
    return res, dwgu, dwd


def kernel(x, c, w_ada, b_ada, norm_ffn1, ffn1_w_gate, ffn1_w_up, ffn1_w_down, norm_mix, w_in, q_norm, k_norm, conv_w, w_attn_branch, w_conv_branch, w_out, norm_ffn2, ffn2_w_gate, ffn2_w_up, ffn2_w_down, loss_target, m_w_ada, m_b_ada, m_norm_ffn1, m_ffn1_w_gate, m_ffn1_w_up, m_ffn1_w_down, m_norm_mix, m_w_in, m_q_norm, m_k_norm, m_conv_w, m_w_attn_branch, m_w_conv_branch, m_w_out, m_norm_ffn2, m_ffn2_w_gate, m_ffn2_w_up, m_ffn2_w_down, v_w_ada, v_b_ada, v_norm_ffn1, v_ffn1_w_gate, v_ffn1_w_up, v_ffn1_w_down, v_norm_mix, v_w_in, v_q_norm, v_k_norm, v_conv_w, v_w_attn_branch, v_w_conv_branch, v_w_out, v_norm_ffn2, v_ffn2_w_gate, v_ffn2_w_up, v_ffn2_w_down):
    me = 4 * lax.axis_index("x") + 2 * lax.axis_index("y") + lax.axis_index("c")
    x0, target = x[0], loss_target[0]
    s = x0.shape[0]
    ada_cols = w_ada.shape[2]
    cw_cols = conv_w.shape[2]

    gathered = _small_allgather(
        "gather_c_conv", jnp.concatenate([c, conv_w[0].reshape(1, 3 * cw_cols)], axis=1))[:, 0]
    c_all = gathered[:, :D]
    cw = gathered[:, D:].reshape(N_DEV, 3, cw_cols).transpose(1, 0, 2).reshape(3, D)
    b_part = lax.dynamic_slice(b_ada, (0, me * ada_cols), (1, ada_cols))
    mod_part = _mod_part(c_all, w_ada[0], b_part)
    mod_all = _small_allgather("gather_mod", mod_part.reshape(1, N_DEV * ada_cols))
    mod = lax.dynamic_slice(mod_all.reshape(N_DEV, N_DEV, ada_cols), (0, me, 0), (N_DEV, 1, ada_cols))
    mod = mod.reshape(N_MOD, 1, D)
    sh1, sc1, gt1, sh2, sc2, gt2, sh3, sc3, gt3 = [mod[i] for i in range(N_MOD)]

    tb = lambda w: w[0].T.astype(BF16)
    nb = lambda w: w[0].astype(BF16)
    ffn1_shards = [tb(ffn1_w_gate), tb(ffn1_w_up), nb(ffn1_w_down)]
    ffn2_shards = [tb(ffn2_w_gate), tb(ffn2_w_up), nb(ffn2_w_down)]
    mix_shards = [tb(w_in), tb(w_attn_branch), nb(w_conv_branch), nb(w_out)]
    ffn_dst, ffn_base, ffn_jump, ffn_shapes = [0, 0, 1], [0, HALF, 0], [HALF, HALF, 0], [(2 * FF, D), (FF, D)]
    mix_dst, mix_base, mix_shapes = [0, 1, 2, 3], [0, 0, 0, 0], [(IN_W, D), (D, COL), (D, D), (D, D)]
    (wgu1,) = _run_plan_on_sequencer(
        "gather_ffn1_gate_up", _gather_plan(ffn1_shards[:2], ffn_dst[:2], ffn_base[:2], ffn_shapes[:1], ffn_jump[:2]), 1)
    (wd1,) = _run_plan_on_sequencer(
        "gather_ffn1_down", _gather_plan(ffn1_shards[2:], [0], [0], ffn_shapes[1:]), 8)
    win_t, wa_t, wc, wo = _run_plan_on_sequencer(
        "gather_mix_weights", _gather_plan(mix_shards, mix_dst, mix_base, mix_shapes), 2)
    wgu2, wd2 = _run_plan_on_sequencer(
        "gather_ffn2_weights", _gather_plan(ffn2_shards, ffn_dst, ffn_base, ffn_shapes, ffn_jump), 3)

    h1 = _normmod("ffn1_normmod", x0, norm_ffn1, sc1, sh1)
    ab1, s1 = _gate_up_swiglu("ffn1_gate_up", h1, wgu1)
    f1, x1, h2 = _out_residual("ffn1_down", s1, wd1, x0, gt1, 0.5, (norm_mix, sc2, sh2))
    proj = _mm("mix_in_proj", h2, win_t, "NT", BF16, 1024, IN_W // 4, D, n_outer=True)
    wqk = jnp.concatenate([jnp.tile(q_norm, (1, 12)), jnp.tile(k_norm, (1, 12))], axis=1)
    qkn = _qknorm(proj, wqk)
    group_out = [_attn_fwd(g, qkn, proj) for g in range(3)]
    o, lse = _attn_combine([go[0] for go in group_out], [go[1] for go in group_out])
    ya = _mm("mix_attn_branch", o, wa_t, "NT", BF16, 1024, 1024, COL)
    z = _conv_fwd(proj, cw)
    yc, merged = _conv_branch_merge(z, wc, ya, proj)
    mix, x2, h3 = _out_residual("mix_out_proj", merged, wo, x1, gt2, 1.0, (norm_ffn2, sc3, sh3), tm=1024, tk=D)
    ab3, s3 = _gate_up_swiglu("ffn2_gate_up", h3, wgu2)
    f3, g3, df3, acc_out = _out_loss("ffn2_down", s3, wd2, x2, gt3, 0.5, target)
    loss_part = jnp.sum(acc_out[1])

    ffn_rows = [sh_.shape[0] for sh_ in ffn1_shards]
    mix_rows = [sh_.shape[0] for sh_ in mix_shards]
    (g2, acc3, dmix), dwgu2, dwd2 = _ffn_bwd(
        "ffn2", df3, x2, g3, h3, ab3, s3, norm_ffn2, sc3, sh3, wgu2, wd2, before=(mix, gt2, 1.0))
    dya, dyc, dgates = _d_merged_branches(dmix, wo, ya, yc, proj)
    dwo = _mm("mix_dw_out", merged, dmix, "TN", BF16, 1024, 1024, 2048)
    dproj = _copy_columns("dproj_gates", dgates, jax.ShapeDtypeStruct((s, IN_W), BF16), GA_BLK)
    dwc = _mm("mix_dw_conv_branch", z, dyc, "TN", BF16, 1024, 1024, 2048)
    dz = _mm("mix_d_z", dyc, wc, "NT", F32, 1024, 1024, D)
    dproj, d_c, cw_acc = _conv_bwd(dz, proj, cw, dproj)
    dproj = _copy_columns("copy_d_c", d_c, dproj, C_BLK)
    dwa_t = _mm("mix_dw_attn_branch", dya, o, "TN", BF16, 1024, COL, 2048)
    do, delta = _d_o_delta(dya, wa_t, o)
    dqn = dkn = None
    for g in range(3):
        dqn, dkn, dproj = _attn_bwd(g, qkn, proj, do, lse, delta, dqn, dkn, dproj)
    dproj, wq_acc = _qknorm_bwd("qnorm_bwd", proj, dqn, wqk[:, :QKW // 2], dproj, 0)
    dproj, wk_acc = _qknorm_bwd("knorm_bwd", proj, dkn, wqk[:, QKW // 2:], dproj, 1)
    r_f2g, r_f2u, r_f2d, r_wa, r_wc, r_wo = _run_plan_on_sequencer(
        "scatter_ffn2_and_branch_grads",
        _scatter_plan([dwgu2, dwd2, dwa_t, dwc, dwo], [0, 0, 1, 2, 3, 4], [0, HALF, 0, 0, 0, 0],
                      ffn_rows + mix_rows[1:], [D, D, D, COL, D, D], [HALF, HALF, 0, 0, 0, 0]), 4)
    dwin_t = _mm("mix_dw_in", dproj, h2, "TN", BF16, IN_W // 4, COL, 2048)
    (r_win,) = _run_plan_on_sequencer(
        "scatter_w_in_grad", _scatter_plan([dwin_t], [0], [0], mix_rows[:1], [D]), 5)
    g1, acc2, df1 = _d_h_norm_bwd("mix_d_h", dproj, win_t, x1, g2, norm_mix, sc2, sh2, before=(f1, gt1, 0.5),
                                  after=[dwin_t])
    dwd1 = _mm("ffn1_dw_down", s1, df1, "TN", BF16, HALF, D, 2048)
    (r_f1d,) = _run_plan_on_sequencer(
        "scatter_ffn1_down_grad", _scatter_plan([dwd1], [0], [0], ffn_rows[2:], [D]), 6)
    dab1 = _d_hidden_swiglu("ffn1_d_hidden", df1, wd1, ab1, after=[dwd1, r_win])
    dwgu1 = _mm("ffn1_dw_gate_up", dab1, h1, "TN", BF16, HALF, D, 2048)
    r_f1g, r_f1u = _run_plan_on_sequencer(
        "scatter_ffn1_gate_up_grads",
        _scatter_plan([dwgu1], [0, 0], [0, HALF], ffn_rows[:2], [D, D], [HALF, HALF]), 7)
    g0, acc1 = _d_h_norm_bwd("ffn1_d_h", dab1, wgu1, x0, g1, norm_ffn1, sc1, sh1, after=[dwgu1, r_f1d], tm=512)

    dqw = jnp.sum(wq_acc[0].reshape(12, HD), axis=0)
    dkw = jnp.sum(wk_acc[0].reshape(12, HD), axis=0)
    small = jnp.concatenate([
        acc1[0], acc1[1], acc2[3], acc2[0], acc2[1], acc3[3], acc3[0], acc3[1], acc_out[0],
        acc1[2], acc2[2], acc3[2], dqw, dkw, cw_acc[0:3].reshape(3 * D),
        jnp.zeros((HD,), F32).at[0].set(loss_part)]).reshape(1, -1)
    small_all = _small_allgather("gather_small_grads", small)
    small_sum = _sum_rows("sum_small_grads", small_all)[0]
    n_mod = N_MOD * D
    g_b_ada = small_sum[:n_mod].reshape(1, n_mod)
    g_norm1, g_norm2, g_norm3 = [small_sum[n_mod + i * D:n_mod + (i + 1) * D].reshape(1, D) for i in range(3)]
    off = n_mod + 3 * D
    g_qn, g_kn = small_sum[off:off + HD].reshape(1, HD), small_sum[off + HD:off + 2 * HD].reshape(1, HD)
    g_cw_full = small_sum[off + 2 * HD:off + 2 * HD + 3 * D].reshape(3, D)
    loss = small_sum[off + 2 * HD + 3 * D]
    g_cw = lax.dynamic_slice(g_cw_full, (0, me * cw_cols), (3, cw_cols))
    dmod_part = lax.dynamic_slice(small_all[:, 0, :n_mod], (0, me * ada_cols), (N_DEV, ada_cols))
    g_w_ada = _w_ada_grad(c_all.T, dmod_part)

    as_rows = {"ffn1_w_gate", "ffn1_w_up", "w_in", "w_attn_branch", "ffn2_w_gate", "ffn2_w_up"}
    grad_list = [g_w_ada, g_b_ada, g_norm1, r_f1g, r_f1u, r_f1d, g_norm2, r_win,
                 g_qn, g_kn, g_cw, r_wa, r_wc, r_wo, g_norm3, r_f2g, r_f2u, r_f2d]
    weights = [w_ada, b_ada, norm_ffn1, ffn1_w_gate, ffn1_w_up, ffn1_w_down, norm_mix, w_in, q_norm, k_norm,
               conv_w, w_attn_branch, w_conv_branch, w_out, norm_ffn2, ffn2_w_gate, ffn2_w_up, ffn2_w_down]
    ms = [m_w_ada, m_b_ada, m_norm_ffn1, m_ffn1_w_gate, m_ffn1_w_up, m_ffn1_w_down, m_norm_mix, m_w_in, m_q_norm,
          m_k_norm, m_conv_w, m_w_attn_branch, m_w_conv_branch, m_w_out, m_norm_ffn2, m_ffn2_w_gate,
          m_ffn2_w_up, m_ffn2_w_down]
    vs = [v_w_ada, v_b_ada, v_norm_ffn1, v_ffn1_w_gate, v_ffn1_w_up, v_ffn1_w_down, v_norm_mix, v_w_in, v_q_norm,
          v_k_norm, v_conv_w, v_w_attn_branch, v_w_conv_branch, v_w_out, v_norm_ffn2, v_ffn2_w_gate,
          v_ffn2_w_up, v_ffn2_w_down]
    wnames = ["w_ada", "b_ada", "norm_ffn1", "ffn1_w_gate", "ffn1_w_up", "ffn1_w_down", "norm_mix", "w_in",
              "q_norm", "k_norm", "conv_w", "w_attn_branch", "w_conv_branch", "w_out", "norm_ffn2",
              "ffn2_w_gate", "ffn2_w_up", "ffn2_w_down"]
    small = [i for i, gr in enumerate(grad_list) if gr.ndim == 2 and gr.size <= 16384]
    flat = lambda a, i: a.reshape(-1, weights[i].shape[-1])
    small_res = dict(zip(small, _adamw_small(
        [flat(weights[i], i) for i in small], [flat(grad_list[i], i) for i in small],
        [flat(ms[i], i) for i in small], [flat(vs[i], i) for i in small])))
    grad_out, deltas, new_ms, new_vs = [], [], [], []
    for idx, (nm, w, gr, m_, v_) in enumerate(zip(wnames, weights, grad_list, ms, vs)):
        if idx in small_res:
            gr, dl, nm_, nv_ = [r.reshape(w.shape) for r in (gr, *small_res[idx])]
        elif nm in as_rows:
            res = _adamw(f"adamw_{nm}", w[0].T, gr, m_[0].T, v_[0].T)
            gr, dl, nm_, nv_ = [r.T[None] for r in res]
        else:
            two_d = (-1, w.shape[-1])
            res = _adamw(f"adamw_{nm}", w.reshape(two_d), gr if gr.ndim == 3 else gr.reshape(two_d),
                         m_.reshape(two_d), v_.reshape(two_d))
            gr, dl, nm_, nv_ = [r.reshape(w.shape) for r in res]
        grad_out.append(gr)
        deltas.append(dl)
        new_ms.append(nm_)
        new_vs.append(nv_)
    return (loss, g0[None], *grad_out, *deltas, *new_ms, *new_vs)
```

```python
import jax
import jax.numpy as jnp
from jax import lax
from jax.experimental import pallas as pl
from jax.experimental.pallas import tpu as pltpu
from jax.experimental.pallas import tpu_sc as plsc

F32 = jnp.float32
BF16 = jnp.bfloat16
MESH = pl.DeviceIdType.MESH

N_DEV = 8
D = 1024
FF = 2816
HD = 128
N_HEADS = 4
DILATIONS = (1, 4, 16)
BAND = 128
QKW = 2 * 3 * N_HEADS * HD
IN_W = 9728
COL = 512
V_BLK, U_BLK, B_BLK, C_BLK, GA_BLK, GC_BLK = 6, 9, 11, 13, 15, 17
EPS = 1e-6
N_MOD = 9
ADAM_LR, ADAM_B1, ADAM_B2, ADAM_EPS, ADAM_WD, ADAM_STEP = 0.001, 0.9, 0.999, 1e-08, 0.01, 10

NT_DIMS = (((1,), (1,)), ((), ()))
TN_DIMS = (((0,), (0,)), ((), ()))
NN_DIMS = (((1,), (0,)), ((), ()))


def _place():
    return lax.axis_index("x"), lax.axis_index("y"), lax.axis_index("c")


def _flip(coord, bit):
    return 1 - coord if bit else coord


def _params(*sem):
    return pltpu.CompilerParams(dimension_semantics=sem)


def _small_allgather(name, v):
    n = v.shape[-1]

    def body(v_ref, out_ref, send_sems, recv_sems):
        x, y, c = _place()
        me = 4 * x + 2 * y + c
        out_ref[me] = v_ref[...]
        copies = []
        for k in range(1, N_DEV):
            peer = (_flip(x, (k >> 2) & 1), _flip(y, (k >> 1) & 1), _flip(c, k & 1))
            cp = pltpu.make_async_remote_copy(
                src_ref=v_ref, dst_ref=out_ref.at[me], send_sem=send_sems.at[k - 1],
                recv_sem=recv_sems.at[k - 1], device_id=peer, device_id_type=MESH)
            cp.start()
            copies.append(cp)
        for cp in copies:
            cp.wait()

    return pl.pallas_call(
        body, name=name,
        out_shape=jax.ShapeDtypeStruct((N_DEV, 1, n), F32),
        in_specs=[pl.BlockSpec(memory_space=pltpu.VMEM)],
        out_specs=pl.BlockSpec(memory_space=pltpu.VMEM),
        scratch_shapes=[pltpu.SemaphoreType.DMA((N_DEV - 1,)), pltpu.SemaphoreType.DMA((N_DEV - 1,))],
    )(v)


class _Plan:
    def __init__(self, operands, out_shapes, sems, phases):
        self.operands, self.out_shapes, self.sems, self.phases = operands, out_shapes, sems, phases


def _slab_start(base, rows, jump, idx):
    return pl.multiple_of(base + idx * rows + (idx // 4) * jump, 16)


def _gather_plan(shards, dst_of, base_of, dst_shapes, jump_of=None):
    n = len(shards)
    rows = [s.shape[0] for s in shards]
    jump_of = jump_of or [0] * n

    def phases(srcs, dsts, sems):
        send_sems, recv_sems, local_sems = sems
        x, y, c = _place()
        me, sibling = (x, y, c), (x, y, 1 - c)
        chips = [(1 - x, y), (x, 1 - y), (1 - x, 1 - y)]

        def slab(i, px, py, pc):
            start = _slab_start(base_of[i], rows[i], jump_of[i], 4 * px + 2 * py + pc)
            return dsts[dst_of[i]].at[pl.ds(start, rows[i])]

        def copy(i, k, block, to, src=None):
            return pltpu.make_async_remote_copy(
                src_ref=slab(i, *block) if src is None else src, dst_ref=slab(i, *block),
                send_sem=send_sems.at[i, k], recv_sem=recv_sems.at[i, k],
                device_id=to, device_id_type=MESH)

        def mine():
            return [pltpu.make_async_copy(srcs[i], slab(i, *me), local_sems.at[i]) for i in range(n)]

        def first():
            out = []
            for i in range(n):
                out.append(copy(i, 0, me, sibling, src=srcs[i]))
                out += [copy(i, 1 + j, me, (*chip, c), src=srcs[i]) for j, chip in enumerate(chips)]
            return out

        def passed():
            return [(copy(i, 1 + j, (*chip, c), me), copy(i, 4 + j, (*chip, c), sibling))
                    for j, chip in enumerate(chips) for i in range(n)]

        def start():
            for cp in mine() + first():
                cp.start()

        def middle():
            for landed, onward in passed():
                landed.wait_recv()
                onward.start()

        def finish():
            for i in range(n):
                copy(i, 0, sibling, me).wait_recv()
                for j, chip in enumerate(chips):
                    copy(i, 4 + j, (*chip, 1 - c), me).wait_recv()
            for cp in first() + [onward for _, onward in passed()]:
                cp.wait_send()
            for cp in mine():
                cp.wait()

        return start, middle, finish

    sems = [pltpu.SemaphoreType.DMA((n, 7)), pltpu.SemaphoreType.DMA((n, 7)), pltpu.SemaphoreType.DMA((n,))]
    return _Plan(list(shards), [jax.ShapeDtypeStruct(s, BF16) for s in dst_shapes], sems, phases)


def _scatter_plan(grads, src_of, base_of, rows, cols, jump_of=None):
    n = len(rows)
    jump_of = jump_of or [0] * n

    def phases(srcs, recvs, sems):
        send_sems, recv_sems, local_sems = sems
        x, y, c = _place()
        me = 4 * x + 2 * y + c

        def slab(i, idx):
            start = _slab_start(base_of[i], rows[i], jump_of[i], idx)
            return srcs[src_of[i]].at[pl.ds(start, rows[i])]

        def copies():
            out = [pltpu.make_async_copy(slab(i, me), recvs[i].at[me], local_sems.at[i]) for i in range(n)]
            for k in range(1, N_DEV):
                px, py, pc = _flip(x, (k >> 2) & 1), _flip(y, (k >> 1) & 1), _flip(c, k & 1)
                out += [pltpu.make_async_remote_copy(
                    src_ref=slab(i, 4 * px + 2 * py + pc), dst_ref=recvs[i].at[me],
                    send_sem=send_sems.at[i, k - 1], recv_sem=recv_sems.at[i, k - 1],
                    device_id=(px, py, pc), device_id_type=MESH) for i in range(n)]
            return out

        def start():
            for cp in copies():
                cp.start()

        def finish():
            for cp in copies():
                cp.wait()

        return start, None, finish

    sems = [pltpu.SemaphoreType.DMA((n, 7)), pltpu.SemaphoreType.DMA((n, 7)), pltpu.SemaphoreType.DMA((n,))]
    out_shapes = [jax.ShapeDtypeStruct((N_DEV, rows[i], cols[i]), BF16) for i in range(n)]
    return _Plan(list(grads), out_shapes, sems, phases)


def _run_plan_on_sequencer(name, plan, collective_id):
    src_refs = [jax.new_ref(a, memory_space=pltpu.MemorySpace.HBM) for a in plan.operands]
    dst_refs = [jax.empty_ref(s, memory_space=pltpu.MemorySpace.HBM) for s in plan.out_shapes]

    @pl.kernel(mesh=plsc.ScalarSubcoreMesh(axis_name="sequencer", num_cores=1), name=name,
               scratch_types=tuple(plan.sems),
               compiler_params=pltpu.CompilerParams(collective_id=collective_id))
    def launch(*sems):
        x, y, c = _place()
        barrier = pltpu.get_barrier_semaphore()
        for k in range(1, N_DEV):
            peer = (_flip(x, (k >> 2) & 1), _flip(y, (k >> 1) & 1), _flip(c, k & 1))
            pl.semaphore_signal(barrier, inc=1, device_id=peer, device_id_type=MESH)
        pl.semaphore_wait(barrier, N_DEV - 1)
        for phase in plan.phases(src_refs, dst_refs, sems):
            if phase is not None:
                phase()

    launch()
    return [r[...] for r in dst_refs]


def _mm(name, a, b, mode, out_dtype, tm, tn, tk, *, tiles_in=(), tiles_out=(), epilogue=None,
        n_outer=False, keep_b=False, col_chunks=None, after=()):
    if mode == "TN":
        kk, m = a.shape
    else:
        m, kk = a.shape
    n = b.shape[0] if mode == "NT" else b.shape[1]
    tm, tn, tk = min(tm, m), min(tn, n), min(tk, kk)
    assert m % tm == 0 and n % tn == 0 and kk % tk == 0, (name, m, n, kk, tm, tn, tk)
    ni, nj, nk = m // tm, n // tn, kk // tk
    dims = {"NN": NN_DIMS, "NT": NT_DIMS, "TN": TN_DIMS}[mode]
    if epilogue is None:
        tiles_out = [(jax.ShapeDtypeStruct((m, n), out_dtype), (tm, tn), lambda i, j: (i, j))]
    n_tin, n_tout = len(tiles_in), len(tiles_out)
    n_acc = 1 if nk > 1 else 0
    n_after = len(after)
    assert not keep_b or (nk == 1 and nj == 1)
    assert not col_chunks or (epilogue is not None and nk == 1 and mode != "TN")
    ij = (lambda p, q: (q, p)) if n_outer else (lambda p, q: (p, q))
    inner = ni if n_outer else nj

    def body(a_ref, b_ref, *rest):
        tin = rest[:n_tin]
        tout = rest[n_tin + n_after:n_tin + n_after + n_tout]
        scratch = rest[n_tin + n_after + n_tout:]
        k = pl.program_id(2)
        visit = pl.program_id(0) * inner + pl.program_id(1)
        if keep_b:
            b_kept, b_sem = scratch[n_acc:n_acc + 2]

            @pl.when((visit == 0) & (k == 0))
            def _():
                cp = pltpu.make_async_copy(b_ref, b_kept, b_sem)
                cp.start()
                cp.wait()

            b_ref = b_kept

        def store(prod, c=0, cols=()):
            if epilogue is None:
                tout[0][...] = prod.astype(out_dtype)
            else:
                epilogue(prod, jnp.logical_and(visit == 0, c == 0), tin, tout, *cols)

        if col_chunks:
            for c, (c0, cw) in enumerate(col_chunks):
                b_part = b_ref[pl.ds(c0, cw), :] if mode == "NT" else b_ref[:, pl.ds(c0, cw)]
                store(lax.dot_general(a_ref[...], b_part, dims, preferred_element_type=F32), c, ((c0, cw),))
        else:
            part = lax.dot_general(a_ref[...], b_ref[...], dims, preferred_element_type=F32)
            if nk == 1:
                store(part)
            else:
                acc_ref = scratch[0]

                @pl.when(k == 0)
                def _():
                    acc_ref[...] = part

                @pl.when((k > 0) & (k < nk - 1))
                def _():
                    acc_ref[...] += part

                @pl.when(k == nk - 1)
                def _():
                    store(acc_ref[...] + part)

    def spec(shape, fn):
        return pl.BlockSpec(shape, lambda p, q, k: fn(*ij(p, q)))

    a_spec = (pl.BlockSpec((tk, tm), lambda p, q, k: (k, ij(p, q)[0])) if mode == "TN"
              else pl.BlockSpec((tm, tk), lambda p, q, k: (ij(p, q)[0], k)))
    if keep_b:
        b_spec = pl.BlockSpec(memory_space=pl.ANY)
    elif mode == "NT":
        b_spec = pl.BlockSpec((tn, tk), lambda p, q, k: (ij(p, q)[1], k))
    else:
        b_spec = pl.BlockSpec((tk, tn), lambda p, q, k: (k, ij(p, q)[1]))
    sequential = epilogue or keep_b
    out = pl.pallas_call(
        body, name=name, grid=(nj, ni, nk) if n_outer else (ni, nj, nk),
        out_shape=[t[0] for t in tiles_out],
        in_specs=([a_spec, b_spec] + [spec(t[1], t[2]) for t in tiles_in]
                  + [pl.BlockSpec(memory_space=pl.ANY)] * n_after),
        out_specs=[spec(t[1], t[2]) for t in tiles_out],
        scratch_shapes=([pltpu.VMEM((tm, tn), F32)] * n_acc
                        + ([pltpu.VMEM(b.shape, b.dtype), pltpu.SemaphoreType.DMA] if keep_b else [])),
        compiler_params=(_params("arbitrary", "arbitrary", "arbitrary") if sequential
                         else _params("parallel", "parallel", "arbitrary")),
    )(a, b, *[t[0] for t in tiles_in], *after)
    return out if epilogue else out[0]


def _row(tm, w, off=0):
    return pl.BlockSpec((tm, w), lambda i: (i, off))


def _vec(w):
    return pl.BlockSpec((1, w), lambda i: (0, 0))


def _sigmoid(x):
    return 0.5 * jnp.tanh(0.5 * x) + 0.5


def _normmod(name, x, g, sc, sh, tm=1024):
    s = x.shape[0]

    def body(x_ref, g_ref, sc_ref, sh_ref, h_ref):
        xv = x_ref[...]
        r = lax.rsqrt(jnp.mean(xv * xv, axis=-1, keepdims=True) + EPS)
        h_ref[...] = ((xv * r) * g_ref[...] * (1.0 + sc_ref[...]) + sh_ref[...]).astype(BF16)

    return pl.pallas_call(
        body, name=name, grid=(s // tm,),
        out_shape=jax.ShapeDtypeStruct((s, D), BF16),
        in_specs=[_row(tm, D), _vec(D), _vec(D), _vec(D)], out_specs=_row(tm, D),
        compiler_params=_params("parallel"),
    )(x, g, sc, sh)


def _heads(x, fn):
    return jnp.concatenate([fn(x[:, h * HD:(h + 1) * HD], h) for h in range(x.shape[1] // HD)], axis=1)


def _qknorm(proj, wqk, tm=512):
    s = proj.shape[0]

    def body(p_ref, w_ref, o_ref):
        pv = p_ref[...].astype(F32)
        wv = w_ref[...]

        def one(qh, h):
            r = lax.rsqrt(jnp.mean(qh * qh, axis=-1, keepdims=True) + EPS)
            return (qh * r) * wv[:, h * HD:(h + 1) * HD]

        o_ref[...] = _heads(pv, one).astype(BF16)

    return pl.pallas_call(
        body, name="qknorm", grid=(s // tm,),
        out_shape=jax.ShapeDtypeStruct((s, QKW), BF16),
        in_specs=[pl.BlockSpec((tm, QKW), lambda i: (i, 0)), pl.BlockSpec((1, QKW), lambda i: (0, 0))],
        out_specs=pl.BlockSpec((tm, QKW), lambda i: (i, 0)),
        compiler_params=_params("parallel"),
    )(proj, wqk)


def _qknorm_bwd(name, proj, dn, w, dproj, blk0, tm=512):
    s, width = dn.shape

    def body(p_ref, d_ref, w_ref, _, o_ref, acc_ref):
        pv = p_ref[...].astype(F32)
        dv = d_ref[...]
        wv = w_ref[...]
        sums = []

        def one(qh, h):
            dn = dv[:, h * HD:(h + 1) * HD]
            r = lax.rsqrt(jnp.mean(qh * qh, axis=-1, keepdims=True) + EPS)
            nh = qh * r
            sums.append(jnp.sum(dn * nh, axis=0, keepdims=True))
            dnw = dn * wv[:, h * HD:(h + 1) * HD]
            return r * (dnw - nh * jnp.mean(dnw * nh, axis=-1, keepdims=True))

        o_ref[...] = _heads(pv, one).astype(BF16)

        @pl.when(pl.program_id(0) == 0)
        def _():
            acc_ref[...] = jnp.zeros_like(acc_ref)

        acc_ref[0:1, :] += jnp.concatenate(sums, axis=1)

    return pl.pallas_call(
        body, name=name, grid=(s // tm,),
        out_shape=[jax.ShapeDtypeStruct((s, IN_W), BF16), jax.ShapeDtypeStruct((8, width), F32)],
        in_specs=[pl.BlockSpec((tm, width), lambda i: (i, blk0)), pl.BlockSpec((tm, width), lambda i: (i, 0)),
                  pl.BlockSpec((1, width), lambda i: (0, 0)), pl.BlockSpec(memory_space=pl.ANY)],
        out_specs=[pl.BlockSpec((tm, width), lambda i: (i, blk0)), pl.BlockSpec((8, width), lambda i: (0, 0))],
        input_output_aliases={3: 0},
        compiler_params=_params("arbitrary"),
    )(proj, dn, w, dproj)


def _attn_shapes(s, g):
    d = DILATIONS[g]
    tb = min(s, 4096)
    sb = min(256, tb // d)
    pb = BAND * d
    assert s % tb == 0 and tb % pb == 0 and (tb // d) % sb == 0 and sb % BAND == 0
    return d, tb, sb, pb


def _lanes(x, width):
    return jnp.concatenate([x] * (width // HD), axis=1)


def _every(start, size, d):
    return pl.ds(start, size, stride=d) if d > 1 else pl.ds(start, size)


def _attn_specs(g, tb, pb, s, ahead):
    ratio = tb // pb
    if ahead:
        nbr = lambda n: jnp.minimum((n + 1) * ratio, s // pb - 1)
    else:
        nbr = lambda n: jnp.maximum(n * ratio - 1, 0)
    cur = lambda base: pl.BlockSpec((tb, HD), lambda h, n: (n, base + g * N_HEADS + h))
    side = lambda base: pl.BlockSpec((pb, HD), lambda h, n: (nbr(n), base + g * N_HEADS + h))
    tok = pl.BlockSpec((tb, HD), lambda h, n: (n, h))
    tok_side = pl.BlockSpec((pb, HD), lambda h, n: (nbr(n), h))
    return cur, side, tok, tok_side


Q_COL, K_COL, V_COL = 0, 12, 24


def _attn_fwd(g, qkn, proj):
    s = qkn.shape[0]
    d, tb, sb, pb = _attn_shapes(s, g)
    ft = F32 if d > 1 else BF16
    nj = tb // d // sb
    scale = HD ** -0.5

    def body(q_ref, kc_ref, kp_ref, vc_ref, vp_ref, o_ref, lse_ref, qf, kf, vf):
        n = pl.program_id(1)
        qf[...] = q_ref[...].astype(ft)
        kf[0:pb] = kp_ref[...].astype(ft)
        kf[pb:] = kc_ref[...].astype(ft)
        vf[0:pb] = vp_ref[...].astype(ft)
        vf[pb:] = vc_ref[...].astype(ft)
        for r in range(d):
            for j in range(nj):
                at = j * sb * d + r
                q = qf[_every(at, sb, d), :].astype(BF16)
                k = kf[_every(at, sb + BAND, d), :].astype(BF16)
                v = vf[_every(at, sb + BAND, d), :].astype(BF16)
                sc = lax.dot_general(q, k, NT_DIMS, preferred_element_type=F32) * scale
                qi = lax.broadcasted_iota(jnp.int32, sc.shape, 0)
                kj = lax.broadcasted_iota(jnp.int32, sc.shape, 1)
                valid = (kj >= qi) & (kj <= qi + BAND)
                if j == 0:
                    valid = valid & ((kj >= BAND) | (n > 0))
                sc = jnp.where(valid, sc, -1e30)
                m = jnp.max(sc, axis=-1, keepdims=True)
                p = jnp.exp(sc - m)
                l = jnp.sum(p, axis=-1, keepdims=True)
                o = lax.dot_general(p.astype(BF16), v, NN_DIMS, preferred_element_type=F32)
                o_ref[_every(at, sb, d), :] = o / l
                lse_ref[_every(at, sb, d), :] = jnp.broadcast_to(m + jnp.log(l), (sb, HD))

    cur, side, tok, _ = _attn_specs(g, tb, pb, s, ahead=False)
    return pl.pallas_call(
        body, name=f"attn_fwd_g{g}", grid=(N_HEADS, s // tb),
        out_shape=[jax.ShapeDtypeStruct((s, COL), F32)] * 2,
        in_specs=[cur(Q_COL), cur(K_COL), side(K_COL), cur(V_COL), side(V_COL)],
        out_specs=[tok, tok],
        scratch_shapes=[pltpu.VMEM((tb, HD), ft), pltpu.VMEM((tb + pb, HD), ft),
                        pltpu.VMEM((tb + pb, HD), ft)],
        compiler_params=_params("parallel", "arbitrary"),
    )(qkn, qkn, qkn, proj, proj)


def _attn_combine(os_, lses, tm=1024):
    s = os_[0].shape[0]

    def body(o0, o1, o2, l0, l1, l2, o_ref, lse_ref):
        a, b, c = l0[...], l1[...], l2[...]
        m = jnp.maximum(jnp.maximum(a, b), c)
        ea, eb, ec = jnp.exp(a - m), jnp.exp(b - m), jnp.exp(c - m)
        tot = ea + eb + ec
        o_ref[...] = ((ea * o0[...] + eb * o1[...] + ec * o2[...]) / tot).astype(BF16)
        lse_ref[...] = m + jnp.log(tot)

    return pl.pallas_call(
        body, name="attn_combine", grid=(s // tm,),
        out_shape=[jax.ShapeDtypeStruct((s, COL), BF16), jax.ShapeDtypeStruct((s, COL), F32)],
        in_specs=[_row(tm, COL)] * 6, out_specs=[_row(tm, COL)] * 2,
        compiler_params=_params("parallel"),
    )(*os_, *lses)


def _attn_bwd(g, qkn, proj, do, lse, delta, dqn, dkn, dproj):
    s = qkn.shape[0]
    d, tb, sb, pb = _attn_shapes(s, g)
    ft = F32 if d > 1 else BF16
    nj = tb // d // sb
    nt = s // tb
    scale = HD ** -0.5
    chained = dqn is not None

    def body(k_ref, v_ref, qc_ref, qn_ref, doc_ref, don_ref, lc_ref, ln_ref, dc_ref, dn_ref, *rest):
        dq_ref, dk_ref, dv_ref, kf, vf, qf, dvf, later = rest[-8:]
        n = pl.program_id(1)
        kf[...] = k_ref[...].astype(ft)
        vf[...] = v_ref[...].astype(ft)
        qf[0:tb] = qc_ref[...].astype(ft)
        qf[tb:] = qn_ref[...].astype(ft)

        @pl.when(n == 0)
        def _():
            later[...] = jnp.zeros_like(later)

        def window(c_ref, n_ref, r, j):
            at = j * sb * d + r
            if j < nj - 1:
                return c_ref[_every(at, sb + BAND, d), :]
            return jnp.concatenate([c_ref[_every(at, sb, d), :], n_ref[_every(r, BAND, d), :]], axis=0)

        for r in range(d):
            tail = later[r]
            for j in range(nj):
                at = j * sb * d + r
                rows = _every(at, sb, d)
                k = kf[rows, :].astype(BF16)
                v = vf[rows, :].astype(BF16)
                q = qf[_every(at, sb + BAND, d), :].astype(BF16)
                dov = window(doc_ref, don_ref, r, j).astype(BF16)
                sc = lax.dot_general(q, k, NT_DIMS, preferred_element_type=F32) * scale
                qi = lax.broadcasted_iota(jnp.int32, sc.shape, 0)
                kj = lax.broadcasted_iota(jnp.int32, sc.shape, 1)
                valid = (qi >= kj) & (qi <= kj + BAND)
                if j == nj - 1:
                    valid = valid & ((qi < sb) | (n < nt - 1))
                p = jnp.exp(jnp.where(valid, sc - _lanes(window(lc_ref, ln_ref, r, j), sb), -1e30))
                dp = lax.dot_general(dov, v, NT_DIMS, preferred_element_type=F32)
                ds = (p * (dp - _lanes(window(dc_ref, dn_ref, r, j), sb)) * scale).astype(BF16)
                dvf[rows, :] = lax.dot_general(p.astype(BF16), dov, TN_DIMS, preferred_element_type=F32)
                dk_ref[rows, :] = lax.dot_general(ds, q, TN_DIMS, preferred_element_type=F32)
                dqw = lax.dot_general(ds, k, NN_DIMS, preferred_element_type=F32)
                first = dqw[:BAND] + tail
                dq_ref[rows, :] = first if sb == BAND else jnp.concatenate([first, dqw[BAND:sb]], axis=0)
                tail = dqw[sb:]
            later[r] = tail
        dv_ref[...] = dvf[...].astype(BF16)

    cur, side, tok, tok_side = _attn_specs(g, tb, pb, s, ahead=True)
    anyspec = pl.BlockSpec(memory_space=pl.ANY)
    n_heads_cols = 3 * N_HEADS * HD
    return pl.pallas_call(
        body, name=f"attn_bwd_g{g}", grid=(N_HEADS, nt),
        out_shape=[jax.ShapeDtypeStruct((s, n_heads_cols), F32), jax.ShapeDtypeStruct((s, n_heads_cols), F32),
                   jax.ShapeDtypeStruct((s, IN_W), BF16)],
        in_specs=[cur(K_COL), cur(V_COL), cur(Q_COL), side(Q_COL), tok, tok_side, tok, tok_side,
                  tok, tok_side] + ([anyspec, anyspec] if chained else []) + [anyspec],
        out_specs=[cur(0), cur(0), cur(V_COL)],
        input_output_aliases={10: 0, 11: 1, 12: 2} if chained else {10: 2},
        scratch_shapes=[pltpu.VMEM((tb, HD), ft), pltpu.VMEM((tb, HD), ft),
                        pltpu.VMEM((tb + pb, HD), ft), pltpu.VMEM((tb, HD), F32),
                        pltpu.VMEM((d, BAND, HD), F32)],
        compiler_params=_params("arbitrary", "arbitrary"),
    )(qkn, proj, qkn, qkn, do, do, lse, lse, delta, delta, *([dqn, dkn] if chained else []), dproj)


def _shift_down(x, before, k):
    rolled = pltpu.roll(x, k, 0)
    head = jnp.where(lax.broadcasted_iota(jnp.int32, before.shape, 0) < k, pltpu.roll(before, k, 0), rolled[:8])
    return jnp.concatenate([head, rolled[8:]], axis=0)


def _shift_up(x, after, k):
    rows = x.shape[0]
    rolled = pltpu.roll(x, rows - k, 0)
    tail = jnp.where(lax.broadcasted_iota(jnp.int32, after.shape, 0) >= 8 - k,
                     pltpu.roll(after, 8 - k, 0), rolled[rows - 8:])
    return jnp.concatenate([rolled[:rows - 8], tail], axis=0)


def _conv_fwd(proj, cw, tm=1024):
    s = proj.shape[0]
    r16 = tm // 16

    def body(u_ref, b_ref, c_ref, up_ref, cp_ref, w_ref, z_ref):
        i = pl.program_id(1)
        xc = c_ref[...].astype(F32) * u_ref[...].astype(F32)
        xp = jnp.where(i > 0, cp_ref[8:16, :].astype(F32) * up_ref[8:16, :].astype(F32), 0.0)
        w = w_ref[...]
        conv = _shift_down(xc, xp, 2) * w[0:1] + _shift_down(xc, xp, 1) * w[1:2] + xc * w[2:3]
        z_ref[...] = (b_ref[...].astype(F32) * conv).astype(BF16)

    tile = lambda blk: pl.BlockSpec((tm, COL), lambda j, i: (i, blk + j))
    before = lambda blk: pl.BlockSpec((16, COL), lambda j, i: (jnp.maximum(i * r16 - 1, 0), blk + j))
    return pl.pallas_call(
        body, name="conv_fwd", grid=(D // COL, s // tm),
        out_shape=jax.ShapeDtypeStruct((s, D), BF16),
        in_specs=[tile(U_BLK), tile(B_BLK), tile(C_BLK), before(U_BLK), before(C_BLK),
                  pl.BlockSpec((3, COL), lambda j, i: (0, j))],
        out_specs=pl.BlockSpec((tm, COL), lambda j, i: (i, j)),
        compiler_params=_params("parallel", "parallel"),
    )(proj, proj, proj, proj, proj, cw)


def _conv_bwd(dz, proj, cw, dproj, tm=1024):
    s = proj.shape[0]
    r16 = tm // 16
    nrow = s // tm

    def body(dz_ref, u_ref, b_ref, c_ref, up_ref, cp_ref, dzn_ref, bn_ref, w_ref, _, o_ref, dc_ref, acc_ref):
        piece, i = pl.program_id(1), pl.program_id(2)
        u, c = u_ref[...].astype(F32), c_ref[...].astype(F32)
        bv = b_ref[...].astype(F32)
        dzv = dz_ref[...]
        w = w_ref[...]

        @pl.when((piece == 0) & (i == 0))
        def _():
            acc_ref[...] = jnp.zeros_like(acc_ref)

        @pl.when(piece == 0)
        def _():
            xc = c * u
            xp = jnp.where(i > 0, cp_ref[8:16, :].astype(F32) * up_ref[8:16, :].astype(F32), 0.0)
            x2, x1 = _shift_down(xc, xp, 2), _shift_down(xc, xp, 1)
            o_ref[...] = (dzv * (x2 * w[0:1] + x1 * w[1:2] + xc * w[2:3])).astype(BF16)
            dc_ref[...] = jnp.zeros_like(dc_ref)
            dconv = dzv * bv
            acc_ref[0:1, :] += jnp.sum(dconv * x2, axis=0, keepdims=True)
            acc_ref[1:2, :] += jnp.sum(dconv * x1, axis=0, keepdims=True)
            acc_ref[2:3, :] += jnp.sum(dconv * xc, axis=0, keepdims=True)

        @pl.when(piece == 1)
        def _():
            dconv = dzv * bv
            dn = jnp.where(i < nrow - 1, dzn_ref[...] * bn_ref[0:8, :].astype(F32), 0.0)
            dxc = dconv * w[2:3] + _shift_up(dconv, dn, 1) * w[1:2] + _shift_up(dconv, dn, 2) * w[0:1]
            o_ref[...] = (dxc * c).astype(BF16)
            dc_ref[...] = (dxc * u).astype(BF16)

    tile = lambda blk: pl.BlockSpec((tm, COL), lambda j, p, i: (i, blk + j))
    before = lambda blk: pl.BlockSpec((16, COL), lambda j, p, i: (jnp.maximum(i * r16 - 1, 0), blk + j))
    after = lambda rows, blk: pl.BlockSpec(
        (rows, COL), lambda j, p, i: (jnp.minimum((i + 1) * (tm // rows), s // rows - 1), blk + j))
    return pl.pallas_call(
        body, name="conv_bwd", grid=(D // COL, 2, nrow),
        out_shape=[jax.ShapeDtypeStruct((s, IN_W), BF16), jax.ShapeDtypeStruct((s + tm, D), BF16),
                   jax.ShapeDtypeStruct((8, D), F32)],
        in_specs=[tile(0), tile(U_BLK), tile(B_BLK), tile(C_BLK), before(U_BLK), before(C_BLK),
                  after(8, 0), after(16, B_BLK), pl.BlockSpec((3, COL), lambda j, p, i: (0, j)),
                  pl.BlockSpec(memory_space=pl.ANY)],
        out_specs=[pl.BlockSpec((tm, COL), lambda j, p, i: (i, jnp.where(p == 0, B_BLK, U_BLK) + j)),
                   pl.BlockSpec((tm, COL), lambda j, p, i: (jnp.where(p == 0, nrow, i), j)),
                   pl.BlockSpec((8, COL), lambda j, p, i: (0, j))],
        input_output_aliases={9: 0},
        compiler_params=_params("arbitrary", "arbitrary", "arbitrary"),
    )(dz, proj, proj, proj, proj, proj, dz, proj, cw, dproj)


def _copy_columns(name, src, dst, blk0, tm=2048):
    s, w = dst.shape[0], src.shape[1]
    fresh = isinstance(dst, jax.ShapeDtypeStruct)

    def body(x_ref, *rest):
        rest[-1][...] = x_ref[...]

    return pl.pallas_call(
        body, name=name, grid=(w // COL, s // tm),
        out_shape=jax.ShapeDtypeStruct(dst.shape, dst.dtype),
        in_specs=[pl.BlockSpec((tm, COL), lambda j, i: (i, j))] + ([] if fresh else [pl.BlockSpec(memory_space=pl.ANY)]),
        out_specs=pl.BlockSpec((tm, COL), lambda j, i: (i, blk0 + j)),
        input_output_aliases={} if fresh else {1: 0},
        compiler_params=_params("parallel", "parallel"),
    )(src, *([] if fresh else [dst]))


def _mod_part(c_all, w_ada, b_part):
    def body(c_ref, w_ref, b_ref, o_ref):
        cv = c_ref[...]
        act = cv * _sigmoid(cv)
        o_ref[...] = jnp.dot(act, w_ref[...], preferred_element_type=F32,
                             precision=lax.Precision.HIGHEST) + b_ref[...]

    return pl.pallas_call(
        body, name="mod_part", out_shape=jax.ShapeDtypeStruct((N_DEV, w_ada.shape[1]), F32),
    )(c_all, w_ada, b_part)


def _w_ada_grad(c_all_t, dmod_part):
    def body(c_ref, d_ref, o_ref):
        cv = c_ref[...]
        act = cv * _sigmoid(cv)
        dv = d_ref[...]
        acc = act[:, 0:1] * dv[0:1, :]
        for b in range(1, N_DEV):
            acc = acc + act[:, b:b + 1] * dv[b:b + 1, :]
        o_ref[...] = acc

    return pl.pallas_call(
        body, name="w_ada_grad", out_shape=jax.ShapeDtypeStruct((D, dmod_part.shape[1]), F32),
    )(c_all_t, dmod_part)


def _sum_rows(name, v):
    def body(v_ref, o_ref):
        acc = v_ref[0]
        for k in range(1, N_DEV):
            acc = acc + v_ref[k]
        o_ref[...] = acc

    return pl.pallas_call(body, name=name, out_shape=jax.ShapeDtypeStruct(v.shape[1:], F32))(v)


def _adamw(name, w, g, m, v):
    rows, cols = w.shape
    limit = max(16, (1 << 20) // (4 * cols))
    tr = rows if rows <= limit else next((t for t in range(limit - limit % 16, 15, -16) if rows % t == 0), rows)
    c1 = 1.0 - ADAM_B1 ** ADAM_STEP
    c2 = 1.0 - ADAM_B2 ** ADAM_STEP
    parts = g.ndim == 3

    def body(w_ref, g_ref, m_ref, v_ref, go_ref, d_ref, nm_ref, nv_ref):
        if parts:
            gv = g_ref[0].astype(F32)
            for k in range(1, N_DEV):
                gv = gv + g_ref[k].astype(F32)
        else:
            gv = g_ref[...]
        go_ref[...] = gv
        nm = ADAM_B1 * m_ref[...] + (1.0 - ADAM_B1) * gv
        nv = ADAM_B2 * v_ref[...] + (1.0 - ADAM_B2) * (gv * gv)
        nm_ref[...] = nm
        nv_ref[...] = nv
        d_ref[...] = -ADAM_LR * ((nm / c1) / (jnp.sqrt(nv / c2) + ADAM_EPS) + ADAM_WD * w_ref[...])

    spec = pl.BlockSpec((tr, cols), lambda i: (i, 0))
    g_spec = pl.BlockSpec((N_DEV, tr, cols), lambda i: (0, i, 0)) if parts else spec
    return pl.pallas_call(
        body, name=name, grid=(rows // tr,),
        out_shape=[jax.ShapeDtypeStruct((rows, cols), F32)] * 4,
        in_specs=[spec, g_spec, spec, spec], out_specs=[spec] * 4,
        compiler_params=_params("parallel"),
    )(w, g, m, v)


def _adamw_small(ws, gs, ms, vs):
    n = len(ws)
    c1 = 1.0 - ADAM_B1 ** ADAM_STEP
    c2 = 1.0 - ADAM_B2 ** ADAM_STEP

    def body(*refs):
        for i in range(n):
            w_ref, g_ref, m_ref, v_ref = refs[i], refs[n + i], refs[2 * n + i], refs[3 * n + i]
            d_ref, nm_ref, nv_ref = refs[4 * n + 3 * i:4 * n + 3 * i + 3]
            gv = g_ref[...]
            nm = ADAM_B1 * m_ref[...] + (1.0 - ADAM_B1) * gv
            nv = ADAM_B2 * v_ref[...] + (1.0 - ADAM_B2) * (gv * gv)
            nm_ref[...] = nm
            nv_ref[...] = nv
            d_ref[...] = -ADAM_LR * ((nm / c1) / (jnp.sqrt(nv / c2) + ADAM_EPS) + ADAM_WD * w_ref[...])

    outs = pl.pallas_call(
        body, name="adamw_small",
        out_shape=[jax.ShapeDtypeStruct(w.shape, F32) for w in ws for _ in range(3)],
    )(*ws, *gs, *ms, *vs)
    return [tuple(outs[3 * i:3 * i + 3]) for i in range(n)]


HALF = FF // 2


def _sds(shape, dtype):
    return jax.ShapeDtypeStruct(shape, dtype)


def _row_tile(w):
    return lambda tm: ((tm, w), lambda i, j: (i, 0))


def _one(w):
    return lambda rows: ((rows, w), lambda i, j: (0, 0))


def _gate_up_swiglu(name, h, wgu, tm=1024):
    s = h.shape[0]
    tm = min(tm, s)

    def epilogue(prod, first, tin, tout):
        pq_ref, s_ref = tout
        a, b = prod[:, :HALF], prod[:, HALF:]
        sig = _sigmoid(a)
        act = a * sig
        pq_ref[:, :HALF] = (b * (sig * (1.0 + a * (1.0 - sig)))).astype(BF16)
        pq_ref[:, HALF:] = act.astype(BF16)
        s_ref[...] = (act * b).astype(BF16)

    return _mm(name, h, wgu, "NT", None, tm, FF, D, n_outer=True, epilogue=epilogue,
               tiles_out=[(_sds((s, 2 * FF), BF16), (tm, FF), lambda i, j: (i, j)),
                          (_sds((s, FF), BF16), (tm, HALF), lambda i, j: (i, j))])


def _d_hidden_swiglu(name, df, wd, ab, after=(), tm=1024):
    s = df.shape[0]
    tm = min(tm, s)

    def epilogue(prod, first, tin, tout, cols):
        da_cols = slice(cols[0], cols[0] + cols[1])
        db_cols = slice(HALF + cols[0], HALF + cols[0] + cols[1])
        tout[0][:, da_cols] = (prod * tin[0][:, da_cols].astype(F32)).astype(BF16)
        tout[0][:, db_cols] = (prod * tin[0][:, db_cols].astype(F32)).astype(BF16)

    chunks = [(c0, min(384, HALF - c0)) for c0 in range(0, HALF, 384)]
    return _mm(name, df, wd, "NT", None, tm, HALF, D, n_outer=True, epilogue=epilogue, col_chunks=chunks, after=after,
               tiles_in=[(ab, (tm, FF), lambda i, j: (i, j))],
               tiles_out=[(_sds((s, 2 * FF), BF16), (tm, FF), lambda i, j: (i, j))])[0]


def _out_residual(name, a, w, x, gt, coef, nxt, tm=512, tk=FF):
    s = a.shape[0]
    tm = min(tm, s)

    def epilogue(prod, first, tin, tout):
        x_ref, gt_ref, g_ref, sc_ref, sh_ref = tin
        f_ref, xn_ref, h_ref = tout
        f_ref[...] = prod
        xn = x_ref[...] + (coef * gt_ref[...]) * prod
        xn_ref[...] = xn
        r = lax.rsqrt(jnp.mean(xn * xn, axis=-1, keepdims=True) + EPS)
        h_ref[...] = ((xn * r) * g_ref[...] * (1.0 + sc_ref[...]) + sh_ref[...]).astype(BF16)

    row, vec = _row_tile(D)(tm), _one(D)(1)
    return _mm(name, a, w, "NN", None, tm, D, tk, epilogue=epilogue,
               tiles_in=[(x, *row), (gt, *vec)] + [(v, *vec) for v in nxt],
               tiles_out=[(_sds((s, D), F32), *row), (_sds((s, D), F32), *row), (_sds((s, D), BF16), *row)])


def _out_loss(name, a, w, x, gt, coef, target, tm=512):
    s = a.shape[0]
    tm = min(tm, s)

    def epilogue(prod, first, tin, tout):
        x_ref, gt_ref, t_ref = tin
        f_ref, g_ref, df_ref, acc_ref = tout
        f_ref[...] = prod
        cg = coef * gt_ref[...]
        e = x_ref[...] + cg * prod - t_ref[...]
        gv = e * (1.0 / D)
        g_ref[...] = gv
        df_ref[...] = (cg * gv).astype(BF16)

        @pl.when(first)
        def _():
            acc_ref[...] = jnp.zeros_like(acc_ref)

        acc_ref[0:1, :] += coef * jnp.sum(gv * prod, axis=0, keepdims=True)
        acc_ref[1:2, :] += (0.5 / D) * jnp.sum(e * e, axis=0, keepdims=True)

    row, vec = _row_tile(D)(tm), _one(D)(1)
    return _mm(name, a, w, "NN", None, tm, D, FF, epilogue=epilogue,
               tiles_in=[(x, *row), (gt, *vec), (target, *row)],
               tiles_out=[(_sds((s, D), F32), *row), (_sds((s, D), F32), *row), (_sds((s, D), BF16), *row),
                          (_sds((8, D), F32), *_one(D)(8))])


def _d_h_norm_bwd(name, da, w, x, gin, g, sc, sh, before=None, after=(), tm=256):
    s = da.shape[0]
    tm = min(tm, s)
    coef = before[2] if before else None

    def epilogue(prod, first, tin, tout):
        x_ref, gin_ref, g_ref, sc_ref, sh_ref = tin[:5]
        gout_ref, acc_ref = tout[:2]
        xv = x_ref[...]
        r = lax.rsqrt(jnp.mean(xv * xv, axis=-1, keepdims=True) + EPS)
        nv = xv * r
        gv, one_sc = g_ref[...], 1.0 + sc_ref[...]
        dn = prod * gv * one_sc
        gout = gin_ref[...] + r * (dn - nv * jnp.mean(dn * nv, axis=-1, keepdims=True))
        gout_ref[...] = gout

        @pl.when(first)
        def _():
            acc_ref[...] = jnp.zeros_like(acc_ref)

        dhn = prod * nv
        acc_ref[0:1, :] += jnp.sum(prod, axis=0, keepdims=True)
        acc_ref[1:2, :] += jnp.sum(dhn * gv, axis=0, keepdims=True)
        acc_ref[2:3, :] += jnp.sum(dhn * one_sc, axis=0, keepdims=True)
        if before:
            f_ref, gt_ref = tin[5:]
            tout[2][...] = ((coef * gt_ref[...]) * gout).astype(BF16)
            acc_ref[3:4, :] += coef * jnp.sum(gout * f_ref[...], axis=0, keepdims=True)

    row, vec = _row_tile(D)(tm), _one(D)(1)
    tiles_in = [(x, *row), (gin, *row), (g, *vec), (sc, *vec), (sh, *vec)]
    tiles_out = [(_sds((s, D), F32), *row), (_sds((8, D), F32), *_one(D)(8))]
    if before:
        tiles_in += [(before[0], *row), (before[1], *vec)]
        tiles_out.append((_sds((s, D), BF16), *row))
    return _mm(name, da, w, "NN", None, tm, D, da.shape[1], epilogue=epilogue, keep_b=True, after=after,
               tiles_in=tiles_in, tiles_out=tiles_out)


def _gate_tiles(proj, tm):
    return [(proj, (tm, COL), (lambda i, j, blk=blk: (i, blk))) for blk in (GA_BLK, GA_BLK + 1, GC_BLK, GC_BLK + 1)]


def _conv_branch_merge(z, wc, ya, proj, tm=1024):
    s = z.shape[0]
    tm = min(tm, s)

    def epilogue(prod, first, tin, tout):
        ya_ref, ga0, ga1, gc0, gc1 = tin
        tout[0][...] = prod.astype(BF16)
        for half, (ga, gc) in enumerate(((ga0, gc0), (ga1, gc1))):
            cols = slice(half * COL, (half + 1) * COL)
            tout[1][:, cols] = (_sigmoid(ga[...].astype(F32)) * ya_ref[:, cols].astype(F32)
                                + _sigmoid(gc[...].astype(F32)) * prod[:, cols]).astype(BF16)

    row = _row_tile(D)(tm)
    return _mm("mix_conv_branch", z, wc, "NN", None, tm, D, D, epilogue=epilogue,
               tiles_in=[(ya, *row)] + _gate_tiles(proj, tm),
               tiles_out=[(_sds((s, D), BF16), *row), (_sds((s, D), BF16), *row)])


def _d_merged_branches(dmix, wo, ya, yc, proj, tm=1024):
    s = dmix.shape[0]
    tm = min(tm, s)

    def epilogue(prod, first, tin, tout):
        ya_ref, yc_ref, ga0, ga1, gc0, gc1 = tin
        dya_ref, dyc_ref, dg_ref = tout
        for half, (ga, gc) in enumerate(((ga0, gc0), (ga1, gc1))):
            cols = slice(half * COL, (half + 1) * COL)
            dm = prod[:, cols]
            for y_ref, g_ref, dy_ref, off in ((ya_ref, ga, dya_ref, 0), (yc_ref, gc, dyc_ref, D)):
                sig = _sigmoid(g_ref[...].astype(F32))
                dms = dm * sig
                dy_ref[:, cols] = dms.astype(BF16)
                dg_ref[:, off + half * COL:off + (half + 1) * COL] = (
                    dms * y_ref[:, cols].astype(F32) * (1.0 - sig)).astype(BF16)

    row = _row_tile(D)(tm)
    return _mm("mix_d_merged", dmix, wo, "NT", None, tm, D, D, epilogue=epilogue,
               tiles_in=[(ya, *row), (yc, *row)] + _gate_tiles(proj, tm),
               tiles_out=[(_sds((s, D), BF16), *row), (_sds((s, D), BF16), *row),
                          (_sds((s, 2 * D), BF16), *_row_tile(2 * D)(tm))])


def _d_o_delta(dya, wa_t, o, tm=1024):
    s = dya.shape[0]
    tm = min(tm, s)

    def epilogue(prod, first, tin, tout):
        tout[0][...] = prod
        tout[1][...] = _heads(prod * tin[0][...].astype(F32), lambda ph, h: jnp.broadcast_to(
            jnp.sum(ph, axis=-1, keepdims=True), ph.shape))

    row = _row_tile(COL)(tm)
    return _mm("mix_d_o", dya, wa_t, "NN", None, tm, COL, D, epilogue=epilogue,
               tiles_in=[(o, *row)], tiles_out=[(_sds((s, COL), F32), *row), (_sds((s, COL), F32), *row)])


def _ffn_bwd(tag, df, x, gin, h, ab, sw, g, sc, sh, wgu, wd, before=None, tk_dw=2048):
    dwd = _mm(f"{tag}_dw_down", sw, df, "TN", BF16, HALF, D, tk_dw)
    dab = _d_hidden_swiglu(f"{tag}_d_hidden", df, wd, ab, after=[dwd])
    dwgu = _mm(f"{tag}_dw_gate_up", dab, h, "TN", BF16, HALF, D, tk_dw)
    res = _d_h_norm_bwd(f"{tag}_d_h", dab, wgu, x, gin, g, sc, sh, before=before, after=[dwgu], tm=512)
    return res, dwgu, dwd


def kernel(x, c, w_ada, b_ada, norm_ffn1, ffn1_w_gate, ffn1_w_up, ffn1_w_down, norm_mix, w_in, q_norm, k_norm, conv_w, w_attn_branch, w_conv_branch, w_out, norm_ffn2, ffn2_w_gate, ffn2_w_up, ffn2_w_down, loss_target, m_w_ada, m_b_ada, m_norm_ffn1, m_ffn1_w_gate, m_ffn1_w_up, m_ffn1_w_down, m_norm_mix, m_w_in, m_q_norm, m_k_norm, m_conv_w, m_w_attn_branch, m_w_conv_branch, m_w_out, m_norm_ffn2, m_ffn2_w_gate, m_ffn2_w_up, m_ffn2_w_down, v_w_ada, v_b_ada, v_norm_ffn1, v_ffn1_w_gate, v_ffn1_w_up, v_ffn1_w_down, v_norm_mix, v_w_in, v_q_norm, v_k_norm, v_conv_w, v_w_attn_branch, v_w_conv_branch, v_w_out, v_norm_ffn2, v_ffn2_w_gate, v_ffn2_w_up, v_ffn2_w_down):
    me = 4 * lax.axis_index("x") + 2 * lax.axis_index("y") + lax.axis_index("c")
    x0, target = x[0], loss_target[0]
    s = x0.shape[0]
    ada_cols = w_ada.shape[2]
    cw_cols = conv_w.shape[2]

    gathered = _small_allgather(
        "gather_c_conv", jnp.concatenate([c, conv_w[0].reshape(1, 3 * cw_cols)], axis=1))[:, 0]
    c_all = gathered[:, :D]
    cw = gathered[:, D:].reshape(N_DEV, 3, cw_cols).transpose(1, 0, 2).reshape(3, D)
    b_part = lax.dynamic_slice(b_ada, (0, me * ada_cols), (1, ada_cols))
    mod_part = _mod_part(c_all, w_ada[0], b_part)
    mod_all = _small_allgather("gather_mod", mod_part.reshape(1, N_DEV * ada_cols))
    mod = lax.dynamic_slice(mod_all.reshape(N_DEV, N_DEV, ada_cols), (0, me, 0), (N_DEV, 1, ada_cols))
    mod = mod.reshape(N_MOD, 1, D)
    sh1, sc1, gt1, sh2, sc2, gt2, sh3, sc3, gt3 = [mod[i] for i in range(N_MOD)]

    tb = lambda w: w[0].T.astype(BF16)
    nb = lambda w: w[0].astype(BF16)
    ffn1_shards = [tb(ffn1_w_gate), tb(ffn1_w_up), nb(ffn1_w_down)]
    ffn2_shards = [tb(ffn2_w_gate), tb(ffn2_w_up), nb(ffn2_w_down)]
    mix_shards = [tb(w_in), tb(w_attn_branch), nb(w_conv_branch), nb(w_out)]
    ffn_dst, ffn_base, ffn_jump, ffn_shapes = [0, 0, 1], [0, HALF, 0], [HALF, HALF, 0], [(2 * FF, D), (FF, D)]
    mix_dst, mix_base, mix_shapes = [0, 1, 2, 3], [0, 0, 0, 0], [(IN_W, D), (D, COL), (D, D), (D, D)]
    (wgu1,) = _run_plan_on_sequencer(
        "gather_ffn1_gate_up", _gather_plan(ffn1_shards[:2], ffn_dst[:2], ffn_base[:2], ffn_shapes[:1], ffn_jump[:2]), 1)
    (wd1,) = _run_plan_on_sequencer(
        "gather_ffn1_down", _gather_plan(ffn1_shards[2:], [0], [0], ffn_shapes[1:]), 8)
    win_t, wa_t, wc, wo = _run_plan_on_sequencer(
        "gather_mix_weights", _gather_plan(mix_shards, mix_dst, mix_base, mix_shapes), 2)
    wgu2, wd2 = _run_plan_on_sequencer(
        "gather_ffn2_weights", _gather_plan(ffn2_shards, ffn_dst, ffn_base, ffn_shapes, ffn_jump), 3)

    h1 = _normmod("ffn1_normmod", x0, norm_ffn1, sc1, sh1)
    ab1, s1 = _gate_up_swiglu("ffn1_gate_up", h1, wgu1)
    f1, x1, h2 = _out_residual("ffn1_down", s1, wd1, x0, gt1, 0.5, (norm_mix, sc2, sh2))
    proj = _mm("mix_in_proj", h2, win_t, "NT", BF16, 1024, IN_W // 4, D, n_outer=True)
    wqk = jnp.concatenate([jnp.tile(q_norm, (1, 12)), jnp.tile(k_norm, (1, 12))], axis=1)
    qkn = _qknorm(proj, wqk)
    group_out = [_attn_fwd(g, qkn, proj) for g in range(3)]
    o, lse = _attn_combine([go[0] for go in group_out], [go[1] for go in group_out])
    ya = _mm("mix_attn_branch", o, wa_t, "NT", BF16, 1024, 1024, COL)
    z = _conv_fwd(proj, cw)
    yc, merged = _conv_branch_merge(z, wc, ya, proj)
    mix, x2, h3 = _out_residual("mix_out_proj", merged, wo, x1, gt2, 1.0, (norm_ffn2, sc3, sh3), tm=1024, tk=D)
    ab3, s3 = _gate_up_swiglu("ffn2_gate_up", h3, wgu2)
    f3, g3, df3, acc_out = _out_loss("ffn2_down", s3, wd2, x2, gt3, 0.5, target)
    loss_part = jnp.sum(acc_out[1])

    ffn_rows = [sh_.shape[0] for sh_ in ffn1_shards]
    mix_rows = [sh_.shape[0] for sh_ in mix_shards]
    (g2, acc3, dmix), dwgu2, dwd2 = _ffn_bwd(
        "ffn2", df3, x2, g3, h3, ab3, s3, norm_ffn2, sc3, sh3, wgu2, wd2, before=(mix, gt2, 1.0))
    dya, dyc, dgates = _d_merged_branches(dmix, wo, ya, yc, proj)
    dwo = _mm("mix_dw_out", merged, dmix, "TN", BF16, 1024, 1024, 2048)
    dproj = _copy_columns("dproj_gates", dgates, jax.ShapeDtypeStruct((s, IN_W), BF16), GA_BLK)
    dwc = _mm("mix_dw_conv_branch", z, dyc, "TN", BF16, 1024, 1024, 2048)
    dz = _mm("mix_d_z", dyc, wc, "NT", F32, 1024, 1024, D)
    dproj, d_c, cw_acc = _conv_bwd(dz, proj, cw, dproj)
    dproj = _copy_columns("copy_d_c", d_c, dproj, C_BLK)
    dwa_t = _mm("mix_dw_attn_branch", dya, o, "TN", BF16, 1024, COL, 2048)
    do, delta = _d_o_delta(dya, wa_t, o)
    dqn = dkn = None
    for g in range(3):
        dqn, dkn, dproj = _attn_bwd(g, qkn, proj, do, lse, delta, dqn, dkn, dproj)
    dproj, wq_acc = _qknorm_bwd("qnorm_bwd", proj, dqn, wqk[:, :QKW // 2], dproj, 0)
    dproj, wk_acc = _qknorm_bwd("knorm_bwd", proj, dkn, wqk[:, QKW // 2:], dproj, 1)
    r_f2g, r_f2u, r_f2d, r_wa, r_wc, r_wo = _run_plan_on_sequencer(
        "scatter_ffn2_and_branch_grads",
        _scatter_plan([dwgu2, dwd2, dwa_t, dwc, dwo], [0, 0, 1, 2, 3, 4], [0, HALF, 0, 0, 0, 0],
                      ffn_rows + mix_rows[1:], [D, D, D, COL, D, D], [HALF, HALF, 0, 0, 0, 0]), 4)
    dwin_t = _mm("mix_dw_in", dproj, h2, "TN", BF16, IN_W // 4, COL, 2048)
    (r_win,) = _run_plan_on_sequencer(
        "scatter_w_in_grad", _scatter_plan([dwin_t], [0], [0], mix_rows[:1], [D]), 5)
    g1, acc2, df1 = _d_h_norm_bwd("mix_d_h", dproj, win_t, x1, g2, norm_mix, sc2, sh2, before=(f1, gt1, 0.5),
                                  after=[dwin_t])
    dwd1 = _mm("ffn1_dw_down", s1, df1, "TN", BF16, HALF, D, 2048)
    (r_f1d,) = _run_plan_on_sequencer(
        "scatter_ffn1_down_grad", _scatter_plan([dwd1], [0], [0], ffn_rows[2:], [D]), 6)
    dab1 = _d_hidden_swiglu("ffn1_d_hidden", df1, wd1, ab1, after=[dwd1, r_win])
    dwgu1 = _mm("ffn1_dw_gate_up", dab1, h1, "TN", BF16, HALF, D, 2048)
    r_f1g, r_f1u = _run_plan_on_sequencer(
        "scatter_ffn1_gate_up_grads",
        _scatter_plan([dwgu1], [0, 0], [0, HALF], ffn_rows[:2], [D, D], [HALF, HALF]), 7)
    g0, acc1 = _d_h_norm_bwd("ffn1_d_h", dab1, wgu1, x0, g1, norm_ffn1, sc1, sh1, after=[dwgu1, r_f1d], tm=512)

    dqw = jnp.sum(wq_acc[0].reshape(12, HD), axis=0)
    dkw = jnp.sum(wk_acc[0].reshape(12, HD), axis=0)
    small = jnp.concatenate([
        acc1[0], acc1[1], acc2[3], acc2[0], acc2[1], acc3[3], acc3[0], acc3[1], acc_out[0],
        acc1[2], acc2[2], acc3[2], dqw, dkw, cw_acc[0:3].reshape(3 * D),
        jnp.zeros((HD,), F32).at[0].set(loss_part)]).reshape(1, -1)
    small_all = _small_allgather("gather_small_grads", small)
    small_sum = _sum_rows("sum_small_grads", small_all)[0]
    n_mod = N_MOD * D
    g_b_ada = small_sum[:n_mod].reshape(1, n_mod)
    g_norm1, g_norm2, g_norm3 = [small_sum[n_mod + i * D:n_mod + (i + 1) * D].reshape(1, D) for i in range(3)]
    off = n_mod + 3 * D
    g_qn, g_kn = small_sum[off:off + HD].reshape(1, HD), small_sum[off + HD:off + 2 * HD].reshape(1, HD)
    g_cw_full = small_sum[off + 2 * HD:off + 2 * HD + 3 * D].reshape(3, D)
    loss = small_sum[off + 2 * HD + 3 * D]
    g_cw = lax.dynamic_slice(g_cw_full, (0, me * cw_cols), (3, cw_cols))
    dmod_part = lax.dynamic_slice(small_all[:, 0, :n_mod], (0, me * ada_cols), (N_DEV, ada_cols))
    g_w_ada = _w_ada_grad(c_all.T, dmod_part)

    as_rows = {"ffn1_w_gate", "ffn1_w_up", "w_in", "w_attn_branch", "ffn2_w_gate", "ffn2_w_up"}
    grad_list = [g_w_ada, g_b_ada, g_norm1, r_f1g, r_f1u, r_f1d, g_norm2, r_win,
                 g_qn, g_kn, g_cw, r_wa, r_wc, r_wo, g_norm3, r_f2g, r_f2u, r_f2d]
    weights = [w_ada, b_ada, norm_ffn1, ffn1_w_gate, ffn1_w_up, ffn1_w_down, norm_mix, w_in, q_norm, k_norm,
               conv_w, w_attn_branch, w_conv_branch, w_out, norm_ffn2, ffn2_w_gate, ffn2_w_up, ffn2_w_down]
    ms = [m_w_ada, m_b_ada, m_norm_ffn1, m_ffn1_w_gate, m_ffn1_w_up, m_ffn1_w_down, m_norm_mix, m_w_in, m_q_norm,
          m_k_norm, m_conv_w, m_w_attn_branch, m_w_conv_branch, m_w_out, m_norm_ffn2, m_ffn2_w_gate,
          m_ffn2_w_up, m_ffn2_w_down]
    vs = [v_w_ada, v_b_ada, v_norm_ffn1, v_ffn1_w_gate, v_ffn1_w_up, v_ffn1_w_down, v_norm_mix, v_w_in, v_q_norm,
          v_k_norm, v_conv_w, v_w_attn_branch, v_w_conv_branch, v_w_out, v_norm_ffn2, v_ffn2_w_gate,
          v_ffn2_w_up, v_ffn2_w_down]
    wnames = ["w_ada", "b_ada", "norm_ffn1", "ffn1_w_gate", "ffn1_w_up", "ffn1_w_down", "norm_mix", "w_in",
              "q_norm", "k_norm", "conv_w", "w_attn_branch", "w_conv_branch", "w_out", "norm_ffn2",
              "ffn2_w_gate", "ffn2_w_up", "ffn2_w_down"]
    small = [i for i, gr in enumerate(grad_list) if gr.ndim == 2 and gr.size <= 16384]
    flat = lambda a, i: a.reshape(-1, weights[i].shape[-1])
    small_res = dict(zip(small, _adamw_small(
        [flat(weights[i], i) for i in small], [flat(grad_list[i], i) for i in small],
        [flat(ms[i], i) for i in small], [flat(vs[i], i) for i in small])))
    grad_out, deltas, new_ms, new_vs = [], [], [], []
    for idx, (nm, w, gr, m_, v_) in enumerate(zip(wnames, weights, grad_list, ms, vs)):
        if idx in small_res:
            gr, dl, nm_, nv_ = [r.reshape(w.shape) for r in (gr, *small_res[idx])]
        elif nm in as_rows:
            res = _adamw(f"adamw_{nm}", w[0].T, gr, m_[0].T, v_[0].T)
            gr, dl, nm_, nv_ = [r.T[None] for r in res]
        else:
            two_d = (-1, w.shape[-1])
            res = _adamw(f"adamw_{nm}", w.reshape(two_d), gr if gr.ndim == 3 else gr.reshape(two_d),
                         m_.reshape(two_d), v_.reshape(two_d))
            gr, dl, nm_, nv_ = [r.reshape(w.shape) for r in res]
        grad_out.append(gr)
        deltas.append(dl)
        new_ms.append(nm_)
        new_vs.append(nv_)
    return (loss, g0[None], *grad_out, *deltas, *new_ms, *new_vs)
```

```python
import jax
import jax.numpy as jnp
from jax import lax
from jax.experimental import pallas as pl
from jax.experimental.pallas import tpu as pltpu
from jax.experimental.pallas import tpu_sc as plsc

F32 = jnp.float32
BF16 = jnp.bfloat16
MESH = pl.DeviceIdType.MESH

N_DEV = 8
D = 1024
FF = 2816
HD = 128
N_HEADS = 4
DILATIONS = (1, 4, 16)
BAND = 128
QKW = 2 * 3 * N_HEADS * HD
IN_W = 9728
COL = 512
V_BLK, U_BLK, B_BLK, C_BLK, GA_BLK, GC_BLK = 6, 9, 11, 13, 15, 17
EPS = 1e-6
N_MOD = 9
ADAM_LR, ADAM_B1, ADAM_B2, ADAM_EPS, ADAM_WD, ADAM_STEP = 0.001, 0.9, 0.999, 1e-08, 0.01, 10

NT_DIMS = (((1,), (1,)), ((), ()))
TN_DIMS = (((0,), (0,)), ((), ()))
NN_DIMS = (((1,), (0,)), ((), ()))


def _place():
    return lax.axis_index("x"), lax.axis_index("y"), lax.axis_index("c")


def _flip(coord, bit):
    return 1 - coord if bit else coord


def _params(*sem):
    return pltpu.CompilerParams(dimension_semantics=sem)


def _small_allgather(name, v):
    n = v.shape[-1]

    def body(v_ref, out_ref, send_sems, recv_sems):
        x, y, c = _place()
        me = 4 * x + 2 * y + c
        out_ref[me] = v_ref[...]
        copies = []
        for k in range(1, N_DEV):
            peer = (_flip(x, (k >> 2) & 1), _flip(y, (k >> 1) & 1), _flip(c, k & 1))
            cp = pltpu.make_async_remote_copy(
                src_ref=v_ref, dst_ref=out_ref.at[me], send_sem=send_sems.at[k - 1],
                recv_sem=recv_sems.at[k - 1], device_id=peer, device_id_type=MESH)
            cp.start()
            copies.append(cp)
        for cp in copies:
            cp.wait()

    return pl.pallas_call(
        body, name=name,
        out_shape=jax.ShapeDtypeStruct((N_DEV, 1, n), F32),
        in_specs=[pl.BlockSpec(memory_space=pltpu.VMEM)],
        out_specs=pl.BlockSpec(memory_space=pltpu.VMEM),
        scratch_shapes=[pltpu.SemaphoreType.DMA((N_DEV - 1,)), pltpu.SemaphoreType.DMA((N_DEV - 1,))],
    )(v)


class _Plan:
    def __init__(self, operands, out_shapes, sems, phases):
        self.operands, self.out_shapes, self.sems, self.phases = operands, out_shapes, sems, phases


def _slab_start(base, rows, jump, idx):
    return pl.multiple_of(base + idx * rows + (idx // 4) * jump, 16)


def _gather_plan(shards, dst_of, base_of, dst_shapes, jump_of=None):
    n = len(shards)
    rows = [s.shape[0] for s in shards]
    jump_of = jump_of or [0] * n

    def phases(srcs, dsts, sems):
        send_sems, recv_sems, local_sems = sems
        x, y, c = _place()
        me, sibling = (x, y, c), (x, y, 1 - c)
        chips = [(1 - x, y), (x, 1 - y), (1 - x, 1 - y)]

        def slab(i, px, py, pc):
            start = _slab_start(base_of[i], rows[i], jump_of[i], 4 * px + 2 * py + pc)
            return dsts[dst_of[i]].at[pl.ds(start, rows[i])]

        def copy(i, k, block, to, src=None):
            return pltpu.make_async_remote_copy(
                src_ref=slab(i, *block) if src is None else src, dst_ref=slab(i, *block),
                send_sem=send_sems.at[i, k], recv_sem=recv_sems.at[i, k],
                device_id=to, device_id_type=MESH)

        def mine():
            return [pltpu.make_async_copy(srcs[i], slab(i, *me), local_sems.at[i]) for i in range(n)]

        def first():
            out = []
            for i in range(n):
                out.append(copy(i, 0, me, sibling, src=srcs[i]))
                out += [copy(i, 1 + j, me, (*chip, c), src=srcs[i]) for j, chip in enumerate(chips)]
            return out

        def passed():
            return [(copy(i, 1 + j, (*chip, c), me), copy(i, 4 + j, (*chip, c), sibling))
                    for j, chip in enumerate(chips) for i in range(n)]

        def start():
            for cp in mine() + first():
                cp.start()

        def middle():
            for landed, onward in passed():
                landed.wait_recv()
                onward.start()

        def finish():
            for i in range(n):
                copy(i, 0, sibling, me).wait_recv()
                for j, chip in enumerate(chips):
                    copy(i, 4 + j, (*chip, 1 - c), me).wait_recv()
            for cp in first() + [onward for _, onward in passed()]:
                cp.wait_send()
            for cp in mine():
                cp.wait()

        return start, middle, finish

    sems = [pltpu.SemaphoreType.DMA((n, 7)), pltpu.SemaphoreType.DMA((n, 7)), pltpu.SemaphoreType.DMA((n,))]
    return _Plan(list(shards), [jax.ShapeDtypeStruct(s, BF16) for s in dst_shapes], sems, phases)


def _scatter_plan(grads, src_of, base_of, rows, cols, jump_of=None):
    n = len(rows)
    jump_of = jump_of or [0] * n

    def phases(srcs, recvs, sems):
        send_sems, recv_sems, local_sems = sems
        x, y, c = _place()
        me = 4 * x + 2 * y + c

        def slab(i, idx):
            start = _slab_start(base_of[i], rows[i], jump_of[i], idx)
            return srcs[src_of[i]].at[pl.ds(start, rows[i])]

        def copies():
            out = [pltpu.make_async_copy(slab(i, me), recvs[i].at[me], local_sems.at[i]) for i in range(n)]
            for k in range(1, N_DEV):
                px, py, pc = _flip(x, (k >> 2) & 1), _flip(y, (k >> 1) & 1), _flip(c, k & 1)
                out += [pltpu.make_async_remote_copy(
                    src_ref=slab(i, 4 * px + 2 * py + pc), dst_ref=recvs[i].at[me],
                    send_sem=send_sems.at[i, k - 1], recv_sem=recv_sems.at[i, k - 1],
                    device_id=(px, py, pc), device_id_type=MESH) for i in range(n)]
            return out

        def start():
            for cp in copies():
                cp.start()

        def finish():
            for cp in copies():
                cp.wait()

        return start, None, finish

    sems = [pltpu.SemaphoreType.DMA((n, 7)), pltpu.SemaphoreType.DMA((n, 7)), pltpu.SemaphoreType.DMA((n,))]
    out_shapes = [jax.ShapeDtypeStruct((N_DEV, rows[i], cols[i]), BF16) for i in range(n)]
    return _Plan(list(grads), out_shapes, sems, phases)


def _run_plan_on_sequencer(name, plan, collective_id):
    src_refs = [jax.new_ref(a, memory_space=pltpu.MemorySpace.HBM) for a in plan.operands]
    dst_refs = [jax.empty_ref(s, memory_space=pltpu.MemorySpace.HBM) for s in plan.out_shapes]

    @pl.kernel(mesh=plsc.ScalarSubcoreMesh(axis_name="sequencer", num_cores=1), name=name,
               scratch_types=tuple(plan.sems),
               compiler_params=pltpu.CompilerParams(collective_id=collective_id))
    def launch(*sems):
        x, y, c = _place()
        barrier = pltpu.get_barrier_semaphore()
        for k in range(1, N_DEV):
            peer = (_flip(x, (k >> 2) & 1), _flip(y, (k >> 1) & 1), _flip(c, k & 1))
            pl.semaphore_signal(barrier, inc=1, device_id=peer, device_id_type=MESH)
        pl.semaphore_wait(barrier, N_DEV - 1)
        for phase in plan.phases(src_refs, dst_refs, sems):
            if phase is not None:
                phase()

    launch()
    return [r[...] for r in dst_refs]


def _mm(name, a, b, mode, out_dtype, tm, tn, tk, *, tiles_in=(), tiles_out=(), epilogue=None,
        n_outer=False, keep_b=False, col_chunks=None, after=()):
    if mode == "TN":
        kk, m = a.shape
    else:
        m, kk = a.shape
    n = b.shape[0] if mode == "NT" else b.shape[1]
    tm, tn, tk = min(tm, m), min(tn, n), min(tk, kk)
    assert m % tm == 0 and n % tn == 0 and kk % tk == 0, (name, m, n, kk, tm, tn, tk)
    ni, nj, nk = m // tm, n // tn, kk // tk
    dims = {"NN": NN_DIMS, "NT": NT_DIMS, "TN": TN_DIMS}[mode]
    if epilogue is None:
        tiles_out = [(jax.ShapeDtypeStruct((m, n), out_dtype), (tm, tn), lambda i, j: (i, j))]
    n_tin, n_tout = len(tiles_in), len(tiles_out)
    n_acc = 1 if nk > 1 else 0
    n_after = len(after)
    assert not keep_b or (nk == 1 and nj == 1)
    assert not col_chunks or (epilogue is not None and nk == 1 and mode != "TN")
    ij = (lambda p, q: (q, p)) if n_outer else (lambda p, q: (p, q))
    inner = ni if n_outer else nj

    def body(a_ref, b_ref, *rest):
        tin = rest[:n_tin]
        tout = rest[n_tin + n_after:n_tin + n_after + n_tout]
        scratch = rest[n_tin + n_after + n_tout:]
        k = pl.program_id(2)
        visit = pl.program_id(0) * inner + pl.program_id(1)
        if keep_b:
            b_kept, b_sem = scratch[n_acc:n_acc + 2]

            @pl.when((visit == 0) & (k == 0))
            def _():
                cp = pltpu.make_async_copy(b_ref, b_kept, b_sem)
                cp.start()
                cp.wait()

            b_ref = b_kept

        def store(prod, c=0, cols=()):
            if epilogue is None:
                tout[0][...] = prod.astype(out_dtype)
            else:
                epilogue(prod, jnp.logical_and(visit == 0, c == 0), tin, tout, *cols)

        if col_chunks:
            for c, (c0, cw) in enumerate(col_chunks):
                b_part = b_ref[pl.ds(c0, cw), :] if mode == "NT" else b_ref[:, pl.ds(c0, cw)]
                store(lax.dot_general(a_ref[...], b_part, dims, preferred_element_type=F32), c, ((c0, cw),))
        else:
            part = lax.dot_general(a_ref[...], b_ref[...], dims, preferred_element_type=F32)
            if nk == 1:
                store(part)
            else:
                acc_ref = scratch[0]

                @pl.when(k == 0)
                def _():
                    acc_ref[...] = part

                @pl.when((k > 0) & (k < nk - 1))
                def _():
                    acc_ref[...] += part

                @pl.when(k == nk - 1)
                def _():
                    store(acc_ref[...] + part)

    def spec(shape, fn):
        return pl.BlockSpec(shape, lambda p, q, k: fn(*ij(p, q)))

    a_spec = (pl.BlockSpec((tk, tm), lambda p, q, k: (k, ij(p, q)[0])) if mode == "TN"
              else pl.BlockSpec((tm, tk), lambda p, q, k: (ij(p, q)[0], k)))
    if keep_b:
        b_spec = pl.BlockSpec(memory_space=pl.ANY)
    elif mode == "NT":
        b_spec = pl.BlockSpec((tn, tk), lambda p, q, k: (ij(p, q)[1], k))
    else:
        b_spec = pl.BlockSpec((tk, tn), lambda p, q, k: (k, ij(p, q)[1]))
    sequential = epilogue or keep_b
    out = pl.pallas_call(
        body, name=name, grid=(nj, ni, nk) if n_outer else (ni, nj, nk),
        out_shape=[t[0] for t in tiles_out],
        in_specs=([a_spec, b_spec] + [spec(t[1], t[2]) for t in tiles_in]
                  + [pl.BlockSpec(memory_space=pl.ANY)] * n_after),
        out_specs=[spec(t[1], t[2]) for t in tiles_out],
        scratch_shapes=([pltpu.VMEM((tm, tn), F32)] * n_acc
                        + ([pltpu.VMEM(b.shape, b.dtype), pltpu.SemaphoreType.DMA] if keep_b else [])),
        compiler_params=(_params("arbitrary", "arbitrary", "arbitrary") if sequential
                         else _params("parallel", "parallel", "arbitrary")),
    )(a, b, *[t[0] for t in tiles_in], *after)
    return out if epilogue else out[0]


def _row(tm, w, off=0):
    return pl.BlockSpec((tm, w), lambda i: (i, off))


def _vec(w):
    return pl.BlockSpec((1, w), lambda i: (0, 0))


def _sigmoid(x):
    return 0.5 * jnp.tanh(0.5 * x) + 0.5


def _normmod(name, x, g, sc, sh, tm=1024):
    s = x.shape[0]

    def body(x_ref, g_ref, sc_ref, sh_ref, h_ref):
        xv = x_ref[...]
        r = lax.rsqrt(jnp.mean(xv * xv, axis=-1, keepdims=True) + EPS)
        h_ref[...] = ((xv * r) * g_ref[...] * (1.0 + sc_ref[...]) + sh_ref[...]).astype(BF16)

    return pl.pallas_call(
        body, name=name, grid=(s // tm,),
        out_shape=jax.ShapeDtypeStruct((s, D), BF16),
        in_specs=[_row(tm, D), _vec(D), _vec(D), _vec(D)], out_specs=_row(tm, D),
        compiler_params=_params("parallel"),
    )(x, g, sc, sh)


def _heads(x, fn):
    return jnp.concatenate([fn(x[:, h * HD:(h + 1) * HD], h) for h in range(x.shape[1] // HD)], axis=1)


def _qknorm(proj, wqk, tm=512):
    s = proj.shape[0]

    def body(p_ref, w_ref, o_ref):
        pv = p_ref[...].astype(F32)
        wv = w_ref[...]

        def one(qh, h):
            r = lax.rsqrt(jnp.mean(qh * qh, axis=-1, keepdims=True) + EPS)
            return (qh * r) * wv[:, h * HD:(h + 1) * HD]

        o_ref[...] = _heads(pv, one).astype(BF16)

    return pl.pallas_call(
        body, name="qknorm", grid=(s // tm,),
        out_shape=jax.ShapeDtypeStruct((s, QKW), BF16),
        in_specs=[pl.BlockSpec((tm, QKW), lambda i: (i, 0)), pl.BlockSpec((1, QKW), lambda i: (0, 0))],
        out_specs=pl.BlockSpec((tm, QKW), lambda i: (i, 0)),
        compiler_params=_params("parallel"),
    )(proj, wqk)


def _qknorm_bwd(name, proj, dn, w, dproj, blk0, tm=512):
    s, width = dn.shape

    def body(p_ref, d_ref, w_ref, _, o_ref, acc_ref):
        pv = p_ref[...].astype(F32)
        dv = d_ref[...]
        wv = w_ref[...]
        sums = []

        def one(qh, h):
            dn = dv[:, h * HD:(h + 1) * HD]
            r = lax.rsqrt(jnp.mean(qh * qh, axis=-1, keepdims=True) + EPS)
            nh = qh * r
            sums.append(jnp.sum(dn * nh, axis=0, keepdims=True))
            dnw = dn * wv[:, h * HD:(h + 1) * HD]
            return r * (dnw - nh * jnp.mean(dnw * nh, axis=-1, keepdims=True))

        o_ref[...] = _heads(pv, one).astype(BF16)

        @pl.when(pl.program_id(0) == 0)
        def _():
            acc_ref[...] = jnp.zeros_like(acc_ref)

        acc_ref[0:1, :] += jnp.concatenate(sums, axis=1)

    return pl.pallas_call(
        body, name=name, grid=(s // tm,),
        out_shape=[jax.ShapeDtypeStruct((s, IN_W), BF16), jax.ShapeDtypeStruct((8, width), F32)],
        in_specs=[pl.BlockSpec((tm, width), lambda i: (i, blk0)), pl.BlockSpec((tm, width), lambda i: (i, 0)),
                  pl.BlockSpec((1, width), lambda i: (0, 0)), pl.BlockSpec(memory_space=pl.ANY)],
        out_specs=[pl.BlockSpec((tm, width), lambda i: (i, blk0)), pl.BlockSpec((8, width), lambda i: (0, 0))],
        input_output_aliases={3: 0},
        compiler_params=_params("arbitrary"),
    )(proj, dn, w, dproj)


def _attn_shapes(s, g):
    d = DILATIONS[g]
    tb = min(s, max(2048, 256 * d))
    sb = min(128, tb // d)
    pb = BAND * d
    assert s % tb == 0 and tb % pb == 0 and (tb // d) % sb == 0 and sb % BAND == 0
    return d, tb, sb, pb


def _lanes(x, width):
    return jnp.concatenate([x] * (width // HD), axis=1)


def _every(start, size, d):
    return pl.ds(start, size, stride=d) if d > 1 else pl.ds(start, size)


def _attn_specs(g, tb, pb, s, ahead):
    ratio = tb // pb
    if ahead:
        nbr = lambda n: jnp.minimum((n + 1) * ratio, s // pb - 1)
    else:
        nbr = lambda n: jnp.maximum(n * ratio - 1, 0)
    cur = lambda base: pl.BlockSpec((tb, HD), lambda h, n: (n, base + g * N_HEADS + h))
    side = lambda base: pl.BlockSpec((pb, HD), lambda h, n: (nbr(n), base + g * N_HEADS + h))
    tok = pl.BlockSpec((tb, HD), lambda h, n: (n, h))
    tok_side = pl.BlockSpec((pb, HD), lambda h, n: (nbr(n), h))
    return cur, side, tok, tok_side


Q_COL, K_COL, V_COL = 0, 12, 24


def _attn_fwd(g, qkn, proj):
    s = qkn.shape[0]
    d, tb, sb, pb = _attn_shapes(s, g)
    ft = F32 if d > 1 else BF16
    nj = tb // d // sb
    scale = HD ** -0.5

    def body(q_ref, kc_ref, kp_ref, vc_ref, vp_ref, o_ref, lse_ref, qf, kf, vf):
        n = pl.program_id(1)
        qf[...] = q_ref[...].astype(ft)
        kf[0:pb] = kp_ref[...].astype(ft)
        kf[pb:] = kc_ref[...].astype(ft)
        vf[0:pb] = vp_ref[...].astype(ft)
        vf[pb:] = vc_ref[...].astype(ft)
        for r in range(d):
            for j in range(nj):
                at = j * sb * d + r
                q = qf[_every(at, sb, d), :].astype(BF16)
                k = kf[_every(at, sb + BAND, d), :].astype(BF16)
                v = vf[_every(at, sb + BAND, d), :].astype(BF16)
                sc = lax.dot_general(q, k, NT_DIMS, preferred_element_type=F32) * scale
                qi = lax.broadcasted_iota(jnp.int32, sc.shape, 0)
                kj = lax.broadcasted_iota(jnp.int32, sc.shape, 1)
                valid = (kj >= qi) & (kj <= qi + BAND)
                if j == 0:
                    valid = valid & ((kj >= BAND) | (n > 0))
                sc = jnp.where(valid, sc, -1e30)
                m = jnp.max(sc, axis=-1, keepdims=True)
                p = jnp.exp(sc - m)
                l = jnp.sum(p, axis=-1, keepdims=True)
                o = lax.dot_general(p.astype(BF16), v, NN_DIMS, preferred_element_type=F32)
                o_ref[_every(at, sb, d), :] = o / l
                lse_ref[_every(at, sb, d), :] = jnp.broadcast_to(m + jnp.log(l), (sb, HD))

    cur, side, tok, _ = _attn_specs(g, tb, pb, s, ahead=False)
    return pl.pallas_call(
        body, name=f"attn_fwd_g{g}", grid=(N_HEADS, s // tb),
        out_shape=[jax.ShapeDtypeStruct((s, COL), F32)] * 2,
        in_specs=[cur(Q_COL), cur(K_COL), side(K_COL), cur(V_COL), side(V_COL)],
        out_specs=[tok, tok],
        scratch_shapes=[pltpu.VMEM((tb, HD), ft), pltpu.VMEM((tb + pb, HD), ft),
                        pltpu.VMEM((tb + pb, HD), ft)],
        compiler_params=_params("parallel", "arbitrary"),
    )(qkn, qkn, qkn, proj, proj)


def _attn_combine(os_, lses, tm=1024):
    s = os_[0].shape[0]

    def body(o0, o1, o2, l0, l1, l2, o_ref, lse_ref):
        a, b, c = l0[...], l1[...], l2[...]
        m = jnp.maximum(jnp.maximum(a, b), c)
        ea, eb, ec = jnp.exp(a - m), jnp.exp(b - m), jnp.exp(c - m)
        tot = ea + eb + ec
        o_ref[...] = ((ea * o0[...] + eb * o1[...] + ec * o2[...]) / tot).astype(BF16)
        lse_ref[...] = m + jnp.log(tot)

    return pl.pallas_call(
        body, name="attn_combine", grid=(s // tm,),
        out_shape=[jax.ShapeDtypeStruct((s, COL), BF16), jax.ShapeDtypeStruct((s, COL), F32)],
        in_specs=[_row(tm, COL)] * 6, out_specs=[_row(tm, COL)] * 2,
        compiler_params=_params("parallel"),
    )(*os_, *lses)


def _attn_bwd(g, qkn, proj, do, lse, delta, dqn, dkn, dproj):
    s = qkn.shape[0]
    d, tb, sb, pb = _attn_shapes(s, g)
    ft = F32 if d > 1 else BF16
    nj = tb // d // sb
    nt = s // tb
    scale = HD ** -0.5
    chained = dqn is not None

    def body(k_ref, v_ref, qc_ref, qn_ref, doc_ref, don_ref, lc_ref, ln_ref, dc_ref, dn_ref, *rest):
        dq_ref, dk_ref, dv_ref, kf, vf, qf, dvf, later = rest[-8:]
        n = pl.program_id(1)
        kf[...] = k_ref[...].astype(ft)
        vf[...] = v_ref[...].astype(ft)
        qf[0:tb] = qc_ref[...].astype(ft)
        qf[tb:] = qn_ref[...].astype(ft)

        @pl.when(n == 0)
        def _():
            later[...] = jnp.zeros_like(later)

        def window(c_ref, n_ref, r, j):
            at = j * sb * d + r
            if j < nj - 1:
                return c_ref[_every(at, sb + BAND, d), :]
            return jnp.concatenate([c_ref[_every(at, sb, d), :], n_ref[_every(r, BAND, d), :]], axis=0)

        for r in range(d):
            tail = later[r]
            for j in range(nj):
                at = j * sb * d + r
                rows = _every(at, sb, d)
                k = kf[rows, :].astype(BF16)
                v = vf[rows, :].astype(BF16)
                q = qf[_every(at, sb + BAND, d), :].astype(BF16)
                dov = window(doc_ref, don_ref, r, j).astype(BF16)
                sc = lax.dot_general(q, k, NT_DIMS, preferred_element_type=F32) * scale
                qi = lax.broadcasted_iota(jnp.int32, sc.shape, 0)
                kj = lax.broadcasted_iota(jnp.int32, sc.shape, 1)
                valid = (qi >= kj) & (qi <= kj + BAND)
                if j == nj - 1:
                    valid = valid & ((qi < sb) | (n < nt - 1))
                p = jnp.exp(jnp.where(valid, sc - _lanes(window(lc_ref, ln_ref, r, j), sb), -1e30))
                dp = lax.dot_general(dov, v, NT_DIMS, preferred_element_type=F32)
                ds = (p * (dp - _lanes(window(dc_ref, dn_ref, r, j), sb)) * scale).astype(BF16)
                dvf[rows, :] = lax.dot_general(p.astype(BF16), dov, TN_DIMS, preferred_element_type=F32)
                dk_ref[rows, :] = lax.dot_general(ds, q, TN_DIMS, preferred_element_type=F32)
                dqw = lax.dot_general(ds, k, NN_DIMS, preferred_element_type=F32)
                first = dqw[:BAND] + tail
                dq_ref[rows, :] = first if sb == BAND else jnp.concatenate([first, dqw[BAND:sb]], axis=0)
                tail = dqw[sb:]
            later[r] = tail
        dv_ref[...] = dvf[...].astype(BF16)

    cur, side, tok, tok_side = _attn_specs(g, tb, pb, s, ahead=True)
    anyspec = pl.BlockSpec(memory_space=pl.ANY)
    n_heads_cols = 3 * N_HEADS * HD
    return pl.pallas_call(
        body, name=f"attn_bwd_g{g}", grid=(N_HEADS, nt),
        out_shape=[jax.ShapeDtypeStruct((s, n_heads_cols), F32), jax.ShapeDtypeStruct((s, n_heads_cols), F32),
                   jax.ShapeDtypeStruct((s, IN_W), BF16)],
        in_specs=[cur(K_COL), cur(V_COL), cur(Q_COL), side(Q_COL), tok, tok_side, tok, tok_side,
                  tok, tok_side] + ([anyspec, anyspec] if chained else []) + [anyspec],
        out_specs=[cur(0), cur(0), cur(V_COL)],
        input_output_aliases={10: 0, 11: 1, 12: 2} if chained else {10: 2},
        scratch_shapes=[pltpu.VMEM((tb, HD), ft), pltpu.VMEM((tb, HD), ft),
                        pltpu.VMEM((tb + pb, HD), ft), pltpu.VMEM((tb, HD), F32),
                        pltpu.VMEM((d, BAND, HD), F32)],
        compiler_params=_params("arbitrary", "arbitrary"),
    )(qkn, proj, qkn, qkn, do, do, lse, lse, delta, delta, *([dqn, dkn] if chained else []), dproj)


def _shift_down(x, before, k):
    rolled = pltpu.roll(x, k, 0)
    head = jnp.where(lax.broadcasted_iota(jnp.int32, before.shape, 0) < k, pltpu.roll(before, k, 0), rolled[:8])
    return jnp.concatenate([head, rolled[8:]], axis=0)


def _shift_up(x, after, k):
    rows = x.shape[0]
    rolled = pltpu.roll(x, rows - k, 0)
    tail = jnp.where(lax.broadcasted_iota(jnp.int32, after.shape, 0) >= 8 - k,
                     pltpu.roll(after, 8 - k, 0), rolled[rows - 8:])
    return jnp.concatenate([rolled[:rows - 8], tail], axis=0)


def _conv_fwd(proj, cw, tm=1024):
    s = proj.shape[0]
    r16 = tm // 16

    def body(u_ref, b_ref, c_ref, up_ref, cp_ref, w_ref, z_ref):
        i = pl.program_id(1)
        xc = c_ref[...].astype(F32) * u_ref[...].astype(F32)
        xp = jnp.where(i > 0, cp_ref[8:16, :].astype(F32) * up_ref[8:16, :].astype(F32), 0.0)
        w = w_ref[...]
        conv = _shift_down(xc, xp, 2) * w[0:1] + _shift_down(xc, xp, 1) * w[1:2] + xc * w[2:3]
        z_ref[...] = (b_ref[...].astype(F32) * conv).astype(BF16)

    tile = lambda blk: pl.BlockSpec((tm, COL), lambda j, i: (i, blk + j))
    before = lambda blk: pl.BlockSpec((16, COL), lambda j, i: (jnp.maximum(i * r16 - 1, 0), blk + j))
    return pl.pallas_call(
        body, name="conv_fwd", grid=(D // COL, s // tm),
        out_shape=jax.ShapeDtypeStruct((s, D), BF16),
        in_specs=[tile(U_BLK), tile(B_BLK), tile(C_BLK), before(U_BLK), before(C_BLK),
                  pl.BlockSpec((3, COL), lambda j, i: (0, j))],
        out_specs=pl.BlockSpec((tm, COL), lambda j, i: (i, j)),
        compiler_params=_params("parallel", "parallel"),
    )(proj, proj, proj, proj, proj, cw)


def _conv_bwd(dz, proj, cw, dproj, tm=1024):
    s = proj.shape[0]
    r16 = tm // 16
    nrow = s // tm

    def body(dz_ref, u_ref, b_ref, c_ref, up_ref, cp_ref, dzn_ref, bn_ref, w_ref, _, o_ref, dc_ref, acc_ref):
        piece, i = pl.program_id(1), pl.program_id(2)
        u, c = u_ref[...].astype(F32), c_ref[...].astype(F32)
        bv = b_ref[...].astype(F32)
        dzv = dz_ref[...]
        w = w_ref[...]

        @pl.when((piece == 0) & (i == 0))
        def _():
            acc_ref[...] = jnp.zeros_like(acc_ref)

        @pl.when(piece == 0)
        def _():
            xc = c * u
            xp = jnp.where(i > 0, cp_ref[8:16, :].astype(F32) * up_ref[8:16, :].astype(F32), 0.0)
            x2, x1 = _shift_down(xc, xp, 2), _shift_down(xc, xp, 1)
            o_ref[...] = (dzv * (x2 * w[0:1] + x1 * w[1:2] + xc * w[2:3])).astype(BF16)
            dc_ref[...] = jnp.zeros_like(dc_ref)
            dconv = dzv * bv
            acc_ref[0:1, :] += jnp.sum(dconv * x2, axis=0, keepdims=True)
            acc_ref[1:2, :] += jnp.sum(dconv * x1, axis=0, keepdims=True)
            acc_ref[2:3, :] += jnp.sum(dconv * xc, axis=0, keepdims=True)

        @pl.when(piece == 1)
        def _():
            dconv = dzv * bv
            dn = jnp.where(i < nrow - 1, dzn_ref[...] * bn_ref[0:8, :].astype(F32), 0.0)
            dxc = dconv * w[2:3] + _shift_up(dconv, dn, 1) * w[1:2] + _shift_up(dconv, dn, 2) * w[0:1]
            o_ref[...] = (dxc * c).astype(BF16)
            dc_ref[...] = (dxc * u).astype(BF16)

    tile = lambda blk: pl.BlockSpec((tm, COL), lambda j, p, i: (i, blk + j))
    before = lambda blk: pl.BlockSpec((16, COL), lambda j, p, i: (jnp.maximum(i * r16 - 1, 0), blk + j))
    after = lambda rows, blk: pl.BlockSpec(
        (rows, COL), lambda j, p, i: (jnp.minimum((i + 1) * (tm // rows), s // rows - 1), blk + j))
    return pl.pallas_call(
        body, name="conv_bwd", grid=(D // COL, 2, nrow),
        out_shape=[jax.ShapeDtypeStruct((s, IN_W), BF16), jax.ShapeDtypeStruct((s + tm, D), BF16),
                   jax.ShapeDtypeStruct((8, D), F32)],
        in_specs=[tile(0), tile(U_BLK), tile(B_BLK), tile(C_BLK), before(U_BLK), before(C_BLK),
                  after(8, 0), after(16, B_BLK), pl.BlockSpec((3, COL), lambda j, p, i: (0, j)),
                  pl.BlockSpec(memory_space=pl.ANY)],
        out_specs=[pl.BlockSpec((tm, COL), lambda j, p, i: (i, jnp.where(p == 0, B_BLK, U_BLK) + j)),
                   pl.BlockSpec((tm, COL), lambda j, p, i: (jnp.where(p == 0, nrow, i), j)),
                   pl.BlockSpec((8, COL), lambda j, p, i: (0, j))],
        input_output_aliases={9: 0},
        compiler_params=_params("arbitrary", "arbitrary", "arbitrary"),
    )(dz, proj, proj, proj, proj, proj, dz, proj, cw, dproj)


def _copy_columns(name, src, dst, blk0, tm=2048):
    s, w = dst.shape[0], src.shape[1]
    fresh = isinstance(dst, jax.ShapeDtypeStruct)

    def body(x_ref, *rest):
        rest[-1][...] = x_ref[...]

    return pl.pallas_call(
        body, name=name, grid=(w // COL, s // tm),
        out_shape=jax.ShapeDtypeStruct(dst.shape, dst.dtype),
        in_specs=[pl.BlockSpec((tm, COL), lambda j, i: (i, j))] + ([] if fresh else [pl.BlockSpec(memory_space=pl.ANY)]),
        out_specs=pl.BlockSpec((tm, COL), lambda j, i: (i, blk0 + j)),
        input_output_aliases={} if fresh else {1: 0},
        compiler_params=_params("parallel", "parallel"),
    )(src, *([] if fresh else [dst]))


def _mod_part(c_all, w_ada, b_part):
    def body(c_ref, w_ref, b_ref, o_ref):
        cv = c_ref[...]
        act = cv * _sigmoid(cv)
        o_ref[...] = jnp.dot(act, w_ref[...], preferred_element_type=F32,
                             precision=lax.Precision.HIGHEST) + b_ref[...]

    return pl.pallas_call(
        body, name="mod_part", out_shape=jax.ShapeDtypeStruct((N_DEV, w_ada.shape[1]), F32),
    )(c_all, w_ada, b_part)


def _w_ada_grad(c_all_t, dmod_part):
    def body(c_ref, d_ref, o_ref):
        cv = c_ref[...]
        act = cv * _sigmoid(cv)
        dv = d_ref[...]
        acc = act[:, 0:1] * dv[0:1, :]
        for b in range(1, N_DEV):
            acc = acc + act[:, b:b + 1] * dv[b:b + 1, :]
        o_ref[...] = acc

    return pl.pallas_call(
        body, name="w_ada_grad", out_shape=jax.ShapeDtypeStruct((D, dmod_part.shape[1]), F32),
    )(c_all_t, dmod_part)


def _sum_rows(name, v):
    def body(v_ref, o_ref):
        acc = v_ref[0]
        for k in range(1, N_DEV):
            acc = acc + v_ref[k]
        o_ref[...] = acc

    return pl.pallas_call(body, name=name, out_shape=jax.ShapeDtypeStruct(v.shape[1:], F32))(v)


def _adamw(name, w, g, m, v):
    rows, cols = w.shape
    limit = max(16, (1 << 20) // (4 * cols))
    tr = rows if rows <= limit else next((t for t in range(limit - limit % 16, 15, -16) if rows % t == 0), rows)
    c1 = 1.0 - ADAM_B1 ** ADAM_STEP
    c2 = 1.0 - ADAM_B2 ** ADAM_STEP
    parts = g.ndim == 3

    def body(w_ref, g_ref, m_ref, v_ref, go_ref, d_ref, nm_ref, nv_ref):
        if parts:
            gv = g_ref[0].astype(F32)
            for k in range(1, N_DEV):
                gv = gv + g_ref[k].astype(F32)
        else:
            gv = g_ref[...]
        go_ref[...] = gv
        nm = ADAM_B1 * m_ref[...] + (1.0 - ADAM_B1) * gv
        nv = ADAM_B2 * v_ref[...] + (1.0 - ADAM_B2) * (gv * gv)
        nm_ref[...] = nm
        nv_ref[...] = nv
        d_ref[...] = -ADAM_LR * ((nm / c1) / (jnp.sqrt(nv / c2) + ADAM_EPS) + ADAM_WD * w_ref[...])

    spec = pl.BlockSpec((tr, cols), lambda i: (i, 0))
    g_spec = pl.BlockSpec((N_DEV, tr, cols), lambda i: (0, i, 0)) if parts else spec
    return pl.pallas_call(
        body, name=name, grid=(rows // tr,),
        out_shape=[jax.ShapeDtypeStruct((rows, cols), F32)] * 4,
        in_specs=[spec, g_spec, spec, spec], out_specs=[spec] * 4,
        compiler_params=_params("parallel"),
    )(w, g, m, v)


def _adamw_small(ws, gs, ms, vs):
    n = len(ws)
    c1 = 1.0 - ADAM_B1 ** ADAM_STEP
    c2 = 1.0 - ADAM_B2 ** ADAM_STEP

    def body(*refs):
        for i in range(n):
            w_ref, g_ref, m_ref, v_ref = refs[i], refs[n + i], refs[2 * n + i], refs[3 * n + i]
            d_ref, nm_ref, nv_ref = refs[4 * n + 3 * i:4 * n + 3 * i + 3]
            gv = g_ref[...]
            nm = ADAM_B1 * m_ref[...] + (1.0 - ADAM_B1) * gv
            nv = ADAM_B2 * v_ref[...] + (1.0 - ADAM_B2) * (gv * gv)
            nm_ref[...] = nm
            nv_ref[...] = nv
            d_ref[...] = -ADAM_LR * ((nm / c1) / (jnp.sqrt(nv / c2) + ADAM_EPS) + ADAM_WD * w_ref[...])

    outs = pl.pallas_call(
        body, name="adamw_small",
        out_shape=[jax.ShapeDtypeStruct(w.shape, F32) for w in ws for _ in range(3)],
    )(*ws, *gs, *ms, *vs)
    return [tuple(outs[3 * i:3 * i + 3]) for i in range(n)]


HALF = FF // 2


def _sds(shape, dtype):
    return jax.ShapeDtypeStruct(shape, dtype)


def _row_tile(w):
    return lambda tm: ((tm, w), lambda i, j: (i, 0))


def _one(w):
    return lambda rows: ((rows, w), lambda i, j: (0, 0))


def _gate_up_swiglu(name, h, wgu, tm=1024):
    s = h.shape[0]
    tm = min(tm, s)

    def epilogue(prod, first, tin, tout):
        pq_ref, s_ref = tout
        a, b = prod[:, :HALF], prod[:, HALF:]
        sig = _sigmoid(a)
        act = a * sig
        pq_ref[:, :HALF] = (b * (sig * (1.0 + a * (1.0 - sig)))).astype(BF16)
        pq_ref[:, HALF:] = act.astype(BF16)
        s_ref[...] = (act * b).astype(BF16)

    return _mm(name, h, wgu, "NT", None, tm, FF, D, n_outer=True, epilogue=epilogue,
               tiles_out=[(_sds((s, 2 * FF), BF16), (tm, FF), lambda i, j: (i, j)),
                          (_sds((s, FF), BF16), (tm, HALF), lambda i, j: (i, j))])


def _d_hidden_swiglu(name, df, wd, ab, after=(), tm=1024):
    s = df.shape[0]
    tm = min(tm, s)

    def epilogue(prod, first, tin, tout, cols):
        da_cols = slice(cols[0], cols[0] + cols[1])
        db_cols = slice(HALF + cols[0], HALF + cols[0] + cols[1])
        tout[0][:, da_cols] = (prod * tin[0][:, da_cols].astype(F32)).astype(BF16)
        tout[0][:, db_cols] = (prod * tin[0][:, db_cols].astype(F32)).astype(BF16)

    chunks = [(c0, min(384, HALF - c0)) for c0 in range(0, HALF, 384)]
    return _mm(name, df, wd, "NT", None, tm, HALF, D, n_outer=True, epilogue=epilogue, col_chunks=chunks, after=after,
               tiles_in=[(ab, (tm, FF), lambda i, j: (i, j))],
               tiles_out=[(_sds((s, 2 * FF), BF16), (tm, FF), lambda i, j: (i, j))])[0]


def _out_residual(name, a, w, x, gt, coef, nxt, tm=512, tk=FF):
    s = a.shape[0]
    tm = min(tm, s)

    def epilogue(prod, first, tin, tout):
        x_ref, gt_ref, g_ref, sc_ref, sh_ref = tin
        f_ref, xn_ref, h_ref = tout
        f_ref[...] = prod
        xn = x_ref[...] + (coef * gt_ref[...]) * prod
        xn_ref[...] = xn
        r = lax.rsqrt(jnp.mean(xn * xn, axis=-1, keepdims=True) + EPS)
        h_ref[...] = ((xn * r) * g_ref[...] * (1.0 + sc_ref[...]) + sh_ref[...]).astype(BF16)

    row, vec = _row_tile(D)(tm), _one(D)(1)
    return _mm(name, a, w, "NN", None, tm, D, tk, epilogue=epilogue,
               tiles_in=[(x, *row), (gt, *vec)] + [(v, *vec) for v in nxt],
               tiles_out=[(_sds((s, D), F32), *row), (_sds((s, D), F32), *row), (_sds((s, D), BF16), *row)])


def _out_loss(name, a, w, x, gt, coef, target, tm=512):
    s = a.shape[0]
    tm = min(tm, s)

    def epilogue(prod, first, tin, tout):
        x_ref, gt_ref, t_ref = tin
        f_ref, g_ref, df_ref, acc_ref = tout
        f_ref[...] = prod
        cg = coef * gt_ref[...]
        e = x_ref[...] + cg * prod - t_ref[...]
        gv = e * (1.0 / D)
        g_ref[...] = gv
        df_ref[...] = (cg * gv).astype(BF16)

        @pl.when(first)
        def _():
            acc_ref[...] = jnp.zeros_like(acc_ref)

        acc_ref[0:1, :] += coef * jnp.sum(gv * prod, axis=0, keepdims=True)
        acc_ref[1:2, :] += (0.5 / D) * jnp.sum(e * e, axis=0, keepdims=True)

    row, vec = _row_tile(D)(tm), _one(D)(1)
    return _mm(name, a, w, "NN", None, tm, D, FF, epilogue=epilogue,
               tiles_in=[(x, *row), (gt, *vec), (target, *row)],
               tiles_out=[(_sds((s, D), F32), *row), (_sds((s, D), F32), *row), (_sds((s, D), BF16), *row),
                          (_sds((8, D), F32), *_one(D)(8))])


def _d_h_norm_bwd(name, da, w, x, gin, g, sc, sh, before=None, after=(), tm=256):
    s = da.shape[0]
    tm = min(tm, s)
    coef = before[2] if before else None

    def epilogue(prod, first, tin, tout):
        x_ref, gin_ref, g_ref, sc_ref, sh_ref = tin[:5]
        gout_ref, acc_ref = tout[:2]
        xv = x_ref[...]
        r = lax.rsqrt(jnp.mean(xv * xv, axis=-1, keepdims=True) + EPS)
        nv = xv * r
        gv, one_sc = g_ref[...], 1.0 + sc_ref[...]
        dn = prod * gv * one_sc
        gout = gin_ref[...] + r * (dn - nv * jnp.mean(dn * nv, axis=-1, keepdims=True))
        gout_ref[...] = gout

        @pl.when(first)
        def _():
            acc_ref[...] = jnp.zeros_like(acc_ref)

        dhn = prod * nv
        acc_ref[0:1, :] += jnp.sum(prod, axis=0, keepdims=True)
        acc_ref[1:2, :] += jnp.sum(dhn * gv, axis=0, keepdims=True)
        acc_ref[2:3, :] += jnp.sum(dhn * one_sc, axis=0, keepdims=True)
        if before:
            f_ref, gt_ref = tin[5:]
            tout[2][...] = ((coef * gt_ref[...]) * gout).astype(BF16)
            acc_ref[3:4, :] += coef * jnp.sum(gout * f_ref[...], axis=0, keepdims=True)

    row, vec = _row_tile(D)(tm), _one(D)(1)
    tiles_in = [(x, *row), (gin, *row), (g, *vec), (sc, *vec), (sh, *vec)]
    tiles_out = [(_sds((s, D), F32), *row), (_sds((8, D), F32), *_one(D)(8))]
    if before:
        tiles_in += [(before[0], *row), (before[1], *vec)]
        tiles_out.append((_sds((s, D), BF16), *row))
    return _mm(name, da, w, "NN", None, tm, D, da.shape[1], epilogue=epilogue, keep_b=True, after=after,
               tiles_in=tiles_in, tiles_out=tiles_out)


def _gate_tiles(proj, tm):
    return [(proj, (tm, COL), (lambda i, j, blk=blk: (i, blk))) for blk in (GA_BLK, GA_BLK + 1, GC_BLK, GC_BLK + 1)]


def _conv_branch_merge(z, wc, ya, proj, tm=1024):
    s = z.shape[0]
    tm = min(tm, s)

    def epilogue(prod, first, tin, tout):
        ya_ref, ga0, ga1, gc0, gc1 = tin
        tout[0][...] = prod.astype(BF16)
        for half, (ga, gc) in enumerate(((ga0, gc0), (ga1, gc1))):
            cols = slice(half * COL, (half + 1) * COL)
            tout[1][:, cols] = (_sigmoid(ga[...].astype(F32)) * ya_ref[:, cols].astype(F32)
                                + _sigmoid(gc[...].astype(F32)) * prod[:, cols]).astype(BF16)

    row = _row_tile(D)(tm)
    return _mm("mix_conv_branch", z, wc, "NN", None, tm, D, D, epilogue=epilogue,
               tiles_in=[(ya, *row)] + _gate_tiles(proj, tm),
               tiles_out=[(_sds((s, D), BF16), *row), (_sds((s, D), BF16), *row)])


def _d_merged_branches(dmix, wo, ya, yc, proj, tm=1024):
    s = dmix.shape[0]
    tm = min(tm, s)

    def epilogue(prod, first, tin, tout):
        ya_ref, yc_ref, ga0, ga1, gc0, gc1 = tin
        dya_ref, dyc_ref, dg_ref = tout
        for half, (ga, gc) in enumerate(((ga0, gc0), (ga1, gc1))):
            cols = slice(half * COL, (half + 1) * COL)
            dm = prod[:, cols]
            for y_ref, g_ref, dy_ref, off in ((ya_ref, ga, dya_ref, 0), (yc_ref, gc, dyc_ref, D)):
                sig = _sigmoid(g_ref[...].astype(F32))
                dms = dm * sig
                dy_ref[:, cols] = dms.astype(BF16)
                dg_ref[:, off + half * COL:off + (half + 1) * COL] = (
                    dms * y_ref[:, cols].astype(F32) * (1.0 - sig)).astype(BF16)

    row = _row_tile(D)(tm)
    return _mm("mix_d_merged", dmix, wo, "NT", None, tm, D, D, epilogue=epilogue,
               tiles_in=[(ya, *row), (yc, *row)] + _gate_tiles(proj, tm),
               tiles_out=[(_sds((s, D), BF16), *row), (_sds((s, D), BF16), *row),
                          (_sds((s, 2 * D), BF16), *_row_tile(2 * D)(tm))])


def _d_o_delta(dya, wa_t, o, tm=1024):
    s = dya.shape[0]
    tm = min(tm, s)

    def epilogue(prod, first, tin, tout):
        tout[0][...] = prod
        tout[1][...] = _heads(prod * tin[0][...].astype(F32), lambda ph, h: jnp.broadcast_to(
            jnp.sum(ph, axis=-1, keepdims=True), ph.shape))

    row = _row_tile(COL)(tm)
    return _mm("mix_d_o", dya, wa_t, "NN", None, tm, COL, D, epilogue=epilogue,
               tiles_in=[(o, *row)], tiles_out=[(_sds((s, COL), F32), *row), (_sds((s, COL), F32), *row)])


def _ffn_bwd(tag, df, x, gin, h, ab, sw, g, sc, sh, wgu, wd, before=None, tk_dw=2048):
    dwd = _mm(f"{tag}_dw_down", sw, df, "TN", BF16, HALF, D, tk_dw)
    dab = _d_hidden_swiglu(f"{tag}_d_hidden", df, wd, ab, after=[dwd])
    dwgu = _mm(f"{tag}_dw_gate_up", dab, h, "TN", BF16, HALF, D, tk_dw)
    res = _d_h_norm_bwd(f"{tag}_d_h", dab, wgu, x, gin, g, sc, sh, before=before, after=[dwgu], tm=512)
    return res, dwgu, dwd


def kernel(x, c, w_ada, b_ada, norm_ffn1, ffn1_w_gate, ffn1_w_up, ffn1_w_down, norm_mix, w_in, q_norm, k_norm, conv_w, w_attn_branch, w_conv_branch, w_out, norm_ffn2, ffn2_w_gate, ffn2_w_up, ffn2_w_down, loss_target, m_w_ada, m_b_ada, m_norm_ffn1, m_ffn1_w_gate, m_ffn1_w_up, m_ffn1_w_down, m_norm_mix, m_w_in, m_q_norm, m_k_norm, m_conv_w, m_w_attn_branch, m_w_conv_branch, m_w_out, m_norm_ffn2, m_ffn2_w_gate, m_ffn2_w_up, m_ffn2_w_down, v_w_ada, v_b_ada, v_norm_ffn1, v_ffn1_w_gate, v_ffn1_w_up, v_ffn1_w_down, v_norm_mix, v_w_in, v_q_norm, v_k_norm, v_conv_w, v_w_attn_branch, v_w_conv_branch, v_w_out, v_norm_ffn2, v_ffn2_w_gate, v_ffn2_w_up, v_ffn2_w_down):
    me = 4 * lax.axis_index("x") + 2 * lax.axis_index("y") + lax.axis_index("c")
    x0, target = x[0], loss_target[0]
    s = x0.shape[0]
    ada_cols = w_ada.shape[2]
    cw_cols = conv_w.shape[2]

    gathered = _small_allgather(
        "gather_c_conv", jnp.concatenate([c, conv_w[0].reshape(1, 3 * cw_cols)], axis=1))[:, 0]
    c_all = gathered[:, :D]
    cw = gathered[:, D:].reshape(N_DEV, 3, cw_cols).transpose(1, 0, 2).reshape(3, D)
    b_part = lax.dynamic_slice(b_ada, (0, me * ada_cols), (1, ada_cols))
    mod_part = _mod_part(c_all, w_ada[0], b_part)
    mod_all = _small_allgather("gather_mod", mod_part.reshape(1, N_DEV * ada_cols))
    mod = lax.dynamic_slice(mod_all.reshape(N_DEV, N_DEV, ada_cols), (0, me, 0), (N_DEV, 1, ada_cols))
    mod = mod.reshape(N_MOD, 1, D)
    sh1, sc1, gt1, sh2, sc2, gt2, sh3, sc3, gt3 = [mod[i] for i in range(N_MOD)]

    tb = lambda w: w[0].T.astype(BF16)
    nb = lambda w: w[0].astype(BF16)
    ffn1_shards = [tb(ffn1_w_gate), tb(ffn1_w_up), nb(ffn1_w_down)]
    ffn2_shards = [tb(ffn2_w_gate), tb(ffn2_w_up), nb(ffn2_w_down)]
    mix_shards = [tb(w_in), tb(w_attn_branch), nb(w_conv_branch), nb(w_out)]
    ffn_dst, ffn_base, ffn_jump, ffn_shapes = [0, 0, 1], [0, HALF, 0], [HALF, HALF, 0], [(2 * FF, D), (FF, D)]
    mix_dst, mix_base, mix_shapes = [0, 1, 2, 3], [0, 0, 0, 0], [(IN_W, D), (D, COL), (D, D), (D, D)]
    (wgu1,) = _run_plan_on_sequencer(
        "gather_ffn1_gate_up", _gather_plan(ffn1_shards[:2], ffn_dst[:2], ffn_base[:2], ffn_shapes[:1], ffn_jump[:2]), 1)
    (wd1,) = _run_plan_on_sequencer(
        "gather_ffn1_down", _gather_plan(ffn1_shards[2:], [0], [0], ffn_shapes[1:]), 8)
    win_t, wa_t, wc, wo = _run_plan_on_sequencer(
        "gather_mix_weights", _gather_plan(mix_shards, mix_dst, mix_base, mix_shapes), 2)
    wgu2, wd2 = _run_plan_on_sequencer(
        "gather_ffn2_weights", _gather_plan(ffn2_shards, ffn_dst, ffn_base, ffn_shapes, ffn_jump), 3)

    h1 = _normmod("ffn1_normmod", x0, norm_ffn1, sc1, sh1)
    ab1, s1 = _gate_up_swiglu("ffn1_gate_up", h1, wgu1)
    f1, x1, h2 = _out_residual("ffn1_down", s1, wd1, x0, gt1, 0.5, (norm_mix, sc2, sh2))
    proj = _mm("mix_in_proj", h2, win_t, "NT", BF16, 1024, IN_W // 4, D, n_outer=True)
    wqk = jnp.concatenate([jnp.tile(q_norm, (1, 12)), jnp.tile(k_norm, (1, 12))], axis=1)
    qkn = _qknorm(proj, wqk)
    group_out = [_attn_fwd(g, qkn, proj) for g in range(3)]
    o, lse = _attn_combine([go[0] for go in group_out], [go[1] for go in group_out])
    ya = _mm("mix_attn_branch", o, wa_t, "NT", BF16, 1024, 1024, COL)
    z = _conv_fwd(proj, cw)
    yc, merged = _conv_branch_merge(z, wc, ya, proj)
    mix, x2, h3 = _out_residual("mix_out_proj", merged, wo, x1, gt2, 1.0, (norm_ffn2, sc3, sh3), tm=1024, tk=D)
    ab3, s3 = _gate_up_swiglu("ffn2_gate_up", h3, wgu2)
    f3, g3, df3, acc_out = _out_loss("ffn2_down", s3, wd2, x2, gt3, 0.5, target)
    loss_part = jnp.sum(acc_out[1])

    ffn_rows = [sh_.shape[0] for sh_ in ffn1_shards]
    mix_rows = [sh_.shape[0] for sh_ in mix_shards]
    (g2, acc3, dmix), dwgu2, dwd2 = _ffn_bwd(
        "ffn2", df3, x2, g3, h3, ab3, s3, norm_ffn2, sc3, sh3, wgu2, wd2, before=(mix, gt2, 1.0))
    dya, dyc, dgates = _d_merged_branches(dmix, wo, ya, yc, proj)
    dwo = _mm("mix_dw_out", merged, dmix, "TN", BF16, 1024, 1024, 2048)
    dproj = _copy_columns("dproj_gates", dgates, jax.ShapeDtypeStruct((s, IN_W), BF16), GA_BLK)
    dwc = _mm("mix_dw_conv_branch", z, dyc, "TN", BF16, 1024, 1024, 2048)
    dz = _mm("mix_d_z", dyc, wc, "NT", F32, 1024, 1024, D)
    dproj, d_c, cw_acc = _conv_bwd(dz, proj, cw, dproj)
    dproj = _copy_columns("copy_d_c", d_c, dproj, C_BLK)
    dwa_t = _mm("mix_dw_attn_branch", dya, o, "TN", BF16, 1024, COL, 2048)
    do, delta = _d_o_delta(dya, wa_t, o)
    dqn = dkn = None
    for g in range(3):
        dqn, dkn, dproj = _attn_bwd(g, qkn, proj, do, lse, delta, dqn, dkn, dproj)
    dproj, wq_acc = _qknorm_bwd("qnorm_bwd", proj, dqn, wqk[:, :QKW // 2], dproj, 0)
    dproj, wk_acc = _qknorm_bwd("knorm_bwd", proj, dkn, wqk[:, QKW // 2:], dproj, 1)
    r_f2g, r_f2u, r_f2d, r_wa, r_wc, r_wo = _run_plan_on_sequencer(
        "scatter_ffn2_and_branch_grads",
        _scatter_plan([dwgu2, dwd2, dwa_t, dwc, dwo], [0, 0, 1, 2, 3, 4], [0, HALF, 0, 0, 0, 0],
                      ffn_rows + mix_rows[1:], [D, D, D, COL, D, D], [HALF, HALF, 0, 0, 0, 0]), 4)
    dwin_t = _mm("mix_dw_in", dproj, h2, "TN", BF16, IN_W // 4, COL, 2048)
    (r_win,) = _run_plan_on_sequencer(
        "scatter_w_in_grad", _scatter_plan([dwin_t], [0], [0], mix_rows[:1], [D]), 5)
    g1, acc2, df1 = _d_h_norm_bwd("mix_d_h", dproj, win_t, x1, g2, norm_mix, sc2, sh2, before=(f1, gt1, 0.5),
                                  after=[dwin_t])
    dwd1 = _mm("ffn1_dw_down", s1, df1, "TN", BF16, HALF, D, 2048)
    (r_f1d,) = _run_plan_on_sequencer(
        "scatter_ffn1_down_grad", _scatter_plan([dwd1], [0], [0], ffn_rows[2:], [D]), 6)
    dab1 = _d_hidden_swiglu("ffn1_d_hidden", df1, wd1, ab1, after=[dwd1, r_win])
    dwgu1 = _mm("ffn1_dw_gate_up", dab1, h1, "TN", BF16, HALF, D, 2048)
    r_f1g, r_f1u = _run_plan_on_sequencer(
        "scatter_ffn1_gate_up_grads",
        _scatter_plan([dwgu1], [0, 0], [0, HALF], ffn_rows[:2], [D, D], [HALF, HALF]), 7)
    g0, acc1 = _d_h_norm_bwd("ffn1_d_h", dab1, wgu1, x0, g1, norm_ffn1, sc1, sh1, after=[dwgu1, r_f1d], tm=512)

    dqw = jnp.sum(wq_acc[0].reshape(12, HD), axis=0)
    dkw = jnp.sum(wk_acc[0].reshape(12, HD), axis=0)
    small = jnp.concatenate([
        acc1[0], acc1[1], acc2[3], acc2[0], acc2[1], acc3[3], acc3[0], acc3[1], acc_out[0],
        acc1[2], acc2[2], acc3[2], dqw, dkw, cw_acc[0:3].reshape(3 * D),
        jnp.zeros((HD,), F32).at[0].set(loss_part)]).reshape(1, -1)
    small_all = _small_allgather("gather_small_grads", small)
    small_sum = _sum_rows("sum_small_grads", small_all)[0]
    n_mod = N_MOD * D
    g_b_ada = small_sum[:n_mod].reshape(1, n_mod)
    g_norm1, g_norm2, g_norm3 = [small_sum[n_mod + i * D:n_mod + (i + 1) * D].reshape(1, D) for i in range(3)]
    off = n_mod + 3 * D
    g_qn, g_kn = small_sum[off:off + HD].reshape(1, HD), small_sum[off + HD:off + 2 * HD].reshape(1, HD)
    g_cw_full = small_sum[off + 2 * HD:off + 2 * HD + 3 * D].reshape(3, D)
    loss = small_sum[off + 2 * HD + 3 * D]
    g_cw = lax.dynamic_slice(g_cw_full, (0, me * cw_cols), (3, cw_cols))
    dmod_part = lax.dynamic_slice(small_all[:, 0, :n_mod], (0, me * ada_cols), (N_DEV, ada_cols))
    g_w_ada = _w_ada_grad(c_all.T, dmod_part)

    as_rows = {"ffn1_w_gate", "ffn1_w_up", "w_in", "w_attn_branch", "ffn2_w_gate", "ffn2_w_up"}
    grad_list = [g_w_ada, g_b_ada, g_norm1, r_f1g, r_f1u, r_f1d, g_norm2, r_win,
                 g_qn, g_kn, g_cw, r_wa, r_wc, r_wo, g_norm3, r_f2g, r_f2u, r_f2d]
    weights = [w_ada, b_ada, norm_ffn1, ffn1_w_gate, ffn1_w_up, ffn1_w_down, norm_mix, w_in, q_norm, k_norm,
               conv_w, w_attn_branch, w_conv_branch, w_out, norm_ffn2, ffn2_w_gate, ffn2_w_up, ffn2_w_down]
    ms = [m_w_ada, m_b_ada, m_norm_ffn1, m_ffn1_w_gate, m_ffn1_w_up, m_ffn1_w_down, m_norm_mix, m_w_in, m_q_norm,
          m_k_norm, m_conv_w, m_w_attn_branch, m_w_conv_branch, m_w_out, m_norm_ffn2, m_ffn2_w_gate,
          m_ffn2_w_up, m_ffn2_w_down]
    vs = [v_w_ada, v_b_ada, v_norm_ffn1, v_ffn1_w_gate, v_ffn1_w_up, v_ffn1_w_down, v_norm_mix, v_w_in, v_q_norm,
          v_k_norm, v_conv_w, v_w_attn_branch, v_w_conv_branch, v_w_out, v_norm_ffn2, v_ffn2_w_gate,
          v_ffn2_w_up, v_ffn2_w_down]
    wnames = ["w_ada", "b_ada", "norm_ffn1", "ffn1_w_gate", "ffn1_w_up", "ffn1_w_down", "norm_mix", "w_in",
              "q_norm", "k_norm", "conv_w", "w_attn_branch", "w_conv_branch", "w_out", "norm_ffn2",
              "ffn2_w_gate", "ffn2_w_up", "ffn2_w_down"]
    small = [i for i, gr in enumerate(grad_list) if gr.ndim == 2 and gr.size <= 16384]
    flat = lambda a, i: a.reshape(-1, weights[i].shape[-1])
    small_res = dict(zip(small, _adamw_small(
        [flat(weights[i], i) for i in small], [flat(grad_list[i], i) for i in small],
        [flat(ms[i], i) for i in small], [flat(vs[i], i) for i in small])))
    grad_out, deltas, new_ms, new_vs = [], [], [], []
    for idx, (nm, w, gr, m_, v_) in enumerate(zip(wnames, weights, grad_list, ms, vs)):
        if idx in small_res:
            gr, dl, nm_, nv_ = [r.reshape(w.shape) for r in (gr, *small_res[idx])]
        elif nm in as_rows:
            res = _adamw(f"adamw_{nm}", w[0].T, gr, m_[0].T, v_[0].T)
            gr, dl, nm_, nv_ = [r.T[None] for r in res]
        else:
            two_d = (-1, w.shape[-1])
            res = _adamw(f"adamw_{nm}", w.reshape(two_d), gr if gr.ndim == 3 else gr.reshape(two_d),
                         m_.reshape(two_d), v_.reshape(two_d))
            gr, dl, nm_, nv_ = [r.reshape(w.shape) for r in res]
        grad_out.append(gr)
        deltas.append(dl)
        new_ms.append(nm_)
        new_vs.append(nv_)
    return (loss, g0[None], *grad_out, *deltas, *new_ms, *new_vs)
```

```python
import jax
import jax.numpy as jnp
from jax import lax
from jax.experimental import pallas as pl
from jax.experimental.pallas import tpu as pltpu
from jax.experimental.pallas import tpu_sc as plsc

F32 = jnp.float32
BF16 = jnp.bfloat16
MESH = pl.DeviceIdType.MESH

N_DEV = 8
D = 1024
FF = 2816
HD = 128
N_HEADS = 4
DILATIONS = (1, 4, 16)
BAND = 128
QKW = 2 * 3 * N_HEADS * HD
IN_W = 9728
COL = 512
V_BLK, U_BLK, B_BLK, C_BLK, GA_BLK, GC_BLK = 6, 9, 11, 13, 15, 17
EPS = 1e-6
N_MOD = 9
ADAM_LR, ADAM_B1, ADAM_B2, ADAM_EPS, ADAM_WD, ADAM_STEP = 0.001, 0.9, 0.999, 1e-08, 0.01, 10

NT_DIMS = (((1,), (1,)), ((), ()))
TN_DIMS = (((0,), (0,)), ((), ()))
NN_DIMS = (((1,), (0,)), ((), ()))


def _place():
    return lax.axis_index("x"), lax.axis_index("y"), lax.axis_index("c")


def _flip(coord, bit):
    return 1 - coord if bit else coord


def _params(*sem):
    return pltpu.CompilerParams(dimension_semantics=sem)


def _small_allgather(name, v):
    n = v.shape[-1]

    def body(v_ref, out_ref, send_sems, recv_sems):
        x, y, c = _place()
        me = 4 * x + 2 * y + c
        out_ref[me] = v_ref[...]
        copies = []
        for k in range(1, N_DEV):
            peer = (_flip(x, (k >> 2) & 1), _flip(y, (k >> 1) & 1), _flip(c, k & 1))
            cp = pltpu.make_async_remote_copy(
                src_ref=v_ref, dst_ref=out_ref.at[me], send_sem=send_sems.at[k - 1],
                recv_sem=recv_sems.at[k - 1], device_id=peer, device_id_type=MESH)
            cp.start()
            copies.append(cp)
        for cp in copies:
            cp.wait()

    return pl.pallas_call(
        body, name=name,
        out_shape=jax.ShapeDtypeStruct((N_DEV, 1, n), F32),
        in_specs=[pl.BlockSpec(memory_space=pltpu.VMEM)],
        out_specs=pl.BlockSpec(memory_space=pltpu.VMEM),
        scratch_shapes=[pltpu.SemaphoreType.DMA((N_DEV - 1,)), pltpu.SemaphoreType.DMA((N_DEV - 1,))],
    )(v)


class _Plan:
    def __init__(self, operands, out_shapes, sems, phases):
        self.operands, self.out_shapes, self.sems, self.phases = operands, out_shapes, sems, phases


def _slab_start(base, rows, jump, idx):
    return pl.multiple_of(base + idx * rows + (idx // 4) * jump, 16)


def _gather_plan(shards, dst_of, base_of, dst_shapes, jump_of=None):
    n = len(shards)
    rows = [s.shape[0] for s in shards]
    jump_of = jump_of or [0] * n

    def phases(srcs, dsts, sems):
        send_sems, recv_sems, local_sems = sems
        x, y, c = _place()
        me, sibling = (x, y, c), (x, y, 1 - c)
        chips = [(1 - x, y), (x, 1 - y), (1 - x, 1 - y)]

        def slab(i, px, py, pc):
            start = _slab_start(base_of[i], rows[i], jump_of[i], 4 * px + 2 * py + pc)
            return dsts[dst_of[i]].at[pl.ds(start, rows[i])]

        def copy(i, k, block, to, src=None):
            return pltpu.make_async_remote_copy(
                src_ref=slab(i, *block) if src is None else src, dst_ref=slab(i, *block),
                send_sem=send_sems.at[i, k], recv_sem=recv_sems.at[i, k],
                device_id=to, device_id_type=MESH)

        def mine():
            return [pltpu.make_async_copy(srcs[i], slab(i, *me), local_sems.at[i]) for i in range(n)]

        def first():
            out = []
            for i in range(n):
                out.append(copy(i, 0, me, sibling, src=srcs[i]))
                out += [copy(i, 1 + j, me, (*chip, c), src=srcs[i]) for j, chip in enumerate(chips)]
            return out

        def passed():
            return [(copy(i, 1 + j, (*chip, c), me), copy(i, 4 + j, (*chip, c), sibling))
                    for j, chip in enumerate(chips) for i in range(n)]

        def start():
            for cp in mine() + first():
                cp.start()

        def middle():
            for landed, onward in passed():
                landed.wait_recv()
                onward.start()

        def finish():
            for i in range(n):
                copy(i, 0, sibling, me).wait_recv()
                for j, chip in enumerate(chips):
                    copy(i, 4 + j, (*chip, 1 - c), me).wait_recv()
            for cp in first() + [onward for _, onward in passed()]:
                cp.wait_send()
            for cp in mine():
                cp.wait()

        return start, middle, finish

    sems = [pltpu.SemaphoreType.DMA((n, 7)), pltpu.SemaphoreType.DMA((n, 7)), pltpu.SemaphoreType.DMA((n,))]
    return _Plan(list(shards), [jax.ShapeDtypeStruct(s, BF16) for s in dst_shapes], sems, phases)


def _scatter_plan(grads, src_of, base_of, rows, cols, jump_of=None):
    n = len(rows)
    jump_of = jump_of or [0] * n

    def phases(srcs, recvs, sems):
        send_sems, recv_sems, local_sems = sems
        x, y, c = _place()
        me = 4 * x + 2 * y + c

        def slab(i, idx):
            start = _slab_start(base_of[i], rows[i], jump_of[i], idx)
            return srcs[src_of[i]].at[pl.ds(start, rows[i])]

        def copies():
            out = [pltpu.make_async_copy(slab(i, me), recvs[i].at[me], local_sems.at[i]) for i in range(n)]
            for k in range(1, N_DEV):
                px, py, pc = _flip(x, (k >> 2) & 1), _flip(y, (k >> 1) & 1), _flip(c, k & 1)
                out += [pltpu.make_async_remote_copy(
                    src_ref=slab(i, 4 * px + 2 * py + pc), dst_ref=recvs[i].at[me],
                    send_sem=send_sems.at[i, k - 1], recv_sem=recv_sems.at[i, k - 1],
                    device_id=(px, py, pc), device_id_type=MESH) for i in range(n)]
            return out

        def start():
            for cp in copies():
                cp.start()

        def finish():
            for cp in copies():
                cp.wait()

        return start, None, finish

    sems = [pltpu.SemaphoreType.DMA((n, 7)), pltpu.SemaphoreType.DMA((n, 7)), pltpu.SemaphoreType.DMA((n,))]
    out_shapes = [jax.ShapeDtypeStruct((N_DEV, rows[i], cols[i]), BF16) for i in range(n)]
    return _Plan(list(grads), out_shapes, sems, phases)


def _run_plan_on_sequencer(name, plan, collective_id):
    src_refs = [jax.new_ref(a, memory_space=pltpu.MemorySpace.HBM) for a in plan.operands]
    dst_refs = [jax.empty_ref(s, memory_space=pltpu.MemorySpace.HBM) for s in plan.out_shapes]

    @pl.kernel(mesh=plsc.ScalarSubcoreMesh(axis_name="sequencer", num_cores=1), name=name,
               scratch_types=tuple(plan.sems),
               compiler_params=pltpu.CompilerParams(collective_id=collective_id))
    def launch(*sems):
        x, y, c = _place()
        barrier = pltpu.get_barrier_semaphore()
        for k in range(1, N_DEV):
            peer = (_flip(x, (k >> 2) & 1), _flip(y, (k >> 1) & 1), _flip(c, k & 1))
            pl.semaphore_signal(barrier, inc=1, device_id=peer, device_id_type=MESH)
        pl.semaphore_wait(barrier, N_DEV - 1)
        for phase in plan.phases(src_refs, dst_refs, sems):
            if phase is not None:
                phase()

    launch()
    return [r[...] for r in dst_refs]


def _mm(name, a, b, mode, out_dtype, tm, tn, tk, *, tiles_in=(), tiles_out=(), epilogue=None,
        n_outer=False, keep_b=False, col_chunks=None, after=()):
    if mode == "TN":
        kk, m = a.shape
    else:
        m, kk = a.shape
    n = b.shape[0] if mode == "NT" else b.shape[1]
    tm, tn, tk = min(tm, m), min(tn, n), min(tk, kk)
    assert m % tm == 0 and n % tn == 0 and kk % tk == 0, (name, m, n, kk, tm, tn, tk)
    ni, nj, nk = m // tm, n // tn, kk // tk
    dims = {"NN": NN_DIMS, "NT": NT_DIMS, "TN": TN_DIMS}[mode]
    if epilogue is None:
        tiles_out = [(jax.ShapeDtypeStruct((m, n), out_dtype), (tm, tn), lambda i, j: (i, j))]
    n_tin, n_tout = len(tiles_in), len(tiles_out)
    n_acc = 1 if nk > 1 else 0
    n_after = len(after)
    assert not keep_b or (nk == 1 and nj == 1)
    assert not col_chunks or (epilogue is not None and nk == 1 and mode != "TN")
    ij = (lambda p, q: (q, p)) if n_outer else (lambda p, q: (p, q))
    inner = ni if n_outer else nj

    def body(a_ref, b_ref, *rest):
        tin = rest[:n_tin]
        tout = rest[n_tin + n_after:n_tin + n_after + n_tout]
        scratch = rest[n_tin + n_after + n_tout:]
        k = pl.program_id(2)
        visit = pl.program_id(0) * inner + pl.program_id(1)
        if keep_b:
            b_kept, b_sem = scratch[n_acc:n_acc + 2]

            @pl.when((visit == 0) & (k == 0))
            def _():
                cp = pltpu.make_async_copy(b_ref, b_kept, b_sem)
                cp.start()
                cp.wait()

            b_ref = b_kept

        def store(prod, c=0, cols=()):
            if epilogue is None:
                tout[0][...] = prod.astype(out_dtype)
            else:
                epilogue(prod, jnp.logical_and(visit == 0, c == 0), tin, tout, *cols)

        if col_chunks:
            for c, (c0, cw) in enumerate(col_chunks):
                b_part = b_ref[pl.ds(c0, cw), :] if mode == "NT" else b_ref[:, pl.ds(c0, cw)]
                store(lax.dot_general(a_ref[...], b_part, dims, preferred_element_type=F32), c, ((c0, cw),))
        else:
            part = lax.dot_general(a_ref[...], b_ref[...], dims, preferred_element_type=F32)
            if nk == 1:
                store(part)
            else:
                acc_ref = scratch[0]

                @pl.when(k == 0)
                def _():
                    acc_ref[...] = part

                @pl.when((k > 0) & (k < nk - 1))
                def _():
                    acc_ref[...] += part

                @pl.when(k == nk - 1)
                def _():
                    store(acc_ref[...] + part)

    def spec(shape, fn):
        return pl.BlockSpec(shape, lambda p, q, k: fn(*ij(p, q)))

    a_spec = (pl.BlockSpec((tk, tm), lambda p, q, k: (k, ij(p, q)[0])) if mode == "TN"
              else pl.BlockSpec((tm, tk), lambda p, q, k: (ij(p, q)[0], k)))
    if keep_b:
        b_spec = pl.BlockSpec(memory_space=pl.ANY)
    elif mode == "NT":
        b_spec = pl.BlockSpec((tn, tk), lambda p, q, k: (ij(p, q)[1], k))
    else:
        b_spec = pl.BlockSpec((tk, tn), lambda p, q, k: (k, ij(p, q)[1]))
    sequential = epilogue or keep_b
    out = pl.pallas_call(
        body, name=name, grid=(nj, ni, nk) if n_outer else (ni, nj, nk),
        out_shape=[t[0] for t in tiles_out],
        in_specs=([a_spec, b_spec] + [spec(t[1], t[2]) for t in tiles_in]
                  + [pl.BlockSpec(memory_space=pl.ANY)] * n_after),
        out_specs=[spec(t[1], t[2]) for t in tiles_out],
        scratch_shapes=([pltpu.VMEM((tm, tn), F32)] * n_acc
                        + ([pltpu.VMEM(b.shape, b.dtype), pltpu.SemaphoreType.DMA] if keep_b else [])),
        compiler_params=(_params("arbitrary", "arbitrary", "arbitrary") if sequential
                         else _params("parallel", "parallel", "arbitrary")),
    )(a, b, *[t[0] for t in tiles_in], *after)
    return out if epilogue else out[0]


def _row(tm, w, off=0):
    return pl.BlockSpec((tm, w), lambda i: (i, off))


def _vec(w):
    return pl.BlockSpec((1, w), lambda i: (0, 0))


def _sigmoid(x):
    return 0.5 * jnp.tanh(0.5 * x) + 0.5


def _normmod(name, x, g, sc, sh, tm=1024):
    s = x.shape[0]

    def body(x_ref, g_ref, sc_ref, sh_ref, h_ref):
        xv = x_ref[...]
        r = lax.rsqrt(jnp.mean(xv * xv, axis=-1, keepdims=True) + EPS)
        h_ref[...] = ((xv * r) * g_ref[...] * (1.0 + sc_ref[...]) + sh_ref[...]).astype(BF16)

    return pl.pallas_call(
        body, name=name, grid=(s // tm,),
        out_shape=jax.ShapeDtypeStruct((s, D), BF16),
        in_specs=[_row(tm, D), _vec(D), _vec(D), _vec(D)], out_specs=_row(tm, D),
        compiler_params=_params("parallel"),
    )(x, g, sc, sh)


def _heads(x, fn):
    return jnp.concatenate([fn(x[:, h * HD:(h + 1) * HD], h) for h in range(x.shape[1] // HD)], axis=1)


def _qknorm(proj, wqk, tm=512):
    s = proj.shape[0]

    def body(p_ref, w_ref, o_ref):
        pv = p_ref[...].astype(F32)
        wv = w_ref[...]

        def one(qh, h):
            r = lax.rsqrt(jnp.mean(qh * qh, axis=-1, keepdims=True) + EPS)
            return (qh * r) * wv[:, h * HD:(h + 1) * HD]

        o_ref[...] = _heads(pv, one).astype(BF16)

    return pl.pallas_call(
        body, name="qknorm", grid=(s // tm,),
        out_shape=jax.ShapeDtypeStruct((s, QKW), BF16),
        in_specs=[pl.BlockSpec((tm, QKW), lambda i: (i, 0)), pl.BlockSpec((1, QKW), lambda i: (0, 0))],
        out_specs=pl.BlockSpec((tm, QKW), lambda i: (i, 0)),
        compiler_params=_params("parallel"),
    )(proj, wqk)


def _qknorm_bwd(name, proj, dn, w, dproj, blk0, tm=512):
    s, width = dn.shape

    def body(p_ref, d_ref, w_ref, _, o_ref, acc_ref):
        pv = p_ref[...].astype(F32)
        dv = d_ref[...]
        wv = w_ref[...]
        sums = []

        def one(qh, h):
            dn = dv[:, h * HD:(h + 1) * HD]
            r = lax.rsqrt(jnp.mean(qh * qh, axis=-1, keepdims=True) + EPS)
            nh = qh * r
            sums.append(jnp.sum(dn * nh, axis=0, keepdims=True))
            dnw = dn * wv[:, h * HD:(h + 1) * HD]
            return r * (dnw - nh * jnp.mean(dnw * nh, axis=-1, keepdims=True))

        o_ref[...] = _heads(pv, one).astype(BF16)

        @pl.when(pl.program_id(0) == 0)
        def _():
            acc_ref[...] = jnp.zeros_like(acc_ref)

        acc_ref[0:1, :] += jnp.concatenate(sums, axis=1)

    return pl.pallas_call(
        body, name=name, grid=(s // tm,),
        out_shape=[jax.ShapeDtypeStruct((s, IN_W), BF16), jax.ShapeDtypeStruct((8, width), F32)],
        in_specs=[pl.BlockSpec((tm, width), lambda i: (i, blk0)), pl.BlockSpec((tm, width), lambda i: (i, 0)),
                  pl.BlockSpec((1, width), lambda i: (0, 0)), pl.BlockSpec(memory_space=pl.ANY)],
        out_specs=[pl.BlockSpec((tm, width), lambda i: (i, blk0)), pl.BlockSpec((8, width), lambda i: (0, 0))],
        input_output_aliases={3: 0},
        compiler_params=_params("arbitrary"),
    )(proj, dn, w, dproj)


def _attn_shapes(s, g, sub_block):
    d = DILATIONS[g]
    tb = min(s, max(2048, 256 * d))
    sb = min(sub_block, tb // d)
    pb = BAND * d
    assert s % tb == 0 and tb % pb == 0 and (tb // d) % sb == 0 and sb % BAND == 0
    return d, tb, sb, pb


def _lanes(x, width):
    return jnp.concatenate([x] * (width // HD), axis=1)


def _every(start, size, d):
    return pl.ds(start, size, stride=d) if d > 1 else pl.ds(start, size)


def _attn_specs(g, tb, pb, s, ahead):
    ratio = tb // pb
    if ahead:
        nbr = lambda n: jnp.minimum((n + 1) * ratio, s // pb - 1)
    else:
        nbr = lambda n: jnp.maximum(n * ratio - 1, 0)
    cur = lambda base: pl.BlockSpec((tb, HD), lambda h, n: (n, base + g * N_HEADS + h))
    side = lambda base: pl.BlockSpec((pb, HD), lambda h, n: (nbr(n), base + g * N_HEADS + h))
    tok = pl.BlockSpec((tb, HD), lambda h, n: (n, h))
    tok_side = pl.BlockSpec((pb, HD), lambda h, n: (nbr(n), h))
    return cur, side, tok, tok_side


Q_COL, K_COL, V_COL = 0, 12, 24


def _attn_fwd(g, qkn, proj):
    s = qkn.shape[0]
    d, tb, sb, pb = _attn_shapes(s, g, 128)
    ft = F32 if d > 1 else BF16
    nj = tb // d // sb
    scale = HD ** -0.5

    def body(q_ref, kc_ref, kp_ref, vc_ref, vp_ref, o_ref, lse_ref, qf, kf, vf):
        n = pl.program_id(1)
        qf[...] = q_ref[...].astype(ft)
        kf[0:pb] = kp_ref[...].astype(ft)
        kf[pb:] = kc_ref[...].astype(ft)
        vf[0:pb] = vp_ref[...].astype(ft)
        vf[pb:] = vc_ref[...].astype(ft)
        for r in range(d):
            for j in range(nj):
                at = j * sb * d + r
                q = qf[_every(at, sb, d), :].astype(BF16)
                k = kf[_every(at, sb + BAND, d), :].astype(BF16)
                v = vf[_every(at, sb + BAND, d), :].astype(BF16)
                sc = lax.dot_general(q, k, NT_DIMS, preferred_element_type=F32) * scale
                qi = lax.broadcasted_iota(jnp.int32, sc.shape, 0)
                kj = lax.broadcasted_iota(jnp.int32, sc.shape, 1)
                valid = (kj >= qi) & (kj <= qi + BAND)
                if j == 0:
                    valid = valid & ((kj >= BAND) | (n > 0))
                sc = jnp.where(valid, sc, -1e30)
                m = jnp.max(sc, axis=-1, keepdims=True)
                p = jnp.exp(sc - m)
                l = jnp.sum(p, axis=-1, keepdims=True)
                o = lax.dot_general(p.astype(BF16), v, NN_DIMS, preferred_element_type=F32)
                o_ref[_every(at, sb, d), :] = o / l
                lse_ref[_every(at, sb, d), :] = jnp.broadcast_to(m + jnp.log(l), (sb, HD))

    cur, side, tok, _ = _attn_specs(g, tb, pb, s, ahead=False)
    return pl.pallas_call(
        body, name=f"attn_fwd_g{g}", grid=(N_HEADS, s // tb),
        out_shape=[jax.ShapeDtypeStruct((s, COL), F32)] * 2,
        in_specs=[cur(Q_COL), cur(K_COL), side(K_COL), cur(V_COL), side(V_COL)],
        out_specs=[tok, tok],
        scratch_shapes=[pltpu.VMEM((tb, HD), ft), pltpu.VMEM((tb + pb, HD), ft),
                        pltpu.VMEM((tb + pb, HD), ft)],
        compiler_params=_params("parallel", "arbitrary"),
    )(qkn, qkn, qkn, proj, proj)


def _attn_combine(os_, lses, tm=1024):
    s = os_[0].shape[0]

    def body(o0, o1, o2, l0, l1, l2, o_ref, lse_ref):
        a, b, c = l0[...], l1[...], l2[...]
        m = jnp.maximum(jnp.maximum(a, b), c)
        ea, eb, ec = jnp.exp(a - m), jnp.exp(b - m), jnp.exp(c - m)
        tot = ea + eb + ec
        o_ref[...] = ((ea * o0[...] + eb * o1[...] + ec * o2[...]) / tot).astype(BF16)
        lse_ref[...] = m + jnp.log(tot)

    return pl.pallas_call(
        body, name="attn_combine", grid=(s // tm,),
        out_shape=[jax.ShapeDtypeStruct((s, COL), BF16), jax.ShapeDtypeStruct((s, COL), F32)],
        in_specs=[_row(tm, COL)] * 6, out_specs=[_row(tm, COL)] * 2,
        compiler_params=_params("parallel"),
    )(*os_, *lses)


def _attn_bwd(g, qkn, proj, do, lse, delta, dqn, dkn, dproj):
    s = qkn.shape[0]
    d, tb, sb, pb = _attn_shapes(s, g, 256)
    ft = F32 if d > 1 else BF16
    nj = tb // d // sb
    nt = s // tb
    scale = HD ** -0.5
    chained = dqn is not None

    def body(k_ref, v_ref, qc_ref, qn_ref, doc_ref, don_ref, lc_ref, ln_ref, dc_ref, dn_ref, *rest):
        dq_ref, dk_ref, dv_ref, kf, vf, qf, dvf, later = rest[-8:]
        n = pl.program_id(1)
        kf[...] = k_ref[...].astype(ft)
        vf[...] = v_ref[...].astype(ft)
        qf[0:tb] = qc_ref[...].astype(ft)
        qf[tb:] = qn_ref[...].astype(ft)

        @pl.when(n == 0)
        def _():
            later[...] = jnp.zeros_like(later)

        def window(c_ref, n_ref, r, j):
            at = j * sb * d + r
            if j < nj - 1:
                return c_ref[_every(at, sb + BAND, d), :]
            return jnp.concatenate([c_ref[_every(at, sb, d), :], n_ref[_every(r, BAND, d), :]], axis=0)

        for r in range(d):
            tail = later[r]
            for j in range(nj):
                at = j * sb * d + r
                rows = _every(at, sb, d)
                k = kf[rows, :].astype(BF16)
                v = vf[rows, :].astype(BF16)
                q = qf[_every(at, sb + BAND, d), :].astype(BF16)
                dov = window(doc_ref, don_ref, r, j).astype(BF16)
                sc = lax.dot_general(q, k, NT_DIMS, preferred_element_type=F32) * scale
                qi = lax.broadcasted_iota(jnp.int32, sc.shape, 0)
                kj = lax.broadcasted_iota(jnp.int32, sc.shape, 1)
                valid = (qi >= kj) & (qi <= kj + BAND)
                if j == nj - 1:
                    valid = valid & ((qi < sb) | (n < nt - 1))
                p = jnp.exp(jnp.where(valid, sc - _lanes(window(lc_ref, ln_ref, r, j), sb), -1e30))
                dp = lax.dot_general(dov, v, NT_DIMS, preferred_element_type=F32)
                ds = (p * (dp - _lanes(window(dc_ref, dn_ref, r, j), sb)) * scale).astype(BF16)
                dvf[rows, :] = lax.dot_general(p.astype(BF16), dov, TN_DIMS, preferred_element_type=F32)
                dk_ref[rows, :] = lax.dot_general(ds, q, TN_DIMS, preferred_element_type=F32)
                dqw = lax.dot_general(ds, k, NN_DIMS, preferred_element_type=F32)
                first = dqw[:BAND] + tail
                dq_ref[rows, :] = first if sb == BAND else jnp.concatenate([first, dqw[BAND:sb]], axis=0)
                tail = dqw[sb:]
            later[r] = tail
        dv_ref[...] = dvf[...].astype(BF16)

    cur, side, tok, tok_side = _attn_specs(g, tb, pb, s, ahead=True)
    anyspec = pl.BlockSpec(memory_space=pl.ANY)
    n_heads_cols = 3 * N_HEADS * HD
    return pl.pallas_call(
        body, name=f"attn_bwd_g{g}", grid=(N_HEADS, nt),
        out_shape=[jax.ShapeDtypeStruct((s, n_heads_cols), F32), jax.ShapeDtypeStruct((s, n_heads_cols), F32),
                   jax.ShapeDtypeStruct((s, IN_W), BF16)],
        in_specs=[cur(K_COL), cur(V_COL), cur(Q_COL), side(Q_COL), tok, tok_side, tok, tok_side,
                  tok, tok_side] + ([anyspec, anyspec] if chained else []) + [anyspec],
        out_specs=[cur(0), cur(0), cur(V_COL)],
        input_output_aliases={10: 0, 11: 1, 12: 2} if chained else {10: 2},
        scratch_shapes=[pltpu.VMEM((tb, HD), ft), pltpu.VMEM((tb, HD), ft),
                        pltpu.VMEM((tb + pb, HD), ft), pltpu.VMEM((tb, HD), F32),
                        pltpu.VMEM((d, BAND, HD), F32)],
        compiler_params=_params("arbitrary", "arbitrary"),
    )(qkn, proj, qkn, qkn, do, do, lse, lse, delta, delta, *([dqn, dkn] if chained else []), dproj)


def _shift_down(x, before, k):
    rolled = pltpu.roll(x, k, 0)
    head = jnp.where(lax.broadcasted_iota(jnp.int32, before.shape, 0) < k, pltpu.roll(before, k, 0), rolled[:8])
    return jnp.concatenate([head, rolled[8:]], axis=0)


def _shift_up(x, after, k):
    rows = x.shape[0]
    rolled = pltpu.roll(x, rows - k, 0)
    tail = jnp.where(lax.broadcasted_iota(jnp.int32, after.shape, 0) >= 8 - k,
                     pltpu.roll(after, 8 - k, 0), rolled[rows - 8:])
    return jnp.concatenate([rolled[:rows - 8], tail], axis=0)


def _conv_fwd(proj, cw, tm=1024):
    s = proj.shape[0]
    r16 = tm // 16

    def body(u_ref, b_ref, c_ref, up_ref, cp_ref, w_ref, z_ref):
        i = pl.program_id(1)
        xc = c_ref[...].astype(F32) * u_ref[...].astype(F32)
        xp = jnp.where(i > 0, cp_ref[8:16, :].astype(F32) * up_ref[8:16, :].astype(F32), 0.0)
        w = w_ref[...]
        conv = _shift_down(xc, xp, 2) * w[0:1] + _shift_down(xc, xp, 1) * w[1:2] + xc * w[2:3]
        z_ref[...] = (b_ref[...].astype(F32) * conv).astype(BF16)

    tile = lambda blk: pl.BlockSpec((tm, COL), lambda j, i: (i, blk + j))
    before = lambda blk: pl.BlockSpec((16, COL), lambda j, i: (jnp.maximum(i * r16 - 1, 0), blk + j))
    return pl.pallas_call(
        body, name="conv_fwd", grid=(D // COL, s // tm),
        out_shape=jax.ShapeDtypeStruct((s, D), BF16),
        in_specs=[tile(U_BLK), tile(B_BLK), tile(C_BLK), before(U_BLK), before(C_BLK),
                  pl.BlockSpec((3, COL), lambda j, i: (0, j))],
        out_specs=pl.BlockSpec((tm, COL), lambda j, i: (i, j)),
        compiler_params=_params("parallel", "parallel"),
    )(proj, proj, proj, proj, proj, cw)


def _conv_bwd(dz, proj, cw, dproj, tm=1024):
    s = proj.shape[0]
    r16 = tm // 16
    nrow = s // tm

    def body(dz_ref, u_ref, b_ref, c_ref, up_ref, cp_ref, dzn_ref, bn_ref, w_ref, _, o_ref, dc_ref, acc_ref):
        piece, i = pl.program_id(1), pl.program_id(2)
        u, c = u_ref[...].astype(F32), c_ref[...].astype(F32)
        bv = b_ref[...].astype(F32)
        dzv = dz_ref[...]
        w = w_ref[...]

        @pl.when((piece == 0) & (i == 0))
        def _():
            acc_ref[...] = jnp.zeros_like(acc_ref)

        @pl.when(piece == 0)
        def _():
            xc = c * u
            xp = jnp.where(i > 0, cp_ref[8:16, :].astype(F32) * up_ref[8:16, :].astype(F32), 0.0)
            x2, x1 = _shift_down(xc, xp, 2), _shift_down(xc, xp, 1)
            o_ref[...] = (dzv * (x2 * w[0:1] + x1 * w[1:2] + xc * w[2:3])).astype(BF16)
            dc_ref[...] = jnp.zeros_like(dc_ref)
            dconv = dzv * bv
            acc_ref[0:1, :] += jnp.sum(dconv * x2, axis=0, keepdims=True)
            acc_ref[1:2, :] += jnp.sum(dconv * x1, axis=0, keepdims=True)
            acc_ref[2:3, :] += jnp.sum(dconv * xc, axis=0, keepdims=True)

        @pl.when(piece == 1)
        def _():
            dconv = dzv * bv
            dn = jnp.where(i < nrow - 1, dzn_ref[...] * bn_ref[0:8, :].astype(F32), 0.0)
            dxc = dconv * w[2:3] + _shift_up(dconv, dn, 1) * w[1:2] + _shift_up(dconv, dn, 2) * w[0:1]
            o_ref[...] = (dxc * c).astype(BF16)
            dc_ref[...] = (dxc * u).astype(BF16)

    tile = lambda blk: pl.BlockSpec((tm, COL), lambda j, p, i: (i, blk + j))
    before = lambda blk: pl.BlockSpec((16, COL), lambda j, p, i: (jnp.maximum(i * r16 - 1, 0), blk + j))
    after = lambda rows, blk: pl.BlockSpec(
        (rows, COL), lambda j, p, i: (jnp.minimum((i + 1) * (tm // rows), s // rows - 1), blk + j))
    return pl.pallas_call(
        body, name="conv_bwd", grid=(D // COL, 2, nrow),
        out_shape=[jax.ShapeDtypeStruct((s, IN_W), BF16), jax.ShapeDtypeStruct((s + tm, D), BF16),
                   jax.ShapeDtypeStruct((8, D), F32)],
        in_specs=[tile(0), tile(U_BLK), tile(B_BLK), tile(C_BLK), before(U_BLK), before(C_BLK),
                  after(8, 0), after(16, B_BLK), pl.BlockSpec((3, COL), lambda j, p, i: (0, j)),
                  pl.BlockSpec(memory_space=pl.ANY)],
        out_specs=[pl.BlockSpec((tm, COL), lambda j, p, i: (i, jnp.where(p == 0, B_BLK, U_BLK) + j)),
                   pl.BlockSpec((tm, COL), lambda j, p, i: (jnp.where(p == 0, nrow, i), j)),
                   pl.BlockSpec((8, COL), lambda j, p, i: (0, j))],
        input_output_aliases={9: 0},
        compiler_params=_params("arbitrary", "arbitrary", "arbitrary"),
    )(dz, proj, proj, proj, proj, proj, dz, proj, cw, dproj)


def _copy_columns(name, src, dst, blk0, tm=2048):
    s, w = dst.shape[0], src.shape[1]
    fresh = isinstance(dst, jax.ShapeDtypeStruct)

    def body(x_ref, *rest):
        rest[-1][...] = x_ref[...]

    return pl.pallas_call(
        body, name=name, grid=(w // COL, s // tm),
        out_shape=jax.ShapeDtypeStruct(dst.shape, dst.dtype),
        in_specs=[pl.BlockSpec((tm, COL), lambda j, i: (i, j))] + ([] if fresh else [pl.BlockSpec(memory_space=pl.ANY)]),
        out_specs=pl.BlockSpec((tm, COL), lambda j, i: (i, blk0 + j)),
        input_output_aliases={} if fresh else {1: 0},
        compiler_params=_params("parallel", "parallel"),
    )(src, *([] if fresh else [dst]))


def _mod_part(c_all, w_ada, b_part):
    def body(c_ref, w_ref, b_ref, o_ref):
        cv = c_ref[...]
        act = cv * _sigmoid(cv)
        o_ref[...] = jnp.dot(act, w_ref[...], preferred_element_type=F32,
                             precision=lax.Precision.HIGHEST) + b_ref[...]

    return pl.pallas_call(
        body, name="mod_part", out_shape=jax.ShapeDtypeStruct((N_DEV, w_ada.shape[1]), F32),
    )(c_all, w_ada, b_part)


def _w_ada_grad(c_all_t, dmod_part):
    def body(c_ref, d_ref, o_ref):
        cv = c_ref[...]
        act = cv * _sigmoid(cv)
        dv = d_ref[...]
        acc = act[:, 0:1] * dv[0:1, :]
        for b in range(1, N_DEV):
            acc = acc + act[:, b:b + 1] * dv[b:b + 1, :]
        o_ref[...] = acc

    return pl.pallas_call(
        body, name="w_ada_grad", out_shape=jax.ShapeDtypeStruct((D, dmod_part.shape[1]), F32),
    )(c_all_t, dmod_part)


def _sum_rows(name, v):
    def body(v_ref, o_ref):
        acc = v_ref[0]
        for k in range(1, N_DEV):
            acc = acc + v_ref[k]
        o_ref[...] = acc

    return pl.pallas_call(body, name=name, out_shape=jax.ShapeDtypeStruct(v.shape[1:], F32))(v)


def _adamw(name, w, g, m, v):
    rows, cols = w.shape
    limit = max(16, (1 << 20) // (4 * cols))
    tr = rows if rows <= limit else next((t for t in range(limit - limit % 16, 15, -16) if rows % t == 0), rows)
    c1 = 1.0 - ADAM_B1 ** ADAM_STEP
    c2 = 1.0 - ADAM_B2 ** ADAM_STEP
    parts = g.ndim == 3

    def body(w_ref, g_ref, m_ref, v_ref, go_ref, d_ref, nm_ref, nv_ref):
        if parts:
            gv = g_ref[0].astype(F32)
            for k in range(1, N_DEV):
                gv = gv + g_ref[k].astype(F32)
        else:
            gv = g_ref[...]
        go_ref[...] = gv
        nm = ADAM_B1 * m_ref[...] + (1.0 - ADAM_B1) * gv
        nv = ADAM_B2 * v_ref[...] + (1.0 - ADAM_B2) * (gv * gv)
        nm_ref[...] = nm
        nv_ref[...] = nv
        d_ref[...] = -ADAM_LR * ((nm / c1) / (jnp.sqrt(nv / c2) + ADAM_EPS) + ADAM_WD * w_ref[...])

    spec = pl.BlockSpec((tr, cols), lambda i: (i, 0))
    g_spec = pl.BlockSpec((N_DEV, tr, cols), lambda i: (0, i, 0)) if parts else spec
    return pl.pallas_call(
        body, name=name, grid=(rows // tr,),
        out_shape=[jax.ShapeDtypeStruct((rows, cols), F32)] * 4,
        in_specs=[spec, g_spec, spec, spec], out_specs=[spec] * 4,
        compiler_params=_params("parallel"),
    )(w, g, m, v)


def _adamw_small(ws, gs, ms, vs):
    n = len(ws)
    c1 = 1.0 - ADAM_B1 ** ADAM_STEP
    c2 = 1.0 - ADAM_B2 ** ADAM_STEP

    def body(*refs):
        for i in range(n):
            w_ref, g_ref, m_ref, v_ref = refs[i], refs[n + i], refs[2 * n + i], refs[3 * n + i]
            d_ref, nm_ref, nv_ref = refs[4 * n + 3 * i:4 * n + 3 * i + 3]
            gv = g_ref[...]
            nm = ADAM_B1 * m_ref[...] + (1.0 - ADAM_B1) * gv
            nv = ADAM_B2 * v_ref[...] + (1.0 - ADAM_B2) * (gv * gv)
            nm_ref[...] = nm
            nv_ref[...] = nv
            d_ref[...] = -ADAM_LR * ((nm / c1) / (jnp.sqrt(nv / c2) + ADAM_EPS) + ADAM_WD * w_ref[...])

    outs = pl.pallas_call(
        body, name="adamw_small",
        out_shape=[jax.ShapeDtypeStruct(w.shape, F32) for w in ws for _ in range(3)],
    )(*ws, *gs, *ms, *vs)
    return [tuple(outs[3 * i:3 * i + 3]) for i in range(n)]


HALF = FF // 2


def _sds(shape, dtype):
    return jax.ShapeDtypeStruct(shape, dtype)


def _row_tile(w):
    return lambda tm: ((tm, w), lambda i, j: (i, 0))


def _one(w):
    return lambda rows: ((rows, w), lambda i, j: (0, 0))


def _gate_up_swiglu(name, h, wgu, tm=1024):
    s = h.shape[0]
    tm = min(tm, s)

    def epilogue(prod, first, tin, tout):
        pq_ref, s_ref = tout
        a, b = prod[:, :HALF], prod[:, HALF:]
        sig = _sigmoid(a)
        act = a * sig
        pq_ref[:, :HALF] = (b * (sig * (1.0 + a * (1.0 - sig)))).astype(BF16)
        pq_ref[:, HALF:] = act.astype(BF16)
        s_ref[...] = (act * b).astype(BF16)

    return _mm(name, h, wgu, "NT", None, tm, FF, D, n_outer=True, epilogue=epilogue,
               tiles_out=[(_sds((s, 2 * FF), BF16), (tm, FF), lambda i, j: (i, j)),
                          (_sds((s, FF), BF16), (tm, HALF), lambda i, j: (i, j))])


def _d_hidden_swiglu(name, df, wd, ab, after=(), tm=1024):
    s = df.shape[0]
    tm = min(tm, s)

    def epilogue(prod, first, tin, tout, cols):
        da_cols = slice(cols[0], cols[0] + cols[1])
        db_cols = slice(HALF + cols[0], HALF + cols[0] + cols[1])
        tout[0][:, da_cols] = (prod * tin[0][:, da_cols].astype(F32)).astype(BF16)
        tout[0][:, db_cols] = (prod * tin[0][:, db_cols].astype(F32)).astype(BF16)

    chunks = [(c0, min(384, HALF - c0)) for c0 in range(0, HALF, 384)]
    return _mm(name, df, wd, "NT", None, tm, HALF, D, n_outer=True, epilogue=epilogue, col_chunks=chunks, after=after,
               tiles_in=[(ab, (tm, FF), lambda i, j: (i, j))],
               tiles_out=[(_sds((s, 2 * FF), BF16), (tm, FF), lambda i, j: (i, j))])[0]


def _out_residual(name, a, w, x, gt, coef, nxt, tm=512, tk=FF):
    s = a.shape[0]
    tm = min(tm, s)

    def epilogue(prod, first, tin, tout):
        x_ref, gt_ref, g_ref, sc_ref, sh_ref = tin
        f_ref, xn_ref, h_ref = tout
        f_ref[...] = prod
        xn = x_ref[...] + (coef * gt_ref[...]) * prod
        xn_ref[...] = xn
        r = lax.rsqrt(jnp.mean(xn * xn, axis=-1, keepdims=True) + EPS)
        h_ref[...] = ((xn * r) * g_ref[...] * (1.0 + sc_ref[...]) + sh_ref[...]).astype(BF16)

    row, vec = _row_tile(D)(tm), _one(D)(1)
    return _mm(name, a, w, "NN", None, tm, D, tk, epilogue=epilogue,
               tiles_in=[(x, *row), (gt, *vec)] + [(v, *vec) for v in nxt],
               tiles_out=[(_sds((s, D), F32), *row), (_sds((s, D), F32), *row), (_sds((s, D), BF16), *row)])


def _out_loss(name, a, w, x, gt, coef, target, tm=512):
    s = a.shape[0]
    tm = min(tm, s)

    def epilogue(prod, first, tin, tout):
        x_ref, gt_ref, t_ref = tin
        f_ref, g_ref, df_ref, acc_ref = tout
        f_ref[...] = prod
        cg = coef * gt_ref[...]
        e = x_ref[...] + cg * prod - t_ref[...]
        gv = e * (1.0 / D)
        g_ref[...] = gv
        df_ref[...] = (cg * gv).astype(BF16)

        @pl.when(first)
        def _():
            acc_ref[...] = jnp.zeros_like(acc_ref)

        acc_ref[0:1, :] += coef * jnp.sum(gv * prod, axis=0, keepdims=True)
        acc_ref[1:2, :] += (0.5 / D) * jnp.sum(e * e, axis=0, keepdims=True)

    row, vec = _row_tile(D)(tm), _one(D)(1)
    return _mm(name, a, w, "NN", None, tm, D, FF, epilogue=epilogue,
               tiles_in=[(x, *row), (gt, *vec), (target, *row)],
               tiles_out=[(_sds((s, D), F32), *row), (_sds((s, D), F32), *row), (_sds((s, D), BF16), *row),
                          (_sds((8, D), F32), *_one(D)(8))])


def _d_h_norm_bwd(name, da, w, x, gin, g, sc, sh, before=None, after=(), tm=256):
    s = da.shape[0]
    tm = min(tm, s)
    coef = before[2] if before else None

    def epilogue(prod, first, tin, tout):
        x_ref, gin_ref, g_ref, sc_ref, sh_ref = tin[:5]
        gout_ref, acc_ref = tout[:2]
        xv = x_ref[...]
        r = lax.rsqrt(jnp.mean(xv * xv, axis=-1, keepdims=True) + EPS)
        nv = xv * r
        gv, one_sc = g_ref[...], 1.0 + sc_ref[...]
        dn = prod * gv * one_sc
        gout = gin_ref[...] + r * (dn - nv * jnp.mean(dn * nv, axis=-1, keepdims=True))
        gout_ref[...] = gout

        @pl.when(first)
        def _():
            acc_ref[...] = jnp.zeros_like(acc_ref)

        dhn = prod * nv
        acc_ref[0:1, :] += jnp.sum(prod, axis=0, keepdims=True)
        acc_ref[1:2, :] += jnp.sum(dhn * gv, axis=0, keepdims=True)
        acc_ref[2:3, :] += jnp.sum(dhn * one_sc, axis=0, keepdims=True)
        if before:
            f_ref, gt_ref = tin[5:]
            tout[2][...] = ((coef * gt_ref[...]) * gout).astype(BF16)
            acc_ref[3:4, :] += coef * jnp.sum(gout * f_ref[...], axis=0, keepdims=True)

    row, vec = _row_tile(D)(tm), _one(D)(1)
    tiles_in = [(x, *row), (gin, *row), (g, *vec), (sc, *vec), (sh, *vec)]
    tiles_out = [(_sds((s, D), F32), *row), (_sds((8, D), F32), *_one(D)(8))]
    if before:
        tiles_in += [(before[0], *row), (before[1], *vec)]
        tiles_out.append((_sds((s, D), BF16), *row))
    return _mm(name, da, w, "NN", None, tm, D, da.shape[1], epilogue=epilogue, keep_b=True, after=after,
               tiles_in=tiles_in, tiles_out=tiles_out)


def _gate_tiles(proj, tm):
    return [(proj, (tm, COL), (lambda i, j, blk=blk: (i, blk))) for blk in (GA_BLK, GA_BLK + 1, GC_BLK, GC_BLK + 1)]


def _conv_branch_merge(z, wc, ya, proj, tm=1024):
    s = z.shape[0]
    tm = min(tm, s)

    def epilogue(prod, first, tin, tout):
        ya_ref, ga0, ga1, gc0, gc1 = tin
        tout[0][...] = prod.astype(BF16)
        for half, (ga, gc) in enumerate(((ga0, gc0), (ga1, gc1))):
            cols = slice(half * COL, (half + 1) * COL)
            tout[1][:, cols] = (_sigmoid(ga[...].astype(F32)) * ya_ref[:, cols].astype(F32)
                                + _sigmoid(gc[...].astype(F32)) * prod[:, cols]).astype(BF16)

    row = _row_tile(D)(tm)
    return _mm("mix_conv_branch", z, wc, "NN", None, tm, D, D, epilogue=epilogue,
               tiles_in=[(ya, *row)] + _gate_tiles(proj, tm),
               tiles_out=[(_sds((s, D), BF16), *row), (_sds((s, D), BF16), *row)])


def _d_merged_branches(dmix, wo, ya, yc, proj, tm=1024):
    s = dmix.shape[0]
    tm = min(tm, s)

    def epilogue(prod, first, tin, tout):
        ya_ref, yc_ref, ga0, ga1, gc0, gc1 = tin
        dya_ref, dyc_ref, dg_ref = tout
        for half, (ga, gc) in enumerate(((ga0, gc0), (ga1, gc1))):
            cols = slice(half * COL, (half + 1) * COL)
            dm = prod[:, cols]
            for y_ref, g_ref, dy_ref, off in ((ya_ref, ga, dya_ref, 0), (yc_ref, gc, dyc_ref, D)):
                sig = _sigmoid(g_ref[...].astype(F32))
                dms = dm * sig
                dy_ref[:, cols] = dms.astype(BF16)
                dg_ref[:, off + half * COL:off + (half + 1) * COL] = (
                    dms * y_ref[:, cols].astype(F32) * (1.0 - sig)).astype(BF16)

    row = _row_tile(D)(tm)
    return _mm("mix_d_merged", dmix, wo, "NT", None, tm, D, D, epilogue=epilogue,
               tiles_in=[(ya, *row), (yc, *row)] + _gate_tiles(proj, tm),
               tiles_out=[(_sds((s, D), BF16), *row), (_sds((s, D), BF16), *row),
                          (_sds((s, 2 * D), BF16), *_row_tile(2 * D)(tm))])


def _d_o_delta(dya, wa_t, o, tm=1024):
    s = dya.shape[0]
    tm = min(tm, s)

    def epilogue(prod, first, tin, tout):
        tout[0][...] = prod
        tout[1][...] = _heads(prod * tin[0][...].astype(F32), lambda ph, h: jnp.broadcast_to(
            jnp.sum(ph, axis=-1, keepdims=True), ph.shape))

    row = _row_tile(COL)(tm)
    return _mm("mix_d_o", dya, wa_t, "NN", None, tm, COL, D, epilogue=epilogue,
               tiles_in=[(o, *row)], tiles_out=[(_sds((s, COL), F32), *row), (_sds((s, COL), F32), *row)])


def _ffn_bwd(tag, df, x, gin, h, ab, sw, g, sc, sh, wgu, wd, before=None, tk_dw=2048):
    dwd = _mm(f"{tag}_dw_down", sw, df, "TN", BF16, HALF, D, tk_dw)
    dab = _d_hidden_swiglu(f"{tag}_d_hidden", df, wd, ab, after=[dwd])
    dwgu = _mm(f"{tag}_dw_gate_up", dab, h, "TN", BF16, HALF, D, tk_dw)
    res = _d_h_norm_bwd(f"{tag}_d_h", dab, wgu, x, gin, g, sc, sh, before=before, after=[dwgu], tm=512)
    return res, dwgu, dwd


def kernel(x, c, w_ada, b_ada, norm_ffn1, ffn1_w_gate, ffn1_w_up, ffn1_w_down, norm_mix, w_in, q_norm, k_norm, conv_w, w_attn_branch, w_conv_branch, w_out, norm_ffn2, ffn2_w_gate, ffn2_w_up, ffn2_w_down, loss_target, m_w_ada, m_b_ada, m_norm_ffn1, m_ffn1_w_gate, m_ffn1_w_up, m_ffn1_w_down, m_norm_mix, m_w_in, m_q_norm, m_k_norm, m_conv_w, m_w_attn_branch, m_w_conv_branch, m_w_out, m_norm_ffn2, m_ffn2_w_gate, m_ffn2_w_up, m_ffn2_w_down, v_w_ada, v_b_ada, v_norm_ffn1, v_ffn1_w_gate, v_ffn1_w_up, v_ffn1_w_down, v_norm_mix, v_w_in, v_q_norm, v_k_norm, v_conv_w, v_w_attn_branch, v_w_conv_branch, v_w_out, v_norm_ffn2, v_ffn2_w_gate, v_ffn2_w_up, v_ffn2_w_down):
    me = 4 * lax.axis_index("x") + 2 * lax.axis_index("y") + lax.axis_index("c")
    x0, target = x[0], loss_target[0]
    s = x0.shape[0]
    ada_cols = w_ada.shape[2]
    cw_cols = conv_w.shape[2]

    gathered = _small_allgather(
        "gather_c_conv", jnp.concatenate([c, conv_w[0].reshape(1, 3 * cw_cols)], axis=1))[:, 0]
    c_all = gathered[:, :D]
    cw = gathered[:, D:].reshape(N_DEV, 3, cw_cols).transpose(1, 0, 2).reshape(3, D)
    b_part = lax.dynamic_slice(b_ada, (0, me * ada_cols), (1, ada_cols))
    mod_part = _mod_part(c_all, w_ada[0], b_part)
    mod_all = _small_allgather("gather_mod", mod_part.reshape(1, N_DEV * ada_cols))
    mod = lax.dynamic_slice(mod_all.reshape(N_DEV, N_DEV, ada_cols), (0, me, 0), (N_DEV, 1, ada_cols))
    mod = mod.reshape(N_MOD, 1, D)
    sh1, sc1, gt1, sh2, sc2, gt2, sh3, sc3, gt3 = [mod[i] for i in range(N_MOD)]

    tb = lambda w: w[0].T.astype(BF16)
    nb = lambda w: w[0].astype(BF16)
    ffn1_shards = [tb(ffn1_w_gate), tb(ffn1_w_up), nb(ffn1_w_down)]
    ffn2_shards = [tb(ffn2_w_gate), tb(ffn2_w_up), nb(ffn2_w_down)]
    mix_shards = [tb(w_in), tb(w_attn_branch), nb(w_conv_branch), nb(w_out)]
    ffn_dst, ffn_base, ffn_jump, ffn_shapes = [0, 0, 1], [0, HALF, 0], [HALF, HALF, 0], [(2 * FF, D), (FF, D)]
    mix_dst, mix_base, mix_shapes = [0, 1, 2, 3], [0, 0, 0, 0], [(IN_W, D), (D, COL), (D, D), (D, D)]
    (wgu1,) = _run_plan_on_sequencer(
        "gather_ffn1_gate_up", _gather_plan(ffn1_shards[:2], ffn_dst[:2], ffn_base[:2], ffn_shapes[:1], ffn_jump[:2]), 1)
    (wd1,) = _run_plan_on_sequencer(
        "gather_ffn1_down", _gather_plan(ffn1_shards[2:], [0], [0], ffn_shapes[1:]), 8)
    win_t, wa_t, wc, wo = _run_plan_on_sequencer(
        "gather_mix_weights", _gather_plan(mix_shards, mix_dst, mix_base, mix_shapes), 2)
    wgu2, wd2 = _run_plan_on_sequencer(
        "gather_ffn2_weights", _gather_plan(ffn2_shards, ffn_dst, ffn_base, ffn_shapes, ffn_jump), 3)

    h1 = _normmod("ffn1_normmod", x0, norm_ffn1, sc1, sh1)
    ab1, s1 = _gate_up_swiglu("ffn1_gate_up", h1, wgu1)
    f1, x1, h2 = _out_residual("ffn1_down", s1, wd1, x0, gt1, 0.5, (norm_mix, sc2, sh2))
    proj = _mm("mix_in_proj", h2, win_t, "NT", BF16, 1024, IN_W // 4, D, n_outer=True)
    wqk = jnp.concatenate([jnp.tile(q_norm, (1, 12)), jnp.tile(k_norm, (1, 12))], axis=1)
    qkn = _qknorm(proj, wqk)
    group_out = [_attn_fwd(g, qkn, proj) for g in range(3)]
    o, lse = _attn_combine([go[0] for go in group_out], [go[1] for go in group_out])
    ya = _mm("mix_attn_branch", o, wa_t, "NT", BF16, 1024, 1024, COL)
    z = _conv_fwd(proj, cw)
    yc, merged = _conv_branch_merge(z, wc, ya, proj)
    mix, x2, h3 = _out_residual("mix_out_proj", merged, wo, x1, gt2, 1.0, (norm_ffn2, sc3, sh3), tm=1024, tk=D)
    ab3, s3 = _gate_up_swiglu("ffn2_gate_up", h3, wgu2)
    f3, g3, df3, acc_out = _out_loss("ffn2_down", s3, wd2, x2, gt3, 0.5, target)
    loss_part = jnp.sum(acc_out[1])

    ffn_rows = [sh_.shape[0] for sh_ in ffn1_shards]
    mix_rows = [sh_.shape[0] for sh_ in mix_shards]
    (g2, acc3, dmix), dwgu2, dwd2 = _ffn_bwd(
        "ffn2", df3, x2, g3, h3, ab3, s3, norm_ffn2, sc3, sh3, wgu2, wd2, before=(mix, gt2, 1.0))
    dya, dyc, dgates = _d_merged_branches(dmix, wo, ya, yc, proj)
    dwo = _mm("mix_dw_out", merged, dmix, "TN", BF16, 1024, 1024, 2048)
    dproj = _copy_columns("dproj_gates", dgates, jax.ShapeDtypeStruct((s, IN_W), BF16), GA_BLK)
    dwc = _mm("mix_dw_conv_branch", z, dyc, "TN", BF16, 1024, 1024, 2048)
    dz = _mm("mix_d_z", dyc, wc, "NT", F32, 1024, 1024, D)
    dproj, d_c, cw_acc = _conv_bwd(dz, proj, cw, dproj)
    dproj = _copy_columns("copy_d_c", d_c, dproj, C_BLK)
    dwa_t = _mm("mix_dw_attn_branch", dya, o, "TN", BF16, 1024, COL, 2048)
    do, delta = _d_o_delta(dya, wa_t, o)
    dqn = dkn = None
    for g in range(3):
        dqn, dkn, dproj = _attn_bwd(g, qkn, proj, do, lse, delta, dqn, dkn, dproj)
    dproj, wq_acc = _qknorm_bwd("qnorm_bwd", proj, dqn, wqk[:, :QKW // 2], dproj, 0)
    dproj, wk_acc = _qknorm_bwd("knorm_bwd", proj, dkn, wqk[:, QKW // 2:], dproj, 1)
    r_f2g, r_f2u, r_f2d, r_wa, r_wc, r_wo = _run_plan_on_sequencer(
        "scatter_ffn2_and_branch_grads",
        _scatter_plan([dwgu2, dwd2, dwa_t, dwc, dwo], [0, 0, 1, 2, 3, 4], [0, HALF, 0, 0, 0, 0],
                      ffn_rows + mix_rows[1:], [D, D, D, COL, D, D], [HALF, HALF, 0, 0, 0, 0]), 4)
    dwin_t = _mm("mix_dw_in", dproj, h2, "TN", BF16, IN_W // 4, COL, 2048)
    (r_win,) = _run_plan_on_sequencer(
        "scatter_w_in_grad", _scatter_plan([dwin_t], [0], [0], mix_rows[:1], [D]), 5)
    g1, acc2, df1 = _d_h_norm_bwd("mix_d_h", dproj, win_t, x1, g2, norm_mix, sc2, sh2, before=(f1, gt1, 0.5),
                                  after=[dwin_t])
    dwd1 = _mm("ffn1_dw_down", s1, df1, "TN", BF16, HALF, D, 2048)
    (r_f1d,) = _run_plan_on_sequencer(
        "scatter_ffn1_down_grad", _scatter_plan([dwd1], [0], [0], ffn_rows[2:], [D]), 6)
    dab1 = _d_hidden_swiglu("ffn1_d_hidden", df1, wd1, ab1, after=[dwd1, r_win])
    dwgu1 = _mm("ffn1_dw_gate_up", dab1, h1, "TN", BF16, HALF, D, 2048)
    r_f1g, r_f1u = _run_plan_on_sequencer(
        "scatter_ffn1_gate_up_grads",
        _scatter_plan([dwgu1], [0, 0], [0, HALF], ffn_rows[:2], [D, D], [HALF, HALF]), 7)
    g0, acc1 = _d_h_norm_bwd("ffn1_d_h", dab1, wgu1, x0, g1, norm_ffn1, sc1, sh1, after=[dwgu1, r_f1d], tm=512)

    dqw = jnp.sum(wq_acc[0].reshape(12, HD), axis=0)
    dkw = jnp.sum(wk_acc[0].reshape(12, HD), axis=0)
    small = jnp.concatenate([
        acc1[0], acc1[1], acc2[3], acc2[0], acc2[1], acc3[3], acc3[0], acc3[1], acc_out[0],
        acc1[2], acc2[2], acc3[2], dqw, dkw, cw_acc[0:3].reshape(3 * D),
        jnp.zeros((HD,), F32).at[0].set(loss_part)]).reshape(1, -1)
    small_all = _small_allgather("gather_small_grads", small)
    small_sum = _sum_rows("sum_small_grads", small_all)[0]
    n_mod = N_MOD * D
    g_b_ada = small_sum[:n_mod].reshape(1, n_mod)
    g_norm1, g_norm2, g_norm3 = [small_sum[n_mod + i * D:n_mod + (i + 1) * D].reshape(1, D) for i in range(3)]
    off = n_mod + 3 * D
    g_qn, g_kn = small_sum[off:off + HD].reshape(1, HD), small_sum[off + HD:off + 2 * HD].reshape(1, HD)
    g_cw_full = small_sum[off + 2 * HD:off + 2 * HD + 3 * D].reshape(3, D)
    loss = small_sum[off + 2 * HD + 3 * D]
    g_cw = lax.dynamic_slice(g_cw_full, (0, me * cw_cols), (3, cw_cols))
    dmod_part = lax.dynamic_slice(small_all[:, 0, :n_mod], (0, me * ada_cols), (N_DEV, ada_cols))
    g_w_ada = _w_ada_grad(c_all.T, dmod_part)

    as_rows = {"ffn1_w_gate", "ffn1_w_up", "w_in", "w_attn_branch", "ffn2_w_gate", "ffn2_w_up"}
    grad_list = [g_w_ada, g_b_ada, g_norm1, r_f1g, r_f1u, r_f1d, g_norm2, r_win,
                 g_qn, g_kn, g_cw, r_wa, r_wc, r_wo, g_norm3, r_f2g, r_f2u, r_f2d]
    weights = [w_ada, b_ada, norm_ffn1, ffn1_w_gate, ffn1_w_up, ffn1_w_down, norm_mix, w_in, q_norm, k_norm,
               conv_w, w_attn_branch, w_conv_branch, w_out, norm_ffn2, ffn2_w_gate, ffn2_w_up, ffn2_w_down]
    ms = [m_w_ada, m_b_ada, m_norm_ffn1, m_ffn1_w_gate, m_ffn1_w_up, m_ffn1_w_down, m_norm_mix, m_w_in, m_q_norm,
          m_k_norm, m_conv_w, m_w_attn_branch, m_w_conv_branch, m_w_out, m_norm_ffn2, m_ffn2_w_gate,
          m_ffn2_w_up, m_ffn2_w_down]
    vs = [v_w_ada, v_b_ada, v_norm_ffn1, v_ffn1_w_gate, v_ffn1_w_up, v_ffn1_w_down, v_norm_mix, v_w_in, v_q_norm,
          v_k_norm, v_conv_w, v_w_attn_branch, v_w_conv_branch, v_w_out, v_norm_ffn2, v_ffn2_w_gate,
          v_ffn2_w_up, v_ffn2_w_down]
    wnames = ["w_ada", "b_ada", "norm_ffn1", "ffn1_w_gate", "ffn1_w_up", "ffn1_w_down", "norm_mix", "w_in",
              "q_norm", "k_norm", "conv_w", "w_attn_branch", "w_conv_branch", "w_out", "norm_ffn2",
              "ffn2_w_gate", "ffn2_w_up", "ffn2_w_down"]
    small = [i for i, gr in enumerate(grad_list) if gr.ndim == 2 and gr.size <= 16384]
    flat = lambda a, i: a.reshape(-1, weights[i].shape[-1])
    small_res = dict(zip(small, _adamw_small(
        [flat(weights[i], i) for i in small], [flat(grad_list[i], i) for i in small],
        [flat(ms[i], i) for i in small], [flat(vs[i], i) for i in small])))
    grad_out, deltas, new_ms, new_vs = [], [], [], []
    for idx, (nm, w, gr, m_, v_) in enumerate(zip(wnames, weights, grad_list, ms, vs)):
        if idx in small_res:
            gr, dl, nm_, nv_ = [r.reshape(w.shape) for r in (gr, *small_res[idx])]
        elif nm in as_rows:
            res = _adamw(f"adamw_{nm}", w[0].T, gr, m_[0].T, v_[0].T)
            gr, dl, nm_, nv_ = [r.T[None] for r in res]
        else:
            two_d = (-1, w.shape[-1])
            res = _adamw(f"adamw_{nm}", w.reshape(two_d), gr if gr.ndim == 3 else gr.reshape(two_d),
                         m_.reshape(two_d), v_.reshape(two_d))
            gr, dl, nm_, nv_ = [r.reshape(w.shape) for r in res]
        grad_out.append(gr)
        deltas.append(dl)
        new_ms.append(nm_)
        new_vs.append(nv_)
    return (loss, g0[None], *grad_out, *deltas, *new_ms, *new_vs)
```

```python
import jax
import jax.numpy as jnp
from jax import lax
from jax.experimental import pallas as pl
from jax.experimental.pallas import tpu as pltpu
from jax.experimental.pallas import tpu_sc as plsc

F32 = jnp.float32
BF16 = jnp.bfloat16
MESH = pl.DeviceIdType.MESH

N_DEV = 8
D = 1024
FF = 2816
HD = 128
N_HEADS = 4
DILATIONS = (1, 4, 16)
BAND = 128
QKW = 2 * 3 * N_HEADS * HD
IN_W = 9728
COL = 512
V_BLK, U_BLK, B_BLK, C_BLK, GA_BLK, GC_BLK = 6, 9, 11, 13, 15, 17
EPS = 1e-6
N_MOD = 9
ADAM_LR, ADAM_B1, ADAM_B2, ADAM_EPS, ADAM_WD, ADAM_STEP = 0.001, 0.9, 0.999, 1e-08, 0.01, 10

NT_DIMS = (((1,), (1,)), ((), ()))
TN_DIMS = (((0,), (0,)), ((), ()))
NN_DIMS = (((1,), (0,)), ((), ()))


def _place():
    return lax.axis_index("x"), lax.axis_index("y"), lax.axis_index("c")


def _flip(coord, bit):
    return 1 - coord if bit else coord


def _params(*sem):
    return pltpu.CompilerParams(dimension_semantics=sem)


def _small_allgather(name, v):
    n = v.shape[-1]

    def body(v_ref, out_ref, send_sems, recv_sems):
        x, y, c = _place()
        me = 4 * x + 2 * y + c
        out_ref[me] = v_ref[...]
        copies = []
        for k in range(1, N_DEV):
            peer = (_flip(x, (k >> 2) & 1), _flip(y, (k >> 1) & 1), _flip(c, k & 1))
            cp = pltpu.make_async_remote_copy(
                src_ref=v_ref, dst_ref=out_ref.at[me], send_sem=send_sems.at[k - 1],
                recv_sem=recv_sems.at[k - 1], device_id=peer, device_id_type=MESH)
            cp.start()
            copies.append(cp)
        for cp in copies:
            cp.wait()

    return pl.pallas_call(
        body, name=name,
        out_shape=jax.ShapeDtypeStruct((N_DEV, 1, n), F32),
        in_specs=[pl.BlockSpec(memory_space=pltpu.VMEM)],
        out_specs=pl.BlockSpec(memory_space=pltpu.VMEM),
        scratch_shapes=[pltpu.SemaphoreType.DMA((N_DEV - 1,)), pltpu.SemaphoreType.DMA((N_DEV - 1,))],
    )(v)


class _Plan:
    def __init__(self, operands, out_shapes, sems, phases):
        self.operands, self.out_shapes, self.sems, self.phases = operands, out_shapes, sems, phases


def _slab_start(base, rows, jump, idx):
    return pl.multiple_of(base + idx * rows + (idx // 4) * jump, 16)


def _gather_plan(shards, dst_of, base_of, dst_shapes, jump_of=None):
    n = len(shards)
    rows = [s.shape[0] for s in shards]
    jump_of = jump_of or [0] * n

    def phases(srcs, dsts, sems):
        send_sems, recv_sems, local_sems = sems
        x, y, c = _place()
        me, sibling = (x, y, c), (x, y, 1 - c)
        chips = [(1 - x, y), (x, 1 - y), (1 - x, 1 - y)]

        def slab(i, px, py, pc):
            start = _slab_start(base_of[i], rows[i], jump_of[i], 4 * px + 2 * py + pc)
            return dsts[dst_of[i]].at[pl.ds(start, rows[i])]

        def copy(i, k, block, to, src=None):
            return pltpu.make_async_remote_copy(
                src_ref=slab(i, *block) if src is None else src, dst_ref=slab(i, *block),
                send_sem=send_sems.at[i, k], recv_sem=recv_sems.at[i, k],
                device_id=to, device_id_type=MESH)

        def mine():
            return [pltpu.make_async_copy(srcs[i], slab(i, *me), local_sems.at[i]) for i in range(n)]

        def first():
            out = []
            for i in range(n):
                out.append(copy(i, 0, me, sibling, src=srcs[i]))
                out += [copy(i, 1 + j, me, (*chip, c), src=srcs[i]) for j, chip in enumerate(chips)]
            return out

        def passed():
            return [(copy(i, 1 + j, (*chip, c), me), copy(i, 4 + j, (*chip, c), sibling))
                    for j, chip in enumerate(chips) for i in range(n)]

        def start():
            for cp in mine() + first():
                cp.start()

        def middle():
            for landed, onward in passed():
                landed.wait_recv()
                onward.start()

        def finish():
            for i in range(n):
                copy(i, 0, sibling, me).wait_recv()
                for j, chip in enumerate(chips):
                    copy(i, 4 + j, (*chip, 1 - c), me).wait_recv()
            for cp in first() + [onward for _, onward in passed()]:
                cp.wait_send()
            for cp in mine():
                cp.wait()

        return start, middle, finish

    sems = [pltpu.SemaphoreType.DMA((n, 7)), pltpu.SemaphoreType.DMA((n, 7)), pltpu.SemaphoreType.DMA((n,))]
    return _Plan(list(shards), [jax.ShapeDtypeStruct(s, BF16) for s in dst_shapes], sems, phases)


def _scatter_plan(grads, src_of, base_of, rows, cols, jump_of=None):
    n = len(rows)
    jump_of = jump_of or [0] * n

    def phases(srcs, recvs, sems):
        send_sems, recv_sems, local_sems = sems
        x, y, c = _place()
        me = 4 * x + 2 * y + c

        def slab(i, idx):
            start = _slab_start(base_of[i], rows[i], jump_of[i], idx)
            return srcs[src_of[i]].at[pl.ds(start, rows[i])]

        def copies():
            out = [pltpu.make_async_copy(slab(i, me), recvs[i].at[me], local_sems.at[i]) for i in range(n)]
            for k in range(1, N_DEV):
                px, py, pc = _flip(x, (k >> 2) & 1), _flip(y, (k >> 1) & 1), _flip(c, k & 1)
                out += [pltpu.make_async_remote_copy(
                    src_ref=slab(i, 4 * px + 2 * py + pc), dst_ref=recvs[i].at[me],
                    send_sem=send_sems.at[i, k - 1], recv_sem=recv_sems.at[i, k - 1],
                    device_id=(px, py, pc), device_id_type=MESH) for i in range(n)]
            return out

        def start():
            for cp in copies():
                cp.start()

        def finish():
            for cp in copies():
                cp.wait()

        return start, None, finish

    sems = [pltpu.SemaphoreType.DMA((n, 7)), pltpu.SemaphoreType.DMA((n, 7)), pltpu.SemaphoreType.DMA((n,))]
    out_shapes = [jax.ShapeDtypeStruct((N_DEV, rows[i], cols[i]), BF16) for i in range(n)]
    return _Plan(list(grads), out_shapes, sems, phases)


def _run_plan_on_sequencer(name, plan, collective_id):
    src_refs = [jax.new_ref(a, memory_space=pltpu.MemorySpace.HBM) for a in plan.operands]
    dst_refs = [jax.empty_ref(s, memory_space=pltpu.MemorySpace.HBM) for s in plan.out_shapes]

    @pl.kernel(mesh=plsc.ScalarSubcoreMesh(axis_name="sequencer", num_cores=1), name=name,
               scratch_types=tuple(plan.sems),
               compiler_params=pltpu.CompilerParams(collective_id=collective_id))
    def launch(*sems):
        x, y, c = _place()
        barrier = pltpu.get_barrier_semaphore()
        for k in range(1, N_DEV):
            peer = (_flip(x, (k >> 2) & 1), _flip(y, (k >> 1) & 1), _flip(c, k & 1))
            pl.semaphore_signal(barrier, inc=1, device_id=peer, device_id_type=MESH)
        pl.semaphore_wait(barrier, N_DEV - 1)
        for phase in plan.phases(src_refs, dst_refs, sems):
            if phase is not None:
                phase()

    launch()
    return [r[...] for r in dst_refs]


def _mm(name, a, b, mode, out_dtype, tm, tn, tk, *, tiles_in=(), tiles_out=(), epilogue=None,
        n_outer=False, keep_b=False, col_chunks=None, after=()):
    if mode == "TN":
        kk, m = a.shape
    else:
        m, kk = a.shape
    n = b.shape[0] if mode == "NT" else b.shape[1]
    tm, tn, tk = min(tm, m), min(tn, n), min(tk, kk)
    assert m % tm == 0 and n % tn == 0 and kk % tk == 0, (name, m, n, kk, tm, tn, tk)
    ni, nj, nk = m // tm, n // tn, kk // tk
    dims = {"NN": NN_DIMS, "NT": NT_DIMS, "TN": TN_DIMS}[mode]
    if epilogue is None:
        tiles_out = [(jax.ShapeDtypeStruct((m, n), out_dtype), (tm, tn), lambda i, j: (i, j))]
    n_tin, n_tout = len(tiles_in), len(tiles_out)
    n_acc = 1 if nk > 1 else 0
    n_after = len(after)
    assert not keep_b or (nk == 1 and nj == 1)
    assert not col_chunks or (epilogue is not None and nk == 1 and mode != "TN")
    ij = (lambda p, q: (q, p)) if n_outer else (lambda p, q: (p, q))
    inner = ni if n_outer else nj

    def body(a_ref, b_ref, *rest):
        tin = rest[:n_tin]
        tout = rest[n_tin + n_after:n_tin + n_after + n_tout]
        scratch = rest[n_tin + n_after + n_tout:]
        k = pl.program_id(2)
        visit = pl.program_id(0) * inner + pl.program_id(1)
        if keep_b:
            b_kept, b_sem = scratch[n_acc:n_acc + 2]

            @pl.when((visit == 0) & (k == 0))
            def _():
                cp = pltpu.make_async_copy(b_ref, b_kept, b_sem)
                cp.start()
                cp.wait()

            b_ref = b_kept

        def store(prod, c=0, cols=()):
            if epilogue is None:
                tout[0][...] = prod.astype(out_dtype)
            else:
                epilogue(prod, jnp.logical_and(visit == 0, c == 0), tin, tout, *cols)

        if col_chunks:
            for c, (c0, cw) in enumerate(col_chunks):
                b_part = b_ref[pl.ds(c0, cw), :] if mode == "NT" else b_ref[:, pl.ds(c0, cw)]
                store(lax.dot_general(a_ref[...], b_part, dims, preferred_element_type=F32), c, ((c0, cw),))
        else:
            part = lax.dot_general(a_ref[...], b_ref[...], dims, preferred_element_type=F32)
            if nk == 1:
                store(part)
            else:
                acc_ref = scratch[0]

                @pl.when(k == 0)
                def _():
                    acc_ref[...] = part

                @pl.when((k > 0) & (k < nk - 1))
                def _():
                    acc_ref[...] += part

                @pl.when(k == nk - 1)
                def _():
                    store(acc_ref[...] + part)

    def spec(shape, fn):
        return pl.BlockSpec(shape, lambda p, q, k: fn(*ij(p, q)))

    a_spec = (pl.BlockSpec((tk, tm), lambda p, q, k: (k, ij(p, q)[0])) if mode == "TN"
              else pl.BlockSpec((tm, tk), lambda p, q, k: (ij(p, q)[0], k)))
    if keep_b:
        b_spec = pl.BlockSpec(memory_space=pl.ANY)
    elif mode == "NT":
        b_spec = pl.BlockSpec((tn, tk), lambda p, q, k: (ij(p, q)[1], k))
    else:
        b_spec = pl.BlockSpec((tk, tn), lambda p, q, k: (k, ij(p, q)[1]))
    sequential = epilogue or keep_b
    out = pl.pallas_call(
        body, name=name, grid=(nj, ni, nk) if n_outer else (ni, nj, nk),
        out_shape=[t[0] for t in tiles_out],
        in_specs=([a_spec, b_spec] + [spec(t[1], t[2]) for t in tiles_in]
                  + [pl.BlockSpec(memory_space=pl.ANY)] * n_after),
        out_specs=[spec(t[1], t[2]) for t in tiles_out],
        scratch_shapes=([pltpu.VMEM((tm, tn), F32)] * n_acc
                        + ([pltpu.VMEM(b.shape, b.dtype), pltpu.SemaphoreType.DMA] if keep_b else [])),
        compiler_params=(_params("arbitrary", "arbitrary", "arbitrary") if sequential
                         else _params("parallel", "parallel", "arbitrary")),
    )(a, b, *[t[0] for t in tiles_in], *after)
    return out if epilogue else out[0]


def _row(tm, w, off=0):
    return pl.BlockSpec((tm, w), lambda i: (i, off))


def _vec(w):
    return pl.BlockSpec((1, w), lambda i: (0, 0))


def _sigmoid(x):
    return 0.5 * jnp.tanh(0.5 * x) + 0.5


def _normmod(name, x, g, sc, sh, tm=1024):
    s = x.shape[0]

    def body(x_ref, g_ref, sc_ref, sh_ref, h_ref):
        xv = x_ref[...]
        r = lax.rsqrt(jnp.mean(xv * xv, axis=-1, keepdims=True) + EPS)
        h_ref[...] = ((xv * r) * g_ref[...] * (1.0 + sc_ref[...]) + sh_ref[...]).astype(BF16)

    return pl.pallas_call(
        body, name=name, grid=(s // tm,),
        out_shape=jax.ShapeDtypeStruct((s, D), BF16),
        in_specs=[_row(tm, D), _vec(D), _vec(D), _vec(D)], out_specs=_row(tm, D),
        compiler_params=_params("parallel"),
    )(x, g, sc, sh)


def _heads(x, fn):
    return jnp.concatenate([fn(x[:, h * HD:(h + 1) * HD], h) for h in range(x.shape[1] // HD)], axis=1)


def _qknorm(proj, wqk, tm=512):
    s = proj.shape[0]

    def body(p_ref, w_ref, o_ref):
        pv = p_ref[...].astype(F32)
        wv = w_ref[...]

        def one(qh, h):
            r = lax.rsqrt(jnp.mean(qh * qh, axis=-1, keepdims=True) + EPS)
            return (qh * r) * wv[:, h * HD:(h + 1) * HD]

        o_ref[...] = _heads(pv, one).astype(BF16)

    return pl.pallas_call(
        body, name="qknorm", grid=(s // tm,),
        out_shape=jax.ShapeDtypeStruct((s, QKW), BF16),
        in_specs=[pl.BlockSpec((tm, QKW), lambda i: (i, 0)), pl.BlockSpec((1, QKW), lambda i: (0, 0))],
        out_specs=pl.BlockSpec((tm, QKW), lambda i: (i, 0)),
        compiler_params=_params("parallel"),
    )(proj, wqk)


def _qknorm_bwd(name, proj, dn, w, dproj, blk0, tm=512):
    s, width = dn.shape

    def body(p_ref, d_ref, w_ref, _, o_ref, acc_ref):
        pv = p_ref[...].astype(F32)
        dv = d_ref[...]
        wv = w_ref[...]
        sums = []

        def one(qh, h):
            dn = dv[:, h * HD:(h + 1) * HD]
            r = lax.rsqrt(jnp.mean(qh * qh, axis=-1, keepdims=True) + EPS)
            nh = qh * r
            sums.append(jnp.sum(dn * nh, axis=0, keepdims=True))
            dnw = dn * wv[:, h * HD:(h + 1) * HD]
            return r * (dnw - nh * jnp.mean(dnw * nh, axis=-1, keepdims=True))

        o_ref[...] = _heads(pv, one).astype(BF16)

        @pl.when(pl.program_id(0) == 0)
        def _():
            acc_ref[...] = jnp.zeros_like(acc_ref)

        acc_ref[0:1, :] += jnp.concatenate(sums, axis=1)

    return pl.pallas_call(
        body, name=name, grid=(s // tm,),
        out_shape=[jax.ShapeDtypeStruct((s, IN_W), BF16), jax.ShapeDtypeStruct((8, width), F32)],
        in_specs=[pl.BlockSpec((tm, width), lambda i: (i, blk0)), pl.BlockSpec((tm, width), lambda i: (i, 0)),
                  pl.BlockSpec((1, width), lambda i: (0, 0)), pl.BlockSpec(memory_space=pl.ANY)],
        out_specs=[pl.BlockSpec((tm, width), lambda i: (i, blk0)), pl.BlockSpec((8, width), lambda i: (0, 0))],
        input_output_aliases={3: 0},
        compiler_params=_params("arbitrary"),
    )(proj, dn, w, dproj)


def _attn_shapes(s, g, sub_block):
    d = DILATIONS[g]
    tb = min(s, max(2048, 256 * d))
    sb = min(sub_block, tb // d)
    pb = BAND * d
    assert s % tb == 0 and tb % pb == 0 and (tb // d) % sb == 0 and sb % BAND == 0
    return d, tb, sb, pb


def _lanes(x, width):
    return jnp.concatenate([x] * (width // HD), axis=1)


def _every(start, size, d):
    return pl.ds(start, size, stride=d) if d > 1 else pl.ds(start, size)


def _attn_specs(g, tb, pb, s, ahead):
    ratio = tb // pb
    if ahead:
        nbr = lambda n: jnp.minimum((n + 1) * ratio, s // pb - 1)
    else:
        nbr = lambda n: jnp.maximum(n * ratio - 1, 0)
    cur = lambda base: pl.BlockSpec((tb, HD), lambda h, n: (n, base + g * N_HEADS + h))
    side = lambda base: pl.BlockSpec((pb, HD), lambda h, n: (nbr(n), base + g * N_HEADS + h))
    tok = pl.BlockSpec((tb, HD), lambda h, n: (n, h))
    tok_side = pl.BlockSpec((pb, HD), lambda h, n: (nbr(n), h))
    return cur, side, tok, tok_side


Q_COL, K_COL, V_COL = 0, 12, 24


def _attn_fwd(g, qkn, proj):
    s = qkn.shape[0]
    d, tb, sb, pb = _attn_shapes(s, g, 128)
    ft = F32 if d > 1 else BF16
    nj = tb // d // sb
    scale = HD ** -0.5

    def body(q_ref, kc_ref, kp_ref, vc_ref, vp_ref, o_ref, lse_ref, qf, kf, vf):
        n = pl.program_id(1)
        qf[...] = q_ref[...].astype(ft)
        kf[0:pb] = kp_ref[...].astype(ft)
        kf[pb:] = kc_ref[...].astype(ft)
        vf[0:pb] = vp_ref[...].astype(ft)
        vf[pb:] = vc_ref[...].astype(ft)
        for r in range(d):
            for j in range(nj):
                at = j * sb * d + r
                q = qf[_every(at, sb, d), :].astype(BF16)
                k = kf[_every(at, sb + BAND, d), :].astype(BF16)
                v = vf[_every(at, sb + BAND, d), :].astype(BF16)
                sc = lax.dot_general(q, k, NT_DIMS, preferred_element_type=F32) * scale
                qi = lax.broadcasted_iota(jnp.int32, sc.shape, 0)
                kj = lax.broadcasted_iota(jnp.int32, sc.shape, 1)
                valid = (kj >= qi) & (kj <= qi + BAND)
                if j == 0:
                    valid = valid & ((kj >= BAND) | (n > 0))
                sc = jnp.where(valid, sc, -1e30)
                m = jnp.max(sc, axis=-1, keepdims=True)
                p = jnp.exp(sc - m)
                l = jnp.sum(p, axis=-1, keepdims=True)
                o = lax.dot_general(p.astype(BF16), v, NN_DIMS, preferred_element_type=F32)
                o_ref[_every(at, sb, d), :] = o / l
                lse_ref[_every(at, sb, d), :] = jnp.broadcast_to(m + jnp.log(l), (sb, HD))

    cur, side, tok, _ = _attn_specs(g, tb, pb, s, ahead=False)
    return pl.pallas_call(
        body, name=f"attn_fwd_g{g}", grid=(N_HEADS, s // tb),
        out_shape=[jax.ShapeDtypeStruct((s, COL), F32)] * 2,
        in_specs=[cur(Q_COL), cur(K_COL), side(K_COL), cur(V_COL), side(V_COL)],
        out_specs=[tok, tok],
        scratch_shapes=[pltpu.VMEM((tb, HD), ft), pltpu.VMEM((tb + pb, HD), ft),
                        pltpu.VMEM((tb + pb, HD), ft)],
        compiler_params=_params("parallel", "arbitrary"),
    )(qkn, qkn, qkn, proj, proj)


def _attn_combine(os_, lses, tm=1024):
    s = os_[0].shape[0]

    def body(o0, o1, o2, l0, l1, l2, o_ref, lse_ref):
        a, b, c = l0[...], l1[...], l2[...]
        m = jnp.maximum(jnp.maximum(a, b), c)
        ea, eb, ec = jnp.exp(a - m), jnp.exp(b - m), jnp.exp(c - m)
        tot = ea + eb + ec
        o_ref[...] = ((ea * o0[...] + eb * o1[...] + ec * o2[...]) / tot).astype(BF16)
        lse_ref[...] = m + jnp.log(tot)

    return pl.pallas_call(
        body, name="attn_combine", grid=(s // tm,),
        out_shape=[jax.ShapeDtypeStruct((s, COL), BF16), jax.ShapeDtypeStruct((s, COL), F32)],
        in_specs=[_row(tm, COL)] * 6, out_specs=[_row(tm, COL)] * 2,
        compiler_params=_params("parallel"),
    )(*os_, *lses)


def _attn_bwd(g, qkn, proj, do, lse, delta, dqn, dkn, dproj):
    s = qkn.shape[0]
    d, tb, sb, pb = _attn_shapes(s, g, 512)
    ft = F32 if d > 1 else BF16
    nj = tb // d // sb
    nt = s // tb
    scale = HD ** -0.5
    chained = dqn is not None

    def body(k_ref, v_ref, qc_ref, qn_ref, doc_ref, don_ref, lc_ref, ln_ref, dc_ref, dn_ref, *rest):
        dq_ref, dk_ref, dv_ref, kf, vf, qf, dvf, later = rest[-8:]
        n = pl.program_id(1)
        kf[...] = k_ref[...].astype(ft)
        vf[...] = v_ref[...].astype(ft)
        qf[0:tb] = qc_ref[...].astype(ft)
        qf[tb:] = qn_ref[...].astype(ft)

        @pl.when(n == 0)
        def _():
            later[...] = jnp.zeros_like(later)

        def window(c_ref, n_ref, r, j):
            at = j * sb * d + r
            if j < nj - 1:
                return c_ref[_every(at, sb + BAND, d), :]
            return jnp.concatenate([c_ref[_every(at, sb, d), :], n_ref[_every(r, BAND, d), :]], axis=0)

        for r in range(d):
            tail = later[r]
            for j in range(nj):
                at = j * sb * d + r
                rows = _every(at, sb, d)
                k = kf[rows, :].astype(BF16)
                v = vf[rows, :].astype(BF16)
                q = qf[_every(at, sb + BAND, d), :].astype(BF16)
                dov = window(doc_ref, don_ref, r, j).astype(BF16)
                sc = lax.dot_general(q, k, NT_DIMS, preferred_element_type=F32) * scale
                qi = lax.broadcasted_iota(jnp.int32, sc.shape, 0)
                kj = lax.broadcasted_iota(jnp.int32, sc.shape, 1)
                valid = (qi >= kj) & (qi <= kj + BAND)
                if j == nj - 1:
                    valid = valid & ((qi < sb) | (n < nt - 1))
                p = jnp.exp(jnp.where(valid, sc - _lanes(window(lc_ref, ln_ref, r, j), sb), -1e30))
                dp = lax.dot_general(dov, v, NT_DIMS, preferred_element_type=F32)
                ds = (p * (dp - _lanes(window(dc_ref, dn_ref, r, j), sb)) * scale).astype(BF16)
                dvf[rows, :] = lax.dot_general(p.astype(BF16), dov, TN_DIMS, preferred_element_type=F32)
                dk_ref[rows, :] = lax.dot_general(ds, q, TN_DIMS, preferred_element_type=F32)
                dqw = lax.dot_general(ds, k, NN_DIMS, preferred_element_type=F32)
                first = dqw[:BAND] + tail
                dq_ref[rows, :] = first if sb == BAND else jnp.concatenate([first, dqw[BAND:sb]], axis=0)
                tail = dqw[sb:]
            later[r] = tail
        dv_ref[...] = dvf[...].astype(BF16)

    cur, side, tok, tok_side = _attn_specs(g, tb, pb, s, ahead=True)
    anyspec = pl.BlockSpec(memory_space=pl.ANY)
    n_heads_cols = 3 * N_HEADS * HD
    return pl.pallas_call(
        body, name=f"attn_bwd_g{g}", grid=(N_HEADS, nt),
        out_shape=[jax.ShapeDtypeStruct((s, n_heads_cols), F32), jax.ShapeDtypeStruct((s, n_heads_cols), F32),
                   jax.ShapeDtypeStruct((s, IN_W), BF16)],
        in_specs=[cur(K_COL), cur(V_COL), cur(Q_COL), side(Q_COL), tok, tok_side, tok, tok_side,
                  tok, tok_side] + ([anyspec, anyspec] if chained else []) + [anyspec],
        out_specs=[cur(0), cur(0), cur(V_COL)],
        input_output_aliases={10: 0, 11: 1, 12: 2} if chained else {10: 2},
        scratch_shapes=[pltpu.VMEM((tb, HD), ft), pltpu.VMEM((tb, HD), ft),
                        pltpu.VMEM((tb + pb, HD), ft), pltpu.VMEM((tb, HD), F32),
                        pltpu.VMEM((d, BAND, HD), F32)],
        compiler_params=_params("arbitrary", "arbitrary"),
    )(qkn, proj, qkn, qkn, do, do, lse, lse, delta, delta, *([dqn, dkn] if chained else []), dproj)


def _shift_down(x, before, k):
    rolled = pltpu.roll(x, k, 0)
    head = jnp.where(lax.broadcasted_iota(jnp.int32, before.shape, 0) < k, pltpu.roll(before, k, 0), rolled[:8])
    return jnp.concatenate([head, rolled[8:]], axis=0)


def _shift_up(x, after, k):
    rows = x.shape[0]
    rolled = pltpu.roll(x, rows - k, 0)
    tail = jnp.where(lax.broadcasted_iota(jnp.int32, after.shape, 0) >= 8 - k,
                     pltpu.roll(after, 8 - k, 0), rolled[rows - 8:])
    return jnp.concatenate([rolled[:rows - 8], tail], axis=0)


def _conv_fwd(proj, cw, tm=1024):
    s = proj.shape[0]
    r16 = tm // 16

    def body(u_ref, b_ref, c_ref, up_ref, cp_ref, w_ref, z_ref):
        i = pl.program_id(1)
        xc = c_ref[...].astype(F32) * u_ref[...].astype(F32)
        xp = jnp.where(i > 0, cp_ref[8:16, :].astype(F32) * up_ref[8:16, :].astype(F32), 0.0)
        w = w_ref[...]
        conv = _shift_down(xc, xp, 2) * w[0:1] + _shift_down(xc, xp, 1) * w[1:2] + xc * w[2:3]
        z_ref[...] = (b_ref[...].astype(F32) * conv).astype(BF16)

    tile = lambda blk: pl.BlockSpec((tm, COL), lambda j, i: (i, blk + j))
    before = lambda blk: pl.BlockSpec((16, COL), lambda j, i: (jnp.maximum(i * r16 - 1, 0), blk + j))
    return pl.pallas_call(
        body, name="conv_fwd", grid=(D // COL, s // tm),
        out_shape=jax.ShapeDtypeStruct((s, D), BF16),
        in_specs=[tile(U_BLK), tile(B_BLK), tile(C_BLK), before(U_BLK), before(C_BLK),
                  pl.BlockSpec((3, COL), lambda j, i: (0, j))],
        out_specs=pl.BlockSpec((tm, COL), lambda j, i: (i, j)),
        compiler_params=_params("parallel", "parallel"),
    )(proj, proj, proj, proj, proj, cw)


def _conv_bwd(dz, proj, cw, dproj, tm=1024):
    s = proj.shape[0]
    r16 = tm // 16
    nrow = s // tm

    def body(dz_ref, u_ref, b_ref, c_ref, up_ref, cp_ref, dzn_ref, bn_ref, w_ref, _, o_ref, dc_ref, acc_ref):
        piece, i = pl.program_id(1), pl.program_id(2)
        u, c = u_ref[...].astype(F32), c_ref[...].astype(F32)
        bv = b_ref[...].astype(F32)
        dzv = dz_ref[...]
        w = w_ref[...]

        @pl.when((piece == 0) & (i == 0))
        def _():
            acc_ref[...] = jnp.zeros_like(acc_ref)

        @pl.when(piece == 0)
        def _():
            xc = c * u
            xp = jnp.where(i > 0, cp_ref[8:16, :].astype(F32) * up_ref[8:16, :].astype(F32), 0.0)
            x2, x1 = _shift_down(xc, xp, 2), _shift_down(xc, xp, 1)
            o_ref[...] = (dzv * (x2 * w[0:1] + x1 * w[1:2] + xc * w[2:3])).astype(BF16)
            dc_ref[...] = jnp.zeros_like(dc_ref)
            dconv = dzv * bv
            acc_ref[0:1, :] += jnp.sum(dconv * x2, axis=0, keepdims=True)
            acc_ref[1:2, :] += jnp.sum(dconv * x1, axis=0, keepdims=True)
            acc_ref[2:3, :] += jnp.sum(dconv * xc, axis=0, keepdims=True)

        @pl.when(piece == 1)
        def _():
            dconv = dzv * bv
            dn = jnp.where(i < nrow - 1, dzn_ref[...] * bn_ref[0:8, :].astype(F32), 0.0)
            dxc = dconv * w[2:3] + _shift_up(dconv, dn, 1) * w[1:2] + _shift_up(dconv, dn, 2) * w[0:1]
            o_ref[...] = (dxc * c).astype(BF16)
            dc_ref[...] = (dxc * u).astype(BF16)

    tile = lambda blk: pl.BlockSpec((tm, COL), lambda j, p, i: (i, blk + j))
    before = lambda blk: pl.BlockSpec((16, COL), lambda j, p, i: (jnp.maximum(i * r16 - 1, 0), blk + j))
    after = lambda rows, blk: pl.BlockSpec(
        (rows, COL), lambda j, p, i: (jnp.minimum((i + 1) * (tm // rows), s // rows - 1), blk + j))
    return pl.pallas_call(
        body, name="conv_bwd", grid=(D // COL, 2, nrow),
        out_shape=[jax.ShapeDtypeStruct((s, IN_W), BF16), jax.ShapeDtypeStruct((s + tm, D), BF16),
                   jax.ShapeDtypeStruct((8, D), F32)],
        in_specs=[tile(0), tile(U_BLK), tile(B_BLK), tile(C_BLK), before(U_BLK), before(C_BLK),
                  after(8, 0), after(16, B_BLK), pl.BlockSpec((3, COL), lambda j, p, i: (0, j)),
                  pl.BlockSpec(memory_space=pl.ANY)],
        out_specs=[pl.BlockSpec((tm, COL), lambda j, p, i: (i, jnp.where(p == 0, B_BLK, U_BLK) + j)),
                   pl.BlockSpec((tm, COL), lambda j, p, i: (jnp.where(p == 0, nrow, i), j)),
                   pl.BlockSpec((8, COL), lambda j, p, i: (0, j))],
        input_output_aliases={9: 0},
        compiler_params=_params("arbitrary", "arbitrary", "arbitrary"),
    )(dz, proj, proj, proj, proj, proj, dz, proj, cw, dproj)


def _copy_columns(name, src, dst, blk0, tm=2048):
    s, w = dst.shape[0], src.shape[1]
    fresh = isinstance(dst, jax.ShapeDtypeStruct)

    def body(x_ref, *rest):
        rest[-1][...] = x_ref[...]

    return pl.pallas_call(
        body, name=name, grid=(w // COL, s // tm),
        out_shape=jax.ShapeDtypeStruct(dst.shape, dst.dtype),
        in_specs=[pl.BlockSpec((tm, COL), lambda j, i: (i, j))] + ([] if fresh else [pl.BlockSpec(memory_space=pl.ANY)]),
        out_specs=pl.BlockSpec((tm, COL), lambda j, i: (i, blk0 + j)),
        input_output_aliases={} if fresh else {1: 0},
        compiler_params=_params("parallel", "parallel"),
    )(src, *([] if fresh else [dst]))


def _mod_part(c_all, w_ada, b_part):
    def body(c_ref, w_ref, b_ref, o_ref):
        cv = c_ref[...]
        act = cv * _sigmoid(cv)
        o_ref[...] = jnp.dot(act, w_ref[...], preferred_element_type=F32,
                             precision=lax.Precision.HIGHEST) + b_ref[...]

    return pl.pallas_call(
        body, name="mod_part", out_shape=jax.ShapeDtypeStruct((N_DEV, w_ada.shape[1]), F32),
    )(c_all, w_ada, b_part)


def _w_ada_grad(c_all_t, dmod_part):
    def body(c_ref, d_ref, o_ref):
        cv = c_ref[...]
        act = cv * _sigmoid(cv)
        dv = d_ref[...]
        acc = act[:, 0:1] * dv[0:1, :]
        for b in range(1, N_DEV):
            acc = acc + act[:, b:b + 1] * dv[b:b + 1, :]
        o_ref[...] = acc

    return pl.pallas_call(
        body, name="w_ada_grad", out_shape=jax.ShapeDtypeStruct((D, dmod_part.shape[1]), F32),
    )(c_all_t, dmod_part)


def _sum_rows(name, v):
    def body(v_ref, o_ref):
        acc = v_ref[0]
        for k in range(1, N_DEV):
            acc = acc + v_ref[k]
        o_ref[...] = acc

    return pl.pallas_call(body, name=name, out_shape=jax.ShapeDtypeStruct(v.shape[1:], F32))(v)


def _adamw(name, w, g, m, v):
    rows, cols = w.shape
    limit = max(16, (1 << 20) // (4 * cols))
    tr = rows if rows <= limit else next((t for t in range(limit - limit % 16, 15, -16) if rows % t == 0), rows)
    c1 = 1.0 - ADAM_B1 ** ADAM_STEP
    c2 = 1.0 - ADAM_B2 ** ADAM_STEP
    parts = g.ndim == 3

    def body(w_ref, g_ref, m_ref, v_ref, go_ref, d_ref, nm_ref, nv_ref):
        if parts:
            gv = g_ref[0].astype(F32)
            for k in range(1, N_DEV):
                gv = gv + g_ref[k].astype(F32)
        else:
            gv = g_ref[...]
        go_ref[...] = gv
        nm = ADAM_B1 * m_ref[...] + (1.0 - ADAM_B1) * gv
        nv = ADAM_B2 * v_ref[...] + (1.0 - ADAM_B2) * (gv * gv)
        nm_ref[...] = nm
        nv_ref[...] = nv
        d_ref[...] = -ADAM_LR * ((nm / c1) / (jnp.sqrt(nv / c2) + ADAM_EPS) + ADAM_WD * w_ref[...])

    spec = pl.BlockSpec((tr, cols), lambda i: (i, 0))
    g_spec = pl.BlockSpec((N_DEV, tr, cols), lambda i: (0, i, 0)) if parts else spec
    return pl.pallas_call(
        body, name=name, grid=(rows // tr,),
        out_shape=[jax.ShapeDtypeStruct((rows, cols), F32)] * 4,
        in_specs=[spec, g_spec, spec, spec], out_specs=[spec] * 4,
        compiler_params=_params("parallel"),
    )(w, g, m, v)


def _adamw_small(ws, gs, ms, vs):
    n = len(ws)
    c1 = 1.0 - ADAM_B1 ** ADAM_STEP
    c2 = 1.0 - ADAM_B2 ** ADAM_STEP

    def body(*refs):
        for i in range(n):
            w_ref, g_ref, m_ref, v_ref = refs[i], refs[n + i], refs[2 * n + i], refs[3 * n + i]
            d_ref, nm_ref, nv_ref = refs[4 * n + 3 * i:4 * n + 3 * i + 3]
            gv = g_ref[...]
            nm = ADAM_B1 * m_ref[...] + (1.0 - ADAM_B1) * gv
            nv = ADAM_B2 * v_ref[...] + (1.0 - ADAM_B2) * (gv * gv)
            nm_ref[...] = nm
            nv_ref[...] = nv
            d_ref[...] = -ADAM_LR * ((nm / c1) / (jnp.sqrt(nv / c2) + ADAM_EPS) + ADAM_WD * w_ref[...])

    outs = pl.pallas_call(
        body, name="adamw_small",
        out_shape=[jax.ShapeDtypeStruct(w.shape, F32) for w in ws for _ in range(3)],
    )(*ws, *gs, *ms, *vs)
    return [tuple(outs[3 * i:3 * i + 3]) for i in range(n)]


HALF = FF // 2


def _sds(shape, dtype):
    return jax.ShapeDtypeStruct(shape, dtype)


def _row_tile(w):
    return lambda tm: ((tm, w), lambda i, j: (i, 0))


def _one(w):
    return lambda rows: ((rows, w), lambda i, j: (0, 0))


def _gate_up_swiglu(name, h, wgu, tm=1024):
    s = h.shape[0]
    tm = min(tm, s)

    def epilogue(prod, first, tin, tout):
        pq_ref, s_ref = tout
        a, b = prod[:, :HALF], prod[:, HALF:]
        sig = _sigmoid(a)
        act = a * sig
        pq_ref[:, :HALF] = (b * (sig * (1.0 + a * (1.0 - sig)))).astype(BF16)
        pq_ref[:, HALF:] = act.astype(BF16)
        s_ref[...] = (act * b).astype(BF16)

    return _mm(name, h, wgu, "NT", None, tm, FF, D, n_outer=True, epilogue=epilogue,
               tiles_out=[(_sds((s, 2 * FF), BF16), (tm, FF), lambda i, j: (i, j)),
                          (_sds((s, FF), BF16), (tm, HALF), lambda i, j: (i, j))])


def _d_hidden_swiglu(name, df, wd, ab, after=(), tm=1024):
    s = df.shape[0]
    tm = min(tm, s)

    def epilogue(prod, first, tin, tout, cols):
        da_cols = slice(cols[0], cols[0] + cols[1])
        db_cols = slice(HALF + cols[0], HALF + cols[0] + cols[1])
        tout[0][:, da_cols] = (prod * tin[0][:, da_cols].astype(F32)).astype(BF16)
        tout[0][:, db_cols] = (prod * tin[0][:, db_cols].astype(F32)).astype(BF16)

    chunks = [(c0, min(384, HALF - c0)) for c0 in range(0, HALF, 384)]
    return _mm(name, df, wd, "NT", None, tm, HALF, D, n_outer=True, epilogue=epilogue, col_chunks=chunks, after=after,
               tiles_in=[(ab, (tm, FF), lambda i, j: (i, j))],
               tiles_out=[(_sds((s, 2 * FF), BF16), (tm, FF), lambda i, j: (i, j))])[0]


def _out_residual(name, a, w, x, gt, coef, nxt, tm=512, tk=FF):
    s = a.shape[0]
    tm = min(tm, s)

    def epilogue(prod, first, tin, tout):
        x_ref, gt_ref, g_ref, sc_ref, sh_ref = tin
        f_ref, xn_ref, h_ref = tout
        f_ref[...] = prod
        xn = x_ref[...] + (coef * gt_ref[...]) * prod
        xn_ref[...] = xn
        r = lax.rsqrt(jnp.mean(xn * xn, axis=-1, keepdims=True) + EPS)
        h_ref[...] = ((xn * r) * g_ref[...] * (1.0 + sc_ref[...]) + sh_ref[...]).astype(BF16)

    row, vec = _row_tile(D)(tm), _one(D)(1)
    return _mm(name, a, w, "NN", None, tm, D, tk, epilogue=epilogue,
               tiles_in=[(x, *row), (gt, *vec)] + [(v, *vec) for v in nxt],
               tiles_out=[(_sds((s, D), F32), *row), (_sds((s, D), F32), *row), (_sds((s, D), BF16), *row)])


def _out_loss(name, a, w, x, gt, coef, target, tm=512):
    s = a.shape[0]
    tm = min(tm, s)

    def epilogue(prod, first, tin, tout):
        x_ref, gt_ref, t_ref = tin
        f_ref, g_ref, df_ref, acc_ref = tout
        f_ref[...] = prod
        cg = coef * gt_ref[...]
        e = x_ref[...] + cg * prod - t_ref[...]
        gv = e * (1.0 / D)
        g_ref[...] = gv
        df_ref[...] = (cg * gv).astype(BF16)

        @pl.when(first)
        def _():
            acc_ref[...] = jnp.zeros_like(acc_ref)

        acc_ref[0:1, :] += coef * jnp.sum(gv * prod, axis=0, keepdims=True)
        acc_ref[1:2, :] += (0.5 / D) * jnp.sum(e * e, axis=0, keepdims=True)

    row, vec = _row_tile(D)(tm), _one(D)(1)
    return _mm(name, a, w, "NN", None, tm, D, FF, epilogue=epilogue,
               tiles_in=[(x, *row), (gt, *vec), (target, *row)],
               tiles_out=[(_sds((s, D), F32), *row), (_sds((s, D), F32), *row), (_sds((s, D), BF16), *row),
                          (_sds((8, D), F32), *_one(D)(8))])


def _d_h_norm_bwd(name, da, w, x, gin, g, sc, sh, before=None, after=(), tm=256):
    s = da.shape[0]
    tm = min(tm, s)
    coef = before[2] if before else None

    def epilogue(prod, first, tin, tout):
        x_ref, gin_ref, g_ref, sc_ref, sh_ref = tin[:5]
        gout_ref, acc_ref = tout[:2]
        xv = x_ref[...]
        r = lax.rsqrt(jnp.mean(xv * xv, axis=-1, keepdims=True) + EPS)
        nv = xv * r
        gv, one_sc = g_ref[...], 1.0 + sc_ref[...]
        dn = prod * gv * one_sc
        gout = gin_ref[...] + r * (dn - nv * jnp.mean(dn * nv, axis=-1, keepdims=True))
        gout_ref[...] = gout

        @pl.when(first)
        def _():
            acc_ref[...] = jnp.zeros_like(acc_ref)

        dhn = prod * nv
        acc_ref[0:1, :] += jnp.sum(prod, axis=0, keepdims=True)
        acc_ref[1:2, :] += jnp.sum(dhn * gv, axis=0, keepdims=True)
        acc_ref[2:3, :] += jnp.sum(dhn * one_sc, axis=0, keepdims=True)
        if before:
            f_ref, gt_ref = tin[5:]
            tout[2][...] = ((coef * gt_ref[...]) * gout).astype(BF16)
            acc_ref[3:4, :] += coef * jnp.sum(gout * f_ref[...], axis=0, keepdims=True)

    row, vec = _row_tile(D)(tm), _one(D)(1)
    tiles_in = [(x, *row), (gin, *row), (g, *vec), (sc, *vec), (sh, *vec)]
    tiles_out = [(_sds((s, D), F32), *row), (_sds((8, D), F32), *_one(D)(8))]
    if before:
        tiles_in += [(before[0], *row), (before[1], *vec)]
        tiles_out.append((_sds((s, D), BF16), *row))
    return _mm(name, da, w, "NN", None, tm, D, da.shape[1], epilogue=epilogue, keep_b=True, after=after,
               tiles_in=tiles_in, tiles_out=tiles_out)


def _gate_tiles(proj, tm):
    return [(proj, (tm, COL), (lambda i, j, blk=blk: (i, blk))) for blk in (GA_BLK, GA_BLK + 1, GC_BLK, GC_BLK + 1)]


def _conv_branch_merge(z, wc, ya, proj, tm=1024):
    s = z.shape[0]
    tm = min(tm, s)

    def epilogue(prod, first, tin, tout):
        ya_ref, ga0, ga1, gc0, gc1 = tin
        tout[0][...] = prod.astype(BF16)
        for half, (ga, gc) in enumerate(((ga0, gc0), (ga1, gc1))):
            cols = slice(half * COL, (half + 1) * COL)
            tout[1][:, cols] = (_sigmoid(ga[...].astype(F32)) * ya_ref[:, cols].astype(F32)
                                + _sigmoid(gc[...].astype(F32)) * prod[:, cols]).astype(BF16)

    row = _row_tile(D)(tm)
    return _mm("mix_conv_branch", z, wc, "NN", None, tm, D, D, epilogue=epilogue,
               tiles_in=[(ya, *row)] + _gate_tiles(proj, tm),
               tiles_out=[(_sds((s, D), BF16), *row), (_sds((s, D), BF16), *row)])


def _d_merged_branches(dmix, wo, ya, yc, proj, tm=1024):
    s = dmix.shape[0]
    tm = min(tm, s)

    def epilogue(prod, first, tin, tout):
        ya_ref, yc_ref, ga0, ga1, gc0, gc1 = tin
        dya_ref, dyc_ref, dg_ref = tout
        for half, (ga, gc) in enumerate(((ga0, gc0), (ga1, gc1))):
            cols = slice(half * COL, (half + 1) * COL)
            dm = prod[:, cols]
            for y_ref, g_ref, dy_ref, off in ((ya_ref, ga, dya_ref, 0), (yc_ref, gc, dyc_ref, D)):
                sig = _sigmoid(g_ref[...].astype(F32))
                dms = dm * sig
                dy_ref[:, cols] = dms.astype(BF16)
                dg_ref[:, off + half * COL:off + (half + 1) * COL] = (
                    dms * y_ref[:, cols].astype(F32) * (1.0 - sig)).astype(BF16)

    row = _row_tile(D)(tm)
    return _mm("mix_d_merged", dmix, wo, "NT", None, tm, D, D, epilogue=epilogue,
               tiles_in=[(ya, *row), (yc, *row)] + _gate_tiles(proj, tm),
               tiles_out=[(_sds((s, D), BF16), *row), (_sds((s, D), BF16), *row),
                          (_sds((s, 2 * D), BF16), *_row_tile(2 * D)(tm))])


def _d_o_delta(dya, wa_t, o, tm=1024):
    s = dya.shape[0]
    tm = min(tm, s)

    def epilogue(prod, first, tin, tout):
        tout[0][...] = prod
        tout[1][...] = _heads(prod * tin[0][...].astype(F32), lambda ph, h: jnp.broadcast_to(
            jnp.sum(ph, axis=-1, keepdims=True), ph.shape))

    row = _row_tile(COL)(tm)
    return _mm("mix_d_o", dya, wa_t, "NN", None, tm, COL, D, epilogue=epilogue,
               tiles_in=[(o, *row)], tiles_out=[(_sds((s, COL), F32), *row), (_sds((s, COL), F32), *row)])


def _ffn_bwd(tag, df, x, gin, h, ab, sw, g, sc, sh, wgu, wd, before=None, tk_dw=2048):
    dwd = _mm(f"{tag}_dw_down", sw, df, "TN", BF16, HALF, D, tk_dw)
    dab = _d_hidden_swiglu(f"{tag}_d_hidden", df, wd, ab, after=[dwd])
    dwgu = _mm(f"{tag}_dw_gate_up", dab, h, "TN", BF16, HALF, D, tk_dw)
    res = _d_h_norm_bwd(f"{tag}_d_h", dab, wgu, x, gin, g, sc, sh, before=before, after=[dwgu], tm=512)
    return res, dwgu, dwd


def kernel(x, c, w_ada, b_ada, norm_ffn1, ffn1_w_gate, ffn1_w_up, ffn1_w_down, norm_mix, w_in, q_norm, k_norm, conv_w, w_attn_branch, w_conv_branch, w_out, norm_ffn2, ffn2_w_gate, ffn2_w_up, ffn2_w_down, loss_target, m_w_ada, m_b_ada, m_norm_ffn1, m_ffn1_w_gate, m_ffn1_w_up, m_ffn1_w_down, m_norm_mix, m_w_in, m_q_norm, m_k_norm, m_conv_w, m_w_attn_branch, m_w_conv_branch, m_w_out, m_norm_ffn2, m_ffn2_w_gate, m_ffn2_w_up, m_ffn2_w_down, v_w_ada, v_b_ada, v_norm_ffn1, v_ffn1_w_gate, v_ffn1_w_up, v_ffn1_w_down, v_norm_mix, v_w_in, v_q_norm, v_k_norm, v_conv_w, v_w_attn_branch, v_w_conv_branch, v_w_out, v_norm_ffn2, v_ffn2_w_gate, v_ffn2_w_up, v_ffn2_w_down):
    me = 4 * lax.axis_index("x") + 2 * lax.axis_index("y") + lax.axis_index("c")
    x0, target = x[0], loss_target[0]
    s = x0.shape[0]
    ada_cols = w_ada.shape[2]
    cw_cols = conv_w.shape[2]

    gathered = _small_allgather(
        "gather_c_conv", jnp.concatenate([c, conv_w[0].reshape(1, 3 * cw_cols)], axis=1))[:, 0]
    c_all = gathered[:, :D]
    cw = gathered[:, D:].reshape(N_DEV, 3, cw_cols).transpose(1, 0, 2).reshape(3, D)
    b_part = lax.dynamic_slice(b_ada, (0, me * ada_cols), (1, ada_cols))
    mod_part = _mod_part(c_all, w_ada[0], b_part)
    mod_all = _small_allgather("gather_mod", mod_part.reshape(1, N_DEV * ada_cols))
    mod = lax.dynamic_slice(mod_all.reshape(N_DEV, N_DEV, ada_cols), (0, me, 0), (N_DEV, 1, ada_cols))
    mod = mod.reshape(N_MOD, 1, D)
    sh1, sc1, gt1, sh2, sc2, gt2, sh3, sc3, gt3 = [mod[i] for i in range(N_MOD)]

    tb = lambda w: w[0].T.astype(BF16)
    nb = lambda w: w[0].astype(BF16)
    ffn1_shards = [tb(ffn1_w_gate), tb(ffn1_w_up), nb(ffn1_w_down)]
    ffn2_shards = [tb(ffn2_w_gate), tb(ffn2_w_up), nb(ffn2_w_down)]
    mix_shards = [tb(w_in), tb(w_attn_branch), nb(w_conv_branch), nb(w_out)]
    ffn_dst, ffn_base, ffn_jump, ffn_shapes = [0, 0, 1], [0, HALF, 0], [HALF, HALF, 0], [(2 * FF, D), (FF, D)]
    mix_dst, mix_base, mix_shapes = [0, 1, 2, 3], [0, 0, 0, 0], [(IN_W, D), (D, COL), (D, D), (D, D)]
    (wgu1,) = _run_plan_on_sequencer(
        "gather_ffn1_gate_up", _gather_plan(ffn1_shards[:2], ffn_dst[:2], ffn_base[:2], ffn_shapes[:1], ffn_jump[:2]), 1)
    (wd1,) = _run_plan_on_sequencer(
        "gather_ffn1_down", _gather_plan(ffn1_shards[2:], [0], [0], ffn_shapes[1:]), 8)
    win_t, wa_t, wc, wo = _run_plan_on_sequencer(
        "gather_mix_weights", _gather_plan(mix_shards, mix_dst, mix_base, mix_shapes), 2)
    wgu2, wd2 = _run_plan_on_sequencer(
        "gather_ffn2_weights", _gather_plan(ffn2_shards, ffn_dst, ffn_base, ffn_shapes, ffn_jump), 3)

    h1 = _normmod("ffn1_normmod", x0, norm_ffn1, sc1, sh1)
    ab1, s1 = _gate_up_swiglu("ffn1_gate_up", h1, wgu1)
    f1, x1, h2 = _out_residual("ffn1_down", s1, wd1, x0, gt1, 0.5, (norm_mix, sc2, sh2))
    proj = _mm("mix_in_proj", h2, win_t, "NT", BF16, 1024, IN_W // 4, D, n_outer=True)
    wqk = jnp.concatenate([jnp.tile(q_norm, (1, 12)), jnp.tile(k_norm, (1, 12))], axis=1)
    qkn = _qknorm(proj, wqk)
    group_out = [_attn_fwd(g, qkn, proj) for g in range(3)]
    o, lse = _attn_combine([go[0] for go in group_out], [go[1] for go in group_out])
    ya = _mm("mix_attn_branch", o, wa_t, "NT", BF16, 1024, 1024, COL)
    z = _conv_fwd(proj, cw)
    yc, merged = _conv_branch_merge(z, wc, ya, proj)
    mix, x2, h3 = _out_residual("mix_out_proj", merged, wo, x1, gt2, 1.0, (norm_ffn2, sc3, sh3), tm=1024, tk=D)
    ab3, s3 = _gate_up_swiglu("ffn2_gate_up", h3, wgu2)
    f3, g3, df3, acc_out = _out_loss("ffn2_down", s3, wd2, x2, gt3, 0.5, target)
    loss_part = jnp.sum(acc_out[1])

    ffn_rows = [sh_.shape[0] for sh_ in ffn1_shards]
    mix_rows = [sh_.shape[0] for sh_ in mix_shards]
    (g2, acc3, dmix), dwgu2, dwd2 = _ffn_bwd(
        "ffn2", df3, x2, g3, h3, ab3, s3, norm_ffn2, sc3, sh3, wgu2, wd2, before=(mix, gt2, 1.0))
    dya, dyc, dgates = _d_merged_branches(dmix, wo, ya, yc, proj)
    dwo = _mm("mix_dw_out", merged, dmix, "TN", BF16, 1024, 1024, 2048)
    dproj = _copy_columns("dproj_gates", dgates, jax.ShapeDtypeStruct((s, IN_W), BF16), GA_BLK)
    dwc = _mm("mix_dw_conv_branch", z, dyc, "TN", BF16, 1024, 1024, 2048)
    dz = _mm("mix_d_z", dyc, wc, "NT", F32, 1024, 1024, D)
    dproj, d_c, cw_acc = _conv_bwd(dz, proj, cw, dproj)
    dproj = _copy_columns("copy_d_c", d_c, dproj, C_BLK)
    dwa_t = _mm("mix_dw_attn_branch", dya, o, "TN", BF16, 1024, COL, 2048)
    do, delta = _d_o_delta(dya, wa_t, o)
    dqn = dkn = None
    for g in range(3):
        dqn, dkn, dproj = _attn_bwd(g, qkn, proj, do, lse, delta, dqn, dkn, dproj)
    dproj, wq_acc = _qknorm_bwd("qnorm_bwd", proj, dqn, wqk[:, :QKW // 2], dproj, 0)
    dproj, wk_acc = _qknorm_bwd("knorm_bwd", proj, dkn, wqk[:, QKW // 2:], dproj, 1)
    r_f2g, r_f2u, r_f2d, r_wa, r_wc, r_wo = _run_plan_on_sequencer(
        "scatter_ffn2_and_branch_grads",
        _scatter_plan([dwgu2, dwd2, dwa_t, dwc, dwo], [0, 0, 1, 2, 3, 4], [0, HALF, 0, 0, 0, 0],
                      ffn_rows + mix_rows[1:], [D, D, D, COL, D, D], [HALF, HALF, 0, 0, 0, 0]), 4)
    dwin_t = _mm("mix_dw_in", dproj, h2, "TN", BF16, IN_W // 4, COL, 2048)
    (r_win,) = _run_plan_on_sequencer(
        "scatter_w_in_grad", _scatter_plan([dwin_t], [0], [0], mix_rows[:1], [D]), 5)
    g1, acc2, df1 = _d_h_norm_bwd("mix_d_h", dproj, win_t, x1, g2, norm_mix, sc2, sh2, before=(f1, gt1, 0.5),
                                  after=[dwin_t])
    dwd1 = _mm("ffn1_dw_down", s1, df1, "TN", BF16, HALF, D, 2048)
    (r_f1d,) = _run_plan_on_sequencer(
        "scatter_ffn1_down_grad", _scatter_plan([dwd1], [0], [0], ffn_rows[2:], [D]), 6)
    dab1 = _d_hidden_swiglu("ffn1_d_hidden", df1, wd1, ab1, after=[dwd1, r_win])
    dwgu1 = _mm("ffn1_dw_gate_up", dab1, h1, "TN", BF16, HALF, D, 2048)
    r_f1g, r_f1u = _run_plan_on_sequencer(
        "scatter_ffn1_gate_up_grads",
        _scatter_plan([dwgu1], [0, 0], [0, HALF], ffn_rows[:2], [D, D], [HALF, HALF]), 7)
    g0, acc1 = _d_h_norm_bwd("ffn1_d_h", dab1, wgu1, x0, g1, norm_ffn1, sc1, sh1, after=[dwgu1, r_f1d], tm=512)

    dqw = jnp.sum(wq_acc[0].reshape(12, HD), axis=0)
    dkw = jnp.sum(wk_acc[0].reshape(12, HD), axis=0)
    small = jnp.concatenate([
        acc1[0], acc1[1], acc2[3], acc2[0], acc2[1], acc3[3], acc3[0], acc3[1], acc_out[0],
        acc1[2], acc2[2], acc3[2], dqw, dkw, cw_acc[0:3].reshape(3 * D),
        jnp.zeros((HD,), F32).at[0].set(loss_part)]).reshape(1, -1)
    small_all = _small_allgather("gather_small_grads", small)
    small_sum = _sum_rows("sum_small_grads", small_all)[0]
    n_mod = N_MOD * D
    g_b_ada = small_sum[:n_mod].reshape(1, n_mod)
    g_norm1, g_norm2, g_norm3 = [small_sum[n_mod + i * D:n_mod + (i + 1) * D].reshape(1, D) for i in range(3)]
    off = n_mod + 3 * D
    g_qn, g_kn = small_sum[off:off + HD].reshape(1, HD), small_sum[off + HD:off + 2 * HD].reshape(1, HD)
    g_cw_full = small_sum[off + 2 * HD:off + 2 * HD + 3 * D].reshape(3, D)
    loss = small_sum[off + 2 * HD + 3 * D]
    g_cw = lax.dynamic_slice(g_cw_full, (0, me * cw_cols), (3, cw_cols))
    dmod_part = lax.dynamic_slice(small_all[:, 0, :n_mod], (0, me * ada_cols), (N_DEV, ada_cols))
    g_w_ada = _w_ada_grad(c_all.T, dmod_part)

    as_rows = {"ffn1_w_gate", "ffn1_w_up", "w_in", "w_attn_branch", "ffn2_w_gate", "ffn2_w_up"}
    grad_list = [g_w_ada, g_b_ada, g_norm1, r_f1g, r_f1u, r_f1d, g_norm2, r_win,
                 g_qn, g_kn, g_cw, r_wa, r_wc, r_wo, g_norm3, r_f2g, r_f2u, r_f2d]
    weights = [w_ada, b_ada, norm_ffn1, ffn1_w_gate, ffn1_w_up, ffn1_w_down, norm_mix, w_in, q_norm, k_norm,
               conv_w, w_attn_branch, w_conv_branch, w_out, norm_ffn2, ffn2_w_gate, ffn2_w_up, ffn2_w_down]
    ms = [m_w_ada, m_b_ada, m_norm_ffn1, m_ffn1_w_gate, m_ffn1_w_up, m_ffn1_w_down, m_norm_mix, m_w_in, m_q_norm,
          m_k_norm, m_conv_w, m_w_attn_branch, m_w_conv_branch, m_w_out, m_norm_ffn2, m_ffn2_w_gate,
          m_ffn2_w_up, m_ffn2_w_down]
    vs = [v_w_ada, v_b_ada, v_norm_ffn1, v_ffn1_w_gate, v_ffn1_w_up, v_ffn1_w_down, v_norm_mix, v_w_in, v_q_norm,
          v_k_norm, v_conv_w, v_w_attn_branch, v_w_conv_branch, v_w_out, v_norm_ffn2, v_ffn2_w_gate,
          v_ffn2_w_up, v_ffn2_w_down]
    wnames = ["w_ada", "b_ada", "norm_ffn1", "ffn1_w_gate", "ffn1_w_up", "ffn1_w_down", "norm_mix", "w_in",
              "q_norm", "k_norm", "conv_w", "w_attn_branch", "w_conv_branch", "w_out", "norm_ffn2",
              "ffn2_w_gate", "ffn2_w_up", "ffn2_w_down"]
    small = [i for i, gr in enumerate(grad_list) if gr.ndim == 2 and gr.size <= 16384]
    flat = lambda a, i: a.reshape(-1, weights[i].shape[-1])
    small_res = dict(zip(small, _adamw_small(
        [flat(weights[i], i) for i in small], [flat(grad_list[i], i) for i in small],
        [flat(ms[i], i) for i in small], [flat(vs[i], i) for i in small])))
    grad_out, deltas, new_ms, new_vs = [], [], [], []
    for idx, (nm, w, gr, m_, v_) in enumerate(zip(wnames, weights, grad_list, ms, vs)):
        if idx in small_res:
            gr, dl, nm_, nv_ = [r.reshape(w.shape) for r in (gr, *small_res[idx])]
        elif nm in as_rows:
            res = _adamw(f"adamw_{nm}", w[0].T, gr, m_[0].T, v_[0].T)
            gr, dl, nm_, nv_ = [r.T[None] for r in res]
        else:
            two_d = (-1, w.shape[-1])
            res = _adamw(f"adamw_{nm}", w.reshape(two_d), gr if gr.ndim == 3 else gr.reshape(two_d),
                         m_.reshape(two_d), v_.reshape(two_d))
            gr, dl, nm_, nv_ = [r.reshape(w.shape) for r in res]
        grad_out.append(gr)
        deltas.append(dl)
        new_ms.append(nm_)
        new_vs.append(nv_)
    return (loss, g0[None], *grad_out, *deltas, *new_ms, *new_vs)
```

```python
import jax
import jax.numpy as jnp
from jax import lax
from jax.experimental import pallas as pl
from jax.experimental.pallas import tpu as pltpu
from jax.experimental.pallas import tpu_sc as plsc

F32 = jnp.float32
BF16 = jnp.bfloat16
MESH = pl.DeviceIdType.MESH

N_DEV = 8
D = 1024
FF = 2816
HD = 128
N_HEADS = 4
DILATIONS = (1, 4, 16)
BAND = 128
QKW = 2 * 3 * N_HEADS * HD
IN_W = 9728
COL = 512
V_BLK, U_BLK, B_BLK, C_BLK, GA_BLK, GC_BLK = 6, 9, 11, 13, 15, 17
EPS = 1e-6
N_MOD = 9
ADAM_LR, ADAM_B1, ADAM_B2, ADAM_EPS, ADAM_WD, ADAM_STEP = 0.001, 0.9, 0.999, 1e-08, 0.01, 10

NT_DIMS = (((1,), (1,)), ((), ()))
TN_DIMS = (((0,), (0,)), ((), ()))
NN_DIMS = (((1,), (0,)), ((), ()))


def _place():
    return lax.axis_index("x"), lax.axis_index("y"), lax.axis_index("c")


def _flip(coord, bit):
    return 1 - coord if bit else coord


def _params(*sem):
    return pltpu.CompilerParams(dimension_semantics=sem)


def _small_allgather(name, v):
    n = v.shape[-1]

    def body(v_ref, out_ref, send_sems, recv_sems):
        x, y, c = _place()
        me = 4 * x + 2 * y + c
        out_ref[me] = v_ref[...]
        copies = []
        for k in range(1, N_DEV):
            peer = (_flip(x, (k >> 2) & 1), _flip(y, (k >> 1) & 1), _flip(c, k & 1))
            cp = pltpu.make_async_remote_copy(
                src_ref=v_ref, dst_ref=out_ref.at[me], send_sem=send_sems.at[k - 1],
                recv_sem=recv_sems.at[k - 1], device_id=peer, device_id_type=MESH)
            cp.start()
            copies.append(cp)
        for cp in copies:
            cp.wait()

    return pl.pallas_call(
        body, name=name,
        out_shape=jax.ShapeDtypeStruct((N_DEV, 1, n), F32),
        in_specs=[pl.BlockSpec(memory_space=pltpu.VMEM)],
        out_specs=pl.BlockSpec(memory_space=pltpu.VMEM),
        scratch_shapes=[pltpu.SemaphoreType.DMA((N_DEV - 1,)), pltpu.SemaphoreType.DMA((N_DEV - 1,))],
    )(v)


class _Plan:
    def __init__(self, operands, out_shapes, sems, phases):
        self.operands, self.out_shapes, self.sems, self.phases = operands, out_shapes, sems, phases


def _slab_start(base, rows, jump, idx):
    return pl.multiple_of(base + idx * rows + (idx // 4) * jump, 16)


def _gather_plan(shards, dst_of, base_of, dst_shapes, jump_of=None):
    n = len(shards)
    rows = [s.shape[0] for s in shards]
    jump_of = jump_of or [0] * n

    def phases(srcs, dsts, sems):
        send_sems, recv_sems, local_sems = sems
        x, y, c = _place()
        me, sibling = (x, y, c), (x, y, 1 - c)
        chips = [(1 - x, y), (x, 1 - y), (1 - x, 1 - y)]

        def slab(i, px, py, pc):
            start = _slab_start(base_of[i], rows[i], jump_of[i], 4 * px + 2 * py + pc)
            return dsts[dst_of[i]].at[pl.ds(start, rows[i])]

        def copy(i, k, block, to, src=None):
            return pltpu.make_async_remote_copy(
                src_ref=slab(i, *block) if src is None else src, dst_ref=slab(i, *block),
                send_sem=send_sems.at[i, k], recv_sem=recv_sems.at[i, k],
                device_id=to, device_id_type=MESH)

        def mine():
            return [pltpu.make_async_copy(srcs[i], slab(i, *me), local_sems.at[i]) for i in range(n)]

        def first():
            out = []
            for i in range(n):
                out.append(copy(i, 0, me, sibling, src=srcs[i]))
                out += [copy(i, 1 + j, me, (*chip, c), src=srcs[i]) for j, chip in enumerate(chips)]
            return out

        def passed():
            return [(copy(i, 1 + j, (*chip, c), me), copy(i, 4 + j, (*chip, c), sibling))
                    for j, chip in enumerate(chips) for i in range(n)]

        def start():
            for cp in mine() + first():
                cp.start()

        def middle():
            for landed, onward in passed():
                landed.wait_recv()
                onward.start()

        def finish():
            for i in range(n):
                copy(i, 0, sibling, me).wait_recv()
                for j, chip in enumerate(chips):
                    copy(i, 4 + j, (*chip, 1 - c), me).wait_recv()
            for cp in first() + [onward for _, onward in passed()]:
                cp.wait_send()
            for cp in mine():
                cp.wait()

        return start, middle, finish

    sems = [pltpu.SemaphoreType.DMA((n, 7)), pltpu.SemaphoreType.DMA((n, 7)), pltpu.SemaphoreType.DMA((n,))]
    return _Plan(list(shards), [jax.ShapeDtypeStruct(s, BF16) for s in dst_shapes], sems, phases)


def _scatter_plan(grads, src_of, base_of, rows, cols, jump_of=None):
    n = len(rows)
    jump_of = jump_of or [0] * n

    def phases(srcs, recvs, sems):
        send_sems, recv_sems, local_sems = sems
        x, y, c = _place()
        me = 4 * x + 2 * y + c

        def slab(i, idx):
            start = _slab_start(base_of[i], rows[i], jump_of[i], idx)
            return srcs[src_of[i]].at[pl.ds(start, rows[i])]

        def copies():
            out = [pltpu.make_async_copy(slab(i, me), recvs[i].at[me], local_sems.at[i]) for i in range(n)]
            for k in range(1, N_DEV):
                px, py, pc = _flip(x, (k >> 2) & 1), _flip(y, (k >> 1) & 1), _flip(c, k & 1)
                out += [pltpu.make_async_remote_copy(
                    src_ref=slab(i, 4 * px + 2 * py + pc), dst_ref=recvs[i].at[me],
                    send_sem=send_sems.at[i, k - 1], recv_sem=recv_sems.at[i, k - 1],
                    device_id=(px, py, pc), device_id_type=MESH) for i in range(n)]
            return out

        def start():
            for cp in copies():
                cp.start()

        def finish():
            for cp in copies():
                cp.wait()

        return start, None, finish

    sems = [pltpu.SemaphoreType.DMA((n, 7)), pltpu.SemaphoreType.DMA((n, 7)), pltpu.SemaphoreType.DMA((n,))]
    out_shapes = [jax.ShapeDtypeStruct((N_DEV, rows[i], cols[i]), BF16) for i in range(n)]
    return _Plan(list(grads), out_shapes, sems, phases)


def _run_plan_on_sequencer(name, plan, collective_id):
    src_refs = [jax.new_ref(a, memory_space=pltpu.MemorySpace.HBM) for a in plan.operands]
    dst_refs = [jax.empty_ref(s, memory_space=pltpu.MemorySpace.HBM) for s in plan.out_shapes]

    @pl.kernel(mesh=plsc.ScalarSubcoreMesh(axis_name="sequencer", num_cores=1), name=name,
               scratch_types=tuple(plan.sems),
               compiler_params=pltpu.CompilerParams(collective_id=collective_id))
    def launch(*sems):
        x, y, c = _place()
        barrier = pltpu.get_barrier_semaphore()
        for k in range(1, N_DEV):
            peer = (_flip(x, (k >> 2) & 1), _flip(y, (k >> 1) & 1), _flip(c, k & 1))
            pl.semaphore_signal(barrier, inc=1, device_id=peer, device_id_type=MESH)
        pl.semaphore_wait(barrier, N_DEV - 1)
        for phase in plan.phases(src_refs, dst_refs, sems):
            if phase is not None:
                phase()

    launch()
    return [r[...] for r in dst_refs]


def _mm(name, a, b, mode, out_dtype, tm, tn, tk, *, tiles_in=(), tiles_out=(), epilogue=None,
        n_outer=False, keep_b=False, col_chunks=None, after=()):
    if mode == "TN":
        kk, m = a.shape
    else:
        m, kk = a.shape
    n = b.shape[0] if mode == "NT" else b.shape[1]
    tm, tn, tk = min(tm, m), min(tn, n), min(tk, kk)
    assert m % tm == 0 and n % tn == 0 and kk % tk == 0, (name, m, n, kk, tm, tn, tk)
    ni, nj, nk = m // tm, n // tn, kk // tk
    dims = {"NN": NN_DIMS, "NT": NT_DIMS, "TN": TN_DIMS}[mode]
    if epilogue is None:
        tiles_out = [(jax.ShapeDtypeStruct((m, n), out_dtype), (tm, tn), lambda i, j: (i, j))]
    n_tin, n_tout = len(tiles_in), len(tiles_out)
    n_acc = 1 if nk > 1 else 0
    n_after = len(after)
    assert not keep_b or (nk == 1 and nj == 1)
    assert not col_chunks or (epilogue is not None and nk == 1 and mode != "TN")
    ij = (lambda p, q: (q, p)) if n_outer else (lambda p, q: (p, q))
    inner = ni if n_outer else nj

    def body(a_ref, b_ref, *rest):
        tin = rest[:n_tin]
        tout = rest[n_tin + n_after:n_tin + n_after + n_tout]
        scratch = rest[n_tin + n_after + n_tout:]
        k = pl.program_id(2)
        visit = pl.program_id(0) * inner + pl.program_id(1)
        if keep_b:
            b_kept, b_sem = scratch[n_acc:n_acc + 2]

            @pl.when((visit == 0) & (k == 0))
            def _():
                cp = pltpu.make_async_copy(b_ref, b_kept, b_sem)
                cp.start()
                cp.wait()

            b_ref = b_kept

        def store(prod, c=0, cols=()):
            if epilogue is None:
                tout[0][...] = prod.astype(out_dtype)
            else:
                epilogue(prod, jnp.logical_and(visit == 0, c == 0), tin, tout, *cols)

        if col_chunks:
            for c, (c0, cw) in enumerate(col_chunks):
                b_part = b_ref[pl.ds(c0, cw), :] if mode == "NT" else b_ref[:, pl.ds(c0, cw)]
                store(lax.dot_general(a_ref[...], b_part, dims, preferred_element_type=F32), c, ((c0, cw),))
        else:
            part = lax.dot_general(a_ref[...], b_ref[...], dims, preferred_element_type=F32)
            if nk == 1:
                store(part)
            else:
                acc_ref = scratch[0]

                @pl.when(k == 0)
                def _():
                    acc_ref[...] = part

                @pl.when((k > 0) & (k < nk - 1))
                def _():
                    acc_ref[...] += part

                @pl.when(k == nk - 1)
                def _():
                    store(acc_ref[...] + part)

    def spec(shape, fn):
        return pl.BlockSpec(shape, lambda p, q, k: fn(*ij(p, q)))

    a_spec = (pl.BlockSpec((tk, tm), lambda p, q, k: (k, ij(p, q)[0])) if mode == "TN"
              else pl.BlockSpec((tm, tk), lambda p, q, k: (ij(p, q)[0], k)))
    if keep_b:
        b_spec = pl.BlockSpec(memory_space=pl.ANY)
    elif mode == "NT":
        b_spec = pl.BlockSpec((tn, tk), lambda p, q, k: (ij(p, q)[1], k))
    else:
        b_spec = pl.BlockSpec((tk, tn), lambda p, q, k: (k, ij(p, q)[1]))
    sequential = epilogue or keep_b
    out = pl.pallas_call(
        body, name=name, grid=(nj, ni, nk) if n_outer else (ni, nj, nk),
        out_shape=[t[0] for t in tiles_out],
        in_specs=([a_spec, b_spec] + [spec(t[1], t[2]) for t in tiles_in]
                  + [pl.BlockSpec(memory_space=pl.ANY)] * n_after),
        out_specs=[spec(t[1], t[2]) for t in tiles_out],
        scratch_shapes=([pltpu.VMEM((tm, tn), F32)] * n_acc
                        + ([pltpu.VMEM(b.shape, b.dtype), pltpu.SemaphoreType.DMA] if keep_b else [])),
        compiler_params=(_params("arbitrary", "arbitrary", "arbitrary") if sequential
                         else _params("parallel", "parallel", "arbitrary")),
    )(a, b, *[t[0] for t in tiles_in], *after)
    return out if epilogue else out[0]


def _row(tm, w, off=0):
    return pl.BlockSpec((tm, w), lambda i: (i, off))


def _vec(w):
    return pl.BlockSpec((1, w), lambda i: (0, 0))


def _sigmoid(x):
    return 0.5 * jnp.tanh(0.5 * x) + 0.5


def _normmod(name, x, g, sc, sh, tm=1024):
    s = x.shape[0]

    def body(x_ref, g_ref, sc_ref, sh_ref, h_ref):
        xv = x_ref[...]
        r = lax.rsqrt(jnp.mean(xv * xv, axis=-1, keepdims=True) + EPS)
        h_ref[...] = ((xv * r) * g_ref[...] * (1.0 + sc_ref[...]) + sh_ref[...]).astype(BF16)

    return pl.pallas_call(
        body, name=name, grid=(s // tm,),
        out_shape=jax.ShapeDtypeStruct((s, D), BF16),
        in_specs=[_row(tm, D), _vec(D), _vec(D), _vec(D)], out_specs=_row(tm, D),
        compiler_params=_params("parallel"),
    )(x, g, sc, sh)


def _heads(x, fn):
    return jnp.concatenate([fn(x[:, h * HD:(h + 1) * HD], h) for h in range(x.shape[1] // HD)], axis=1)


def _qknorm(proj, wqk, tm=512):
    s = proj.shape[0]

    def body(p_ref, w_ref, o_ref):
        pv = p_ref[...].astype(F32)
        wv = w_ref[...]

        def one(qh, h):
            r = lax.rsqrt(jnp.mean(qh * qh, axis=-1, keepdims=True) + EPS)
            return (qh * r) * wv[:, h * HD:(h + 1) * HD]

        o_ref[...] = _heads(pv, one).astype(BF16)

    return pl.pallas_call(
        body, name="qknorm", grid=(s // tm,),
        out_shape=jax.ShapeDtypeStruct((s, QKW), BF16),
        in_specs=[pl.BlockSpec((tm, QKW), lambda i: (i, 0)), pl.BlockSpec((1, QKW), lambda i: (0, 0))],
        out_specs=pl.BlockSpec((tm, QKW), lambda i: (i, 0)),
        compiler_params=_params("parallel"),
    )(proj, wqk)


def _qknorm_bwd(name, proj, dn, w, dproj, blk0, tm=512):
    s, width = dn.shape

    def body(p_ref, d_ref, w_ref, _, o_ref, acc_ref):
        pv = p_ref[...].astype(F32)
        dv = d_ref[...]
        wv = w_ref[...]
        sums = []

        def one(qh, h):
            dn = dv[:, h * HD:(h + 1) * HD]
            r = lax.rsqrt(jnp.mean(qh * qh, axis=-1, keepdims=True) + EPS)
            nh = qh * r
            sums.append(jnp.sum(dn * nh, axis=0, keepdims=True))
            dnw = dn * wv[:, h * HD:(h + 1) * HD]
            return r * (dnw - nh * jnp.mean(dnw * nh, axis=-1, keepdims=True))

        o_ref[...] = _heads(pv, one).astype(BF16)

        @pl.when(pl.program_id(0) == 0)
        def _():
            acc_ref[...] = jnp.zeros_like(acc_ref)

        acc_ref[0:1, :] += jnp.concatenate(sums, axis=1)

    return pl.pallas_call(
        body, name=name, grid=(s // tm,),
        out_shape=[jax.ShapeDtypeStruct((s, IN_W), BF16), jax.ShapeDtypeStruct((8, width), F32)],
        in_specs=[pl.BlockSpec((tm, width), lambda i: (i, blk0)), pl.BlockSpec((tm, width), lambda i: (i, 0)),
                  pl.BlockSpec((1, width), lambda i: (0, 0)), pl.BlockSpec(memory_space=pl.ANY)],
        out_specs=[pl.BlockSpec((tm, width), lambda i: (i, blk0)), pl.BlockSpec((8, width), lambda i: (0, 0))],
        input_output_aliases={3: 0},
        compiler_params=_params("arbitrary"),
    )(proj, dn, w, dproj)


def _attn_shapes(s, g, sub_block):
    d = DILATIONS[g]
    tb = min(s, max(2048, 256 * d))
    sb = min(sub_block, tb // d)
    pb = BAND * d
    assert s % tb == 0 and tb % pb == 0 and (tb // d) % sb == 0 and sb % BAND == 0
    return d, tb, sb, pb


def _lanes(x, width):
    return jnp.concatenate([x] * (width // HD), axis=1)


def _every(start, size, d):
    return pl.ds(start, size, stride=d) if d > 1 else pl.ds(start, size)


def _attn_specs(g, tb, pb, s, ahead):
    ratio = tb // pb
    if ahead:
        nbr = lambda n: jnp.minimum((n + 1) * ratio, s // pb - 1)
    else:
        nbr = lambda n: jnp.maximum(n * ratio - 1, 0)
    cur = lambda base: pl.BlockSpec((tb, HD), lambda h, n: (n, base + g * N_HEADS + h))
    side = lambda base: pl.BlockSpec((pb, HD), lambda h, n: (nbr(n), base + g * N_HEADS + h))
    tok = pl.BlockSpec((tb, HD), lambda h, n: (n, h))
    tok_side = pl.BlockSpec((pb, HD), lambda h, n: (nbr(n), h))
    return cur, side, tok, tok_side


Q_COL, K_COL, V_COL = 0, 12, 24


def _attn_fwd(g, qkn, proj):
    s = qkn.shape[0]
    d, tb, sb, pb = _attn_shapes(s, g, 128)
    ft = F32 if d > 1 else BF16
    nj = tb // d // sb
    scale = HD ** -0.5

    def body(q_ref, kc_ref, kp_ref, vc_ref, vp_ref, o_ref, lse_ref, qf, kf, vf):
        n = pl.program_id(1)
        qf[...] = q_ref[...].astype(ft)
        kf[0:pb] = kp_ref[...].astype(ft)
        kf[pb:] = kc_ref[...].astype(ft)
        vf[0:pb] = vp_ref[...].astype(ft)
        vf[pb:] = vc_ref[...].astype(ft)
        for r in range(d):
            for j in range(nj):
                at = j * sb * d + r
                q = qf[_every(at, sb, d), :].astype(BF16)
                k = kf[_every(at, sb + BAND, d), :].astype(BF16)
                v = vf[_every(at, sb + BAND, d), :].astype(BF16)
                sc = lax.dot_general(q, k, NT_DIMS, preferred_element_type=F32) * scale
                qi = lax.broadcasted_iota(jnp.int32, sc.shape, 0)
                kj = lax.broadcasted_iota(jnp.int32, sc.shape, 1)
                valid = (kj >= qi) & (kj <= qi + BAND)
                if j == 0:
                    valid = valid & ((kj >= BAND) | (n > 0))
                sc = jnp.where(valid, sc, -1e30)
                m = jnp.max(sc, axis=-1, keepdims=True)
                p = jnp.exp(sc - m)
                l = jnp.sum(p, axis=-1, keepdims=True)
                o = lax.dot_general(p.astype(BF16), v, NN_DIMS, preferred_element_type=F32)
                o_ref[_every(at, sb, d), :] = o / l
                lse_ref[_every(at, sb, d), :] = jnp.broadcast_to(m + jnp.log(l), (sb, HD))

    cur, side, tok, _ = _attn_specs(g, tb, pb, s, ahead=False)
    return pl.pallas_call(
        body, name=f"attn_fwd_g{g}", grid=(N_HEADS, s // tb),
        out_shape=[jax.ShapeDtypeStruct((s, COL), F32)] * 2,
        in_specs=[cur(Q_COL), cur(K_COL), side(K_COL), cur(V_COL), side(V_COL)],
        out_specs=[tok, tok],
        scratch_shapes=[pltpu.VMEM((tb, HD), ft), pltpu.VMEM((tb + pb, HD), ft),
                        pltpu.VMEM((tb + pb, HD), ft)],
        compiler_params=_params("parallel", "arbitrary"),
    )(qkn, qkn, qkn, proj, proj)


def _attn_combine(os_, lses, tm=1024):
    s = os_[0].shape[0]

    def body(o0, o1, o2, l0, l1, l2, o_ref, lse_ref):
        a, b, c = l0[...], l1[...], l2[...]
        m = jnp.maximum(jnp.maximum(a, b), c)
        ea, eb, ec = jnp.exp(a - m), jnp.exp(b - m), jnp.exp(c - m)
        tot = ea + eb + ec
        o_ref[...] = ((ea * o0[...] + eb * o1[...] + ec * o2[...]) / tot).astype(BF16)
        lse_ref[...] = m + jnp.log(tot)

    return pl.pallas_call(
        body, name="attn_combine", grid=(s // tm,),
        out_shape=[jax.ShapeDtypeStruct((s, COL), BF16), jax.ShapeDtypeStruct((s, COL), F32)],
        in_specs=[_row(tm, COL)] * 6, out_specs=[_row(tm, COL)] * 2,
        compiler_params=_params("parallel"),
    )(*os_, *lses)


def _attn_bwd(g, qkn, proj, do, lse, delta, dqn, dkn, dproj):
    s = qkn.shape[0]
    d, tb, sb, pb = _attn_shapes(s, g, 256)
    ft = F32 if d > 1 else BF16
    nj = tb // d // sb
    nt = s // tb
    scale = HD ** -0.5
    chained = dqn is not None

    def body(k_ref, v_ref, qc_ref, qn_ref, doc_ref, don_ref, lc_ref, ln_ref, dc_ref, dn_ref, *rest):
        dq_ref, dk_ref, dv_ref, kf, vf, qf, dvf, later = rest[-8:]
        n = pl.program_id(1)
        kf[...] = k_ref[...].astype(ft)
        vf[...] = v_ref[...].astype(ft)
        qf[0:tb] = qc_ref[...].astype(ft)
        qf[tb:] = qn_ref[...].astype(ft)

        @pl.when(n == 0)
        def _():
            later[...] = jnp.zeros_like(later)

        def window(c_ref, n_ref, r, j):
            at = j * sb * d + r
            if j < nj - 1:
                return c_ref[_every(at, sb + BAND, d), :]
            return jnp.concatenate([c_ref[_every(at, sb, d), :], n_ref[_every(r, BAND, d), :]], axis=0)

        for r in range(d):
            tail = later[r]
            for j in range(nj):
                at = j * sb * d + r
                rows = _every(at, sb, d)
                k = kf[rows, :].astype(BF16)
                v = vf[rows, :].astype(BF16)
                q = qf[_every(at, sb + BAND, d), :].astype(BF16)
                dov = window(doc_ref, don_ref, r, j).astype(BF16)
                sc = lax.dot_general(q, k, NT_DIMS, preferred_element_type=F32) * scale
                qi = lax.broadcasted_iota(jnp.int32, sc.shape, 0)
                kj = lax.broadcasted_iota(jnp.int32, sc.shape, 1)
                valid = (qi >= kj) & (qi <= kj + BAND)
                if j == nj - 1:
                    valid = valid & ((qi < sb) | (n < nt - 1))
                p = jnp.exp(jnp.where(valid, sc - _lanes(window(lc_ref, ln_ref, r, j), sb), -1e30))
                dp = lax.dot_general(dov, v, NT_DIMS, preferred_element_type=F32)
                ds = (p * (dp - _lanes(window(dc_ref, dn_ref, r, j), sb)) * scale).astype(BF16)
                dvf[rows, :] = lax.dot_general(p.astype(BF16), dov, TN_DIMS, preferred_element_type=F32)
                dk_ref[rows, :] = lax.dot_general(ds, q, TN_DIMS, preferred_element_type=F32)
                dqw = lax.dot_general(ds, k, NN_DIMS, preferred_element_type=F32)
                first = dqw[:BAND] + tail
                dq_ref[rows, :] = first if sb == BAND else jnp.concatenate([first, dqw[BAND:sb]], axis=0)
                tail = dqw[sb:]
            later[r] = tail
        dv_ref[...] = dvf[...].astype(BF16)

    cur, side, tok, tok_side = _attn_specs(g, tb, pb, s, ahead=True)
    anyspec = pl.BlockSpec(memory_space=pl.ANY)
    n_heads_cols = 3 * N_HEADS * HD
    return pl.pallas_call(
        body, name=f"attn_bwd_g{g}", grid=(N_HEADS, nt),
        out_shape=[jax.ShapeDtypeStruct((s, n_heads_cols), F32), jax.ShapeDtypeStruct((s, n_heads_cols), F32),
                   jax.ShapeDtypeStruct((s, IN_W), BF16)],
        in_specs=[cur(K_COL), cur(V_COL), cur(Q_COL), side(Q_COL), tok, tok_side, tok, tok_side,
                  tok, tok_side] + ([anyspec, anyspec] if chained else []) + [anyspec],
        out_specs=[cur(0), cur(0), cur(V_COL)],
        input_output_aliases={10: 0, 11: 1, 12: 2} if chained else {10: 2},
        scratch_shapes=[pltpu.VMEM((tb, HD), ft), pltpu.VMEM((tb, HD), ft),
                        pltpu.VMEM((tb + pb, HD), ft), pltpu.VMEM((tb, HD), F32),
                        pltpu.VMEM((d, BAND, HD), F32)],
        compiler_params=_params("arbitrary", "arbitrary"),
    )(qkn, proj, qkn, qkn, do, do, lse, lse, delta, delta, *([dqn, dkn] if chained else []), dproj)


def _shift_down(x, before, k):
    rolled = pltpu.roll(x, k, 0)
    head = jnp.where(lax.broadcasted_iota(jnp.int32, before.shape, 0) < k, pltpu.roll(before, k, 0), rolled[:8])
    return jnp.concatenate([head, rolled[8:]], axis=0)


def _shift_up(x, after, k):
    rows = x.shape[0]
    rolled = pltpu.roll(x, rows - k, 0)
    tail = jnp.where(lax.broadcasted_iota(jnp.int32, after.shape, 0) >= 8 - k,
                     pltpu.roll(after, 8 - k, 0), rolled[rows - 8:])
    return jnp.concatenate([rolled[:rows - 8], tail], axis=0)


def _conv_fwd(proj, cw, tm=1024):
    s = proj.shape[0]
    r16 = tm // 16

    def body(u_ref, b_ref, c_ref, up_ref, cp_ref, w_ref, z_ref):
        i = pl.program_id(1)
        xc = c_ref[...].astype(F32) * u_ref[...].astype(F32)
        xp = jnp.where(i > 0, cp_ref[8:16, :].astype(F32) * up_ref[8:16, :].astype(F32), 0.0)
        w = w_ref[...]
        conv = _shift_down(xc, xp, 2) * w[0:1] + _shift_down(xc, xp, 1) * w[1:2] + xc * w[2:3]
        z_ref[...] = (b_ref[...].astype(F32) * conv).astype(BF16)

    tile = lambda blk: pl.BlockSpec((tm, COL), lambda j, i: (i, blk + j))
    before = lambda blk: pl.BlockSpec((16, COL), lambda j, i: (jnp.maximum(i * r16 - 1, 0), blk + j))
    return pl.pallas_call(
        body, name="conv_fwd", grid=(D // COL, s // tm),
        out_shape=jax.ShapeDtypeStruct((s, D), BF16),
        in_specs=[tile(U_BLK), tile(B_BLK), tile(C_BLK), before(U_BLK), before(C_BLK),
                  pl.BlockSpec((3, COL), lambda j, i: (0, j))],
        out_specs=pl.BlockSpec((tm, COL), lambda j, i: (i, j)),
        compiler_params=_params("parallel", "parallel"),
    )(proj, proj, proj, proj, proj, cw)


def _conv_bwd(dz, proj, cw, dproj, tm=2048):
    s = proj.shape[0]
    r16 = tm // 16
    nrow = s // tm

    def body(dz_ref, u_ref, b_ref, c_ref, up_ref, cp_ref, dzn_ref, bn_ref, w_ref, _, o_ref, dc_ref, acc_ref):
        piece, i = pl.program_id(1), pl.program_id(2)
        u, c = u_ref[...].astype(F32), c_ref[...].astype(F32)
        bv = b_ref[...].astype(F32)
        dzv = dz_ref[...]
        w = w_ref[...]

        @pl.when((piece == 0) & (i == 0))
        def _():
            acc_ref[...] = jnp.zeros_like(acc_ref)

        @pl.when(piece == 0)
        def _():
            xc = c * u
            xp = jnp.where(i > 0, cp_ref[8:16, :].astype(F32) * up_ref[8:16, :].astype(F32), 0.0)
            x2, x1 = _shift_down(xc, xp, 2), _shift_down(xc, xp, 1)
            o_ref[...] = (dzv * (x2 * w[0:1] + x1 * w[1:2] + xc * w[2:3])).astype(BF16)
            dc_ref[...] = jnp.zeros_like(dc_ref)
            dconv = dzv * bv
            acc_ref[0:1, :] += jnp.sum(dconv * x2, axis=0, keepdims=True)
            acc_ref[1:2, :] += jnp.sum(dconv * x1, axis=0, keepdims=True)
            acc_ref[2:3, :] += jnp.sum(dconv * xc, axis=0, keepdims=True)

        @pl.when(piece == 1)
        def _():
            dconv = dzv * bv
            dn = jnp.where(i < nrow - 1, dzn_ref[...] * bn_ref[0:8, :].astype(F32), 0.0)
            dxc = dconv * w[2:3] + _shift_up(dconv, dn, 1) * w[1:2] + _shift_up(dconv, dn, 2) * w[0:1]
            o_ref[...] = (dxc * c).astype(BF16)
            dc_ref[...] = (dxc * u).astype(BF16)

    tile = lambda blk: pl.BlockSpec((tm, COL), lambda j, p, i: (i, blk + j))
    before = lambda blk: pl.BlockSpec((16, COL), lambda j, p, i: (jnp.maximum(i * r16 - 1, 0), blk + j))
    after = lambda rows, blk: pl.BlockSpec(
        (rows, COL), lambda j, p, i: (jnp.minimum((i + 1) * (tm // rows), s // rows - 1), blk + j))
    return pl.pallas_call(
        body, name="conv_bwd", grid=(D // COL, 2, nrow),
        out_shape=[jax.ShapeDtypeStruct((s, IN_W), BF16), jax.ShapeDtypeStruct((s + tm, D), BF16),
                   jax.ShapeDtypeStruct((8, D), F32)],
        in_specs=[tile(0), tile(U_BLK), tile(B_BLK), tile(C_BLK), before(U_BLK), before(C_BLK),
                  after(8, 0), after(16, B_BLK), pl.BlockSpec((3, COL), lambda j, p, i: (0, j)),
                  pl.BlockSpec(memory_space=pl.ANY)],
        out_specs=[pl.BlockSpec((tm, COL), lambda j, p, i: (i, jnp.where(p == 0, B_BLK, U_BLK) + j)),
                   pl.BlockSpec((tm, COL), lambda j, p, i: (jnp.where(p == 0, nrow, i), j)),
                   pl.BlockSpec((8, COL), lambda j, p, i: (0, j))],
        input_output_aliases={9: 0},
        compiler_params=_params("arbitrary", "arbitrary", "arbitrary"),
    )(dz, proj, proj, proj, proj, proj, dz, proj, cw, dproj)


def _copy_columns(name, src, dst, blk0, tm=2048):
    s, w = dst.shape[0], src.shape[1]
    fresh = isinstance(dst, jax.ShapeDtypeStruct)

    def body(x_ref, *rest):
        rest[-1][...] = x_ref[...]

    return pl.pallas_call(
        body, name=name, grid=(w // COL, s // tm),
        out_shape=jax.ShapeDtypeStruct(dst.shape, dst.dtype),
        in_specs=[pl.BlockSpec((tm, COL), lambda j, i: (i, j))] + ([] if fresh else [pl.BlockSpec(memory_space=pl.ANY)]),
        out_specs=pl.BlockSpec((tm, COL), lambda j, i: (i, blk0 + j)),
        input_output_aliases={} if fresh else {1: 0},
        compiler_params=_params("parallel", "parallel"),
    )(src, *([] if fresh else [dst]))


def _mod_part(c_all, w_ada, b_part):
    def body(c_ref, w_ref, b_ref, o_ref):
        cv = c_ref[...]
        act = cv * _sigmoid(cv)
        o_ref[...] = jnp.dot(act, w_ref[...], preferred_element_type=F32,
                             precision=lax.Precision.HIGHEST) + b_ref[...]

    return pl.pallas_call(
        body, name="mod_part", out_shape=jax.ShapeDtypeStruct((N_DEV, w_ada.shape[1]), F32),
    )(c_all, w_ada, b_part)


def _w_ada_grad(c_all_t, dmod_part):
    def body(c_ref, d_ref, o_ref):
        cv = c_ref[...]
        act = cv * _sigmoid(cv)
        dv = d_ref[...]
        acc = act[:, 0:1] * dv[0:1, :]
        for b in range(1, N_DEV):
            acc = acc + act[:, b:b + 1] * dv[b:b + 1, :]
        o_ref[...] = acc

    return pl.pallas_call(
        body, name="w_ada_grad", out_shape=jax.ShapeDtypeStruct((D, dmod_part.shape[1]), F32),
    )(c_all_t, dmod_part)


def _sum_rows(name, v):
    def body(v_ref, o_ref):
        acc = v_ref[0]
        for k in range(1, N_DEV):
            acc = acc + v_ref[k]
        o_ref[...] = acc

    return pl.pallas_call(body, name=name, out_shape=jax.ShapeDtypeStruct(v.shape[1:], F32))(v)


def _adamw(name, w, g, m, v):
    rows, cols = w.shape
    limit = max(16, (1 << 20) // (4 * cols))
    tr = rows if rows <= limit else next((t for t in range(limit - limit % 16, 15, -16) if rows % t == 0), rows)
    c1 = 1.0 - ADAM_B1 ** ADAM_STEP
    c2 = 1.0 - ADAM_B2 ** ADAM_STEP
    parts = g.ndim == 3

    def body(w_ref, g_ref, m_ref, v_ref, go_ref, d_ref, nm_ref, nv_ref):
        if parts:
            gv = g_ref[0].astype(F32)
            for k in range(1, N_DEV):
                gv = gv + g_ref[k].astype(F32)
        else:
            gv = g_ref[...]
        go_ref[...] = gv
        nm = ADAM_B1 * m_ref[...] + (1.0 - ADAM_B1) * gv
        nv = ADAM_B2 * v_ref[...] + (1.0 - ADAM_B2) * (gv * gv)
        nm_ref[...] = nm
        nv_ref[...] = nv
        d_ref[...] = -ADAM_LR * ((nm / c1) / (jnp.sqrt(nv / c2) + ADAM_EPS) + ADAM_WD * w_ref[...])

    spec = pl.BlockSpec((tr, cols), lambda i: (i, 0))
    g_spec = pl.BlockSpec((N_DEV, tr, cols), lambda i: (0, i, 0)) if parts else spec
    return pl.pallas_call(
        body, name=name, grid=(rows // tr,),
        out_shape=[jax.ShapeDtypeStruct((rows, cols), F32)] * 4,
        in_specs=[spec, g_spec, spec, spec], out_specs=[spec] * 4,
        compiler_params=_params("parallel"),
    )(w, g, m, v)


def _adamw_small(ws, gs, ms, vs):
    n = len(ws)
    c1 = 1.0 - ADAM_B1 ** ADAM_STEP
    c2 = 1.0 - ADAM_B2 ** ADAM_STEP

    def body(*refs):
        for i in range(n):
            w_ref, g_ref, m_ref, v_ref = refs[i], refs[n + i], refs[2 * n + i], refs[3 * n + i]
            d_ref, nm_ref, nv_ref = refs[4 * n + 3 * i:4 * n + 3 * i + 3]
            gv = g_ref[...]
            nm = ADAM_B1 * m_ref[...] + (1.0 - ADAM_B1) * gv
            nv = ADAM_B2 * v_ref[...] + (1.0 - ADAM_B2) * (gv * gv)
            nm_ref[...] = nm
            nv_ref[...] = nv
            d_ref[...] = -ADAM_LR * ((nm / c1) / (jnp.sqrt(nv / c2) + ADAM_EPS) + ADAM_WD * w_ref[...])

    outs = pl.pallas_call(
        body, name="adamw_small",
        out_shape=[jax.ShapeDtypeStruct(w.shape, F32) for w in ws for _ in range(3)],
    )(*ws, *gs, *ms, *vs)
    return [tuple(outs[3 * i:3 * i + 3]) for i in range(n)]


HALF = FF // 2


def _sds(shape, dtype):
    return jax.ShapeDtypeStruct(shape, dtype)


def _row_tile(w):
    return lambda tm: ((tm, w), lambda i, j: (i, 0))


def _one(w):
    return lambda rows: ((rows, w), lambda i, j: (0, 0))


def _gate_up_swiglu(name, h, wgu, tm=1024):
    s = h.shape[0]
    tm = min(tm, s)

    def epilogue(prod, first, tin, tout):
        pq_ref, s_ref = tout
        a, b = prod[:, :HALF], prod[:, HALF:]
        sig = _sigmoid(a)
        act = a * sig
        pq_ref[:, :HALF] = (b * (sig * (1.0 + a * (1.0 - sig)))).astype(BF16)
        pq_ref[:, HALF:] = act.astype(BF16)
        s_ref[...] = (act * b).astype(BF16)

    return _mm(name, h, wgu, "NT", None, tm, FF, D, n_outer=True, epilogue=epilogue,
               tiles_out=[(_sds((s, 2 * FF), BF16), (tm, FF), lambda i, j: (i, j)),
                          (_sds((s, FF), BF16), (tm, HALF), lambda i, j: (i, j))])


def _d_hidden_swiglu(name, df, wd, ab, after=(), tm=1024):
    s = df.shape[0]
    tm = min(tm, s)

    def epilogue(prod, first, tin, tout, cols):
        da_cols = slice(cols[0], cols[0] + cols[1])
        db_cols = slice(HALF + cols[0], HALF + cols[0] + cols[1])
        tout[0][:, da_cols] = (prod * tin[0][:, da_cols].astype(F32)).astype(BF16)
        tout[0][:, db_cols] = (prod * tin[0][:, db_cols].astype(F32)).astype(BF16)

    chunks = [(c0, min(384, HALF - c0)) for c0 in range(0, HALF, 384)]
    return _mm(name, df, wd, "NT", None, tm, HALF, D, n_outer=True, epilogue=epilogue, col_chunks=chunks, after=after,
               tiles_in=[(ab, (tm, FF), lambda i, j: (i, j))],
               tiles_out=[(_sds((s, 2 * FF), BF16), (tm, FF), lambda i, j: (i, j))])[0]


def _out_residual(name, a, w, x, gt, coef, nxt, tm=512, tk=FF):
    s = a.shape[0]
    tm = min(tm, s)

    def epilogue(prod, first, tin, tout):
        x_ref, gt_ref, g_ref, sc_ref, sh_ref = tin
        f_ref, xn_ref, h_ref = tout
        f_ref[...] = prod
        xn = x_ref[...] + (coef * gt_ref[...]) * prod
        xn_ref[...] = xn
        r = lax.rsqrt(jnp.mean(xn * xn, axis=-1, keepdims=True) + EPS)
        h_ref[...] = ((xn * r) * g_ref[...] * (1.0 + sc_ref[...]) + sh_ref[...]).astype(BF16)

    row, vec = _row_tile(D)(tm), _one(D)(1)
    return _mm(name, a, w, "NN", None, tm, D, tk, epilogue=epilogue,
               tiles_in=[(x, *row), (gt, *vec)] + [(v, *vec) for v in nxt],
               tiles_out=[(_sds((s, D), F32), *row), (_sds((s, D), F32), *row), (_sds((s, D), BF16), *row)])


def _out_loss(name, a, w, x, gt, coef, target, tm=512):
    s = a.shape[0]
    tm = min(tm, s)

    def epilogue(prod, first, tin, tout):
        x_ref, gt_ref, t_ref = tin
        f_ref, g_ref, df_ref, acc_ref = tout
        f_ref[...] = prod
        cg = coef * gt_ref[...]
        e = x_ref[...] + cg * prod - t_ref[...]
        gv = e * (1.0 / D)
        g_ref[...] = gv
        df_ref[...] = (cg * gv).astype(BF16)

        @pl.when(first)
        def _():
            acc_ref[...] = jnp.zeros_like(acc_ref)

        acc_ref[0:1, :] += coef * jnp.sum(gv * prod, axis=0, keepdims=True)
        acc_ref[1:2, :] += (0.5 / D) * jnp.sum(e * e, axis=0, keepdims=True)

    row, vec = _row_tile(D)(tm), _one(D)(1)
    return _mm(name, a, w, "NN", None, tm, D, FF, epilogue=epilogue,
               tiles_in=[(x, *row), (gt, *vec), (target, *row)],
               tiles_out=[(_sds((s, D), F32), *row), (_sds((s, D), F32), *row), (_sds((s, D), BF16), *row),
                          (_sds((8, D), F32), *_one(D)(8))])


def _d_h_norm_bwd(name, da, w, x, gin, g, sc, sh, before=None, after=(), tm=256):
    s = da.shape[0]
    tm = min(tm, s)
    coef = before[2] if before else None

    def epilogue(prod, first, tin, tout):
        x_ref, gin_ref, g_ref, sc_ref, sh_ref = tin[:5]
        gout_ref, acc_ref = tout[:2]
        xv = x_ref[...]
        r = lax.rsqrt(jnp.mean(xv * xv, axis=-1, keepdims=True) + EPS)
        nv = xv * r
        gv, one_sc = g_ref[...], 1.0 + sc_ref[...]
        dn = prod * gv * one_sc
        gout = gin_ref[...] + r * (dn - nv * jnp.mean(dn * nv, axis=-1, keepdims=True))
        gout_ref[...] = gout

        @pl.when(first)
        def _():
            acc_ref[...] = jnp.zeros_like(acc_ref)

        dhn = prod * nv
        acc_ref[0:1, :] += jnp.sum(prod, axis=0, keepdims=True)
        acc_ref[1:2, :] += jnp.sum(dhn * gv, axis=0, keepdims=True)
        acc_ref[2:3, :] += jnp.sum(dhn * one_sc, axis=0, keepdims=True)
        if before:
            f_ref, gt_ref = tin[5:]
            tout[2][...] = ((coef * gt_ref[...]) * gout).astype(BF16)
            acc_ref[3:4, :] += coef * jnp.sum(gout * f_ref[...], axis=0, keepdims=True)

    row, vec = _row_tile(D)(tm), _one(D)(1)
    tiles_in = [(x, *row), (gin, *row), (g, *vec), (sc, *vec), (sh, *vec)]
    tiles_out = [(_sds((s, D), F32), *row), (_sds((8, D), F32), *_one(D)(8))]
    if before:
        tiles_in += [(before[0], *row), (before[1], *vec)]
        tiles_out.append((_sds((s, D), BF16), *row))
    return _mm(name, da, w, "NN", None, tm, D, da.shape[1], epilogue=epilogue, keep_b=True, after=after,
               tiles_in=tiles_in, tiles_out=tiles_out)


def _gate_tiles(proj, tm):
    return [(proj, (tm, COL), (lambda i, j, blk=blk: (i, blk))) for blk in (GA_BLK, GA_BLK + 1, GC_BLK, GC_BLK + 1)]


def _conv_branch_merge(z, wc, ya, proj, tm=1024):
    s = z.shape[0]
    tm = min(tm, s)

    def epilogue(prod, first, tin, tout):
        ya_ref, ga0, ga1, gc0, gc1 = tin
        tout[0][...] = prod.astype(BF16)
        for half, (ga, gc) in enumerate(((ga0, gc0), (ga1, gc1))):
            cols = slice(half * COL, (half + 1) * COL)
            tout[1][:, cols] = (_sigmoid(ga[...].astype(F32)) * ya_ref[:, cols].astype(F32)
                                + _sigmoid(gc[...].astype(F32)) * prod[:, cols]).astype(BF16)

    row = _row_tile(D)(tm)
    return _mm("mix_conv_branch", z, wc, "NN", None, tm, D, D, epilogue=epilogue,
               tiles_in=[(ya, *row)] + _gate_tiles(proj, tm),
               tiles_out=[(_sds((s, D), BF16), *row), (_sds((s, D), BF16), *row)])


def _d_merged_branches(dmix, wo, ya, yc, proj, tm=1024):
    s = dmix.shape[0]
    tm = min(tm, s)

    def epilogue(prod, first, tin, tout):
        ya_ref, yc_ref, ga0, ga1, gc0, gc1 = tin
        dya_ref, dyc_ref, dg_ref = tout
        for half, (ga, gc) in enumerate(((ga0, gc0), (ga1, gc1))):
            cols = slice(half * COL, (half + 1) * COL)
            dm = prod[:, cols]
            for y_ref, g_ref, dy_ref, off in ((ya_ref, ga, dya_ref, 0), (yc_ref, gc, dyc_ref, D)):
                sig = _sigmoid(g_ref[...].astype(F32))
                dms = dm * sig
                dy_ref[:, cols] = dms.astype(BF16)
                dg_ref[:, off + half * COL:off + (half + 1) * COL] = (
                    dms * y_ref[:, cols].astype(F32) * (1.0 - sig)).astype(BF16)

    row = _row_tile(D)(tm)
    return _mm("mix_d_merged", dmix, wo, "NT", None, tm, D, D, epilogue=epilogue,
               tiles_in=[(ya, *row), (yc, *row)] + _gate_tiles(proj, tm),
               tiles_out=[(_sds((s, D), BF16), *row), (_sds((s, D), BF16), *row),
                          (_sds((s, 2 * D), BF16), *_row_tile(2 * D)(tm))])


def _d_o_delta(dya, wa_t, o, tm=1024):
    s = dya.shape[0]
    tm = min(tm, s)

    def epilogue(prod, first, tin, tout):
        tout[0][...] = prod
        tout[1][...] = _heads(prod * tin[0][...].astype(F32), lambda ph, h: jnp.broadcast_to(
            jnp.sum(ph, axis=-1, keepdims=True), ph.shape))

    row = _row_tile(COL)(tm)
    return _mm("mix_d_o", dya, wa_t, "NN", None, tm, COL, D, epilogue=epilogue,
               tiles_in=[(o, *row)], tiles_out=[(_sds((s, COL), F32), *row), (_sds((s, COL), F32), *row)])


def _ffn_bwd(tag, df, x, gin, h, ab, sw, g, sc, sh, wgu, wd, before=None, tk_dw=2048):
    dwd = _mm(f"{tag}_dw_down", sw, df, "TN", BF16, HALF, D, tk_dw)
    dab = _d_hidden_swiglu(f"{tag}_d_hidden", df, wd, ab, after=[dwd])
    dwgu = _mm(f"{tag}_dw_gate_up", dab, h, "TN", BF16, HALF, D, tk_dw)
    res = _d_h_norm_bwd(f"{tag}_d_h", dab, wgu, x, gin, g, sc, sh, before=before, after=[dwgu], tm=512)
    return res, dwgu, dwd


def kernel(x, c, w_ada, b_ada, norm_ffn1, ffn1_w_gate, ffn1_w_up, ffn1_w_down, norm_mix, w_in, q_norm, k_norm, conv_w, w_attn_branch, w_conv_branch, w_out, norm_ffn2, ffn2_w_gate, ffn2_w_up, ffn2_w_down, loss_target, m_w_ada, m_b_ada, m_norm_ffn1, m_ffn1_w_gate, m_ffn1_w_up, m_ffn1_w_down, m_norm_mix, m_w_in, m_q_norm, m_k_norm, m_conv_w, m_w_attn_branch, m_w_conv_branch, m_w_out, m_norm_ffn2, m_ffn2_w_gate, m_ffn2_w_up, m_ffn2_w_down, v_w_ada, v_b_ada, v_norm_ffn1, v_ffn1_w_gate, v_ffn1_w_up, v_ffn1_w_down, v_norm_mix, v_w_in, v_q_norm, v_k_norm, v_conv_w, v_w_attn_branch, v_w_conv_branch, v_w_out, v_norm_ffn2, v_ffn2_w_gate, v_ffn2_w_up, v_ffn2_w_down):
    me = 4 * lax.axis_index("x") + 2 * lax.axis_index("y") + lax.axis_index("c")
    x0, target = x[0], loss_target[0]
    s = x0.shape[0]
    ada_cols = w_ada.shape[2]
    cw_cols = conv_w.shape[2]

    gathered = _small_allgather(
        "gather_c_conv", jnp.concatenate([c, conv_w[0].reshape(1, 3 * cw_cols)], axis=1))[:, 0]
    c_all = gathered[:, :D]
    cw = gathered[:, D:].reshape(N_DEV, 3, cw_cols).transpose(1, 0, 2).reshape(3, D)
    b_part = lax.dynamic_slice(b_ada, (0, me * ada_cols), (1, ada_cols))
    mod_part = _mod_part(c_all, w_ada[0], b_part)
    mod_all = _small_allgather("gather_mod", mod_part.reshape(1, N_DEV * ada_cols))
    mod = lax.dynamic_slice(mod_all.reshape(N_DEV, N_DEV, ada_cols), (0, me, 0), (N_DEV, 1, ada_cols))
    mod = mod.reshape(N_MOD, 1, D)
    sh1, sc1, gt1, sh2, sc2, gt2, sh3, sc3, gt3 = [mod[i] for i in range(N_MOD)]

    tb = lambda w: w[0].T.astype(BF16)
    nb = lambda w: w[0].astype(BF16)
    ffn1_shards = [tb(ffn1_w_gate), tb(ffn1_w_up), nb(ffn1_w_down)]
    ffn2_shards = [tb(ffn2_w_gate), tb(ffn2_w_up), nb(ffn2_w_down)]
    mix_shards = [tb(w_in), tb(w_attn_branch), nb(w_conv_branch), nb(w_out)]
    ffn_dst, ffn_base, ffn_jump, ffn_shapes = [0, 0, 1], [0, HALF, 0], [HALF, HALF, 0], [(2 * FF, D), (FF, D)]
    mix_dst, mix_base, mix_shapes = [0, 1, 2, 3], [0, 0, 0, 0], [(IN_W, D), (D, COL), (D, D), (D, D)]
    (wgu1,) = _run_plan_on_sequencer(
        "gather_ffn1_gate_up", _gather_plan(ffn1_shards[:2], ffn_dst[:2], ffn_base[:2], ffn_shapes[:1], ffn_jump[:2]), 1)
    (wd1,) = _run_plan_on_sequencer(
        "gather_ffn1_down", _gather_plan(ffn1_shards[2:], [0], [0], ffn_shapes[1:]), 8)
    win_t, wa_t, wc, wo = _run_plan_on_sequencer(
        "gather_mix_weights", _gather_plan(mix_shards, mix_dst, mix_base, mix_shapes), 2)
    wgu2, wd2 = _run_plan_on_sequencer(
        "gather_ffn2_weights", _gather_plan(ffn2_shards, ffn_dst, ffn_base, ffn_shapes, ffn_jump), 3)

    h1 = _normmod("ffn1_normmod", x0, norm_ffn1, sc1, sh1)
    ab1, s1 = _gate_up_swiglu("ffn1_gate_up", h1, wgu1)
    f1, x1, h2 = _out_residual("ffn1_down", s1, wd1, x0, gt1, 0.5, (norm_mix, sc2, sh2))
    proj = _mm("mix_in_proj", h2, win_t, "NT", BF16, 1024, IN_W // 4, D, n_outer=True)
    wqk = jnp.concatenate([jnp.tile(q_norm, (1, 12)), jnp.tile(k_norm, (1, 12))], axis=1)
    qkn = _qknorm(proj, wqk)
    group_out = [_attn_fwd(g, qkn, proj) for g in range(3)]
    o, lse = _attn_combine([go[0] for go in group_out], [go[1] for go in group_out])
    ya = _mm("mix_attn_branch", o, wa_t, "NT", BF16, 1024, 1024, COL)
    z = _conv_fwd(proj, cw)
    yc, merged = _conv_branch_merge(z, wc, ya, proj)
    mix, x2, h3 = _out_residual("mix_out_proj", merged, wo, x1, gt2, 1.0, (norm_ffn2, sc3, sh3), tm=1024, tk=D)
    ab3, s3 = _gate_up_swiglu("ffn2_gate_up", h3, wgu2)
    f3, g3, df3, acc_out = _out_loss("ffn2_down", s3, wd2, x2, gt3, 0.5, target)
    loss_part = jnp.sum(acc_out[1])

    ffn_rows = [sh_.shape[0] for sh_ in ffn1_shards]
    mix_rows = [sh_.shape[0] for sh_ in mix_shards]
    (g2, acc3, dmix), dwgu2, dwd2 = _ffn_bwd(
        "ffn2", df3, x2, g3, h3, ab3, s3, norm_ffn2, sc3, sh3, wgu2, wd2, before=(mix, gt2, 1.0))
    dya, dyc, dgates = _d_merged_branches(dmix, wo, ya, yc, proj)
    dwo = _mm("mix_dw_out", merged, dmix, "TN", BF16, 1024, 1024, 2048)
    dproj = _copy_columns("dproj_gates", dgates, jax.ShapeDtypeStruct((s, IN_W), BF16), GA_BLK)
    dwc = _mm("mix_dw_conv_branch", z, dyc, "TN", BF16, 1024, 1024, 2048)
    dz = _mm("mix_d_z", dyc, wc, "NT", F32, 1024, 1024, D)
    dproj, d_c, cw_acc = _conv_bwd(dz, proj, cw, dproj)
    dproj = _copy_columns("copy_d_c", d_c, dproj, C_BLK)
    dwa_t = _mm("mix_dw_attn_branch", dya, o, "TN", BF16, 1024, COL, 2048)
    do, delta = _d_o_delta(dya, wa_t, o)
    dqn = dkn = None
    for g in range(3):
        dqn, dkn, dproj = _attn_bwd(g, qkn, proj, do, lse, delta, dqn, dkn, dproj)
    dproj, wq_acc = _qknorm_bwd("qnorm_bwd", proj, dqn, wqk[:, :QKW // 2], dproj, 0)
    dproj, wk_acc = _qknorm_bwd("knorm_bwd", proj, dkn, wqk[:, QKW // 2:], dproj, 1)
    r_f2g, r_f2u, r_f2d, r_wa, r_wc, r_wo = _run_plan_on_sequencer(
        "scatter_ffn2_and_branch_grads",
        _scatter_plan([dwgu2, dwd2, dwa_t, dwc, dwo], [0, 0, 1, 2, 3, 4], [0, HALF, 0, 0, 0, 0],
                      ffn_rows + mix_rows[1:], [D, D, D, COL, D, D], [HALF, HALF, 0, 0, 0, 0]), 4)
    dwin_t = _mm("mix_dw_in", dproj, h2, "TN", BF16, IN_W // 4, COL, 2048)
    (r_win,) = _run_plan_on_sequencer(
        "scatter_w_in_grad", _scatter_plan([dwin_t], [0], [0], mix_rows[:1], [D]), 5)
    g1, acc2, df1 = _d_h_norm_bwd("mix_d_h", dproj, win_t, x1, g2, norm_mix, sc2, sh2, before=(f1, gt1, 0.5),
                                  after=[dwin_t])
    dwd1 = _mm("ffn1_dw_down", s1, df1, "TN", BF16, HALF, D, 2048)
    (r_f1d,) = _run_plan_on_sequencer(
        "scatter_ffn1_down_grad", _scatter_plan([dwd1], [0], [0], ffn_rows[2:], [D]), 6)
    dab1 = _d_hidden_swiglu("ffn1_d_hidden", df1, wd1, ab1, after=[dwd1, r_win])
    dwgu1 = _mm("ffn1_dw_gate_up", dab1, h1, "TN", BF16, HALF, D, 2048)
    r_f1g, r_f1u = _run_plan_on_sequencer(
        "scatter_ffn1_gate_up_grads",
        _scatter_plan([dwgu1], [0, 0], [0, HALF], ffn_rows[:2], [D, D], [HALF, HALF]), 7)
    g0, acc1 = _d_h_norm_bwd("ffn1_d_h", dab1, wgu1, x0, g1, norm_ffn1, sc1, sh1, after=[dwgu1, r_f1d], tm=512)

    dqw = jnp.sum(wq_acc[0].reshape(12, HD), axis=0)
    dkw = jnp.sum(wk_acc[0].reshape(12, HD), axis=0)
    small = jnp.concatenate([
        acc1[0], acc1[1], acc2[3], acc2[0], acc2[1], acc3[3], acc3[0], acc3[1], acc_out[0],
        acc1[2], acc2[2], acc3[2], dqw, dkw, cw_acc[0:3].reshape(3 * D),
        jnp.zeros((HD,), F32).at[0].set(loss_part)]).reshape(1, -1)
    small_all = _small_allgather("gather_small_grads", small)
    small_sum = _sum_rows("sum_small_grads", small_all)[0]
    n_mod = N_MOD * D
    g_b_ada = small_sum[:n_mod].reshape(1, n_mod)
    g_norm1, g_norm2, g_norm3 = [small_sum[n_mod + i * D:n_mod + (i + 1) * D].reshape(1, D) for i in range(3)]
    off = n_mod + 3 * D
    g_qn, g_kn = small_sum[off:off + HD].reshape(1, HD), small_sum[off + HD:off + 2 * HD].reshape(1, HD)
    g_cw_full = small_sum[off + 2 * HD:off + 2 * HD + 3 * D].reshape(3, D)
    loss = small_sum[off + 2 * HD + 3 * D]
    g_cw = lax.dynamic_slice(g_cw_full, (0, me * cw_cols), (3, cw_cols))
    dmod_part = lax.dynamic_slice(small_all[:, 0, :n_mod], (0, me * ada_cols), (N_DEV, ada_cols))
    g_w_ada = _w_ada_grad(c_all.T, dmod_part)

    as_rows = {"ffn1_w_gate", "ffn1_w_up", "w_in", "w_attn_branch", "ffn2_w_gate", "ffn2_w_up"}
    grad_list = [g_w_ada, g_b_ada, g_norm1, r_f1g, r_f1u, r_f1d, g_norm2, r_win,
                 g_qn, g_kn, g_cw, r_wa, r_wc, r_wo, g_norm3, r_f2g, r_f2u, r_f2d]
    weights = [w_ada, b_ada, norm_ffn1, ffn1_w_gate, ffn1_w_up, ffn1_w_down, norm_mix, w_in, q_norm, k_norm,
               conv_w, w_attn_branch, w_conv_branch, w_out, norm_ffn2, ffn2_w_gate, ffn2_w_up, ffn2_w_down]
    ms = [m_w_ada, m_b_ada, m_norm_ffn1, m_ffn1_w_gate, m_ffn1_w_up, m_ffn1_w_down, m_norm_mix, m_w_in, m_q_norm,
          m_k_norm, m_conv_w, m_w_attn_branch, m_w_conv_branch, m_w_out, m_norm_ffn2, m_ffn2_w_gate,
          m_ffn2_w_up, m_ffn2_w_down]
    vs = [v_w_ada, v_b_ada, v_norm_ffn1, v_ffn1_w_gate, v_ffn1_w_up, v_ffn1_w_down, v_norm_mix, v_w_in, v_q_norm,
          v_k_norm, v_conv_w, v_w_attn_branch, v_w_conv_branch, v_w_out, v_norm_ffn2, v_ffn2_w_gate,
          v_ffn2_w_up, v_ffn2_w_down]
    wnames = ["w_ada", "b_ada", "norm_ffn1", "ffn1_w_gate", "ffn1_w_up", "ffn1_w_down", "norm_mix", "w_in",
              "q_norm", "k_norm", "conv_w", "w_attn_branch", "w_conv_branch", "w_out", "norm_ffn2",
              "ffn2_w_gate", "ffn2_w_up", "ffn2_w_down"]
    small = [i for i, gr in enumerate(grad_list) if gr.ndim == 2 and gr.size <= 16384]
    flat = lambda a, i: a.reshape(-1, weights[i].shape[-1])
    small_res = dict(zip(small, _adamw_small(
        [flat(weights[i], i) for i in small], [flat(grad_list[i], i) for i in small],
        [flat(ms[i], i) for i in small], [flat(vs[i], i) for i in small])))
    grad_out, deltas, new_ms, new_vs = [], [], [], []
    for idx, (nm, w, gr, m_, v_) in enumerate(zip(wnames, weights, grad_list, ms, vs)):
        if idx in small_res:
            gr, dl, nm_, nv_ = [r.reshape(w.shape) for r in (gr, *small_res[idx])]
        elif nm in as_rows:
            res = _adamw(f"adamw_{nm}", w[0].T, gr, m_[0].T, v_[0].T)
            gr, dl, nm_, nv_ = [r.T[None] for r in res]
        else:
            two_d = (-1, w.shape[-1])
            res = _adamw(f"adamw_{nm}", w.reshape(two_d), gr if gr.ndim == 3 else gr.reshape(two_d),
                         m_.reshape(two_d), v_.reshape(two_d))
            gr, dl, nm_, nv_ = [r.reshape(w.shape) for r in res]
        grad_out.append(gr)
        deltas.append(dl)
        new_ms.append(nm_)
        new_vs.append(nv_)
    return (loss, g0[None], *grad_out, *deltas, *new_ms, *new_vs)
```

```python
import jax
import jax.numpy as jnp
from jax import lax
from jax.experimental import pallas as pl
from jax.experimental.pallas import tpu as pltpu
from jax.experimental.pallas import tpu_sc as plsc

F32 = jnp.float32
BF16 = jnp.bfloat16
MESH = pl.DeviceIdType.MESH

N_DEV = 8
D = 1024
FF = 2816
HD = 128
N_HEADS = 4
DILATIONS = (1, 4, 16)
BAND = 128
QKW = 2 * 3 * N_HEADS * HD
IN_W = 9728
COL = 512
V_BLK, U_BLK, B_BLK, C_BLK, GA_BLK, GC_BLK = 6, 9, 11, 13, 15, 17
EPS = 1e-6
N_MOD = 9
ADAM_LR, ADAM_B1, ADAM_B2, ADAM_EPS, ADAM_WD, ADAM_STEP = 0.001, 0.9, 0.999, 1e-08, 0.01, 10

NT_DIMS = (((1,), (1,)), ((), ()))
TN_DIMS = (((0,), (0,)), ((), ()))
NN_DIMS = (((1,), (0,)), ((), ()))


def _place():
    return lax.axis_index("x"), lax.axis_index("y"), lax.axis_index("c")


def _flip(coord, bit):
    return 1 - coord if bit else coord


def _params(*sem):
    return pltpu.CompilerParams(dimension_semantics=sem)


def _small_allgather(name, v):
    n = v.shape[-1]

    def body(v_ref, out_ref, send_sems, recv_sems):
        x, y, c = _place()
        me = 4 * x + 2 * y + c
        out_ref[me] = v_ref[...]
        copies = []
        for k in range(1, N_DEV):
            peer = (_flip(x, (k >> 2) & 1), _flip(y, (k >> 1) & 1), _flip(c, k & 1))
            cp = pltpu.make_async_remote_copy(
                src_ref=v_ref, dst_ref=out_ref.at[me], send_sem=send_sems.at[k - 1],
                recv_sem=recv_sems.at[k - 1], device_id=peer, device_id_type=MESH)
            cp.start()
            copies.append(cp)
        for cp in copies:
            cp.wait()

    return pl.pallas_call(
        body, name=name,
        out_shape=jax.ShapeDtypeStruct((N_DEV, 1, n), F32),
        in_specs=[pl.BlockSpec(memory_space=pltpu.VMEM)],
        out_specs=pl.BlockSpec(memory_space=pltpu.VMEM),
        scratch_shapes=[pltpu.SemaphoreType.DMA((N_DEV - 1,)), pltpu.SemaphoreType.DMA((N_DEV - 1,))],
    )(v)


class _Plan:
    def __init__(self, operands, out_shapes, sems, phases):
        self.operands, self.out_shapes, self.sems, self.phases = operands, out_shapes, sems, phases


def _slab_start(base, rows, jump, idx):
    return pl.multiple_of(base + idx * rows + (idx // 4) * jump, 16)


def _gather_plan(shards, dst_of, base_of, dst_shapes, jump_of=None):
    n = len(shards)
    rows = [s.shape[0] for s in shards]
    jump_of = jump_of or [0] * n

    def phases(srcs, dsts, sems):
        send_sems, recv_sems, local_sems = sems
        x, y, c = _place()
        me, sibling = (x, y, c), (x, y, 1 - c)
        chips = [(1 - x, y), (x, 1 - y), (1 - x, 1 - y)]

        def slab(i, px, py, pc):
            start = _slab_start(base_of[i], rows[i], jump_of[i], 4 * px + 2 * py + pc)
            return dsts[dst_of[i]].at[pl.ds(start, rows[i])]

        def copy(i, k, block, to, src=None):
            return pltpu.make_async_remote_copy(
                src_ref=slab(i, *block) if src is None else src, dst_ref=slab(i, *block),
                send_sem=send_sems.at[i, k], recv_sem=recv_sems.at[i, k],
                device_id=to, device_id_type=MESH)

        def mine():
            return [pltpu.make_async_copy(srcs[i], slab(i, *me), local_sems.at[i]) for i in range(n)]

        def first():
            out = []
            for i in range(n):
                out.append(copy(i, 0, me, sibling, src=srcs[i]))
                out += [copy(i, 1 + j, me, (*chip, c), src=srcs[i]) for j, chip in enumerate(chips)]
            return out

        def passed():
            return [(copy(i, 1 + j, (*chip, c), me), copy(i, 4 + j, (*chip, c), sibling))
                    for j, chip in enumerate(chips) for i in range(n)]

        def start():
            for cp in mine() + first():
                cp.start()

        def middle():
            for landed, onward in passed():
                landed.wait_recv()
                onward.start()

        def finish():
            for i in range(n):
                copy(i, 0, sibling, me).wait_recv()
                for j, chip in enumerate(chips):
                    copy(i, 4 + j, (*chip, 1 - c), me).wait_recv()
            for cp in first() + [onward for _, onward in passed()]:
                cp.wait_send()
            for cp in mine():
                cp.wait()

        return start, middle, finish

    sems = [pltpu.SemaphoreType.DMA((n, 7)), pltpu.SemaphoreType.DMA((n, 7)), pltpu.SemaphoreType.DMA((n,))]
    return _Plan(list(shards), [jax.ShapeDtypeStruct(s, BF16) for s in dst_shapes], sems, phases)


def _scatter_plan(grads, src_of, base_of, rows, cols, jump_of=None):
    n = len(rows)
    jump_of = jump_of or [0] * n

    def phases(srcs, recvs, sems):
        send_sems, recv_sems, local_sems = sems
        x, y, c = _place()
        me = 4 * x + 2 * y + c

        def slab(i, idx):
            start = _slab_start(base_of[i], rows[i], jump_of[i], idx)
            return srcs[src_of[i]].at[pl.ds(start, rows[i])]

        def copies():
            out = [pltpu.make_async_copy(slab(i, me), recvs[i].at[me], local_sems.at[i]) for i in range(n)]
            for k in range(1, N_DEV):
                px, py, pc = _flip(x, (k >> 2) & 1), _flip(y, (k >> 1) & 1), _flip(c, k & 1)
                out += [pltpu.make_async_remote_copy(
                    src_ref=slab(i, 4 * px + 2 * py + pc), dst_ref=recvs[i].at[me],
                    send_sem=send_sems.at[i, k - 1], recv_sem=recv_sems.at[i, k - 1],
                    device_id=(px, py, pc), device_id_type=MESH) for i in range(n)]
            return out

        def start():
            for cp in copies():
                cp.start()

        def finish():
            for cp in copies():
                cp.wait()

        return start, None, finish

    sems = [pltpu.SemaphoreType.DMA((n, 7)), pltpu.SemaphoreType.DMA((n, 7)), pltpu.SemaphoreType.DMA((n,))]
    out_shapes = [jax.ShapeDtypeStruct((N_DEV, rows[i], cols[i]), BF16) for i in range(n)]
    return _Plan(list(grads), out_shapes, sems, phases)


def _run_plan_on_sequencer(name, plan, collective_id):
    src_refs = [jax.new_ref(a, memory_space=pltpu.MemorySpace.HBM) for a in plan.operands]
    dst_refs = [jax.empty_ref(s, memory_space=pltpu.MemorySpace.HBM) for s in plan.out_shapes]

    @pl.kernel(mesh=plsc.ScalarSubcoreMesh(axis_name="sequencer", num_cores=1), name=name,
               scratch_types=tuple(plan.sems),
               compiler_params=pltpu.CompilerParams(collective_id=collective_id))
    def launch(*sems):
        x, y, c = _place()
        barrier = pltpu.get_barrier_semaphore()
        for k in range(1, N_DEV):
            peer = (_flip(x, (k >> 2) & 1), _flip(y, (k >> 1) & 1), _flip(c, k & 1))
            pl.semaphore_signal(barrier, inc=1, device_id=peer, device_id_type=MESH)
        pl.semaphore_wait(barrier, N_DEV - 1)
        for phase in plan.phases(src_refs, dst_refs, sems):
            if phase is not None:
                phase()

    launch()
    return [r[...] for r in dst_refs]


def _mm(name, a, b, mode, out_dtype, tm, tn, tk, *, tiles_in=(), tiles_out=(), epilogue=None,
        n_outer=False, keep_b=False, col_chunks=None, after=()):
    if mode == "TN":
        kk, m = a.shape
    else:
        m, kk = a.shape
    n = b.shape[0] if mode == "NT" else b.shape[1]
    tm, tn, tk = min(tm, m), min(tn, n), min(tk, kk)
    assert m % tm == 0 and n % tn == 0 and kk % tk == 0, (name, m, n, kk, tm, tn, tk)
    ni, nj, nk = m // tm, n // tn, kk // tk
    dims = {"NN": NN_DIMS, "NT": NT_DIMS, "TN": TN_DIMS}[mode]
    if epilogue is None:
        tiles_out = [(jax.ShapeDtypeStruct((m, n), out_dtype), (tm, tn), lambda i, j: (i, j))]
    n_tin, n_tout = len(tiles_in), len(tiles_out)
    n_acc = 1 if nk > 1 else 0
    n_after = len(after)
    assert not keep_b or (nk == 1 and nj == 1)
    assert not col_chunks or (epilogue is not None and nk == 1 and mode != "TN")
    ij = (lambda p, q: (q, p)) if n_outer else (lambda p, q: (p, q))
    inner = ni if n_outer else nj

    def body(a_ref, b_ref, *rest):
        tin = rest[:n_tin]
        tout = rest[n_tin + n_after:n_tin + n_after + n_tout]
        scratch = rest[n_tin + n_after + n_tout:]
        k = pl.program_id(2)
        visit = pl.program_id(0) * inner + pl.program_id(1)
        if keep_b:
            b_kept, b_sem = scratch[n_acc:n_acc + 2]

            @pl.when((visit == 0) & (k == 0))
            def _():
                cp = pltpu.make_async_copy(b_ref, b_kept, b_sem)
                cp.start()
                cp.wait()

            b_ref = b_kept

        def store(prod, c=0, cols=()):
            if epilogue is None:
                tout[0][...] = prod.astype(out_dtype)
            else:
                epilogue(prod, jnp.logical_and(visit == 0, c == 0), tin, tout, *cols)

        if col_chunks:
            for c, (c0, cw) in enumerate(col_chunks):
                b_part = b_ref[pl.ds(c0, cw), :] if mode == "NT" else b_ref[:, pl.ds(c0, cw)]
                store(lax.dot_general(a_ref[...], b_part, dims, preferred_element_type=F32), c, ((c0, cw),))
        else:
            part = lax.dot_general(a_ref[...], b_ref[...], dims, preferred_element_type=F32)
            if nk == 1:
                store(part)
            else:
                acc_ref = scratch[0]

                @pl.when(k == 0)
                def _():
                    acc_ref[...] = part

                @pl.when((k > 0) & (k < nk - 1))
                def _():
                    acc_ref[...] += part

                @pl.when(k == nk - 1)
                def _():
                    store(acc_ref[...] + part)

    def spec(shape, fn):
        return pl.BlockSpec(shape, lambda p, q, k: fn(*ij(p, q)))

    a_spec = (pl.BlockSpec((tk, tm), lambda p, q, k: (k, ij(p, q)[0])) if mode == "TN"
              else pl.BlockSpec((tm, tk), lambda p, q, k: (ij(p, q)[0], k)))
    if keep_b:
        b_spec = pl.BlockSpec(memory_space=pl.ANY)
    elif mode == "NT":
        b_spec = pl.BlockSpec((tn, tk), lambda p, q, k: (ij(p, q)[1], k))
    else:
        b_spec = pl.BlockSpec((tk, tn), lambda p, q, k: (k, ij(p, q)[1]))
    sequential = epilogue or keep_b
    out = pl.pallas_call(
        body, name=name, grid=(nj, ni, nk) if n_outer else (ni, nj, nk),
        out_shape=[t[0] for t in tiles_out],
        in_specs=([a_spec, b_spec] + [spec(t[1], t[2]) for t in tiles_in]
                  + [pl.BlockSpec(memory_space=pl.ANY)] * n_after),
        out_specs=[spec(t[1], t[2]) for t in tiles_out],
        scratch_shapes=([pltpu.VMEM((tm, tn), F32)] * n_acc
                        + ([pltpu.VMEM(b.shape, b.dtype), pltpu.SemaphoreType.DMA] if keep_b else [])),
        compiler_params=(_params("arbitrary", "arbitrary", "arbitrary") if sequential
                         else _params("parallel", "parallel", "arbitrary")),
    )(a, b, *[t[0] for t in tiles_in], *after)
    return out if epilogue else out[0]


def _row(tm, w, off=0):
    return pl.BlockSpec((tm, w), lambda i: (i, off))


def _vec(w):
    return pl.BlockSpec((1, w), lambda i: (0, 0))


def _sigmoid(x):
    return 0.5 * jnp.tanh(0.5 * x) + 0.5


def _normmod(name, x, g, sc, sh, tm=1024):
    s = x.shape[0]

    def body(x_ref, g_ref, sc_ref, sh_ref, h_ref):
        xv = x_ref[...]
        r = lax.rsqrt(jnp.mean(xv * xv, axis=-1, keepdims=True) + EPS)
        h_ref[...] = ((xv * r) * g_ref[...] * (1.0 + sc_ref[...]) + sh_ref[...]).astype(BF16)

    return pl.pallas_call(
        body, name=name, grid=(s // tm,),
        out_shape=jax.ShapeDtypeStruct((s, D), BF16),
        in_specs=[_row(tm, D), _vec(D), _vec(D), _vec(D)], out_specs=_row(tm, D),
        compiler_params=_params("parallel"),
    )(x, g, sc, sh)


def _heads(x, fn):
    return jnp.concatenate([fn(x[:, h * HD:(h + 1) * HD], h) for h in range(x.shape[1] // HD)], axis=1)


def _qknorm(proj, wqk, tm=512):
    s = proj.shape[0]

    def body(p_ref, w_ref, o_ref):
        pv = p_ref[...].astype(F32)
        wv = w_ref[...]

        def one(qh, h):
            r = lax.rsqrt(jnp.mean(qh * qh, axis=-1, keepdims=True) + EPS)
            return (qh * r) * wv[:, h * HD:(h + 1) * HD]

        o_ref[...] = _heads(pv, one).astype(BF16)

    return pl.pallas_call(
        body, name="qknorm", grid=(s // tm,),
        out_shape=jax.ShapeDtypeStruct((s, QKW), BF16),
        in_specs=[pl.BlockSpec((tm, QKW), lambda i: (i, 0)), pl.BlockSpec((1, QKW), lambda i: (0, 0))],
        out_specs=pl.BlockSpec((tm, QKW), lambda i: (i, 0)),
        compiler_params=_params("parallel"),
    )(proj, wqk)


def _qknorm_bwd(name, proj, dn, w, dproj, blk0, tm=512):
    s, width = dn.shape

    def body(p_ref, d_ref, w_ref, _, o_ref, acc_ref):
        pv = p_ref[...].astype(F32)
        dv = d_ref[...]
        wv = w_ref[...]
        sums = []

        def one(qh, h):
            dn = dv[:, h * HD:(h + 1) * HD]
            r = lax.rsqrt(jnp.mean(qh * qh, axis=-1, keepdims=True) + EPS)
            nh = qh * r
            sums.append(jnp.sum(dn * nh, axis=0, keepdims=True))
            dnw = dn * wv[:, h * HD:(h + 1) * HD]
            return r * (dnw - nh * jnp.mean(dnw * nh, axis=-1, keepdims=True))

        o_ref[...] = _heads(pv, one).astype(BF16)

        @pl.when(pl.program_id(0) == 0)
        def _():
            acc_ref[...] = jnp.zeros_like(acc_ref)

        acc_ref[0:1, :] += jnp.concatenate(sums, axis=1)

    return pl.pallas_call(
        body, name=name, grid=(s // tm,),
        out_shape=[jax.ShapeDtypeStruct((s, IN_W), BF16), jax.ShapeDtypeStruct((8, width), F32)],
        in_specs=[pl.BlockSpec((tm, width), lambda i: (i, blk0)), pl.BlockSpec((tm, width), lambda i: (i, 0)),
                  pl.BlockSpec((1, width), lambda i: (0, 0)), pl.BlockSpec(memory_space=pl.ANY)],
        out_specs=[pl.BlockSpec((tm, width), lambda i: (i, blk0)), pl.BlockSpec((8, width), lambda i: (0, 0))],
        input_output_aliases={3: 0},
        compiler_params=_params("arbitrary"),
    )(proj, dn, w, dproj)


def _attn_shapes(s, g, sub_block):
    d = DILATIONS[g]
    tb = min(s, max(2048, 256 * d))
    sb = min(sub_block, tb // d)
    pb = BAND * d
    assert s % tb == 0 and tb % pb == 0 and (tb // d) % sb == 0 and sb % BAND == 0
    return d, tb, sb, pb


def _lanes(x, width):
    return jnp.concatenate([x] * (width // HD), axis=1)


def _every(start, size, d):
    return pl.ds(start, size, stride=d) if d > 1 else pl.ds(start, size)


def _attn_specs(g, tb, pb, s, ahead):
    ratio = tb // pb
    if ahead:
        nbr = lambda n: jnp.minimum((n + 1) * ratio, s // pb - 1)
    else:
        nbr = lambda n: jnp.maximum(n * ratio - 1, 0)
    cur = lambda base: pl.BlockSpec((tb, HD), lambda h, n: (n, base + g * N_HEADS + h))
    side = lambda base: pl.BlockSpec((pb, HD), lambda h, n: (nbr(n), base + g * N_HEADS + h))
    tok = pl.BlockSpec((tb, HD), lambda h, n: (n, h))
    tok_side = pl.BlockSpec((pb, HD), lambda h, n: (nbr(n), h))
    return cur, side, tok, tok_side


Q_COL, K_COL, V_COL = 0, 12, 24


def _attn_fwd(g, qkn, proj):
    s = qkn.shape[0]
    d, tb, sb, pb = _attn_shapes(s, g, 128)
    ft = F32 if d > 1 else BF16
    nj = tb // d // sb
    scale = HD ** -0.5

    def body(q_ref, kc_ref, kp_ref, vc_ref, vp_ref, o_ref, lse_ref, qf, kf, vf):
        n = pl.program_id(1)
        qf[...] = q_ref[...].astype(ft)
        kf[0:pb] = kp_ref[...].astype(ft)
        kf[pb:] = kc_ref[...].astype(ft)
        vf[0:pb] = vp_ref[...].astype(ft)
        vf[pb:] = vc_ref[...].astype(ft)
        for r in range(d):
            for j in range(nj):
                at = j * sb * d + r
                q = qf[_every(at, sb, d), :].astype(BF16)
                k = kf[_every(at, sb + BAND, d), :].astype(BF16)
                v = vf[_every(at, sb + BAND, d), :].astype(BF16)
                sc = lax.dot_general(q, k, NT_DIMS, preferred_element_type=F32) * scale
                qi = lax.broadcasted_iota(jnp.int32, sc.shape, 0)
                kj = lax.broadcasted_iota(jnp.int32, sc.shape, 1)
                valid = (kj >= qi) & (kj <= qi + BAND)
                if j == 0:
                    valid = valid & ((kj >= BAND) | (n > 0))
                sc = jnp.where(valid, sc, -1e30)
                m = jnp.max(sc, axis=-1, keepdims=True)
                p = jnp.exp(sc - m)
                l = jnp.sum(p, axis=-1, keepdims=True)
                o = lax.dot_general(p.astype(BF16), v, NN_DIMS, preferred_element_type=F32)
                o_ref[_every(at, sb, d), :] = o / l
                lse_ref[_every(at, sb, d), :] = jnp.broadcast_to(m + jnp.log(l), (sb, HD))

    cur, side, tok, _ = _attn_specs(g, tb, pb, s, ahead=False)
    return pl.pallas_call(
        body, name=f"attn_fwd_g{g}", grid=(N_HEADS, s // tb),
        out_shape=[jax.ShapeDtypeStruct((s, COL), F32)] * 2,
        in_specs=[cur(Q_COL), cur(K_COL), side(K_COL), cur(V_COL), side(V_COL)],
        out_specs=[tok, tok],
        scratch_shapes=[pltpu.VMEM((tb, HD), ft), pltpu.VMEM((tb + pb, HD), ft),
                        pltpu.VMEM((tb + pb, HD), ft)],
        compiler_params=_params("parallel", "arbitrary"),
    )(qkn, qkn, qkn, proj, proj)


def _attn_combine(os_, lses, tm=1024):
    s = os_[0].shape[0]

    def body(o0, o1, o2, l0, l1, l2, o_ref, lse_ref):
        a, b, c = l0[...], l1[...], l2[...]
        m = jnp.maximum(jnp.maximum(a, b), c)
        ea, eb, ec = jnp.exp(a - m), jnp.exp(b - m), jnp.exp(c - m)
        tot = ea + eb + ec
        o_ref[...] = ((ea * o0[...] + eb * o1[...] + ec * o2[...]) / tot).astype(BF16)
        lse_ref[...] = m + jnp.log(tot)

    return pl.pallas_call(
        body, name="attn_combine", grid=(s // tm,),
        out_shape=[jax.ShapeDtypeStruct((s, COL), BF16), jax.ShapeDtypeStruct((s, COL), F32)],
        in_specs=[_row(tm, COL)] * 6, out_specs=[_row(tm, COL)] * 2,
        compiler_params=_params("parallel"),
    )(*os_, *lses)


def _attn_bwd(g, qkn, proj, do, lse, delta, dqn, dkn, dproj):
    s = qkn.shape[0]
    d, tb, sb, pb = _attn_shapes(s, g, 256)
    ft = F32 if d > 1 else BF16
    nj = tb // d // sb
    nt = s // tb
    scale = HD ** -0.5
    chained = dqn is not None

    def body(k_ref, v_ref, qc_ref, qn_ref, doc_ref, don_ref, lc_ref, ln_ref, dc_ref, dn_ref, *rest):
        dq_ref, dk_ref, dv_ref, kf, vf, qf, dvf, later = rest[-8:]
        n = pl.program_id(1)
        kf[...] = k_ref[...].astype(ft)
        vf[...] = v_ref[...].astype(ft)
        qf[0:tb] = qc_ref[...].astype(ft)
        qf[tb:] = qn_ref[...].astype(ft)

        @pl.when(n == 0)
        def _():
            later[...] = jnp.zeros_like(later)

        def window(c_ref, n_ref, r, j):
            at = j * sb * d + r
            if j < nj - 1:
                return c_ref[_every(at, sb + BAND, d), :]
            return jnp.concatenate([c_ref[_every(at, sb, d), :], n_ref[_every(r, BAND, d), :]], axis=0)

        for r in range(d):
            tail = later[r]
            for j in range(nj):
                at = j * sb * d + r
                rows = _every(at, sb, d)
                k = kf[rows, :].astype(BF16)
                v = vf[rows, :].astype(BF16)
                q = qf[_every(at, sb + BAND, d), :].astype(BF16)
                dov = window(doc_ref, don_ref, r, j).astype(BF16)
                sc = lax.dot_general(q, k, NT_DIMS, preferred_element_type=F32) * scale
                qi = lax.broadcasted_iota(jnp.int32, sc.shape, 0)
                kj = lax.broadcasted_iota(jnp.int32, sc.shape, 1)
                valid = (qi >= kj) & (qi <= kj + BAND)
                if j == nj - 1:
                    valid = valid & ((qi < sb) | (n < nt - 1))
                p = jnp.exp(jnp.where(valid, sc - _lanes(window(lc_ref, ln_ref, r, j), sb), -1e30))
                dp = lax.dot_general(dov, v, NT_DIMS, preferred_element_type=F32)
                ds = (p * (dp - _lanes(window(dc_ref, dn_ref, r, j), sb)) * scale).astype(BF16)
                dvf[rows, :] = lax.dot_general(p.astype(BF16), dov, TN_DIMS, preferred_element_type=F32)
                dk_ref[rows, :] = lax.dot_general(ds, q, TN_DIMS, preferred_element_type=F32)
                dqw = lax.dot_general(ds, k, NN_DIMS, preferred_element_type=F32)
                first = dqw[:BAND] + tail
                dq_ref[rows, :] = first if sb == BAND else jnp.concatenate([first, dqw[BAND:sb]], axis=0)
                tail = dqw[sb:]
            later[r] = tail
        dv_ref[...] = dvf[...].astype(BF16)

    cur, side, tok, tok_side = _attn_specs(g, tb, pb, s, ahead=True)
    anyspec = pl.BlockSpec(memory_space=pl.ANY)
    n_heads_cols = 3 * N_HEADS * HD
    return pl.pallas_call(
        body, name=f"attn_bwd_g{g}", grid=(N_HEADS, nt),
        out_shape=[jax.ShapeDtypeStruct((s, n_heads_cols), F32), jax.ShapeDtypeStruct((s, n_heads_cols), F32),
                   jax.ShapeDtypeStruct((s, IN_W), BF16)],
        in_specs=[cur(K_COL), cur(V_COL), cur(Q_COL), side(Q_COL), tok, tok_side, tok, tok_side,
                  tok, tok_side] + ([anyspec, anyspec] if chained else []) + [anyspec],
        out_specs=[cur(0), cur(0), cur(V_COL)],
        input_output_aliases={10: 0, 11: 1, 12: 2} if chained else {10: 2},
        scratch_shapes=[pltpu.VMEM((tb, HD), ft), pltpu.VMEM((tb, HD), ft),
                        pltpu.VMEM((tb + pb, HD), ft), pltpu.VMEM((tb, HD), F32),
                        pltpu.VMEM((d, BAND, HD), F32)],
        compiler_params=_params("arbitrary", "arbitrary"),
    )(qkn, proj, qkn, qkn, do, do, lse, lse, delta, delta, *([dqn, dkn] if chained else []), dproj)


def _shift_down(x, before, k):
    rolled = pltpu.roll(x, k, 0)
    head = jnp.where(lax.broadcasted_iota(jnp.int32, before.shape, 0) < k, pltpu.roll(before, k, 0), rolled[:8])
    return jnp.concatenate([head, rolled[8:]], axis=0)


def _shift_up(x, after, k):
    rows = x.shape[0]
    rolled = pltpu.roll(x, rows - k, 0)
    tail = jnp.where(lax.broadcasted_iota(jnp.int32, after.shape, 0) >= 8 - k,
                     pltpu.roll(after, 8 - k, 0), rolled[rows - 8:])
    return jnp.concatenate([rolled[:rows - 8], tail], axis=0)


def _conv_fwd(proj, cw, tm=2048):
    s = proj.shape[0]
    r16 = tm // 16

    def body(u_ref, b_ref, c_ref, up_ref, cp_ref, w_ref, z_ref):
        i = pl.program_id(1)
        xc = c_ref[...].astype(F32) * u_ref[...].astype(F32)
        xp = jnp.where(i > 0, cp_ref[8:16, :].astype(F32) * up_ref[8:16, :].astype(F32), 0.0)
        w = w_ref[...]
        conv = _shift_down(xc, xp, 2) * w[0:1] + _shift_down(xc, xp, 1) * w[1:2] + xc * w[2:3]
        z_ref[...] = (b_ref[...].astype(F32) * conv).astype(BF16)

    tile = lambda blk: pl.BlockSpec((tm, COL), lambda j, i: (i, blk + j))
    before = lambda blk: pl.BlockSpec((16, COL), lambda j, i: (jnp.maximum(i * r16 - 1, 0), blk + j))
    return pl.pallas_call(
        body, name="conv_fwd", grid=(D // COL, s // tm),
        out_shape=jax.ShapeDtypeStruct((s, D), BF16),
        in_specs=[tile(U_BLK), tile(B_BLK), tile(C_BLK), before(U_BLK), before(C_BLK),
                  pl.BlockSpec((3, COL), lambda j, i: (0, j))],
        out_specs=pl.BlockSpec((tm, COL), lambda j, i: (i, j)),
        compiler_params=_params("parallel", "parallel"),
    )(proj, proj, proj, proj, proj, cw)


def _conv_bwd(dz, proj, cw, dproj, tm=2048):
    s = proj.shape[0]
    r16 = tm // 16
    nrow = s // tm

    def body(dz_ref, u_ref, b_ref, c_ref, up_ref, cp_ref, dzn_ref, bn_ref, w_ref, _, o_ref, dc_ref, acc_ref):
        piece, i = pl.program_id(1), pl.program_id(2)
        u, c = u_ref[...].astype(F32), c_ref[...].astype(F32)
        bv = b_ref[...].astype(F32)
        dzv = dz_ref[...]
        w = w_ref[...]

        @pl.when((piece == 0) & (i == 0))
        def _():
            acc_ref[...] = jnp.zeros_like(acc_ref)

        @pl.when(piece == 0)
        def _():
            xc = c * u
            xp = jnp.where(i > 0, cp_ref[8:16, :].astype(F32) * up_ref[8:16, :].astype(F32), 0.0)
            x2, x1 = _shift_down(xc, xp, 2), _shift_down(xc, xp, 1)
            o_ref[...] = (dzv * (x2 * w[0:1] + x1 * w[1:2] + xc * w[2:3])).astype(BF16)
            dc_ref[...] = jnp.zeros_like(dc_ref)
            dconv = dzv * bv
            acc_ref[0:1, :] += jnp.sum(dconv * x2, axis=0, keepdims=True)
            acc_ref[1:2, :] += jnp.sum(dconv * x1, axis=0, keepdims=True)
            acc_ref[2:3, :] += jnp.sum(dconv * xc, axis=0, keepdims=True)

        @pl.when(piece == 1)
        def _():
            dconv = dzv * bv
            dn = jnp.where(i < nrow - 1, dzn_ref[...] * bn_ref[0:8, :].astype(F32), 0.0)
            dxc = dconv * w[2:3] + _shift_up(dconv, dn, 1) * w[1:2] + _shift_up(dconv, dn, 2) * w[0:1]
            o_ref[...] = (dxc * c).astype(BF16)
            dc_ref[...] = (dxc * u).astype(BF16)

    tile = lambda blk: pl.BlockSpec((tm, COL), lambda j, p, i: (i, blk + j))
    before = lambda blk: pl.BlockSpec((16, COL), lambda j, p, i: (jnp.maximum(i * r16 - 1, 0), blk + j))
    after = lambda rows, blk: pl.BlockSpec(
        (rows, COL), lambda j, p, i: (jnp.minimum((i + 1) * (tm // rows), s // rows - 1), blk + j))
    return pl.pallas_call(
        body, name="conv_bwd", grid=(D // COL, 2, nrow),
        out_shape=[jax.ShapeDtypeStruct((s, IN_W), BF16), jax.ShapeDtypeStruct((s + tm, D), BF16),
                   jax.ShapeDtypeStruct((8, D), F32)],
        in_specs=[tile(0), tile(U_BLK), tile(B_BLK), tile(C_BLK), before(U_BLK), before(C_BLK),
                  after(8, 0), after(16, B_BLK), pl.BlockSpec((3, COL), lambda j, p, i: (0, j)),
                  pl.BlockSpec(memory_space=pl.ANY)],
        out_specs=[pl.BlockSpec((tm, COL), lambda j, p, i: (i, jnp.where(p == 0, B_BLK, U_BLK) + j)),
                   pl.BlockSpec((tm, COL), lambda j, p, i: (jnp.where(p == 0, nrow, i), j)),
                   pl.BlockSpec((8, COL), lambda j, p, i: (0, j))],
        input_output_aliases={9: 0},
        compiler_params=_params("arbitrary", "arbitrary", "arbitrary"),
    )(dz, proj, proj, proj, proj, proj, dz, proj, cw, dproj)


def _copy_columns(name, src, dst, blk0, tm=2048):
    s, w = dst.shape[0], src.shape[1]
    fresh = isinstance(dst, jax.ShapeDtypeStruct)

    def body(x_ref, *rest):
        rest[-1][...] = x_ref[...]

    return pl.pallas_call(
        body, name=name, grid=(w // COL, s // tm),
        out_shape=jax.ShapeDtypeStruct(dst.shape, dst.dtype),
        in_specs=[pl.BlockSpec((tm, COL), lambda j, i: (i, j))] + ([] if fresh else [pl.BlockSpec(memory_space=pl.ANY)]),
        out_specs=pl.BlockSpec((tm, COL), lambda j, i: (i, blk0 + j)),
        input_output_aliases={} if fresh else {1: 0},
        compiler_params=_params("parallel", "parallel"),
    )(src, *([] if fresh else [dst]))


def _mod_part(c_all, w_ada, b_part):
    def body(c_ref, w_ref, b_ref, o_ref):
        cv = c_ref[...]
        act = cv * _sigmoid(cv)
        o_ref[...] = jnp.dot(act, w_ref[...], preferred_element_type=F32,
                             precision=lax.Precision.HIGHEST) + b_ref[...]

    return pl.pallas_call(
        body, name="mod_part", out_shape=jax.ShapeDtypeStruct((N_DEV, w_ada.shape[1]), F32),
    )(c_all, w_ada, b_part)


def _w_ada_grad(c_all_t, dmod_part):
    def body(c_ref, d_ref, o_ref):
        cv = c_ref[...]
        act = cv * _sigmoid(cv)
        dv = d_ref[...]
        acc = act[:, 0:1] * dv[0:1, :]
        for b in range(1, N_DEV):
            acc = acc + act[:, b:b + 1] * dv[b:b + 1, :]
        o_ref[...] = acc

    return pl.pallas_call(
        body, name="w_ada_grad", out_shape=jax.ShapeDtypeStruct((D, dmod_part.shape[1]), F32),
    )(c_all_t, dmod_part)


def _sum_rows(name, v):
    def body(v_ref, o_ref):
        acc = v_ref[0]
        for k in range(1, N_DEV):
            acc = acc + v_ref[k]
        o_ref[...] = acc

    return pl.pallas_call(body, name=name, out_shape=jax.ShapeDtypeStruct(v.shape[1:], F32))(v)


def _adamw(name, w, g, m, v):
    rows, cols = w.shape
    limit = max(16, (1 << 20) // (4 * cols))
    tr = rows if rows <= limit else next((t for t in range(limit - limit % 16, 15, -16) if rows % t == 0), rows)
    c1 = 1.0 - ADAM_B1 ** ADAM_STEP
    c2 = 1.0 - ADAM_B2 ** ADAM_STEP
    parts = g.ndim == 3

    def body(w_ref, g_ref, m_ref, v_ref, go_ref, d_ref, nm_ref, nv_ref):
        if parts:
            gv = g_ref[0].astype(F32)
            for k in range(1, N_DEV):
                gv = gv + g_ref[k].astype(F32)
        else:
            gv = g_ref[...]
        go_ref[...] = gv
        nm = ADAM_B1 * m_ref[...] + (1.0 - ADAM_B1) * gv
        nv = ADAM_B2 * v_ref[...] + (1.0 - ADAM_B2) * (gv * gv)
        nm_ref[...] = nm
        nv_ref[...] = nv
        d_ref[...] = -ADAM_LR * ((nm / c1) / (jnp.sqrt(nv / c2) + ADAM_EPS) + ADAM_WD * w_ref[...])

    spec = pl.BlockSpec((tr, cols), lambda i: (i, 0))
    g_spec = pl.BlockSpec((N_DEV, tr, cols), lambda i: (0, i, 0)) if parts else spec
    return pl.pallas_call(
        body, name=name, grid=(rows // tr,),
        out_shape=[jax.ShapeDtypeStruct((rows, cols), F32)] * 4,
        in_specs=[spec, g_spec, spec, spec], out_specs=[spec] * 4,
        compiler_params=_params("parallel"),
    )(w, g, m, v)


def _adamw_small(ws, gs, ms, vs):
    n = len(ws)
    c1 = 1.0 - ADAM_B1 ** ADAM_STEP
    c2 = 1.0 - ADAM_B2 ** ADAM_STEP

    def body(*refs):
        for i in range(n):
            w_ref, g_ref, m_ref, v_ref = refs[i], refs[n + i], refs[2 * n + i], refs[3 * n + i]
            d_ref, nm_ref, nv_ref = refs[4 * n + 3 * i:4 * n + 3 * i + 3]
            gv = g_ref[...]
            nm = ADAM_B1 * m_ref[...] + (1.0 - ADAM_B1) * gv
            nv = ADAM_B2 * v_ref[...] + (1.0 - ADAM_B2) * (gv * gv)
            nm_ref[...] = nm
            nv_ref[...] = nv
            d_ref[...] = -ADAM_LR * ((nm / c1) / (jnp.sqrt(nv / c2) + ADAM_EPS) + ADAM_WD * w_ref[...])

    outs = pl.pallas_call(
        body, name="adamw_small",
        out_shape=[jax.ShapeDtypeStruct(w.shape, F32) for w in ws for _ in range(3)],
    )(*ws, *gs, *ms, *vs)
    return [tuple(outs[3 * i:3 * i + 3]) for i in range(n)]


HALF = FF // 2


def _sds(shape, dtype):
    return jax.ShapeDtypeStruct(shape, dtype)


def _row_tile(w):
    return lambda tm: ((tm, w), lambda i, j: (i, 0))


def _one(w):
    return lambda rows: ((rows, w), lambda i, j: (0, 0))


def _gate_up_swiglu(name, h, wgu, tm=1024):
    s = h.shape[0]
    tm = min(tm, s)

    def epilogue(prod, first, tin, tout):
        pq_ref, s_ref = tout
        a, b = prod[:, :HALF], prod[:, HALF:]
        sig = _sigmoid(a)
        act = a * sig
        pq_ref[:, :HALF] = (b * (sig * (1.0 + a * (1.0 - sig)))).astype(BF16)
        pq_ref[:, HALF:] = act.astype(BF16)
        s_ref[...] = (act * b).astype(BF16)

    return _mm(name, h, wgu, "NT", None, tm, FF, D, n_outer=True, epilogue=epilogue,
               tiles_out=[(_sds((s, 2 * FF), BF16), (tm, FF), lambda i, j: (i, j)),
                          (_sds((s, FF), BF16), (tm, HALF), lambda i, j: (i, j))])


def _d_hidden_swiglu(name, df, wd, ab, after=(), tm=1024):
    s = df.shape[0]
    tm = min(tm, s)

    def epilogue(prod, first, tin, tout, cols):
        da_cols = slice(cols[0], cols[0] + cols[1])
        db_cols = slice(HALF + cols[0], HALF + cols[0] + cols[1])
        tout[0][:, da_cols] = (prod * tin[0][:, da_cols].astype(F32)).astype(BF16)
        tout[0][:, db_cols] = (prod * tin[0][:, db_cols].astype(F32)).astype(BF16)

    chunks = [(c0, min(384, HALF - c0)) for c0 in range(0, HALF, 384)]
    return _mm(name, df, wd, "NT", None, tm, HALF, D, n_outer=True, epilogue=epilogue, col_chunks=chunks, after=after,
               tiles_in=[(ab, (tm, FF), lambda i, j: (i, j))],
               tiles_out=[(_sds((s, 2 * FF), BF16), (tm, FF), lambda i, j: (i, j))])[0]


def _out_residual(name, a, w, x, gt, coef, nxt, tm=512, tk=FF):
    s = a.shape[0]
    tm = min(tm, s)

    def epilogue(prod, first, tin, tout):
        x_ref, gt_ref, g_ref, sc_ref, sh_ref = tin
        f_ref, xn_ref, h_ref = tout
        f_ref[...] = prod
        xn = x_ref[...] + (coef * gt_ref[...]) * prod
        xn_ref[...] = xn
        r = lax.rsqrt(jnp.mean(xn * xn, axis=-1, keepdims=True) + EPS)
        h_ref[...] = ((xn * r) * g_ref[...] * (1.0 + sc_ref[...]) + sh_ref[...]).astype(BF16)

    row, vec = _row_tile(D)(tm), _one(D)(1)
    return _mm(name, a, w, "NN", None, tm, D, tk, epilogue=epilogue,
               tiles_in=[(x, *row), (gt, *vec)] + [(v, *vec) for v in nxt],
               tiles_out=[(_sds((s, D), F32), *row), (_sds((s, D), F32), *row), (_sds((s, D), BF16), *row)])


def _out_loss(name, a, w, x, gt, coef, target, tm=512):
    s = a.shape[0]
    tm = min(tm, s)

    def epilogue(prod, first, tin, tout):
        x_ref, gt_ref, t_ref = tin
        f_ref, g_ref, df_ref, acc_ref = tout
        f_ref[...] = prod
        cg = coef * gt_ref[...]
        e = x_ref[...] + cg * prod - t_ref[...]
        gv = e * (1.0 / D)
        g_ref[...] = gv
        df_ref[...] = (cg * gv).astype(BF16)

        @pl.when(first)
        def _():
            acc_ref[...] = jnp.zeros_like(acc_ref)

        acc_ref[0:1, :] += coef * jnp.sum(gv * prod, axis=0, keepdims=True)
        acc_ref[1:2, :] += (0.5 / D) * jnp.sum(e * e, axis=0, keepdims=True)

    row, vec = _row_tile(D)(tm), _one(D)(1)
    return _mm(name, a, w, "NN", None, tm, D, FF, epilogue=epilogue,
               tiles_in=[(x, *row), (gt, *vec), (target, *row)],
               tiles_out=[(_sds((s, D), F32), *row), (_sds((s, D), F32), *row), (_sds((s, D), BF16), *row),
                          (_sds((8, D), F32), *_one(D)(8))])


def _d_h_norm_bwd(name, da, w, x, gin, g, sc, sh, before=None, after=(), tm=256):
    s = da.shape[0]
    tm = min(tm, s)
    coef = before[2] if before else None

    def epilogue(prod, first, tin, tout):
        x_ref, gin_ref, g_ref, sc_ref, sh_ref = tin[:5]
        gout_ref, acc_ref = tout[:2]
        xv = x_ref[...]
        r = lax.rsqrt(jnp.mean(xv * xv, axis=-1, keepdims=True) + EPS)
        nv = xv * r
        gv, one_sc = g_ref[...], 1.0 + sc_ref[...]
        dn = prod * gv * one_sc
        gout = gin_ref[...] + r * (dn - nv * jnp.mean(dn * nv, axis=-1, keepdims=True))
        gout_ref[...] = gout

        @pl.when(first)
        def _():
            acc_ref[...] = jnp.zeros_like(acc_ref)

        dhn = prod * nv
        acc_ref[0:1, :] += jnp.sum(prod, axis=0, keepdims=True)
        acc_ref[1:2, :] += jnp.sum(dhn * gv, axis=0, keepdims=True)
        acc_ref[2:3, :] += jnp.sum(dhn * one_sc, axis=0, keepdims=True)
        if before:
            f_ref, gt_ref = tin[5:]
            tout[2][...] = ((coef * gt_ref[...]) * gout).astype(BF16)
            acc_ref[3:4, :] += coef * jnp.sum(gout * f_ref[...], axis=0, keepdims=True)

    row, vec = _row_tile(D)(tm), _one(D)(1)
    tiles_in = [(x, *row), (gin, *row), (g, *vec), (sc, *vec), (sh, *vec)]
    tiles_out = [(_sds((s, D), F32), *row), (_sds((8, D), F32), *_one(D)(8))]
    if before:
        tiles_in += [(before[0], *row), (before[1], *vec)]
        tiles_out.append((_sds((s, D), BF16), *row))
    return _mm(name, da, w, "NN", None, tm, D, da.shape[1], epilogue=epilogue, keep_b=True, after=after,
               tiles_in=tiles_in, tiles_out=tiles_out)


def _gate_tiles(proj, tm):
    return [(proj, (tm, COL), (lambda i, j, blk=blk: (i, blk))) for blk in (GA_BLK, GA_BLK + 1, GC_BLK, GC_BLK + 1)]


def _conv_branch_merge(z, wc, ya, proj, tm=1024):
    s = z.shape[0]
    tm = min(tm, s)

    def epilogue(prod, first, tin, tout):
        ya_ref, ga0, ga1, gc0, gc1 = tin
        tout[0][...] = prod.astype(BF16)
        for half, (ga, gc) in enumerate(((ga0, gc0), (ga1, gc1))):
            cols = slice(half * COL, (half + 1) * COL)
            tout[1][:, cols] = (_sigmoid(ga[...].astype(F32)) * ya_ref[:, cols].astype(F32)
                                + _sigmoid(gc[...].astype(F32)) * prod[:, cols]).astype(BF16)

    row = _row_tile(D)(tm)
    return _mm("mix_conv_branch", z, wc, "NN", None, tm, D, D, epilogue=epilogue,
               tiles_in=[(ya, *row)] + _gate_tiles(proj, tm),
               tiles_out=[(_sds((s, D), BF16), *row), (_sds((s, D), BF16), *row)])


def _d_merged_branches(dmix, wo, ya, yc, proj, tm=1024):
    s = dmix.shape[0]
    tm = min(tm, s)

    def epilogue(prod, first, tin, tout):
        ya_ref, yc_ref, ga0, ga1, gc0, gc1 = tin
        dya_ref, dyc_ref, dg_ref = tout
        for half, (ga, gc) in enumerate(((ga0, gc0), (ga1, gc1))):
            cols = slice(half * COL, (half + 1) * COL)
            dm = prod[:, cols]
            for y_ref, g_ref, dy_ref, off in ((ya_ref, ga, dya_ref, 0), (yc_ref, gc, dyc_ref, D)):
                sig = _sigmoid(g_ref[...].astype(F32))
                dms = dm * sig
                dy_ref[:, cols] = dms.astype(BF16)
                dg_ref[:, off + half * COL:off + (half + 1) * COL] = (
                    dms * y_ref[:, cols].astype(F32) * (1.0 - sig)).astype(BF16)

    row = _row_tile(D)(tm)
    return _mm("mix_d_merged", dmix, wo, "NT", None, tm, D, D, epilogue=epilogue,
               tiles_in=[(ya, *row), (yc, *row)] + _gate_tiles(proj, tm),
               tiles_out=[(_sds((s, D), BF16), *row), (_sds((s, D), BF16), *row),
                          (_sds((s, 2 * D), BF16), *_row_tile(2 * D)(tm))])


def _d_o_delta(dya, wa_t, o, tm=1024):
    s = dya.shape[0]
    tm = min(tm, s)

    def epilogue(prod, first, tin, tout):
        tout[0][...] = prod
        tout[1][...] = _heads(prod * tin[0][...].astype(F32), lambda ph, h: jnp.broadcast_to(
            jnp.sum(ph, axis=-1, keepdims=True), ph.shape))

    row = _row_tile(COL)(tm)
    return _mm("mix_d_o", dya, wa_t, "NN", None, tm, COL, D, epilogue=epilogue,
               tiles_in=[(o, *row)], tiles_out=[(_sds((s, COL), F32), *row), (_sds((s, COL), F32), *row)])


def _ffn_bwd(tag, df, x, gin, h, ab, sw, g, sc, sh, wgu, wd, before=None, tk_dw=2048):
    dwd = _mm(f"{tag}_dw_down", sw, df, "TN", BF16, HALF, D, tk_dw)
    dab = _d_hidden_swiglu(f"{tag}_d_hidden", df, wd, ab, after=[dwd])
    dwgu = _mm(f"{tag}_dw_gate_up", dab, h, "TN", BF16, HALF, D, tk_dw)
    res = _d_h_norm_bwd(f"{tag}_d_h", dab, wgu, x, gin, g, sc, sh, before=before, after=[dwgu], tm=512)
    return res, dwgu, dwd


def kernel(x, c, w_ada, b_ada, norm_ffn1, ffn1_w_gate, ffn1_w_up, ffn1_w_down, norm_mix, w_in, q_norm, k_norm, conv_w, w_attn_branch, w_conv_branch, w_out, norm_ffn2, ffn2_w_gate, ffn2_w_up, ffn2_w_down, loss_target, m_w_ada, m_b_ada, m_norm_ffn1, m_ffn1_w_gate, m_ffn1_w_up, m_ffn1_w_down, m_norm_mix, m_w_in, m_q_norm, m_k_norm, m_conv_w, m_w_attn_branch, m_w_conv_branch, m_w_out, m_norm_ffn2, m_ffn2_w_gate, m_ffn2_w_up, m_ffn2_w_down, v_w_ada, v_b_ada, v_norm_ffn1, v_ffn1_w_gate, v_ffn1_w_up, v_ffn1_w_down, v_norm_mix, v_w_in, v_q_norm, v_k_norm, v_conv_w, v_w_attn_branch, v_w_conv_branch, v_w_out, v_norm_ffn2, v_ffn2_w_gate, v_ffn2_w_up, v_ffn2_w_down):
    me = 4 * lax.axis_index("x") + 2 * lax.axis_index("y") + lax.axis_index("c")
    x0, target = x[0], loss_target[0]
    s = x0.shape[0]
    ada_cols = w_ada.shape[2]
    cw_cols = conv_w.shape[2]

    gathered = _small_allgather(
        "gather_c_conv", jnp.concatenate([c, conv_w[0].reshape(1, 3 * cw_cols)], axis=1))[:, 0]
    c_all = gathered[:, :D]
    cw = gathered[:, D:].reshape(N_DEV, 3, cw_cols).transpose(1, 0, 2).reshape(3, D)
    b_part = lax.dynamic_slice(b_ada, (0, me * ada_cols), (1, ada_cols))
    mod_part = _mod_part(c_all, w_ada[0], b_part)
    mod_all = _small_allgather("gather_mod", mod_part.reshape(1, N_DEV * ada_cols))
    mod = lax.dynamic_slice(mod_all.reshape(N_DEV, N_DEV, ada_cols), (0, me, 0), (N_DEV, 1, ada_cols))
    mod = mod.reshape(N_MOD, 1, D)
    sh1, sc1, gt1, sh2, sc2, gt2, sh3, sc3, gt3 = [mod[i] for i in range(N_MOD)]

    tb = lambda w: w[0].T.astype(BF16)
    nb = lambda w: w[0].astype(BF16)
    ffn1_shards = [tb(ffn1_w_gate), tb(ffn1_w_up), nb(ffn1_w_down)]
    ffn2_shards = [tb(ffn2_w_gate), tb(ffn2_w_up), nb(ffn2_w_down)]
    mix_shards = [tb(w_in), tb(w_attn_branch), nb(w_conv_branch), nb(w_out)]
    ffn_dst, ffn_base, ffn_jump, ffn_shapes = [0, 0, 1], [0, HALF, 0], [HALF, HALF, 0], [(2 * FF, D), (FF, D)]
    mix_dst, mix_base, mix_shapes = [0, 1, 2, 3], [0, 0, 0, 0], [(IN_W, D), (D, COL), (D, D), (D, D)]
    (wgu1,) = _run_plan_on_sequencer(
        "gather_ffn1_gate_up", _gather_plan(ffn1_shards[:2], ffn_dst[:2], ffn_base[:2], ffn_shapes[:1], ffn_jump[:2]), 1)
    (wd1,) = _run_plan_on_sequencer(
        "gather_ffn1_down", _gather_plan(ffn1_shards[2:], [0], [0], ffn_shapes[1:]), 8)
    win_t, wa_t, wc, wo = _run_plan_on_sequencer(
        "gather_mix_weights", _gather_plan(mix_shards, mix_dst, mix_base, mix_shapes), 2)
    wgu2, wd2 = _run_plan_on_sequencer(
        "gather_ffn2_weights", _gather_plan(ffn2_shards, ffn_dst, ffn_base, ffn_shapes, ffn_jump), 3)

    h1 = _normmod("ffn1_normmod", x0, norm_ffn1, sc1, sh1)
    ab1, s1 = _gate_up_swiglu("ffn1_gate_up", h1, wgu1)
    f1, x1, h2 = _out_residual("ffn1_down", s1, wd1, x0, gt1, 0.5, (norm_mix, sc2, sh2))
    proj = _mm("mix_in_proj", h2, win_t, "NT", BF16, 1024, IN_W // 4, D, n_outer=True)
    wqk = jnp.concatenate([jnp.tile(q_norm, (1, 12)), jnp.tile(k_norm, (1, 12))], axis=1)
    qkn = _qknorm(proj, wqk)
    group_out = [_attn_fwd(g, qkn, proj) for g in range(3)]
    o, lse = _attn_combine([go[0] for go in group_out], [go[1] for go in group_out])
    ya = _mm("mix_attn_branch", o, wa_t, "NT", BF16, 1024, 1024, COL)
    z = _conv_fwd(proj, cw)
    yc, merged = _conv_branch_merge(z, wc, ya, proj)
    mix, x2, h3 = _out_residual("mix_out_proj", merged, wo, x1, gt2, 1.0, (norm_ffn2, sc3, sh3), tm=1024, tk=D)
    ab3, s3 = _gate_up_swiglu("ffn2_gate_up", h3, wgu2)
    f3, g3, df3, acc_out = _out_loss("ffn2_down", s3, wd2, x2, gt3, 0.5, target)
    loss_part = jnp.sum(acc_out[1])

    ffn_rows = [sh_.shape[0] for sh_ in ffn1_shards]
    mix_rows = [sh_.shape[0] for sh_ in mix_shards]
    (g2, acc3, dmix), dwgu2, dwd2 = _ffn_bwd(
        "ffn2", df3, x2, g3, h3, ab3, s3, norm_ffn2, sc3, sh3, wgu2, wd2, before=(mix, gt2, 1.0))
    dya, dyc, dgates = _d_merged_branches(dmix, wo, ya, yc, proj)
    dwo = _mm("mix_dw_out", merged, dmix, "TN", BF16, 1024, 1024, 2048)
    dproj = _copy_columns("dproj_gates", dgates, jax.ShapeDtypeStruct((s, IN_W), BF16), GA_BLK)
    dwc = _mm("mix_dw_conv_branch", z, dyc, "TN", BF16, 1024, 1024, 2048)
    dz = _mm("mix_d_z", dyc, wc, "NT", F32, 1024, 1024, D)
    dproj, d_c, cw_acc = _conv_bwd(dz, proj, cw, dproj)
    dproj = _copy_columns("copy_d_c", d_c, dproj, C_BLK)
    dwa_t = _mm("mix_dw_attn_branch", dya, o, "TN", BF16, 1024, COL, 2048)
    do, delta = _d_o_delta(dya, wa_t, o)
    dqn = dkn = None
    for g in range(3):
        dqn, dkn, dproj = _attn_bwd(g, qkn, proj, do, lse, delta, dqn, dkn, dproj)
    dproj, wq_acc = _qknorm_bwd("qnorm_bwd", proj, dqn, wqk[:, :QKW // 2], dproj, 0)
    dproj, wk_acc = _qknorm_bwd("knorm_bwd", proj, dkn, wqk[:, QKW // 2:], dproj, 1)
    r_f2g, r_f2u, r_f2d, r_wa, r_wc, r_wo = _run_plan_on_sequencer(
        "scatter_ffn2_and_branch_grads",
        _scatter_plan([dwgu2, dwd2, dwa_t, dwc, dwo], [0, 0, 1, 2, 3, 4], [0, HALF, 0, 0, 0, 0],
                      ffn_rows + mix_rows[1:], [D, D, D, COL, D, D], [HALF, HALF, 0, 0, 0, 0]), 4)
    dwin_t = _mm("mix_dw_in", dproj, h2, "TN", BF16, IN_W // 4, COL, 2048)
    (r_win,) = _run_plan_on_sequencer(
        "scatter_w_in_grad", _scatter_plan([dwin_t], [0], [0], mix_rows[:1], [D]), 5)
    g1, acc2, df1 = _d_h_norm_bwd("mix_d_h", dproj, win_t, x1, g2, norm_mix, sc2, sh2, before=(f1, gt1, 0.5),
                                  after=[dwin_t])
    dwd1 = _mm("ffn1_dw_down", s1, df1, "TN", BF16, HALF, D, 2048)
    (r_f1d,) = _run_plan_on_sequencer(
        "scatter_ffn1_down_grad", _scatter_plan([dwd1], [0], [0], ffn_rows[2:], [D]), 6)
    dab1 = _d_hidden_swiglu("ffn1_d_hidden", df1, wd1, ab1, after=[dwd1, r_win])
    dwgu1 = _mm("ffn1_dw_gate_up", dab1, h1, "TN", BF16, HALF, D, 2048)
    r_f1g, r_f1u = _run_plan_on_sequencer(
        "scatter_ffn1_gate_up_grads",
        _scatter_plan([dwgu1], [0, 0], [0, HALF], ffn_rows[:2], [D, D], [HALF, HALF]), 7)
    g0, acc1 = _d_h_norm_bwd("ffn1_d_h", dab1, wgu1, x0, g1, norm_ffn1, sc1, sh1, after=[dwgu1, r_f1d], tm=512)

    dqw = jnp.sum(wq_acc[0].reshape(12, HD), axis=0)
    dkw = jnp.sum(wk_acc[0].reshape(12, HD), axis=0)
    small = jnp.concatenate([
        acc1[0], acc1[1], acc2[3], acc2[0], acc2[1], acc3[3], acc3[0], acc3[1], acc_out[0],
        acc1[2], acc2[2], acc3[2], dqw, dkw, cw_acc[0:3].reshape(3 * D),
        jnp.zeros((HD,), F32).at[0].set(loss_part)]).reshape(1, -1)
    small_all = _small_allgather("gather_small_grads", small)
    small_sum = _sum_rows("sum_small_grads", small_all)[0]
    n_mod = N_MOD * D
    g_b_ada = small_sum[:n_mod].reshape(1, n_mod)
    g_norm1, g_norm2, g_norm3 = [small_sum[n_mod + i * D:n_mod + (i + 1) * D].reshape(1, D) for i in range(3)]
    off = n_mod + 3 * D
    g_qn, g_kn = small_sum[off:off + HD].reshape(1, HD), small_sum[off + HD:off + 2 * HD].reshape(1, HD)
    g_cw_full = small_sum[off + 2 * HD:off + 2 * HD + 3 * D].reshape(3, D)
    loss = small_sum[off + 2 * HD + 3 * D]
    g_cw = lax.dynamic_slice(g_cw_full, (0, me * cw_cols), (3, cw_cols))
    dmod_part = lax.dynamic_slice(small_all[:, 0, :n_mod], (0, me * ada_cols), (N_DEV, ada_cols))
    g_w_ada = _w_ada_grad(c_all.T, dmod_part)

    as_rows = {"ffn1_w_gate", "ffn1_w_up", "w_in", "w_attn_branch", "ffn2_w_gate", "ffn2_w_up"}
    grad_list = [g_w_ada, g_b_ada, g_norm1, r_f1g, r_f1u, r_f1d, g_norm2, r_win,
                 g_qn, g_kn, g_cw, r_wa, r_wc, r_wo, g_norm3, r_f2g, r_f2u, r_f2d]
    weights = [w_ada, b_ada, norm_ffn1, ffn1_w_gate, ffn1_w_up, ffn1_w_down, norm_mix, w_in, q_norm, k_norm,
               conv_w, w_attn_branch, w_conv_branch, w_out, norm_ffn2, ffn2_w_gate, ffn2_w_up, ffn2_w_down]
    ms = [m_w_ada, m_b_ada, m_norm_ffn1, m_ffn1_w_gate, m_ffn1_w_up, m_ffn1_w_down, m_norm_mix, m_w_in, m_q_norm,
          m_k_norm, m_conv_w, m_w_attn_branch, m_w_conv_branch, m_w_out, m_norm_ffn2, m_ffn2_w_gate,
          m_ffn2_w_up, m_ffn2_w_down]
    vs = [v_w_ada, v_b_ada, v_norm_ffn1, v_ffn1_w_gate, v_ffn1_w_up, v_ffn1_w_down, v_norm_mix, v_w_in, v_q_norm,
          v_k_norm, v_conv_w, v_w_attn_branch, v_w_conv_branch, v_w_out, v_norm_ffn2, v_ffn2_w_gate,
          v_ffn2_w_up, v_ffn2_w_down]
    wnames = ["w_ada", "b_ada", "norm_ffn1", "ffn1_w_gate", "ffn1_w_up", "ffn1_w_down", "norm_mix", "w_in",
              "q_norm", "k_norm", "conv_w", "w_attn_branch", "w_conv_branch", "w_out", "norm_ffn2",
              "ffn2_w_gate", "ffn2_w_up", "ffn2_w_down"]
    small = [i for i, gr in enumerate(grad_list) if gr.ndim == 2 and gr.size <= 16384]
    flat = lambda a, i: a.reshape(-1, weights[i].shape[-1])
    small_res = dict(zip(small, _adamw_small(
        [flat(weights[i], i) for i in small], [flat(grad_list[i], i) for i in small],
        [flat(ms[i], i) for i in small], [flat(vs[i], i) for i in small])))
    grad_out, deltas, new_ms, new_vs = [], [], [], []
    for idx, (nm, w, gr, m_, v_) in enumerate(zip(wnames, weights, grad_list, ms, vs)):
        if idx in small_res:
            gr, dl, nm_, nv_ = [r.reshape(w.shape) for r in (gr, *small_res[idx])]
        elif nm in as_rows:
            res = _adamw(f"adamw_{nm}", w[0].T, gr, m_[0].T, v_[0].T)
            gr, dl, nm_, nv_ = [r.T[None] for r in res]
        else:
            two_d = (-1, w.shape[-1])
            res = _adamw(f"adamw_{nm}", w.reshape(two_d), gr if gr.ndim == 3 else gr.reshape(two_d),
                         m_.reshape(two_d), v_.reshape(two_d))
            gr, dl, nm_, nv_ = [r.reshape(w.shape) for r in res]
        grad_out.append(gr)
        deltas.append(dl)
        new_ms.append(nm_)
        new_vs.append(nv_)
    return (loss, g0[None], *grad_out, *deltas, *new_ms, *new_vs)
```
